```python
import math
import jax, jax.numpy as jnp
from jax import lax
import numpy as np

D_MODEL = 1024
BATCH = 8
SEQ = 2048
DEPTH = 2

N_A_LAYERS = DEPTH // 2
N_B_LAYERS = DEPTH - N_A_LAYERS
N_META = 16

SSM_EXPAND = 2
D_INNER = SSM_EXPAND * D_MODEL
SSM_HEAD_DIM = 64
SSM_HEADS = D_INNER // SSM_HEAD_DIM
SSM_GROUPS = 4
SSM_HEADS_PER_GROUP = SSM_HEADS // SSM_GROUPS
D_STATE = 128
SSM_CONV = 4
CHUNK = 128
D_BC = SSM_GROUPS * D_STATE
D_XBC = D_INNER + 2 * D_BC
D_IN_PROJ = D_INNER + D_XBC + SSM_HEADS

ATTN_HEAD_DIM = 64
N_Q_HEADS = D_MODEL // ATTN_HEAD_DIM
N_KV_HEADS = 4
Q_PER_KV = N_Q_HEADS // N_KV_HEADS
D_ATTN = N_Q_HEADS * ATTN_HEAD_DIM
D_KV = N_KV_HEADS * ATTN_HEAD_DIM
WINDOW = 128
BLOCK = 128

D_FF = 2816
FFN_CONV = 3

RMS_EPS = 1e-6
NEG_INF = -1e30

kernel_name = "yoco_mamba2_swa_sink_hybrid"


def rms_norm(x, w):
    xf = x.astype(jnp.float32)
    y = xf * lax.rsqrt(jnp.mean(xf * xf, axis=-1, keepdims=True) + RMS_EPS)
    return (y * w.astype(jnp.float32)).astype(x.dtype)


def causal_dwconv(x, w, b):
    k, c = w.shape
    y = lax.conv_general_dilated(
        x, w[:, None, :].astype(x.dtype), window_strides=(1,), padding=[(k - 1, 0)],
        dimension_numbers=("NWC", "WIO", "NWC"), feature_group_count=c)
    return y + b.astype(x.dtype)


def ssd_chunked(xdt, a, b_in, c_in):
    bsz, t = xdt.shape[:2]
    nc = t // CHUNK
    xdt = xdt.reshape(bsz, nc, CHUNK, *xdt.shape[2:])
    a = a.reshape(bsz, nc, CHUNK, *a.shape[2:])
    bm = b_in.reshape(bsz, nc, CHUNK, *b_in.shape[2:])
    cm = c_in.reshape(bsz, nc, CHUNK, *c_in.shape[2:])
    a_cs = jnp.cumsum(a, axis=2)
    seg = a_cs[:, :, :, None] - a_cs[:, :, None, :]
    causal = jnp.tril(jnp.ones((CHUNK, CHUNK), bool))[None, None, :, :, None, None]
    decay_ls = jnp.where(causal, jnp.exp(jnp.where(causal, seg, 0.0)), 0.0)
    cb = jnp.einsum("bclgn,bcsgn->bclsg", cm, bm)
    y_diag = jnp.einsum("bclsg,bclsgk,bcsgkp->bclgkp", cb, decay_ls, xdt)
    decay_to_end = jnp.exp(a_cs[:, :, -1:] - a_cs)
    chunk_states = jnp.einsum("bclgn,bclgk,bclgkp->bcgkpn", bm, decay_to_end, xdt)
    chunk_decay = jnp.exp(a_cs[:, :, -1])

    def step(state, inp):
        st, dec = inp
        return state * dec[..., None, None] + st, state

    init = jnp.zeros_like(chunk_states[:, 0])
    _, prev = lax.scan(step, init, (jnp.moveaxis(chunk_states, 1, 0), jnp.moveaxis(chunk_decay, 1, 0)))
    prev = jnp.moveaxis(prev, 0, 1)
    y_off = jnp.einsum("bclgn,bcgkpn,bclgk->bclgkp", cm, prev, jnp.exp(a_cs))
    return (y_diag + y_off).reshape(bsz, t, *y_diag.shape[3:])


def mamba2_mixer(hn, w_in, conv_w, conv_b, dt_bias, a_log, d_skip, gate_norm, w_out):
    bsz, seq_len, _ = hn.shape
    zxbcdt = hn @ w_in
    z = zxbcdt[..., :D_INNER]
    xbc = zxbcdt[..., D_INNER:D_INNER + D_XBC]
    dt = zxbcdt[..., D_INNER + D_XBC:]
    xbc = jax.nn.silu(causal_dwconv(xbc, conv_w, conv_b))
    xs = xbc[..., :D_INNER]
    bm = xbc[..., D_INNER:D_INNER + D_BC]
    cm = xbc[..., D_INNER + D_BC:]
    dt = jax.nn.softplus(dt.astype(jnp.float32) + dt_bias.astype(jnp.float32))
    a = -jnp.exp(a_log.astype(jnp.float32)).reshape(SSM_GROUPS, SSM_HEADS_PER_GROUP)
    xh = xs.astype(jnp.float32).reshape(bsz, seq_len, SSM_GROUPS, SSM_HEADS_PER_GROUP, SSM_HEAD_DIM)
    dtg = dt.reshape(bsz, seq_len, SSM_GROUPS, SSM_HEADS_PER_GROUP)
    bg = bm.astype(jnp.float32).reshape(bsz, seq_len, SSM_GROUPS, D_STATE)
    cg = cm.astype(jnp.float32).reshape(bsz, seq_len, SSM_GROUPS, D_STATE)
    pad = CHUNK - N_META

    def padt(t):
        return jnp.pad(t, ((0, 0), (pad, 0)) + ((0, 0),) * (t.ndim - 2))

    y = ssd_chunked(padt(xh * dtg[..., None]), padt(dtg * a), padt(bg), padt(cg))[:, pad:]
    y = y + d_skip.astype(jnp.float32).reshape(SSM_GROUPS, SSM_HEADS_PER_GROUP)[:, :, None] * xh
    y = y.reshape(bsz, seq_len, D_INNER).astype(hn.dtype)
    y = rms_norm(y * jax.nn.silu(z), gate_norm)
    return y @ w_out


def swa_sink_attention(hn, w_q, k, v, sinks, w_o):
    bsz, seq_len, _ = hn.shape
    s_real = seq_len - N_META
    nb = s_real // BLOCK
    scale = 1.0 / math.sqrt(ATTN_HEAD_DIM)
    q = (hn @ w_q).reshape(bsz, seq_len, N_KV_HEADS, Q_PER_KV, ATTN_HEAD_DIM) * scale
    qm, qr = q[:, :N_META], q[:, N_META:]
    km, kr = k[:, :N_META], k[:, N_META:]
    vm, vr = v[:, :N_META], v[:, N_META:]
    sink = sinks.astype(jnp.float32).reshape(N_KV_HEADS, Q_PER_KV)

    sm = jnp.einsum("bqkgd,bskd->bkgqs", qm, km).astype(jnp.float32)
    sm = jnp.where(jnp.tril(jnp.ones((N_META, N_META), bool)), sm, NEG_INF)
    sm = jnp.concatenate([sm, jnp.broadcast_to(sink[None, :, :, None, None], sm.shape[:-1] + (1,))], -1)
    pm = jax.nn.softmax(sm, axis=-1)[..., :N_META].astype(v.dtype)
    om = jnp.einsum("bkgqs,bskd->bqkgd", pm, vm).reshape(bsz, N_META, D_ATTN)

    qb = qr.reshape(bsz, nb, BLOCK, N_KV_HEADS, Q_PER_KV, ATTN_HEAD_DIM)

    def band(t):
        tb = t.reshape(bsz, nb, BLOCK, N_KV_HEADS, ATTN_HEAD_DIM)
        prev = jnp.pad(tb, ((0, 0), (1, 0), (0, 0), (0, 0), (0, 0)))[:, :-1]
        return jnp.concatenate([prev, tb], axis=2)

    kband, vband = band(kr), band(vr)
    qi = jnp.arange(BLOCK)[:, None]
    si = jnp.arange(2 * BLOCK)[None, :]
    in_window = (si > qi + BLOCK - WINDOW) & (si <= qi + BLOCK)
    valid = (jnp.arange(nb)[:, None, None] > 0) | (si >= BLOCK)[None]
    mask = in_window[None] & valid
    s_meta = jnp.einsum("bnqkgd,bmkd->bnkgqm", qb, km).astype(jnp.float32)
    s_band = jnp.einsum("bnqkgd,bnskd->bnkgqs", qb, kband).astype(jnp.float32)
    s_band = jnp.where(mask[None, :, None, None], s_band, NEG_INF)
    s_sink = jnp.broadcast_to(sink[None, None, :, :, None, None], s_meta.shape[:-1] + (1,))
    p = jax.nn.softmax(jnp.concatenate([s_meta, s_band, s_sink], -1), axis=-1).astype(v.dtype)
    ob = (jnp.einsum("bnkgqm,bmkd->bnqkgd", p[..., :N_META], vm)
          + jnp.einsum("bnkgqs,bnskd->bnqkgd", p[..., N_META:N_META + 2 * BLOCK], vband))
    ob = ob.reshape(bsz, s_real, D_ATTN)
    return jnp.concatenate([om, ob], axis=1) @ w_o


def conv_ffn(hn, w_up, conv_w, conv_b, w_down):
    u = causal_dwconv(hn @ w_up, conv_w, conv_b)
    gate, val = u[..., :D_FF], u[..., D_FF:]
    return (jax.nn.silu(gate) * val) @ w_down


def _fwd_setup_inputs(seed: int = 0) -> dict:
    key = jax.random.key(seed)
    ks = jax.random.split(key, 32)
    f32 = jnp.float32

    def nrm(k, shape, scale):
        return jax.random.normal(k, shape, f32) * scale

    def gain(k, shape):
        return 1.0 + 0.1 * jax.random.normal(k, shape, f32)

    na, nbl = N_A_LAYERS, N_B_LAYERS
    dt0 = jnp.exp(jax.random.uniform(ks[5], (na, SSM_HEADS), f32) * (math.log(0.1) - math.log(0.001)) + math.log(0.001))
    return {
        "x": jax.random.normal(ks[0], (BATCH, SEQ, D_MODEL), f32),
        "meta_tokens": nrm(ks[1], (N_META, D_MODEL), 1.0),
        "a_norm_pre": gain(ks[2], (na, D_MODEL)),
        "a_w_in": nrm(ks[3], (na, D_MODEL, D_IN_PROJ), D_MODEL ** -0.5),
        "a_conv_w": nrm(ks[4], (na, SSM_CONV, D_XBC), SSM_CONV ** -0.5),
        "a_conv_b": nrm(ks[6], (na, D_XBC), 0.02),
        "a_dt_bias": dt0 + jnp.log(-jnp.expm1(-dt0)),
        "a_a_log": jnp.log(jax.random.uniform(ks[7], (na, SSM_HEADS), f32, 1.0, 16.0)),
        "a_d_skip": gain(ks[8], (na, SSM_HEADS)),
        "a_gate_norm": gain(ks[9], (na, D_INNER)),
        "a_w_out": nrm(ks[10], (na, D_INNER, D_MODEL), D_INNER ** -0.5),
        "a_norm_post": gain(ks[11], (na, D_MODEL)),
        "kv_norm": gain(ks[12], (D_MODEL,)),
        "w_kv": nrm(ks[13], (D_MODEL, 2 * D_KV), D_MODEL ** -0.5),
        "b_norm_pre": gain(ks[14], (nbl, D_MODEL)),
        "b_w_q": nrm(ks[15], (nbl, D_MODEL, D_ATTN), D_MODEL ** -0.5),
        "b_sinks": nrm(ks[16], (nbl, N_Q_HEADS), 0.5),
        "b_w_o": nrm(ks[17], (nbl, D_ATTN, D_MODEL), D_ATTN ** -0.5),
        "b_norm_post": gain(ks[18], (nbl, D_MODEL)),
        "f_norm_pre": gain(ks[19], (DEPTH, D_MODEL)),
        "f_w_up": nrm(ks[20], (DEPTH, D_MODEL, 2 * D_FF), D_MODEL ** -0.5),
        "f_conv_w": nrm(ks[21], (DEPTH, FFN_CONV, 2 * D_FF), FFN_CONV ** -0.5),
        "f_conv_b": nrm(ks[22], (DEPTH, 2 * D_FF), 0.02),
        "f_w_down": nrm(ks[23], (DEPTH, D_FF, D_MODEL), D_FF ** -0.5),
        "f_norm_post": gain(ks[24], (DEPTH, D_MODEL)),
    }


def _fwd_reference(x, meta_tokens, a_norm_pre, a_w_in, a_conv_w, a_conv_b, a_dt_bias, a_a_log, a_d_skip,
              a_gate_norm, a_w_out, a_norm_post, kv_norm, w_kv, b_norm_pre, b_w_q, b_sinks, b_w_o,
              b_norm_post, f_norm_pre, f_w_up, f_conv_w, f_conv_b, f_w_down, f_norm_post):
    bsz = x.shape[0]
    h = jnp.concatenate([jnp.broadcast_to(meta_tokens[None].astype(x.dtype), (bsz, N_META, D_MODEL)), x], axis=1)
    seq_len = h.shape[1]
    k_shared = None
    v_shared = None
    for i in range(DEPTH):
        if i < N_A_LAYERS:
            j = i
            mix = mamba2_mixer(rms_norm(h, a_norm_pre[j]), a_w_in[j], a_conv_w[j], a_conv_b[j], a_dt_bias[j],
                               a_a_log[j], a_d_skip[j], a_gate_norm[j], a_w_out[j])
            h = h + rms_norm(mix, a_norm_post[j])
        else:
            j = i - N_A_LAYERS
            if j == 0:
                kv = (rms_norm(h, kv_norm) @ w_kv).reshape(bsz, seq_len, 2, N_KV_HEADS, ATTN_HEAD_DIM)
                k_shared, v_shared = kv[:, :, 0], kv[:, :, 1]
            mix = swa_sink_attention(rms_norm(h, b_norm_pre[j]), b_w_q[j], k_shared, v_shared, b_sinks[j], b_w_o[j])
            h = h + rms_norm(mix, b_norm_post[j])
        ffn = conv_ffn(rms_norm(h, f_norm_pre[i]), f_w_up[i], f_conv_w[i], f_conv_b[i], f_w_down[i])
        h = h + rms_norm(ffn, f_norm_post[i])
    return h[:, N_META:]


import jax as _jax
import jax.numpy as _jnp

TWIN_FORMAT = 'train_step'
FWD_PARAMS = ['x', 'meta_tokens', 'a_norm_pre', 'a_w_in', 'a_conv_w', 'a_conv_b', 'a_dt_bias', 'a_a_log', 'a_d_skip', 'a_gate_norm', 'a_w_out', 'a_norm_post', 'kv_norm', 'w_kv', 'b_norm_pre', 'b_w_q', 'b_sinks', 'b_w_o', 'b_norm_post', 'f_norm_pre', 'f_w_up', 'f_conv_w', 'f_conv_b', 'f_w_down', 'f_norm_post']
TWIN_WEIGHTS = ['meta_tokens', 'a_norm_pre', 'a_w_in', 'a_conv_w', 'a_conv_b', 'a_dt_bias', 'a_a_log', 'a_d_skip', 'a_gate_norm', 'a_w_out', 'a_norm_post', 'kv_norm', 'w_kv', 'b_norm_pre', 'b_w_q', 'b_sinks', 'b_w_o', 'b_norm_post', 'f_norm_pre', 'f_w_up', 'f_conv_w', 'f_conv_b', 'f_w_down', 'f_norm_post']
TWIN_DIFF_INPUT = 'x'
TWIN_INPUTS = ['x', 'meta_tokens', 'a_norm_pre', 'a_w_in', 'a_conv_w', 'a_conv_b', 'a_dt_bias', 'a_a_log', 'a_d_skip', 'a_gate_norm', 'a_w_out', 'a_norm_post', 'kv_norm', 'w_kv', 'b_norm_pre', 'b_w_q', 'b_sinks', 'b_w_o', 'b_norm_post', 'f_norm_pre', 'f_w_up', 'f_conv_w', 'f_conv_b', 'f_w_down', 'f_norm_post', 'loss_target', 'm_meta_tokens', 'm_a_norm_pre', 'm_a_w_in', 'm_a_conv_w', 'm_a_conv_b', 'm_a_dt_bias', 'm_a_a_log', 'm_a_d_skip', 'm_a_gate_norm', 'm_a_w_out', 'm_a_norm_post', 'm_kv_norm', 'm_w_kv', 'm_b_norm_pre', 'm_b_w_q', 'm_b_sinks', 'm_b_w_o', 'm_b_norm_post', 'm_f_norm_pre', 'm_f_w_up', 'm_f_conv_w', 'm_f_conv_b', 'm_f_w_down', 'm_f_norm_post', 'v_meta_tokens', 'v_a_norm_pre', 'v_a_w_in', 'v_a_conv_w', 'v_a_conv_b', 'v_a_dt_bias', 'v_a_a_log', 'v_a_d_skip', 'v_a_gate_norm', 'v_a_w_out', 'v_a_norm_post', 'v_kv_norm', 'v_w_kv', 'v_b_norm_pre', 'v_b_w_q', 'v_b_sinks', 'v_b_w_o', 'v_b_norm_post', 'v_f_norm_pre', 'v_f_w_up', 'v_f_conv_w', 'v_f_conv_b', 'v_f_w_down', 'v_f_norm_post']
TWIN_OUTPUTS = ['loss', 'grad_x', 'grad_meta_tokens', 'grad_a_norm_pre', 'grad_a_w_in', 'grad_a_conv_w', 'grad_a_conv_b', 'grad_a_dt_bias', 'grad_a_a_log', 'grad_a_d_skip', 'grad_a_gate_norm', 'grad_a_w_out', 'grad_a_norm_post', 'grad_kv_norm', 'grad_w_kv', 'grad_b_norm_pre', 'grad_b_w_q', 'grad_b_sinks', 'grad_b_w_o', 'grad_b_norm_post', 'grad_f_norm_pre', 'grad_f_w_up', 'grad_f_conv_w', 'grad_f_conv_b', 'grad_f_w_down', 'grad_f_norm_post', 'delta_meta_tokens', 'delta_a_norm_pre', 'delta_a_w_in', 'delta_a_conv_w', 'delta_a_conv_b', 'delta_a_dt_bias', 'delta_a_a_log', 'delta_a_d_skip', 'delta_a_gate_norm', 'delta_a_w_out', 'delta_a_norm_post', 'delta_kv_norm', 'delta_w_kv', 'delta_b_norm_pre', 'delta_b_w_q', 'delta_b_sinks', 'delta_b_w_o', 'delta_b_norm_post', 'delta_f_norm_pre', 'delta_f_w_up', 'delta_f_conv_w', 'delta_f_conv_b', 'delta_f_w_down', 'delta_f_norm_post', 'new_m_meta_tokens', 'new_m_a_norm_pre', 'new_m_a_w_in', 'new_m_a_conv_w', 'new_m_a_conv_b', 'new_m_a_dt_bias', 'new_m_a_a_log', 'new_m_a_d_skip', 'new_m_a_gate_norm', 'new_m_a_w_out', 'new_m_a_norm_post', 'new_m_kv_norm', 'new_m_w_kv', 'new_m_b_norm_pre', 'new_m_b_w_q', 'new_m_b_sinks', 'new_m_b_w_o', 'new_m_b_norm_post', 'new_m_f_norm_pre', 'new_m_f_w_up', 'new_m_f_conv_w', 'new_m_f_conv_b', 'new_m_f_w_down', 'new_m_f_norm_post', 'new_v_meta_tokens', 'new_v_a_norm_pre', 'new_v_a_w_in', 'new_v_a_conv_w', 'new_v_a_conv_b', 'new_v_a_dt_bias', 'new_v_a_a_log', 'new_v_a_d_skip', 'new_v_a_gate_norm', 'new_v_a_w_out', 'new_v_a_norm_post', 'new_v_kv_norm', 'new_v_w_kv', 'new_v_b_norm_pre', 'new_v_b_w_q', 'new_v_b_sinks', 'new_v_b_w_o', 'new_v_b_norm_post', 'new_v_f_norm_pre', 'new_v_f_w_up', 'new_v_f_conv_w', 'new_v_f_conv_b', 'new_v_f_w_down', 'new_v_f_norm_post']
TWIN_LEAF_KINDS = {'loss': 'loss', 'grad_x': 'grad_x', 'grad_meta_tokens': 'grad_w', 'grad_a_norm_pre': 'grad_w', 'grad_a_w_in': 'grad_w', 'grad_a_conv_w': 'grad_w', 'grad_a_conv_b': 'grad_w', 'grad_a_dt_bias': 'grad_w', 'grad_a_a_log': 'grad_w', 'grad_a_d_skip': 'grad_w', 'grad_a_gate_norm': 'grad_w', 'grad_a_w_out': 'grad_w', 'grad_a_norm_post': 'grad_w', 'grad_kv_norm': 'grad_w', 'grad_w_kv': 'grad_w', 'grad_b_norm_pre': 'grad_w', 'grad_b_w_q': 'grad_w', 'grad_b_sinks': 'grad_w', 'grad_b_w_o': 'grad_w', 'grad_b_norm_post': 'grad_w', 'grad_f_norm_pre': 'grad_w', 'grad_f_w_up': 'grad_w', 'grad_f_conv_w': 'grad_w', 'grad_f_conv_b': 'grad_w', 'grad_f_w_down': 'grad_w', 'grad_f_norm_post': 'grad_w', 'delta_meta_tokens': 'delta_w', 'delta_a_norm_pre': 'delta_w', 'delta_a_w_in': 'delta_w', 'delta_a_conv_w': 'delta_w', 'delta_a_conv_b': 'delta_w', 'delta_a_dt_bias': 'delta_w', 'delta_a_a_log': 'delta_w', 'delta_a_d_skip': 'delta_w', 'delta_a_gate_norm': 'delta_w', 'delta_a_w_out': 'delta_w', 'delta_a_norm_post': 'delta_w', 'delta_kv_norm': 'delta_w', 'delta_w_kv': 'delta_w', 'delta_b_norm_pre': 'delta_w', 'delta_b_w_q': 'delta_w', 'delta_b_sinks': 'delta_w', 'delta_b_w_o': 'delta_w', 'delta_b_norm_post': 'delta_w', 'delta_f_norm_pre': 'delta_w', 'delta_f_w_up': 'delta_w', 'delta_f_conv_w': 'delta_w', 'delta_f_conv_b': 'delta_w', 'delta_f_w_down': 'delta_w', 'delta_f_norm_post': 'delta_w', 'new_m_meta_tokens': 'new_m', 'new_m_a_norm_pre': 'new_m', 'new_m_a_w_in': 'new_m', 'new_m_a_conv_w': 'new_m', 'new_m_a_conv_b': 'new_m', 'new_m_a_dt_bias': 'new_m', 'new_m_a_a_log': 'new_m', 'new_m_a_d_skip': 'new_m', 'new_m_a_gate_norm': 'new_m', 'new_m_a_w_out': 'new_m', 'new_m_a_norm_post': 'new_m', 'new_m_kv_norm': 'new_m', 'new_m_w_kv': 'new_m', 'new_m_b_norm_pre': 'new_m', 'new_m_b_w_q': 'new_m', 'new_m_b_sinks': 'new_m', 'new_m_b_w_o': 'new_m', 'new_m_b_norm_post': 'new_m', 'new_m_f_norm_pre': 'new_m', 'new_m_f_w_up': 'new_m', 'new_m_f_conv_w': 'new_m', 'new_m_f_conv_b': 'new_m', 'new_m_f_w_down': 'new_m', 'new_m_f_norm_post': 'new_m', 'new_v_meta_tokens': 'new_v', 'new_v_a_norm_pre': 'new_v', 'new_v_a_w_in': 'new_v', 'new_v_a_conv_w': 'new_v', 'new_v_a_conv_b': 'new_v', 'new_v_a_dt_bias': 'new_v', 'new_v_a_a_log': 'new_v', 'new_v_a_d_skip': 'new_v', 'new_v_a_gate_norm': 'new_v', 'new_v_a_w_out': 'new_v', 'new_v_a_norm_post': 'new_v', 'new_v_kv_norm': 'new_v', 'new_v_w_kv': 'new_v', 'new_v_b_norm_pre': 'new_v', 'new_v_b_w_q': 'new_v', 'new_v_b_sinks': 'new_v', 'new_v_b_w_o': 'new_v', 'new_v_b_norm_post': 'new_v', 'new_v_f_norm_pre': 'new_v', 'new_v_f_w_up': 'new_v', 'new_v_f_conv_w': 'new_v', 'new_v_f_conv_b': 'new_v', 'new_v_f_w_down': 'new_v', 'new_v_f_norm_post': 'new_v'}


def _forward(args):
    return _fwd_reference(*[args[k] for k in FWD_PARAMS])


def _output_shape():
    out = _jax.eval_shape(lambda: _forward(_fwd_setup_inputs(0)))
    return out.shape, out.dtype

N_MICROBATCH = 1
ADAM_LR = 0.001
ADAM_B1 = 0.9
ADAM_B2 = 0.999
ADAM_EPS = 1e-08
ADAM_WD = 0.01
ADAM_STEP = 10
PER_EXAMPLE_BATCH_AXIS = {'x': 0, 'loss_target': 0}
SHARED_INPUTS = []
_WEIGHT_DTYPES = {'meta_tokens': _jnp.float32, 'a_norm_pre': _jnp.float32, 'a_w_in': _jnp.float32, 'a_conv_w': _jnp.float32, 'a_conv_b': _jnp.float32, 'a_dt_bias': _jnp.float32, 'a_a_log': _jnp.float32, 'a_d_skip': _jnp.float32, 'a_gate_norm': _jnp.float32, 'a_w_out': _jnp.float32, 'a_norm_post': _jnp.float32, 'kv_norm': _jnp.float32, 'w_kv': _jnp.float32, 'b_norm_pre': _jnp.float32, 'b_w_q': _jnp.float32, 'b_sinks': _jnp.float32, 'b_w_o': _jnp.float32, 'b_norm_post': _jnp.float32, 'f_norm_pre': _jnp.float32, 'f_w_up': _jnp.float32, 'f_conv_w': _jnp.float32, 'f_conv_b': _jnp.float32, 'f_w_down': _jnp.float32, 'f_norm_post': _jnp.float32}
MOMENT_SCALE = {'meta_tokens': 3.162190e-01, 'a_norm_pre': 1.577058e+00, 'a_w_in': 6.585749e-01, 'a_conv_w': 1.757830e+00, 'a_conv_b': 5.672012e+00, 'a_dt_bias': 2.602869e+00, 'a_a_log': 1.185802e+01, 'a_d_skip': 1.256957e+01, 'a_gate_norm': 3.203659e+00, 'a_w_out': 4.476441e+00, 'a_norm_post': 1.668435e+01, 'kv_norm': 4.260168e+00, 'w_kv': 5.425676e+00, 'b_norm_pre': 3.452882e-01, 'b_w_q': 3.521054e-01, 'b_sinks': 1.046079e-01, 'b_w_o': 3.652384e+00, 'b_norm_post': 1.847857e+01, 'f_norm_pre': 1.815588e+00, 'f_w_up': 7.848276e-01, 'f_conv_w': 8.742690e-01, 'f_conv_b': 3.317816e+00, 'f_w_down': 1.552676e+00, 'f_norm_post': 1.626364e+01}


def _to_microbatches(a, axis):
    t = _jnp.moveaxis(a, axis, 0)
    t = t.reshape((N_MICROBATCH, t.shape[0] // N_MICROBATCH) + t.shape[1:])
    return _jnp.moveaxis(t, 1, axis + 1)


def setup_inputs(seed: int = 0) -> dict:
    inp = _fwd_setup_inputs(seed)
    key = _jax.random.fold_in(_jax.random.key(seed), 7919)
    shape, _ = _output_shape()
    out = dict(inp)
    out["loss_target"] = _jax.random.normal(_jax.random.fold_in(key, 0), shape, _jnp.float32)
    for i, name in enumerate(TWIN_WEIGHTS):
        w = inp[name].astype(_jnp.float32)
        if MOMENT_SCALE is None:
            s = _jnp.sqrt(_jnp.mean(_jnp.square(w)) + 1e-30)
        else:
            s = MOMENT_SCALE[name]
        km, kv = _jax.random.split(_jax.random.fold_in(key, i + 1))
        out[name] = w
        out["m_" + name] = s * _jax.random.normal(km, w.shape, _jnp.float32)
        out["v_" + name] = (s * s) * _jax.random.uniform(kv, w.shape, _jnp.float32, 0.5, 1.5)
    if N_MICROBATCH > 1:
        for name, axis in PER_EXAMPLE_BATCH_AXIS.items():
            out[name] = _to_microbatches(out[name], axis)
    return {'x': out['x'], 'meta_tokens': out['meta_tokens'], 'a_norm_pre': out['a_norm_pre'], 'a_w_in': out['a_w_in'], 'a_conv_w': out['a_conv_w'], 'a_conv_b': out['a_conv_b'], 'a_dt_bias': out['a_dt_bias'], 'a_a_log': out['a_a_log'], 'a_d_skip': out['a_d_skip'], 'a_gate_norm': out['a_gate_norm'], 'a_w_out': out['a_w_out'], 'a_norm_post': out['a_norm_post'], 'kv_norm': out['kv_norm'], 'w_kv': out['w_kv'], 'b_norm_pre': out['b_norm_pre'], 'b_w_q': out['b_w_q'], 'b_sinks': out['b_sinks'], 'b_w_o': out['b_w_o'], 'b_norm_post': out['b_norm_post'], 'f_norm_pre': out['f_norm_pre'], 'f_w_up': out['f_w_up'], 'f_conv_w': out['f_conv_w'], 'f_conv_b': out['f_conv_b'], 'f_w_down': out['f_w_down'], 'f_norm_post': out['f_norm_post'], 'loss_target': out['loss_target'], 'm_meta_tokens': out['m_meta_tokens'], 'm_a_norm_pre': out['m_a_norm_pre'], 'm_a_w_in': out['m_a_w_in'], 'm_a_conv_w': out['m_a_conv_w'], 'm_a_conv_b': out['m_a_conv_b'], 'm_a_dt_bias': out['m_a_dt_bias'], 'm_a_a_log': out['m_a_a_log'], 'm_a_d_skip': out['m_a_d_skip'], 'm_a_gate_norm': out['m_a_gate_norm'], 'm_a_w_out': out['m_a_w_out'], 'm_a_norm_post': out['m_a_norm_post'], 'm_kv_norm': out['m_kv_norm'], 'm_w_kv': out['m_w_kv'], 'm_b_norm_pre': out['m_b_norm_pre'], 'm_b_w_q': out['m_b_w_q'], 'm_b_sinks': out['m_b_sinks'], 'm_b_w_o': out['m_b_w_o'], 'm_b_norm_post': out['m_b_norm_post'], 'm_f_norm_pre': out['m_f_norm_pre'], 'm_f_w_up': out['m_f_w_up'], 'm_f_conv_w': out['m_f_conv_w'], 'm_f_conv_b': out['m_f_conv_b'], 'm_f_w_down': out['m_f_w_down'], 'm_f_norm_post': out['m_f_norm_post'], 'v_meta_tokens': out['v_meta_tokens'], 'v_a_norm_pre': out['v_a_norm_pre'], 'v_a_w_in': out['v_a_w_in'], 'v_a_conv_w': out['v_a_conv_w'], 'v_a_conv_b': out['v_a_conv_b'], 'v_a_dt_bias': out['v_a_dt_bias'], 'v_a_a_log': out['v_a_a_log'], 'v_a_d_skip': out['v_a_d_skip'], 'v_a_gate_norm': out['v_a_gate_norm'], 'v_a_w_out': out['v_a_w_out'], 'v_a_norm_post': out['v_a_norm_post'], 'v_kv_norm': out['v_kv_norm'], 'v_w_kv': out['v_w_kv'], 'v_b_norm_pre': out['v_b_norm_pre'], 'v_b_w_q': out['v_b_w_q'], 'v_b_sinks': out['v_b_sinks'], 'v_b_w_o': out['v_b_w_o'], 'v_b_norm_post': out['v_b_norm_post'], 'v_f_norm_pre': out['v_f_norm_pre'], 'v_f_w_up': out['v_f_w_up'], 'v_f_conv_w': out['v_f_conv_w'], 'v_f_conv_b': out['v_f_conv_b'], 'v_f_w_down': out['v_f_w_down'], 'v_f_norm_post': out['v_f_norm_post']}


def _loss(weights, diff, rest, loss_target):
    with _jax.named_scope("forward"):
        args = {**rest, TWIN_DIFF_INPUT: diff, **{k: w.astype(_WEIGHT_DTYPES[k]) for k, w in weights.items()}}
        y = _forward(args)
    with _jax.named_scope("loss_head"):
        err = _jnp.square(y.astype(_jnp.float32) - loss_target)
        return 0.5 * _jnp.sum(_jnp.mean(err, axis=-1)) if err.ndim else 0.5 * err


def _adamw(w, g, m, v):
    m = ADAM_B1 * m + (1.0 - ADAM_B1) * g
    v = ADAM_B2 * v + (1.0 - ADAM_B2) * _jnp.square(g)
    m_hat = m / (1.0 - ADAM_B1 ** ADAM_STEP)
    v_hat = v / (1.0 - ADAM_B2 ** ADAM_STEP)
    delta = -ADAM_LR * (m_hat / (_jnp.sqrt(v_hat) + ADAM_EPS) + ADAM_WD * w)
    return delta, m, v


def reference(x, meta_tokens, a_norm_pre, a_w_in, a_conv_w, a_conv_b, a_dt_bias, a_a_log, a_d_skip, a_gate_norm, a_w_out, a_norm_post, kv_norm, w_kv, b_norm_pre, b_w_q, b_sinks, b_w_o, b_norm_post, f_norm_pre, f_w_up, f_conv_w, f_conv_b, f_w_down, f_norm_post, loss_target, m_meta_tokens, m_a_norm_pre, m_a_w_in, m_a_conv_w, m_a_conv_b, m_a_dt_bias, m_a_a_log, m_a_d_skip, m_a_gate_norm, m_a_w_out, m_a_norm_post, m_kv_norm, m_w_kv, m_b_norm_pre, m_b_w_q, m_b_sinks, m_b_w_o, m_b_norm_post, m_f_norm_pre, m_f_w_up, m_f_conv_w, m_f_conv_b, m_f_w_down, m_f_norm_post, v_meta_tokens, v_a_norm_pre, v_a_w_in, v_a_conv_w, v_a_conv_b, v_a_dt_bias, v_a_a_log, v_a_d_skip, v_a_gate_norm, v_a_w_out, v_a_norm_post, v_kv_norm, v_w_kv, v_b_norm_pre, v_b_w_q, v_b_sinks, v_b_w_o, v_b_norm_post, v_f_norm_pre, v_f_w_up, v_f_conv_w, v_f_conv_b, v_f_w_down, v_f_norm_post):
    given = dict(x=x, meta_tokens=meta_tokens, a_norm_pre=a_norm_pre, a_w_in=a_w_in, a_conv_w=a_conv_w, a_conv_b=a_conv_b, a_dt_bias=a_dt_bias, a_a_log=a_a_log, a_d_skip=a_d_skip, a_gate_norm=a_gate_norm, a_w_out=a_w_out, a_norm_post=a_norm_post, kv_norm=kv_norm, w_kv=w_kv, b_norm_pre=b_norm_pre, b_w_q=b_w_q, b_sinks=b_sinks, b_w_o=b_w_o, b_norm_post=b_norm_post, f_norm_pre=f_norm_pre, f_w_up=f_w_up, f_conv_w=f_conv_w, f_conv_b=f_conv_b, f_w_down=f_w_down, f_norm_post=f_norm_post, loss_target=loss_target, m_meta_tokens=m_meta_tokens, m_a_norm_pre=m_a_norm_pre, m_a_w_in=m_a_w_in, m_a_conv_w=m_a_conv_w, m_a_conv_b=m_a_conv_b, m_a_dt_bias=m_a_dt_bias, m_a_a_log=m_a_a_log, m_a_d_skip=m_a_d_skip, m_a_gate_norm=m_a_gate_norm, m_a_w_out=m_a_w_out, m_a_norm_post=m_a_norm_post, m_kv_norm=m_kv_norm, m_w_kv=m_w_kv, m_b_norm_pre=m_b_norm_pre, m_b_w_q=m_b_w_q, m_b_sinks=m_b_sinks, m_b_w_o=m_b_w_o, m_b_norm_post=m_b_norm_post, m_f_norm_pre=m_f_norm_pre, m_f_w_up=m_f_w_up, m_f_conv_w=m_f_conv_w, m_f_conv_b=m_f_conv_b, m_f_w_down=m_f_w_down, m_f_norm_post=m_f_norm_post, v_meta_tokens=v_meta_tokens, v_a_norm_pre=v_a_norm_pre, v_a_w_in=v_a_w_in, v_a_conv_w=v_a_conv_w, v_a_conv_b=v_a_conv_b, v_a_dt_bias=v_a_dt_bias, v_a_a_log=v_a_a_log, v_a_d_skip=v_a_d_skip, v_a_gate_norm=v_a_gate_norm, v_a_w_out=v_a_w_out, v_a_norm_post=v_a_norm_post, v_kv_norm=v_kv_norm, v_w_kv=v_w_kv, v_b_norm_pre=v_b_norm_pre, v_b_w_q=v_b_w_q, v_b_sinks=v_b_sinks, v_b_w_o=v_b_w_o, v_b_norm_post=v_b_norm_post, v_f_norm_pre=v_f_norm_pre, v_f_w_up=v_f_w_up, v_f_conv_w=v_f_conv_w, v_f_conv_b=v_f_conv_b, v_f_w_down=v_f_w_down, v_f_norm_post=v_f_norm_post)
    weights = {n: given[n] for n in TWIN_WEIGHTS}
    shared = {n: given[n] for n in SHARED_INPUTS}
    per_example = {n: given[n] for n in ['x']}
    grad_fn = _jax.value_and_grad(_loss, argnums=(0, 1))

    def one_microbatch(ex, loss_target):
        ex = dict(ex)
        diff = ex.pop(TWIN_DIFF_INPUT)
        return grad_fn(weights, diff, {**shared, **ex}, loss_target)

    if N_MICROBATCH == 1:
        loss, (grad_w, grad_x) = one_microbatch(per_example, given["loss_target"])
    else:
        def body(carry, xs):
            loss_sum, grad_sum = carry
            l_k, (gw_k, gx_k) = one_microbatch(xs[0], xs[1])
            with _jax.named_scope("update"):
                return (loss_sum + l_k, _jax.tree.map(_jnp.add, grad_sum, gw_k)), gx_k

        init = (_jnp.zeros((), _jnp.float32), _jax.tree.map(_jnp.zeros_like, weights))
        (loss, grad_w), grad_x = _jax.lax.scan(body, init, (per_example, given["loss_target"]))
    with _jax.named_scope("update"):
        delta_w, new_m, new_v = {}, {}, {}
        for n in TWIN_WEIGHTS:
            delta_w[n], new_m[n], new_v[n] = _adamw(weights[n], grad_w[n], given["m_" + n], given["v_" + n])
    return (loss, grad_x, *[grad_w[n] for n in TWIN_WEIGHTS], *[delta_w[n] for n in TWIN_WEIGHTS],
            *[new_m[n] for n in TWIN_WEIGHTS], *[new_v[n] for n in TWIN_WEIGHTS])
```

```python
import functools
import math

import jax
import jax.numpy as jnp
from jax import lax
from jax.experimental import pallas as pl
from jax.experimental.pallas import tpu as pltpu

F32, BF16 = jnp.float32, jnp.bfloat16
S = jax.ShapeDtypeStruct

D_MODEL = 1024
SEQ = 2048
N_META = 16
D_INNER = 2048
HEAD_P = 64
SSM_HEADS = D_INNER // HEAD_P
SSM_GROUPS = 4
D_STATE = 128
SSM_CONV = 4
D_BC = SSM_GROUPS * D_STATE
D_XBC = D_INNER + 2 * D_BC
ATTN_DH = 64
N_Q_HEADS = D_MODEL // ATTN_DH
N_KV_HEADS = 4
D_KV = N_KV_HEADS * ATTN_DH
WINDOW = 128
D_FF = 2816
FFN_CONV = 3
RMS_EPS = 1e-6
NEG = -1e30
LR, B1, B2, EPS, WD, STEP = 0.001, 0.9, 0.999, 1e-08, 0.01, 10

N_DEV = 8
T = 128
LANE = 128
VMEM_LIMIT = 48 * 1024 * 1024

BIG = ("a_w_in", "a_w_out", "w_kv", "b_w_q", "b_w_o", "f_w_up", "f_w_down")
SMALL = ("meta_tokens", "a_norm_pre", "a_conv_w", "a_conv_b", "a_gate_norm", "a_norm_post", "f_conv_w")
REPL = ("a_dt_bias", "a_a_log", "a_d_skip", "kv_norm", "b_norm_pre", "b_sinks", "b_norm_post",
        "f_norm_pre", "f_conv_b", "f_norm_post")
SHARD_AXIS = dict(a_w_in=2, a_w_out=1, w_kv=0, b_w_q=1, b_w_o=1, f_w_up=2, f_w_down=1, meta_tokens=1,
                  a_norm_pre=1, a_conv_w=2, a_conv_b=1, a_gate_norm=1, a_norm_post=1, f_conv_w=2)
WEIGHTS = ("meta_tokens", "a_norm_pre", "a_w_in", "a_conv_w", "a_conv_b", "a_dt_bias", "a_a_log", "a_d_skip",
           "a_gate_norm", "a_w_out", "a_norm_post", "kv_norm", "w_kv", "b_norm_pre", "b_w_q", "b_sinks", "b_w_o",
           "b_norm_post", "f_norm_pre", "f_w_up", "f_conv_w", "f_conv_b", "f_w_down", "f_norm_post")


def _seq_rows():
    return -(-(N_META + SEQ) // T) * T


def _cp(sem=None):
    return pltpu.CompilerParams(dimension_semantics=sem, vmem_limit_bytes=VMEM_LIMIT)


def _pick(n, target):
    t = min(n, target)
    t -= t % LANE
    while n % t:
        t -= LANE
    return t


def _sigmoid(x):
    return 1.0 / (1.0 + jnp.exp(-x))


def _softplus(x):
    return jnp.maximum(x, 0.0) + jnp.log(1.0 + jnp.exp(-jnp.abs(x)))


_NN = (((1,), (0,)), ((), ()))
_NT = (((1,), (1,)), ((), ()))
_TN = (((0,), (0,)), ((), ()))


def _dot(a, b, dims=_NN):
    return lax.dot_general(a, b, dims, preferred_element_type=F32)


def _dot_hi(a, b):
    return lax.dot_general(a, b, _NN, precision=lax.Precision.HIGHEST, preferred_element_type=F32)


def _mm(a, b, mode, out_dtype, name):
    if mode == "tn":
        m, kk = a.shape
        n = b.shape[1]
        tko, tn = _pick(kk, 512), _pick(n, 512)

        def body(a_ref, b_ref, o_ref):
            o_ref[...] = _dot(a_ref[...], b_ref[...], _TN).astype(o_ref.dtype)

        return pl.pallas_call(
            body, name=name, out_shape=S((kk, n), out_dtype), grid=(kk // tko, n // tn),
            in_specs=[pl.BlockSpec((m, tko), lambda i, j: (0, i)), pl.BlockSpec((m, tn), lambda i, j: (0, j))],
            out_specs=pl.BlockSpec((tko, tn), lambda i, j: (i, j)),
            compiler_params=_cp(("parallel", "parallel")))(a, b)

    m, kk = a.shape
    n = b.shape[1] if mode == "nn" else b.shape[0]
    tn = _pick(n, 512)
    tk = kk if kk <= 2048 else _pick(kk, 1536)
    nk = kk // tk
    dims = _NN if mode == "nn" else _NT

    def body(a_ref, b_ref, o_ref, *acc):
        part = _dot(a_ref[...], b_ref[...], dims)
        if nk == 1:
            o_ref[...] = part.astype(o_ref.dtype)
        else:
            k = pl.program_id(1)

            @pl.when(k == 0)
            def _():
                acc[0][...] = part

            @pl.when(k > 0)
            def _():
                acc[0][...] += part

            @pl.when(k == nk - 1)
            def _():
                o_ref[...] = acc[0][...].astype(o_ref.dtype)

    b_spec = (pl.BlockSpec((tk, tn), lambda j, k: (k, j)) if mode == "nn"
              else pl.BlockSpec((tn, tk), lambda j, k: (j, k)))
    return pl.pallas_call(
        body, name=name, out_shape=S((m, n), out_dtype), grid=(n // tn, nk),
        in_specs=[pl.BlockSpec((m, tk), lambda j, k: (0, k)), b_spec],
        out_specs=pl.BlockSpec((m, tn), lambda j, k: (0, j)),
        scratch_shapes=[pltpu.VMEM((m, tn), F32)] if nk > 1 else [],
        compiler_params=_cp(("parallel", "arbitrary")))(a, b)


def _rms(x, w):
    return x * lax.rsqrt(jnp.mean(x * x, axis=-1, keepdims=True) + RMS_EPS) * w


def _row_tile(rows):
    return rows // 8


def _resid_norm(h, br, w_post, next_ws, name):
    rows, d = h.shape
    tr = _row_tile(rows)
    has_br = br is not None
    nw = len(next_ws)

    def body(*refs):
        h_ref = refs[0]
        pos = 1
        x = h_ref[...]
        if has_br:
            x = x + _rms(refs[1][...], refs[2][...])
            pos = 3
        w_refs = refs[pos:pos + nw]
        outs = refs[pos + nw:]
        if has_br:
            outs[0][...] = x
            outs = outs[1:]
        for w_ref, o_ref in zip(w_refs, outs):
            o_ref[...] = _rms(x, w_ref[...]).astype(o_ref.dtype)

    row = pl.BlockSpec((tr, d), lambda i: (i, 0))
    vec = pl.BlockSpec((1, d), lambda i: (0, 0))
    ins = [h] + ([br, w_post] if has_br else []) + list(next_ws)
    in_specs = [row] + ([row, vec] if has_br else []) + [vec] * nw
    out_shape = ([S((rows, d), F32)] if has_br else []) + [S((rows, d), BF16)] * nw
    res = pl.pallas_call(body, name=name, out_shape=out_shape, grid=(rows // tr,), in_specs=in_specs,
                         out_specs=[row] * len(out_shape), compiler_params=_cp(("parallel",)))(*ins)
    if has_br:
        return res[0], list(res[1:])
    return h, list(res)


def _norm_bwd(x, w, dy, add, out_dtype, name):
    rows, d = x.shape
    tr = _row_tile(rows)
    has_add = add is not None

    def body(*refs):
        x_ref, w_ref, dy_ref = refs[:3]
        dx_ref, dw_ref = refs[-2:]
        xv = x_ref[...]
        r = lax.rsqrt(jnp.mean(xv * xv, axis=-1, keepdims=True) + RMS_EPS)
        dyv = dy_ref[...].astype(F32)
        wdy = dyv * w_ref[...]
        dx = r * wdy - xv * (r * r * r) * jnp.mean(xv * wdy, axis=-1, keepdims=True)
        if has_add:
            dx = dx + refs[3][...]
        dx_ref[...] = dx.astype(dx_ref.dtype)

        @pl.when(pl.program_id(0) == 0)
        def _():
            dw_ref[...] = jnp.zeros_like(dw_ref)

        dw_ref[...] += jnp.sum(dyv * xv * r, axis=0, keepdims=True)

    row = pl.BlockSpec((tr, d), lambda i: (i, 0))
    vec = pl.BlockSpec((1, d), lambda i: (0, 0))
    ins = [x, w, dy] + ([add] if has_add else [])
    return pl.pallas_call(body, name=name, out_shape=[S((rows, d), out_dtype), S((1, d), F32)], grid=(rows // tr,),
                          in_specs=[row, vec, row] + ([row] if has_add else []), out_specs=[row, vec],
                          compiler_params=_cp(("arbitrary",)))(*ins)


def _final_loss(h, br, w_post, tgt, name):
    rows, d = h.shape
    tr = _row_tile(rows)

    def body(h_ref, br_ref, w_ref, t_ref, dh_ref, loss_ref):
        i = pl.program_id(0)
        y = h_ref[...] + _rms(br_ref[...], w_ref[...])
        r = i * tr + lax.broadcasted_iota(jnp.int32, (tr, 1), 0)
        real = (r >= N_META) & (r < N_META + SEQ)
        diff = jnp.where(real, y - t_ref[...], 0.0)
        dh_ref[...] = diff * (1.0 / d)

        @pl.when(i == 0)
        def _():
            loss_ref[...] = jnp.zeros_like(loss_ref)

        loss_ref[...] += jnp.sum(diff * diff) * (0.5 / d)

    row = pl.BlockSpec((tr, d), lambda i: (i, 0))
    return pl.pallas_call(body, name=name, out_shape=[S((rows, d), F32), S((1, LANE), F32)], grid=(rows // tr,),
                          in_specs=[row, row, pl.BlockSpec((1, d), lambda i: (0, 0)), row],
                          out_specs=[row, pl.BlockSpec((1, LANE), lambda i: (0, 0))],
                          compiler_params=_cp(("arbitrary",)))(h, br, w_post, tgt)


def _gatenorm_fwd(y, zx, w, name):
    rows, d = y.shape
    tr = _row_tile(rows)

    def body(y_ref, z_ref, w_ref, o_ref):
        z = z_ref[...]
        o_ref[...] = _rms(y_ref[...] * z * _sigmoid(z), w_ref[...]).astype(o_ref.dtype)

    row = pl.BlockSpec((tr, d), lambda i: (i, 0))
    return pl.pallas_call(body, name=name, out_shape=S((rows, d), BF16), grid=(rows // tr,),
                          in_specs=[row, row, pl.BlockSpec((1, d), lambda i: (0, 0))], out_specs=row,
                          compiler_params=_cp(("parallel",)))(y, zx, w)


def _gatenorm_bwd(y, zx, w, dyn, name):
    rows, d = y.shape
    tr = _row_tile(rows)

    def body(y_ref, z_ref, w_ref, dyn_ref, dy_ref, dz_ref, dw_ref):
        yv, z = y_ref[...], z_ref[...]
        sg = _sigmoid(z)
        sz = z * sg
        g = yv * sz
        r = lax.rsqrt(jnp.mean(g * g, axis=-1, keepdims=True) + RMS_EPS)
        dyn_v = dyn_ref[...]
        wdy = dyn_v * w_ref[...]
        dg = r * wdy - g * (r * r * r) * jnp.mean(g * wdy, axis=-1, keepdims=True)
        dy_ref[...] = dg * sz
        dz_ref[...] = (dg * yv * sg * (1.0 + z * (1.0 - sg))).astype(dz_ref.dtype)

        @pl.when(pl.program_id(0) == 0)
        def _():
            dw_ref[...] = jnp.zeros_like(dw_ref)

        dw_ref[...] += jnp.sum(dyn_v * g * r, axis=0, keepdims=True)

    row = pl.BlockSpec((tr, d), lambda i: (i, 0))
    vec = pl.BlockSpec((1, d), lambda i: (0, 0))
    return pl.pallas_call(body, name=name, out_shape=[S((rows, d), F32), S((rows, d), BF16), S((1, d), F32)],
                          grid=(rows // tr,), in_specs=[row, row, vec, row], out_specs=[row, row, vec],
                          compiler_params=_cp(("arbitrary",)))(y, zx, w, dyn)


def _shift_down(x, s, rows_iota):
    if s == 0:
        return x
    return jnp.where(rows_iota >= s, pltpu.roll(x, s, 0), 0.0)


def _shift_up(x, s, rows_iota):
    if s == 0:
        return x
    rows = x.shape[0]
    return jnp.where(rows_iota < rows - s, pltpu.roll(x, rows - s, 0), 0.0)


def _r16(v):
    return v.astype(BF16).astype(F32)


def _conv(x, w_ref, b_ref, taps, rows_iota):
    x = _r16(x)
    acc = jnp.zeros_like(x)
    for k in range(taps):
        acc = acc + _r16(w_ref[k:k + 1, :]) * _shift_down(x, taps - 1 - k, rows_iota)
    return acc + b_ref[...]


def _conv_bwd(x, du, w_ref, dw_ref, db_ref, taps, rows_iota):
    db_ref[...] = jnp.sum(du, axis=0, keepdims=True)
    x, du = _r16(x), _r16(du)
    dx = jnp.zeros_like(x)
    for k in range(taps):
        s = taps - 1 - k
        dx = dx + _r16(w_ref[k:k + 1, :]) * _shift_up(du, s, rows_iota)
        dw_ref[k:k + 1, :] = jnp.sum(du * _shift_down(x, s, rows_iota), axis=0, keepdims=True)
    return dx


def _conv_silu_fwd(zx, w, b, name):
    rows = zx.shape[0]
    cb = 512
    off = D_INNER // cb

    def body(x_ref, w_ref, b_ref, o_ref):
        it = lax.broadcasted_iota(jnp.int32, (rows, 1), 0)
        u = _conv(x_ref[...], w_ref, b_ref, SSM_CONV, it)
        o_ref[...] = u * _sigmoid(u)

    return pl.pallas_call(
        body, name=name, out_shape=S((rows, D_XBC), F32), grid=(D_XBC // cb,),
        in_specs=[pl.BlockSpec((rows, cb), lambda j: (0, off + j)), pl.BlockSpec((SSM_CONV, cb), lambda j: (0, j)),
                  pl.BlockSpec((1, cb), lambda j: (0, j))],
        out_specs=pl.BlockSpec((rows, cb), lambda j: (0, j)), compiler_params=_cp(("parallel",)))(zx, w, b)


def _conv_silu_bwd(zx, dxbc, w, b, name):
    rows = zx.shape[0]
    cb = 512
    off = D_INNER // cb

    def body(x_ref, d_ref, w_ref, b_ref, dx_ref, dw_ref, db_ref):
        it = lax.broadcasted_iota(jnp.int32, (rows, 1), 0)
        x = x_ref[...]
        u = _conv(x, w_ref, b_ref, SSM_CONV, it)
        sg = _sigmoid(u)
        du = d_ref[...] * sg * (1.0 + u * (1.0 - sg))
        dx_ref[...] = _conv_bwd(x, du, w_ref, dw_ref, db_ref, SSM_CONV, it).astype(dx_ref.dtype)

    col = pl.BlockSpec((rows, cb), lambda j: (0, j))
    wsp = pl.BlockSpec((SSM_CONV, cb), lambda j: (0, j))
    bsp = pl.BlockSpec((1, cb), lambda j: (0, j))
    return pl.pallas_call(
        body, name=name, out_shape=[S((rows, D_XBC), BF16), S((SSM_CONV, D_XBC), F32), S((1, D_XBC), F32)],
        grid=(D_XBC // cb,), in_specs=[pl.BlockSpec((rows, cb), lambda j: (0, off + j)), col, wsp, bsp],
        out_specs=[col, wsp, bsp], compiler_params=_cp(("parallel",)))(zx, dxbc, w, b)


def _ffn_act_fwd(u, w, b, name):
    rows = u.shape[0]
    cb = 256
    nb = D_FF // cb

    def body(g_ref, v_ref, wg_ref, wv_ref, bg_ref, bv_ref, o_ref):
        it = lax.broadcasted_iota(jnp.int32, (rows, 1), 0)
        g = _conv(g_ref[...], wg_ref, bg_ref, FFN_CONV, it)
        v = _conv(v_ref[...], wv_ref, bv_ref, FFN_CONV, it)
        o_ref[...] = (g * _sigmoid(g) * v).astype(o_ref.dtype)

    def sp(r, shift):
        return pl.BlockSpec((r, cb), lambda j: (0, shift + j))

    return pl.pallas_call(
        body, name=name, out_shape=S((rows, D_FF), BF16), grid=(nb,),
        in_specs=[sp(rows, 0), sp(rows, nb), sp(FFN_CONV, 0), sp(FFN_CONV, nb), sp(1, 0), sp(1, nb)],
        out_specs=sp(rows, 0), compiler_params=_cp(("parallel",)))(u, u, w, w, b, b)


def _ffn_act_bwd(u, dact, w, b, name):
    rows = u.shape[0]
    cb = 256
    nb = D_FF // cb

    def body(g_ref, v_ref, d_ref, wg_ref, wv_ref, bg_ref, bv_ref, dg_ref, dv_ref, dwg_ref, dwv_ref, dbg_ref, dbv_ref):
        it = lax.broadcasted_iota(jnp.int32, (rows, 1), 0)
        xg, xv = g_ref[...], v_ref[...]
        g = _conv(xg, wg_ref, bg_ref, FFN_CONV, it)
        v = _conv(xv, wv_ref, bv_ref, FFN_CONV, it)
        sg = _sigmoid(g)
        d = d_ref[...]
        dgate = d * v * sg * (1.0 + g * (1.0 - sg))
        dval = d * g * sg
        dg_ref[...] = _conv_bwd(xg, dgate, wg_ref, dwg_ref, dbg_ref, FFN_CONV, it).astype(dg_ref.dtype)
        dv_ref[...] = _conv_bwd(xv, dval, wv_ref, dwv_ref, dbv_ref, FFN_CONV, it).astype(dv_ref.dtype)

    def sp(r, shift):
        return pl.BlockSpec((r, cb), lambda j: (0, shift + j))

    return pl.pallas_call(
        body, name=name,
        out_shape=[S((rows, D_FF), BF16), S((rows, D_FF), BF16), S((FFN_CONV, D_FF), F32), S((FFN_CONV, D_FF), F32),
                   S((1, D_FF), F32), S((1, D_FF), F32)],
        grid=(nb,),
        in_specs=[sp(rows, 0), sp(rows, nb), sp(rows, 0), sp(FFN_CONV, 0), sp(FFN_CONV, nb), sp(1, 0), sp(1, nb)],
        out_specs=[sp(rows, 0), sp(rows, 0), sp(FFN_CONV, 0), sp(FFN_CONV, 0), sp(1, 0), sp(1, 0)],
        compiler_params=_cp(("parallel",)))(u, u, dact, w, w, b, b)


def _ssd_consts(dtp_ref, bias_ref, alog_ref, hg):
    lane = lax.broadcasted_iota(jnp.int32, (1, LANE), 1)
    pre = dtp_ref[...] + bias_ref[...]
    dt = _softplus(pre)
    a_row = jnp.where(lane < hg, -jnp.exp(alog_ref[...]), 0.0)
    ri = lax.broadcasted_iota(jnp.int32, (T, T), 0)
    ci = lax.broadcasted_iota(jnp.int32, (T, T), 1)
    cs = _dot_hi((ri >= ci).astype(F32), dt * a_row)
    return pre, dt, a_row, cs, ri, ci, lane


def _ssd_fwd(xbc, zx, bias, alog, dsk, name):
    rows = xbc.shape[0]
    nc = rows // T
    hg = SSM_HEADS // SSM_GROUPS
    gw = hg * HEAD_P
    xoff, boff, coff = 0, D_INNER // D_STATE, (D_INNER + D_BC) // D_STATE
    dtoff = (D_INNER + D_XBC) // LANE

    def body(x_ref, b_ref, c_ref, dtp_ref, bias_ref, alog_ref, dsk_ref, y_ref, hst_ref, hs):
        c = pl.program_id(1)

        @pl.when(c == 0)
        def _():
            hs[...] = jnp.zeros_like(hs)

        _, dt, _, cs, ri, ci, _ = _ssd_consts(dtp_ref, bias_ref, alog_ref, hg)
        cst = cs.T
        x = x_ref[...]
        bb, cbf = b_ref[...].astype(BF16), c_ref[...].astype(BF16)
        gmat = _dot(cbf, bb, _NT)
        causal = ri >= ci
        dskv = dsk_ref[...]
        hst_ref[0, 0] = hs[...]
        for k in range(hg):
            csk, csr = cs[:, k:k + 1], cst[k:k + 1, :]
            lm = jnp.exp(jnp.where(causal, csk - csr, NEG))
            xk = x[:, k * HEAD_P:(k + 1) * HEAD_P]
            xdt = xk * dt[:, k:k + 1]
            hk = hs[k * HEAD_P:(k + 1) * HEAD_P, :]
            yd = _dot((gmat * lm).astype(BF16), xdt.astype(BF16))
            yo = jnp.exp(csk) * _dot(cbf, hk.astype(BF16), _NT)
            y_ref[:, k * HEAD_P:(k + 1) * HEAD_P] = yd + yo + dskv[:, k:k + 1] * xk
            cl = cs[T - 1:T, k:k + 1]
            st = _dot((xdt * jnp.exp(cl - csk)).astype(BF16), bb, _TN)
            hs[k * HEAD_P:(k + 1) * HEAD_P, :] = jnp.exp(cl) * hk + st

    vec = pl.BlockSpec((1, LANE), lambda g, c: (0, g))
    return pl.pallas_call(
        body, name=name,
        out_shape=[S((rows, D_INNER), F32), S((nc, SSM_GROUPS, gw, D_STATE), F32)],
        grid=(SSM_GROUPS, nc),
        in_specs=[pl.BlockSpec((T, gw), lambda g, c: (c, xoff + g)),
                  pl.BlockSpec((T, D_STATE), lambda g, c: (c, boff + g)),
                  pl.BlockSpec((T, D_STATE), lambda g, c: (c, coff + g)),
                  pl.BlockSpec((T, LANE), lambda g, c: (c, dtoff + g)), vec, vec, vec],
        out_specs=[pl.BlockSpec((T, gw), lambda g, c: (c, g)),
                   pl.BlockSpec((1, 1, gw, D_STATE), lambda g, c: (c, g, 0, 0))],
        scratch_shapes=[pltpu.VMEM((gw, D_STATE), F32)],
        compiler_params=_cp(("parallel", "arbitrary")))(xbc, xbc, xbc, zx, bias, alog, dsk)


def _ssd_bwd(xbc, zx, bias, alog, dsk, dy, hst, name):
    rows = xbc.shape[0]
    nc = rows // T
    hg = SSM_HEADS // SSM_GROUPS
    gw = hg * HEAD_P
    boff, coff = D_INNER // D_STATE, (D_INNER + D_BC) // D_STATE
    dtoff = (D_INNER + D_XBC) // LANE

    def body(x_ref, b_ref, c_ref, dtp_ref, bias_ref, alog_ref, dsk_ref, dy_ref, hst_ref,
             dx_ref, db_ref, dc_ref, ddtp_ref, dalog_ref, ddsk_ref, dbias_ref, dhs):
        step = pl.program_id(1)

        @pl.when(step == 0)
        def _():
            dhs[...] = jnp.zeros_like(dhs)
            dalog_ref[...] = jnp.zeros_like(dalog_ref)
            ddsk_ref[...] = jnp.zeros_like(ddsk_ref)
            dbias_ref[...] = jnp.zeros_like(dbias_ref)

        pre, dt, a_row, cs, ri, ci, lane = _ssd_consts(dtp_ref, bias_ref, alog_ref, hg)
        cst = cs.T
        x, dyv = x_ref[...], dy_ref[...]
        bb, cbf = b_ref[...].astype(BF16), c_ref[...].astype(BF16)
        gt = _dot(bb, cbf, _NT)
        causal_t = ci >= ri
        dskv = dsk_ref[...]
        last = lax.broadcasted_iota(jnp.int32, (T, 1), 0) == T - 1
        dgt = jnp.zeros((T, T), F32)
        dc_acc = jnp.zeros((T, D_STATE), F32)
        db_acc = jnp.zeros((T, D_STATE), F32)
        ddt_acc = jnp.zeros((T, LANE), F32)
        dcs_acc = jnp.zeros((T, LANE), F32)
        dcs_row = jnp.zeros((T, LANE), F32)
        ddsk_acc = jnp.zeros((1, LANE), F32)
        head_row = lax.broadcasted_iota(jnp.int32, (T, 1), 0)
        for k in range(hg):
            sl = slice(k * HEAD_P, (k + 1) * HEAD_P)
            csk, csr = cs[:, k:k + 1], cst[k:k + 1, :]
            lt = jnp.exp(jnp.where(causal_t, csr - csk, NEG))
            xk, dyk = x[:, sl], dyv[:, sl]
            dtk, dk = dt[:, k:k + 1], dskv[:, k:k + 1]
            dyb = dyk.astype(BF16)
            mpt = gt * lt
            z = _dot(mpt.astype(BF16), dyb)
            dmt = _dot((xk * dtk).astype(BF16), dyb, _NT)
            dgt = dgt + dmt * lt
            q = dmt * mpt
            xz = jnp.sum(xk * z, axis=1, keepdims=True)
            q_rows = jnp.sum(q, axis=1, keepdims=True)
            dcs_row = dcs_row + jnp.where(head_row == k, jnp.sum(q, axis=0, keepdims=True), 0.0)
            hk = hst_ref[0, 0, sl, :]
            dhn = dhs[sl, :]
            e = jnp.exp(csk)
            cl = cs[T - 1:T, k:k + 1]
            wdec = jnp.exp(cl - csk)
            w = wdec * dtk
            r = _dot(bb, dhn.astype(BF16), _NT)
            dx_ref[:, sl] = dtk * z + dk * dyk + r * w
            dw = jnp.sum(r * xk, axis=1, keepdims=True)
            dcl = jnp.exp(cl) * jnp.sum(dhn * hk) + jnp.sum(dw * w)
            yo = e * _dot(cbf, hk.astype(BF16), _NT)
            dcs_k = jnp.sum(dyk * yo, axis=1, keepdims=True) - q_rows - dw * w + jnp.where(last, dcl, 0.0)
            dye = (dyk * e).astype(BF16)
            dc_acc = dc_acc + _dot(dye, hk.astype(BF16))
            db_acc = db_acc + _dot((xk * w).astype(BF16), dhn.astype(BF16))
            dhs[sl, :] = jnp.exp(cl) * dhn + _dot(dye, cbf, _TN)
            onehot = (lane == k).astype(F32)
            ddt_acc = ddt_acc + (xz + dw * wdec) * onehot
            dcs_acc = dcs_acc + dcs_k * onehot
            ddsk_acc = ddsk_acc + jnp.sum(dyk * xk) * onehot
        dc_ref[...] = _dot(dgt.T.astype(BF16), bb) + dc_acc
        db_ref[...] = _dot(dgt.astype(BF16), cbf) + db_acc
        da = _dot_hi((ci >= ri).astype(F32), dcs_acc + dcs_row.T)
        ddtp = (ddt_acc + da * a_row) * _sigmoid(pre)
        ddtp = jnp.where(lane < hg, ddtp, 0.0)
        ddtp_ref[...] = ddtp
        dbias_ref[...] += jnp.sum(ddtp, axis=0, keepdims=True)
        dalog_ref[...] += jnp.sum(da * dt, axis=0, keepdims=True) * a_row
        ddsk_ref[...] += ddsk_acc

    def rc(c):
        return nc - 1 - c

    vec = pl.BlockSpec((1, LANE), lambda g, c: (0, g))
    xsp = pl.BlockSpec((T, gw), lambda g, c: (rc(c), g))
    return pl.pallas_call(
        body, name=name,
        out_shape=[S((rows, D_INNER), F32), S((rows, D_BC), F32), S((rows, D_BC), F32),
                   S((rows, SSM_GROUPS * LANE), F32), S((1, SSM_GROUPS * LANE), F32),
                   S((1, SSM_GROUPS * LANE), F32), S((1, SSM_GROUPS * LANE), F32)],
        grid=(SSM_GROUPS, nc),
        in_specs=[xsp,
                  pl.BlockSpec((T, D_STATE), lambda g, c: (rc(c), boff + g)),
                  pl.BlockSpec((T, D_STATE), lambda g, c: (rc(c), coff + g)),
                  pl.BlockSpec((T, LANE), lambda g, c: (rc(c), dtoff + g)), vec, vec, vec,
                  xsp, pl.BlockSpec((1, 1, gw, D_STATE), lambda g, c: (rc(c), g, 0, 0))],
        out_specs=[xsp,
                   pl.BlockSpec((T, D_STATE), lambda g, c: (rc(c), g)),
                   pl.BlockSpec((T, D_STATE), lambda g, c: (rc(c), g)),
                   pl.BlockSpec((T, LANE), lambda g, c: (rc(c), g)), vec, vec, vec],
        scratch_shapes=[pltpu.VMEM((gw, D_STATE), F32)],
        compiler_params=_cp(("parallel", "arbitrary")))(xbc, xbc, xbc, zx, bias, alog, dsk, dy, hst)


def _attn_tiles(kv_ref, j):
    prev = jnp.maximum(j - 1, 0)
    meta = kv_ref[0:T, :]
    prv = kv_ref[pl.ds(pl.multiple_of(prev * T, T), T), :]
    cur = kv_ref[pl.ds(pl.multiple_of(j * T, T), T), :]
    return jnp.concatenate([meta, prv, cur], axis=0)


def _attn_mask(j):
    r = j * T + lax.broadcasted_iota(jnp.int32, (T, 3 * T), 0)
    col = lax.broadcasted_iota(jnp.int32, (T, 3 * T), 1)
    t0, t1 = col < T, col < 2 * T
    s = jnp.where(t0, col, (j - 2) * T + col)
    ok = (s <= r) & ((s < N_META) | (s > r - WINDOW))
    use = (t0 & (j >= 2) & (col < N_META)) | (jnp.logical_not(t0) & t1 & (j >= 1)) | jnp.logical_not(t1)
    return ok & use


def _attn_probs(qh, k3, mask, sink):
    sc = jnp.where(mask, _dot(qh, k3, _NT), NEG)
    m = jnp.maximum(jnp.max(sc, axis=1, keepdims=True), sink)
    p = jnp.exp(sc - m)
    es = jnp.exp(sink - m)
    inv = 1.0 / (jnp.sum(p, axis=1, keepdims=True) + es)
    return p * inv, es * inv


def _attn_fwd(q, kv, sinks, name):
    rows = q.shape[0]
    scale = 1.0 / math.sqrt(ATTN_DH)
    qpk = N_Q_HEADS // N_KV_HEADS

    def body(q_ref, kv_ref, s_ref, o_ref):
        j = pl.program_id(0)
        kv3 = _attn_tiles(kv_ref, j).astype(BF16)
        mask = _attn_mask(j)
        qv = (q_ref[...] * scale).astype(BF16)
        sk = s_ref[...]
        for kh in range(N_KV_HEADS):
            k3 = kv3[:, kh * ATTN_DH:(kh + 1) * ATTN_DH]
            v3 = kv3[:, D_KV + kh * ATTN_DH:D_KV + (kh + 1) * ATTN_DH]
            for g in range(qpk):
                h = kh * qpk + g
                p, _ = _attn_probs(qv[:, h * ATTN_DH:(h + 1) * ATTN_DH], k3, mask, sk[:, h:h + 1])
                o_ref[:, h * ATTN_DH:(h + 1) * ATTN_DH] = _dot(p.astype(BF16), v3).astype(o_ref.dtype)

    return pl.pallas_call(
        body, name=name, out_shape=S((rows, D_MODEL), BF16), grid=(rows // T,),
        in_specs=[pl.BlockSpec((T, D_MODEL), lambda j: (j, 0)), pl.BlockSpec((rows, 2 * D_KV), lambda j: (0, 0)),
                  pl.BlockSpec((1, N_Q_HEADS), lambda j: (0, 0))],
        out_specs=pl.BlockSpec((T, D_MODEL), lambda j: (j, 0)), compiler_params=_cp(("parallel",)))(q, kv, sinks)


def _attn_bwd(q, kv, sinks, do, name):
    rows = q.shape[0]
    scale = 1.0 / math.sqrt(ATTN_DH)
    qpk = N_Q_HEADS // N_KV_HEADS

    def body(q_ref, kv_ref, s_ref, do_ref, dq_ref, dkv_ref, ds_ref):
        j = pl.program_id(0)

        @pl.when(j == 0)
        def _():
            dkv_ref[...] = jnp.zeros_like(dkv_ref)
            ds_ref[...] = jnp.zeros_like(ds_ref)

        kv3 = _attn_tiles(kv_ref, j).astype(BF16)
        mask = _attn_mask(j)
        qv = (q_ref[...] * scale).astype(BF16)
        dov = do_ref[...].astype(BF16)
        sk = s_ref[...]
        lane = lax.broadcasted_iota(jnp.int32, (1, LANE), 1)
        ds_acc = jnp.zeros((1, LANE), F32)
        prev = jnp.maximum(j - 1, 0)
        for kh in range(N_KV_HEADS):
            ksl = slice(kh * ATTN_DH, (kh + 1) * ATTN_DH)
            vsl = slice(D_KV + kh * ATTN_DH, D_KV + (kh + 1) * ATTN_DH)
            k3, v3 = kv3[:, ksl], kv3[:, vsl]
            dk3 = jnp.zeros((3 * T, ATTN_DH), F32)
            dv3 = jnp.zeros((3 * T, ATTN_DH), F32)
            for g in range(qpk):
                h = kh * qpk + g
                hs = slice(h * ATTN_DH, (h + 1) * ATTN_DH)
                qh, doh = qv[:, hs], dov[:, hs]
                p, ps = _attn_probs(qh, k3, mask, sk[:, h:h + 1])
                dp = _dot(doh, v3, _NT)
                delta = jnp.sum(p * dp, axis=1, keepdims=True)
                dsc = (p * (dp - delta)).astype(BF16)
                dq_ref[:, hs] = (_dot(dsc, k3) * scale).astype(dq_ref.dtype)
                dk3 = dk3 + _dot(dsc, qh, _TN)
                dv3 = dv3 + _dot(p.astype(BF16), doh, _TN)
                ds_acc = ds_acc - jnp.sum(ps * delta) * (lane == h).astype(F32)
            for t, start in enumerate((0, pl.multiple_of(prev * T, T), pl.multiple_of(j * T, T))):
                rsl = pl.ds(start, T)
                dkv_ref[rsl, ksl] += dk3[t * T:(t + 1) * T, :]
                dkv_ref[rsl, vsl] += dv3[t * T:(t + 1) * T, :]
        ds_ref[...] += ds_acc

    blk = pl.BlockSpec((T, D_MODEL), lambda j: (j, 0))
    full = pl.BlockSpec((rows, 2 * D_KV), lambda j: (0, 0))
    return pl.pallas_call(
        body, name=name, out_shape=[S((rows, D_MODEL), BF16), S((rows, 2 * D_KV), F32), S((1, LANE), F32)],
        grid=(rows // T,), in_specs=[blk, full, pl.BlockSpec((1, N_Q_HEADS), lambda j: (0, 0)), blk],
        out_specs=[blk, full, pl.BlockSpec((1, LANE), lambda j: (0, 0))],
        compiler_params=_cp(("arbitrary",)))(q, kv, sinks, do)


BLOCK_BYTES = 1 << 20


def _div_tile(rows, cols):
    cap = max(16, BLOCK_BYTES // (4 * cols))
    best = None
    for t in range(16, min(rows, cap) + 1, 16):
        if rows % t == 0:
            best = t
    return best if best is not None else rows


def _adamw(parts, w, m, v, name):
    n, rows, cols = parts.shape
    tr = _div_tile(rows, cols)
    c1 = 1.0 / (1.0 - B1 ** STEP)
    c2 = 1.0 / (1.0 - B2 ** STEP)

    def body(p_ref, w_ref, m_ref, v_ref, g_ref, d_ref, nm_ref, nv_ref):
        g = p_ref[0].astype(F32)
        for i in range(1, n):
            g = g + p_ref[i].astype(F32)
        nm = B1 * m_ref[...] + (1.0 - B1) * g
        nv = B2 * v_ref[...] + (1.0 - B2) * (g * g)
        g_ref[...] = g
        nm_ref[...] = nm
        nv_ref[...] = nv
        d_ref[...] = -LR * ((nm * c1) / (jnp.sqrt(nv * c2) + EPS) + WD * w_ref[...])

    row = pl.BlockSpec((tr, cols), lambda i: (i, 0))
    return pl.pallas_call(
        body, name=name, out_shape=[S((rows, cols), F32)] * 4, grid=(rows // tr,),
        in_specs=[pl.BlockSpec((n, tr, cols), lambda i: (0, i, 0)), row, row, row], out_specs=[row] * 4,
        compiler_params=_cp(("parallel",)))(parts, w, m, v)


def _add_pairs(mine, theirs, core, name):
    _, rows, cols = mine.shape
    tr = _div_tile(rows, cols)

    def body(core_ref, a_ref, b_ref, o_ref):
        o_ref[...] = (a_ref[...].astype(F32) + b_ref[...].astype(F32)).astype(o_ref.dtype)

    return pl.pallas_call(
        body, name=name, out_shape=S((4, rows, cols), BF16),
        grid_spec=pltpu.PrefetchScalarGridSpec(
            num_scalar_prefetch=1, grid=(4, rows // tr),
            in_specs=[pl.BlockSpec((None, tr, cols), lambda k, i, c: (2 * k + c[0], i, 0)),
                      pl.BlockSpec((None, tr, cols), lambda k, i, c: (k, i, 0))],
            out_specs=pl.BlockSpec((None, tr, cols), lambda k, i, c: (k, i, 0))),
        compiler_params=_cp(("parallel", "parallel")))(core, mine, theirs)


def _col_segments(ws, runs):
    segs = []
    for glo, mlo, n in runs:
        while n > 0:
            d, off = divmod(glo, ws)
            take = min(n, ws - off)
            segs.append((d, off, mlo, take))
            glo, mlo, n = glo + take, mlo + take, n - take
    return segs


def _assemble_cols(g, width, segs, name):
    _, rows, ws = g.shape
    rb = _div_tile(rows, width // 2)

    def body(g_ref, o_ref):
        o_ref[...] = jnp.zeros_like(o_ref)
        for d, off, mlo, n in segs:
            o_ref[:, mlo:mlo + n] = g_ref[d, :, off:off + n]

    return pl.pallas_call(
        body, name=name, out_shape=S((rows, width), g.dtype), grid=(rows // rb,),
        in_specs=[pl.BlockSpec((N_DEV, rb, ws), lambda i: (0, i, 0))],
        out_specs=pl.BlockSpec((rb, width), lambda i: (i, 0)), compiler_params=_cp(("parallel",)))(g)


def _scatter_cols(dw, ws, segs, name):
    rows, width = dw.shape
    rb = _div_tile(rows, width)

    def body(w_ref, o_ref):
        for d, off, mlo, n in segs:
            o_ref[d, :, off:off + n] = w_ref[:, mlo:mlo + n].astype(o_ref.dtype)

    return pl.pallas_call(
        body, name=name, out_shape=S((N_DEV, rows, ws), BF16), grid=(rows // rb,),
        in_specs=[pl.BlockSpec((rb, width), lambda i: (i, 0))],
        out_specs=pl.BlockSpec((N_DEV, rb, ws), lambda i: (0, i, 0)), compiler_params=_cp(("parallel",)))(dw)


_HBM = pl.BlockSpec(memory_space=pltpu.HBM)
_MESH = pl.DeviceIdType.MESH


def _all_gather(xs, name):
    n = len(xs)

    def body(*refs):
        x_refs, out_refs = refs[:n], refs[n:2 * n]
        send_sems, recv_sems, local_sems = refs[2 * n:]
        mx, my, mc = lax.axis_index("x"), lax.axis_index("y"), lax.axis_index("c")
        me, sibling = (mx, my, mc), (mx, my, 1 - mc)
        chips = [(1 - mx, my), (mx, 1 - my), (1 - mx, 1 - my)]

        def blk(a, px, py, pc):
            return out_refs[a].at[4 * px + 2 * py + pc]

        def copy(a, k, block, to, src=None):
            return pltpu.make_async_remote_copy(
                src_ref=blk(a, *block) if src is None else src, dst_ref=blk(a, *block),
                send_sem=send_sems.at[a, k], recv_sem=recv_sems.at[a, k], device_id=to, device_id_type=_MESH)

        mine = [pltpu.make_async_copy(x_refs[a], blk(a, *me), local_sems.at[a]) for a in range(n)]
        first = []
        for a in range(n):
            mine[a].start()
            first.append(copy(a, 0, me, sibling, src=x_refs[a]))
            first += [copy(a, 1 + i, me, (*chip, mc), src=x_refs[a]) for i, chip in enumerate(chips)]
        for cp in first:
            cp.start()
        passed = []
        for a in range(n):
            for i, chip in enumerate(chips):
                copy(a, 1 + i, (*chip, mc), me).wait_recv()
                passed.append(copy(a, 4 + i, (*chip, mc), sibling))
                passed[-1].start()
        for a in range(n):
            copy(a, 0, sibling, me).wait_recv()
            for i, chip in enumerate(chips):
                copy(a, 4 + i, (*chip, 1 - mc), me).wait_recv()
        for cp in first + passed:
            cp.wait_send()
        for cp in mine:
            cp.wait()

    return pl.pallas_call(
        body, name=name, out_shape=[S((N_DEV,) + x.shape, x.dtype) for x in xs], in_specs=[_HBM] * n,
        out_specs=[_HBM] * n,
        scratch_shapes=[pltpu.SemaphoreType.DMA((n, 7)), pltpu.SemaphoreType.DMA((n, 7)),
                        pltpu.SemaphoreType.DMA((n,))])(*xs)


def _swap_sibling(gs, name):
    n = len(gs)

    def body(*refs):
        g_refs, out_refs = refs[:n], refs[n:2 * n]
        send_sems, recv_sems = refs[2 * n:]
        mx, my, mc = lax.axis_index("x"), lax.axis_index("y"), lax.axis_index("c")
        cps = [pltpu.make_async_remote_copy(
            src_ref=g_refs[a].at[2 * k + 1 - mc], dst_ref=out_refs[a].at[k], send_sem=send_sems.at[a, k],
            recv_sem=recv_sems.at[a, k], device_id=(mx, my, 1 - mc), device_id_type=_MESH)
            for a in range(n) for k in range(4)]
        for cp in cps:
            cp.start()
        for cp in cps:
            cp.wait()

    return pl.pallas_call(
        body, name=name, out_shape=[S((4,) + g.shape[1:], g.dtype) for g in gs], in_specs=[_HBM] * n,
        out_specs=[_HBM] * n,
        scratch_shapes=[pltpu.SemaphoreType.DMA((n, 4)), pltpu.SemaphoreType.DMA((n, 4))])(*gs)


def _exchange_chips(parts, name):
    n = len(parts)

    def body(*refs):
        p_refs, out_refs = refs[:n], refs[n:2 * n]
        send_sems, recv_sems, local_sems = refs[2 * n:]
        mx, my, mc = lax.axis_index("x"), lax.axis_index("y"), lax.axis_index("c")
        mychip = 2 * mx + my
        chips = [(1 - mx, my), (mx, 1 - my), (1 - mx, 1 - my)]
        mine = [pltpu.make_async_copy(p_refs[a].at[mychip], out_refs[a].at[mychip], local_sems.at[a])
                for a in range(n)]
        for cp in mine:
            cp.start()
        cps = [pltpu.make_async_remote_copy(
            src_ref=p_refs[a].at[2 * cx + cy], dst_ref=out_refs[a].at[mychip], send_sem=send_sems.at[a, i],
            recv_sem=recv_sems.at[a, i], device_id=(cx, cy, mc), device_id_type=_MESH)
            for a in range(n) for i, (cx, cy) in enumerate(chips)]
        for cp in cps:
            cp.start()
        for cp in cps:
            cp.wait()
        for cp in mine:
            cp.wait()

    return pl.pallas_call(
        body, name=name, out_shape=[S(p.shape, p.dtype) for p in parts], in_specs=[_HBM] * n, out_specs=[_HBM] * n,
        scratch_shapes=[pltpu.SemaphoreType.DMA((n, 3)), pltpu.SemaphoreType.DMA((n, 3)),
                        pltpu.SemaphoreType.DMA((n,))])(*parts)


def _flat_rows(n_elems, mult):
    rows = -(-n_elems // LANE)
    return -(-rows // mult) * mult


def _pack(arrs, lead, mult, dtype):
    lead_shape = arrs[0].shape[:lead]
    flat = jnp.concatenate([a.astype(dtype).reshape(lead_shape + (-1,)) for a in arrs], axis=-1)
    n = flat.shape[-1]
    rows = _flat_rows(n, mult)
    flat = jnp.pad(flat, [(0, 0)] * lead + [(0, rows * LANE - n)])
    return flat.reshape(lead_shape + (rows, LANE))


def _unpack(flat, lead, shapes):
    lead_shape = flat.shape[:lead]
    flat = flat.reshape(lead_shape + (-1,))
    out, off = [], 0
    for shp in shapes:
        n = math.prod(shp)
        out.append(flat[..., off:off + n].reshape(lead_shape + tuple(shp)))
        off += n
    return out


def _split8(full, ax, n):
    shp = full.shape
    return jnp.moveaxis(full.reshape(shp[:ax] + (N_DEV, n) + shp[ax + 1:]), ax, 0)


def _join8(g, ax):
    shp = g.shape[1:]
    return jnp.moveaxis(g, 0, ax).reshape(shp[:ax] + (N_DEV * shp[ax],) + shp[ax + 1:])


def _group_lanes(v, hg):
    v = v.reshape(SSM_GROUPS, hg)
    return jnp.pad(v, ((0, 0), (0, LANE - hg))).reshape(1, SSM_GROUPS * LANE)


def _ungroup_lanes(v, hg):
    return v.reshape(SSM_GROUPS, LANE)[:, :hg].reshape(1, SSM_GROUPS * hg)


def kernel(x, meta_tokens, a_norm_pre, a_w_in, a_conv_w, a_conv_b, a_dt_bias, a_a_log, a_d_skip, a_gate_norm, a_w_out, a_norm_post, kv_norm, w_kv, b_norm_pre, b_w_q, b_sinks, b_w_o, b_norm_post, f_norm_pre, f_w_up, f_conv_w, f_conv_b, f_w_down, f_norm_post, loss_target, m_meta_tokens, m_a_norm_pre, m_a_w_in, m_a_conv_w, m_a_conv_b, m_a_dt_bias, m_a_a_log, m_a_d_skip, m_a_gate_norm, m_a_w_out, m_a_norm_post, m_kv_norm, m_w_kv, m_b_norm_pre, m_b_w_q, m_b_sinks, m_b_w_o, m_b_norm_post, m_f_norm_pre, m_f_w_up, m_f_conv_w, m_f_conv_b, m_f_w_down, m_f_norm_post, v_meta_tokens, v_a_norm_pre, v_a_w_in, v_a_conv_w, v_a_conv_b, v_a_dt_bias, v_a_a_log, v_a_d_skip, v_a_gate_norm, v_a_w_out, v_a_norm_post, v_kv_norm, v_w_kv, v_b_norm_pre, v_b_w_q, v_b_sinks, v_b_w_o, v_b_norm_post, v_f_norm_pre, v_f_w_up, v_f_conv_w, v_f_conv_b, v_f_w_down, v_f_norm_post):
    args = locals()
    wts = {n: args[n] for n in WEIGHTS}
    mom = {n: args["m_" + n] for n in WEIGHTS}
    var = {n: args["v_" + n] for n in WEIGHTS}
    mx, my, mc = lax.axis_index("x"), lax.axis_index("y"), lax.axis_index("c")
    me = 4 * mx + 2 * my + mc
    rows = _seq_rows()
    hg = SSM_HEADS // SSM_GROUPS
    d = D_MODEL

    n_main = D_INNER + D_XBC
    ws_in, ws_up = a_w_in.shape[2], f_w_up.shape[2]
    segs_in = _col_segments(ws_in, [(0, 0, n_main)] + [(n_main + hg * g, n_main + LANE * g, hg)
                                                      for g in range(SSM_GROUPS)])
    segs_up = _col_segments(ws_up, [(0, 0, 2 * D_FF)])
    g_in, g_out, g_kv, g_q, g_o, g_up0, g_up1, g_dn0, g_dn1, small_full = _all_gather(
        [a_w_in[0].astype(BF16), a_w_out[0].astype(BF16), w_kv.astype(BF16), b_w_q[0].astype(BF16),
         b_w_o[0].astype(BF16), f_w_up[0].astype(BF16), f_w_up[1].astype(BF16), f_w_down[0].astype(BF16),
         f_w_down[1].astype(BF16), _pack([wts[n] for n in SMALL], 0, 8, F32)], "gather_weights")
    full = {}
    for n, g in zip(SMALL, _unpack(small_full, 1, [wts[n].shape for n in SMALL])):
        full[n] = _join8(g, SHARD_AXIS[n])
    w_in_all = _assemble_cols(g_in, n_main + SSM_GROUPS * LANE, segs_in, "asm_w_in")
    w_up = [_assemble_cols(g, 2 * D_FF, segs_up, f"asm_w_up{i}") for i, g in enumerate((g_up0, g_up1))]
    w_out, w_kvf = g_out.reshape(D_INNER, d), g_kv.reshape(d, 2 * D_KV)
    w_q, w_o = g_q.reshape(d, d), g_o.reshape(d, d)
    w_down = [g_dn0.reshape(D_FF, d), g_dn1.reshape(D_FF, d)]
    bias_g = _group_lanes(wts["a_dt_bias"], hg)
    alog_g = _group_lanes(wts["a_a_log"], hg)
    dsk_g = _group_lanes(wts["a_d_skip"], hg)
    a_conv_w, a_conv_b = full["a_conv_w"][0], full["a_conv_b"]
    f_cw, f_cb = full["f_conv_w"], wts["f_conv_b"]
    fpre, fpost = wts["f_norm_pre"], wts["f_norm_post"]

    pad_rows = rows - N_META - SEQ
    h0 = jnp.concatenate([full["meta_tokens"], x[0], jnp.zeros((pad_rows, d), F32)], axis=0)
    tgt = jnp.pad(loss_target[0], ((N_META, pad_rows), (0, 0)))

    _, (hn0,) = _resid_norm(h0, None, None, [full["a_norm_pre"]], "norm_a_pre")
    zx = _mm(hn0, w_in_all, "nn", F32, "mm_in")
    xbc = _conv_silu_fwd(zx, a_conv_w, a_conv_b, "conv_a")
    y_ssd, hst = _ssd_fwd(xbc, zx, bias_g, alog_g, dsk_g, "ssd_fwd")
    yn = _gatenorm_fwd(y_ssd, zx, full["a_gate_norm"], "gatenorm")
    mix_a = _mm(yn, w_out, "nn", F32, "mm_out")
    h1, (fn0,) = _resid_norm(h0, mix_a, full["a_norm_post"], [fpre[0:1]], "resid_a")

    def ffn_fwd(fn, i):
        u = _mm(fn, w_up[i], "nn", F32, f"mm_up{i}")
        act = _ffn_act_fwd(u, f_cw[i], f_cb[i:i + 1], f"ffn_act{i}")
        return u, act, _mm(act, w_down[i], "nn", F32, f"mm_down{i}")

    u0, act0, ffn0 = ffn_fwd(fn0, 0)
    h2, (kvn, bn) = _resid_norm(h1, ffn0, fpost[0:1], [wts["kv_norm"].reshape(1, d), wts["b_norm_pre"]], "resid_f0")
    kv = _mm(kvn, w_kvf, "nn", F32, "mm_kv")
    q = _mm(bn, w_q, "nn", F32, "mm_q")
    o = _attn_fwd(q, kv, wts["b_sinks"], "attn_fwd")
    mix_b = _mm(o, w_o, "nn", F32, "mm_o")
    h3, (fn1,) = _resid_norm(h2, mix_b, wts["b_norm_post"], [fpre[1:2]], "resid_b")
    u1, act1, ffn1 = ffn_fwd(fn1, 1)
    dh4, loss_row = _final_loss(h3, ffn1, fpost[1:2], tgt, "loss")
    loss = lax.psum(loss_row[0, 0], ("x", "y", "c"))

    grads = {}

    def ffn_bwd(dh_out, h_in, fn, u, act, ffn, i):
        dffn, dw_post = _norm_bwd(ffn, fpost[i:i + 1], dh_out, None, BF16, f"nb_fpost{i}")
        dact = _mm(dffn, w_down[i], "nt", F32, f"mm_dact{i}")
        dw_down = _mm(act, dffn, "tn", BF16, f"mm_dwdown{i}")
        dg, dv, dwg, dwv, dbg, dbv = _ffn_act_bwd(u, dact, f_cw[i], f_cb[i:i + 1], f"ffn_act_bwd{i}")
        du = jnp.concatenate([dg, dv], axis=1)
        dfn = _mm(du, w_up[i], "nt", F32, f"mm_dfn{i}")
        dw_up = _mm(fn, du, "tn", BF16, f"mm_dwup{i}")
        dh_in, dw_pre = _norm_bwd(h_in, fpre[i:i + 1], dfn, dh_out, F32, f"nb_fpre{i}")
        return dh_in, dict(post=dw_post, down=dw_down, cw=jnp.concatenate([dwg, dwv], axis=1),
                           cb=jnp.concatenate([dbg, dbv], axis=1), up=dw_up, pre=dw_pre)

    dh3, gf1 = ffn_bwd(dh4, h3, fn1, u1, act1, ffn1, 1)
    dmix_b, grads["b_norm_post"] = _norm_bwd(mix_b, wts["b_norm_post"], dh3, None, BF16, "nb_bpost")
    do = _mm(dmix_b, w_o, "nt", F32, "mm_do")
    dw_o = _mm(o, dmix_b, "tn", BF16, "mm_dwo")
    dq, dkv, dsinks = _attn_bwd(q, kv, wts["b_sinks"], do, "attn_bwd")
    grads["b_sinks"] = dsinks[:, :N_Q_HEADS]
    dbn = _mm(dq, w_q, "nt", F32, "mm_dbn")
    dw_q = _mm(bn, dq, "tn", BF16, "mm_dwq")
    dkv16 = dkv.astype(BF16)
    dkvn = _mm(dkv16, w_kvf, "nt", F32, "mm_dkvn")
    dw_kv = _mm(kvn, dkv16, "tn", BF16, "mm_dwkv")
    dh2, grads["b_norm_pre"] = _norm_bwd(h2, wts["b_norm_pre"], dbn, dh3, F32, "nb_bpre")
    dh2, dw_kvn = _norm_bwd(h2, wts["kv_norm"].reshape(1, d), dkvn, dh2, F32, "nb_kv")
    grads["kv_norm"] = dw_kvn.reshape(d)
    dh1, gf0 = ffn_bwd(dh2, h1, fn0, u0, act0, ffn0, 0)
    grads["f_norm_post"] = jnp.concatenate([gf0["post"], gf1["post"]], axis=0)
    grads["f_norm_pre"] = jnp.concatenate([gf0["pre"], gf1["pre"]], axis=0)
    grads["f_conv_w"] = jnp.stack([gf0["cw"], gf1["cw"]])
    grads["f_conv_b"] = jnp.concatenate([gf0["cb"], gf1["cb"]], axis=0)

    dmix_a, grads["a_norm_post"] = _norm_bwd(mix_a, full["a_norm_post"], dh1, None, BF16, "nb_apost")
    dyn = _mm(dmix_a, w_out, "nt", F32, "mm_dyn")
    dw_out = _mm(yn, dmix_a, "tn", BF16, "mm_dwout")
    dy_ssd, dz, grads["a_gate_norm"] = _gatenorm_bwd(y_ssd, zx, full["a_gate_norm"], dyn, "gatenorm_bwd")
    dxs, dbm, dcm, ddtp, dalog, ddsk, dbias = _ssd_bwd(xbc, zx, bias_g, alog_g, dsk_g, dy_ssd, hst, "ssd_bwd")
    dxbc = jnp.concatenate([dxs, dbm, dcm], axis=1)
    grads["a_a_log"] = _ungroup_lanes(dalog, hg)
    grads["a_d_skip"] = _ungroup_lanes(ddsk, hg)
    grads["a_dt_bias"] = _ungroup_lanes(dbias, hg)
    dpre, dcw, dcb = _conv_silu_bwd(zx, dxbc, a_conv_w, a_conv_b, "conv_a_bwd")
    grads["a_conv_w"], grads["a_conv_b"] = dcw[None], dcb
    dzx = jnp.concatenate([dz, dpre, ddtp.astype(BF16)], axis=1)
    dhn0 = _mm(dzx, w_in_all, "nt", F32, "mm_dhn0")
    dw_in_all = _mm(hn0, dzx, "tn", BF16, "mm_dwin")
    dh0, grads["a_norm_pre"] = _norm_bwd(h0, full["a_norm_pre"], dhn0, dh1, F32, "nb_apre")
    grad_x = dh0[N_META:N_META + SEQ][None]
    grads["meta_tokens"] = dh0[:N_META]

    by_dest = [_scatter_cols(dw_in_all, ws_in, segs_in, "scat_w_in"),
               dw_out.reshape(N_DEV, -1, d), dw_kv.reshape(N_DEV, -1, 2 * D_KV), dw_q.reshape(N_DEV, -1, d),
               dw_o.reshape(N_DEV, -1, d),
               _scatter_cols(gf0["up"], ws_up, segs_up, "scat_w_up0"), _scatter_cols(gf1["up"], ws_up, segs_up, "scat_w_up1"),
               gf0["down"].reshape(N_DEV, -1, d), gf1["down"].reshape(N_DEV, -1, d)]
    core = mc.astype(jnp.int32).reshape(1)
    from_sib = _swap_sibling(by_dest, "rs_sibling")
    chip_sums = [_add_pairs(g, s, core, f"rs_add{i}") for i, (g, s) in enumerate(zip(by_dest, from_sib))]
    p_in, p_out, p_kv, p_q, p_o, p_up0, p_up1, p_dn0, p_dn1 = _exchange_chips(chip_sums, "rs_chips")
    parts_big = dict(a_w_in=p_in, a_w_out=p_out, w_kv=p_kv, b_w_q=p_q, b_w_o=p_o,
                     f_w_up=jnp.concatenate([p_up0, p_up1], axis=1), f_w_down=jnp.concatenate([p_dn0, p_dn1], axis=1))

    small_local = _pack([_split8(grads[n], SHARD_AXIS[n], wts[n].shape[SHARD_AXIS[n]]) for n in SMALL], 1, 8, F32)
    repl_local = _pack([grads[n] for n in REPL], 0, 8, F32)
    n_sr = small_local.shape[1]
    small_all, = _all_gather([jnp.concatenate([small_local.reshape(N_DEV * n_sr, LANE), repl_local], axis=0)],
                             "gather_small_grads")
    mine_small = lax.dynamic_slice_in_dim(small_all, me * n_sr, n_sr, axis=1)
    parts_small = jnp.concatenate([mine_small, small_all[:, N_DEV * n_sr:]], axis=1)

    def flat_f32(dct, names, mult):
        return _pack([dct[n] for n in names], 0, mult, F32)

    big_out = {}
    for n in BIG:
        cols = wts[n].shape[-1]
        res = _adamw(parts_big[n], *[dct[n].reshape(-1, cols) for dct in (wts, mom, var)], f"adamw_{n}")
        big_out[n] = [r.reshape(wts[n].shape) for r in res]
    sm_in = [jnp.concatenate([flat_f32(dct, SMALL, 8), flat_f32(dct, REPL, 8)], axis=0) for dct in (wts, mom, var)]
    small_out = _adamw(parts_small, *sm_in, "adamw_small")

    outs = []
    for kind in range(4):
        res = {n: big_out[n][kind] for n in BIG}
        for n, a in zip(SMALL, _unpack(small_out[kind][:n_sr], 0, [wts[n].shape for n in SMALL])):
            res[n] = a
        for n, a in zip(REPL, _unpack(small_out[kind][n_sr:], 0, [wts[n].shape for n in REPL])):
            res[n] = a
        outs.append(res)
    return (loss, grad_x, *[outs[0][n] for n in WEIGHTS], *[outs[1][n] for n in WEIGHTS],
            *[outs[2][n] for n in WEIGHTS], *[outs[3][n] for n in WEIGHTS])
```

```python
import functools
import math

import jax
import jax.numpy as jnp
from jax import lax
from jax.experimental import pallas as pl
from jax.experimental.pallas import tpu as pltpu

F32, BF16 = jnp.float32, jnp.bfloat16
S = jax.ShapeDtypeStruct

D_MODEL = 1024
SEQ = 2048
N_META = 16
D_INNER = 2048
HEAD_P = 64
SSM_HEADS = D_INNER // HEAD_P
SSM_GROUPS = 4
D_STATE = 128
SSM_CONV = 4
D_BC = SSM_GROUPS * D_STATE
D_XBC = D_INNER + 2 * D_BC
ATTN_DH = 64
N_Q_HEADS = D_MODEL // ATTN_DH
N_KV_HEADS = 4
D_KV = N_KV_HEADS * ATTN_DH
WINDOW = 128
D_FF = 2816
FFN_CONV = 3
RMS_EPS = 1e-6
NEG = -1e30
LR, B1, B2, EPS, WD, STEP = 0.001, 0.9, 0.999, 1e-08, 0.01, 10

N_DEV = 8
T = 128
LANE = 128
VMEM_LIMIT = 48 * 1024 * 1024

BIG = ("a_w_in", "a_w_out", "w_kv", "b_w_q", "b_w_o", "f_w_up", "f_w_down")
SMALL = ("meta_tokens", "a_norm_pre", "a_conv_w", "a_conv_b", "a_gate_norm", "a_norm_post", "f_conv_w")
REPL = ("a_dt_bias", "a_a_log", "a_d_skip", "kv_norm", "b_norm_pre", "b_sinks", "b_norm_post",
        "f_norm_pre", "f_conv_b", "f_norm_post")
SHARD_AXIS = dict(a_w_in=2, a_w_out=1, w_kv=0, b_w_q=1, b_w_o=1, f_w_up=2, f_w_down=1, meta_tokens=1,
                  a_norm_pre=1, a_conv_w=2, a_conv_b=1, a_gate_norm=1, a_norm_post=1, f_conv_w=2)
WEIGHTS = ("meta_tokens", "a_norm_pre", "a_w_in", "a_conv_w", "a_conv_b", "a_dt_bias", "a_a_log", "a_d_skip",
           "a_gate_norm", "a_w_out", "a_norm_post", "kv_norm", "w_kv", "b_norm_pre", "b_w_q", "b_sinks", "b_w_o",
           "b_norm_post", "f_norm_pre", "f_w_up", "f_conv_w", "f_conv_b", "f_w_down", "f_norm_post")


def _seq_rows():
    return -(-(N_META + SEQ) // T) * T


def _cp(sem=None):
    return pltpu.CompilerParams(dimension_semantics=sem, vmem_limit_bytes=VMEM_LIMIT)


def _pick(n, target):
    t = min(n, target)
    t -= t % LANE
    while n % t:
        t -= LANE
    return t


def _sigmoid(x):
    return 1.0 / (1.0 + jnp.exp(-x))


def _softplus(x):
    return jnp.maximum(x, 0.0) + jnp.log(1.0 + jnp.exp(-jnp.abs(x)))


_NN = (((1,), (0,)), ((), ()))
_NT = (((1,), (1,)), ((), ()))
_TN = (((0,), (0,)), ((), ()))


def _dot(a, b, dims=_NN):
    return lax.dot_general(a, b, dims, preferred_element_type=F32)


def _dot_hi(a, b):
    return lax.dot_general(a, b, _NN, precision=lax.Precision.HIGHEST, preferred_element_type=F32)


_HBM = pl.BlockSpec(memory_space=pltpu.HBM)
_MESH = pl.DeviceIdType.MESH


class _Comm:
    def __init__(self, ins, out_shapes, scratch, first, last):
        self.ins, self.out_shapes, self.scratch, self.first, self.last = ins, out_shapes, scratch, first, last


def _call(body, name, out_shape, grid, in_specs, out_specs, sem, args, scratch=(), comm=None):
    if comm is None:
        return pl.pallas_call(body, name=name, out_shape=out_shape, grid=grid, in_specs=in_specs, out_specs=out_specs,
                              scratch_shapes=list(scratch), compiler_params=_cp(sem))(*args)
    single = not isinstance(out_shape, (list, tuple))
    outs = [out_shape] if single else list(out_shape)
    ospecs = [out_specs] if single else list(out_specs)
    n_in, n_out, n_scr, ci, co = len(in_specs), len(outs), len(scratch), len(comm.ins), len(comm.out_shapes)

    def carrier(*refs):
        p = 0
        parts = []
        for cnt in (n_in, ci, n_out, co, n_scr, len(comm.scratch)):
            parts.append(refs[p:p + cnt])
            p += cnt
        ins, cins, outs_r, couts, scr, cscr = parts
        ids = [pl.program_id(i) for i in range(len(grid))]
        first, last = ids[0] == 0, ids[0] == grid[0] - 1
        for i in range(1, len(grid)):
            first, last = first & (ids[i] == 0), last & (ids[i] == grid[i] - 1)

        @pl.when(first)
        def _():
            comm.first(cins, couts, cscr)

        body(*ins, *outs_r, *scr)

        @pl.when(last)
        def _():
            comm.last(cins, couts, cscr)

    res = pl.pallas_call(
        carrier, name=name, out_shape=outs + list(comm.out_shapes), grid=grid,
        in_specs=list(in_specs) + [_HBM] * ci, out_specs=ospecs + [_HBM] * co,
        scratch_shapes=list(scratch) + list(comm.scratch),
        compiler_params=_cp(("arbitrary",) * len(grid)))(*args, *comm.ins)
    mine = res[0] if single else list(res[:n_out])
    return mine, list(res[n_out:])


def _mm(a, b, mode, out_dtype, name, comm=None):
    if mode == "tn":
        m, kk = a.shape
        n = b.shape[1]
        tko, tn = _pick(kk, 512), _pick(n, 512)

        def body(a_ref, b_ref, o_ref):
            o_ref[...] = _dot(a_ref[...], b_ref[...], _TN).astype(o_ref.dtype)

        return _call(
            body, name, S((kk, n), out_dtype), (kk // tko, n // tn),
            [pl.BlockSpec((m, tko), lambda i, j: (0, i)), pl.BlockSpec((m, tn), lambda i, j: (0, j))],
            pl.BlockSpec((tko, tn), lambda i, j: (i, j)), ("parallel", "parallel"), (a, b), comm=comm)

    m, kk = a.shape
    n = b.shape[1] if mode == "nn" else b.shape[0]
    tn = _pick(n, 512)
    tk = kk if kk <= 2048 else _pick(kk, 1536)
    nk = kk // tk
    dims = _NN if mode == "nn" else _NT

    def body(a_ref, b_ref, o_ref, *acc):
        part = _dot(a_ref[...], b_ref[...], dims)
        if nk == 1:
            o_ref[...] = part.astype(o_ref.dtype)
        else:
            k = pl.program_id(1)

            @pl.when(k == 0)
            def _():
                acc[0][...] = part

            @pl.when(k > 0)
            def _():
                acc[0][...] += part

            @pl.when(k == nk - 1)
            def _():
                o_ref[...] = acc[0][...].astype(o_ref.dtype)

    b_spec = (pl.BlockSpec((tk, tn), lambda j, k: (k, j)) if mode == "nn"
              else pl.BlockSpec((tn, tk), lambda j, k: (j, k)))
    return _call(
        body, name, S((m, n), out_dtype), (n // tn, nk), [pl.BlockSpec((m, tk), lambda j, k: (0, k)), b_spec],
        pl.BlockSpec((m, tn), lambda j, k: (0, j)), ("parallel", "arbitrary"), (a, b),
        scratch=[pltpu.VMEM((m, tn), F32)] if nk > 1 else [], comm=comm)


def _rms(x, w):
    return x * lax.rsqrt(jnp.mean(x * x, axis=-1, keepdims=True) + RMS_EPS) * w


def _row_tile(rows):
    return rows // 8


def _resid_norm(h, br, w_post, next_ws, name):
    rows, d = h.shape
    tr = _row_tile(rows)
    has_br = br is not None
    nw = len(next_ws)

    def body(*refs):
        h_ref = refs[0]
        pos = 1
        x = h_ref[...]
        if has_br:
            x = x + _rms(refs[1][...], refs[2][...])
            pos = 3
        w_refs = refs[pos:pos + nw]
        outs = refs[pos + nw:]
        if has_br:
            outs[0][...] = x
            outs = outs[1:]
        for w_ref, o_ref in zip(w_refs, outs):
            o_ref[...] = _rms(x, w_ref[...]).astype(o_ref.dtype)

    row = pl.BlockSpec((tr, d), lambda i: (i, 0))
    vec = pl.BlockSpec((1, d), lambda i: (0, 0))
    ins = [h] + ([br, w_post] if has_br else []) + list(next_ws)
    in_specs = [row] + ([row, vec] if has_br else []) + [vec] * nw
    out_shape = ([S((rows, d), F32)] if has_br else []) + [S((rows, d), BF16)] * nw
    res = pl.pallas_call(body, name=name, out_shape=out_shape, grid=(rows // tr,), in_specs=in_specs,
                         out_specs=[row] * len(out_shape), compiler_params=_cp(("parallel",)))(*ins)
    if has_br:
        return res[0], list(res[1:])
    return h, list(res)


def _norm_bwd(x, w, dy, add, out_dtype, name):
    rows, d = x.shape
    tr = _row_tile(rows)
    has_add = add is not None

    def body(*refs):
        x_ref, w_ref, dy_ref = refs[:3]
        dx_ref, dw_ref = refs[-2:]
        xv = x_ref[...]
        r = lax.rsqrt(jnp.mean(xv * xv, axis=-1, keepdims=True) + RMS_EPS)
        dyv = dy_ref[...].astype(F32)
        wdy = dyv * w_ref[...]
        dx = r * wdy - xv * (r * r * r) * jnp.mean(xv * wdy, axis=-1, keepdims=True)
        if has_add:
            dx = dx + refs[3][...]
        dx_ref[...] = dx.astype(dx_ref.dtype)

        @pl.when(pl.program_id(0) == 0)
        def _():
            dw_ref[...] = jnp.zeros_like(dw_ref)

        dw_ref[...] += jnp.sum(dyv * xv * r, axis=0, keepdims=True)

    row = pl.BlockSpec((tr, d), lambda i: (i, 0))
    vec = pl.BlockSpec((1, d), lambda i: (0, 0))
    ins = [x, w, dy] + ([add] if has_add else [])
    return pl.pallas_call(body, name=name, out_shape=[S((rows, d), out_dtype), S((1, d), F32)], grid=(rows // tr,),
                          in_specs=[row, vec, row] + ([row] if has_add else []), out_specs=[row, vec],
                          compiler_params=_cp(("arbitrary",)))(*ins)


def _final_loss(h, br, w_post, tgt, name):
    rows, d = h.shape
    tr = _row_tile(rows)

    def body(h_ref, br_ref, w_ref, t_ref, dh_ref, loss_ref):
        i = pl.program_id(0)
        y = h_ref[...] + _rms(br_ref[...], w_ref[...])
        r = i * tr + lax.broadcasted_iota(jnp.int32, (tr, 1), 0)
        real = (r >= N_META) & (r < N_META + SEQ)
        diff = jnp.where(real, y - t_ref[...], 0.0)
        dh_ref[...] = diff * (1.0 / d)

        @pl.when(i == 0)
        def _():
            loss_ref[...] = jnp.zeros_like(loss_ref)

        loss_ref[...] += jnp.sum(diff * diff) * (0.5 / d)

    row = pl.BlockSpec((tr, d), lambda i: (i, 0))
    return pl.pallas_call(body, name=name, out_shape=[S((rows, d), F32), S((1, LANE), F32)], grid=(rows // tr,),
                          in_specs=[row, row, pl.BlockSpec((1, d), lambda i: (0, 0)), row],
                          out_specs=[row, pl.BlockSpec((1, LANE), lambda i: (0, 0))],
                          compiler_params=_cp(("arbitrary",)))(h, br, w_post, tgt)


def _gatenorm_fwd(y, zx, w, name):
    rows, d = y.shape
    tr = _row_tile(rows)

    def body(y_ref, z_ref, w_ref, o_ref):
        z = z_ref[...]
        o_ref[...] = _rms(y_ref[...] * z * _sigmoid(z), w_ref[...]).astype(o_ref.dtype)

    row = pl.BlockSpec((tr, d), lambda i: (i, 0))
    return pl.pallas_call(body, name=name, out_shape=S((rows, d), BF16), grid=(rows // tr,),
                          in_specs=[row, row, pl.BlockSpec((1, d), lambda i: (0, 0))], out_specs=row,
                          compiler_params=_cp(("parallel",)))(y, zx, w)


def _gatenorm_bwd(y, zx, w, dyn, name):
    rows, d = y.shape
    tr = _row_tile(rows)

    def body(y_ref, z_ref, w_ref, dyn_ref, dy_ref, dz_ref, dw_ref):
        yv, z = y_ref[...], z_ref[...]
        sg = _sigmoid(z)
        sz = z * sg
        g = yv * sz
        r = lax.rsqrt(jnp.mean(g * g, axis=-1, keepdims=True) + RMS_EPS)
        dyn_v = dyn_ref[...]
        wdy = dyn_v * w_ref[...]
        dg = r * wdy - g * (r * r * r) * jnp.mean(g * wdy, axis=-1, keepdims=True)
        dy_ref[...] = dg * sz
        dz_ref[...] = (dg * yv * sg * (1.0 + z * (1.0 - sg))).astype(dz_ref.dtype)

        @pl.when(pl.program_id(0) == 0)
        def _():
            dw_ref[...] = jnp.zeros_like(dw_ref)

        dw_ref[...] += jnp.sum(dyn_v * g * r, axis=0, keepdims=True)

    row = pl.BlockSpec((tr, d), lambda i: (i, 0))
    vec = pl.BlockSpec((1, d), lambda i: (0, 0))
    return pl.pallas_call(body, name=name, out_shape=[S((rows, d), F32), S((rows, d), BF16), S((1, d), F32)],
                          grid=(rows // tr,), in_specs=[row, row, vec, row], out_specs=[row, row, vec],
                          compiler_params=_cp(("arbitrary",)))(y, zx, w, dyn)


def _shift_down(x, s, rows_iota):
    if s == 0:
        return x
    return jnp.where(rows_iota >= s, pltpu.roll(x, s, 0), 0.0)


def _shift_up(x, s, rows_iota):
    if s == 0:
        return x
    rows = x.shape[0]
    return jnp.where(rows_iota < rows - s, pltpu.roll(x, rows - s, 0), 0.0)


def _r16(v):
    return v.astype(BF16).astype(F32)


def _conv(x, w_ref, b_ref, taps, rows_iota):
    x = _r16(x)
    acc = jnp.zeros_like(x)
    for k in range(taps):
        acc = acc + _r16(w_ref[k:k + 1, :]) * _shift_down(x, taps - 1 - k, rows_iota)
    return acc + b_ref[...]


def _conv_bwd(x, du, w_ref, dw_ref, db_ref, taps, rows_iota):
    db_ref[...] = jnp.sum(du, axis=0, keepdims=True)
    x, du = _r16(x), _r16(du)
    dx = jnp.zeros_like(x)
    for k in range(taps):
        s = taps - 1 - k
        dx = dx + _r16(w_ref[k:k + 1, :]) * _shift_up(du, s, rows_iota)
        dw_ref[k:k + 1, :] = jnp.sum(du * _shift_down(x, s, rows_iota), axis=0, keepdims=True)
    return dx


def _conv_silu_fwd(zx, w, b, name):
    rows = zx.shape[0]
    cb = 512
    off = D_INNER // cb

    def body(x_ref, w_ref, b_ref, o_ref):
        it = lax.broadcasted_iota(jnp.int32, (rows, 1), 0)
        u = _conv(x_ref[...], w_ref, b_ref, SSM_CONV, it)
        o_ref[...] = u * _sigmoid(u)

    return pl.pallas_call(
        body, name=name, out_shape=S((rows, D_XBC), F32), grid=(D_XBC // cb,),
        in_specs=[pl.BlockSpec((rows, cb), lambda j: (0, off + j)), pl.BlockSpec((SSM_CONV, cb), lambda j: (0, j)),
                  pl.BlockSpec((1, cb), lambda j: (0, j))],
        out_specs=pl.BlockSpec((rows, cb), lambda j: (0, j)), compiler_params=_cp(("parallel",)))(zx, w, b)


def _conv_silu_bwd(zx, dxbc, w, b, name):
    rows = zx.shape[0]
    cb = 512
    off = D_INNER // cb

    def body(x_ref, d_ref, w_ref, b_ref, dx_ref, dw_ref, db_ref):
        it = lax.broadcasted_iota(jnp.int32, (rows, 1), 0)
        x = x_ref[...]
        u = _conv(x, w_ref, b_ref, SSM_CONV, it)
        sg = _sigmoid(u)
        du = d_ref[...] * sg * (1.0 + u * (1.0 - sg))
        dx_ref[...] = _conv_bwd(x, du, w_ref, dw_ref, db_ref, SSM_CONV, it).astype(dx_ref.dtype)

    col = pl.BlockSpec((rows, cb), lambda j: (0, j))
    wsp = pl.BlockSpec((SSM_CONV, cb), lambda j: (0, j))
    bsp = pl.BlockSpec((1, cb), lambda j: (0, j))
    return pl.pallas_call(
        body, name=name, out_shape=[S((rows, D_XBC), BF16), S((SSM_CONV, D_XBC), F32), S((1, D_XBC), F32)],
        grid=(D_XBC // cb,), in_specs=[pl.BlockSpec((rows, cb), lambda j: (0, off + j)), col, wsp, bsp],
        out_specs=[col, wsp, bsp], compiler_params=_cp(("parallel",)))(zx, dxbc, w, b)


def _ffn_act_fwd(u, w, b, name):
    rows = u.shape[0]
    cb = 256
    nb = D_FF // cb

    def body(g_ref, v_ref, wg_ref, wv_ref, bg_ref, bv_ref, o_ref):
        it = lax.broadcasted_iota(jnp.int32, (rows, 1), 0)
        g = _conv(g_ref[...], wg_ref, bg_ref, FFN_CONV, it)
        v = _conv(v_ref[...], wv_ref, bv_ref, FFN_CONV, it)
        o_ref[...] = (g * _sigmoid(g) * v).astype(o_ref.dtype)

    def sp(r, shift):
        return pl.BlockSpec((r, cb), lambda j: (0, shift + j))

    return pl.pallas_call(
        body, name=name, out_shape=S((rows, D_FF), BF16), grid=(nb,),
        in_specs=[sp(rows, 0), sp(rows, nb), sp(FFN_CONV, 0), sp(FFN_CONV, nb), sp(1, 0), sp(1, nb)],
        out_specs=sp(rows, 0), compiler_params=_cp(("parallel",)))(u, u, w, w, b, b)


def _ffn_act_bwd(u, dact, w, b, name, comm=None):
    rows = u.shape[0]
    cb = 256
    nb = D_FF // cb

    def body(g_ref, v_ref, d_ref, wg_ref, wv_ref, bg_ref, bv_ref, dg_ref, dv_ref, dwg_ref, dwv_ref, dbg_ref, dbv_ref):
        it = lax.broadcasted_iota(jnp.int32, (rows, 1), 0)
        xg, xv = g_ref[...], v_ref[...]
        g = _conv(xg, wg_ref, bg_ref, FFN_CONV, it)
        v = _conv(xv, wv_ref, bv_ref, FFN_CONV, it)
        sg = _sigmoid(g)
        d = d_ref[...]
        dgate = d * v * sg * (1.0 + g * (1.0 - sg))
        dval = d * g * sg
        dg_ref[...] = _conv_bwd(xg, dgate, wg_ref, dwg_ref, dbg_ref, FFN_CONV, it).astype(dg_ref.dtype)
        dv_ref[...] = _conv_bwd(xv, dval, wv_ref, dwv_ref, dbv_ref, FFN_CONV, it).astype(dv_ref.dtype)

    def sp(r, shift):
        return pl.BlockSpec((r, cb), lambda j: (0, shift + j))

    return _call(
        body, name,
        [S((rows, D_FF), BF16), S((rows, D_FF), BF16), S((FFN_CONV, D_FF), F32), S((FFN_CONV, D_FF), F32),
         S((1, D_FF), F32), S((1, D_FF), F32)],
        (nb,),
        [sp(rows, 0), sp(rows, nb), sp(rows, 0), sp(FFN_CONV, 0), sp(FFN_CONV, nb), sp(1, 0), sp(1, nb)],
        [sp(rows, 0), sp(rows, 0), sp(FFN_CONV, 0), sp(FFN_CONV, 0), sp(1, 0), sp(1, 0)],
        ("parallel",), (u, u, dact, w, w, b, b), comm=comm)


def _ssd_consts(dtp_ref, bias_ref, alog_ref, hg):
    lane = lax.broadcasted_iota(jnp.int32, (1, LANE), 1)
    pre = dtp_ref[...] + bias_ref[...]
    dt = _softplus(pre)
    a_row = jnp.where(lane < hg, -jnp.exp(alog_ref[...]), 0.0)
    ri = lax.broadcasted_iota(jnp.int32, (T, T), 0)
    ci = lax.broadcasted_iota(jnp.int32, (T, T), 1)
    cs = _dot_hi((ri >= ci).astype(F32), dt * a_row)
    return pre, dt, a_row, cs, ri, ci, lane


def _ssd_fwd(xbc, zx, bias, alog, dsk, name, comm=None):
    rows = xbc.shape[0]
    nc = rows // T
    hg = SSM_HEADS // SSM_GROUPS
    gw = hg * HEAD_P
    xoff, boff, coff = 0, D_INNER // D_STATE, (D_INNER + D_BC) // D_STATE
    dtoff = (D_INNER + D_XBC) // LANE

    def body(x_ref, b_ref, c_ref, dtp_ref, bias_ref, alog_ref, dsk_ref, y_ref, hst_ref, hs):
        c = pl.program_id(1)

        @pl.when(c == 0)
        def _():
            hs[...] = jnp.zeros_like(hs)

        _, dt, _, cs, ri, ci, _ = _ssd_consts(dtp_ref, bias_ref, alog_ref, hg)
        cst = cs.T
        x = x_ref[...]
        bb, cbf = b_ref[...].astype(BF16), c_ref[...].astype(BF16)
        gmat = _dot(cbf, bb, _NT)
        causal = ri >= ci
        dskv = dsk_ref[...]
        hst_ref[0, 0] = hs[...]
        for k in range(hg):
            csk, csr = cs[:, k:k + 1], cst[k:k + 1, :]
            lm = jnp.exp(jnp.where(causal, csk - csr, NEG))
            xk = x[:, k * HEAD_P:(k + 1) * HEAD_P]
            xdt = xk * dt[:, k:k + 1]
            hk = hs[k * HEAD_P:(k + 1) * HEAD_P, :]
            yd = _dot((gmat * lm).astype(BF16), xdt.astype(BF16))
            yo = jnp.exp(csk) * _dot(cbf, hk.astype(BF16), _NT)
            y_ref[:, k * HEAD_P:(k + 1) * HEAD_P] = yd + yo + dskv[:, k:k + 1] * xk
            cl = cs[T - 1:T, k:k + 1]
            st = _dot((xdt * jnp.exp(cl - csk)).astype(BF16), bb, _TN)
            hs[k * HEAD_P:(k + 1) * HEAD_P, :] = jnp.exp(cl) * hk + st

    vec = pl.BlockSpec((1, LANE), lambda g, c: (0, g))
    return _call(
        body, name, [S((rows, D_INNER), F32), S((nc, SSM_GROUPS, gw, D_STATE), F32)], (SSM_GROUPS, nc),
        [pl.BlockSpec((T, gw), lambda g, c: (c, xoff + g)),
         pl.BlockSpec((T, D_STATE), lambda g, c: (c, boff + g)),
         pl.BlockSpec((T, D_STATE), lambda g, c: (c, coff + g)),
         pl.BlockSpec((T, LANE), lambda g, c: (c, dtoff + g)), vec, vec, vec],
        [pl.BlockSpec((T, gw), lambda g, c: (c, g)), pl.BlockSpec((1, 1, gw, D_STATE), lambda g, c: (c, g, 0, 0))],
        ("parallel", "arbitrary"), (xbc, xbc, xbc, zx, bias, alog, dsk),
        scratch=[pltpu.VMEM((gw, D_STATE), F32)], comm=comm)


def _ssd_bwd(xbc, zx, bias, alog, dsk, dy, hst, name, comm=None):
    rows = xbc.shape[0]
    nc = rows // T
    hg = SSM_HEADS // SSM_GROUPS
    gw = hg * HEAD_P
    boff, coff = D_INNER // D_STATE, (D_INNER + D_BC) // D_STATE
    dtoff = (D_INNER + D_XBC) // LANE

    def body(x_ref, b_ref, c_ref, dtp_ref, bias_ref, alog_ref, dsk_ref, dy_ref, hst_ref,
             dx_ref, db_ref, dc_ref, ddtp_ref, dalog_ref, ddsk_ref, dbias_ref, dhs):
        step = pl.program_id(1)

        @pl.when(step == 0)
        def _():
            dhs[...] = jnp.zeros_like(dhs)
            dalog_ref[...] = jnp.zeros_like(dalog_ref)
            ddsk_ref[...] = jnp.zeros_like(ddsk_ref)
            dbias_ref[...] = jnp.zeros_like(dbias_ref)

        pre, dt, a_row, cs, ri, ci, lane = _ssd_consts(dtp_ref, bias_ref, alog_ref, hg)
        cst = cs.T
        x, dyv = x_ref[...], dy_ref[...]
        bb, cbf = b_ref[...].astype(BF16), c_ref[...].astype(BF16)
        gt = _dot(bb, cbf, _NT)
        causal_t = ci >= ri
        dskv = dsk_ref[...]
        last = lax.broadcasted_iota(jnp.int32, (T, 1), 0) == T - 1
        dgt = jnp.zeros((T, T), F32)
        dc_acc = jnp.zeros((T, D_STATE), F32)
        db_acc = jnp.zeros((T, D_STATE), F32)
        ddt_acc = jnp.zeros((T, LANE), F32)
        dcs_acc = jnp.zeros((T, LANE), F32)
        dcs_row = jnp.zeros((T, LANE), F32)
        ddsk_acc = jnp.zeros((1, LANE), F32)
        head_row = lax.broadcasted_iota(jnp.int32, (T, 1), 0)
        for k in range(hg):
            sl = slice(k * HEAD_P, (k + 1) * HEAD_P)
            csk, csr = cs[:, k:k + 1], cst[k:k + 1, :]
            lt = jnp.exp(jnp.where(causal_t, csr - csk, NEG))
            xk, dyk = x[:, sl], dyv[:, sl]
            dtk, dk = dt[:, k:k + 1], dskv[:, k:k + 1]
            dyb = dyk.astype(BF16)
            mpt = gt * lt
            z = _dot(mpt.astype(BF16), dyb)
            dmt = _dot((xk * dtk).astype(BF16), dyb, _NT)
            dgt = dgt + dmt * lt
            q = dmt * mpt
            xz = jnp.sum(xk * z, axis=1, keepdims=True)
            q_rows = jnp.sum(q, axis=1, keepdims=True)
            dcs_row = dcs_row + jnp.where(head_row == k, jnp.sum(q, axis=0, keepdims=True), 0.0)
            hk = hst_ref[0, 0, sl, :]
            dhn = dhs[sl, :]
            e = jnp.exp(csk)
            cl = cs[T - 1:T, k:k + 1]
            wdec = jnp.exp(cl - csk)
            w = wdec * dtk
            r = _dot(bb, dhn.astype(BF16), _NT)
            dx_ref[:, sl] = dtk * z + dk * dyk + r * w
            dw = jnp.sum(r * xk, axis=1, keepdims=True)
            dcl = jnp.exp(cl) * jnp.sum(dhn * hk) + jnp.sum(dw * w)
            yo = e * _dot(cbf, hk.astype(BF16), _NT)
            dcs_k = jnp.sum(dyk * yo, axis=1, keepdims=True) - q_rows - dw * w + jnp.where(last, dcl, 0.0)
            dye = (dyk * e).astype(BF16)
            dc_acc = dc_acc + _dot(dye, hk.astype(BF16))
            db_acc = db_acc + _dot((xk * w).astype(BF16), dhn.astype(BF16))
            dhs[sl, :] = jnp.exp(cl) * dhn + _dot(dye, cbf, _TN)
            onehot = (lane == k).astype(F32)
            ddt_acc = ddt_acc + (xz + dw * wdec) * onehot
            dcs_acc = dcs_acc + dcs_k * onehot
            ddsk_acc = ddsk_acc + jnp.sum(dyk * xk) * onehot
        dc_ref[...] = _dot(dgt.T.astype(BF16), bb) + dc_acc
        db_ref[...] = _dot(dgt.astype(BF16), cbf) + db_acc
        da = _dot_hi((ci >= ri).astype(F32), dcs_acc + dcs_row.T)
        ddtp = (ddt_acc + da * a_row) * _sigmoid(pre)
        ddtp = jnp.where(lane < hg, ddtp, 0.0)
        ddtp_ref[...] = ddtp
        dbias_ref[...] += jnp.sum(ddtp, axis=0, keepdims=True)
        dalog_ref[...] += jnp.sum(da * dt, axis=0, keepdims=True) * a_row
        ddsk_ref[...] += ddsk_acc

    def rc(c):
        return nc - 1 - c

    vec = pl.BlockSpec((1, LANE), lambda g, c: (0, g))
    xsp = pl.BlockSpec((T, gw), lambda g, c: (rc(c), g))
    return _call(
        body, name,
        [S((rows, D_INNER), F32), S((rows, D_BC), F32), S((rows, D_BC), F32),
         S((rows, SSM_GROUPS * LANE), F32), S((1, SSM_GROUPS * LANE), F32),
         S((1, SSM_GROUPS * LANE), F32), S((1, SSM_GROUPS * LANE), F32)],
        (SSM_GROUPS, nc),
        [xsp,
         pl.BlockSpec((T, D_STATE), lambda g, c: (rc(c), boff + g)),
         pl.BlockSpec((T, D_STATE), lambda g, c: (rc(c), coff + g)),
         pl.BlockSpec((T, LANE), lambda g, c: (rc(c), dtoff + g)), vec, vec, vec,
         xsp, pl.BlockSpec((1, 1, gw, D_STATE), lambda g, c: (rc(c), g, 0, 0))],
        [xsp,
         pl.BlockSpec((T, D_STATE), lambda g, c: (rc(c), g)),
         pl.BlockSpec((T, D_STATE), lambda g, c: (rc(c), g)),
         pl.BlockSpec((T, LANE), lambda g, c: (rc(c), g)), vec, vec, vec],
        ("parallel", "arbitrary"), (xbc, xbc, xbc, zx, bias, alog, dsk, dy, hst),
        scratch=[pltpu.VMEM((gw, D_STATE), F32)], comm=comm)


def _attn_tiles(kv_ref, j):
    prev = jnp.maximum(j - 1, 0)
    meta = kv_ref[0:T, :]
    prv = kv_ref[pl.ds(pl.multiple_of(prev * T, T), T), :]
    cur = kv_ref[pl.ds(pl.multiple_of(j * T, T), T), :]
    return jnp.concatenate([meta, prv, cur], axis=0)


def _attn_mask(j):
    r = j * T + lax.broadcasted_iota(jnp.int32, (T, 3 * T), 0)
    col = lax.broadcasted_iota(jnp.int32, (T, 3 * T), 1)
    t0, t1 = col < T, col < 2 * T
    s = jnp.where(t0, col, (j - 2) * T + col)
    ok = (s <= r) & ((s < N_META) | (s > r - WINDOW))
    use = (t0 & (j >= 2) & (col < N_META)) | (jnp.logical_not(t0) & t1 & (j >= 1)) | jnp.logical_not(t1)
    return ok & use


def _attn_probs(qh, k3, mask, sink):
    sc = jnp.where(mask, _dot(qh, k3, _NT), NEG)
    m = jnp.maximum(jnp.max(sc, axis=1, keepdims=True), sink)
    p = jnp.exp(sc - m)
    es = jnp.exp(sink - m)
    inv = 1.0 / (jnp.sum(p, axis=1, keepdims=True) + es)
    return p * inv, es * inv


def _attn_fwd(q, kv, sinks, name, comm=None):
    rows = q.shape[0]
    scale = 1.0 / math.sqrt(ATTN_DH)
    qpk = N_Q_HEADS // N_KV_HEADS

    def body(q_ref, kv_ref, s_ref, o_ref):
        j = pl.program_id(0)
        kv3 = _attn_tiles(kv_ref, j).astype(BF16)
        mask = _attn_mask(j)
        qv = (q_ref[...] * scale).astype(BF16)
        sk = s_ref[...]
        for kh in range(N_KV_HEADS):
            k3 = kv3[:, kh * ATTN_DH:(kh + 1) * ATTN_DH]
            v3 = kv3[:, D_KV + kh * ATTN_DH:D_KV + (kh + 1) * ATTN_DH]
            for g in range(qpk):
                h = kh * qpk + g
                p, _ = _attn_probs(qv[:, h * ATTN_DH:(h + 1) * ATTN_DH], k3, mask, sk[:, h:h + 1])
                o_ref[:, h * ATTN_DH:(h + 1) * ATTN_DH] = _dot(p.astype(BF16), v3).astype(o_ref.dtype)

    return _call(
        body, name, S((rows, D_MODEL), BF16), (rows // T,),
        [pl.BlockSpec((T, D_MODEL), lambda j: (j, 0)), pl.BlockSpec((rows, 2 * D_KV), lambda j: (0, 0)),
         pl.BlockSpec((1, N_Q_HEADS), lambda j: (0, 0))],
        pl.BlockSpec((T, D_MODEL), lambda j: (j, 0)), ("parallel",), (q, kv, sinks), comm=comm)


def _attn_bwd(q, kv, sinks, do, name, comm=None):
    rows = q.shape[0]
    scale = 1.0 / math.sqrt(ATTN_DH)
    qpk = N_Q_HEADS // N_KV_HEADS

    def body(q_ref, kv_ref, s_ref, do_ref, dq_ref, dkv_ref, ds_ref):
        j = pl.program_id(0)

        @pl.when(j == 0)
        def _():
            dkv_ref[...] = jnp.zeros_like(dkv_ref)
            ds_ref[...] = jnp.zeros_like(ds_ref)

        kv3 = _attn_tiles(kv_ref, j).astype(BF16)
        mask = _attn_mask(j)
        qv = (q_ref[...] * scale).astype(BF16)
        dov = do_ref[...].astype(BF16)
        sk = s_ref[...]
        lane = lax.broadcasted_iota(jnp.int32, (1, LANE), 1)
        ds_acc = jnp.zeros((1, LANE), F32)
        prev = jnp.maximum(j - 1, 0)
        for kh in range(N_KV_HEADS):
            ksl = slice(kh * ATTN_DH, (kh + 1) * ATTN_DH)
            vsl = slice(D_KV + kh * ATTN_DH, D_KV + (kh + 1) * ATTN_DH)
            k3, v3 = kv3[:, ksl], kv3[:, vsl]
            dk3 = jnp.zeros((3 * T, ATTN_DH), F32)
            dv3 = jnp.zeros((3 * T, ATTN_DH), F32)
            for g in range(qpk):
                h = kh * qpk + g
                hs = slice(h * ATTN_DH, (h + 1) * ATTN_DH)
                qh, doh = qv[:, hs], dov[:, hs]
                p, ps = _attn_probs(qh, k3, mask, sk[:, h:h + 1])
                dp = _dot(doh, v3, _NT)
                delta = jnp.sum(p * dp, axis=1, keepdims=True)
                dsc = (p * (dp - delta)).astype(BF16)
                dq_ref[:, hs] = (_dot(dsc, k3) * scale).astype(dq_ref.dtype)
                dk3 = dk3 + _dot(dsc, qh, _TN)
                dv3 = dv3 + _dot(p.astype(BF16), doh, _TN)
                ds_acc = ds_acc - jnp.sum(ps * delta) * (lane == h).astype(F32)
            for t, start in enumerate((0, pl.multiple_of(prev * T, T), pl.multiple_of(j * T, T))):
                rsl = pl.ds(start, T)
                dkv_ref[rsl, ksl] += dk3[t * T:(t + 1) * T, :]
                dkv_ref[rsl, vsl] += dv3[t * T:(t + 1) * T, :]
        ds_ref[...] += ds_acc

    blk = pl.BlockSpec((T, D_MODEL), lambda j: (j, 0))
    full = pl.BlockSpec((rows, 2 * D_KV), lambda j: (0, 0))
    return _call(
        body, name, [S((rows, D_MODEL), BF16), S((rows, 2 * D_KV), F32), S((1, LANE), F32)], (rows // T,),
        [blk, full, pl.BlockSpec((1, N_Q_HEADS), lambda j: (0, 0)), blk],
        [blk, full, pl.BlockSpec((1, LANE), lambda j: (0, 0))], ("arbitrary",), (q, kv, sinks, do), comm=comm)


BLOCK_BYTES = 1 << 20


def _div_tile(rows, cols):
    cap = max(16, BLOCK_BYTES // (4 * cols))
    best = None
    for t in range(16, min(rows, cap) + 1, 16):
        if rows % t == 0:
            best = t
    return best if best is not None else rows


def _adamw(parts, w, m, v, name):
    n, rows, cols = parts.shape
    tr = _div_tile(rows, cols)
    c1 = 1.0 / (1.0 - B1 ** STEP)
    c2 = 1.0 / (1.0 - B2 ** STEP)

    def body(p_ref, w_ref, m_ref, v_ref, g_ref, d_ref, nm_ref, nv_ref):
        g = p_ref[0].astype(F32)
        for i in range(1, n):
            g = g + p_ref[i].astype(F32)
        nm = B1 * m_ref[...] + (1.0 - B1) * g
        nv = B2 * v_ref[...] + (1.0 - B2) * (g * g)
        g_ref[...] = g
        nm_ref[...] = nm
        nv_ref[...] = nv
        d_ref[...] = -LR * ((nm * c1) / (jnp.sqrt(nv * c2) + EPS) + WD * w_ref[...])

    row = pl.BlockSpec((tr, cols), lambda i: (i, 0))
    return pl.pallas_call(
        body, name=name, out_shape=[S((rows, cols), F32)] * 4, grid=(rows // tr,),
        in_specs=[pl.BlockSpec((n, tr, cols), lambda i: (0, i, 0)), row, row, row], out_specs=[row] * 4,
        compiler_params=_cp(("parallel",)))(parts, w, m, v)


def _col_segments(ws, runs):
    segs = []
    for glo, mlo, n in runs:
        while n > 0:
            d, off = divmod(glo, ws)
            take = min(n, ws - off)
            segs.append((d, off, mlo, take))
            glo, mlo, n = glo + take, mlo + take, n - take
    return segs


def _assemble_cols(g, width, segs, name):
    _, rows, ws = g.shape
    rb = _div_tile(rows, width // 2)

    def body(g_ref, o_ref):
        o_ref[...] = jnp.zeros_like(o_ref)
        for d, off, mlo, n in segs:
            o_ref[:, mlo:mlo + n] = g_ref[d, :, off:off + n]

    return pl.pallas_call(
        body, name=name, out_shape=S((rows, width), g.dtype), grid=(rows // rb,),
        in_specs=[pl.BlockSpec((N_DEV, rb, ws), lambda i: (0, i, 0))],
        out_specs=pl.BlockSpec((rb, width), lambda i: (i, 0)), compiler_params=_cp(("parallel",)))(g)


def _scatter_cols(dw, ws, segs, name):
    rows, width = dw.shape
    rb = _div_tile(rows, width)

    def body(w_ref, o_ref):
        for d, off, mlo, n in segs:
            o_ref[d, :, off:off + n] = w_ref[:, mlo:mlo + n].astype(o_ref.dtype)

    return pl.pallas_call(
        body, name=name, out_shape=S((N_DEV, rows, ws), BF16), grid=(rows // rb,),
        in_specs=[pl.BlockSpec((rb, width), lambda i: (i, 0))],
        out_specs=pl.BlockSpec((N_DEV, rb, ws), lambda i: (0, i, 0)), compiler_params=_cp(("parallel",)))(dw)


def _gather_comm(xs):
    n = len(xs)

    def setup(x_refs, out_refs, sems):
        send_sems, recv_sems, local_sems = sems
        mx, my, mc = lax.axis_index("x"), lax.axis_index("y"), lax.axis_index("c")
        me, sibling = (mx, my, mc), (mx, my, 1 - mc)
        chips = [(1 - mx, my), (mx, 1 - my), (1 - mx, 1 - my)]

        def blk(a, px, py, pc):
            return out_refs[a].at[4 * px + 2 * py + pc]

        def copy(a, k, block, to, src=None):
            return pltpu.make_async_remote_copy(
                src_ref=blk(a, *block) if src is None else src, dst_ref=blk(a, *block),
                send_sem=send_sems.at[a, k], recv_sem=recv_sems.at[a, k], device_id=to, device_id_type=_MESH)

        mine = [pltpu.make_async_copy(x_refs[a], blk(a, *me), local_sems.at[a]) for a in range(n)]
        own = []
        for a in range(n):
            own.append(copy(a, 0, me, sibling, src=x_refs[a]))
            own += [copy(a, 1 + i, me, (*chip, mc), src=x_refs[a]) for i, chip in enumerate(chips)]
        return me, sibling, chips, mc, copy, mine, own

    def first(x_refs, out_refs, sems):
        _, _, _, _, _, mine, own = setup(x_refs, out_refs, sems)
        for cp in mine + own:
            cp.start()

    def last(x_refs, out_refs, sems):
        me, sibling, chips, mc, copy, mine, own = setup(x_refs, out_refs, sems)
        passed = []
        for a in range(n):
            for i, chip in enumerate(chips):
                copy(a, 1 + i, (*chip, mc), me).wait_recv()
                passed.append(copy(a, 4 + i, (*chip, mc), sibling))
                passed[-1].start()
        for a in range(n):
            copy(a, 0, sibling, me).wait_recv()
            for i, chip in enumerate(chips):
                copy(a, 4 + i, (*chip, 1 - mc), me).wait_recv()
        for cp in own + passed:
            cp.wait_send()
        for cp in mine:
            cp.wait()

    return _Comm(list(xs), [S((N_DEV,) + x.shape, x.dtype) for x in xs],
                 [pltpu.SemaphoreType.DMA((n, 7)), pltpu.SemaphoreType.DMA((n, 7)), pltpu.SemaphoreType.DMA((n,))],
                 first, last)


def _scatter_comm(gs):
    n = len(gs)

    def copies(g_refs, out_refs, sems):
        send_sems, recv_sems, local_sems = sems
        mx, my, mc = lax.axis_index("x"), lax.axis_index("y"), lax.axis_index("c")
        me = 4 * mx + 2 * my + mc
        mine = [pltpu.make_async_copy(g_refs[a].at[me], out_refs[a].at[me], local_sems.at[a]) for a in range(n)]
        cps = []
        for a in range(n):
            for k in range(1, N_DEV):
                px, py, pc = mx ^ (k >> 2), my ^ ((k >> 1) & 1), mc ^ (k & 1)
                cps.append(pltpu.make_async_remote_copy(
                    src_ref=g_refs[a].at[4 * px + 2 * py + pc], dst_ref=out_refs[a].at[me],
                    send_sem=send_sems.at[a, k - 1], recv_sem=recv_sems.at[a, k - 1],
                    device_id=(px, py, pc), device_id_type=_MESH))
        return mine + cps

    def first(g_refs, out_refs, sems):
        for cp in copies(g_refs, out_refs, sems):
            cp.start()

    def last(g_refs, out_refs, sems):
        for cp in copies(g_refs, out_refs, sems):
            cp.wait()

    return _Comm(list(gs), [S(g.shape, g.dtype) for g in gs],
                 [pltpu.SemaphoreType.DMA((n, 7)), pltpu.SemaphoreType.DMA((n, 7)), pltpu.SemaphoreType.DMA((n,))],
                 first, last)


def _run_comm(comm, name):
    ci, co = len(comm.ins), len(comm.out_shapes)

    def body(*refs):
        comm.first(refs[:ci], refs[ci:ci + co], refs[ci + co:])
        comm.last(refs[:ci], refs[ci:ci + co], refs[ci + co:])

    return pl.pallas_call(body, name=name, out_shape=list(comm.out_shapes), in_specs=[_HBM] * ci,
                          out_specs=[_HBM] * co, scratch_shapes=list(comm.scratch))(*comm.ins)


def _flat_rows(n_elems, mult):
    rows = -(-n_elems // LANE)
    return -(-rows // mult) * mult


def _pack(arrs, lead, mult, dtype):
    lead_shape = arrs[0].shape[:lead]
    flat = jnp.concatenate([a.astype(dtype).reshape(lead_shape + (-1,)) for a in arrs], axis=-1)
    n = flat.shape[-1]
    rows = _flat_rows(n, mult)
    flat = jnp.pad(flat, [(0, 0)] * lead + [(0, rows * LANE - n)])
    return flat.reshape(lead_shape + (rows, LANE))


def _unpack(flat, lead, shapes):
    lead_shape = flat.shape[:lead]
    flat = flat.reshape(lead_shape + (-1,))
    out, off = [], 0
    for shp in shapes:
        n = math.prod(shp)
        out.append(flat[..., off:off + n].reshape(lead_shape + tuple(shp)))
        off += n
    return out


def _split8(full, ax, n):
    shp = full.shape
    return jnp.moveaxis(full.reshape(shp[:ax] + (N_DEV, n) + shp[ax + 1:]), ax, 0)


def _join8(g, ax):
    shp = g.shape[1:]
    return jnp.moveaxis(g, 0, ax).reshape(shp[:ax] + (N_DEV * shp[ax],) + shp[ax + 1:])


def _group_lanes(v, hg):
    v = v.reshape(SSM_GROUPS, hg)
    return jnp.pad(v, ((0, 0), (0, LANE - hg))).reshape(1, SSM_GROUPS * LANE)


def _ungroup_lanes(v, hg):
    return v.reshape(SSM_GROUPS, LANE)[:, :hg].reshape(1, SSM_GROUPS * hg)


def kernel(x, meta_tokens, a_norm_pre, a_w_in, a_conv_w, a_conv_b, a_dt_bias, a_a_log, a_d_skip, a_gate_norm, a_w_out, a_norm_post, kv_norm, w_kv, b_norm_pre, b_w_q, b_sinks, b_w_o, b_norm_post, f_norm_pre, f_w_up, f_conv_w, f_conv_b, f_w_down, f_norm_post, loss_target, m_meta_tokens, m_a_norm_pre, m_a_w_in, m_a_conv_w, m_a_conv_b, m_a_dt_bias, m_a_a_log, m_a_d_skip, m_a_gate_norm, m_a_w_out, m_a_norm_post, m_kv_norm, m_w_kv, m_b_norm_pre, m_b_w_q, m_b_sinks, m_b_w_o, m_b_norm_post, m_f_norm_pre, m_f_w_up, m_f_conv_w, m_f_conv_b, m_f_w_down, m_f_norm_post, v_meta_tokens, v_a_norm_pre, v_a_w_in, v_a_conv_w, v_a_conv_b, v_a_dt_bias, v_a_a_log, v_a_d_skip, v_a_gate_norm, v_a_w_out, v_a_norm_post, v_kv_norm, v_w_kv, v_b_norm_pre, v_b_w_q, v_b_sinks, v_b_w_o, v_b_norm_post, v_f_norm_pre, v_f_w_up, v_f_conv_w, v_f_conv_b, v_f_w_down, v_f_norm_post):
    args = locals()
    wts = {n: args[n] for n in WEIGHTS}
    mom = {n: args["m_" + n] for n in WEIGHTS}
    var = {n: args["v_" + n] for n in WEIGHTS}
    mx, my, mc = lax.axis_index("x"), lax.axis_index("y"), lax.axis_index("c")
    me = 4 * mx + 2 * my + mc
    rows = _seq_rows()
    hg = SSM_HEADS // SSM_GROUPS
    d = D_MODEL

    n_main = D_INNER + D_XBC
    ws_in, ws_up = a_w_in.shape[2], f_w_up.shape[2]
    segs_in = _col_segments(ws_in, [(0, 0, n_main)] + [(n_main + hg * g, n_main + LANE * g, hg)
                                                      for g in range(SSM_GROUPS)])
    segs_up = _col_segments(ws_up, [(0, 0, 2 * D_FF)])
    def gather_of(*ws):
        return _gather_comm([w.astype(BF16) for w in ws])

    g_in, small_full = _run_comm(_gather_comm([a_w_in[0].astype(BF16), _pack([wts[n] for n in SMALL], 0, 8, F32)]),
                                 "gather_first")
    full = {}
    for n, g in zip(SMALL, _unpack(small_full, 1, [wts[n].shape for n in SMALL])):
        full[n] = _join8(g, SHARD_AXIS[n])
    w_in_all = _assemble_cols(g_in, n_main + SSM_GROUPS * LANE, segs_in, "asm_w_in")
    w_up, w_down = [None, None], [None, None]
    bias_g = _group_lanes(wts["a_dt_bias"], hg)
    alog_g = _group_lanes(wts["a_a_log"], hg)
    dsk_g = _group_lanes(wts["a_d_skip"], hg)
    a_conv_w, a_conv_b = full["a_conv_w"][0], full["a_conv_b"]
    f_cw, f_cb = full["f_conv_w"], wts["f_conv_b"]
    fpre, fpost = wts["f_norm_pre"], wts["f_norm_post"]

    pad_rows = rows - N_META - SEQ
    h0 = jnp.concatenate([full["meta_tokens"], x[0], jnp.zeros((pad_rows, d), F32)], axis=0)
    tgt = jnp.pad(loss_target[0], ((N_META, pad_rows), (0, 0)))

    _, (hn0,) = _resid_norm(h0, None, None, [full["a_norm_pre"]], "norm_a_pre")
    zx, (g_out,) = _mm(hn0, w_in_all, "nn", F32, "mm_in", comm=gather_of(a_w_out[0]))
    w_out = g_out.reshape(D_INNER, d)
    xbc = _conv_silu_fwd(zx, a_conv_w, a_conv_b, "conv_a")
    (y_ssd, hst), (g_up0, g_dn0) = _ssd_fwd(xbc, zx, bias_g, alog_g, dsk_g, "ssd_fwd",
                                            comm=gather_of(f_w_up[0], f_w_down[0]))
    w_up[0], w_down[0] = _assemble_cols(g_up0, 2 * D_FF, segs_up, "asm_w_up0"), g_dn0.reshape(D_FF, d)
    yn = _gatenorm_fwd(y_ssd, zx, full["a_gate_norm"], "gatenorm")
    mix_a = _mm(yn, w_out, "nn", F32, "mm_out")
    h1, (fn0,) = _resid_norm(h0, mix_a, full["a_norm_post"], [fpre[0:1]], "resid_a")

    def ffn_fwd(fn, i, comm=None):
        u = _mm(fn, w_up[i], "nn", F32, f"mm_up{i}", comm=comm)
        u, got = u if comm is not None else (u, None)
        act = _ffn_act_fwd(u, f_cw[i], f_cb[i:i + 1], f"ffn_act{i}")
        return u, act, _mm(act, w_down[i], "nn", F32, f"mm_down{i}"), got

    u0, act0, ffn0, (g_kv, g_q, g_o) = ffn_fwd(fn0, 0, gather_of(w_kv, b_w_q[0], b_w_o[0]))
    w_kvf, w_q, w_o = g_kv.reshape(d, 2 * D_KV), g_q.reshape(d, d), g_o.reshape(d, d)
    h2, (kvn, bn) = _resid_norm(h1, ffn0, fpost[0:1], [wts["kv_norm"].reshape(1, d), wts["b_norm_pre"]], "resid_f0")
    kv = _mm(kvn, w_kvf, "nn", F32, "mm_kv")
    q = _mm(bn, w_q, "nn", F32, "mm_q")
    o, (g_up1, g_dn1) = _attn_fwd(q, kv, wts["b_sinks"], "attn_fwd", comm=gather_of(f_w_up[1], f_w_down[1]))
    w_up[1], w_down[1] = _assemble_cols(g_up1, 2 * D_FF, segs_up, "asm_w_up1"), g_dn1.reshape(D_FF, d)
    mix_b = _mm(o, w_o, "nn", F32, "mm_o")
    h3, (fn1,) = _resid_norm(h2, mix_b, wts["b_norm_post"], [fpre[1:2]], "resid_b")
    u1, act1, ffn1, _ = ffn_fwd(fn1, 1)
    dh4, loss_row = _final_loss(h3, ffn1, fpost[1:2], tgt, "loss")
    loss = lax.psum(loss_row[0, 0], ("x", "y", "c"))

    grads = {}

    def ffn_bwd(dh_out, h_in, fn, u, act, ffn, i, carry_down):
        dffn, dw_post = _norm_bwd(ffn, fpost[i:i + 1], dh_out, None, BF16, f"nb_fpost{i}")
        dact = _mm(dffn, w_down[i], "nt", F32, f"mm_dact{i}")
        dw_down = _mm(act, dffn, "tn", BF16, f"mm_dwdown{i}").reshape(N_DEV, -1, d)
        res = _ffn_act_bwd(u, dact, f_cw[i], f_cb[i:i + 1], f"ffn_act_bwd{i}",
                           comm=_scatter_comm([dw_down]) if carry_down else None)
        (dg, dv, dwg, dwv, dbg, dbv), dw_down = (res[0], res[1][0]) if carry_down else (res, dw_down)
        du = jnp.concatenate([dg, dv], axis=1)
        dfn = _mm(du, w_up[i], "nt", F32, f"mm_dfn{i}")
        dw_up = _scatter_cols(_mm(fn, du, "tn", BF16, f"mm_dwup{i}"), ws_up, segs_up, f"scat_w_up{i}")
        dh_in, dw_pre = _norm_bwd(h_in, fpre[i:i + 1], dfn, dh_out, F32, f"nb_fpre{i}")
        return dh_in, dict(post=dw_post, down=dw_down, cw=jnp.concatenate([dwg, dwv], axis=1),
                           cb=jnp.concatenate([dbg, dbv], axis=1), up=dw_up, pre=dw_pre)

    dh3, gf1 = ffn_bwd(dh4, h3, fn1, u1, act1, ffn1, 1, True)
    p_dn1 = gf1["down"]
    dmix_b, grads["b_norm_post"] = _norm_bwd(mix_b, wts["b_norm_post"], dh3, None, BF16, "nb_bpost")
    do = _mm(dmix_b, w_o, "nt", F32, "mm_do")
    dw_o = _mm(o, dmix_b, "tn", BF16, "mm_dwo").reshape(N_DEV, -1, d)
    (dq, dkv, dsinks), (p_up1, p_o) = _attn_bwd(q, kv, wts["b_sinks"], do, "attn_bwd",
                                                comm=_scatter_comm([gf1["up"], dw_o]))
    grads["b_sinks"] = dsinks[:, :N_Q_HEADS]
    dbn = _mm(dq, w_q, "nt", F32, "mm_dbn")
    dw_q = _mm(bn, dq, "tn", BF16, "mm_dwq").reshape(N_DEV, -1, d)
    dkv16 = dkv.astype(BF16)
    dkvn = _mm(dkv16, w_kvf, "nt", F32, "mm_dkvn")
    dw_kv = _mm(kvn, dkv16, "tn", BF16, "mm_dwkv").reshape(N_DEV, -1, 2 * D_KV)
    dh2, grads["b_norm_pre"] = _norm_bwd(h2, wts["b_norm_pre"], dbn, dh3, F32, "nb_bpre")
    dh2, dw_kvn = _norm_bwd(h2, wts["kv_norm"].reshape(1, d), dkvn, dh2, F32, "nb_kv")
    grads["kv_norm"] = dw_kvn.reshape(d)
    dh1, gf0 = ffn_bwd(dh2, h1, fn0, u0, act0, ffn0, 0, False)
    grads["f_norm_post"] = jnp.concatenate([gf0["post"], gf1["post"]], axis=0)
    grads["f_norm_pre"] = jnp.concatenate([gf0["pre"], gf1["pre"]], axis=0)
    grads["f_conv_w"] = jnp.stack([gf0["cw"], gf1["cw"]])
    grads["f_conv_b"] = jnp.concatenate([gf0["cb"], gf1["cb"]], axis=0)

    dmix_a, grads["a_norm_post"] = _norm_bwd(mix_a, full["a_norm_post"], dh1, None, BF16, "nb_apost")
    dyn = _mm(dmix_a, w_out, "nt", F32, "mm_dyn")
    dw_out = _mm(yn, dmix_a, "tn", BF16, "mm_dwout").reshape(N_DEV, -1, d)
    dy_ssd, dz, grads["a_gate_norm"] = _gatenorm_bwd(y_ssd, zx, full["a_gate_norm"], dyn, "gatenorm_bwd")
    (dxs, dbm, dcm, ddtp, dalog, ddsk, dbias), (p_q, p_kv, p_dn0, p_up0, p_out) = _ssd_bwd(
        xbc, zx, bias_g, alog_g, dsk_g, dy_ssd, hst, "ssd_bwd",
        comm=_scatter_comm([dw_q, dw_kv, gf0["down"], gf0["up"], dw_out]))
    dxbc = jnp.concatenate([dxs, dbm, dcm], axis=1)
    grads["a_a_log"] = _ungroup_lanes(dalog, hg)
    grads["a_d_skip"] = _ungroup_lanes(ddsk, hg)
    grads["a_dt_bias"] = _ungroup_lanes(dbias, hg)
    dpre, dcw, dcb = _conv_silu_bwd(zx, dxbc, a_conv_w, a_conv_b, "conv_a_bwd")
    grads["a_conv_w"], grads["a_conv_b"] = dcw[None], dcb
    dzx = jnp.concatenate([dz, dpre, ddtp.astype(BF16)], axis=1)
    dhn0 = _mm(dzx, w_in_all, "nt", F32, "mm_dhn0")
    dw_in_all = _mm(hn0, dzx, "tn", BF16, "mm_dwin")
    dh0, grads["a_norm_pre"] = _norm_bwd(h0, full["a_norm_pre"], dhn0, dh1, F32, "nb_apre")
    grad_x = dh0[N_META:N_META + SEQ][None]
    grads["meta_tokens"] = dh0[:N_META]

    small_local = _pack([_split8(grads[n], SHARD_AXIS[n], wts[n].shape[SHARD_AXIS[n]]) for n in SMALL], 1, 8, F32)
    repl_local = _pack([grads[n] for n in REPL], 0, 8, F32)
    n_sr = small_local.shape[1]
    p_in, = _run_comm(_scatter_comm([_scatter_cols(dw_in_all, ws_in, segs_in, "scat_w_in")]), "rs_w_in")
    small_all, = _run_comm(
        _gather_comm([jnp.concatenate([small_local.reshape(N_DEV * n_sr, LANE), repl_local], axis=0)]),
        "gather_small_grads")
    parts_big = dict(a_w_in=p_in, a_w_out=p_out, w_kv=p_kv, b_w_q=p_q, b_w_o=p_o,
                     f_w_up=jnp.concatenate([p_up0, p_up1], axis=1), f_w_down=jnp.concatenate([p_dn0, p_dn1], axis=1))
    mine_small = lax.dynamic_slice_in_dim(small_all, me * n_sr, n_sr, axis=1)
    parts_small = jnp.concatenate([mine_small, small_all[:, N_DEV * n_sr:]], axis=1)

    def flat_f32(dct, names, mult):
        return _pack([dct[n] for n in names], 0, mult, F32)

    big_out = {}
    for n in BIG:
        cols = wts[n].shape[-1]
        res = _adamw(parts_big[n], *[dct[n].reshape(-1, cols) for dct in (wts, mom, var)], f"adamw_{n}")
        big_out[n] = [r.reshape(wts[n].shape) for r in res]
    sm_in = [jnp.concatenate([flat_f32(dct, SMALL, 8), flat_f32(dct, REPL, 8)], axis=0) for dct in (wts, mom, var)]
    small_out = _adamw(parts_small, *sm_in, "adamw_small")

    outs = []
    for kind in range(4):
        res = {n: big_out[n][kind] for n in BIG}
        for n, a in zip(SMALL, _unpack(small_out[kind][:n_sr], 0, [wts[n].shape for n in SMALL])):
            res[n] = a
        for n, a in zip(REPL, _unpack(small_out[kind][n_sr:], 0, [wts[n].shape for n in REPL])):
            res[n] = a
        outs.append(res)
    return (loss, grad_x, *[outs[0][n] for n in WEIGHTS], *[outs[1][n] for n in WEIGHTS],
            *[outs[2][n] for n in WEIGHTS], *[outs[3][n] for n in WEIGHTS])
```

```python
import functools
import math

import jax
import jax.numpy as jnp
from jax import lax
from jax.experimental import pallas as pl
from jax.experimental.pallas import tpu as pltpu

F32, BF16 = jnp.float32, jnp.bfloat16
S = jax.ShapeDtypeStruct

D_MODEL = 1024
SEQ = 2048
N_META = 16
D_INNER = 2048
HEAD_P = 64
SSM_HEADS = D_INNER // HEAD_P
SSM_GROUPS = 4
D_STATE = 128
SSM_CONV = 4
D_BC = SSM_GROUPS * D_STATE
D_XBC = D_INNER + 2 * D_BC
ATTN_DH = 64
N_Q_HEADS = D_MODEL // ATTN_DH
N_KV_HEADS = 4
D_KV = N_KV_HEADS * ATTN_DH
WINDOW = 128
D_FF = 2816
FFN_CONV = 3
RMS_EPS = 1e-6
NEG = -1e30
LR, B1, B2, EPS, WD, STEP = 0.001, 0.9, 0.999, 1e-08, 0.01, 10

N_DEV = 8
T = 128
LANE = 128
VMEM_LIMIT = 48 * 1024 * 1024

BIG = ("a_w_in", "a_w_out", "w_kv", "b_w_q", "b_w_o", "f_w_up", "f_w_down")
SMALL = ("meta_tokens", "a_norm_pre", "a_conv_w", "a_conv_b", "a_gate_norm", "a_norm_post", "f_conv_w")
REPL = ("a_dt_bias", "a_a_log", "a_d_skip", "kv_norm", "b_norm_pre", "b_sinks", "b_norm_post",
        "f_norm_pre", "f_conv_b", "f_norm_post")
SHARD_AXIS = dict(a_w_in=2, a_w_out=1, w_kv=0, b_w_q=1, b_w_o=1, f_w_up=2, f_w_down=1, meta_tokens=1,
                  a_norm_pre=1, a_conv_w=2, a_conv_b=1, a_gate_norm=1, a_norm_post=1, f_conv_w=2)
WEIGHTS = ("meta_tokens", "a_norm_pre", "a_w_in", "a_conv_w", "a_conv_b", "a_dt_bias", "a_a_log", "a_d_skip",
           "a_gate_norm", "a_w_out", "a_norm_post", "kv_norm", "w_kv", "b_norm_pre", "b_w_q", "b_sinks", "b_w_o",
           "b_norm_post", "f_norm_pre", "f_w_up", "f_conv_w", "f_conv_b", "f_w_down", "f_norm_post")


def _seq_rows():
    return -(-(N_META + SEQ) // T) * T


def _cp(sem=None):
    return pltpu.CompilerParams(dimension_semantics=sem, vmem_limit_bytes=VMEM_LIMIT)


def _pick(n, target):
    t = min(n, target)
    t -= t % LANE
    while n % t:
        t -= LANE
    return t


def _sigmoid(x):
    return 1.0 / (1.0 + jnp.exp(-x))


def _softplus(x):
    return jnp.maximum(x, 0.0) + jnp.log(1.0 + jnp.exp(-jnp.abs(x)))


_NN = (((1,), (0,)), ((), ()))
_NT = (((1,), (1,)), ((), ()))
_TN = (((0,), (0,)), ((), ()))


def _dot(a, b, dims=_NN):
    return lax.dot_general(a, b, dims, preferred_element_type=F32)


def _dot_hi(a, b):
    return lax.dot_general(a, b, _NN, precision=lax.Precision.HIGHEST, preferred_element_type=F32)


_HBM = pl.BlockSpec(memory_space=pltpu.HBM)
_MESH = pl.DeviceIdType.MESH


class _Comm:
    def __init__(self, ins, out_shapes, scratch, first, last):
        self.ins, self.out_shapes, self.scratch, self.first, self.last = ins, out_shapes, scratch, first, last


def _call(body, name, out_shape, grid, in_specs, out_specs, sem, args, scratch=(), comm=None):
    if comm is None:
        return pl.pallas_call(body, name=name, out_shape=out_shape, grid=grid, in_specs=in_specs, out_specs=out_specs,
                              scratch_shapes=list(scratch), compiler_params=_cp(sem))(*args)
    single = not isinstance(out_shape, (list, tuple))
    outs = [out_shape] if single else list(out_shape)
    ospecs = [out_specs] if single else list(out_specs)
    n_in, n_out, n_scr, ci, co = len(in_specs), len(outs), len(scratch), len(comm.ins), len(comm.out_shapes)

    def carrier(*refs):
        p = 0
        parts = []
        for cnt in (n_in, ci, n_out, co, n_scr, len(comm.scratch)):
            parts.append(refs[p:p + cnt])
            p += cnt
        ins, cins, outs_r, couts, scr, cscr = parts
        ids = [pl.program_id(i) for i in range(len(grid))]
        first, last = ids[0] == 0, ids[0] == grid[0] - 1
        for i in range(1, len(grid)):
            first, last = first & (ids[i] == 0), last & (ids[i] == grid[i] - 1)

        @pl.when(first)
        def _():
            comm.first(cins, couts, cscr)

        body(*ins, *outs_r, *scr)

        @pl.when(last)
        def _():
            comm.last(cins, couts, cscr)

    res = pl.pallas_call(
        carrier, name=name, out_shape=outs + list(comm.out_shapes), grid=grid,
        in_specs=list(in_specs) + [_HBM] * ci, out_specs=ospecs + [_HBM] * co,
        scratch_shapes=list(scratch) + list(comm.scratch),
        compiler_params=_cp(("arbitrary",) * len(grid)))(*args, *comm.ins)
    mine = res[0] if single else list(res[:n_out])
    return mine, list(res[n_out:])


def _mm(a, b, mode, out_dtype, name, comm=None):
    if mode == "tn":
        m, kk = a.shape
        n = b.shape[1]
        tko, tn = _pick(kk, 512), _pick(n, 512)

        def body(a_ref, b_ref, o_ref):
            o_ref[...] = _dot(a_ref[...], b_ref[...], _TN).astype(o_ref.dtype)

        return _call(
            body, name, S((kk, n), out_dtype), (kk // tko, n // tn),
            [pl.BlockSpec((m, tko), lambda i, j: (0, i)), pl.BlockSpec((m, tn), lambda i, j: (0, j))],
            pl.BlockSpec((tko, tn), lambda i, j: (i, j)), ("parallel", "parallel"), (a, b), comm=comm)

    m, kk = a.shape
    n = b.shape[1] if mode == "nn" else b.shape[0]
    tn = _pick(n, 512)
    tk = kk if kk <= 2048 else _pick(kk, 1536)
    nk = kk // tk
    dims = _NN if mode == "nn" else _NT

    def body(a_ref, b_ref, o_ref, *acc):
        part = _dot(a_ref[...], b_ref[...], dims)
        if nk == 1:
            o_ref[...] = part.astype(o_ref.dtype)
        else:
            k = pl.program_id(1)

            @pl.when(k == 0)
            def _():
                acc[0][...] = part

            @pl.when(k > 0)
            def _():
                acc[0][...] += part

            @pl.when(k == nk - 1)
            def _():
                o_ref[...] = acc[0][...].astype(o_ref.dtype)

    b_spec = (pl.BlockSpec((tk, tn), lambda j, k: (k, j)) if mode == "nn"
              else pl.BlockSpec((tn, tk), lambda j, k: (j, k)))
    return _call(
        body, name, S((m, n), out_dtype), (n // tn, nk), [pl.BlockSpec((m, tk), lambda j, k: (0, k)), b_spec],
        pl.BlockSpec((m, tn), lambda j, k: (0, j)), ("parallel", "arbitrary"), (a, b),
        scratch=[pltpu.VMEM((m, tn), F32)] if nk > 1 else [], comm=comm)


def _rms(x, w):
    return x * lax.rsqrt(jnp.mean(x * x, axis=-1, keepdims=True) + RMS_EPS) * w


def _row_tile(rows):
    return rows // 8


def _resid_norm(h, br, w_post, next_ws, name):
    rows, d = h.shape
    tr = _row_tile(rows)
    has_br = br is not None
    nw = len(next_ws)

    def body(*refs):
        h_ref = refs[0]
        pos = 1
        x = h_ref[...]
        if has_br:
            x = x + _rms(refs[1][...], refs[2][...])
            pos = 3
        w_refs = refs[pos:pos + nw]
        outs = refs[pos + nw:]
        if has_br:
            outs[0][...] = x
            outs = outs[1:]
        for w_ref, o_ref in zip(w_refs, outs):
            o_ref[...] = _rms(x, w_ref[...]).astype(o_ref.dtype)

    row = pl.BlockSpec((tr, d), lambda i: (i, 0))
    vec = pl.BlockSpec((1, d), lambda i: (0, 0))
    ins = [h] + ([br, w_post] if has_br else []) + list(next_ws)
    in_specs = [row] + ([row, vec] if has_br else []) + [vec] * nw
    out_shape = ([S((rows, d), F32)] if has_br else []) + [S((rows, d), BF16)] * nw
    res = pl.pallas_call(body, name=name, out_shape=out_shape, grid=(rows // tr,), in_specs=in_specs,
                         out_specs=[row] * len(out_shape), compiler_params=_cp(("parallel",)))(*ins)
    if has_br:
        return res[0], list(res[1:])
    return h, list(res)


def _norm_bwd(x, w, dy, add, out_dtype, name):
    rows, d = x.shape
    tr = _row_tile(rows)
    has_add = add is not None

    def body(*refs):
        x_ref, w_ref, dy_ref = refs[:3]
        dx_ref, dw_ref = refs[-2:]
        xv = x_ref[...]
        r = lax.rsqrt(jnp.mean(xv * xv, axis=-1, keepdims=True) + RMS_EPS)
        dyv = dy_ref[...].astype(F32)
        wdy = dyv * w_ref[...]
        dx = r * wdy - xv * (r * r * r) * jnp.mean(xv * wdy, axis=-1, keepdims=True)
        if has_add:
            dx = dx + refs[3][...]
        dx_ref[...] = dx.astype(dx_ref.dtype)

        @pl.when(pl.program_id(0) == 0)
        def _():
            dw_ref[...] = jnp.zeros_like(dw_ref)

        dw_ref[...] += jnp.sum(dyv * xv * r, axis=0, keepdims=True)

    row = pl.BlockSpec((tr, d), lambda i: (i, 0))
    vec = pl.BlockSpec((1, d), lambda i: (0, 0))
    ins = [x, w, dy] + ([add] if has_add else [])
    return pl.pallas_call(body, name=name, out_shape=[S((rows, d), out_dtype), S((1, d), F32)], grid=(rows // tr,),
                          in_specs=[row, vec, row] + ([row] if has_add else []), out_specs=[row, vec],
                          compiler_params=_cp(("arbitrary",)))(*ins)


def _final_loss(h, br, w_post, tgt, name):
    rows, d = h.shape
    tr = _row_tile(rows)

    def body(h_ref, br_ref, w_ref, t_ref, dh_ref, loss_ref):
        i = pl.program_id(0)
        y = h_ref[...] + _rms(br_ref[...], w_ref[...])
        r = i * tr + lax.broadcasted_iota(jnp.int32, (tr, 1), 0)
        real = (r >= N_META) & (r < N_META + SEQ)
        diff = jnp.where(real, y - t_ref[...], 0.0)
        dh_ref[...] = diff * (1.0 / d)

        @pl.when(i == 0)
        def _():
            loss_ref[...] = jnp.zeros_like(loss_ref)

        loss_ref[...] += jnp.sum(diff * diff) * (0.5 / d)

    row = pl.BlockSpec((tr, d), lambda i: (i, 0))
    return pl.pallas_call(body, name=name, out_shape=[S((rows, d), F32), S((1, LANE), F32)], grid=(rows // tr,),
                          in_specs=[row, row, pl.BlockSpec((1, d), lambda i: (0, 0)), row],
                          out_specs=[row, pl.BlockSpec((1, LANE), lambda i: (0, 0))],
                          compiler_params=_cp(("arbitrary",)))(h, br, w_post, tgt)


def _gatenorm_fwd(y, zx, w, name):
    rows, d = y.shape
    tr = _row_tile(rows)

    def body(y_ref, z_ref, w_ref, o_ref):
        z = z_ref[...]
        o_ref[...] = _rms(y_ref[...] * z * _sigmoid(z), w_ref[...]).astype(o_ref.dtype)

    row = pl.BlockSpec((tr, d), lambda i: (i, 0))
    return pl.pallas_call(body, name=name, out_shape=S((rows, d), BF16), grid=(rows // tr,),
                          in_specs=[row, row, pl.BlockSpec((1, d), lambda i: (0, 0))], out_specs=row,
                          compiler_params=_cp(("parallel",)))(y, zx, w)


def _gatenorm_bwd(y, zx, w, dyn, name):
    rows, d = y.shape
    tr = _row_tile(rows)

    def body(y_ref, z_ref, w_ref, dyn_ref, dy_ref, dz_ref, dw_ref):
        yv, z = y_ref[...], z_ref[...]
        sg = _sigmoid(z)
        sz = z * sg
        g = yv * sz
        r = lax.rsqrt(jnp.mean(g * g, axis=-1, keepdims=True) + RMS_EPS)
        dyn_v = dyn_ref[...]
        wdy = dyn_v * w_ref[...]
        dg = r * wdy - g * (r * r * r) * jnp.mean(g * wdy, axis=-1, keepdims=True)
        dy_ref[...] = dg * sz
        dz_ref[...] = (dg * yv * sg * (1.0 + z * (1.0 - sg))).astype(dz_ref.dtype)

        @pl.when(pl.program_id(0) == 0)
        def _():
            dw_ref[...] = jnp.zeros_like(dw_ref)

        dw_ref[...] += jnp.sum(dyn_v * g * r, axis=0, keepdims=True)

    row = pl.BlockSpec((tr, d), lambda i: (i, 0))
    vec = pl.BlockSpec((1, d), lambda i: (0, 0))
    return pl.pallas_call(body, name=name, out_shape=[S((rows, d), F32), S((rows, d), BF16), S((1, d), F32)],
                          grid=(rows // tr,), in_specs=[row, row, vec, row], out_specs=[row, row, vec],
                          compiler_params=_cp(("arbitrary",)))(y, zx, w, dyn)


def _shift_down(x, s, rows_iota):
    if s == 0:
        return x
    return jnp.where(rows_iota >= s, pltpu.roll(x, s, 0), 0.0)


def _shift_up(x, s, rows_iota):
    if s == 0:
        return x
    rows = x.shape[0]
    return jnp.where(rows_iota < rows - s, pltpu.roll(x, rows - s, 0), 0.0)


def _r16(v):
    return v.astype(BF16).astype(F32)


def _conv(x, w_ref, b_ref, taps, rows_iota):
    x = _r16(x)
    acc = jnp.zeros_like(x)
    for k in range(taps):
        acc = acc + _r16(w_ref[k:k + 1, :]) * _shift_down(x, taps - 1 - k, rows_iota)
    return acc + b_ref[...]


def _conv_bwd(x, du, w_ref, dw_ref, db_ref, taps, rows_iota):
    db_ref[...] = jnp.sum(du, axis=0, keepdims=True)
    x, du = _r16(x), _r16(du)
    dx = jnp.zeros_like(x)
    for k in range(taps):
        s = taps - 1 - k
        dx = dx + _r16(w_ref[k:k + 1, :]) * _shift_up(du, s, rows_iota)
        dw_ref[k:k + 1, :] = jnp.sum(du * _shift_down(x, s, rows_iota), axis=0, keepdims=True)
    return dx


def _conv_silu_fwd(zx, w, b, name):
    rows = zx.shape[0]
    cb = 512
    off = D_INNER // cb

    def body(x_ref, w_ref, b_ref, o_ref):
        it = lax.broadcasted_iota(jnp.int32, (rows, 1), 0)
        u = _conv(x_ref[...], w_ref, b_ref, SSM_CONV, it)
        o_ref[...] = u * _sigmoid(u)

    return pl.pallas_call(
        body, name=name, out_shape=S((rows, D_XBC), F32), grid=(D_XBC // cb,),
        in_specs=[pl.BlockSpec((rows, cb), lambda j: (0, off + j)), pl.BlockSpec((SSM_CONV, cb), lambda j: (0, j)),
                  pl.BlockSpec((1, cb), lambda j: (0, j))],
        out_specs=pl.BlockSpec((rows, cb), lambda j: (0, j)), compiler_params=_cp(("parallel",)))(zx, w, b)


def _conv_silu_bwd(zx, dxbc, w, b, name):
    rows = zx.shape[0]
    cb = 512
    off = D_INNER // cb

    def body(x_ref, d_ref, w_ref, b_ref, dx_ref, dw_ref, db_ref):
        it = lax.broadcasted_iota(jnp.int32, (rows, 1), 0)
        x = x_ref[...]
        u = _conv(x, w_ref, b_ref, SSM_CONV, it)
        sg = _sigmoid(u)
        du = d_ref[...] * sg * (1.0 + u * (1.0 - sg))
        dx_ref[...] = _conv_bwd(x, du, w_ref, dw_ref, db_ref, SSM_CONV, it).astype(dx_ref.dtype)

    col = pl.BlockSpec((rows, cb), lambda j: (0, j))
    wsp = pl.BlockSpec((SSM_CONV, cb), lambda j: (0, j))
    bsp = pl.BlockSpec((1, cb), lambda j: (0, j))
    return pl.pallas_call(
        body, name=name, out_shape=[S((rows, D_XBC), BF16), S((SSM_CONV, D_XBC), F32), S((1, D_XBC), F32)],
        grid=(D_XBC // cb,), in_specs=[pl.BlockSpec((rows, cb), lambda j: (0, off + j)), col, wsp, bsp],
        out_specs=[col, wsp, bsp], compiler_params=_cp(("parallel",)))(zx, dxbc, w, b)


def _ffn_act_fwd(u, w, b, name):
    rows = u.shape[0]
    cb = 256
    nb = D_FF // cb

    def body(g_ref, v_ref, wg_ref, wv_ref, bg_ref, bv_ref, o_ref):
        it = lax.broadcasted_iota(jnp.int32, (rows, 1), 0)
        g = _conv(g_ref[...], wg_ref, bg_ref, FFN_CONV, it)
        v = _conv(v_ref[...], wv_ref, bv_ref, FFN_CONV, it)
        o_ref[...] = (g * _sigmoid(g) * v).astype(o_ref.dtype)

    def sp(r, shift):
        return pl.BlockSpec((r, cb), lambda j: (0, shift + j))

    return pl.pallas_call(
        body, name=name, out_shape=S((rows, D_FF), BF16), grid=(nb,),
        in_specs=[sp(rows, 0), sp(rows, nb), sp(FFN_CONV, 0), sp(FFN_CONV, nb), sp(1, 0), sp(1, nb)],
        out_specs=sp(rows, 0), compiler_params=_cp(("parallel",)))(u, u, w, w, b, b)


def _ffn_act_bwd(u, dact, w, b, name, comm=None):
    rows = u.shape[0]
    cb = 256
    nb = D_FF // cb

    def body(g_ref, v_ref, d_ref, wg_ref, wv_ref, bg_ref, bv_ref, dg_ref, dv_ref, dwg_ref, dwv_ref, dbg_ref, dbv_ref):
        it = lax.broadcasted_iota(jnp.int32, (rows, 1), 0)
        xg, xv = g_ref[...], v_ref[...]
        g = _conv(xg, wg_ref, bg_ref, FFN_CONV, it)
        v = _conv(xv, wv_ref, bv_ref, FFN_CONV, it)
        sg = _sigmoid(g)
        d = d_ref[...]
        dgate = d * v * sg * (1.0 + g * (1.0 - sg))
        dval = d * g * sg
        dg_ref[...] = _conv_bwd(xg, dgate, wg_ref, dwg_ref, dbg_ref, FFN_CONV, it).astype(dg_ref.dtype)
        dv_ref[...] = _conv_bwd(xv, dval, wv_ref, dwv_ref, dbv_ref, FFN_CONV, it).astype(dv_ref.dtype)

    def sp(r, shift):
        return pl.BlockSpec((r, cb), lambda j: (0, shift + j))

    return _call(
        body, name,
        [S((rows, D_FF), BF16), S((rows, D_FF), BF16), S((FFN_CONV, D_FF), F32), S((FFN_CONV, D_FF), F32),
         S((1, D_FF), F32), S((1, D_FF), F32)],
        (nb,),
        [sp(rows, 0), sp(rows, nb), sp(rows, 0), sp(FFN_CONV, 0), sp(FFN_CONV, nb), sp(1, 0), sp(1, nb)],
        [sp(rows, 0), sp(rows, 0), sp(FFN_CONV, 0), sp(FFN_CONV, 0), sp(1, 0), sp(1, 0)],
        ("parallel",), (u, u, dact, w, w, b, b), comm=comm)


def _ssd_consts(dtp_ref, bias_ref, alog_ref, hg):
    lane = lax.broadcasted_iota(jnp.int32, (1, LANE), 1)
    pre = dtp_ref[...] + bias_ref[...]
    dt = _softplus(pre)
    a_row = jnp.where(lane < hg, -jnp.exp(alog_ref[...]), 0.0)
    ri = lax.broadcasted_iota(jnp.int32, (T, T), 0)
    ci = lax.broadcasted_iota(jnp.int32, (T, T), 1)
    cs = _dot_hi((ri >= ci).astype(F32), dt * a_row)
    return pre, dt, a_row, cs, ri, ci, lane


def _ssd_fwd(xbc, zx, bias, alog, dsk, name, comm=None):
    rows = xbc.shape[0]
    nc = rows // T
    hg = SSM_HEADS // SSM_GROUPS
    gw = hg * HEAD_P
    xoff, boff, coff = 0, D_INNER // D_STATE, (D_INNER + D_BC) // D_STATE
    dtoff = (D_INNER + D_XBC) // LANE

    def body(x_ref, b_ref, c_ref, dtp_ref, bias_ref, alog_ref, dsk_ref, y_ref, hst_ref, hs):
        c = pl.program_id(1)

        @pl.when(c == 0)
        def _():
            hs[...] = jnp.zeros_like(hs)

        _, dt, _, cs, ri, ci, _ = _ssd_consts(dtp_ref, bias_ref, alog_ref, hg)
        cst = cs.T
        x = x_ref[...]
        bb, cbf = b_ref[...].astype(BF16), c_ref[...].astype(BF16)
        gmat = _dot(cbf, bb, _NT)
        causal = ri >= ci
        dskv = dsk_ref[...]
        hst_ref[0, 0] = hs[...]
        for k in range(hg):
            csk, csr = cs[:, k:k + 1], cst[k:k + 1, :]
            lm = jnp.exp(jnp.where(causal, csk - csr, NEG))
            xk = x[:, k * HEAD_P:(k + 1) * HEAD_P]
            xdt = xk * dt[:, k:k + 1]
            hk = hs[k * HEAD_P:(k + 1) * HEAD_P, :]
            yd = _dot((gmat * lm).astype(BF16), xdt.astype(BF16))
            yo = jnp.exp(csk) * _dot(cbf, hk.astype(BF16), _NT)
            y_ref[:, k * HEAD_P:(k + 1) * HEAD_P] = yd + yo + dskv[:, k:k + 1] * xk
            cl = cs[T - 1:T, k:k + 1]
            st = _dot((xdt * jnp.exp(cl - csk)).astype(BF16), bb, _TN)
            hs[k * HEAD_P:(k + 1) * HEAD_P, :] = jnp.exp(cl) * hk + st

    vec = pl.BlockSpec((1, LANE), lambda g, c: (0, g))
    return _call(
        body, name, [S((rows, D_INNER), F32), S((nc, SSM_GROUPS, gw, D_STATE), F32)], (SSM_GROUPS, nc),
        [pl.BlockSpec((T, gw), lambda g, c: (c, xoff + g)),
         pl.BlockSpec((T, D_STATE), lambda g, c: (c, boff + g)),
         pl.BlockSpec((T, D_STATE), lambda g, c: (c, coff + g)),
         pl.BlockSpec((T, LANE), lambda g, c: (c, dtoff + g)), vec, vec, vec],
        [pl.BlockSpec((T, gw), lambda g, c: (c, g)), pl.BlockSpec((1, 1, gw, D_STATE), lambda g, c: (c, g, 0, 0))],
        ("parallel", "arbitrary"), (xbc, xbc, xbc, zx, bias, alog, dsk),
        scratch=[pltpu.VMEM((gw, D_STATE), F32)], comm=comm)


def _ssd_bwd(xbc, zx, bias, alog, dsk, dy, hst, name, comm=None):
    rows = xbc.shape[0]
    nc = rows // T
    hg = SSM_HEADS // SSM_GROUPS
    gw = hg * HEAD_P
    boff, coff = D_INNER // D_STATE, (D_INNER + D_BC) // D_STATE
    dtoff = (D_INNER + D_XBC) // LANE

    def body(x_ref, b_ref, c_ref, dtp_ref, bias_ref, alog_ref, dsk_ref, dy_ref, hst_ref,
             dx_ref, db_ref, dc_ref, ddtp_ref, dalog_ref, ddsk_ref, dbias_ref, dhs):
        step = pl.program_id(1)

        @pl.when(step == 0)
        def _():
            dhs[...] = jnp.zeros_like(dhs)
            dalog_ref[...] = jnp.zeros_like(dalog_ref)
            ddsk_ref[...] = jnp.zeros_like(ddsk_ref)
            dbias_ref[...] = jnp.zeros_like(dbias_ref)

        pre, dt, a_row, cs, ri, ci, lane = _ssd_consts(dtp_ref, bias_ref, alog_ref, hg)
        cst = cs.T
        x, dyv = x_ref[...], dy_ref[...]
        bb, cbf = b_ref[...].astype(BF16), c_ref[...].astype(BF16)
        gt = _dot(bb, cbf, _NT)
        causal_t = ci >= ri
        dskv = dsk_ref[...]
        last = lax.broadcasted_iota(jnp.int32, (T, 1), 0) == T - 1
        dgt = jnp.zeros((T, T), F32)
        dc_acc = jnp.zeros((T, D_STATE), F32)
        db_acc = jnp.zeros((T, D_STATE), F32)
        ddt_acc = jnp.zeros((T, LANE), F32)
        dcs_acc = jnp.zeros((T, LANE), F32)
        dcs_row = jnp.zeros((T, LANE), F32)
        ddsk_acc = jnp.zeros((1, LANE), F32)
        head_row = lax.broadcasted_iota(jnp.int32, (T, 1), 0)
        for k in range(hg):
            sl = slice(k * HEAD_P, (k + 1) * HEAD_P)
            csk, csr = cs[:, k:k + 1], cst[k:k + 1, :]
            lt = jnp.exp(jnp.where(causal_t, csr - csk, NEG))
            xk, dyk = x[:, sl], dyv[:, sl]
            dtk, dk = dt[:, k:k + 1], dskv[:, k:k + 1]
            dyb = dyk.astype(BF16)
            mpt = gt * lt
            z = _dot(mpt.astype(BF16), dyb)
            dmt = _dot((xk * dtk).astype(BF16), dyb, _NT)
            dgt = dgt + dmt * lt
            q = dmt * mpt
            xz = jnp.sum(xk * z, axis=1, keepdims=True)
            q_rows = jnp.sum(q, axis=1, keepdims=True)
            dcs_row = dcs_row + jnp.where(head_row == k, jnp.sum(q, axis=0, keepdims=True), 0.0)
            hk = hst_ref[0, 0, sl, :]
            dhn = dhs[sl, :]
            e = jnp.exp(csk)
            cl = cs[T - 1:T, k:k + 1]
            wdec = jnp.exp(cl - csk)
            w = wdec * dtk
            r = _dot(bb, dhn.astype(BF16), _NT)
            dx_ref[:, sl] = dtk * z + dk * dyk + r * w
            dw = jnp.sum(r * xk, axis=1, keepdims=True)
            dcl = jnp.exp(cl) * jnp.sum(dhn * hk) + jnp.sum(dw * w)
            yo = e * _dot(cbf, hk.astype(BF16), _NT)
            dcs_k = jnp.sum(dyk * yo, axis=1, keepdims=True) - q_rows - dw * w + jnp.where(last, dcl, 0.0)
            dye = (dyk * e).astype(BF16)
            dc_acc = dc_acc + _dot(dye, hk.astype(BF16))
            db_acc = db_acc + _dot((xk * w).astype(BF16), dhn.astype(BF16))
            dhs[sl, :] = jnp.exp(cl) * dhn + _dot(dye, cbf, _TN)
            onehot = (lane == k).astype(F32)
            ddt_acc = ddt_acc + (xz + dw * wdec) * onehot
            dcs_acc = dcs_acc + dcs_k * onehot
            ddsk_acc = ddsk_acc + jnp.sum(dyk * xk) * onehot
        dc_ref[...] = _dot(dgt.T.astype(BF16), bb) + dc_acc
        db_ref[...] = _dot(dgt.astype(BF16), cbf) + db_acc
        da = _dot_hi((ci >= ri).astype(F32), dcs_acc + dcs_row.T)
        ddtp = (ddt_acc + da * a_row) * _sigmoid(pre)
        ddtp = jnp.where(lane < hg, ddtp, 0.0)
        ddtp_ref[...] = ddtp
        dbias_ref[...] += jnp.sum(ddtp, axis=0, keepdims=True)
        dalog_ref[...] += jnp.sum(da * dt, axis=0, keepdims=True) * a_row
        ddsk_ref[...] += ddsk_acc

    def rc(c):
        return nc - 1 - c

    vec = pl.BlockSpec((1, LANE), lambda g, c: (0, g))
    xsp = pl.BlockSpec((T, gw), lambda g, c: (rc(c), g))
    return _call(
        body, name,
        [S((rows, D_INNER), F32), S((rows, D_BC), F32), S((rows, D_BC), F32),
         S((rows, SSM_GROUPS * LANE), F32), S((1, SSM_GROUPS * LANE), F32),
         S((1, SSM_GROUPS * LANE), F32), S((1, SSM_GROUPS * LANE), F32)],
        (SSM_GROUPS, nc),
        [xsp,
         pl.BlockSpec((T, D_STATE), lambda g, c: (rc(c), boff + g)),
         pl.BlockSpec((T, D_STATE), lambda g, c: (rc(c), coff + g)),
         pl.BlockSpec((T, LANE), lambda g, c: (rc(c), dtoff + g)), vec, vec, vec,
         xsp, pl.BlockSpec((1, 1, gw, D_STATE), lambda g, c: (rc(c), g, 0, 0))],
        [xsp,
         pl.BlockSpec((T, D_STATE), lambda g, c: (rc(c), g)),
         pl.BlockSpec((T, D_STATE), lambda g, c: (rc(c), g)),
         pl.BlockSpec((T, LANE), lambda g, c: (rc(c), g)), vec, vec, vec],
        ("parallel", "arbitrary"), (xbc, xbc, xbc, zx, bias, alog, dsk, dy, hst),
        scratch=[pltpu.VMEM((gw, D_STATE), F32)], comm=comm)


def _attn_tiles(kv_ref, j):
    prev = jnp.maximum(j - 1, 0)
    meta = kv_ref[0:T, :]
    prv = kv_ref[pl.ds(pl.multiple_of(prev * T, T), T), :]
    cur = kv_ref[pl.ds(pl.multiple_of(j * T, T), T), :]
    return jnp.concatenate([meta, prv, cur], axis=0)


def _attn_mask(j):
    r = j * T + lax.broadcasted_iota(jnp.int32, (3 * T, T), 1)
    row = lax.broadcasted_iota(jnp.int32, (3 * T, T), 0)
    t0, t1 = row < T, row < 2 * T
    s = jnp.where(t0, row, (j - 2) * T + row)
    ok = (s <= r) & ((s < N_META) | (s > r - WINDOW))
    use = (t0 & (j >= 2) & (row < N_META)) | (jnp.logical_not(t0) & t1 & (j >= 1)) | jnp.logical_not(t1)
    return ok & use


def _attn_probs(qh, k3, mask, sink):
    sc = jnp.where(mask, _dot(k3, qh, _NT), NEG)
    m = jnp.maximum(jnp.max(sc, axis=0, keepdims=True), sink)
    p = jnp.exp(sc - m)
    es = jnp.exp(sink - m)
    inv = 1.0 / (jnp.sum(p, axis=0, keepdims=True) + es)
    return p * inv, es * inv


def _attn_fwd(q, kv, sinks, name, comm=None):
    rows = q.shape[0]
    scale = 1.0 / math.sqrt(ATTN_DH)
    qpk = N_Q_HEADS // N_KV_HEADS

    def body(q_ref, kv_ref, s_ref, o_ref):
        j = pl.program_id(0)
        kv3 = _attn_tiles(kv_ref, j).astype(BF16)
        mask = _attn_mask(j)
        qv = (q_ref[...] * scale).astype(BF16)
        sk = s_ref[...]
        for kh in range(N_KV_HEADS):
            k3 = kv3[:, kh * ATTN_DH:(kh + 1) * ATTN_DH]
            v3 = kv3[:, D_KV + kh * ATTN_DH:D_KV + (kh + 1) * ATTN_DH]
            for g in range(qpk):
                h = kh * qpk + g
                p, _ = _attn_probs(qv[:, h * ATTN_DH:(h + 1) * ATTN_DH], k3, mask, sk[:, h:h + 1])
                o_ref[:, h * ATTN_DH:(h + 1) * ATTN_DH] = _dot(p.astype(BF16), v3, _TN).astype(o_ref.dtype)

    return _call(
        body, name, S((rows, D_MODEL), BF16), (rows // T,),
        [pl.BlockSpec((T, D_MODEL), lambda j: (j, 0)), pl.BlockSpec((rows, 2 * D_KV), lambda j: (0, 0)),
         pl.BlockSpec((1, N_Q_HEADS), lambda j: (0, 0))],
        pl.BlockSpec((T, D_MODEL), lambda j: (j, 0)), ("parallel",), (q, kv, sinks), comm=comm)


def _attn_bwd(q, kv, sinks, do, name, comm=None):
    rows = q.shape[0]
    scale = 1.0 / math.sqrt(ATTN_DH)
    qpk = N_Q_HEADS // N_KV_HEADS

    def body(q_ref, kv_ref, s_ref, do_ref, dq_ref, dkv_ref, ds_ref):
        j = pl.program_id(0)

        @pl.when(j == 0)
        def _():
            dkv_ref[...] = jnp.zeros_like(dkv_ref)
            ds_ref[...] = jnp.zeros_like(ds_ref)

        kv3 = _attn_tiles(kv_ref, j).astype(BF16)
        mask = _attn_mask(j)
        qv = (q_ref[...] * scale).astype(BF16)
        dov = do_ref[...].astype(BF16)
        sk = s_ref[...]
        lane = lax.broadcasted_iota(jnp.int32, (1, LANE), 1)
        ds_acc = jnp.zeros((1, LANE), F32)
        prev = jnp.maximum(j - 1, 0)
        for kh in range(N_KV_HEADS):
            ksl = slice(kh * ATTN_DH, (kh + 1) * ATTN_DH)
            vsl = slice(D_KV + kh * ATTN_DH, D_KV + (kh + 1) * ATTN_DH)
            k3, v3 = kv3[:, ksl], kv3[:, vsl]
            dk3 = jnp.zeros((3 * T, ATTN_DH), F32)
            dv3 = jnp.zeros((3 * T, ATTN_DH), F32)
            for g in range(qpk):
                h = kh * qpk + g
                hs = slice(h * ATTN_DH, (h + 1) * ATTN_DH)
                qh, doh = qv[:, hs], dov[:, hs]
                p, ps = _attn_probs(qh, k3, mask, sk[:, h:h + 1])
                dp = _dot(v3, doh, _NT)
                delta = jnp.sum(p * dp, axis=0, keepdims=True)
                dsc = (p * (dp - delta)).astype(BF16)
                dq_ref[:, hs] = (_dot(dsc, k3, _TN) * scale).astype(dq_ref.dtype)
                dk3 = dk3 + _dot(dsc, qh)
                dv3 = dv3 + _dot(p.astype(BF16), doh)
                ds_acc = ds_acc - jnp.sum(ps * delta) * (lane == h).astype(F32)
            for t, start in enumerate((0, pl.multiple_of(prev * T, T), pl.multiple_of(j * T, T))):
                rsl = pl.ds(start, T)
                dkv_ref[rsl, ksl] += dk3[t * T:(t + 1) * T, :]
                dkv_ref[rsl, vsl] += dv3[t * T:(t + 1) * T, :]
        ds_ref[...] += ds_acc

    blk = pl.BlockSpec((T, D_MODEL), lambda j: (j, 0))
    full = pl.BlockSpec((rows, 2 * D_KV), lambda j: (0, 0))
    return _call(
        body, name, [S((rows, D_MODEL), BF16), S((rows, 2 * D_KV), F32), S((1, LANE), F32)], (rows // T,),
        [blk, full, pl.BlockSpec((1, N_Q_HEADS), lambda j: (0, 0)), blk],
        [blk, full, pl.BlockSpec((1, LANE), lambda j: (0, 0))], ("arbitrary",), (q, kv, sinks, do), comm=comm)


BLOCK_BYTES = 1 << 20


def _div_tile(rows, cols):
    cap = max(16, BLOCK_BYTES // (4 * cols))
    best = None
    for t in range(16, min(rows, cap) + 1, 16):
        if rows % t == 0:
            best = t
    return best if best is not None else rows


def _adamw(parts, w, m, v, name):
    n, rows, cols = parts.shape
    tr = _div_tile(rows, cols)
    c1 = 1.0 / (1.0 - B1 ** STEP)
    c2 = 1.0 / (1.0 - B2 ** STEP)

    def body(p_ref, w_ref, m_ref, v_ref, g_ref, d_ref, nm_ref, nv_ref):
        g = p_ref[0].astype(F32)
        for i in range(1, n):
            g = g + p_ref[i].astype(F32)
        nm = B1 * m_ref[...] + (1.0 - B1) * g
        nv = B2 * v_ref[...] + (1.0 - B2) * (g * g)
        g_ref[...] = g
        nm_ref[...] = nm
        nv_ref[...] = nv
        d_ref[...] = -LR * ((nm * c1) / (jnp.sqrt(nv * c2) + EPS) + WD * w_ref[...])

    row = pl.BlockSpec((tr, cols), lambda i: (i, 0))
    return pl.pallas_call(
        body, name=name, out_shape=[S((rows, cols), F32)] * 4, grid=(rows // tr,),
        in_specs=[pl.BlockSpec((n, tr, cols), lambda i: (0, i, 0)), row, row, row], out_specs=[row] * 4,
        compiler_params=_cp(("parallel",)))(parts, w, m, v)


def _col_segments(ws, runs):
    segs = []
    for glo, mlo, n in runs:
        while n > 0:
            d, off = divmod(glo, ws)
            take = min(n, ws - off)
            segs.append((d, off, mlo, take))
            glo, mlo, n = glo + take, mlo + take, n - take
    return segs


def _assemble_cols(g, width, segs, name):
    _, rows, ws = g.shape
    rb = _div_tile(rows, width // 2)

    def body(g_ref, o_ref):
        o_ref[...] = jnp.zeros_like(o_ref)
        for d, off, mlo, n in segs:
            o_ref[:, mlo:mlo + n] = g_ref[d, :, off:off + n]

    return pl.pallas_call(
        body, name=name, out_shape=S((rows, width), g.dtype), grid=(rows // rb,),
        in_specs=[pl.BlockSpec((N_DEV, rb, ws), lambda i: (0, i, 0))],
        out_specs=pl.BlockSpec((rb, width), lambda i: (i, 0)), compiler_params=_cp(("parallel",)))(g)


def _scatter_cols(dw, ws, segs, name):
    rows, width = dw.shape
    rb = _div_tile(rows, width)

    def body(w_ref, o_ref):
        for d, off, mlo, n in segs:
            o_ref[d, :, off:off + n] = w_ref[:, mlo:mlo + n].astype(o_ref.dtype)

    return pl.pallas_call(
        body, name=name, out_shape=S((N_DEV, rows, ws), BF16), grid=(rows // rb,),
        in_specs=[pl.BlockSpec((rb, width), lambda i: (i, 0))],
        out_specs=pl.BlockSpec((N_DEV, rb, ws), lambda i: (0, i, 0)), compiler_params=_cp(("parallel",)))(dw)


def _gather_comm(xs):
    n = len(xs)

    def setup(x_refs, out_refs, sems):
        send_sems, recv_sems, local_sems = sems
        mx, my, mc = lax.axis_index("x"), lax.axis_index("y"), lax.axis_index("c")
        me, sibling = (mx, my, mc), (mx, my, 1 - mc)
        chips = [(1 - mx, my), (mx, 1 - my), (1 - mx, 1 - my)]

        def blk(a, px, py, pc):
            return out_refs[a].at[4 * px + 2 * py + pc]

        def copy(a, k, block, to, src=None):
            return pltpu.make_async_remote_copy(
                src_ref=blk(a, *block) if src is None else src, dst_ref=blk(a, *block),
                send_sem=send_sems.at[a, k], recv_sem=recv_sems.at[a, k], device_id=to, device_id_type=_MESH)

        mine = [pltpu.make_async_copy(x_refs[a], blk(a, *me), local_sems.at[a]) for a in range(n)]
        own = []
        for a in range(n):
            own.append(copy(a, 0, me, sibling, src=x_refs[a]))
            own += [copy(a, 1 + i, me, (*chip, mc), src=x_refs[a]) for i, chip in enumerate(chips)]
        return me, sibling, chips, mc, copy, mine, own

    def first(x_refs, out_refs, sems):
        _, _, _, _, _, mine, own = setup(x_refs, out_refs, sems)
        for cp in mine + own:
            cp.start()

    def last(x_refs, out_refs, sems):
        me, sibling, chips, mc, copy, mine, own = setup(x_refs, out_refs, sems)
        passed = []
        for a in range(n):
            for i, chip in enumerate(chips):
                copy(a, 1 + i, (*chip, mc), me).wait_recv()
                passed.append(copy(a, 4 + i, (*chip, mc), sibling))
                passed[-1].start()
        for a in range(n):
            copy(a, 0, sibling, me).wait_recv()
            for i, chip in enumerate(chips):
                copy(a, 4 + i, (*chip, 1 - mc), me).wait_recv()
        for cp in own + passed:
            cp.wait_send()
        for cp in mine:
            cp.wait()

    return _Comm(list(xs), [S((N_DEV,) + x.shape, x.dtype) for x in xs],
                 [pltpu.SemaphoreType.DMA((n, 7)), pltpu.SemaphoreType.DMA((n, 7)), pltpu.SemaphoreType.DMA((n,))],
                 first, last)


def _scatter_comm(gs):
    n = len(gs)

    def copies(g_refs, out_refs, sems):
        send_sems, recv_sems, local_sems = sems
        mx, my, mc = lax.axis_index("x"), lax.axis_index("y"), lax.axis_index("c")
        me = 4 * mx + 2 * my + mc
        mine = [pltpu.make_async_copy(g_refs[a].at[me], out_refs[a].at[me], local_sems.at[a]) for a in range(n)]
        cps = []
        for a in range(n):
            for k in range(1, N_DEV):
                px, py, pc = mx ^ (k >> 2), my ^ ((k >> 1) & 1), mc ^ (k & 1)
                cps.append(pltpu.make_async_remote_copy(
                    src_ref=g_refs[a].at[4 * px + 2 * py + pc], dst_ref=out_refs[a].at[me],
                    send_sem=send_sems.at[a, k - 1], recv_sem=recv_sems.at[a, k - 1],
                    device_id=(px, py, pc), device_id_type=_MESH))
        return mine + cps

    def first(g_refs, out_refs, sems):
        for cp in copies(g_refs, out_refs, sems):
            cp.start()

    def last(g_refs, out_refs, sems):
        for cp in copies(g_refs, out_refs, sems):
            cp.wait()

    return _Comm(list(gs), [S(g.shape, g.dtype) for g in gs],
                 [pltpu.SemaphoreType.DMA((n, 7)), pltpu.SemaphoreType.DMA((n, 7)), pltpu.SemaphoreType.DMA((n,))],
                 first, last)


def _run_comm(comm, name):
    ci, co = len(comm.ins), len(comm.out_shapes)

    def body(*refs):
        comm.first(refs[:ci], refs[ci:ci + co], refs[ci + co:])
        comm.last(refs[:ci], refs[ci:ci + co], refs[ci + co:])

    return pl.pallas_call(body, name=name, out_shape=list(comm.out_shapes), in_specs=[_HBM] * ci,
                          out_specs=[_HBM] * co, scratch_shapes=list(comm.scratch))(*comm.ins)


def _flat_rows(n_elems, mult):
    rows = -(-n_elems // LANE)
    return -(-rows // mult) * mult


def _pack(arrs, lead, mult, dtype):
    lead_shape = arrs[0].shape[:lead]
    flat = jnp.concatenate([a.astype(dtype).reshape(lead_shape + (-1,)) for a in arrs], axis=-1)
    n = flat.shape[-1]
    rows = _flat_rows(n, mult)
    flat = jnp.pad(flat, [(0, 0)] * lead + [(0, rows * LANE - n)])
    return flat.reshape(lead_shape + (rows, LANE))


def _unpack(flat, lead, shapes):
    lead_shape = flat.shape[:lead]
    flat = flat.reshape(lead_shape + (-1,))
    out, off = [], 0
    for shp in shapes:
        n = math.prod(shp)
        out.append(flat[..., off:off + n].reshape(lead_shape + tuple(shp)))
        off += n
    return out


def _split8(full, ax, n):
    shp = full.shape
    return jnp.moveaxis(full.reshape(shp[:ax] + (N_DEV, n) + shp[ax + 1:]), ax, 0)


def _join8(g, ax):
    shp = g.shape[1:]
    return jnp.moveaxis(g, 0, ax).reshape(shp[:ax] + (N_DEV * shp[ax],) + shp[ax + 1:])


def _group_lanes(v, hg):
    v = v.reshape(SSM_GROUPS, hg)
    return jnp.pad(v, ((0, 0), (0, LANE - hg))).reshape(1, SSM_GROUPS * LANE)


def _ungroup_lanes(v, hg):
    return v.reshape(SSM_GROUPS, LANE)[:, :hg].reshape(1, SSM_GROUPS * hg)


def kernel(x, meta_tokens, a_norm_pre, a_w_in, a_conv_w, a_conv_b, a_dt_bias, a_a_log, a_d_skip, a_gate_norm, a_w_out, a_norm_post, kv_norm, w_kv, b_norm_pre, b_w_q, b_sinks, b_w_o, b_norm_post, f_norm_pre, f_w_up, f_conv_w, f_conv_b, f_w_down, f_norm_post, loss_target, m_meta_tokens, m_a_norm_pre, m_a_w_in, m_a_conv_w, m_a_conv_b, m_a_dt_bias, m_a_a_log, m_a_d_skip, m_a_gate_norm, m_a_w_out, m_a_norm_post, m_kv_norm, m_w_kv, m_b_norm_pre, m_b_w_q, m_b_sinks, m_b_w_o, m_b_norm_post, m_f_norm_pre, m_f_w_up, m_f_conv_w, m_f_conv_b, m_f_w_down, m_f_norm_post, v_meta_tokens, v_a_norm_pre, v_a_w_in, v_a_conv_w, v_a_conv_b, v_a_dt_bias, v_a_a_log, v_a_d_skip, v_a_gate_norm, v_a_w_out, v_a_norm_post, v_kv_norm, v_w_kv, v_b_norm_pre, v_b_w_q, v_b_sinks, v_b_w_o, v_b_norm_post, v_f_norm_pre, v_f_w_up, v_f_conv_w, v_f_conv_b, v_f_w_down, v_f_norm_post):
    args = locals()
    wts = {n: args[n] for n in WEIGHTS}
    mom = {n: args["m_" + n] for n in WEIGHTS}
    var = {n: args["v_" + n] for n in WEIGHTS}
    mx, my, mc = lax.axis_index("x"), lax.axis_index("y"), lax.axis_index("c")
    me = 4 * mx + 2 * my + mc
    rows = _seq_rows()
    hg = SSM_HEADS // SSM_GROUPS
    d = D_MODEL

    n_main = D_INNER + D_XBC
    ws_in, ws_up = a_w_in.shape[2], f_w_up.shape[2]
    segs_in = _col_segments(ws_in, [(0, 0, n_main)] + [(n_main + hg * g, n_main + LANE * g, hg)
                                                      for g in range(SSM_GROUPS)])
    segs_up = _col_segments(ws_up, [(0, 0, 2 * D_FF)])
    def gather_of(*ws):
        return _gather_comm([w.astype(BF16) for w in ws])

    g_in, small_full = _run_comm(_gather_comm([a_w_in[0].astype(BF16), _pack([wts[n] for n in SMALL], 0, 8, F32)]),
                                 "gather_first")
    full = {}
    for n, g in zip(SMALL, _unpack(small_full, 1, [wts[n].shape for n in SMALL])):
        full[n] = _join8(g, SHARD_AXIS[n])
    w_in_all = _assemble_cols(g_in, n_main + SSM_GROUPS * LANE, segs_in, "asm_w_in")
    w_up, w_down = [None, None], [None, None]
    bias_g = _group_lanes(wts["a_dt_bias"], hg)
    alog_g = _group_lanes(wts["a_a_log"], hg)
    dsk_g = _group_lanes(wts["a_d_skip"], hg)
    a_conv_w, a_conv_b = full["a_conv_w"][0], full["a_conv_b"]
    f_cw, f_cb = full["f_conv_w"], wts["f_conv_b"]
    fpre, fpost = wts["f_norm_pre"], wts["f_norm_post"]

    pad_rows = rows - N_META - SEQ
    h0 = jnp.concatenate([full["meta_tokens"], x[0], jnp.zeros((pad_rows, d), F32)], axis=0)
    tgt = jnp.pad(loss_target[0], ((N_META, pad_rows), (0, 0)))

    _, (hn0,) = _resid_norm(h0, None, None, [full["a_norm_pre"]], "norm_a_pre")
    zx, (g_out,) = _mm(hn0, w_in_all, "nn", F32, "mm_in", comm=gather_of(a_w_out[0]))
    w_out = g_out.reshape(D_INNER, d)
    xbc = _conv_silu_fwd(zx, a_conv_w, a_conv_b, "conv_a")
    (y_ssd, hst), (g_up0, g_dn0) = _ssd_fwd(xbc, zx, bias_g, alog_g, dsk_g, "ssd_fwd",
                                            comm=gather_of(f_w_up[0], f_w_down[0]))
    w_up[0], w_down[0] = _assemble_cols(g_up0, 2 * D_FF, segs_up, "asm_w_up0"), g_dn0.reshape(D_FF, d)
    yn = _gatenorm_fwd(y_ssd, zx, full["a_gate_norm"], "gatenorm")
    mix_a = _mm(yn, w_out, "nn", F32, "mm_out")
    h1, (fn0,) = _resid_norm(h0, mix_a, full["a_norm_post"], [fpre[0:1]], "resid_a")

    def ffn_fwd(fn, i, comm=None):
        u = _mm(fn, w_up[i], "nn", F32, f"mm_up{i}", comm=comm)
        u, got = u if comm is not None else (u, None)
        act = _ffn_act_fwd(u, f_cw[i], f_cb[i:i + 1], f"ffn_act{i}")
        return u, act, _mm(act, w_down[i], "nn", F32, f"mm_down{i}"), got

    u0, act0, ffn0, (g_kv, g_q, g_o) = ffn_fwd(fn0, 0, gather_of(w_kv, b_w_q[0], b_w_o[0]))
    w_kvf, w_q, w_o = g_kv.reshape(d, 2 * D_KV), g_q.reshape(d, d), g_o.reshape(d, d)
    h2, (kvn, bn) = _resid_norm(h1, ffn0, fpost[0:1], [wts["kv_norm"].reshape(1, d), wts["b_norm_pre"]], "resid_f0")
    kv = _mm(kvn, w_kvf, "nn", F32, "mm_kv")
    q = _mm(bn, w_q, "nn", F32, "mm_q")
    o, (g_up1, g_dn1) = _attn_fwd(q, kv, wts["b_sinks"], "attn_fwd", comm=gather_of(f_w_up[1], f_w_down[1]))
    w_up[1], w_down[1] = _assemble_cols(g_up1, 2 * D_FF, segs_up, "asm_w_up1"), g_dn1.reshape(D_FF, d)
    mix_b = _mm(o, w_o, "nn", F32, "mm_o")
    h3, (fn1,) = _resid_norm(h2, mix_b, wts["b_norm_post"], [fpre[1:2]], "resid_b")
    u1, act1, ffn1, _ = ffn_fwd(fn1, 1)
    dh4, loss_row = _final_loss(h3, ffn1, fpost[1:2], tgt, "loss")
    loss = lax.psum(loss_row[0, 0], ("x", "y", "c"))

    grads = {}

    def ffn_bwd(dh_out, h_in, fn, u, act, ffn, i, carry_down):
        dffn, dw_post = _norm_bwd(ffn, fpost[i:i + 1], dh_out, None, BF16, f"nb_fpost{i}")
        dact = _mm(dffn, w_down[i], "nt", F32, f"mm_dact{i}")
        dw_down = _mm(act, dffn, "tn", BF16, f"mm_dwdown{i}").reshape(N_DEV, -1, d)
        res = _ffn_act_bwd(u, dact, f_cw[i], f_cb[i:i + 1], f"ffn_act_bwd{i}",
                           comm=_scatter_comm([dw_down]) if carry_down else None)
        (dg, dv, dwg, dwv, dbg, dbv), dw_down = (res[0], res[1][0]) if carry_down else (res, dw_down)
        du = jnp.concatenate([dg, dv], axis=1)
        dfn = _mm(du, w_up[i], "nt", F32, f"mm_dfn{i}")
        dw_up = _scatter_cols(_mm(fn, du, "tn", BF16, f"mm_dwup{i}"), ws_up, segs_up, f"scat_w_up{i}")
        dh_in, dw_pre = _norm_bwd(h_in, fpre[i:i + 1], dfn, dh_out, F32, f"nb_fpre{i}")
        return dh_in, dict(post=dw_post, down=dw_down, cw=jnp.concatenate([dwg, dwv], axis=1),
                           cb=jnp.concatenate([dbg, dbv], axis=1), up=dw_up, pre=dw_pre)

    dh3, gf1 = ffn_bwd(dh4, h3, fn1, u1, act1, ffn1, 1, True)
    p_dn1 = gf1["down"]
    dmix_b, grads["b_norm_post"] = _norm_bwd(mix_b, wts["b_norm_post"], dh3, None, BF16, "nb_bpost")
    do = _mm(dmix_b, w_o, "nt", F32, "mm_do")
    dw_o = _mm(o, dmix_b, "tn", BF16, "mm_dwo").reshape(N_DEV, -1, d)
    (dq, dkv, dsinks), (p_up1, p_o) = _attn_bwd(q, kv, wts["b_sinks"], do, "attn_bwd",
                                                comm=_scatter_comm([gf1["up"], dw_o]))
    grads["b_sinks"] = dsinks[:, :N_Q_HEADS]
    dbn = _mm(dq, w_q, "nt", F32, "mm_dbn")
    dw_q = _mm(bn, dq, "tn", BF16, "mm_dwq").reshape(N_DEV, -1, d)
    dkv16 = dkv.astype(BF16)
    dkvn = _mm(dkv16, w_kvf, "nt", F32, "mm_dkvn")
    dw_kv = _mm(kvn, dkv16, "tn", BF16, "mm_dwkv").reshape(N_DEV, -1, 2 * D_KV)
    dh2, grads["b_norm_pre"] = _norm_bwd(h2, wts["b_norm_pre"], dbn, dh3, F32, "nb_bpre")
    dh2, dw_kvn = _norm_bwd(h2, wts["kv_norm"].reshape(1, d), dkvn, dh2, F32, "nb_kv")
    grads["kv_norm"] = dw_kvn.reshape(d)
    dh1, gf0 = ffn_bwd(dh2, h1, fn0, u0, act0, ffn0, 0, False)
    grads["f_norm_post"] = jnp.concatenate([gf0["post"], gf1["post"]], axis=0)
    grads["f_norm_pre"] = jnp.concatenate([gf0["pre"], gf1["pre"]], axis=0)
    grads["f_conv_w"] = jnp.stack([gf0["cw"], gf1["cw"]])
    grads["f_conv_b"] = jnp.concatenate([gf0["cb"], gf1["cb"]], axis=0)

    dmix_a, grads["a_norm_post"] = _norm_bwd(mix_a, full["a_norm_post"], dh1, None, BF16, "nb_apost")
    dyn = _mm(dmix_a, w_out, "nt", F32, "mm_dyn")
    dw_out = _mm(yn, dmix_a, "tn", BF16, "mm_dwout").reshape(N_DEV, -1, d)
    dy_ssd, dz, grads["a_gate_norm"] = _gatenorm_bwd(y_ssd, zx, full["a_gate_norm"], dyn, "gatenorm_bwd")
    (dxs, dbm, dcm, ddtp, dalog, ddsk, dbias), (p_q, p_kv, p_dn0, p_up0, p_out) = _ssd_bwd(
        xbc, zx, bias_g, alog_g, dsk_g, dy_ssd, hst, "ssd_bwd",
        comm=_scatter_comm([dw_q, dw_kv, gf0["down"], gf0["up"], dw_out]))
    dxbc = jnp.concatenate([dxs, dbm, dcm], axis=1)
    grads["a_a_log"] = _ungroup_lanes(dalog, hg)
    grads["a_d_skip"] = _ungroup_lanes(ddsk, hg)
    grads["a_dt_bias"] = _ungroup_lanes(dbias, hg)
    dpre, dcw, dcb = _conv_silu_bwd(zx, dxbc, a_conv_w, a_conv_b, "conv_a_bwd")
    grads["a_conv_w"], grads["a_conv_b"] = dcw[None], dcb
    dzx = jnp.concatenate([dz, dpre, ddtp.astype(BF16)], axis=1)
    dhn0 = _mm(dzx, w_in_all, "nt", F32, "mm_dhn0")
    dw_in_all = _mm(hn0, dzx, "tn", BF16, "mm_dwin")
    dh0, grads["a_norm_pre"] = _norm_bwd(h0, full["a_norm_pre"], dhn0, dh1, F32, "nb_apre")
    grad_x = dh0[N_META:N_META + SEQ][None]
    grads["meta_tokens"] = dh0[:N_META]

    small_local = _pack([_split8(grads[n], SHARD_AXIS[n], wts[n].shape[SHARD_AXIS[n]]) for n in SMALL], 1, 8, F32)
    repl_local = _pack([grads[n] for n in REPL], 0, 8, F32)
    n_sr = small_local.shape[1]
    p_in, = _run_comm(_scatter_comm([_scatter_cols(dw_in_all, ws_in, segs_in, "scat_w_in")]), "rs_w_in")
    small_all, = _run_comm(
        _gather_comm([jnp.concatenate([small_local.reshape(N_DEV * n_sr, LANE), repl_local], axis=0)]),
        "gather_small_grads")
    parts_big = dict(a_w_in=p_in, a_w_out=p_out, w_kv=p_kv, b_w_q=p_q, b_w_o=p_o,
                     f_w_up=jnp.concatenate([p_up0, p_up1], axis=1), f_w_down=jnp.concatenate([p_dn0, p_dn1], axis=1))
    mine_small = lax.dynamic_slice_in_dim(small_all, me * n_sr, n_sr, axis=1)
    parts_small = jnp.concatenate([mine_small, small_all[:, N_DEV * n_sr:]], axis=1)

    def flat_f32(dct, names, mult):
        return _pack([dct[n] for n in names], 0, mult, F32)

    big_out = {}
    for n in BIG:
        cols = wts[n].shape[-1]
        res = _adamw(parts_big[n], *[dct[n].reshape(-1, cols) for dct in (wts, mom, var)], f"adamw_{n}")
        big_out[n] = [r.reshape(wts[n].shape) for r in res]
    sm_in = [jnp.concatenate([flat_f32(dct, SMALL, 8), flat_f32(dct, REPL, 8)], axis=0) for dct in (wts, mom, var)]
    small_out = _adamw(parts_small, *sm_in, "adamw_small")

    outs = []
    for kind in range(4):
        res = {n: big_out[n][kind] for n in BIG}
        for n, a in zip(SMALL, _unpack(small_out[kind][:n_sr], 0, [wts[n].shape for n in SMALL])):
            res[n] = a
        for n, a in zip(REPL, _unpack(small_out[kind][n_sr:], 0, [wts[n].shape for n in REPL])):
            res[n] = a
        outs.append(res)
    return (loss, grad_x, *[outs[0][n] for n in WEIGHTS], *[outs[1][n] for n in WEIGHTS],
            *[outs[2][n] for n in WEIGHTS], *[outs[3][n] for n in WEIGHTS])
```

```python
import functools
import math

import jax
import jax.numpy as jnp
from jax import lax
from jax.experimental import pallas as pl
from jax.experimental.pallas import tpu as pltpu

F32, BF16 = jnp.float32, jnp.bfloat16
S = jax.ShapeDtypeStruct

D_MODEL = 1024
SEQ = 2048
N_META = 16
D_INNER = 2048
HEAD_P = 64
SSM_HEADS = D_INNER // HEAD_P
SSM_GROUPS = 4
D_STATE = 128
SSM_CONV = 4
D_BC = SSM_GROUPS * D_STATE
D_XBC = D_INNER + 2 * D_BC
ATTN_DH = 64
N_Q_HEADS = D_MODEL // ATTN_DH
N_KV_HEADS = 4
D_KV = N_KV_HEADS * ATTN_DH
WINDOW = 128
D_FF = 2816
FFN_CONV = 3
RMS_EPS = 1e-6
NEG = -1e30
LR, B1, B2, EPS, WD, STEP = 0.001, 0.9, 0.999, 1e-08, 0.01, 10

N_DEV = 8
T = 128
LANE = 128
VMEM_LIMIT = 48 * 1024 * 1024

BIG = ("a_w_in", "a_w_out", "w_kv", "b_w_q", "b_w_o", "f_w_up", "f_w_down")
SMALL = ("meta_tokens", "a_norm_pre", "a_conv_w", "a_conv_b", "a_gate_norm", "a_norm_post", "f_conv_w")
REPL = ("a_dt_bias", "a_a_log", "a_d_skip", "kv_norm", "b_norm_pre", "b_sinks", "b_norm_post",
        "f_norm_pre", "f_conv_b", "f_norm_post")
SHARD_AXIS = dict(a_w_in=2, a_w_out=1, w_kv=0, b_w_q=1, b_w_o=1, f_w_up=2, f_w_down=1, meta_tokens=1,
                  a_norm_pre=1, a_conv_w=2, a_conv_b=1, a_gate_norm=1, a_norm_post=1, f_conv_w=2)
WEIGHTS = ("meta_tokens", "a_norm_pre", "a_w_in", "a_conv_w", "a_conv_b", "a_dt_bias", "a_a_log", "a_d_skip",
           "a_gate_norm", "a_w_out", "a_norm_post", "kv_norm", "w_kv", "b_norm_pre", "b_w_q", "b_sinks", "b_w_o",
           "b_norm_post", "f_norm_pre", "f_w_up", "f_conv_w", "f_conv_b", "f_w_down", "f_norm_post")


def _seq_rows():
    return -(-(N_META + SEQ) // T) * T


def _cp(sem=None):
    return pltpu.CompilerParams(dimension_semantics=sem, vmem_limit_bytes=VMEM_LIMIT)


def _pick(n, target):
    t = min(n, target)
    t -= t % LANE
    while n % t:
        t -= LANE
    return t


def _sigmoid(x):
    return 1.0 / (1.0 + jnp.exp(-x))


def _softplus(x):
    return jnp.maximum(x, 0.0) + jnp.log(1.0 + jnp.exp(-jnp.abs(x)))


_NN = (((1,), (0,)), ((), ()))
_NT = (((1,), (1,)), ((), ()))
_TN = (((0,), (0,)), ((), ()))


def _dot(a, b, dims=_NN):
    return lax.dot_general(a, b, dims, preferred_element_type=F32)


def _dot_hi(a, b):
    return lax.dot_general(a, b, _NN, precision=lax.Precision.HIGHEST, preferred_element_type=F32)


_HBM = pl.BlockSpec(memory_space=pltpu.HBM)
_MESH = pl.DeviceIdType.MESH


class _Comm:
    def __init__(self, ins, out_shapes, scratch, first, last):
        self.ins, self.out_shapes, self.scratch, self.first, self.last = ins, out_shapes, scratch, first, last


def _call(body, name, out_shape, grid, in_specs, out_specs, sem, args, scratch=(), comm=None):
    if comm is None:
        return pl.pallas_call(body, name=name, out_shape=out_shape, grid=grid, in_specs=in_specs, out_specs=out_specs,
                              scratch_shapes=list(scratch), compiler_params=_cp(sem))(*args)
    single = not isinstance(out_shape, (list, tuple))
    outs = [out_shape] if single else list(out_shape)
    ospecs = [out_specs] if single else list(out_specs)
    n_in, n_out, n_scr, ci, co = len(in_specs), len(outs), len(scratch), len(comm.ins), len(comm.out_shapes)

    def carrier(*refs):
        p = 0
        parts = []
        for cnt in (n_in, ci, n_out, co, n_scr, len(comm.scratch)):
            parts.append(refs[p:p + cnt])
            p += cnt
        ins, cins, outs_r, couts, scr, cscr = parts
        ids = [pl.program_id(i) for i in range(len(grid))]
        first, last = ids[0] == 0, ids[0] == grid[0] - 1
        for i in range(1, len(grid)):
            first, last = first & (ids[i] == 0), last & (ids[i] == grid[i] - 1)

        @pl.when(first)
        def _():
            comm.first(cins, couts, cscr)

        body(*ins, *outs_r, *scr)

        @pl.when(last)
        def _():
            comm.last(cins, couts, cscr)

    res = pl.pallas_call(
        carrier, name=name, out_shape=outs + list(comm.out_shapes), grid=grid,
        in_specs=list(in_specs) + [_HBM] * ci, out_specs=ospecs + [_HBM] * co,
        scratch_shapes=list(scratch) + list(comm.scratch),
        compiler_params=_cp(("arbitrary",) * len(grid)))(*args, *comm.ins)
    mine = res[0] if single else list(res[:n_out])
    return mine, list(res[n_out:])


def _mm(a, b, mode, out_dtype, name, comm=None):
    if mode == "tn":
        m, kk = a.shape
        n = b.shape[1]
        tko, tn = _pick(kk, 512), _pick(n, 512)

        def body(a_ref, b_ref, o_ref):
            o_ref[...] = _dot(a_ref[...], b_ref[...], _TN).astype(o_ref.dtype)

        return _call(
            body, name, S((kk, n), out_dtype), (kk // tko, n // tn),
            [pl.BlockSpec((m, tko), lambda i, j: (0, i)), pl.BlockSpec((m, tn), lambda i, j: (0, j))],
            pl.BlockSpec((tko, tn), lambda i, j: (i, j)), ("parallel", "parallel"), (a, b), comm=comm)

    m, kk = a.shape
    n = b.shape[1] if mode == "nn" else b.shape[0]
    tn = _pick(n, 512)
    tk = kk if kk <= 2048 else _pick(kk, 1536)
    nk = kk // tk
    dims = _NN if mode == "nn" else _NT

    def body(a_ref, b_ref, o_ref, *acc):
        part = _dot(a_ref[...], b_ref[...], dims)
        if nk == 1:
            o_ref[...] = part.astype(o_ref.dtype)
        else:
            k = pl.program_id(1)

            @pl.when(k == 0)
            def _():
                acc[0][...] = part

            @pl.when(k > 0)
            def _():
                acc[0][...] += part

            @pl.when(k == nk - 1)
            def _():
                o_ref[...] = acc[0][...].astype(o_ref.dtype)

    b_spec = (pl.BlockSpec((tk, tn), lambda j, k: (k, j)) if mode == "nn"
              else pl.BlockSpec((tn, tk), lambda j, k: (j, k)))
    return _call(
        body, name, S((m, n), out_dtype), (n // tn, nk), [pl.BlockSpec((m, tk), lambda j, k: (0, k)), b_spec],
        pl.BlockSpec((m, tn), lambda j, k: (0, j)), ("parallel", "arbitrary"), (a, b),
        scratch=[pltpu.VMEM((m, tn), F32)] if nk > 1 else [], comm=comm)


def _rms(x, w):
    return x * lax.rsqrt(jnp.mean(x * x, axis=-1, keepdims=True) + RMS_EPS) * w


def _row_tile(rows):
    return rows // 8


def _resid_norm(h, br, w_post, next_ws, name):
    rows, d = h.shape
    tr = _row_tile(rows)
    has_br = br is not None
    nw = len(next_ws)

    def body(*refs):
        h_ref = refs[0]
        pos = 1
        x = h_ref[...]
        if has_br:
            x = x + _rms(refs[1][...], refs[2][...])
            pos = 3
        w_refs = refs[pos:pos + nw]
        outs = refs[pos + nw:]
        if has_br:
            outs[0][...] = x
            outs = outs[1:]
        for w_ref, o_ref in zip(w_refs, outs):
            o_ref[...] = _rms(x, w_ref[...]).astype(o_ref.dtype)

    row = pl.BlockSpec((tr, d), lambda i: (i, 0))
    vec = pl.BlockSpec((1, d), lambda i: (0, 0))
    ins = [h] + ([br, w_post] if has_br else []) + list(next_ws)
    in_specs = [row] + ([row, vec] if has_br else []) + [vec] * nw
    out_shape = ([S((rows, d), F32)] if has_br else []) + [S((rows, d), BF16)] * nw
    res = pl.pallas_call(body, name=name, out_shape=out_shape, grid=(rows // tr,), in_specs=in_specs,
                         out_specs=[row] * len(out_shape), compiler_params=_cp(("parallel",)))(*ins)
    if has_br:
        return res[0], list(res[1:])
    return h, list(res)


def _norm_bwd(x, w, dy, add, out_dtype, name):
    rows, d = x.shape
    tr = _row_tile(rows)
    has_add = add is not None

    def body(*refs):
        x_ref, w_ref, dy_ref = refs[:3]
        dx_ref, dw_ref = refs[-2:]
        xv = x_ref[...]
        r = lax.rsqrt(jnp.mean(xv * xv, axis=-1, keepdims=True) + RMS_EPS)
        dyv = dy_ref[...].astype(F32)
        wdy = dyv * w_ref[...]
        dx = r * wdy - xv * (r * r * r) * jnp.mean(xv * wdy, axis=-1, keepdims=True)
        if has_add:
            dx = dx + refs[3][...]
        dx_ref[...] = dx.astype(dx_ref.dtype)

        @pl.when(pl.program_id(0) == 0)
        def _():
            dw_ref[...] = jnp.zeros_like(dw_ref)

        dw_ref[...] += jnp.sum(dyv * xv * r, axis=0, keepdims=True)

    row = pl.BlockSpec((tr, d), lambda i: (i, 0))
    vec = pl.BlockSpec((1, d), lambda i: (0, 0))
    ins = [x, w, dy] + ([add] if has_add else [])
    return pl.pallas_call(body, name=name, out_shape=[S((rows, d), out_dtype), S((1, d), F32)], grid=(rows // tr,),
                          in_specs=[row, vec, row] + ([row] if has_add else []), out_specs=[row, vec],
                          compiler_params=_cp(("arbitrary",)))(*ins)


def _final_loss(h, br, w_post, tgt, name):
    rows, d = h.shape
    tr = _row_tile(rows)

    def body(h_ref, br_ref, w_ref, t_ref, dh_ref, loss_ref):
        i = pl.program_id(0)
        y = h_ref[...] + _rms(br_ref[...], w_ref[...])
        r = i * tr + lax.broadcasted_iota(jnp.int32, (tr, 1), 0)
        real = (r >= N_META) & (r < N_META + SEQ)
        diff = jnp.where(real, y - t_ref[...], 0.0)
        dh_ref[...] = diff * (1.0 / d)

        @pl.when(i == 0)
        def _():
            loss_ref[...] = jnp.zeros_like(loss_ref)

        loss_ref[...] += jnp.sum(diff * diff) * (0.5 / d)

    row = pl.BlockSpec((tr, d), lambda i: (i, 0))
    return pl.pallas_call(body, name=name, out_shape=[S((rows, d), F32), S((1, LANE), F32)], grid=(rows // tr,),
                          in_specs=[row, row, pl.BlockSpec((1, d), lambda i: (0, 0)), row],
                          out_specs=[row, pl.BlockSpec((1, LANE), lambda i: (0, 0))],
                          compiler_params=_cp(("arbitrary",)))(h, br, w_post, tgt)


def _gatenorm_fwd(y, zx, w, name):
    rows, d = y.shape
    tr = _row_tile(rows)

    def body(y_ref, z_ref, w_ref, o_ref):
        z = z_ref[...]
        o_ref[...] = _rms(y_ref[...] * z * _sigmoid(z), w_ref[...]).astype(o_ref.dtype)

    row = pl.BlockSpec((tr, d), lambda i: (i, 0))
    return pl.pallas_call(body, name=name, out_shape=S((rows, d), BF16), grid=(rows // tr,),
                          in_specs=[row, row, pl.BlockSpec((1, d), lambda i: (0, 0))], out_specs=row,
                          compiler_params=_cp(("parallel",)))(y, zx, w)


def _gatenorm_bwd(y, zx, w, dyn, name):
    rows, d = y.shape
    tr = _row_tile(rows)

    def body(y_ref, z_ref, w_ref, dyn_ref, dy_ref, dz_ref, dw_ref):
        yv, z = y_ref[...], z_ref[...]
        sg = _sigmoid(z)
        sz = z * sg
        g = yv * sz
        r = lax.rsqrt(jnp.mean(g * g, axis=-1, keepdims=True) + RMS_EPS)
        dyn_v = dyn_ref[...]
        wdy = dyn_v * w_ref[...]
        dg = r * wdy - g * (r * r * r) * jnp.mean(g * wdy, axis=-1, keepdims=True)
        dy_ref[...] = dg * sz
        dz_ref[...] = (dg * yv * sg * (1.0 + z * (1.0 - sg))).astype(dz_ref.dtype)

        @pl.when(pl.program_id(0) == 0)
        def _():
            dw_ref[...] = jnp.zeros_like(dw_ref)

        dw_ref[...] += jnp.sum(dyn_v * g * r, axis=0, keepdims=True)

    row = pl.BlockSpec((tr, d), lambda i: (i, 0))
    vec = pl.BlockSpec((1, d), lambda i: (0, 0))
    return pl.pallas_call(body, name=name, out_shape=[S((rows, d), F32), S((rows, d), BF16), S((1, d), F32)],
                          grid=(rows // tr,), in_specs=[row, row, vec, row], out_specs=[row, row, vec],
                          compiler_params=_cp(("arbitrary",)))(y, zx, w, dyn)


def _shift_down(x, s, rows_iota):
    if s == 0:
        return x
    return jnp.where(rows_iota >= s, pltpu.roll(x, s, 0), 0.0)


def _shift_up(x, s, rows_iota):
    if s == 0:
        return x
    rows = x.shape[0]
    return jnp.where(rows_iota < rows - s, pltpu.roll(x, rows - s, 0), 0.0)


def _r16(v):
    return v.astype(BF16).astype(F32)


def _conv(x, w_ref, b_ref, taps, rows_iota):
    x = _r16(x)
    acc = jnp.zeros_like(x)
    for k in range(taps):
        acc = acc + _r16(w_ref[k:k + 1, :]) * _shift_down(x, taps - 1 - k, rows_iota)
    return acc + b_ref[...]


def _conv_bwd(x, du, w_ref, dw_ref, db_ref, taps, rows_iota):
    db_ref[...] = jnp.sum(du, axis=0, keepdims=True)
    x, du = _r16(x), _r16(du)
    dx = jnp.zeros_like(x)
    for k in range(taps):
        s = taps - 1 - k
        dx = dx + _r16(w_ref[k:k + 1, :]) * _shift_up(du, s, rows_iota)
        dw_ref[k:k + 1, :] = jnp.sum(du * _shift_down(x, s, rows_iota), axis=0, keepdims=True)
    return dx


def _conv_silu_fwd(zx, w, b, name, comm=None):
    rows = zx.shape[0]
    cb = 512
    off = D_INNER // cb

    def body(x_ref, w_ref, b_ref, o_ref):
        it = lax.broadcasted_iota(jnp.int32, (rows, 1), 0)
        u = _conv(x_ref[...], w_ref, b_ref, SSM_CONV, it)
        o_ref[...] = u * _sigmoid(u)

    return _call(
        body, name, S((rows, D_XBC), F32), (D_XBC // cb,),
        [pl.BlockSpec((rows, cb), lambda j: (0, off + j)), pl.BlockSpec((SSM_CONV, cb), lambda j: (0, j)),
         pl.BlockSpec((1, cb), lambda j: (0, j))],
        pl.BlockSpec((rows, cb), lambda j: (0, j)), ("parallel",), (zx, w, b), comm=comm)


def _conv_silu_bwd(zx, dxbc, w, b, name):
    rows = zx.shape[0]
    cb = 512
    off = D_INNER // cb

    def body(x_ref, d_ref, w_ref, b_ref, dx_ref, dw_ref, db_ref):
        it = lax.broadcasted_iota(jnp.int32, (rows, 1), 0)
        x = x_ref[...]
        u = _conv(x, w_ref, b_ref, SSM_CONV, it)
        sg = _sigmoid(u)
        du = d_ref[...] * sg * (1.0 + u * (1.0 - sg))
        dx_ref[...] = _conv_bwd(x, du, w_ref, dw_ref, db_ref, SSM_CONV, it).astype(dx_ref.dtype)

    col = pl.BlockSpec((rows, cb), lambda j: (0, j))
    wsp = pl.BlockSpec((SSM_CONV, cb), lambda j: (0, j))
    bsp = pl.BlockSpec((1, cb), lambda j: (0, j))
    return pl.pallas_call(
        body, name=name, out_shape=[S((rows, D_XBC), BF16), S((SSM_CONV, D_XBC), F32), S((1, D_XBC), F32)],
        grid=(D_XBC // cb,), in_specs=[pl.BlockSpec((rows, cb), lambda j: (0, off + j)), col, wsp, bsp],
        out_specs=[col, wsp, bsp], compiler_params=_cp(("parallel",)))(zx, dxbc, w, b)


def _ffn_act_fwd(u, w, b, name, comm=None):
    rows = u.shape[0]
    cb = 256
    nb = D_FF // cb

    def body(g_ref, v_ref, wg_ref, wv_ref, bg_ref, bv_ref, o_ref):
        it = lax.broadcasted_iota(jnp.int32, (rows, 1), 0)
        g = _conv(g_ref[...], wg_ref, bg_ref, FFN_CONV, it)
        v = _conv(v_ref[...], wv_ref, bv_ref, FFN_CONV, it)
        o_ref[...] = (g * _sigmoid(g) * v).astype(o_ref.dtype)

    def sp(r, shift):
        return pl.BlockSpec((r, cb), lambda j: (0, shift + j))

    return _call(
        body, name, S((rows, D_FF), BF16), (nb,),
        [sp(rows, 0), sp(rows, nb), sp(FFN_CONV, 0), sp(FFN_CONV, nb), sp(1, 0), sp(1, nb)],
        sp(rows, 0), ("parallel",), (u, u, w, w, b, b), comm=comm)


def _ffn_act_bwd(u, dact, w, b, name, comm=None):
    rows = u.shape[0]
    cb = 256
    nb = D_FF // cb

    def body(g_ref, v_ref, d_ref, wg_ref, wv_ref, bg_ref, bv_ref, dg_ref, dv_ref, dwg_ref, dwv_ref, dbg_ref, dbv_ref):
        it = lax.broadcasted_iota(jnp.int32, (rows, 1), 0)
        xg, xv = g_ref[...], v_ref[...]
        g = _conv(xg, wg_ref, bg_ref, FFN_CONV, it)
        v = _conv(xv, wv_ref, bv_ref, FFN_CONV, it)
        sg = _sigmoid(g)
        d = d_ref[...]
        dgate = d * v * sg * (1.0 + g * (1.0 - sg))
        dval = d * g * sg
        dg_ref[...] = _conv_bwd(xg, dgate, wg_ref, dwg_ref, dbg_ref, FFN_CONV, it).astype(dg_ref.dtype)
        dv_ref[...] = _conv_bwd(xv, dval, wv_ref, dwv_ref, dbv_ref, FFN_CONV, it).astype(dv_ref.dtype)

    def sp(r, shift):
        return pl.BlockSpec((r, cb), lambda j: (0, shift + j))

    return _call(
        body, name,
        [S((rows, D_FF), BF16), S((rows, D_FF), BF16), S((FFN_CONV, D_FF), F32), S((FFN_CONV, D_FF), F32),
         S((1, D_FF), F32), S((1, D_FF), F32)],
        (nb,),
        [sp(rows, 0), sp(rows, nb), sp(rows, 0), sp(FFN_CONV, 0), sp(FFN_CONV, nb), sp(1, 0), sp(1, nb)],
        [sp(rows, 0), sp(rows, 0), sp(FFN_CONV, 0), sp(FFN_CONV, 0), sp(1, 0), sp(1, 0)],
        ("parallel",), (u, u, dact, w, w, b, b), comm=comm)


def _ssd_consts(dtp_ref, bias_ref, alog_ref, hg):
    lane = lax.broadcasted_iota(jnp.int32, (1, LANE), 1)
    pre = dtp_ref[...] + bias_ref[...]
    dt = _softplus(pre)
    a_row = jnp.where(lane < hg, -jnp.exp(alog_ref[...]), 0.0)
    ri = lax.broadcasted_iota(jnp.int32, (T, T), 0)
    ci = lax.broadcasted_iota(jnp.int32, (T, T), 1)
    cs = _dot_hi((ri >= ci).astype(F32), dt * a_row)
    return pre, dt, a_row, cs, ri, ci, lane


def _ssd_fwd(xbc, zx, bias, alog, dsk, name, comm=None):
    rows = xbc.shape[0]
    nc = rows // T
    hg = SSM_HEADS // SSM_GROUPS
    gw = hg * HEAD_P
    xoff, boff, coff = 0, D_INNER // D_STATE, (D_INNER + D_BC) // D_STATE
    dtoff = (D_INNER + D_XBC) // LANE

    def body(x_ref, b_ref, c_ref, dtp_ref, bias_ref, alog_ref, dsk_ref, y_ref, hst_ref, hs):
        c = pl.program_id(1)

        @pl.when(c == 0)
        def _():
            hs[...] = jnp.zeros_like(hs)

        _, dt, _, cs, ri, ci, _ = _ssd_consts(dtp_ref, bias_ref, alog_ref, hg)
        cst = cs.T
        x = x_ref[...]
        bb, cbf = b_ref[...].astype(BF16), c_ref[...].astype(BF16)
        gmat = _dot(cbf, bb, _NT)
        causal = ri >= ci
        dskv = dsk_ref[...]
        hst_ref[0, 0] = hs[...]
        for k in range(hg):
            csk, csr = cs[:, k:k + 1], cst[k:k + 1, :]
            lm = jnp.exp(jnp.where(causal, csk - csr, NEG))
            xk = x[:, k * HEAD_P:(k + 1) * HEAD_P]
            xdt = xk * dt[:, k:k + 1]
            hk = hs[k * HEAD_P:(k + 1) * HEAD_P, :]
            yd = _dot((gmat * lm).astype(BF16), xdt.astype(BF16))
            yo = jnp.exp(csk) * _dot(cbf, hk.astype(BF16), _NT)
            y_ref[:, k * HEAD_P:(k + 1) * HEAD_P] = yd + yo + dskv[:, k:k + 1] * xk
            cl = cs[T - 1:T, k:k + 1]
            st = _dot((xdt * jnp.exp(cl - csk)).astype(BF16), bb, _TN)
            hs[k * HEAD_P:(k + 1) * HEAD_P, :] = jnp.exp(cl) * hk + st

    vec = pl.BlockSpec((1, LANE), lambda g, c: (0, g))
    return _call(
        body, name, [S((rows, D_INNER), F32), S((nc, SSM_GROUPS, gw, D_STATE), F32)], (SSM_GROUPS, nc),
        [pl.BlockSpec((T, gw), lambda g, c: (c, xoff + g)),
         pl.BlockSpec((T, D_STATE), lambda g, c: (c, boff + g)),
         pl.BlockSpec((T, D_STATE), lambda g, c: (c, coff + g)),
         pl.BlockSpec((T, LANE), lambda g, c: (c, dtoff + g)), vec, vec, vec],
        [pl.BlockSpec((T, gw), lambda g, c: (c, g)), pl.BlockSpec((1, 1, gw, D_STATE), lambda g, c: (c, g, 0, 0))],
        ("parallel", "arbitrary"), (xbc, xbc, xbc, zx, bias, alog, dsk),
        scratch=[pltpu.VMEM((gw, D_STATE), F32)], comm=comm)


def _ssd_bwd(xbc, zx, bias, alog, dsk, dy, hst, name, comm=None):
    rows = xbc.shape[0]
    nc = rows // T
    hg = SSM_HEADS // SSM_GROUPS
    gw = hg * HEAD_P
    boff, coff = D_INNER // D_STATE, (D_INNER + D_BC) // D_STATE
    dtoff = (D_INNER + D_XBC) // LANE

    def body(x_ref, b_ref, c_ref, dtp_ref, bias_ref, alog_ref, dsk_ref, dy_ref, hst_ref,
             dx_ref, db_ref, dc_ref, ddtp_ref, dalog_ref, ddsk_ref, dbias_ref, dhs):
        step = pl.program_id(1)

        @pl.when(step == 0)
        def _():
            dhs[...] = jnp.zeros_like(dhs)
            dalog_ref[...] = jnp.zeros_like(dalog_ref)
            ddsk_ref[...] = jnp.zeros_like(ddsk_ref)
            dbias_ref[...] = jnp.zeros_like(dbias_ref)

        pre, dt, a_row, cs, ri, ci, lane = _ssd_consts(dtp_ref, bias_ref, alog_ref, hg)
        cst = cs.T
        x, dyv = x_ref[...], dy_ref[...]
        bb, cbf = b_ref[...].astype(BF16), c_ref[...].astype(BF16)
        gt = _dot(bb, cbf, _NT)
        causal_t = ci >= ri
        dskv = dsk_ref[...]
        last = lax.broadcasted_iota(jnp.int32, (T, 1), 0) == T - 1
        dgt = jnp.zeros((T, T), F32)
        dc_acc = jnp.zeros((T, D_STATE), F32)
        db_acc = jnp.zeros((T, D_STATE), F32)
        ddt_acc = jnp.zeros((T, LANE), F32)
        dcs_acc = jnp.zeros((T, LANE), F32)
        dcs_row = jnp.zeros((T, LANE), F32)
        ddsk_acc = jnp.zeros((1, LANE), F32)
        head_row = lax.broadcasted_iota(jnp.int32, (T, 1), 0)
        for k in range(hg):
            sl = slice(k * HEAD_P, (k + 1) * HEAD_P)
            csk, csr = cs[:, k:k + 1], cst[k:k + 1, :]
            lt = jnp.exp(jnp.where(causal_t, csr - csk, NEG))
            xk, dyk = x[:, sl], dyv[:, sl]
            dtk, dk = dt[:, k:k + 1], dskv[:, k:k + 1]
            dyb = dyk.astype(BF16)
            mpt = gt * lt
            z = _dot(mpt.astype(BF16), dyb)
            dmt = _dot((xk * dtk).astype(BF16), dyb, _NT)
            dgt = dgt + dmt * lt
            q = dmt * mpt
            xz = jnp.sum(xk * z, axis=1, keepdims=True)
            q_rows = jnp.sum(q, axis=1, keepdims=True)
            dcs_row = dcs_row + jnp.where(head_row == k, jnp.sum(q, axis=0, keepdims=True), 0.0)
            hk = hst_ref[0, 0, sl, :]
            dhn = dhs[sl, :]
            e = jnp.exp(csk)
            cl = cs[T - 1:T, k:k + 1]
            wdec = jnp.exp(cl - csk)
            w = wdec * dtk
            r = _dot(bb, dhn.astype(BF16), _NT)
            dx_ref[:, sl] = dtk * z + dk * dyk + r * w
            dw = jnp.sum(r * xk, axis=1, keepdims=True)
            dcl = jnp.exp(cl) * jnp.sum(dhn * hk) + jnp.sum(dw * w)
            yo = e * _dot(cbf, hk.astype(BF16), _NT)
            dcs_k = jnp.sum(dyk * yo, axis=1, keepdims=True) - q_rows - dw * w + jnp.where(last, dcl, 0.0)
            dye = (dyk * e).astype(BF16)
            dc_acc = dc_acc + _dot(dye, hk.astype(BF16))
            db_acc = db_acc + _dot((xk * w).astype(BF16), dhn.astype(BF16))
            dhs[sl, :] = jnp.exp(cl) * dhn + _dot(dye, cbf, _TN)
            onehot = (lane == k).astype(F32)
            ddt_acc = ddt_acc + (xz + dw * wdec) * onehot
            dcs_acc = dcs_acc + dcs_k * onehot
            ddsk_acc = ddsk_acc + jnp.sum(dyk * xk) * onehot
        dc_ref[...] = _dot(dgt.T.astype(BF16), bb) + dc_acc
        db_ref[...] = _dot(dgt.astype(BF16), cbf) + db_acc
        da = _dot_hi((ci >= ri).astype(F32), dcs_acc + dcs_row.T)
        ddtp = (ddt_acc + da * a_row) * _sigmoid(pre)
        ddtp = jnp.where(lane < hg, ddtp, 0.0)
        ddtp_ref[...] = ddtp
        dbias_ref[...] += jnp.sum(ddtp, axis=0, keepdims=True)
        dalog_ref[...] += jnp.sum(da * dt, axis=0, keepdims=True) * a_row
        ddsk_ref[...] += ddsk_acc

    def rc(c):
        return nc - 1 - c

    vec = pl.BlockSpec((1, LANE), lambda g, c: (0, g))
    xsp = pl.BlockSpec((T, gw), lambda g, c: (rc(c), g))
    return _call(
        body, name,
        [S((rows, D_INNER), F32), S((rows, D_BC), F32), S((rows, D_BC), F32),
         S((rows, SSM_GROUPS * LANE), F32), S((1, SSM_GROUPS * LANE), F32),
         S((1, SSM_GROUPS * LANE), F32), S((1, SSM_GROUPS * LANE), F32)],
        (SSM_GROUPS, nc),
        [xsp,
         pl.BlockSpec((T, D_STATE), lambda g, c: (rc(c), boff + g)),
         pl.BlockSpec((T, D_STATE), lambda g, c: (rc(c), coff + g)),
         pl.BlockSpec((T, LANE), lambda g, c: (rc(c), dtoff + g)), vec, vec, vec,
         xsp, pl.BlockSpec((1, 1, gw, D_STATE), lambda g, c: (rc(c), g, 0, 0))],
        [xsp,
         pl.BlockSpec((T, D_STATE), lambda g, c: (rc(c), g)),
         pl.BlockSpec((T, D_STATE), lambda g, c: (rc(c), g)),
         pl.BlockSpec((T, LANE), lambda g, c: (rc(c), g)), vec, vec, vec],
        ("parallel", "arbitrary"), (xbc, xbc, xbc, zx, bias, alog, dsk, dy, hst),
        scratch=[pltpu.VMEM((gw, D_STATE), F32)], comm=comm)


def _attn_tiles(kv_ref, j):
    prev = jnp.maximum(j - 1, 0)
    meta = kv_ref[0:T, :]
    prv = kv_ref[pl.ds(pl.multiple_of(prev * T, T), T), :]
    cur = kv_ref[pl.ds(pl.multiple_of(j * T, T), T), :]
    return jnp.concatenate([meta, prv, cur], axis=0)


def _attn_mask(j):
    r = j * T + lax.broadcasted_iota(jnp.int32, (3 * T, T), 1)
    row = lax.broadcasted_iota(jnp.int32, (3 * T, T), 0)
    t0, t1 = row < T, row < 2 * T
    s = jnp.where(t0, row, (j - 2) * T + row)
    ok = (s <= r) & ((s < N_META) | (s > r - WINDOW))
    use = (t0 & (j >= 2) & (row < N_META)) | (jnp.logical_not(t0) & t1 & (j >= 1)) | jnp.logical_not(t1)
    return ok & use


def _attn_probs(qh, k3, mask, sink):
    sc = jnp.where(mask, _dot(k3, qh, _NT), NEG)
    m = jnp.maximum(jnp.max(sc, axis=0, keepdims=True), sink)
    p = jnp.exp(sc - m)
    es = jnp.exp(sink - m)
    inv = 1.0 / (jnp.sum(p, axis=0, keepdims=True) + es)
    return p * inv, es * inv


def _attn_fwd(q, kv, sinks, name, comm=None):
    rows = q.shape[0]
    scale = 1.0 / math.sqrt(ATTN_DH)
    qpk = N_Q_HEADS // N_KV_HEADS

    def body(q_ref, kv_ref, s_ref, o_ref):
        j = pl.program_id(0)
        kv3 = _attn_tiles(kv_ref, j).astype(BF16)
        mask = _attn_mask(j)
        qv = (q_ref[...] * scale).astype(BF16)
        sk = s_ref[...]
        for kh in range(N_KV_HEADS):
            k3 = kv3[:, kh * ATTN_DH:(kh + 1) * ATTN_DH]
            v3 = kv3[:, D_KV + kh * ATTN_DH:D_KV + (kh + 1) * ATTN_DH]
            for g in range(qpk):
                h = kh * qpk + g
                p, _ = _attn_probs(qv[:, h * ATTN_DH:(h + 1) * ATTN_DH], k3, mask, sk[:, h:h + 1])
                o_ref[:, h * ATTN_DH:(h + 1) * ATTN_DH] = _dot(p.astype(BF16), v3, _TN).astype(o_ref.dtype)

    return _call(
        body, name, S((rows, D_MODEL), BF16), (rows // T,),
        [pl.BlockSpec((T, D_MODEL), lambda j: (j, 0)), pl.BlockSpec((rows, 2 * D_KV), lambda j: (0, 0)),
         pl.BlockSpec((1, N_Q_HEADS), lambda j: (0, 0))],
        pl.BlockSpec((T, D_MODEL), lambda j: (j, 0)), ("parallel",), (q, kv, sinks), comm=comm)


def _attn_bwd(q, kv, sinks, do, name, comm=None):
    rows = q.shape[0]
    scale = 1.0 / math.sqrt(ATTN_DH)
    qpk = N_Q_HEADS // N_KV_HEADS

    def body(q_ref, kv_ref, s_ref, do_ref, dq_ref, dkv_ref, ds_ref):
        j = pl.program_id(0)

        @pl.when(j == 0)
        def _():
            dkv_ref[...] = jnp.zeros_like(dkv_ref)
            ds_ref[...] = jnp.zeros_like(ds_ref)

        kv3 = _attn_tiles(kv_ref, j).astype(BF16)
        mask = _attn_mask(j)
        qv = (q_ref[...] * scale).astype(BF16)
        dov = do_ref[...].astype(BF16)
        sk = s_ref[...]
        lane = lax.broadcasted_iota(jnp.int32, (1, LANE), 1)
        ds_acc = jnp.zeros((1, LANE), F32)
        prev = jnp.maximum(j - 1, 0)
        for kh in range(N_KV_HEADS):
            ksl = slice(kh * ATTN_DH, (kh + 1) * ATTN_DH)
            vsl = slice(D_KV + kh * ATTN_DH, D_KV + (kh + 1) * ATTN_DH)
            k3, v3 = kv3[:, ksl], kv3[:, vsl]
            dk3 = jnp.zeros((3 * T, ATTN_DH), F32)
            dv3 = jnp.zeros((3 * T, ATTN_DH), F32)
            for g in range(qpk):
                h = kh * qpk + g
                hs = slice(h * ATTN_DH, (h + 1) * ATTN_DH)
                qh, doh = qv[:, hs], dov[:, hs]
                p, ps = _attn_probs(qh, k3, mask, sk[:, h:h + 1])
                dp = _dot(v3, doh, _NT)
                delta = jnp.sum(p * dp, axis=0, keepdims=True)
                dsc = (p * (dp - delta)).astype(BF16)
                dq_ref[:, hs] = (_dot(dsc, k3, _TN) * scale).astype(dq_ref.dtype)
                dk3 = dk3 + _dot(dsc, qh)
                dv3 = dv3 + _dot(p.astype(BF16), doh)
                ds_acc = ds_acc - jnp.sum(ps * delta) * (lane == h).astype(F32)
            for t, start in enumerate((0, pl.multiple_of(prev * T, T), pl.multiple_of(j * T, T))):
                rsl = pl.ds(start, T)
                dkv_ref[rsl, ksl] += dk3[t * T:(t + 1) * T, :]
                dkv_ref[rsl, vsl] += dv3[t * T:(t + 1) * T, :]
        ds_ref[...] += ds_acc

    blk = pl.BlockSpec((T, D_MODEL), lambda j: (j, 0))
    full = pl.BlockSpec((rows, 2 * D_KV), lambda j: (0, 0))
    return _call(
        body, name, [S((rows, D_MODEL), BF16), S((rows, 2 * D_KV), F32), S((1, LANE), F32)], (rows // T,),
        [blk, full, pl.BlockSpec((1, N_Q_HEADS), lambda j: (0, 0)), blk],
        [blk, full, pl.BlockSpec((1, LANE), lambda j: (0, 0))], ("arbitrary",), (q, kv, sinks, do), comm=comm)


BLOCK_BYTES = 1 << 20


def _div_tile(rows, cols):
    cap = max(16, BLOCK_BYTES // (4 * cols))
    best = None
    for t in range(16, min(rows, cap) + 1, 16):
        if rows % t == 0:
            best = t
    return best if best is not None else rows


def _adamw(parts, w, m, v, name):
    n, rows, cols = parts.shape
    tr = _div_tile(rows, cols)
    c1 = 1.0 / (1.0 - B1 ** STEP)
    c2 = 1.0 / (1.0 - B2 ** STEP)

    def body(p_ref, w_ref, m_ref, v_ref, g_ref, d_ref, nm_ref, nv_ref):
        g = p_ref[0].astype(F32)
        for i in range(1, n):
            g = g + p_ref[i].astype(F32)
        nm = B1 * m_ref[...] + (1.0 - B1) * g
        nv = B2 * v_ref[...] + (1.0 - B2) * (g * g)
        g_ref[...] = g
        nm_ref[...] = nm
        nv_ref[...] = nv
        d_ref[...] = -LR * ((nm * c1) / (jnp.sqrt(nv * c2) + EPS) + WD * w_ref[...])

    row = pl.BlockSpec((tr, cols), lambda i: (i, 0))
    return pl.pallas_call(
        body, name=name, out_shape=[S((rows, cols), F32)] * 4, grid=(rows // tr,),
        in_specs=[pl.BlockSpec((n, tr, cols), lambda i: (0, i, 0)), row, row, row], out_specs=[row] * 4,
        compiler_params=_cp(("parallel",)))(parts, w, m, v)


def _col_segments(ws, runs):
    segs = []
    for glo, mlo, n in runs:
        while n > 0:
            d, off = divmod(glo, ws)
            take = min(n, ws - off)
            segs.append((d, off, mlo, take))
            glo, mlo, n = glo + take, mlo + take, n - take
    return segs


def _assemble_cols(g, width, segs, name):
    _, rows, ws = g.shape
    rb = _div_tile(rows, width // 2)

    def body(g_ref, o_ref):
        o_ref[...] = jnp.zeros_like(o_ref)
        for d, off, mlo, n in segs:
            o_ref[:, mlo:mlo + n] = g_ref[d, :, off:off + n]

    return pl.pallas_call(
        body, name=name, out_shape=S((rows, width), g.dtype), grid=(rows // rb,),
        in_specs=[pl.BlockSpec((N_DEV, rb, ws), lambda i: (0, i, 0))],
        out_specs=pl.BlockSpec((rb, width), lambda i: (i, 0)), compiler_params=_cp(("parallel",)))(g)


def _scatter_cols(dw, ws, segs, name):
    rows, width = dw.shape
    rb = _div_tile(rows, width)

    def body(w_ref, o_ref):
        for d, off, mlo, n in segs:
            o_ref[d, :, off:off + n] = w_ref[:, mlo:mlo + n].astype(o_ref.dtype)

    return pl.pallas_call(
        body, name=name, out_shape=S((N_DEV, rows, ws), BF16), grid=(rows // rb,),
        in_specs=[pl.BlockSpec((rb, width), lambda i: (i, 0))],
        out_specs=pl.BlockSpec((N_DEV, rb, ws), lambda i: (0, i, 0)), compiler_params=_cp(("parallel",)))(dw)


def _gather_comm(xs):
    n = len(xs)

    def setup(x_refs, out_refs, sems):
        send_sems, recv_sems, local_sems = sems
        mx, my, mc = lax.axis_index("x"), lax.axis_index("y"), lax.axis_index("c")
        me, sibling = (mx, my, mc), (mx, my, 1 - mc)
        chips = [(1 - mx, my), (mx, 1 - my), (1 - mx, 1 - my)]

        def blk(a, px, py, pc):
            return out_refs[a].at[4 * px + 2 * py + pc]

        def copy(a, k, block, to, src=None):
            return pltpu.make_async_remote_copy(
                src_ref=blk(a, *block) if src is None else src, dst_ref=blk(a, *block),
                send_sem=send_sems.at[a, k], recv_sem=recv_sems.at[a, k], device_id=to, device_id_type=_MESH)

        mine = [pltpu.make_async_copy(x_refs[a], blk(a, *me), local_sems.at[a]) for a in range(n)]
        own = []
        for a in range(n):
            own.append(copy(a, 0, me, sibling, src=x_refs[a]))
            own += [copy(a, 1 + i, me, (*chip, mc), src=x_refs[a]) for i, chip in enumerate(chips)]
        return me, sibling, chips, mc, copy, mine, own

    def first(x_refs, out_refs, sems):
        _, _, _, _, _, mine, own = setup(x_refs, out_refs, sems)
        for cp in mine + own:
            cp.start()

    def last(x_refs, out_refs, sems):
        me, sibling, chips, mc, copy, mine, own = setup(x_refs, out_refs, sems)
        passed = []
        for a in range(n):
            for i, chip in enumerate(chips):
                copy(a, 1 + i, (*chip, mc), me).wait_recv()
                passed.append(copy(a, 4 + i, (*chip, mc), sibling))
                passed[-1].start()
        for a in range(n):
            copy(a, 0, sibling, me).wait_recv()
            for i, chip in enumerate(chips):
                copy(a, 4 + i, (*chip, 1 - mc), me).wait_recv()
        for cp in own + passed:
            cp.wait_send()
        for cp in mine:
            cp.wait()

    return _Comm(list(xs), [S((N_DEV,) + x.shape, x.dtype) for x in xs],
                 [pltpu.SemaphoreType.DMA((n, 7)), pltpu.SemaphoreType.DMA((n, 7)), pltpu.SemaphoreType.DMA((n,))],
                 first, last)


def _scatter_comm(gs):
    n = len(gs)

    def copies(g_refs, out_refs, sems):
        send_sems, recv_sems, local_sems = sems
        mx, my, mc = lax.axis_index("x"), lax.axis_index("y"), lax.axis_index("c")
        me = 4 * mx + 2 * my + mc
        mine = [pltpu.make_async_copy(g_refs[a].at[me], out_refs[a].at[me], local_sems.at[a]) for a in range(n)]
        cps = []
        for a in range(n):
            for k in range(1, N_DEV):
                px, py, pc = mx ^ (k >> 2), my ^ ((k >> 1) & 1), mc ^ (k & 1)
                cps.append(pltpu.make_async_remote_copy(
                    src_ref=g_refs[a].at[4 * px + 2 * py + pc], dst_ref=out_refs[a].at[me],
                    send_sem=send_sems.at[a, k - 1], recv_sem=recv_sems.at[a, k - 1],
                    device_id=(px, py, pc), device_id_type=_MESH))
        return mine + cps

    def first(g_refs, out_refs, sems):
        for cp in copies(g_refs, out_refs, sems):
            cp.start()

    def last(g_refs, out_refs, sems):
        for cp in copies(g_refs, out_refs, sems):
            cp.wait()

    return _Comm(list(gs), [S(g.shape, g.dtype) for g in gs],
                 [pltpu.SemaphoreType.DMA((n, 7)), pltpu.SemaphoreType.DMA((n, 7)), pltpu.SemaphoreType.DMA((n,))],
                 first, last)


def _run_comm(comm, name):
    ci, co = len(comm.ins), len(comm.out_shapes)

    def body(*refs):
        comm.first(refs[:ci], refs[ci:ci + co], refs[ci + co:])
        comm.last(refs[:ci], refs[ci:ci + co], refs[ci + co:])

    return pl.pallas_call(body, name=name, out_shape=list(comm.out_shapes), in_specs=[_HBM] * ci,
                          out_specs=[_HBM] * co, scratch_shapes=list(comm.scratch))(*comm.ins)


def _flat_rows(n_elems, mult):
    rows = -(-n_elems // LANE)
    return -(-rows // mult) * mult


def _pack(arrs, lead, mult, dtype):
    lead_shape = arrs[0].shape[:lead]
    flat = jnp.concatenate([a.astype(dtype).reshape(lead_shape + (-1,)) for a in arrs], axis=-1)
    n = flat.shape[-1]
    rows = _flat_rows(n, mult)
    flat = jnp.pad(flat, [(0, 0)] * lead + [(0, rows * LANE - n)])
    return flat.reshape(lead_shape + (rows, LANE))


def _unpack(flat, lead, shapes):
    lead_shape = flat.shape[:lead]
    flat = flat.reshape(lead_shape + (-1,))
    out, off = [], 0
    for shp in shapes:
        n = math.prod(shp)
        out.append(flat[..., off:off + n].reshape(lead_shape + tuple(shp)))
        off += n
    return out


def _split8(full, ax, n):
    shp = full.shape
    return jnp.moveaxis(full.reshape(shp[:ax] + (N_DEV, n) + shp[ax + 1:]), ax, 0)


def _join8(g, ax):
    shp = g.shape[1:]
    return jnp.moveaxis(g, 0, ax).reshape(shp[:ax] + (N_DEV * shp[ax],) + shp[ax + 1:])


def _group_lanes(v, hg):
    v = v.reshape(SSM_GROUPS, hg)
    return jnp.pad(v, ((0, 0), (0, LANE - hg))).reshape(1, SSM_GROUPS * LANE)


def _ungroup_lanes(v, hg):
    return v.reshape(SSM_GROUPS, LANE)[:, :hg].reshape(1, SSM_GROUPS * hg)


def kernel(x, meta_tokens, a_norm_pre, a_w_in, a_conv_w, a_conv_b, a_dt_bias, a_a_log, a_d_skip, a_gate_norm, a_w_out, a_norm_post, kv_norm, w_kv, b_norm_pre, b_w_q, b_sinks, b_w_o, b_norm_post, f_norm_pre, f_w_up, f_conv_w, f_conv_b, f_w_down, f_norm_post, loss_target, m_meta_tokens, m_a_norm_pre, m_a_w_in, m_a_conv_w, m_a_conv_b, m_a_dt_bias, m_a_a_log, m_a_d_skip, m_a_gate_norm, m_a_w_out, m_a_norm_post, m_kv_norm, m_w_kv, m_b_norm_pre, m_b_w_q, m_b_sinks, m_b_w_o, m_b_norm_post, m_f_norm_pre, m_f_w_up, m_f_conv_w, m_f_conv_b, m_f_w_down, m_f_norm_post, v_meta_tokens, v_a_norm_pre, v_a_w_in, v_a_conv_w, v_a_conv_b, v_a_dt_bias, v_a_a_log, v_a_d_skip, v_a_gate_norm, v_a_w_out, v_a_norm_post, v_kv_norm, v_w_kv, v_b_norm_pre, v_b_w_q, v_b_sinks, v_b_w_o, v_b_norm_post, v_f_norm_pre, v_f_w_up, v_f_conv_w, v_f_conv_b, v_f_w_down, v_f_norm_post):
    args = locals()
    wts = {n: args[n] for n in WEIGHTS}
    mom = {n: args["m_" + n] for n in WEIGHTS}
    var = {n: args["v_" + n] for n in WEIGHTS}
    mx, my, mc = lax.axis_index("x"), lax.axis_index("y"), lax.axis_index("c")
    me = 4 * mx + 2 * my + mc
    rows = _seq_rows()
    hg = SSM_HEADS // SSM_GROUPS
    d = D_MODEL

    n_main = D_INNER + D_XBC
    ws_in, ws_up = a_w_in.shape[2], f_w_up.shape[2]
    segs_in = _col_segments(ws_in, [(0, 0, n_main)] + [(n_main + hg * g, n_main + LANE * g, hg)
                                                      for g in range(SSM_GROUPS)])
    segs_up = _col_segments(ws_up, [(0, 0, 2 * D_FF)])
    def gather_of(*ws):
        return _gather_comm([w.astype(BF16) for w in ws])

    g_in, small_full = _run_comm(_gather_comm([a_w_in[0].astype(BF16), _pack([wts[n] for n in SMALL], 0, 8, F32)]),
                                 "gather_first")
    full = {}
    for n, g in zip(SMALL, _unpack(small_full, 1, [wts[n].shape for n in SMALL])):
        full[n] = _join8(g, SHARD_AXIS[n])
    w_in_all = _assemble_cols(g_in, n_main + SSM_GROUPS * LANE, segs_in, "asm_w_in")
    w_up, w_down = [None, None], [None, None]
    bias_g = _group_lanes(wts["a_dt_bias"], hg)
    alog_g = _group_lanes(wts["a_a_log"], hg)
    dsk_g = _group_lanes(wts["a_d_skip"], hg)
    a_conv_w, a_conv_b = full["a_conv_w"][0], full["a_conv_b"]
    f_cw, f_cb = full["f_conv_w"], wts["f_conv_b"]
    fpre, fpost = wts["f_norm_pre"], wts["f_norm_post"]

    pad_rows = rows - N_META - SEQ
    h0 = jnp.concatenate([full["meta_tokens"], x[0], jnp.zeros((pad_rows, d), F32)], axis=0)
    tgt = jnp.pad(loss_target[0], ((N_META, pad_rows), (0, 0)))

    _, (hn0,) = _resid_norm(h0, None, None, [full["a_norm_pre"]], "norm_a_pre")
    zx, (g_out,) = _mm(hn0, w_in_all, "nn", F32, "mm_in", comm=gather_of(a_w_out[0]))
    w_out = g_out.reshape(D_INNER, d)
    xbc, (g_kv, g_q, g_o) = _conv_silu_fwd(zx, a_conv_w, a_conv_b, "conv_a", comm=gather_of(w_kv, b_w_q[0], b_w_o[0]))
    w_kvf, w_q, w_o = g_kv.reshape(d, 2 * D_KV), g_q.reshape(d, d), g_o.reshape(d, d)
    (y_ssd, hst), (g_up0, g_dn0, g_dn1) = _ssd_fwd(xbc, zx, bias_g, alog_g, dsk_g, "ssd_fwd",
                                                   comm=gather_of(f_w_up[0], f_w_down[0], f_w_down[1]))
    w_up[0] = _assemble_cols(g_up0, 2 * D_FF, segs_up, "asm_w_up0")
    w_down = [g_dn0.reshape(D_FF, d), g_dn1.reshape(D_FF, d)]
    yn = _gatenorm_fwd(y_ssd, zx, full["a_gate_norm"], "gatenorm")
    mix_a = _mm(yn, w_out, "nn", F32, "mm_out")
    h1, (fn0,) = _resid_norm(h0, mix_a, full["a_norm_post"], [fpre[0:1]], "resid_a")

    half = d // 2
    u0, (g_up1a,) = _mm(fn0, w_up[0], "nn", F32, "mm_up0", comm=gather_of(f_w_up[1, :half]))
    act0, (g_up1b,) = _ffn_act_fwd(u0, f_cw[0], f_cb[0:1], "ffn_act0", comm=gather_of(f_w_up[1, half:]))
    w_up[1] = jnp.concatenate([_assemble_cols(g_up1a, 2 * D_FF, segs_up, "asm_w_up1a"),
                               _assemble_cols(g_up1b, 2 * D_FF, segs_up, "asm_w_up1b")], axis=0)
    ffn0 = _mm(act0, w_down[0], "nn", F32, "mm_down0")
    h2, (kvn, bn) = _resid_norm(h1, ffn0, fpost[0:1], [wts["kv_norm"].reshape(1, d), wts["b_norm_pre"]], "resid_f0")
    kv = _mm(kvn, w_kvf, "nn", F32, "mm_kv")
    q = _mm(bn, w_q, "nn", F32, "mm_q")
    o = _attn_fwd(q, kv, wts["b_sinks"], "attn_fwd")
    mix_b = _mm(o, w_o, "nn", F32, "mm_o")
    h3, (fn1,) = _resid_norm(h2, mix_b, wts["b_norm_post"], [fpre[1:2]], "resid_b")
    u1 = _mm(fn1, w_up[1], "nn", F32, "mm_up1")
    act1 = _ffn_act_fwd(u1, f_cw[1], f_cb[1:2], "ffn_act1")
    ffn1 = _mm(act1, w_down[1], "nn", F32, "mm_down1")
    dh4, loss_row = _final_loss(h3, ffn1, fpost[1:2], tgt, "loss")
    loss = lax.psum(loss_row[0, 0], ("x", "y", "c"))

    grads = {}

    def ffn_bwd(dh_out, h_in, fn, u, act, ffn, i):
        dffn, dw_post = _norm_bwd(ffn, fpost[i:i + 1], dh_out, None, BF16, f"nb_fpost{i}")
        dact = _mm(dffn, w_down[i], "nt", F32, f"mm_dact{i}")
        dw_down = _mm(act, dffn, "tn", BF16, f"mm_dwdown{i}").reshape(N_DEV, -1, d)
        dg, dv, dwg, dwv, dbg, dbv = _ffn_act_bwd(u, dact, f_cw[i], f_cb[i:i + 1], f"ffn_act_bwd{i}")
        du = jnp.concatenate([dg, dv], axis=1)
        dfn = _mm(du, w_up[i], "nt", F32, f"mm_dfn{i}")
        dw_up = _scatter_cols(_mm(fn, du, "tn", BF16, f"mm_dwup{i}"), ws_up, segs_up, f"scat_w_up{i}")
        dh_in, dw_pre = _norm_bwd(h_in, fpre[i:i + 1], dfn, dh_out, F32, f"nb_fpre{i}")
        return dh_in, dict(post=dw_post, down=dw_down, cw=jnp.concatenate([dwg, dwv], axis=1),
                           cb=jnp.concatenate([dbg, dbv], axis=1), up=dw_up, pre=dw_pre)

    dh3, gf1 = ffn_bwd(dh4, h3, fn1, u1, act1, ffn1, 1)
    dmix_b, grads["b_norm_post"] = _norm_bwd(mix_b, wts["b_norm_post"], dh3, None, BF16, "nb_bpost")
    do = _mm(dmix_b, w_o, "nt", F32, "mm_do")
    dw_o = _mm(o, dmix_b, "tn", BF16, "mm_dwo").reshape(N_DEV, -1, d)
    (dq, dkv, dsinks), (p_o,) = _attn_bwd(q, kv, wts["b_sinks"], do, "attn_bwd", comm=_scatter_comm([dw_o]))
    grads["b_sinks"] = dsinks[:, :N_Q_HEADS]
    dbn = _mm(dq, w_q, "nt", F32, "mm_dbn")
    dw_q = _mm(bn, dq, "tn", BF16, "mm_dwq").reshape(N_DEV, -1, d)
    dkv16 = dkv.astype(BF16)
    dkvn = _mm(dkv16, w_kvf, "nt", F32, "mm_dkvn")
    dw_kv = _mm(kvn, dkv16, "tn", BF16, "mm_dwkv").reshape(N_DEV, -1, 2 * D_KV)
    dh2, grads["b_norm_pre"] = _norm_bwd(h2, wts["b_norm_pre"], dbn, dh3, F32, "nb_bpre")
    dh2, dw_kvn = _norm_bwd(h2, wts["kv_norm"].reshape(1, d), dkvn, dh2, F32, "nb_kv")
    grads["kv_norm"] = dw_kvn.reshape(d)
    dh1, gf0 = ffn_bwd(dh2, h1, fn0, u0, act0, ffn0, 0)
    grads["f_norm_post"] = jnp.concatenate([gf0["post"], gf1["post"]], axis=0)
    grads["f_norm_pre"] = jnp.concatenate([gf0["pre"], gf1["pre"]], axis=0)
    grads["f_conv_w"] = jnp.stack([gf0["cw"], gf1["cw"]])
    grads["f_conv_b"] = jnp.concatenate([gf0["cb"], gf1["cb"]], axis=0)

    dmix_a, grads["a_norm_post"] = _norm_bwd(mix_a, full["a_norm_post"], dh1, None, BF16, "nb_apost")
    dyn = _mm(dmix_a, w_out, "nt", F32, "mm_dyn")
    dw_out = _mm(yn, dmix_a, "tn", BF16, "mm_dwout").reshape(N_DEV, -1, d)
    dy_ssd, dz, grads["a_gate_norm"] = _gatenorm_bwd(y_ssd, zx, full["a_gate_norm"], dyn, "gatenorm_bwd")
    (dxs, dbm, dcm, ddtp, dalog, ddsk, dbias), (p_dn1, p_up1, p_q, p_kv, p_dn0, p_up0, p_out) = _ssd_bwd(
        xbc, zx, bias_g, alog_g, dsk_g, dy_ssd, hst, "ssd_bwd",
        comm=_scatter_comm([gf1["down"], gf1["up"], dw_q, dw_kv, gf0["down"], gf0["up"], dw_out]))
    dxbc = jnp.concatenate([dxs, dbm, dcm], axis=1)
    grads["a_a_log"] = _ungroup_lanes(dalog, hg)
    grads["a_d_skip"] = _ungroup_lanes(ddsk, hg)
    grads["a_dt_bias"] = _ungroup_lanes(dbias, hg)
    dpre, dcw, dcb = _conv_silu_bwd(zx, dxbc, a_conv_w, a_conv_b, "conv_a_bwd")
    grads["a_conv_w"], grads["a_conv_b"] = dcw[None], dcb
    dzx = jnp.concatenate([dz, dpre, ddtp.astype(BF16)], axis=1)
    dhn0 = _mm(dzx, w_in_all, "nt", F32, "mm_dhn0")
    dw_in_all = _mm(hn0, dzx, "tn", BF16, "mm_dwin")
    dh0, grads["a_norm_pre"] = _norm_bwd(h0, full["a_norm_pre"], dhn0, dh1, F32, "nb_apre")
    grad_x = dh0[N_META:N_META + SEQ][None]
    grads["meta_tokens"] = dh0[:N_META]

    small_local = _pack([_split8(grads[n], SHARD_AXIS[n], wts[n].shape[SHARD_AXIS[n]]) for n in SMALL], 1, 8, F32)
    repl_local = _pack([grads[n] for n in REPL], 0, 8, F32)
    n_sr = small_local.shape[1]
    p_in, = _run_comm(_scatter_comm([_scatter_cols(dw_in_all, ws_in, segs_in, "scat_w_in")]), "rs_w_in")
    small_all, = _run_comm(
        _gather_comm([jnp.concatenate([small_local.reshape(N_DEV * n_sr, LANE), repl_local], axis=0)]),
        "gather_small_grads")
    parts_big = dict(a_w_in=p_in, a_w_out=p_out, w_kv=p_kv, b_w_q=p_q, b_w_o=p_o,
                     f_w_up=jnp.concatenate([p_up0, p_up1], axis=1), f_w_down=jnp.concatenate([p_dn0, p_dn1], axis=1))
    mine_small = lax.dynamic_slice_in_dim(small_all, me * n_sr, n_sr, axis=1)
    parts_small = jnp.concatenate([mine_small, small_all[:, N_DEV * n_sr:]], axis=1)

    def flat_f32(dct, names, mult):
        return _pack([dct[n] for n in names], 0, mult, F32)

    big_out = {}
    for n in BIG:
        cols = wts[n].shape[-1]
        res = _adamw(parts_big[n], *[dct[n].reshape(-1, cols) for dct in (wts, mom, var)], f"adamw_{n}")
        big_out[n] = [r.reshape(wts[n].shape) for r in res]
    sm_in = [jnp.concatenate([flat_f32(dct, SMALL, 8), flat_f32(dct, REPL, 8)], axis=0) for dct in (wts, mom, var)]
    small_out = _adamw(parts_small, *sm_in, "adamw_small")

    outs = []
    for kind in range(4):
        res = {n: big_out[n][kind] for n in BIG}
        for n, a in zip(SMALL, _unpack(small_out[kind][:n_sr], 0, [wts[n].shape for n in SMALL])):
            res[n] = a
        for n, a in zip(REPL, _unpack(small_out[kind][n_sr:], 0, [wts[n].shape for n in REPL])):
            res[n] = a
        outs.append(res)
    return (loss, grad_x, *[outs[0][n] for n in WEIGHTS], *[outs[1][n] for n in WEIGHTS],
            *[outs[2][n] for n in WEIGHTS], *[outs[3][n] for n in WEIGHTS])
```

```python
import functools
import math

import jax
import jax.numpy as jnp
from jax import lax
from jax.experimental import pallas as pl
from jax.experimental.pallas import tpu as pltpu

F32, BF16 = jnp.float32, jnp.bfloat16
S = jax.ShapeDtypeStruct

D_MODEL = 1024
SEQ = 2048
N_META = 16
D_INNER = 2048
HEAD_P = 64
SSM_HEADS = D_INNER // HEAD_P
SSM_GROUPS = 4
D_STATE = 128
SSM_CONV = 4
D_BC = SSM_GROUPS * D_STATE
D_XBC = D_INNER + 2 * D_BC
ATTN_DH = 64
N_Q_HEADS = D_MODEL // ATTN_DH
N_KV_HEADS = 4
D_KV = N_KV_HEADS * ATTN_DH
WINDOW = 128
D_FF = 2816
FFN_CONV = 3
RMS_EPS = 1e-6
NEG = -1e30
LR, B1, B2, EPS, WD, STEP = 0.001, 0.9, 0.999, 1e-08, 0.01, 10

N_DEV = 8
T = 128
LANE = 128
VMEM_LIMIT = 48 * 1024 * 1024

BIG = ("a_w_in", "a_w_out", "w_kv", "b_w_q", "b_w_o", "f_w_up", "f_w_down")
SMALL = ("meta_tokens", "a_norm_pre", "a_conv_w", "a_conv_b", "a_gate_norm", "a_norm_post", "f_conv_w")
REPL = ("a_dt_bias", "a_a_log", "a_d_skip", "kv_norm", "b_norm_pre", "b_sinks", "b_norm_post",
        "f_norm_pre", "f_conv_b", "f_norm_post")
SHARD_AXIS = dict(a_w_in=2, a_w_out=1, w_kv=0, b_w_q=1, b_w_o=1, f_w_up=2, f_w_down=1, meta_tokens=1,
                  a_norm_pre=1, a_conv_w=2, a_conv_b=1, a_gate_norm=1, a_norm_post=1, f_conv_w=2)
WEIGHTS = ("meta_tokens", "a_norm_pre", "a_w_in", "a_conv_w", "a_conv_b", "a_dt_bias", "a_a_log", "a_d_skip",
           "a_gate_norm", "a_w_out", "a_norm_post", "kv_norm", "w_kv", "b_norm_pre", "b_w_q", "b_sinks", "b_w_o",
           "b_norm_post", "f_norm_pre", "f_w_up", "f_conv_w", "f_conv_b", "f_w_down", "f_norm_post")


def _seq_rows():
    return -(-(N_META + SEQ) // T) * T


def _cp(sem=None):
    return pltpu.CompilerParams(dimension_semantics=sem, vmem_limit_bytes=VMEM_LIMIT)


def _pick(n, target):
    t = min(n, target)
    t -= t % LANE
    while n % t:
        t -= LANE
    return t


def _sigmoid(x):
    return 1.0 / (1.0 + jnp.exp(-x))


def _softplus(x):
    return jnp.maximum(x, 0.0) + jnp.log(1.0 + jnp.exp(-jnp.abs(x)))


_NN = (((1,), (0,)), ((), ()))
_NT = (((1,), (1,)), ((), ()))
_TN = (((0,), (0,)), ((), ()))


def _dot(a, b, dims=_NN):
    return lax.dot_general(a, b, dims, preferred_element_type=F32)


def _dot_hi(a, b):
    return lax.dot_general(a, b, _NN, precision=lax.Precision.HIGHEST, preferred_element_type=F32)


_HBM = pl.BlockSpec(memory_space=pltpu.HBM)
_MESH = pl.DeviceIdType.MESH


class _Comm:
    def __init__(self, ins, out_shapes, scratch, first, last):
        self.ins, self.out_shapes, self.scratch, self.first, self.last = ins, out_shapes, scratch, first, last


def _call(body, name, out_shape, grid, in_specs, out_specs, sem, args, scratch=(), comm=None):
    if comm is None:
        return pl.pallas_call(body, name=name, out_shape=out_shape, grid=grid, in_specs=in_specs, out_specs=out_specs,
                              scratch_shapes=list(scratch), compiler_params=_cp(sem))(*args)
    single = not isinstance(out_shape, (list, tuple))
    outs = [out_shape] if single else list(out_shape)
    ospecs = [out_specs] if single else list(out_specs)
    n_in, n_out, n_scr, ci, co = len(in_specs), len(outs), len(scratch), len(comm.ins), len(comm.out_shapes)

    def carrier(*refs):
        p = 0
        parts = []
        for cnt in (n_in, ci, n_out, co, n_scr, len(comm.scratch)):
            parts.append(refs[p:p + cnt])
            p += cnt
        ins, cins, outs_r, couts, scr, cscr = parts
        ids = [pl.program_id(i) for i in range(len(grid))]
        first, last = ids[0] == 0, ids[0] == grid[0] - 1
        for i in range(1, len(grid)):
            first, last = first & (ids[i] == 0), last & (ids[i] == grid[i] - 1)

        @pl.when(first)
        def _():
            comm.first(cins, couts, cscr)

        body(*ins, *outs_r, *scr)

        @pl.when(last)
        def _():
            comm.last(cins, couts, cscr)

    res = pl.pallas_call(
        carrier, name=name, out_shape=outs + list(comm.out_shapes), grid=grid,
        in_specs=list(in_specs) + [_HBM] * ci, out_specs=ospecs + [_HBM] * co,
        scratch_shapes=list(scratch) + list(comm.scratch),
        compiler_params=_cp(("arbitrary",) * len(grid)))(*args, *comm.ins)
    mine = res[0] if single else list(res[:n_out])
    return mine, list(res[n_out:])


def _mm(a, b, mode, out_dtype, name, comm=None):
    if mode == "tn":
        m, kk = a.shape
        n = b.shape[1]
        tko, tn = _pick(kk, 512), _pick(n, 512)

        def body(a_ref, b_ref, o_ref):
            o_ref[...] = _dot(a_ref[...], b_ref[...], _TN).astype(o_ref.dtype)

        return _call(
            body, name, S((kk, n), out_dtype), (kk // tko, n // tn),
            [pl.BlockSpec((m, tko), lambda i, j: (0, i)), pl.BlockSpec((m, tn), lambda i, j: (0, j))],
            pl.BlockSpec((tko, tn), lambda i, j: (i, j)), ("parallel", "parallel"), (a, b), comm=comm)

    m, kk = a.shape
    n = b.shape[1] if mode == "nn" else b.shape[0]
    tn = _pick(n, 512)
    tk = kk if kk <= 2048 else _pick(kk, 1536)
    nk = kk // tk
    dims = _NN if mode == "nn" else _NT

    def body(a_ref, b_ref, o_ref, *acc):
        part = _dot(a_ref[...], b_ref[...], dims)
        if nk == 1:
            o_ref[...] = part.astype(o_ref.dtype)
        else:
            k = pl.program_id(1)

            @pl.when(k == 0)
            def _():
                acc[0][...] = part

            @pl.when(k > 0)
            def _():
                acc[0][...] += part

            @pl.when(k == nk - 1)
            def _():
                o_ref[...] = acc[0][...].astype(o_ref.dtype)

    b_spec = (pl.BlockSpec((tk, tn), lambda j, k: (k, j)) if mode == "nn"
              else pl.BlockSpec((tn, tk), lambda j, k: (j, k)))
    return _call(
        body, name, S((m, n), out_dtype), (n // tn, nk), [pl.BlockSpec((m, tk), lambda j, k: (0, k)), b_spec],
        pl.BlockSpec((m, tn), lambda j, k: (0, j)), ("parallel", "arbitrary"), (a, b),
        scratch=[pltpu.VMEM((m, tn), F32)] if nk > 1 else [], comm=comm)


def _rms(x, w):
    return x * lax.rsqrt(jnp.mean(x * x, axis=-1, keepdims=True) + RMS_EPS) * w


def _row_tile(rows):
    return rows // 8


def _resid_norm(h, br, w_post, next_ws, name):
    rows, d = h.shape
    tr = _row_tile(rows)
    has_br = br is not None
    nw = len(next_ws)

    def body(*refs):
        h_ref = refs[0]
        pos = 1
        x = h_ref[...]
        if has_br:
            x = x + _rms(refs[1][...], refs[2][...])
            pos = 3
        w_refs = refs[pos:pos + nw]
        outs = refs[pos + nw:]
        if has_br:
            outs[0][...] = x
            outs = outs[1:]
        for w_ref, o_ref in zip(w_refs, outs):
            o_ref[...] = _rms(x, w_ref[...]).astype(o_ref.dtype)

    row = pl.BlockSpec((tr, d), lambda i: (i, 0))
    vec = pl.BlockSpec((1, d), lambda i: (0, 0))
    ins = [h] + ([br, w_post] if has_br else []) + list(next_ws)
    in_specs = [row] + ([row, vec] if has_br else []) + [vec] * nw
    out_shape = ([S((rows, d), F32)] if has_br else []) + [S((rows, d), BF16)] * nw
    res = pl.pallas_call(body, name=name, out_shape=out_shape, grid=(rows // tr,), in_specs=in_specs,
                         out_specs=[row] * len(out_shape), compiler_params=_cp(("parallel",)))(*ins)
    if has_br:
        return res[0], list(res[1:])
    return h, list(res)


def _norm_bwd(x, w, dy, add, out_dtype, name, comm=None):
    rows, d = x.shape
    tr = _row_tile(rows)
    has_add = add is not None

    def body(*refs):
        x_ref, w_ref, dy_ref = refs[:3]
        dx_ref, dw_ref = refs[-2:]
        xv = x_ref[...]
        r = lax.rsqrt(jnp.mean(xv * xv, axis=-1, keepdims=True) + RMS_EPS)
        dyv = dy_ref[...].astype(F32)
        wdy = dyv * w_ref[...]
        dx = r * wdy - xv * (r * r * r) * jnp.mean(xv * wdy, axis=-1, keepdims=True)
        if has_add:
            dx = dx + refs[3][...]
        dx_ref[...] = dx.astype(dx_ref.dtype)

        @pl.when(pl.program_id(0) == 0)
        def _():
            dw_ref[...] = jnp.zeros_like(dw_ref)

        dw_ref[...] += jnp.sum(dyv * xv * r, axis=0, keepdims=True)

    row = pl.BlockSpec((tr, d), lambda i: (i, 0))
    vec = pl.BlockSpec((1, d), lambda i: (0, 0))
    ins = [x, w, dy] + ([add] if has_add else [])
    return _call(body, name, [S((rows, d), out_dtype), S((1, d), F32)], (rows // tr,),
                 [row, vec, row] + ([row] if has_add else []), [row, vec], ("arbitrary",), ins, comm=comm)


def _final_loss(h, br, w_post, tgt, name):
    rows, d = h.shape
    tr = _row_tile(rows)

    def body(h_ref, br_ref, w_ref, t_ref, dh_ref, loss_ref):
        i = pl.program_id(0)
        y = h_ref[...] + _rms(br_ref[...], w_ref[...])
        r = i * tr + lax.broadcasted_iota(jnp.int32, (tr, 1), 0)
        real = (r >= N_META) & (r < N_META + SEQ)
        diff = jnp.where(real, y - t_ref[...], 0.0)
        dh_ref[...] = diff * (1.0 / d)

        @pl.when(i == 0)
        def _():
            loss_ref[...] = jnp.zeros_like(loss_ref)

        loss_ref[...] += jnp.sum(diff * diff) * (0.5 / d)

    row = pl.BlockSpec((tr, d), lambda i: (i, 0))
    return pl.pallas_call(body, name=name, out_shape=[S((rows, d), F32), S((1, LANE), F32)], grid=(rows // tr,),
                          in_specs=[row, row, pl.BlockSpec((1, d), lambda i: (0, 0)), row],
                          out_specs=[row, pl.BlockSpec((1, LANE), lambda i: (0, 0))],
                          compiler_params=_cp(("arbitrary",)))(h, br, w_post, tgt)


def _gatenorm_fwd(y, zx, w, name):
    rows, d = y.shape
    tr = _row_tile(rows)

    def body(y_ref, z_ref, w_ref, o_ref):
        z = z_ref[...]
        o_ref[...] = _rms(y_ref[...] * z * _sigmoid(z), w_ref[...]).astype(o_ref.dtype)

    row = pl.BlockSpec((tr, d), lambda i: (i, 0))
    return pl.pallas_call(body, name=name, out_shape=S((rows, d), BF16), grid=(rows // tr,),
                          in_specs=[row, row, pl.BlockSpec((1, d), lambda i: (0, 0))], out_specs=row,
                          compiler_params=_cp(("parallel",)))(y, zx, w)


def _gatenorm_bwd(y, zx, w, dyn, name, comm=None):
    rows, d = y.shape
    tr = _row_tile(rows)

    def body(y_ref, z_ref, w_ref, dyn_ref, dy_ref, dz_ref, dw_ref):
        yv, z = y_ref[...], z_ref[...]
        sg = _sigmoid(z)
        sz = z * sg
        g = yv * sz
        r = lax.rsqrt(jnp.mean(g * g, axis=-1, keepdims=True) + RMS_EPS)
        dyn_v = dyn_ref[...]
        wdy = dyn_v * w_ref[...]
        dg = r * wdy - g * (r * r * r) * jnp.mean(g * wdy, axis=-1, keepdims=True)
        dy_ref[...] = dg * sz
        dz_ref[...] = (dg * yv * sg * (1.0 + z * (1.0 - sg))).astype(dz_ref.dtype)

        @pl.when(pl.program_id(0) == 0)
        def _():
            dw_ref[...] = jnp.zeros_like(dw_ref)

        dw_ref[...] += jnp.sum(dyn_v * g * r, axis=0, keepdims=True)

    row = pl.BlockSpec((tr, d), lambda i: (i, 0))
    vec = pl.BlockSpec((1, d), lambda i: (0, 0))
    return _call(body, name, [S((rows, d), F32), S((rows, d), BF16), S((1, d), F32)], (rows // tr,),
                 [row, row, vec, row], [row, row, vec], ("arbitrary",), (y, zx, w, dyn), comm=comm)


def _shift_down(x, s, rows_iota):
    if s == 0:
        return x
    return jnp.where(rows_iota >= s, pltpu.roll(x, s, 0), 0.0)


def _shift_up(x, s, rows_iota):
    if s == 0:
        return x
    rows = x.shape[0]
    return jnp.where(rows_iota < rows - s, pltpu.roll(x, rows - s, 0), 0.0)


def _r16(v):
    return v.astype(BF16).astype(F32)


def _conv(x, w_ref, b_ref, taps, rows_iota):
    x = _r16(x)
    acc = jnp.zeros_like(x)
    for k in range(taps):
        acc = acc + _r16(w_ref[k:k + 1, :]) * _shift_down(x, taps - 1 - k, rows_iota)
    return acc + b_ref[...]


def _conv_bwd(x, du, w_ref, dw_ref, db_ref, taps, rows_iota):
    db_ref[...] = jnp.sum(du, axis=0, keepdims=True)
    x, du = _r16(x), _r16(du)
    dx = jnp.zeros_like(x)
    for k in range(taps):
        s = taps - 1 - k
        dx = dx + _r16(w_ref[k:k + 1, :]) * _shift_up(du, s, rows_iota)
        dw_ref[k:k + 1, :] = jnp.sum(du * _shift_down(x, s, rows_iota), axis=0, keepdims=True)
    return dx


def _conv_silu_fwd(zx, w, b, name, comm=None):
    rows = zx.shape[0]
    cb = 512
    off = D_INNER // cb

    def body(x_ref, w_ref, b_ref, o_ref):
        it = lax.broadcasted_iota(jnp.int32, (rows, 1), 0)
        u = _conv(x_ref[...], w_ref, b_ref, SSM_CONV, it)
        o_ref[...] = u * _sigmoid(u)

    return _call(
        body, name, S((rows, D_XBC), F32), (D_XBC // cb,),
        [pl.BlockSpec((rows, cb), lambda j: (0, off + j)), pl.BlockSpec((SSM_CONV, cb), lambda j: (0, j)),
         pl.BlockSpec((1, cb), lambda j: (0, j))],
        pl.BlockSpec((rows, cb), lambda j: (0, j)), ("parallel",), (zx, w, b), comm=comm)


def _conv_silu_bwd(zx, dxbc, w, b, name):
    rows = zx.shape[0]
    cb = 512
    off = D_INNER // cb

    def body(x_ref, d_ref, w_ref, b_ref, dx_ref, dw_ref, db_ref):
        it = lax.broadcasted_iota(jnp.int32, (rows, 1), 0)
        x = x_ref[...]
        u = _conv(x, w_ref, b_ref, SSM_CONV, it)
        sg = _sigmoid(u)
        du = d_ref[...] * sg * (1.0 + u * (1.0 - sg))
        dx_ref[...] = _conv_bwd(x, du, w_ref, dw_ref, db_ref, SSM_CONV, it).astype(dx_ref.dtype)

    col = pl.BlockSpec((rows, cb), lambda j: (0, j))
    wsp = pl.BlockSpec((SSM_CONV, cb), lambda j: (0, j))
    bsp = pl.BlockSpec((1, cb), lambda j: (0, j))
    return pl.pallas_call(
        body, name=name, out_shape=[S((rows, D_XBC), BF16), S((SSM_CONV, D_XBC), F32), S((1, D_XBC), F32)],
        grid=(D_XBC // cb,), in_specs=[pl.BlockSpec((rows, cb), lambda j: (0, off + j)), col, wsp, bsp],
        out_specs=[col, wsp, bsp], compiler_params=_cp(("parallel",)))(zx, dxbc, w, b)


def _ffn_act_fwd(u, w, b, name, comm=None):
    rows = u.shape[0]
    cb = 256
    nb = D_FF // cb

    def body(g_ref, v_ref, wg_ref, wv_ref, bg_ref, bv_ref, o_ref):
        it = lax.broadcasted_iota(jnp.int32, (rows, 1), 0)
        g = _conv(g_ref[...], wg_ref, bg_ref, FFN_CONV, it)
        v = _conv(v_ref[...], wv_ref, bv_ref, FFN_CONV, it)
        o_ref[...] = (g * _sigmoid(g) * v).astype(o_ref.dtype)

    def sp(r, shift):
        return pl.BlockSpec((r, cb), lambda j: (0, shift + j))

    return _call(
        body, name, S((rows, D_FF), BF16), (nb,),
        [sp(rows, 0), sp(rows, nb), sp(FFN_CONV, 0), sp(FFN_CONV, nb), sp(1, 0), sp(1, nb)],
        sp(rows, 0), ("parallel",), (u, u, w, w, b, b), comm=comm)


def _ffn_act_bwd(u, dact, w, b, name, comm=None):
    rows = u.shape[0]
    cb = 256
    nb = D_FF // cb

    def body(g_ref, v_ref, d_ref, wg_ref, wv_ref, bg_ref, bv_ref, dg_ref, dv_ref, dwg_ref, dwv_ref, dbg_ref, dbv_ref):
        it = lax.broadcasted_iota(jnp.int32, (rows, 1), 0)
        xg, xv = g_ref[...], v_ref[...]
        g = _conv(xg, wg_ref, bg_ref, FFN_CONV, it)
        v = _conv(xv, wv_ref, bv_ref, FFN_CONV, it)
        sg = _sigmoid(g)
        d = d_ref[...]
        dgate = d * v * sg * (1.0 + g * (1.0 - sg))
        dval = d * g * sg
        dg_ref[...] = _conv_bwd(xg, dgate, wg_ref, dwg_ref, dbg_ref, FFN_CONV, it).astype(dg_ref.dtype)
        dv_ref[...] = _conv_bwd(xv, dval, wv_ref, dwv_ref, dbv_ref, FFN_CONV, it).astype(dv_ref.dtype)

    def sp(r, shift):
        return pl.BlockSpec((r, cb), lambda j: (0, shift + j))

    return _call(
        body, name,
        [S((rows, D_FF), BF16), S((rows, D_FF), BF16), S((FFN_CONV, D_FF), F32), S((FFN_CONV, D_FF), F32),
         S((1, D_FF), F32), S((1, D_FF), F32)],
        (nb,),
        [sp(rows, 0), sp(rows, nb), sp(rows, 0), sp(FFN_CONV, 0), sp(FFN_CONV, nb), sp(1, 0), sp(1, nb)],
        [sp(rows, 0), sp(rows, 0), sp(FFN_CONV, 0), sp(FFN_CONV, 0), sp(1, 0), sp(1, 0)],
        ("parallel",), (u, u, dact, w, w, b, b), comm=comm)


def _ssd_consts(dtp_ref, bias_ref, alog_ref, hg):
    lane = lax.broadcasted_iota(jnp.int32, (1, LANE), 1)
    pre = dtp_ref[...] + bias_ref[...]
    dt = _softplus(pre)
    a_row = jnp.where(lane < hg, -jnp.exp(alog_ref[...]), 0.0)
    ri = lax.broadcasted_iota(jnp.int32, (T, T), 0)
    ci = lax.broadcasted_iota(jnp.int32, (T, T), 1)
    cs = _dot_hi((ri >= ci).astype(F32), dt * a_row)
    return pre, dt, a_row, cs, ri, ci, lane


def _ssd_fwd(xbc, zx, bias, alog, dsk, name, comm=None):
    rows = xbc.shape[0]
    nc = rows // T
    hg = SSM_HEADS // SSM_GROUPS
    gw = hg * HEAD_P
    xoff, boff, coff = 0, D_INNER // D_STATE, (D_INNER + D_BC) // D_STATE
    dtoff = (D_INNER + D_XBC) // LANE

    def body(x_ref, b_ref, c_ref, dtp_ref, bias_ref, alog_ref, dsk_ref, y_ref, hst_ref, hs):
        c = pl.program_id(1)

        @pl.when(c == 0)
        def _():
            hs[...] = jnp.zeros_like(hs)

        _, dt, _, cs, ri, ci, _ = _ssd_consts(dtp_ref, bias_ref, alog_ref, hg)
        cst, dtt = cs.T, dt.T
        xt = x_ref[...].T
        bb, cbf = b_ref[...].astype(BF16), c_ref[...].astype(BF16)
        gt = _dot(bb, cbf, _NT)
        causal_t = ci >= ri
        dskv = dsk_ref[...]
        hall = hs[...]
        hst_ref[0, 0] = hall
        yts, new_h = [], []
        for k in range(hg):
            sl = slice(k * HEAD_P, (k + 1) * HEAD_P)
            csc, csr = cs[:, k:k + 1], cst[k:k + 1, :]
            lt = jnp.exp(jnp.where(causal_t, csr - csc, NEG))
            xk = xt[sl, :]
            xdt = xk * dtt[k:k + 1, :]
            hk = hall[sl, :]
            yd = _dot(xdt.astype(BF16), (gt * lt).astype(BF16))
            yo = jnp.exp(csr) * _dot(hk.astype(BF16), cbf, _NT)
            yts.append(yd + yo + dskv[:, k:k + 1] * xk)
            cl = cs[T - 1:T, k:k + 1]
            st = _dot((xdt * jnp.exp(cl - csr)).astype(BF16), bb)
            new_h.append(jnp.exp(cl) * hk + st)
        y_ref[...] = jnp.concatenate(yts, axis=0).T
        hs[...] = jnp.concatenate(new_h, axis=0)

    vec = pl.BlockSpec((1, LANE), lambda g, c: (0, g))
    return _call(
        body, name, [S((rows, D_INNER), F32), S((nc, SSM_GROUPS, gw, D_STATE), F32)], (SSM_GROUPS, nc),
        [pl.BlockSpec((T, gw), lambda g, c: (c, xoff + g)),
         pl.BlockSpec((T, D_STATE), lambda g, c: (c, boff + g)),
         pl.BlockSpec((T, D_STATE), lambda g, c: (c, coff + g)),
         pl.BlockSpec((T, LANE), lambda g, c: (c, dtoff + g)), vec, vec, vec],
        [pl.BlockSpec((T, gw), lambda g, c: (c, g)), pl.BlockSpec((1, 1, gw, D_STATE), lambda g, c: (c, g, 0, 0))],
        ("parallel", "arbitrary"), (xbc, xbc, xbc, zx, bias, alog, dsk),
        scratch=[pltpu.VMEM((gw, D_STATE), F32)], comm=comm)


def _ssd_bwd(xbc, zx, bias, alog, dsk, dy, hst, name, comm=None):
    rows = xbc.shape[0]
    nc = rows // T
    hg = SSM_HEADS // SSM_GROUPS
    gw = hg * HEAD_P
    boff, coff = D_INNER // D_STATE, (D_INNER + D_BC) // D_STATE
    dtoff = (D_INNER + D_XBC) // LANE

    def body(x_ref, b_ref, c_ref, dtp_ref, bias_ref, alog_ref, dsk_ref, dy_ref, hst_ref,
             dx_ref, db_ref, dc_ref, ddtp_ref, dalog_ref, ddsk_ref, dbias_ref, dhs):
        step = pl.program_id(1)

        @pl.when(step == 0)
        def _():
            dhs[...] = jnp.zeros_like(dhs)
            dalog_ref[...] = jnp.zeros_like(dalog_ref)
            ddsk_ref[...] = jnp.zeros_like(ddsk_ref)
            dbias_ref[...] = jnp.zeros_like(dbias_ref)

        pre, dt, a_row, cs, ri, ci, lane = _ssd_consts(dtp_ref, bias_ref, alog_ref, hg)
        cst, dtt = cs.T, dt.T
        xt, dyt = x_ref[...].T, dy_ref[...].T
        bb, cbf = b_ref[...].astype(BF16), c_ref[...].astype(BF16)
        gt = _dot(bb, cbf, _NT)
        causal_t = ci >= ri
        dskv = dsk_ref[...]
        hall, dhall = hst_ref[0, 0], dhs[...]
        head_row = lax.broadcasted_iota(jnp.int32, (T, 1), 0)
        last_l = lax.broadcasted_iota(jnp.int32, (1, T), 1) == T - 1
        dgt = jnp.zeros((T, T), F32)
        dc_acc = jnp.zeros((T, D_STATE), F32)
        db_acc = jnp.zeros((T, D_STATE), F32)
        ddt_rows = jnp.zeros((T, T), F32)
        dcs_rows = jnp.zeros((T, T), F32)
        qrow_cols = jnp.zeros((T, LANE), F32)
        ddsk_acc = jnp.zeros((1, LANE), F32)
        dxts, new_dh = [], []
        for k in range(hg):
            sl = slice(k * HEAD_P, (k + 1) * HEAD_P)
            csc, csr = cs[:, k:k + 1], cst[k:k + 1, :]
            lt = jnp.exp(jnp.where(causal_t, csr - csc, NEG))
            xk, dyk = xt[sl, :], dyt[sl, :]
            dtr, dk = dtt[k:k + 1, :], dskv[:, k:k + 1]
            xdt = xk * dtr
            mpt = gt * lt
            dyb = dyk.astype(BF16)
            dxdt = _dot(dyb, mpt.astype(BF16), _NT)
            dmt = _dot(xdt.astype(BF16), dyb, _TN)
            dgt = dgt + dmt * lt
            q = dmt * mpt
            q_rows = jnp.sum(q, axis=1, keepdims=True)
            q_cols = jnp.sum(q, axis=0, keepdims=True)
            hk, dhn = hall[sl, :], dhall[sl, :]
            e = jnp.exp(csr)
            cl = cs[T - 1:T, k:k + 1]
            wdec = jnp.exp(cl - csr)
            w = wdec * dtr
            rt = _dot(dhn.astype(BF16), bb, _NT)
            dxts.append(dtr * dxdt + dk * dyk + rt * w)
            xz = jnp.sum(xk * dxdt, axis=0, keepdims=True)
            dw = jnp.sum(rt * xk, axis=0, keepdims=True)
            dcl = jnp.exp(cl) * jnp.sum(dhn * hk) + jnp.sum(dw * w)
            yo = e * _dot(hk.astype(BF16), cbf, _NT)
            dcs_r = jnp.sum(dyk * yo, axis=0, keepdims=True) + q_cols - dw * w + jnp.where(last_l, dcl, 0.0)
            dye = (dyk * e).astype(BF16)
            dc_acc = dc_acc + _dot(dye, hk.astype(BF16), _TN)
            db_acc = db_acc + _dot((xk * w).astype(BF16), dhn.astype(BF16), _TN)
            new_dh.append(jnp.exp(cl) * dhn + _dot(dye, cbf))
            onehot = (lane == k).astype(F32)
            ddt_rows = ddt_rows + jnp.where(head_row == k, xz + dw * wdec, 0.0)
            dcs_rows = dcs_rows + jnp.where(head_row == k, dcs_r, 0.0)
            qrow_cols = qrow_cols + q_rows * onehot
            ddsk_acc = ddsk_acc + jnp.sum(dyk * xk) * onehot
        dx_ref[...] = jnp.concatenate(dxts, axis=0).T
        dhs[...] = jnp.concatenate(new_dh, axis=0)
        dc_ref[...] = _dot(dgt.T.astype(BF16), bb) + dc_acc
        db_ref[...] = _dot(dgt.astype(BF16), cbf) + db_acc
        da = _dot_hi((ci >= ri).astype(F32), dcs_rows.T - qrow_cols)
        ddtp = (ddt_rows.T + da * a_row) * _sigmoid(pre)
        ddtp = jnp.where(lane < hg, ddtp, 0.0)
        ddtp_ref[...] = ddtp
        dbias_ref[...] += jnp.sum(ddtp, axis=0, keepdims=True)
        dalog_ref[...] += jnp.sum(da * dt, axis=0, keepdims=True) * a_row
        ddsk_ref[...] += ddsk_acc

    def rc(c):
        return nc - 1 - c

    vec = pl.BlockSpec((1, LANE), lambda g, c: (0, g))
    xsp = pl.BlockSpec((T, gw), lambda g, c: (rc(c), g))
    return _call(
        body, name,
        [S((rows, D_INNER), F32), S((rows, D_BC), F32), S((rows, D_BC), F32),
         S((rows, SSM_GROUPS * LANE), F32), S((1, SSM_GROUPS * LANE), F32),
         S((1, SSM_GROUPS * LANE), F32), S((1, SSM_GROUPS * LANE), F32)],
        (SSM_GROUPS, nc),
        [xsp,
         pl.BlockSpec((T, D_STATE), lambda g, c: (rc(c), boff + g)),
         pl.BlockSpec((T, D_STATE), lambda g, c: (rc(c), coff + g)),
         pl.BlockSpec((T, LANE), lambda g, c: (rc(c), dtoff + g)), vec, vec, vec,
         xsp, pl.BlockSpec((1, 1, gw, D_STATE), lambda g, c: (rc(c), g, 0, 0))],
        [xsp,
         pl.BlockSpec((T, D_STATE), lambda g, c: (rc(c), g)),
         pl.BlockSpec((T, D_STATE), lambda g, c: (rc(c), g)),
         pl.BlockSpec((T, LANE), lambda g, c: (rc(c), g)), vec, vec, vec],
        ("parallel", "arbitrary"), (xbc, xbc, xbc, zx, bias, alog, dsk, dy, hst),
        scratch=[pltpu.VMEM((gw, D_STATE), F32)], comm=comm)


def _attn_tiles(kv_ref, j):
    prev = jnp.maximum(j - 1, 0)
    meta = kv_ref[0:T, :]
    prv = kv_ref[pl.ds(pl.multiple_of(prev * T, T), T), :]
    cur = kv_ref[pl.ds(pl.multiple_of(j * T, T), T), :]
    return jnp.concatenate([meta, prv, cur], axis=0)


def _attn_mask(j):
    r = j * T + lax.broadcasted_iota(jnp.int32, (3 * T, T), 1)
    row = lax.broadcasted_iota(jnp.int32, (3 * T, T), 0)
    t0, t1 = row < T, row < 2 * T
    s = jnp.where(t0, row, (j - 2) * T + row)
    ok = (s <= r) & ((s < N_META) | (s > r - WINDOW))
    use = (t0 & (j >= 2) & (row < N_META)) | (jnp.logical_not(t0) & t1 & (j >= 1)) | jnp.logical_not(t1)
    return ok & use


def _attn_probs(qh, k3, mask, sink):
    sc = jnp.where(mask, _dot(k3, qh, _NT), NEG)
    m = jnp.maximum(jnp.max(sc, axis=0, keepdims=True), sink)
    p = jnp.exp(sc - m)
    es = jnp.exp(sink - m)
    inv = 1.0 / (jnp.sum(p, axis=0, keepdims=True) + es)
    return p * inv, es * inv


def _attn_fwd(q, kv, sinks, name, comm=None):
    rows = q.shape[0]
    scale = 1.0 / math.sqrt(ATTN_DH)
    qpk = N_Q_HEADS // N_KV_HEADS

    def body(q_ref, kv_ref, s_ref, o_ref):
        j = pl.program_id(0)
        kv3 = _attn_tiles(kv_ref, j).astype(BF16)
        mask = _attn_mask(j)
        qv = (q_ref[...] * scale).astype(BF16)
        sk = s_ref[...]
        for kh in range(N_KV_HEADS):
            k3 = kv3[:, kh * ATTN_DH:(kh + 1) * ATTN_DH]
            v3 = kv3[:, D_KV + kh * ATTN_DH:D_KV + (kh + 1) * ATTN_DH]
            for g in range(qpk):
                h = kh * qpk + g
                p, _ = _attn_probs(qv[:, h * ATTN_DH:(h + 1) * ATTN_DH], k3, mask, sk[:, h:h + 1])
                o_ref[:, h * ATTN_DH:(h + 1) * ATTN_DH] = _dot(p.astype(BF16), v3, _TN).astype(o_ref.dtype)

    return _call(
        body, name, S((rows, D_MODEL), BF16), (rows // T,),
        [pl.BlockSpec((T, D_MODEL), lambda j: (j, 0)), pl.BlockSpec((rows, 2 * D_KV), lambda j: (0, 0)),
         pl.BlockSpec((1, N_Q_HEADS), lambda j: (0, 0))],
        pl.BlockSpec((T, D_MODEL), lambda j: (j, 0)), ("parallel",), (q, kv, sinks), comm=comm)


def _attn_bwd(q, kv, sinks, do, name, comm=None):
    rows = q.shape[0]
    scale = 1.0 / math.sqrt(ATTN_DH)
    qpk = N_Q_HEADS // N_KV_HEADS

    def body(q_ref, kv_ref, s_ref, do_ref, dq_ref, dkv_ref, ds_ref):
        j = pl.program_id(0)

        @pl.when(j == 0)
        def _():
            dkv_ref[...] = jnp.zeros_like(dkv_ref)
            ds_ref[...] = jnp.zeros_like(ds_ref)

        kv3 = _attn_tiles(kv_ref, j).astype(BF16)
        mask = _attn_mask(j)
        qv = (q_ref[...] * scale).astype(BF16)
        dov = do_ref[...].astype(BF16)
        sk = s_ref[...]
        lane = lax.broadcasted_iota(jnp.int32, (1, LANE), 1)
        ds_acc = jnp.zeros((1, LANE), F32)
        prev = jnp.maximum(j - 1, 0)
        for kh in range(N_KV_HEADS):
            ksl = slice(kh * ATTN_DH, (kh + 1) * ATTN_DH)
            vsl = slice(D_KV + kh * ATTN_DH, D_KV + (kh + 1) * ATTN_DH)
            k3, v3 = kv3[:, ksl], kv3[:, vsl]
            dk3 = jnp.zeros((3 * T, ATTN_DH), F32)
            dv3 = jnp.zeros((3 * T, ATTN_DH), F32)
            for g in range(qpk):
                h = kh * qpk + g
                hs = slice(h * ATTN_DH, (h + 1) * ATTN_DH)
                qh, doh = qv[:, hs], dov[:, hs]
                p, ps = _attn_probs(qh, k3, mask, sk[:, h:h + 1])
                dp = _dot(v3, doh, _NT)
                delta = jnp.sum(p * dp, axis=0, keepdims=True)
                dsc = (p * (dp - delta)).astype(BF16)
                dq_ref[:, hs] = (_dot(dsc, k3, _TN) * scale).astype(dq_ref.dtype)
                dk3 = dk3 + _dot(dsc, qh)
                dv3 = dv3 + _dot(p.astype(BF16), doh)
                ds_acc = ds_acc - jnp.sum(ps * delta) * (lane == h).astype(F32)
            for t, start in enumerate((0, pl.multiple_of(prev * T, T), pl.multiple_of(j * T, T))):
                rsl = pl.ds(start, T)
                dkv_ref[rsl, ksl] += dk3[t * T:(t + 1) * T, :]
                dkv_ref[rsl, vsl] += dv3[t * T:(t + 1) * T, :]
        ds_ref[...] += ds_acc

    blk = pl.BlockSpec((T, D_MODEL), lambda j: (j, 0))
    full = pl.BlockSpec((rows, 2 * D_KV), lambda j: (0, 0))
    return _call(
        body, name, [S((rows, D_MODEL), BF16), S((rows, 2 * D_KV), F32), S((1, LANE), F32)], (rows // T,),
        [blk, full, pl.BlockSpec((1, N_Q_HEADS), lambda j: (0, 0)), blk],
        [blk, full, pl.BlockSpec((1, LANE), lambda j: (0, 0))], ("arbitrary",), (q, kv, sinks, do), comm=comm)


BLOCK_BYTES = 1 << 20


def _div_tile(rows, cols):
    cap = max(16, BLOCK_BYTES // (4 * cols))
    best = None
    for t in range(16, min(rows, cap) + 1, 16):
        if rows % t == 0:
            best = t
    return best if best is not None else rows


def _adamw(parts, w, m, v, name):
    n, rows, cols = parts.shape
    tr = _div_tile(rows, cols)
    c1 = 1.0 / (1.0 - B1 ** STEP)
    c2 = 1.0 / (1.0 - B2 ** STEP)

    def body(p_ref, w_ref, m_ref, v_ref, g_ref, d_ref, nm_ref, nv_ref):
        g = p_ref[0].astype(F32)
        for i in range(1, n):
            g = g + p_ref[i].astype(F32)
        nm = B1 * m_ref[...] + (1.0 - B1) * g
        nv = B2 * v_ref[...] + (1.0 - B2) * (g * g)
        g_ref[...] = g
        nm_ref[...] = nm
        nv_ref[...] = nv
        d_ref[...] = -LR * ((nm * c1) / (jnp.sqrt(nv * c2) + EPS) + WD * w_ref[...])

    row = pl.BlockSpec((tr, cols), lambda i: (i, 0))
    return pl.pallas_call(
        body, name=name, out_shape=[S((rows, cols), F32)] * 4, grid=(rows // tr,),
        in_specs=[pl.BlockSpec((n, tr, cols), lambda i: (0, i, 0)), row, row, row], out_specs=[row] * 4,
        compiler_params=_cp(("parallel",)))(parts, w, m, v)


def _col_segments(ws, runs):
    segs = []
    for glo, mlo, n in runs:
        while n > 0:
            d, off = divmod(glo, ws)
            take = min(n, ws - off)
            segs.append((d, off, mlo, take))
            glo, mlo, n = glo + take, mlo + take, n - take
    return segs


def _assemble_cols(g, width, segs, name):
    _, rows, ws = g.shape
    rb = _div_tile(rows, width // 2)

    def body(g_ref, o_ref):
        o_ref[...] = jnp.zeros_like(o_ref)
        for d, off, mlo, n in segs:
            o_ref[:, mlo:mlo + n] = g_ref[d, :, off:off + n]

    return pl.pallas_call(
        body, name=name, out_shape=S((rows, width), g.dtype), grid=(rows // rb,),
        in_specs=[pl.BlockSpec((N_DEV, rb, ws), lambda i: (0, i, 0))],
        out_specs=pl.BlockSpec((rb, width), lambda i: (i, 0)), compiler_params=_cp(("parallel",)))(g)


def _scatter_cols(dw, ws, segs, name):
    rows, width = dw.shape
    rb = _div_tile(rows, width)

    def body(w_ref, o_ref):
        for d, off, mlo, n in segs:
            o_ref[d, :, off:off + n] = w_ref[:, mlo:mlo + n].astype(o_ref.dtype)

    return pl.pallas_call(
        body, name=name, out_shape=S((N_DEV, rows, ws), BF16), grid=(rows // rb,),
        in_specs=[pl.BlockSpec((rb, width), lambda i: (i, 0))],
        out_specs=pl.BlockSpec((N_DEV, rb, ws), lambda i: (0, i, 0)), compiler_params=_cp(("parallel",)))(dw)


def _gather_comm(xs):
    n = len(xs)

    def setup(x_refs, out_refs, sems):
        send_sems, recv_sems, local_sems = sems
        mx, my, mc = lax.axis_index("x"), lax.axis_index("y"), lax.axis_index("c")
        me, sibling = (mx, my, mc), (mx, my, 1 - mc)
        chips = [(1 - mx, my), (mx, 1 - my), (1 - mx, 1 - my)]

        def blk(a, px, py, pc):
            return out_refs[a].at[4 * px + 2 * py + pc]

        def copy(a, k, block, to, src=None):
            return pltpu.make_async_remote_copy(
                src_ref=blk(a, *block) if src is None else src, dst_ref=blk(a, *block),
                send_sem=send_sems.at[a, k], recv_sem=recv_sems.at[a, k], device_id=to, device_id_type=_MESH)

        mine = [pltpu.make_async_copy(x_refs[a], blk(a, *me), local_sems.at[a]) for a in range(n)]
        own = []
        for a in range(n):
            own.append(copy(a, 0, me, sibling, src=x_refs[a]))
            own += [copy(a, 1 + i, me, (*chip, mc), src=x_refs[a]) for i, chip in enumerate(chips)]
        return me, sibling, chips, mc, copy, mine, own

    def first(x_refs, out_refs, sems):
        _, _, _, _, _, mine, own = setup(x_refs, out_refs, sems)
        for cp in mine + own:
            cp.start()

    def last(x_refs, out_refs, sems):
        me, sibling, chips, mc, copy, mine, own = setup(x_refs, out_refs, sems)
        passed = []
        for a in range(n):
            for i, chip in enumerate(chips):
                copy(a, 1 + i, (*chip, mc), me).wait_recv()
                passed.append(copy(a, 4 + i, (*chip, mc), sibling))
                passed[-1].start()
        for a in range(n):
            copy(a, 0, sibling, me).wait_recv()
            for i, chip in enumerate(chips):
                copy(a, 4 + i, (*chip, 1 - mc), me).wait_recv()
        for cp in own + passed:
            cp.wait_send()
        for cp in mine:
            cp.wait()

    return _Comm(list(xs), [S((N_DEV,) + x.shape, x.dtype) for x in xs],
                 [pltpu.SemaphoreType.DMA((n, 7)), pltpu.SemaphoreType.DMA((n, 7)), pltpu.SemaphoreType.DMA((n,))],
                 first, last)


def _swap_comm(gs):
    n = len(gs)

    def copies(g_refs, out_refs, sems):
        send_sems, recv_sems = sems
        mx, my, mc = lax.axis_index("x"), lax.axis_index("y"), lax.axis_index("c")
        return [pltpu.make_async_remote_copy(
            src_ref=g_refs[a].at[2 * k + 1 - mc], dst_ref=out_refs[a].at[k], send_sem=send_sems.at[a, k],
            recv_sem=recv_sems.at[a, k], device_id=(mx, my, 1 - mc), device_id_type=_MESH)
            for a in range(n) for k in range(4)]

    def first(g_refs, out_refs, sems):
        for cp in copies(g_refs, out_refs, sems):
            cp.start()

    def last(g_refs, out_refs, sems):
        for cp in copies(g_refs, out_refs, sems):
            cp.wait()

    return _Comm(list(gs), [S((4,) + g.shape[1:], g.dtype) for g in gs],
                 [pltpu.SemaphoreType.DMA((n, 4)), pltpu.SemaphoreType.DMA((n, 4))], first, last)


def _chips_comm(parts):
    n = len(parts)

    def copies(p_refs, out_refs, sems):
        send_sems, recv_sems, local_sems = sems
        mx, my, mc = lax.axis_index("x"), lax.axis_index("y"), lax.axis_index("c")
        mychip = 2 * mx + my
        chips = [(1 - mx, my), (mx, 1 - my), (1 - mx, 1 - my)]
        mine = [pltpu.make_async_copy(p_refs[a].at[mychip], out_refs[a].at[mychip], local_sems.at[a])
                for a in range(n)]
        return mine + [pltpu.make_async_remote_copy(
            src_ref=p_refs[a].at[2 * cx + cy], dst_ref=out_refs[a].at[mychip], send_sem=send_sems.at[a, i],
            recv_sem=recv_sems.at[a, i], device_id=(cx, cy, mc), device_id_type=_MESH)
            for a in range(n) for i, (cx, cy) in enumerate(chips)]

    def first(p_refs, out_refs, sems):
        for cp in copies(p_refs, out_refs, sems):
            cp.start()

    def last(p_refs, out_refs, sems):
        for cp in copies(p_refs, out_refs, sems):
            cp.wait()

    return _Comm(list(parts), [S(p.shape, p.dtype) for p in parts],
                 [pltpu.SemaphoreType.DMA((n, 3)), pltpu.SemaphoreType.DMA((n, 3)), pltpu.SemaphoreType.DMA((n,))],
                 first, last)


def _join_comms(comms):
    def split(refs, counts):
        out, p = [], 0
        for cnt in counts:
            out.append(refs[p:p + cnt])
            p += cnt
        return out

    ni = [len(c.ins) for c in comms]
    no = [len(c.out_shapes) for c in comms]
    ns = [len(c.scratch) for c in comms]

    def first(in_refs, out_refs, sems):
        for c, i, o, s in zip(comms, split(in_refs, ni), split(out_refs, no), split(sems, ns)):
            c.first(i, o, s)

    def last(in_refs, out_refs, sems):
        for c, i, o, s in zip(comms, split(in_refs, ni), split(out_refs, no), split(sems, ns)):
            c.last(i, o, s)

    return _Comm([x for c in comms for x in c.ins], [x for c in comms for x in c.out_shapes],
                 [x for c in comms for x in c.scratch], first, last)


def _add_pairs(mine, theirs, core, name):
    _, rows, cols = mine.shape
    tr = _div_tile(rows, cols)

    def body(core_ref, a_ref, b_ref, o_ref):
        o_ref[...] = (a_ref[...].astype(F32) + b_ref[...].astype(F32)).astype(o_ref.dtype)

    return pl.pallas_call(
        body, name=name, out_shape=S((4, rows, cols), BF16),
        grid_spec=pltpu.PrefetchScalarGridSpec(
            num_scalar_prefetch=1, grid=(4, rows // tr),
            in_specs=[pl.BlockSpec((None, tr, cols), lambda k, i, c: (2 * k + c[0], i, 0)),
                      pl.BlockSpec((None, tr, cols), lambda k, i, c: (k, i, 0))],
            out_specs=pl.BlockSpec((None, tr, cols), lambda k, i, c: (k, i, 0))),
        compiler_params=_cp(("parallel", "parallel")))(core, mine, theirs)


def _run_comm(comm, name):
    ci, co = len(comm.ins), len(comm.out_shapes)

    def body(*refs):
        comm.first(refs[:ci], refs[ci:ci + co], refs[ci + co:])
        comm.last(refs[:ci], refs[ci:ci + co], refs[ci + co:])

    return pl.pallas_call(body, name=name, out_shape=list(comm.out_shapes), in_specs=[_HBM] * ci,
                          out_specs=[_HBM] * co, scratch_shapes=list(comm.scratch))(*comm.ins)


def _flat_rows(n_elems, mult):
    rows = -(-n_elems // LANE)
    return -(-rows // mult) * mult


def _pack(arrs, lead, mult, dtype):
    lead_shape = arrs[0].shape[:lead]
    flat = jnp.concatenate([a.astype(dtype).reshape(lead_shape + (-1,)) for a in arrs], axis=-1)
    n = flat.shape[-1]
    rows = _flat_rows(n, mult)
    flat = jnp.pad(flat, [(0, 0)] * lead + [(0, rows * LANE - n)])
    return flat.reshape(lead_shape + (rows, LANE))


def _unpack(flat, lead, shapes):
    lead_shape = flat.shape[:lead]
    flat = flat.reshape(lead_shape + (-1,))
    out, off = [], 0
    for shp in shapes:
        n = math.prod(shp)
        out.append(flat[..., off:off + n].reshape(lead_shape + tuple(shp)))
        off += n
    return out


def _split8(full, ax, n):
    shp = full.shape
    return jnp.moveaxis(full.reshape(shp[:ax] + (N_DEV, n) + shp[ax + 1:]), ax, 0)


def _join8(g, ax):
    shp = g.shape[1:]
    return jnp.moveaxis(g, 0, ax).reshape(shp[:ax] + (N_DEV * shp[ax],) + shp[ax + 1:])


def _group_lanes(v, hg):
    v = v.reshape(SSM_GROUPS, hg)
    return jnp.pad(v, ((0, 0), (0, LANE - hg))).reshape(1, SSM_GROUPS * LANE)


def _ungroup_lanes(v, hg):
    return v.reshape(SSM_GROUPS, LANE)[:, :hg].reshape(1, SSM_GROUPS * hg)


def kernel(x, meta_tokens, a_norm_pre, a_w_in, a_conv_w, a_conv_b, a_dt_bias, a_a_log, a_d_skip, a_gate_norm, a_w_out, a_norm_post, kv_norm, w_kv, b_norm_pre, b_w_q, b_sinks, b_w_o, b_norm_post, f_norm_pre, f_w_up, f_conv_w, f_conv_b, f_w_down, f_norm_post, loss_target, m_meta_tokens, m_a_norm_pre, m_a_w_in, m_a_conv_w, m_a_conv_b, m_a_dt_bias, m_a_a_log, m_a_d_skip, m_a_gate_norm, m_a_w_out, m_a_norm_post, m_kv_norm, m_w_kv, m_b_norm_pre, m_b_w_q, m_b_sinks, m_b_w_o, m_b_norm_post, m_f_norm_pre, m_f_w_up, m_f_conv_w, m_f_conv_b, m_f_w_down, m_f_norm_post, v_meta_tokens, v_a_norm_pre, v_a_w_in, v_a_conv_w, v_a_conv_b, v_a_dt_bias, v_a_a_log, v_a_d_skip, v_a_gate_norm, v_a_w_out, v_a_norm_post, v_kv_norm, v_w_kv, v_b_norm_pre, v_b_w_q, v_b_sinks, v_b_w_o, v_b_norm_post, v_f_norm_pre, v_f_w_up, v_f_conv_w, v_f_conv_b, v_f_w_down, v_f_norm_post):
    args = locals()
    wts = {n: args[n] for n in WEIGHTS}
    mom = {n: args["m_" + n] for n in WEIGHTS}
    var = {n: args["v_" + n] for n in WEIGHTS}
    mx, my, mc = lax.axis_index("x"), lax.axis_index("y"), lax.axis_index("c")
    me = 4 * mx + 2 * my + mc
    rows = _seq_rows()
    hg = SSM_HEADS // SSM_GROUPS
    d = D_MODEL

    n_main = D_INNER + D_XBC
    ws_in, ws_up = a_w_in.shape[2], f_w_up.shape[2]
    segs_in = _col_segments(ws_in, [(0, 0, n_main)] + [(n_main + hg * g, n_main + LANE * g, hg)
                                                      for g in range(SSM_GROUPS)])
    segs_up = _col_segments(ws_up, [(0, 0, 2 * D_FF)])
    def gather_of(*ws):
        return _gather_comm([w.astype(BF16) for w in ws])

    g_in, small_full = _run_comm(_gather_comm([a_w_in[0].astype(BF16), _pack([wts[n] for n in SMALL], 0, 8, F32)]),
                                 "gather_first")
    full = {}
    for n, g in zip(SMALL, _unpack(small_full, 1, [wts[n].shape for n in SMALL])):
        full[n] = _join8(g, SHARD_AXIS[n])
    w_in_all = _assemble_cols(g_in, n_main + SSM_GROUPS * LANE, segs_in, "asm_w_in")
    w_up, w_down = [None, None], [None, None]
    bias_g = _group_lanes(wts["a_dt_bias"], hg)
    alog_g = _group_lanes(wts["a_a_log"], hg)
    dsk_g = _group_lanes(wts["a_d_skip"], hg)
    a_conv_w, a_conv_b = full["a_conv_w"][0], full["a_conv_b"]
    f_cw, f_cb = full["f_conv_w"], wts["f_conv_b"]
    fpre, fpost = wts["f_norm_pre"], wts["f_norm_post"]

    pad_rows = rows - N_META - SEQ
    h0 = jnp.concatenate([full["meta_tokens"], x[0], jnp.zeros((pad_rows, d), F32)], axis=0)
    tgt = jnp.pad(loss_target[0], ((N_META, pad_rows), (0, 0)))

    _, (hn0,) = _resid_norm(h0, None, None, [full["a_norm_pre"]], "norm_a_pre")
    zx, (g_out,) = _mm(hn0, w_in_all, "nn", F32, "mm_in", comm=gather_of(a_w_out[0]))
    w_out = g_out.reshape(D_INNER, d)
    xbc, (g_kv, g_q, g_o) = _conv_silu_fwd(zx, a_conv_w, a_conv_b, "conv_a", comm=gather_of(w_kv, b_w_q[0], b_w_o[0]))
    w_kvf, w_q, w_o = g_kv.reshape(d, 2 * D_KV), g_q.reshape(d, d), g_o.reshape(d, d)
    (y_ssd, hst), (g_up0, g_dn0, g_dn1) = _ssd_fwd(xbc, zx, bias_g, alog_g, dsk_g, "ssd_fwd",
                                                   comm=gather_of(f_w_up[0], f_w_down[0], f_w_down[1]))
    w_up[0] = _assemble_cols(g_up0, 2 * D_FF, segs_up, "asm_w_up0")
    w_down = [g_dn0.reshape(D_FF, d), g_dn1.reshape(D_FF, d)]
    yn = _gatenorm_fwd(y_ssd, zx, full["a_gate_norm"], "gatenorm")
    mix_a = _mm(yn, w_out, "nn", F32, "mm_out")
    h1, (fn0,) = _resid_norm(h0, mix_a, full["a_norm_post"], [fpre[0:1]], "resid_a")

    half = d // 2
    u0, (g_up1a,) = _mm(fn0, w_up[0], "nn", F32, "mm_up0", comm=gather_of(f_w_up[1, :half]))
    act0, (g_up1b,) = _ffn_act_fwd(u0, f_cw[0], f_cb[0:1], "ffn_act0", comm=gather_of(f_w_up[1, half:]))
    w_up[1] = jnp.concatenate([_assemble_cols(g_up1a, 2 * D_FF, segs_up, "asm_w_up1a"),
                               _assemble_cols(g_up1b, 2 * D_FF, segs_up, "asm_w_up1b")], axis=0)
    ffn0 = _mm(act0, w_down[0], "nn", F32, "mm_down0")
    h2, (kvn, bn) = _resid_norm(h1, ffn0, fpost[0:1], [wts["kv_norm"].reshape(1, d), wts["b_norm_pre"]], "resid_f0")
    kv = _mm(kvn, w_kvf, "nn", F32, "mm_kv")
    q = _mm(bn, w_q, "nn", F32, "mm_q")
    o = _attn_fwd(q, kv, wts["b_sinks"], "attn_fwd")
    mix_b = _mm(o, w_o, "nn", F32, "mm_o")
    h3, (fn1,) = _resid_norm(h2, mix_b, wts["b_norm_post"], [fpre[1:2]], "resid_b")
    u1 = _mm(fn1, w_up[1], "nn", F32, "mm_up1")
    act1 = _ffn_act_fwd(u1, f_cw[1], f_cb[1:2], "ffn_act1")
    ffn1 = _mm(act1, w_down[1], "nn", F32, "mm_down1")
    dh4, loss_row = _final_loss(h3, ffn1, fpost[1:2], tgt, "loss")
    loss = lax.psum(loss_row[0, 0], ("x", "y", "c"))

    grads = {}

    core = mc.astype(jnp.int32).reshape(1)

    def carried(res, comm):
        return res if comm is not None else (res, None)

    def ffn_bwd(dh_out, h_in, fn, u, act, ffn, i, c_dact=None, c_dwdown=None):
        dffn, dw_post = _norm_bwd(ffn, fpost[i:i + 1], dh_out, None, BF16, f"nb_fpost{i}")
        dact, got_a = carried(_mm(dffn, w_down[i], "nt", F32, f"mm_dact{i}", comm=c_dact), c_dact)
        dw_down, got_b = carried(_mm(act, dffn, "tn", BF16, f"mm_dwdown{i}", comm=c_dwdown), c_dwdown)
        dw_down = dw_down.reshape(N_DEV, -1, d)
        (dg, dv, dwg, dwv, dbg, dbv), (s_dn,) = _ffn_act_bwd(u, dact, f_cw[i], f_cb[i:i + 1], f"ffn_act_bwd{i}",
                                                             comm=_swap_comm([dw_down]))
        sum_dn = _add_pairs(dw_down, s_dn, core, f"rs_add_dn{i}")
        du = jnp.concatenate([dg, dv], axis=1)
        dfn, (p_dn,) = _mm(du, w_up[i], "nt", F32, f"mm_dfn{i}", comm=_chips_comm([sum_dn]))
        dw_up = _scatter_cols(_mm(fn, du, "tn", BF16, f"mm_dwup{i}"), ws_up, segs_up, f"scat_w_up{i}")
        (dh_in, dw_pre), (s_up,) = _norm_bwd(h_in, fpre[i:i + 1], dfn, dh_out, F32, f"nb_fpre{i}",
                                             comm=_swap_comm([dw_up]))
        sum_up = _add_pairs(dw_up, s_up, core, f"rs_add_up{i}")
        return dh_in, dict(post=dw_post, p_down=p_dn, cw=jnp.concatenate([dwg, dwv], axis=1),
                           cb=jnp.concatenate([dbg, dbv], axis=1), sum_up=sum_up, pre=dw_pre), got_a, got_b

    dh3, gf1, _, _ = ffn_bwd(dh4, h3, fn1, u1, act1, ffn1, 1)
    dmix_b, grads["b_norm_post"] = _norm_bwd(mix_b, wts["b_norm_post"], dh3, None, BF16, "nb_bpost")
    do = _mm(dmix_b, w_o, "nt", F32, "mm_do")
    dw_o = _mm(o, dmix_b, "tn", BF16, "mm_dwo").reshape(N_DEV, -1, d)
    (dq, dkv, dsinks), (p_up1, s_o) = _attn_bwd(q, kv, wts["b_sinks"], do, "attn_bwd",
                                                comm=_join_comms([_chips_comm([gf1["sum_up"]]), _swap_comm([dw_o])]))
    sum_o = _add_pairs(dw_o, s_o, core, "rs_add_o")
    grads["b_sinks"] = dsinks[:, :N_Q_HEADS]
    dbn = _mm(dq, w_q, "nt", F32, "mm_dbn")
    dw_q = _mm(bn, dq, "tn", BF16, "mm_dwq").reshape(N_DEV, -1, d)
    dkv16 = dkv.astype(BF16)
    dkvn = _mm(dkv16, w_kvf, "nt", F32, "mm_dkvn")
    dw_kv = _mm(kvn, dkv16, "tn", BF16, "mm_dwkv").reshape(N_DEV, -1, 2 * D_KV)
    (dh2, grads["b_norm_pre"]), (s_q, s_kv) = _norm_bwd(h2, wts["b_norm_pre"], dbn, dh3, F32, "nb_bpre",
                                                        comm=_swap_comm([dw_q, dw_kv]))
    sum_q, sum_kv = _add_pairs(dw_q, s_q, core, "rs_add_q"), _add_pairs(dw_kv, s_kv, core, "rs_add_kv")
    dh2, dw_kvn = _norm_bwd(h2, wts["kv_norm"].reshape(1, d), dkvn, dh2, F32, "nb_kv")
    grads["kv_norm"] = dw_kvn.reshape(d)
    dh1, gf0, (p_o,), (p_q, p_kv) = ffn_bwd(dh2, h1, fn0, u0, act0, ffn0, 0, c_dact=_chips_comm([sum_o]),
                                            c_dwdown=_chips_comm([sum_q, sum_kv]))
    p_dn0, p_dn1 = gf0["p_down"], gf1["p_down"]
    grads["f_norm_post"] = jnp.concatenate([gf0["post"], gf1["post"]], axis=0)
    grads["f_norm_pre"] = jnp.concatenate([gf0["pre"], gf1["pre"]], axis=0)
    grads["f_conv_w"] = jnp.stack([gf0["cw"], gf1["cw"]])
    grads["f_conv_b"] = jnp.concatenate([gf0["cb"], gf1["cb"]], axis=0)

    dmix_a, grads["a_norm_post"] = _norm_bwd(mix_a, full["a_norm_post"], dh1, None, BF16, "nb_apost")
    dyn = _mm(dmix_a, w_out, "nt", F32, "mm_dyn")
    dw_out = _mm(yn, dmix_a, "tn", BF16, "mm_dwout").reshape(N_DEV, -1, d)
    (dy_ssd, dz, grads["a_gate_norm"]), (s_out,) = _gatenorm_bwd(y_ssd, zx, full["a_gate_norm"], dyn, "gatenorm_bwd",
                                                                 comm=_swap_comm([dw_out]))
    sum_out = _add_pairs(dw_out, s_out, core, "rs_add_out")
    (dxs, dbm, dcm, ddtp, dalog, ddsk, dbias), (p_up0, p_out) = _ssd_bwd(
        xbc, zx, bias_g, alog_g, dsk_g, dy_ssd, hst, "ssd_bwd", comm=_chips_comm([gf0["sum_up"], sum_out]))
    dxbc = jnp.concatenate([dxs, dbm, dcm], axis=1)
    grads["a_a_log"] = _ungroup_lanes(dalog, hg)
    grads["a_d_skip"] = _ungroup_lanes(ddsk, hg)
    grads["a_dt_bias"] = _ungroup_lanes(dbias, hg)
    dpre, dcw, dcb = _conv_silu_bwd(zx, dxbc, a_conv_w, a_conv_b, "conv_a_bwd")
    grads["a_conv_w"], grads["a_conv_b"] = dcw[None], dcb
    dzx = jnp.concatenate([dz, dpre, ddtp.astype(BF16)], axis=1)
    dhn0 = _mm(dzx, w_in_all, "nt", F32, "mm_dhn0")
    dw_in_all = _mm(hn0, dzx, "tn", BF16, "mm_dwin")
    dw_in8 = _scatter_cols(dw_in_all, ws_in, segs_in, "scat_w_in")
    (dh0, grads["a_norm_pre"]), (s_in,) = _norm_bwd(h0, full["a_norm_pre"], dhn0, dh1, F32, "nb_apre",
                                                    comm=_swap_comm([dw_in8]))
    sum_in = _add_pairs(dw_in8, s_in, core, "rs_add_in")
    grad_x = dh0[N_META:N_META + SEQ][None]
    grads["meta_tokens"] = dh0[:N_META]

    small_local = _pack([_split8(grads[n], SHARD_AXIS[n], wts[n].shape[SHARD_AXIS[n]]) for n in SMALL], 1, 8, F32)
    repl_local = _pack([grads[n] for n in REPL], 0, 8, F32)
    n_sr = small_local.shape[1]
    small_vec = jnp.concatenate([small_local.reshape(N_DEV * n_sr, LANE), repl_local], axis=0)
    p_in, small_all = _run_comm(_join_comms([_chips_comm([sum_in]), _gather_comm([small_vec])]), "rs_tail")
    parts_big = dict(a_w_in=p_in, a_w_out=p_out, w_kv=p_kv, b_w_q=p_q, b_w_o=p_o,
                     f_w_up=jnp.concatenate([p_up0, p_up1], axis=1), f_w_down=jnp.concatenate([p_dn0, p_dn1], axis=1))
    mine_small = lax.dynamic_slice_in_dim(small_all, me * n_sr, n_sr, axis=1)
    parts_small = jnp.concatenate([mine_small, small_all[:, N_DEV * n_sr:]], axis=1)

    def flat_f32(dct, names, mult):
        return _pack([dct[n] for n in names], 0, mult, F32)

    big_out = {}
    for n in BIG:
        cols = wts[n].shape[-1]
        res = _adamw(parts_big[n], *[dct[n].reshape(-1, cols) for dct in (wts, mom, var)], f"adamw_{n}")
        big_out[n] = [r.reshape(wts[n].shape) for r in res]
    sm_in = [jnp.concatenate([flat_f32(dct, SMALL, 8), flat_f32(dct, REPL, 8)], axis=0) for dct in (wts, mom, var)]
    small_out = _adamw(parts_small, *sm_in, "adamw_small")

    outs = []
    for kind in range(4):
        res = {n: big_out[n][kind] for n in BIG}
        for n, a in zip(SMALL, _unpack(small_out[kind][:n_sr], 0, [wts[n].shape for n in SMALL])):
            res[n] = a
        for n, a in zip(REPL, _unpack(small_out[kind][n_sr:], 0, [wts[n].shape for n in REPL])):
            res[n] = a
        outs.append(res)
    return (loss, grad_x, *[outs[0][n] for n in WEIGHTS], *[outs[1][n] for n in WEIGHTS],
            *[outs[2][n] for n in WEIGHTS], *[outs[3][n] for n in WEIGHTS])
```

```python
import functools
import math

import jax
import jax.numpy as jnp
from jax import lax
from jax.experimental import pallas as pl
from jax.experimental.pallas import tpu as pltpu

F32, BF16 = jnp.float32, jnp.bfloat16
S = jax.ShapeDtypeStruct

D_MODEL = 1024
SEQ = 2048
N_META = 16
D_INNER = 2048
HEAD_P = 64
SSM_HEADS = D_INNER // HEAD_P
SSM_GROUPS = 4
D_STATE = 128
SSM_CONV = 4
D_BC = SSM_GROUPS * D_STATE
D_XBC = D_INNER + 2 * D_BC
ATTN_DH = 64
N_Q_HEADS = D_MODEL // ATTN_DH
N_KV_HEADS = 4
D_KV = N_KV_HEADS * ATTN_DH
WINDOW = 128
D_FF = 2816
FFN_CONV = 3
RMS_EPS = 1e-6
NEG = -1e30
LR, B1, B2, EPS, WD, STEP = 0.001, 0.9, 0.999, 1e-08, 0.01, 10

N_DEV = 8
T = 128
LANE = 128
VMEM_LIMIT = 48 * 1024 * 1024

BIG = ("a_w_in", "a_w_out", "w_kv", "b_w_q", "b_w_o", "f_w_up", "f_w_down")
SMALL = ("meta_tokens", "a_norm_pre", "a_conv_w", "a_conv_b", "a_gate_norm", "a_norm_post", "f_conv_w")
REPL = ("a_dt_bias", "a_a_log", "a_d_skip", "kv_norm", "b_norm_pre", "b_sinks", "b_norm_post",
        "f_norm_pre", "f_conv_b", "f_norm_post")
SHARD_AXIS = dict(a_w_in=2, a_w_out=1, w_kv=0, b_w_q=1, b_w_o=1, f_w_up=2, f_w_down=1, meta_tokens=1,
                  a_norm_pre=1, a_conv_w=2, a_conv_b=1, a_gate_norm=1, a_norm_post=1, f_conv_w=2)
WEIGHTS = ("meta_tokens", "a_norm_pre", "a_w_in", "a_conv_w", "a_conv_b", "a_dt_bias", "a_a_log", "a_d_skip",
           "a_gate_norm", "a_w_out", "a_norm_post", "kv_norm", "w_kv", "b_norm_pre", "b_w_q", "b_sinks", "b_w_o",
           "b_norm_post", "f_norm_pre", "f_w_up", "f_conv_w", "f_conv_b", "f_w_down", "f_norm_post")


def _seq_rows():
    return -(-(N_META + SEQ) // T) * T


def _cp(sem=None):
    return pltpu.CompilerParams(dimension_semantics=sem, vmem_limit_bytes=VMEM_LIMIT)


def _pick(n, target):
    t = min(n, target)
    t -= t % LANE
    while n % t:
        t -= LANE
    return t


def _sigmoid(x):
    return 1.0 / (1.0 + jnp.exp(-x))


def _softplus(x):
    return jnp.maximum(x, 0.0) + jnp.log(1.0 + jnp.exp(-jnp.abs(x)))


_NN = (((1,), (0,)), ((), ()))
_NT = (((1,), (1,)), ((), ()))
_TN = (((0,), (0,)), ((), ()))


def _dot(a, b, dims=_NN):
    return lax.dot_general(a, b, dims, preferred_element_type=F32)


def _dot_hi(a, b):
    return lax.dot_general(a, b, _NN, precision=lax.Precision.HIGHEST, preferred_element_type=F32)


_HBM = pl.BlockSpec(memory_space=pltpu.HBM)
_MESH = pl.DeviceIdType.MESH


class _Comm:
    def __init__(self, ins, out_shapes, scratch, first, last):
        self.ins, self.out_shapes, self.scratch, self.first, self.last = ins, out_shapes, scratch, first, last


def _call(body, name, out_shape, grid, in_specs, out_specs, sem, args, scratch=(), comm=None):
    if comm is None:
        return pl.pallas_call(body, name=name, out_shape=out_shape, grid=grid, in_specs=in_specs, out_specs=out_specs,
                              scratch_shapes=list(scratch), compiler_params=_cp(sem))(*args)
    single = not isinstance(out_shape, (list, tuple))
    outs = [out_shape] if single else list(out_shape)
    ospecs = [out_specs] if single else list(out_specs)
    n_in, n_out, n_scr, ci, co = len(in_specs), len(outs), len(scratch), len(comm.ins), len(comm.out_shapes)

    def carrier(*refs):
        p = 0
        parts = []
        for cnt in (n_in, ci, n_out, co, n_scr, len(comm.scratch)):
            parts.append(refs[p:p + cnt])
            p += cnt
        ins, cins, outs_r, couts, scr, cscr = parts
        ids = [pl.program_id(i) for i in range(len(grid))]
        first, last = ids[0] == 0, ids[0] == grid[0] - 1
        for i in range(1, len(grid)):
            first, last = first & (ids[i] == 0), last & (ids[i] == grid[i] - 1)

        @pl.when(first)
        def _():
            comm.first(cins, couts, cscr)

        body(*ins, *outs_r, *scr)

        @pl.when(last)
        def _():
            comm.last(cins, couts, cscr)

    res = pl.pallas_call(
        carrier, name=name, out_shape=outs + list(comm.out_shapes), grid=grid,
        in_specs=list(in_specs) + [_HBM] * ci, out_specs=ospecs + [_HBM] * co,
        scratch_shapes=list(scratch) + list(comm.scratch),
        compiler_params=_cp(("arbitrary",) * len(grid)))(*args, *comm.ins)
    mine = res[0] if single else list(res[:n_out])
    return mine, list(res[n_out:])


def _mm(a, b, mode, out_dtype, name, comm=None):
    if mode == "tn":
        m, kk = a.shape
        n = b.shape[1]
        tko, tn = _pick(kk, 512), _pick(n, 512)

        def body(a_ref, b_ref, o_ref):
            o_ref[...] = _dot(a_ref[...], b_ref[...], _TN).astype(o_ref.dtype)

        return _call(
            body, name, S((kk, n), out_dtype), (kk // tko, n // tn),
            [pl.BlockSpec((m, tko), lambda i, j: (0, i)), pl.BlockSpec((m, tn), lambda i, j: (0, j))],
            pl.BlockSpec((tko, tn), lambda i, j: (i, j)), ("parallel", "parallel"), (a, b), comm=comm)

    m, kk = a.shape
    n = b.shape[1] if mode == "nn" else b.shape[0]
    tn = _pick(n, 512)
    tk = kk if kk <= 2048 else _pick(kk, 1536)
    nk = kk // tk
    dims = _NN if mode == "nn" else _NT

    def body(a_ref, b_ref, o_ref, *acc):
        part = _dot(a_ref[...], b_ref[...], dims)
        if nk == 1:
            o_ref[...] = part.astype(o_ref.dtype)
        else:
            k = pl.program_id(1)

            @pl.when(k == 0)
            def _():
                acc[0][...] = part

            @pl.when(k > 0)
            def _():
                acc[0][...] += part

            @pl.when(k == nk - 1)
            def _():
                o_ref[...] = acc[0][...].astype(o_ref.dtype)

    b_spec = (pl.BlockSpec((tk, tn), lambda j, k: (k, j)) if mode == "nn"
              else pl.BlockSpec((tn, tk), lambda j, k: (j, k)))
    return _call(
        body, name, S((m, n), out_dtype), (n // tn, nk), [pl.BlockSpec((m, tk), lambda j, k: (0, k)), b_spec],
        pl.BlockSpec((m, tn), lambda j, k: (0, j)), ("parallel", "arbitrary"), (a, b),
        scratch=[pltpu.VMEM((m, tn), F32)] if nk > 1 else [], comm=comm)


def _rms(x, w):
    return x * lax.rsqrt(jnp.mean(x * x, axis=-1, keepdims=True) + RMS_EPS) * w


def _row_tile(rows):
    return rows // 8


def _resid_norm(h, br, w_post, next_ws, name):
    rows, d = h.shape
    tr = _row_tile(rows)
    has_br = br is not None
    nw = len(next_ws)

    def body(*refs):
        h_ref = refs[0]
        pos = 1
        x = h_ref[...]
        if has_br:
            x = x + _rms(refs[1][...], refs[2][...])
            pos = 3
        w_refs = refs[pos:pos + nw]
        outs = refs[pos + nw:]
        if has_br:
            outs[0][...] = x
            outs = outs[1:]
        for w_ref, o_ref in zip(w_refs, outs):
            o_ref[...] = _rms(x, w_ref[...]).astype(o_ref.dtype)

    row = pl.BlockSpec((tr, d), lambda i: (i, 0))
    vec = pl.BlockSpec((1, d), lambda i: (0, 0))
    ins = [h] + ([br, w_post] if has_br else []) + list(next_ws)
    in_specs = [row] + ([row, vec] if has_br else []) + [vec] * nw
    out_shape = ([S((rows, d), F32)] if has_br else []) + [S((rows, d), BF16)] * nw
    res = pl.pallas_call(body, name=name, out_shape=out_shape, grid=(rows // tr,), in_specs=in_specs,
                         out_specs=[row] * len(out_shape), compiler_params=_cp(("parallel",)))(*ins)
    if has_br:
        return res[0], list(res[1:])
    return h, list(res)


def _norm_bwd(x, w, dy, add, out_dtype, name, comm=None):
    rows, d = x.shape
    tr = _row_tile(rows)
    has_add = add is not None

    def body(*refs):
        x_ref, w_ref, dy_ref = refs[:3]
        dx_ref, dw_ref = refs[-2:]
        xv = x_ref[...]
        r = lax.rsqrt(jnp.mean(xv * xv, axis=-1, keepdims=True) + RMS_EPS)
        dyv = dy_ref[...].astype(F32)
        wdy = dyv * w_ref[...]
        dx = r * wdy - xv * (r * r * r) * jnp.mean(xv * wdy, axis=-1, keepdims=True)
        if has_add:
            dx = dx + refs[3][...]
        dx_ref[...] = dx.astype(dx_ref.dtype)

        @pl.when(pl.program_id(0) == 0)
        def _():
            dw_ref[...] = jnp.zeros_like(dw_ref)

        dw_ref[...] += jnp.sum(dyv * xv * r, axis=0, keepdims=True)

    row = pl.BlockSpec((tr, d), lambda i: (i, 0))
    vec = pl.BlockSpec((1, d), lambda i: (0, 0))
    ins = [x, w, dy] + ([add] if has_add else [])
    return _call(body, name, [S((rows, d), out_dtype), S((1, d), F32)], (rows // tr,),
                 [row, vec, row] + ([row] if has_add else []), [row, vec], ("arbitrary",), ins, comm=comm)


def _final_loss(h, br, w_post, tgt, name):
    rows, d = h.shape
    tr = _row_tile(rows)

    def body(h_ref, br_ref, w_ref, t_ref, dh_ref, loss_ref):
        i = pl.program_id(0)
        y = h_ref[...] + _rms(br_ref[...], w_ref[...])
        r = i * tr + lax.broadcasted_iota(jnp.int32, (tr, 1), 0)
        real = (r >= N_META) & (r < N_META + SEQ)
        diff = jnp.where(real, y - t_ref[...], 0.0)
        dh_ref[...] = diff * (1.0 / d)

        @pl.when(i == 0)
        def _():
            loss_ref[...] = jnp.zeros_like(loss_ref)

        loss_ref[...] += jnp.sum(diff * diff) * (0.5 / d)

    row = pl.BlockSpec((tr, d), lambda i: (i, 0))
    return pl.pallas_call(body, name=name, out_shape=[S((rows, d), F32), S((1, LANE), F32)], grid=(rows // tr,),
                          in_specs=[row, row, pl.BlockSpec((1, d), lambda i: (0, 0)), row],
                          out_specs=[row, pl.BlockSpec((1, LANE), lambda i: (0, 0))],
                          compiler_params=_cp(("arbitrary",)))(h, br, w_post, tgt)


def _gatenorm_fwd(y, zx, w, name, comm=None):
    rows, d = y.shape
    tr = _row_tile(rows)

    def body(y_ref, z_ref, w_ref, o_ref):
        z = z_ref[...]
        o_ref[...] = _rms(y_ref[...] * z * _sigmoid(z), w_ref[...]).astype(o_ref.dtype)

    row = pl.BlockSpec((tr, d), lambda i: (i, 0))
    return _call(body, name, S((rows, d), BF16), (rows // tr,), [row, row, pl.BlockSpec((1, d), lambda i: (0, 0))],
                 row, ("parallel",), (y, zx, w), comm=comm)


def _place_rows(pieces, rows, name):
    d = pieces[0][0].shape[1]
    gaps, at = [], 0
    for arr, off in pieces + [(None, rows)]:
        if off > at:
            gaps.append((at, off - at))
        at = off + (arr.shape[0] if arr is not None else 0)
    zrows = max(n for _, n in gaps)

    def body(*refs):
        srcs, out_ref, zeros, sems = refs[:len(pieces)], refs[len(pieces)], refs[-2], refs[-1]
        zeros[...] = jnp.zeros_like(zeros)
        cps = [pltpu.make_async_copy(s, out_ref.at[pl.ds(off, arr.shape[0])], sems.at[i])
               for i, (s, (arr, off)) in enumerate(zip(srcs, pieces))]
        cps += [pltpu.make_async_copy(zeros.at[pl.ds(0, n)], out_ref.at[pl.ds(lo, n)], sems.at[len(pieces) + i])
                for i, (lo, n) in enumerate(gaps)]
        for cp in cps:
            cp.start()
        for cp in cps:
            cp.wait()

    return pl.pallas_call(
        body, name=name, out_shape=S((rows, d), F32), in_specs=[_HBM] * len(pieces), out_specs=_HBM,
        scratch_shapes=[pltpu.VMEM((zrows, d), F32), pltpu.SemaphoreType.DMA((len(pieces) + len(gaps),))],
    )(*[arr for arr, _ in pieces])


def _gatenorm_bwd(y, zx, w, dyn, name, comm=None):
    rows, d = y.shape
    tr = _row_tile(rows)

    def body(y_ref, z_ref, w_ref, dyn_ref, dy_ref, dz_ref, dw_ref):
        yv, z = y_ref[...], z_ref[...]
        sg = _sigmoid(z)
        sz = z * sg
        g = yv * sz
        r = lax.rsqrt(jnp.mean(g * g, axis=-1, keepdims=True) + RMS_EPS)
        dyn_v = dyn_ref[...]
        wdy = dyn_v * w_ref[...]
        dg = r * wdy - g * (r * r * r) * jnp.mean(g * wdy, axis=-1, keepdims=True)
        dy_ref[...] = dg * sz
        dz_ref[...] = (dg * yv * sg * (1.0 + z * (1.0 - sg))).astype(dz_ref.dtype)

        @pl.when(pl.program_id(0) == 0)
        def _():
            dw_ref[...] = jnp.zeros_like(dw_ref)

        dw_ref[...] += jnp.sum(dyn_v * g * r, axis=0, keepdims=True)

    row = pl.BlockSpec((tr, d), lambda i: (i, 0))
    vec = pl.BlockSpec((1, d), lambda i: (0, 0))
    return _call(body, name, [S((rows, d), F32), S((rows, d), BF16), S((1, d), F32)], (rows // tr,),
                 [row, row, vec, row], [row, row, vec], ("arbitrary",), (y, zx, w, dyn), comm=comm)


def _shift_down(x, s, rows_iota):
    if s == 0:
        return x
    return jnp.where(rows_iota >= s, pltpu.roll(x, s, 0), 0.0)


def _shift_up(x, s, rows_iota):
    if s == 0:
        return x
    rows = x.shape[0]
    return jnp.where(rows_iota < rows - s, pltpu.roll(x, rows - s, 0), 0.0)


def _r16(v):
    return v.astype(BF16).astype(F32)


def _conv(x, w_ref, b_ref, taps, rows_iota):
    x = _r16(x)
    acc = jnp.zeros_like(x)
    for k in range(taps):
        acc = acc + _r16(w_ref[k:k + 1, :]) * _shift_down(x, taps - 1 - k, rows_iota)
    return acc + b_ref[...]


def _conv_bwd(x, du, w_ref, dw_ref, db_ref, taps, rows_iota):
    db_ref[...] = jnp.sum(du, axis=0, keepdims=True)
    x, du = _r16(x), _r16(du)
    dx = jnp.zeros_like(x)
    for k in range(taps):
        s = taps - 1 - k
        dx = dx + _r16(w_ref[k:k + 1, :]) * _shift_up(du, s, rows_iota)
        dw_ref[k:k + 1, :] = jnp.sum(du * _shift_down(x, s, rows_iota), axis=0, keepdims=True)
    return dx


def _conv_silu_fwd(zx, w, b, name, comm=None):
    rows = zx.shape[0]
    cb = 512
    off = D_INNER // cb

    def body(x_ref, w_ref, b_ref, o_ref):
        it = lax.broadcasted_iota(jnp.int32, (rows, 1), 0)
        u = _conv(x_ref[...], w_ref, b_ref, SSM_CONV, it)
        o_ref[...] = u * _sigmoid(u)

    return _call(
        body, name, S((rows, D_XBC), F32), (D_XBC // cb,),
        [pl.BlockSpec((rows, cb), lambda j: (0, off + j)), pl.BlockSpec((SSM_CONV, cb), lambda j: (0, j)),
         pl.BlockSpec((1, cb), lambda j: (0, j))],
        pl.BlockSpec((rows, cb), lambda j: (0, j)), ("parallel",), (zx, w, b), comm=comm)


def _conv_silu_bwd(zx, dxbc, w, b, name):
    rows = zx.shape[0]
    cb = 512
    off = D_INNER // cb

    def body(x_ref, d_ref, w_ref, b_ref, dx_ref, dw_ref, db_ref):
        it = lax.broadcasted_iota(jnp.int32, (rows, 1), 0)
        x = x_ref[...]
        u = _conv(x, w_ref, b_ref, SSM_CONV, it)
        sg = _sigmoid(u)
        du = d_ref[...] * sg * (1.0 + u * (1.0 - sg))
        dx_ref[...] = _conv_bwd(x, du, w_ref, dw_ref, db_ref, SSM_CONV, it).astype(dx_ref.dtype)

    col = pl.BlockSpec((rows, cb), lambda j: (0, j))
    wsp = pl.BlockSpec((SSM_CONV, cb), lambda j: (0, j))
    bsp = pl.BlockSpec((1, cb), lambda j: (0, j))
    return pl.pallas_call(
        body, name=name, out_shape=[S((rows, D_XBC), BF16), S((SSM_CONV, D_XBC), F32), S((1, D_XBC), F32)],
        grid=(D_XBC // cb,), in_specs=[pl.BlockSpec((rows, cb), lambda j: (0, off + j)), col, wsp, bsp],
        out_specs=[col, wsp, bsp], compiler_params=_cp(("parallel",)))(zx, dxbc, w, b)


def _ffn_act_fwd(u, w, b, name, comm=None):
    rows = u.shape[0]
    cb = 256
    nb = D_FF // cb

    def body(g_ref, v_ref, wg_ref, wv_ref, bg_ref, bv_ref, o_ref):
        it = lax.broadcasted_iota(jnp.int32, (rows, 1), 0)
        g = _conv(g_ref[...], wg_ref, bg_ref, FFN_CONV, it)
        v = _conv(v_ref[...], wv_ref, bv_ref, FFN_CONV, it)
        o_ref[...] = (g * _sigmoid(g) * v).astype(o_ref.dtype)

    def sp(r, shift):
        return pl.BlockSpec((r, cb), lambda j: (0, shift + j))

    return _call(
        body, name, S((rows, D_FF), BF16), (nb,),
        [sp(rows, 0), sp(rows, nb), sp(FFN_CONV, 0), sp(FFN_CONV, nb), sp(1, 0), sp(1, nb)],
        sp(rows, 0), ("parallel",), (u, u, w, w, b, b), comm=comm)


def _ffn_act_bwd(u, dact, w, b, name, comm=None):
    rows = u.shape[0]
    cb = 256
    nb = D_FF // cb

    def body(g_ref, v_ref, d_ref, wg_ref, wv_ref, bg_ref, bv_ref, dg_ref, dv_ref, dwg_ref, dwv_ref, dbg_ref, dbv_ref):
        it = lax.broadcasted_iota(jnp.int32, (rows, 1), 0)
        xg, xv = g_ref[...], v_ref[...]
        g = _conv(xg, wg_ref, bg_ref, FFN_CONV, it)
        v = _conv(xv, wv_ref, bv_ref, FFN_CONV, it)
        sg = _sigmoid(g)
        d = d_ref[...]
        dgate = d * v * sg * (1.0 + g * (1.0 - sg))
        dval = d * g * sg
        dg_ref[...] = _conv_bwd(xg, dgate, wg_ref, dwg_ref, dbg_ref, FFN_CONV, it).astype(dg_ref.dtype)
        dv_ref[...] = _conv_bwd(xv, dval, wv_ref, dwv_ref, dbv_ref, FFN_CONV, it).astype(dv_ref.dtype)

    def sp(r, shift):
        return pl.BlockSpec((r, cb), lambda j: (0, shift + j))

    return _call(
        body, name,
        [S((rows, D_FF), BF16), S((rows, D_FF), BF16), S((FFN_CONV, D_FF), F32), S((FFN_CONV, D_FF), F32),
         S((1, D_FF), F32), S((1, D_FF), F32)],
        (nb,),
        [sp(rows, 0), sp(rows, nb), sp(rows, 0), sp(FFN_CONV, 0), sp(FFN_CONV, nb), sp(1, 0), sp(1, nb)],
        [sp(rows, 0), sp(rows, 0), sp(FFN_CONV, 0), sp(FFN_CONV, 0), sp(1, 0), sp(1, 0)],
        ("parallel",), (u, u, dact, w, w, b, b), comm=comm)


def _ssd_consts(dtp_ref, bias_ref, alog_ref, hg):
    lane = lax.broadcasted_iota(jnp.int32, (1, LANE), 1)
    pre = dtp_ref[...] + bias_ref[...]
    dt = _softplus(pre)
    a_row = jnp.where(lane < hg, -jnp.exp(alog_ref[...]), 0.0)
    ri = lax.broadcasted_iota(jnp.int32, (T, T), 0)
    ci = lax.broadcasted_iota(jnp.int32, (T, T), 1)
    cs = _dot_hi((ri >= ci).astype(F32), dt * a_row)
    return pre, dt, a_row, cs, ri, ci, lane


def _ssd_fwd(xbc, zx, bias, alog, dsk, name, comm=None):
    rows = xbc.shape[0]
    nc = rows // T
    hg = SSM_HEADS // SSM_GROUPS
    gw = hg * HEAD_P
    xoff, boff, coff = 0, D_INNER // D_STATE, (D_INNER + D_BC) // D_STATE
    dtoff = (D_INNER + D_XBC) // LANE

    def body(x_ref, b_ref, c_ref, dtp_ref, bias_ref, alog_ref, dsk_ref, y_ref, hst_ref, hs):
        c = pl.program_id(1)

        @pl.when(c == 0)
        def _():
            hs[...] = jnp.zeros_like(hs)

        _, dt, _, cs, ri, ci, _ = _ssd_consts(dtp_ref, bias_ref, alog_ref, hg)
        cst, dtt = cs.T, dt.T
        xt = x_ref[...].T
        bb, cbf = b_ref[...].astype(BF16), c_ref[...].astype(BF16)
        gt = _dot(bb, cbf, _NT)
        causal_t = ci >= ri
        dskv = dsk_ref[...]
        hall = hs[...]
        hst_ref[0, 0] = hall
        yts, new_h = [], []
        for k in range(hg):
            sl = slice(k * HEAD_P, (k + 1) * HEAD_P)
            csc, csr = cs[:, k:k + 1], cst[k:k + 1, :]
            lt = jnp.exp(jnp.where(causal_t, csr - csc, NEG))
            xk = xt[sl, :]
            xdt = xk * dtt[k:k + 1, :]
            hk = hall[sl, :]
            yd = _dot(xdt.astype(BF16), (gt * lt).astype(BF16))
            yo = jnp.exp(csr) * _dot(hk.astype(BF16), cbf, _NT)
            yts.append(yd + yo + dskv[:, k:k + 1] * xk)
            cl = cs[T - 1:T, k:k + 1]
            st = _dot((xdt * jnp.exp(cl - csr)).astype(BF16), bb)
            new_h.append(jnp.exp(cl) * hk + st)
        y_ref[...] = jnp.concatenate(yts, axis=0).T
        hs[...] = jnp.concatenate(new_h, axis=0)

    vec = pl.BlockSpec((1, LANE), lambda g, c: (0, g))
    return _call(
        body, name, [S((rows, D_INNER), F32), S((nc, SSM_GROUPS, gw, D_STATE), F32)], (SSM_GROUPS, nc),
        [pl.BlockSpec((T, gw), lambda g, c: (c, xoff + g)),
         pl.BlockSpec((T, D_STATE), lambda g, c: (c, boff + g)),
         pl.BlockSpec((T, D_STATE), lambda g, c: (c, coff + g)),
         pl.BlockSpec((T, LANE), lambda g, c: (c, dtoff + g)), vec, vec, vec],
        [pl.BlockSpec((T, gw), lambda g, c: (c, g)), pl.BlockSpec((1, 1, gw, D_STATE), lambda g, c: (c, g, 0, 0))],
        ("parallel", "arbitrary"), (xbc, xbc, xbc, zx, bias, alog, dsk),
        scratch=[pltpu.VMEM((gw, D_STATE), F32)], comm=comm)


def _ssd_bwd(xbc, zx, bias, alog, dsk, dy, hst, name, comm=None):
    rows = xbc.shape[0]
    nc = rows // T
    hg = SSM_HEADS // SSM_GROUPS
    gw = hg * HEAD_P
    boff, coff = D_INNER // D_STATE, (D_INNER + D_BC) // D_STATE
    dtoff = (D_INNER + D_XBC) // LANE

    def body(x_ref, b_ref, c_ref, dtp_ref, bias_ref, alog_ref, dsk_ref, dy_ref, hst_ref,
             dx_ref, db_ref, dc_ref, ddtp_ref, dalog_ref, ddsk_ref, dbias_ref, dhs):
        step = pl.program_id(1)

        @pl.when(step == 0)
        def _():
            dhs[...] = jnp.zeros_like(dhs)
            dalog_ref[...] = jnp.zeros_like(dalog_ref)
            ddsk_ref[...] = jnp.zeros_like(ddsk_ref)
            dbias_ref[...] = jnp.zeros_like(dbias_ref)

        pre, dt, a_row, cs, ri, ci, lane = _ssd_consts(dtp_ref, bias_ref, alog_ref, hg)
        cst, dtt = cs.T, dt.T
        xt, dyt = x_ref[...].T, dy_ref[...].T
        bb, cbf = b_ref[...].astype(BF16), c_ref[...].astype(BF16)
        gt = _dot(bb, cbf, _NT)
        causal_t = ci >= ri
        dskv = dsk_ref[...]
        hall, dhall = hst_ref[0, 0], dhs[...]
        head_row = lax.broadcasted_iota(jnp.int32, (T, 1), 0)
        last_l = lax.broadcasted_iota(jnp.int32, (1, T), 1) == T - 1
        dgt = jnp.zeros((T, T), F32)
        dc_acc = jnp.zeros((T, D_STATE), F32)
        db_acc = jnp.zeros((T, D_STATE), F32)
        ddt_rows = jnp.zeros((T, T), F32)
        dcs_rows = jnp.zeros((T, T), F32)
        qrow_cols = jnp.zeros((T, LANE), F32)
        ddsk_acc = jnp.zeros((1, LANE), F32)
        dxts, new_dh = [], []
        for k in range(hg):
            sl = slice(k * HEAD_P, (k + 1) * HEAD_P)
            csc, csr = cs[:, k:k + 1], cst[k:k + 1, :]
            lt = jnp.exp(jnp.where(causal_t, csr - csc, NEG))
            xk, dyk = xt[sl, :], dyt[sl, :]
            dtr, dk = dtt[k:k + 1, :], dskv[:, k:k + 1]
            xdt = xk * dtr
            mpt = gt * lt
            dyb = dyk.astype(BF16)
            dxdt = _dot(dyb, mpt.astype(BF16), _NT)
            dmt = _dot(xdt.astype(BF16), dyb, _TN)
            dgt = dgt + dmt * lt
            q = dmt * mpt
            q_rows = jnp.sum(q, axis=1, keepdims=True)
            q_cols = jnp.sum(q, axis=0, keepdims=True)
            hk, dhn = hall[sl, :], dhall[sl, :]
            e = jnp.exp(csr)
            cl = cs[T - 1:T, k:k + 1]
            wdec = jnp.exp(cl - csr)
            w = wdec * dtr
            rt = _dot(dhn.astype(BF16), bb, _NT)
            dxts.append(dtr * dxdt + dk * dyk + rt * w)
            xz = jnp.sum(xk * dxdt, axis=0, keepdims=True)
            dw = jnp.sum(rt * xk, axis=0, keepdims=True)
            dcl = jnp.exp(cl) * jnp.sum(dhn * hk) + jnp.sum(dw * w)
            yo = e * _dot(hk.astype(BF16), cbf, _NT)
            dcs_r = jnp.sum(dyk * yo, axis=0, keepdims=True) + q_cols - dw * w + jnp.where(last_l, dcl, 0.0)
            dye = (dyk * e).astype(BF16)
            dc_acc = dc_acc + _dot(dye, hk.astype(BF16), _TN)
            db_acc = db_acc + _dot((xk * w).astype(BF16), dhn.astype(BF16), _TN)
            new_dh.append(jnp.exp(cl) * dhn + _dot(dye, cbf))
            onehot = (lane == k).astype(F32)
            ddt_rows = ddt_rows + jnp.where(head_row == k, xz + dw * wdec, 0.0)
            dcs_rows = dcs_rows + jnp.where(head_row == k, dcs_r, 0.0)
            qrow_cols = qrow_cols + q_rows * onehot
            ddsk_acc = ddsk_acc + jnp.sum(dyk * xk) * onehot
        dx_ref[...] = jnp.concatenate(dxts, axis=0).T
        dhs[...] = jnp.concatenate(new_dh, axis=0)
        dc_ref[...] = _dot(dgt.T.astype(BF16), bb) + dc_acc
        db_ref[...] = _dot(dgt.astype(BF16), cbf) + db_acc
        da = _dot_hi((ci >= ri).astype(F32), dcs_rows.T - qrow_cols)
        ddtp = (ddt_rows.T + da * a_row) * _sigmoid(pre)
        ddtp = jnp.where(lane < hg, ddtp, 0.0)
        ddtp_ref[...] = ddtp
        dbias_ref[...] += jnp.sum(ddtp, axis=0, keepdims=True)
        dalog_ref[...] += jnp.sum(da * dt, axis=0, keepdims=True) * a_row
        ddsk_ref[...] += ddsk_acc

    def rc(c):
        return nc - 1 - c

    vec = pl.BlockSpec((1, LANE), lambda g, c: (0, g))
    xsp = pl.BlockSpec((T, gw), lambda g, c: (rc(c), g))
    return _call(
        body, name,
        [S((rows, D_INNER), F32), S((rows, D_BC), F32), S((rows, D_BC), F32),
         S((rows, SSM_GROUPS * LANE), F32), S((1, SSM_GROUPS * LANE), F32),
         S((1, SSM_GROUPS * LANE), F32), S((1, SSM_GROUPS * LANE), F32)],
        (SSM_GROUPS, nc),
        [xsp,
         pl.BlockSpec((T, D_STATE), lambda g, c: (rc(c), boff + g)),
         pl.BlockSpec((T, D_STATE), lambda g, c: (rc(c), coff + g)),
         pl.BlockSpec((T, LANE), lambda g, c: (rc(c), dtoff + g)), vec, vec, vec,
         xsp, pl.BlockSpec((1, 1, gw, D_STATE), lambda g, c: (rc(c), g, 0, 0))],
        [xsp,
         pl.BlockSpec((T, D_STATE), lambda g, c: (rc(c), g)),
         pl.BlockSpec((T, D_STATE), lambda g, c: (rc(c), g)),
         pl.BlockSpec((T, LANE), lambda g, c: (rc(c), g)), vec, vec, vec],
        ("parallel", "arbitrary"), (xbc, xbc, xbc, zx, bias, alog, dsk, dy, hst),
        scratch=[pltpu.VMEM((gw, D_STATE), F32)], comm=comm)


def _attn_tiles(kv_ref, j):
    prev = jnp.maximum(j - 1, 0)
    meta = kv_ref[0:T, :]
    prv = kv_ref[pl.ds(pl.multiple_of(prev * T, T), T), :]
    cur = kv_ref[pl.ds(pl.multiple_of(j * T, T), T), :]
    return jnp.concatenate([meta, prv, cur], axis=0)


def _attn_mask(j):
    r = j * T + lax.broadcasted_iota(jnp.int32, (3 * T, T), 1)
    row = lax.broadcasted_iota(jnp.int32, (3 * T, T), 0)
    t0, t1 = row < T, row < 2 * T
    s = jnp.where(t0, row, (j - 2) * T + row)
    ok = (s <= r) & ((s < N_META) | (s > r - WINDOW))
    use = (t0 & (j >= 2) & (row < N_META)) | (jnp.logical_not(t0) & t1 & (j >= 1)) | jnp.logical_not(t1)
    return ok & use


def _attn_probs(qh, k3, mask, sink):
    sc = jnp.where(mask, _dot(k3, qh, _NT), NEG)
    m = jnp.maximum(jnp.max(sc, axis=0, keepdims=True), sink)
    p = jnp.exp(sc - m)
    es = jnp.exp(sink - m)
    inv = 1.0 / (jnp.sum(p, axis=0, keepdims=True) + es)
    return p * inv, es * inv


def _attn_fwd(q, kv, sinks, name, comm=None):
    rows = q.shape[0]
    scale = 1.0 / math.sqrt(ATTN_DH)
    qpk = N_Q_HEADS // N_KV_HEADS

    def body(q_ref, kv_ref, s_ref, o_ref):
        j = pl.program_id(0)
        kv3 = _attn_tiles(kv_ref, j).astype(BF16)
        mask = _attn_mask(j)
        qv = (q_ref[...] * scale).astype(BF16)
        sk = s_ref[...]
        for kh in range(N_KV_HEADS):
            k3 = kv3[:, kh * ATTN_DH:(kh + 1) * ATTN_DH]
            v3 = kv3[:, D_KV + kh * ATTN_DH:D_KV + (kh + 1) * ATTN_DH]
            for g in range(qpk):
                h = kh * qpk + g
                p, _ = _attn_probs(qv[:, h * ATTN_DH:(h + 1) * ATTN_DH], k3, mask, sk[:, h:h + 1])
                o_ref[:, h * ATTN_DH:(h + 1) * ATTN_DH] = _dot(p.astype(BF16), v3, _TN).astype(o_ref.dtype)

    return _call(
        body, name, S((rows, D_MODEL), BF16), (rows // T,),
        [pl.BlockSpec((T, D_MODEL), lambda j: (j, 0)), pl.BlockSpec((rows, 2 * D_KV), lambda j: (0, 0)),
         pl.BlockSpec((1, N_Q_HEADS), lambda j: (0, 0))],
        pl.BlockSpec((T, D_MODEL), lambda j: (j, 0)), ("parallel",), (q, kv, sinks), comm=comm)


def _attn_bwd(q, kv, sinks, do, name, comm=None):
    rows = q.shape[0]
    scale = 1.0 / math.sqrt(ATTN_DH)
    qpk = N_Q_HEADS // N_KV_HEADS

    def body(q_ref, kv_ref, s_ref, do_ref, dq_ref, dkv_ref, ds_ref):
        j = pl.program_id(0)

        @pl.when(j == 0)
        def _():
            dkv_ref[...] = jnp.zeros_like(dkv_ref)
            ds_ref[...] = jnp.zeros_like(ds_ref)

        kv3 = _attn_tiles(kv_ref, j).astype(BF16)
        mask = _attn_mask(j)
        qv = (q_ref[...] * scale).astype(BF16)
        dov = do_ref[...].astype(BF16)
        sk = s_ref[...]
        lane = lax.broadcasted_iota(jnp.int32, (1, LANE), 1)
        ds_acc = jnp.zeros((1, LANE), F32)
        prev = jnp.maximum(j - 1, 0)
        for kh in range(N_KV_HEADS):
            ksl = slice(kh * ATTN_DH, (kh + 1) * ATTN_DH)
            vsl = slice(D_KV + kh * ATTN_DH, D_KV + (kh + 1) * ATTN_DH)
            k3, v3 = kv3[:, ksl], kv3[:, vsl]
            dk3 = jnp.zeros((3 * T, ATTN_DH), F32)
            dv3 = jnp.zeros((3 * T, ATTN_DH), F32)
            for g in range(qpk):
                h = kh * qpk + g
                hs = slice(h * ATTN_DH, (h + 1) * ATTN_DH)
                qh, doh = qv[:, hs], dov[:, hs]
                p, ps = _attn_probs(qh, k3, mask, sk[:, h:h + 1])
                dp = _dot(v3, doh, _NT)
                delta = jnp.sum(p * dp, axis=0, keepdims=True)
                dsc = (p * (dp - delta)).astype(BF16)
                dq_ref[:, hs] = (_dot(dsc, k3, _TN) * scale).astype(dq_ref.dtype)
                dk3 = dk3 + _dot(dsc, qh)
                dv3 = dv3 + _dot(p.astype(BF16), doh)
                ds_acc = ds_acc - jnp.sum(ps * delta) * (lane == h).astype(F32)
            for t, start in enumerate((0, pl.multiple_of(prev * T, T), pl.multiple_of(j * T, T))):
                rsl = pl.ds(start, T)
                dkv_ref[rsl, ksl] += dk3[t * T:(t + 1) * T, :]
                dkv_ref[rsl, vsl] += dv3[t * T:(t + 1) * T, :]
        ds_ref[...] += ds_acc

    blk = pl.BlockSpec((T, D_MODEL), lambda j: (j, 0))
    full = pl.BlockSpec((rows, 2 * D_KV), lambda j: (0, 0))
    return _call(
        body, name, [S((rows, D_MODEL), BF16), S((rows, 2 * D_KV), F32), S((1, LANE), F32)], (rows // T,),
        [blk, full, pl.BlockSpec((1, N_Q_HEADS), lambda j: (0, 0)), blk],
        [blk, full, pl.BlockSpec((1, LANE), lambda j: (0, 0))], ("arbitrary",), (q, kv, sinks, do), comm=comm)


BLOCK_BYTES = 1 << 20


def _div_tile(rows, cols):
    cap = max(16, BLOCK_BYTES // (4 * cols))
    best = None
    for t in range(16, min(rows, cap) + 1, 16):
        if rows % t == 0:
            best = t
    return best if best is not None else rows


def _adamw(parts, w, m, v, name):
    layers, rows, cols = w.shape
    n = parts[0].shape[0]
    tr = _div_tile(rows, cols)
    c1 = 1.0 / (1.0 - B1 ** STEP)
    c2 = 1.0 / (1.0 - B2 ** STEP)

    def body(*refs):
        p_refs = refs[:layers]
        w_ref, m_ref, v_ref, g_ref, d_ref, nm_ref, nv_ref = refs[layers:]
        layer = pl.program_id(0)
        for l in range(layers):
            @pl.when(layer == l)
            def _(p_ref=p_refs[l]):
                g = p_ref[0].astype(F32)
                for i in range(1, n):
                    g = g + p_ref[i].astype(F32)
                nm = B1 * m_ref[...] + (1.0 - B1) * g
                nv = B2 * v_ref[...] + (1.0 - B2) * (g * g)
                g_ref[...] = g
                nm_ref[...] = nm
                nv_ref[...] = nv
                d_ref[...] = -LR * ((nm * c1) / (jnp.sqrt(nv * c2) + EPS) + WD * w_ref[...])

    def part_spec(l):
        return pl.BlockSpec((n, tr, cols), lambda k, i: (0, jnp.where(k == l, i, 0), 0))

    row = pl.BlockSpec((None, tr, cols), lambda k, i: (k, i, 0))
    return pl.pallas_call(
        body, name=name, out_shape=[S((layers, rows, cols), F32)] * 4, grid=(layers, rows // tr),
        in_specs=[part_spec(l) for l in range(layers)] + [row, row, row], out_specs=[row] * 4,
        compiler_params=_cp(("parallel", "parallel")))(*parts, w, m, v)


def _col_segments(ws, runs):
    segs = []
    for glo, mlo, n in runs:
        while n > 0:
            d, off = divmod(glo, ws)
            take = min(n, ws - off)
            segs.append((d, off, mlo, take))
            glo, mlo, n = glo + take, mlo + take, n - take
    return segs


def _assemble_cols(g, width, segs, name):
    _, rows, ws = g.shape
    rb = _div_tile(rows, width // 2)

    def body(g_ref, o_ref):
        o_ref[...] = jnp.zeros_like(o_ref)
        for d, off, mlo, n in segs:
            o_ref[:, mlo:mlo + n] = g_ref[d, :, off:off + n]

    return pl.pallas_call(
        body, name=name, out_shape=S((rows, width), g.dtype), grid=(rows // rb,),
        in_specs=[pl.BlockSpec((N_DEV, rb, ws), lambda i: (0, i, 0))],
        out_specs=pl.BlockSpec((rb, width), lambda i: (i, 0)), compiler_params=_cp(("parallel",)))(g)


def _scatter_cols(dw, ws, segs, name):
    rows, width = dw.shape
    rb = _div_tile(rows, width)

    def body(w_ref, o_ref):
        for d, off, mlo, n in segs:
            o_ref[d, :, off:off + n] = w_ref[:, mlo:mlo + n].astype(o_ref.dtype)

    return pl.pallas_call(
        body, name=name, out_shape=S((N_DEV, rows, ws), BF16), grid=(rows // rb,),
        in_specs=[pl.BlockSpec((rb, width), lambda i: (i, 0))],
        out_specs=pl.BlockSpec((N_DEV, rb, ws), lambda i: (0, i, 0)), compiler_params=_cp(("parallel",)))(dw)


def _gather_comm(xs):
    n = len(xs)

    def setup(x_refs, out_refs, sems):
        send_sems, recv_sems, local_sems = sems
        mx, my, mc = lax.axis_index("x"), lax.axis_index("y"), lax.axis_index("c")
        me, sibling = (mx, my, mc), (mx, my, 1 - mc)
        chips = [(1 - mx, my), (mx, 1 - my), (1 - mx, 1 - my)]

        def blk(a, px, py, pc):
            return out_refs[a].at[4 * px + 2 * py + pc]

        def copy(a, k, block, to, src=None):
            return pltpu.make_async_remote_copy(
                src_ref=blk(a, *block) if src is None else src, dst_ref=blk(a, *block),
                send_sem=send_sems.at[a, k], recv_sem=recv_sems.at[a, k], device_id=to, device_id_type=_MESH)

        mine = [pltpu.make_async_copy(x_refs[a], blk(a, *me), local_sems.at[a]) for a in range(n)]
        own = []
        for a in range(n):
            own.append(copy(a, 0, me, sibling, src=x_refs[a]))
            own += [copy(a, 1 + i, me, (*chip, mc), src=x_refs[a]) for i, chip in enumerate(chips)]
        return me, sibling, chips, mc, copy, mine, own

    def first(x_refs, out_refs, sems):
        _, _, _, _, _, mine, own = setup(x_refs, out_refs, sems)
        for cp in mine + own:
            cp.start()

    def last(x_refs, out_refs, sems):
        me, sibling, chips, mc, copy, mine, own = setup(x_refs, out_refs, sems)
        passed = []
        for a in range(n):
            for i, chip in enumerate(chips):
                copy(a, 1 + i, (*chip, mc), me).wait_recv()
                passed.append(copy(a, 4 + i, (*chip, mc), sibling))
                passed[-1].start()
        for a in range(n):
            copy(a, 0, sibling, me).wait_recv()
            for i, chip in enumerate(chips):
                copy(a, 4 + i, (*chip, 1 - mc), me).wait_recv()
        for cp in own + passed:
            cp.wait_send()
        for cp in mine:
            cp.wait()

    return _Comm(list(xs), [S((N_DEV,) + x.shape, x.dtype) for x in xs],
                 [pltpu.SemaphoreType.DMA((n, 7)), pltpu.SemaphoreType.DMA((n, 7)), pltpu.SemaphoreType.DMA((n,))],
                 first, last)


def _swap_comm(gs):
    n = len(gs)

    def copies(g_refs, out_refs, sems):
        send_sems, recv_sems = sems
        mx, my, mc = lax.axis_index("x"), lax.axis_index("y"), lax.axis_index("c")
        return [pltpu.make_async_remote_copy(
            src_ref=g_refs[a].at[2 * k + 1 - mc], dst_ref=out_refs[a].at[k], send_sem=send_sems.at[a, k],
            recv_sem=recv_sems.at[a, k], device_id=(mx, my, 1 - mc), device_id_type=_MESH)
            for a in range(n) for k in range(4)]

    def first(g_refs, out_refs, sems):
        for cp in copies(g_refs, out_refs, sems):
            cp.start()

    def last(g_refs, out_refs, sems):
        for cp in copies(g_refs, out_refs, sems):
            cp.wait()

    return _Comm(list(gs), [S((4,) + g.shape[1:], g.dtype) for g in gs],
                 [pltpu.SemaphoreType.DMA((n, 4)), pltpu.SemaphoreType.DMA((n, 4))], first, last)


def _chips_comm(parts):
    n = len(parts)

    def copies(p_refs, out_refs, sems):
        send_sems, recv_sems, local_sems = sems
        mx, my, mc = lax.axis_index("x"), lax.axis_index("y"), lax.axis_index("c")
        mychip = 2 * mx + my
        chips = [(1 - mx, my), (mx, 1 - my), (1 - mx, 1 - my)]
        mine = [pltpu.make_async_copy(p_refs[a].at[mychip], out_refs[a].at[mychip], local_sems.at[a])
                for a in range(n)]
        return mine + [pltpu.make_async_remote_copy(
            src_ref=p_refs[a].at[2 * cx + cy], dst_ref=out_refs[a].at[mychip], send_sem=send_sems.at[a, i],
            recv_sem=recv_sems.at[a, i], device_id=(cx, cy, mc), device_id_type=_MESH)
            for a in range(n) for i, (cx, cy) in enumerate(chips)]

    def first(p_refs, out_refs, sems):
        for cp in copies(p_refs, out_refs, sems):
            cp.start()

    def last(p_refs, out_refs, sems):
        for cp in copies(p_refs, out_refs, sems):
            cp.wait()

    return _Comm(list(parts), [S(p.shape, p.dtype) for p in parts],
                 [pltpu.SemaphoreType.DMA((n, 3)), pltpu.SemaphoreType.DMA((n, 3)), pltpu.SemaphoreType.DMA((n,))],
                 first, last)


def _join_comms(comms):
    def split(refs, counts):
        out, p = [], 0
        for cnt in counts:
            out.append(refs[p:p + cnt])
            p += cnt
        return out

    ni = [len(c.ins) for c in comms]
    no = [len(c.out_shapes) for c in comms]
    ns = [len(c.scratch) for c in comms]

    def first(in_refs, out_refs, sems):
        for c, i, o, s in zip(comms, split(in_refs, ni), split(out_refs, no), split(sems, ns)):
            c.first(i, o, s)

    def last(in_refs, out_refs, sems):
        for c, i, o, s in zip(comms, split(in_refs, ni), split(out_refs, no), split(sems, ns)):
            c.last(i, o, s)

    return _Comm([x for c in comms for x in c.ins], [x for c in comms for x in c.out_shapes],
                 [x for c in comms for x in c.scratch], first, last)


def _add_pairs(mine, theirs, core, name):
    _, rows, cols = mine.shape
    tr = _div_tile(rows, cols)

    def body(core_ref, a_ref, b_ref, o_ref):
        o_ref[...] = (a_ref[...].astype(F32) + b_ref[...].astype(F32)).astype(o_ref.dtype)

    return pl.pallas_call(
        body, name=name, out_shape=S((4, rows, cols), BF16),
        grid_spec=pltpu.PrefetchScalarGridSpec(
            num_scalar_prefetch=1, grid=(4, rows // tr),
            in_specs=[pl.BlockSpec((None, tr, cols), lambda k, i, c: (2 * k + c[0], i, 0)),
                      pl.BlockSpec((None, tr, cols), lambda k, i, c: (k, i, 0))],
            out_specs=pl.BlockSpec((None, tr, cols), lambda k, i, c: (k, i, 0))),
        compiler_params=_cp(("parallel", "parallel")))(core, mine, theirs)


def _run_comm(comm, name):
    ci, co = len(comm.ins), len(comm.out_shapes)

    def body(*refs):
        comm.first(refs[:ci], refs[ci:ci + co], refs[ci + co:])
        comm.last(refs[:ci], refs[ci:ci + co], refs[ci + co:])

    return pl.pallas_call(body, name=name, out_shape=list(comm.out_shapes), in_specs=[_HBM] * ci,
                          out_specs=[_HBM] * co, scratch_shapes=list(comm.scratch))(*comm.ins)


def _flat_rows(n_elems, mult):
    rows = -(-n_elems // LANE)
    return -(-rows // mult) * mult


def _pack(arrs, lead, mult, dtype):
    lead_shape = arrs[0].shape[:lead]
    flat = jnp.concatenate([a.astype(dtype).reshape(lead_shape + (-1,)) for a in arrs], axis=-1)
    n = flat.shape[-1]
    rows = _flat_rows(n, mult)
    flat = jnp.pad(flat, [(0, 0)] * lead + [(0, rows * LANE - n)])
    return flat.reshape(lead_shape + (rows, LANE))


def _unpack(flat, lead, shapes):
    lead_shape = flat.shape[:lead]
    flat = flat.reshape(lead_shape + (-1,))
    out, off = [], 0
    for shp in shapes:
        n = math.prod(shp)
        out.append(flat[..., off:off + n].reshape(lead_shape + tuple(shp)))
        off += n
    return out


def _split8(full, ax, n):
    shp = full.shape
    return jnp.moveaxis(full.reshape(shp[:ax] + (N_DEV, n) + shp[ax + 1:]), ax, 0)


def _join8(g, ax):
    shp = g.shape[1:]
    return jnp.moveaxis(g, 0, ax).reshape(shp[:ax] + (N_DEV * shp[ax],) + shp[ax + 1:])


def _group_lanes(v, hg):
    v = v.reshape(SSM_GROUPS, hg)
    return jnp.pad(v, ((0, 0), (0, LANE - hg))).reshape(1, SSM_GROUPS * LANE)


def _ungroup_lanes(v, hg):
    return v.reshape(SSM_GROUPS, LANE)[:, :hg].reshape(1, SSM_GROUPS * hg)


def kernel(x, meta_tokens, a_norm_pre, a_w_in, a_conv_w, a_conv_b, a_dt_bias, a_a_log, a_d_skip, a_gate_norm, a_w_out, a_norm_post, kv_norm, w_kv, b_norm_pre, b_w_q, b_sinks, b_w_o, b_norm_post, f_norm_pre, f_w_up, f_conv_w, f_conv_b, f_w_down, f_norm_post, loss_target, m_meta_tokens, m_a_norm_pre, m_a_w_in, m_a_conv_w, m_a_conv_b, m_a_dt_bias, m_a_a_log, m_a_d_skip, m_a_gate_norm, m_a_w_out, m_a_norm_post, m_kv_norm, m_w_kv, m_b_norm_pre, m_b_w_q, m_b_sinks, m_b_w_o, m_b_norm_post, m_f_norm_pre, m_f_w_up, m_f_conv_w, m_f_conv_b, m_f_w_down, m_f_norm_post, v_meta_tokens, v_a_norm_pre, v_a_w_in, v_a_conv_w, v_a_conv_b, v_a_dt_bias, v_a_a_log, v_a_d_skip, v_a_gate_norm, v_a_w_out, v_a_norm_post, v_kv_norm, v_w_kv, v_b_norm_pre, v_b_w_q, v_b_sinks, v_b_w_o, v_b_norm_post, v_f_norm_pre, v_f_w_up, v_f_conv_w, v_f_conv_b, v_f_w_down, v_f_norm_post):
    args = locals()
    wts = {n: args[n] for n in WEIGHTS}
    mom = {n: args["m_" + n] for n in WEIGHTS}
    var = {n: args["v_" + n] for n in WEIGHTS}
    mx, my, mc = lax.axis_index("x"), lax.axis_index("y"), lax.axis_index("c")
    me = 4 * mx + 2 * my + mc
    rows = _seq_rows()
    hg = SSM_HEADS // SSM_GROUPS
    d = D_MODEL

    n_main = D_INNER + D_XBC
    ws_in, ws_up = a_w_in.shape[2], f_w_up.shape[2]
    segs_in = _col_segments(ws_in, [(0, 0, n_main)] + [(n_main + hg * g, n_main + LANE * g, hg)
                                                      for g in range(SSM_GROUPS)])
    segs_up = _col_segments(ws_up, [(0, 0, 2 * D_FF)])
    def gather_of(*ws):
        return _gather_comm([w.astype(BF16) for w in ws])

    g_in, small_full = _run_comm(_gather_comm([a_w_in[0].astype(BF16), _pack([wts[n] for n in SMALL], 0, 8, F32)]),
                                 "gather_first")
    full = {}
    for n, g in zip(SMALL, _unpack(small_full, 1, [wts[n].shape for n in SMALL])):
        full[n] = _join8(g, SHARD_AXIS[n])
    w_in_all = _assemble_cols(g_in, n_main + SSM_GROUPS * LANE, segs_in, "asm_w_in")
    w_up, w_down = [None, None], [None, None]
    bias_g = _group_lanes(wts["a_dt_bias"], hg)
    alog_g = _group_lanes(wts["a_a_log"], hg)
    dsk_g = _group_lanes(wts["a_d_skip"], hg)
    a_conv_w, a_conv_b = full["a_conv_w"][0], full["a_conv_b"]
    f_cw, f_cb = full["f_conv_w"], wts["f_conv_b"]
    fpre, fpost = wts["f_norm_pre"], wts["f_norm_post"]

    h0 = _place_rows([(full["meta_tokens"], 0), (x[0], N_META)], rows, "place_h0")
    tgt = _place_rows([(loss_target[0], N_META)], rows, "place_target")

    _, (hn0,) = _resid_norm(h0, None, None, [full["a_norm_pre"]], "norm_a_pre")
    zx, (g_out,) = _mm(hn0, w_in_all, "nn", F32, "mm_in", comm=gather_of(a_w_out[0]))
    w_out = g_out.reshape(D_INNER, d)
    xbc, (g_dn1,) = _conv_silu_fwd(zx, a_conv_w, a_conv_b, "conv_a", comm=gather_of(f_w_down[1]))
    (y_ssd, hst), (g_up0,) = _ssd_fwd(xbc, zx, bias_g, alog_g, dsk_g, "ssd_fwd", comm=gather_of(f_w_up[0]))
    w_up[0] = _assemble_cols(g_up0, 2 * D_FF, segs_up, "asm_w_up0")
    yn, (g_kv, g_q) = _gatenorm_fwd(y_ssd, zx, full["a_gate_norm"], "gatenorm", comm=gather_of(w_kv, b_w_q[0]))
    mix_a, (g_o,) = _mm(yn, w_out, "nn", F32, "mm_out", comm=gather_of(b_w_o[0]))
    w_kvf, w_q, w_o = g_kv.reshape(d, 2 * D_KV), g_q.reshape(d, d), g_o.reshape(d, d)
    h1, (fn0,) = _resid_norm(h0, mix_a, full["a_norm_post"], [fpre[0:1]], "resid_a")

    half = d // 2
    u0, (g_dn0,) = _mm(fn0, w_up[0], "nn", F32, "mm_up0", comm=gather_of(f_w_down[0]))
    w_down = [g_dn0.reshape(D_FF, d), g_dn1.reshape(D_FF, d)]
    act0, (g_up1a,) = _ffn_act_fwd(u0, f_cw[0], f_cb[0:1], "ffn_act0", comm=gather_of(f_w_up[1, :half]))
    ffn0 = _mm(act0, w_down[0], "nn", F32, "mm_down0")
    h2, (kvn, bn) = _resid_norm(h1, ffn0, fpost[0:1], [wts["kv_norm"].reshape(1, d), wts["b_norm_pre"]], "resid_f0")
    kv = _mm(kvn, w_kvf, "nn", F32, "mm_kv")
    q = _mm(bn, w_q, "nn", F32, "mm_q")
    o, (g_up1b,) = _attn_fwd(q, kv, wts["b_sinks"], "attn_fwd", comm=gather_of(f_w_up[1, half:]))
    w_up[1] = jnp.concatenate([_assemble_cols(g_up1a, 2 * D_FF, segs_up, "asm_w_up1a"),
                               _assemble_cols(g_up1b, 2 * D_FF, segs_up, "asm_w_up1b")], axis=0)
    mix_b = _mm(o, w_o, "nn", F32, "mm_o")
    h3, (fn1,) = _resid_norm(h2, mix_b, wts["b_norm_post"], [fpre[1:2]], "resid_b")
    u1 = _mm(fn1, w_up[1], "nn", F32, "mm_up1")
    act1 = _ffn_act_fwd(u1, f_cw[1], f_cb[1:2], "ffn_act1")
    ffn1 = _mm(act1, w_down[1], "nn", F32, "mm_down1")
    dh4, loss_row = _final_loss(h3, ffn1, fpost[1:2], tgt, "loss")
    loss = lax.psum(loss_row[0, 0], ("x", "y", "c"))

    grads = {}

    core = mc.astype(jnp.int32).reshape(1)

    def carried(res, comm):
        return res if comm is not None else (res, None)

    def ffn_bwd(dh_out, h_in, fn, u, act, ffn, i, c_dact=None, c_dwdown=None):
        dffn, dw_post = _norm_bwd(ffn, fpost[i:i + 1], dh_out, None, BF16, f"nb_fpost{i}")
        dact, got_a = carried(_mm(dffn, w_down[i], "nt", F32, f"mm_dact{i}", comm=c_dact), c_dact)
        dw_down, got_b = carried(_mm(act, dffn, "tn", BF16, f"mm_dwdown{i}", comm=c_dwdown), c_dwdown)
        dw_down = dw_down.reshape(N_DEV, -1, d)
        dg, dv, dwg, dwv, dbg, dbv = _ffn_act_bwd(u, dact, f_cw[i], f_cb[i:i + 1], f"ffn_act_bwd{i}")
        du = jnp.concatenate([dg, dv], axis=1)
        dfn, (s_dn,) = _mm(du, w_up[i], "nt", F32, f"mm_dfn{i}", comm=_swap_comm([dw_down]))
        sum_dn = _add_pairs(dw_down, s_dn, core, f"rs_add_dn{i}")
        dw_up = _scatter_cols(_mm(fn, du, "tn", BF16, f"mm_dwup{i}"), ws_up, segs_up, f"scat_w_up{i}")
        (dh_in, dw_pre), (s_up,) = _norm_bwd(h_in, fpre[i:i + 1], dfn, dh_out, F32, f"nb_fpre{i}",
                                             comm=_swap_comm([dw_up]))
        sum_up = _add_pairs(dw_up, s_up, core, f"rs_add_up{i}")
        return dh_in, dict(post=dw_post, sum_down=sum_dn, cw=jnp.concatenate([dwg, dwv], axis=1),
                           cb=jnp.concatenate([dbg, dbv], axis=1), sum_up=sum_up, pre=dw_pre), got_a, got_b

    dh3, gf1, _, _ = ffn_bwd(dh4, h3, fn1, u1, act1, ffn1, 1)
    dmix_b, grads["b_norm_post"] = _norm_bwd(mix_b, wts["b_norm_post"], dh3, None, BF16, "nb_bpost")
    do = _mm(dmix_b, w_o, "nt", F32, "mm_do")
    dw_o = _mm(o, dmix_b, "tn", BF16, "mm_dwo").reshape(N_DEV, -1, d)
    (dq, dkv, dsinks), (p_up1, p_dn1, s_o) = _attn_bwd(
        q, kv, wts["b_sinks"], do, "attn_bwd",
        comm=_join_comms([_chips_comm([gf1["sum_up"], gf1["sum_down"]]), _swap_comm([dw_o])]))
    sum_o = _add_pairs(dw_o, s_o, core, "rs_add_o")
    grads["b_sinks"] = dsinks[:, :N_Q_HEADS]
    dbn = _mm(dq, w_q, "nt", F32, "mm_dbn")
    dw_q = _mm(bn, dq, "tn", BF16, "mm_dwq").reshape(N_DEV, -1, d)
    dkv16 = dkv.astype(BF16)
    dkvn = _mm(dkv16, w_kvf, "nt", F32, "mm_dkvn")
    dw_kv = _mm(kvn, dkv16, "tn", BF16, "mm_dwkv").reshape(N_DEV, -1, 2 * D_KV)
    (dh2, grads["b_norm_pre"]), (s_q, s_kv) = _norm_bwd(h2, wts["b_norm_pre"], dbn, dh3, F32, "nb_bpre",
                                                        comm=_swap_comm([dw_q, dw_kv]))
    sum_q, sum_kv = _add_pairs(dw_q, s_q, core, "rs_add_q"), _add_pairs(dw_kv, s_kv, core, "rs_add_kv")
    dh2, dw_kvn = _norm_bwd(h2, wts["kv_norm"].reshape(1, d), dkvn, dh2, F32, "nb_kv")
    grads["kv_norm"] = dw_kvn.reshape(d)
    dh1, gf0, (p_o,), (p_q, p_kv) = ffn_bwd(dh2, h1, fn0, u0, act0, ffn0, 0, c_dact=_chips_comm([sum_o]),
                                            c_dwdown=_chips_comm([sum_q, sum_kv]))
    grads["f_norm_post"] = jnp.concatenate([gf0["post"], gf1["post"]], axis=0)
    grads["f_norm_pre"] = jnp.concatenate([gf0["pre"], gf1["pre"]], axis=0)
    grads["f_conv_w"] = jnp.stack([gf0["cw"], gf1["cw"]])
    grads["f_conv_b"] = jnp.concatenate([gf0["cb"], gf1["cb"]], axis=0)

    dmix_a, grads["a_norm_post"] = _norm_bwd(mix_a, full["a_norm_post"], dh1, None, BF16, "nb_apost")
    dyn = _mm(dmix_a, w_out, "nt", F32, "mm_dyn")
    dw_out = _mm(yn, dmix_a, "tn", BF16, "mm_dwout").reshape(N_DEV, -1, d)
    (dy_ssd, dz, grads["a_gate_norm"]), (s_out,) = _gatenorm_bwd(y_ssd, zx, full["a_gate_norm"], dyn, "gatenorm_bwd",
                                                                 comm=_swap_comm([dw_out]))
    sum_out = _add_pairs(dw_out, s_out, core, "rs_add_out")
    (dxs, dbm, dcm, ddtp, dalog, ddsk, dbias), (p_up0, p_dn0, p_out) = _ssd_bwd(
        xbc, zx, bias_g, alog_g, dsk_g, dy_ssd, hst, "ssd_bwd",
        comm=_chips_comm([gf0["sum_up"], gf0["sum_down"], sum_out]))
    dxbc = jnp.concatenate([dxs, dbm, dcm], axis=1)
    grads["a_a_log"] = _ungroup_lanes(dalog, hg)
    grads["a_d_skip"] = _ungroup_lanes(ddsk, hg)
    grads["a_dt_bias"] = _ungroup_lanes(dbias, hg)
    dpre, dcw, dcb = _conv_silu_bwd(zx, dxbc, a_conv_w, a_conv_b, "conv_a_bwd")
    grads["a_conv_w"], grads["a_conv_b"] = dcw[None], dcb
    dzx = jnp.concatenate([dz, dpre, ddtp.astype(BF16)], axis=1)
    dhn0 = _mm(dzx, w_in_all, "nt", F32, "mm_dhn0")
    dw_in_all = _mm(hn0, dzx, "tn", BF16, "mm_dwin")
    dw_in8 = _scatter_cols(dw_in_all, ws_in, segs_in, "scat_w_in")
    (dh0, grads["a_norm_pre"]), (s_in,) = _norm_bwd(h0, full["a_norm_pre"], dhn0, dh1, F32, "nb_apre",
                                                    comm=_swap_comm([dw_in8]))
    sum_in = _add_pairs(dw_in8, s_in, core, "rs_add_in")
    grad_x = dh0[N_META:N_META + SEQ][None]
    grads["meta_tokens"] = dh0[:N_META]

    small_local = _pack([_split8(grads[n], SHARD_AXIS[n], wts[n].shape[SHARD_AXIS[n]]) for n in SMALL], 1, 8, F32)
    repl_local = _pack([grads[n] for n in REPL], 0, 8, F32)
    n_sr = small_local.shape[1]
    small_vec = jnp.concatenate([small_local.reshape(N_DEV * n_sr, LANE), repl_local], axis=0)
    p_in, small_all = _run_comm(_join_comms([_chips_comm([sum_in]), _gather_comm([small_vec])]), "rs_tail")
    parts_big = dict(a_w_in=[p_in], a_w_out=[p_out], w_kv=[p_kv], b_w_q=[p_q], b_w_o=[p_o],
                     f_w_up=[p_up0, p_up1], f_w_down=[p_dn0, p_dn1])
    mine_small = lax.dynamic_slice_in_dim(small_all, me * n_sr, n_sr, axis=1)
    parts_small = jnp.concatenate([mine_small, small_all[:, N_DEV * n_sr:]], axis=1)

    def flat_f32(dct, names, mult):
        return _pack([dct[n] for n in names], 0, mult, F32)

    big_out = {}
    for n in BIG:
        shp3 = (len(parts_big[n]),) + parts_big[n][0].shape[1:]
        res = _adamw(parts_big[n], *[dct[n].reshape(shp3) for dct in (wts, mom, var)], f"adamw_{n}")
        big_out[n] = [r.reshape(wts[n].shape) for r in res]
    sm_in = [jnp.concatenate([flat_f32(dct, SMALL, 8), flat_f32(dct, REPL, 8)], axis=0)[None] for dct in (wts, mom, var)]
    small_out = [r[0] for r in _adamw([parts_small], *sm_in, "adamw_small")]

    outs = []
    for kind in range(4):
        res = {n: big_out[n][kind] for n in BIG}
        for n, a in zip(SMALL, _unpack(small_out[kind][:n_sr], 0, [wts[n].shape for n in SMALL])):
            res[n] = a
        for n, a in zip(REPL, _unpack(small_out[kind][n_sr:], 0, [wts[n].shape for n in REPL])):
            res[n] = a
        outs.append(res)
    return (loss, grad_x, *[outs[0][n] for n in WEIGHTS], *[outs[1][n] for n in WEIGHTS],
            *[outs[2][n] for n in WEIGHTS], *[outs[3][n] for n in WEIGHTS])
```

```python
import functools
import math

import jax
import jax.numpy as jnp
from jax import lax
from jax.experimental import pallas as pl
from jax.experimental.pallas import tpu as pltpu

F32, BF16 = jnp.float32, jnp.bfloat16
S = jax.ShapeDtypeStruct

D_MODEL = 1024
SEQ = 2048
N_META = 16
D_INNER = 2048
HEAD_P = 64
SSM_HEADS = D_INNER // HEAD_P
SSM_GROUPS = 4
D_STATE = 128
SSM_CONV = 4
D_BC = SSM_GROUPS * D_STATE
D_XBC = D_INNER + 2 * D_BC
ATTN_DH = 64
N_Q_HEADS = D_MODEL // ATTN_DH
N_KV_HEADS = 4
D_KV = N_KV_HEADS * ATTN_DH
WINDOW = 128
D_FF = 2816
FFN_CONV = 3
RMS_EPS = 1e-6
NEG = -1e30
LR, B1, B2, EPS, WD, STEP = 0.001, 0.9, 0.999, 1e-08, 0.01, 10

N_DEV = 8
T = 128
LANE = 128
VMEM_LIMIT = 48 * 1024 * 1024

BIG = ("a_w_in", "a_w_out", "w_kv", "b_w_q", "b_w_o", "f_w_up", "f_w_down")
SMALL = ("meta_tokens", "a_norm_pre", "a_conv_w", "a_conv_b", "a_gate_norm", "a_norm_post", "f_conv_w")
REPL = ("a_dt_bias", "a_a_log", "a_d_skip", "kv_norm", "b_norm_pre", "b_sinks", "b_norm_post",
        "f_norm_pre", "f_conv_b", "f_norm_post")
SHARD_AXIS = dict(a_w_in=2, a_w_out=1, w_kv=0, b_w_q=1, b_w_o=1, f_w_up=2, f_w_down=1, meta_tokens=1,
                  a_norm_pre=1, a_conv_w=2, a_conv_b=1, a_gate_norm=1, a_norm_post=1, f_conv_w=2)
WEIGHTS = ("meta_tokens", "a_norm_pre", "a_w_in", "a_conv_w", "a_conv_b", "a_dt_bias", "a_a_log", "a_d_skip",
           "a_gate_norm", "a_w_out", "a_norm_post", "kv_norm", "w_kv", "b_norm_pre", "b_w_q", "b_sinks", "b_w_o",
           "b_norm_post", "f_norm_pre", "f_w_up", "f_conv_w", "f_conv_b", "f_w_down", "f_norm_post")


def _seq_rows():
    return -(-(N_META + SEQ) // T) * T


def _cp(sem=None):
    return pltpu.CompilerParams(dimension_semantics=sem, vmem_limit_bytes=VMEM_LIMIT)


def _pick(n, target):
    t = min(n, target)
    t -= t % LANE
    while n % t:
        t -= LANE
    return t


def _sigmoid(x):
    return 1.0 / (1.0 + jnp.exp(-x))


def _softplus(x):
    return jnp.maximum(x, 0.0) + jnp.log(1.0 + jnp.exp(-jnp.abs(x)))


_NN = (((1,), (0,)), ((), ()))
_NT = (((1,), (1,)), ((), ()))
_TN = (((0,), (0,)), ((), ()))


def _dot(a, b, dims=_NN):
    return lax.dot_general(a, b, dims, preferred_element_type=F32)


def _dot_hi(a, b):
    return lax.dot_general(a, b, _NN, precision=lax.Precision.HIGHEST, preferred_element_type=F32)


_HBM = pl.BlockSpec(memory_space=pltpu.HBM)
_MESH = pl.DeviceIdType.MESH


class _Comm:
    def __init__(self, ins, out_shapes, scratch, first, last):
        self.ins, self.out_shapes, self.scratch, self.first, self.last = ins, out_shapes, scratch, first, last


def _call(body, name, out_shape, grid, in_specs, out_specs, sem, args, scratch=(), comm=None):
    if comm is None:
        return pl.pallas_call(body, name=name, out_shape=out_shape, grid=grid, in_specs=in_specs, out_specs=out_specs,
                              scratch_shapes=list(scratch), compiler_params=_cp(sem))(*args)
    single = not isinstance(out_shape, (list, tuple))
    outs = [out_shape] if single else list(out_shape)
    ospecs = [out_specs] if single else list(out_specs)
    n_in, n_out, n_scr, ci, co = len(in_specs), len(outs), len(scratch), len(comm.ins), len(comm.out_shapes)

    def carrier(*refs):
        p = 0
        parts = []
        for cnt in (n_in, ci, n_out, co, n_scr, len(comm.scratch)):
            parts.append(refs[p:p + cnt])
            p += cnt
        ins, cins, outs_r, couts, scr, cscr = parts
        ids = [pl.program_id(i) for i in range(len(grid))]
        first, last = ids[0] == 0, ids[0] == grid[0] - 1
        for i in range(1, len(grid)):
            first, last = first & (ids[i] == 0), last & (ids[i] == grid[i] - 1)

        @pl.when(first)
        def _():
            comm.first(cins, couts, cscr)

        body(*ins, *outs_r, *scr)

        @pl.when(last)
        def _():
            comm.last(cins, couts, cscr)

    res = pl.pallas_call(
        carrier, name=name, out_shape=outs + list(comm.out_shapes), grid=grid,
        in_specs=list(in_specs) + [_HBM] * ci, out_specs=ospecs + [_HBM] * co,
        scratch_shapes=list(scratch) + list(comm.scratch),
        compiler_params=_cp(("arbitrary",) * len(grid)))(*args, *comm.ins)
    mine = res[0] if single else list(res[:n_out])
    return mine, list(res[n_out:])


def _mm(a, b, mode, out_dtype, name, comm=None):
    if mode == "tn":
        m, kk = a.shape
        n = b.shape[1]
        tko, tn = _pick(kk, 512), _pick(n, 512)

        def body(a_ref, b_ref, o_ref):
            o_ref[...] = _dot(a_ref[...], b_ref[...], _TN).astype(o_ref.dtype)

        return _call(
            body, name, S((kk, n), out_dtype), (kk // tko, n // tn),
            [pl.BlockSpec((m, tko), lambda i, j: (0, i)), pl.BlockSpec((m, tn), lambda i, j: (0, j))],
            pl.BlockSpec((tko, tn), lambda i, j: (i, j)), ("parallel", "parallel"), (a, b), comm=comm)

    m, kk = a.shape
    n = b.shape[1] if mode == "nn" else b.shape[0]
    tn = _pick(n, 512)
    tk = kk if kk <= 2048 else _pick(kk, 1536)
    nk = kk // tk
    dims = _NN if mode == "nn" else _NT

    def body(a_ref, b_ref, o_ref, *acc):
        part = _dot(a_ref[...], b_ref[...], dims)
        if nk == 1:
            o_ref[...] = part.astype(o_ref.dtype)
        else:
            k = pl.program_id(1)

            @pl.when(k == 0)
            def _():
                acc[0][...] = part

            @pl.when(k > 0)
            def _():
                acc[0][...] += part

            @pl.when(k == nk - 1)
            def _():
                o_ref[...] = acc[0][...].astype(o_ref.dtype)

    b_spec = (pl.BlockSpec((tk, tn), lambda j, k: (k, j)) if mode == "nn"
              else pl.BlockSpec((tn, tk), lambda j, k: (j, k)))
    return _call(
        body, name, S((m, n), out_dtype), (n // tn, nk), [pl.BlockSpec((m, tk), lambda j, k: (0, k)), b_spec],
        pl.BlockSpec((m, tn), lambda j, k: (0, j)), ("parallel", "arbitrary"), (a, b),
        scratch=[pltpu.VMEM((m, tn), F32)] if nk > 1 else [], comm=comm)


def _rms(x, w):
    return x * lax.rsqrt(jnp.mean(x * x, axis=-1, keepdims=True) + RMS_EPS) * w


def _row_tile(rows):
    return rows // 8


def _resid_norm(h, br, w_post, next_ws, name):
    rows, d = h.shape
    tr = _row_tile(rows)
    has_br = br is not None
    nw = len(next_ws)

    def body(*refs):
        h_ref = refs[0]
        pos = 1
        x = h_ref[...]
        if has_br:
            x = x + _rms(refs[1][...], refs[2][...])
            pos = 3
        w_refs = refs[pos:pos + nw]
        outs = refs[pos + nw:]
        if has_br:
            outs[0][...] = x
            outs = outs[1:]
        for w_ref, o_ref in zip(w_refs, outs):
            o_ref[...] = _rms(x, w_ref[...]).astype(o_ref.dtype)

    row = pl.BlockSpec((tr, d), lambda i: (i, 0))
    vec = pl.BlockSpec((1, d), lambda i: (0, 0))
    ins = [h] + ([br, w_post] if has_br else []) + list(next_ws)
    in_specs = [row] + ([row, vec] if has_br else []) + [vec] * nw
    out_shape = ([S((rows, d), F32)] if has_br else []) + [S((rows, d), BF16)] * nw
    res = pl.pallas_call(body, name=name, out_shape=out_shape, grid=(rows // tr,), in_specs=in_specs,
                         out_specs=[row] * len(out_shape), compiler_params=_cp(("parallel",)))(*ins)
    if has_br:
        return res[0], list(res[1:])
    return h, list(res)


def _norm_bwd(x, w, dy, add, out_dtype, name, comm=None):
    rows, d = x.shape
    tr = _row_tile(rows)
    has_add = add is not None

    def body(*refs):
        x_ref, w_ref, dy_ref = refs[:3]
        dx_ref, dw_ref = refs[-2:]
        xv = x_ref[...]
        r = lax.rsqrt(jnp.mean(xv * xv, axis=-1, keepdims=True) + RMS_EPS)
        dyv = dy_ref[...].astype(F32)
        wdy = dyv * w_ref[...]
        dx = r * wdy - xv * (r * r * r) * jnp.mean(xv * wdy, axis=-1, keepdims=True)
        if has_add:
            dx = dx + refs[3][...]
        dx_ref[...] = dx.astype(dx_ref.dtype)

        @pl.when(pl.program_id(0) == 0)
        def _():
            dw_ref[...] = jnp.zeros_like(dw_ref)

        dw_ref[...] += jnp.sum(dyv * xv * r, axis=0, keepdims=True)

    row = pl.BlockSpec((tr, d), lambda i: (i, 0))
    vec = pl.BlockSpec((1, d), lambda i: (0, 0))
    ins = [x, w, dy] + ([add] if has_add else [])
    return _call(body, name, [S((rows, d), out_dtype), S((1, d), F32)], (rows // tr,),
                 [row, vec, row] + ([row] if has_add else []), [row, vec], ("arbitrary",), ins, comm=comm)


def _final_loss(h, br, w_post, tgt, name):
    rows, d = h.shape
    tr = _row_tile(rows)

    def body(h_ref, br_ref, w_ref, t_ref, dh_ref, loss_ref):
        i = pl.program_id(0)
        y = h_ref[...] + _rms(br_ref[...], w_ref[...])
        r = i * tr + lax.broadcasted_iota(jnp.int32, (tr, 1), 0)
        real = (r >= N_META) & (r < N_META + SEQ)
        diff = jnp.where(real, y - t_ref[...], 0.0)
        dh_ref[...] = diff * (1.0 / d)

        @pl.when(i == 0)
        def _():
            loss_ref[...] = jnp.zeros_like(loss_ref)

        loss_ref[...] += jnp.sum(diff * diff) * (0.5 / d)

    row = pl.BlockSpec((tr, d), lambda i: (i, 0))
    return pl.pallas_call(body, name=name, out_shape=[S((rows, d), F32), S((1, LANE), F32)], grid=(rows // tr,),
                          in_specs=[row, row, pl.BlockSpec((1, d), lambda i: (0, 0)), row],
                          out_specs=[row, pl.BlockSpec((1, LANE), lambda i: (0, 0))],
                          compiler_params=_cp(("arbitrary",)))(h, br, w_post, tgt)


def _gatenorm_fwd(y, zx, w, name, comm=None):
    rows, d = y.shape
    tr = _row_tile(rows)

    def body(y_ref, z_ref, w_ref, o_ref):
        z = z_ref[...]
        o_ref[...] = _rms(y_ref[...] * z * _sigmoid(z), w_ref[...]).astype(o_ref.dtype)

    row = pl.BlockSpec((tr, d), lambda i: (i, 0))
    return _call(body, name, S((rows, d), BF16), (rows // tr,), [row, row, pl.BlockSpec((1, d), lambda i: (0, 0))],
                 row, ("parallel",), (y, zx, w), comm=comm)


def _gatenorm_bwd(y, zx, w, dyn, name, comm=None):
    rows, d = y.shape
    tr = _row_tile(rows)

    def body(y_ref, z_ref, w_ref, dyn_ref, dy_ref, dz_ref, dw_ref):
        yv, z = y_ref[...], z_ref[...]
        sg = _sigmoid(z)
        sz = z * sg
        g = yv * sz
        r = lax.rsqrt(jnp.mean(g * g, axis=-1, keepdims=True) + RMS_EPS)
        dyn_v = dyn_ref[...]
        wdy = dyn_v * w_ref[...]
        dg = r * wdy - g * (r * r * r) * jnp.mean(g * wdy, axis=-1, keepdims=True)
        dy_ref[...] = dg * sz
        dz_ref[...] = (dg * yv * sg * (1.0 + z * (1.0 - sg))).astype(dz_ref.dtype)

        @pl.when(pl.program_id(0) == 0)
        def _():
            dw_ref[...] = jnp.zeros_like(dw_ref)

        dw_ref[...] += jnp.sum(dyn_v * g * r, axis=0, keepdims=True)

    row = pl.BlockSpec((tr, d), lambda i: (i, 0))
    vec = pl.BlockSpec((1, d), lambda i: (0, 0))
    return _call(body, name, [S((rows, d), F32), S((rows, d), BF16), S((1, d), F32)], (rows // tr,),
                 [row, row, vec, row], [row, row, vec], ("arbitrary",), (y, zx, w, dyn), comm=comm)


def _shift_down(x, s, rows_iota):
    if s == 0:
        return x
    return jnp.where(rows_iota >= s, pltpu.roll(x, s, 0), 0.0)


def _shift_up(x, s, rows_iota):
    if s == 0:
        return x
    rows = x.shape[0]
    return jnp.where(rows_iota < rows - s, pltpu.roll(x, rows - s, 0), 0.0)


def _r16(v):
    return v.astype(BF16).astype(F32)


def _conv(x, w_ref, b_ref, taps, rows_iota):
    x = _r16(x)
    acc = jnp.zeros_like(x)
    for k in range(taps):
        acc = acc + _r16(w_ref[k:k + 1, :]) * _shift_down(x, taps - 1 - k, rows_iota)
    return acc + b_ref[...]


def _conv_bwd(x, du, w_ref, dw_ref, db_ref, taps, rows_iota):
    db_ref[...] = jnp.sum(du, axis=0, keepdims=True)
    x, du = _r16(x), _r16(du)
    dx = jnp.zeros_like(x)
    for k in range(taps):
        s = taps - 1 - k
        dx = dx + _r16(w_ref[k:k + 1, :]) * _shift_up(du, s, rows_iota)
        dw_ref[k:k + 1, :] = jnp.sum(du * _shift_down(x, s, rows_iota), axis=0, keepdims=True)
    return dx


def _conv_silu_fwd(zx, w, b, name, comm=None):
    rows = zx.shape[0]
    cb = 512
    off = D_INNER // cb

    def body(x_ref, w_ref, b_ref, o_ref):
        it = lax.broadcasted_iota(jnp.int32, (rows, 1), 0)
        u = _conv(x_ref[...], w_ref, b_ref, SSM_CONV, it)
        o_ref[...] = u * _sigmoid(u)

    return _call(
        body, name, S((rows, D_XBC), F32), (D_XBC // cb,),
        [pl.BlockSpec((rows, cb), lambda j: (0, off + j)), pl.BlockSpec((SSM_CONV, cb), lambda j: (0, j)),
         pl.BlockSpec((1, cb), lambda j: (0, j))],
        pl.BlockSpec((rows, cb), lambda j: (0, j)), ("parallel",), (zx, w, b), comm=comm)


def _conv_silu_bwd(zx, dxbc, w, b, name):
    rows = zx.shape[0]
    cb = 512
    off = D_INNER // cb

    def body(x_ref, d_ref, w_ref, b_ref, dx_ref, dw_ref, db_ref):
        it = lax.broadcasted_iota(jnp.int32, (rows, 1), 0)
        x = x_ref[...]
        u = _conv(x, w_ref, b_ref, SSM_CONV, it)
        sg = _sigmoid(u)
        du = d_ref[...] * sg * (1.0 + u * (1.0 - sg))
        dx_ref[...] = _conv_bwd(x, du, w_ref, dw_ref, db_ref, SSM_CONV, it).astype(dx_ref.dtype)

    col = pl.BlockSpec((rows, cb), lambda j: (0, j))
    wsp = pl.BlockSpec((SSM_CONV, cb), lambda j: (0, j))
    bsp = pl.BlockSpec((1, cb), lambda j: (0, j))
    return pl.pallas_call(
        body, name=name, out_shape=[S((rows, D_XBC), BF16), S((SSM_CONV, D_XBC), F32), S((1, D_XBC), F32)],
        grid=(D_XBC // cb,), in_specs=[pl.BlockSpec((rows, cb), lambda j: (0, off + j)), col, wsp, bsp],
        out_specs=[col, wsp, bsp], compiler_params=_cp(("parallel",)))(zx, dxbc, w, b)


def _ffn_act_fwd(u, w, b, name, comm=None):
    rows = u.shape[0]
    cb = 256
    nb = D_FF // cb

    def body(g_ref, v_ref, wg_ref, wv_ref, bg_ref, bv_ref, o_ref):
        it = lax.broadcasted_iota(jnp.int32, (rows, 1), 0)
        g = _conv(g_ref[...], wg_ref, bg_ref, FFN_CONV, it)
        v = _conv(v_ref[...], wv_ref, bv_ref, FFN_CONV, it)
        o_ref[...] = (g * _sigmoid(g) * v).astype(o_ref.dtype)

    def sp(r, shift):
        return pl.BlockSpec((r, cb), lambda j: (0, shift + j))

    return _call(
        body, name, S((rows, D_FF), BF16), (nb,),
        [sp(rows, 0), sp(rows, nb), sp(FFN_CONV, 0), sp(FFN_CONV, nb), sp(1, 0), sp(1, nb)],
        sp(rows, 0), ("parallel",), (u, u, w, w, b, b), comm=comm)


def _ffn_act_bwd(u, dact, w, b, name, comm=None):
    rows = u.shape[0]
    cb = 256
    nb = D_FF // cb

    def body(g_ref, v_ref, d_ref, wg_ref, wv_ref, bg_ref, bv_ref, dg_ref, dv_ref, dwg_ref, dwv_ref, dbg_ref, dbv_ref):
        it = lax.broadcasted_iota(jnp.int32, (rows, 1), 0)
        xg, xv = g_ref[...], v_ref[...]
        g = _conv(xg, wg_ref, bg_ref, FFN_CONV, it)
        v = _conv(xv, wv_ref, bv_ref, FFN_CONV, it)
        sg = _sigmoid(g)
        d = d_ref[...]
        dgate = d * v * sg * (1.0 + g * (1.0 - sg))
        dval = d * g * sg
        dg_ref[...] = _conv_bwd(xg, dgate, wg_ref, dwg_ref, dbg_ref, FFN_CONV, it).astype(dg_ref.dtype)
        dv_ref[...] = _conv_bwd(xv, dval, wv_ref, dwv_ref, dbv_ref, FFN_CONV, it).astype(dv_ref.dtype)

    def sp(r, shift):
        return pl.BlockSpec((r, cb), lambda j: (0, shift + j))

    return _call(
        body, name,
        [S((rows, D_FF), BF16), S((rows, D_FF), BF16), S((FFN_CONV, D_FF), F32), S((FFN_CONV, D_FF), F32),
         S((1, D_FF), F32), S((1, D_FF), F32)],
        (nb,),
        [sp(rows, 0), sp(rows, nb), sp(rows, 0), sp(FFN_CONV, 0), sp(FFN_CONV, nb), sp(1, 0), sp(1, nb)],
        [sp(rows, 0), sp(rows, 0), sp(FFN_CONV, 0), sp(FFN_CONV, 0), sp(1, 0), sp(1, 0)],
        ("parallel",), (u, u, dact, w, w, b, b), comm=comm)


def _ssd_consts(dtp_ref, bias_ref, alog_ref, hg):
    lane = lax.broadcasted_iota(jnp.int32, (1, LANE), 1)
    pre = dtp_ref[...] + bias_ref[...]
    dt = _softplus(pre)
    a_row = jnp.where(lane < hg, -jnp.exp(alog_ref[...]), 0.0)
    ri = lax.broadcasted_iota(jnp.int32, (T, T), 0)
    ci = lax.broadcasted_iota(jnp.int32, (T, T), 1)
    cs = _dot_hi((ri >= ci).astype(F32), dt * a_row)
    return pre, dt, a_row, cs, ri, ci, lane


def _ssd_fwd(xbc, zx, bias, alog, dsk, name, comm=None):
    rows = xbc.shape[0]
    nc = rows // T
    hg = SSM_HEADS // SSM_GROUPS
    gw = hg * HEAD_P
    xoff, boff, coff = 0, D_INNER // D_STATE, (D_INNER + D_BC) // D_STATE
    dtoff = (D_INNER + D_XBC) // LANE

    def body(x_ref, b_ref, c_ref, dtp_ref, bias_ref, alog_ref, dsk_ref, y_ref, hst_ref, hs):
        c = pl.program_id(1)

        @pl.when(c == 0)
        def _():
            hs[...] = jnp.zeros_like(hs)

        _, dt, _, cs, ri, ci, _ = _ssd_consts(dtp_ref, bias_ref, alog_ref, hg)
        cst, dtt = cs.T, dt.T
        xt = x_ref[...].T
        bb, cbf = b_ref[...].astype(BF16), c_ref[...].astype(BF16)
        gt = _dot(bb, cbf, _NT)
        causal_t = ci >= ri
        dskv = dsk_ref[...]
        hall = hs[...]
        hst_ref[0, 0] = hall
        yts, new_h = [], []
        for k in range(hg):
            sl = slice(k * HEAD_P, (k + 1) * HEAD_P)
            csc, csr = cs[:, k:k + 1], cst[k:k + 1, :]
            lt = jnp.exp(jnp.where(causal_t, csr - csc, NEG))
            xk = xt[sl, :]
            xdt = xk * dtt[k:k + 1, :]
            hk = hall[sl, :]
            yd = _dot(xdt.astype(BF16), (gt * lt).astype(BF16))
            yo = jnp.exp(csr) * _dot(hk.astype(BF16), cbf, _NT)
            yts.append(yd + yo + dskv[:, k:k + 1] * xk)
            cl = cs[T - 1:T, k:k + 1]
            st = _dot((xdt * jnp.exp(cl - csr)).astype(BF16), bb)
            new_h.append(jnp.exp(cl) * hk + st)
        y_ref[...] = jnp.concatenate(yts, axis=0).T
        hs[...] = jnp.concatenate(new_h, axis=0)

    vec = pl.BlockSpec((1, LANE), lambda g, c: (0, g))
    return _call(
        body, name, [S((rows, D_INNER), F32), S((nc, SSM_GROUPS, gw, D_STATE), F32)], (SSM_GROUPS, nc),
        [pl.BlockSpec((T, gw), lambda g, c: (c, xoff + g)),
         pl.BlockSpec((T, D_STATE), lambda g, c: (c, boff + g)),
         pl.BlockSpec((T, D_STATE), lambda g, c: (c, coff + g)),
         pl.BlockSpec((T, LANE), lambda g, c: (c, dtoff + g)), vec, vec, vec],
        [pl.BlockSpec((T, gw), lambda g, c: (c, g)), pl.BlockSpec((1, 1, gw, D_STATE), lambda g, c: (c, g, 0, 0))],
        ("parallel", "arbitrary"), (xbc, xbc, xbc, zx, bias, alog, dsk),
        scratch=[pltpu.VMEM((gw, D_STATE), F32)], comm=comm)


def _ssd_bwd(xbc, zx, bias, alog, dsk, dy, hst, name, comm=None):
    rows = xbc.shape[0]
    nc = rows // T
    hg = SSM_HEADS // SSM_GROUPS
    gw = hg * HEAD_P
    boff, coff = D_INNER // D_STATE, (D_INNER + D_BC) // D_STATE
    dtoff = (D_INNER + D_XBC) // LANE

    def body(x_ref, b_ref, c_ref, dtp_ref, bias_ref, alog_ref, dsk_ref, dy_ref, hst_ref,
             dx_ref, db_ref, dc_ref, ddtp_ref, dalog_ref, ddsk_ref, dbias_ref, dhs):
        step = pl.program_id(1)

        @pl.when(step == 0)
        def _():
            dhs[...] = jnp.zeros_like(dhs)
            dalog_ref[...] = jnp.zeros_like(dalog_ref)
            ddsk_ref[...] = jnp.zeros_like(ddsk_ref)
            dbias_ref[...] = jnp.zeros_like(dbias_ref)

        pre, dt, a_row, cs, ri, ci, lane = _ssd_consts(dtp_ref, bias_ref, alog_ref, hg)
        cst, dtt = cs.T, dt.T
        xt, dyt = x_ref[...].T, dy_ref[...].T
        bb, cbf = b_ref[...].astype(BF16), c_ref[...].astype(BF16)
        gt = _dot(bb, cbf, _NT)
        causal_t = ci >= ri
        dskv = dsk_ref[...]
        hall, dhall = hst_ref[0, 0], dhs[...]
        head_row = lax.broadcasted_iota(jnp.int32, (T, 1), 0)
        last_l = lax.broadcasted_iota(jnp.int32, (1, T), 1) == T - 1
        dgt = jnp.zeros((T, T), F32)
        dc_acc = jnp.zeros((T, D_STATE), F32)
        db_acc = jnp.zeros((T, D_STATE), F32)
        ddt_rows = jnp.zeros((T, T), F32)
        dcs_rows = jnp.zeros((T, T), F32)
        qrow_cols = jnp.zeros((T, LANE), F32)
        ddsk_acc = jnp.zeros((1, LANE), F32)
        dxts, new_dh = [], []
        for k in range(hg):
            sl = slice(k * HEAD_P, (k + 1) * HEAD_P)
            csc, csr = cs[:, k:k + 1], cst[k:k + 1, :]
            lt = jnp.exp(jnp.where(causal_t, csr - csc, NEG))
            xk, dyk = xt[sl, :], dyt[sl, :]
            dtr, dk = dtt[k:k + 1, :], dskv[:, k:k + 1]
            xdt = xk * dtr
            mpt = gt * lt
            dyb = dyk.astype(BF16)
            dxdt = _dot(dyb, mpt.astype(BF16), _NT)
            dmt = _dot(xdt.astype(BF16), dyb, _TN)
            dgt = dgt + dmt * lt
            q = dmt * mpt
            q_rows = jnp.sum(q, axis=1, keepdims=True)
            q_cols = jnp.sum(q, axis=0, keepdims=True)
            hk, dhn = hall[sl, :], dhall[sl, :]
            e = jnp.exp(csr)
            cl = cs[T - 1:T, k:k + 1]
            wdec = jnp.exp(cl - csr)
            w = wdec * dtr
            rt = _dot(dhn.astype(BF16), bb, _NT)
            dxts.append(dtr * dxdt + dk * dyk + rt * w)
            xz = jnp.sum(xk * dxdt, axis=0, keepdims=True)
            dw = jnp.sum(rt * xk, axis=0, keepdims=True)
            dcl = jnp.exp(cl) * jnp.sum(dhn * hk) + jnp.sum(dw * w)
            yo = e * _dot(hk.astype(BF16), cbf, _NT)
            dcs_r = jnp.sum(dyk * yo, axis=0, keepdims=True) + q_cols - dw * w + jnp.where(last_l, dcl, 0.0)
            dye = (dyk * e).astype(BF16)
            dc_acc = dc_acc + _dot(dye, hk.astype(BF16), _TN)
            db_acc = db_acc + _dot((xk * w).astype(BF16), dhn.astype(BF16), _TN)
            new_dh.append(jnp.exp(cl) * dhn + _dot(dye, cbf))
            onehot = (lane == k).astype(F32)
            ddt_rows = ddt_rows + jnp.where(head_row == k, xz + dw * wdec, 0.0)
            dcs_rows = dcs_rows + jnp.where(head_row == k, dcs_r, 0.0)
            qrow_cols = qrow_cols + q_rows * onehot
            ddsk_acc = ddsk_acc + jnp.sum(dyk * xk) * onehot
        dx_ref[...] = jnp.concatenate(dxts, axis=0).T
        dhs[...] = jnp.concatenate(new_dh, axis=0)
        dc_ref[...] = _dot(dgt.T.astype(BF16), bb) + dc_acc
        db_ref[...] = _dot(dgt.astype(BF16), cbf) + db_acc
        da = _dot_hi((ci >= ri).astype(F32), dcs_rows.T - qrow_cols)
        ddtp = (ddt_rows.T + da * a_row) * _sigmoid(pre)
        ddtp = jnp.where(lane < hg, ddtp, 0.0)
        ddtp_ref[...] = ddtp
        dbias_ref[...] += jnp.sum(ddtp, axis=0, keepdims=True)
        dalog_ref[...] += jnp.sum(da * dt, axis=0, keepdims=True) * a_row
        ddsk_ref[...] += ddsk_acc

    def rc(c):
        return nc - 1 - c

    vec = pl.BlockSpec((1, LANE), lambda g, c: (0, g))
    xsp = pl.BlockSpec((T, gw), lambda g, c: (rc(c), g))
    return _call(
        body, name,
        [S((rows, D_INNER), F32), S((rows, D_BC), F32), S((rows, D_BC), F32),
         S((rows, SSM_GROUPS * LANE), F32), S((1, SSM_GROUPS * LANE), F32),
         S((1, SSM_GROUPS * LANE), F32), S((1, SSM_GROUPS * LANE), F32)],
        (SSM_GROUPS, nc),
        [xsp,
         pl.BlockSpec((T, D_STATE), lambda g, c: (rc(c), boff + g)),
         pl.BlockSpec((T, D_STATE), lambda g, c: (rc(c), coff + g)),
         pl.BlockSpec((T, LANE), lambda g, c: (rc(c), dtoff + g)), vec, vec, vec,
         xsp, pl.BlockSpec((1, 1, gw, D_STATE), lambda g, c: (rc(c), g, 0, 0))],
        [xsp,
         pl.BlockSpec((T, D_STATE), lambda g, c: (rc(c), g)),
         pl.BlockSpec((T, D_STATE), lambda g, c: (rc(c), g)),
         pl.BlockSpec((T, LANE), lambda g, c: (rc(c), g)), vec, vec, vec],
        ("parallel", "arbitrary"), (xbc, xbc, xbc, zx, bias, alog, dsk, dy, hst),
        scratch=[pltpu.VMEM((gw, D_STATE), F32)], comm=comm)


def _attn_tiles(kv_ref, j):
    prev = jnp.maximum(j - 1, 0)
    meta = kv_ref[0:T, :]
    prv = kv_ref[pl.ds(pl.multiple_of(prev * T, T), T), :]
    cur = kv_ref[pl.ds(pl.multiple_of(j * T, T), T), :]
    return jnp.concatenate([meta, prv, cur], axis=0)


def _attn_mask(j):
    r = j * T + lax.broadcasted_iota(jnp.int32, (3 * T, T), 1)
    row = lax.broadcasted_iota(jnp.int32, (3 * T, T), 0)
    t0, t1 = row < T, row < 2 * T
    s = jnp.where(t0, row, (j - 2) * T + row)
    ok = (s <= r) & ((s < N_META) | (s > r - WINDOW))
    use = (t0 & (j >= 2) & (row < N_META)) | (jnp.logical_not(t0) & t1 & (j >= 1)) | jnp.logical_not(t1)
    return ok & use


def _attn_fwd(q, kv, sinks, name, comm=None):
    rows = q.shape[0]
    scale = 1.0 / math.sqrt(ATTN_DH)
    qpk = N_Q_HEADS // N_KV_HEADS

    def body(q_ref, kv_ref, s_ref, o_ref, lse_ref):
        j = pl.program_id(0)
        kv3 = _attn_tiles(kv_ref, j).astype(BF16)
        mask = _attn_mask(j)
        qv = (q_ref[...] * scale).astype(BF16)
        sk = s_ref[...]
        lses = []
        for kh in range(N_KV_HEADS):
            k3 = kv3[:, kh * ATTN_DH:(kh + 1) * ATTN_DH]
            v3 = kv3[:, D_KV + kh * ATTN_DH:D_KV + (kh + 1) * ATTN_DH]
            for g in range(qpk):
                h = kh * qpk + g
                sink = sk[:, h:h + 1]
                sc = jnp.where(mask, _dot(k3, qv[:, h * ATTN_DH:(h + 1) * ATTN_DH], _NT), NEG)
                m = jnp.maximum(jnp.max(sc, axis=0, keepdims=True), sink)
                p = jnp.exp(sc - m)
                den = jnp.sum(p, axis=0, keepdims=True) + jnp.exp(sink - m)
                p = p * (1.0 / den)
                lses.append(m + jnp.log(den))
                o_ref[:, h * ATTN_DH:(h + 1) * ATTN_DH] = _dot(p.astype(BF16), v3, _TN).astype(o_ref.dtype)
        lse_ref[...] = jnp.concatenate(lses, axis=0)

    return _call(
        body, name, [S((rows, D_MODEL), BF16), S((N_Q_HEADS, rows), F32)], (rows // T,),
        [pl.BlockSpec((T, D_MODEL), lambda j: (j, 0)), pl.BlockSpec((rows, 2 * D_KV), lambda j: (0, 0)),
         pl.BlockSpec((1, N_Q_HEADS), lambda j: (0, 0))],
        [pl.BlockSpec((T, D_MODEL), lambda j: (j, 0)), pl.BlockSpec((N_Q_HEADS, T), lambda j: (0, j))],
        ("parallel",), (q, kv, sinks), comm=comm)


def _attn_bwd(q, kv, sinks, do, lse, name, comm=None):
    rows = q.shape[0]
    scale = 1.0 / math.sqrt(ATTN_DH)
    qpk = N_Q_HEADS // N_KV_HEADS

    def body(q_ref, kv_ref, s_ref, do_ref, lse_ref, dq_ref, dkv_ref, ds_ref):
        j = pl.program_id(0)

        @pl.when(j == 0)
        def _():
            dkv_ref[...] = jnp.zeros_like(dkv_ref)
            ds_ref[...] = jnp.zeros_like(ds_ref)

        kv3 = _attn_tiles(kv_ref, j).astype(BF16)
        mask = _attn_mask(j)
        qv = (q_ref[...] * scale).astype(BF16)
        dov = do_ref[...].astype(BF16)
        sk = s_ref[...]
        lsev = lse_ref[...]
        lane = lax.broadcasted_iota(jnp.int32, (1, LANE), 1)
        ds_acc = jnp.zeros((1, LANE), F32)
        prev = jnp.maximum(j - 1, 0)
        dqts = []
        for kh in range(N_KV_HEADS):
            ksl = slice(kh * ATTN_DH, (kh + 1) * ATTN_DH)
            vsl = slice(D_KV + kh * ATTN_DH, D_KV + (kh + 1) * ATTN_DH)
            k3, v3 = kv3[:, ksl], kv3[:, vsl]
            k3t = k3.T
            dk3 = jnp.zeros((3 * T, ATTN_DH), F32)
            dv3 = jnp.zeros((3 * T, ATTN_DH), F32)
            for g in range(qpk):
                h = kh * qpk + g
                hs = slice(h * ATTN_DH, (h + 1) * ATTN_DH)
                qh, doh = qv[:, hs], dov[:, hs]
                lh = lsev[h:h + 1, :]
                p = jnp.exp(jnp.where(mask, _dot(k3, qh, _NT), NEG) - lh)
                ps = jnp.exp(sk[:, h:h + 1] - lh)
                dp = _dot(v3, doh, _NT)
                delta = jnp.sum(p * dp, axis=0, keepdims=True)
                dsc = (p * (dp - delta)).astype(BF16)
                dqts.append(_dot(k3t, dsc) * scale)
                dk3 = dk3 + _dot(dsc, qh)
                dv3 = dv3 + _dot(p.astype(BF16), doh)
                ds_acc = ds_acc - jnp.sum(ps * delta) * (lane == h).astype(F32)
            for t, start in enumerate((0, pl.multiple_of(prev * T, T), pl.multiple_of(j * T, T))):
                rsl = pl.ds(start, T)
                dkv_ref[rsl, ksl] += dk3[t * T:(t + 1) * T, :]
                dkv_ref[rsl, vsl] += dv3[t * T:(t + 1) * T, :]
        ds_ref[...] += ds_acc
        dq_ref[...] = jnp.concatenate(dqts, axis=0).T.astype(dq_ref.dtype)

    blk = pl.BlockSpec((T, D_MODEL), lambda j: (j, 0))
    full = pl.BlockSpec((rows, 2 * D_KV), lambda j: (0, 0))
    return _call(
        body, name, [S((rows, D_MODEL), BF16), S((rows, 2 * D_KV), F32), S((1, LANE), F32)], (rows // T,),
        [blk, full, pl.BlockSpec((1, N_Q_HEADS), lambda j: (0, 0)), blk, pl.BlockSpec((N_Q_HEADS, T), lambda j: (0, j))],
        [blk, full, pl.BlockSpec((1, LANE), lambda j: (0, 0))], ("arbitrary",), (q, kv, sinks, do, lse), comm=comm)


BLOCK_BYTES = 1 << 20


def _div_tile(rows, cols):
    cap = max(16, BLOCK_BYTES // (4 * cols))
    best = None
    for t in range(16, min(rows, cap) + 1, 16):
        if rows % t == 0:
            best = t
    return best if best is not None else rows


def _adamw(parts, w, m, v, name):
    layers, rows, cols = w.shape
    n = parts[0].shape[0]
    tr = _div_tile(rows, cols)
    c1 = 1.0 / (1.0 - B1 ** STEP)
    c2 = 1.0 / (1.0 - B2 ** STEP)

    def body(*refs):
        p_refs = refs[:layers]
        w_ref, m_ref, v_ref, g_ref, d_ref, nm_ref, nv_ref = refs[layers:]
        layer = pl.program_id(0)
        for l in range(layers):
            @pl.when(layer == l)
            def _(p_ref=p_refs[l]):
                g = p_ref[0].astype(F32)
                for i in range(1, n):
                    g = g + p_ref[i].astype(F32)
                nm = B1 * m_ref[...] + (1.0 - B1) * g
                nv = B2 * v_ref[...] + (1.0 - B2) * (g * g)
                g_ref[...] = g
                nm_ref[...] = nm
                nv_ref[...] = nv
                d_ref[...] = -LR * ((nm * c1) / (jnp.sqrt(nv * c2) + EPS) + WD * w_ref[...])

    def part_spec(l):
        return pl.BlockSpec((n, tr, cols), lambda k, i: (0, jnp.where(k == l, i, 0), 0))

    row = pl.BlockSpec((None, tr, cols), lambda k, i: (k, i, 0))
    return pl.pallas_call(
        body, name=name, out_shape=[S((layers, rows, cols), F32)] * 4, grid=(layers, rows // tr),
        in_specs=[part_spec(l) for l in range(layers)] + [row, row, row], out_specs=[row] * 4,
        compiler_params=_cp(("parallel", "parallel")))(*parts, w, m, v)


def _col_segments(ws, runs):
    segs = []
    for glo, mlo, n in runs:
        while n > 0:
            d, off = divmod(glo, ws)
            take = min(n, ws - off)
            segs.append((d, off, mlo, take))
            glo, mlo, n = glo + take, mlo + take, n - take
    return segs


def _assemble_cols(g, width, segs, name):
    _, rows, ws = g.shape
    rb = _div_tile(rows, width // 2)

    def body(g_ref, o_ref):
        o_ref[...] = jnp.zeros_like(o_ref)
        for d, off, mlo, n in segs:
            o_ref[:, mlo:mlo + n] = g_ref[d, :, off:off + n]

    return pl.pallas_call(
        body, name=name, out_shape=S((rows, width), g.dtype), grid=(rows // rb,),
        in_specs=[pl.BlockSpec((N_DEV, rb, ws), lambda i: (0, i, 0))],
        out_specs=pl.BlockSpec((rb, width), lambda i: (i, 0)), compiler_params=_cp(("parallel",)))(g)


def _scatter_cols(dw, ws, segs, name):
    rows, width = dw.shape
    rb = _div_tile(rows, width)

    def body(w_ref, o_ref):
        for d, off, mlo, n in segs:
            o_ref[d, :, off:off + n] = w_ref[:, mlo:mlo + n].astype(o_ref.dtype)

    return pl.pallas_call(
        body, name=name, out_shape=S((N_DEV, rows, ws), BF16), grid=(rows // rb,),
        in_specs=[pl.BlockSpec((rb, width), lambda i: (i, 0))],
        out_specs=pl.BlockSpec((N_DEV, rb, ws), lambda i: (0, i, 0)), compiler_params=_cp(("parallel",)))(dw)


def _gather_comm(xs):
    n = len(xs)

    def setup(x_refs, out_refs, sems):
        send_sems, recv_sems, local_sems = sems
        mx, my, mc = lax.axis_index("x"), lax.axis_index("y"), lax.axis_index("c")
        me, sibling = (mx, my, mc), (mx, my, 1 - mc)
        chips = [(1 - mx, my), (mx, 1 - my), (1 - mx, 1 - my)]

        def blk(a, px, py, pc):
            return out_refs[a].at[4 * px + 2 * py + pc]

        def copy(a, k, block, to, src=None):
            return pltpu.make_async_remote_copy(
                src_ref=blk(a, *block) if src is None else src, dst_ref=blk(a, *block),
                send_sem=send_sems.at[a, k], recv_sem=recv_sems.at[a, k], device_id=to, device_id_type=_MESH)

        mine = [pltpu.make_async_copy(x_refs[a], blk(a, *me), local_sems.at[a]) for a in range(n)]
        own = []
        for a in range(n):
            own.append(copy(a, 0, me, sibling, src=x_refs[a]))
            own += [copy(a, 1 + i, me, (*chip, mc), src=x_refs[a]) for i, chip in enumerate(chips)]
        return me, sibling, chips, mc, copy, mine, own

    def first(x_refs, out_refs, sems):
        _, _, _, _, _, mine, own = setup(x_refs, out_refs, sems)
        for cp in mine + own:
            cp.start()

    def last(x_refs, out_refs, sems):
        me, sibling, chips, mc, copy, mine, own = setup(x_refs, out_refs, sems)
        passed = []
        for a in range(n):
            for i, chip in enumerate(chips):
                copy(a, 1 + i, (*chip, mc), me).wait_recv()
                passed.append(copy(a, 4 + i, (*chip, mc), sibling))
                passed[-1].start()
        for a in range(n):
            copy(a, 0, sibling, me).wait_recv()
            for i, chip in enumerate(chips):
                copy(a, 4 + i, (*chip, 1 - mc), me).wait_recv()
        for cp in own + passed:
            cp.wait_send()
        for cp in mine:
            cp.wait()

    return _Comm(list(xs), [S((N_DEV,) + x.shape, x.dtype) for x in xs],
                 [pltpu.SemaphoreType.DMA((n, 7)), pltpu.SemaphoreType.DMA((n, 7)), pltpu.SemaphoreType.DMA((n,))],
                 first, last)


def _swap_comm(gs):
    n = len(gs)

    def copies(g_refs, out_refs, sems):
        send_sems, recv_sems = sems
        mx, my, mc = lax.axis_index("x"), lax.axis_index("y"), lax.axis_index("c")
        return [pltpu.make_async_remote_copy(
            src_ref=g_refs[a].at[2 * k + 1 - mc], dst_ref=out_refs[a].at[k], send_sem=send_sems.at[a, k],
            recv_sem=recv_sems.at[a, k], device_id=(mx, my, 1 - mc), device_id_type=_MESH)
            for a in range(n) for k in range(4)]

    def first(g_refs, out_refs, sems):
        for cp in copies(g_refs, out_refs, sems):
            cp.start()

    def last(g_refs, out_refs, sems):
        for cp in copies(g_refs, out_refs, sems):
            cp.wait()

    return _Comm(list(gs), [S((4,) + g.shape[1:], g.dtype) for g in gs],
                 [pltpu.SemaphoreType.DMA((n, 4)), pltpu.SemaphoreType.DMA((n, 4))], first, last)


def _chips_comm(parts):
    n = len(parts)

    def copies(p_refs, out_refs, sems):
        send_sems, recv_sems, local_sems = sems
        mx, my, mc = lax.axis_index("x"), lax.axis_index("y"), lax.axis_index("c")
        mychip = 2 * mx + my
        chips = [(1 - mx, my), (mx, 1 - my), (1 - mx, 1 - my)]
        mine = [pltpu.make_async_copy(p_refs[a].at[mychip], out_refs[a].at[mychip], local_sems.at[a])
                for a in range(n)]
        return mine + [pltpu.make_async_remote_copy(
            src_ref=p_refs[a].at[2 * cx + cy], dst_ref=out_refs[a].at[mychip], send_sem=send_sems.at[a, i],
            recv_sem=recv_sems.at[a, i], device_id=(cx, cy, mc), device_id_type=_MESH)
            for a in range(n) for i, (cx, cy) in enumerate(chips)]

    def first(p_refs, out_refs, sems):
        for cp in copies(p_refs, out_refs, sems):
            cp.start()

    def last(p_refs, out_refs, sems):
        for cp in copies(p_refs, out_refs, sems):
            cp.wait()

    return _Comm(list(parts), [S(p.shape, p.dtype) for p in parts],
                 [pltpu.SemaphoreType.DMA((n, 3)), pltpu.SemaphoreType.DMA((n, 3)), pltpu.SemaphoreType.DMA((n,))],
                 first, last)


def _join_comms(comms):
    def split(refs, counts):
        out, p = [], 0
        for cnt in counts:
            out.append(refs[p:p + cnt])
            p += cnt
        return out

    ni = [len(c.ins) for c in comms]
    no = [len(c.out_shapes) for c in comms]
    ns = [len(c.scratch) for c in comms]

    def first(in_refs, out_refs, sems):
        for c, i, o, s in zip(comms, split(in_refs, ni), split(out_refs, no), split(sems, ns)):
            c.first(i, o, s)

    def last(in_refs, out_refs, sems):
        for c, i, o, s in zip(comms, split(in_refs, ni), split(out_refs, no), split(sems, ns)):
            c.last(i, o, s)

    return _Comm([x for c in comms for x in c.ins], [x for c in comms for x in c.out_shapes],
                 [x for c in comms for x in c.scratch], first, last)


def _add_pairs(mine, theirs, core, name):
    _, rows, cols = mine.shape
    tr = _div_tile(rows, cols)

    def body(core_ref, a_ref, b_ref, o_ref):
        o_ref[...] = (a_ref[...].astype(F32) + b_ref[...].astype(F32)).astype(o_ref.dtype)

    return pl.pallas_call(
        body, name=name, out_shape=S((4, rows, cols), BF16),
        grid_spec=pltpu.PrefetchScalarGridSpec(
            num_scalar_prefetch=1, grid=(4, rows // tr),
            in_specs=[pl.BlockSpec((None, tr, cols), lambda k, i, c: (2 * k + c[0], i, 0)),
                      pl.BlockSpec((None, tr, cols), lambda k, i, c: (k, i, 0))],
            out_specs=pl.BlockSpec((None, tr, cols), lambda k, i, c: (k, i, 0))),
        compiler_params=_cp(("parallel", "parallel")))(core, mine, theirs)


def _run_comm(comm, name):
    ci, co = len(comm.ins), len(comm.out_shapes)

    def body(*refs):
        comm.first(refs[:ci], refs[ci:ci + co], refs[ci + co:])
        comm.last(refs[:ci], refs[ci:ci + co], refs[ci + co:])

    return pl.pallas_call(body, name=name, out_shape=list(comm.out_shapes), in_specs=[_HBM] * ci,
                          out_specs=[_HBM] * co, scratch_shapes=list(comm.scratch))(*comm.ins)


def _flat_rows(n_elems, mult):
    rows = -(-n_elems // LANE)
    return -(-rows // mult) * mult


def _pack(arrs, lead, mult, dtype):
    lead_shape = arrs[0].shape[:lead]
    flat = jnp.concatenate([a.astype(dtype).reshape(lead_shape + (-1,)) for a in arrs], axis=-1)
    n = flat.shape[-1]
    rows = _flat_rows(n, mult)
    flat = jnp.pad(flat, [(0, 0)] * lead + [(0, rows * LANE - n)])
    return flat.reshape(lead_shape + (rows, LANE))


def _unpack(flat, lead, shapes):
    lead_shape = flat.shape[:lead]
    flat = flat.reshape(lead_shape + (-1,))
    out, off = [], 0
    for shp in shapes:
        n = math.prod(shp)
        out.append(flat[..., off:off + n].reshape(lead_shape + tuple(shp)))
        off += n
    return out


def _split8(full, ax, n):
    shp = full.shape
    return jnp.moveaxis(full.reshape(shp[:ax] + (N_DEV, n) + shp[ax + 1:]), ax, 0)


def _join8(g, ax):
    shp = g.shape[1:]
    return jnp.moveaxis(g, 0, ax).reshape(shp[:ax] + (N_DEV * shp[ax],) + shp[ax + 1:])


def _group_lanes(v, hg):
    v = v.reshape(SSM_GROUPS, hg)
    return jnp.pad(v, ((0, 0), (0, LANE - hg))).reshape(1, SSM_GROUPS * LANE)


def _ungroup_lanes(v, hg):
    return v.reshape(SSM_GROUPS, LANE)[:, :hg].reshape(1, SSM_GROUPS * hg)


def kernel(x, meta_tokens, a_norm_pre, a_w_in, a_conv_w, a_conv_b, a_dt_bias, a_a_log, a_d_skip, a_gate_norm, a_w_out, a_norm_post, kv_norm, w_kv, b_norm_pre, b_w_q, b_sinks, b_w_o, b_norm_post, f_norm_pre, f_w_up, f_conv_w, f_conv_b, f_w_down, f_norm_post, loss_target, m_meta_tokens, m_a_norm_pre, m_a_w_in, m_a_conv_w, m_a_conv_b, m_a_dt_bias, m_a_a_log, m_a_d_skip, m_a_gate_norm, m_a_w_out, m_a_norm_post, m_kv_norm, m_w_kv, m_b_norm_pre, m_b_w_q, m_b_sinks, m_b_w_o, m_b_norm_post, m_f_norm_pre, m_f_w_up, m_f_conv_w, m_f_conv_b, m_f_w_down, m_f_norm_post, v_meta_tokens, v_a_norm_pre, v_a_w_in, v_a_conv_w, v_a_conv_b, v_a_dt_bias, v_a_a_log, v_a_d_skip, v_a_gate_norm, v_a_w_out, v_a_norm_post, v_kv_norm, v_w_kv, v_b_norm_pre, v_b_w_q, v_b_sinks, v_b_w_o, v_b_norm_post, v_f_norm_pre, v_f_w_up, v_f_conv_w, v_f_conv_b, v_f_w_down, v_f_norm_post):
    args = locals()
    wts = {n: args[n] for n in WEIGHTS}
    mom = {n: args["m_" + n] for n in WEIGHTS}
    var = {n: args["v_" + n] for n in WEIGHTS}
    mx, my, mc = lax.axis_index("x"), lax.axis_index("y"), lax.axis_index("c")
    me = 4 * mx + 2 * my + mc
    rows = _seq_rows()
    hg = SSM_HEADS // SSM_GROUPS
    d = D_MODEL

    n_main = D_INNER + D_XBC
    ws_in, ws_up = a_w_in.shape[2], f_w_up.shape[2]
    segs_in = _col_segments(ws_in, [(0, 0, n_main)] + [(n_main + hg * g, n_main + LANE * g, hg)
                                                      for g in range(SSM_GROUPS)])
    segs_up = _col_segments(ws_up, [(0, 0, 2 * D_FF)])
    def gather_of(*ws):
        return _gather_comm([w.astype(BF16) for w in ws])

    g_in, small_full = _run_comm(_gather_comm([a_w_in[0].astype(BF16), _pack([wts[n] for n in SMALL], 0, 8, F32)]),
                                 "gather_first")
    full = {}
    for n, g in zip(SMALL, _unpack(small_full, 1, [wts[n].shape for n in SMALL])):
        full[n] = _join8(g, SHARD_AXIS[n])
    w_in_all = _assemble_cols(g_in, n_main + SSM_GROUPS * LANE, segs_in, "asm_w_in")
    w_up, w_down = [None, None], [None, None]
    bias_g = _group_lanes(wts["a_dt_bias"], hg)
    alog_g = _group_lanes(wts["a_a_log"], hg)
    dsk_g = _group_lanes(wts["a_d_skip"], hg)
    a_conv_w, a_conv_b = full["a_conv_w"][0], full["a_conv_b"]
    f_cw, f_cb = full["f_conv_w"], wts["f_conv_b"]
    fpre, fpost = wts["f_norm_pre"], wts["f_norm_post"]

    pad_rows = rows - N_META - SEQ
    h0 = jnp.concatenate([full["meta_tokens"], x[0], jnp.zeros((pad_rows, d), F32)], axis=0)
    tgt = jnp.pad(loss_target[0], ((N_META, pad_rows), (0, 0)))

    _, (hn0,) = _resid_norm(h0, None, None, [full["a_norm_pre"]], "norm_a_pre")
    zx, (g_out,) = _mm(hn0, w_in_all, "nn", F32, "mm_in", comm=gather_of(a_w_out[0]))
    w_out = g_out.reshape(D_INNER, d)
    xbc, (g_dn1,) = _conv_silu_fwd(zx, a_conv_w, a_conv_b, "conv_a", comm=gather_of(f_w_down[1]))
    (y_ssd, hst), (g_up0,) = _ssd_fwd(xbc, zx, bias_g, alog_g, dsk_g, "ssd_fwd", comm=gather_of(f_w_up[0]))
    w_up[0] = _assemble_cols(g_up0, 2 * D_FF, segs_up, "asm_w_up0")
    yn, (g_kv, g_q) = _gatenorm_fwd(y_ssd, zx, full["a_gate_norm"], "gatenorm", comm=gather_of(w_kv, b_w_q[0]))
    mix_a, (g_o,) = _mm(yn, w_out, "nn", F32, "mm_out", comm=gather_of(b_w_o[0]))
    w_kvf, w_q, w_o = g_kv.reshape(d, 2 * D_KV), g_q.reshape(d, d), g_o.reshape(d, d)
    h1, (fn0,) = _resid_norm(h0, mix_a, full["a_norm_post"], [fpre[0:1]], "resid_a")

    half = d // 2
    u0, (g_dn0,) = _mm(fn0, w_up[0], "nn", F32, "mm_up0", comm=gather_of(f_w_down[0]))
    w_down = [g_dn0.reshape(D_FF, d), g_dn1.reshape(D_FF, d)]
    act0, (g_up1a,) = _ffn_act_fwd(u0, f_cw[0], f_cb[0:1], "ffn_act0", comm=gather_of(f_w_up[1, :half]))
    ffn0 = _mm(act0, w_down[0], "nn", F32, "mm_down0")
    h2, (kvn, bn) = _resid_norm(h1, ffn0, fpost[0:1], [wts["kv_norm"].reshape(1, d), wts["b_norm_pre"]], "resid_f0")
    kv = _mm(kvn, w_kvf, "nn", F32, "mm_kv")
    q = _mm(bn, w_q, "nn", F32, "mm_q")
    (o, lse), (g_up1b,) = _attn_fwd(q, kv, wts["b_sinks"], "attn_fwd", comm=gather_of(f_w_up[1, half:]))
    w_up[1] = jnp.concatenate([_assemble_cols(g_up1a, 2 * D_FF, segs_up, "asm_w_up1a"),
                               _assemble_cols(g_up1b, 2 * D_FF, segs_up, "asm_w_up1b")], axis=0)
    mix_b = _mm(o, w_o, "nn", F32, "mm_o")
    h3, (fn1,) = _resid_norm(h2, mix_b, wts["b_norm_post"], [fpre[1:2]], "resid_b")
    u1 = _mm(fn1, w_up[1], "nn", F32, "mm_up1")
    act1 = _ffn_act_fwd(u1, f_cw[1], f_cb[1:2], "ffn_act1")
    ffn1 = _mm(act1, w_down[1], "nn", F32, "mm_down1")
    dh4, loss_row = _final_loss(h3, ffn1, fpost[1:2], tgt, "loss")
    loss = lax.psum(loss_row[0, 0], ("x", "y", "c"))

    grads = {}

    core = mc.astype(jnp.int32).reshape(1)

    def carried(res, comm):
        return res if comm is not None else (res, None)

    def ffn_bwd(dh_out, h_in, fn, u, act, ffn, i, c_dact=None, c_dwdown=None):
        dffn, dw_post = _norm_bwd(ffn, fpost[i:i + 1], dh_out, None, BF16, f"nb_fpost{i}")
        dact, got_a = carried(_mm(dffn, w_down[i], "nt", F32, f"mm_dact{i}", comm=c_dact), c_dact)
        dw_down, got_b = carried(_mm(act, dffn, "tn", BF16, f"mm_dwdown{i}", comm=c_dwdown), c_dwdown)
        dw_down = dw_down.reshape(N_DEV, -1, d)
        dg, dv, dwg, dwv, dbg, dbv = _ffn_act_bwd(u, dact, f_cw[i], f_cb[i:i + 1], f"ffn_act_bwd{i}")
        du = jnp.concatenate([dg, dv], axis=1)
        dfn, (s_dn,) = _mm(du, w_up[i], "nt", F32, f"mm_dfn{i}", comm=_swap_comm([dw_down]))
        sum_dn = _add_pairs(dw_down, s_dn, core, f"rs_add_dn{i}")
        dw_up = _scatter_cols(_mm(fn, du, "tn", BF16, f"mm_dwup{i}"), ws_up, segs_up, f"scat_w_up{i}")
        (dh_in, dw_pre), (s_up,) = _norm_bwd(h_in, fpre[i:i + 1], dfn, dh_out, F32, f"nb_fpre{i}",
                                             comm=_swap_comm([dw_up]))
        sum_up = _add_pairs(dw_up, s_up, core, f"rs_add_up{i}")
        return dh_in, dict(post=dw_post, sum_down=sum_dn, cw=jnp.concatenate([dwg, dwv], axis=1),
                           cb=jnp.concatenate([dbg, dbv], axis=1), sum_up=sum_up, pre=dw_pre), got_a, got_b

    dh3, gf1, _, _ = ffn_bwd(dh4, h3, fn1, u1, act1, ffn1, 1)
    dmix_b, grads["b_norm_post"] = _norm_bwd(mix_b, wts["b_norm_post"], dh3, None, BF16, "nb_bpost")
    do = _mm(dmix_b, w_o, "nt", F32, "mm_do")
    dw_o = _mm(o, dmix_b, "tn", BF16, "mm_dwo").reshape(N_DEV, -1, d)
    (dq, dkv, dsinks), (p_up1, p_dn1, s_o) = _attn_bwd(
        q, kv, wts["b_sinks"], do, lse, "attn_bwd",
        comm=_join_comms([_chips_comm([gf1["sum_up"], gf1["sum_down"]]), _swap_comm([dw_o])]))
    sum_o = _add_pairs(dw_o, s_o, core, "rs_add_o")
    grads["b_sinks"] = dsinks[:, :N_Q_HEADS]
    dbn = _mm(dq, w_q, "nt", F32, "mm_dbn")
    dw_q = _mm(bn, dq, "tn", BF16, "mm_dwq").reshape(N_DEV, -1, d)
    dkv16 = dkv.astype(BF16)
    dkvn = _mm(dkv16, w_kvf, "nt", F32, "mm_dkvn")
    dw_kv = _mm(kvn, dkv16, "tn", BF16, "mm_dwkv").reshape(N_DEV, -1, 2 * D_KV)
    (dh2, grads["b_norm_pre"]), (s_q, s_kv) = _norm_bwd(h2, wts["b_norm_pre"], dbn, dh3, F32, "nb_bpre",
                                                        comm=_swap_comm([dw_q, dw_kv]))
    sum_q, sum_kv = _add_pairs(dw_q, s_q, core, "rs_add_q"), _add_pairs(dw_kv, s_kv, core, "rs_add_kv")
    dh2, dw_kvn = _norm_bwd(h2, wts["kv_norm"].reshape(1, d), dkvn, dh2, F32, "nb_kv")
    grads["kv_norm"] = dw_kvn.reshape(d)
    dh1, gf0, (p_o,), (p_q, p_kv) = ffn_bwd(dh2, h1, fn0, u0, act0, ffn0, 0, c_dact=_chips_comm([sum_o]),
                                            c_dwdown=_chips_comm([sum_q, sum_kv]))
    grads["f_norm_post"] = jnp.concatenate([gf0["post"], gf1["post"]], axis=0)
    grads["f_norm_pre"] = jnp.concatenate([gf0["pre"], gf1["pre"]], axis=0)
    grads["f_conv_w"] = jnp.stack([gf0["cw"], gf1["cw"]])
    grads["f_conv_b"] = jnp.concatenate([gf0["cb"], gf1["cb"]], axis=0)

    dmix_a, grads["a_norm_post"] = _norm_bwd(mix_a, full["a_norm_post"], dh1, None, BF16, "nb_apost")
    dyn = _mm(dmix_a, w_out, "nt", F32, "mm_dyn")
    dw_out = _mm(yn, dmix_a, "tn", BF16, "mm_dwout").reshape(N_DEV, -1, d)
    (dy_ssd, dz, grads["a_gate_norm"]), (s_out,) = _gatenorm_bwd(y_ssd, zx, full["a_gate_norm"], dyn, "gatenorm_bwd",
                                                                 comm=_swap_comm([dw_out]))
    sum_out = _add_pairs(dw_out, s_out, core, "rs_add_out")
    (dxs, dbm, dcm, ddtp, dalog, ddsk, dbias), (p_up0, p_dn0, p_out) = _ssd_bwd(
        xbc, zx, bias_g, alog_g, dsk_g, dy_ssd, hst, "ssd_bwd",
        comm=_chips_comm([gf0["sum_up"], gf0["sum_down"], sum_out]))
    dxbc = jnp.concatenate([dxs, dbm, dcm], axis=1)
    grads["a_a_log"] = _ungroup_lanes(dalog, hg)
    grads["a_d_skip"] = _ungroup_lanes(ddsk, hg)
    grads["a_dt_bias"] = _ungroup_lanes(dbias, hg)
    dpre, dcw, dcb = _conv_silu_bwd(zx, dxbc, a_conv_w, a_conv_b, "conv_a_bwd")
    grads["a_conv_w"], grads["a_conv_b"] = dcw[None], dcb
    dzx = jnp.concatenate([dz, dpre, ddtp.astype(BF16)], axis=1)
    dhn0 = _mm(dzx, w_in_all, "nt", F32, "mm_dhn0")
    dw_in_all = _mm(hn0, dzx, "tn", BF16, "mm_dwin")
    dw_in8 = _scatter_cols(dw_in_all, ws_in, segs_in, "scat_w_in")
    (dh0, grads["a_norm_pre"]), (s_in,) = _norm_bwd(h0, full["a_norm_pre"], dhn0, dh1, F32, "nb_apre",
                                                    comm=_swap_comm([dw_in8]))
    sum_in = _add_pairs(dw_in8, s_in, core, "rs_add_in")
    grad_x = dh0[N_META:N_META + SEQ][None]
    grads["meta_tokens"] = dh0[:N_META]

    small_local = _pack([_split8(grads[n], SHARD_AXIS[n], wts[n].shape[SHARD_AXIS[n]]) for n in SMALL], 1, 8, F32)
    repl_local = _pack([grads[n] for n in REPL], 0, 8, F32)
    n_sr = small_local.shape[1]
    small_vec = jnp.concatenate([small_local.reshape(N_DEV * n_sr, LANE), repl_local], axis=0)
    p_in, small_all = _run_comm(_join_comms([_chips_comm([sum_in]), _gather_comm([small_vec])]), "rs_tail")
    parts_big = dict(a_w_in=[p_in], a_w_out=[p_out], w_kv=[p_kv], b_w_q=[p_q], b_w_o=[p_o],
                     f_w_up=[p_up0, p_up1], f_w_down=[p_dn0, p_dn1])
    mine_small = lax.dynamic_slice_in_dim(small_all, me * n_sr, n_sr, axis=1)
    parts_small = jnp.concatenate([mine_small, small_all[:, N_DEV * n_sr:]], axis=1)

    def flat_f32(dct, names, mult):
        return _pack([dct[n] for n in names], 0, mult, F32)

    big_out = {}
    for n in BIG:
        shp3 = (len(parts_big[n]),) + parts_big[n][0].shape[1:]
        res = _adamw(parts_big[n], *[dct[n].reshape(shp3) for dct in (wts, mom, var)], f"adamw_{n}")
        big_out[n] = [r.reshape(wts[n].shape) for r in res]
    sm_in = [jnp.concatenate([flat_f32(dct, SMALL, 8), flat_f32(dct, REPL, 8)], axis=0)[None] for dct in (wts, mom, var)]
    small_out = [r[0] for r in _adamw([parts_small], *sm_in, "adamw_small")]

    outs = []
    for kind in range(4):
        res = {n: big_out[n][kind] for n in BIG}
        for n, a in zip(SMALL, _unpack(small_out[kind][:n_sr], 0, [wts[n].shape for n in SMALL])):
            res[n] = a
        for n, a in zip(REPL, _unpack(small_out[kind][n_sr:], 0, [wts[n].shape for n in REPL])):
            res[n] = a
        outs.append(res)
    return (loss, grad_x, *[outs[0][n] for n in WEIGHTS], *[outs[1][n] for n in WEIGHTS],
            *[outs[2][n] for n in WEIGHTS], *[outs[3][n] for n in WEIGHTS])
```

```python
import functools
import math

import jax
import jax.numpy as jnp
from jax import lax
from jax.experimental import pallas as pl
from jax.experimental.pallas import tpu as pltpu

F32, BF16 = jnp.float32, jnp.bfloat16
S = jax.ShapeDtypeStruct

D_MODEL = 1024
SEQ = 2048
N_META = 16
D_INNER = 2048
HEAD_P = 64
SSM_HEADS = D_INNER // HEAD_P
SSM_GROUPS = 4
D_STATE = 128
SSM_CONV = 4
D_BC = SSM_GROUPS * D_STATE
D_XBC = D_INNER + 2 * D_BC
ATTN_DH = 64
N_Q_HEADS = D_MODEL // ATTN_DH
N_KV_HEADS = 4
D_KV = N_KV_HEADS * ATTN_DH
WINDOW = 128
D_FF = 2816
FFN_CONV = 3
RMS_EPS = 1e-6
NEG = -1e30
LR, B1, B2, EPS, WD, STEP = 0.001, 0.9, 0.999, 1e-08, 0.01, 10

N_DEV = 8
T = 128
LANE = 128
VMEM_LIMIT = 48 * 1024 * 1024

BIG = ("a_w_in", "a_w_out", "w_kv", "b_w_q", "b_w_o", "f_w_up", "f_w_down")
SMALL = ("meta_tokens", "a_norm_pre", "a_conv_w", "a_conv_b", "a_gate_norm", "a_norm_post", "f_conv_w")
REPL = ("a_dt_bias", "a_a_log", "a_d_skip", "kv_norm", "b_norm_pre", "b_sinks", "b_norm_post",
        "f_norm_pre", "f_conv_b", "f_norm_post")
SHARD_AXIS = dict(a_w_in=2, a_w_out=1, w_kv=0, b_w_q=1, b_w_o=1, f_w_up=2, f_w_down=1, meta_tokens=1,
                  a_norm_pre=1, a_conv_w=2, a_conv_b=1, a_gate_norm=1, a_norm_post=1, f_conv_w=2)
WEIGHTS = ("meta_tokens", "a_norm_pre", "a_w_in", "a_conv_w", "a_conv_b", "a_dt_bias", "a_a_log", "a_d_skip",
           "a_gate_norm", "a_w_out", "a_norm_post", "kv_norm", "w_kv", "b_norm_pre", "b_w_q", "b_sinks", "b_w_o",
           "b_norm_post", "f_norm_pre", "f_w_up", "f_conv_w", "f_conv_b", "f_w_down", "f_norm_post")


def _seq_rows():
    return -(-(N_META + SEQ) // T) * T


def _cp(sem=None):
    return pltpu.CompilerParams(dimension_semantics=sem, vmem_limit_bytes=VMEM_LIMIT)


def _pick(n, target):
    t = min(n, target)
    t -= t % LANE
    while n % t:
        t -= LANE
    return t


def _sigmoid(x):
    return 1.0 / (1.0 + jnp.exp(-x))


def _softplus(x):
    return jnp.maximum(x, 0.0) + jnp.log(1.0 + jnp.exp(-jnp.abs(x)))


_NN = (((1,), (0,)), ((), ()))
_NT = (((1,), (1,)), ((), ()))
_TN = (((0,), (0,)), ((), ()))


def _dot(a, b, dims=_NN):
    return lax.dot_general(a, b, dims, preferred_element_type=F32)


def _dot_hi(a, b):
    return lax.dot_general(a, b, _NN, precision=lax.Precision.HIGHEST, preferred_element_type=F32)


_HBM = pl.BlockSpec(memory_space=pltpu.HBM)
_MESH = pl.DeviceIdType.MESH


class _Comm:
    def __init__(self, ins, out_shapes, scratch, first, last):
        self.ins, self.out_shapes, self.scratch, self.first, self.last = ins, out_shapes, scratch, first, last


def _call(body, name, out_shape, grid, in_specs, out_specs, sem, args, scratch=(), comm=None):
    if comm is None:
        return pl.pallas_call(body, name=name, out_shape=out_shape, grid=grid, in_specs=in_specs, out_specs=out_specs,
                              scratch_shapes=list(scratch), compiler_params=_cp(sem))(*args)
    single = not isinstance(out_shape, (list, tuple))
    outs = [out_shape] if single else list(out_shape)
    ospecs = [out_specs] if single else list(out_specs)
    n_in, n_out, n_scr, ci, co = len(in_specs), len(outs), len(scratch), len(comm.ins), len(comm.out_shapes)

    def carrier(*refs):
        p = 0
        parts = []
        for cnt in (n_in, ci, n_out, co, n_scr, len(comm.scratch)):
            parts.append(refs[p:p + cnt])
            p += cnt
        ins, cins, outs_r, couts, scr, cscr = parts
        ids = [pl.program_id(i) for i in range(len(grid))]
        first, last = ids[0] == 0, ids[0] == grid[0] - 1
        for i in range(1, len(grid)):
            first, last = first & (ids[i] == 0), last & (ids[i] == grid[i] - 1)

        @pl.when(first)
        def _():
            comm.first(cins, couts, cscr)

        body(*ins, *outs_r, *scr)

        @pl.when(last)
        def _():
            comm.last(cins, couts, cscr)

    res = pl.pallas_call(
        carrier, name=name, out_shape=outs + list(comm.out_shapes), grid=grid,
        in_specs=list(in_specs) + [_HBM] * ci, out_specs=ospecs + [_HBM] * co,
        scratch_shapes=list(scratch) + list(comm.scratch),
        compiler_params=_cp(("arbitrary",) * len(grid)))(*args, *comm.ins)
    mine = res[0] if single else list(res[:n_out])
    return mine, list(res[n_out:])


def _mm(a, b, mode, out_dtype, name, comm=None):
    if mode == "tn":
        m, kk = a.shape
        planes, width = (b.shape[0], b.shape[2]) if b.ndim == 3 else (1, b.shape[1])
        n = planes * width
        tko, tn = _pick(kk, 512), _pick(width, 512)
        per = width // tn

        def body(a_ref, b_ref, o_ref, at):
            @pl.when(pl.program_id(1) == 0)
            def _():
                at[...] = a_ref[...].T

            o_ref[...] = _dot(at[...], b_ref[...]).astype(o_ref.dtype)

        b_spec = (pl.BlockSpec((None, m, tn), lambda i, j: (j // per, 0, j % per)) if b.ndim == 3
                  else pl.BlockSpec((m, tn), lambda i, j: (0, j)))
        return _call(
            body, name, S((kk, n), out_dtype), (kk // tko, n // tn), [pl.BlockSpec((m, tko), lambda i, j: (0, i)), b_spec],
            pl.BlockSpec((tko, tn), lambda i, j: (i, j)), ("parallel", "arbitrary"), (a, b),
            scratch=[pltpu.VMEM((tko, m), a.dtype)], comm=comm)

    planes, width = (a.shape[0], a.shape[2]) if a.ndim == 3 else (1, a.shape[1])
    m, kk = a.shape[-2], planes * width
    n = b.shape[1] if mode == "nn" else b.shape[0]
    tn = _pick(n, 512)
    tk = kk if kk <= 2048 else _pick(width, 1536)
    nk = kk // tk
    per = width // tk
    dims = _NN if mode == "nn" else _NT

    def body(a_ref, b_ref, o_ref, *acc):
        part = _dot(a_ref[...], b_ref[...], dims)
        if nk == 1:
            o_ref[...] = part.astype(o_ref.dtype)
        else:
            k = pl.program_id(1)

            @pl.when(k == 0)
            def _():
                acc[0][...] = part

            @pl.when(k > 0)
            def _():
                acc[0][...] += part

            @pl.when(k == nk - 1)
            def _():
                o_ref[...] = acc[0][...].astype(o_ref.dtype)

    b_spec = (pl.BlockSpec((tk, tn), lambda j, k: (k, j)) if mode == "nn"
              else pl.BlockSpec((tn, tk), lambda j, k: (j, k)))
    a_spec = (pl.BlockSpec((None, m, tk), lambda j, k: (k // per, 0, k % per)) if a.ndim == 3
              else pl.BlockSpec((m, tk), lambda j, k: (0, k)))
    return _call(
        body, name, S((m, n), out_dtype), (n // tn, nk), [a_spec, b_spec],
        pl.BlockSpec((m, tn), lambda j, k: (0, j)), ("parallel", "arbitrary"), (a, b),
        scratch=[pltpu.VMEM((m, tn), F32)] if nk > 1 else [], comm=comm)


def _rms(x, w):
    return x * lax.rsqrt(jnp.mean(x * x, axis=-1, keepdims=True) + RMS_EPS) * w


def _row_tile(rows):
    return rows // 8


def _resid_norm(h, br, w_post, next_ws, name):
    rows, d = h.shape
    tr = _row_tile(rows)
    has_br = br is not None
    nw = len(next_ws)

    def body(*refs):
        h_ref = refs[0]
        pos = 1
        x = h_ref[...]
        if has_br:
            x = x + _rms(refs[1][...], refs[2][...])
            pos = 3
        w_refs = refs[pos:pos + nw]
        outs = refs[pos + nw:]
        if has_br:
            outs[0][...] = x
            outs = outs[1:]
        for w_ref, o_ref in zip(w_refs, outs):
            o_ref[...] = _rms(x, w_ref[...]).astype(o_ref.dtype)

    row = pl.BlockSpec((tr, d), lambda i: (i, 0))
    vec = pl.BlockSpec((1, d), lambda i: (0, 0))
    ins = [h] + ([br, w_post] if has_br else []) + list(next_ws)
    in_specs = [row] + ([row, vec] if has_br else []) + [vec] * nw
    out_shape = ([S((rows, d), F32)] if has_br else []) + [S((rows, d), BF16)] * nw
    res = pl.pallas_call(body, name=name, out_shape=out_shape, grid=(rows // tr,), in_specs=in_specs,
                         out_specs=[row] * len(out_shape), compiler_params=_cp(("parallel",)))(*ins)
    if has_br:
        return res[0], list(res[1:])
    return h, list(res)


def _norm_bwd(x, w, dy, add, out_dtype, name, comm=None):
    rows, d = x.shape
    tr = _row_tile(rows)
    has_add = add is not None

    def body(*refs):
        x_ref, w_ref, dy_ref = refs[:3]
        dx_ref, dw_ref = refs[-2:]
        xv = x_ref[...]
        r = lax.rsqrt(jnp.mean(xv * xv, axis=-1, keepdims=True) + RMS_EPS)
        dyv = dy_ref[...].astype(F32)
        wdy = dyv * w_ref[...]
        dx = r * wdy - xv * (r * r * r) * jnp.mean(xv * wdy, axis=-1, keepdims=True)
        if has_add:
            dx = dx + refs[3][...]
        dx_ref[...] = dx.astype(dx_ref.dtype)

        @pl.when(pl.program_id(0) == 0)
        def _():
            dw_ref[...] = jnp.zeros_like(dw_ref)

        dw_ref[...] += jnp.sum(dyv * xv * r, axis=0, keepdims=True)

    row = pl.BlockSpec((tr, d), lambda i: (i, 0))
    vec = pl.BlockSpec((1, d), lambda i: (0, 0))
    ins = [x, w, dy] + ([add] if has_add else [])
    return _call(body, name, [S((rows, d), out_dtype), S((1, d), F32)], (rows // tr,),
                 [row, vec, row] + ([row] if has_add else []), [row, vec], ("arbitrary",), ins, comm=comm)


def _final_loss(h, br, w_post, tgt, name):
    rows, d = h.shape
    tr = _row_tile(rows)

    def body(h_ref, br_ref, w_ref, t_ref, dh_ref, loss_ref):
        i = pl.program_id(0)
        y = h_ref[...] + _rms(br_ref[...], w_ref[...])
        r = i * tr + lax.broadcasted_iota(jnp.int32, (tr, 1), 0)
        real = (r >= N_META) & (r < N_META + SEQ)
        diff = jnp.where(real, y - t_ref[...], 0.0)
        dh_ref[...] = diff * (1.0 / d)

        @pl.when(i == 0)
        def _():
            loss_ref[...] = jnp.zeros_like(loss_ref)

        loss_ref[...] += jnp.sum(diff * diff) * (0.5 / d)

    row = pl.BlockSpec((tr, d), lambda i: (i, 0))
    return pl.pallas_call(body, name=name, out_shape=[S((rows, d), F32), S((1, LANE), F32)], grid=(rows // tr,),
                          in_specs=[row, row, pl.BlockSpec((1, d), lambda i: (0, 0)), row],
                          out_specs=[row, pl.BlockSpec((1, LANE), lambda i: (0, 0))],
                          compiler_params=_cp(("arbitrary",)))(h, br, w_post, tgt)


def _gatenorm_fwd(y, zx, w, name, comm=None):
    rows, d = y.shape
    tr = _row_tile(rows)

    def body(y_ref, z_ref, w_ref, o_ref):
        z = z_ref[...]
        o_ref[...] = _rms(y_ref[...] * z * _sigmoid(z), w_ref[...]).astype(o_ref.dtype)

    row = pl.BlockSpec((tr, d), lambda i: (i, 0))
    return _call(body, name, S((rows, d), BF16), (rows // tr,), [row, row, pl.BlockSpec((1, d), lambda i: (0, 0))],
                 row, ("parallel",), (y, zx, w), comm=comm)


def _gatenorm_bwd(y, zx, w, dyn, name, comm=None):
    rows, d = y.shape
    tr = _row_tile(rows)

    def body(y_ref, z_ref, w_ref, dyn_ref, dy_ref, dz_ref, dw_ref):
        yv, z = y_ref[...], z_ref[...]
        sg = _sigmoid(z)
        sz = z * sg
        g = yv * sz
        r = lax.rsqrt(jnp.mean(g * g, axis=-1, keepdims=True) + RMS_EPS)
        dyn_v = dyn_ref[...]
        wdy = dyn_v * w_ref[...]
        dg = r * wdy - g * (r * r * r) * jnp.mean(g * wdy, axis=-1, keepdims=True)
        dy_ref[...] = dg * sz
        dz_ref[...] = (dg * yv * sg * (1.0 + z * (1.0 - sg))).astype(dz_ref.dtype)

        @pl.when(pl.program_id(0) == 0)
        def _():
            dw_ref[...] = jnp.zeros_like(dw_ref)

        dw_ref[...] += jnp.sum(dyn_v * g * r, axis=0, keepdims=True)

    row = pl.BlockSpec((tr, d), lambda i: (i, 0))
    vec = pl.BlockSpec((1, d), lambda i: (0, 0))
    return _call(body, name, [S((rows, d), F32), S((rows, d), BF16), S((1, d), F32)], (rows // tr,),
                 [row, row, vec, row], [row, row, vec], ("arbitrary",), (y, zx, w, dyn), comm=comm)


def _shift_down(x, s, rows_iota):
    if s == 0:
        return x
    return jnp.where(rows_iota >= s, pltpu.roll(x, s, 0), 0.0)


def _shift_up(x, s, rows_iota):
    if s == 0:
        return x
    rows = x.shape[0]
    return jnp.where(rows_iota < rows - s, pltpu.roll(x, rows - s, 0), 0.0)


def _r16(v):
    return v.astype(BF16).astype(F32)


def _conv(x, w_ref, b_ref, taps, rows_iota):
    x = _r16(x)
    acc = jnp.zeros_like(x)
    for k in range(taps):
        acc = acc + _r16(w_ref[k:k + 1, :]) * _shift_down(x, taps - 1 - k, rows_iota)
    return acc + b_ref[...]


def _conv_bwd(x, du, w_ref, dw_ref, db_ref, taps, rows_iota):
    db_ref[...] = jnp.sum(du, axis=0, keepdims=True)
    x, du = _r16(x), _r16(du)
    dx = jnp.zeros_like(x)
    for k in range(taps):
        s = taps - 1 - k
        dx = dx + _r16(w_ref[k:k + 1, :]) * _shift_up(du, s, rows_iota)
        dw_ref[k:k + 1, :] = jnp.sum(du * _shift_down(x, s, rows_iota), axis=0, keepdims=True)
    return dx


def _conv_silu_fwd(zx, w, b, name, comm=None):
    rows = zx.shape[0]
    cb = 512
    off = D_INNER // cb

    def body(x_ref, w_ref, b_ref, o_ref):
        it = lax.broadcasted_iota(jnp.int32, (rows, 1), 0)
        u = _conv(x_ref[...], w_ref, b_ref, SSM_CONV, it)
        o_ref[...] = u * _sigmoid(u)

    return _call(
        body, name, S((rows, D_XBC), F32), (D_XBC // cb,),
        [pl.BlockSpec((rows, cb), lambda j: (0, off + j)), pl.BlockSpec((SSM_CONV, cb), lambda j: (0, j)),
         pl.BlockSpec((1, cb), lambda j: (0, j))],
        pl.BlockSpec((rows, cb), lambda j: (0, j)), ("parallel",), (zx, w, b), comm=comm)


def _conv_silu_bwd(zx, dxbc, w, b, name):
    rows = zx.shape[0]
    cb = 512
    off = D_INNER // cb

    def body(x_ref, d_ref, w_ref, b_ref, dx_ref, dw_ref, db_ref):
        it = lax.broadcasted_iota(jnp.int32, (rows, 1), 0)
        x = x_ref[...]
        u = _conv(x, w_ref, b_ref, SSM_CONV, it)
        sg = _sigmoid(u)
        du = d_ref[...] * sg * (1.0 + u * (1.0 - sg))
        dx_ref[...] = _conv_bwd(x, du, w_ref, dw_ref, db_ref, SSM_CONV, it).astype(dx_ref.dtype)

    col = pl.BlockSpec((rows, cb), lambda j: (0, j))
    wsp = pl.BlockSpec((SSM_CONV, cb), lambda j: (0, j))
    bsp = pl.BlockSpec((1, cb), lambda j: (0, j))
    return pl.pallas_call(
        body, name=name, out_shape=[S((rows, D_XBC), BF16), S((SSM_CONV, D_XBC), F32), S((1, D_XBC), F32)],
        grid=(D_XBC // cb,), in_specs=[pl.BlockSpec((rows, cb), lambda j: (0, off + j)), col, wsp, bsp],
        out_specs=[col, wsp, bsp], compiler_params=_cp(("parallel",)))(zx, dxbc, w, b)


def _ffn_act_fwd(u, w, b, name, comm=None):
    rows = u.shape[0]
    cb = 256
    nb = D_FF // cb

    def body(g_ref, v_ref, wg_ref, wv_ref, bg_ref, bv_ref, o_ref):
        it = lax.broadcasted_iota(jnp.int32, (rows, 1), 0)
        g = _conv(g_ref[...], wg_ref, bg_ref, FFN_CONV, it)
        v = _conv(v_ref[...], wv_ref, bv_ref, FFN_CONV, it)
        o_ref[...] = (g * _sigmoid(g) * v).astype(o_ref.dtype)

    def sp(r, shift):
        return pl.BlockSpec((r, cb), lambda j: (0, shift + j))

    return _call(
        body, name, S((rows, D_FF), BF16), (nb,),
        [sp(rows, 0), sp(rows, nb), sp(FFN_CONV, 0), sp(FFN_CONV, nb), sp(1, 0), sp(1, nb)],
        sp(rows, 0), ("parallel",), (u, u, w, w, b, b), comm=comm)


def _ffn_act_bwd(u, dact, w, b, name, comm=None):
    rows = u.shape[0]
    cb = 256
    nb = D_FF // cb

    def body(g_ref, v_ref, d_ref, wg_ref, wv_ref, bg_ref, bv_ref, du_ref, dw_ref, db_ref):
        it = lax.broadcasted_iota(jnp.int32, (rows, 1), 0)
        xg, xv = g_ref[...], v_ref[...]
        g = _conv(xg, wg_ref, bg_ref, FFN_CONV, it)
        v = _conv(xv, wv_ref, bv_ref, FFN_CONV, it)
        sg = _sigmoid(g)
        d = d_ref[...]
        dgate = d * v * sg * (1.0 + g * (1.0 - sg))
        dval = d * g * sg
        du_ref[0] = _conv_bwd(xg, dgate, wg_ref, dw_ref.at[0], db_ref.at[0], FFN_CONV, it).astype(du_ref.dtype)
        du_ref[1] = _conv_bwd(xv, dval, wv_ref, dw_ref.at[1], db_ref.at[1], FFN_CONV, it).astype(du_ref.dtype)

    def sp(r, shift):
        return pl.BlockSpec((r, cb), lambda j: (0, shift + j))

    def both(r):
        return pl.BlockSpec((2, r, cb), lambda j: (0, 0, j))

    return _call(
        body, name, [S((2, rows, D_FF), BF16), S((2, FFN_CONV, D_FF), F32), S((2, 1, D_FF), F32)], (nb,),
        [sp(rows, 0), sp(rows, nb), sp(rows, 0), sp(FFN_CONV, 0), sp(FFN_CONV, nb), sp(1, 0), sp(1, nb)],
        [both(rows), both(FFN_CONV), both(1)], ("parallel",), (u, u, dact, w, w, b, b), comm=comm)


def _ssd_consts(dtp_ref, bias_ref, alog_ref, hg):
    lane = lax.broadcasted_iota(jnp.int32, (1, LANE), 1)
    pre = dtp_ref[...] + bias_ref[...]
    dt = _softplus(pre)
    a_row = jnp.where(lane < hg, -jnp.exp(alog_ref[...]), 0.0)
    ri = lax.broadcasted_iota(jnp.int32, (T, T), 0)
    ci = lax.broadcasted_iota(jnp.int32, (T, T), 1)
    cs = _dot_hi((ri >= ci).astype(F32), dt * a_row)
    return pre, dt, a_row, cs, ri, ci, lane


def _ssd_fwd(xbc, zx, bias, alog, dsk, name, comm=None):
    rows = xbc.shape[0]
    nc = rows // T
    hg = SSM_HEADS // SSM_GROUPS
    gw = hg * HEAD_P
    xoff, boff, coff = 0, D_INNER // D_STATE, (D_INNER + D_BC) // D_STATE
    dtoff = (D_INNER + D_XBC) // LANE

    def body(x_ref, b_ref, c_ref, dtp_ref, bias_ref, alog_ref, dsk_ref, y_ref, hst_ref, hs):
        c = pl.program_id(1)

        @pl.when(c == 0)
        def _():
            hs[...] = jnp.zeros_like(hs)

        _, dt, _, cs, ri, ci, _ = _ssd_consts(dtp_ref, bias_ref, alog_ref, hg)
        cst, dtt = cs.T, dt.T
        xt = x_ref[...].T
        bb, cbf = b_ref[...].astype(BF16), c_ref[...].astype(BF16)
        gt = _dot(bb, cbf, _NT)
        causal_t = ci >= ri
        dskv = dsk_ref[...]
        hall = hs[...]
        hst_ref[0, 0] = hall
        yts, new_h = [], []
        for k in range(hg):
            sl = slice(k * HEAD_P, (k + 1) * HEAD_P)
            csc, csr = cs[:, k:k + 1], cst[k:k + 1, :]
            lt = jnp.exp(jnp.where(causal_t, csr - csc, NEG))
            xk = xt[sl, :]
            xdt = xk * dtt[k:k + 1, :]
            hk = hall[sl, :]
            yd = _dot(xdt.astype(BF16), (gt * lt).astype(BF16))
            yo = jnp.exp(csr) * _dot(hk.astype(BF16), cbf, _NT)
            yts.append(yd + yo + dskv[:, k:k + 1] * xk)
            cl = cs[T - 1:T, k:k + 1]
            st = _dot((xdt * jnp.exp(cl - csr)).astype(BF16), bb)
            new_h.append(jnp.exp(cl) * hk + st)
        y_ref[...] = jnp.concatenate(yts, axis=0).T
        hs[...] = jnp.concatenate(new_h, axis=0)

    vec = pl.BlockSpec((1, LANE), lambda g, c: (0, g))
    return _call(
        body, name, [S((rows, D_INNER), F32), S((nc, SSM_GROUPS, gw, D_STATE), F32)], (SSM_GROUPS, nc),
        [pl.BlockSpec((T, gw), lambda g, c: (c, xoff + g)),
         pl.BlockSpec((T, D_STATE), lambda g, c: (c, boff + g)),
         pl.BlockSpec((T, D_STATE), lambda g, c: (c, coff + g)),
         pl.BlockSpec((T, LANE), lambda g, c: (c, dtoff + g)), vec, vec, vec],
        [pl.BlockSpec((T, gw), lambda g, c: (c, g)), pl.BlockSpec((1, 1, gw, D_STATE), lambda g, c: (c, g, 0, 0))],
        ("parallel", "arbitrary"), (xbc, xbc, xbc, zx, bias, alog, dsk),
        scratch=[pltpu.VMEM((gw, D_STATE), F32)], comm=comm)


def _ssd_bwd(xbc, zx, bias, alog, dsk, dy, hst, name, comm=None):
    rows = xbc.shape[0]
    nc = rows // T
    hg = SSM_HEADS // SSM_GROUPS
    gw = hg * HEAD_P
    boff, coff = D_INNER // D_STATE, (D_INNER + D_BC) // D_STATE
    dtoff = (D_INNER + D_XBC) // LANE

    def body(x_ref, b_ref, c_ref, dtp_ref, bias_ref, alog_ref, dsk_ref, dy_ref, hst_ref,
             dx_ref, db_ref, dc_ref, ddtp_ref, dalog_ref, ddsk_ref, dbias_ref, dhs):
        step = pl.program_id(1)

        @pl.when(step == 0)
        def _():
            dhs[...] = jnp.zeros_like(dhs)
            dalog_ref[...] = jnp.zeros_like(dalog_ref)
            ddsk_ref[...] = jnp.zeros_like(ddsk_ref)
            dbias_ref[...] = jnp.zeros_like(dbias_ref)

        pre, dt, a_row, cs, ri, ci, lane = _ssd_consts(dtp_ref, bias_ref, alog_ref, hg)
        cst, dtt = cs.T, dt.T
        xt, dyt = x_ref[...].T, dy_ref[...].T
        bb, cbf = b_ref[...].astype(BF16), c_ref[...].astype(BF16)
        gt = _dot(bb, cbf, _NT)
        causal_t = ci >= ri
        dskv = dsk_ref[...]
        hall, dhall = hst_ref[0, 0], dhs[...]
        head_row = lax.broadcasted_iota(jnp.int32, (T, 1), 0)
        last_l = lax.broadcasted_iota(jnp.int32, (1, T), 1) == T - 1
        dgt = jnp.zeros((T, T), F32)
        dc_acc = jnp.zeros((T, D_STATE), F32)
        db_acc = jnp.zeros((T, D_STATE), F32)
        ddt_rows = jnp.zeros((T, T), F32)
        dcs_rows = jnp.zeros((T, T), F32)
        qrow_cols = jnp.zeros((T, LANE), F32)
        ddsk_acc = jnp.zeros((1, LANE), F32)
        dxts, new_dh = [], []
        for k in range(hg):
            sl = slice(k * HEAD_P, (k + 1) * HEAD_P)
            csc, csr = cs[:, k:k + 1], cst[k:k + 1, :]
            lt = jnp.exp(jnp.where(causal_t, csr - csc, NEG))
            xk, dyk = xt[sl, :], dyt[sl, :]
            dtr, dk = dtt[k:k + 1, :], dskv[:, k:k + 1]
            xdt = xk * dtr
            mpt = gt * lt
            dyb = dyk.astype(BF16)
            dxdt = _dot(dyb, mpt.astype(BF16), _NT)
            dmt = _dot(xdt.astype(BF16), dyb, _TN)
            dgt = dgt + dmt * lt
            q = dmt * mpt
            q_rows = jnp.sum(q, axis=1, keepdims=True)
            q_cols = jnp.sum(q, axis=0, keepdims=True)
            hk, dhn = hall[sl, :], dhall[sl, :]
            e = jnp.exp(csr)
            cl = cs[T - 1:T, k:k + 1]
            wdec = jnp.exp(cl - csr)
            w = wdec * dtr
            rt = _dot(dhn.astype(BF16), bb, _NT)
            dxts.append(dtr * dxdt + dk * dyk + rt * w)
            xz = jnp.sum(xk * dxdt, axis=0, keepdims=True)
            dw = jnp.sum(rt * xk, axis=0, keepdims=True)
            dcl = jnp.exp(cl) * jnp.sum(dhn * hk) + jnp.sum(dw * w)
            yo = e * _dot(hk.astype(BF16), cbf, _NT)
            dcs_r = jnp.sum(dyk * yo, axis=0, keepdims=True) + q_cols - dw * w + jnp.where(last_l, dcl, 0.0)
            dye = (dyk * e).astype(BF16)
            dc_acc = dc_acc + _dot(dye, hk.astype(BF16), _TN)
            db_acc = db_acc + _dot((xk * w).astype(BF16), dhn.astype(BF16), _TN)
            new_dh.append(jnp.exp(cl) * dhn + _dot(dye, cbf))
            onehot = (lane == k).astype(F32)
            ddt_rows = ddt_rows + jnp.where(head_row == k, xz + dw * wdec, 0.0)
            dcs_rows = dcs_rows + jnp.where(head_row == k, dcs_r, 0.0)
            qrow_cols = qrow_cols + q_rows * onehot
            ddsk_acc = ddsk_acc + jnp.sum(dyk * xk) * onehot
        dx_ref[...] = jnp.concatenate(dxts, axis=0).T
        dhs[...] = jnp.concatenate(new_dh, axis=0)
        dc_ref[...] = _dot(dgt.T.astype(BF16), bb) + dc_acc
        db_ref[...] = _dot(dgt.astype(BF16), cbf) + db_acc
        da = _dot_hi((ci >= ri).astype(F32), dcs_rows.T - qrow_cols)
        ddtp = (ddt_rows.T + da * a_row) * _sigmoid(pre)
        ddtp = jnp.where(lane < hg, ddtp, 0.0)
        ddtp_ref[...] = ddtp
        dbias_ref[...] += jnp.sum(ddtp, axis=0, keepdims=True)
        dalog_ref[...] += jnp.sum(da * dt, axis=0, keepdims=True) * a_row
        ddsk_ref[...] += ddsk_acc

    def rc(c):
        return nc - 1 - c

    vec = pl.BlockSpec((1, LANE), lambda g, c: (0, g))
    xsp = pl.BlockSpec((T, gw), lambda g, c: (rc(c), g))
    return _call(
        body, name,
        [S((rows, D_INNER), F32), S((rows, D_BC), F32), S((rows, D_BC), F32),
         S((rows, SSM_GROUPS * LANE), F32), S((1, SSM_GROUPS * LANE), F32),
         S((1, SSM_GROUPS * LANE), F32), S((1, SSM_GROUPS * LANE), F32)],
        (SSM_GROUPS, nc),
        [xsp,
         pl.BlockSpec((T, D_STATE), lambda g, c: (rc(c), boff + g)),
         pl.BlockSpec((T, D_STATE), lambda g, c: (rc(c), coff + g)),
         pl.BlockSpec((T, LANE), lambda g, c: (rc(c), dtoff + g)), vec, vec, vec,
         xsp, pl.BlockSpec((1, 1, gw, D_STATE), lambda g, c: (rc(c), g, 0, 0))],
        [xsp,
         pl.BlockSpec((T, D_STATE), lambda g, c: (rc(c), g)),
         pl.BlockSpec((T, D_STATE), lambda g, c: (rc(c), g)),
         pl.BlockSpec((T, LANE), lambda g, c: (rc(c), g)), vec, vec, vec],
        ("parallel", "arbitrary"), (xbc, xbc, xbc, zx, bias, alog, dsk, dy, hst),
        scratch=[pltpu.VMEM((gw, D_STATE), F32)], comm=comm)


def _attn_tiles(kv_ref, j):
    prev = jnp.maximum(j - 1, 0)
    meta = kv_ref[0:T, :]
    prv = kv_ref[pl.ds(pl.multiple_of(prev * T, T), T), :]
    cur = kv_ref[pl.ds(pl.multiple_of(j * T, T), T), :]
    return jnp.concatenate([meta, prv, cur], axis=0)


def _attn_mask(j):
    r = j * T + lax.broadcasted_iota(jnp.int32, (3 * T, T), 1)
    row = lax.broadcasted_iota(jnp.int32, (3 * T, T), 0)
    t0, t1 = row < T, row < 2 * T
    s = jnp.where(t0, row, (j - 2) * T + row)
    ok = (s <= r) & ((s < N_META) | (s > r - WINDOW))
    use = (t0 & (j >= 2) & (row < N_META)) | (jnp.logical_not(t0) & t1 & (j >= 1)) | jnp.logical_not(t1)
    return ok & use


def _attn_fwd(q, kv, sinks, name, comm=None):
    rows = q.shape[0]
    scale = 1.0 / math.sqrt(ATTN_DH)
    qpk = N_Q_HEADS // N_KV_HEADS

    def body(q_ref, kv_ref, s_ref, o_ref, lse_ref):
        j = pl.program_id(0)
        kv3 = _attn_tiles(kv_ref, j).astype(BF16)
        mask = _attn_mask(j)
        qv = (q_ref[...] * scale).astype(BF16)
        sk = s_ref[...]
        lses = []
        for kh in range(N_KV_HEADS):
            k3 = kv3[:, kh * ATTN_DH:(kh + 1) * ATTN_DH]
            v3 = kv3[:, D_KV + kh * ATTN_DH:D_KV + (kh + 1) * ATTN_DH]
            for g in range(qpk):
                h = kh * qpk + g
                sink = sk[:, h:h + 1]
                sc = jnp.where(mask, _dot(k3, qv[:, h * ATTN_DH:(h + 1) * ATTN_DH], _NT), NEG)
                m = jnp.maximum(jnp.max(sc, axis=0, keepdims=True), sink)
                p = jnp.exp(sc - m)
                den = jnp.sum(p, axis=0, keepdims=True) + jnp.exp(sink - m)
                p = p * (1.0 / den)
                lses.append(m + jnp.log(den))
                o_ref[:, h * ATTN_DH:(h + 1) * ATTN_DH] = _dot(p.astype(BF16), v3, _TN).astype(o_ref.dtype)
        lse_ref[...] = jnp.concatenate(lses, axis=0)

    return _call(
        body, name, [S((rows, D_MODEL), BF16), S((N_Q_HEADS, rows), F32)], (rows // T,),
        [pl.BlockSpec((T, D_MODEL), lambda j: (j, 0)), pl.BlockSpec((rows, 2 * D_KV), lambda j: (0, 0)),
         pl.BlockSpec((1, N_Q_HEADS), lambda j: (0, 0))],
        [pl.BlockSpec((T, D_MODEL), lambda j: (j, 0)), pl.BlockSpec((N_Q_HEADS, T), lambda j: (0, j))],
        ("parallel",), (q, kv, sinks), comm=comm)


def _attn_bwd(q, kv, sinks, do, lse, name, comm=None):
    rows = q.shape[0]
    scale = 1.0 / math.sqrt(ATTN_DH)
    qpk = N_Q_HEADS // N_KV_HEADS

    def body(q_ref, kv_ref, s_ref, do_ref, lse_ref, dq_ref, dkv_ref, ds_ref):
        j = pl.program_id(0)

        @pl.when(j == 0)
        def _():
            dkv_ref[...] = jnp.zeros_like(dkv_ref)
            ds_ref[...] = jnp.zeros_like(ds_ref)

        kv3 = _attn_tiles(kv_ref, j).astype(BF16)
        mask = _attn_mask(j)
        qv = (q_ref[...] * scale).astype(BF16)
        dov = do_ref[...].astype(BF16)
        sk = s_ref[...]
        lsev = lse_ref[...]
        lane = lax.broadcasted_iota(jnp.int32, (1, LANE), 1)
        ds_acc = jnp.zeros((1, LANE), F32)
        prev = jnp.maximum(j - 1, 0)
        dqts = []
        for kh in range(N_KV_HEADS):
            ksl = slice(kh * ATTN_DH, (kh + 1) * ATTN_DH)
            vsl = slice(D_KV + kh * ATTN_DH, D_KV + (kh + 1) * ATTN_DH)
            k3, v3 = kv3[:, ksl], kv3[:, vsl]
            k3t = k3.T
            dk3 = jnp.zeros((3 * T, ATTN_DH), F32)
            dv3 = jnp.zeros((3 * T, ATTN_DH), F32)
            for g in range(qpk):
                h = kh * qpk + g
                hs = slice(h * ATTN_DH, (h + 1) * ATTN_DH)
                qh, doh = qv[:, hs], dov[:, hs]
                lh = lsev[h:h + 1, :]
                p = jnp.exp(jnp.where(mask, _dot(k3, qh, _NT), NEG) - lh)
                ps = jnp.exp(sk[:, h:h + 1] - lh)
                dp = _dot(v3, doh, _NT)
                delta = jnp.sum(p * dp, axis=0, keepdims=True)
                dsc = (p * (dp - delta)).astype(BF16)
                dqts.append(_dot(k3t, dsc) * scale)
                dk3 = dk3 + _dot(dsc, qh)
                dv3 = dv3 + _dot(p.astype(BF16), doh)
                ds_acc = ds_acc - jnp.sum(ps * delta) * (lane == h).astype(F32)
            for t, start in enumerate((0, pl.multiple_of(prev * T, T), pl.multiple_of(j * T, T))):
                rsl = pl.ds(start, T)
                dkv_ref[rsl, ksl] += dk3[t * T:(t + 1) * T, :]
                dkv_ref[rsl, vsl] += dv3[t * T:(t + 1) * T, :]
        ds_ref[...] += ds_acc
        dq_ref[...] = jnp.concatenate(dqts, axis=0).T.astype(dq_ref.dtype)

    blk = pl.BlockSpec((T, D_MODEL), lambda j: (j, 0))
    full = pl.BlockSpec((rows, 2 * D_KV), lambda j: (0, 0))
    return _call(
        body, name, [S((rows, D_MODEL), BF16), S((rows, 2 * D_KV), F32), S((1, LANE), F32)], (rows // T,),
        [blk, full, pl.BlockSpec((1, N_Q_HEADS), lambda j: (0, 0)), blk, pl.BlockSpec((N_Q_HEADS, T), lambda j: (0, j))],
        [blk, full, pl.BlockSpec((1, LANE), lambda j: (0, 0))], ("arbitrary",), (q, kv, sinks, do, lse), comm=comm)


BLOCK_BYTES = 1 << 20


def _div_tile(rows, cols):
    cap = max(16, BLOCK_BYTES // (4 * cols))
    best = None
    for t in range(16, min(rows, cap) + 1, 16):
        if rows % t == 0:
            best = t
    return best if best is not None else rows


def _adamw(parts, w, m, v, name, comm=None):
    layers, rows, cols = w.shape
    n = parts[0].shape[0]
    tr = _div_tile(rows, cols)
    c1 = 1.0 / (1.0 - B1 ** STEP)
    c2 = 1.0 / (1.0 - B2 ** STEP)

    def body(*refs):
        p_refs = refs[:layers]
        w_ref, m_ref, v_ref, g_ref, d_ref, nm_ref, nv_ref = refs[layers:]
        layer = pl.program_id(0)
        for l in range(layers):
            @pl.when(layer == l)
            def _(p_ref=p_refs[l]):
                g = p_ref[0].astype(F32)
                for i in range(1, n):
                    g = g + p_ref[i].astype(F32)
                nm = B1 * m_ref[...] + (1.0 - B1) * g
                nv = B2 * v_ref[...] + (1.0 - B2) * (g * g)
                g_ref[...] = g
                nm_ref[...] = nm
                nv_ref[...] = nv
                d_ref[...] = -LR * ((nm * c1) / (jnp.sqrt(nv * c2) + EPS) + WD * w_ref[...])

    def part_spec(l):
        return pl.BlockSpec((n, tr, cols), lambda k, i: (0, jnp.where(k == l, i, 0), 0))

    row = pl.BlockSpec((None, tr, cols), lambda k, i: (k, i, 0))
    return _call(body, name, [S((layers, rows, cols), F32)] * 4, (layers, rows // tr),
                 [part_spec(l) for l in range(layers)] + [row, row, row], [row] * 4, ("parallel", "parallel"),
                 (*parts, w, m, v), comm=comm)


def _col_segments(ws, runs):
    segs = []
    for glo, mlo, n in runs:
        while n > 0:
            d, off = divmod(glo, ws)
            take = min(n, ws - off)
            segs.append((d, off, mlo, take))
            glo, mlo, n = glo + take, mlo + take, n - take
    return segs


def _assemble_cols(g, width, segs, name):
    _, rows, ws = g.shape
    rb = _div_tile(rows, width // 2)

    def body(g_ref, o_ref):
        o_ref[...] = jnp.zeros_like(o_ref)
        for d, off, mlo, n in segs:
            o_ref[:, mlo:mlo + n] = g_ref[d, :, off:off + n]

    return pl.pallas_call(
        body, name=name, out_shape=S((rows, width), g.dtype), grid=(rows // rb,),
        in_specs=[pl.BlockSpec((N_DEV, rb, ws), lambda i: (0, i, 0))],
        out_specs=pl.BlockSpec((rb, width), lambda i: (i, 0)), compiler_params=_cp(("parallel",)))(g)


def _scatter_cols(dw, ws, segs, name):
    rows, width = dw.shape
    rb = _div_tile(rows, width)

    def body(w_ref, o_ref):
        for d, off, mlo, n in segs:
            o_ref[d, :, off:off + n] = w_ref[:, mlo:mlo + n].astype(o_ref.dtype)

    return pl.pallas_call(
        body, name=name, out_shape=S((N_DEV, rows, ws), BF16), grid=(rows // rb,),
        in_specs=[pl.BlockSpec((rb, width), lambda i: (i, 0))],
        out_specs=pl.BlockSpec((N_DEV, rb, ws), lambda i: (0, i, 0)), compiler_params=_cp(("parallel",)))(dw)


def _gather_comm(xs):
    n = len(xs)

    def setup(x_refs, out_refs, sems):
        send_sems, recv_sems, local_sems = sems
        mx, my, mc = lax.axis_index("x"), lax.axis_index("y"), lax.axis_index("c")
        me, sibling = (mx, my, mc), (mx, my, 1 - mc)
        chips = [(1 - mx, my), (mx, 1 - my), (1 - mx, 1 - my)]

        def blk(a, px, py, pc):
            return out_refs[a].at[4 * px + 2 * py + pc]

        def copy(a, k, block, to, src=None):
            return pltpu.make_async_remote_copy(
                src_ref=blk(a, *block) if src is None else src, dst_ref=blk(a, *block),
                send_sem=send_sems.at[a, k], recv_sem=recv_sems.at[a, k], device_id=to, device_id_type=_MESH)

        mine = [pltpu.make_async_copy(x_refs[a], blk(a, *me), local_sems.at[a]) for a in range(n)]
        own = []
        for a in range(n):
            own.append(copy(a, 0, me, sibling, src=x_refs[a]))
            own += [copy(a, 1 + i, me, (*chip, mc), src=x_refs[a]) for i, chip in enumerate(chips)]
        return me, sibling, chips, mc, copy, mine, own

    def first(x_refs, out_refs, sems):
        _, _, _, _, _, mine, own = setup(x_refs, out_refs, sems)
        for cp in mine + own:
            cp.start()

    def last(x_refs, out_refs, sems):
        me, sibling, chips, mc, copy, mine, own = setup(x_refs, out_refs, sems)
        passed = []
        for a in range(n):
            for i, chip in enumerate(chips):
                copy(a, 1 + i, (*chip, mc), me).wait_recv()
                passed.append(copy(a, 4 + i, (*chip, mc), sibling))
                passed[-1].start()
        for a in range(n):
            copy(a, 0, sibling, me).wait_recv()
            for i, chip in enumerate(chips):
                copy(a, 4 + i, (*chip, 1 - mc), me).wait_recv()
        for cp in own + passed:
            cp.wait_send()
        for cp in mine:
            cp.wait()

    return _Comm(list(xs), [S((N_DEV,) + x.shape, x.dtype) for x in xs],
                 [pltpu.SemaphoreType.DMA((n, 7)), pltpu.SemaphoreType.DMA((n, 7)), pltpu.SemaphoreType.DMA((n,))],
                 first, last)


def _swap_comm(gs):
    n = len(gs)

    def copies(g_refs, out_refs, sems):
        send_sems, recv_sems = sems
        mx, my, mc = lax.axis_index("x"), lax.axis_index("y"), lax.axis_index("c")
        return [pltpu.make_async_remote_copy(
            src_ref=g_refs[a].at[2 * k + 1 - mc], dst_ref=out_refs[a].at[k], send_sem=send_sems.at[a, k],
            recv_sem=recv_sems.at[a, k], device_id=(mx, my, 1 - mc), device_id_type=_MESH)
            for a in range(n) for k in range(4)]

    def first(g_refs, out_refs, sems):
        for cp in copies(g_refs, out_refs, sems):
            cp.start()

    def last(g_refs, out_refs, sems):
        for cp in copies(g_refs, out_refs, sems):
            cp.wait()

    return _Comm(list(gs), [S((4,) + g.shape[1:], g.dtype) for g in gs],
                 [pltpu.SemaphoreType.DMA((n, 4)), pltpu.SemaphoreType.DMA((n, 4))], first, last)


def _chips_comm(parts):
    n = len(parts)

    def copies(p_refs, out_refs, sems):
        send_sems, recv_sems, local_sems = sems
        mx, my, mc = lax.axis_index("x"), lax.axis_index("y"), lax.axis_index("c")
        mychip = 2 * mx + my
        chips = [(1 - mx, my), (mx, 1 - my), (1 - mx, 1 - my)]
        mine = [pltpu.make_async_copy(p_refs[a].at[mychip], out_refs[a].at[mychip], local_sems.at[a])
                for a in range(n)]
        return mine + [pltpu.make_async_remote_copy(
            src_ref=p_refs[a].at[2 * cx + cy], dst_ref=out_refs[a].at[mychip], send_sem=send_sems.at[a, i],
            recv_sem=recv_sems.at[a, i], device_id=(cx, cy, mc), device_id_type=_MESH)
            for a in range(n) for i, (cx, cy) in enumerate(chips)]

    def first(p_refs, out_refs, sems):
        for cp in copies(p_refs, out_refs, sems):
            cp.start()

    def last(p_refs, out_refs, sems):
        for cp in copies(p_refs, out_refs, sems):
            cp.wait()

    return _Comm(list(parts), [S(p.shape, p.dtype) for p in parts],
                 [pltpu.SemaphoreType.DMA((n, 3)), pltpu.SemaphoreType.DMA((n, 3)), pltpu.SemaphoreType.DMA((n,))],
                 first, last)


def _join_comms(comms):
    def split(refs, counts):
        out, p = [], 0
        for cnt in counts:
            out.append(refs[p:p + cnt])
            p += cnt
        return out

    ni = [len(c.ins) for c in comms]
    no = [len(c.out_shapes) for c in comms]
    ns = [len(c.scratch) for c in comms]

    def first(in_refs, out_refs, sems):
        for c, i, o, s in zip(comms, split(in_refs, ni), split(out_refs, no), split(sems, ns)):
            c.first(i, o, s)

    def last(in_refs, out_refs, sems):
        for c, i, o, s in zip(comms, split(in_refs, ni), split(out_refs, no), split(sems, ns)):
            c.last(i, o, s)

    return _Comm([x for c in comms for x in c.ins], [x for c in comms for x in c.out_shapes],
                 [x for c in comms for x in c.scratch], first, last)


def _add_pairs(mine, theirs, core, name):
    _, rows, cols = mine.shape
    tr = _div_tile(rows, cols)

    def body(core_ref, a_ref, b_ref, o_ref):
        o_ref[...] = (a_ref[...].astype(F32) + b_ref[...].astype(F32)).astype(o_ref.dtype)

    return pl.pallas_call(
        body, name=name, out_shape=S((4, rows, cols), BF16),
        grid_spec=pltpu.PrefetchScalarGridSpec(
            num_scalar_prefetch=1, grid=(4, rows // tr),
            in_specs=[pl.BlockSpec((None, tr, cols), lambda k, i, c: (2 * k + c[0], i, 0)),
                      pl.BlockSpec((None, tr, cols), lambda k, i, c: (k, i, 0))],
            out_specs=pl.BlockSpec((None, tr, cols), lambda k, i, c: (k, i, 0))),
        compiler_params=_cp(("parallel", "parallel")))(core, mine, theirs)


def _run_comm(comm, name):
    ci, co = len(comm.ins), len(comm.out_shapes)

    def body(*refs):
        comm.first(refs[:ci], refs[ci:ci + co], refs[ci + co:])
        comm.last(refs[:ci], refs[ci:ci + co], refs[ci + co:])

    return pl.pallas_call(body, name=name, out_shape=list(comm.out_shapes), in_specs=[_HBM] * ci,
                          out_specs=[_HBM] * co, scratch_shapes=list(comm.scratch))(*comm.ins)


def _flat_rows(n_elems, mult):
    rows = -(-n_elems // LANE)
    return -(-rows // mult) * mult


def _pack(arrs, lead, mult, dtype):
    lead_shape = arrs[0].shape[:lead]
    flat = jnp.concatenate([a.astype(dtype).reshape(lead_shape + (-1,)) for a in arrs], axis=-1)
    n = flat.shape[-1]
    rows = _flat_rows(n, mult)
    flat = jnp.pad(flat, [(0, 0)] * lead + [(0, rows * LANE - n)])
    return flat.reshape(lead_shape + (rows, LANE))


def _unpack(flat, lead, shapes):
    lead_shape = flat.shape[:lead]
    flat = flat.reshape(lead_shape + (-1,))
    out, off = [], 0
    for shp in shapes:
        n = math.prod(shp)
        out.append(flat[..., off:off + n].reshape(lead_shape + tuple(shp)))
        off += n
    return out


def _split8(full, ax, n):
    shp = full.shape
    return jnp.moveaxis(full.reshape(shp[:ax] + (N_DEV, n) + shp[ax + 1:]), ax, 0)


def _join8(g, ax):
    shp = g.shape[1:]
    return jnp.moveaxis(g, 0, ax).reshape(shp[:ax] + (N_DEV * shp[ax],) + shp[ax + 1:])


def _group_lanes(v, hg):
    v = v.reshape(SSM_GROUPS, hg)
    return jnp.pad(v, ((0, 0), (0, LANE - hg))).reshape(1, SSM_GROUPS * LANE)


def _ungroup_lanes(v, hg):
    return v.reshape(SSM_GROUPS, LANE)[:, :hg].reshape(1, SSM_GROUPS * hg)


def kernel(x, meta_tokens, a_norm_pre, a_w_in, a_conv_w, a_conv_b, a_dt_bias, a_a_log, a_d_skip, a_gate_norm, a_w_out, a_norm_post, kv_norm, w_kv, b_norm_pre, b_w_q, b_sinks, b_w_o, b_norm_post, f_norm_pre, f_w_up, f_conv_w, f_conv_b, f_w_down, f_norm_post, loss_target, m_meta_tokens, m_a_norm_pre, m_a_w_in, m_a_conv_w, m_a_conv_b, m_a_dt_bias, m_a_a_log, m_a_d_skip, m_a_gate_norm, m_a_w_out, m_a_norm_post, m_kv_norm, m_w_kv, m_b_norm_pre, m_b_w_q, m_b_sinks, m_b_w_o, m_b_norm_post, m_f_norm_pre, m_f_w_up, m_f_conv_w, m_f_conv_b, m_f_w_down, m_f_norm_post, v_meta_tokens, v_a_norm_pre, v_a_w_in, v_a_conv_w, v_a_conv_b, v_a_dt_bias, v_a_a_log, v_a_d_skip, v_a_gate_norm, v_a_w_out, v_a_norm_post, v_kv_norm, v_w_kv, v_b_norm_pre, v_b_w_q, v_b_sinks, v_b_w_o, v_b_norm_post, v_f_norm_pre, v_f_w_up, v_f_conv_w, v_f_conv_b, v_f_w_down, v_f_norm_post):
    args = locals()
    wts = {n: args[n] for n in WEIGHTS}
    mom = {n: args["m_" + n] for n in WEIGHTS}
    var = {n: args["v_" + n] for n in WEIGHTS}
    mx, my, mc = lax.axis_index("x"), lax.axis_index("y"), lax.axis_index("c")
    me = 4 * mx + 2 * my + mc
    rows = _seq_rows()
    hg = SSM_HEADS // SSM_GROUPS
    d = D_MODEL

    n_main = D_INNER + D_XBC
    ws_in, ws_up = a_w_in.shape[2], f_w_up.shape[2]
    segs_in = _col_segments(ws_in, [(0, 0, n_main)] + [(n_main + hg * g, n_main + LANE * g, hg)
                                                      for g in range(SSM_GROUPS)])
    segs_up = _col_segments(ws_up, [(0, 0, 2 * D_FF)])
    def gather_of(*ws):
        return _gather_comm([w.astype(BF16) for w in ws])

    g_in, small_full = _run_comm(_gather_comm([a_w_in[0].astype(BF16), _pack([wts[n] for n in SMALL], 0, 8, F32)]),
                                 "gather_first")
    full = {}
    for n, g in zip(SMALL, _unpack(small_full, 1, [wts[n].shape for n in SMALL])):
        full[n] = _join8(g, SHARD_AXIS[n])
    w_in_all = _assemble_cols(g_in, n_main + SSM_GROUPS * LANE, segs_in, "asm_w_in")
    w_up, w_down = [None, None], [None, None]
    bias_g = _group_lanes(wts["a_dt_bias"], hg)
    alog_g = _group_lanes(wts["a_a_log"], hg)
    dsk_g = _group_lanes(wts["a_d_skip"], hg)
    a_conv_w, a_conv_b = full["a_conv_w"][0], full["a_conv_b"]
    f_cw, f_cb = full["f_conv_w"], wts["f_conv_b"]
    fpre, fpost = wts["f_norm_pre"], wts["f_norm_post"]

    pad_rows = rows - N_META - SEQ
    h0 = jnp.concatenate([full["meta_tokens"], x[0], jnp.zeros((pad_rows, d), F32)], axis=0)
    tgt = jnp.pad(loss_target[0], ((N_META, pad_rows), (0, 0)))

    _, (hn0,) = _resid_norm(h0, None, None, [full["a_norm_pre"]], "norm_a_pre")
    zx, (g_out,) = _mm(hn0, w_in_all, "nn", F32, "mm_in", comm=gather_of(a_w_out[0]))
    w_out = g_out.reshape(D_INNER, d)
    xbc, (g_dn1,) = _conv_silu_fwd(zx, a_conv_w, a_conv_b, "conv_a", comm=gather_of(f_w_down[1]))
    (y_ssd, hst), (g_up0,) = _ssd_fwd(xbc, zx, bias_g, alog_g, dsk_g, "ssd_fwd", comm=gather_of(f_w_up[0]))
    w_up[0] = _assemble_cols(g_up0, 2 * D_FF, segs_up, "asm_w_up0")
    yn, (g_kv, g_q) = _gatenorm_fwd(y_ssd, zx, full["a_gate_norm"], "gatenorm", comm=gather_of(w_kv, b_w_q[0]))
    mix_a, (g_o,) = _mm(yn, w_out, "nn", F32, "mm_out", comm=gather_of(b_w_o[0]))
    w_kvf, w_q, w_o = g_kv.reshape(d, 2 * D_KV), g_q.reshape(d, d), g_o.reshape(d, d)
    h1, (fn0,) = _resid_norm(h0, mix_a, full["a_norm_post"], [fpre[0:1]], "resid_a")

    half = d // 2
    u0, (g_dn0,) = _mm(fn0, w_up[0], "nn", F32, "mm_up0", comm=gather_of(f_w_down[0]))
    w_down = [g_dn0.reshape(D_FF, d), g_dn1.reshape(D_FF, d)]
    act0, (g_up1a,) = _ffn_act_fwd(u0, f_cw[0], f_cb[0:1], "ffn_act0", comm=gather_of(f_w_up[1, :half]))
    ffn0 = _mm(act0, w_down[0], "nn", F32, "mm_down0")
    h2, (kvn, bn) = _resid_norm(h1, ffn0, fpost[0:1], [wts["kv_norm"].reshape(1, d), wts["b_norm_pre"]], "resid_f0")
    kv = _mm(kvn, w_kvf, "nn", F32, "mm_kv")
    q = _mm(bn, w_q, "nn", F32, "mm_q")
    (o, lse), (g_up1b,) = _attn_fwd(q, kv, wts["b_sinks"], "attn_fwd", comm=gather_of(f_w_up[1, half:]))
    w_up[1] = jnp.concatenate([_assemble_cols(g_up1a, 2 * D_FF, segs_up, "asm_w_up1a"),
                               _assemble_cols(g_up1b, 2 * D_FF, segs_up, "asm_w_up1b")], axis=0)
    mix_b = _mm(o, w_o, "nn", F32, "mm_o")
    h3, (fn1,) = _resid_norm(h2, mix_b, wts["b_norm_post"], [fpre[1:2]], "resid_b")
    u1 = _mm(fn1, w_up[1], "nn", F32, "mm_up1")
    act1 = _ffn_act_fwd(u1, f_cw[1], f_cb[1:2], "ffn_act1")
    ffn1 = _mm(act1, w_down[1], "nn", F32, "mm_down1")
    dh4, loss_row = _final_loss(h3, ffn1, fpost[1:2], tgt, "loss")
    loss = lax.psum(loss_row[0, 0], ("x", "y", "c"))

    grads = {}

    core = mc.astype(jnp.int32).reshape(1)

    def carried(res, comm):
        return res if comm is not None else (res, None)

    def ffn_bwd(dh_out, h_in, fn, u, act, ffn, i, c_dact=None, c_dwdown=None):
        dffn, dw_post = _norm_bwd(ffn, fpost[i:i + 1], dh_out, None, BF16, f"nb_fpost{i}")
        dact, got_a = carried(_mm(dffn, w_down[i], "nt", F32, f"mm_dact{i}", comm=c_dact), c_dact)
        dw_down, got_b = carried(_mm(act, dffn, "tn", BF16, f"mm_dwdown{i}", comm=c_dwdown), c_dwdown)
        dw_down = dw_down.reshape(N_DEV, -1, d)
        du, dwc, dbc = _ffn_act_bwd(u, dact, f_cw[i], f_cb[i:i + 1], f"ffn_act_bwd{i}")
        dfn, (s_dn,) = _mm(du, w_up[i], "nt", F32, f"mm_dfn{i}", comm=_swap_comm([dw_down]))
        sum_dn = _add_pairs(dw_down, s_dn, core, f"rs_add_dn{i}")
        dw_up = _scatter_cols(_mm(fn, du, "tn", BF16, f"mm_dwup{i}"), ws_up, segs_up, f"scat_w_up{i}")
        (dh_in, dw_pre), (s_up,) = _norm_bwd(h_in, fpre[i:i + 1], dfn, dh_out, F32, f"nb_fpre{i}",
                                             comm=_swap_comm([dw_up]))
        sum_up = _add_pairs(dw_up, s_up, core, f"rs_add_up{i}")
        return dh_in, dict(post=dw_post, sum_down=sum_dn, cw=jnp.concatenate([dwc[0], dwc[1]], axis=1),
                           cb=jnp.concatenate([dbc[0], dbc[1]], axis=1), sum_up=sum_up, pre=dw_pre), got_a, got_b

    dh3, gf1, _, _ = ffn_bwd(dh4, h3, fn1, u1, act1, ffn1, 1)
    dmix_b, grads["b_norm_post"] = _norm_bwd(mix_b, wts["b_norm_post"], dh3, None, BF16, "nb_bpost")
    do = _mm(dmix_b, w_o, "nt", F32, "mm_do")
    dw_o = _mm(o, dmix_b, "tn", BF16, "mm_dwo").reshape(N_DEV, -1, d)
    (dq, dkv, dsinks), (p_up1, p_dn1, s_o) = _attn_bwd(
        q, kv, wts["b_sinks"], do, lse, "attn_bwd",
        comm=_join_comms([_chips_comm([gf1["sum_up"], gf1["sum_down"]]), _swap_comm([dw_o])]))
    sum_o = _add_pairs(dw_o, s_o, core, "rs_add_o")
    grads["b_sinks"] = dsinks[:, :N_Q_HEADS]
    dbn = _mm(dq, w_q, "nt", F32, "mm_dbn")
    dw_q = _mm(bn, dq, "tn", BF16, "mm_dwq").reshape(N_DEV, -1, d)
    dkv16 = dkv.astype(BF16)
    dkvn = _mm(dkv16, w_kvf, "nt", F32, "mm_dkvn")
    dw_kv = _mm(kvn, dkv16, "tn", BF16, "mm_dwkv").reshape(N_DEV, -1, 2 * D_KV)
    (dh2, grads["b_norm_pre"]), (s_q, s_kv) = _norm_bwd(h2, wts["b_norm_pre"], dbn, dh3, F32, "nb_bpre",
                                                        comm=_swap_comm([dw_q, dw_kv]))
    sum_q, sum_kv = _add_pairs(dw_q, s_q, core, "rs_add_q"), _add_pairs(dw_kv, s_kv, core, "rs_add_kv")
    dh2, dw_kvn = _norm_bwd(h2, wts["kv_norm"].reshape(1, d), dkvn, dh2, F32, "nb_kv")
    grads["kv_norm"] = dw_kvn.reshape(d)
    dh1, gf0, (p_o,), (p_q, p_kv) = ffn_bwd(dh2, h1, fn0, u0, act0, ffn0, 0, c_dact=_chips_comm([sum_o]),
                                            c_dwdown=_chips_comm([sum_q, sum_kv]))
    grads["f_norm_post"] = jnp.concatenate([gf0["post"], gf1["post"]], axis=0)
    grads["f_norm_pre"] = jnp.concatenate([gf0["pre"], gf1["pre"]], axis=0)
    grads["f_conv_w"] = jnp.stack([gf0["cw"], gf1["cw"]])
    grads["f_conv_b"] = jnp.concatenate([gf0["cb"], gf1["cb"]], axis=0)

    dmix_a, grads["a_norm_post"] = _norm_bwd(mix_a, full["a_norm_post"], dh1, None, BF16, "nb_apost")
    dyn = _mm(dmix_a, w_out, "nt", F32, "mm_dyn")
    dw_out = _mm(yn, dmix_a, "tn", BF16, "mm_dwout").reshape(N_DEV, -1, d)
    (dy_ssd, dz, grads["a_gate_norm"]), (s_out,) = _gatenorm_bwd(y_ssd, zx, full["a_gate_norm"], dyn, "gatenorm_bwd",
                                                                 comm=_swap_comm([dw_out]))
    sum_out = _add_pairs(dw_out, s_out, core, "rs_add_out")
    (dxs, dbm, dcm, ddtp, dalog, ddsk, dbias), (p_up0, p_dn0, p_out) = _ssd_bwd(
        xbc, zx, bias_g, alog_g, dsk_g, dy_ssd, hst, "ssd_bwd",
        comm=_chips_comm([gf0["sum_up"], gf0["sum_down"], sum_out]))
    dxbc = jnp.concatenate([dxs, dbm, dcm], axis=1)
    grads["a_a_log"] = _ungroup_lanes(dalog, hg)
    grads["a_d_skip"] = _ungroup_lanes(ddsk, hg)
    grads["a_dt_bias"] = _ungroup_lanes(dbias, hg)
    dpre, dcw, dcb = _conv_silu_bwd(zx, dxbc, a_conv_w, a_conv_b, "conv_a_bwd")
    grads["a_conv_w"], grads["a_conv_b"] = dcw[None], dcb
    dzx = jnp.concatenate([dz, dpre, ddtp.astype(BF16)], axis=1)
    dw_in8 = _scatter_cols(_mm(hn0, dzx, "tn", BF16, "mm_dwin"), ws_in, segs_in, "scat_w_in")
    dhn0, (s_in,) = _mm(dzx, w_in_all, "nt", F32, "mm_dhn0", comm=_swap_comm([dw_in8]))
    sum_in = _add_pairs(dw_in8, s_in, core, "rs_add_in")
    half_in = sum_in.shape[1] // 2
    (dh0, grads["a_norm_pre"]), (p_in_a,) = _norm_bwd(h0, full["a_norm_pre"], dhn0, dh1, F32, "nb_apre",
                                                      comm=_chips_comm([sum_in[:, :half_in]]))
    grad_x = dh0[N_META:N_META + SEQ][None]
    grads["meta_tokens"] = dh0[:N_META]

    small_local = _pack([_split8(grads[n], SHARD_AXIS[n], wts[n].shape[SHARD_AXIS[n]]) for n in SMALL], 1, 8, F32)
    repl_local = _pack([grads[n] for n in REPL], 0, 8, F32)
    n_sr = small_local.shape[1]
    small_vec = jnp.concatenate([small_local.reshape(N_DEV * n_sr, LANE), repl_local], axis=0)
    tail = _join_comms([_chips_comm([sum_in[:, half_in:]]), _gather_comm([small_vec])])
    parts_big = dict(a_w_out=[p_out], w_kv=[p_kv], b_w_q=[p_q], b_w_o=[p_o], f_w_up=[p_up0, p_up1],
                     f_w_down=[p_dn0, p_dn1])

    def flat_f32(dct, names, mult):
        return _pack([dct[n] for n in names], 0, mult, F32)

    def adamw_big(n, comm=None):
        shp3 = (len(parts_big[n]),) + parts_big[n][0].shape[1:]
        res = _adamw(parts_big[n], *[dct[n].reshape(shp3) for dct in (wts, mom, var)], f"adamw_{n}", comm=comm)
        res, got = res if comm is not None else (res, None)
        big_out[n] = [r.reshape(wts[n].shape) for r in res]
        return got

    big_out = {}
    p_in_b, small_all = adamw_big("f_w_up", tail)
    parts_big["a_w_in"] = [p_in_a, p_in_b]
    for n in BIG:
        if n != "f_w_up":
            adamw_big(n)
    mine_small = lax.dynamic_slice_in_dim(small_all, me * n_sr, n_sr, axis=1)
    parts_small = jnp.concatenate([mine_small, small_all[:, N_DEV * n_sr:]], axis=1)
    sm_in = [jnp.concatenate([flat_f32(dct, SMALL, 8), flat_f32(dct, REPL, 8)], axis=0)[None] for dct in (wts, mom, var)]
    small_out = [r[0] for r in _adamw([parts_small], *sm_in, "adamw_small")]

    outs = []
    for kind in range(4):
        res = {n: big_out[n][kind] for n in BIG}
        for n, a in zip(SMALL, _unpack(small_out[kind][:n_sr], 0, [wts[n].shape for n in SMALL])):
            res[n] = a
        for n, a in zip(REPL, _unpack(small_out[kind][n_sr:], 0, [wts[n].shape for n in REPL])):
            res[n] = a
        outs.append(res)
    return (loss, grad_x, *[outs[0][n] for n in WEIGHTS], *[outs[1][n] for n in WEIGHTS],
            *[outs[2][n] for n in WEIGHTS], *[outs[3][n] for n in WEIGHTS])
```

```python
import functools
import math

import jax
import jax.numpy as jnp
from jax import lax
from jax.experimental import pallas as pl
from jax.experimental.pallas import tpu as pltpu

F32, BF16 = jnp.float32, jnp.bfloat16
S = jax.ShapeDtypeStruct

D_MODEL = 1024
SEQ = 2048
N_META = 16
D_INNER = 2048
HEAD_P = 64
SSM_HEADS = D_INNER // HEAD_P
SSM_GROUPS = 4
D_STATE = 128
SSM_CONV = 4
D_BC = SSM_GROUPS * D_STATE
D_XBC = D_INNER + 2 * D_BC
ATTN_DH = 64
N_Q_HEADS = D_MODEL // ATTN_DH
N_KV_HEADS = 4
D_KV = N_KV_HEADS * ATTN_DH
WINDOW = 128
D_FF = 2816
FFN_CONV = 3
RMS_EPS = 1e-6
NEG = -1e30
LR, B1, B2, EPS, WD, STEP = 0.001, 0.9, 0.999, 1e-08, 0.01, 10

N_DEV = 8
T = 128
LANE = 128
VMEM_LIMIT = 48 * 1024 * 1024

BIG = ("a_w_in", "a_w_out", "w_kv", "b_w_q", "b_w_o", "f_w_up", "f_w_down")
SMALL = ("meta_tokens", "a_norm_pre", "a_conv_w", "a_conv_b", "a_gate_norm", "a_norm_post", "f_conv_w")
REPL = ("a_dt_bias", "a_a_log", "a_d_skip", "kv_norm", "b_norm_pre", "b_sinks", "b_norm_post",
        "f_norm_pre", "f_conv_b", "f_norm_post")
SHARD_AXIS = dict(a_w_in=2, a_w_out=1, w_kv=0, b_w_q=1, b_w_o=1, f_w_up=2, f_w_down=1, meta_tokens=1,
                  a_norm_pre=1, a_conv_w=2, a_conv_b=1, a_gate_norm=1, a_norm_post=1, f_conv_w=2)
WEIGHTS = ("meta_tokens", "a_norm_pre", "a_w_in", "a_conv_w", "a_conv_b", "a_dt_bias", "a_a_log", "a_d_skip",
           "a_gate_norm", "a_w_out", "a_norm_post", "kv_norm", "w_kv", "b_norm_pre", "b_w_q", "b_sinks", "b_w_o",
           "b_norm_post", "f_norm_pre", "f_w_up", "f_conv_w", "f_conv_b", "f_w_down", "f_norm_post")


def _seq_rows():
    return -(-(N_META + SEQ) // T) * T


def _cp(sem=None):
    return pltpu.CompilerParams(dimension_semantics=sem, vmem_limit_bytes=VMEM_LIMIT)


def _pick(n, target):
    t = min(n, target)
    t -= t % LANE
    while n % t:
        t -= LANE
    return t


def _sigmoid(x):
    return 0.5 * jnp.tanh(0.5 * x) + 0.5


def _softplus(x):
    return jnp.maximum(x, 0.0) + jnp.log(1.0 + jnp.exp(-jnp.abs(x)))


_NN = (((1,), (0,)), ((), ()))
_NT = (((1,), (1,)), ((), ()))
_TN = (((0,), (0,)), ((), ()))


def _dot(a, b, dims=_NN):
    return lax.dot_general(a, b, dims, preferred_element_type=F32)


def _dot_hi(a, b):
    return lax.dot_general(a, b, _NN, precision=lax.Precision.HIGHEST, preferred_element_type=F32)


_HBM = pl.BlockSpec(memory_space=pltpu.HBM)
_MESH = pl.DeviceIdType.MESH


class _Comm:
    def __init__(self, ins, out_shapes, scratch, first, last):
        self.ins, self.out_shapes, self.scratch, self.first, self.last = ins, out_shapes, scratch, first, last


def _call(body, name, out_shape, grid, in_specs, out_specs, sem, args, scratch=(), comm=None):
    if comm is None:
        return pl.pallas_call(body, name=name, out_shape=out_shape, grid=grid, in_specs=in_specs, out_specs=out_specs,
                              scratch_shapes=list(scratch), compiler_params=_cp(sem))(*args)
    single = not isinstance(out_shape, (list, tuple))
    outs = [out_shape] if single else list(out_shape)
    ospecs = [out_specs] if single else list(out_specs)
    n_in, n_out, n_scr, ci, co = len(in_specs), len(outs), len(scratch), len(comm.ins), len(comm.out_shapes)

    def carrier(*refs):
        p = 0
        parts = []
        for cnt in (n_in, ci, n_out, co, n_scr, len(comm.scratch)):
            parts.append(refs[p:p + cnt])
            p += cnt
        ins, cins, outs_r, couts, scr, cscr = parts
        ids = [pl.program_id(i) for i in range(len(grid))]
        first, last = ids[0] == 0, ids[0] == grid[0] - 1
        for i in range(1, len(grid)):
            first, last = first & (ids[i] == 0), last & (ids[i] == grid[i] - 1)

        @pl.when(first)
        def _():
            comm.first(cins, couts, cscr)

        body(*ins, *outs_r, *scr)

        @pl.when(last)
        def _():
            comm.last(cins, couts, cscr)

    res = pl.pallas_call(
        carrier, name=name, out_shape=outs + list(comm.out_shapes), grid=grid,
        in_specs=list(in_specs) + [_HBM] * ci, out_specs=ospecs + [_HBM] * co,
        scratch_shapes=list(scratch) + list(comm.scratch),
        compiler_params=_cp(("arbitrary",) * len(grid)))(*args, *comm.ins)
    mine = res[0] if single else list(res[:n_out])
    return mine, list(res[n_out:])


def _mm(a, b, mode, out_dtype, name, comm=None):
    if mode == "tn":
        m, kk = a.shape
        planes, width = (b.shape[0], b.shape[2]) if b.ndim == 3 else (1, b.shape[1])
        n = planes * width
        tko, tn = _pick(kk, 512), _pick(width, 1536)
        per = width // tn

        def body(a_ref, b_ref, o_ref):
            o_ref[...] = _dot(a_ref[...], b_ref[...], _TN).astype(o_ref.dtype)

        b_spec = (pl.BlockSpec((None, m, tn), lambda i, j: (j // per, 0, j % per)) if b.ndim == 3
                  else pl.BlockSpec((m, tn), lambda i, j: (0, j)))
        return _call(
            body, name, S((kk, n), out_dtype), (kk // tko, n // tn), [pl.BlockSpec((m, tko), lambda i, j: (0, i)), b_spec],
            pl.BlockSpec((tko, tn), lambda i, j: (i, j)), ("parallel", "parallel"), (a, b), comm=comm)

    planes, width = (a.shape[0], a.shape[2]) if a.ndim == 3 else (1, a.shape[1])
    m, kk = a.shape[-2], planes * width
    n = b.shape[1] if mode == "nn" else b.shape[0]
    tn = _pick(n, 512)
    tk = kk if kk <= 2048 else _pick(width, 1536)
    nk = kk // tk
    per = width // tk
    dims = _NN if mode == "nn" else _NT

    def body(a_ref, b_ref, o_ref, *acc):
        part = _dot(a_ref[...], b_ref[...], dims)
        if nk == 1:
            o_ref[...] = part.astype(o_ref.dtype)
        else:
            k = pl.program_id(1)

            @pl.when(k == 0)
            def _():
                acc[0][...] = part

            @pl.when(k > 0)
            def _():
                acc[0][...] += part

            @pl.when(k == nk - 1)
            def _():
                o_ref[...] = acc[0][...].astype(o_ref.dtype)

    b_spec = (pl.BlockSpec((tk, tn), lambda j, k: (k, j)) if mode == "nn"
              else pl.BlockSpec((tn, tk), lambda j, k: (j, k)))
    a_spec = (pl.BlockSpec((None, m, tk), lambda j, k: (k // per, 0, k % per)) if a.ndim == 3
              else pl.BlockSpec((m, tk), lambda j, k: (0, k)))
    return _call(
        body, name, S((m, n), out_dtype), (n // tn, nk), [a_spec, b_spec],
        pl.BlockSpec((m, tn), lambda j, k: (0, j)), ("parallel", "arbitrary"), (a, b),
        scratch=[pltpu.VMEM((m, tn), F32)] if nk > 1 else [], comm=comm)


def _rms(x, w):
    return x * lax.rsqrt(jnp.mean(x * x, axis=-1, keepdims=True) + RMS_EPS) * w


def _row_tile(rows):
    return rows // 8


def _embed_norm(meta, x, w, rows, name, comm=None):
    n_meta, d = meta.shape
    n_x = x.shape[0]
    last = rows // T - 1
    assert n_meta % 8 == 0 and n_meta < T and n_meta + n_x == last * T + n_meta and last * T >= n_x

    def body(m_ref, x_ref, w_ref, h_ref, hn_ref):
        i = pl.program_id(0)

        @pl.when(i == 0)
        def _():
            h_ref[0:n_meta, :] = m_ref[...]
            h_ref[n_meta:T, :] = x_ref[0:T - n_meta, :]

        @pl.when((i > 0) & (i < last))
        def _():
            h_ref[...] = x_ref[pl.ds(pl.multiple_of(i * T - n_meta, 8), T), :]

        @pl.when(i == last)
        def _():
            h_ref[0:n_meta, :] = x_ref[n_x - n_meta:n_x, :]
            h_ref[n_meta:T, :] = jnp.zeros((T - n_meta, d), F32)

        hn_ref[...] = _rms(h_ref[...], w_ref[...]).astype(hn_ref.dtype)

    row = pl.BlockSpec((T, d), lambda i: (i, 0))
    return _call(body, name, [S((rows, d), F32), S((rows, d), BF16)], (rows // T,),
                 [pl.BlockSpec((n_meta, d), lambda i: (0, 0)), pl.BlockSpec((n_x, d), lambda i: (0, 0)),
                  pl.BlockSpec((1, d), lambda i: (0, 0))], [row, row], ("parallel",), (meta, x, w), comm=comm)


def _resid_norm(h, br, w_post, next_ws, name):
    rows, d = h.shape
    tr = _row_tile(rows)
    has_br = br is not None
    nw = len(next_ws)

    def body(*refs):
        h_ref = refs[0]
        pos = 1
        x = h_ref[...]
        if has_br:
            x = x + _rms(refs[1][...], refs[2][...])
            pos = 3
        w_refs = refs[pos:pos + nw]
        outs = refs[pos + nw:]
        if has_br:
            outs[0][...] = x
            outs = outs[1:]
        for w_ref, o_ref in zip(w_refs, outs):
            o_ref[...] = _rms(x, w_ref[...]).astype(o_ref.dtype)

    row = pl.BlockSpec((tr, d), lambda i: (i, 0))
    vec = pl.BlockSpec((1, d), lambda i: (0, 0))
    ins = [h] + ([br, w_post] if has_br else []) + list(next_ws)
    in_specs = [row] + ([row, vec] if has_br else []) + [vec] * nw
    out_shape = ([S((rows, d), F32)] if has_br else []) + [S((rows, d), BF16)] * nw
    res = pl.pallas_call(body, name=name, out_shape=out_shape, grid=(rows // tr,), in_specs=in_specs,
                         out_specs=[row] * len(out_shape), compiler_params=_cp(("parallel",)))(*ins)
    if has_br:
        return res[0], list(res[1:])
    return h, list(res)


def _norm_bwd(x, w, dy, add, out_dtype, name, comm=None):
    rows, d = x.shape
    tr = _row_tile(rows)
    has_add = add is not None

    def body(*refs):
        x_ref, w_ref, dy_ref = refs[:3]
        dx_ref, dw_ref = refs[-2:]
        xv = x_ref[...]
        r = lax.rsqrt(jnp.mean(xv * xv, axis=-1, keepdims=True) + RMS_EPS)
        dyv = dy_ref[...].astype(F32)
        wdy = dyv * w_ref[...]
        dx = r * wdy - xv * (r * r * r) * jnp.mean(xv * wdy, axis=-1, keepdims=True)
        if has_add:
            dx = dx + refs[3][...]
        dx_ref[...] = dx.astype(dx_ref.dtype)

        @pl.when(pl.program_id(0) == 0)
        def _():
            dw_ref[...] = jnp.zeros_like(dw_ref)

        dw_ref[...] += jnp.sum(dyv * xv * r, axis=0, keepdims=True)

    row = pl.BlockSpec((tr, d), lambda i: (i, 0))
    vec = pl.BlockSpec((1, d), lambda i: (0, 0))
    ins = [x, w, dy] + ([add] if has_add else [])
    return _call(body, name, [S((rows, d), out_dtype), S((1, d), F32)], (rows // tr,),
                 [row, vec, row] + ([row] if has_add else []), [row, vec], ("arbitrary",), ins, comm=comm)


def _final_loss(h, br, w_post, tgt, name):
    rows, d = h.shape
    tr = _row_tile(rows)

    def body(h_ref, br_ref, w_ref, t_ref, dh_ref, loss_ref):
        i = pl.program_id(0)
        y = h_ref[...] + _rms(br_ref[...], w_ref[...])
        r = i * tr + lax.broadcasted_iota(jnp.int32, (tr, 1), 0)
        real = (r >= N_META) & (r < N_META + SEQ)
        diff = jnp.where(real, y - t_ref[...], 0.0)
        dh_ref[...] = diff * (1.0 / d)

        @pl.when(i == 0)
        def _():
            loss_ref[...] = jnp.zeros_like(loss_ref)

        loss_ref[...] += jnp.sum(diff * diff) * (0.5 / d)

    row = pl.BlockSpec((tr, d), lambda i: (i, 0))
    return pl.pallas_call(body, name=name, out_shape=[S((rows, d), F32), S((1, LANE), F32)], grid=(rows // tr,),
                          in_specs=[row, row, pl.BlockSpec((1, d), lambda i: (0, 0)), row],
                          out_specs=[row, pl.BlockSpec((1, LANE), lambda i: (0, 0))],
                          compiler_params=_cp(("arbitrary",)))(h, br, w_post, tgt)


def _gatenorm_fwd(y, zx, w, name, comm=None):
    rows, d = y.shape
    tr = _row_tile(rows)

    def body(y_ref, z_ref, w_ref, o_ref):
        z = z_ref[...]
        o_ref[...] = _rms(y_ref[...] * z * _sigmoid(z), w_ref[...]).astype(o_ref.dtype)

    row = pl.BlockSpec((tr, d), lambda i: (i, 0))
    return _call(body, name, S((rows, d), BF16), (rows // tr,), [row, row, pl.BlockSpec((1, d), lambda i: (0, 0))],
                 row, ("parallel",), (y, zx, w), comm=comm)


def _gatenorm_bwd(y, zx, w, dyn, name, comm=None):
    rows, d = y.shape
    tr = _row_tile(rows)

    def body(y_ref, z_ref, w_ref, dyn_ref, dy_ref, dz_ref, dw_ref):
        yv, z = y_ref[...], z_ref[...]
        sg = _sigmoid(z)
        sz = z * sg
        g = yv * sz
        r = lax.rsqrt(jnp.mean(g * g, axis=-1, keepdims=True) + RMS_EPS)
        dyn_v = dyn_ref[...]
        wdy = dyn_v * w_ref[...]
        dg = r * wdy - g * (r * r * r) * jnp.mean(g * wdy, axis=-1, keepdims=True)
        dy_ref[...] = dg * sz
        dz_ref[...] = (dg * yv * sg * (1.0 + z * (1.0 - sg))).astype(dz_ref.dtype)

        @pl.when(pl.program_id(0) == 0)
        def _():
            dw_ref[...] = jnp.zeros_like(dw_ref)

        dw_ref[...] += jnp.sum(dyn_v * g * r, axis=0, keepdims=True)

    row = pl.BlockSpec((tr, d), lambda i: (i, 0))
    vec = pl.BlockSpec((1, d), lambda i: (0, 0))
    return _call(body, name, [S((rows, d), F32), S((rows, d), BF16), S((1, d), F32)], (rows // tr,),
                 [row, row, vec, row], [row, row, vec], ("arbitrary",), (y, zx, w, dyn), comm=comm)


def _shift_down(x, s, rows_iota):
    if s == 0:
        return x
    return jnp.where(rows_iota >= s, pltpu.roll(x, s, 0), 0.0)


def _shift_up(x, s, rows_iota):
    if s == 0:
        return x
    rows = x.shape[0]
    return jnp.where(rows_iota < rows - s, pltpu.roll(x, rows - s, 0), 0.0)


def _r16(v):
    return v.astype(BF16).astype(F32)


def _conv(x, w_ref, b_ref, taps, rows_iota):
    x = _r16(x)
    acc = jnp.zeros_like(x)
    for k in range(taps):
        acc = acc + _r16(w_ref[k:k + 1, :]) * _shift_down(x, taps - 1 - k, rows_iota)
    return acc + b_ref[...]


def _conv_bwd(x, du, w_ref, dw_ref, db_ref, taps, rows_iota):
    db_ref[...] = jnp.sum(du, axis=0, keepdims=True)
    x, du = _r16(x), _r16(du)
    dx = jnp.zeros_like(x)
    for k in range(taps):
        s = taps - 1 - k
        dx = dx + _r16(w_ref[k:k + 1, :]) * _shift_up(du, s, rows_iota)
        dw_ref[k:k + 1, :] = jnp.sum(du * _shift_down(x, s, rows_iota), axis=0, keepdims=True)
    return dx


def _conv_silu_fwd(zx, w, b, name, comm=None):
    rows = zx.shape[0]
    cb = 512
    off = D_INNER // cb

    def body(x_ref, w_ref, b_ref, o_ref):
        it = lax.broadcasted_iota(jnp.int32, (rows, 1), 0)
        u = _conv(x_ref[...], w_ref, b_ref, SSM_CONV, it)
        o_ref[...] = u * _sigmoid(u)

    return _call(
        body, name, S((rows, D_XBC), F32), (D_XBC // cb,),
        [pl.BlockSpec((rows, cb), lambda j: (0, off + j)), pl.BlockSpec((SSM_CONV, cb), lambda j: (0, j)),
         pl.BlockSpec((1, cb), lambda j: (0, j))],
        pl.BlockSpec((rows, cb), lambda j: (0, j)), ("parallel",), (zx, w, b), comm=comm)


def _conv_silu_bwd(zx, dxbc, w, b, name):
    rows = zx.shape[0]
    cb = 512
    off = D_INNER // cb

    def body(x_ref, d_ref, w_ref, b_ref, dx_ref, dw_ref, db_ref):
        it = lax.broadcasted_iota(jnp.int32, (rows, 1), 0)
        x = x_ref[...]
        u = _conv(x, w_ref, b_ref, SSM_CONV, it)
        sg = _sigmoid(u)
        du = d_ref[...] * sg * (1.0 + u * (1.0 - sg))
        dx_ref[...] = _conv_bwd(x, du, w_ref, dw_ref, db_ref, SSM_CONV, it).astype(dx_ref.dtype)

    col = pl.BlockSpec((rows, cb), lambda j: (0, j))
    wsp = pl.BlockSpec((SSM_CONV, cb), lambda j: (0, j))
    bsp = pl.BlockSpec((1, cb), lambda j: (0, j))
    return pl.pallas_call(
        body, name=name, out_shape=[S((rows, D_XBC), BF16), S((SSM_CONV, D_XBC), F32), S((1, D_XBC), F32)],
        grid=(D_XBC // cb,), in_specs=[pl.BlockSpec((rows, cb), lambda j: (0, off + j)), col, wsp, bsp],
        out_specs=[col, wsp, bsp], compiler_params=_cp(("parallel",)))(zx, dxbc, w, b)


def _ffn_act_fwd(u, w, b, name, comm=None):
    rows = u.shape[0]
    cb = 256
    nb = D_FF // cb

    def body(g_ref, v_ref, wg_ref, wv_ref, bg_ref, bv_ref, o_ref):
        it = lax.broadcasted_iota(jnp.int32, (rows, 1), 0)
        g = _conv(g_ref[...], wg_ref, bg_ref, FFN_CONV, it)
        v = _conv(v_ref[...], wv_ref, bv_ref, FFN_CONV, it)
        o_ref[...] = (g * _sigmoid(g) * v).astype(o_ref.dtype)

    def sp(r, shift):
        return pl.BlockSpec((r, cb), lambda j: (0, shift + j))

    return _call(
        body, name, S((rows, D_FF), BF16), (nb,),
        [sp(rows, 0), sp(rows, nb), sp(FFN_CONV, 0), sp(FFN_CONV, nb), sp(1, 0), sp(1, nb)],
        sp(rows, 0), ("parallel",), (u, u, w, w, b, b), comm=comm)


def _ffn_act_bwd(u, dact, w, b, name, comm=None):
    rows = u.shape[0]
    cb = 256
    nb = D_FF // cb

    def body(g_ref, v_ref, d_ref, wg_ref, wv_ref, bg_ref, bv_ref, du_ref, dw_ref, db_ref):
        it = lax.broadcasted_iota(jnp.int32, (rows, 1), 0)
        xg, xv = g_ref[...], v_ref[...]
        g = _conv(xg, wg_ref, bg_ref, FFN_CONV, it)
        v = _conv(xv, wv_ref, bv_ref, FFN_CONV, it)
        sg = _sigmoid(g)
        d = d_ref[...]
        dgate = d * v * sg * (1.0 + g * (1.0 - sg))
        dval = d * g * sg
        du_ref[0] = _conv_bwd(xg, dgate, wg_ref, dw_ref.at[0], db_ref.at[0], FFN_CONV, it).astype(du_ref.dtype)
        du_ref[1] = _conv_bwd(xv, dval, wv_ref, dw_ref.at[1], db_ref.at[1], FFN_CONV, it).astype(du_ref.dtype)

    def sp(r, shift):
        return pl.BlockSpec((r, cb), lambda j: (0, shift + j))

    def both(r):
        return pl.BlockSpec((2, r, cb), lambda j: (0, 0, j))

    return _call(
        body, name, [S((2, rows, D_FF), BF16), S((2, FFN_CONV, D_FF), F32), S((2, 1, D_FF), F32)], (nb,),
        [sp(rows, 0), sp(rows, nb), sp(rows, 0), sp(FFN_CONV, 0), sp(FFN_CONV, nb), sp(1, 0), sp(1, nb)],
        [both(rows), both(FFN_CONV), both(1)], ("parallel",), (u, u, dact, w, w, b, b), comm=comm)


def _ssd_consts(dtp_ref, bias_ref, alog_ref, hg):
    lane = lax.broadcasted_iota(jnp.int32, (1, LANE), 1)
    pre = dtp_ref[...] + bias_ref[...]
    dt = _softplus(pre)
    a_row = jnp.where(lane < hg, -jnp.exp(alog_ref[...]), 0.0)
    ri = lax.broadcasted_iota(jnp.int32, (T, T), 0)
    ci = lax.broadcasted_iota(jnp.int32, (T, T), 1)
    cs = _dot_hi((ri >= ci).astype(F32), dt * a_row)
    return pre, dt, a_row, cs, ri, ci, lane


def _ssd_fwd(xbc, zx, bias, alog, dsk, name, comm=None):
    rows = xbc.shape[0]
    nc = rows // T
    hg = SSM_HEADS // SSM_GROUPS
    gw = hg * HEAD_P
    xoff, boff, coff = 0, D_INNER // D_STATE, (D_INNER + D_BC) // D_STATE
    dtoff = (D_INNER + D_XBC) // LANE

    def body(x_ref, b_ref, c_ref, dtp_ref, bias_ref, alog_ref, dsk_ref, y_ref, hst_ref, hs):
        c = pl.program_id(1)

        @pl.when(c == 0)
        def _():
            hs[...] = jnp.zeros_like(hs)

        _, dt, _, cs, ri, ci, _ = _ssd_consts(dtp_ref, bias_ref, alog_ref, hg)
        cst, dtt = cs.T, dt.T
        xt = x_ref[...].T
        bb, cbf = b_ref[...].astype(BF16), c_ref[...].astype(BF16)
        gt = _dot(bb, cbf, _NT)
        causal_t = ci >= ri
        dskv = dsk_ref[...]
        hall = hs[...]
        hst_ref[0, 0] = hall
        yts, new_h = [], []
        for k in range(hg):
            sl = slice(k * HEAD_P, (k + 1) * HEAD_P)
            csc, csr = cs[:, k:k + 1], cst[k:k + 1, :]
            lt = jnp.exp(jnp.where(causal_t, csr - csc, NEG))
            xk = xt[sl, :]
            xdt = xk * dtt[k:k + 1, :]
            hk = hall[sl, :]
            yd = _dot(xdt.astype(BF16), (gt * lt).astype(BF16))
            yo = jnp.exp(csr) * _dot(hk.astype(BF16), cbf, _NT)
            yts.append(yd + yo + dskv[:, k:k + 1] * xk)
            cl = cs[T - 1:T, k:k + 1]
            st = _dot((xdt * jnp.exp(cl - csr)).astype(BF16), bb)
            new_h.append(jnp.exp(cl) * hk + st)
        y_ref[...] = jnp.concatenate(yts, axis=0).T
        hs[...] = jnp.concatenate(new_h, axis=0)

    vec = pl.BlockSpec((1, LANE), lambda g, c: (0, g))
    return _call(
        body, name, [S((rows, D_INNER), F32), S((nc, SSM_GROUPS, gw, D_STATE), F32)], (SSM_GROUPS, nc),
        [pl.BlockSpec((T, gw), lambda g, c: (c, xoff + g)),
         pl.BlockSpec((T, D_STATE), lambda g, c: (c, boff + g)),
         pl.BlockSpec((T, D_STATE), lambda g, c: (c, coff + g)),
         pl.BlockSpec((T, LANE), lambda g, c: (c, dtoff + g)), vec, vec, vec],
        [pl.BlockSpec((T, gw), lambda g, c: (c, g)), pl.BlockSpec((1, 1, gw, D_STATE), lambda g, c: (c, g, 0, 0))],
        ("parallel", "arbitrary"), (xbc, xbc, xbc, zx, bias, alog, dsk),
        scratch=[pltpu.VMEM((gw, D_STATE), F32)], comm=comm)


def _ssd_bwd(xbc, zx, bias, alog, dsk, dy, hst, name, comm=None):
    rows = xbc.shape[0]
    nc = rows // T
    hg = SSM_HEADS // SSM_GROUPS
    gw = hg * HEAD_P
    boff, coff = D_INNER // D_STATE, (D_INNER + D_BC) // D_STATE
    dtoff = (D_INNER + D_XBC) // LANE

    def body(x_ref, b_ref, c_ref, dtp_ref, bias_ref, alog_ref, dsk_ref, dy_ref, hst_ref,
             dx_ref, db_ref, dc_ref, ddtp_ref, dalog_ref, ddsk_ref, dbias_ref, dhs):
        step = pl.program_id(1)

        @pl.when(step == 0)
        def _():
            dhs[...] = jnp.zeros_like(dhs)
            dalog_ref[...] = jnp.zeros_like(dalog_ref)
            ddsk_ref[...] = jnp.zeros_like(ddsk_ref)
            dbias_ref[...] = jnp.zeros_like(dbias_ref)

        pre, dt, a_row, cs, ri, ci, lane = _ssd_consts(dtp_ref, bias_ref, alog_ref, hg)
        cst, dtt = cs.T, dt.T
        xt, dyt = x_ref[...].T, dy_ref[...].T
        bb, cbf = b_ref[...].astype(BF16), c_ref[...].astype(BF16)
        gt = _dot(bb, cbf, _NT)
        causal_t = ci >= ri
        dskv = dsk_ref[...]
        hall, dhall = hst_ref[0, 0], dhs[...]
        head_row = lax.broadcasted_iota(jnp.int32, (T, 1), 0)
        last_l = lax.broadcasted_iota(jnp.int32, (1, T), 1) == T - 1
        dgt = jnp.zeros((T, T), F32)
        dc_acc = jnp.zeros((T, D_STATE), F32)
        db_acc = jnp.zeros((T, D_STATE), F32)
        ddt_rows = jnp.zeros((T, T), F32)
        dcs_rows = jnp.zeros((T, T), F32)
        qrow_cols = jnp.zeros((T, LANE), F32)
        ddsk_acc = jnp.zeros((1, LANE), F32)
        dxts, new_dh = [], []
        for k in range(hg):
            sl = slice(k * HEAD_P, (k + 1) * HEAD_P)
            csc, csr = cs[:, k:k + 1], cst[k:k + 1, :]
            lt = jnp.exp(jnp.where(causal_t, csr - csc, NEG))
            xk, dyk = xt[sl, :], dyt[sl, :]
            dtr, dk = dtt[k:k + 1, :], dskv[:, k:k + 1]
            xdt = xk * dtr
            mpt = gt * lt
            dyb = dyk.astype(BF16)
            dxdt = _dot(dyb, mpt.astype(BF16), _NT)
            dmt = _dot(xdt.astype(BF16), dyb, _TN)
            dgt = dgt + dmt * lt
            q = dmt * mpt
            q_rows = jnp.sum(q, axis=1, keepdims=True)
            q_cols = jnp.sum(q, axis=0, keepdims=True)
            hk, dhn = hall[sl, :], dhall[sl, :]
            e = jnp.exp(csr)
            cl = cs[T - 1:T, k:k + 1]
            wdec = jnp.exp(cl - csr)
            w = wdec * dtr
            rt = _dot(dhn.astype(BF16), bb, _NT)
            dxts.append(dtr * dxdt + dk * dyk + rt * w)
            xz = jnp.sum(xk * dxdt, axis=0, keepdims=True)
            dw = jnp.sum(rt * xk, axis=0, keepdims=True)
            dcl = jnp.exp(cl) * jnp.sum(dhn * hk) + jnp.sum(dw * w)
            yo = e * _dot(hk.astype(BF16), cbf, _NT)
            dcs_r = jnp.sum(dyk * yo, axis=0, keepdims=True) + q_cols - dw * w + jnp.where(last_l, dcl, 0.0)
            dye = (dyk * e).astype(BF16)
            dc_acc = dc_acc + _dot(dye, hk.astype(BF16), _TN)
            db_acc = db_acc + _dot((xk * w).astype(BF16), dhn.astype(BF16), _TN)
            new_dh.append(jnp.exp(cl) * dhn + _dot(dye, cbf))
            onehot = (lane == k).astype(F32)
            ddt_rows = ddt_rows + jnp.where(head_row == k, xz + dw * wdec, 0.0)
            dcs_rows = dcs_rows + jnp.where(head_row == k, dcs_r, 0.0)
            qrow_cols = qrow_cols + q_rows * onehot
            ddsk_acc = ddsk_acc + jnp.sum(dyk * xk) * onehot
        dx_ref[...] = jnp.concatenate(dxts, axis=0).T
        dhs[...] = jnp.concatenate(new_dh, axis=0)
        dc_ref[...] = _dot(dgt.T.astype(BF16), bb) + dc_acc
        db_ref[...] = _dot(dgt.astype(BF16), cbf) + db_acc
        da = _dot_hi((ci >= ri).astype(F32), dcs_rows.T - qrow_cols)
        ddtp = (ddt_rows.T + da * a_row) * _sigmoid(pre)
        ddtp = jnp.where(lane < hg, ddtp, 0.0)
        ddtp_ref[...] = ddtp
        dbias_ref[...] += jnp.sum(ddtp, axis=0, keepdims=True)
        dalog_ref[...] += jnp.sum(da * dt, axis=0, keepdims=True) * a_row
        ddsk_ref[...] += ddsk_acc

    def rc(c):
        return nc - 1 - c

    vec = pl.BlockSpec((1, LANE), lambda g, c: (0, g))
    xsp = pl.BlockSpec((T, gw), lambda g, c: (rc(c), g))
    return _call(
        body, name,
        [S((rows, D_INNER), F32), S((rows, D_BC), F32), S((rows, D_BC), F32),
         S((rows, SSM_GROUPS * LANE), F32), S((1, SSM_GROUPS * LANE), F32),
         S((1, SSM_GROUPS * LANE), F32), S((1, SSM_GROUPS * LANE), F32)],
        (SSM_GROUPS, nc),
        [xsp,
         pl.BlockSpec((T, D_STATE), lambda g, c: (rc(c), boff + g)),
         pl.BlockSpec((T, D_STATE), lambda g, c: (rc(c), coff + g)),
         pl.BlockSpec((T, LANE), lambda g, c: (rc(c), dtoff + g)), vec, vec, vec,
         xsp, pl.BlockSpec((1, 1, gw, D_STATE), lambda g, c: (rc(c), g, 0, 0))],
        [xsp,
         pl.BlockSpec((T, D_STATE), lambda g, c: (rc(c), g)),
         pl.BlockSpec((T, D_STATE), lambda g, c: (rc(c), g)),
         pl.BlockSpec((T, LANE), lambda g, c: (rc(c), g)), vec, vec, vec],
        ("parallel", "arbitrary"), (xbc, xbc, xbc, zx, bias, alog, dsk, dy, hst),
        scratch=[pltpu.VMEM((gw, D_STATE), F32)], comm=comm)


def _attn_tiles(kv_ref, j):
    prev = jnp.maximum(j - 1, 0)
    meta = kv_ref[0:T, :]
    prv = kv_ref[pl.ds(pl.multiple_of(prev * T, T), T), :]
    cur = kv_ref[pl.ds(pl.multiple_of(j * T, T), T), :]
    return jnp.concatenate([meta, prv, cur], axis=0)


def _attn_mask(j):
    r = j * T + lax.broadcasted_iota(jnp.int32, (3 * T, T), 1)
    row = lax.broadcasted_iota(jnp.int32, (3 * T, T), 0)
    t0, t1 = row < T, row < 2 * T
    s = jnp.where(t0, row, (j - 2) * T + row)
    ok = (s <= r) & ((s < N_META) | (s > r - WINDOW))
    use = (t0 & (j >= 2) & (row < N_META)) | (jnp.logical_not(t0) & t1 & (j >= 1)) | jnp.logical_not(t1)
    return ok & use


def _attn_fwd(q, kv, sinks, name, comm=None):
    rows = q.shape[0]
    scale = 1.0 / math.sqrt(ATTN_DH)
    qpk = N_Q_HEADS // N_KV_HEADS

    def body(q_ref, kv_ref, s_ref, o_ref, lse_ref):
        j = pl.program_id(0)
        kv3 = _attn_tiles(kv_ref, j).astype(BF16)
        mask = _attn_mask(j)
        qv = (q_ref[...] * scale).astype(BF16)
        sk = s_ref[...]
        lses = []
        for kh in range(N_KV_HEADS):
            k3 = kv3[:, kh * ATTN_DH:(kh + 1) * ATTN_DH]
            v3 = kv3[:, D_KV + kh * ATTN_DH:D_KV + (kh + 1) * ATTN_DH]
            for g in range(qpk):
                h = kh * qpk + g
                sink = sk[:, h:h + 1]
                sc = jnp.where(mask, _dot(k3, qv[:, h * ATTN_DH:(h + 1) * ATTN_DH], _NT), NEG)
                m = jnp.maximum(jnp.max(sc, axis=0, keepdims=True), sink)
                p = jnp.exp(sc - m)
                den = jnp.sum(p, axis=0, keepdims=True) + jnp.exp(sink - m)
                p = p * (1.0 / den)
                lses.append(m + jnp.log(den))
                o_ref[:, h * ATTN_DH:(h + 1) * ATTN_DH] = _dot(p.astype(BF16), v3, _TN).astype(o_ref.dtype)
        lse_ref[...] = jnp.concatenate(lses, axis=0)

    return _call(
        body, name, [S((rows, D_MODEL), BF16), S((N_Q_HEADS, rows), F32)], (rows // T,),
        [pl.BlockSpec((T, D_MODEL), lambda j: (j, 0)), pl.BlockSpec((rows, 2 * D_KV), lambda j: (0, 0)),
         pl.BlockSpec((1, N_Q_HEADS), lambda j: (0, 0))],
        [pl.BlockSpec((T, D_MODEL), lambda j: (j, 0)), pl.BlockSpec((N_Q_HEADS, T), lambda j: (0, j))],
        ("parallel",), (q, kv, sinks), comm=comm)


def _attn_bwd(q, kv, sinks, do, lse, name, comm=None):
    rows = q.shape[0]
    scale = 1.0 / math.sqrt(ATTN_DH)
    qpk = N_Q_HEADS // N_KV_HEADS

    def body(q_ref, kv_ref, s_ref, do_ref, lse_ref, dq_ref, dkv_ref, ds_ref):
        j = pl.program_id(0)

        @pl.when(j == 0)
        def _():
            dkv_ref[...] = jnp.zeros_like(dkv_ref)
            ds_ref[...] = jnp.zeros_like(ds_ref)

        kv3 = _attn_tiles(kv_ref, j).astype(BF16)
        mask = _attn_mask(j)
        qv = (q_ref[...] * scale).astype(BF16)
        dov = do_ref[...].astype(BF16)
        sk = s_ref[...]
        lsev = lse_ref[...]
        lane = lax.broadcasted_iota(jnp.int32, (1, LANE), 1)
        ds_acc = jnp.zeros((1, LANE), F32)
        prev = jnp.maximum(j - 1, 0)
        dqts = []
        for kh in range(N_KV_HEADS):
            ksl = slice(kh * ATTN_DH, (kh + 1) * ATTN_DH)
            vsl = slice(D_KV + kh * ATTN_DH, D_KV + (kh + 1) * ATTN_DH)
            k3, v3 = kv3[:, ksl], kv3[:, vsl]
            k3t = k3.T
            dk3 = jnp.zeros((3 * T, ATTN_DH), F32)
            dv3 = jnp.zeros((3 * T, ATTN_DH), F32)
            for g in range(qpk):
                h = kh * qpk + g
                hs = slice(h * ATTN_DH, (h + 1) * ATTN_DH)
                qh, doh = qv[:, hs], dov[:, hs]
                lh = lsev[h:h + 1, :]
                p = jnp.exp(jnp.where(mask, _dot(k3, qh, _NT), NEG) - lh)
                ps = jnp.exp(sk[:, h:h + 1] - lh)
                dp = _dot(v3, doh, _NT)
                delta = jnp.sum(p * dp, axis=0, keepdims=True)
                dsc = (p * (dp - delta)).astype(BF16)
                dqts.append(_dot(k3t, dsc) * scale)
                dk3 = dk3 + _dot(dsc, qh)
                dv3 = dv3 + _dot(p.astype(BF16), doh)
                ds_acc = ds_acc - jnp.sum(ps * delta) * (lane == h).astype(F32)
            for t, start in enumerate((0, pl.multiple_of(prev * T, T), pl.multiple_of(j * T, T))):
                rsl = pl.ds(start, T)
                dkv_ref[rsl, ksl] += dk3[t * T:(t + 1) * T, :]
                dkv_ref[rsl, vsl] += dv3[t * T:(t + 1) * T, :]
        ds_ref[...] += ds_acc
        dq_ref[...] = jnp.concatenate(dqts, axis=0).T.astype(dq_ref.dtype)

    blk = pl.BlockSpec((T, D_MODEL), lambda j: (j, 0))
    full = pl.BlockSpec((rows, 2 * D_KV), lambda j: (0, 0))
    return _call(
        body, name, [S((rows, D_MODEL), BF16), S((rows, 2 * D_KV), F32), S((1, LANE), F32)], (rows // T,),
        [blk, full, pl.BlockSpec((1, N_Q_HEADS), lambda j: (0, 0)), blk, pl.BlockSpec((N_Q_HEADS, T), lambda j: (0, j))],
        [blk, full, pl.BlockSpec((1, LANE), lambda j: (0, 0))], ("arbitrary",), (q, kv, sinks, do, lse), comm=comm)


BLOCK_BYTES = 1 << 20


def _div_tile(rows, cols):
    cap = max(16, BLOCK_BYTES // (4 * cols))
    best = None
    for t in range(16, min(rows, cap) + 1, 16):
        if rows % t == 0:
            best = t
    return best if best is not None else rows


def _adamw(parts, w, m, v, name, comm=None):
    layers, rows, cols = w.shape
    n = parts[0].shape[0]
    tr = _div_tile(rows, cols)
    c1 = 1.0 / (1.0 - B1 ** STEP)
    c2 = 1.0 / (1.0 - B2 ** STEP)

    def body(*refs):
        p_refs = refs[:layers]
        w_ref, m_ref, v_ref, g_ref, d_ref, nm_ref, nv_ref = refs[layers:]
        layer = pl.program_id(0)
        for l in range(layers):
            @pl.when(layer == l)
            def _(p_ref=p_refs[l]):
                g = p_ref[0].astype(F32)
                for i in range(1, n):
                    g = g + p_ref[i].astype(F32)
                nm = B1 * m_ref[...] + (1.0 - B1) * g
                nv = B2 * v_ref[...] + (1.0 - B2) * (g * g)
                g_ref[...] = g
                nm_ref[...] = nm
                nv_ref[...] = nv
                d_ref[...] = -LR * ((nm * c1) / (jnp.sqrt(nv * c2) + EPS) + WD * w_ref[...])

    def part_spec(l):
        return pl.BlockSpec((n, tr, cols), lambda k, i: (0, jnp.where(k == l, i, 0), 0))

    row = pl.BlockSpec((None, tr, cols), lambda k, i: (k, i, 0))
    return _call(body, name, [S((layers, rows, cols), F32)] * 4, (layers, rows // tr),
                 [part_spec(l) for l in range(layers)] + [row, row, row], [row] * 4, ("parallel", "parallel"),
                 (*parts, w, m, v), comm=comm)


def _col_segments(ws, runs):
    segs = []
    for glo, mlo, n in runs:
        while n > 0:
            d, off = divmod(glo, ws)
            take = min(n, ws - off)
            segs.append((d, off, mlo, take))
            glo, mlo, n = glo + take, mlo + take, n - take
    return segs


def _assemble_cols(g, width, segs, name):
    _, rows, ws = g.shape
    rb = _div_tile(rows, width // 2)

    def body(g_ref, o_ref):
        o_ref[...] = jnp.zeros_like(o_ref)
        for d, off, mlo, n in segs:
            o_ref[:, mlo:mlo + n] = g_ref[d, :, off:off + n]

    return pl.pallas_call(
        body, name=name, out_shape=S((rows, width), g.dtype), grid=(rows // rb,),
        in_specs=[pl.BlockSpec((N_DEV, rb, ws), lambda i: (0, i, 0))],
        out_specs=pl.BlockSpec((rb, width), lambda i: (i, 0)), compiler_params=_cp(("parallel",)))(g)


def _scatter_cols(dw, ws, segs, name):
    rows, width = dw.shape
    rb = _div_tile(rows, width)

    def body(w_ref, o_ref):
        for d, off, mlo, n in segs:
            o_ref[d, :, off:off + n] = w_ref[:, mlo:mlo + n].astype(o_ref.dtype)

    return pl.pallas_call(
        body, name=name, out_shape=S((N_DEV, rows, ws), BF16), grid=(rows // rb,),
        in_specs=[pl.BlockSpec((rb, width), lambda i: (i, 0))],
        out_specs=pl.BlockSpec((N_DEV, rb, ws), lambda i: (0, i, 0)), compiler_params=_cp(("parallel",)))(dw)


def _gather_comm(xs):
    n = len(xs)

    def setup(x_refs, out_refs, sems):
        send_sems, recv_sems, local_sems = sems
        mx, my, mc = lax.axis_index("x"), lax.axis_index("y"), lax.axis_index("c")
        me, sibling = (mx, my, mc), (mx, my, 1 - mc)
        chips = [(1 - mx, my), (mx, 1 - my), (1 - mx, 1 - my)]

        def blk(a, px, py, pc):
            return out_refs[a].at[4 * px + 2 * py + pc]

        def copy(a, k, block, to, src=None):
            return pltpu.make_async_remote_copy(
                src_ref=blk(a, *block) if src is None else src, dst_ref=blk(a, *block),
                send_sem=send_sems.at[a, k], recv_sem=recv_sems.at[a, k], device_id=to, device_id_type=_MESH)

        mine = [pltpu.make_async_copy(x_refs[a], blk(a, *me), local_sems.at[a]) for a in range(n)]
        own = []
        for a in range(n):
            own.append(copy(a, 0, me, sibling, src=x_refs[a]))
            own += [copy(a, 1 + i, me, (*chip, mc), src=x_refs[a]) for i, chip in enumerate(chips)]
        return me, sibling, chips, mc, copy, mine, own

    def first(x_refs, out_refs, sems):
        _, _, _, _, _, mine, own = setup(x_refs, out_refs, sems)
        for cp in mine + own:
            cp.start()

    def last(x_refs, out_refs, sems):
        me, sibling, chips, mc, copy, mine, own = setup(x_refs, out_refs, sems)
        passed = []
        for a in range(n):
            for i, chip in enumerate(chips):
                copy(a, 1 + i, (*chip, mc), me).wait_recv()
                passed.append(copy(a, 4 + i, (*chip, mc), sibling))
                passed[-1].start()
        for a in range(n):
            copy(a, 0, sibling, me).wait_recv()
            for i, chip in enumerate(chips):
                copy(a, 4 + i, (*chip, 1 - mc), me).wait_recv()
        for cp in own + passed:
            cp.wait_send()
        for cp in mine:
            cp.wait()

    return _Comm(list(xs), [S((N_DEV,) + x.shape, x.dtype) for x in xs],
                 [pltpu.SemaphoreType.DMA((n, 7)), pltpu.SemaphoreType.DMA((n, 7)), pltpu.SemaphoreType.DMA((n,))],
                 first, last)


def _swap_comm(gs):
    n = len(gs)

    def copies(g_refs, out_refs, sems):
        send_sems, recv_sems = sems
        mx, my, mc = lax.axis_index("x"), lax.axis_index("y"), lax.axis_index("c")
        return [pltpu.make_async_remote_copy(
            src_ref=g_refs[a].at[2 * k + 1 - mc], dst_ref=out_refs[a].at[k], send_sem=send_sems.at[a, k],
            recv_sem=recv_sems.at[a, k], device_id=(mx, my, 1 - mc), device_id_type=_MESH)
            for a in range(n) for k in range(4)]

    def first(g_refs, out_refs, sems):
        for cp in copies(g_refs, out_refs, sems):
            cp.start()

    def last(g_refs, out_refs, sems):
        for cp in copies(g_refs, out_refs, sems):
            cp.wait()

    return _Comm(list(gs), [S((4,) + g.shape[1:], g.dtype) for g in gs],
                 [pltpu.SemaphoreType.DMA((n, 4)), pltpu.SemaphoreType.DMA((n, 4))], first, last)


def _chips_comm(parts):
    n = len(parts)

    def copies(p_refs, out_refs, sems):
        send_sems, recv_sems, local_sems = sems
        mx, my, mc = lax.axis_index("x"), lax.axis_index("y"), lax.axis_index("c")
        mychip = 2 * mx + my
        chips = [(1 - mx, my), (mx, 1 - my), (1 - mx, 1 - my)]
        mine = [pltpu.make_async_copy(p_refs[a].at[mychip], out_refs[a].at[mychip], local_sems.at[a])
                for a in range(n)]
        return mine + [pltpu.make_async_remote_copy(
            src_ref=p_refs[a].at[2 * cx + cy], dst_ref=out_refs[a].at[mychip], send_sem=send_sems.at[a, i],
            recv_sem=recv_sems.at[a, i], device_id=(cx, cy, mc), device_id_type=_MESH)
            for a in range(n) for i, (cx, cy) in enumerate(chips)]

    def first(p_refs, out_refs, sems):
        for cp in copies(p_refs, out_refs, sems):
            cp.start()

    def last(p_refs, out_refs, sems):
        for cp in copies(p_refs, out_refs, sems):
            cp.wait()

    return _Comm(list(parts), [S(p.shape, p.dtype) for p in parts],
                 [pltpu.SemaphoreType.DMA((n, 3)), pltpu.SemaphoreType.DMA((n, 3)), pltpu.SemaphoreType.DMA((n,))],
                 first, last)


def _join_comms(comms):
    def split(refs, counts):
        out, p = [], 0
        for cnt in counts:
            out.append(refs[p:p + cnt])
            p += cnt
        return out

    ni = [len(c.ins) for c in comms]
    no = [len(c.out_shapes) for c in comms]
    ns = [len(c.scratch) for c in comms]

    def first(in_refs, out_refs, sems):
        for c, i, o, s in zip(comms, split(in_refs, ni), split(out_refs, no), split(sems, ns)):
            c.first(i, o, s)

    def last(in_refs, out_refs, sems):
        for c, i, o, s in zip(comms, split(in_refs, ni), split(out_refs, no), split(sems, ns)):
            c.last(i, o, s)

    return _Comm([x for c in comms for x in c.ins], [x for c in comms for x in c.out_shapes],
                 [x for c in comms for x in c.scratch], first, last)


def _add_pairs(mine, theirs, core, name):
    _, rows, cols = mine.shape
    tr = _div_tile(rows, cols)

    def body(core_ref, a_ref, b_ref, o_ref):
        o_ref[...] = (a_ref[...].astype(F32) + b_ref[...].astype(F32)).astype(o_ref.dtype)

    return pl.pallas_call(
        body, name=name, out_shape=S((4, rows, cols), BF16),
        grid_spec=pltpu.PrefetchScalarGridSpec(
            num_scalar_prefetch=1, grid=(4, rows // tr),
            in_specs=[pl.BlockSpec((None, tr, cols), lambda k, i, c: (2 * k + c[0], i, 0)),
                      pl.BlockSpec((None, tr, cols), lambda k, i, c: (k, i, 0))],
            out_specs=pl.BlockSpec((None, tr, cols), lambda k, i, c: (k, i, 0))),
        compiler_params=_cp(("parallel", "parallel")))(core, mine, theirs)


def _run_comm(comm, name):
    ci, co = len(comm.ins), len(comm.out_shapes)

    def body(*refs):
        comm.first(refs[:ci], refs[ci:ci + co], refs[ci + co:])
        comm.last(refs[:ci], refs[ci:ci + co], refs[ci + co:])

    return pl.pallas_call(body, name=name, out_shape=list(comm.out_shapes), in_specs=[_HBM] * ci,
                          out_specs=[_HBM] * co, scratch_shapes=list(comm.scratch))(*comm.ins)


def _flat_rows(n_elems, mult):
    rows = -(-n_elems // LANE)
    return -(-rows // mult) * mult


def _pack(arrs, lead, mult, dtype):
    lead_shape = arrs[0].shape[:lead]
    flat = jnp.concatenate([a.astype(dtype).reshape(lead_shape + (-1,)) for a in arrs], axis=-1)
    n = flat.shape[-1]
    rows = _flat_rows(n, mult)
    flat = jnp.pad(flat, [(0, 0)] * lead + [(0, rows * LANE - n)])
    return flat.reshape(lead_shape + (rows, LANE))


def _unpack(flat, lead, shapes):
    lead_shape = flat.shape[:lead]
    flat = flat.reshape(lead_shape + (-1,))
    out, off = [], 0
    for shp in shapes:
        n = math.prod(shp)
        out.append(flat[..., off:off + n].reshape(lead_shape + tuple(shp)))
        off += n
    return out


def _split8(full, ax, n):
    shp = full.shape
    return jnp.moveaxis(full.reshape(shp[:ax] + (N_DEV, n) + shp[ax + 1:]), ax, 0)


def _join8(g, ax):
    shp = g.shape[1:]
    return jnp.moveaxis(g, 0, ax).reshape(shp[:ax] + (N_DEV * shp[ax],) + shp[ax + 1:])


def _group_lanes(v, hg):
    v = v.reshape(SSM_GROUPS, hg)
    return jnp.pad(v, ((0, 0), (0, LANE - hg))).reshape(1, SSM_GROUPS * LANE)


def _ungroup_lanes(v, hg):
    return v.reshape(SSM_GROUPS, LANE)[:, :hg].reshape(1, SSM_GROUPS * hg)


def kernel(x, meta_tokens, a_norm_pre, a_w_in, a_conv_w, a_conv_b, a_dt_bias, a_a_log, a_d_skip, a_gate_norm, a_w_out, a_norm_post, kv_norm, w_kv, b_norm_pre, b_w_q, b_sinks, b_w_o, b_norm_post, f_norm_pre, f_w_up, f_conv_w, f_conv_b, f_w_down, f_norm_post, loss_target, m_meta_tokens, m_a_norm_pre, m_a_w_in, m_a_conv_w, m_a_conv_b, m_a_dt_bias, m_a_a_log, m_a_d_skip, m_a_gate_norm, m_a_w_out, m_a_norm_post, m_kv_norm, m_w_kv, m_b_norm_pre, m_b_w_q, m_b_sinks, m_b_w_o, m_b_norm_post, m_f_norm_pre, m_f_w_up, m_f_conv_w, m_f_conv_b, m_f_w_down, m_f_norm_post, v_meta_tokens, v_a_norm_pre, v_a_w_in, v_a_conv_w, v_a_conv_b, v_a_dt_bias, v_a_a_log, v_a_d_skip, v_a_gate_norm, v_a_w_out, v_a_norm_post, v_kv_norm, v_w_kv, v_b_norm_pre, v_b_w_q, v_b_sinks, v_b_w_o, v_b_norm_post, v_f_norm_pre, v_f_w_up, v_f_conv_w, v_f_conv_b, v_f_w_down, v_f_norm_post):
    args = locals()
    wts = {n: args[n] for n in WEIGHTS}
    mom = {n: args["m_" + n] for n in WEIGHTS}
    var = {n: args["v_" + n] for n in WEIGHTS}
    mx, my, mc = lax.axis_index("x"), lax.axis_index("y"), lax.axis_index("c")
    me = 4 * mx + 2 * my + mc
    rows = _seq_rows()
    hg = SSM_HEADS // SSM_GROUPS
    d = D_MODEL

    n_main = D_INNER + D_XBC
    ws_in, ws_up = a_w_in.shape[2], f_w_up.shape[2]
    segs_in = _col_segments(ws_in, [(0, 0, n_main)] + [(n_main + hg * g, n_main + LANE * g, hg)
                                                      for g in range(SSM_GROUPS)])
    segs_up = _col_segments(ws_up, [(0, 0, 2 * D_FF)])
    def gather_of(*ws):
        return _gather_comm([w.astype(BF16) for w in ws])

    small_full, = _run_comm(_gather_comm([_pack([wts[n] for n in SMALL], 0, 8, F32)]), "gather_small")
    full = {}
    for n, g in zip(SMALL, _unpack(small_full, 1, [wts[n].shape for n in SMALL])):
        full[n] = _join8(g, SHARD_AXIS[n])
    (h0, hn0), (g_in,) = _embed_norm(full["meta_tokens"], x[0], full["a_norm_pre"], rows, "embed_norm",
                                     comm=gather_of(a_w_in[0]))
    w_in_all = _assemble_cols(g_in, n_main + SSM_GROUPS * LANE, segs_in, "asm_w_in")
    w_up, w_down = [None, None], [None, None]
    bias_g = _group_lanes(wts["a_dt_bias"], hg)
    alog_g = _group_lanes(wts["a_a_log"], hg)
    dsk_g = _group_lanes(wts["a_d_skip"], hg)
    a_conv_w, a_conv_b = full["a_conv_w"][0], full["a_conv_b"]
    f_cw, f_cb = full["f_conv_w"], wts["f_conv_b"]
    fpre, fpost = wts["f_norm_pre"], wts["f_norm_post"]

    tgt = jnp.pad(loss_target[0], ((N_META, rows - N_META - SEQ), (0, 0)))

    zx, (g_out,) = _mm(hn0, w_in_all, "nn", F32, "mm_in", comm=gather_of(a_w_out[0]))
    w_out = g_out.reshape(D_INNER, d)
    xbc, (g_dn1,) = _conv_silu_fwd(zx, a_conv_w, a_conv_b, "conv_a", comm=gather_of(f_w_down[1]))
    (y_ssd, hst), (g_up0,) = _ssd_fwd(xbc, zx, bias_g, alog_g, dsk_g, "ssd_fwd", comm=gather_of(f_w_up[0]))
    w_up[0] = _assemble_cols(g_up0, 2 * D_FF, segs_up, "asm_w_up0")
    yn, (g_kv, g_q) = _gatenorm_fwd(y_ssd, zx, full["a_gate_norm"], "gatenorm", comm=gather_of(w_kv, b_w_q[0]))
    mix_a, (g_o,) = _mm(yn, w_out, "nn", F32, "mm_out", comm=gather_of(b_w_o[0]))
    w_kvf, w_q, w_o = g_kv.reshape(d, 2 * D_KV), g_q.reshape(d, d), g_o.reshape(d, d)
    h1, (fn0,) = _resid_norm(h0, mix_a, full["a_norm_post"], [fpre[0:1]], "resid_a")

    half = d // 2
    u0, (g_dn0,) = _mm(fn0, w_up[0], "nn", F32, "mm_up0", comm=gather_of(f_w_down[0]))
    w_down = [g_dn0.reshape(D_FF, d), g_dn1.reshape(D_FF, d)]
    act0, (g_up1a,) = _ffn_act_fwd(u0, f_cw[0], f_cb[0:1], "ffn_act0", comm=gather_of(f_w_up[1, :half]))
    ffn0 = _mm(act0, w_down[0], "nn", F32, "mm_down0")
    h2, (kvn, bn) = _resid_norm(h1, ffn0, fpost[0:1], [wts["kv_norm"].reshape(1, d), wts["b_norm_pre"]], "resid_f0")
    kv = _mm(kvn, w_kvf, "nn", F32, "mm_kv")
    q = _mm(bn, w_q, "nn", F32, "mm_q")
    (o, lse), (g_up1b,) = _attn_fwd(q, kv, wts["b_sinks"], "attn_fwd", comm=gather_of(f_w_up[1, half:]))
    w_up[1] = jnp.concatenate([_assemble_cols(g_up1a, 2 * D_FF, segs_up, "asm_w_up1a"),
                               _assemble_cols(g_up1b, 2 * D_FF, segs_up, "asm_w_up1b")], axis=0)
    mix_b = _mm(o, w_o, "nn", F32, "mm_o")
    h3, (fn1,) = _resid_norm(h2, mix_b, wts["b_norm_post"], [fpre[1:2]], "resid_b")
    u1 = _mm(fn1, w_up[1], "nn", F32, "mm_up1")
    act1 = _ffn_act_fwd(u1, f_cw[1], f_cb[1:2], "ffn_act1")
    ffn1 = _mm(act1, w_down[1], "nn", F32, "mm_down1")
    dh4, loss_row = _final_loss(h3, ffn1, fpost[1:2], tgt, "loss")
    loss = lax.psum(loss_row[0, 0], ("x", "y", "c"))

    grads = {}

    core = mc.astype(jnp.int32).reshape(1)

    def carried(res, comm):
        return res if comm is not None else (res, None)

    def ffn_bwd(dh_out, h_in, fn, u, act, ffn, i, c_dact=None, c_dwdown=None):
        dffn, dw_post = _norm_bwd(ffn, fpost[i:i + 1], dh_out, None, BF16, f"nb_fpost{i}")
        dact, got_a = carried(_mm(dffn, w_down[i], "nt", F32, f"mm_dact{i}", comm=c_dact), c_dact)
        dw_down, got_b = carried(_mm(act, dffn, "tn", BF16, f"mm_dwdown{i}", comm=c_dwdown), c_dwdown)
        dw_down = dw_down.reshape(N_DEV, -1, d)
        du, dwc, dbc = _ffn_act_bwd(u, dact, f_cw[i], f_cb[i:i + 1], f"ffn_act_bwd{i}")
        dfn, (s_dn,) = _mm(du, w_up[i], "nt", F32, f"mm_dfn{i}", comm=_swap_comm([dw_down]))
        sum_dn = _add_pairs(dw_down, s_dn, core, f"rs_add_dn{i}")
        dw_up = _scatter_cols(_mm(fn, du, "tn", BF16, f"mm_dwup{i}"), ws_up, segs_up, f"scat_w_up{i}")
        (dh_in, dw_pre), (s_up,) = _norm_bwd(h_in, fpre[i:i + 1], dfn, dh_out, F32, f"nb_fpre{i}",
                                             comm=_swap_comm([dw_up]))
        sum_up = _add_pairs(dw_up, s_up, core, f"rs_add_up{i}")
        return dh_in, dict(post=dw_post, sum_down=sum_dn, cw=jnp.concatenate([dwc[0], dwc[1]], axis=1),
                           cb=jnp.concatenate([dbc[0], dbc[1]], axis=1), sum_up=sum_up, pre=dw_pre), got_a, got_b

    dh3, gf1, _, _ = ffn_bwd(dh4, h3, fn1, u1, act1, ffn1, 1)
    dmix_b, grads["b_norm_post"] = _norm_bwd(mix_b, wts["b_norm_post"], dh3, None, BF16, "nb_bpost")
    do = _mm(dmix_b, w_o, "nt", F32, "mm_do")
    dw_o = _mm(o, dmix_b, "tn", BF16, "mm_dwo").reshape(N_DEV, -1, d)
    (dq, dkv, dsinks), (p_up1, p_dn1, s_o) = _attn_bwd(
        q, kv, wts["b_sinks"], do, lse, "attn_bwd",
        comm=_join_comms([_chips_comm([gf1["sum_up"], gf1["sum_down"]]), _swap_comm([dw_o])]))
    sum_o = _add_pairs(dw_o, s_o, core, "rs_add_o")
    grads["b_sinks"] = dsinks[:, :N_Q_HEADS]
    dbn = _mm(dq, w_q, "nt", F32, "mm_dbn")
    dw_q = _mm(bn, dq, "tn", BF16, "mm_dwq").reshape(N_DEV, -1, d)
    dkv16 = dkv.astype(BF16)
    dkvn = _mm(dkv16, w_kvf, "nt", F32, "mm_dkvn")
    dw_kv = _mm(kvn, dkv16, "tn", BF16, "mm_dwkv").reshape(N_DEV, -1, 2 * D_KV)
    (dh2, grads["b_norm_pre"]), (s_q, s_kv) = _norm_bwd(h2, wts["b_norm_pre"], dbn, dh3, F32, "nb_bpre",
                                                        comm=_swap_comm([dw_q, dw_kv]))
    sum_q, sum_kv = _add_pairs(dw_q, s_q, core, "rs_add_q"), _add_pairs(dw_kv, s_kv, core, "rs_add_kv")
    dh2, dw_kvn = _norm_bwd(h2, wts["kv_norm"].reshape(1, d), dkvn, dh2, F32, "nb_kv")
    grads["kv_norm"] = dw_kvn.reshape(d)
    dh1, gf0, (p_o,), (p_q, p_kv) = ffn_bwd(dh2, h1, fn0, u0, act0, ffn0, 0, c_dact=_chips_comm([sum_o]),
                                            c_dwdown=_chips_comm([sum_q, sum_kv]))
    grads["f_norm_post"] = jnp.concatenate([gf0["post"], gf1["post"]], axis=0)
    grads["f_norm_pre"] = jnp.concatenate([gf0["pre"], gf1["pre"]], axis=0)
    grads["f_conv_w"] = jnp.stack([gf0["cw"], gf1["cw"]])
    grads["f_conv_b"] = jnp.concatenate([gf0["cb"], gf1["cb"]], axis=0)

    dmix_a, grads["a_norm_post"] = _norm_bwd(mix_a, full["a_norm_post"], dh1, None, BF16, "nb_apost")
    dyn = _mm(dmix_a, w_out, "nt", F32, "mm_dyn")
    dw_out = _mm(yn, dmix_a, "tn", BF16, "mm_dwout").reshape(N_DEV, -1, d)
    (dy_ssd, dz, grads["a_gate_norm"]), (s_out,) = _gatenorm_bwd(y_ssd, zx, full["a_gate_norm"], dyn, "gatenorm_bwd",
                                                                 comm=_swap_comm([dw_out]))
    sum_out = _add_pairs(dw_out, s_out, core, "rs_add_out")
    (dxs, dbm, dcm, ddtp, dalog, ddsk, dbias), (p_up0, p_dn0, p_out) = _ssd_bwd(
        xbc, zx, bias_g, alog_g, dsk_g, dy_ssd, hst, "ssd_bwd",
        comm=_chips_comm([gf0["sum_up"], gf0["sum_down"], sum_out]))
    dxbc = jnp.concatenate([dxs, dbm, dcm], axis=1)
    grads["a_a_log"] = _ungroup_lanes(dalog, hg)
    grads["a_d_skip"] = _ungroup_lanes(ddsk, hg)
    grads["a_dt_bias"] = _ungroup_lanes(dbias, hg)
    dpre, dcw, dcb = _conv_silu_bwd(zx, dxbc, a_conv_w, a_conv_b, "conv_a_bwd")
    grads["a_conv_w"], grads["a_conv_b"] = dcw[None], dcb
    dzx = jnp.concatenate([dz, dpre, ddtp.astype(BF16)], axis=1)
    dw_in8 = _scatter_cols(_mm(hn0, dzx, "tn", BF16, "mm_dwin"), ws_in, segs_in, "scat_w_in")
    dhn0, (s_in,) = _mm(dzx, w_in_all, "nt", F32, "mm_dhn0", comm=_swap_comm([dw_in8]))
    sum_in = _add_pairs(dw_in8, s_in, core, "rs_add_in")
    half_in = sum_in.shape[1] // 2
    (dh0, grads["a_norm_pre"]), (p_in_a,) = _norm_bwd(h0, full["a_norm_pre"], dhn0, dh1, F32, "nb_apre",
                                                      comm=_chips_comm([sum_in[:, :half_in]]))
    grad_x = dh0[N_META:N_META + SEQ][None]
    grads["meta_tokens"] = dh0[:N_META]

    small_local = _pack([_split8(grads[n], SHARD_AXIS[n], wts[n].shape[SHARD_AXIS[n]]) for n in SMALL], 1, 8, F32)
    repl_local = _pack([grads[n] for n in REPL], 0, 8, F32)
    n_sr = small_local.shape[1]
    small_vec = jnp.concatenate([small_local.reshape(N_DEV * n_sr, LANE), repl_local], axis=0)
    tail = _join_comms([_chips_comm([sum_in[:, half_in:]]), _gather_comm([small_vec])])
    parts_big = dict(a_w_out=[p_out], w_kv=[p_kv], b_w_q=[p_q], b_w_o=[p_o], f_w_up=[p_up0, p_up1],
                     f_w_down=[p_dn0, p_dn1])

    def flat_f32(dct, names, mult):
        return _pack([dct[n] for n in names], 0, mult, F32)

    def adamw_big(n, comm=None):
        shp3 = (len(parts_big[n]),) + parts_big[n][0].shape[1:]
        res = _adamw(parts_big[n], *[dct[n].reshape(shp3) for dct in (wts, mom, var)], f"adamw_{n}", comm=comm)
        res, got = res if comm is not None else (res, None)
        big_out[n] = [r.reshape(wts[n].shape) for r in res]
        return got

    big_out = {}
    p_in_b, small_all = adamw_big("f_w_up", tail)
    parts_big["a_w_in"] = [p_in_a, p_in_b]
    for n in BIG:
        if n != "f_w_up":
            adamw_big(n)
    mine_small = lax.dynamic_slice_in_dim(small_all, me * n_sr, n_sr, axis=1)
    parts_small = jnp.concatenate([mine_small, small_all[:, N_DEV * n_sr:]], axis=1)
    sm_in = [jnp.concatenate([flat_f32(dct, SMALL, 8), flat_f32(dct, REPL, 8)], axis=0)[None] for dct in (wts, mom, var)]
    small_out = [r[0] for r in _adamw([parts_small], *sm_in, "adamw_small")]

    outs = []
    for kind in range(4):
        res = {n: big_out[n][kind] for n in BIG}
        for n, a in zip(SMALL, _unpack(small_out[kind][:n_sr], 0, [wts[n].shape for n in SMALL])):
            res[n] = a
        for n, a in zip(REPL, _unpack(small_out[kind][n_sr:], 0, [wts[n].shape for n in REPL])):
            res[n] = a
        outs.append(res)
    return (loss, grad_x, *[outs[0][n] for n in WEIGHTS], *[outs[1][n] for n in WEIGHTS],
            *[outs[2][n] for n in WEIGHTS], *[outs[3][n] for n in WEIGHTS])
```

```python
import functools
import math

import jax
import jax.numpy as jnp
from jax import lax
from jax.experimental import pallas as pl
from jax.experimental.pallas import tpu as pltpu

F32, BF16 = jnp.float32, jnp.bfloat16
S = jax.ShapeDtypeStruct

D_MODEL = 1024
SEQ = 2048
N_META = 16
D_INNER = 2048
HEAD_P = 64
SSM_HEADS = D_INNER // HEAD_P
SSM_GROUPS = 4
D_STATE = 128
SSM_CONV = 4
D_BC = SSM_GROUPS * D_STATE
D_XBC = D_INNER + 2 * D_BC
ATTN_DH = 64
N_Q_HEADS = D_MODEL // ATTN_DH
N_KV_HEADS = 4
D_KV = N_KV_HEADS * ATTN_DH
WINDOW = 128
D_FF = 2816
FFN_CONV = 3
RMS_EPS = 1e-6
NEG = -1e30
LR, B1, B2, EPS, WD, STEP = 0.001, 0.9, 0.999, 1e-08, 0.01, 10

N_DEV = 8
T = 128
LANE = 128
VMEM_LIMIT = 48 * 1024 * 1024

BIG = ("a_w_in", "a_w_out", "w_kv", "b_w_q", "b_w_o", "f_w_up", "f_w_down")
SMALL = ("meta_tokens", "a_norm_pre", "a_conv_w", "a_conv_b", "a_gate_norm", "a_norm_post", "f_conv_w")
REPL = ("a_dt_bias", "a_a_log", "a_d_skip", "kv_norm", "b_norm_pre", "b_sinks", "b_norm_post",
        "f_norm_pre", "f_conv_b", "f_norm_post")
SHARD_AXIS = dict(a_w_in=2, a_w_out=1, w_kv=0, b_w_q=1, b_w_o=1, f_w_up=2, f_w_down=1, meta_tokens=1,
                  a_norm_pre=1, a_conv_w=2, a_conv_b=1, a_gate_norm=1, a_norm_post=1, f_conv_w=2)
WEIGHTS = ("meta_tokens", "a_norm_pre", "a_w_in", "a_conv_w", "a_conv_b", "a_dt_bias", "a_a_log", "a_d_skip",
           "a_gate_norm", "a_w_out", "a_norm_post", "kv_norm", "w_kv", "b_norm_pre", "b_w_q", "b_sinks", "b_w_o",
           "b_norm_post", "f_norm_pre", "f_w_up", "f_conv_w", "f_conv_b", "f_w_down", "f_norm_post")


def _seq_rows():
    return -(-(N_META + SEQ) // T) * T


def _cp(sem=None):
    return pltpu.CompilerParams(dimension_semantics=sem, vmem_limit_bytes=VMEM_LIMIT)


def _pick(n, target):
    t = min(n, target)
    t -= t % LANE
    while n % t:
        t -= LANE
    return t


def _sigmoid(x):
    return 0.5 * jnp.tanh(0.5 * x) + 0.5


def _softplus(x):
    return jnp.maximum(x, 0.0) + jnp.log(1.0 + jnp.exp(-jnp.abs(x)))


_NN = (((1,), (0,)), ((), ()))
_NT = (((1,), (1,)), ((), ()))
_TN = (((0,), (0,)), ((), ()))


def _dot(a, b, dims=_NN):
    return lax.dot_general(a, b, dims, preferred_element_type=F32)


def _dot_hi(a, b):
    return lax.dot_general(a, b, _NN, precision=lax.Precision.HIGHEST, preferred_element_type=F32)


_HBM = pl.BlockSpec(memory_space=pltpu.HBM)
_MESH = pl.DeviceIdType.MESH


class _Comm:
    def __init__(self, ins, out_shapes, scratch, first, last):
        self.ins, self.out_shapes, self.scratch, self.first, self.last = ins, out_shapes, scratch, first, last


def _call(body, name, out_shape, grid, in_specs, out_specs, sem, args, scratch=(), comm=None):
    if comm is None:
        return pl.pallas_call(body, name=name, out_shape=out_shape, grid=grid, in_specs=in_specs, out_specs=out_specs,
                              scratch_shapes=list(scratch), compiler_params=_cp(sem))(*args)
    single = not isinstance(out_shape, (list, tuple))
    outs = [out_shape] if single else list(out_shape)
    ospecs = [out_specs] if single else list(out_specs)
    n_in, n_out, n_scr, ci, co = len(in_specs), len(outs), len(scratch), len(comm.ins), len(comm.out_shapes)

    def carrier(*refs):
        p = 0
        parts = []
        for cnt in (n_in, ci, n_out, co, n_scr, len(comm.scratch)):
            parts.append(refs[p:p + cnt])
            p += cnt
        ins, cins, outs_r, couts, scr, cscr = parts
        ids = [pl.program_id(i) for i in range(len(grid))]
        first, last = ids[0] == 0, ids[0] == grid[0] - 1
        for i in range(1, len(grid)):
            first, last = first & (ids[i] == 0), last & (ids[i] == grid[i] - 1)

        @pl.when(first)
        def _():
            comm.first(cins, couts, cscr)

        body(*ins, *outs_r, *scr)

        @pl.when(last)
        def _():
            comm.last(cins, couts, cscr)

    res = pl.pallas_call(
        carrier, name=name, out_shape=outs + list(comm.out_shapes), grid=grid,
        in_specs=list(in_specs) + [_HBM] * ci, out_specs=ospecs + [_HBM] * co,
        scratch_shapes=list(scratch) + list(comm.scratch),
        compiler_params=_cp(("arbitrary",) * len(grid)))(*args, *comm.ins)
    mine = res[0] if single else list(res[:n_out])
    return mine, list(res[n_out:])


def _mm(a, b, mode, out_dtype, name, comm=None):
    if mode == "tn":
        m, kk = a.shape
        planes, width = (b.shape[0], b.shape[2]) if b.ndim == 3 else (1, b.shape[1])
        n = planes * width
        tko, tn = _pick(kk, 512), _pick(width, 1536)
        per = width // tn

        def body(a_ref, b_ref, o_ref):
            o_ref[...] = _dot(a_ref[...], b_ref[...], _TN).astype(o_ref.dtype)

        b_spec = (pl.BlockSpec((None, m, tn), lambda i, j: (j // per, 0, j % per)) if b.ndim == 3
                  else pl.BlockSpec((m, tn), lambda i, j: (0, j)))
        return _call(
            body, name, S((kk, n), out_dtype), (kk // tko, n // tn), [pl.BlockSpec((m, tko), lambda i, j: (0, i)), b_spec],
            pl.BlockSpec((tko, tn), lambda i, j: (i, j)), ("parallel", "parallel"), (a, b), comm=comm)

    planes, width = (a.shape[0], a.shape[2]) if a.ndim == 3 else (1, a.shape[1])
    m, kk = a.shape[-2], planes * width
    n = b.shape[1] if mode == "nn" else b.shape[0]
    tn = _pick(n, 512)
    tk = kk if kk <= 2048 else _pick(width, 1536)
    nk = kk // tk
    per = width // tk
    dims = _NN if mode == "nn" else _NT

    def body(a_ref, b_ref, o_ref, *acc):
        part = _dot(a_ref[...], b_ref[...], dims)
        if nk == 1:
            o_ref[...] = part.astype(o_ref.dtype)
        else:
            k = pl.program_id(1)

            @pl.when(k == 0)
            def _():
                acc[0][...] = part

            @pl.when(k > 0)
            def _():
                acc[0][...] += part

            @pl.when(k == nk - 1)
            def _():
                o_ref[...] = acc[0][...].astype(o_ref.dtype)

    b_spec = (pl.BlockSpec((tk, tn), lambda j, k: (k, j)) if mode == "nn"
              else pl.BlockSpec((tn, tk), lambda j, k: (j, k)))
    a_spec = (pl.BlockSpec((None, m, tk), lambda j, k: (k // per, 0, k % per)) if a.ndim == 3
              else pl.BlockSpec((m, tk), lambda j, k: (0, k)))
    return _call(
        body, name, S((m, n), out_dtype), (n // tn, nk), [a_spec, b_spec],
        pl.BlockSpec((m, tn), lambda j, k: (0, j)), ("parallel", "arbitrary"), (a, b),
        scratch=[pltpu.VMEM((m, tn), F32)] if nk > 1 else [], comm=comm)


def _rms(x, w):
    return x * lax.rsqrt(jnp.mean(x * x, axis=-1, keepdims=True) + RMS_EPS) * w


def _row_tile(rows):
    return rows // 8


def _embed_norm(meta, x, w, rows, name, comm=None):
    n_meta, d = meta.shape
    n_x = x.shape[0]
    last = rows // T - 1
    assert n_meta % 8 == 0 and n_meta < T and n_meta + n_x == last * T + n_meta and last * T >= n_x

    def body(m_ref, x_ref, w_ref, h_ref, hn_ref):
        i = pl.program_id(0)

        @pl.when(i == 0)
        def _():
            h_ref[0:n_meta, :] = m_ref[...]
            h_ref[n_meta:T, :] = x_ref[0:T - n_meta, :]

        @pl.when((i > 0) & (i < last))
        def _():
            h_ref[...] = x_ref[pl.ds(pl.multiple_of(i * T - n_meta, 8), T), :]

        @pl.when(i == last)
        def _():
            h_ref[0:n_meta, :] = x_ref[n_x - n_meta:n_x, :]
            h_ref[n_meta:T, :] = jnp.zeros((T - n_meta, d), F32)

        hn_ref[...] = _rms(h_ref[...], w_ref[...]).astype(hn_ref.dtype)

    row = pl.BlockSpec((T, d), lambda i: (i, 0))
    return _call(body, name, [S((rows, d), F32), S((rows, d), BF16)], (rows // T,),
                 [pl.BlockSpec((n_meta, d), lambda i: (0, 0)), pl.BlockSpec((n_x, d), lambda i: (0, 0)),
                  pl.BlockSpec((1, d), lambda i: (0, 0))], [row, row], ("parallel",), (meta, x, w), comm=comm)


def _resid_norm(h, br, w_post, next_ws, name):
    rows, d = h.shape
    tr = _row_tile(rows)
    has_br = br is not None
    nw = len(next_ws)

    def body(*refs):
        h_ref = refs[0]
        pos = 1
        x = h_ref[...]
        if has_br:
            x = x + _rms(refs[1][...], refs[2][...])
            pos = 3
        w_refs = refs[pos:pos + nw]
        outs = refs[pos + nw:]
        if has_br:
            outs[0][...] = x
            outs = outs[1:]
        for w_ref, o_ref in zip(w_refs, outs):
            o_ref[...] = _rms(x, w_ref[...]).astype(o_ref.dtype)

    row = pl.BlockSpec((tr, d), lambda i: (i, 0))
    vec = pl.BlockSpec((1, d), lambda i: (0, 0))
    ins = [h] + ([br, w_post] if has_br else []) + list(next_ws)
    in_specs = [row] + ([row, vec] if has_br else []) + [vec] * nw
    out_shape = ([S((rows, d), F32)] if has_br else []) + [S((rows, d), BF16)] * nw
    res = pl.pallas_call(body, name=name, out_shape=out_shape, grid=(rows // tr,), in_specs=in_specs,
                         out_specs=[row] * len(out_shape), compiler_params=_cp(("parallel",)))(*ins)
    if has_br:
        return res[0], list(res[1:])
    return h, list(res)


def _norm_bwd(x, w, dy, add, out_dtype, name, comm=None):
    rows, d = x.shape
    tr = _row_tile(rows)
    has_add = add is not None

    def body(*refs):
        x_ref, w_ref, dy_ref = refs[:3]
        dx_ref, dw_ref = refs[-2:]
        xv = x_ref[...]
        r = lax.rsqrt(jnp.mean(xv * xv, axis=-1, keepdims=True) + RMS_EPS)
        dyv = dy_ref[...].astype(F32)
        wdy = dyv * w_ref[...]
        dx = r * wdy - xv * (r * r * r) * jnp.mean(xv * wdy, axis=-1, keepdims=True)
        if has_add:
            dx = dx + refs[3][...]
        dx_ref[...] = dx.astype(dx_ref.dtype)

        @pl.when(pl.program_id(0) == 0)
        def _():
            dw_ref[...] = jnp.zeros_like(dw_ref)

        dw_ref[...] += jnp.sum(dyv * xv * r, axis=0, keepdims=True)

    row = pl.BlockSpec((tr, d), lambda i: (i, 0))
    vec = pl.BlockSpec((1, d), lambda i: (0, 0))
    ins = [x, w, dy] + ([add] if has_add else [])
    return _call(body, name, [S((rows, d), out_dtype), S((1, d), F32)], (rows // tr,),
                 [row, vec, row] + ([row] if has_add else []), [row, vec], ("arbitrary",), ins, comm=comm)


def _final_loss(h, br, w_post, tgt, name):
    rows, d = h.shape
    tr = _row_tile(rows)

    def body(h_ref, br_ref, w_ref, t_ref, dh_ref, loss_ref):
        i = pl.program_id(0)
        y = h_ref[...] + _rms(br_ref[...], w_ref[...])
        r = i * tr + lax.broadcasted_iota(jnp.int32, (tr, 1), 0)
        real = (r >= N_META) & (r < N_META + SEQ)
        diff = jnp.where(real, y - t_ref[...], 0.0)
        dh_ref[...] = diff * (1.0 / d)

        @pl.when(i == 0)
        def _():
            loss_ref[...] = jnp.zeros_like(loss_ref)

        loss_ref[...] += jnp.sum(diff * diff) * (0.5 / d)

    row = pl.BlockSpec((tr, d), lambda i: (i, 0))
    return pl.pallas_call(body, name=name, out_shape=[S((rows, d), F32), S((1, LANE), F32)], grid=(rows // tr,),
                          in_specs=[row, row, pl.BlockSpec((1, d), lambda i: (0, 0)), row],
                          out_specs=[row, pl.BlockSpec((1, LANE), lambda i: (0, 0))],
                          compiler_params=_cp(("arbitrary",)))(h, br, w_post, tgt)


def _gatenorm_fwd(y, zx, w, name, comm=None):
    rows, d = y.shape
    tr = _row_tile(rows)

    def body(y_ref, z_ref, w_ref, o_ref):
        z = z_ref[...]
        o_ref[...] = _rms(y_ref[...] * z * _sigmoid(z), w_ref[...]).astype(o_ref.dtype)

    row = pl.BlockSpec((tr, d), lambda i: (i, 0))
    return _call(body, name, S((rows, d), BF16), (rows // tr,), [row, row, pl.BlockSpec((1, d), lambda i: (0, 0))],
                 row, ("parallel",), (y, zx, w), comm=comm)


def _gatenorm_bwd(y, zx, w, dyn, name, comm=None):
    rows, d = y.shape
    tr = _row_tile(rows)

    def body(y_ref, z_ref, w_ref, dyn_ref, dy_ref, dz_ref, dw_ref):
        yv, z = y_ref[...], z_ref[...]
        sg = _sigmoid(z)
        sz = z * sg
        g = yv * sz
        r = lax.rsqrt(jnp.mean(g * g, axis=-1, keepdims=True) + RMS_EPS)
        dyn_v = dyn_ref[...]
        wdy = dyn_v * w_ref[...]
        dg = r * wdy - g * (r * r * r) * jnp.mean(g * wdy, axis=-1, keepdims=True)
        dy_ref[...] = dg * sz
        dz_ref[...] = (dg * yv * sg * (1.0 + z * (1.0 - sg))).astype(dz_ref.dtype)

        @pl.when(pl.program_id(0) == 0)
        def _():
            dw_ref[...] = jnp.zeros_like(dw_ref)

        dw_ref[...] += jnp.sum(dyn_v * g * r, axis=0, keepdims=True)

    row = pl.BlockSpec((tr, d), lambda i: (i, 0))
    vec = pl.BlockSpec((1, d), lambda i: (0, 0))
    return _call(body, name, [S((rows, d), F32), S((rows, d), BF16), S((1, d), F32)], (rows // tr,),
                 [row, row, vec, row], [row, row, vec], ("arbitrary",), (y, zx, w, dyn), comm=comm)


def _shift_down(x, s, rows_iota):
    if s == 0:
        return x
    return jnp.where(rows_iota >= s, pltpu.roll(x, s, 0), 0.0)


def _shift_up(x, s, rows_iota):
    if s == 0:
        return x
    rows = x.shape[0]
    return jnp.where(rows_iota < rows - s, pltpu.roll(x, rows - s, 0), 0.0)


def _r16(v):
    return v.astype(BF16).astype(F32)


def _conv(x, w_ref, b_ref, taps, rows_iota):
    x = _r16(x)
    acc = jnp.zeros_like(x)
    for k in range(taps):
        acc = acc + _r16(w_ref[k:k + 1, :]) * _shift_down(x, taps - 1 - k, rows_iota)
    return acc + b_ref[...]


def _conv_bwd(x, du, w_ref, dw_ref, db_ref, taps, rows_iota):
    db_ref[...] = jnp.sum(du, axis=0, keepdims=True)
    x, du = _r16(x), _r16(du)
    dx = jnp.zeros_like(x)
    for k in range(taps):
        s = taps - 1 - k
        dx = dx + _r16(w_ref[k:k + 1, :]) * _shift_up(du, s, rows_iota)
        dw_ref[k:k + 1, :] = jnp.sum(du * _shift_down(x, s, rows_iota), axis=0, keepdims=True)
    return dx


def _conv_silu_fwd(zx, w, b, name, comm=None):
    rows = zx.shape[0]
    cb = 512
    off = D_INNER // cb

    def body(x_ref, w_ref, b_ref, o_ref):
        it = lax.broadcasted_iota(jnp.int32, (rows, 1), 0)
        u = _conv(x_ref[...], w_ref, b_ref, SSM_CONV, it)
        o_ref[...] = u * _sigmoid(u)

    return _call(
        body, name, S((rows, D_XBC), F32), (D_XBC // cb,),
        [pl.BlockSpec((rows, cb), lambda j: (0, off + j)), pl.BlockSpec((SSM_CONV, cb), lambda j: (0, j)),
         pl.BlockSpec((1, cb), lambda j: (0, j))],
        pl.BlockSpec((rows, cb), lambda j: (0, j)), ("parallel",), (zx, w, b), comm=comm)


def _conv_silu_bwd(zx, dxbc, w, b, name):
    rows = zx.shape[0]
    cb = 512
    off = D_INNER // cb

    def body(x_ref, d_ref, w_ref, b_ref, dx_ref, dw_ref, db_ref):
        it = lax.broadcasted_iota(jnp.int32, (rows, 1), 0)
        x = x_ref[...]
        u = _conv(x, w_ref, b_ref, SSM_CONV, it)
        sg = _sigmoid(u)
        du = d_ref[...] * sg * (1.0 + u * (1.0 - sg))
        dx_ref[...] = _conv_bwd(x, du, w_ref, dw_ref, db_ref, SSM_CONV, it).astype(dx_ref.dtype)

    col = pl.BlockSpec((rows, cb), lambda j: (0, j))
    wsp = pl.BlockSpec((SSM_CONV, cb), lambda j: (0, j))
    bsp = pl.BlockSpec((1, cb), lambda j: (0, j))
    return pl.pallas_call(
        body, name=name, out_shape=[S((rows, D_XBC), BF16), S((SSM_CONV, D_XBC), F32), S((1, D_XBC), F32)],
        grid=(D_XBC // cb,), in_specs=[pl.BlockSpec((rows, cb), lambda j: (0, off + j)), col, wsp, bsp],
        out_specs=[col, wsp, bsp], compiler_params=_cp(("parallel",)))(zx, dxbc, w, b)


def _ffn_act_fwd(u, w, b, name, comm=None):
    rows = u.shape[0]
    cb = 256
    nb = D_FF // cb

    def body(g_ref, v_ref, wg_ref, wv_ref, bg_ref, bv_ref, o_ref):
        it = lax.broadcasted_iota(jnp.int32, (rows, 1), 0)
        g = _conv(g_ref[...], wg_ref, bg_ref, FFN_CONV, it)
        v = _conv(v_ref[...], wv_ref, bv_ref, FFN_CONV, it)
        o_ref[...] = (g * _sigmoid(g) * v).astype(o_ref.dtype)

    def sp(r, shift):
        return pl.BlockSpec((r, cb), lambda j: (0, shift + j))

    return _call(
        body, name, S((rows, D_FF), BF16), (nb,),
        [sp(rows, 0), sp(rows, nb), sp(FFN_CONV, 0), sp(FFN_CONV, nb), sp(1, 0), sp(1, nb)],
        sp(rows, 0), ("parallel",), (u, u, w, w, b, b), comm=comm)


def _ffn_act_bwd(u, dact, w, b, name, comm=None):
    rows = u.shape[0]
    cb = 256
    nb = D_FF // cb

    def body(g_ref, v_ref, d_ref, wg_ref, wv_ref, bg_ref, bv_ref, du_ref, dw_ref, db_ref):
        it = lax.broadcasted_iota(jnp.int32, (rows, 1), 0)
        xg, xv = g_ref[...], v_ref[...]
        g = _conv(xg, wg_ref, bg_ref, FFN_CONV, it)
        v = _conv(xv, wv_ref, bv_ref, FFN_CONV, it)
        sg = _sigmoid(g)
        d = d_ref[...]
        dgate = d * v * sg * (1.0 + g * (1.0 - sg))
        dval = d * g * sg
        du_ref[0] = _conv_bwd(xg, dgate, wg_ref, dw_ref.at[0], db_ref.at[0], FFN_CONV, it).astype(du_ref.dtype)
        du_ref[1] = _conv_bwd(xv, dval, wv_ref, dw_ref.at[1], db_ref.at[1], FFN_CONV, it).astype(du_ref.dtype)

    def sp(r, shift):
        return pl.BlockSpec((r, cb), lambda j: (0, shift + j))

    def both(r):
        return pl.BlockSpec((2, r, cb), lambda j: (0, 0, j))

    return _call(
        body, name, [S((2, rows, D_FF), BF16), S((2, FFN_CONV, D_FF), F32), S((2, 1, D_FF), F32)], (nb,),
        [sp(rows, 0), sp(rows, nb), sp(rows, 0), sp(FFN_CONV, 0), sp(FFN_CONV, nb), sp(1, 0), sp(1, nb)],
        [both(rows), both(FFN_CONV), both(1)], ("parallel",), (u, u, dact, w, w, b, b), comm=comm)


def _ssd_consts(dtp_ref, bias_ref, alog_ref, hg):
    lane = lax.broadcasted_iota(jnp.int32, (1, LANE), 1)
    pre = dtp_ref[...] + bias_ref[...]
    dt = _softplus(pre)
    a_row = jnp.where(lane < hg, -jnp.exp(alog_ref[...]), 0.0)
    ri = lax.broadcasted_iota(jnp.int32, (T, T), 0)
    ci = lax.broadcasted_iota(jnp.int32, (T, T), 1)
    cs = _dot_hi((ri >= ci).astype(F32), dt * a_row)
    return pre, dt, a_row, cs, ri, ci, lane


def _ssd_fwd(xbc, zx, bias, alog, dsk, name, comm=None):
    rows = xbc.shape[0]
    nc = rows // T
    hg = SSM_HEADS // SSM_GROUPS
    gw = hg * HEAD_P
    xoff, boff, coff = 0, D_INNER // D_STATE, (D_INNER + D_BC) // D_STATE
    dtoff = (D_INNER + D_XBC) // LANE

    def body(x_ref, b_ref, c_ref, dtp_ref, bias_ref, alog_ref, dsk_ref, y_ref, hst_ref, hs):
        c = pl.program_id(1)

        @pl.when(c == 0)
        def _():
            hs[...] = jnp.zeros_like(hs)

        _, dt, _, cs, ri, ci, _ = _ssd_consts(dtp_ref, bias_ref, alog_ref, hg)
        cst, dtt = cs.T, dt.T
        xt = x_ref[...].T
        bb, cbf = b_ref[...].astype(BF16), c_ref[...].astype(BF16)
        gt = _dot(bb, cbf, _NT)
        causal_t = ci >= ri
        dskv = dsk_ref[...]
        hall = hs[...]
        hst_ref[0, 0] = hall
        yts, new_h = [], []
        for k in range(hg):
            sl = slice(k * HEAD_P, (k + 1) * HEAD_P)
            csc, csr = cs[:, k:k + 1], cst[k:k + 1, :]
            lt = jnp.exp(jnp.where(causal_t, csr - csc, NEG))
            xk = xt[sl, :]
            xdt = xk * dtt[k:k + 1, :]
            hk = hall[sl, :]
            yd = _dot(xdt.astype(BF16), (gt * lt).astype(BF16))
            yo = jnp.exp(csr) * _dot(hk.astype(BF16), cbf, _NT)
            yts.append(yd + yo + dskv[:, k:k + 1] * xk)
            cl = cs[T - 1:T, k:k + 1]
            st = _dot((xdt * jnp.exp(cl - csr)).astype(BF16), bb)
            new_h.append(jnp.exp(cl) * hk + st)
        y_ref[...] = jnp.concatenate(yts, axis=0).T
        hs[...] = jnp.concatenate(new_h, axis=0)

    vec = pl.BlockSpec((1, LANE), lambda g, c: (0, g))
    return _call(
        body, name, [S((rows, D_INNER), F32), S((nc, SSM_GROUPS, gw, D_STATE), F32)], (SSM_GROUPS, nc),
        [pl.BlockSpec((T, gw), lambda g, c: (c, xoff + g)),
         pl.BlockSpec((T, D_STATE), lambda g, c: (c, boff + g)),
         pl.BlockSpec((T, D_STATE), lambda g, c: (c, coff + g)),
         pl.BlockSpec((T, LANE), lambda g, c: (c, dtoff + g)), vec, vec, vec],
        [pl.BlockSpec((T, gw), lambda g, c: (c, g)), pl.BlockSpec((1, 1, gw, D_STATE), lambda g, c: (c, g, 0, 0))],
        ("parallel", "arbitrary"), (xbc, xbc, xbc, zx, bias, alog, dsk),
        scratch=[pltpu.VMEM((gw, D_STATE), F32)], comm=comm)


def _ssd_bwd(xbc, zx, bias, alog, dsk, dy, hst, name, comm=None):
    rows = xbc.shape[0]
    nc = rows // T
    hg = SSM_HEADS // SSM_GROUPS
    gw = hg * HEAD_P
    boff, coff = D_INNER // D_STATE, (D_INNER + D_BC) // D_STATE
    dtoff = (D_INNER + D_XBC) // LANE

    def body(x_ref, b_ref, c_ref, dtp_ref, bias_ref, alog_ref, dsk_ref, dy_ref, hst_ref,
             dx_ref, db_ref, dc_ref, ddtp_ref, dalog_ref, ddsk_ref, dbias_ref, dhs):
        step = pl.program_id(1)

        @pl.when(step == 0)
        def _():
            dhs[...] = jnp.zeros_like(dhs)
            dalog_ref[...] = jnp.zeros_like(dalog_ref)
            ddsk_ref[...] = jnp.zeros_like(ddsk_ref)
            dbias_ref[...] = jnp.zeros_like(dbias_ref)

        pre, dt, a_row, cs, ri, ci, lane = _ssd_consts(dtp_ref, bias_ref, alog_ref, hg)
        cst, dtt = cs.T, dt.T
        xt, dyt = x_ref[...].T, dy_ref[...].T
        bb, cbf = b_ref[...].astype(BF16), c_ref[...].astype(BF16)
        gt = _dot(bb, cbf, _NT)
        causal_t = ci >= ri
        dskv = dsk_ref[...]
        hall, dhall = hst_ref[0, 0], dhs[...]
        head_row = lax.broadcasted_iota(jnp.int32, (T, 1), 0)
        last_l = lax.broadcasted_iota(jnp.int32, (1, T), 1) == T - 1
        dgt = jnp.zeros((T, T), F32)
        dc_acc = jnp.zeros((T, D_STATE), F32)
        db_acc = jnp.zeros((T, D_STATE), F32)
        ddt_rows = jnp.zeros((T, T), F32)
        dcs_rows = jnp.zeros((T, T), F32)
        qrow_cols = jnp.zeros((T, LANE), F32)
        ddsk_acc = jnp.zeros((1, LANE), F32)
        dxts, new_dh = [], []
        for k in range(hg):
            sl = slice(k * HEAD_P, (k + 1) * HEAD_P)
            csc, csr = cs[:, k:k + 1], cst[k:k + 1, :]
            lt = jnp.exp(jnp.where(causal_t, csr - csc, NEG))
            xk, dyk = xt[sl, :], dyt[sl, :]
            dtr, dk = dtt[k:k + 1, :], dskv[:, k:k + 1]
            xdt = xk * dtr
            mpt = gt * lt
            dyb = dyk.astype(BF16)
            dxdt = _dot(dyb, mpt.astype(BF16), _NT)
            dmt = _dot(xdt.astype(BF16), dyb, _TN)
            dgt = dgt + dmt * lt
            q = dmt * mpt
            q_rows = jnp.sum(q, axis=1, keepdims=True)
            q_cols = jnp.sum(q, axis=0, keepdims=True)
            hk, dhn = hall[sl, :], dhall[sl, :]
            e = jnp.exp(csr)
            cl = cs[T - 1:T, k:k + 1]
            wdec = jnp.exp(cl - csr)
            w = wdec * dtr
            rt = _dot(dhn.astype(BF16), bb, _NT)
            dxts.append(dtr * dxdt + dk * dyk + rt * w)
            xz = jnp.sum(xk * dxdt, axis=0, keepdims=True)
            dw = jnp.sum(rt * xk, axis=0, keepdims=True)
            dcl = jnp.exp(cl) * jnp.sum(dhn * hk) + jnp.sum(dw * w)
            yo = e * _dot(hk.astype(BF16), cbf, _NT)
            dcs_r = jnp.sum(dyk * yo, axis=0, keepdims=True) + q_cols - dw * w + jnp.where(last_l, dcl, 0.0)
            dye = (dyk * e).astype(BF16)
            dc_acc = dc_acc + _dot(dye, hk.astype(BF16), _TN)
            db_acc = db_acc + _dot((xk * w).astype(BF16), dhn.astype(BF16), _TN)
            new_dh.append(jnp.exp(cl) * dhn + _dot(dye, cbf))
            onehot = (lane == k).astype(F32)
            ddt_rows = ddt_rows + jnp.where(head_row == k, xz + dw * wdec, 0.0)
            dcs_rows = dcs_rows + jnp.where(head_row == k, dcs_r, 0.0)
            qrow_cols = qrow_cols + q_rows * onehot
            ddsk_acc = ddsk_acc + jnp.sum(dyk * xk) * onehot
        dx_ref[...] = jnp.concatenate(dxts, axis=0).T
        dhs[...] = jnp.concatenate(new_dh, axis=0)
        dc_ref[...] = _dot(dgt.T.astype(BF16), bb) + dc_acc
        db_ref[...] = _dot(dgt.astype(BF16), cbf) + db_acc
        da = _dot_hi((ci >= ri).astype(F32), dcs_rows.T - qrow_cols)
        ddtp = (ddt_rows.T + da * a_row) * _sigmoid(pre)
        ddtp = jnp.where(lane < hg, ddtp, 0.0)
        ddtp_ref[...] = ddtp
        dbias_ref[...] += jnp.sum(ddtp, axis=0, keepdims=True)
        dalog_ref[...] += jnp.sum(da * dt, axis=0, keepdims=True) * a_row
        ddsk_ref[...] += ddsk_acc

    def rc(c):
        return nc - 1 - c

    vec = pl.BlockSpec((1, LANE), lambda g, c: (0, g))
    xsp = pl.BlockSpec((T, gw), lambda g, c: (rc(c), g))
    return _call(
        body, name,
        [S((rows, D_INNER), F32), S((rows, D_BC), F32), S((rows, D_BC), F32),
         S((rows, SSM_GROUPS * LANE), F32), S((1, SSM_GROUPS * LANE), F32),
         S((1, SSM_GROUPS * LANE), F32), S((1, SSM_GROUPS * LANE), F32)],
        (SSM_GROUPS, nc),
        [xsp,
         pl.BlockSpec((T, D_STATE), lambda g, c: (rc(c), boff + g)),
         pl.BlockSpec((T, D_STATE), lambda g, c: (rc(c), coff + g)),
         pl.BlockSpec((T, LANE), lambda g, c: (rc(c), dtoff + g)), vec, vec, vec,
         xsp, pl.BlockSpec((1, 1, gw, D_STATE), lambda g, c: (rc(c), g, 0, 0))],
        [xsp,
         pl.BlockSpec((T, D_STATE), lambda g, c: (rc(c), g)),
         pl.BlockSpec((T, D_STATE), lambda g, c: (rc(c), g)),
         pl.BlockSpec((T, LANE), lambda g, c: (rc(c), g)), vec, vec, vec],
        ("parallel", "arbitrary"), (xbc, xbc, xbc, zx, bias, alog, dsk, dy, hst),
        scratch=[pltpu.VMEM((gw, D_STATE), F32)], comm=comm)


def _attn_tiles(kv_ref, j):
    prev = jnp.maximum(j - 1, 0)
    meta = kv_ref[0:T, :]
    prv = kv_ref[pl.ds(pl.multiple_of(prev * T, T), T), :]
    cur = kv_ref[pl.ds(pl.multiple_of(j * T, T), T), :]
    return jnp.concatenate([meta, prv, cur], axis=0)


def _attn_mask(j):
    r = j * T + lax.broadcasted_iota(jnp.int32, (3 * T, T), 1)
    row = lax.broadcasted_iota(jnp.int32, (3 * T, T), 0)
    t0, t1 = row < T, row < 2 * T
    s = jnp.where(t0, row, (j - 2) * T + row)
    ok = (s <= r) & ((s < N_META) | (s > r - WINDOW))
    use = (t0 & (j >= 2) & (row < N_META)) | (jnp.logical_not(t0) & t1 & (j >= 1)) | jnp.logical_not(t1)
    return ok & use


def _attn_fwd(q, kv, sinks, name, comm=None):
    rows = q.shape[0]
    scale = 1.0 / math.sqrt(ATTN_DH)
    qpk = N_Q_HEADS // N_KV_HEADS

    def body(q_ref, kv_ref, s_ref, o_ref, lse_ref):
        j = pl.program_id(0)
        kv3 = _attn_tiles(kv_ref, j).astype(BF16)
        mask = _attn_mask(j)
        qv = (q_ref[...] * scale).astype(BF16)
        sk = s_ref[...]
        lses = []
        for kh in range(N_KV_HEADS):
            k3 = kv3[:, kh * ATTN_DH:(kh + 1) * ATTN_DH]
            v3 = kv3[:, D_KV + kh * ATTN_DH:D_KV + (kh + 1) * ATTN_DH]
            for g in range(qpk):
                h = kh * qpk + g
                sink = sk[:, h:h + 1]
                sc = jnp.where(mask, _dot(k3, qv[:, h * ATTN_DH:(h + 1) * ATTN_DH], _NT), NEG)
                m = jnp.maximum(jnp.max(sc, axis=0, keepdims=True), sink)
                p = jnp.exp(sc - m)
                den = jnp.sum(p, axis=0, keepdims=True) + jnp.exp(sink - m)
                p = p * (1.0 / den)
                lses.append(m + jnp.log(den))
                o_ref[:, h * ATTN_DH:(h + 1) * ATTN_DH] = _dot(p.astype(BF16), v3, _TN).astype(o_ref.dtype)
        lse_ref[...] = jnp.concatenate(lses, axis=0)

    return _call(
        body, name, [S((rows, D_MODEL), BF16), S((N_Q_HEADS, rows), F32)], (rows // T,),
        [pl.BlockSpec((T, D_MODEL), lambda j: (j, 0)), pl.BlockSpec((rows, 2 * D_KV), lambda j: (0, 0)),
         pl.BlockSpec((1, N_Q_HEADS), lambda j: (0, 0))],
        [pl.BlockSpec((T, D_MODEL), lambda j: (j, 0)), pl.BlockSpec((N_Q_HEADS, T), lambda j: (0, j))],
        ("parallel",), (q, kv, sinks), comm=comm)


def _attn_bwd(q, kv, sinks, do, lse, name, comm=None):
    rows = q.shape[0]
    scale = 1.0 / math.sqrt(ATTN_DH)
    qpk = N_Q_HEADS // N_KV_HEADS

    def body(q_ref, kv_ref, s_ref, do_ref, lse_ref, dq_ref, dkv_ref, ds_ref):
        j = pl.program_id(0)

        @pl.when(j == 0)
        def _():
            dkv_ref[...] = jnp.zeros_like(dkv_ref)
            ds_ref[...] = jnp.zeros_like(ds_ref)

        kv3 = _attn_tiles(kv_ref, j).astype(BF16)
        mask = _attn_mask(j)
        qv = (q_ref[...] * scale).astype(BF16)
        dov = do_ref[...].astype(BF16)
        sk = s_ref[...]
        lsev = lse_ref[...]
        lane = lax.broadcasted_iota(jnp.int32, (1, LANE), 1)
        ds_acc = jnp.zeros((1, LANE), F32)
        prev = jnp.maximum(j - 1, 0)
        dqts = []
        for kh in range(N_KV_HEADS):
            ksl = slice(kh * ATTN_DH, (kh + 1) * ATTN_DH)
            vsl = slice(D_KV + kh * ATTN_DH, D_KV + (kh + 1) * ATTN_DH)
            k3, v3 = kv3[:, ksl], kv3[:, vsl]
            k3t = k3.T
            dk3 = jnp.zeros((3 * T, ATTN_DH), F32)
            dv3 = jnp.zeros((3 * T, ATTN_DH), F32)
            for g in range(qpk):
                h = kh * qpk + g
                hs = slice(h * ATTN_DH, (h + 1) * ATTN_DH)
                qh, doh = qv[:, hs], dov[:, hs]
                lh = lsev[h:h + 1, :]
                p = jnp.exp(jnp.where(mask, _dot(k3, qh, _NT), NEG) - lh)
                ps = jnp.exp(sk[:, h:h + 1] - lh)
                dp = _dot(v3, doh, _NT)
                delta = jnp.sum(p * dp, axis=0, keepdims=True)
                dsc = (p * (dp - delta)).astype(BF16)
                dqts.append(_dot(k3t, dsc) * scale)
                dk3 = dk3 + _dot(dsc, qh)
                dv3 = dv3 + _dot(p.astype(BF16), doh)
                ds_acc = ds_acc - jnp.sum(ps * delta) * (lane == h).astype(F32)
            for t, start in enumerate((0, pl.multiple_of(prev * T, T), pl.multiple_of(j * T, T))):
                rsl = pl.ds(start, T)
                dkv_ref[rsl, ksl] += dk3[t * T:(t + 1) * T, :]
                dkv_ref[rsl, vsl] += dv3[t * T:(t + 1) * T, :]
        ds_ref[...] += ds_acc
        dq_ref[...] = jnp.concatenate(dqts, axis=0).T.astype(dq_ref.dtype)

    blk = pl.BlockSpec((T, D_MODEL), lambda j: (j, 0))
    full = pl.BlockSpec((rows, 2 * D_KV), lambda j: (0, 0))
    return _call(
        body, name, [S((rows, D_MODEL), BF16), S((rows, 2 * D_KV), F32), S((1, LANE), F32)], (rows // T,),
        [blk, full, pl.BlockSpec((1, N_Q_HEADS), lambda j: (0, 0)), blk, pl.BlockSpec((N_Q_HEADS, T), lambda j: (0, j))],
        [blk, full, pl.BlockSpec((1, LANE), lambda j: (0, 0))], ("arbitrary",), (q, kv, sinks, do, lse), comm=comm)


BLOCK_BYTES = 1 << 20


def _div_tile(rows, cols):
    cap = max(16, BLOCK_BYTES // (4 * cols))
    best = None
    for t in range(16, min(rows, cap) + 1, 16):
        if rows % t == 0:
            best = t
    return best if best is not None else rows


def _adamw(parts, w, m, v, name, comm=None):
    layers, rows, cols = w.shape
    n = parts[0].shape[0]
    tr = _div_tile(rows, cols)
    tc = _pick(cols, 256) if tr == rows and rows * cols * 4 > 2 * BLOCK_BYTES else cols
    c1 = 1.0 / (1.0 - B1 ** STEP)
    c2 = 1.0 / (1.0 - B2 ** STEP)

    def body(*refs):
        p_refs = refs[:layers]
        w_ref, m_ref, v_ref, g_ref, d_ref, nm_ref, nv_ref = refs[layers:]
        layer = pl.program_id(0)
        for l in range(layers):
            @pl.when(layer == l)
            def _(p_ref=p_refs[l]):
                g = p_ref[0].astype(F32)
                for i in range(1, n):
                    g = g + p_ref[i].astype(F32)
                nm = B1 * m_ref[...] + (1.0 - B1) * g
                nv = B2 * v_ref[...] + (1.0 - B2) * (g * g)
                g_ref[...] = g
                nm_ref[...] = nm
                nv_ref[...] = nv
                d_ref[...] = -LR * ((nm * c1) / (jnp.sqrt(nv * c2) + EPS) + WD * w_ref[...])

    def part_spec(l):
        return pl.BlockSpec((n, tr, tc), lambda k, i, j: (0, jnp.where(k == l, i, 0), jnp.where(k == l, j, 0)))

    row = pl.BlockSpec((None, tr, tc), lambda k, i, j: (k, i, j))
    return _call(body, name, [S((layers, rows, cols), F32)] * 4, (layers, rows // tr, cols // tc),
                 [part_spec(l) for l in range(layers)] + [row, row, row], [row] * 4,
                 ("parallel", "parallel", "parallel"), (*parts, w, m, v), comm=comm)


def _sum_parts(parts, name):
    n, rows, cols = parts[0].shape
    nb = len(parts)
    tr = _div_tile(rows, cols)

    def body(*refs):
        o_ref = refs[nb]
        blk = pl.program_id(0)
        for l in range(nb):
            @pl.when(blk == l)
            def _(p_ref=refs[l]):
                g = p_ref[0].astype(F32)
                for i in range(1, n):
                    g = g + p_ref[i].astype(F32)
                o_ref[...] = g

    def part_spec(l):
        return pl.BlockSpec((n, tr, cols), lambda k, i: (0, jnp.where(k == l, i, 0), 0))

    per = rows // tr
    return pl.pallas_call(body, name=name, out_shape=S((nb * rows, cols), F32), grid=(nb, per),
                          in_specs=[part_spec(l) for l in range(nb)],
                          out_specs=pl.BlockSpec((tr, cols), lambda k, i: (k * per + i, 0)),
                          compiler_params=_cp(("parallel", "parallel")))(*parts)


def _col_segments(ws, runs):
    segs = []
    for glo, mlo, n in runs:
        while n > 0:
            d, off = divmod(glo, ws)
            take = min(n, ws - off)
            segs.append((d, off, mlo, take))
            glo, mlo, n = glo + take, mlo + take, n - take
    return segs


def _assemble_cols(g, width, segs, name):
    _, rows, ws = g.shape
    rb = _div_tile(rows, width // 2)

    def body(g_ref, o_ref):
        o_ref[...] = jnp.zeros_like(o_ref)
        for d, off, mlo, n in segs:
            o_ref[:, mlo:mlo + n] = g_ref[d, :, off:off + n]

    return pl.pallas_call(
        body, name=name, out_shape=S((rows, width), g.dtype), grid=(rows // rb,),
        in_specs=[pl.BlockSpec((N_DEV, rb, ws), lambda i: (0, i, 0))],
        out_specs=pl.BlockSpec((rb, width), lambda i: (i, 0)), compiler_params=_cp(("parallel",)))(g)


def _scatter_cols(dw, ws, segs, name):
    rows, width = dw.shape
    rb = _div_tile(rows, width)

    def body(w_ref, o_ref):
        for d, off, mlo, n in segs:
            o_ref[d, :, off:off + n] = w_ref[:, mlo:mlo + n].astype(o_ref.dtype)

    return pl.pallas_call(
        body, name=name, out_shape=S((N_DEV, rows, ws), BF16), grid=(rows // rb,),
        in_specs=[pl.BlockSpec((rb, width), lambda i: (i, 0))],
        out_specs=pl.BlockSpec((N_DEV, rb, ws), lambda i: (0, i, 0)), compiler_params=_cp(("parallel",)))(dw)


def _gather_comm(xs):
    n = len(xs)

    def setup(x_refs, out_refs, sems):
        send_sems, recv_sems, local_sems = sems
        mx, my, mc = lax.axis_index("x"), lax.axis_index("y"), lax.axis_index("c")
        me, sibling = (mx, my, mc), (mx, my, 1 - mc)
        chips = [(1 - mx, my), (mx, 1 - my), (1 - mx, 1 - my)]

        def blk(a, px, py, pc):
            return out_refs[a].at[4 * px + 2 * py + pc]

        def copy(a, k, block, to, src=None):
            return pltpu.make_async_remote_copy(
                src_ref=blk(a, *block) if src is None else src, dst_ref=blk(a, *block),
                send_sem=send_sems.at[a, k], recv_sem=recv_sems.at[a, k], device_id=to, device_id_type=_MESH)

        mine = [pltpu.make_async_copy(x_refs[a], blk(a, *me), local_sems.at[a]) for a in range(n)]
        own = []
        for a in range(n):
            own.append(copy(a, 0, me, sibling, src=x_refs[a]))
            own += [copy(a, 1 + i, me, (*chip, mc), src=x_refs[a]) for i, chip in enumerate(chips)]
        return me, sibling, chips, mc, copy, mine, own

    def first(x_refs, out_refs, sems):
        _, _, _, _, _, mine, own = setup(x_refs, out_refs, sems)
        for cp in mine + own:
            cp.start()

    def last(x_refs, out_refs, sems):
        me, sibling, chips, mc, copy, mine, own = setup(x_refs, out_refs, sems)
        passed = []
        for a in range(n):
            for i, chip in enumerate(chips):
                copy(a, 1 + i, (*chip, mc), me).wait_recv()
                passed.append(copy(a, 4 + i, (*chip, mc), sibling))
                passed[-1].start()
        for a in range(n):
            copy(a, 0, sibling, me).wait_recv()
            for i, chip in enumerate(chips):
                copy(a, 4 + i, (*chip, 1 - mc), me).wait_recv()
        for cp in own + passed:
            cp.wait_send()
        for cp in mine:
            cp.wait()

    return _Comm(list(xs), [S((N_DEV,) + x.shape, x.dtype) for x in xs],
                 [pltpu.SemaphoreType.DMA((n, 7)), pltpu.SemaphoreType.DMA((n, 7)), pltpu.SemaphoreType.DMA((n,))],
                 first, last)


def _swap_comm(gs):
    n = len(gs)

    def copies(g_refs, out_refs, sems):
        send_sems, recv_sems = sems
        mx, my, mc = lax.axis_index("x"), lax.axis_index("y"), lax.axis_index("c")
        return [pltpu.make_async_remote_copy(
            src_ref=g_refs[a].at[2 * k + 1 - mc], dst_ref=out_refs[a].at[k], send_sem=send_sems.at[a, k],
            recv_sem=recv_sems.at[a, k], device_id=(mx, my, 1 - mc), device_id_type=_MESH)
            for a in range(n) for k in range(4)]

    def first(g_refs, out_refs, sems):
        for cp in copies(g_refs, out_refs, sems):
            cp.start()

    def last(g_refs, out_refs, sems):
        for cp in copies(g_refs, out_refs, sems):
            cp.wait()

    return _Comm(list(gs), [S((4,) + g.shape[1:], g.dtype) for g in gs],
                 [pltpu.SemaphoreType.DMA((n, 4)), pltpu.SemaphoreType.DMA((n, 4))], first, last)


def _chips_comm(parts):
    n = len(parts)

    def copies(p_refs, out_refs, sems):
        send_sems, recv_sems, local_sems = sems
        mx, my, mc = lax.axis_index("x"), lax.axis_index("y"), lax.axis_index("c")
        mychip = 2 * mx + my
        chips = [(1 - mx, my), (mx, 1 - my), (1 - mx, 1 - my)]
        mine = [pltpu.make_async_copy(p_refs[a].at[mychip], out_refs[a].at[mychip], local_sems.at[a])
                for a in range(n)]
        return mine + [pltpu.make_async_remote_copy(
            src_ref=p_refs[a].at[2 * cx + cy], dst_ref=out_refs[a].at[mychip], send_sem=send_sems.at[a, i],
            recv_sem=recv_sems.at[a, i], device_id=(cx, cy, mc), device_id_type=_MESH)
            for a in range(n) for i, (cx, cy) in enumerate(chips)]

    def first(p_refs, out_refs, sems):
        for cp in copies(p_refs, out_refs, sems):
            cp.start()

    def last(p_refs, out_refs, sems):
        for cp in copies(p_refs, out_refs, sems):
            cp.wait()

    return _Comm(list(parts), [S(p.shape, p.dtype) for p in parts],
                 [pltpu.SemaphoreType.DMA((n, 3)), pltpu.SemaphoreType.DMA((n, 3)), pltpu.SemaphoreType.DMA((n,))],
                 first, last)


def _join_comms(comms):
    def split(refs, counts):
        out, p = [], 0
        for cnt in counts:
            out.append(refs[p:p + cnt])
            p += cnt
        return out

    ni = [len(c.ins) for c in comms]
    no = [len(c.out_shapes) for c in comms]
    ns = [len(c.scratch) for c in comms]

    def first(in_refs, out_refs, sems):
        for c, i, o, s in zip(comms, split(in_refs, ni), split(out_refs, no), split(sems, ns)):
            c.first(i, o, s)

    def last(in_refs, out_refs, sems):
        for c, i, o, s in zip(comms, split(in_refs, ni), split(out_refs, no), split(sems, ns)):
            c.last(i, o, s)

    return _Comm([x for c in comms for x in c.ins], [x for c in comms for x in c.out_shapes],
                 [x for c in comms for x in c.scratch], first, last)


def _add_pairs(mine, theirs, core, name):
    _, rows, cols = mine.shape
    tr = _div_tile(rows, cols)

    def body(core_ref, a_ref, b_ref, o_ref):
        o_ref[...] = (a_ref[...].astype(F32) + b_ref[...].astype(F32)).astype(o_ref.dtype)

    return pl.pallas_call(
        body, name=name, out_shape=S((4, rows, cols), BF16),
        grid_spec=pltpu.PrefetchScalarGridSpec(
            num_scalar_prefetch=1, grid=(4, rows // tr),
            in_specs=[pl.BlockSpec((None, tr, cols), lambda k, i, c: (2 * k + c[0], i, 0)),
                      pl.BlockSpec((None, tr, cols), lambda k, i, c: (k, i, 0))],
            out_specs=pl.BlockSpec((None, tr, cols), lambda k, i, c: (k, i, 0))),
        compiler_params=_cp(("parallel", "parallel")))(core, mine, theirs)


def _run_comm(comm, name):
    ci, co = len(comm.ins), len(comm.out_shapes)

    def body(*refs):
        comm.first(refs[:ci], refs[ci:ci + co], refs[ci + co:])
        comm.last(refs[:ci], refs[ci:ci + co], refs[ci + co:])

    return pl.pallas_call(body, name=name, out_shape=list(comm.out_shapes), in_specs=[_HBM] * ci,
                          out_specs=[_HBM] * co, scratch_shapes=list(comm.scratch))(*comm.ins)


def _flat_rows(n_elems, mult):
    rows = -(-n_elems // LANE)
    return -(-rows // mult) * mult


def _pack(arrs, lead, mult, dtype):
    lead_shape = arrs[0].shape[:lead]
    flat = jnp.concatenate([a.astype(dtype).reshape(lead_shape + (-1,)) for a in arrs], axis=-1)
    n = flat.shape[-1]
    rows = _flat_rows(n, mult)
    flat = jnp.pad(flat, [(0, 0)] * lead + [(0, rows * LANE - n)])
    return flat.reshape(lead_shape + (rows, LANE))


def _unpack(flat, lead, shapes):
    lead_shape = flat.shape[:lead]
    flat = flat.reshape(lead_shape + (-1,))
    out, off = [], 0
    for shp in shapes:
        n = math.prod(shp)
        out.append(flat[..., off:off + n].reshape(lead_shape + tuple(shp)))
        off += n
    return out


def _split8(full, ax, n):
    shp = full.shape
    return jnp.moveaxis(full.reshape(shp[:ax] + (N_DEV, n) + shp[ax + 1:]), ax, 0)


def _join8(g, ax):
    shp = g.shape[1:]
    return jnp.moveaxis(g, 0, ax).reshape(shp[:ax] + (N_DEV * shp[ax],) + shp[ax + 1:])


def _group_lanes(v, hg):
    v = v.reshape(SSM_GROUPS, hg)
    return jnp.pad(v, ((0, 0), (0, LANE - hg))).reshape(1, SSM_GROUPS * LANE)


def _ungroup_lanes(v, hg):
    return v.reshape(SSM_GROUPS, LANE)[:, :hg].reshape(1, SSM_GROUPS * hg)


def kernel(x, meta_tokens, a_norm_pre, a_w_in, a_conv_w, a_conv_b, a_dt_bias, a_a_log, a_d_skip, a_gate_norm, a_w_out, a_norm_post, kv_norm, w_kv, b_norm_pre, b_w_q, b_sinks, b_w_o, b_norm_post, f_norm_pre, f_w_up, f_conv_w, f_conv_b, f_w_down, f_norm_post, loss_target, m_meta_tokens, m_a_norm_pre, m_a_w_in, m_a_conv_w, m_a_conv_b, m_a_dt_bias, m_a_a_log, m_a_d_skip, m_a_gate_norm, m_a_w_out, m_a_norm_post, m_kv_norm, m_w_kv, m_b_norm_pre, m_b_w_q, m_b_sinks, m_b_w_o, m_b_norm_post, m_f_norm_pre, m_f_w_up, m_f_conv_w, m_f_conv_b, m_f_w_down, m_f_norm_post, v_meta_tokens, v_a_norm_pre, v_a_w_in, v_a_conv_w, v_a_conv_b, v_a_dt_bias, v_a_a_log, v_a_d_skip, v_a_gate_norm, v_a_w_out, v_a_norm_post, v_kv_norm, v_w_kv, v_b_norm_pre, v_b_w_q, v_b_sinks, v_b_w_o, v_b_norm_post, v_f_norm_pre, v_f_w_up, v_f_conv_w, v_f_conv_b, v_f_w_down, v_f_norm_post):
    args = locals()
    wts = {n: args[n] for n in WEIGHTS}
    mom = {n: args["m_" + n] for n in WEIGHTS}
    var = {n: args["v_" + n] for n in WEIGHTS}
    mx, my, mc = lax.axis_index("x"), lax.axis_index("y"), lax.axis_index("c")
    me = 4 * mx + 2 * my + mc
    rows = _seq_rows()
    hg = SSM_HEADS // SSM_GROUPS
    d = D_MODEL

    n_main = D_INNER + D_XBC
    ws_in, ws_up = a_w_in.shape[2], f_w_up.shape[2]
    segs_in = _col_segments(ws_in, [(0, 0, n_main)] + [(n_main + hg * g, n_main + LANE * g, hg)
                                                      for g in range(SSM_GROUPS)])
    segs_up = _col_segments(ws_up, [(0, 0, 2 * D_FF)])
    def gather_of(*ws):
        return _gather_comm([w.astype(BF16) for w in ws])

    small_full, = _run_comm(_gather_comm([_pack([wts[n] for n in SMALL], 0, 8, F32)]), "gather_small")
    full = {}
    for n, g in zip(SMALL, _unpack(small_full, 1, [wts[n].shape for n in SMALL])):
        full[n] = _join8(g, SHARD_AXIS[n])
    (h0, hn0), (g_in,) = _embed_norm(full["meta_tokens"], x[0], full["a_norm_pre"], rows, "embed_norm",
                                     comm=gather_of(a_w_in[0]))
    w_in_all = _assemble_cols(g_in, n_main + SSM_GROUPS * LANE, segs_in, "asm_w_in")
    w_up, w_down = [None, None], [None, None]
    bias_g = _group_lanes(wts["a_dt_bias"], hg)
    alog_g = _group_lanes(wts["a_a_log"], hg)
    dsk_g = _group_lanes(wts["a_d_skip"], hg)
    a_conv_w, a_conv_b = full["a_conv_w"][0], full["a_conv_b"]
    f_cw, f_cb = full["f_conv_w"], wts["f_conv_b"]
    fpre, fpost = wts["f_norm_pre"], wts["f_norm_post"]

    tgt = jnp.pad(loss_target[0], ((N_META, rows - N_META - SEQ), (0, 0)))

    zx, (g_out,) = _mm(hn0, w_in_all, "nn", F32, "mm_in", comm=gather_of(a_w_out[0]))
    w_out = g_out.reshape(D_INNER, d)
    xbc, (g_dn1,) = _conv_silu_fwd(zx, a_conv_w, a_conv_b, "conv_a", comm=gather_of(f_w_down[1]))
    (y_ssd, hst), (g_up0,) = _ssd_fwd(xbc, zx, bias_g, alog_g, dsk_g, "ssd_fwd", comm=gather_of(f_w_up[0]))
    w_up[0] = _assemble_cols(g_up0, 2 * D_FF, segs_up, "asm_w_up0")
    yn, (g_kv, g_q) = _gatenorm_fwd(y_ssd, zx, full["a_gate_norm"], "gatenorm", comm=gather_of(w_kv, b_w_q[0]))
    mix_a, (g_o,) = _mm(yn, w_out, "nn", F32, "mm_out", comm=gather_of(b_w_o[0]))
    w_kvf, w_q, w_o = g_kv.reshape(d, 2 * D_KV), g_q.reshape(d, d), g_o.reshape(d, d)
    h1, (fn0,) = _resid_norm(h0, mix_a, full["a_norm_post"], [fpre[0:1]], "resid_a")

    half = d // 2
    u0, (g_dn0,) = _mm(fn0, w_up[0], "nn", F32, "mm_up0", comm=gather_of(f_w_down[0]))
    w_down = [g_dn0.reshape(D_FF, d), g_dn1.reshape(D_FF, d)]
    act0, (g_up1a,) = _ffn_act_fwd(u0, f_cw[0], f_cb[0:1], "ffn_act0", comm=gather_of(f_w_up[1, :half]))
    ffn0 = _mm(act0, w_down[0], "nn", F32, "mm_down0")
    h2, (kvn, bn) = _resid_norm(h1, ffn0, fpost[0:1], [wts["kv_norm"].reshape(1, d), wts["b_norm_pre"]], "resid_f0")
    kv = _mm(kvn, w_kvf, "nn", F32, "mm_kv")
    q = _mm(bn, w_q, "nn", F32, "mm_q")
    (o, lse), (g_up1b,) = _attn_fwd(q, kv, wts["b_sinks"], "attn_fwd", comm=gather_of(f_w_up[1, half:]))
    w_up[1] = jnp.concatenate([_assemble_cols(g_up1a, 2 * D_FF, segs_up, "asm_w_up1a"),
                               _assemble_cols(g_up1b, 2 * D_FF, segs_up, "asm_w_up1b")], axis=0)
    mix_b = _mm(o, w_o, "nn", F32, "mm_o")
    h3, (fn1,) = _resid_norm(h2, mix_b, wts["b_norm_post"], [fpre[1:2]], "resid_b")
    u1 = _mm(fn1, w_up[1], "nn", F32, "mm_up1")
    act1 = _ffn_act_fwd(u1, f_cw[1], f_cb[1:2], "ffn_act1")
    ffn1 = _mm(act1, w_down[1], "nn", F32, "mm_down1")
    dh4, loss_row = _final_loss(h3, ffn1, fpost[1:2], tgt, "loss")
    loss = lax.psum(loss_row[0, 0], ("x", "y", "c"))

    grads = {}

    core = mc.astype(jnp.int32).reshape(1)

    def carried(res, comm):
        return res if comm is not None else (res, None)

    def ffn_bwd(dh_out, h_in, fn, u, act, ffn, i, c_dact=None, c_dwdown=None):
        dffn, dw_post = _norm_bwd(ffn, fpost[i:i + 1], dh_out, None, BF16, f"nb_fpost{i}")
        dact, got_a = carried(_mm(dffn, w_down[i], "nt", F32, f"mm_dact{i}", comm=c_dact), c_dact)
        dw_down, got_b = carried(_mm(act, dffn, "tn", BF16, f"mm_dwdown{i}", comm=c_dwdown), c_dwdown)
        dw_down = dw_down.reshape(N_DEV, -1, d)
        du, dwc, dbc = _ffn_act_bwd(u, dact, f_cw[i], f_cb[i:i + 1], f"ffn_act_bwd{i}")
        dfn, (s_dn,) = _mm(du, w_up[i], "nt", F32, f"mm_dfn{i}", comm=_swap_comm([dw_down]))
        sum_dn = _add_pairs(dw_down, s_dn, core, f"rs_add_dn{i}")
        dw_up = _scatter_cols(_mm(fn, du, "tn", BF16, f"mm_dwup{i}"), ws_up, segs_up, f"scat_w_up{i}")
        (dh_in, dw_pre), (s_up,) = _norm_bwd(h_in, fpre[i:i + 1], dfn, dh_out, F32, f"nb_fpre{i}",
                                             comm=_swap_comm([dw_up]))
        sum_up = _add_pairs(dw_up, s_up, core, f"rs_add_up{i}")
        return dh_in, dict(post=dw_post, sum_down=sum_dn, cw=jnp.concatenate([dwc[0], dwc[1]], axis=1),
                           cb=jnp.concatenate([dbc[0], dbc[1]], axis=1), sum_up=sum_up, pre=dw_pre), got_a, got_b

    dh3, gf1, _, _ = ffn_bwd(dh4, h3, fn1, u1, act1, ffn1, 1)
    dmix_b, grads["b_norm_post"] = _norm_bwd(mix_b, wts["b_norm_post"], dh3, None, BF16, "nb_bpost")
    do = _mm(dmix_b, w_o, "nt", F32, "mm_do")
    dw_o = _mm(o, dmix_b, "tn", BF16, "mm_dwo").reshape(N_DEV, -1, d)
    (dq, dkv, dsinks), (p_up1, p_dn1, s_o) = _attn_bwd(
        q, kv, wts["b_sinks"], do, lse, "attn_bwd",
        comm=_join_comms([_chips_comm([gf1["sum_up"], gf1["sum_down"]]), _swap_comm([dw_o])]))
    sum_o = _add_pairs(dw_o, s_o, core, "rs_add_o")
    grads["b_sinks"] = dsinks[:, :N_Q_HEADS]
    dbn = _mm(dq, w_q, "nt", F32, "mm_dbn")
    dw_q = _mm(bn, dq, "tn", BF16, "mm_dwq").reshape(N_DEV, -1, d)
    dkv16 = dkv.astype(BF16)
    dkvn = _mm(dkv16, w_kvf, "nt", F32, "mm_dkvn")
    dw_kv = _mm(kvn, dkv16, "tn", BF16, "mm_dwkv").reshape(N_DEV, -1, 2 * D_KV)
    (dh2, grads["b_norm_pre"]), (s_q, s_kv) = _norm_bwd(h2, wts["b_norm_pre"], dbn, dh3, F32, "nb_bpre",
                                                        comm=_swap_comm([dw_q, dw_kv]))
    sum_q, sum_kv = _add_pairs(dw_q, s_q, core, "rs_add_q"), _add_pairs(dw_kv, s_kv, core, "rs_add_kv")
    dh2, dw_kvn = _norm_bwd(h2, wts["kv_norm"].reshape(1, d), dkvn, dh2, F32, "nb_kv")
    grads["kv_norm"] = dw_kvn.reshape(d)
    dh1, gf0, (p_o,), (p_q, p_kv) = ffn_bwd(dh2, h1, fn0, u0, act0, ffn0, 0, c_dact=_chips_comm([sum_o]),
                                            c_dwdown=_chips_comm([sum_q, sum_kv]))
    grads["f_norm_post"] = jnp.concatenate([gf0["post"], gf1["post"]], axis=0)
    grads["f_norm_pre"] = jnp.concatenate([gf0["pre"], gf1["pre"]], axis=0)
    grads["f_conv_w"] = jnp.stack([gf0["cw"], gf1["cw"]])
    grads["f_conv_b"] = jnp.concatenate([gf0["cb"], gf1["cb"]], axis=0)

    dmix_a, grads["a_norm_post"] = _norm_bwd(mix_a, full["a_norm_post"], dh1, None, BF16, "nb_apost")
    dyn = _mm(dmix_a, w_out, "nt", F32, "mm_dyn")
    dw_out = _mm(yn, dmix_a, "tn", BF16, "mm_dwout").reshape(N_DEV, -1, d)
    (dy_ssd, dz, grads["a_gate_norm"]), (s_out,) = _gatenorm_bwd(y_ssd, zx, full["a_gate_norm"], dyn, "gatenorm_bwd",
                                                                 comm=_swap_comm([dw_out]))
    sum_out = _add_pairs(dw_out, s_out, core, "rs_add_out")
    (dxs, dbm, dcm, ddtp, dalog, ddsk, dbias), (p_up0, p_dn0, p_out) = _ssd_bwd(
        xbc, zx, bias_g, alog_g, dsk_g, dy_ssd, hst, "ssd_bwd",
        comm=_chips_comm([gf0["sum_up"], gf0["sum_down"], sum_out]))
    dxbc = jnp.concatenate([dxs, dbm, dcm], axis=1)
    grads["a_a_log"] = _ungroup_lanes(dalog, hg)
    grads["a_d_skip"] = _ungroup_lanes(ddsk, hg)
    grads["a_dt_bias"] = _ungroup_lanes(dbias, hg)
    dpre, dcw, dcb = _conv_silu_bwd(zx, dxbc, a_conv_w, a_conv_b, "conv_a_bwd")
    grads["a_conv_w"], grads["a_conv_b"] = dcw[None], dcb
    dzx = jnp.concatenate([dz, dpre, ddtp.astype(BF16)], axis=1)
    dw_in8 = _scatter_cols(_mm(hn0, dzx, "tn", BF16, "mm_dwin"), ws_in, segs_in, "scat_w_in")
    dhn0, (s_in,) = _mm(dzx, w_in_all, "nt", F32, "mm_dhn0", comm=_swap_comm([dw_in8]))
    sum_in = _add_pairs(dw_in8, s_in, core, "rs_add_in")
    half_in = sum_in.shape[1] // 2
    (dh0, grads["a_norm_pre"]), (p_in_a,) = _norm_bwd(h0, full["a_norm_pre"], dhn0, dh1, F32, "nb_apre",
                                                      comm=_chips_comm([sum_in[:, :half_in]]))
    grad_x = dh0[N_META:N_META + SEQ][None]
    grads["meta_tokens"] = dh0[:N_META]

    small_local = _pack([_split8(grads[n], SHARD_AXIS[n], wts[n].shape[SHARD_AXIS[n]]) for n in SMALL], 1, 8, F32)
    repl_local = _pack([grads[n] for n in REPL], 0, 8, F32)
    n_sr = small_local.shape[1]
    small_vec = jnp.concatenate([small_local.reshape(N_DEV * n_sr, LANE), repl_local], axis=0)
    tail = _join_comms([_chips_comm([sum_in[:, half_in:]]), _gather_comm([small_vec])])
    parts_big = dict(a_w_out=[p_out], w_kv=[p_kv], b_w_q=[p_q], b_w_o=[p_o], f_w_up=[p_up0, p_up1],
                     f_w_down=[p_dn0, p_dn1])

    def flat_f32(dct, names, mult):
        return _pack([dct[n] for n in names], 0, mult, F32)

    def adamw_big(n, comm=None):
        shp3 = (len(parts_big[n]),) + parts_big[n][0].shape[1:]
        res = _adamw(parts_big[n], *[dct[n].reshape(shp3) for dct in (wts, mom, var)], f"adamw_{n}", comm=comm)
        res, got = res if comm is not None else (res, None)
        big_out[n] = [r.reshape(wts[n].shape) for r in res]
        return got

    big_out = {}
    p_in_b, small_all = adamw_big("f_w_up", tail)
    for n in BIG:
        if n not in ("f_w_up", "a_w_in"):
            adamw_big(n)
    g_in_t = jnp.transpose(_sum_parts([p_in_a, p_in_b], "sum_w_in"))[None]
    res = _adamw([g_in_t], *[jnp.transpose(dct["a_w_in"][0])[None] for dct in (wts, mom, var)], "adamw_a_w_in")
    big_out["a_w_in"] = [jnp.transpose(r[0])[None] for r in res]
    mine_small = lax.dynamic_slice_in_dim(small_all, me * n_sr, n_sr, axis=1)
    parts_small = jnp.concatenate([mine_small, small_all[:, N_DEV * n_sr:]], axis=1)
    sm_in = [jnp.concatenate([flat_f32(dct, SMALL, 8), flat_f32(dct, REPL, 8)], axis=0)[None] for dct in (wts, mom, var)]
    small_out = [r[0] for r in _adamw([parts_small], *sm_in, "adamw_small")]

    outs = []
    for kind in range(4):
        res = {n: big_out[n][kind] for n in BIG}
        for n, a in zip(SMALL, _unpack(small_out[kind][:n_sr], 0, [wts[n].shape for n in SMALL])):
            res[n] = a
        for n, a in zip(REPL, _unpack(small_out[kind][n_sr:], 0, [wts[n].shape for n in REPL])):
            res[n] = a
        outs.append(res)
    return (loss, grad_x, *[outs[0][n] for n in WEIGHTS], *[outs[1][n] for n in WEIGHTS],
            *[outs[2][n] for n in WEIGHTS], *[outs[3][n] for n in WEIGHTS])
```

```python
import functools
import math

import jax
import jax.numpy as jnp
from jax import lax
from jax.experimental import pallas as pl
from jax.experimental.pallas import tpu as pltpu

F32, BF16 = jnp.float32, jnp.bfloat16
S = jax.ShapeDtypeStruct

D_MODEL = 1024
SEQ = 2048
N_META = 16
D_INNER = 2048
HEAD_P = 64
SSM_HEADS = D_INNER // HEAD_P
SSM_GROUPS = 4
D_STATE = 128
SSM_CONV = 4
D_BC = SSM_GROUPS * D_STATE
D_XBC = D_INNER + 2 * D_BC
ATTN_DH = 64
N_Q_HEADS = D_MODEL // ATTN_DH
N_KV_HEADS = 4
D_KV = N_KV_HEADS * ATTN_DH
WINDOW = 128
D_FF = 2816
FFN_CONV = 3
RMS_EPS = 1e-6
NEG = -1e30
LR, B1, B2, EPS, WD, STEP = 0.001, 0.9, 0.999, 1e-08, 0.01, 10

N_DEV = 8
T = 128
LANE = 128
VMEM_LIMIT = 48 * 1024 * 1024

BIG = ("a_w_in", "a_w_out", "w_kv", "b_w_q", "b_w_o", "f_w_up", "f_w_down")
SMALL = ("meta_tokens", "a_norm_pre", "a_conv_w", "a_conv_b", "a_gate_norm", "a_norm_post", "f_conv_w")
REPL = ("a_dt_bias", "a_a_log", "a_d_skip", "kv_norm", "b_norm_pre", "b_sinks", "b_norm_post",
        "f_norm_pre", "f_conv_b", "f_norm_post")
SHARD_AXIS = dict(a_w_in=2, a_w_out=1, w_kv=0, b_w_q=1, b_w_o=1, f_w_up=2, f_w_down=1, meta_tokens=1,
                  a_norm_pre=1, a_conv_w=2, a_conv_b=1, a_gate_norm=1, a_norm_post=1, f_conv_w=2)
WEIGHTS = ("meta_tokens", "a_norm_pre", "a_w_in", "a_conv_w", "a_conv_b", "a_dt_bias", "a_a_log", "a_d_skip",
           "a_gate_norm", "a_w_out", "a_norm_post", "kv_norm", "w_kv", "b_norm_pre", "b_w_q", "b_sinks", "b_w_o",
           "b_norm_post", "f_norm_pre", "f_w_up", "f_conv_w", "f_conv_b", "f_w_down", "f_norm_post")


def _seq_rows():
    return -(-(N_META + SEQ) // T) * T


def _cp(sem=None):
    return pltpu.CompilerParams(dimension_semantics=sem, vmem_limit_bytes=VMEM_LIMIT)


def _pick(n, target):
    t = min(n, target)
    t -= t % LANE
    while n % t:
        t -= LANE
    return t


def _sigmoid(x):
    return 0.5 * jnp.tanh(0.5 * x) + 0.5


def _softplus(x):
    return jnp.maximum(x, 0.0) + jnp.log(1.0 + jnp.exp(-jnp.abs(x)))


_NN = (((1,), (0,)), ((), ()))
_NT = (((1,), (1,)), ((), ()))
_TN = (((0,), (0,)), ((), ()))


def _dot(a, b, dims=_NN):
    return lax.dot_general(a, b, dims, preferred_element_type=F32)


def _dot_hi(a, b):
    return lax.dot_general(a, b, _NN, precision=lax.Precision.HIGHEST, preferred_element_type=F32)


_HBM = pl.BlockSpec(memory_space=pltpu.HBM)
_MESH = pl.DeviceIdType.MESH


class _Comm:
    def __init__(self, ins, out_shapes, scratch, first, last):
        self.ins, self.out_shapes, self.scratch, self.first, self.last = ins, out_shapes, scratch, first, last


def _call(body, name, out_shape, grid, in_specs, out_specs, sem, args, scratch=(), comm=None):
    if comm is None:
        return pl.pallas_call(body, name=name, out_shape=out_shape, grid=grid, in_specs=in_specs, out_specs=out_specs,
                              scratch_shapes=list(scratch), compiler_params=_cp(sem))(*args)
    single = not isinstance(out_shape, (list, tuple))
    outs = [out_shape] if single else list(out_shape)
    ospecs = [out_specs] if single else list(out_specs)
    n_in, n_out, n_scr, ci, co = len(in_specs), len(outs), len(scratch), len(comm.ins), len(comm.out_shapes)

    def carrier(*refs):
        p = 0
        parts = []
        for cnt in (n_in, ci, n_out, co, n_scr, len(comm.scratch)):
            parts.append(refs[p:p + cnt])
            p += cnt
        ins, cins, outs_r, couts, scr, cscr = parts
        ids = [pl.program_id(i) for i in range(len(grid))]
        first, last = ids[0] == 0, ids[0] == grid[0] - 1
        for i in range(1, len(grid)):
            first, last = first & (ids[i] == 0), last & (ids[i] == grid[i] - 1)

        @pl.when(first)
        def _():
            comm.first(cins, couts, cscr)

        body(*ins, *outs_r, *scr)

        @pl.when(last)
        def _():
            comm.last(cins, couts, cscr)

    res = pl.pallas_call(
        carrier, name=name, out_shape=outs + list(comm.out_shapes), grid=grid,
        in_specs=list(in_specs) + [_HBM] * ci, out_specs=ospecs + [_HBM] * co,
        scratch_shapes=list(scratch) + list(comm.scratch),
        compiler_params=_cp(("arbitrary",) * len(grid)))(*args, *comm.ins)
    mine = res[0] if single else list(res[:n_out])
    return mine, list(res[n_out:])


def _mm(a, b, mode, out_dtype, name, comm=None):
    if mode == "tn":
        m, kk = a.shape
        planes, width = (b.shape[0], b.shape[2]) if b.ndim == 3 else (1, b.shape[1])
        n = planes * width
        tko, tn = _pick(kk, 512), _pick(width, 1536)
        per = width // tn

        def body(a_ref, b_ref, o_ref):
            o_ref[...] = _dot(a_ref[...], b_ref[...], _TN).astype(o_ref.dtype)

        b_spec = (pl.BlockSpec((None, m, tn), lambda i, j: (j // per, 0, j % per)) if b.ndim == 3
                  else pl.BlockSpec((m, tn), lambda i, j: (0, j)))
        return _call(
            body, name, S((kk, n), out_dtype), (kk // tko, n // tn), [pl.BlockSpec((m, tko), lambda i, j: (0, i)), b_spec],
            pl.BlockSpec((tko, tn), lambda i, j: (i, j)), ("parallel", "parallel"), (a, b), comm=comm)

    planes, width = (a.shape[0], a.shape[2]) if a.ndim == 3 else (1, a.shape[1])
    m, kk = a.shape[-2], planes * width
    n = b.shape[1] if mode == "nn" else b.shape[0]
    tn = _pick(n, 512)
    tk = kk if kk <= 2048 else _pick(width, 1536)
    nk = kk // tk
    per = width // tk
    dims = _NN if mode == "nn" else _NT

    def body(a_ref, b_ref, o_ref, *acc):
        part = _dot(a_ref[...], b_ref[...], dims)
        if nk == 1:
            o_ref[...] = part.astype(o_ref.dtype)
        else:
            k = pl.program_id(1)

            @pl.when(k == 0)
            def _():
                acc[0][...] = part

            @pl.when(k > 0)
            def _():
                acc[0][...] += part

            @pl.when(k == nk - 1)
            def _():
                o_ref[...] = acc[0][...].astype(o_ref.dtype)

    b_spec = (pl.BlockSpec((tk, tn), lambda j, k: (k, j)) if mode == "nn"
              else pl.BlockSpec((tn, tk), lambda j, k: (j, k)))
    a_spec = (pl.BlockSpec((None, m, tk), lambda j, k: (k // per, 0, k % per)) if a.ndim == 3
              else pl.BlockSpec((m, tk), lambda j, k: (0, k)))
    return _call(
        body, name, S((m, n), out_dtype), (n // tn, nk), [a_spec, b_spec],
        pl.BlockSpec((m, tn), lambda j, k: (0, j)), ("parallel", "arbitrary"), (a, b),
        scratch=[pltpu.VMEM((m, tn), F32)] if nk > 1 else [], comm=comm)


def _rms(x, w):
    return x * lax.rsqrt(jnp.mean(x * x, axis=-1, keepdims=True) + RMS_EPS) * w


def _row_tile(rows):
    return rows // 8


def _embed_norm(meta, x, w, rows, name, comm=None):
    n_meta, d = meta.shape
    n_x = x.shape[0]
    last = rows // T - 1
    assert n_meta % 8 == 0 and n_meta < T and n_meta + n_x == last * T + n_meta and last * T >= n_x

    def body(m_ref, x_ref, w_ref, h_ref, hn_ref):
        i = pl.program_id(0)

        @pl.when(i == 0)
        def _():
            h_ref[0:n_meta, :] = m_ref[...]
            h_ref[n_meta:T, :] = x_ref[0:T - n_meta, :]

        @pl.when((i > 0) & (i < last))
        def _():
            h_ref[...] = x_ref[pl.ds(pl.multiple_of(i * T - n_meta, 8), T), :]

        @pl.when(i == last)
        def _():
            h_ref[0:n_meta, :] = x_ref[n_x - n_meta:n_x, :]
            h_ref[n_meta:T, :] = jnp.zeros((T - n_meta, d), F32)

        hn_ref[...] = _rms(h_ref[...], w_ref[...]).astype(hn_ref.dtype)

    row = pl.BlockSpec((T, d), lambda i: (i, 0))
    return _call(body, name, [S((rows, d), F32), S((rows, d), BF16)], (rows // T,),
                 [pl.BlockSpec((n_meta, d), lambda i: (0, 0)), pl.BlockSpec((n_x, d), lambda i: (0, 0)),
                  pl.BlockSpec((1, d), lambda i: (0, 0))], [row, row], ("parallel",), (meta, x, w), comm=comm)


def _resid_norm(h, br, w_post, next_ws, name):
    rows, d = h.shape
    tr = _row_tile(rows)
    has_br = br is not None
    nw = len(next_ws)

    def body(*refs):
        h_ref = refs[0]
        pos = 1
        x = h_ref[...]
        if has_br:
            x = x + _rms(refs[1][...], refs[2][...])
            pos = 3
        w_refs = refs[pos:pos + nw]
        outs = refs[pos + nw:]
        if has_br:
            outs[0][...] = x
            outs = outs[1:]
        for w_ref, o_ref in zip(w_refs, outs):
            o_ref[...] = _rms(x, w_ref[...]).astype(o_ref.dtype)

    row = pl.BlockSpec((tr, d), lambda i: (i, 0))
    vec = pl.BlockSpec((1, d), lambda i: (0, 0))
    ins = [h] + ([br, w_post] if has_br else []) + list(next_ws)
    in_specs = [row] + ([row, vec] if has_br else []) + [vec] * nw
    out_shape = ([S((rows, d), F32)] if has_br else []) + [S((rows, d), BF16)] * nw
    res = pl.pallas_call(body, name=name, out_shape=out_shape, grid=(rows // tr,), in_specs=in_specs,
                         out_specs=[row] * len(out_shape), compiler_params=_cp(("parallel",)))(*ins)
    if has_br:
        return res[0], list(res[1:])
    return h, list(res)


def _norm_bwd(x, w, dy, add, out_dtype, name, comm=None):
    rows, d = x.shape
    tr = _row_tile(rows)
    has_add = add is not None

    def body(*refs):
        x_ref, w_ref, dy_ref = refs[:3]
        dx_ref, dw_ref = refs[-2:]
        xv = x_ref[...]
        r = lax.rsqrt(jnp.mean(xv * xv, axis=-1, keepdims=True) + RMS_EPS)
        dyv = dy_ref[...].astype(F32)
        wdy = dyv * w_ref[...]
        dx = r * wdy - xv * (r * r * r) * jnp.mean(xv * wdy, axis=-1, keepdims=True)
        if has_add:
            dx = dx + refs[3][...]
        dx_ref[...] = dx.astype(dx_ref.dtype)

        @pl.when(pl.program_id(0) == 0)
        def _():
            dw_ref[...] = jnp.zeros_like(dw_ref)

        dw_ref[...] += jnp.sum(dyv * xv * r, axis=0, keepdims=True)

    row = pl.BlockSpec((tr, d), lambda i: (i, 0))
    vec = pl.BlockSpec((1, d), lambda i: (0, 0))
    ins = [x, w, dy] + ([add] if has_add else [])
    return _call(body, name, [S((rows, d), out_dtype), S((1, d), F32)], (rows // tr,),
                 [row, vec, row] + ([row] if has_add else []), [row, vec], ("arbitrary",), ins, comm=comm)


def _final_loss(h, br, w_post, tgt, name):
    rows, d = h.shape
    tr = _row_tile(rows)

    def body(h_ref, br_ref, w_ref, t_ref, dh_ref, loss_ref):
        i = pl.program_id(0)
        y = h_ref[...] + _rms(br_ref[...], w_ref[...])
        r = i * tr + lax.broadcasted_iota(jnp.int32, (tr, 1), 0)
        real = (r >= N_META) & (r < N_META + SEQ)
        diff = jnp.where(real, y - t_ref[...], 0.0)
        dh_ref[...] = diff * (1.0 / d)

        @pl.when(i == 0)
        def _():
            loss_ref[...] = jnp.zeros_like(loss_ref)

        loss_ref[...] += jnp.sum(diff * diff) * (0.5 / d)

    row = pl.BlockSpec((tr, d), lambda i: (i, 0))
    return pl.pallas_call(body, name=name, out_shape=[S((rows, d), F32), S((1, LANE), F32)], grid=(rows // tr,),
                          in_specs=[row, row, pl.BlockSpec((1, d), lambda i: (0, 0)), row],
                          out_specs=[row, pl.BlockSpec((1, LANE), lambda i: (0, 0))],
                          compiler_params=_cp(("arbitrary",)))(h, br, w_post, tgt)


def _gatenorm_fwd(y, zx, w, name, comm=None):
    rows, d = y.shape
    tr = _row_tile(rows)

    def body(y_ref, z_ref, w_ref, o_ref):
        z = z_ref[...]
        o_ref[...] = _rms(y_ref[...] * z * _sigmoid(z), w_ref[...]).astype(o_ref.dtype)

    row = pl.BlockSpec((tr, d), lambda i: (i, 0))
    return _call(body, name, S((rows, d), BF16), (rows // tr,), [row, row, pl.BlockSpec((1, d), lambda i: (0, 0))],
                 row, ("parallel",), (y, zx, w), comm=comm)


def _gatenorm_bwd(y, zx, w, dyn, name, comm=None):
    rows, d = y.shape
    tr = _row_tile(rows)

    def body(y_ref, z_ref, w_ref, dyn_ref, dy_ref, dz_ref, dw_ref):
        yv, z = y_ref[...], z_ref[...]
        sg = _sigmoid(z)
        sz = z * sg
        g = yv * sz
        r = lax.rsqrt(jnp.mean(g * g, axis=-1, keepdims=True) + RMS_EPS)
        dyn_v = dyn_ref[...]
        wdy = dyn_v * w_ref[...]
        dg = r * wdy - g * (r * r * r) * jnp.mean(g * wdy, axis=-1, keepdims=True)
        dy_ref[...] = dg * sz
        dz_ref[...] = (dg * yv * sg * (1.0 + z * (1.0 - sg))).astype(dz_ref.dtype)

        @pl.when(pl.program_id(0) == 0)
        def _():
            dw_ref[...] = jnp.zeros_like(dw_ref)

        dw_ref[...] += jnp.sum(dyn_v * g * r, axis=0, keepdims=True)

    row = pl.BlockSpec((tr, d), lambda i: (i, 0))
    vec = pl.BlockSpec((1, d), lambda i: (0, 0))
    return _call(body, name, [S((rows, d), F32), S((rows, d), BF16), S((1, d), F32)], (rows // tr,),
                 [row, row, vec, row], [row, row, vec], ("arbitrary",), (y, zx, w, dyn), comm=comm)


def _shift_down(x, s, rows_iota):
    if s == 0:
        return x
    return jnp.where(rows_iota >= s, pltpu.roll(x, s, 0), 0.0)


def _shift_up(x, s, rows_iota):
    if s == 0:
        return x
    rows = x.shape[0]
    return jnp.where(rows_iota < rows - s, pltpu.roll(x, rows - s, 0), 0.0)


def _r16(v):
    return v.astype(BF16).astype(F32)


def _conv(x, w_ref, b_ref, taps, rows_iota):
    x = _r16(x)
    acc = jnp.zeros_like(x)
    for k in range(taps):
        acc = acc + _r16(w_ref[k:k + 1, :]) * _shift_down(x, taps - 1 - k, rows_iota)
    return acc + b_ref[...]


def _conv_bwd(x, du, w_ref, dw_ref, db_ref, taps, rows_iota):
    db_ref[...] = jnp.sum(du, axis=0, keepdims=True)
    x, du = _r16(x), _r16(du)
    dx = jnp.zeros_like(x)
    for k in range(taps):
        s = taps - 1 - k
        dx = dx + _r16(w_ref[k:k + 1, :]) * _shift_up(du, s, rows_iota)
        dw_ref[k:k + 1, :] = jnp.sum(du * _shift_down(x, s, rows_iota), axis=0, keepdims=True)
    return dx


def _conv_silu_fwd(zx, w, b, name, comm=None):
    rows = zx.shape[0]
    cb = 512
    off = D_INNER // cb

    def body(x_ref, w_ref, b_ref, o_ref):
        it = lax.broadcasted_iota(jnp.int32, (rows, 1), 0)
        u = _conv(x_ref[...], w_ref, b_ref, SSM_CONV, it)
        o_ref[...] = u * _sigmoid(u)

    return _call(
        body, name, S((rows, D_XBC), F32), (D_XBC // cb,),
        [pl.BlockSpec((rows, cb), lambda j: (0, off + j)), pl.BlockSpec((SSM_CONV, cb), lambda j: (0, j)),
         pl.BlockSpec((1, cb), lambda j: (0, j))],
        pl.BlockSpec((rows, cb), lambda j: (0, j)), ("parallel",), (zx, w, b), comm=comm)


def _conv_silu_bwd(zx, dxs, dbm, dcm, w, b, name):
    rows = zx.shape[0]
    cb = 256
    off = D_INNER // cb
    nx, nbc = D_INNER // cb, D_BC // cb

    def body(x_ref, dx_in, db_in, dc_in, w_ref, b_ref, dx_ref, dw_ref, db_ref, dbuf):
        j = pl.program_id(0)
        for cond, src in ((j < nx, dx_in), ((j >= nx) & (j < nx + nbc), db_in), (j >= nx + nbc, dc_in)):
            @pl.when(cond)
            def _(src=src):
                dbuf[...] = src[...]
        it = lax.broadcasted_iota(jnp.int32, (rows, 1), 0)
        x = x_ref[...]
        u = _conv(x, w_ref, b_ref, SSM_CONV, it)
        sg = _sigmoid(u)
        du = dbuf[...] * sg * (1.0 + u * (1.0 - sg))
        dx_ref[...] = _conv_bwd(x, du, w_ref, dw_ref, db_ref, SSM_CONV, it).astype(dx_ref.dtype)

    def part(first, count):
        return pl.BlockSpec((rows, cb), lambda j: (0, jnp.clip(j - first, 0, count - 1)))

    col = pl.BlockSpec((rows, cb), lambda j: (0, j))
    wsp = pl.BlockSpec((SSM_CONV, cb), lambda j: (0, j))
    bsp = pl.BlockSpec((1, cb), lambda j: (0, j))
    return pl.pallas_call(
        body, name=name, out_shape=[S((rows, D_XBC), BF16), S((SSM_CONV, D_XBC), F32), S((1, D_XBC), F32)],
        grid=(D_XBC // cb,),
        in_specs=[pl.BlockSpec((rows, cb), lambda j: (0, off + j)), part(0, nx), part(nx, nbc), part(nx + nbc, nbc),
                  wsp, bsp],
        out_specs=[col, wsp, bsp], scratch_shapes=[pltpu.VMEM((rows, cb), F32)],
        compiler_params=_cp(("arbitrary",)))(zx, dxs, dbm, dcm, w, b)


def _ffn_act_fwd(u, w, b, name, comm=None):
    rows = u.shape[0]
    cb = 256
    nb = D_FF // cb

    def body(g_ref, v_ref, wg_ref, wv_ref, bg_ref, bv_ref, o_ref):
        it = lax.broadcasted_iota(jnp.int32, (rows, 1), 0)
        g = _conv(g_ref[...], wg_ref, bg_ref, FFN_CONV, it)
        v = _conv(v_ref[...], wv_ref, bv_ref, FFN_CONV, it)
        o_ref[...] = (g * _sigmoid(g) * v).astype(o_ref.dtype)

    def sp(r, shift):
        return pl.BlockSpec((r, cb), lambda j: (0, shift + j))

    return _call(
        body, name, S((rows, D_FF), BF16), (nb,),
        [sp(rows, 0), sp(rows, nb), sp(FFN_CONV, 0), sp(FFN_CONV, nb), sp(1, 0), sp(1, nb)],
        sp(rows, 0), ("parallel",), (u, u, w, w, b, b), comm=comm)


def _ffn_act_bwd(u, dact, w, b, name, comm=None):
    rows = u.shape[0]
    cb = 256
    nb = D_FF // cb

    def body(g_ref, v_ref, d_ref, wg_ref, wv_ref, bg_ref, bv_ref, du_ref, dw_ref, db_ref):
        it = lax.broadcasted_iota(jnp.int32, (rows, 1), 0)
        xg, xv = g_ref[...], v_ref[...]
        g = _conv(xg, wg_ref, bg_ref, FFN_CONV, it)
        v = _conv(xv, wv_ref, bv_ref, FFN_CONV, it)
        sg = _sigmoid(g)
        d = d_ref[...]
        dgate = d * v * sg * (1.0 + g * (1.0 - sg))
        dval = d * g * sg
        du_ref[0] = _conv_bwd(xg, dgate, wg_ref, dw_ref.at[0], db_ref.at[0], FFN_CONV, it).astype(du_ref.dtype)
        du_ref[1] = _conv_bwd(xv, dval, wv_ref, dw_ref.at[1], db_ref.at[1], FFN_CONV, it).astype(du_ref.dtype)

    def sp(r, shift):
        return pl.BlockSpec((r, cb), lambda j: (0, shift + j))

    def both(r):
        return pl.BlockSpec((2, r, cb), lambda j: (0, 0, j))

    return _call(
        body, name, [S((2, rows, D_FF), BF16), S((2, FFN_CONV, D_FF), F32), S((2, 1, D_FF), F32)], (nb,),
        [sp(rows, 0), sp(rows, nb), sp(rows, 0), sp(FFN_CONV, 0), sp(FFN_CONV, nb), sp(1, 0), sp(1, nb)],
        [both(rows), both(FFN_CONV), both(1)], ("parallel",), (u, u, dact, w, w, b, b), comm=comm)


def _ssd_consts(dtp_ref, bias_ref, alog_ref, hg):
    lane = lax.broadcasted_iota(jnp.int32, (1, LANE), 1)
    pre = dtp_ref[...] + bias_ref[...]
    dt = _softplus(pre)
    a_row = jnp.where(lane < hg, -jnp.exp(alog_ref[...]), 0.0)
    ri = lax.broadcasted_iota(jnp.int32, (T, T), 0)
    ci = lax.broadcasted_iota(jnp.int32, (T, T), 1)
    cs = _dot_hi((ri >= ci).astype(F32), dt * a_row)
    return pre, dt, a_row, cs, ri, ci, lane


def _ssd_fwd(xbc, zx, bias, alog, dsk, name, comm=None):
    rows = xbc.shape[0]
    nc = rows // T
    hg = SSM_HEADS // SSM_GROUPS
    gw = hg * HEAD_P
    xoff, boff, coff = 0, D_INNER // D_STATE, (D_INNER + D_BC) // D_STATE
    dtoff = (D_INNER + D_XBC) // LANE

    def body(x_ref, b_ref, c_ref, dtp_ref, bias_ref, alog_ref, dsk_ref, y_ref, hst_ref, hs):
        c = pl.program_id(1)

        @pl.when(c == 0)
        def _():
            hs[...] = jnp.zeros_like(hs)

        _, dt, _, cs, ri, ci, _ = _ssd_consts(dtp_ref, bias_ref, alog_ref, hg)
        cst, dtt = cs.T, dt.T
        xt = x_ref[...].T
        bb, cbf = b_ref[...].astype(BF16), c_ref[...].astype(BF16)
        gt = _dot(bb, cbf, _NT)
        causal_t = ci >= ri
        dskv = dsk_ref[...]
        hall = hs[...]
        hst_ref[0, 0] = hall
        yts, new_h = [], []
        for k in range(hg):
            sl = slice(k * HEAD_P, (k + 1) * HEAD_P)
            csc, csr = cs[:, k:k + 1], cst[k:k + 1, :]
            lt = jnp.exp(jnp.where(causal_t, csr - csc, NEG))
            xk = xt[sl, :]
            xdt = xk * dtt[k:k + 1, :]
            hk = hall[sl, :]
            yd = _dot(xdt.astype(BF16), (gt * lt).astype(BF16))
            yo = jnp.exp(csr) * _dot(hk.astype(BF16), cbf, _NT)
            yts.append(yd + yo + dskv[:, k:k + 1] * xk)
            cl = cs[T - 1:T, k:k + 1]
            st = _dot((xdt * jnp.exp(cl - csr)).astype(BF16), bb)
            new_h.append(jnp.exp(cl) * hk + st)
        y_ref[...] = jnp.concatenate(yts, axis=0).T
        hs[...] = jnp.concatenate(new_h, axis=0)

    vec = pl.BlockSpec((1, LANE), lambda g, c: (0, g))
    return _call(
        body, name, [S((rows, D_INNER), F32), S((nc, SSM_GROUPS, gw, D_STATE), F32)], (SSM_GROUPS, nc),
        [pl.BlockSpec((T, gw), lambda g, c: (c, xoff + g)),
         pl.BlockSpec((T, D_STATE), lambda g, c: (c, boff + g)),
         pl.BlockSpec((T, D_STATE), lambda g, c: (c, coff + g)),
         pl.BlockSpec((T, LANE), lambda g, c: (c, dtoff + g)), vec, vec, vec],
        [pl.BlockSpec((T, gw), lambda g, c: (c, g)), pl.BlockSpec((1, 1, gw, D_STATE), lambda g, c: (c, g, 0, 0))],
        ("parallel", "arbitrary"), (xbc, xbc, xbc, zx, bias, alog, dsk),
        scratch=[pltpu.VMEM((gw, D_STATE), F32)], comm=comm)


def _ssd_bwd(xbc, zx, bias, alog, dsk, dy, hst, name, comm=None):
    rows = xbc.shape[0]
    nc = rows // T
    hg = SSM_HEADS // SSM_GROUPS
    gw = hg * HEAD_P
    boff, coff = D_INNER // D_STATE, (D_INNER + D_BC) // D_STATE
    dtoff = (D_INNER + D_XBC) // LANE

    def body(x_ref, b_ref, c_ref, dtp_ref, bias_ref, alog_ref, dsk_ref, dy_ref, hst_ref,
             dx_ref, db_ref, dc_ref, ddtp_ref, dalog_ref, ddsk_ref, dbias_ref, dhs):
        step = pl.program_id(1)

        @pl.when(step == 0)
        def _():
            dhs[...] = jnp.zeros_like(dhs)
            dalog_ref[...] = jnp.zeros_like(dalog_ref)
            ddsk_ref[...] = jnp.zeros_like(ddsk_ref)
            dbias_ref[...] = jnp.zeros_like(dbias_ref)

        pre, dt, a_row, cs, ri, ci, lane = _ssd_consts(dtp_ref, bias_ref, alog_ref, hg)
        cst, dtt = cs.T, dt.T
        xt, dyt = x_ref[...].T, dy_ref[...].T
        bb, cbf = b_ref[...].astype(BF16), c_ref[...].astype(BF16)
        gt = _dot(bb, cbf, _NT)
        causal_t = ci >= ri
        dskv = dsk_ref[...]
        hall, dhall = hst_ref[0, 0], dhs[...]
        head_row = lax.broadcasted_iota(jnp.int32, (T, 1), 0)
        last_l = lax.broadcasted_iota(jnp.int32, (1, T), 1) == T - 1
        dgt = jnp.zeros((T, T), F32)
        dc_acc = jnp.zeros((T, D_STATE), F32)
        db_acc = jnp.zeros((T, D_STATE), F32)
        ddt_rows = jnp.zeros((T, T), F32)
        dcs_rows = jnp.zeros((T, T), F32)
        qrow_cols = jnp.zeros((T, LANE), F32)
        ddsk_acc = jnp.zeros((1, LANE), F32)
        dxts, new_dh = [], []
        for k in range(hg):
            sl = slice(k * HEAD_P, (k + 1) * HEAD_P)
            csc, csr = cs[:, k:k + 1], cst[k:k + 1, :]
            lt = jnp.exp(jnp.where(causal_t, csr - csc, NEG))
            xk, dyk = xt[sl, :], dyt[sl, :]
            dtr, dk = dtt[k:k + 1, :], dskv[:, k:k + 1]
            xdt = xk * dtr
            mpt = gt * lt
            dyb = dyk.astype(BF16)
            dxdt = _dot(dyb, mpt.astype(BF16), _NT)
            dmt = _dot(xdt.astype(BF16), dyb, _TN)
            dgt = dgt + dmt * lt
            q = dmt * mpt
            q_rows = jnp.sum(q, axis=1, keepdims=True)
            q_cols = jnp.sum(q, axis=0, keepdims=True)
            hk, dhn = hall[sl, :], dhall[sl, :]
            e = jnp.exp(csr)
            cl = cs[T - 1:T, k:k + 1]
            wdec = jnp.exp(cl - csr)
            w = wdec * dtr
            rt = _dot(dhn.astype(BF16), bb, _NT)
            dxts.append(dtr * dxdt + dk * dyk + rt * w)
            xz = jnp.sum(xk * dxdt, axis=0, keepdims=True)
            dw = jnp.sum(rt * xk, axis=0, keepdims=True)
            dcl = jnp.exp(cl) * jnp.sum(dhn * hk) + jnp.sum(dw * w)
            yo = e * _dot(hk.astype(BF16), cbf, _NT)
            dcs_r = jnp.sum(dyk * yo, axis=0, keepdims=True) + q_cols - dw * w + jnp.where(last_l, dcl, 0.0)
            dye = (dyk * e).astype(BF16)
            dc_acc = dc_acc + _dot(dye, hk.astype(BF16), _TN)
            db_acc = db_acc + _dot((xk * w).astype(BF16), dhn.astype(BF16), _TN)
            new_dh.append(jnp.exp(cl) * dhn + _dot(dye, cbf))
            onehot = (lane == k).astype(F32)
            ddt_rows = ddt_rows + jnp.where(head_row == k, xz + dw * wdec, 0.0)
            dcs_rows = dcs_rows + jnp.where(head_row == k, dcs_r, 0.0)
            qrow_cols = qrow_cols + q_rows * onehot
            ddsk_acc = ddsk_acc + jnp.sum(dyk * xk) * onehot
        dx_ref[...] = jnp.concatenate(dxts, axis=0).T
        dhs[...] = jnp.concatenate(new_dh, axis=0)
        dc_ref[...] = _dot(dgt.T.astype(BF16), bb) + dc_acc
        db_ref[...] = _dot(dgt.astype(BF16), cbf) + db_acc
        da = _dot_hi((ci >= ri).astype(F32), dcs_rows.T - qrow_cols)
        ddtp = (ddt_rows.T + da * a_row) * _sigmoid(pre)
        ddtp = jnp.where(lane < hg, ddtp, 0.0)
        ddtp_ref[...] = ddtp
        dbias_ref[...] += jnp.sum(ddtp, axis=0, keepdims=True)
        dalog_ref[...] += jnp.sum(da * dt, axis=0, keepdims=True) * a_row
        ddsk_ref[...] += ddsk_acc

    def rc(c):
        return nc - 1 - c

    vec = pl.BlockSpec((1, LANE), lambda g, c: (0, g))
    xsp = pl.BlockSpec((T, gw), lambda g, c: (rc(c), g))
    return _call(
        body, name,
        [S((rows, D_INNER), F32), S((rows, D_BC), F32), S((rows, D_BC), F32),
         S((rows, SSM_GROUPS * LANE), F32), S((1, SSM_GROUPS * LANE), F32),
         S((1, SSM_GROUPS * LANE), F32), S((1, SSM_GROUPS * LANE), F32)],
        (SSM_GROUPS, nc),
        [xsp,
         pl.BlockSpec((T, D_STATE), lambda g, c: (rc(c), boff + g)),
         pl.BlockSpec((T, D_STATE), lambda g, c: (rc(c), coff + g)),
         pl.BlockSpec((T, LANE), lambda g, c: (rc(c), dtoff + g)), vec, vec, vec,
         xsp, pl.BlockSpec((1, 1, gw, D_STATE), lambda g, c: (rc(c), g, 0, 0))],
        [xsp,
         pl.BlockSpec((T, D_STATE), lambda g, c: (rc(c), g)),
         pl.BlockSpec((T, D_STATE), lambda g, c: (rc(c), g)),
         pl.BlockSpec((T, LANE), lambda g, c: (rc(c), g)), vec, vec, vec],
        ("parallel", "arbitrary"), (xbc, xbc, xbc, zx, bias, alog, dsk, dy, hst),
        scratch=[pltpu.VMEM((gw, D_STATE), F32)], comm=comm)


def _attn_tiles(kv_ref, j):
    prev = jnp.maximum(j - 1, 0)
    meta = kv_ref[0:T, :]
    prv = kv_ref[pl.ds(pl.multiple_of(prev * T, T), T), :]
    cur = kv_ref[pl.ds(pl.multiple_of(j * T, T), T), :]
    return jnp.concatenate([meta, prv, cur], axis=0)


def _attn_mask(j):
    r = j * T + lax.broadcasted_iota(jnp.int32, (3 * T, T), 1)
    row = lax.broadcasted_iota(jnp.int32, (3 * T, T), 0)
    t0, t1 = row < T, row < 2 * T
    s = jnp.where(t0, row, (j - 2) * T + row)
    ok = (s <= r) & ((s < N_META) | (s > r - WINDOW))
    use = (t0 & (j >= 2) & (row < N_META)) | (jnp.logical_not(t0) & t1 & (j >= 1)) | jnp.logical_not(t1)
    return ok & use


def _attn_fwd(q, kv, sinks, name, comm=None):
    rows = q.shape[0]
    scale = 1.0 / math.sqrt(ATTN_DH)
    qpk = N_Q_HEADS // N_KV_HEADS

    def body(q_ref, kv_ref, s_ref, o_ref, lse_ref):
        j = pl.program_id(0)
        kv3 = _attn_tiles(kv_ref, j).astype(BF16)
        mask = _attn_mask(j)
        qv = (q_ref[...] * scale).astype(BF16)
        sk = s_ref[...]
        lses = []
        for kh in range(N_KV_HEADS):
            k3 = kv3[:, kh * ATTN_DH:(kh + 1) * ATTN_DH]
            v3 = kv3[:, D_KV + kh * ATTN_DH:D_KV + (kh + 1) * ATTN_DH]
            for g in range(qpk):
                h = kh * qpk + g
                sink = sk[:, h:h + 1]
                sc = jnp.where(mask, _dot(k3, qv[:, h * ATTN_DH:(h + 1) * ATTN_DH], _NT), NEG)
                m = jnp.maximum(jnp.max(sc, axis=0, keepdims=True), sink)
                p = jnp.exp(sc - m)
                den = jnp.sum(p, axis=0, keepdims=True) + jnp.exp(sink - m)
                p = p * (1.0 / den)
                lses.append(m + jnp.log(den))
                o_ref[:, h * ATTN_DH:(h + 1) * ATTN_DH] = _dot(p.astype(BF16), v3, _TN).astype(o_ref.dtype)
        lse_ref[...] = jnp.concatenate(lses, axis=0)

    return _call(
        body, name, [S((rows, D_MODEL), BF16), S((N_Q_HEADS, rows), F32)], (rows // T,),
        [pl.BlockSpec((T, D_MODEL), lambda j: (j, 0)), pl.BlockSpec((rows, 2 * D_KV), lambda j: (0, 0)),
         pl.BlockSpec((1, N_Q_HEADS), lambda j: (0, 0))],
        [pl.BlockSpec((T, D_MODEL), lambda j: (j, 0)), pl.BlockSpec((N_Q_HEADS, T), lambda j: (0, j))],
        ("parallel",), (q, kv, sinks), comm=comm)


def _attn_bwd(q, kv, sinks, do, lse, name, comm=None):
    rows = q.shape[0]
    scale = 1.0 / math.sqrt(ATTN_DH)
    qpk = N_Q_HEADS // N_KV_HEADS

    def body(q_ref, kv_ref, s_ref, do_ref, lse_ref, dq_ref, dkv_ref, ds_ref):
        j = pl.program_id(0)

        @pl.when(j == 0)
        def _():
            dkv_ref[...] = jnp.zeros_like(dkv_ref)
            ds_ref[...] = jnp.zeros_like(ds_ref)

        kv3 = _attn_tiles(kv_ref, j).astype(BF16)
        mask = _attn_mask(j)
        qv = (q_ref[...] * scale).astype(BF16)
        dov = do_ref[...].astype(BF16)
        sk = s_ref[...]
        lsev = lse_ref[...]
        lane = lax.broadcasted_iota(jnp.int32, (1, LANE), 1)
        ds_acc = jnp.zeros((1, LANE), F32)
        prev = jnp.maximum(j - 1, 0)
        dqts = []
        for kh in range(N_KV_HEADS):
            ksl = slice(kh * ATTN_DH, (kh + 1) * ATTN_DH)
            vsl = slice(D_KV + kh * ATTN_DH, D_KV + (kh + 1) * ATTN_DH)
            k3, v3 = kv3[:, ksl], kv3[:, vsl]
            k3t = k3.T
            dk3 = jnp.zeros((3 * T, ATTN_DH), F32)
            dv3 = jnp.zeros((3 * T, ATTN_DH), F32)
            for g in range(qpk):
                h = kh * qpk + g
                hs = slice(h * ATTN_DH, (h + 1) * ATTN_DH)
                qh, doh = qv[:, hs], dov[:, hs]
                lh = lsev[h:h + 1, :]
                p = jnp.exp(jnp.where(mask, _dot(k3, qh, _NT), NEG) - lh)
                ps = jnp.exp(sk[:, h:h + 1] - lh)
                dp = _dot(v3, doh, _NT)
                delta = jnp.sum(p * dp, axis=0, keepdims=True)
                dsc = (p * (dp - delta)).astype(BF16)
                dqts.append(_dot(k3t, dsc) * scale)
                dk3 = dk3 + _dot(dsc, qh)
                dv3 = dv3 + _dot(p.astype(BF16), doh)
                ds_acc = ds_acc - jnp.sum(ps * delta) * (lane == h).astype(F32)
            for t, start in enumerate((0, pl.multiple_of(prev * T, T), pl.multiple_of(j * T, T))):
                rsl = pl.ds(start, T)
                dkv_ref[rsl, ksl] += dk3[t * T:(t + 1) * T, :]
                dkv_ref[rsl, vsl] += dv3[t * T:(t + 1) * T, :]
        ds_ref[...] += ds_acc
        dq_ref[...] = jnp.concatenate(dqts, axis=0).T.astype(dq_ref.dtype)

    blk = pl.BlockSpec((T, D_MODEL), lambda j: (j, 0))
    full = pl.BlockSpec((rows, 2 * D_KV), lambda j: (0, 0))
    return _call(
        body, name, [S((rows, D_MODEL), BF16), S((rows, 2 * D_KV), F32), S((1, LANE), F32)], (rows // T,),
        [blk, full, pl.BlockSpec((1, N_Q_HEADS), lambda j: (0, 0)), blk, pl.BlockSpec((N_Q_HEADS, T), lambda j: (0, j))],
        [blk, full, pl.BlockSpec((1, LANE), lambda j: (0, 0))], ("arbitrary",), (q, kv, sinks, do, lse), comm=comm)


BLOCK_BYTES = 1 << 20


def _div_tile(rows, cols):
    cap = max(16, BLOCK_BYTES // (4 * cols))
    best = None
    for t in range(16, min(rows, cap) + 1, 16):
        if rows % t == 0:
            best = t
    return best if best is not None else rows


def _adamw(parts, w, m, v, name, comm=None):
    layers, rows, cols = w.shape
    n = parts[0].shape[0]
    tr = _div_tile(rows, cols)
    tc = _pick(cols, 256) if tr == rows and rows * cols * 4 > 2 * BLOCK_BYTES else cols
    c1 = 1.0 / (1.0 - B1 ** STEP)
    c2 = 1.0 / (1.0 - B2 ** STEP)

    def body(*refs):
        p_refs = refs[:layers]
        w_ref, m_ref, v_ref, g_ref, d_ref, nm_ref, nv_ref = refs[layers:]
        layer = pl.program_id(0)
        for l in range(layers):
            @pl.when(layer == l)
            def _(p_ref=p_refs[l]):
                g = p_ref[0].astype(F32)
                for i in range(1, n):
                    g = g + p_ref[i].astype(F32)
                nm = B1 * m_ref[...] + (1.0 - B1) * g
                nv = B2 * v_ref[...] + (1.0 - B2) * (g * g)
                g_ref[...] = g
                nm_ref[...] = nm
                nv_ref[...] = nv
                d_ref[...] = -LR * ((nm * c1) / (jnp.sqrt(nv * c2) + EPS) + WD * w_ref[...])

    def part_spec(l):
        return pl.BlockSpec((n, tr, tc), lambda k, i, j: (0, jnp.where(k == l, i, 0), jnp.where(k == l, j, 0)))

    row = pl.BlockSpec((None, tr, tc), lambda k, i, j: (k, i, j))
    return _call(body, name, [S((layers, rows, cols), F32)] * 4, (layers, rows // tr, cols // tc),
                 [part_spec(l) for l in range(layers)] + [row, row, row], [row] * 4,
                 ("parallel", "parallel", "parallel"), (*parts, w, m, v), comm=comm)


def _sum_parts(parts, name):
    n, rows, cols = parts[0].shape
    nb = len(parts)
    tr = _div_tile(rows, cols)

    def body(*refs):
        o_ref = refs[nb]
        blk = pl.program_id(0)
        for l in range(nb):
            @pl.when(blk == l)
            def _(p_ref=refs[l]):
                g = p_ref[0].astype(F32)
                for i in range(1, n):
                    g = g + p_ref[i].astype(F32)
                o_ref[...] = g

    def part_spec(l):
        return pl.BlockSpec((n, tr, cols), lambda k, i: (0, jnp.where(k == l, i, 0), 0))

    per = rows // tr
    return pl.pallas_call(body, name=name, out_shape=S((nb * rows, cols), F32), grid=(nb, per),
                          in_specs=[part_spec(l) for l in range(nb)],
                          out_specs=pl.BlockSpec((tr, cols), lambda k, i: (k * per + i, 0)),
                          compiler_params=_cp(("parallel", "parallel")))(*parts)


def _col_segments(ws, runs):
    segs = []
    for glo, mlo, n in runs:
        while n > 0:
            d, off = divmod(glo, ws)
            take = min(n, ws - off)
            segs.append((d, off, mlo, take))
            glo, mlo, n = glo + take, mlo + take, n - take
    return segs


def _assemble_cols(gs, width, segs, name):
    _, rows, ws = gs[0].shape
    nb = len(gs)
    rb = _div_tile(rows, width // 2)
    per = rows // rb

    def body(*refs):
        o_ref = refs[nb]
        piece = pl.program_id(0)
        for l in range(nb):
            @pl.when(piece == l)
            def _(g_ref=refs[l]):
                o_ref[...] = jnp.zeros_like(o_ref)
                for d, off, mlo, n in segs:
                    o_ref[:, mlo:mlo + n] = g_ref[d, :, off:off + n]

    def piece_spec(l):
        return pl.BlockSpec((N_DEV, rb, ws), lambda k, i: (0, jnp.where(k == l, i, 0), 0))

    return pl.pallas_call(
        body, name=name, out_shape=S((nb * rows, width), gs[0].dtype), grid=(nb, per),
        in_specs=[piece_spec(l) for l in range(nb)],
        out_specs=pl.BlockSpec((rb, width), lambda k, i: (k * per + i, 0)),
        compiler_params=_cp(("parallel", "parallel")))(*gs)


def _scatter_cols(dw, ws, segs, name):
    rows, width = dw.shape
    rb = _div_tile(rows, width)

    def body(w_ref, o_ref):
        for d, off, mlo, n in segs:
            o_ref[d, :, off:off + n] = w_ref[:, mlo:mlo + n].astype(o_ref.dtype)

    return pl.pallas_call(
        body, name=name, out_shape=S((N_DEV, rows, ws), BF16), grid=(rows // rb,),
        in_specs=[pl.BlockSpec((rb, width), lambda i: (i, 0))],
        out_specs=pl.BlockSpec((N_DEV, rb, ws), lambda i: (0, i, 0)), compiler_params=_cp(("parallel",)))(dw)


def _gather_comm(xs):
    n = len(xs)

    def setup(x_refs, out_refs, sems):
        send_sems, recv_sems, local_sems = sems
        mx, my, mc = lax.axis_index("x"), lax.axis_index("y"), lax.axis_index("c")
        me, sibling = (mx, my, mc), (mx, my, 1 - mc)
        chips = [(1 - mx, my), (mx, 1 - my), (1 - mx, 1 - my)]

        def blk(a, px, py, pc):
            return out_refs[a].at[4 * px + 2 * py + pc]

        def copy(a, k, block, to, src=None):
            return pltpu.make_async_remote_copy(
                src_ref=blk(a, *block) if src is None else src, dst_ref=blk(a, *block),
                send_sem=send_sems.at[a, k], recv_sem=recv_sems.at[a, k], device_id=to, device_id_type=_MESH)

        mine = [pltpu.make_async_copy(x_refs[a], blk(a, *me), local_sems.at[a]) for a in range(n)]
        own = []
        for a in range(n):
            own.append(copy(a, 0, me, sibling, src=x_refs[a]))
            own += [copy(a, 1 + i, me, (*chip, mc), src=x_refs[a]) for i, chip in enumerate(chips)]
        return me, sibling, chips, mc, copy, mine, own

    def first(x_refs, out_refs, sems):
        _, _, _, _, _, mine, own = setup(x_refs, out_refs, sems)
        for cp in mine + own:
            cp.start()

    def last(x_refs, out_refs, sems):
        me, sibling, chips, mc, copy, mine, own = setup(x_refs, out_refs, sems)
        passed = []
        for a in range(n):
            for i, chip in enumerate(chips):
                copy(a, 1 + i, (*chip, mc), me).wait_recv()
                passed.append(copy(a, 4 + i, (*chip, mc), sibling))
                passed[-1].start()
        for a in range(n):
            copy(a, 0, sibling, me).wait_recv()
            for i, chip in enumerate(chips):
                copy(a, 4 + i, (*chip, 1 - mc), me).wait_recv()
        for cp in own + passed:
            cp.wait_send()
        for cp in mine:
            cp.wait()

    return _Comm(list(xs), [S((N_DEV,) + x.shape, x.dtype) for x in xs],
                 [pltpu.SemaphoreType.DMA((n, 7)), pltpu.SemaphoreType.DMA((n, 7)), pltpu.SemaphoreType.DMA((n,))],
                 first, last)


def _swap_comm(gs):
    n = len(gs)

    def copies(g_refs, out_refs, sems):
        send_sems, recv_sems = sems
        mx, my, mc = lax.axis_index("x"), lax.axis_index("y"), lax.axis_index("c")
        return [pltpu.make_async_remote_copy(
            src_ref=g_refs[a].at[2 * k + 1 - mc], dst_ref=out_refs[a].at[k], send_sem=send_sems.at[a, k],
            recv_sem=recv_sems.at[a, k], device_id=(mx, my, 1 - mc), device_id_type=_MESH)
            for a in range(n) for k in range(4)]

    def first(g_refs, out_refs, sems):
        for cp in copies(g_refs, out_refs, sems):
            cp.start()

    def last(g_refs, out_refs, sems):
        for cp in copies(g_refs, out_refs, sems):
            cp.wait()

    return _Comm(list(gs), [S((4,) + g.shape[1:], g.dtype) for g in gs],
                 [pltpu.SemaphoreType.DMA((n, 4)), pltpu.SemaphoreType.DMA((n, 4))], first, last)


def _chips_comm(parts):
    n = len(parts)

    def copies(p_refs, out_refs, sems):
        send_sems, recv_sems, local_sems = sems
        mx, my, mc = lax.axis_index("x"), lax.axis_index("y"), lax.axis_index("c")
        mychip = 2 * mx + my
        chips = [(1 - mx, my), (mx, 1 - my), (1 - mx, 1 - my)]
        mine = [pltpu.make_async_copy(p_refs[a].at[mychip], out_refs[a].at[mychip], local_sems.at[a])
                for a in range(n)]
        return mine + [pltpu.make_async_remote_copy(
            src_ref=p_refs[a].at[2 * cx + cy], dst_ref=out_refs[a].at[mychip], send_sem=send_sems.at[a, i],
            recv_sem=recv_sems.at[a, i], device_id=(cx, cy, mc), device_id_type=_MESH)
            for a in range(n) for i, (cx, cy) in enumerate(chips)]

    def first(p_refs, out_refs, sems):
        for cp in copies(p_refs, out_refs, sems):
            cp.start()

    def last(p_refs, out_refs, sems):
        for cp in copies(p_refs, out_refs, sems):
            cp.wait()

    return _Comm(list(parts), [S(p.shape, p.dtype) for p in parts],
                 [pltpu.SemaphoreType.DMA((n, 3)), pltpu.SemaphoreType.DMA((n, 3)), pltpu.SemaphoreType.DMA((n,))],
                 first, last)


def _join_comms(comms):
    def split(refs, counts):
        out, p = [], 0
        for cnt in counts:
            out.append(refs[p:p + cnt])
            p += cnt
        return out

    ni = [len(c.ins) for c in comms]
    no = [len(c.out_shapes) for c in comms]
    ns = [len(c.scratch) for c in comms]

    def first(in_refs, out_refs, sems):
        for c, i, o, s in zip(comms, split(in_refs, ni), split(out_refs, no), split(sems, ns)):
            c.first(i, o, s)

    def last(in_refs, out_refs, sems):
        for c, i, o, s in zip(comms, split(in_refs, ni), split(out_refs, no), split(sems, ns)):
            c.last(i, o, s)

    return _Comm([x for c in comms for x in c.ins], [x for c in comms for x in c.out_shapes],
                 [x for c in comms for x in c.scratch], first, last)


def _add_pairs(mine, theirs, core, name):
    _, rows, cols = mine.shape
    tr = _div_tile(rows, cols)

    def body(core_ref, a_ref, b_ref, o_ref):
        o_ref[...] = (a_ref[...].astype(F32) + b_ref[...].astype(F32)).astype(o_ref.dtype)

    return pl.pallas_call(
        body, name=name, out_shape=S((4, rows, cols), BF16),
        grid_spec=pltpu.PrefetchScalarGridSpec(
            num_scalar_prefetch=1, grid=(4, rows // tr),
            in_specs=[pl.BlockSpec((None, tr, cols), lambda k, i, c: (2 * k + c[0], i, 0)),
                      pl.BlockSpec((None, tr, cols), lambda k, i, c: (k, i, 0))],
            out_specs=pl.BlockSpec((None, tr, cols), lambda k, i, c: (k, i, 0))),
        compiler_params=_cp(("parallel", "parallel")))(core, mine, theirs)


def _run_comm(comm, name):
    ci, co = len(comm.ins), len(comm.out_shapes)

    def body(*refs):
        comm.first(refs[:ci], refs[ci:ci + co], refs[ci + co:])
        comm.last(refs[:ci], refs[ci:ci + co], refs[ci + co:])

    return pl.pallas_call(body, name=name, out_shape=list(comm.out_shapes), in_specs=[_HBM] * ci,
                          out_specs=[_HBM] * co, scratch_shapes=list(comm.scratch))(*comm.ins)


def _flat_rows(n_elems, mult):
    rows = -(-n_elems // LANE)
    return -(-rows // mult) * mult


def _pack(arrs, lead, mult, dtype):
    lead_shape = arrs[0].shape[:lead]
    flat = jnp.concatenate([a.astype(dtype).reshape(lead_shape + (-1,)) for a in arrs], axis=-1)
    n = flat.shape[-1]
    rows = _flat_rows(n, mult)
    flat = jnp.pad(flat, [(0, 0)] * lead + [(0, rows * LANE - n)])
    return flat.reshape(lead_shape + (rows, LANE))


def _unpack(flat, lead, shapes):
    lead_shape = flat.shape[:lead]
    flat = flat.reshape(lead_shape + (-1,))
    out, off = [], 0
    for shp in shapes:
        n = math.prod(shp)
        out.append(flat[..., off:off + n].reshape(lead_shape + tuple(shp)))
        off += n
    return out


def _split8(full, ax, n):
    shp = full.shape
    return jnp.moveaxis(full.reshape(shp[:ax] + (N_DEV, n) + shp[ax + 1:]), ax, 0)


def _join8(g, ax):
    shp = g.shape[1:]
    return jnp.moveaxis(g, 0, ax).reshape(shp[:ax] + (N_DEV * shp[ax],) + shp[ax + 1:])


def _group_lanes(v, hg):
    v = v.reshape(SSM_GROUPS, hg)
    return jnp.pad(v, ((0, 0), (0, LANE - hg))).reshape(1, SSM_GROUPS * LANE)


def _ungroup_lanes(v, hg):
    return v.reshape(SSM_GROUPS, LANE)[:, :hg].reshape(1, SSM_GROUPS * hg)


def kernel(x, meta_tokens, a_norm_pre, a_w_in, a_conv_w, a_conv_b, a_dt_bias, a_a_log, a_d_skip, a_gate_norm, a_w_out, a_norm_post, kv_norm, w_kv, b_norm_pre, b_w_q, b_sinks, b_w_o, b_norm_post, f_norm_pre, f_w_up, f_conv_w, f_conv_b, f_w_down, f_norm_post, loss_target, m_meta_tokens, m_a_norm_pre, m_a_w_in, m_a_conv_w, m_a_conv_b, m_a_dt_bias, m_a_a_log, m_a_d_skip, m_a_gate_norm, m_a_w_out, m_a_norm_post, m_kv_norm, m_w_kv, m_b_norm_pre, m_b_w_q, m_b_sinks, m_b_w_o, m_b_norm_post, m_f_norm_pre, m_f_w_up, m_f_conv_w, m_f_conv_b, m_f_w_down, m_f_norm_post, v_meta_tokens, v_a_norm_pre, v_a_w_in, v_a_conv_w, v_a_conv_b, v_a_dt_bias, v_a_a_log, v_a_d_skip, v_a_gate_norm, v_a_w_out, v_a_norm_post, v_kv_norm, v_w_kv, v_b_norm_pre, v_b_w_q, v_b_sinks, v_b_w_o, v_b_norm_post, v_f_norm_pre, v_f_w_up, v_f_conv_w, v_f_conv_b, v_f_w_down, v_f_norm_post):
    args = locals()
    wts = {n: args[n] for n in WEIGHTS}
    mom = {n: args["m_" + n] for n in WEIGHTS}
    var = {n: args["v_" + n] for n in WEIGHTS}
    mx, my, mc = lax.axis_index("x"), lax.axis_index("y"), lax.axis_index("c")
    me = 4 * mx + 2 * my + mc
    rows = _seq_rows()
    hg = SSM_HEADS // SSM_GROUPS
    d = D_MODEL

    n_main = D_INNER + D_XBC
    ws_in, ws_up = a_w_in.shape[2], f_w_up.shape[2]
    segs_in = _col_segments(ws_in, [(0, 0, n_main)] + [(n_main + hg * g, n_main + LANE * g, hg)
                                                      for g in range(SSM_GROUPS)])
    segs_up = _col_segments(ws_up, [(0, 0, 2 * D_FF)])
    def gather_of(*ws):
        return _gather_comm([w.astype(BF16) for w in ws])

    small_full, = _run_comm(_gather_comm([_pack([wts[n] for n in SMALL], 0, 8, F32)]), "gather_small")
    full = {}
    for n, g in zip(SMALL, _unpack(small_full, 1, [wts[n].shape for n in SMALL])):
        full[n] = _join8(g, SHARD_AXIS[n])
    (h0, hn0), (g_in,) = _embed_norm(full["meta_tokens"], x[0], full["a_norm_pre"], rows, "embed_norm",
                                     comm=gather_of(a_w_in[0]))
    w_in_all = _assemble_cols([g_in], n_main + SSM_GROUPS * LANE, segs_in, "asm_w_in")
    w_up, w_down = [None, None], [None, None]
    bias_g = _group_lanes(wts["a_dt_bias"], hg)
    alog_g = _group_lanes(wts["a_a_log"], hg)
    dsk_g = _group_lanes(wts["a_d_skip"], hg)
    a_conv_w, a_conv_b = full["a_conv_w"][0], full["a_conv_b"]
    f_cw, f_cb = full["f_conv_w"], wts["f_conv_b"]
    fpre, fpost = wts["f_norm_pre"], wts["f_norm_post"]

    tgt = jnp.pad(loss_target[0], ((N_META, rows - N_META - SEQ), (0, 0)))

    zx, (g_out,) = _mm(hn0, w_in_all, "nn", F32, "mm_in", comm=gather_of(a_w_out[0]))
    w_out = g_out.reshape(D_INNER, d)
    xbc, (g_dn1,) = _conv_silu_fwd(zx, a_conv_w, a_conv_b, "conv_a", comm=gather_of(f_w_down[1]))
    (y_ssd, hst), (g_up0,) = _ssd_fwd(xbc, zx, bias_g, alog_g, dsk_g, "ssd_fwd", comm=gather_of(f_w_up[0]))
    w_up[0] = _assemble_cols([g_up0], 2 * D_FF, segs_up, "asm_w_up0")
    yn, (g_kv, g_q) = _gatenorm_fwd(y_ssd, zx, full["a_gate_norm"], "gatenorm", comm=gather_of(w_kv, b_w_q[0]))
    mix_a, (g_o,) = _mm(yn, w_out, "nn", F32, "mm_out", comm=gather_of(b_w_o[0]))
    w_kvf, w_q, w_o = g_kv.reshape(d, 2 * D_KV), g_q.reshape(d, d), g_o.reshape(d, d)
    h1, (fn0,) = _resid_norm(h0, mix_a, full["a_norm_post"], [fpre[0:1]], "resid_a")

    half = d // 2
    u0, (g_dn0,) = _mm(fn0, w_up[0], "nn", F32, "mm_up0", comm=gather_of(f_w_down[0]))
    w_down = [g_dn0.reshape(D_FF, d), g_dn1.reshape(D_FF, d)]
    act0, (g_up1a,) = _ffn_act_fwd(u0, f_cw[0], f_cb[0:1], "ffn_act0", comm=gather_of(f_w_up[1, :half]))
    ffn0 = _mm(act0, w_down[0], "nn", F32, "mm_down0")
    h2, (kvn, bn) = _resid_norm(h1, ffn0, fpost[0:1], [wts["kv_norm"].reshape(1, d), wts["b_norm_pre"]], "resid_f0")
    kv = _mm(kvn, w_kvf, "nn", F32, "mm_kv")
    q = _mm(bn, w_q, "nn", F32, "mm_q")
    (o, lse), (g_up1b,) = _attn_fwd(q, kv, wts["b_sinks"], "attn_fwd", comm=gather_of(f_w_up[1, half:]))
    w_up[1] = _assemble_cols([g_up1a, g_up1b], 2 * D_FF, segs_up, "asm_w_up1")
    mix_b = _mm(o, w_o, "nn", F32, "mm_o")
    h3, (fn1,) = _resid_norm(h2, mix_b, wts["b_norm_post"], [fpre[1:2]], "resid_b")
    u1 = _mm(fn1, w_up[1], "nn", F32, "mm_up1")
    act1 = _ffn_act_fwd(u1, f_cw[1], f_cb[1:2], "ffn_act1")
    ffn1 = _mm(act1, w_down[1], "nn", F32, "mm_down1")
    dh4, loss_row = _final_loss(h3, ffn1, fpost[1:2], tgt, "loss")
    loss = lax.psum(loss_row[0, 0], ("x", "y", "c"))

    grads = {}

    core = mc.astype(jnp.int32).reshape(1)

    def carried(res, comm):
        return res if comm is not None else (res, None)

    def ffn_bwd(dh_out, h_in, fn, u, act, ffn, i, c_dact=None, c_dwdown=None):
        dffn, dw_post = _norm_bwd(ffn, fpost[i:i + 1], dh_out, None, BF16, f"nb_fpost{i}")
        dact, got_a = carried(_mm(dffn, w_down[i], "nt", F32, f"mm_dact{i}", comm=c_dact), c_dact)
        dw_down, got_b = carried(_mm(act, dffn, "tn", BF16, f"mm_dwdown{i}", comm=c_dwdown), c_dwdown)
        dw_down = dw_down.reshape(N_DEV, -1, d)
        du, dwc, dbc = _ffn_act_bwd(u, dact, f_cw[i], f_cb[i:i + 1], f"ffn_act_bwd{i}")
        dfn, (s_dn,) = _mm(du, w_up[i], "nt", F32, f"mm_dfn{i}", comm=_swap_comm([dw_down]))
        sum_dn = _add_pairs(dw_down, s_dn, core, f"rs_add_dn{i}")
        dw_up = _scatter_cols(_mm(fn, du, "tn", BF16, f"mm_dwup{i}"), ws_up, segs_up, f"scat_w_up{i}")
        (dh_in, dw_pre), (s_up,) = _norm_bwd(h_in, fpre[i:i + 1], dfn, dh_out, F32, f"nb_fpre{i}",
                                             comm=_swap_comm([dw_up]))
        sum_up = _add_pairs(dw_up, s_up, core, f"rs_add_up{i}")
        return dh_in, dict(post=dw_post, sum_down=sum_dn, cw=jnp.concatenate([dwc[0], dwc[1]], axis=1),
                           cb=jnp.concatenate([dbc[0], dbc[1]], axis=1), sum_up=sum_up, pre=dw_pre), got_a, got_b

    dh3, gf1, _, _ = ffn_bwd(dh4, h3, fn1, u1, act1, ffn1, 1)
    dmix_b, grads["b_norm_post"] = _norm_bwd(mix_b, wts["b_norm_post"], dh3, None, BF16, "nb_bpost")
    do = _mm(dmix_b, w_o, "nt", F32, "mm_do")
    dw_o = _mm(o, dmix_b, "tn", BF16, "mm_dwo").reshape(N_DEV, -1, d)
    (dq, dkv, dsinks), (p_up1, p_dn1, s_o) = _attn_bwd(
        q, kv, wts["b_sinks"], do, lse, "attn_bwd",
        comm=_join_comms([_chips_comm([gf1["sum_up"], gf1["sum_down"]]), _swap_comm([dw_o])]))
    sum_o = _add_pairs(dw_o, s_o, core, "rs_add_o")
    grads["b_sinks"] = dsinks[:, :N_Q_HEADS]
    dbn = _mm(dq, w_q, "nt", F32, "mm_dbn")
    dw_q = _mm(bn, dq, "tn", BF16, "mm_dwq").reshape(N_DEV, -1, d)
    dkv16 = dkv.astype(BF16)
    dkvn = _mm(dkv16, w_kvf, "nt", F32, "mm_dkvn")
    dw_kv = _mm(kvn, dkv16, "tn", BF16, "mm_dwkv").reshape(N_DEV, -1, 2 * D_KV)
    (dh2, grads["b_norm_pre"]), (s_q, s_kv) = _norm_bwd(h2, wts["b_norm_pre"], dbn, dh3, F32, "nb_bpre",
                                                        comm=_swap_comm([dw_q, dw_kv]))
    sum_q, sum_kv = _add_pairs(dw_q, s_q, core, "rs_add_q"), _add_pairs(dw_kv, s_kv, core, "rs_add_kv")
    dh2, dw_kvn = _norm_bwd(h2, wts["kv_norm"].reshape(1, d), dkvn, dh2, F32, "nb_kv")
    grads["kv_norm"] = dw_kvn.reshape(d)
    dh1, gf0, (p_o,), (p_q, p_kv) = ffn_bwd(dh2, h1, fn0, u0, act0, ffn0, 0, c_dact=_chips_comm([sum_o]),
                                            c_dwdown=_chips_comm([sum_q, sum_kv]))
    grads["f_norm_post"] = jnp.concatenate([gf0["post"], gf1["post"]], axis=0)
    grads["f_norm_pre"] = jnp.concatenate([gf0["pre"], gf1["pre"]], axis=0)
    grads["f_conv_w"] = jnp.stack([gf0["cw"], gf1["cw"]])
    grads["f_conv_b"] = jnp.concatenate([gf0["cb"], gf1["cb"]], axis=0)

    dmix_a, grads["a_norm_post"] = _norm_bwd(mix_a, full["a_norm_post"], dh1, None, BF16, "nb_apost")
    dyn = _mm(dmix_a, w_out, "nt", F32, "mm_dyn")
    dw_out = _mm(yn, dmix_a, "tn", BF16, "mm_dwout").reshape(N_DEV, -1, d)
    (dy_ssd, dz, grads["a_gate_norm"]), (s_out,) = _gatenorm_bwd(y_ssd, zx, full["a_gate_norm"], dyn, "gatenorm_bwd",
                                                                 comm=_swap_comm([dw_out]))
    sum_out = _add_pairs(dw_out, s_out, core, "rs_add_out")
    (dxs, dbm, dcm, ddtp, dalog, ddsk, dbias), (p_up0, p_dn0, p_out) = _ssd_bwd(
        xbc, zx, bias_g, alog_g, dsk_g, dy_ssd, hst, "ssd_bwd",
        comm=_chips_comm([gf0["sum_up"], gf0["sum_down"], sum_out]))
    grads["a_a_log"] = _ungroup_lanes(dalog, hg)
    grads["a_d_skip"] = _ungroup_lanes(ddsk, hg)
    grads["a_dt_bias"] = _ungroup_lanes(dbias, hg)
    dpre, dcw, dcb = _conv_silu_bwd(zx, dxs, dbm, dcm, a_conv_w, a_conv_b, "conv_a_bwd")
    grads["a_conv_w"], grads["a_conv_b"] = dcw[None], dcb
    dzx = jnp.concatenate([dz, dpre, ddtp.astype(BF16)], axis=1)
    dw_in8 = _scatter_cols(_mm(hn0, dzx, "tn", BF16, "mm_dwin"), ws_in, segs_in, "scat_w_in")
    dhn0, (s_in,) = _mm(dzx, w_in_all, "nt", F32, "mm_dhn0", comm=_swap_comm([dw_in8]))
    sum_in = _add_pairs(dw_in8, s_in, core, "rs_add_in")
    half_in = sum_in.shape[1] // 2
    (dh0, grads["a_norm_pre"]), (p_in_a,) = _norm_bwd(h0, full["a_norm_pre"], dhn0, dh1, F32, "nb_apre",
                                                      comm=_chips_comm([sum_in[:, :half_in]]))
    grad_x = dh0[N_META:N_META + SEQ][None]
    grads["meta_tokens"] = dh0[:N_META]

    small_local = _pack([_split8(grads[n], SHARD_AXIS[n], wts[n].shape[SHARD_AXIS[n]]) for n in SMALL], 1, 8, F32)
    repl_local = _pack([grads[n] for n in REPL], 0, 8, F32)
    n_sr = small_local.shape[1]
    small_vec = jnp.concatenate([small_local.reshape(N_DEV * n_sr, LANE), repl_local], axis=0)
    tail = _join_comms([_chips_comm([sum_in[:, half_in:]]), _gather_comm([small_vec])])
    parts_big = dict(a_w_out=[p_out], w_kv=[p_kv], b_w_q=[p_q], b_w_o=[p_o], f_w_down=[p_dn0, p_dn1])

    def flat_f32(dct, names, mult):
        return _pack([dct[n] for n in names], 0, mult, F32)

    def adamw_big(n, comm=None):
        shp3 = (len(parts_big[n]),) + parts_big[n][0].shape[1:]
        res = _adamw(parts_big[n], *[dct[n].reshape(shp3) for dct in (wts, mom, var)], f"adamw_{n}", comm=comm)
        res, got = res if comm is not None else (res, None)
        big_out[n] = [r.reshape(wts[n].shape) for r in res]
        return got

    big_out = {}
    def swap_last(a):
        return jnp.swapaxes(a, -1, -2)

    g_up_t = swap_last(_sum_parts([p_up0, p_up1], "sum_w_up").reshape(f_w_up.shape))
    res, (p_in_b, small_all) = _adamw([g_up_t[0:1], g_up_t[1:2]], *[swap_last(dct["f_w_up"]) for dct in (wts, mom, var)],
                                      "adamw_f_w_up", comm=tail)
    big_out["f_w_up"] = [swap_last(r) for r in res]
    for n in BIG:
        if n not in ("f_w_up", "a_w_in"):
            adamw_big(n)
    g_in_t = swap_last(_sum_parts([p_in_a, p_in_b], "sum_w_in"))[None]
    res = _adamw([g_in_t], *[swap_last(dct["a_w_in"]) for dct in (wts, mom, var)], "adamw_a_w_in")
    big_out["a_w_in"] = [swap_last(r) for r in res]
    mine_small = lax.dynamic_slice_in_dim(small_all, me * n_sr, n_sr, axis=1)
    parts_small = jnp.concatenate([mine_small, small_all[:, N_DEV * n_sr:]], axis=1)
    sm_in = [jnp.concatenate([flat_f32(dct, SMALL, 8), flat_f32(dct, REPL, 8)], axis=0)[None] for dct in (wts, mom, var)]
    small_out = [r[0] for r in _adamw([parts_small], *sm_in, "adamw_small")]

    outs = []
    for kind in range(4):
        res = {n: big_out[n][kind] for n in BIG}
        for n, a in zip(SMALL, _unpack(small_out[kind][:n_sr], 0, [wts[n].shape for n in SMALL])):
            res[n] = a
        for n, a in zip(REPL, _unpack(small_out[kind][n_sr:], 0, [wts[n].shape for n in REPL])):
            res[n] = a
        outs.append(res)
    return (loss, grad_x, *[outs[0][n] for n in WEIGHTS], *[outs[1][n] for n in WEIGHTS],
            *[outs[2][n] for n in WEIGHTS], *[outs[3][n] for n in WEIGHTS])
```

```python
import functools
import math

import jax
import jax.numpy as jnp
from jax import lax
from jax.experimental import pallas as pl
from jax.experimental.pallas import tpu as pltpu

F32, BF16 = jnp.float32, jnp.bfloat16
S = jax.ShapeDtypeStruct

D_MODEL = 1024
SEQ = 2048
N_META = 16
D_INNER = 2048
HEAD_P = 64
SSM_HEADS = D_INNER // HEAD_P
SSM_GROUPS = 4
D_STATE = 128
SSM_CONV = 4
D_BC = SSM_GROUPS * D_STATE
D_XBC = D_INNER + 2 * D_BC
ATTN_DH = 64
N_Q_HEADS = D_MODEL // ATTN_DH
N_KV_HEADS = 4
D_KV = N_KV_HEADS * ATTN_DH
WINDOW = 128
D_FF = 2816
FFN_CONV = 3
RMS_EPS = 1e-6
NEG = -1e30
LR, B1, B2, EPS, WD, STEP = 0.001, 0.9, 0.999, 1e-08, 0.01, 10

N_DEV = 8
T = 128
LANE = 128
VMEM_LIMIT = 48 * 1024 * 1024

BIG = ("a_w_in", "a_w_out", "w_kv", "b_w_q", "b_w_o", "f_w_up", "f_w_down")
SMALL = ("meta_tokens", "a_norm_pre", "a_conv_w", "a_conv_b", "a_gate_norm", "a_norm_post", "f_conv_w")
REPL = ("a_dt_bias", "a_a_log", "a_d_skip", "kv_norm", "b_norm_pre", "b_sinks", "b_norm_post",
        "f_norm_pre", "f_conv_b", "f_norm_post")
SHARD_AXIS = dict(a_w_in=2, a_w_out=1, w_kv=0, b_w_q=1, b_w_o=1, f_w_up=2, f_w_down=1, meta_tokens=1,
                  a_norm_pre=1, a_conv_w=2, a_conv_b=1, a_gate_norm=1, a_norm_post=1, f_conv_w=2)
WEIGHTS = ("meta_tokens", "a_norm_pre", "a_w_in", "a_conv_w", "a_conv_b", "a_dt_bias", "a_a_log", "a_d_skip",
           "a_gate_norm", "a_w_out", "a_norm_post", "kv_norm", "w_kv", "b_norm_pre", "b_w_q", "b_sinks", "b_w_o",
           "b_norm_post", "f_norm_pre", "f_w_up", "f_conv_w", "f_conv_b", "f_w_down", "f_norm_post")


def _seq_rows():
    return -(-(N_META + SEQ) // T) * T


def _cp(sem=None):
    return pltpu.CompilerParams(dimension_semantics=sem, vmem_limit_bytes=VMEM_LIMIT)


def _pick(n, target):
    t = min(n, target)
    t -= t % LANE
    while n % t:
        t -= LANE
    return t


def _sigmoid(x):
    return 0.5 * jnp.tanh(0.5 * x) + 0.5


def _softplus(x):
    return jnp.maximum(x, 0.0) + jnp.log(1.0 + jnp.exp(-jnp.abs(x)))


_NN = (((1,), (0,)), ((), ()))
_NT = (((1,), (1,)), ((), ()))
_TN = (((0,), (0,)), ((), ()))


def _dot(a, b, dims=_NN):
    return lax.dot_general(a, b, dims, preferred_element_type=F32)


def _dot_hi(a, b):
    return lax.dot_general(a, b, _NN, precision=lax.Precision.HIGHEST, preferred_element_type=F32)


_HBM = pl.BlockSpec(memory_space=pltpu.HBM)
_MESH = pl.DeviceIdType.MESH


class _Comm:
    def __init__(self, ins, out_shapes, scratch, first, last):
        self.ins, self.out_shapes, self.scratch, self.first, self.last = ins, out_shapes, scratch, first, last


def _call(body, name, out_shape, grid, in_specs, out_specs, sem, args, scratch=(), comm=None):
    if comm is None:
        return pl.pallas_call(body, name=name, out_shape=out_shape, grid=grid, in_specs=in_specs, out_specs=out_specs,
                              scratch_shapes=list(scratch), compiler_params=_cp(sem))(*args)
    single = not isinstance(out_shape, (list, tuple))
    outs = [out_shape] if single else list(out_shape)
    ospecs = [out_specs] if single else list(out_specs)
    n_in, n_out, n_scr, ci, co = len(in_specs), len(outs), len(scratch), len(comm.ins), len(comm.out_shapes)

    def carrier(*refs):
        p = 0
        parts = []
        for cnt in (n_in, ci, n_out, co, n_scr, len(comm.scratch)):
            parts.append(refs[p:p + cnt])
            p += cnt
        ins, cins, outs_r, couts, scr, cscr = parts
        ids = [pl.program_id(i) for i in range(len(grid))]
        first, last = ids[0] == 0, ids[0] == grid[0] - 1
        for i in range(1, len(grid)):
            first, last = first & (ids[i] == 0), last & (ids[i] == grid[i] - 1)

        @pl.when(first)
        def _():
            comm.first(cins, couts, cscr)

        body(*ins, *outs_r, *scr)

        @pl.when(last)
        def _():
            comm.last(cins, couts, cscr)

    res = pl.pallas_call(
        carrier, name=name, out_shape=outs + list(comm.out_shapes), grid=grid,
        in_specs=list(in_specs) + [_HBM] * ci, out_specs=ospecs + [_HBM] * co,
        scratch_shapes=list(scratch) + list(comm.scratch),
        compiler_params=_cp(("arbitrary",) * len(grid)))(*args, *comm.ins)
    mine = res[0] if single else list(res[:n_out])
    return mine, list(res[n_out:])


def _mm(a, b, mode, out_dtype, name, comm=None):
    if mode == "tn":
        m, kk = a.shape
        planes, width = (b.shape[0], b.shape[2]) if b.ndim == 3 else (1, b.shape[1])
        n = planes * width
        tko, tn = _pick(kk, 512), _pick(width, 1536)
        per = width // tn

        def body(a_ref, b_ref, o_ref):
            o_ref[...] = _dot(a_ref[...], b_ref[...], _TN).astype(o_ref.dtype)

        b_spec = (pl.BlockSpec((None, m, tn), lambda i, j: (j // per, 0, j % per)) if b.ndim == 3
                  else pl.BlockSpec((m, tn), lambda i, j: (0, j)))
        return _call(
            body, name, S((kk, n), out_dtype), (kk // tko, n // tn), [pl.BlockSpec((m, tko), lambda i, j: (0, i)), b_spec],
            pl.BlockSpec((tko, tn), lambda i, j: (i, j)), ("parallel", "parallel"), (a, b), comm=comm)

    planes, width = (a.shape[0], a.shape[2]) if a.ndim == 3 else (1, a.shape[1])
    m, kk = a.shape[-2], planes * width
    n = b.shape[1] if mode == "nn" else b.shape[0]
    dims = _NN if mode == "nn" else _NT

    if kk > 2048:
        tm = m // 4
        assert m % 4 == 0 and tm % 16 == 0

        def body(a_ref, b_ref, o_ref):
            if a.ndim == 2:
                res = _dot(a_ref[...], b_ref[...], dims)
            else:
                res = None
                for p in range(planes):
                    bp = b_ref[p * width:(p + 1) * width, :] if mode == "nn" else b_ref[:, p * width:(p + 1) * width]
                    part = _dot(a_ref[p], bp, dims)
                    res = part if res is None else res + part
            o_ref[...] = res.astype(o_ref.dtype)

        a_spec = (pl.BlockSpec((planes, tm, width), lambda i: (0, i, 0)) if a.ndim == 3
                  else pl.BlockSpec((tm, kk), lambda i: (i, 0)))
        return _call(
            body, name, S((m, n), out_dtype), (m // tm,),
            [a_spec, pl.BlockSpec(b.shape, lambda i: (0, 0), pipeline_mode=pl.Buffered(1))],
            pl.BlockSpec((tm, n), lambda i: (i, 0)), ("parallel",), (a, b), comm=comm)

    tn = _pick(n, 512)

    def body(a_ref, b_ref, o_ref):
        o_ref[...] = _dot(a_ref[...], b_ref[...], dims).astype(o_ref.dtype)

    b_spec = (pl.BlockSpec((kk, tn), lambda j: (0, j)) if mode == "nn" else pl.BlockSpec((tn, kk), lambda j: (j, 0)))
    return _call(
        body, name, S((m, n), out_dtype), (n // tn,), [pl.BlockSpec((m, kk), lambda j: (0, 0)), b_spec],
        pl.BlockSpec((m, tn), lambda j: (0, j)), ("parallel",), (a, b), comm=comm)


def _rms(x, w):
    return x * lax.rsqrt(jnp.mean(x * x, axis=-1, keepdims=True) + RMS_EPS) * w


def _row_tile(rows):
    return rows // 8


def _embed_norm(meta, x, w, rows, name, comm=None):
    n_meta, d = meta.shape
    n_x = x.shape[0]
    last = rows // T - 1
    assert n_meta % 8 == 0 and n_meta < T and n_meta + n_x == last * T + n_meta and last * T >= n_x

    def body(m_ref, x_ref, w_ref, h_ref, hn_ref):
        i = pl.program_id(0)

        @pl.when(i == 0)
        def _():
            h_ref[0:n_meta, :] = m_ref[...]
            h_ref[n_meta:T, :] = x_ref[0:T - n_meta, :]

        @pl.when((i > 0) & (i < last))
        def _():
            h_ref[...] = x_ref[pl.ds(pl.multiple_of(i * T - n_meta, 8), T), :]

        @pl.when(i == last)
        def _():
            h_ref[0:n_meta, :] = x_ref[n_x - n_meta:n_x, :]
            h_ref[n_meta:T, :] = jnp.zeros((T - n_meta, d), F32)

        hn_ref[...] = _rms(h_ref[...], w_ref[...]).astype(hn_ref.dtype)

    row = pl.BlockSpec((T, d), lambda i: (i, 0))
    return _call(body, name, [S((rows, d), F32), S((rows, d), BF16)], (rows // T,),
                 [pl.BlockSpec((n_meta, d), lambda i: (0, 0)), pl.BlockSpec((n_x, d), lambda i: (0, 0)),
                  pl.BlockSpec((1, d), lambda i: (0, 0))], [row, row], ("parallel",), (meta, x, w), comm=comm)


def _resid_norm(h, br, w_post, next_ws, name):
    rows, d = h.shape
    tr = _row_tile(rows)
    has_br = br is not None
    nw = len(next_ws)

    def body(*refs):
        h_ref = refs[0]
        pos = 1
        x = h_ref[...]
        if has_br:
            x = x + _rms(refs[1][...], refs[2][...])
            pos = 3
        w_refs = refs[pos:pos + nw]
        outs = refs[pos + nw:]
        if has_br:
            outs[0][...] = x
            outs = outs[1:]
        for w_ref, o_ref in zip(w_refs, outs):
            o_ref[...] = _rms(x, w_ref[...]).astype(o_ref.dtype)

    row = pl.BlockSpec((tr, d), lambda i: (i, 0))
    vec = pl.BlockSpec((1, d), lambda i: (0, 0))
    ins = [h] + ([br, w_post] if has_br else []) + list(next_ws)
    in_specs = [row] + ([row, vec] if has_br else []) + [vec] * nw
    out_shape = ([S((rows, d), F32)] if has_br else []) + [S((rows, d), BF16)] * nw
    res = pl.pallas_call(body, name=name, out_shape=out_shape, grid=(rows // tr,), in_specs=in_specs,
                         out_specs=[row] * len(out_shape), compiler_params=_cp(("parallel",)))(*ins)
    if has_br:
        return res[0], list(res[1:])
    return h, list(res)


def _norm_bwd(x, w, dy, add, out_dtype, name, comm=None):
    rows, d = x.shape
    tr = _row_tile(rows)
    has_add = add is not None

    def body(*refs):
        x_ref, w_ref, dy_ref = refs[:3]
        dx_ref, dw_ref = refs[-2:]
        xv = x_ref[...]
        r = lax.rsqrt(jnp.mean(xv * xv, axis=-1, keepdims=True) + RMS_EPS)
        dyv = dy_ref[...].astype(F32)
        wdy = dyv * w_ref[...]
        dx = r * wdy - xv * (r * r * r) * jnp.mean(xv * wdy, axis=-1, keepdims=True)
        if has_add:
            dx = dx + refs[3][...]
        dx_ref[...] = dx.astype(dx_ref.dtype)

        @pl.when(pl.program_id(0) == 0)
        def _():
            dw_ref[...] = jnp.zeros_like(dw_ref)

        dw_ref[...] += jnp.sum(dyv * xv * r, axis=0, keepdims=True)

    row = pl.BlockSpec((tr, d), lambda i: (i, 0))
    vec = pl.BlockSpec((1, d), lambda i: (0, 0))
    ins = [x, w, dy] + ([add] if has_add else [])
    return _call(body, name, [S((rows, d), out_dtype), S((1, d), F32)], (rows // tr,),
                 [row, vec, row] + ([row] if has_add else []), [row, vec], ("arbitrary",), ins, comm=comm)


def _final_loss(h, br, w_post, tgt, name):
    rows, d = h.shape
    tr = _row_tile(rows)

    def body(h_ref, br_ref, w_ref, t_ref, dh_ref, loss_ref):
        i = pl.program_id(0)
        y = h_ref[...] + _rms(br_ref[...], w_ref[...])
        r = i * tr + lax.broadcasted_iota(jnp.int32, (tr, 1), 0)
        real = (r >= N_META) & (r < N_META + SEQ)
        diff = jnp.where(real, y - t_ref[...], 0.0)
        dh_ref[...] = diff * (1.0 / d)

        @pl.when(i == 0)
        def _():
            loss_ref[...] = jnp.zeros_like(loss_ref)

        loss_ref[...] += jnp.sum(diff * diff) * (0.5 / d)

    row = pl.BlockSpec((tr, d), lambda i: (i, 0))
    return pl.pallas_call(body, name=name, out_shape=[S((rows, d), F32), S((1, LANE), F32)], grid=(rows // tr,),
                          in_specs=[row, row, pl.BlockSpec((1, d), lambda i: (0, 0)), row],
                          out_specs=[row, pl.BlockSpec((1, LANE), lambda i: (0, 0))],
                          compiler_params=_cp(("arbitrary",)))(h, br, w_post, tgt)


def _gatenorm_fwd(y, zx, w, name, comm=None):
    rows, d = y.shape
    tr = _row_tile(rows)

    def body(y_ref, z_ref, w_ref, o_ref):
        z = z_ref[...]
        o_ref[...] = _rms(y_ref[...] * z * _sigmoid(z), w_ref[...]).astype(o_ref.dtype)

    row = pl.BlockSpec((tr, d), lambda i: (i, 0))
    return _call(body, name, S((rows, d), BF16), (rows // tr,), [row, row, pl.BlockSpec((1, d), lambda i: (0, 0))],
                 row, ("parallel",), (y, zx, w), comm=comm)


def _gatenorm_bwd(y, zx, w, dyn, name, comm=None):
    rows, d = y.shape
    tr = _row_tile(rows)

    def body(y_ref, z_ref, w_ref, dyn_ref, dy_ref, dz_ref, dw_ref):
        yv, z = y_ref[...], z_ref[...]
        sg = _sigmoid(z)
        sz = z * sg
        g = yv * sz
        r = lax.rsqrt(jnp.mean(g * g, axis=-1, keepdims=True) + RMS_EPS)
        dyn_v = dyn_ref[...]
        wdy = dyn_v * w_ref[...]
        dg = r * wdy - g * (r * r * r) * jnp.mean(g * wdy, axis=-1, keepdims=True)
        dy_ref[...] = dg * sz
        dz_ref[...] = (dg * yv * sg * (1.0 + z * (1.0 - sg))).astype(dz_ref.dtype)

        @pl.when(pl.program_id(0) == 0)
        def _():
            dw_ref[...] = jnp.zeros_like(dw_ref)

        dw_ref[...] += jnp.sum(dyn_v * g * r, axis=0, keepdims=True)

    row = pl.BlockSpec((tr, d), lambda i: (i, 0))
    vec = pl.BlockSpec((1, d), lambda i: (0, 0))
    return _call(body, name, [S((rows, d), F32), S((rows, d), BF16), S((1, d), F32)], (rows // tr,),
                 [row, row, vec, row], [row, row, vec], ("arbitrary",), (y, zx, w, dyn), comm=comm)


def _shift_down(x, s, rows_iota):
    if s == 0:
        return x
    return jnp.where(rows_iota >= s, pltpu.roll(x, s, 0), 0.0)


def _shift_up(x, s, rows_iota):
    if s == 0:
        return x
    rows = x.shape[0]
    return jnp.where(rows_iota < rows - s, pltpu.roll(x, rows - s, 0), 0.0)


def _r16(v):
    return v.astype(BF16).astype(F32)


def _conv(x, w_ref, b_ref, taps, rows_iota):
    x = _r16(x)
    acc = jnp.zeros_like(x)
    for k in range(taps):
        acc = acc + _r16(w_ref[k:k + 1, :]) * _shift_down(x, taps - 1 - k, rows_iota)
    return acc + b_ref[...]


def _conv_bwd(x, du, w_ref, dw_ref, db_ref, taps, rows_iota):
    db_ref[...] = jnp.sum(du, axis=0, keepdims=True)
    x, du = _r16(x), _r16(du)
    dx = jnp.zeros_like(x)
    for k in range(taps):
        s = taps - 1 - k
        dx = dx + _r16(w_ref[k:k + 1, :]) * _shift_up(du, s, rows_iota)
        dw_ref[k:k + 1, :] = jnp.sum(du * _shift_down(x, s, rows_iota), axis=0, keepdims=True)
    return dx


def _conv_silu_fwd(zx, w, b, name, comm=None):
    rows = zx.shape[0]
    cb = 512
    off = D_INNER // cb

    def body(x_ref, w_ref, b_ref, o_ref):
        it = lax.broadcasted_iota(jnp.int32, (rows, 1), 0)
        u = _conv(x_ref[...], w_ref, b_ref, SSM_CONV, it)
        o_ref[...] = u * _sigmoid(u)

    return _call(
        body, name, S((rows, D_XBC), F32), (D_XBC // cb,),
        [pl.BlockSpec((rows, cb), lambda j: (0, off + j)), pl.BlockSpec((SSM_CONV, cb), lambda j: (0, j)),
         pl.BlockSpec((1, cb), lambda j: (0, j))],
        pl.BlockSpec((rows, cb), lambda j: (0, j)), ("parallel",), (zx, w, b), comm=comm)


def _conv_silu_bwd(zx, dxs, dbm, dcm, w, b, name):
    rows = zx.shape[0]
    cb = 256
    off = D_INNER // cb
    nx, nbc = D_INNER // cb, D_BC // cb

    def body(x_ref, dx_in, db_in, dc_in, w_ref, b_ref, dx_ref, dw_ref, db_ref, dbuf):
        j = pl.program_id(0)
        for cond, src in ((j < nx, dx_in), ((j >= nx) & (j < nx + nbc), db_in), (j >= nx + nbc, dc_in)):
            @pl.when(cond)
            def _(src=src):
                dbuf[...] = src[...]
        it = lax.broadcasted_iota(jnp.int32, (rows, 1), 0)
        x = x_ref[...]
        u = _conv(x, w_ref, b_ref, SSM_CONV, it)
        sg = _sigmoid(u)
        du = dbuf[...] * sg * (1.0 + u * (1.0 - sg))
        dx_ref[...] = _conv_bwd(x, du, w_ref, dw_ref, db_ref, SSM_CONV, it).astype(dx_ref.dtype)

    def part(first, count):
        return pl.BlockSpec((rows, cb), lambda j: (0, jnp.clip(j - first, 0, count - 1)))

    col = pl.BlockSpec((rows, cb), lambda j: (0, j))
    wsp = pl.BlockSpec((SSM_CONV, cb), lambda j: (0, j))
    bsp = pl.BlockSpec((1, cb), lambda j: (0, j))
    return pl.pallas_call(
        body, name=name, out_shape=[S((rows, D_XBC), BF16), S((SSM_CONV, D_XBC), F32), S((1, D_XBC), F32)],
        grid=(D_XBC // cb,),
        in_specs=[pl.BlockSpec((rows, cb), lambda j: (0, off + j)), part(0, nx), part(nx, nbc), part(nx + nbc, nbc),
                  wsp, bsp],
        out_specs=[col, wsp, bsp], scratch_shapes=[pltpu.VMEM((rows, cb), F32)],
        compiler_params=_cp(("arbitrary",)))(zx, dxs, dbm, dcm, w, b)


def _ffn_act_fwd(u, w, b, name, comm=None):
    rows = u.shape[0]
    cb = 256
    nb = D_FF // cb

    def body(g_ref, v_ref, wg_ref, wv_ref, bg_ref, bv_ref, o_ref):
        it = lax.broadcasted_iota(jnp.int32, (rows, 1), 0)
        g = _conv(g_ref[...], wg_ref, bg_ref, FFN_CONV, it)
        v = _conv(v_ref[...], wv_ref, bv_ref, FFN_CONV, it)
        o_ref[...] = (g * _sigmoid(g) * v).astype(o_ref.dtype)

    def sp(r, shift):
        return pl.BlockSpec((r, cb), lambda j: (0, shift + j))

    return _call(
        body, name, S((rows, D_FF), BF16), (nb,),
        [sp(rows, 0), sp(rows, nb), sp(FFN_CONV, 0), sp(FFN_CONV, nb), sp(1, 0), sp(1, nb)],
        sp(rows, 0), ("parallel",), (u, u, w, w, b, b), comm=comm)


def _ffn_act_bwd(u, dact, w, b, name, comm=None):
    rows = u.shape[0]
    cb = 256
    nb = D_FF // cb

    def body(g_ref, v_ref, d_ref, wg_ref, wv_ref, bg_ref, bv_ref, du_ref, dw_ref, db_ref):
        it = lax.broadcasted_iota(jnp.int32, (rows, 1), 0)
        xg, xv = g_ref[...], v_ref[...]
        g = _conv(xg, wg_ref, bg_ref, FFN_CONV, it)
        v = _conv(xv, wv_ref, bv_ref, FFN_CONV, it)
        sg = _sigmoid(g)
        d = d_ref[...]
        dgate = d * v * sg * (1.0 + g * (1.0 - sg))
        dval = d * g * sg
        du_ref[0] = _conv_bwd(xg, dgate, wg_ref, dw_ref.at[0], db_ref.at[0], FFN_CONV, it).astype(du_ref.dtype)
        du_ref[1] = _conv_bwd(xv, dval, wv_ref, dw_ref.at[1], db_ref.at[1], FFN_CONV, it).astype(du_ref.dtype)

    def sp(r, shift):
        return pl.BlockSpec((r, cb), lambda j: (0, shift + j))

    def both(r):
        return pl.BlockSpec((2, r, cb), lambda j: (0, 0, j))

    return _call(
        body, name, [S((2, rows, D_FF), BF16), S((2, FFN_CONV, D_FF), F32), S((2, 1, D_FF), F32)], (nb,),
        [sp(rows, 0), sp(rows, nb), sp(rows, 0), sp(FFN_CONV, 0), sp(FFN_CONV, nb), sp(1, 0), sp(1, nb)],
        [both(rows), both(FFN_CONV), both(1)], ("parallel",), (u, u, dact, w, w, b, b), comm=comm)


def _ssd_consts(dtp_ref, bias_ref, alog_ref, hg):
    lane = lax.broadcasted_iota(jnp.int32, (1, LANE), 1)
    pre = dtp_ref[...] + bias_ref[...]
    dt = _softplus(pre)
    a_row = jnp.where(lane < hg, -jnp.exp(alog_ref[...]), 0.0)
    ri = lax.broadcasted_iota(jnp.int32, (T, T), 0)
    ci = lax.broadcasted_iota(jnp.int32, (T, T), 1)
    cs = _dot_hi((ri >= ci).astype(F32), dt * a_row)
    return pre, dt, a_row, cs, ri, ci, lane


def _ssd_fwd(xbc, zx, bias, alog, dsk, name, comm=None):
    rows = xbc.shape[0]
    nc = rows // T
    hg = SSM_HEADS // SSM_GROUPS
    gw = hg * HEAD_P
    xoff, boff, coff = 0, D_INNER // D_STATE, (D_INNER + D_BC) // D_STATE
    dtoff = (D_INNER + D_XBC) // LANE

    def body(x_ref, b_ref, c_ref, dtp_ref, bias_ref, alog_ref, dsk_ref, y_ref, hst_ref, hs):
        c = pl.program_id(1)

        @pl.when(c == 0)
        def _():
            hs[...] = jnp.zeros_like(hs)

        _, dt, _, cs, ri, ci, _ = _ssd_consts(dtp_ref, bias_ref, alog_ref, hg)
        cst, dtt = cs.T, dt.T
        xt = x_ref[...].T
        bb, cbf = b_ref[...].astype(BF16), c_ref[...].astype(BF16)
        gt = _dot(bb, cbf, _NT)
        causal_t = ci >= ri
        dskv = dsk_ref[...]
        hall = hs[...]
        hst_ref[0, 0] = hall
        yts, new_h = [], []
        for k in range(hg):
            sl = slice(k * HEAD_P, (k + 1) * HEAD_P)
            csc, csr = cs[:, k:k + 1], cst[k:k + 1, :]
            lt = jnp.exp(jnp.where(causal_t, csr - csc, NEG))
            xk = xt[sl, :]
            xdt = xk * dtt[k:k + 1, :]
            hk = hall[sl, :]
            yd = _dot(xdt.astype(BF16), (gt * lt).astype(BF16))
            yo = jnp.exp(csr) * _dot(hk.astype(BF16), cbf, _NT)
            yts.append(yd + yo + dskv[:, k:k + 1] * xk)
            cl = cs[T - 1:T, k:k + 1]
            st = _dot((xdt * jnp.exp(cl - csr)).astype(BF16), bb)
            new_h.append(jnp.exp(cl) * hk + st)
        y_ref[...] = jnp.concatenate(yts, axis=0).T
        hs[...] = jnp.concatenate(new_h, axis=0)

    vec = pl.BlockSpec((1, LANE), lambda g, c: (0, g))
    return _call(
        body, name, [S((rows, D_INNER), F32), S((nc, SSM_GROUPS, gw, D_STATE), F32)], (SSM_GROUPS, nc),
        [pl.BlockSpec((T, gw), lambda g, c: (c, xoff + g)),
         pl.BlockSpec((T, D_STATE), lambda g, c: (c, boff + g)),
         pl.BlockSpec((T, D_STATE), lambda g, c: (c, coff + g)),
         pl.BlockSpec((T, LANE), lambda g, c: (c, dtoff + g)), vec, vec, vec],
        [pl.BlockSpec((T, gw), lambda g, c: (c, g)), pl.BlockSpec((1, 1, gw, D_STATE), lambda g, c: (c, g, 0, 0))],
        ("parallel", "arbitrary"), (xbc, xbc, xbc, zx, bias, alog, dsk),
        scratch=[pltpu.VMEM((gw, D_STATE), F32)], comm=comm)


def _ssd_bwd(xbc, zx, bias, alog, dsk, dy, hst, name, comm=None):
    rows = xbc.shape[0]
    nc = rows // T
    hg = SSM_HEADS // SSM_GROUPS
    gw = hg * HEAD_P
    boff, coff = D_INNER // D_STATE, (D_INNER + D_BC) // D_STATE
    dtoff = (D_INNER + D_XBC) // LANE

    def body(x_ref, b_ref, c_ref, dtp_ref, bias_ref, alog_ref, dsk_ref, dy_ref, hst_ref,
             dx_ref, db_ref, dc_ref, ddtp_ref, dalog_ref, ddsk_ref, dbias_ref, dhs):
        step = pl.program_id(1)

        @pl.when(step == 0)
        def _():
            dhs[...] = jnp.zeros_like(dhs)
            dalog_ref[...] = jnp.zeros_like(dalog_ref)
            ddsk_ref[...] = jnp.zeros_like(ddsk_ref)
            dbias_ref[...] = jnp.zeros_like(dbias_ref)

        pre, dt, a_row, cs, ri, ci, lane = _ssd_consts(dtp_ref, bias_ref, alog_ref, hg)
        cst, dtt = cs.T, dt.T
        xt, dyt = x_ref[...].T, dy_ref[...].T
        bb, cbf = b_ref[...].astype(BF16), c_ref[...].astype(BF16)
        gt = _dot(bb, cbf, _NT)
        causal_t = ci >= ri
        dskv = dsk_ref[...]
        hall, dhall = hst_ref[0, 0], dhs[...]
        head_row = lax.broadcasted_iota(jnp.int32, (T, 1), 0)
        last_l = lax.broadcasted_iota(jnp.int32, (1, T), 1) == T - 1
        dgt = jnp.zeros((T, T), F32)
        dc_acc = jnp.zeros((T, D_STATE), F32)
        db_acc = jnp.zeros((T, D_STATE), F32)
        ddt_rows = jnp.zeros((T, T), F32)
        dcs_rows = jnp.zeros((T, T), F32)
        qrow_cols = jnp.zeros((T, LANE), F32)
        ddsk_acc = jnp.zeros((1, LANE), F32)
        dxts, new_dh = [], []
        for k in range(hg):
            sl = slice(k * HEAD_P, (k + 1) * HEAD_P)
            csc, csr = cs[:, k:k + 1], cst[k:k + 1, :]
            lt = jnp.exp(jnp.where(causal_t, csr - csc, NEG))
            xk, dyk = xt[sl, :], dyt[sl, :]
            dtr, dk = dtt[k:k + 1, :], dskv[:, k:k + 1]
            xdt = xk * dtr
            mpt = gt * lt
            dyb = dyk.astype(BF16)
            dxdt = _dot(dyb, mpt.astype(BF16), _NT)
            dmt = _dot(xdt.astype(BF16), dyb, _TN)
            dgt = dgt + dmt * lt
            q = dmt * mpt
            q_rows = jnp.sum(q, axis=1, keepdims=True)
            q_cols = jnp.sum(q, axis=0, keepdims=True)
            hk, dhn = hall[sl, :], dhall[sl, :]
            e = jnp.exp(csr)
            cl = cs[T - 1:T, k:k + 1]
            wdec = jnp.exp(cl - csr)
            w = wdec * dtr
            rt = _dot(dhn.astype(BF16), bb, _NT)
            dxts.append(dtr * dxdt + dk * dyk + rt * w)
            xz = jnp.sum(xk * dxdt, axis=0, keepdims=True)
            dw = jnp.sum(rt * xk, axis=0, keepdims=True)
            dcl = jnp.exp(cl) * jnp.sum(dhn * hk) + jnp.sum(dw * w)
            yo = e * _dot(hk.astype(BF16), cbf, _NT)
            dcs_r = jnp.sum(dyk * yo, axis=0, keepdims=True) + q_cols - dw * w + jnp.where(last_l, dcl, 0.0)
            dye = (dyk * e).astype(BF16)
            dc_acc = dc_acc + _dot(dye, hk.astype(BF16), _TN)
            db_acc = db_acc + _dot((xk * w).astype(BF16), dhn.astype(BF16), _TN)
            new_dh.append(jnp.exp(cl) * dhn + _dot(dye, cbf))
            onehot = (lane == k).astype(F32)
            ddt_rows = ddt_rows + jnp.where(head_row == k, xz + dw * wdec, 0.0)
            dcs_rows = dcs_rows + jnp.where(head_row == k, dcs_r, 0.0)
            qrow_cols = qrow_cols + q_rows * onehot
            ddsk_acc = ddsk_acc + jnp.sum(dyk * xk) * onehot
        dx_ref[...] = jnp.concatenate(dxts, axis=0).T
        dhs[...] = jnp.concatenate(new_dh, axis=0)
        dc_ref[...] = _dot(dgt.T.astype(BF16), bb) + dc_acc
        db_ref[...] = _dot(dgt.astype(BF16), cbf) + db_acc
        da = _dot_hi((ci >= ri).astype(F32), dcs_rows.T - qrow_cols)
        ddtp = (ddt_rows.T + da * a_row) * _sigmoid(pre)
        ddtp = jnp.where(lane < hg, ddtp, 0.0)
        ddtp_ref[...] = ddtp
        dbias_ref[...] += jnp.sum(ddtp, axis=0, keepdims=True)
        dalog_ref[...] += jnp.sum(da * dt, axis=0, keepdims=True) * a_row
        ddsk_ref[...] += ddsk_acc

    def rc(c):
        return nc - 1 - c

    vec = pl.BlockSpec((1, LANE), lambda g, c: (0, g))
    xsp = pl.BlockSpec((T, gw), lambda g, c: (rc(c), g))
    return _call(
        body, name,
        [S((rows, D_INNER), F32), S((rows, D_BC), F32), S((rows, D_BC), F32),
         S((rows, SSM_GROUPS * LANE), F32), S((1, SSM_GROUPS * LANE), F32),
         S((1, SSM_GROUPS * LANE), F32), S((1, SSM_GROUPS * LANE), F32)],
        (SSM_GROUPS, nc),
        [xsp,
         pl.BlockSpec((T, D_STATE), lambda g, c: (rc(c), boff + g)),
         pl.BlockSpec((T, D_STATE), lambda g, c: (rc(c), coff + g)),
         pl.BlockSpec((T, LANE), lambda g, c: (rc(c), dtoff + g)), vec, vec, vec,
         xsp, pl.BlockSpec((1, 1, gw, D_STATE), lambda g, c: (rc(c), g, 0, 0))],
        [xsp,
         pl.BlockSpec((T, D_STATE), lambda g, c: (rc(c), g)),
         pl.BlockSpec((T, D_STATE), lambda g, c: (rc(c), g)),
         pl.BlockSpec((T, LANE), lambda g, c: (rc(c), g)), vec, vec, vec],
        ("parallel", "arbitrary"), (xbc, xbc, xbc, zx, bias, alog, dsk, dy, hst),
        scratch=[pltpu.VMEM((gw, D_STATE), F32)], comm=comm)


def _attn_tiles(kv_ref, j):
    prev = jnp.maximum(j - 1, 0)
    meta = kv_ref[0:T, :]
    prv = kv_ref[pl.ds(pl.multiple_of(prev * T, T), T), :]
    cur = kv_ref[pl.ds(pl.multiple_of(j * T, T), T), :]
    return jnp.concatenate([meta, prv, cur], axis=0)


def _attn_mask(j):
    r = j * T + lax.broadcasted_iota(jnp.int32, (3 * T, T), 1)
    row = lax.broadcasted_iota(jnp.int32, (3 * T, T), 0)
    t0, t1 = row < T, row < 2 * T
    s = jnp.where(t0, row, (j - 2) * T + row)
    ok = (s <= r) & ((s < N_META) | (s > r - WINDOW))
    use = (t0 & (j >= 2) & (row < N_META)) | (jnp.logical_not(t0) & t1 & (j >= 1)) | jnp.logical_not(t1)
    return ok & use


def _attn_fwd(q, kv, sinks, name, comm=None):
    rows = q.shape[0]
    scale = 1.0 / math.sqrt(ATTN_DH)
    qpk = N_Q_HEADS // N_KV_HEADS

    def body(q_ref, kv_ref, s_ref, o_ref, lse_ref):
        j = pl.program_id(0)
        kv3 = _attn_tiles(kv_ref, j).astype(BF16)
        mask = _attn_mask(j)
        qv = (q_ref[...] * scale).astype(BF16)
        sk = s_ref[...]
        lses = []
        for kh in range(N_KV_HEADS):
            k3 = kv3[:, kh * ATTN_DH:(kh + 1) * ATTN_DH]
            v3 = kv3[:, D_KV + kh * ATTN_DH:D_KV + (kh + 1) * ATTN_DH]
            for g in range(qpk):
                h = kh * qpk + g
                sink = sk[:, h:h + 1]
                sc = jnp.where(mask, _dot(k3, qv[:, h * ATTN_DH:(h + 1) * ATTN_DH], _NT), NEG)
                m = jnp.maximum(jnp.max(sc, axis=0, keepdims=True), sink)
                p = jnp.exp(sc - m)
                den = jnp.sum(p, axis=0, keepdims=True) + jnp.exp(sink - m)
                p = p * (1.0 / den)
                lses.append(m + jnp.log(den))
                o_ref[:, h * ATTN_DH:(h + 1) * ATTN_DH] = _dot(p.astype(BF16), v3, _TN).astype(o_ref.dtype)
        lse_ref[...] = jnp.concatenate(lses, axis=0)

    return _call(
        body, name, [S((rows, D_MODEL), BF16), S((N_Q_HEADS, rows), F32)], (rows // T,),
        [pl.BlockSpec((T, D_MODEL), lambda j: (j, 0)), pl.BlockSpec((rows, 2 * D_KV), lambda j: (0, 0)),
         pl.BlockSpec((1, N_Q_HEADS), lambda j: (0, 0))],
        [pl.BlockSpec((T, D_MODEL), lambda j: (j, 0)), pl.BlockSpec((N_Q_HEADS, T), lambda j: (0, j))],
        ("parallel",), (q, kv, sinks), comm=comm)


def _attn_bwd(q, kv, sinks, do, lse, name, comm=None):
    rows = q.shape[0]
    scale = 1.0 / math.sqrt(ATTN_DH)
    qpk = N_Q_HEADS // N_KV_HEADS

    def body(q_ref, kv_ref, s_ref, do_ref, lse_ref, dq_ref, dkv_ref, ds_ref):
        j = pl.program_id(0)

        @pl.when(j == 0)
        def _():
            dkv_ref[...] = jnp.zeros_like(dkv_ref)
            ds_ref[...] = jnp.zeros_like(ds_ref)

        kv3 = _attn_tiles(kv_ref, j).astype(BF16)
        mask = _attn_mask(j)
        qv = (q_ref[...] * scale).astype(BF16)
        dov = do_ref[...].astype(BF16)
        sk = s_ref[...]
        lsev = lse_ref[...]
        lane = lax.broadcasted_iota(jnp.int32, (1, LANE), 1)
        ds_acc = jnp.zeros((1, LANE), F32)
        prev = jnp.maximum(j - 1, 0)
        dqts = []
        for kh in range(N_KV_HEADS):
            ksl = slice(kh * ATTN_DH, (kh + 1) * ATTN_DH)
            vsl = slice(D_KV + kh * ATTN_DH, D_KV + (kh + 1) * ATTN_DH)
            k3, v3 = kv3[:, ksl], kv3[:, vsl]
            k3t = k3.T
            dk3 = jnp.zeros((3 * T, ATTN_DH), F32)
            dv3 = jnp.zeros((3 * T, ATTN_DH), F32)
            for g in range(qpk):
                h = kh * qpk + g
                hs = slice(h * ATTN_DH, (h + 1) * ATTN_DH)
                qh, doh = qv[:, hs], dov[:, hs]
                lh = lsev[h:h + 1, :]
                p = jnp.exp(jnp.where(mask, _dot(k3, qh, _NT), NEG) - lh)
                ps = jnp.exp(sk[:, h:h + 1] - lh)
                dp = _dot(v3, doh, _NT)
                delta = jnp.sum(p * dp, axis=0, keepdims=True)
                dsc = (p * (dp - delta)).astype(BF16)
                dqts.append(_dot(k3t, dsc) * scale)
                dk3 = dk3 + _dot(dsc, qh)
                dv3 = dv3 + _dot(p.astype(BF16), doh)
                ds_acc = ds_acc - jnp.sum(ps * delta) * (lane == h).astype(F32)
            for t, start in enumerate((0, pl.multiple_of(prev * T, T), pl.multiple_of(j * T, T))):
                rsl = pl.ds(start, T)
                dkv_ref[rsl, ksl] += dk3[t * T:(t + 1) * T, :]
                dkv_ref[rsl, vsl] += dv3[t * T:(t + 1) * T, :]
        ds_ref[...] += ds_acc
        dq_ref[...] = jnp.concatenate(dqts, axis=0).T.astype(dq_ref.dtype)

    blk = pl.BlockSpec((T, D_MODEL), lambda j: (j, 0))
    full = pl.BlockSpec((rows, 2 * D_KV), lambda j: (0, 0))
    return _call(
        body, name, [S((rows, D_MODEL), BF16), S((rows, 2 * D_KV), F32), S((1, LANE), F32)], (rows // T,),
        [blk, full, pl.BlockSpec((1, N_Q_HEADS), lambda j: (0, 0)), blk, pl.BlockSpec((N_Q_HEADS, T), lambda j: (0, j))],
        [blk, full, pl.BlockSpec((1, LANE), lambda j: (0, 0))], ("arbitrary",), (q, kv, sinks, do, lse), comm=comm)


BLOCK_BYTES = 1 << 20


def _div_tile(rows, cols):
    cap = max(16, BLOCK_BYTES // (4 * cols))
    best = None
    for t in range(16, min(rows, cap) + 1, 16):
        if rows % t == 0:
            best = t
    return best if best is not None else rows


def _adamw(parts, w, m, v, name, comm=None):
    layers, rows, cols = w.shape
    n = parts[0].shape[0]
    tr = _div_tile(rows, cols)
    tc = _pick(cols, 256) if tr == rows and rows * cols * 4 > 2 * BLOCK_BYTES else cols
    c1 = 1.0 / (1.0 - B1 ** STEP)
    c2 = 1.0 / (1.0 - B2 ** STEP)

    def body(*refs):
        p_refs = refs[:layers]
        w_ref, m_ref, v_ref, g_ref, d_ref, nm_ref, nv_ref = refs[layers:]
        layer = pl.program_id(0)
        for l in range(layers):
            @pl.when(layer == l)
            def _(p_ref=p_refs[l]):
                g = p_ref[0].astype(F32)
                for i in range(1, n):
                    g = g + p_ref[i].astype(F32)
                nm = B1 * m_ref[...] + (1.0 - B1) * g
                nv = B2 * v_ref[...] + (1.0 - B2) * (g * g)
                g_ref[...] = g
                nm_ref[...] = nm
                nv_ref[...] = nv
                d_ref[...] = -LR * ((nm * c1) / (jnp.sqrt(nv * c2) + EPS) + WD * w_ref[...])

    def part_spec(l):
        return pl.BlockSpec((n, tr, tc), lambda k, i, j: (0, jnp.where(k == l, i, 0), jnp.where(k == l, j, 0)))

    row = pl.BlockSpec((None, tr, tc), lambda k, i, j: (k, i, j))
    return _call(body, name, [S((layers, rows, cols), F32)] * 4, (layers, rows // tr, cols // tc),
                 [part_spec(l) for l in range(layers)] + [row, row, row], [row] * 4,
                 ("parallel", "parallel", "parallel"), (*parts, w, m, v), comm=comm)


def _sum_parts(parts, name):
    n, rows, cols = parts[0].shape
    nb = len(parts)
    tr = _div_tile(rows, cols)

    def body(*refs):
        o_ref = refs[nb]
        blk = pl.program_id(0)
        for l in range(nb):
            @pl.when(blk == l)
            def _(p_ref=refs[l]):
                g = p_ref[0].astype(F32)
                for i in range(1, n):
                    g = g + p_ref[i].astype(F32)
                o_ref[...] = g

    def part_spec(l):
        return pl.BlockSpec((n, tr, cols), lambda k, i: (0, jnp.where(k == l, i, 0), 0))

    per = rows // tr
    return pl.pallas_call(body, name=name, out_shape=S((nb * rows, cols), F32), grid=(nb, per),
                          in_specs=[part_spec(l) for l in range(nb)],
                          out_specs=pl.BlockSpec((tr, cols), lambda k, i: (k * per + i, 0)),
                          compiler_params=_cp(("parallel", "parallel")))(*parts)


def _col_segments(ws, runs):
    segs = []
    for glo, mlo, n in runs:
        while n > 0:
            d, off = divmod(glo, ws)
            take = min(n, ws - off)
            segs.append((d, off, mlo, take))
            glo, mlo, n = glo + take, mlo + take, n - take
    return segs


def _assemble_cols(gs, width, segs, name):
    _, rows, ws = gs[0].shape
    nb = len(gs)
    rb = _div_tile(rows, width // 2)
    per = rows // rb

    def body(*refs):
        o_ref = refs[nb]
        piece = pl.program_id(0)
        for l in range(nb):
            @pl.when(piece == l)
            def _(g_ref=refs[l]):
                o_ref[...] = jnp.zeros_like(o_ref)
                for d, off, mlo, n in segs:
                    o_ref[:, mlo:mlo + n] = g_ref[d, :, off:off + n]

    def piece_spec(l):
        return pl.BlockSpec((N_DEV, rb, ws), lambda k, i: (0, jnp.where(k == l, i, 0), 0))

    return pl.pallas_call(
        body, name=name, out_shape=S((nb * rows, width), gs[0].dtype), grid=(nb, per),
        in_specs=[piece_spec(l) for l in range(nb)],
        out_specs=pl.BlockSpec((rb, width), lambda k, i: (k * per + i, 0)),
        compiler_params=_cp(("parallel", "parallel")))(*gs)


def _scatter_cols(dw, ws, segs, name):
    rows, width = dw.shape
    rb = _div_tile(rows, width)

    def body(w_ref, o_ref):
        for d, off, mlo, n in segs:
            o_ref[d, :, off:off + n] = w_ref[:, mlo:mlo + n].astype(o_ref.dtype)

    return pl.pallas_call(
        body, name=name, out_shape=S((N_DEV, rows, ws), BF16), grid=(rows // rb,),
        in_specs=[pl.BlockSpec((rb, width), lambda i: (i, 0))],
        out_specs=pl.BlockSpec((N_DEV, rb, ws), lambda i: (0, i, 0)), compiler_params=_cp(("parallel",)))(dw)


def _gather_comm(xs):
    n = len(xs)

    def setup(x_refs, out_refs, sems):
        send_sems, recv_sems, local_sems = sems
        mx, my, mc = lax.axis_index("x"), lax.axis_index("y"), lax.axis_index("c")
        me, sibling = (mx, my, mc), (mx, my, 1 - mc)
        chips = [(1 - mx, my), (mx, 1 - my), (1 - mx, 1 - my)]

        def blk(a, px, py, pc):
            return out_refs[a].at[4 * px + 2 * py + pc]

        def copy(a, k, block, to, src=None):
            return pltpu.make_async_remote_copy(
                src_ref=blk(a, *block) if src is None else src, dst_ref=blk(a, *block),
                send_sem=send_sems.at[a, k], recv_sem=recv_sems.at[a, k], device_id=to, device_id_type=_MESH)

        mine = [pltpu.make_async_copy(x_refs[a], blk(a, *me), local_sems.at[a]) for a in range(n)]
        own = []
        for a in range(n):
            own.append(copy(a, 0, me, sibling, src=x_refs[a]))
            own += [copy(a, 1 + i, me, (*chip, mc), src=x_refs[a]) for i, chip in enumerate(chips)]
        return me, sibling, chips, mc, copy, mine, own

    def first(x_refs, out_refs, sems):
        _, _, _, _, _, mine, own = setup(x_refs, out_refs, sems)
        for cp in mine + own:
            cp.start()

    def last(x_refs, out_refs, sems):
        me, sibling, chips, mc, copy, mine, own = setup(x_refs, out_refs, sems)
        passed = []
        for a in range(n):
            for i, chip in enumerate(chips):
                copy(a, 1 + i, (*chip, mc), me).wait_recv()
                passed.append(copy(a, 4 + i, (*chip, mc), sibling))
                passed[-1].start()
        for a in range(n):
            copy(a, 0, sibling, me).wait_recv()
            for i, chip in enumerate(chips):
                copy(a, 4 + i, (*chip, 1 - mc), me).wait_recv()
        for cp in own + passed:
            cp.wait_send()
        for cp in mine:
            cp.wait()

    return _Comm(list(xs), [S((N_DEV,) + x.shape, x.dtype) for x in xs],
                 [pltpu.SemaphoreType.DMA((n, 7)), pltpu.SemaphoreType.DMA((n, 7)), pltpu.SemaphoreType.DMA((n,))],
                 first, last)


def _swap_comm(gs):
    n = len(gs)

    def copies(g_refs, out_refs, sems):
        send_sems, recv_sems = sems
        mx, my, mc = lax.axis_index("x"), lax.axis_index("y"), lax.axis_index("c")
        return [pltpu.make_async_remote_copy(
            src_ref=g_refs[a].at[2 * k + 1 - mc], dst_ref=out_refs[a].at[k], send_sem=send_sems.at[a, k],
            recv_sem=recv_sems.at[a, k], device_id=(mx, my, 1 - mc), device_id_type=_MESH)
            for a in range(n) for k in range(4)]

    def first(g_refs, out_refs, sems):
        for cp in copies(g_refs, out_refs, sems):
            cp.start()

    def last(g_refs, out_refs, sems):
        for cp in copies(g_refs, out_refs, sems):
            cp.wait()

    return _Comm(list(gs), [S((4,) + g.shape[1:], g.dtype) for g in gs],
                 [pltpu.SemaphoreType.DMA((n, 4)), pltpu.SemaphoreType.DMA((n, 4))], first, last)


def _chips_comm(parts):
    n = len(parts)

    def copies(p_refs, out_refs, sems):
        send_sems, recv_sems, local_sems = sems
        mx, my, mc = lax.axis_index("x"), lax.axis_index("y"), lax.axis_index("c")
        mychip = 2 * mx + my
        chips = [(1 - mx, my), (mx, 1 - my), (1 - mx, 1 - my)]
        mine = [pltpu.make_async_copy(p_refs[a].at[mychip], out_refs[a].at[mychip], local_sems.at[a])
                for a in range(n)]
        return mine + [pltpu.make_async_remote_copy(
            src_ref=p_refs[a].at[2 * cx + cy], dst_ref=out_refs[a].at[mychip], send_sem=send_sems.at[a, i],
            recv_sem=recv_sems.at[a, i], device_id=(cx, cy, mc), device_id_type=_MESH)
            for a in range(n) for i, (cx, cy) in enumerate(chips)]

    def first(p_refs, out_refs, sems):
        for cp in copies(p_refs, out_refs, sems):
            cp.start()

    def last(p_refs, out_refs, sems):
        for cp in copies(p_refs, out_refs, sems):
            cp.wait()

    return _Comm(list(parts), [S(p.shape, p.dtype) for p in parts],
                 [pltpu.SemaphoreType.DMA((n, 3)), pltpu.SemaphoreType.DMA((n, 3)), pltpu.SemaphoreType.DMA((n,))],
                 first, last)


def _join_comms(comms):
    def split(refs, counts):
        out, p = [], 0
        for cnt in counts:
            out.append(refs[p:p + cnt])
            p += cnt
        return out

    ni = [len(c.ins) for c in comms]
    no = [len(c.out_shapes) for c in comms]
    ns = [len(c.scratch) for c in comms]

    def first(in_refs, out_refs, sems):
        for c, i, o, s in zip(comms, split(in_refs, ni), split(out_refs, no), split(sems, ns)):
            c.first(i, o, s)

    def last(in_refs, out_refs, sems):
        for c, i, o, s in zip(comms, split(in_refs, ni), split(out_refs, no), split(sems, ns)):
            c.last(i, o, s)

    return _Comm([x for c in comms for x in c.ins], [x for c in comms for x in c.out_shapes],
                 [x for c in comms for x in c.scratch], first, last)


def _add_pairs(mine, theirs, core, name):
    _, rows, cols = mine.shape
    tr = _div_tile(rows, cols)

    def body(core_ref, a_ref, b_ref, o_ref):
        o_ref[...] = (a_ref[...].astype(F32) + b_ref[...].astype(F32)).astype(o_ref.dtype)

    return pl.pallas_call(
        body, name=name, out_shape=S((4, rows, cols), BF16),
        grid_spec=pltpu.PrefetchScalarGridSpec(
            num_scalar_prefetch=1, grid=(4, rows // tr),
            in_specs=[pl.BlockSpec((None, tr, cols), lambda k, i, c: (2 * k + c[0], i, 0)),
                      pl.BlockSpec((None, tr, cols), lambda k, i, c: (k, i, 0))],
            out_specs=pl.BlockSpec((None, tr, cols), lambda k, i, c: (k, i, 0))),
        compiler_params=_cp(("parallel", "parallel")))(core, mine, theirs)


def _run_comm(comm, name):
    ci, co = len(comm.ins), len(comm.out_shapes)

    def body(*refs):
        comm.first(refs[:ci], refs[ci:ci + co], refs[ci + co:])
        comm.last(refs[:ci], refs[ci:ci + co], refs[ci + co:])

    return pl.pallas_call(body, name=name, out_shape=list(comm.out_shapes), in_specs=[_HBM] * ci,
                          out_specs=[_HBM] * co, scratch_shapes=list(comm.scratch))(*comm.ins)


def _flat_rows(n_elems, mult):
    rows = -(-n_elems // LANE)
    return -(-rows // mult) * mult


def _pack(arrs, lead, mult, dtype):
    lead_shape = arrs[0].shape[:lead]
    flat = jnp.concatenate([a.astype(dtype).reshape(lead_shape + (-1,)) for a in arrs], axis=-1)
    n = flat.shape[-1]
    rows = _flat_rows(n, mult)
    flat = jnp.pad(flat, [(0, 0)] * lead + [(0, rows * LANE - n)])
    return flat.reshape(lead_shape + (rows, LANE))


def _unpack(flat, lead, shapes):
    lead_shape = flat.shape[:lead]
    flat = flat.reshape(lead_shape + (-1,))
    out, off = [], 0
    for shp in shapes:
        n = math.prod(shp)
        out.append(flat[..., off:off + n].reshape(lead_shape + tuple(shp)))
        off += n
    return out


def _split8(full, ax, n):
    shp = full.shape
    return jnp.moveaxis(full.reshape(shp[:ax] + (N_DEV, n) + shp[ax + 1:]), ax, 0)


def _join8(g, ax):
    shp = g.shape[1:]
    return jnp.moveaxis(g, 0, ax).reshape(shp[:ax] + (N_DEV * shp[ax],) + shp[ax + 1:])


def _group_lanes(v, hg):
    v = v.reshape(SSM_GROUPS, hg)
    return jnp.pad(v, ((0, 0), (0, LANE - hg))).reshape(1, SSM_GROUPS * LANE)


def _ungroup_lanes(v, hg):
    return v.reshape(SSM_GROUPS, LANE)[:, :hg].reshape(1, SSM_GROUPS * hg)


def kernel(x, meta_tokens, a_norm_pre, a_w_in, a_conv_w, a_conv_b, a_dt_bias, a_a_log, a_d_skip, a_gate_norm, a_w_out, a_norm_post, kv_norm, w_kv, b_norm_pre, b_w_q, b_sinks, b_w_o, b_norm_post, f_norm_pre, f_w_up, f_conv_w, f_conv_b, f_w_down, f_norm_post, loss_target, m_meta_tokens, m_a_norm_pre, m_a_w_in, m_a_conv_w, m_a_conv_b, m_a_dt_bias, m_a_a_log, m_a_d_skip, m_a_gate_norm, m_a_w_out, m_a_norm_post, m_kv_norm, m_w_kv, m_b_norm_pre, m_b_w_q, m_b_sinks, m_b_w_o, m_b_norm_post, m_f_norm_pre, m_f_w_up, m_f_conv_w, m_f_conv_b, m_f_w_down, m_f_norm_post, v_meta_tokens, v_a_norm_pre, v_a_w_in, v_a_conv_w, v_a_conv_b, v_a_dt_bias, v_a_a_log, v_a_d_skip, v_a_gate_norm, v_a_w_out, v_a_norm_post, v_kv_norm, v_w_kv, v_b_norm_pre, v_b_w_q, v_b_sinks, v_b_w_o, v_b_norm_post, v_f_norm_pre, v_f_w_up, v_f_conv_w, v_f_conv_b, v_f_w_down, v_f_norm_post):
    args = locals()
    wts = {n: args[n] for n in WEIGHTS}
    mom = {n: args["m_" + n] for n in WEIGHTS}
    var = {n: args["v_" + n] for n in WEIGHTS}
    mx, my, mc = lax.axis_index("x"), lax.axis_index("y"), lax.axis_index("c")
    me = 4 * mx + 2 * my + mc
    rows = _seq_rows()
    hg = SSM_HEADS // SSM_GROUPS
    d = D_MODEL

    n_main = D_INNER + D_XBC
    ws_in, ws_up = a_w_in.shape[2], f_w_up.shape[2]
    segs_in = _col_segments(ws_in, [(0, 0, n_main)] + [(n_main + hg * g, n_main + LANE * g, hg)
                                                      for g in range(SSM_GROUPS)])
    segs_up = _col_segments(ws_up, [(0, 0, 2 * D_FF)])
    def gather_of(*ws):
        return _gather_comm([w.astype(BF16) for w in ws])

    small_full, = _run_comm(_gather_comm([_pack([wts[n] for n in SMALL], 0, 8, F32)]), "gather_small")
    full = {}
    for n, g in zip(SMALL, _unpack(small_full, 1, [wts[n].shape for n in SMALL])):
        full[n] = _join8(g, SHARD_AXIS[n])
    (h0, hn0), (g_in,) = _embed_norm(full["meta_tokens"], x[0], full["a_norm_pre"], rows, "embed_norm",
                                     comm=gather_of(a_w_in[0]))
    w_in_all = _assemble_cols([g_in], n_main + SSM_GROUPS * LANE, segs_in, "asm_w_in")
    w_up, w_down = [None, None], [None, None]
    bias_g = _group_lanes(wts["a_dt_bias"], hg)
    alog_g = _group_lanes(wts["a_a_log"], hg)
    dsk_g = _group_lanes(wts["a_d_skip"], hg)
    a_conv_w, a_conv_b = full["a_conv_w"][0], full["a_conv_b"]
    f_cw, f_cb = full["f_conv_w"], wts["f_conv_b"]
    fpre, fpost = wts["f_norm_pre"], wts["f_norm_post"]

    tgt = jnp.pad(loss_target[0], ((N_META, rows - N_META - SEQ), (0, 0)))

    zx, (g_out,) = _mm(hn0, w_in_all, "nn", F32, "mm_in", comm=gather_of(a_w_out[0]))
    w_out = g_out.reshape(D_INNER, d)
    xbc, (g_dn1,) = _conv_silu_fwd(zx, a_conv_w, a_conv_b, "conv_a", comm=gather_of(f_w_down[1]))
    (y_ssd, hst), (g_up0,) = _ssd_fwd(xbc, zx, bias_g, alog_g, dsk_g, "ssd_fwd", comm=gather_of(f_w_up[0]))
    w_up[0] = _assemble_cols([g_up0], 2 * D_FF, segs_up, "asm_w_up0")
    yn, (g_kv, g_q) = _gatenorm_fwd(y_ssd, zx, full["a_gate_norm"], "gatenorm", comm=gather_of(w_kv, b_w_q[0]))
    mix_a, (g_o,) = _mm(yn, w_out, "nn", F32, "mm_out", comm=gather_of(b_w_o[0]))
    w_kvf, w_q, w_o = g_kv.reshape(d, 2 * D_KV), g_q.reshape(d, d), g_o.reshape(d, d)
    h1, (fn0,) = _resid_norm(h0, mix_a, full["a_norm_post"], [fpre[0:1]], "resid_a")

    half = d // 2
    u0, (g_dn0,) = _mm(fn0, w_up[0], "nn", F32, "mm_up0", comm=gather_of(f_w_down[0]))
    w_down = [g_dn0.reshape(D_FF, d), g_dn1.reshape(D_FF, d)]
    act0, (g_up1a,) = _ffn_act_fwd(u0, f_cw[0], f_cb[0:1], "ffn_act0", comm=gather_of(f_w_up[1, :half]))
    ffn0 = _mm(act0, w_down[0], "nn", F32, "mm_down0")
    h2, (kvn, bn) = _resid_norm(h1, ffn0, fpost[0:1], [wts["kv_norm"].reshape(1, d), wts["b_norm_pre"]], "resid_f0")
    kv = _mm(kvn, w_kvf, "nn", F32, "mm_kv")
    q = _mm(bn, w_q, "nn", F32, "mm_q")
    (o, lse), (g_up1b,) = _attn_fwd(q, kv, wts["b_sinks"], "attn_fwd", comm=gather_of(f_w_up[1, half:]))
    w_up[1] = _assemble_cols([g_up1a, g_up1b], 2 * D_FF, segs_up, "asm_w_up1")
    mix_b = _mm(o, w_o, "nn", F32, "mm_o")
    h3, (fn1,) = _resid_norm(h2, mix_b, wts["b_norm_post"], [fpre[1:2]], "resid_b")
    u1 = _mm(fn1, w_up[1], "nn", F32, "mm_up1")
    act1 = _ffn_act_fwd(u1, f_cw[1], f_cb[1:2], "ffn_act1")
    ffn1 = _mm(act1, w_down[1], "nn", F32, "mm_down1")
    dh4, loss_row = _final_loss(h3, ffn1, fpost[1:2], tgt, "loss")
    loss = lax.psum(loss_row[0, 0], ("x", "y", "c"))

    grads = {}

    core = mc.astype(jnp.int32).reshape(1)

    def carried(res, comm):
        return res if comm is not None else (res, None)

    def ffn_bwd(dh_out, h_in, fn, u, act, ffn, i, c_dact=None, c_dwdown=None):
        dffn, dw_post = _norm_bwd(ffn, fpost[i:i + 1], dh_out, None, BF16, f"nb_fpost{i}")
        dact, got_a = carried(_mm(dffn, w_down[i], "nt", F32, f"mm_dact{i}", comm=c_dact), c_dact)
        dw_down, got_b = carried(_mm(act, dffn, "tn", BF16, f"mm_dwdown{i}", comm=c_dwdown), c_dwdown)
        dw_down = dw_down.reshape(N_DEV, -1, d)
        du, dwc, dbc = _ffn_act_bwd(u, dact, f_cw[i], f_cb[i:i + 1], f"ffn_act_bwd{i}")
        dfn, (s_dn,) = _mm(du, w_up[i], "nt", F32, f"mm_dfn{i}", comm=_swap_comm([dw_down]))
        sum_dn = _add_pairs(dw_down, s_dn, core, f"rs_add_dn{i}")
        dw_up = _scatter_cols(_mm(fn, du, "tn", BF16, f"mm_dwup{i}"), ws_up, segs_up, f"scat_w_up{i}")
        (dh_in, dw_pre), (s_up,) = _norm_bwd(h_in, fpre[i:i + 1], dfn, dh_out, F32, f"nb_fpre{i}",
                                             comm=_swap_comm([dw_up]))
        sum_up = _add_pairs(dw_up, s_up, core, f"rs_add_up{i}")
        return dh_in, dict(post=dw_post, sum_down=sum_dn, cw=jnp.concatenate([dwc[0], dwc[1]], axis=1),
                           cb=jnp.concatenate([dbc[0], dbc[1]], axis=1), sum_up=sum_up, pre=dw_pre), got_a, got_b

    dh3, gf1, _, _ = ffn_bwd(dh4, h3, fn1, u1, act1, ffn1, 1)
    dmix_b, grads["b_norm_post"] = _norm_bwd(mix_b, wts["b_norm_post"], dh3, None, BF16, "nb_bpost")
    do = _mm(dmix_b, w_o, "nt", F32, "mm_do")
    dw_o = _mm(o, dmix_b, "tn", BF16, "mm_dwo").reshape(N_DEV, -1, d)
    (dq, dkv, dsinks), (p_up1, p_dn1, s_o) = _attn_bwd(
        q, kv, wts["b_sinks"], do, lse, "attn_bwd",
        comm=_join_comms([_chips_comm([gf1["sum_up"], gf1["sum_down"]]), _swap_comm([dw_o])]))
    sum_o = _add_pairs(dw_o, s_o, core, "rs_add_o")
    grads["b_sinks"] = dsinks[:, :N_Q_HEADS]
    dbn = _mm(dq, w_q, "nt", F32, "mm_dbn")
    dw_q = _mm(bn, dq, "tn", BF16, "mm_dwq").reshape(N_DEV, -1, d)
    dkv16 = dkv.astype(BF16)
    dkvn = _mm(dkv16, w_kvf, "nt", F32, "mm_dkvn")
    dw_kv = _mm(kvn, dkv16, "tn", BF16, "mm_dwkv").reshape(N_DEV, -1, 2 * D_KV)
    (dh2, grads["b_norm_pre"]), (s_q, s_kv) = _norm_bwd(h2, wts["b_norm_pre"], dbn, dh3, F32, "nb_bpre",
                                                        comm=_swap_comm([dw_q, dw_kv]))
    sum_q, sum_kv = _add_pairs(dw_q, s_q, core, "rs_add_q"), _add_pairs(dw_kv, s_kv, core, "rs_add_kv")
    dh2, dw_kvn = _norm_bwd(h2, wts["kv_norm"].reshape(1, d), dkvn, dh2, F32, "nb_kv")
    grads["kv_norm"] = dw_kvn.reshape(d)
    dh1, gf0, (p_o,), (p_q, p_kv) = ffn_bwd(dh2, h1, fn0, u0, act0, ffn0, 0, c_dact=_chips_comm([sum_o]),
                                            c_dwdown=_chips_comm([sum_q, sum_kv]))
    grads["f_norm_post"] = jnp.concatenate([gf0["post"], gf1["post"]], axis=0)
    grads["f_norm_pre"] = jnp.concatenate([gf0["pre"], gf1["pre"]], axis=0)
    grads["f_conv_w"] = jnp.stack([gf0["cw"], gf1["cw"]])
    grads["f_conv_b"] = jnp.concatenate([gf0["cb"], gf1["cb"]], axis=0)

    dmix_a, grads["a_norm_post"] = _norm_bwd(mix_a, full["a_norm_post"], dh1, None, BF16, "nb_apost")
    dyn = _mm(dmix_a, w_out, "nt", F32, "mm_dyn")
    dw_out = _mm(yn, dmix_a, "tn", BF16, "mm_dwout").reshape(N_DEV, -1, d)
    (dy_ssd, dz, grads["a_gate_norm"]), (s_out,) = _gatenorm_bwd(y_ssd, zx, full["a_gate_norm"], dyn, "gatenorm_bwd",
                                                                 comm=_swap_comm([dw_out]))
    sum_out = _add_pairs(dw_out, s_out, core, "rs_add_out")
    (dxs, dbm, dcm, ddtp, dalog, ddsk, dbias), (p_up0, p_dn0, p_out) = _ssd_bwd(
        xbc, zx, bias_g, alog_g, dsk_g, dy_ssd, hst, "ssd_bwd",
        comm=_chips_comm([gf0["sum_up"], gf0["sum_down"], sum_out]))
    grads["a_a_log"] = _ungroup_lanes(dalog, hg)
    grads["a_d_skip"] = _ungroup_lanes(ddsk, hg)
    grads["a_dt_bias"] = _ungroup_lanes(dbias, hg)
    dpre, dcw, dcb = _conv_silu_bwd(zx, dxs, dbm, dcm, a_conv_w, a_conv_b, "conv_a_bwd")
    grads["a_conv_w"], grads["a_conv_b"] = dcw[None], dcb
    dzx = jnp.concatenate([dz, dpre, ddtp.astype(BF16)], axis=1)
    dw_in8 = _scatter_cols(_mm(hn0, dzx, "tn", BF16, "mm_dwin"), ws_in, segs_in, "scat_w_in")
    dhn0, (s_in,) = _mm(dzx, w_in_all, "nt", F32, "mm_dhn0", comm=_swap_comm([dw_in8]))
    sum_in = _add_pairs(dw_in8, s_in, core, "rs_add_in")
    half_in = sum_in.shape[1] // 2
    (dh0, grads["a_norm_pre"]), (p_in_a,) = _norm_bwd(h0, full["a_norm_pre"], dhn0, dh1, F32, "nb_apre",
                                                      comm=_chips_comm([sum_in[:, :half_in]]))
    grad_x = dh0[N_META:N_META + SEQ][None]
    grads["meta_tokens"] = dh0[:N_META]

    small_local = _pack([_split8(grads[n], SHARD_AXIS[n], wts[n].shape[SHARD_AXIS[n]]) for n in SMALL], 1, 8, F32)
    repl_local = _pack([grads[n] for n in REPL], 0, 8, F32)
    n_sr = small_local.shape[1]
    small_vec = jnp.concatenate([small_local.reshape(N_DEV * n_sr, LANE), repl_local], axis=0)
    tail = _join_comms([_chips_comm([sum_in[:, half_in:]]), _gather_comm([small_vec])])
    parts_big = dict(a_w_out=[p_out], w_kv=[p_kv], b_w_q=[p_q], b_w_o=[p_o], f_w_down=[p_dn0, p_dn1])

    def flat_f32(dct, names, mult):
        return _pack([dct[n] for n in names], 0, mult, F32)

    def adamw_big(n, comm=None):
        shp3 = (len(parts_big[n]),) + parts_big[n][0].shape[1:]
        res = _adamw(parts_big[n], *[dct[n].reshape(shp3) for dct in (wts, mom, var)], f"adamw_{n}", comm=comm)
        res, got = res if comm is not None else (res, None)
        big_out[n] = [r.reshape(wts[n].shape) for r in res]
        return got

    big_out = {}
    def swap_last(a):
        return jnp.swapaxes(a, -1, -2)

    g_up_t = swap_last(_sum_parts([p_up0, p_up1], "sum_w_up").reshape(f_w_up.shape))
    res, (p_in_b, small_all) = _adamw([g_up_t[0:1], g_up_t[1:2]], *[swap_last(dct["f_w_up"]) for dct in (wts, mom, var)],
                                      "adamw_f_w_up", comm=tail)
    big_out["f_w_up"] = [swap_last(r) for r in res]
    for n in BIG:
        if n not in ("f_w_up", "a_w_in"):
            adamw_big(n)
    g_in_t = swap_last(_sum_parts([p_in_a, p_in_b], "sum_w_in"))[None]
    res = _adamw([g_in_t], *[swap_last(dct["a_w_in"]) for dct in (wts, mom, var)], "adamw_a_w_in")
    big_out["a_w_in"] = [swap_last(r) for r in res]
    mine_small = lax.dynamic_slice_in_dim(small_all, me * n_sr, n_sr, axis=1)
    parts_small = jnp.concatenate([mine_small, small_all[:, N_DEV * n_sr:]], axis=1)
    sm_in = [jnp.concatenate([flat_f32(dct, SMALL, 8), flat_f32(dct, REPL, 8)], axis=0)[None] for dct in (wts, mom, var)]
    small_out = [r[0] for r in _adamw([parts_small], *sm_in, "adamw_small")]

    outs = []
    for kind in range(4):
        res = {n: big_out[n][kind] for n in BIG}
        for n, a in zip(SMALL, _unpack(small_out[kind][:n_sr], 0, [wts[n].shape for n in SMALL])):
            res[n] = a
        for n, a in zip(REPL, _unpack(small_out[kind][n_sr:], 0, [wts[n].shape for n in REPL])):
            res[n] = a
        outs.append(res)
    return (loss, grad_x, *[outs[0][n] for n in WEIGHTS], *[outs[1][n] for n in WEIGHTS],
            *[outs[2][n] for n in WEIGHTS], *[outs[3][n] for n in WEIGHTS])
```

```python
import functools
import math

import jax
import jax.numpy as jnp
from jax import lax
from jax.experimental import pallas as pl
from jax.experimental.pallas import tpu as pltpu

F32, BF16 = jnp.float32, jnp.bfloat16
S = jax.ShapeDtypeStruct

D_MODEL = 1024
SEQ = 2048
N_META = 16
D_INNER = 2048
HEAD_P = 64
SSM_HEADS = D_INNER // HEAD_P
SSM_GROUPS = 4
D_STATE = 128
SSM_CONV = 4
D_BC = SSM_GROUPS * D_STATE
D_XBC = D_INNER + 2 * D_BC
ATTN_DH = 64
N_Q_HEADS = D_MODEL // ATTN_DH
N_KV_HEADS = 4
D_KV = N_KV_HEADS * ATTN_DH
WINDOW = 128
D_FF = 2816
FFN_CONV = 3
RMS_EPS = 1e-6
NEG = -1e30
LR, B1, B2, EPS, WD, STEP = 0.001, 0.9, 0.999, 1e-08, 0.01, 10

N_DEV = 8
T = 128
LANE = 128
VMEM_LIMIT = 48 * 1024 * 1024

BIG = ("a_w_in", "a_w_out", "w_kv", "b_w_q", "b_w_o", "f_w_up", "f_w_down")
SMALL = ("meta_tokens", "a_norm_pre", "a_conv_w", "a_conv_b", "a_gate_norm", "a_norm_post", "f_conv_w")
REPL = ("a_dt_bias", "a_a_log", "a_d_skip", "kv_norm", "b_norm_pre", "b_sinks", "b_norm_post",
        "f_norm_pre", "f_conv_b", "f_norm_post")
SHARD_AXIS = dict(a_w_in=2, a_w_out=1, w_kv=0, b_w_q=1, b_w_o=1, f_w_up=2, f_w_down=1, meta_tokens=1,
                  a_norm_pre=1, a_conv_w=2, a_conv_b=1, a_gate_norm=1, a_norm_post=1, f_conv_w=2)
WEIGHTS = ("meta_tokens", "a_norm_pre", "a_w_in", "a_conv_w", "a_conv_b", "a_dt_bias", "a_a_log", "a_d_skip",
           "a_gate_norm", "a_w_out", "a_norm_post", "kv_norm", "w_kv", "b_norm_pre", "b_w_q", "b_sinks", "b_w_o",
           "b_norm_post", "f_norm_pre", "f_w_up", "f_conv_w", "f_conv_b", "f_w_down", "f_norm_post")


def _seq_rows():
    return -(-(N_META + SEQ) // T) * T


def _cp(sem=None):
    return pltpu.CompilerParams(dimension_semantics=sem, vmem_limit_bytes=VMEM_LIMIT)


def _pick(n, target):
    t = min(n, target)
    t -= t % LANE
    while n % t:
        t -= LANE
    return t


def _sigmoid(x):
    return 0.5 * jnp.tanh(0.5 * x) + 0.5


def _softplus(x):
    return jnp.maximum(x, 0.0) + jnp.log(1.0 + jnp.exp(-jnp.abs(x)))


_NN = (((1,), (0,)), ((), ()))
_NT = (((1,), (1,)), ((), ()))
_TN = (((0,), (0,)), ((), ()))


def _dot(a, b, dims=_NN):
    return lax.dot_general(a, b, dims, preferred_element_type=F32)


def _dot_hi(a, b):
    return lax.dot_general(a, b, _NN, precision=lax.Precision.HIGHEST, preferred_element_type=F32)


_HBM = pl.BlockSpec(memory_space=pltpu.HBM)
_MESH = pl.DeviceIdType.MESH


class _Comm:
    def __init__(self, ins, out_shapes, scratch, first, last):
        self.ins, self.out_shapes, self.scratch, self.first, self.last = ins, out_shapes, scratch, first, last


def _call(body, name, out_shape, grid, in_specs, out_specs, sem, args, scratch=(), comm=None):
    if comm is None:
        return pl.pallas_call(body, name=name, out_shape=out_shape, grid=grid, in_specs=in_specs, out_specs=out_specs,
                              scratch_shapes=list(scratch), compiler_params=_cp(sem))(*args)
    single = not isinstance(out_shape, (list, tuple))
    outs = [out_shape] if single else list(out_shape)
    ospecs = [out_specs] if single else list(out_specs)
    n_in, n_out, n_scr, ci, co = len(in_specs), len(outs), len(scratch), len(comm.ins), len(comm.out_shapes)

    def carrier(*refs):
        p = 0
        parts = []
        for cnt in (n_in, ci, n_out, co, n_scr, len(comm.scratch)):
            parts.append(refs[p:p + cnt])
            p += cnt
        ins, cins, outs_r, couts, scr, cscr = parts
        ids = [pl.program_id(i) for i in range(len(grid))]
        first, last = ids[0] == 0, ids[0] == grid[0] - 1
        for i in range(1, len(grid)):
            first, last = first & (ids[i] == 0), last & (ids[i] == grid[i] - 1)

        @pl.when(first)
        def _():
            comm.first(cins, couts, cscr)

        body(*ins, *outs_r, *scr)

        @pl.when(last)
        def _():
            comm.last(cins, couts, cscr)

    res = pl.pallas_call(
        carrier, name=name, out_shape=outs + list(comm.out_shapes), grid=grid,
        in_specs=list(in_specs) + [_HBM] * ci, out_specs=ospecs + [_HBM] * co,
        scratch_shapes=list(scratch) + list(comm.scratch),
        compiler_params=_cp(("arbitrary",) * len(grid)))(*args, *comm.ins)
    mine = res[0] if single else list(res[:n_out])
    return mine, list(res[n_out:])


def _mm(a, b, mode, out_dtype, name, comm=None, shard_cols=None):
    if mode == "tn":
        m, kk = a.shape
        planes, width = (b.shape[0], b.shape[2]) if b.ndim == 3 else (1, b.shape[1])
        n = planes * width
        tko, tn = _pick(kk, 512), _pick(width, 1536)
        per = width // tn
        b_spec = (pl.BlockSpec((None, m, tn), lambda i, j: (j // per, 0, j % per)) if b.ndim == 3
                  else pl.BlockSpec((m, tn), lambda i, j: (0, j)))
        if shard_cols is None:
            def body(a_ref, b_ref, o_ref):
                o_ref[...] = _dot(a_ref[...], b_ref[...], _TN).astype(o_ref.dtype)

            out_shape, out_spec = S((kk, n), out_dtype), pl.BlockSpec((tko, tn), lambda i, j: (i, j))
        else:
            shards = tn // shard_cols
            assert tn % shard_cols == 0

            def body(a_ref, b_ref, o_ref):
                res = _dot(a_ref[...], b_ref[...], _TN).astype(o_ref.dtype)
                for p in range(shards):
                    o_ref[p] = res[:, p * shard_cols:(p + 1) * shard_cols]

            out_shape = S((n // shard_cols, kk, shard_cols), out_dtype)
            out_spec = pl.BlockSpec((shards, tko, shard_cols), lambda i, j: (j, i, 0))
        return _call(
            body, name, out_shape, (kk // tko, n // tn), [pl.BlockSpec((m, tko), lambda i, j: (0, i)), b_spec],
            out_spec, ("parallel", "parallel"), (a, b), comm=comm)

    planes, width = (a.shape[0], a.shape[2]) if a.ndim == 3 else (1, a.shape[1])
    m, kk = a.shape[-2], planes * width
    n = b.shape[1] if mode == "nn" else b.shape[0]
    dims = _NN if mode == "nn" else _NT

    if kk > 2048:
        tm = m // 4
        assert m % 4 == 0 and tm % 16 == 0

        def body(a_ref, b_ref, o_ref):
            if a.ndim == 2:
                res = _dot(a_ref[...], b_ref[...], dims)
            else:
                res = None
                for p in range(planes):
                    bp = b_ref[p * width:(p + 1) * width, :] if mode == "nn" else b_ref[:, p * width:(p + 1) * width]
                    part = _dot(a_ref[p], bp, dims)
                    res = part if res is None else res + part
            o_ref[...] = res.astype(o_ref.dtype)

        a_spec = (pl.BlockSpec((planes, tm, width), lambda i: (0, i, 0)) if a.ndim == 3
                  else pl.BlockSpec((tm, kk), lambda i: (i, 0)))
        return _call(
            body, name, S((m, n), out_dtype), (m // tm,),
            [a_spec, pl.BlockSpec(b.shape, lambda i: (0, 0), pipeline_mode=pl.Buffered(1))],
            pl.BlockSpec((tm, n), lambda i: (i, 0)), ("parallel",), (a, b), comm=comm)

    tn = _pick(n, 512)

    def body(a_ref, b_ref, o_ref):
        o_ref[...] = _dot(a_ref[...], b_ref[...], dims).astype(o_ref.dtype)

    b_spec = (pl.BlockSpec((kk, tn), lambda j: (0, j)) if mode == "nn" else pl.BlockSpec((tn, kk), lambda j: (j, 0)))
    return _call(
        body, name, S((m, n), out_dtype), (n // tn,), [pl.BlockSpec((m, kk), lambda j: (0, 0)), b_spec],
        pl.BlockSpec((m, tn), lambda j: (0, j)), ("parallel",), (a, b), comm=comm)


def _rms(x, w):
    return x * lax.rsqrt(jnp.mean(x * x, axis=-1, keepdims=True) + RMS_EPS) * w


def _row_tile(rows):
    return rows // 8


def _embed_norm(meta, x, w, rows, name, comm=None):
    n_meta, d = meta.shape
    n_x = x.shape[0]
    last = rows // T - 1
    assert n_meta % 8 == 0 and n_meta < T and n_meta + n_x == last * T + n_meta and last * T >= n_x

    def body(m_ref, x_ref, w_ref, h_ref, hn_ref):
        i = pl.program_id(0)

        @pl.when(i == 0)
        def _():
            h_ref[0:n_meta, :] = m_ref[...]
            h_ref[n_meta:T, :] = x_ref[0:T - n_meta, :]

        @pl.when((i > 0) & (i < last))
        def _():
            h_ref[...] = x_ref[pl.ds(pl.multiple_of(i * T - n_meta, 8), T), :]

        @pl.when(i == last)
        def _():
            h_ref[0:n_meta, :] = x_ref[n_x - n_meta:n_x, :]
            h_ref[n_meta:T, :] = jnp.zeros((T - n_meta, d), F32)

        hn_ref[...] = _rms(h_ref[...], w_ref[...]).astype(hn_ref.dtype)

    row = pl.BlockSpec((T, d), lambda i: (i, 0))
    return _call(body, name, [S((rows, d), F32), S((rows, d), BF16)], (rows // T,),
                 [pl.BlockSpec((n_meta, d), lambda i: (0, 0)), pl.BlockSpec((n_x, d), lambda i: (0, 0)),
                  pl.BlockSpec((1, d), lambda i: (0, 0))], [row, row], ("parallel",), (meta, x, w), comm=comm)


def _resid_norm(h, br, w_post, next_ws, name):
    rows, d = h.shape
    tr = _row_tile(rows)
    has_br = br is not None
    nw = len(next_ws)

    def body(*refs):
        h_ref = refs[0]
        pos = 1
        x = h_ref[...]
        if has_br:
            x = x + _rms(refs[1][...], refs[2][...])
            pos = 3
        w_refs = refs[pos:pos + nw]
        outs = refs[pos + nw:]
        if has_br:
            outs[0][...] = x
            outs = outs[1:]
        for w_ref, o_ref in zip(w_refs, outs):
            o_ref[...] = _rms(x, w_ref[...]).astype(o_ref.dtype)

    row = pl.BlockSpec((tr, d), lambda i: (i, 0))
    vec = pl.BlockSpec((1, d), lambda i: (0, 0))
    ins = [h] + ([br, w_post] if has_br else []) + list(next_ws)
    in_specs = [row] + ([row, vec] if has_br else []) + [vec] * nw
    out_shape = ([S((rows, d), F32)] if has_br else []) + [S((rows, d), BF16)] * nw
    res = pl.pallas_call(body, name=name, out_shape=out_shape, grid=(rows // tr,), in_specs=in_specs,
                         out_specs=[row] * len(out_shape), compiler_params=_cp(("parallel",)))(*ins)
    if has_br:
        return res[0], list(res[1:])
    return h, list(res)


def _norm_bwd(x, w, dy, add, out_dtype, name, comm=None):
    rows, d = x.shape
    tr = _row_tile(rows)
    has_add = add is not None

    def body(*refs):
        x_ref, w_ref, dy_ref = refs[:3]
        dx_ref, dw_ref = refs[-2:]
        xv = x_ref[...]
        r = lax.rsqrt(jnp.mean(xv * xv, axis=-1, keepdims=True) + RMS_EPS)
        dyv = dy_ref[...].astype(F32)
        wdy = dyv * w_ref[...]
        dx = r * wdy - xv * (r * r * r) * jnp.mean(xv * wdy, axis=-1, keepdims=True)
        if has_add:
            dx = dx + refs[3][...]
        dx_ref[...] = dx.astype(dx_ref.dtype)

        @pl.when(pl.program_id(0) == 0)
        def _():
            dw_ref[...] = jnp.zeros_like(dw_ref)

        dw_ref[...] += jnp.sum(dyv * xv * r, axis=0, keepdims=True)

    row = pl.BlockSpec((tr, d), lambda i: (i, 0))
    vec = pl.BlockSpec((1, d), lambda i: (0, 0))
    ins = [x, w, dy] + ([add] if has_add else [])
    return _call(body, name, [S((rows, d), out_dtype), S((1, d), F32)], (rows // tr,),
                 [row, vec, row] + ([row] if has_add else []), [row, vec], ("arbitrary",), ins, comm=comm)


def _final_loss(h, br, w_post, tgt, name):
    rows, d = h.shape
    tr = _row_tile(rows)

    def body(h_ref, br_ref, w_ref, t_ref, dh_ref, loss_ref):
        i = pl.program_id(0)
        y = h_ref[...] + _rms(br_ref[...], w_ref[...])
        r = i * tr + lax.broadcasted_iota(jnp.int32, (tr, 1), 0)
        real = (r >= N_META) & (r < N_META + SEQ)
        diff = jnp.where(real, y - t_ref[...], 0.0)
        dh_ref[...] = diff * (1.0 / d)

        @pl.when(i == 0)
        def _():
            loss_ref[...] = jnp.zeros_like(loss_ref)

        loss_ref[...] += jnp.sum(diff * diff) * (0.5 / d)

    row = pl.BlockSpec((tr, d), lambda i: (i, 0))
    return pl.pallas_call(body, name=name, out_shape=[S((rows, d), F32), S((1, LANE), F32)], grid=(rows // tr,),
                          in_specs=[row, row, pl.BlockSpec((1, d), lambda i: (0, 0)), row],
                          out_specs=[row, pl.BlockSpec((1, LANE), lambda i: (0, 0))],
                          compiler_params=_cp(("arbitrary",)))(h, br, w_post, tgt)


def _gatenorm_fwd(y, zx, w, name, comm=None):
    rows, d = y.shape
    tr = _row_tile(rows)

    def body(y_ref, z_ref, w_ref, o_ref):
        z = z_ref[...]
        o_ref[...] = _rms(y_ref[...] * z * _sigmoid(z), w_ref[...]).astype(o_ref.dtype)

    row = pl.BlockSpec((tr, d), lambda i: (i, 0))
    return _call(body, name, S((rows, d), BF16), (rows // tr,), [row, row, pl.BlockSpec((1, d), lambda i: (0, 0))],
                 row, ("parallel",), (y, zx, w), comm=comm)


def _gatenorm_bwd(y, zx, w, dyn, name, comm=None):
    rows, d = y.shape
    tr = _row_tile(rows)

    def body(y_ref, z_ref, w_ref, dyn_ref, dy_ref, dz_ref, dw_ref):
        yv, z = y_ref[...], z_ref[...]
        sg = _sigmoid(z)
        sz = z * sg
        g = yv * sz
        r = lax.rsqrt(jnp.mean(g * g, axis=-1, keepdims=True) + RMS_EPS)
        dyn_v = dyn_ref[...]
        wdy = dyn_v * w_ref[...]
        dg = r * wdy - g * (r * r * r) * jnp.mean(g * wdy, axis=-1, keepdims=True)
        dy_ref[...] = dg * sz
        dz_ref[...] = (dg * yv * sg * (1.0 + z * (1.0 - sg))).astype(dz_ref.dtype)

        @pl.when(pl.program_id(0) == 0)
        def _():
            dw_ref[...] = jnp.zeros_like(dw_ref)

        dw_ref[...] += jnp.sum(dyn_v * g * r, axis=0, keepdims=True)

    row = pl.BlockSpec((tr, d), lambda i: (i, 0))
    vec = pl.BlockSpec((1, d), lambda i: (0, 0))
    return _call(body, name, [S((rows, d), F32), S((rows, d), BF16), S((1, d), F32)], (rows // tr,),
                 [row, row, vec, row], [row, row, vec], ("arbitrary",), (y, zx, w, dyn), comm=comm)


def _shift_down(x, s, rows_iota):
    if s == 0:
        return x
    return jnp.where(rows_iota >= s, pltpu.roll(x, s, 0), 0.0)


def _shift_up(x, s, rows_iota):
    if s == 0:
        return x
    rows = x.shape[0]
    return jnp.where(rows_iota < rows - s, pltpu.roll(x, rows - s, 0), 0.0)


def _r16(v):
    return v.astype(BF16).astype(F32)


def _conv(x, w_ref, b_ref, taps, rows_iota):
    x = _r16(x)
    acc = jnp.zeros_like(x)
    for k in range(taps):
        acc = acc + _r16(w_ref[k:k + 1, :]) * _shift_down(x, taps - 1 - k, rows_iota)
    return acc + b_ref[...]


def _conv_bwd(x, du, w_ref, dw_ref, db_ref, taps, rows_iota):
    db_ref[...] = jnp.sum(du, axis=0, keepdims=True)
    x, du = _r16(x), _r16(du)
    dx = jnp.zeros_like(x)
    for k in range(taps):
        s = taps - 1 - k
        dx = dx + _r16(w_ref[k:k + 1, :]) * _shift_up(du, s, rows_iota)
        dw_ref[k:k + 1, :] = jnp.sum(du * _shift_down(x, s, rows_iota), axis=0, keepdims=True)
    return dx


def _conv_silu_fwd(zx, w, b, name, comm=None):
    rows = zx.shape[0]
    cb = 512
    off = D_INNER // cb

    def body(x_ref, w_ref, b_ref, o_ref):
        it = lax.broadcasted_iota(jnp.int32, (rows, 1), 0)
        u = _conv(x_ref[...], w_ref, b_ref, SSM_CONV, it)
        o_ref[...] = u * _sigmoid(u)

    return _call(
        body, name, S((rows, D_XBC), F32), (D_XBC // cb,),
        [pl.BlockSpec((rows, cb), lambda j: (0, off + j)), pl.BlockSpec((SSM_CONV, cb), lambda j: (0, j)),
         pl.BlockSpec((1, cb), lambda j: (0, j))],
        pl.BlockSpec((rows, cb), lambda j: (0, j)), ("parallel",), (zx, w, b), comm=comm)


def _conv_silu_bwd(zx, dxs, dbm, dcm, w, b, name):
    rows = zx.shape[0]
    cb = 256
    off = D_INNER // cb
    nx, nbc = D_INNER // cb, D_BC // cb

    def body(x_ref, dx_in, db_in, dc_in, w_ref, b_ref, dx_ref, dw_ref, db_ref, dbuf):
        j = pl.program_id(0)
        for cond, src in ((j < nx, dx_in), ((j >= nx) & (j < nx + nbc), db_in), (j >= nx + nbc, dc_in)):
            @pl.when(cond)
            def _(src=src):
                dbuf[...] = src[...]
        it = lax.broadcasted_iota(jnp.int32, (rows, 1), 0)
        x = x_ref[...]
        u = _conv(x, w_ref, b_ref, SSM_CONV, it)
        sg = _sigmoid(u)
        du = dbuf[...] * sg * (1.0 + u * (1.0 - sg))
        dx_ref[...] = _conv_bwd(x, du, w_ref, dw_ref, db_ref, SSM_CONV, it).astype(dx_ref.dtype)

    def part(first, count):
        return pl.BlockSpec((rows, cb), lambda j: (0, jnp.clip(j - first, 0, count - 1)))

    col = pl.BlockSpec((rows, cb), lambda j: (0, j))
    wsp = pl.BlockSpec((SSM_CONV, cb), lambda j: (0, j))
    bsp = pl.BlockSpec((1, cb), lambda j: (0, j))
    return pl.pallas_call(
        body, name=name, out_shape=[S((rows, D_XBC), BF16), S((SSM_CONV, D_XBC), F32), S((1, D_XBC), F32)],
        grid=(D_XBC // cb,),
        in_specs=[pl.BlockSpec((rows, cb), lambda j: (0, off + j)), part(0, nx), part(nx, nbc), part(nx + nbc, nbc),
                  wsp, bsp],
        out_specs=[col, wsp, bsp], scratch_shapes=[pltpu.VMEM((rows, cb), F32)],
        compiler_params=_cp(("arbitrary",)))(zx, dxs, dbm, dcm, w, b)


def _ffn_act_fwd(u, w, b, name, comm=None):
    rows = u.shape[0]
    cb = 256
    nb = D_FF // cb

    def body(g_ref, v_ref, wg_ref, wv_ref, bg_ref, bv_ref, o_ref):
        it = lax.broadcasted_iota(jnp.int32, (rows, 1), 0)
        g = _conv(g_ref[...], wg_ref, bg_ref, FFN_CONV, it)
        v = _conv(v_ref[...], wv_ref, bv_ref, FFN_CONV, it)
        o_ref[...] = (g * _sigmoid(g) * v).astype(o_ref.dtype)

    def sp(r, shift):
        return pl.BlockSpec((r, cb), lambda j: (0, shift + j))

    return _call(
        body, name, S((rows, D_FF), BF16), (nb,),
        [sp(rows, 0), sp(rows, nb), sp(FFN_CONV, 0), sp(FFN_CONV, nb), sp(1, 0), sp(1, nb)],
        sp(rows, 0), ("parallel",), (u, u, w, w, b, b), comm=comm)


def _ffn_act_bwd(u, dact, w, b, name, comm=None):
    rows = u.shape[0]
    cb = 256
    nb = D_FF // cb

    def body(g_ref, v_ref, d_ref, wg_ref, wv_ref, bg_ref, bv_ref, du_ref, dw_ref, db_ref):
        it = lax.broadcasted_iota(jnp.int32, (rows, 1), 0)
        xg, xv = g_ref[...], v_ref[...]
        g = _conv(xg, wg_ref, bg_ref, FFN_CONV, it)
        v = _conv(xv, wv_ref, bv_ref, FFN_CONV, it)
        sg = _sigmoid(g)
        d = d_ref[...]
        dgate = d * v * sg * (1.0 + g * (1.0 - sg))
        dval = d * g * sg
        du_ref[0] = _conv_bwd(xg, dgate, wg_ref, dw_ref.at[0], db_ref.at[0], FFN_CONV, it).astype(du_ref.dtype)
        du_ref[1] = _conv_bwd(xv, dval, wv_ref, dw_ref.at[1], db_ref.at[1], FFN_CONV, it).astype(du_ref.dtype)

    def sp(r, shift):
        return pl.BlockSpec((r, cb), lambda j: (0, shift + j))

    def both(r):
        return pl.BlockSpec((2, r, cb), lambda j: (0, 0, j))

    return _call(
        body, name, [S((2, rows, D_FF), BF16), S((2, FFN_CONV, D_FF), F32), S((2, 1, D_FF), F32)], (nb,),
        [sp(rows, 0), sp(rows, nb), sp(rows, 0), sp(FFN_CONV, 0), sp(FFN_CONV, nb), sp(1, 0), sp(1, nb)],
        [both(rows), both(FFN_CONV), both(1)], ("parallel",), (u, u, dact, w, w, b, b), comm=comm)


def _ssd_consts(dtp_ref, bias_ref, alog_ref, hg):
    lane = lax.broadcasted_iota(jnp.int32, (1, LANE), 1)
    pre = dtp_ref[...] + bias_ref[...]
    dt = _softplus(pre)
    a_row = jnp.where(lane < hg, -jnp.exp(alog_ref[...]), 0.0)
    ri = lax.broadcasted_iota(jnp.int32, (T, T), 0)
    ci = lax.broadcasted_iota(jnp.int32, (T, T), 1)
    cs = _dot_hi((ri >= ci).astype(F32), dt * a_row)
    return pre, dt, a_row, cs, ri, ci, lane


def _ssd_fwd(xbc, zx, bias, alog, dsk, name, comm=None):
    rows = xbc.shape[0]
    nc = rows // T
    hg = SSM_HEADS // SSM_GROUPS
    gw = hg * HEAD_P
    xoff, boff, coff = 0, D_INNER // D_STATE, (D_INNER + D_BC) // D_STATE
    dtoff = (D_INNER + D_XBC) // LANE

    def body(x_ref, b_ref, c_ref, dtp_ref, bias_ref, alog_ref, dsk_ref, y_ref, hst_ref, hs):
        c = pl.program_id(1)

        @pl.when(c == 0)
        def _():
            hs[...] = jnp.zeros_like(hs)

        _, dt, _, cs, ri, ci, _ = _ssd_consts(dtp_ref, bias_ref, alog_ref, hg)
        cst, dtt = cs.T, dt.T
        xt = x_ref[...].T
        bb, cbf = b_ref[...].astype(BF16), c_ref[...].astype(BF16)
        gt = _dot(bb, cbf, _NT)
        causal_t = ci >= ri
        dskv = dsk_ref[...]
        hall = hs[...]
        hst_ref[0, 0] = hall
        yts, new_h = [], []
        for k in range(hg):
            sl = slice(k * HEAD_P, (k + 1) * HEAD_P)
            csc, csr = cs[:, k:k + 1], cst[k:k + 1, :]
            lt = jnp.exp(jnp.where(causal_t, csr - csc, NEG))
            xk = xt[sl, :]
            xdt = xk * dtt[k:k + 1, :]
            hk = hall[sl, :]
            yd = _dot(xdt.astype(BF16), (gt * lt).astype(BF16))
            yo = jnp.exp(csr) * _dot(hk.astype(BF16), cbf, _NT)
            yts.append(yd + yo + dskv[:, k:k + 1] * xk)
            cl = cs[T - 1:T, k:k + 1]
            st = _dot((xdt * jnp.exp(cl - csr)).astype(BF16), bb)
            new_h.append(jnp.exp(cl) * hk + st)
        y_ref[...] = jnp.concatenate(yts, axis=0).T
        hs[...] = jnp.concatenate(new_h, axis=0)

    vec = pl.BlockSpec((1, LANE), lambda g, c: (0, g))
    return _call(
        body, name, [S((rows, D_INNER), F32), S((nc, SSM_GROUPS, gw, D_STATE), F32)], (SSM_GROUPS, nc),
        [pl.BlockSpec((T, gw), lambda g, c: (c, xoff + g)),
         pl.BlockSpec((T, D_STATE), lambda g, c: (c, boff + g)),
         pl.BlockSpec((T, D_STATE), lambda g, c: (c, coff + g)),
         pl.BlockSpec((T, LANE), lambda g, c: (c, dtoff + g)), vec, vec, vec],
        [pl.BlockSpec((T, gw), lambda g, c: (c, g)), pl.BlockSpec((1, 1, gw, D_STATE), lambda g, c: (c, g, 0, 0))],
        ("parallel", "arbitrary"), (xbc, xbc, xbc, zx, bias, alog, dsk),
        scratch=[pltpu.VMEM((gw, D_STATE), F32)], comm=comm)


def _ssd_bwd(xbc, zx, bias, alog, dsk, dy, hst, name, comm=None):
    rows = xbc.shape[0]
    nc = rows // T
    hg = SSM_HEADS // SSM_GROUPS
    gw = hg * HEAD_P
    boff, coff = D_INNER // D_STATE, (D_INNER + D_BC) // D_STATE
    dtoff = (D_INNER + D_XBC) // LANE

    def body(x_ref, b_ref, c_ref, dtp_ref, bias_ref, alog_ref, dsk_ref, dy_ref, hst_ref,
             dx_ref, db_ref, dc_ref, ddtp_ref, dalog_ref, ddsk_ref, dbias_ref, dhs):
        step = pl.program_id(1)

        @pl.when(step == 0)
        def _():
            dhs[...] = jnp.zeros_like(dhs)
            dalog_ref[...] = jnp.zeros_like(dalog_ref)
            ddsk_ref[...] = jnp.zeros_like(ddsk_ref)
            dbias_ref[...] = jnp.zeros_like(dbias_ref)

        pre, dt, a_row, cs, ri, ci, lane = _ssd_consts(dtp_ref, bias_ref, alog_ref, hg)
        cst, dtt = cs.T, dt.T
        xt, dyt = x_ref[...].T, dy_ref[...].T
        bb, cbf = b_ref[...].astype(BF16), c_ref[...].astype(BF16)
        gt = _dot(bb, cbf, _NT)
        causal_t = ci >= ri
        dskv = dsk_ref[...]
        hall, dhall = hst_ref[0, 0], dhs[...]
        head_row = lax.broadcasted_iota(jnp.int32, (T, 1), 0)
        last_l = lax.broadcasted_iota(jnp.int32, (1, T), 1) == T - 1
        dgt = jnp.zeros((T, T), F32)
        dc_acc = jnp.zeros((T, D_STATE), F32)
        db_acc = jnp.zeros((T, D_STATE), F32)
        ddt_rows = jnp.zeros((T, T), F32)
        dcs_rows = jnp.zeros((T, T), F32)
        qrow_cols = jnp.zeros((T, LANE), F32)
        ddsk_acc = jnp.zeros((1, LANE), F32)
        dxts, new_dh = [], []
        for k in range(hg):
            sl = slice(k * HEAD_P, (k + 1) * HEAD_P)
            csc, csr = cs[:, k:k + 1], cst[k:k + 1, :]
            lt = jnp.exp(jnp.where(causal_t, csr - csc, NEG))
            xk, dyk = xt[sl, :], dyt[sl, :]
            dtr, dk = dtt[k:k + 1, :], dskv[:, k:k + 1]
            xdt = xk * dtr
            mpt = gt * lt
            dyb = dyk.astype(BF16)
            dxdt = _dot(dyb, mpt.astype(BF16), _NT)
            dmt = _dot(xdt.astype(BF16), dyb, _TN)
            dgt = dgt + dmt * lt
            q = dmt * mpt
            q_rows = jnp.sum(q, axis=1, keepdims=True)
            q_cols = jnp.sum(q, axis=0, keepdims=True)
            hk, dhn = hall[sl, :], dhall[sl, :]
            e = jnp.exp(csr)
            cl = cs[T - 1:T, k:k + 1]
            wdec = jnp.exp(cl - csr)
            w = wdec * dtr
            rt = _dot(dhn.astype(BF16), bb, _NT)
            dxts.append(dtr * dxdt + dk * dyk + rt * w)
            xz = jnp.sum(xk * dxdt, axis=0, keepdims=True)
            dw = jnp.sum(rt * xk, axis=0, keepdims=True)
            dcl = jnp.exp(cl) * jnp.sum(dhn * hk) + jnp.sum(dw * w)
            yo = e * _dot(hk.astype(BF16), cbf, _NT)
            dcs_r = jnp.sum(dyk * yo, axis=0, keepdims=True) + q_cols - dw * w + jnp.where(last_l, dcl, 0.0)
            dye = (dyk * e).astype(BF16)
            dc_acc = dc_acc + _dot(dye, hk.astype(BF16), _TN)
            db_acc = db_acc + _dot((xk * w).astype(BF16), dhn.astype(BF16), _TN)
            new_dh.append(jnp.exp(cl) * dhn + _dot(dye, cbf))
            onehot = (lane == k).astype(F32)
            ddt_rows = ddt_rows + jnp.where(head_row == k, xz + dw * wdec, 0.0)
            dcs_rows = dcs_rows + jnp.where(head_row == k, dcs_r, 0.0)
            qrow_cols = qrow_cols + q_rows * onehot
            ddsk_acc = ddsk_acc + jnp.sum(dyk * xk) * onehot
        dx_ref[...] = jnp.concatenate(dxts, axis=0).T
        dhs[...] = jnp.concatenate(new_dh, axis=0)
        dc_ref[...] = _dot(dgt.T.astype(BF16), bb) + dc_acc
        db_ref[...] = _dot(dgt.astype(BF16), cbf) + db_acc
        da = _dot_hi((ci >= ri).astype(F32), dcs_rows.T - qrow_cols)
        ddtp = (ddt_rows.T + da * a_row) * _sigmoid(pre)
        ddtp = jnp.where(lane < hg, ddtp, 0.0)
        ddtp_ref[...] = ddtp
        dbias_ref[...] += jnp.sum(ddtp, axis=0, keepdims=True)
        dalog_ref[...] += jnp.sum(da * dt, axis=0, keepdims=True) * a_row
        ddsk_ref[...] += ddsk_acc

    def rc(c):
        return nc - 1 - c

    vec = pl.BlockSpec((1, LANE), lambda g, c: (0, g))
    xsp = pl.BlockSpec((T, gw), lambda g, c: (rc(c), g))
    return _call(
        body, name,
        [S((rows, D_INNER), F32), S((rows, D_BC), F32), S((rows, D_BC), F32),
         S((rows, SSM_GROUPS * LANE), F32), S((1, SSM_GROUPS * LANE), F32),
         S((1, SSM_GROUPS * LANE), F32), S((1, SSM_GROUPS * LANE), F32)],
        (SSM_GROUPS, nc),
        [xsp,
         pl.BlockSpec((T, D_STATE), lambda g, c: (rc(c), boff + g)),
         pl.BlockSpec((T, D_STATE), lambda g, c: (rc(c), coff + g)),
         pl.BlockSpec((T, LANE), lambda g, c: (rc(c), dtoff + g)), vec, vec, vec,
         xsp, pl.BlockSpec((1, 1, gw, D_STATE), lambda g, c: (rc(c), g, 0, 0))],
        [xsp,
         pl.BlockSpec((T, D_STATE), lambda g, c: (rc(c), g)),
         pl.BlockSpec((T, D_STATE), lambda g, c: (rc(c), g)),
         pl.BlockSpec((T, LANE), lambda g, c: (rc(c), g)), vec, vec, vec],
        ("parallel", "arbitrary"), (xbc, xbc, xbc, zx, bias, alog, dsk, dy, hst),
        scratch=[pltpu.VMEM((gw, D_STATE), F32)], comm=comm)


def _attn_tiles(kv_ref, j):
    prev = jnp.maximum(j - 1, 0)
    meta = kv_ref[0:T, :]
    prv = kv_ref[pl.ds(pl.multiple_of(prev * T, T), T), :]
    cur = kv_ref[pl.ds(pl.multiple_of(j * T, T), T), :]
    return jnp.concatenate([meta, prv, cur], axis=0)


def _attn_mask(j):
    r = j * T + lax.broadcasted_iota(jnp.int32, (3 * T, T), 1)
    row = lax.broadcasted_iota(jnp.int32, (3 * T, T), 0)
    t0, t1 = row < T, row < 2 * T
    s = jnp.where(t0, row, (j - 2) * T + row)
    ok = (s <= r) & ((s < N_META) | (s > r - WINDOW))
    use = (t0 & (j >= 2) & (row < N_META)) | (jnp.logical_not(t0) & t1 & (j >= 1)) | jnp.logical_not(t1)
    return ok & use


def _attn_fwd(q, kv, sinks, name, comm=None):
    rows = q.shape[0]
    scale = 1.0 / math.sqrt(ATTN_DH)
    qpk = N_Q_HEADS // N_KV_HEADS

    def body(q_ref, kv_ref, s_ref, o_ref, lse_ref):
        j = pl.program_id(0)
        kv3 = _attn_tiles(kv_ref, j).astype(BF16)
        mask = _attn_mask(j)
        qv = (q_ref[...] * scale).astype(BF16)
        sk = s_ref[...]
        lses = []
        for kh in range(N_KV_HEADS):
            k3 = kv3[:, kh * ATTN_DH:(kh + 1) * ATTN_DH]
            v3 = kv3[:, D_KV + kh * ATTN_DH:D_KV + (kh + 1) * ATTN_DH]
            for g in range(qpk):
                h = kh * qpk + g
                sink = sk[:, h:h + 1]
                sc = jnp.where(mask, _dot(k3, qv[:, h * ATTN_DH:(h + 1) * ATTN_DH], _NT), NEG)
                m = jnp.maximum(jnp.max(sc, axis=0, keepdims=True), sink)
                p = jnp.exp(sc - m)
                den = jnp.sum(p, axis=0, keepdims=True) + jnp.exp(sink - m)
                p = p * (1.0 / den)
                lses.append(m + jnp.log(den))
                o_ref[:, h * ATTN_DH:(h + 1) * ATTN_DH] = _dot(p.astype(BF16), v3, _TN).astype(o_ref.dtype)
        lse_ref[...] = jnp.concatenate(lses, axis=0)

    return _call(
        body, name, [S((rows, D_MODEL), BF16), S((N_Q_HEADS, rows), F32)], (rows // T,),
        [pl.BlockSpec((T, D_MODEL), lambda j: (j, 0)), pl.BlockSpec((rows, 2 * D_KV), lambda j: (0, 0)),
         pl.BlockSpec((1, N_Q_HEADS), lambda j: (0, 0))],
        [pl.BlockSpec((T, D_MODEL), lambda j: (j, 0)), pl.BlockSpec((N_Q_HEADS, T), lambda j: (0, j))],
        ("parallel",), (q, kv, sinks), comm=comm)


def _attn_bwd(q, kv, sinks, do, lse, name, comm=None):
    rows = q.shape[0]
    scale = 1.0 / math.sqrt(ATTN_DH)
    qpk = N_Q_HEADS // N_KV_HEADS

    def body(q_ref, kv_ref, s_ref, do_ref, lse_ref, dq_ref, dkv_ref, ds_ref):
        j = pl.program_id(0)

        @pl.when(j == 0)
        def _():
            dkv_ref[...] = jnp.zeros_like(dkv_ref)
            ds_ref[...] = jnp.zeros_like(ds_ref)

        kv3 = _attn_tiles(kv_ref, j).astype(BF16)
        mask = _attn_mask(j)
        qv = (q_ref[...] * scale).astype(BF16)
        dov = do_ref[...].astype(BF16)
        sk = s_ref[...]
        lsev = lse_ref[...]
        lane = lax.broadcasted_iota(jnp.int32, (1, LANE), 1)
        ds_acc = jnp.zeros((1, LANE), F32)
        prev = jnp.maximum(j - 1, 0)
        dqts = []
        for kh in range(N_KV_HEADS):
            ksl = slice(kh * ATTN_DH, (kh + 1) * ATTN_DH)
            vsl = slice(D_KV + kh * ATTN_DH, D_KV + (kh + 1) * ATTN_DH)
            k3, v3 = kv3[:, ksl], kv3[:, vsl]
            k3t = k3.T
            dk3 = jnp.zeros((3 * T, ATTN_DH), F32)
            dv3 = jnp.zeros((3 * T, ATTN_DH), F32)
            for g in range(qpk):
                h = kh * qpk + g
                hs = slice(h * ATTN_DH, (h + 1) * ATTN_DH)
                qh, doh = qv[:, hs], dov[:, hs]
                lh = lsev[h:h + 1, :]
                p = jnp.exp(jnp.where(mask, _dot(k3, qh, _NT), NEG) - lh)
                ps = jnp.exp(sk[:, h:h + 1] - lh)
                dp = _dot(v3, doh, _NT)
                delta = jnp.sum(p * dp, axis=0, keepdims=True)
                dsc = (p * (dp - delta)).astype(BF16)
                dqts.append(_dot(k3t, dsc) * scale)
                dk3 = dk3 + _dot(dsc, qh)
                dv3 = dv3 + _dot(p.astype(BF16), doh)
                ds_acc = ds_acc - jnp.sum(ps * delta) * (lane == h).astype(F32)
            for t, start in enumerate((0, pl.multiple_of(prev * T, T), pl.multiple_of(j * T, T))):
                rsl = pl.ds(start, T)
                dkv_ref[rsl, ksl] += dk3[t * T:(t + 1) * T, :]
                dkv_ref[rsl, vsl] += dv3[t * T:(t + 1) * T, :]
        ds_ref[...] += ds_acc
        dq_ref[...] = jnp.concatenate(dqts, axis=0).T.astype(dq_ref.dtype)

    blk = pl.BlockSpec((T, D_MODEL), lambda j: (j, 0))
    full = pl.BlockSpec((rows, 2 * D_KV), lambda j: (0, 0))
    return _call(
        body, name, [S((rows, D_MODEL), BF16), S((rows, 2 * D_KV), F32), S((1, LANE), F32)], (rows // T,),
        [blk, full, pl.BlockSpec((1, N_Q_HEADS), lambda j: (0, 0)), blk, pl.BlockSpec((N_Q_HEADS, T), lambda j: (0, j))],
        [blk, full, pl.BlockSpec((1, LANE), lambda j: (0, 0))], ("arbitrary",), (q, kv, sinks, do, lse), comm=comm)


BLOCK_BYTES = 1 << 20


def _div_tile(rows, cols):
    cap = max(16, BLOCK_BYTES // (4 * cols))
    best = None
    for t in range(16, min(rows, cap) + 1, 16):
        if rows % t == 0:
            best = t
    return best if best is not None else rows


def _adamw(parts, w, m, v, name, comm=None):
    layers, rows, cols = w.shape
    n = parts[0].shape[0]
    tr = _div_tile(rows, cols)
    tc = _pick(cols, 256) if tr == rows and rows * cols * 4 > 2 * BLOCK_BYTES else cols
    c1 = 1.0 / (1.0 - B1 ** STEP)
    c2 = 1.0 / (1.0 - B2 ** STEP)

    def body(*refs):
        p_refs = refs[:layers]
        w_ref, m_ref, v_ref, g_ref, d_ref, nm_ref, nv_ref = refs[layers:]
        layer = pl.program_id(0)
        for l in range(layers):
            @pl.when(layer == l)
            def _(p_ref=p_refs[l]):
                g = p_ref[0].astype(F32)
                for i in range(1, n):
                    g = g + p_ref[i].astype(F32)
                nm = B1 * m_ref[...] + (1.0 - B1) * g
                nv = B2 * v_ref[...] + (1.0 - B2) * (g * g)
                g_ref[...] = g
                nm_ref[...] = nm
                nv_ref[...] = nv
                d_ref[...] = -LR * ((nm * c1) / (jnp.sqrt(nv * c2) + EPS) + WD * w_ref[...])

    def part_spec(l):
        return pl.BlockSpec((n, tr, tc), lambda k, i, j: (0, jnp.where(k == l, i, 0), jnp.where(k == l, j, 0)))

    row = pl.BlockSpec((None, tr, tc), lambda k, i, j: (k, i, j))
    return _call(body, name, [S((layers, rows, cols), F32)] * 4, (layers, rows // tr, cols // tc),
                 [part_spec(l) for l in range(layers)] + [row, row, row], [row] * 4,
                 ("parallel", "parallel", "parallel"), (*parts, w, m, v), comm=comm)


def _sum_parts(parts, name):
    n, rows, cols = parts[0].shape
    nb = len(parts)
    tr = _div_tile(rows, cols)

    def body(*refs):
        o_ref = refs[nb]
        blk = pl.program_id(0)
        for l in range(nb):
            @pl.when(blk == l)
            def _(p_ref=refs[l]):
                g = p_ref[0].astype(F32)
                for i in range(1, n):
                    g = g + p_ref[i].astype(F32)
                o_ref[...] = g

    def part_spec(l):
        return pl.BlockSpec((n, tr, cols), lambda k, i: (0, jnp.where(k == l, i, 0), 0))

    per = rows // tr
    return pl.pallas_call(body, name=name, out_shape=S((nb * rows, cols), F32), grid=(nb, per),
                          in_specs=[part_spec(l) for l in range(nb)],
                          out_specs=pl.BlockSpec((tr, cols), lambda k, i: (k * per + i, 0)),
                          compiler_params=_cp(("parallel", "parallel")))(*parts)


def _col_segments(ws, runs):
    segs = []
    for glo, mlo, n in runs:
        while n > 0:
            d, off = divmod(glo, ws)
            take = min(n, ws - off)
            segs.append((d, off, mlo, take))
            glo, mlo, n = glo + take, mlo + take, n - take
    return segs


def _assemble_cols(gs, width, segs, name):
    _, rows, ws = gs[0].shape
    nb = len(gs)
    rb = _div_tile(rows, width // 2)
    per = rows // rb

    def body(*refs):
        o_ref = refs[nb]
        piece = pl.program_id(0)
        for l in range(nb):
            @pl.when(piece == l)
            def _(g_ref=refs[l]):
                o_ref[...] = jnp.zeros_like(o_ref)
                for d, off, mlo, n in segs:
                    o_ref[:, mlo:mlo + n] = g_ref[d, :, off:off + n]

    def piece_spec(l):
        return pl.BlockSpec((N_DEV, rb, ws), lambda k, i: (0, jnp.where(k == l, i, 0), 0))

    return pl.pallas_call(
        body, name=name, out_shape=S((nb * rows, width), gs[0].dtype), grid=(nb, per),
        in_specs=[piece_spec(l) for l in range(nb)],
        out_specs=pl.BlockSpec((rb, width), lambda k, i: (k * per + i, 0)),
        compiler_params=_cp(("parallel", "parallel")))(*gs)


def _scatter_cols(dw, ws, segs, name):
    rows, width = dw.shape
    rb = _div_tile(rows, width)

    def body(w_ref, o_ref):
        for d, off, mlo, n in segs:
            o_ref[d, :, off:off + n] = w_ref[:, mlo:mlo + n].astype(o_ref.dtype)

    return pl.pallas_call(
        body, name=name, out_shape=S((N_DEV, rows, ws), BF16), grid=(rows // rb,),
        in_specs=[pl.BlockSpec((rb, width), lambda i: (i, 0))],
        out_specs=pl.BlockSpec((N_DEV, rb, ws), lambda i: (0, i, 0)), compiler_params=_cp(("parallel",)))(dw)


def _gather_comm(xs):
    n = len(xs)

    def setup(x_refs, out_refs, sems):
        send_sems, recv_sems, local_sems = sems
        mx, my, mc = lax.axis_index("x"), lax.axis_index("y"), lax.axis_index("c")
        me, sibling = (mx, my, mc), (mx, my, 1 - mc)
        chips = [(1 - mx, my), (mx, 1 - my), (1 - mx, 1 - my)]

        def blk(a, px, py, pc):
            return out_refs[a].at[4 * px + 2 * py + pc]

        def copy(a, k, block, to, src=None):
            return pltpu.make_async_remote_copy(
                src_ref=blk(a, *block) if src is None else src, dst_ref=blk(a, *block),
                send_sem=send_sems.at[a, k], recv_sem=recv_sems.at[a, k], device_id=to, device_id_type=_MESH)

        mine = [pltpu.make_async_copy(x_refs[a], blk(a, *me), local_sems.at[a]) for a in range(n)]
        own = []
        for a in range(n):
            own.append(copy(a, 0, me, sibling, src=x_refs[a]))
            own += [copy(a, 1 + i, me, (*chip, mc), src=x_refs[a]) for i, chip in enumerate(chips)]
        return me, sibling, chips, mc, copy, mine, own

    def first(x_refs, out_refs, sems):
        _, _, _, _, _, mine, own = setup(x_refs, out_refs, sems)
        for cp in mine + own:
            cp.start()

    def last(x_refs, out_refs, sems):
        me, sibling, chips, mc, copy, mine, own = setup(x_refs, out_refs, sems)
        passed = []
        for a in range(n):
            for i, chip in enumerate(chips):
                copy(a, 1 + i, (*chip, mc), me).wait_recv()
                passed.append(copy(a, 4 + i, (*chip, mc), sibling))
                passed[-1].start()
        for a in range(n):
            copy(a, 0, sibling, me).wait_recv()
            for i, chip in enumerate(chips):
                copy(a, 4 + i, (*chip, 1 - mc), me).wait_recv()
        for cp in own + passed:
            cp.wait_send()
        for cp in mine:
            cp.wait()

    return _Comm(list(xs), [S((N_DEV,) + x.shape, x.dtype) for x in xs],
                 [pltpu.SemaphoreType.DMA((n, 7)), pltpu.SemaphoreType.DMA((n, 7)), pltpu.SemaphoreType.DMA((n,))],
                 first, last)


def _swap_comm(gs):
    n = len(gs)

    def copies(g_refs, out_refs, sems):
        send_sems, recv_sems = sems
        mx, my, mc = lax.axis_index("x"), lax.axis_index("y"), lax.axis_index("c")
        return [pltpu.make_async_remote_copy(
            src_ref=g_refs[a].at[2 * k + 1 - mc], dst_ref=out_refs[a].at[k], send_sem=send_sems.at[a, k],
            recv_sem=recv_sems.at[a, k], device_id=(mx, my, 1 - mc), device_id_type=_MESH)
            for a in range(n) for k in range(4)]

    def first(g_refs, out_refs, sems):
        for cp in copies(g_refs, out_refs, sems):
            cp.start()

    def last(g_refs, out_refs, sems):
        for cp in copies(g_refs, out_refs, sems):
            cp.wait()

    return _Comm(list(gs), [S((4,) + g.shape[1:], g.dtype) for g in gs],
                 [pltpu.SemaphoreType.DMA((n, 4)), pltpu.SemaphoreType.DMA((n, 4))], first, last)


def _chips_comm(parts):
    n = len(parts)

    def copies(p_refs, out_refs, sems):
        send_sems, recv_sems, local_sems = sems
        mx, my, mc = lax.axis_index("x"), lax.axis_index("y"), lax.axis_index("c")
        mychip = 2 * mx + my
        chips = [(1 - mx, my), (mx, 1 - my), (1 - mx, 1 - my)]
        mine = [pltpu.make_async_copy(p_refs[a].at[mychip], out_refs[a].at[mychip], local_sems.at[a])
                for a in range(n)]
        return mine + [pltpu.make_async_remote_copy(
            src_ref=p_refs[a].at[2 * cx + cy], dst_ref=out_refs[a].at[mychip], send_sem=send_sems.at[a, i],
            recv_sem=recv_sems.at[a, i], device_id=(cx, cy, mc), device_id_type=_MESH)
            for a in range(n) for i, (cx, cy) in enumerate(chips)]

    def first(p_refs, out_refs, sems):
        for cp in copies(p_refs, out_refs, sems):
            cp.start()

    def last(p_refs, out_refs, sems):
        for cp in copies(p_refs, out_refs, sems):
            cp.wait()

    return _Comm(list(parts), [S(p.shape, p.dtype) for p in parts],
                 [pltpu.SemaphoreType.DMA((n, 3)), pltpu.SemaphoreType.DMA((n, 3)), pltpu.SemaphoreType.DMA((n,))],
                 first, last)


def _join_comms(comms):
    def split(refs, counts):
        out, p = [], 0
        for cnt in counts:
            out.append(refs[p:p + cnt])
            p += cnt
        return out

    ni = [len(c.ins) for c in comms]
    no = [len(c.out_shapes) for c in comms]
    ns = [len(c.scratch) for c in comms]

    def first(in_refs, out_refs, sems):
        for c, i, o, s in zip(comms, split(in_refs, ni), split(out_refs, no), split(sems, ns)):
            c.first(i, o, s)

    def last(in_refs, out_refs, sems):
        for c, i, o, s in zip(comms, split(in_refs, ni), split(out_refs, no), split(sems, ns)):
            c.last(i, o, s)

    return _Comm([x for c in comms for x in c.ins], [x for c in comms for x in c.out_shapes],
                 [x for c in comms for x in c.scratch], first, last)


def _add_pairs(mine, theirs, core, name):
    _, rows, cols = mine.shape
    tr = _div_tile(rows, cols)

    def body(core_ref, a_ref, b_ref, o_ref):
        o_ref[...] = (a_ref[...].astype(F32) + b_ref[...].astype(F32)).astype(o_ref.dtype)

    return pl.pallas_call(
        body, name=name, out_shape=S((4, rows, cols), BF16),
        grid_spec=pltpu.PrefetchScalarGridSpec(
            num_scalar_prefetch=1, grid=(4, rows // tr),
            in_specs=[pl.BlockSpec((None, tr, cols), lambda k, i, c: (2 * k + c[0], i, 0)),
                      pl.BlockSpec((None, tr, cols), lambda k, i, c: (k, i, 0))],
            out_specs=pl.BlockSpec((None, tr, cols), lambda k, i, c: (k, i, 0))),
        compiler_params=_cp(("parallel", "parallel")))(core, mine, theirs)


def _run_comm(comm, name):
    ci, co = len(comm.ins), len(comm.out_shapes)

    def body(*refs):
        comm.first(refs[:ci], refs[ci:ci + co], refs[ci + co:])
        comm.last(refs[:ci], refs[ci:ci + co], refs[ci + co:])

    return pl.pallas_call(body, name=name, out_shape=list(comm.out_shapes), in_specs=[_HBM] * ci,
                          out_specs=[_HBM] * co, scratch_shapes=list(comm.scratch))(*comm.ins)


def _flat_rows(n_elems, mult):
    rows = -(-n_elems // LANE)
    return -(-rows // mult) * mult


def _pack(arrs, lead, mult, dtype):
    lead_shape = arrs[0].shape[:lead]
    flat = jnp.concatenate([a.astype(dtype).reshape(lead_shape + (-1,)) for a in arrs], axis=-1)
    n = flat.shape[-1]
    rows = _flat_rows(n, mult)
    flat = jnp.pad(flat, [(0, 0)] * lead + [(0, rows * LANE - n)])
    return flat.reshape(lead_shape + (rows, LANE))


def _unpack(flat, lead, shapes):
    lead_shape = flat.shape[:lead]
    flat = flat.reshape(lead_shape + (-1,))
    out, off = [], 0
    for shp in shapes:
        n = math.prod(shp)
        out.append(flat[..., off:off + n].reshape(lead_shape + tuple(shp)))
        off += n
    return out


def _split8(full, ax, n):
    shp = full.shape
    return jnp.moveaxis(full.reshape(shp[:ax] + (N_DEV, n) + shp[ax + 1:]), ax, 0)


def _join8(g, ax):
    shp = g.shape[1:]
    return jnp.moveaxis(g, 0, ax).reshape(shp[:ax] + (N_DEV * shp[ax],) + shp[ax + 1:])


def _group_lanes(v, hg):
    v = v.reshape(SSM_GROUPS, hg)
    return jnp.pad(v, ((0, 0), (0, LANE - hg))).reshape(1, SSM_GROUPS * LANE)


def _ungroup_lanes(v, hg):
    return v.reshape(SSM_GROUPS, LANE)[:, :hg].reshape(1, SSM_GROUPS * hg)


def kernel(x, meta_tokens, a_norm_pre, a_w_in, a_conv_w, a_conv_b, a_dt_bias, a_a_log, a_d_skip, a_gate_norm, a_w_out, a_norm_post, kv_norm, w_kv, b_norm_pre, b_w_q, b_sinks, b_w_o, b_norm_post, f_norm_pre, f_w_up, f_conv_w, f_conv_b, f_w_down, f_norm_post, loss_target, m_meta_tokens, m_a_norm_pre, m_a_w_in, m_a_conv_w, m_a_conv_b, m_a_dt_bias, m_a_a_log, m_a_d_skip, m_a_gate_norm, m_a_w_out, m_a_norm_post, m_kv_norm, m_w_kv, m_b_norm_pre, m_b_w_q, m_b_sinks, m_b_w_o, m_b_norm_post, m_f_norm_pre, m_f_w_up, m_f_conv_w, m_f_conv_b, m_f_w_down, m_f_norm_post, v_meta_tokens, v_a_norm_pre, v_a_w_in, v_a_conv_w, v_a_conv_b, v_a_dt_bias, v_a_a_log, v_a_d_skip, v_a_gate_norm, v_a_w_out, v_a_norm_post, v_kv_norm, v_w_kv, v_b_norm_pre, v_b_w_q, v_b_sinks, v_b_w_o, v_b_norm_post, v_f_norm_pre, v_f_w_up, v_f_conv_w, v_f_conv_b, v_f_w_down, v_f_norm_post):
    args = locals()
    wts = {n: args[n] for n in WEIGHTS}
    mom = {n: args["m_" + n] for n in WEIGHTS}
    var = {n: args["v_" + n] for n in WEIGHTS}
    mx, my, mc = lax.axis_index("x"), lax.axis_index("y"), lax.axis_index("c")
    me = 4 * mx + 2 * my + mc
    rows = _seq_rows()
    hg = SSM_HEADS // SSM_GROUPS
    d = D_MODEL

    n_main = D_INNER + D_XBC
    ws_in, ws_up = a_w_in.shape[2], f_w_up.shape[2]
    segs_in = _col_segments(ws_in, [(0, 0, n_main)] + [(n_main + hg * g, n_main + LANE * g, hg)
                                                      for g in range(SSM_GROUPS)])
    segs_up = _col_segments(ws_up, [(0, 0, 2 * D_FF)])
    def gather_of(*ws):
        return _gather_comm([w.astype(BF16) for w in ws])

    small_full, = _run_comm(_gather_comm([_pack([wts[n] for n in SMALL], 0, 8, F32)]), "gather_small")
    full = {}
    for n, g in zip(SMALL, _unpack(small_full, 1, [wts[n].shape for n in SMALL])):
        full[n] = _join8(g, SHARD_AXIS[n])
    (h0, hn0), (g_in,) = _embed_norm(full["meta_tokens"], x[0], full["a_norm_pre"], rows, "embed_norm",
                                     comm=gather_of(a_w_in[0]))
    w_in_all = _assemble_cols([g_in], n_main + SSM_GROUPS * LANE, segs_in, "asm_w_in")
    w_up, w_down = [None, None], [None, None]
    bias_g = _group_lanes(wts["a_dt_bias"], hg)
    alog_g = _group_lanes(wts["a_a_log"], hg)
    dsk_g = _group_lanes(wts["a_d_skip"], hg)
    a_conv_w, a_conv_b = full["a_conv_w"][0], full["a_conv_b"]
    f_cw, f_cb = full["f_conv_w"], wts["f_conv_b"]
    fpre, fpost = wts["f_norm_pre"], wts["f_norm_post"]

    tgt = jnp.pad(loss_target[0], ((N_META, rows - N_META - SEQ), (0, 0)))

    zx, (g_out,) = _mm(hn0, w_in_all, "nn", F32, "mm_in", comm=gather_of(a_w_out[0]))
    w_out = g_out.reshape(D_INNER, d)
    xbc, (g_dn1,) = _conv_silu_fwd(zx, a_conv_w, a_conv_b, "conv_a", comm=gather_of(f_w_down[1]))
    (y_ssd, hst), (g_up0,) = _ssd_fwd(xbc, zx, bias_g, alog_g, dsk_g, "ssd_fwd", comm=gather_of(f_w_up[0]))
    w_up[0] = _assemble_cols([g_up0], 2 * D_FF, segs_up, "asm_w_up0")
    yn, (g_kv, g_q) = _gatenorm_fwd(y_ssd, zx, full["a_gate_norm"], "gatenorm", comm=gather_of(w_kv, b_w_q[0]))
    mix_a, (g_o,) = _mm(yn, w_out, "nn", F32, "mm_out", comm=gather_of(b_w_o[0]))
    w_kvf, w_q, w_o = g_kv.reshape(d, 2 * D_KV), g_q.reshape(d, d), g_o.reshape(d, d)
    h1, (fn0,) = _resid_norm(h0, mix_a, full["a_norm_post"], [fpre[0:1]], "resid_a")

    half = d // 2
    u0, (g_dn0,) = _mm(fn0, w_up[0], "nn", F32, "mm_up0", comm=gather_of(f_w_down[0]))
    w_down = [g_dn0.reshape(D_FF, d), g_dn1.reshape(D_FF, d)]
    act0, (g_up1a,) = _ffn_act_fwd(u0, f_cw[0], f_cb[0:1], "ffn_act0", comm=gather_of(f_w_up[1, :half]))
    ffn0 = _mm(act0, w_down[0], "nn", F32, "mm_down0")
    h2, (kvn, bn) = _resid_norm(h1, ffn0, fpost[0:1], [wts["kv_norm"].reshape(1, d), wts["b_norm_pre"]], "resid_f0")
    kv = _mm(kvn, w_kvf, "nn", F32, "mm_kv")
    q = _mm(bn, w_q, "nn", F32, "mm_q")
    (o, lse), (g_up1b,) = _attn_fwd(q, kv, wts["b_sinks"], "attn_fwd", comm=gather_of(f_w_up[1, half:]))
    w_up[1] = _assemble_cols([g_up1a, g_up1b], 2 * D_FF, segs_up, "asm_w_up1")
    mix_b = _mm(o, w_o, "nn", F32, "mm_o")
    h3, (fn1,) = _resid_norm(h2, mix_b, wts["b_norm_post"], [fpre[1:2]], "resid_b")
    u1 = _mm(fn1, w_up[1], "nn", F32, "mm_up1")
    act1 = _ffn_act_fwd(u1, f_cw[1], f_cb[1:2], "ffn_act1")
    ffn1 = _mm(act1, w_down[1], "nn", F32, "mm_down1")
    dh4, loss_row = _final_loss(h3, ffn1, fpost[1:2], tgt, "loss")
    loss = lax.psum(loss_row[0, 0], ("x", "y", "c"))

    grads = {}

    core = mc.astype(jnp.int32).reshape(1)

    def carried(res, comm):
        return res if comm is not None else (res, None)

    def ffn_bwd(dh_out, h_in, fn, u, act, ffn, i, c_dact=None, c_dwdown=None, c_dwup=None):
        dffn, dw_post = _norm_bwd(ffn, fpost[i:i + 1], dh_out, None, BF16, f"nb_fpost{i}")
        dact, got_a = carried(_mm(dffn, w_down[i], "nt", F32, f"mm_dact{i}", comm=c_dact), c_dact)
        dw_down, got_b = carried(_mm(act, dffn, "tn", BF16, f"mm_dwdown{i}", comm=c_dwdown), c_dwdown)
        dw_down = dw_down.reshape(N_DEV, -1, d)
        du, dwc, dbc = _ffn_act_bwd(u, dact, f_cw[i], f_cb[i:i + 1], f"ffn_act_bwd{i}")
        dfn, (s_dn,) = _mm(du, w_up[i], "nt", F32, f"mm_dfn{i}", comm=_swap_comm([dw_down]))
        sum_dn = _add_pairs(dw_down, s_dn, core, f"rs_add_dn{i}")
        dw_up, got_c = carried(_mm(fn, du, "tn", BF16, f"mm_dwup{i}", comm=c_dwup, shard_cols=ws_up), c_dwup)
        (dh_in, dw_pre), (s_up,) = _norm_bwd(h_in, fpre[i:i + 1], dfn, dh_out, F32, f"nb_fpre{i}",
                                             comm=_swap_comm([dw_up]))
        sum_up = _add_pairs(dw_up, s_up, core, f"rs_add_up{i}")
        return dh_in, dict(post=dw_post, sum_down=sum_dn, cw=jnp.concatenate([dwc[0], dwc[1]], axis=1),
                           cb=jnp.concatenate([dbc[0], dbc[1]], axis=1), sum_up=sum_up, pre=dw_pre), got_a, got_b, got_c

    dh3, gf1, _, _, _ = ffn_bwd(dh4, h3, fn1, u1, act1, ffn1, 1)
    dmix_b, grads["b_norm_post"] = _norm_bwd(mix_b, wts["b_norm_post"], dh3, None, BF16, "nb_bpost")
    do = _mm(dmix_b, w_o, "nt", F32, "mm_do")
    dw_o = _mm(o, dmix_b, "tn", BF16, "mm_dwo").reshape(N_DEV, -1, d)
    (dq, dkv, dsinks), (p_up1, s_o) = _attn_bwd(
        q, kv, wts["b_sinks"], do, lse, "attn_bwd",
        comm=_join_comms([_chips_comm([gf1["sum_up"]]), _swap_comm([dw_o])]))
    sum_o = _add_pairs(dw_o, s_o, core, "rs_add_o")
    grads["b_sinks"] = dsinks[:, :N_Q_HEADS]
    dbn = _mm(dq, w_q, "nt", F32, "mm_dbn")
    dw_q = _mm(bn, dq, "tn", BF16, "mm_dwq").reshape(N_DEV, -1, d)
    dkv16 = dkv.astype(BF16)
    dkvn = _mm(dkv16, w_kvf, "nt", F32, "mm_dkvn")
    dw_kv = _mm(kvn, dkv16, "tn", BF16, "mm_dwkv").reshape(N_DEV, -1, 2 * D_KV)
    (dh2, grads["b_norm_pre"]), (s_q, s_kv) = _norm_bwd(h2, wts["b_norm_pre"], dbn, dh3, F32, "nb_bpre",
                                                        comm=_swap_comm([dw_q, dw_kv]))
    sum_q, sum_kv = _add_pairs(dw_q, s_q, core, "rs_add_q"), _add_pairs(dw_kv, s_kv, core, "rs_add_kv")
    dh2, dw_kvn = _norm_bwd(h2, wts["kv_norm"].reshape(1, d), dkvn, dh2, F32, "nb_kv")
    grads["kv_norm"] = dw_kvn.reshape(d)
    dh1, gf0, (p_o,), (p_q, p_kv), (p_dn1,) = ffn_bwd(
        dh2, h1, fn0, u0, act0, ffn0, 0, c_dact=_chips_comm([sum_o]), c_dwdown=_chips_comm([sum_q, sum_kv]),
        c_dwup=_chips_comm([gf1["sum_down"]]))
    grads["f_norm_post"] = jnp.concatenate([gf0["post"], gf1["post"]], axis=0)
    grads["f_norm_pre"] = jnp.concatenate([gf0["pre"], gf1["pre"]], axis=0)
    grads["f_conv_w"] = jnp.stack([gf0["cw"], gf1["cw"]])
    grads["f_conv_b"] = jnp.concatenate([gf0["cb"], gf1["cb"]], axis=0)

    dmix_a, grads["a_norm_post"] = _norm_bwd(mix_a, full["a_norm_post"], dh1, None, BF16, "nb_apost")
    dyn = _mm(dmix_a, w_out, "nt", F32, "mm_dyn")
    dw_out = _mm(yn, dmix_a, "tn", BF16, "mm_dwout").reshape(N_DEV, -1, d)
    (dy_ssd, dz, grads["a_gate_norm"]), (s_out,) = _gatenorm_bwd(y_ssd, zx, full["a_gate_norm"], dyn, "gatenorm_bwd",
                                                                 comm=_swap_comm([dw_out]))
    sum_out = _add_pairs(dw_out, s_out, core, "rs_add_out")
    (dxs, dbm, dcm, ddtp, dalog, ddsk, dbias), (p_up0, p_dn0, p_out) = _ssd_bwd(
        xbc, zx, bias_g, alog_g, dsk_g, dy_ssd, hst, "ssd_bwd",
        comm=_chips_comm([gf0["sum_up"], gf0["sum_down"], sum_out]))
    grads["a_a_log"] = _ungroup_lanes(dalog, hg)
    grads["a_d_skip"] = _ungroup_lanes(ddsk, hg)
    grads["a_dt_bias"] = _ungroup_lanes(dbias, hg)
    dpre, dcw, dcb = _conv_silu_bwd(zx, dxs, dbm, dcm, a_conv_w, a_conv_b, "conv_a_bwd")
    grads["a_conv_w"], grads["a_conv_b"] = dcw[None], dcb
    dzx = jnp.concatenate([dz, dpre, ddtp.astype(BF16)], axis=1)
    dw_in8 = _scatter_cols(_mm(hn0, dzx, "tn", BF16, "mm_dwin"), ws_in, segs_in, "scat_w_in")
    dhn0, (s_in,) = _mm(dzx, w_in_all, "nt", F32, "mm_dhn0", comm=_swap_comm([dw_in8]))
    sum_in = _add_pairs(dw_in8, s_in, core, "rs_add_in")
    half_in = sum_in.shape[1] // 2
    (dh0, grads["a_norm_pre"]), (p_in_a,) = _norm_bwd(h0, full["a_norm_pre"], dhn0, dh1, F32, "nb_apre",
                                                      comm=_chips_comm([sum_in[:, :half_in]]))
    grad_x = dh0[N_META:N_META + SEQ][None]
    grads["meta_tokens"] = dh0[:N_META]

    small_local = _pack([_split8(grads[n], SHARD_AXIS[n], wts[n].shape[SHARD_AXIS[n]]) for n in SMALL], 1, 8, F32)
    repl_local = _pack([grads[n] for n in REPL], 0, 8, F32)
    n_sr = small_local.shape[1]
    small_vec = jnp.concatenate([small_local.reshape(N_DEV * n_sr, LANE), repl_local], axis=0)
    tail = _join_comms([_chips_comm([sum_in[:, half_in:]]), _gather_comm([small_vec])])
    parts_big = dict(a_w_out=[p_out], w_kv=[p_kv], b_w_q=[p_q], b_w_o=[p_o], f_w_down=[p_dn0, p_dn1])

    def flat_f32(dct, names, mult):
        return _pack([dct[n] for n in names], 0, mult, F32)

    def adamw_big(n, comm=None):
        shp3 = (len(parts_big[n]),) + parts_big[n][0].shape[1:]
        res = _adamw(parts_big[n], *[dct[n].reshape(shp3) for dct in (wts, mom, var)], f"adamw_{n}", comm=comm)
        res, got = res if comm is not None else (res, None)
        big_out[n] = [r.reshape(wts[n].shape) for r in res]
        return got

    big_out = {}
    def swap_last(a):
        return jnp.swapaxes(a, -1, -2)

    g_up_t = swap_last(_sum_parts([p_up0, p_up1], "sum_w_up").reshape(f_w_up.shape))
    res, (p_in_b, small_all) = _adamw([g_up_t[0:1], g_up_t[1:2]], *[swap_last(dct["f_w_up"]) for dct in (wts, mom, var)],
                                      "adamw_f_w_up", comm=tail)
    big_out["f_w_up"] = [swap_last(r) for r in res]
    for n in BIG:
        if n not in ("f_w_up", "a_w_in"):
            adamw_big(n)
    g_in_t = swap_last(_sum_parts([p_in_a, p_in_b], "sum_w_in"))[None]
    res = _adamw([g_in_t], *[swap_last(dct["a_w_in"]) for dct in (wts, mom, var)], "adamw_a_w_in")
    big_out["a_w_in"] = [swap_last(r) for r in res]
    mine_small = lax.dynamic_slice_in_dim(small_all, me * n_sr, n_sr, axis=1)
    parts_small = jnp.concatenate([mine_small, small_all[:, N_DEV * n_sr:]], axis=1)
    sm_in = [jnp.concatenate([flat_f32(dct, SMALL, 8), flat_f32(dct, REPL, 8)], axis=0)[None] for dct in (wts, mom, var)]
    small_out = [r[0] for r in _adamw([parts_small], *sm_in, "adamw_small")]

    outs = []
    for kind in range(4):
        res = {n: big_out[n][kind] for n in BIG}
        for n, a in zip(SMALL, _unpack(small_out[kind][:n_sr], 0, [wts[n].shape for n in SMALL])):
            res[n] = a
        for n, a in zip(REPL, _unpack(small_out[kind][n_sr:], 0, [wts[n].shape for n in REPL])):
            res[n] = a
        outs.append(res)
    return (loss, grad_x, *[outs[0][n] for n in WEIGHTS], *[outs[1][n] for n in WEIGHTS],
            *[outs[2][n] for n in WEIGHTS], *[outs[3][n] for n in WEIGHTS])
```

```python
import functools
import math

import jax
import jax.numpy as jnp
from jax import lax
from jax.experimental import pallas as pl
from jax.experimental.pallas import tpu as pltpu

F32, BF16 = jnp.float32, jnp.bfloat16
S = jax.ShapeDtypeStruct

D_MODEL = 1024
SEQ = 2048
N_META = 16
D_INNER = 2048
HEAD_P = 64
SSM_HEADS = D_INNER // HEAD_P
SSM_GROUPS = 4
D_STATE = 128
SSM_CONV = 4
D_BC = SSM_GROUPS * D_STATE
D_XBC = D_INNER + 2 * D_BC
ATTN_DH = 64
N_Q_HEADS = D_MODEL // ATTN_DH
N_KV_HEADS = 4
D_KV = N_KV_HEADS * ATTN_DH
WINDOW = 128
D_FF = 2816
FFN_CONV = 3
RMS_EPS = 1e-6
NEG = -1e30
LR, B1, B2, EPS, WD, STEP = 0.001, 0.9, 0.999, 1e-08, 0.01, 10

N_DEV = 8
T = 128
LANE = 128
VMEM_LIMIT = 48 * 1024 * 1024

BIG = ("a_w_in", "a_w_out", "w_kv", "b_w_q", "b_w_o", "f_w_up", "f_w_down")
SMALL = ("meta_tokens", "a_norm_pre", "a_conv_w", "a_conv_b", "a_gate_norm", "a_norm_post", "f_conv_w")
REPL = ("a_dt_bias", "a_a_log", "a_d_skip", "kv_norm", "b_norm_pre", "b_sinks", "b_norm_post",
        "f_norm_pre", "f_conv_b", "f_norm_post")
SHARD_AXIS = dict(a_w_in=2, a_w_out=1, w_kv=0, b_w_q=1, b_w_o=1, f_w_up=2, f_w_down=1, meta_tokens=1,
                  a_norm_pre=1, a_conv_w=2, a_conv_b=1, a_gate_norm=1, a_norm_post=1, f_conv_w=2)
WEIGHTS = ("meta_tokens", "a_norm_pre", "a_w_in", "a_conv_w", "a_conv_b", "a_dt_bias", "a_a_log", "a_d_skip",
           "a_gate_norm", "a_w_out", "a_norm_post", "kv_norm", "w_kv", "b_norm_pre", "b_w_q", "b_sinks", "b_w_o",
           "b_norm_post", "f_norm_pre", "f_w_up", "f_conv_w", "f_conv_b", "f_w_down", "f_norm_post")


def _seq_rows():
    return -(-(N_META + SEQ) // T) * T


def _cp(sem=None):
    return pltpu.CompilerParams(dimension_semantics=sem, vmem_limit_bytes=VMEM_LIMIT)


def _pick(n, target):
    t = min(n, target)
    t -= t % LANE
    while n % t:
        t -= LANE
    return t


def _sigmoid(x):
    return 0.5 * jnp.tanh(0.5 * x) + 0.5


def _softplus(x):
    return jnp.maximum(x, 0.0) + jnp.log(1.0 + jnp.exp(-jnp.abs(x)))


_NN = (((1,), (0,)), ((), ()))
_NT = (((1,), (1,)), ((), ()))
_TN = (((0,), (0,)), ((), ()))


def _dot(a, b, dims=_NN):
    return lax.dot_general(a, b, dims, preferred_element_type=F32)


def _dot_hi(a, b):
    return lax.dot_general(a, b, _NN, precision=lax.Precision.HIGHEST, preferred_element_type=F32)


_HBM = pl.BlockSpec(memory_space=pltpu.HBM)
_MESH = pl.DeviceIdType.MESH


class _Comm:
    def __init__(self, ins, out_shapes, scratch, first, last):
        self.ins, self.out_shapes, self.scratch, self.first, self.last = ins, out_shapes, scratch, first, last


def _call(body, name, out_shape, grid, in_specs, out_specs, sem, args, scratch=(), comm=None):
    if comm is None:
        return pl.pallas_call(body, name=name, out_shape=out_shape, grid=grid, in_specs=in_specs, out_specs=out_specs,
                              scratch_shapes=list(scratch), compiler_params=_cp(sem))(*args)
    single = not isinstance(out_shape, (list, tuple))
    outs = [out_shape] if single else list(out_shape)
    ospecs = [out_specs] if single else list(out_specs)
    n_in, n_out, n_scr, ci, co = len(in_specs), len(outs), len(scratch), len(comm.ins), len(comm.out_shapes)

    def carrier(*refs):
        p = 0
        parts = []
        for cnt in (n_in, ci, n_out, co, n_scr, len(comm.scratch)):
            parts.append(refs[p:p + cnt])
            p += cnt
        ins, cins, outs_r, couts, scr, cscr = parts
        ids = [pl.program_id(i) for i in range(len(grid))]
        first, last = ids[0] == 0, ids[0] == grid[0] - 1
        for i in range(1, len(grid)):
            first, last = first & (ids[i] == 0), last & (ids[i] == grid[i] - 1)

        @pl.when(first)
        def _():
            comm.first(cins, couts, cscr)

        body(*ins, *outs_r, *scr)

        @pl.when(last)
        def _():
            comm.last(cins, couts, cscr)

    res = pl.pallas_call(
        carrier, name=name, out_shape=outs + list(comm.out_shapes), grid=grid,
        in_specs=list(in_specs) + [_HBM] * ci, out_specs=ospecs + [_HBM] * co,
        scratch_shapes=list(scratch) + list(comm.scratch),
        compiler_params=_cp(("arbitrary",) * len(grid)))(*args, *comm.ins)
    mine = res[0] if single else list(res[:n_out])
    return mine, list(res[n_out:])


def _mm(a, b, mode, out_dtype, name, comm=None, shard_cols=None):
    if mode == "tn":
        m, kk = a.shape
        planes, width = (b.shape[0], b.shape[2]) if b.ndim == 3 else (1, b.shape[1])
        n = planes * width
        tko, tn = _pick(kk, 512), _pick(width, 1536)
        per = width // tn
        b_spec = (pl.BlockSpec((None, m, tn), lambda i, j: (j // per, 0, j % per)) if b.ndim == 3
                  else pl.BlockSpec((m, tn), lambda i, j: (0, j)))
        if shard_cols is None:
            def body(a_ref, b_ref, o_ref):
                o_ref[...] = _dot(a_ref[...], b_ref[...], _TN).astype(o_ref.dtype)

            out_shape, out_spec = S((kk, n), out_dtype), pl.BlockSpec((tko, tn), lambda i, j: (i, j))
        else:
            shards = tn // shard_cols
            assert tn % shard_cols == 0

            def body(a_ref, b_ref, o_ref):
                res = _dot(a_ref[...], b_ref[...], _TN).astype(o_ref.dtype)
                for p in range(shards):
                    o_ref[p] = res[:, p * shard_cols:(p + 1) * shard_cols]

            out_shape = S((n // shard_cols, kk, shard_cols), out_dtype)
            out_spec = pl.BlockSpec((shards, tko, shard_cols), lambda i, j: (j, i, 0))
        return _call(
            body, name, out_shape, (kk // tko, n // tn), [pl.BlockSpec((m, tko), lambda i, j: (0, i)), b_spec],
            out_spec, ("parallel", "parallel"), (a, b), comm=comm)

    planes, width = (a.shape[0], a.shape[2]) if a.ndim == 3 else (1, a.shape[1])
    m, kk = a.shape[-2], planes * width
    n = b.shape[1] if mode == "nn" else b.shape[0]
    dims = _NN if mode == "nn" else _NT

    if kk > 2048:
        tm = m // 4
        assert m % 4 == 0 and tm % 16 == 0

        def body(a_ref, b_ref, o_ref):
            if a.ndim == 2:
                res = _dot(a_ref[...], b_ref[...], dims)
            else:
                res = None
                for p in range(planes):
                    bp = b_ref[p * width:(p + 1) * width, :] if mode == "nn" else b_ref[:, p * width:(p + 1) * width]
                    part = _dot(a_ref[p], bp, dims)
                    res = part if res is None else res + part
            o_ref[...] = res.astype(o_ref.dtype)

        a_spec = (pl.BlockSpec((planes, tm, width), lambda i: (0, i, 0)) if a.ndim == 3
                  else pl.BlockSpec((tm, kk), lambda i: (i, 0)))
        return _call(
            body, name, S((m, n), out_dtype), (m // tm,),
            [a_spec, pl.BlockSpec(b.shape, lambda i: (0, 0), pipeline_mode=pl.Buffered(1))],
            pl.BlockSpec((tm, n), lambda i: (i, 0)), ("parallel",), (a, b), comm=comm)

    tn = _pick(n, 512)

    def body(a_ref, b_ref, o_ref):
        o_ref[...] = _dot(a_ref[...], b_ref[...], dims).astype(o_ref.dtype)

    b_spec = (pl.BlockSpec((kk, tn), lambda j: (0, j)) if mode == "nn" else pl.BlockSpec((tn, kk), lambda j: (j, 0)))
    return _call(
        body, name, S((m, n), out_dtype), (n // tn,), [pl.BlockSpec((m, kk), lambda j: (0, 0)), b_spec],
        pl.BlockSpec((m, tn), lambda j: (0, j)), ("parallel",), (a, b), comm=comm)


def _rms(x, w):
    return x * lax.rsqrt(jnp.mean(x * x, axis=-1, keepdims=True) + RMS_EPS) * w


def _row_tile(rows):
    return rows // 8


def _embed_norm(meta, x, w, rows, name, comm=None):
    n_meta, d = meta.shape
    n_x = x.shape[0]
    last = rows // T - 1
    assert n_meta % 8 == 0 and n_meta < T and n_meta + n_x == last * T + n_meta and last * T >= n_x

    def body(m_ref, x_ref, w_ref, h_ref, hn_ref):
        i = pl.program_id(0)

        @pl.when(i == 0)
        def _():
            h_ref[0:n_meta, :] = m_ref[...]
            h_ref[n_meta:T, :] = x_ref[0:T - n_meta, :]

        @pl.when((i > 0) & (i < last))
        def _():
            h_ref[...] = x_ref[pl.ds(pl.multiple_of(i * T - n_meta, 8), T), :]

        @pl.when(i == last)
        def _():
            h_ref[0:n_meta, :] = x_ref[n_x - n_meta:n_x, :]
            h_ref[n_meta:T, :] = jnp.zeros((T - n_meta, d), F32)

        hn_ref[...] = _rms(h_ref[...], w_ref[...]).astype(hn_ref.dtype)

    row = pl.BlockSpec((T, d), lambda i: (i, 0))
    return _call(body, name, [S((rows, d), F32), S((rows, d), BF16)], (rows // T,),
                 [pl.BlockSpec((n_meta, d), lambda i: (0, 0)), pl.BlockSpec((n_x, d), lambda i: (0, 0)),
                  pl.BlockSpec((1, d), lambda i: (0, 0))], [row, row], ("parallel",), (meta, x, w), comm=comm)


def _resid_norm(h, br, w_post, next_ws, name):
    rows, d = h.shape
    tr = _row_tile(rows)
    has_br = br is not None
    nw = len(next_ws)

    def body(*refs):
        h_ref = refs[0]
        pos = 1
        x = h_ref[...]
        if has_br:
            x = x + _rms(refs[1][...], refs[2][...])
            pos = 3
        w_refs = refs[pos:pos + nw]
        outs = refs[pos + nw:]
        if has_br:
            outs[0][...] = x
            outs = outs[1:]
        for w_ref, o_ref in zip(w_refs, outs):
            o_ref[...] = _rms(x, w_ref[...]).astype(o_ref.dtype)

    row = pl.BlockSpec((tr, d), lambda i: (i, 0))
    vec = pl.BlockSpec((1, d), lambda i: (0, 0))
    ins = [h] + ([br, w_post] if has_br else []) + list(next_ws)
    in_specs = [row] + ([row, vec] if has_br else []) + [vec] * nw
    out_shape = ([S((rows, d), F32)] if has_br else []) + [S((rows, d), BF16)] * nw
    res = pl.pallas_call(body, name=name, out_shape=out_shape, grid=(rows // tr,), in_specs=in_specs,
                         out_specs=[row] * len(out_shape), compiler_params=_cp(("parallel",)))(*ins)
    if has_br:
        return res[0], list(res[1:])
    return h, list(res)


def _rms_bwd(xv, w, dyv):
    r = lax.rsqrt(jnp.mean(xv * xv, axis=-1, keepdims=True) + RMS_EPS)
    wdy = dyv * w
    dx = r * wdy - xv * (r * r * r) * jnp.mean(xv * wdy, axis=-1, keepdims=True)
    return dx, jnp.sum(dyv * xv * r, axis=0, keepdims=True)


def _norm_bwd(x, w, dy, add, out_dtype, name, comm=None, then=None):
    rows, d = x.shape
    tr = _row_tile(rows)
    has_add = add is not None
    n_in = 3 + has_add + (2 if then is not None else 0)

    def body(*refs):
        x_ref, w_ref, dy_ref = refs[:3]
        outs = refs[n_in:]
        dx, dw = _rms_bwd(x_ref[...], w_ref[...], dy_ref[...].astype(F32))
        if has_add:
            dx = dx + refs[3][...]
        outs[0][...] = dx.astype(outs[0].dtype)
        first = pl.program_id(0) == 0

        @pl.when(first)
        def _():
            outs[1][...] = jnp.zeros_like(outs[1])

        outs[1][...] += dw
        if then is not None:
            dx2, dw2 = _rms_bwd(refs[n_in - 2][...], refs[n_in - 1][...], dx)
            outs[2][...] = dx2.astype(outs[2].dtype)

            @pl.when(first)
            def _():
                outs[3][...] = jnp.zeros_like(outs[3])

            outs[3][...] += dw2

    row = pl.BlockSpec((tr, d), lambda i: (i, 0))
    vec = pl.BlockSpec((1, d), lambda i: (0, 0))
    ins = [x, w, dy] + ([add] if has_add else []) + (list(then) if then is not None else [])
    in_specs = [row, vec, row] + ([row] if has_add else []) + ([row, vec] if then is not None else [])
    out_shape = [S((rows, d), out_dtype), S((1, d), F32)] + ([S((rows, d), BF16), S((1, d), F32)] if then is not None else [])
    return _call(body, name, out_shape, (rows // tr,), in_specs, [row, vec] * (len(out_shape) // 2), ("arbitrary",),
                 ins, comm=comm)


def _final_loss(h, br, w_post, tgt, name):
    rows, d = h.shape
    tr = _row_tile(rows)

    def body(h_ref, br_ref, w_ref, t_ref, dh_ref, loss_ref, dbr_ref, dw_ref):
        i = pl.program_id(0)
        brv, wv = br_ref[...], w_ref[...]
        y = h_ref[...] + _rms(brv, wv)
        r = i * tr + lax.broadcasted_iota(jnp.int32, (tr, 1), 0)
        real = (r >= N_META) & (r < N_META + SEQ)
        diff = jnp.where(real, y - t_ref[...], 0.0)
        dh = diff * (1.0 / d)
        dh_ref[...] = dh
        dbr, dw = _rms_bwd(brv, wv, dh)
        dbr_ref[...] = dbr.astype(dbr_ref.dtype)

        @pl.when(i == 0)
        def _():
            loss_ref[...] = jnp.zeros_like(loss_ref)
            dw_ref[...] = jnp.zeros_like(dw_ref)

        loss_ref[...] += jnp.sum(diff * diff) * (0.5 / d)
        dw_ref[...] += dw

    row = pl.BlockSpec((tr, d), lambda i: (i, 0))
    vec = pl.BlockSpec((1, d), lambda i: (0, 0))
    return pl.pallas_call(body, name=name,
                          out_shape=[S((rows, d), F32), S((1, LANE), F32), S((rows, d), BF16), S((1, d), F32)],
                          grid=(rows // tr,), in_specs=[row, row, vec, row],
                          out_specs=[row, pl.BlockSpec((1, LANE), lambda i: (0, 0)), row, vec],
                          compiler_params=_cp(("arbitrary",)))(h, br, w_post, tgt)


def _gatenorm_fwd(y, zx, w, name, comm=None):
    rows, d = y.shape
    tr = _row_tile(rows)

    def body(y_ref, z_ref, w_ref, o_ref):
        z = z_ref[...]
        o_ref[...] = _rms(y_ref[...] * z * _sigmoid(z), w_ref[...]).astype(o_ref.dtype)

    row = pl.BlockSpec((tr, d), lambda i: (i, 0))
    return _call(body, name, S((rows, d), BF16), (rows // tr,), [row, row, pl.BlockSpec((1, d), lambda i: (0, 0))],
                 row, ("parallel",), (y, zx, w), comm=comm)


def _gatenorm_bwd(y, zx, w, dyn, name, comm=None):
    rows, d = y.shape
    tr = _row_tile(rows)

    def body(y_ref, z_ref, w_ref, dyn_ref, dy_ref, dz_ref, dw_ref):
        yv, z = y_ref[...], z_ref[...]
        sg = _sigmoid(z)
        sz = z * sg
        g = yv * sz
        r = lax.rsqrt(jnp.mean(g * g, axis=-1, keepdims=True) + RMS_EPS)
        dyn_v = dyn_ref[...]
        wdy = dyn_v * w_ref[...]
        dg = r * wdy - g * (r * r * r) * jnp.mean(g * wdy, axis=-1, keepdims=True)
        dy_ref[...] = dg * sz
        dz_ref[...] = (dg * yv * sg * (1.0 + z * (1.0 - sg))).astype(dz_ref.dtype)

        @pl.when(pl.program_id(0) == 0)
        def _():
            dw_ref[...] = jnp.zeros_like(dw_ref)

        dw_ref[...] += jnp.sum(dyn_v * g * r, axis=0, keepdims=True)

    row = pl.BlockSpec((tr, d), lambda i: (i, 0))
    vec = pl.BlockSpec((1, d), lambda i: (0, 0))
    return _call(body, name, [S((rows, d), F32), S((rows, d), BF16), S((1, d), F32)], (rows // tr,),
                 [row, row, vec, row], [row, row, vec], ("arbitrary",), (y, zx, w, dyn), comm=comm)


def _shift_down(x, s, rows_iota):
    if s == 0:
        return x
    return jnp.where(rows_iota >= s, pltpu.roll(x, s, 0), 0.0)


def _shift_up(x, s, rows_iota):
    if s == 0:
        return x
    rows = x.shape[0]
    return jnp.where(rows_iota < rows - s, pltpu.roll(x, rows - s, 0), 0.0)


def _r16(v):
    return v.astype(BF16).astype(F32)


def _conv(x, w_ref, b_ref, taps, rows_iota):
    x = _r16(x)
    acc = jnp.zeros_like(x)
    for k in range(taps):
        acc = acc + _r16(w_ref[k:k + 1, :]) * _shift_down(x, taps - 1 - k, rows_iota)
    return acc + b_ref[...]


def _conv_bwd(x, du, w_ref, dw_ref, db_ref, taps, rows_iota):
    db_ref[...] = jnp.sum(du, axis=0, keepdims=True)
    x, du = _r16(x), _r16(du)
    dx = jnp.zeros_like(x)
    for k in range(taps):
        s = taps - 1 - k
        dx = dx + _r16(w_ref[k:k + 1, :]) * _shift_up(du, s, rows_iota)
        dw_ref[k:k + 1, :] = jnp.sum(du * _shift_down(x, s, rows_iota), axis=0, keepdims=True)
    return dx


def _conv_silu_fwd(zx, w, b, name, comm=None):
    rows = zx.shape[0]
    cb = 512
    off = D_INNER // cb

    def body(x_ref, w_ref, b_ref, o_ref):
        it = lax.broadcasted_iota(jnp.int32, (rows, 1), 0)
        u = _conv(x_ref[...], w_ref, b_ref, SSM_CONV, it)
        o_ref[...] = u * _sigmoid(u)

    return _call(
        body, name, S((rows, D_XBC), F32), (D_XBC // cb,),
        [pl.BlockSpec((rows, cb), lambda j: (0, off + j)), pl.BlockSpec((SSM_CONV, cb), lambda j: (0, j)),
         pl.BlockSpec((1, cb), lambda j: (0, j))],
        pl.BlockSpec((rows, cb), lambda j: (0, j)), ("parallel",), (zx, w, b), comm=comm)


def _conv_silu_bwd(zx, dxs, dbm, dcm, w, b, name):
    rows = zx.shape[0]
    cb = 256
    off = D_INNER // cb
    nx, nbc = D_INNER // cb, D_BC // cb

    def body(x_ref, dx_in, db_in, dc_in, w_ref, b_ref, dx_ref, dw_ref, db_ref, dbuf):
        j = pl.program_id(0)
        for cond, src in ((j < nx, dx_in), ((j >= nx) & (j < nx + nbc), db_in), (j >= nx + nbc, dc_in)):
            @pl.when(cond)
            def _(src=src):
                dbuf[...] = src[...]
        it = lax.broadcasted_iota(jnp.int32, (rows, 1), 0)
        x = x_ref[...]
        u = _conv(x, w_ref, b_ref, SSM_CONV, it)
        sg = _sigmoid(u)
        du = dbuf[...] * sg * (1.0 + u * (1.0 - sg))
        dx_ref[...] = _conv_bwd(x, du, w_ref, dw_ref, db_ref, SSM_CONV, it).astype(dx_ref.dtype)

    def part(first, count):
        return pl.BlockSpec((rows, cb), lambda j: (0, jnp.clip(j - first, 0, count - 1)))

    col = pl.BlockSpec((rows, cb), lambda j: (0, j))
    wsp = pl.BlockSpec((SSM_CONV, cb), lambda j: (0, j))
    bsp = pl.BlockSpec((1, cb), lambda j: (0, j))
    return pl.pallas_call(
        body, name=name, out_shape=[S((rows, D_XBC), BF16), S((SSM_CONV, D_XBC), F32), S((1, D_XBC), F32)],
        grid=(D_XBC // cb,),
        in_specs=[pl.BlockSpec((rows, cb), lambda j: (0, off + j)), part(0, nx), part(nx, nbc), part(nx + nbc, nbc),
                  wsp, bsp],
        out_specs=[col, wsp, bsp], scratch_shapes=[pltpu.VMEM((rows, cb), F32)],
        compiler_params=_cp(("arbitrary",)))(zx, dxs, dbm, dcm, w, b)


def _ffn_act_fwd(u, w, b, name, comm=None):
    rows = u.shape[0]
    cb = 256
    nb = D_FF // cb

    def body(g_ref, v_ref, wg_ref, wv_ref, bg_ref, bv_ref, o_ref):
        it = lax.broadcasted_iota(jnp.int32, (rows, 1), 0)
        g = _conv(g_ref[...], wg_ref, bg_ref, FFN_CONV, it)
        v = _conv(v_ref[...], wv_ref, bv_ref, FFN_CONV, it)
        o_ref[...] = (g * _sigmoid(g) * v).astype(o_ref.dtype)

    def sp(r, shift):
        return pl.BlockSpec((r, cb), lambda j: (0, shift + j))

    return _call(
        body, name, S((rows, D_FF), BF16), (nb,),
        [sp(rows, 0), sp(rows, nb), sp(FFN_CONV, 0), sp(FFN_CONV, nb), sp(1, 0), sp(1, nb)],
        sp(rows, 0), ("parallel",), (u, u, w, w, b, b), comm=comm)


def _ffn_act_bwd(u, dact, w, b, name, comm=None):
    rows = u.shape[0]
    cb = 256
    nb = D_FF // cb

    def body(g_ref, v_ref, d_ref, wg_ref, wv_ref, bg_ref, bv_ref, du_ref, dw_ref, db_ref):
        it = lax.broadcasted_iota(jnp.int32, (rows, 1), 0)
        xg, xv = g_ref[...], v_ref[...]
        g = _conv(xg, wg_ref, bg_ref, FFN_CONV, it)
        v = _conv(xv, wv_ref, bv_ref, FFN_CONV, it)
        sg = _sigmoid(g)
        d = d_ref[...]
        dgate = d * v * sg * (1.0 + g * (1.0 - sg))
        dval = d * g * sg
        du_ref[0] = _conv_bwd(xg, dgate, wg_ref, dw_ref.at[0], db_ref.at[0], FFN_CONV, it).astype(du_ref.dtype)
        du_ref[1] = _conv_bwd(xv, dval, wv_ref, dw_ref.at[1], db_ref.at[1], FFN_CONV, it).astype(du_ref.dtype)

    def sp(r, shift):
        return pl.BlockSpec((r, cb), lambda j: (0, shift + j))

    def both(r):
        return pl.BlockSpec((2, r, cb), lambda j: (0, 0, j))

    return _call(
        body, name, [S((2, rows, D_FF), BF16), S((2, FFN_CONV, D_FF), F32), S((2, 1, D_FF), F32)], (nb,),
        [sp(rows, 0), sp(rows, nb), sp(rows, 0), sp(FFN_CONV, 0), sp(FFN_CONV, nb), sp(1, 0), sp(1, nb)],
        [both(rows), both(FFN_CONV), both(1)], ("parallel",), (u, u, dact, w, w, b, b), comm=comm)


def _ssd_consts(dtp_ref, bias_ref, alog_ref, hg):
    lane = lax.broadcasted_iota(jnp.int32, (1, LANE), 1)
    pre = dtp_ref[...] + bias_ref[...]
    dt = _softplus(pre)
    a_row = jnp.where(lane < hg, -jnp.exp(alog_ref[...]), 0.0)
    ri = lax.broadcasted_iota(jnp.int32, (T, T), 0)
    ci = lax.broadcasted_iota(jnp.int32, (T, T), 1)
    cs = _dot_hi((ri >= ci).astype(F32), dt * a_row)
    return pre, dt, a_row, cs, ri, ci, lane


def _ssd_fwd(xbc, zx, bias, alog, dsk, name, comm=None):
    rows = xbc.shape[0]
    nc = rows // T
    hg = SSM_HEADS // SSM_GROUPS
    gw = hg * HEAD_P
    xoff, boff, coff = 0, D_INNER // D_STATE, (D_INNER + D_BC) // D_STATE
    dtoff = (D_INNER + D_XBC) // LANE

    def body(x_ref, b_ref, c_ref, dtp_ref, bias_ref, alog_ref, dsk_ref, y_ref, hst_ref, hs):
        c = pl.program_id(1)

        @pl.when(c == 0)
        def _():
            hs[...] = jnp.zeros_like(hs)

        _, dt, _, cs, ri, ci, _ = _ssd_consts(dtp_ref, bias_ref, alog_ref, hg)
        cst, dtt = cs.T, dt.T
        xt = x_ref[...].T
        bb, cbf = b_ref[...].astype(BF16), c_ref[...].astype(BF16)
        gt = _dot(bb, cbf, _NT)
        causal_t = ci >= ri
        dskv = dsk_ref[...]
        hall = hs[...]
        hst_ref[0, 0] = hall
        yts, new_h = [], []
        for k in range(hg):
            sl = slice(k * HEAD_P, (k + 1) * HEAD_P)
            csc, csr = cs[:, k:k + 1], cst[k:k + 1, :]
            lt = jnp.exp(jnp.where(causal_t, csr - csc, NEG))
            xk = xt[sl, :]
            xdt = xk * dtt[k:k + 1, :]
            hk = hall[sl, :]
            yd = _dot(xdt.astype(BF16), (gt * lt).astype(BF16))
            yo = jnp.exp(csr) * _dot(hk.astype(BF16), cbf, _NT)
            yts.append(yd + yo + dskv[:, k:k + 1] * xk)
            cl = cs[T - 1:T, k:k + 1]
            st = _dot((xdt * jnp.exp(cl - csr)).astype(BF16), bb)
            new_h.append(jnp.exp(cl) * hk + st)
        y_ref[...] = jnp.concatenate(yts, axis=0).T
        hs[...] = jnp.concatenate(new_h, axis=0)

    vec = pl.BlockSpec((1, LANE), lambda g, c: (0, g))
    return _call(
        body, name, [S((rows, D_INNER), F32), S((nc, SSM_GROUPS, gw, D_STATE), F32)], (SSM_GROUPS, nc),
        [pl.BlockSpec((T, gw), lambda g, c: (c, xoff + g)),
         pl.BlockSpec((T, D_STATE), lambda g, c: (c, boff + g)),
         pl.BlockSpec((T, D_STATE), lambda g, c: (c, coff + g)),
         pl.BlockSpec((T, LANE), lambda g, c: (c, dtoff + g)), vec, vec, vec],
        [pl.BlockSpec((T, gw), lambda g, c: (c, g)), pl.BlockSpec((1, 1, gw, D_STATE), lambda g, c: (c, g, 0, 0))],
        ("parallel", "arbitrary"), (xbc, xbc, xbc, zx, bias, alog, dsk),
        scratch=[pltpu.VMEM((gw, D_STATE), F32)], comm=comm)


def _ssd_bwd(xbc, zx, bias, alog, dsk, dy, hst, name, comm=None):
    rows = xbc.shape[0]
    nc = rows // T
    hg = SSM_HEADS // SSM_GROUPS
    gw = hg * HEAD_P
    boff, coff = D_INNER // D_STATE, (D_INNER + D_BC) // D_STATE
    dtoff = (D_INNER + D_XBC) // LANE

    def body(x_ref, b_ref, c_ref, dtp_ref, bias_ref, alog_ref, dsk_ref, dy_ref, hst_ref,
             dx_ref, db_ref, dc_ref, ddtp_ref, dalog_ref, ddsk_ref, dbias_ref, dhs):
        step = pl.program_id(1)

        @pl.when(step == 0)
        def _():
            dhs[...] = jnp.zeros_like(dhs)
            dalog_ref[...] = jnp.zeros_like(dalog_ref)
            ddsk_ref[...] = jnp.zeros_like(ddsk_ref)
            dbias_ref[...] = jnp.zeros_like(dbias_ref)

        pre, dt, a_row, cs, ri, ci, lane = _ssd_consts(dtp_ref, bias_ref, alog_ref, hg)
        cst, dtt = cs.T, dt.T
        xt, dyt = x_ref[...].T, dy_ref[...].T
        bb, cbf = b_ref[...].astype(BF16), c_ref[...].astype(BF16)
        gt = _dot(bb, cbf, _NT)
        causal_t = ci >= ri
        dskv = dsk_ref[...]
        hall, dhall = hst_ref[0, 0], dhs[...]
        head_row = lax.broadcasted_iota(jnp.int32, (T, 1), 0)
        last_l = lax.broadcasted_iota(jnp.int32, (1, T), 1) == T - 1
        dgt = jnp.zeros((T, T), F32)
        dc_acc = jnp.zeros((T, D_STATE), F32)
        db_acc = jnp.zeros((T, D_STATE), F32)
        ddt_rows = jnp.zeros((T, T), F32)
        dcs_rows = jnp.zeros((T, T), F32)
        qrow_cols = jnp.zeros((T, LANE), F32)
        ddsk_acc = jnp.zeros((1, LANE), F32)
        dxts, new_dh = [], []
        for k in range(hg):
            sl = slice(k * HEAD_P, (k + 1) * HEAD_P)
            csc, csr = cs[:, k:k + 1], cst[k:k + 1, :]
            lt = jnp.exp(jnp.where(causal_t, csr - csc, NEG))
            xk, dyk = xt[sl, :], dyt[sl, :]
            dtr, dk = dtt[k:k + 1, :], dskv[:, k:k + 1]
            xdt = xk * dtr
            mpt = gt * lt
            dyb = dyk.astype(BF16)
            dxdt = _dot(dyb, mpt.astype(BF16), _NT)
            dmt = _dot(xdt.astype(BF16), dyb, _TN)
            dgt = dgt + dmt * lt
            q = dmt * mpt
            q_rows = jnp.sum(q, axis=1, keepdims=True)
            q_cols = jnp.sum(q, axis=0, keepdims=True)
            hk, dhn = hall[sl, :], dhall[sl, :]
            e = jnp.exp(csr)
            cl = cs[T - 1:T, k:k + 1]
            wdec = jnp.exp(cl - csr)
            w = wdec * dtr
            rt = _dot(dhn.astype(BF16), bb, _NT)
            dxts.append(dtr * dxdt + dk * dyk + rt * w)
            xz = jnp.sum(xk * dxdt, axis=0, keepdims=True)
            dw = jnp.sum(rt * xk, axis=0, keepdims=True)
            dcl = jnp.exp(cl) * jnp.sum(dhn * hk) + jnp.sum(dw * w)
            yo = e * _dot(hk.astype(BF16), cbf, _NT)
            dcs_r = jnp.sum(dyk * yo, axis=0, keepdims=True) + q_cols - dw * w + jnp.where(last_l, dcl, 0.0)
            dye = (dyk * e).astype(BF16)
            dc_acc = dc_acc + _dot(dye, hk.astype(BF16), _TN)
            db_acc = db_acc + _dot((xk * w).astype(BF16), dhn.astype(BF16), _TN)
            new_dh.append(jnp.exp(cl) * dhn + _dot(dye, cbf))
            onehot = (lane == k).astype(F32)
            ddt_rows = ddt_rows + jnp.where(head_row == k, xz + dw * wdec, 0.0)
            dcs_rows = dcs_rows + jnp.where(head_row == k, dcs_r, 0.0)
            qrow_cols = qrow_cols + q_rows * onehot
            ddsk_acc = ddsk_acc + jnp.sum(dyk * xk) * onehot
        dx_ref[...] = jnp.concatenate(dxts, axis=0).T
        dhs[...] = jnp.concatenate(new_dh, axis=0)
        dc_ref[...] = _dot(dgt.T.astype(BF16), bb) + dc_acc
        db_ref[...] = _dot(dgt.astype(BF16), cbf) + db_acc
        da = _dot_hi((ci >= ri).astype(F32), dcs_rows.T - qrow_cols)
        ddtp = (ddt_rows.T + da * a_row) * _sigmoid(pre)
        ddtp = jnp.where(lane < hg, ddtp, 0.0)
        ddtp_ref[...] = ddtp
        dbias_ref[...] += jnp.sum(ddtp, axis=0, keepdims=True)
        dalog_ref[...] += jnp.sum(da * dt, axis=0, keepdims=True) * a_row
        ddsk_ref[...] += ddsk_acc

    def rc(c):
        return nc - 1 - c

    vec = pl.BlockSpec((1, LANE), lambda g, c: (0, g))
    xsp = pl.BlockSpec((T, gw), lambda g, c: (rc(c), g))
    return _call(
        body, name,
        [S((rows, D_INNER), F32), S((rows, D_BC), F32), S((rows, D_BC), F32),
         S((rows, SSM_GROUPS * LANE), F32), S((1, SSM_GROUPS * LANE), F32),
         S((1, SSM_GROUPS * LANE), F32), S((1, SSM_GROUPS * LANE), F32)],
        (SSM_GROUPS, nc),
        [xsp,
         pl.BlockSpec((T, D_STATE), lambda g, c: (rc(c), boff + g)),
         pl.BlockSpec((T, D_STATE), lambda g, c: (rc(c), coff + g)),
         pl.BlockSpec((T, LANE), lambda g, c: (rc(c), dtoff + g)), vec, vec, vec,
         xsp, pl.BlockSpec((1, 1, gw, D_STATE), lambda g, c: (rc(c), g, 0, 0))],
        [xsp,
         pl.BlockSpec((T, D_STATE), lambda g, c: (rc(c), g)),
         pl.BlockSpec((T, D_STATE), lambda g, c: (rc(c), g)),
         pl.BlockSpec((T, LANE), lambda g, c: (rc(c), g)), vec, vec, vec],
        ("parallel", "arbitrary"), (xbc, xbc, xbc, zx, bias, alog, dsk, dy, hst),
        scratch=[pltpu.VMEM((gw, D_STATE), F32)], comm=comm)


def _attn_tiles(kv_ref, j):
    prev = jnp.maximum(j - 1, 0)
    meta = kv_ref[0:T, :]
    prv = kv_ref[pl.ds(pl.multiple_of(prev * T, T), T), :]
    cur = kv_ref[pl.ds(pl.multiple_of(j * T, T), T), :]
    return jnp.concatenate([meta, prv, cur], axis=0)


def _attn_mask(j):
    r = j * T + lax.broadcasted_iota(jnp.int32, (3 * T, T), 1)
    row = lax.broadcasted_iota(jnp.int32, (3 * T, T), 0)
    t0, t1 = row < T, row < 2 * T
    s = jnp.where(t0, row, (j - 2) * T + row)
    ok = (s <= r) & ((s < N_META) | (s > r - WINDOW))
    use = (t0 & (j >= 2) & (row < N_META)) | (jnp.logical_not(t0) & t1 & (j >= 1)) | jnp.logical_not(t1)
    return ok & use


def _attn_fwd(q, kv, sinks, name, comm=None):
    rows = q.shape[0]
    scale = 1.0 / math.sqrt(ATTN_DH)
    qpk = N_Q_HEADS // N_KV_HEADS

    def body(q_ref, kv_ref, s_ref, o_ref, lse_ref):
        j = pl.program_id(0)
        kv3 = _attn_tiles(kv_ref, j).astype(BF16)
        mask = _attn_mask(j)
        qv = (q_ref[...] * scale).astype(BF16)
        sk = s_ref[...]
        lses = []
        for kh in range(N_KV_HEADS):
            k3 = kv3[:, kh * ATTN_DH:(kh + 1) * ATTN_DH]
            v3 = kv3[:, D_KV + kh * ATTN_DH:D_KV + (kh + 1) * ATTN_DH]
            for g in range(qpk):
                h = kh * qpk + g
                sink = sk[:, h:h + 1]
                sc = jnp.where(mask, _dot(k3, qv[:, h * ATTN_DH:(h + 1) * ATTN_DH], _NT), NEG)
                m = jnp.maximum(jnp.max(sc, axis=0, keepdims=True), sink)
                p = jnp.exp(sc - m)
                den = jnp.sum(p, axis=0, keepdims=True) + jnp.exp(sink - m)
                p = p * (1.0 / den)
                lses.append(m + jnp.log(den))
                o_ref[:, h * ATTN_DH:(h + 1) * ATTN_DH] = _dot(p.astype(BF16), v3, _TN).astype(o_ref.dtype)
        lse_ref[...] = jnp.concatenate(lses, axis=0)

    return _call(
        body, name, [S((rows, D_MODEL), BF16), S((N_Q_HEADS, rows), F32)], (rows // T,),
        [pl.BlockSpec((T, D_MODEL), lambda j: (j, 0)), pl.BlockSpec((rows, 2 * D_KV), lambda j: (0, 0)),
         pl.BlockSpec((1, N_Q_HEADS), lambda j: (0, 0))],
        [pl.BlockSpec((T, D_MODEL), lambda j: (j, 0)), pl.BlockSpec((N_Q_HEADS, T), lambda j: (0, j))],
        ("parallel",), (q, kv, sinks), comm=comm)


def _attn_bwd(q, kv, sinks, do, lse, name, comm=None):
    rows = q.shape[0]
    scale = 1.0 / math.sqrt(ATTN_DH)
    qpk = N_Q_HEADS // N_KV_HEADS

    def body(q_ref, kv_ref, s_ref, do_ref, lse_ref, dq_ref, dkv_ref, ds_ref):
        j = pl.program_id(0)

        @pl.when(j == 0)
        def _():
            dkv_ref[...] = jnp.zeros_like(dkv_ref)
            ds_ref[...] = jnp.zeros_like(ds_ref)

        kv3 = _attn_tiles(kv_ref, j).astype(BF16)
        mask = _attn_mask(j)
        qv = (q_ref[...] * scale).astype(BF16)
        dov = do_ref[...].astype(BF16)
        sk = s_ref[...]
        lsev = lse_ref[...]
        lane = lax.broadcasted_iota(jnp.int32, (1, LANE), 1)
        ds_acc = jnp.zeros((1, LANE), F32)
        prev = jnp.maximum(j - 1, 0)
        dqts = []
        for kh in range(N_KV_HEADS):
            ksl = slice(kh * ATTN_DH, (kh + 1) * ATTN_DH)
            vsl = slice(D_KV + kh * ATTN_DH, D_KV + (kh + 1) * ATTN_DH)
            k3, v3 = kv3[:, ksl], kv3[:, vsl]
            k3t = k3.T
            dk3 = jnp.zeros((3 * T, ATTN_DH), F32)
            dv3 = jnp.zeros((3 * T, ATTN_DH), F32)
            for g in range(qpk):
                h = kh * qpk + g
                hs = slice(h * ATTN_DH, (h + 1) * ATTN_DH)
                qh, doh = qv[:, hs], dov[:, hs]
                lh = lsev[h:h + 1, :]
                p = jnp.exp(jnp.where(mask, _dot(k3, qh, _NT), NEG) - lh)
                ps = jnp.exp(sk[:, h:h + 1] - lh)
                dp = _dot(v3, doh, _NT)
                delta = jnp.sum(p * dp, axis=0, keepdims=True)
                dsc = (p * (dp - delta)).astype(BF16)
                dqts.append(_dot(k3t, dsc) * scale)
                dk3 = dk3 + _dot(dsc, qh)
                dv3 = dv3 + _dot(p.astype(BF16), doh)
                ds_acc = ds_acc - jnp.sum(ps * delta) * (lane == h).astype(F32)
            for t, start in enumerate((0, pl.multiple_of(prev * T, T), pl.multiple_of(j * T, T))):
                rsl = pl.ds(start, T)
                dkv_ref[rsl, ksl] += dk3[t * T:(t + 1) * T, :]
                dkv_ref[rsl, vsl] += dv3[t * T:(t + 1) * T, :]
        ds_ref[...] += ds_acc
        dq_ref[...] = jnp.concatenate(dqts, axis=0).T.astype(dq_ref.dtype)

    blk = pl.BlockSpec((T, D_MODEL), lambda j: (j, 0))
    full = pl.BlockSpec((rows, 2 * D_KV), lambda j: (0, 0))
    return _call(
        body, name, [S((rows, D_MODEL), BF16), S((rows, 2 * D_KV), F32), S((1, LANE), F32)], (rows // T,),
        [blk, full, pl.BlockSpec((1, N_Q_HEADS), lambda j: (0, 0)), blk, pl.BlockSpec((N_Q_HEADS, T), lambda j: (0, j))],
        [blk, full, pl.BlockSpec((1, LANE), lambda j: (0, 0))], ("arbitrary",), (q, kv, sinks, do, lse), comm=comm)


BLOCK_BYTES = 1 << 20


def _div_tile(rows, cols):
    cap = max(16, BLOCK_BYTES // (4 * cols))
    best = None
    for t in range(16, min(rows, cap) + 1, 16):
        if rows % t == 0:
            best = t
    return best if best is not None else rows


def _adamw(parts, w, m, v, name, comm=None):
    layers, rows, cols = w.shape
    n = parts[0].shape[0]
    tr = _div_tile(rows, cols)
    tc = _pick(cols, 256) if tr == rows and rows * cols * 4 > 2 * BLOCK_BYTES else cols
    c1 = 1.0 / (1.0 - B1 ** STEP)
    c2 = 1.0 / (1.0 - B2 ** STEP)

    def body(*refs):
        p_refs = refs[:layers]
        w_ref, m_ref, v_ref, g_ref, d_ref, nm_ref, nv_ref = refs[layers:]
        layer = pl.program_id(0)
        for l in range(layers):
            @pl.when(layer == l)
            def _(p_ref=p_refs[l]):
                g = p_ref[0].astype(F32)
                for i in range(1, n):
                    g = g + p_ref[i].astype(F32)
                nm = B1 * m_ref[...] + (1.0 - B1) * g
                nv = B2 * v_ref[...] + (1.0 - B2) * (g * g)
                g_ref[...] = g
                nm_ref[...] = nm
                nv_ref[...] = nv
                d_ref[...] = -LR * ((nm * c1) / (jnp.sqrt(nv * c2) + EPS) + WD * w_ref[...])

    def part_spec(l):
        return pl.BlockSpec((n, tr, tc), lambda k, i, j: (0, jnp.where(k == l, i, 0), jnp.where(k == l, j, 0)))

    row = pl.BlockSpec((None, tr, tc), lambda k, i, j: (k, i, j))
    return _call(body, name, [S((layers, rows, cols), F32)] * 4, (layers, rows // tr, cols // tc),
                 [part_spec(l) for l in range(layers)] + [row, row, row], [row] * 4,
                 ("parallel", "parallel", "parallel"), (*parts, w, m, v), comm=comm)


def _sum_parts(parts, name):
    n, rows, cols = parts[0].shape
    nb = len(parts)
    tr = _div_tile(rows, cols)

    def body(*refs):
        o_ref = refs[nb]
        blk = pl.program_id(0)
        for l in range(nb):
            @pl.when(blk == l)
            def _(p_ref=refs[l]):
                g = p_ref[0].astype(F32)
                for i in range(1, n):
                    g = g + p_ref[i].astype(F32)
                o_ref[...] = g

    def part_spec(l):
        return pl.BlockSpec((n, tr, cols), lambda k, i: (0, jnp.where(k == l, i, 0), 0))

    per = rows // tr
    return pl.pallas_call(body, name=name, out_shape=S((nb * rows, cols), F32), grid=(nb, per),
                          in_specs=[part_spec(l) for l in range(nb)],
                          out_specs=pl.BlockSpec((tr, cols), lambda k, i: (k * per + i, 0)),
                          compiler_params=_cp(("parallel", "parallel")))(*parts)


def _col_segments(ws, runs):
    segs = []
    for glo, mlo, n in runs:
        while n > 0:
            d, off = divmod(glo, ws)
            take = min(n, ws - off)
            segs.append((d, off, mlo, take))
            glo, mlo, n = glo + take, mlo + take, n - take
    return segs


def _assemble_cols(gs, width, segs, name):
    _, rows, ws = gs[0].shape
    nb = len(gs)
    rb = _div_tile(rows, width // 2)
    per = rows // rb

    def body(*refs):
        o_ref = refs[nb]
        piece = pl.program_id(0)
        for l in range(nb):
            @pl.when(piece == l)
            def _(g_ref=refs[l]):
                o_ref[...] = jnp.zeros_like(o_ref)
                for d, off, mlo, n in segs:
                    o_ref[:, mlo:mlo + n] = g_ref[d, :, off:off + n]

    def piece_spec(l):
        return pl.BlockSpec((N_DEV, rb, ws), lambda k, i: (0, jnp.where(k == l, i, 0), 0))

    return pl.pallas_call(
        body, name=name, out_shape=S((nb * rows, width), gs[0].dtype), grid=(nb, per),
        in_specs=[piece_spec(l) for l in range(nb)],
        out_specs=pl.BlockSpec((rb, width), lambda k, i: (k * per + i, 0)),
        compiler_params=_cp(("parallel", "parallel")))(*gs)


def _scatter_cols(dw, ws, segs, name):
    rows, width = dw.shape
    rb = _div_tile(rows, width)

    def body(w_ref, o_ref):
        for d, off, mlo, n in segs:
            o_ref[d, :, off:off + n] = w_ref[:, mlo:mlo + n].astype(o_ref.dtype)

    return pl.pallas_call(
        body, name=name, out_shape=S((N_DEV, rows, ws), BF16), grid=(rows // rb,),
        in_specs=[pl.BlockSpec((rb, width), lambda i: (i, 0))],
        out_specs=pl.BlockSpec((N_DEV, rb, ws), lambda i: (0, i, 0)), compiler_params=_cp(("parallel",)))(dw)


def _gather_comm(xs):
    n = len(xs)

    def setup(x_refs, out_refs, sems):
        send_sems, recv_sems, local_sems = sems
        mx, my, mc = lax.axis_index("x"), lax.axis_index("y"), lax.axis_index("c")
        me, sibling = (mx, my, mc), (mx, my, 1 - mc)
        chips = [(1 - mx, my), (mx, 1 - my), (1 - mx, 1 - my)]

        def blk(a, px, py, pc):
            return out_refs[a].at[4 * px + 2 * py + pc]

        def copy(a, k, block, to, src=None):
            return pltpu.make_async_remote_copy(
                src_ref=blk(a, *block) if src is None else src, dst_ref=blk(a, *block),
                send_sem=send_sems.at[a, k], recv_sem=recv_sems.at[a, k], device_id=to, device_id_type=_MESH)

        mine = [pltpu.make_async_copy(x_refs[a], blk(a, *me), local_sems.at[a]) for a in range(n)]
        own = []
        for a in range(n):
            own.append(copy(a, 0, me, sibling, src=x_refs[a]))
            own += [copy(a, 1 + i, me, (*chip, mc), src=x_refs[a]) for i, chip in enumerate(chips)]
        return me, sibling, chips, mc, copy, mine, own

    def first(x_refs, out_refs, sems):
        _, _, _, _, _, mine, own = setup(x_refs, out_refs, sems)
        for cp in mine + own:
            cp.start()

    def last(x_refs, out_refs, sems):
        me, sibling, chips, mc, copy, mine, own = setup(x_refs, out_refs, sems)
        passed = []
        for a in range(n):
            for i, chip in enumerate(chips):
                copy(a, 1 + i, (*chip, mc), me).wait_recv()
                passed.append(copy(a, 4 + i, (*chip, mc), sibling))
                passed[-1].start()
        for a in range(n):
            copy(a, 0, sibling, me).wait_recv()
            for i, chip in enumerate(chips):
                copy(a, 4 + i, (*chip, 1 - mc), me).wait_recv()
        for cp in own + passed:
            cp.wait_send()
        for cp in mine:
            cp.wait()

    return _Comm(list(xs), [S((N_DEV,) + x.shape, x.dtype) for x in xs],
                 [pltpu.SemaphoreType.DMA((n, 7)), pltpu.SemaphoreType.DMA((n, 7)), pltpu.SemaphoreType.DMA((n,))],
                 first, last)


def _swap_comm(gs):
    n = len(gs)

    def copies(g_refs, out_refs, sems):
        send_sems, recv_sems = sems
        mx, my, mc = lax.axis_index("x"), lax.axis_index("y"), lax.axis_index("c")
        return [pltpu.make_async_remote_copy(
            src_ref=g_refs[a].at[2 * k + 1 - mc], dst_ref=out_refs[a].at[k], send_sem=send_sems.at[a, k],
            recv_sem=recv_sems.at[a, k], device_id=(mx, my, 1 - mc), device_id_type=_MESH)
            for a in range(n) for k in range(4)]

    def first(g_refs, out_refs, sems):
        for cp in copies(g_refs, out_refs, sems):
            cp.start()

    def last(g_refs, out_refs, sems):
        for cp in copies(g_refs, out_refs, sems):
            cp.wait()

    return _Comm(list(gs), [S((4,) + g.shape[1:], g.dtype) for g in gs],
                 [pltpu.SemaphoreType.DMA((n, 4)), pltpu.SemaphoreType.DMA((n, 4))], first, last)


def _chips_comm(parts):
    n = len(parts)

    def copies(p_refs, out_refs, sems):
        send_sems, recv_sems, local_sems = sems
        mx, my, mc = lax.axis_index("x"), lax.axis_index("y"), lax.axis_index("c")
        mychip = 2 * mx + my
        chips = [(1 - mx, my), (mx, 1 - my), (1 - mx, 1 - my)]
        mine = [pltpu.make_async_copy(p_refs[a].at[mychip], out_refs[a].at[mychip], local_sems.at[a])
                for a in range(n)]
        return mine + [pltpu.make_async_remote_copy(
            src_ref=p_refs[a].at[2 * cx + cy], dst_ref=out_refs[a].at[mychip], send_sem=send_sems.at[a, i],
            recv_sem=recv_sems.at[a, i], device_id=(cx, cy, mc), device_id_type=_MESH)
            for a in range(n) for i, (cx, cy) in enumerate(chips)]

    def first(p_refs, out_refs, sems):
        for cp in copies(p_refs, out_refs, sems):
            cp.start()

    def last(p_refs, out_refs, sems):
        for cp in copies(p_refs, out_refs, sems):
            cp.wait()

    return _Comm(list(parts), [S(p.shape, p.dtype) for p in parts],
                 [pltpu.SemaphoreType.DMA((n, 3)), pltpu.SemaphoreType.DMA((n, 3)), pltpu.SemaphoreType.DMA((n,))],
                 first, last)


def _join_comms(comms):
    def split(refs, counts):
        out, p = [], 0
        for cnt in counts:
            out.append(refs[p:p + cnt])
            p += cnt
        return out

    ni = [len(c.ins) for c in comms]
    no = [len(c.out_shapes) for c in comms]
    ns = [len(c.scratch) for c in comms]

    def first(in_refs, out_refs, sems):
        for c, i, o, s in zip(comms, split(in_refs, ni), split(out_refs, no), split(sems, ns)):
            c.first(i, o, s)

    def last(in_refs, out_refs, sems):
        for c, i, o, s in zip(comms, split(in_refs, ni), split(out_refs, no), split(sems, ns)):
            c.last(i, o, s)

    return _Comm([x for c in comms for x in c.ins], [x for c in comms for x in c.out_shapes],
                 [x for c in comms for x in c.scratch], first, last)


def _add_pairs(mine, theirs, core, name):
    _, rows, cols = mine.shape
    tr = _div_tile(rows, cols)

    def body(core_ref, a_ref, b_ref, o_ref):
        o_ref[...] = (a_ref[...].astype(F32) + b_ref[...].astype(F32)).astype(o_ref.dtype)

    return pl.pallas_call(
        body, name=name, out_shape=S((4, rows, cols), BF16),
        grid_spec=pltpu.PrefetchScalarGridSpec(
            num_scalar_prefetch=1, grid=(4, rows // tr),
            in_specs=[pl.BlockSpec((None, tr, cols), lambda k, i, c: (2 * k + c[0], i, 0)),
                      pl.BlockSpec((None, tr, cols), lambda k, i, c: (k, i, 0))],
            out_specs=pl.BlockSpec((None, tr, cols), lambda k, i, c: (k, i, 0))),
        compiler_params=_cp(("parallel", "parallel")))(core, mine, theirs)


def _run_comm(comm, name):
    ci, co = len(comm.ins), len(comm.out_shapes)

    def body(*refs):
        comm.first(refs[:ci], refs[ci:ci + co], refs[ci + co:])
        comm.last(refs[:ci], refs[ci:ci + co], refs[ci + co:])

    return pl.pallas_call(body, name=name, out_shape=list(comm.out_shapes), in_specs=[_HBM] * ci,
                          out_specs=[_HBM] * co, scratch_shapes=list(comm.scratch))(*comm.ins)


def _flat_rows(n_elems, mult):
    rows = -(-n_elems // LANE)
    return -(-rows // mult) * mult


def _pack(arrs, lead, mult, dtype):
    lead_shape = arrs[0].shape[:lead]
    flat = jnp.concatenate([a.astype(dtype).reshape(lead_shape + (-1,)) for a in arrs], axis=-1)
    n = flat.shape[-1]
    rows = _flat_rows(n, mult)
    flat = jnp.pad(flat, [(0, 0)] * lead + [(0, rows * LANE - n)])
    return flat.reshape(lead_shape + (rows, LANE))


def _unpack(flat, lead, shapes):
    lead_shape = flat.shape[:lead]
    flat = flat.reshape(lead_shape + (-1,))
    out, off = [], 0
    for shp in shapes:
        n = math.prod(shp)
        out.append(flat[..., off:off + n].reshape(lead_shape + tuple(shp)))
        off += n
    return out


def _split8(full, ax, n):
    shp = full.shape
    return jnp.moveaxis(full.reshape(shp[:ax] + (N_DEV, n) + shp[ax + 1:]), ax, 0)


def _join8(g, ax):
    shp = g.shape[1:]
    return jnp.moveaxis(g, 0, ax).reshape(shp[:ax] + (N_DEV * shp[ax],) + shp[ax + 1:])


def _group_lanes(v, hg):
    v = v.reshape(SSM_GROUPS, hg)
    return jnp.pad(v, ((0, 0), (0, LANE - hg))).reshape(1, SSM_GROUPS * LANE)


def _ungroup_lanes(v, hg):
    return v.reshape(SSM_GROUPS, LANE)[:, :hg].reshape(1, SSM_GROUPS * hg)


def kernel(x, meta_tokens, a_norm_pre, a_w_in, a_conv_w, a_conv_b, a_dt_bias, a_a_log, a_d_skip, a_gate_norm, a_w_out, a_norm_post, kv_norm, w_kv, b_norm_pre, b_w_q, b_sinks, b_w_o, b_norm_post, f_norm_pre, f_w_up, f_conv_w, f_conv_b, f_w_down, f_norm_post, loss_target, m_meta_tokens, m_a_norm_pre, m_a_w_in, m_a_conv_w, m_a_conv_b, m_a_dt_bias, m_a_a_log, m_a_d_skip, m_a_gate_norm, m_a_w_out, m_a_norm_post, m_kv_norm, m_w_kv, m_b_norm_pre, m_b_w_q, m_b_sinks, m_b_w_o, m_b_norm_post, m_f_norm_pre, m_f_w_up, m_f_conv_w, m_f_conv_b, m_f_w_down, m_f_norm_post, v_meta_tokens, v_a_norm_pre, v_a_w_in, v_a_conv_w, v_a_conv_b, v_a_dt_bias, v_a_a_log, v_a_d_skip, v_a_gate_norm, v_a_w_out, v_a_norm_post, v_kv_norm, v_w_kv, v_b_norm_pre, v_b_w_q, v_b_sinks, v_b_w_o, v_b_norm_post, v_f_norm_pre, v_f_w_up, v_f_conv_w, v_f_conv_b, v_f_w_down, v_f_norm_post):
    args = locals()
    wts = {n: args[n] for n in WEIGHTS}
    mom = {n: args["m_" + n] for n in WEIGHTS}
    var = {n: args["v_" + n] for n in WEIGHTS}
    mx, my, mc = lax.axis_index("x"), lax.axis_index("y"), lax.axis_index("c")
    me = 4 * mx + 2 * my + mc
    rows = _seq_rows()
    hg = SSM_HEADS // SSM_GROUPS
    d = D_MODEL

    n_main = D_INNER + D_XBC
    ws_in, ws_up = a_w_in.shape[2], f_w_up.shape[2]
    segs_in = _col_segments(ws_in, [(0, 0, n_main)] + [(n_main + hg * g, n_main + LANE * g, hg)
                                                      for g in range(SSM_GROUPS)])
    segs_up = _col_segments(ws_up, [(0, 0, 2 * D_FF)])
    def gather_of(*ws):
        return _gather_comm([w.astype(BF16) for w in ws])

    small_full, = _run_comm(_gather_comm([_pack([wts[n] for n in SMALL], 0, 8, F32)]), "gather_small")
    full = {}
    for n, g in zip(SMALL, _unpack(small_full, 1, [wts[n].shape for n in SMALL])):
        full[n] = _join8(g, SHARD_AXIS[n])
    (h0, hn0), (g_in,) = _embed_norm(full["meta_tokens"], x[0], full["a_norm_pre"], rows, "embed_norm",
                                     comm=gather_of(a_w_in[0]))
    w_in_all = _assemble_cols([g_in], n_main + SSM_GROUPS * LANE, segs_in, "asm_w_in")
    w_up, w_down = [None, None], [None, None]
    bias_g = _group_lanes(wts["a_dt_bias"], hg)
    alog_g = _group_lanes(wts["a_a_log"], hg)
    dsk_g = _group_lanes(wts["a_d_skip"], hg)
    a_conv_w, a_conv_b = full["a_conv_w"][0], full["a_conv_b"]
    f_cw, f_cb = full["f_conv_w"], wts["f_conv_b"]
    fpre, fpost = wts["f_norm_pre"], wts["f_norm_post"]

    tgt = jnp.pad(loss_target[0], ((N_META, rows - N_META - SEQ), (0, 0)))

    zx, (g_out,) = _mm(hn0, w_in_all, "nn", F32, "mm_in", comm=gather_of(a_w_out[0]))
    w_out = g_out.reshape(D_INNER, d)
    xbc, (g_dn1,) = _conv_silu_fwd(zx, a_conv_w, a_conv_b, "conv_a", comm=gather_of(f_w_down[1]))
    (y_ssd, hst), (g_up0,) = _ssd_fwd(xbc, zx, bias_g, alog_g, dsk_g, "ssd_fwd", comm=gather_of(f_w_up[0]))
    w_up[0] = _assemble_cols([g_up0], 2 * D_FF, segs_up, "asm_w_up0")
    yn, (g_kv, g_q) = _gatenorm_fwd(y_ssd, zx, full["a_gate_norm"], "gatenorm", comm=gather_of(w_kv, b_w_q[0]))
    mix_a, (g_o,) = _mm(yn, w_out, "nn", F32, "mm_out", comm=gather_of(b_w_o[0]))
    w_kvf, w_q, w_o = g_kv.reshape(d, 2 * D_KV), g_q.reshape(d, d), g_o.reshape(d, d)
    h1, (fn0,) = _resid_norm(h0, mix_a, full["a_norm_post"], [fpre[0:1]], "resid_a")

    half = d // 2
    u0, (g_dn0,) = _mm(fn0, w_up[0], "nn", F32, "mm_up0", comm=gather_of(f_w_down[0]))
    w_down = [g_dn0.reshape(D_FF, d), g_dn1.reshape(D_FF, d)]
    act0, (g_up1a,) = _ffn_act_fwd(u0, f_cw[0], f_cb[0:1], "ffn_act0", comm=gather_of(f_w_up[1, :half]))
    ffn0 = _mm(act0, w_down[0], "nn", F32, "mm_down0")
    h2, (kvn, bn) = _resid_norm(h1, ffn0, fpost[0:1], [wts["kv_norm"].reshape(1, d), wts["b_norm_pre"]], "resid_f0")
    kv = _mm(kvn, w_kvf, "nn", F32, "mm_kv")
    q = _mm(bn, w_q, "nn", F32, "mm_q")
    (o, lse), (g_up1b,) = _attn_fwd(q, kv, wts["b_sinks"], "attn_fwd", comm=gather_of(f_w_up[1, half:]))
    w_up[1] = _assemble_cols([g_up1a, g_up1b], 2 * D_FF, segs_up, "asm_w_up1")
    mix_b = _mm(o, w_o, "nn", F32, "mm_o")
    h3, (fn1,) = _resid_norm(h2, mix_b, wts["b_norm_post"], [fpre[1:2]], "resid_b")
    u1 = _mm(fn1, w_up[1], "nn", F32, "mm_up1")
    act1 = _ffn_act_fwd(u1, f_cw[1], f_cb[1:2], "ffn_act1")
    ffn1 = _mm(act1, w_down[1], "nn", F32, "mm_down1")
    dh4, loss_row, dffn1, dw_post1 = _final_loss(h3, ffn1, fpost[1:2], tgt, "loss")
    loss = lax.psum(loss_row[0, 0], ("x", "y", "c"))

    grads = {}

    core = mc.astype(jnp.int32).reshape(1)

    def carried(res, comm):
        return res if comm is not None else (res, None)

    def ffn_bwd(dh_out, dffn, h_in, fn, u, act, i, then, c_dact=None, c_dwdown=None, c_dwup=None):
        dact, got_a = carried(_mm(dffn, w_down[i], "nt", F32, f"mm_dact{i}", comm=c_dact), c_dact)
        dw_down, got_b = carried(_mm(act, dffn, "tn", BF16, f"mm_dwdown{i}", comm=c_dwdown), c_dwdown)
        dw_down = dw_down.reshape(N_DEV, -1, d)
        du, dwc, dbc = _ffn_act_bwd(u, dact, f_cw[i], f_cb[i:i + 1], f"ffn_act_bwd{i}")
        dfn, (s_dn,) = _mm(du, w_up[i], "nt", F32, f"mm_dfn{i}", comm=_swap_comm([dw_down]))
        sum_dn = _add_pairs(dw_down, s_dn, core, f"rs_add_dn{i}")
        dw_up, got_c = carried(_mm(fn, du, "tn", BF16, f"mm_dwup{i}", comm=c_dwup, shard_cols=ws_up), c_dwup)
        (dh_in, dw_pre, dbranch, dw_branch), (s_up,) = _norm_bwd(
            h_in, fpre[i:i + 1], dfn, dh_out, F32, f"nb_fpre{i}", comm=_swap_comm([dw_up]), then=then)
        sum_up = _add_pairs(dw_up, s_up, core, f"rs_add_up{i}")
        return dh_in, dbranch, dw_branch, dict(sum_down=sum_dn, cw=jnp.concatenate([dwc[0], dwc[1]], axis=1),
                                               cb=jnp.concatenate([dbc[0], dbc[1]], axis=1), sum_up=sum_up,
                                               pre=dw_pre), got_a, got_b, got_c

    dh3, dmix_b, grads["b_norm_post"], gf1, _, _, _ = ffn_bwd(dh4, dffn1, h3, fn1, u1, act1, 1,
                                                              (mix_b, wts["b_norm_post"]))
    do = _mm(dmix_b, w_o, "nt", F32, "mm_do")
    dw_o = _mm(o, dmix_b, "tn", BF16, "mm_dwo").reshape(N_DEV, -1, d)
    (dq, dkv, dsinks), (p_up1, s_o) = _attn_bwd(
        q, kv, wts["b_sinks"], do, lse, "attn_bwd",
        comm=_join_comms([_chips_comm([gf1["sum_up"]]), _swap_comm([dw_o])]))
    sum_o = _add_pairs(dw_o, s_o, core, "rs_add_o")
    grads["b_sinks"] = dsinks[:, :N_Q_HEADS]
    dbn = _mm(dq, w_q, "nt", F32, "mm_dbn")
    dw_q = _mm(bn, dq, "tn", BF16, "mm_dwq").reshape(N_DEV, -1, d)
    dkv16 = dkv.astype(BF16)
    dkvn = _mm(dkv16, w_kvf, "nt", F32, "mm_dkvn")
    dw_kv = _mm(kvn, dkv16, "tn", BF16, "mm_dwkv").reshape(N_DEV, -1, 2 * D_KV)
    (dh2, grads["b_norm_pre"]), (s_q, s_kv) = _norm_bwd(h2, wts["b_norm_pre"], dbn, dh3, F32, "nb_bpre",
                                                        comm=_swap_comm([dw_q, dw_kv]))
    sum_q, sum_kv = _add_pairs(dw_q, s_q, core, "rs_add_q"), _add_pairs(dw_kv, s_kv, core, "rs_add_kv")
    dh2, dw_kvn, dffn0, dw_post0 = _norm_bwd(h2, wts["kv_norm"].reshape(1, d), dkvn, dh2, F32, "nb_kv",
                                             then=(ffn0, fpost[0:1]))
    grads["kv_norm"] = dw_kvn.reshape(d)
    dh1, dmix_a, grads["a_norm_post"], gf0, (p_o,), (p_q, p_kv), (p_dn1,) = ffn_bwd(
        dh2, dffn0, h1, fn0, u0, act0, 0, (mix_a, full["a_norm_post"]), c_dact=_chips_comm([sum_o]),
        c_dwdown=_chips_comm([sum_q, sum_kv]), c_dwup=_chips_comm([gf1["sum_down"]]))
    grads["f_norm_post"] = jnp.concatenate([dw_post0, dw_post1], axis=0)
    grads["f_norm_pre"] = jnp.concatenate([gf0["pre"], gf1["pre"]], axis=0)
    grads["f_conv_w"] = jnp.stack([gf0["cw"], gf1["cw"]])
    grads["f_conv_b"] = jnp.concatenate([gf0["cb"], gf1["cb"]], axis=0)

    dyn = _mm(dmix_a, w_out, "nt", F32, "mm_dyn")
    dw_out = _mm(yn, dmix_a, "tn", BF16, "mm_dwout").reshape(N_DEV, -1, d)
    (dy_ssd, dz, grads["a_gate_norm"]), (s_out,) = _gatenorm_bwd(y_ssd, zx, full["a_gate_norm"], dyn, "gatenorm_bwd",
                                                                 comm=_swap_comm([dw_out]))
    sum_out = _add_pairs(dw_out, s_out, core, "rs_add_out")
    (dxs, dbm, dcm, ddtp, dalog, ddsk, dbias), (p_up0, p_dn0, p_out) = _ssd_bwd(
        xbc, zx, bias_g, alog_g, dsk_g, dy_ssd, hst, "ssd_bwd",
        comm=_chips_comm([gf0["sum_up"], gf0["sum_down"], sum_out]))
    grads["a_a_log"] = _ungroup_lanes(dalog, hg)
    grads["a_d_skip"] = _ungroup_lanes(ddsk, hg)
    grads["a_dt_bias"] = _ungroup_lanes(dbias, hg)
    dpre, dcw, dcb = _conv_silu_bwd(zx, dxs, dbm, dcm, a_conv_w, a_conv_b, "conv_a_bwd")
    grads["a_conv_w"], grads["a_conv_b"] = dcw[None], dcb
    dzx = jnp.concatenate([dz, dpre, ddtp.astype(BF16)], axis=1)
    dw_in8 = _scatter_cols(_mm(hn0, dzx, "tn", BF16, "mm_dwin"), ws_in, segs_in, "scat_w_in")
    dhn0, (s_in,) = _mm(dzx, w_in_all, "nt", F32, "mm_dhn0", comm=_swap_comm([dw_in8]))
    sum_in = _add_pairs(dw_in8, s_in, core, "rs_add_in")
    half_in = sum_in.shape[1] // 2
    (dh0, grads["a_norm_pre"]), (p_in_a,) = _norm_bwd(h0, full["a_norm_pre"], dhn0, dh1, F32, "nb_apre",
                                                      comm=_chips_comm([sum_in[:, :half_in]]))
    grad_x = dh0[N_META:N_META + SEQ][None]
    grads["meta_tokens"] = dh0[:N_META]

    small_local = _pack([_split8(grads[n], SHARD_AXIS[n], wts[n].shape[SHARD_AXIS[n]]) for n in SMALL], 1, 8, F32)
    repl_local = _pack([grads[n] for n in REPL], 0, 8, F32)
    n_sr = small_local.shape[1]
    small_vec = jnp.concatenate([small_local.reshape(N_DEV * n_sr, LANE), repl_local], axis=0)
    tail = _join_comms([_chips_comm([sum_in[:, half_in:]]), _gather_comm([small_vec])])
    parts_big = dict(a_w_out=[p_out], w_kv=[p_kv], b_w_q=[p_q], b_w_o=[p_o], f_w_down=[p_dn0, p_dn1])

    def flat_f32(dct, names, mult):
        return _pack([dct[n] for n in names], 0, mult, F32)

    def adamw_big(n, comm=None):
        shp3 = (len(parts_big[n]),) + parts_big[n][0].shape[1:]
        res = _adamw(parts_big[n], *[dct[n].reshape(shp3) for dct in (wts, mom, var)], f"adamw_{n}", comm=comm)
        res, got = res if comm is not None else (res, None)
        big_out[n] = [r.reshape(wts[n].shape) for r in res]
        return got

    big_out = {}
    def swap_last(a):
        return jnp.swapaxes(a, -1, -2)

    g_up_t = swap_last(_sum_parts([p_up0, p_up1], "sum_w_up").reshape(f_w_up.shape))
    res, (p_in_b, small_all) = _adamw([g_up_t[0:1], g_up_t[1:2]], *[swap_last(dct["f_w_up"]) for dct in (wts, mom, var)],
                                      "adamw_f_w_up", comm=tail)
    big_out["f_w_up"] = [swap_last(r) for r in res]
    for n in BIG:
        if n not in ("f_w_up", "a_w_in"):
            adamw_big(n)
    g_in_t = swap_last(_sum_parts([p_in_a, p_in_b], "sum_w_in"))[None]
    res = _adamw([g_in_t], *[swap_last(dct["a_w_in"]) for dct in (wts, mom, var)], "adamw_a_w_in")
    big_out["a_w_in"] = [swap_last(r) for r in res]
    mine_small = lax.dynamic_slice_in_dim(small_all, me * n_sr, n_sr, axis=1)
    parts_small = jnp.concatenate([mine_small, small_all[:, N_DEV * n_sr:]], axis=1)
    sm_in = [jnp.concatenate([flat_f32(dct, SMALL, 8), flat_f32(dct, REPL, 8)], axis=0)[None] for dct in (wts, mom, var)]
    small_out = [r[0] for r in _adamw([parts_small], *sm_in, "adamw_small")]

    outs = []
    for kind in range(4):
        res = {n: big_out[n][kind] for n in BIG}
        for n, a in zip(SMALL, _unpack(small_out[kind][:n_sr], 0, [wts[n].shape for n in SMALL])):
            res[n] = a
        for n, a in zip(REPL, _unpack(small_out[kind][n_sr:], 0, [wts[n].shape for n in REPL])):
            res[n] = a
        outs.append(res)
    return (loss, grad_x, *[outs[0][n] for n in WEIGHTS], *[outs[1][n] for n in WEIGHTS],
            *[outs[2][n] for n in WEIGHTS], *[outs[3][n] for n in WEIGHTS])
```

```python
import functools
import math

import jax
import jax.numpy as jnp
from jax import lax
from jax.experimental import pallas as pl
from jax.experimental.pallas import tpu as pltpu

F32, BF16 = jnp.float32, jnp.bfloat16
S = jax.ShapeDtypeStruct

D_MODEL = 1024
SEQ = 2048
N_META = 16
D_INNER = 2048
HEAD_P = 64
SSM_HEADS = D_INNER // HEAD_P
SSM_GROUPS = 4
D_STATE = 128
SSM_CONV = 4
D_BC = SSM_GROUPS * D_STATE
D_XBC = D_INNER + 2 * D_BC
ATTN_DH = 64
N_Q_HEADS = D_MODEL // ATTN_DH
N_KV_HEADS = 4
D_KV = N_KV_HEADS * ATTN_DH
WINDOW = 128
D_FF = 2816
FFN_CONV = 3
RMS_EPS = 1e-6
NEG = -1e30
LR, B1, B2, EPS, WD, STEP = 0.001, 0.9, 0.999, 1e-08, 0.01, 10

N_DEV = 8
T = 128
LANE = 128
VMEM_LIMIT = 48 * 1024 * 1024

BIG = ("a_w_in", "a_w_out", "w_kv", "b_w_q", "b_w_o", "f_w_up", "f_w_down")
SMALL = ("meta_tokens", "a_norm_pre", "a_conv_w", "a_conv_b", "a_gate_norm", "a_norm_post", "f_conv_w")
REPL = ("a_dt_bias", "a_a_log", "a_d_skip", "kv_norm", "b_norm_pre", "b_sinks", "b_norm_post",
        "f_norm_pre", "f_conv_b", "f_norm_post")
SHARD_AXIS = dict(a_w_in=2, a_w_out=1, w_kv=0, b_w_q=1, b_w_o=1, f_w_up=2, f_w_down=1, meta_tokens=1,
                  a_norm_pre=1, a_conv_w=2, a_conv_b=1, a_gate_norm=1, a_norm_post=1, f_conv_w=2)
WEIGHTS = ("meta_tokens", "a_norm_pre", "a_w_in", "a_conv_w", "a_conv_b", "a_dt_bias", "a_a_log", "a_d_skip",
           "a_gate_norm", "a_w_out", "a_norm_post", "kv_norm", "w_kv", "b_norm_pre", "b_w_q", "b_sinks", "b_w_o",
           "b_norm_post", "f_norm_pre", "f_w_up", "f_conv_w", "f_conv_b", "f_w_down", "f_norm_post")


def _seq_rows():
    return -(-(N_META + SEQ) // T) * T


def _cp(sem=None):
    return pltpu.CompilerParams(dimension_semantics=sem, vmem_limit_bytes=VMEM_LIMIT)


def _pick(n, target):
    t = min(n, target)
    t -= t % LANE
    while n % t:
        t -= LANE
    return t


def _sigmoid(x):
    return 0.5 * jnp.tanh(0.5 * x) + 0.5


def _softplus(x):
    return jnp.maximum(x, 0.0) + jnp.log(1.0 + jnp.exp(-jnp.abs(x)))


_NN = (((1,), (0,)), ((), ()))
_NT = (((1,), (1,)), ((), ()))
_TN = (((0,), (0,)), ((), ()))


def _dot(a, b, dims=_NN):
    return lax.dot_general(a, b, dims, preferred_element_type=F32)


def _dot_hi(a, b):
    return lax.dot_general(a, b, _NN, precision=lax.Precision.HIGHEST, preferred_element_type=F32)


_HBM = pl.BlockSpec(memory_space=pltpu.HBM)
_MESH = pl.DeviceIdType.MESH


class _Comm:
    def __init__(self, ins, out_shapes, scratch, first, last):
        self.ins, self.out_shapes, self.scratch, self.first, self.last = ins, out_shapes, scratch, first, last


def _call(body, name, out_shape, grid, in_specs, out_specs, sem, args, scratch=(), comm=None):
    if comm is None:
        return pl.pallas_call(body, name=name, out_shape=out_shape, grid=grid, in_specs=in_specs, out_specs=out_specs,
                              scratch_shapes=list(scratch), compiler_params=_cp(sem))(*args)
    single = not isinstance(out_shape, (list, tuple))
    outs = [out_shape] if single else list(out_shape)
    ospecs = [out_specs] if single else list(out_specs)
    n_in, n_out, n_scr, ci, co = len(in_specs), len(outs), len(scratch), len(comm.ins), len(comm.out_shapes)

    def carrier(*refs):
        p = 0
        parts = []
        for cnt in (n_in, ci, n_out, co, n_scr, len(comm.scratch)):
            parts.append(refs[p:p + cnt])
            p += cnt
        ins, cins, outs_r, couts, scr, cscr = parts
        ids = [pl.program_id(i) for i in range(len(grid))]
        first, last = ids[0] == 0, ids[0] == grid[0] - 1
        for i in range(1, len(grid)):
            first, last = first & (ids[i] == 0), last & (ids[i] == grid[i] - 1)

        @pl.when(first)
        def _():
            comm.first(cins, couts, cscr)

        body(*ins, *outs_r, *scr)

        @pl.when(last)
        def _():
            comm.last(cins, couts, cscr)

    res = pl.pallas_call(
        carrier, name=name, out_shape=outs + list(comm.out_shapes), grid=grid,
        in_specs=list(in_specs) + [_HBM] * ci, out_specs=ospecs + [_HBM] * co,
        scratch_shapes=list(scratch) + list(comm.scratch),
        compiler_params=_cp(("arbitrary",) * len(grid)))(*args, *comm.ins)
    mine = res[0] if single else list(res[:n_out])
    return mine, list(res[n_out:])


def _mm(a, b, mode, out_dtype, name, comm=None, shard_cols=None):
    if mode == "tn":
        m, kk = a.shape
        planes, width = (b.shape[0], b.shape[2]) if b.ndim == 3 else (1, b.shape[1])
        n = planes * width
        tko, tn = _pick(kk, 512), _pick(width, 1536)
        per = width // tn
        b_spec = (pl.BlockSpec((None, m, tn), lambda i, j: (j // per, 0, j % per)) if b.ndim == 3
                  else pl.BlockSpec((m, tn), lambda i, j: (0, j)))
        if shard_cols is None:
            def body(a_ref, b_ref, o_ref):
                o_ref[...] = _dot(a_ref[...], b_ref[...], _TN).astype(o_ref.dtype)

            out_shape, out_spec = S((kk, n), out_dtype), pl.BlockSpec((tko, tn), lambda i, j: (i, j))
        else:
            shards = tn // shard_cols
            assert tn % shard_cols == 0

            def body(a_ref, b_ref, o_ref):
                res = _dot(a_ref[...], b_ref[...], _TN).astype(o_ref.dtype)
                for p in range(shards):
                    o_ref[p] = res[:, p * shard_cols:(p + 1) * shard_cols]

            out_shape = S((n // shard_cols, kk, shard_cols), out_dtype)
            out_spec = pl.BlockSpec((shards, tko, shard_cols), lambda i, j: (j, i, 0))
        return _call(
            body, name, out_shape, (kk // tko, n // tn), [pl.BlockSpec((m, tko), lambda i, j: (0, i)), b_spec],
            out_spec, ("parallel", "parallel"), (a, b), comm=comm)

    planes, width = (a.shape[0], a.shape[2]) if a.ndim == 3 else (1, a.shape[1])
    m, kk = a.shape[-2], planes * width
    n = b.shape[1] if mode == "nn" else b.shape[0]
    dims = _NN if mode == "nn" else _NT

    if kk > 2048:
        tm = m // 4
        assert m % 4 == 0 and tm % 16 == 0

        def body(a_ref, b_ref, o_ref):
            if a.ndim == 2:
                res = _dot(a_ref[...], b_ref[...], dims)
            else:
                res = None
                for p in range(planes):
                    bp = b_ref[p * width:(p + 1) * width, :] if mode == "nn" else b_ref[:, p * width:(p + 1) * width]
                    part = _dot(a_ref[p], bp, dims)
                    res = part if res is None else res + part
            o_ref[...] = res.astype(o_ref.dtype)

        a_spec = (pl.BlockSpec((planes, tm, width), lambda i: (0, i, 0)) if a.ndim == 3
                  else pl.BlockSpec((tm, kk), lambda i: (i, 0)))
        return _call(
            body, name, S((m, n), out_dtype), (m // tm,),
            [a_spec, pl.BlockSpec(b.shape, lambda i: (0, 0), pipeline_mode=pl.Buffered(1))],
            pl.BlockSpec((tm, n), lambda i: (i, 0)), ("parallel",), (a, b), comm=comm)

    tn = _pick(n, 512)

    def body(a_ref, b_ref, o_ref):
        o_ref[...] = _dot(a_ref[...], b_ref[...], dims).astype(o_ref.dtype)

    b_spec = (pl.BlockSpec((kk, tn), lambda j: (0, j)) if mode == "nn" else pl.BlockSpec((tn, kk), lambda j: (j, 0)))
    return _call(
        body, name, S((m, n), out_dtype), (n // tn,), [pl.BlockSpec((m, kk), lambda j: (0, 0)), b_spec],
        pl.BlockSpec((m, tn), lambda j: (0, j)), ("parallel",), (a, b), comm=comm)


def _rms(x, w):
    return x * lax.rsqrt(jnp.mean(x * x, axis=-1, keepdims=True) + RMS_EPS) * w


def _row_tile(rows):
    return rows // 8


def _embed_norm(meta, x, w, rows, name, comm=None):
    n_meta, d = meta.shape
    n_x = x.shape[0]
    last = rows // T - 1
    assert n_meta % 8 == 0 and n_meta < T and n_meta + n_x == last * T + n_meta and last * T >= n_x

    def body(m_ref, x_ref, w_ref, h_ref, hn_ref):
        i = pl.program_id(0)

        @pl.when(i == 0)
        def _():
            h_ref[0:n_meta, :] = m_ref[...]
            h_ref[n_meta:T, :] = x_ref[0:T - n_meta, :]

        @pl.when((i > 0) & (i < last))
        def _():
            h_ref[...] = x_ref[pl.ds(pl.multiple_of(i * T - n_meta, 8), T), :]

        @pl.when(i == last)
        def _():
            h_ref[0:n_meta, :] = x_ref[n_x - n_meta:n_x, :]
            h_ref[n_meta:T, :] = jnp.zeros((T - n_meta, d), F32)

        hn_ref[...] = _rms(h_ref[...], w_ref[...]).astype(hn_ref.dtype)

    row = pl.BlockSpec((T, d), lambda i: (i, 0))
    return _call(body, name, [S((rows, d), F32), S((rows, d), BF16)], (rows // T,),
                 [pl.BlockSpec((n_meta, d), lambda i: (0, 0)), pl.BlockSpec((n_x, d), lambda i: (0, 0)),
                  pl.BlockSpec((1, d), lambda i: (0, 0))], [row, row], ("parallel",), (meta, x, w), comm=comm)


def _resid_norm(h, br, w_post, next_ws, name):
    rows, d = h.shape
    tr = _row_tile(rows)
    has_br = br is not None
    nw = len(next_ws)

    def body(*refs):
        h_ref = refs[0]
        pos = 1
        x = h_ref[...]
        if has_br:
            x = x + _rms(refs[1][...], refs[2][...])
            pos = 3
        w_refs = refs[pos:pos + nw]
        outs = refs[pos + nw:]
        if has_br:
            outs[0][...] = x
            outs = outs[1:]
        for w_ref, o_ref in zip(w_refs, outs):
            o_ref[...] = _rms(x, w_ref[...]).astype(o_ref.dtype)

    row = pl.BlockSpec((tr, d), lambda i: (i, 0))
    vec = pl.BlockSpec((1, d), lambda i: (0, 0))
    ins = [h] + ([br, w_post] if has_br else []) + list(next_ws)
    in_specs = [row] + ([row, vec] if has_br else []) + [vec] * nw
    out_shape = ([S((rows, d), F32)] if has_br else []) + [S((rows, d), BF16)] * nw
    res = pl.pallas_call(body, name=name, out_shape=out_shape, grid=(rows // tr,), in_specs=in_specs,
                         out_specs=[row] * len(out_shape), compiler_params=_cp(("parallel",)))(*ins)
    if has_br:
        return res[0], list(res[1:])
    return h, list(res)


def _rms_bwd(xv, w, dyv):
    r = lax.rsqrt(jnp.mean(xv * xv, axis=-1, keepdims=True) + RMS_EPS)
    wdy = dyv * w
    dx = r * wdy - xv * (r * r * r) * jnp.mean(xv * wdy, axis=-1, keepdims=True)
    return dx, jnp.sum(dyv * xv * r, axis=0, keepdims=True)


def _norm_bwd(x, w, dy, add, out_dtype, name, comm=None, then=None):
    rows, d = x.shape
    tr = _row_tile(rows)
    has_add = add is not None
    n_in = 3 + has_add + (2 if then is not None else 0)

    def body(*refs):
        x_ref, w_ref, dy_ref = refs[:3]
        outs = refs[n_in:]
        dx, dw = _rms_bwd(x_ref[...], w_ref[...], dy_ref[...].astype(F32))
        if has_add:
            dx = dx + refs[3][...]
        outs[0][...] = dx.astype(outs[0].dtype)
        first = pl.program_id(0) == 0

        @pl.when(first)
        def _():
            outs[1][...] = jnp.zeros_like(outs[1])

        outs[1][...] += dw
        if then is not None:
            dx2, dw2 = _rms_bwd(refs[n_in - 2][...], refs[n_in - 1][...], dx)
            outs[2][...] = dx2.astype(outs[2].dtype)

            @pl.when(first)
            def _():
                outs[3][...] = jnp.zeros_like(outs[3])

            outs[3][...] += dw2

    row = pl.BlockSpec((tr, d), lambda i: (i, 0))
    vec = pl.BlockSpec((1, d), lambda i: (0, 0))
    ins = [x, w, dy] + ([add] if has_add else []) + (list(then) if then is not None else [])
    in_specs = [row, vec, row] + ([row] if has_add else []) + ([row, vec] if then is not None else [])
    out_shape = [S((rows, d), out_dtype), S((1, d), F32)] + ([S((rows, d), BF16), S((1, d), F32)] if then is not None else [])
    return _call(body, name, out_shape, (rows // tr,), in_specs, [row, vec] * (len(out_shape) // 2), ("arbitrary",),
                 ins, comm=comm)


def _final_loss(h, br, w_post, tgt, name):
    rows, d = h.shape
    tr = _row_tile(rows)

    def body(h_ref, br_ref, w_ref, t_ref, dh_ref, loss_ref, dbr_ref, dw_ref):
        i = pl.program_id(0)
        brv, wv = br_ref[...], w_ref[...]
        y = h_ref[...] + _rms(brv, wv)
        r = i * tr + lax.broadcasted_iota(jnp.int32, (tr, 1), 0)
        real = (r >= N_META) & (r < N_META + SEQ)
        diff = jnp.where(real, y - t_ref[...], 0.0)
        dh = diff * (1.0 / d)
        dh_ref[...] = dh
        dbr, dw = _rms_bwd(brv, wv, dh)
        dbr_ref[...] = dbr.astype(dbr_ref.dtype)

        @pl.when(i == 0)
        def _():
            loss_ref[...] = jnp.zeros_like(loss_ref)
            dw_ref[...] = jnp.zeros_like(dw_ref)

        loss_ref[...] += jnp.sum(diff * diff) * (0.5 / d)
        dw_ref[...] += dw

    row = pl.BlockSpec((tr, d), lambda i: (i, 0))
    vec = pl.BlockSpec((1, d), lambda i: (0, 0))
    return pl.pallas_call(body, name=name,
                          out_shape=[S((rows, d), F32), S((1, LANE), F32), S((rows, d), BF16), S((1, d), F32)],
                          grid=(rows // tr,), in_specs=[row, row, vec, row],
                          out_specs=[row, pl.BlockSpec((1, LANE), lambda i: (0, 0)), row, vec],
                          compiler_params=_cp(("arbitrary",)))(h, br, w_post, tgt)


def _gatenorm_fwd(y, zx, w, name, comm=None):
    rows, d = y.shape
    tr = _row_tile(rows)

    def body(y_ref, z_ref, w_ref, o_ref):
        z = z_ref[...]
        o_ref[...] = _rms(y_ref[...] * z * _sigmoid(z), w_ref[...]).astype(o_ref.dtype)

    row = pl.BlockSpec((tr, d), lambda i: (i, 0))
    return _call(body, name, S((rows, d), BF16), (rows // tr,), [row, row, pl.BlockSpec((1, d), lambda i: (0, 0))],
                 row, ("parallel",), (y, zx, w), comm=comm)


def _gatenorm_bwd(y, zx, w, dyn, name, comm=None):
    rows, d = y.shape
    tr = _row_tile(rows)

    def body(y_ref, z_ref, w_ref, dyn_ref, dy_ref, dz_ref, dw_ref):
        yv, z = y_ref[...], z_ref[...]
        sg = _sigmoid(z)
        sz = z * sg
        g = yv * sz
        r = lax.rsqrt(jnp.mean(g * g, axis=-1, keepdims=True) + RMS_EPS)
        dyn_v = dyn_ref[...]
        wdy = dyn_v * w_ref[...]
        dg = r * wdy - g * (r * r * r) * jnp.mean(g * wdy, axis=-1, keepdims=True)
        dy_ref[...] = dg * sz
        dz_ref[...] = (dg * yv * sg * (1.0 + z * (1.0 - sg))).astype(dz_ref.dtype)

        @pl.when(pl.program_id(0) == 0)
        def _():
            dw_ref[...] = jnp.zeros_like(dw_ref)

        dw_ref[...] += jnp.sum(dyn_v * g * r, axis=0, keepdims=True)

    row = pl.BlockSpec((tr, d), lambda i: (i, 0))
    vec = pl.BlockSpec((1, d), lambda i: (0, 0))
    return _call(body, name, [S((rows, d), F32), S((rows, d), BF16), S((1, d), F32)], (rows // tr,),
                 [row, row, vec, row], [row, row, vec], ("arbitrary",), (y, zx, w, dyn), comm=comm)


def _shift_down(x, s, rows_iota):
    if s == 0:
        return x
    return jnp.where(rows_iota >= s, pltpu.roll(x, s, 0), 0.0)


def _shift_up(x, s, rows_iota):
    if s == 0:
        return x
    rows = x.shape[0]
    return jnp.where(rows_iota < rows - s, pltpu.roll(x, rows - s, 0), 0.0)


def _r16(v):
    return v.astype(BF16).astype(F32)


def _conv_taps(x, taps, rows_iota):
    x = _r16(x)
    return [_shift_down(x, taps - 1 - k, rows_iota) for k in range(taps)]


def _conv(x, w_ref, b_ref, taps, rows_iota, shifted=None):
    shifted = _conv_taps(x, taps, rows_iota) if shifted is None else shifted
    acc = jnp.zeros_like(shifted[0])
    for k in range(taps):
        acc = acc + _r16(w_ref[k:k + 1, :]) * shifted[k]
    return acc + b_ref[...]


def _conv_bwd(shifted, du, w_ref, dw_ref, db_ref, taps, rows_iota):
    db_ref[...] = jnp.sum(du, axis=0, keepdims=True)
    du = _r16(du)
    dx = jnp.zeros_like(du)
    for k in range(taps):
        dx = dx + _r16(w_ref[k:k + 1, :]) * _shift_up(du, taps - 1 - k, rows_iota)
        dw_ref[k:k + 1, :] = jnp.sum(du * shifted[k], axis=0, keepdims=True)
    return dx


def _conv_silu_fwd(zx, w, b, name, comm=None):
    rows = zx.shape[0]
    cb = 512
    off = D_INNER // cb

    def body(x_ref, w_ref, b_ref, o_ref):
        it = lax.broadcasted_iota(jnp.int32, (rows, 1), 0)
        u = _conv(x_ref[...], w_ref, b_ref, SSM_CONV, it)
        o_ref[...] = u * _sigmoid(u)

    return _call(
        body, name, S((rows, D_XBC), F32), (D_XBC // cb,),
        [pl.BlockSpec((rows, cb), lambda j: (0, off + j)), pl.BlockSpec((SSM_CONV, cb), lambda j: (0, j)),
         pl.BlockSpec((1, cb), lambda j: (0, j))],
        pl.BlockSpec((rows, cb), lambda j: (0, j)), ("parallel",), (zx, w, b), comm=comm)


def _conv_silu_bwd(zx, dxs, dbm, dcm, w, b, name, comm=None):
    rows = zx.shape[0]
    cb = 256
    off = D_INNER // cb
    nx, nbc = D_INNER // cb, D_BC // cb

    def body(x_ref, dx_in, db_in, dc_in, w_ref, b_ref, dx_ref, dw_ref, db_ref, dbuf):
        j = pl.program_id(0)
        for cond, src in ((j < nx, dx_in), ((j >= nx) & (j < nx + nbc), db_in), (j >= nx + nbc, dc_in)):
            @pl.when(cond)
            def _(src=src):
                dbuf[...] = src[...]
        it = lax.broadcasted_iota(jnp.int32, (rows, 1), 0)
        xs = _conv_taps(x_ref[...], SSM_CONV, it)
        u = _conv(None, w_ref, b_ref, SSM_CONV, it, xs)
        sg = _sigmoid(u)
        du = dbuf[...] * sg * (1.0 + u * (1.0 - sg))
        dx_ref[...] = _conv_bwd(xs, du, w_ref, dw_ref, db_ref, SSM_CONV, it).astype(dx_ref.dtype)

    def part(first, count):
        return pl.BlockSpec((rows, cb), lambda j: (0, jnp.clip(j - first, 0, count - 1)))

    col = pl.BlockSpec((rows, cb), lambda j: (0, j))
    wsp = pl.BlockSpec((SSM_CONV, cb), lambda j: (0, j))
    bsp = pl.BlockSpec((1, cb), lambda j: (0, j))
    return _call(
        body, name, [S((rows, D_XBC), BF16), S((SSM_CONV, D_XBC), F32), S((1, D_XBC), F32)], (D_XBC // cb,),
        [pl.BlockSpec((rows, cb), lambda j: (0, off + j)), part(0, nx), part(nx, nbc), part(nx + nbc, nbc), wsp, bsp],
        [col, wsp, bsp], ("arbitrary",), (zx, dxs, dbm, dcm, w, b), scratch=[pltpu.VMEM((rows, cb), F32)], comm=comm)


def _ffn_act_fwd(u, w, b, name, comm=None):
    rows = u.shape[0]
    cb = 256
    nb = D_FF // cb

    def body(g_ref, v_ref, wg_ref, wv_ref, bg_ref, bv_ref, o_ref):
        it = lax.broadcasted_iota(jnp.int32, (rows, 1), 0)
        g = _conv(g_ref[...], wg_ref, bg_ref, FFN_CONV, it)
        v = _conv(v_ref[...], wv_ref, bv_ref, FFN_CONV, it)
        o_ref[...] = (g * _sigmoid(g) * v).astype(o_ref.dtype)

    def sp(r, shift):
        return pl.BlockSpec((r, cb), lambda j: (0, shift + j))

    return _call(
        body, name, S((rows, D_FF), BF16), (nb,),
        [sp(rows, 0), sp(rows, nb), sp(FFN_CONV, 0), sp(FFN_CONV, nb), sp(1, 0), sp(1, nb)],
        sp(rows, 0), ("parallel",), (u, u, w, w, b, b), comm=comm)


def _ffn_act_bwd(u, dact, w, b, name, comm=None):
    rows = u.shape[0]
    cb = 256
    nb = D_FF // cb

    def body(g_ref, v_ref, d_ref, wg_ref, wv_ref, bg_ref, bv_ref, du_ref, dw_ref, db_ref):
        it = lax.broadcasted_iota(jnp.int32, (rows, 1), 0)
        xg, xv = _conv_taps(g_ref[...], FFN_CONV, it), _conv_taps(v_ref[...], FFN_CONV, it)
        g = _conv(None, wg_ref, bg_ref, FFN_CONV, it, xg)
        v = _conv(None, wv_ref, bv_ref, FFN_CONV, it, xv)
        sg = _sigmoid(g)
        d = d_ref[...]
        dgate = d * v * sg * (1.0 + g * (1.0 - sg))
        dval = d * g * sg
        du_ref[0] = _conv_bwd(xg, dgate, wg_ref, dw_ref.at[0], db_ref.at[0], FFN_CONV, it).astype(du_ref.dtype)
        du_ref[1] = _conv_bwd(xv, dval, wv_ref, dw_ref.at[1], db_ref.at[1], FFN_CONV, it).astype(du_ref.dtype)

    def sp(r, shift):
        return pl.BlockSpec((r, cb), lambda j: (0, shift + j))

    def both(r):
        return pl.BlockSpec((2, r, cb), lambda j: (0, 0, j))

    return _call(
        body, name, [S((2, rows, D_FF), BF16), S((2, FFN_CONV, D_FF), F32), S((2, 1, D_FF), F32)], (nb,),
        [sp(rows, 0), sp(rows, nb), sp(rows, 0), sp(FFN_CONV, 0), sp(FFN_CONV, nb), sp(1, 0), sp(1, nb)],
        [both(rows), both(FFN_CONV), both(1)], ("parallel",), (u, u, dact, w, w, b, b), comm=comm)


def _ssd_consts(dtp_ref, bias_ref, alog_ref, hg):
    lane = lax.broadcasted_iota(jnp.int32, (1, LANE), 1)
    pre = dtp_ref[...] + bias_ref[...]
    dt = _softplus(pre)
    a_row = jnp.where(lane < hg, -jnp.exp(alog_ref[...]), 0.0)
    ri = lax.broadcasted_iota(jnp.int32, (T, T), 0)
    ci = lax.broadcasted_iota(jnp.int32, (T, T), 1)
    cs = _dot_hi((ri >= ci).astype(F32), dt * a_row)
    return pre, dt, a_row, cs, ri, ci, lane


def _head_rows(src, hg):
    return jnp.concatenate([jnp.broadcast_to(src[k:k + 1, :], (HEAD_P, src.shape[1])) for k in range(hg)], axis=0)


def _ssd_fwd(xbc, zx, bias, alog, dsk, name, comm=None):
    rows = xbc.shape[0]
    nc = rows // T
    hg = SSM_HEADS // SSM_GROUPS
    gw = hg * HEAD_P
    xoff, boff, coff = 0, D_INNER // D_STATE, (D_INNER + D_BC) // D_STATE
    dtoff = (D_INNER + D_XBC) // LANE

    def body(x_ref, b_ref, c_ref, dtp_ref, bias_ref, alog_ref, dsk_ref, y_ref, hst_ref, hs):
        c = pl.program_id(1)

        @pl.when(c == 0)
        def _():
            hs[...] = jnp.zeros_like(hs)

        _, dt, _, cs, ri, ci, _ = _ssd_consts(dtp_ref, bias_ref, alog_ref, hg)
        cst, dtt = cs.T, dt.T
        xt = x_ref[...].T
        bb, cbf = b_ref[...].astype(BF16), c_ref[...].astype(BF16)
        gt = _dot(bb, cbf, _NT)
        causal_t = ci >= ri
        dskv = dsk_ref[...]
        hall = hs[...]
        hst_ref[0, 0] = hall
        cs8 = cst[0:8, :]
        cl8 = cs8[:, T - 1:T]
        xdt = xt * _head_rows(dtt, hg)
        yo = _head_rows(jnp.exp(cs8), hg) * _dot(hall.astype(BF16), cbf, _NT)
        st = _dot((xdt * _head_rows(jnp.exp(cl8 - cs8), hg)).astype(BF16), bb)
        hs[...] = _head_rows(jnp.exp(cl8), hg) * hall + st
        yds = []
        for k in range(hg):
            sl = slice(k * HEAD_P, (k + 1) * HEAD_P)
            lt = jnp.exp(jnp.where(causal_t, cst[k:k + 1, :] - cs[:, k:k + 1], NEG))
            yds.append(_dot(xdt[sl, :].astype(BF16), (gt * lt).astype(BF16)))
        dsk_r = jnp.concatenate([jnp.broadcast_to(dskv[:, k:k + 1], (HEAD_P, 1)) for k in range(hg)], axis=0)
        y_ref[...] = (jnp.concatenate(yds, axis=0) + yo + dsk_r * xt).T

    vec = pl.BlockSpec((1, LANE), lambda g, c: (0, g))
    return _call(
        body, name, [S((rows, D_INNER), F32), S((nc, SSM_GROUPS, gw, D_STATE), F32)], (SSM_GROUPS, nc),
        [pl.BlockSpec((T, gw), lambda g, c: (c, xoff + g)),
         pl.BlockSpec((T, D_STATE), lambda g, c: (c, boff + g)),
         pl.BlockSpec((T, D_STATE), lambda g, c: (c, coff + g)),
         pl.BlockSpec((T, LANE), lambda g, c: (c, dtoff + g)), vec, vec, vec],
        [pl.BlockSpec((T, gw), lambda g, c: (c, g)), pl.BlockSpec((1, 1, gw, D_STATE), lambda g, c: (c, g, 0, 0))],
        ("parallel", "arbitrary"), (xbc, xbc, xbc, zx, bias, alog, dsk),
        scratch=[pltpu.VMEM((gw, D_STATE), F32)], comm=comm)


def _ssd_bwd(xbc, zx, bias, alog, dsk, dy, hst, name, comm=None):
    rows = xbc.shape[0]
    nc = rows // T
    hg = SSM_HEADS // SSM_GROUPS
    gw = hg * HEAD_P
    boff, coff = D_INNER // D_STATE, (D_INNER + D_BC) // D_STATE
    dtoff = (D_INNER + D_XBC) // LANE

    def body(x_ref, b_ref, c_ref, dtp_ref, bias_ref, alog_ref, dsk_ref, dy_ref, hst_ref,
             dx_ref, db_ref, dc_ref, ddtp_ref, dalog_ref, ddsk_ref, dbias_ref, dhs):
        step = pl.program_id(1)

        @pl.when(step == 0)
        def _():
            dhs[...] = jnp.zeros_like(dhs)
            dalog_ref[...] = jnp.zeros_like(dalog_ref)
            ddsk_ref[...] = jnp.zeros_like(ddsk_ref)
            dbias_ref[...] = jnp.zeros_like(dbias_ref)

        pre, dt, a_row, cs, ri, ci, lane = _ssd_consts(dtp_ref, bias_ref, alog_ref, hg)
        cst, dtt = cs.T, dt.T
        xt, dyt = x_ref[...].T, dy_ref[...].T
        bb, cbf = b_ref[...].astype(BF16), c_ref[...].astype(BF16)
        gt = _dot(bb, cbf, _NT)
        causal_t = ci >= ri
        dskv = dsk_ref[...]
        hall, dhall = hst_ref[0, 0], dhs[...]
        head_row = lax.broadcasted_iota(jnp.int32, (T, 1), 0)
        last_l = lax.broadcasted_iota(jnp.int32, (1, T), 1) == T - 1
        cs8, dt8 = cst[0:8, :], dtt[0:8, :]
        cl8 = cs8[:, T - 1:T]
        e8, wdec8 = jnp.exp(cs8), jnp.exp(cl8 - cs8)
        w8 = wdec8 * dt8
        dt_r, e_r, w_r, ecl_r = _head_rows(dt8, hg), _head_rows(e8, hg), _head_rows(w8, hg), _head_rows(jnp.exp(cl8), hg)
        dsk_r = jnp.concatenate([jnp.broadcast_to(dskv[:, k:k + 1], (HEAD_P, 1)) for k in range(hg)], axis=0)
        hb, dhb = hall.astype(BF16), dhall.astype(BF16)
        xdt = xt * dt_r
        dye = (dyt * e_r).astype(BF16)
        rt = _dot(dhb, bb, _NT)
        yo = e_r * _dot(hb, cbf, _NT)
        dhs[...] = ecl_r * dhall + _dot(dye, cbf)
        dc_acc = _dot(dye, hb, _TN)
        db_acc = _dot((xt * w_r).astype(BF16), dhb, _TN)
        rtx, dyyo, hdh, dyx = rt * xt, dyt * yo, dhall * hall, dyt * xt
        dgt = jnp.zeros((T, T), F32)
        ddt_rows = jnp.zeros((T, T), F32)
        dcs_rows = jnp.zeros((T, T), F32)
        qrow_cols = jnp.zeros((T, LANE), F32)
        ddsk_acc = jnp.zeros((1, LANE), F32)
        dxdts = []
        for k in range(hg):
            sl = slice(k * HEAD_P, (k + 1) * HEAD_P)
            lt = jnp.exp(jnp.where(causal_t, cst[k:k + 1, :] - cs[:, k:k + 1], NEG))
            mpt = gt * lt
            dyb = dyt[sl, :].astype(BF16)
            dxdt = _dot(dyb, mpt.astype(BF16), _NT)
            dmt = _dot(xdt[sl, :].astype(BF16), dyb, _TN)
            dgt = dgt + dmt * lt
            q = dmt * mpt
            q_rows = jnp.sum(q, axis=1, keepdims=True)
            q_cols = jnp.sum(q, axis=0, keepdims=True)
            dxdts.append(dxdt)
            xz = jnp.sum(xt[sl, :] * dxdt, axis=0, keepdims=True)
            dw = jnp.sum(rtx[sl, :], axis=0, keepdims=True)
            wk, wdeck = w8[k:k + 1, :], wdec8[k:k + 1, :]
            dcl = jnp.exp(cl8[k:k + 1, :]) * jnp.sum(hdh[sl, :]) + jnp.sum(dw * wk)
            dcs_r = jnp.sum(dyyo[sl, :], axis=0, keepdims=True) + q_cols - dw * wk + jnp.where(last_l, dcl, 0.0)
            onehot = (lane == k).astype(F32)
            ddt_rows = ddt_rows + jnp.where(head_row == k, xz + dw * wdeck, 0.0)
            dcs_rows = dcs_rows + jnp.where(head_row == k, dcs_r, 0.0)
            qrow_cols = qrow_cols + q_rows * onehot
            ddsk_acc = ddsk_acc + jnp.sum(dyx[sl, :]) * onehot
        dx_ref[...] = (dt_r * jnp.concatenate(dxdts, axis=0) + dsk_r * dyt + rt * w_r).T
        dc_ref[...] = _dot(dgt.T.astype(BF16), bb) + dc_acc
        db_ref[...] = _dot(dgt.astype(BF16), cbf) + db_acc
        da = _dot_hi((ci >= ri).astype(F32), dcs_rows.T - qrow_cols)
        ddtp = (ddt_rows.T + da * a_row) * _sigmoid(pre)
        ddtp = jnp.where(lane < hg, ddtp, 0.0)
        ddtp_ref[...] = ddtp
        dbias_ref[...] += jnp.sum(ddtp, axis=0, keepdims=True)
        dalog_ref[...] += jnp.sum(da * dt, axis=0, keepdims=True) * a_row
        ddsk_ref[...] += ddsk_acc

    def rc(c):
        return nc - 1 - c

    vec = pl.BlockSpec((1, LANE), lambda g, c: (0, g))
    xsp = pl.BlockSpec((T, gw), lambda g, c: (rc(c), g))
    return _call(
        body, name,
        [S((rows, D_INNER), F32), S((rows, D_BC), F32), S((rows, D_BC), F32),
         S((rows, SSM_GROUPS * LANE), F32), S((1, SSM_GROUPS * LANE), F32),
         S((1, SSM_GROUPS * LANE), F32), S((1, SSM_GROUPS * LANE), F32)],
        (SSM_GROUPS, nc),
        [xsp,
         pl.BlockSpec((T, D_STATE), lambda g, c: (rc(c), boff + g)),
         pl.BlockSpec((T, D_STATE), lambda g, c: (rc(c), coff + g)),
         pl.BlockSpec((T, LANE), lambda g, c: (rc(c), dtoff + g)), vec, vec, vec,
         xsp, pl.BlockSpec((1, 1, gw, D_STATE), lambda g, c: (rc(c), g, 0, 0))],
        [xsp,
         pl.BlockSpec((T, D_STATE), lambda g, c: (rc(c), g)),
         pl.BlockSpec((T, D_STATE), lambda g, c: (rc(c), g)),
         pl.BlockSpec((T, LANE), lambda g, c: (rc(c), g)), vec, vec, vec],
        ("parallel", "arbitrary"), (xbc, xbc, xbc, zx, bias, alog, dsk, dy, hst),
        scratch=[pltpu.VMEM((gw, D_STATE), F32)], comm=comm)


def _attn_tiles(kv_ref, j):
    prev = jnp.maximum(j - 1, 0)
    meta = kv_ref[0:T, :]
    prv = kv_ref[pl.ds(pl.multiple_of(prev * T, T), T), :]
    cur = kv_ref[pl.ds(pl.multiple_of(j * T, T), T), :]
    return jnp.concatenate([meta, prv, cur], axis=0)


def _attn_mask(j):
    r = j * T + lax.broadcasted_iota(jnp.int32, (3 * T, T), 1)
    row = lax.broadcasted_iota(jnp.int32, (3 * T, T), 0)
    t0, t1 = row < T, row < 2 * T
    s = jnp.where(t0, row, (j - 2) * T + row)
    ok = (s <= r) & ((s < N_META) | (s > r - WINDOW))
    use = (t0 & (j >= 2) & (row < N_META)) | (jnp.logical_not(t0) & t1 & (j >= 1)) | jnp.logical_not(t1)
    return ok & use


def _attn_fwd(q, kv, sinks, name, comm=None):
    rows = q.shape[0]
    scale = 1.0 / math.sqrt(ATTN_DH)
    qpk = N_Q_HEADS // N_KV_HEADS

    def body(q_ref, kv_ref, s_ref, o_ref, lse_ref):
        j = pl.program_id(0)
        kv3 = _attn_tiles(kv_ref, j).astype(BF16)
        mask = _attn_mask(j)
        qv = (q_ref[...] * scale).astype(BF16)
        sk = s_ref[...]
        lses = []
        for kh in range(N_KV_HEADS):
            k3 = kv3[:, kh * ATTN_DH:(kh + 1) * ATTN_DH]
            v3 = kv3[:, D_KV + kh * ATTN_DH:D_KV + (kh + 1) * ATTN_DH]
            for g in range(qpk):
                h = kh * qpk + g
                sink = sk[:, h:h + 1]
                sc = jnp.where(mask, _dot(k3, qv[:, h * ATTN_DH:(h + 1) * ATTN_DH], _NT), NEG)
                m = jnp.maximum(jnp.max(sc, axis=0, keepdims=True), sink)
                p = jnp.exp(sc - m)
                den = jnp.sum(p, axis=0, keepdims=True) + jnp.exp(sink - m)
                p = p * (1.0 / den)
                lses.append(m + jnp.log(den))
                o_ref[:, h * ATTN_DH:(h + 1) * ATTN_DH] = _dot(p.astype(BF16), v3, _TN).astype(o_ref.dtype)
        lse_ref[...] = jnp.concatenate(lses, axis=0)

    return _call(
        body, name, [S((rows, D_MODEL), BF16), S((N_Q_HEADS, rows), F32)], (rows // T,),
        [pl.BlockSpec((T, D_MODEL), lambda j: (j, 0)), pl.BlockSpec((rows, 2 * D_KV), lambda j: (0, 0)),
         pl.BlockSpec((1, N_Q_HEADS), lambda j: (0, 0))],
        [pl.BlockSpec((T, D_MODEL), lambda j: (j, 0)), pl.BlockSpec((N_Q_HEADS, T), lambda j: (0, j))],
        ("parallel",), (q, kv, sinks), comm=comm)


def _attn_bwd(q, kv, sinks, do, lse, name, comm=None):
    rows = q.shape[0]
    scale = 1.0 / math.sqrt(ATTN_DH)
    qpk = N_Q_HEADS // N_KV_HEADS

    def body(q_ref, kv_ref, s_ref, do_ref, lse_ref, dq_ref, dkv_ref, ds_ref):
        j = pl.program_id(0)

        @pl.when(j == 0)
        def _():
            dkv_ref[...] = jnp.zeros_like(dkv_ref)
            ds_ref[...] = jnp.zeros_like(ds_ref)

        kv3 = _attn_tiles(kv_ref, j).astype(BF16)
        mask = _attn_mask(j)
        qv = (q_ref[...] * scale).astype(BF16)
        dov = do_ref[...].astype(BF16)
        sk = s_ref[...]
        lsev = lse_ref[...]
        lane = lax.broadcasted_iota(jnp.int32, (1, LANE), 1)
        ds_acc = jnp.zeros((1, LANE), F32)
        prev = jnp.maximum(j - 1, 0)
        dqts = []
        for kh in range(N_KV_HEADS):
            ksl = slice(kh * ATTN_DH, (kh + 1) * ATTN_DH)
            vsl = slice(D_KV + kh * ATTN_DH, D_KV + (kh + 1) * ATTN_DH)
            k3, v3 = kv3[:, ksl], kv3[:, vsl]
            k3t = k3.T
            dk3 = jnp.zeros((3 * T, ATTN_DH), F32)
            dv3 = jnp.zeros((3 * T, ATTN_DH), F32)
            for g in range(qpk):
                h = kh * qpk + g
                hs = slice(h * ATTN_DH, (h + 1) * ATTN_DH)
                qh, doh = qv[:, hs], dov[:, hs]
                lh = lsev[h:h + 1, :]
                p = jnp.exp(jnp.where(mask, _dot(k3, qh, _NT), NEG) - lh)
                ps = jnp.exp(sk[:, h:h + 1] - lh)
                dp = _dot(v3, doh, _NT)
                delta = jnp.sum(p * dp, axis=0, keepdims=True)
                dsc = (p * (dp - delta)).astype(BF16)
                dqts.append(_dot(k3t, dsc) * scale)
                dk3 = dk3 + _dot(dsc, qh)
                dv3 = dv3 + _dot(p.astype(BF16), doh)
                ds_acc = ds_acc - jnp.sum(ps * delta) * (lane == h).astype(F32)
            for t, start in enumerate((0, pl.multiple_of(prev * T, T), pl.multiple_of(j * T, T))):
                rsl = pl.ds(start, T)
                dkv_ref[rsl, ksl] += dk3[t * T:(t + 1) * T, :]
                dkv_ref[rsl, vsl] += dv3[t * T:(t + 1) * T, :]
        ds_ref[...] += ds_acc
        dq_ref[...] = jnp.concatenate(dqts, axis=0).T.astype(dq_ref.dtype)

    blk = pl.BlockSpec((T, D_MODEL), lambda j: (j, 0))
    full = pl.BlockSpec((rows, 2 * D_KV), lambda j: (0, 0))
    return _call(
        body, name, [S((rows, D_MODEL), BF16), S((rows, 2 * D_KV), F32), S((1, LANE), F32)], (rows // T,),
        [blk, full, pl.BlockSpec((1, N_Q_HEADS), lambda j: (0, 0)), blk, pl.BlockSpec((N_Q_HEADS, T), lambda j: (0, j))],
        [blk, full, pl.BlockSpec((1, LANE), lambda j: (0, 0))], ("arbitrary",), (q, kv, sinks, do, lse), comm=comm)


BLOCK_BYTES = 1 << 20


def _div_tile(rows, cols):
    cap = max(16, BLOCK_BYTES // (4 * cols))
    best = None
    for t in range(16, min(rows, cap) + 1, 16):
        if rows % t == 0:
            best = t
    return best if best is not None else rows


def _adamw(parts, w, m, v, name, comm=None):
    layers, rows, cols = w.shape
    n = parts[0].shape[0]
    tr = _div_tile(rows, cols)
    tc = _pick(cols, 256) if tr == rows and rows * cols * 4 > 2 * BLOCK_BYTES else cols
    c1 = 1.0 / (1.0 - B1 ** STEP)
    c2 = 1.0 / (1.0 - B2 ** STEP)

    def body(*refs):
        p_refs = refs[:layers]
        w_ref, m_ref, v_ref, g_ref, d_ref, nm_ref, nv_ref = refs[layers:]
        layer = pl.program_id(0)
        for l in range(layers):
            @pl.when(layer == l)
            def _(p_ref=p_refs[l]):
                g = p_ref[0].astype(F32)
                for i in range(1, n):
                    g = g + p_ref[i].astype(F32)
                nm = B1 * m_ref[...] + (1.0 - B1) * g
                nv = B2 * v_ref[...] + (1.0 - B2) * (g * g)
                g_ref[...] = g
                nm_ref[...] = nm
                nv_ref[...] = nv
                d_ref[...] = -LR * ((nm * c1) / (jnp.sqrt(nv * c2) + EPS) + WD * w_ref[...])

    def part_spec(l):
        return pl.BlockSpec((n, tr, tc), lambda k, i, j: (0, jnp.where(k == l, i, 0), jnp.where(k == l, j, 0)))

    row = pl.BlockSpec((None, tr, tc), lambda k, i, j: (k, i, j))
    return _call(body, name, [S((layers, rows, cols), F32)] * 4, (layers, rows // tr, cols // tc),
                 [part_spec(l) for l in range(layers)] + [row, row, row], [row] * 4,
                 ("parallel", "parallel", "parallel"), (*parts, w, m, v), comm=comm)


def _sum_parts(parts, name):
    n, rows, cols = parts[0].shape
    nb = len(parts)
    tr = _div_tile(rows, cols)

    def body(*refs):
        o_ref = refs[nb]
        blk = pl.program_id(0)
        for l in range(nb):
            @pl.when(blk == l)
            def _(p_ref=refs[l]):
                g = p_ref[0].astype(F32)
                for i in range(1, n):
                    g = g + p_ref[i].astype(F32)
                o_ref[...] = g

    def part_spec(l):
        return pl.BlockSpec((n, tr, cols), lambda k, i: (0, jnp.where(k == l, i, 0), 0))

    per = rows // tr
    return pl.pallas_call(body, name=name, out_shape=S((nb * rows, cols), F32), grid=(nb, per),
                          in_specs=[part_spec(l) for l in range(nb)],
                          out_specs=pl.BlockSpec((tr, cols), lambda k, i: (k * per + i, 0)),
                          compiler_params=_cp(("parallel", "parallel")))(*parts)


def _col_segments(ws, runs):
    segs = []
    for glo, mlo, n in runs:
        while n > 0:
            d, off = divmod(glo, ws)
            take = min(n, ws - off)
            segs.append((d, off, mlo, take))
            glo, mlo, n = glo + take, mlo + take, n - take
    return segs


def _assemble_cols(gs, width, segs, name):
    _, rows, ws = gs[0].shape
    nb = len(gs)
    rb = _div_tile(rows, width // 2)
    per = rows // rb

    def body(*refs):
        o_ref = refs[nb]
        piece = pl.program_id(0)
        for l in range(nb):
            @pl.when(piece == l)
            def _(g_ref=refs[l]):
                o_ref[...] = jnp.zeros_like(o_ref)
                for d, off, mlo, n in segs:
                    o_ref[:, mlo:mlo + n] = g_ref[d, :, off:off + n]

    def piece_spec(l):
        return pl.BlockSpec((N_DEV, rb, ws), lambda k, i: (0, jnp.where(k == l, i, 0), 0))

    return pl.pallas_call(
        body, name=name, out_shape=S((nb * rows, width), gs[0].dtype), grid=(nb, per),
        in_specs=[piece_spec(l) for l in range(nb)],
        out_specs=pl.BlockSpec((rb, width), lambda k, i: (k * per + i, 0)),
        compiler_params=_cp(("parallel", "parallel")))(*gs)


def _scatter_cols(dw, ws, segs, name):
    rows, width = dw.shape
    rb = _div_tile(rows, width)

    def body(w_ref, o_ref):
        for d, off, mlo, n in segs:
            o_ref[d, :, off:off + n] = w_ref[:, mlo:mlo + n].astype(o_ref.dtype)

    return pl.pallas_call(
        body, name=name, out_shape=S((N_DEV, rows, ws), BF16), grid=(rows // rb,),
        in_specs=[pl.BlockSpec((rb, width), lambda i: (i, 0))],
        out_specs=pl.BlockSpec((N_DEV, rb, ws), lambda i: (0, i, 0)), compiler_params=_cp(("parallel",)))(dw)


def _gather_comm(xs):
    n = len(xs)

    def setup(x_refs, out_refs, sems):
        send_sems, recv_sems, local_sems = sems
        mx, my, mc = lax.axis_index("x"), lax.axis_index("y"), lax.axis_index("c")
        me, sibling = (mx, my, mc), (mx, my, 1 - mc)
        chips = [(1 - mx, my), (mx, 1 - my), (1 - mx, 1 - my)]

        def blk(a, px, py, pc):
            return out_refs[a].at[4 * px + 2 * py + pc]

        def copy(a, k, block, to, src=None):
            return pltpu.make_async_remote_copy(
                src_ref=blk(a, *block) if src is None else src, dst_ref=blk(a, *block),
                send_sem=send_sems.at[a, k], recv_sem=recv_sems.at[a, k], device_id=to, device_id_type=_MESH)

        mine = [pltpu.make_async_copy(x_refs[a], blk(a, *me), local_sems.at[a]) for a in range(n)]
        own = []
        for a in range(n):
            own.append(copy(a, 0, me, sibling, src=x_refs[a]))
            own += [copy(a, 1 + i, me, (*chip, mc), src=x_refs[a]) for i, chip in enumerate(chips)]
        return me, sibling, chips, mc, copy, mine, own

    def first(x_refs, out_refs, sems):
        _, _, _, _, _, mine, own = setup(x_refs, out_refs, sems)
        for cp in mine + own:
            cp.start()

    def last(x_refs, out_refs, sems):
        me, sibling, chips, mc, copy, mine, own = setup(x_refs, out_refs, sems)
        passed = []
        for a in range(n):
            for i, chip in enumerate(chips):
                copy(a, 1 + i, (*chip, mc), me).wait_recv()
                passed.append(copy(a, 4 + i, (*chip, mc), sibling))
                passed[-1].start()
        for a in range(n):
            copy(a, 0, sibling, me).wait_recv()
            for i, chip in enumerate(chips):
                copy(a, 4 + i, (*chip, 1 - mc), me).wait_recv()
        for cp in own + passed:
            cp.wait_send()
        for cp in mine:
            cp.wait()

    return _Comm(list(xs), [S((N_DEV,) + x.shape, x.dtype) for x in xs],
                 [pltpu.SemaphoreType.DMA((n, 7)), pltpu.SemaphoreType.DMA((n, 7)), pltpu.SemaphoreType.DMA((n,))],
                 first, last)


def _swap_comm(gs):
    n = len(gs)

    def copies(g_refs, out_refs, sems):
        send_sems, recv_sems = sems
        mx, my, mc = lax.axis_index("x"), lax.axis_index("y"), lax.axis_index("c")
        return [pltpu.make_async_remote_copy(
            src_ref=g_refs[a].at[2 * k + 1 - mc], dst_ref=out_refs[a].at[k], send_sem=send_sems.at[a, k],
            recv_sem=recv_sems.at[a, k], device_id=(mx, my, 1 - mc), device_id_type=_MESH)
            for a in range(n) for k in range(4)]

    def first(g_refs, out_refs, sems):
        for cp in copies(g_refs, out_refs, sems):
            cp.start()

    def last(g_refs, out_refs, sems):
        for cp in copies(g_refs, out_refs, sems):
            cp.wait()

    return _Comm(list(gs), [S((4,) + g.shape[1:], g.dtype) for g in gs],
                 [pltpu.SemaphoreType.DMA((n, 4)), pltpu.SemaphoreType.DMA((n, 4))], first, last)


def _chips_comm(parts):
    n = len(parts)

    def copies(p_refs, out_refs, sems):
        send_sems, recv_sems, local_sems = sems
        mx, my, mc = lax.axis_index("x"), lax.axis_index("y"), lax.axis_index("c")
        mychip = 2 * mx + my
        chips = [(1 - mx, my), (mx, 1 - my), (1 - mx, 1 - my)]
        mine = [pltpu.make_async_copy(p_refs[a].at[mychip], out_refs[a].at[mychip], local_sems.at[a])
                for a in range(n)]
        return mine + [pltpu.make_async_remote_copy(
            src_ref=p_refs[a].at[2 * cx + cy], dst_ref=out_refs[a].at[mychip], send_sem=send_sems.at[a, i],
            recv_sem=recv_sems.at[a, i], device_id=(cx, cy, mc), device_id_type=_MESH)
            for a in range(n) for i, (cx, cy) in enumerate(chips)]

    def first(p_refs, out_refs, sems):
        for cp in copies(p_refs, out_refs, sems):
            cp.start()

    def last(p_refs, out_refs, sems):
        for cp in copies(p_refs, out_refs, sems):
            cp.wait()

    return _Comm(list(parts), [S(p.shape, p.dtype) for p in parts],
                 [pltpu.SemaphoreType.DMA((n, 3)), pltpu.SemaphoreType.DMA((n, 3)), pltpu.SemaphoreType.DMA((n,))],
                 first, last)


def _join_comms(comms):
    def split(refs, counts):
        out, p = [], 0
        for cnt in counts:
            out.append(refs[p:p + cnt])
            p += cnt
        return out

    ni = [len(c.ins) for c in comms]
    no = [len(c.out_shapes) for c in comms]
    ns = [len(c.scratch) for c in comms]

    def first(in_refs, out_refs, sems):
        for c, i, o, s in zip(comms, split(in_refs, ni), split(out_refs, no), split(sems, ns)):
            c.first(i, o, s)

    def last(in_refs, out_refs, sems):
        for c, i, o, s in zip(comms, split(in_refs, ni), split(out_refs, no), split(sems, ns)):
            c.last(i, o, s)

    return _Comm([x for c in comms for x in c.ins], [x for c in comms for x in c.out_shapes],
                 [x for c in comms for x in c.scratch], first, last)


def _add_pairs(mine, theirs, core, name):
    _, rows, cols = mine.shape
    tr = _div_tile(rows, cols)

    def body(core_ref, a_ref, b_ref, o_ref):
        o_ref[...] = (a_ref[...].astype(F32) + b_ref[...].astype(F32)).astype(o_ref.dtype)

    return pl.pallas_call(
        body, name=name, out_shape=S((4, rows, cols), BF16),
        grid_spec=pltpu.PrefetchScalarGridSpec(
            num_scalar_prefetch=1, grid=(4, rows // tr),
            in_specs=[pl.BlockSpec((None, tr, cols), lambda k, i, c: (2 * k + c[0], i, 0)),
                      pl.BlockSpec((None, tr, cols), lambda k, i, c: (k, i, 0))],
            out_specs=pl.BlockSpec((None, tr, cols), lambda k, i, c: (k, i, 0))),
        compiler_params=_cp(("parallel", "parallel")))(core, mine, theirs)


def _run_comm(comm, name):
    ci, co = len(comm.ins), len(comm.out_shapes)

    def body(*refs):
        comm.first(refs[:ci], refs[ci:ci + co], refs[ci + co:])
        comm.last(refs[:ci], refs[ci:ci + co], refs[ci + co:])

    return pl.pallas_call(body, name=name, out_shape=list(comm.out_shapes), in_specs=[_HBM] * ci,
                          out_specs=[_HBM] * co, scratch_shapes=list(comm.scratch))(*comm.ins)


def _flat_rows(n_elems, mult):
    rows = -(-n_elems // LANE)
    return -(-rows // mult) * mult


def _pack(arrs, lead, mult, dtype):
    lead_shape = arrs[0].shape[:lead]
    flat = jnp.concatenate([a.astype(dtype).reshape(lead_shape + (-1,)) for a in arrs], axis=-1)
    n = flat.shape[-1]
    rows = _flat_rows(n, mult)
    flat = jnp.pad(flat, [(0, 0)] * lead + [(0, rows * LANE - n)])
    return flat.reshape(lead_shape + (rows, LANE))


def _unpack(flat, lead, shapes):
    lead_shape = flat.shape[:lead]
    flat = flat.reshape(lead_shape + (-1,))
    out, off = [], 0
    for shp in shapes:
        n = math.prod(shp)
        out.append(flat[..., off:off + n].reshape(lead_shape + tuple(shp)))
        off += n
    return out


def _split8(full, ax, n):
    shp = full.shape
    return jnp.moveaxis(full.reshape(shp[:ax] + (N_DEV, n) + shp[ax + 1:]), ax, 0)


def _join8(g, ax):
    shp = g.shape[1:]
    return jnp.moveaxis(g, 0, ax).reshape(shp[:ax] + (N_DEV * shp[ax],) + shp[ax + 1:])


def _group_lanes(v, hg):
    v = v.reshape(SSM_GROUPS, hg)
    return jnp.pad(v, ((0, 0), (0, LANE - hg))).reshape(1, SSM_GROUPS * LANE)


def _ungroup_lanes(v, hg):
    return v.reshape(SSM_GROUPS, LANE)[:, :hg].reshape(1, SSM_GROUPS * hg)


def kernel(x, meta_tokens, a_norm_pre, a_w_in, a_conv_w, a_conv_b, a_dt_bias, a_a_log, a_d_skip, a_gate_norm, a_w_out, a_norm_post, kv_norm, w_kv, b_norm_pre, b_w_q, b_sinks, b_w_o, b_norm_post, f_norm_pre, f_w_up, f_conv_w, f_conv_b, f_w_down, f_norm_post, loss_target, m_meta_tokens, m_a_norm_pre, m_a_w_in, m_a_conv_w, m_a_conv_b, m_a_dt_bias, m_a_a_log, m_a_d_skip, m_a_gate_norm, m_a_w_out, m_a_norm_post, m_kv_norm, m_w_kv, m_b_norm_pre, m_b_w_q, m_b_sinks, m_b_w_o, m_b_norm_post, m_f_norm_pre, m_f_w_up, m_f_conv_w, m_f_conv_b, m_f_w_down, m_f_norm_post, v_meta_tokens, v_a_norm_pre, v_a_w_in, v_a_conv_w, v_a_conv_b, v_a_dt_bias, v_a_a_log, v_a_d_skip, v_a_gate_norm, v_a_w_out, v_a_norm_post, v_kv_norm, v_w_kv, v_b_norm_pre, v_b_w_q, v_b_sinks, v_b_w_o, v_b_norm_post, v_f_norm_pre, v_f_w_up, v_f_conv_w, v_f_conv_b, v_f_w_down, v_f_norm_post):
    args = locals()
    wts = {n: args[n] for n in WEIGHTS}
    mom = {n: args["m_" + n] for n in WEIGHTS}
    var = {n: args["v_" + n] for n in WEIGHTS}
    mx, my, mc = lax.axis_index("x"), lax.axis_index("y"), lax.axis_index("c")
    me = 4 * mx + 2 * my + mc
    rows = _seq_rows()
    hg = SSM_HEADS // SSM_GROUPS
    d = D_MODEL

    n_main = D_INNER + D_XBC
    ws_in, ws_up = a_w_in.shape[2], f_w_up.shape[2]
    segs_in = _col_segments(ws_in, [(0, 0, n_main)] + [(n_main + hg * g, n_main + LANE * g, hg)
                                                      for g in range(SSM_GROUPS)])
    segs_up = _col_segments(ws_up, [(0, 0, 2 * D_FF)])
    def gather_of(*ws):
        return _gather_comm([w.astype(BF16) for w in ws])

    small_full, = _run_comm(_gather_comm([_pack([wts[n] for n in SMALL], 0, 8, F32)]), "gather_small")
    full = {}
    for n, g in zip(SMALL, _unpack(small_full, 1, [wts[n].shape for n in SMALL])):
        full[n] = _join8(g, SHARD_AXIS[n])
    (h0, hn0), (g_in,) = _embed_norm(full["meta_tokens"], x[0], full["a_norm_pre"], rows, "embed_norm",
                                     comm=gather_of(a_w_in[0]))
    w_in_all = _assemble_cols([g_in], n_main + SSM_GROUPS * LANE, segs_in, "asm_w_in")
    w_up, w_down = [None, None], [None, None]
    bias_g = _group_lanes(wts["a_dt_bias"], hg)
    alog_g = _group_lanes(wts["a_a_log"], hg)
    dsk_g = _group_lanes(wts["a_d_skip"], hg)
    a_conv_w, a_conv_b = full["a_conv_w"][0], full["a_conv_b"]
    f_cw, f_cb = full["f_conv_w"], wts["f_conv_b"]
    fpre, fpost = wts["f_norm_pre"], wts["f_norm_post"]

    tgt = jnp.pad(loss_target[0], ((N_META, rows - N_META - SEQ), (0, 0)))

    zx, (g_out,) = _mm(hn0, w_in_all, "nn", F32, "mm_in", comm=gather_of(a_w_out[0]))
    w_out = g_out.reshape(D_INNER, d)
    xbc, (g_dn1,) = _conv_silu_fwd(zx, a_conv_w, a_conv_b, "conv_a", comm=gather_of(f_w_down[1]))
    (y_ssd, hst), (g_up0,) = _ssd_fwd(xbc, zx, bias_g, alog_g, dsk_g, "ssd_fwd", comm=gather_of(f_w_up[0]))
    w_up[0] = _assemble_cols([g_up0], 2 * D_FF, segs_up, "asm_w_up0")
    yn, (g_kv, g_q) = _gatenorm_fwd(y_ssd, zx, full["a_gate_norm"], "gatenorm", comm=gather_of(w_kv, b_w_q[0]))
    mix_a, (g_o,) = _mm(yn, w_out, "nn", F32, "mm_out", comm=gather_of(b_w_o[0]))
    w_kvf, w_q, w_o = g_kv.reshape(d, 2 * D_KV), g_q.reshape(d, d), g_o.reshape(d, d)
    h1, (fn0,) = _resid_norm(h0, mix_a, full["a_norm_post"], [fpre[0:1]], "resid_a")

    half = d // 2
    u0, (g_dn0,) = _mm(fn0, w_up[0], "nn", F32, "mm_up0", comm=gather_of(f_w_down[0]))
    w_down = [g_dn0.reshape(D_FF, d), g_dn1.reshape(D_FF, d)]
    act0, (g_up1a,) = _ffn_act_fwd(u0, f_cw[0], f_cb[0:1], "ffn_act0", comm=gather_of(f_w_up[1, :half]))
    ffn0 = _mm(act0, w_down[0], "nn", F32, "mm_down0")
    h2, (kvn, bn) = _resid_norm(h1, ffn0, fpost[0:1], [wts["kv_norm"].reshape(1, d), wts["b_norm_pre"]], "resid_f0")
    kv = _mm(kvn, w_kvf, "nn", F32, "mm_kv")
    q = _mm(bn, w_q, "nn", F32, "mm_q")
    (o, lse), (g_up1b,) = _attn_fwd(q, kv, wts["b_sinks"], "attn_fwd", comm=gather_of(f_w_up[1, half:]))
    w_up[1] = _assemble_cols([g_up1a, g_up1b], 2 * D_FF, segs_up, "asm_w_up1")
    mix_b = _mm(o, w_o, "nn", F32, "mm_o")
    h3, (fn1,) = _resid_norm(h2, mix_b, wts["b_norm_post"], [fpre[1:2]], "resid_b")
    u1 = _mm(fn1, w_up[1], "nn", F32, "mm_up1")
    act1 = _ffn_act_fwd(u1, f_cw[1], f_cb[1:2], "ffn_act1")
    ffn1 = _mm(act1, w_down[1], "nn", F32, "mm_down1")
    dh4, loss_row, dffn1, dw_post1 = _final_loss(h3, ffn1, fpost[1:2], tgt, "loss")
    loss = lax.psum(loss_row[0, 0], ("x", "y", "c"))

    grads = {}

    core = mc.astype(jnp.int32).reshape(1)

    def carried(res, comm):
        return res if comm is not None else (res, None)

    def ffn_bwd(dh_out, dffn, h_in, fn, u, act, i, then, c_dact=None, c_dwdown=None, c_dwup=None):
        dact, got_a = carried(_mm(dffn, w_down[i], "nt", F32, f"mm_dact{i}", comm=c_dact), c_dact)
        dw_down, got_b = carried(_mm(act, dffn, "tn", BF16, f"mm_dwdown{i}", comm=c_dwdown), c_dwdown)
        dw_down = dw_down.reshape(N_DEV, -1, d)
        du, dwc, dbc = _ffn_act_bwd(u, dact, f_cw[i], f_cb[i:i + 1], f"ffn_act_bwd{i}")
        dfn, (s_dn,) = _mm(du, w_up[i], "nt", F32, f"mm_dfn{i}", comm=_swap_comm([dw_down]))
        sum_dn = _add_pairs(dw_down, s_dn, core, f"rs_add_dn{i}")
        dw_up, got_c = carried(_mm(fn, du, "tn", BF16, f"mm_dwup{i}", comm=c_dwup, shard_cols=ws_up), c_dwup)
        (dh_in, dw_pre, dbranch, dw_branch), (s_up,) = _norm_bwd(
            h_in, fpre[i:i + 1], dfn, dh_out, F32, f"nb_fpre{i}", comm=_swap_comm([dw_up]), then=then)
        sum_up = _add_pairs(dw_up, s_up, core, f"rs_add_up{i}")
        return dh_in, dbranch, dw_branch, dict(sum_down=sum_dn, cw=jnp.concatenate([dwc[0], dwc[1]], axis=1),
                                               cb=jnp.concatenate([dbc[0], dbc[1]], axis=1), sum_up=sum_up,
                                               pre=dw_pre), got_a, got_b, got_c

    dh3, dmix_b, grads["b_norm_post"], gf1, _, _, _ = ffn_bwd(dh4, dffn1, h3, fn1, u1, act1, 1,
                                                              (mix_b, wts["b_norm_post"]))
    do = _mm(dmix_b, w_o, "nt", F32, "mm_do")
    dw_o = _mm(o, dmix_b, "tn", BF16, "mm_dwo").reshape(N_DEV, -1, d)
    (dq, dkv, dsinks), (p_up1, s_o) = _attn_bwd(
        q, kv, wts["b_sinks"], do, lse, "attn_bwd",
        comm=_join_comms([_chips_comm([gf1["sum_up"]]), _swap_comm([dw_o])]))
    sum_o = _add_pairs(dw_o, s_o, core, "rs_add_o")
    grads["b_sinks"] = dsinks[:, :N_Q_HEADS]
    dbn = _mm(dq, w_q, "nt", F32, "mm_dbn")
    dw_q = _mm(bn, dq, "tn", BF16, "mm_dwq").reshape(N_DEV, -1, d)
    dkv16 = dkv.astype(BF16)
    dkvn = _mm(dkv16, w_kvf, "nt", F32, "mm_dkvn")
    dw_kv = _mm(kvn, dkv16, "tn", BF16, "mm_dwkv").reshape(N_DEV, -1, 2 * D_KV)
    (dh2, grads["b_norm_pre"]), (s_q, s_kv) = _norm_bwd(h2, wts["b_norm_pre"], dbn, dh3, F32, "nb_bpre",
                                                        comm=_swap_comm([dw_q, dw_kv]))
    sum_q, sum_kv = _add_pairs(dw_q, s_q, core, "rs_add_q"), _add_pairs(dw_kv, s_kv, core, "rs_add_kv")
    dh2, dw_kvn, dffn0, dw_post0 = _norm_bwd(h2, wts["kv_norm"].reshape(1, d), dkvn, dh2, F32, "nb_kv",
                                             then=(ffn0, fpost[0:1]))
    grads["kv_norm"] = dw_kvn.reshape(d)
    dh1, dmix_a, grads["a_norm_post"], gf0, (p_o,), (p_q, p_kv), (p_dn1,) = ffn_bwd(
        dh2, dffn0, h1, fn0, u0, act0, 0, (mix_a, full["a_norm_post"]), c_dact=_chips_comm([sum_o]),
        c_dwdown=_chips_comm([sum_q, sum_kv]), c_dwup=_chips_comm([gf1["sum_down"]]))
    grads["f_norm_post"] = jnp.concatenate([dw_post0, dw_post1], axis=0)
    grads["f_norm_pre"] = jnp.concatenate([gf0["pre"], gf1["pre"]], axis=0)
    grads["f_conv_w"] = jnp.stack([gf0["cw"], gf1["cw"]])
    grads["f_conv_b"] = jnp.concatenate([gf0["cb"], gf1["cb"]], axis=0)

    dyn = _mm(dmix_a, w_out, "nt", F32, "mm_dyn")
    dw_out = _mm(yn, dmix_a, "tn", BF16, "mm_dwout").reshape(N_DEV, -1, d)
    (dy_ssd, dz, grads["a_gate_norm"]), (s_out,) = _gatenorm_bwd(y_ssd, zx, full["a_gate_norm"], dyn, "gatenorm_bwd",
                                                                 comm=_swap_comm([dw_out]))
    sum_out = _add_pairs(dw_out, s_out, core, "rs_add_out")
    (dxs, dbm, dcm, ddtp, dalog, ddsk, dbias), (p_up0,) = _ssd_bwd(
        xbc, zx, bias_g, alog_g, dsk_g, dy_ssd, hst, "ssd_bwd", comm=_chips_comm([gf0["sum_up"]]))
    grads["a_a_log"] = _ungroup_lanes(dalog, hg)
    grads["a_d_skip"] = _ungroup_lanes(ddsk, hg)
    grads["a_dt_bias"] = _ungroup_lanes(dbias, hg)
    (dpre, dcw, dcb), (p_dn0,) = _conv_silu_bwd(zx, dxs, dbm, dcm, a_conv_w, a_conv_b, "conv_a_bwd",
                                                comm=_chips_comm([gf0["sum_down"]]))
    grads["a_conv_w"], grads["a_conv_b"] = dcw[None], dcb
    dzx = jnp.concatenate([dz, dpre, ddtp.astype(BF16)], axis=1)
    dw_in_all, (p_out,) = _mm(hn0, dzx, "tn", BF16, "mm_dwin", comm=_chips_comm([sum_out]))
    dw_in8 = _scatter_cols(dw_in_all, ws_in, segs_in, "scat_w_in")
    dhn0, (s_in,) = _mm(dzx, w_in_all, "nt", F32, "mm_dhn0", comm=_swap_comm([dw_in8]))
    sum_in = _add_pairs(dw_in8, s_in, core, "rs_add_in")
    half_in = sum_in.shape[1] // 2
    (dh0, grads["a_norm_pre"]), (p_in_a,) = _norm_bwd(h0, full["a_norm_pre"], dhn0, dh1, F32, "nb_apre",
                                                      comm=_chips_comm([sum_in[:, :half_in]]))
    grad_x = dh0[N_META:N_META + SEQ][None]
    grads["meta_tokens"] = dh0[:N_META]

    small_local = _pack([_split8(grads[n], SHARD_AXIS[n], wts[n].shape[SHARD_AXIS[n]]) for n in SMALL], 1, 8, F32)
    repl_local = _pack([grads[n] for n in REPL], 0, 8, F32)
    n_sr = small_local.shape[1]
    small_vec = jnp.concatenate([small_local.reshape(N_DEV * n_sr, LANE), repl_local], axis=0)
    tail = _join_comms([_chips_comm([sum_in[:, half_in:]]), _gather_comm([small_vec])])
    parts_big = dict(a_w_out=[p_out], w_kv=[p_kv], b_w_q=[p_q], b_w_o=[p_o], f_w_down=[p_dn0, p_dn1])

    def flat_f32(dct, names, mult):
        return _pack([dct[n] for n in names], 0, mult, F32)

    def adamw_big(n, comm=None):
        shp3 = (len(parts_big[n]),) + parts_big[n][0].shape[1:]
        res = _adamw(parts_big[n], *[dct[n].reshape(shp3) for dct in (wts, mom, var)], f"adamw_{n}", comm=comm)
        res, got = res if comm is not None else (res, None)
        big_out[n] = [r.reshape(wts[n].shape) for r in res]
        return got

    big_out = {}
    def swap_last(a):
        return jnp.swapaxes(a, -1, -2)

    g_up_t = swap_last(_sum_parts([p_up0, p_up1], "sum_w_up").reshape(f_w_up.shape))
    res, (p_in_b, small_all) = _adamw([g_up_t[0:1], g_up_t[1:2]], *[swap_last(dct["f_w_up"]) for dct in (wts, mom, var)],
                                      "adamw_f_w_up", comm=tail)
    big_out["f_w_up"] = [swap_last(r) for r in res]
    for n in BIG:
        if n not in ("f_w_up", "a_w_in"):
            adamw_big(n)
    g_in_t = swap_last(_sum_parts([p_in_a, p_in_b], "sum_w_in"))[None]
    res = _adamw([g_in_t], *[swap_last(dct["a_w_in"]) for dct in (wts, mom, var)], "adamw_a_w_in")
    big_out["a_w_in"] = [swap_last(r) for r in res]
    mine_small = lax.dynamic_slice_in_dim(small_all, me * n_sr, n_sr, axis=1)
    parts_small = jnp.concatenate([mine_small, small_all[:, N_DEV * n_sr:]], axis=1)
    sm_in = [jnp.concatenate([flat_f32(dct, SMALL, 8), flat_f32(dct, REPL, 8)], axis=0)[None] for dct in (wts, mom, var)]
    small_out = [r[0] for r in _adamw([parts_small], *sm_in, "adamw_small")]

    outs = []
    for kind in range(4):
        res = {n: big_out[n][kind] for n in BIG}
        for n, a in zip(SMALL, _unpack(small_out[kind][:n_sr], 0, [wts[n].shape for n in SMALL])):
            res[n] = a
        for n, a in zip(REPL, _unpack(small_out[kind][n_sr:], 0, [wts[n].shape for n in REPL])):
            res[n] = a
        outs.append(res)
    return (loss, grad_x, *[outs[0][n] for n in WEIGHTS], *[outs[1][n] for n in WEIGHTS],
            *[outs[2][n] for n in WEIGHTS], *[outs[3][n] for n in WEIGHTS])
```

```python
import functools
import math

import jax
import jax.numpy as jnp
from jax import lax
from jax.experimental import pallas as pl
from jax.experimental.pallas import tpu as pltpu

F32, BF16 = jnp.float32, jnp.bfloat16
S = jax.ShapeDtypeStruct

D_MODEL = 1024
SEQ = 2048
N_META = 16
D_INNER = 2048
HEAD_P = 64
SSM_HEADS = D_INNER // HEAD_P
SSM_GROUPS = 4
D_STATE = 128
SSM_CONV = 4
D_BC = SSM_GROUPS * D_STATE
D_XBC = D_INNER + 2 * D_BC
ATTN_DH = 64
N_Q_HEADS = D_MODEL // ATTN_DH
N_KV_HEADS = 4
D_KV = N_KV_HEADS * ATTN_DH
WINDOW = 128
D_FF = 2816
FFN_CONV = 3
RMS_EPS = 1e-6
NEG = -1e30
LR, B1, B2, EPS, WD, STEP = 0.001, 0.9, 0.999, 1e-08, 0.01, 10

N_DEV = 8
T = 128
LANE = 128
VMEM_LIMIT = 48 * 1024 * 1024

BIG = ("a_w_in", "a_w_out", "w_kv", "b_w_q", "b_w_o", "f_w_up", "f_w_down")
SMALL = ("meta_tokens", "a_norm_pre", "a_conv_w", "a_conv_b", "a_gate_norm", "a_norm_post", "f_conv_w")
REPL = ("a_dt_bias", "a_a_log", "a_d_skip", "kv_norm", "b_norm_pre", "b_sinks", "b_norm_post",
        "f_norm_pre", "f_conv_b", "f_norm_post")
SHARD_AXIS = dict(a_w_in=2, a_w_out=1, w_kv=0, b_w_q=1, b_w_o=1, f_w_up=2, f_w_down=1, meta_tokens=1,
                  a_norm_pre=1, a_conv_w=2, a_conv_b=1, a_gate_norm=1, a_norm_post=1, f_conv_w=2)
WEIGHTS = ("meta_tokens", "a_norm_pre", "a_w_in", "a_conv_w", "a_conv_b", "a_dt_bias", "a_a_log", "a_d_skip",
           "a_gate_norm", "a_w_out", "a_norm_post", "kv_norm", "w_kv", "b_norm_pre", "b_w_q", "b_sinks", "b_w_o",
           "b_norm_post", "f_norm_pre", "f_w_up", "f_conv_w", "f_conv_b", "f_w_down", "f_norm_post")


def _seq_rows():
    return -(-(N_META + SEQ) // T) * T


def _cp(sem=None):
    return pltpu.CompilerParams(dimension_semantics=sem, vmem_limit_bytes=VMEM_LIMIT)


def _pick(n, target):
    t = min(n, target)
    t -= t % LANE
    while n % t:
        t -= LANE
    return t


def _sigmoid(x):
    return 0.5 * jnp.tanh(0.5 * x) + 0.5


def _softplus(x):
    return jnp.maximum(x, 0.0) + jnp.log(1.0 + jnp.exp(-jnp.abs(x)))


_NN = (((1,), (0,)), ((), ()))
_NT = (((1,), (1,)), ((), ()))
_TN = (((0,), (0,)), ((), ()))


def _dot(a, b, dims=_NN):
    return lax.dot_general(a, b, dims, preferred_element_type=F32)


def _dot_hi(a, b):
    return lax.dot_general(a, b, _NN, precision=lax.Precision.HIGHEST, preferred_element_type=F32)


_HBM = pl.BlockSpec(memory_space=pltpu.HBM)
_MESH = pl.DeviceIdType.MESH


class _Comm:
    def __init__(self, ins, out_shapes, scratch, first, last):
        self.ins, self.out_shapes, self.scratch, self.first, self.last = ins, out_shapes, scratch, first, last


def _call(body, name, out_shape, grid, in_specs, out_specs, sem, args, scratch=(), comm=None):
    if comm is None:
        return pl.pallas_call(body, name=name, out_shape=out_shape, grid=grid, in_specs=in_specs, out_specs=out_specs,
                              scratch_shapes=list(scratch), compiler_params=_cp(sem))(*args)
    single = not isinstance(out_shape, (list, tuple))
    outs = [out_shape] if single else list(out_shape)
    ospecs = [out_specs] if single else list(out_specs)
    n_in, n_out, n_scr, ci, co = len(in_specs), len(outs), len(scratch), len(comm.ins), len(comm.out_shapes)

    def carrier(*refs):
        p = 0
        parts = []
        for cnt in (n_in, ci, n_out, co, n_scr, len(comm.scratch)):
            parts.append(refs[p:p + cnt])
            p += cnt
        ins, cins, outs_r, couts, scr, cscr = parts
        ids = [pl.program_id(i) for i in range(len(grid))]
        first, last = ids[0] == 0, ids[0] == grid[0] - 1
        for i in range(1, len(grid)):
            first, last = first & (ids[i] == 0), last & (ids[i] == grid[i] - 1)

        @pl.when(first)
        def _():
            comm.first(cins, couts, cscr)

        body(*ins, *outs_r, *scr)

        @pl.when(last)
        def _():
            comm.last(cins, couts, cscr)

    res = pl.pallas_call(
        carrier, name=name, out_shape=outs + list(comm.out_shapes), grid=grid,
        in_specs=list(in_specs) + [_HBM] * ci, out_specs=ospecs + [_HBM] * co,
        scratch_shapes=list(scratch) + list(comm.scratch),
        compiler_params=_cp(("arbitrary",) * len(grid)))(*args, *comm.ins)
    mine = res[0] if single else list(res[:n_out])
    return mine, list(res[n_out:])


def _mm(a, b, mode, out_dtype, name, comm=None, shard_cols=None):
    if mode == "tn":
        m, kk = a.shape
        planes, width = (b.shape[0], b.shape[2]) if b.ndim == 3 else (1, b.shape[1])
        n = planes * width
        tko, tn = _pick(kk, 512), _pick(width, 1536)
        per = width // tn
        b_spec = (pl.BlockSpec((None, m, tn), lambda i, j: (j // per, 0, j % per)) if b.ndim == 3
                  else pl.BlockSpec((m, tn), lambda i, j: (0, j)))
        if shard_cols is None:
            def body(a_ref, b_ref, o_ref):
                o_ref[...] = _dot(a_ref[...], b_ref[...], _TN).astype(o_ref.dtype)

            out_shape, out_spec = S((kk, n), out_dtype), pl.BlockSpec((tko, tn), lambda i, j: (i, j))
        else:
            shards = tn // shard_cols
            assert tn % shard_cols == 0

            def body(a_ref, b_ref, o_ref):
                res = _dot(a_ref[...], b_ref[...], _TN).astype(o_ref.dtype)
                for p in range(shards):
                    o_ref[p] = res[:, p * shard_cols:(p + 1) * shard_cols]

            out_shape = S((n // shard_cols, kk, shard_cols), out_dtype)
            out_spec = pl.BlockSpec((shards, tko, shard_cols), lambda i, j: (j, i, 0))
        return _call(
            body, name, out_shape, (kk // tko, n // tn), [pl.BlockSpec((m, tko), lambda i, j: (0, i)), b_spec],
            out_spec, ("parallel", "parallel"), (a, b), comm=comm)

    planes, width = (a.shape[0], a.shape[2]) if a.ndim == 3 else (1, a.shape[1])
    m, kk = a.shape[-2], planes * width
    n = b.shape[1] if mode == "nn" else b.shape[0]
    dims = _NN if mode == "nn" else _NT

    if kk > 2048:
        tm = m // 4
        assert m % 4 == 0 and tm % 16 == 0

        def body(a_ref, b_ref, o_ref):
            if a.ndim == 2:
                res = _dot(a_ref[...], b_ref[...], dims)
            else:
                res = None
                for p in range(planes):
                    bp = b_ref[p * width:(p + 1) * width, :] if mode == "nn" else b_ref[:, p * width:(p + 1) * width]
                    part = _dot(a_ref[p], bp, dims)
                    res = part if res is None else res + part
            o_ref[...] = res.astype(o_ref.dtype)

        a_spec = (pl.BlockSpec((planes, tm, width), lambda i: (0, i, 0)) if a.ndim == 3
                  else pl.BlockSpec((tm, kk), lambda i: (i, 0)))
        return _call(
            body, name, S((m, n), out_dtype), (m // tm,),
            [a_spec, pl.BlockSpec(b.shape, lambda i: (0, 0), pipeline_mode=pl.Buffered(1))],
            pl.BlockSpec((tm, n), lambda i: (i, 0)), ("parallel",), (a, b), comm=comm)

    tn = _pick(n, 512)

    def body(a_ref, b_ref, o_ref):
        o_ref[...] = _dot(a_ref[...], b_ref[...], dims).astype(o_ref.dtype)

    b_spec = (pl.BlockSpec((kk, tn), lambda j: (0, j)) if mode == "nn" else pl.BlockSpec((tn, kk), lambda j: (j, 0)))
    return _call(
        body, name, S((m, n), out_dtype), (n // tn,), [pl.BlockSpec((m, kk), lambda j: (0, 0)), b_spec],
        pl.BlockSpec((m, tn), lambda j: (0, j)), ("parallel",), (a, b), comm=comm)


def _rms(x, w):
    return x * lax.rsqrt(jnp.mean(x * x, axis=-1, keepdims=True) + RMS_EPS) * w


def _row_tile(rows):
    return rows // 8


def _embed_norm(meta, x, w, rows, name, comm=None):
    n_meta, d = meta.shape
    n_x = x.shape[0]
    last = rows // T - 1
    assert n_meta % 8 == 0 and n_meta < T and n_meta + n_x == last * T + n_meta and last * T >= n_x

    def body(m_ref, x_ref, w_ref, h_ref, hn_ref):
        i = pl.program_id(0)

        @pl.when(i == 0)
        def _():
            h_ref[0:n_meta, :] = m_ref[...]
            h_ref[n_meta:T, :] = x_ref[0:T - n_meta, :]

        @pl.when((i > 0) & (i < last))
        def _():
            h_ref[...] = x_ref[pl.ds(pl.multiple_of(i * T - n_meta, 8), T), :]

        @pl.when(i == last)
        def _():
            h_ref[0:n_meta, :] = x_ref[n_x - n_meta:n_x, :]
            h_ref[n_meta:T, :] = jnp.zeros((T - n_meta, d), F32)

        hn_ref[...] = _rms(h_ref[...], w_ref[...]).astype(hn_ref.dtype)

    row = pl.BlockSpec((T, d), lambda i: (i, 0))
    return _call(body, name, [S((rows, d), F32), S((rows, d), BF16)], (rows // T,),
                 [pl.BlockSpec((n_meta, d), lambda i: (0, 0)), pl.BlockSpec((n_x, d), lambda i: (0, 0)),
                  pl.BlockSpec((1, d), lambda i: (0, 0))], [row, row], ("parallel",), (meta, x, w), comm=comm)


def _resid_norm(h, br, w_post, next_ws, name):
    rows, d = h.shape
    tr = _row_tile(rows)
    has_br = br is not None
    nw = len(next_ws)

    def body(*refs):
        h_ref = refs[0]
        pos = 1
        x = h_ref[...]
        if has_br:
            x = x + _rms(refs[1][...], refs[2][...])
            pos = 3
        w_refs = refs[pos:pos + nw]
        outs = refs[pos + nw:]
        if has_br:
            outs[0][...] = x
            outs = outs[1:]
        for w_ref, o_ref in zip(w_refs, outs):
            o_ref[...] = _rms(x, w_ref[...]).astype(o_ref.dtype)

    row = pl.BlockSpec((tr, d), lambda i: (i, 0))
    vec = pl.BlockSpec((1, d), lambda i: (0, 0))
    ins = [h] + ([br, w_post] if has_br else []) + list(next_ws)
    in_specs = [row] + ([row, vec] if has_br else []) + [vec] * nw
    out_shape = ([S((rows, d), F32)] if has_br else []) + [S((rows, d), BF16)] * nw
    res = pl.pallas_call(body, name=name, out_shape=out_shape, grid=(rows // tr,), in_specs=in_specs,
                         out_specs=[row] * len(out_shape), compiler_params=_cp(("parallel",)))(*ins)
    if has_br:
        return res[0], list(res[1:])
    return h, list(res)


def _rms_bwd(xv, w, dyv):
    r = lax.rsqrt(jnp.mean(xv * xv, axis=-1, keepdims=True) + RMS_EPS)
    wdy = dyv * w
    dx = r * wdy - xv * (r * r * r) * jnp.mean(xv * wdy, axis=-1, keepdims=True)
    return dx, jnp.sum(dyv * xv * r, axis=0, keepdims=True)


def _norm_bwd(x, w, dy, add, out_dtype, name, comm=None, then=None):
    rows, d = x.shape
    tr = _row_tile(rows)
    has_add = add is not None
    n_in = 3 + has_add + (2 if then is not None else 0)

    def body(*refs):
        x_ref, w_ref, dy_ref = refs[:3]
        outs = refs[n_in:]
        dx, dw = _rms_bwd(x_ref[...], w_ref[...], dy_ref[...].astype(F32))
        if has_add:
            dx = dx + refs[3][...]
        outs[0][...] = dx.astype(outs[0].dtype)
        first = pl.program_id(0) == 0

        @pl.when(first)
        def _():
            outs[1][...] = jnp.zeros_like(outs[1])

        outs[1][...] += dw
        if then is not None:
            dx2, dw2 = _rms_bwd(refs[n_in - 2][...], refs[n_in - 1][...], dx)
            outs[2][...] = dx2.astype(outs[2].dtype)

            @pl.when(first)
            def _():
                outs[3][...] = jnp.zeros_like(outs[3])

            outs[3][...] += dw2

    row = pl.BlockSpec((tr, d), lambda i: (i, 0))
    vec = pl.BlockSpec((1, d), lambda i: (0, 0))
    ins = [x, w, dy] + ([add] if has_add else []) + (list(then) if then is not None else [])
    in_specs = [row, vec, row] + ([row] if has_add else []) + ([row, vec] if then is not None else [])
    out_shape = [S((rows, d), out_dtype), S((1, d), F32)] + ([S((rows, d), BF16), S((1, d), F32)] if then is not None else [])
    return _call(body, name, out_shape, (rows // tr,), in_specs, [row, vec] * (len(out_shape) // 2), ("arbitrary",),
                 ins, comm=comm)


def _final_loss(h, br, w_post, tgt, name):
    rows, d = h.shape
    tr = _row_tile(rows)

    def body(h_ref, br_ref, w_ref, t_ref, dh_ref, loss_ref, dbr_ref, dw_ref):
        i = pl.program_id(0)
        brv, wv = br_ref[...], w_ref[...]
        y = h_ref[...] + _rms(brv, wv)
        r = i * tr + lax.broadcasted_iota(jnp.int32, (tr, 1), 0)
        real = (r >= N_META) & (r < N_META + SEQ)
        diff = jnp.where(real, y - t_ref[...], 0.0)
        dh = diff * (1.0 / d)
        dh_ref[...] = dh
        dbr, dw = _rms_bwd(brv, wv, dh)
        dbr_ref[...] = dbr.astype(dbr_ref.dtype)

        @pl.when(i == 0)
        def _():
            loss_ref[...] = jnp.zeros_like(loss_ref)
            dw_ref[...] = jnp.zeros_like(dw_ref)

        loss_ref[...] += jnp.sum(diff * diff) * (0.5 / d)
        dw_ref[...] += dw

    row = pl.BlockSpec((tr, d), lambda i: (i, 0))
    vec = pl.BlockSpec((1, d), lambda i: (0, 0))
    return pl.pallas_call(body, name=name,
                          out_shape=[S((rows, d), F32), S((1, LANE), F32), S((rows, d), BF16), S((1, d), F32)],
                          grid=(rows // tr,), in_specs=[row, row, vec, row],
                          out_specs=[row, pl.BlockSpec((1, LANE), lambda i: (0, 0)), row, vec],
                          compiler_params=_cp(("arbitrary",)))(h, br, w_post, tgt)


def _gatenorm_fwd(y, zx, w, name, comm=None):
    rows, d = y.shape
    tr = _row_tile(rows)

    def body(y_ref, z_ref, w_ref, o_ref):
        z = z_ref[...]
        o_ref[...] = _rms(y_ref[...] * z * _sigmoid(z), w_ref[...]).astype(o_ref.dtype)

    row = pl.BlockSpec((tr, d), lambda i: (i, 0))
    return _call(body, name, S((rows, d), BF16), (rows // tr,), [row, row, pl.BlockSpec((1, d), lambda i: (0, 0))],
                 row, ("parallel",), (y, zx, w), comm=comm)


def _gatenorm_bwd(y, zx, w, dyn, name, comm=None):
    rows, d = y.shape
    tr = _row_tile(rows)

    def body(y_ref, z_ref, w_ref, dyn_ref, dy_ref, dz_ref, dw_ref):
        yv, z = y_ref[...], z_ref[...]
        sg = _sigmoid(z)
        sz = z * sg
        g = yv * sz
        r = lax.rsqrt(jnp.mean(g * g, axis=-1, keepdims=True) + RMS_EPS)
        dyn_v = dyn_ref[...]
        wdy = dyn_v * w_ref[...]
        dg = r * wdy - g * (r * r * r) * jnp.mean(g * wdy, axis=-1, keepdims=True)
        dy_ref[...] = dg * sz
        dz_ref[...] = (dg * yv * sg * (1.0 + z * (1.0 - sg))).astype(dz_ref.dtype)

        @pl.when(pl.program_id(0) == 0)
        def _():
            dw_ref[...] = jnp.zeros_like(dw_ref)

        dw_ref[...] += jnp.sum(dyn_v * g * r, axis=0, keepdims=True)

    row = pl.BlockSpec((tr, d), lambda i: (i, 0))
    vec = pl.BlockSpec((1, d), lambda i: (0, 0))
    return _call(body, name, [S((rows, d), F32), S((rows, d), BF16), S((1, d), F32)], (rows // tr,),
                 [row, row, vec, row], [row, row, vec], ("arbitrary",), (y, zx, w, dyn), comm=comm)


def _shift_down(x, s, rows_iota):
    if s == 0:
        return x
    return jnp.where(rows_iota >= s, pltpu.roll(x, s, 0), 0.0)


def _shift_up(x, s, rows_iota):
    if s == 0:
        return x
    rows = x.shape[0]
    return jnp.where(rows_iota < rows - s, pltpu.roll(x, rows - s, 0), 0.0)


def _r16(v):
    return v.astype(BF16).astype(F32)


def _conv_taps(x, taps, rows_iota):
    x = _r16(x)
    return [_shift_down(x, taps - 1 - k, rows_iota) for k in range(taps)]


def _conv(x, w_ref, b_ref, taps, rows_iota, shifted=None):
    shifted = _conv_taps(x, taps, rows_iota) if shifted is None else shifted
    acc = jnp.zeros_like(shifted[0])
    for k in range(taps):
        acc = acc + _r16(w_ref[k:k + 1, :]) * shifted[k]
    return acc + b_ref[...]


def _conv_bwd(shifted, du, w_ref, dw_ref, db_ref, taps, rows_iota):
    db_ref[...] = jnp.sum(du, axis=0, keepdims=True)
    du = _r16(du)
    dx = jnp.zeros_like(du)
    for k in range(taps):
        dx = dx + _r16(w_ref[k:k + 1, :]) * _shift_up(du, taps - 1 - k, rows_iota)
        dw_ref[k:k + 1, :] = jnp.sum(du * shifted[k], axis=0, keepdims=True)
    return dx


def _conv_silu_fwd(zx, w, b, name, comm=None):
    rows = zx.shape[0]
    cb = 512
    off = D_INNER // cb

    def body(x_ref, w_ref, b_ref, o_ref):
        it = lax.broadcasted_iota(jnp.int32, (rows, 1), 0)
        u = _conv(x_ref[...], w_ref, b_ref, SSM_CONV, it)
        o_ref[...] = u * _sigmoid(u)

    return _call(
        body, name, S((rows, D_XBC), F32), (D_XBC // cb,),
        [pl.BlockSpec((rows, cb), lambda j: (0, off + j)), pl.BlockSpec((SSM_CONV, cb), lambda j: (0, j)),
         pl.BlockSpec((1, cb), lambda j: (0, j))],
        pl.BlockSpec((rows, cb), lambda j: (0, j)), ("parallel",), (zx, w, b), comm=comm)


def _conv_silu_bwd(zx, dxs, dbm, dcm, w, b, name, comm=None):
    rows = zx.shape[0]
    cb = 256
    off = D_INNER // cb
    nx, nbc = D_INNER // cb, D_BC // cb

    def body(x_ref, dx_in, db_in, dc_in, w_ref, b_ref, dx_ref, dw_ref, db_ref, dbuf):
        j = pl.program_id(0)
        for cond, src in ((j < nx, dx_in), ((j >= nx) & (j < nx + nbc), db_in), (j >= nx + nbc, dc_in)):
            @pl.when(cond)
            def _(src=src):
                dbuf[...] = src[...]
        it = lax.broadcasted_iota(jnp.int32, (rows, 1), 0)
        xs = _conv_taps(x_ref[...], SSM_CONV, it)
        u = _conv(None, w_ref, b_ref, SSM_CONV, it, xs)
        sg = _sigmoid(u)
        du = dbuf[...] * sg * (1.0 + u * (1.0 - sg))
        dx_ref[...] = _conv_bwd(xs, du, w_ref, dw_ref, db_ref, SSM_CONV, it).astype(dx_ref.dtype)

    def part(first, count):
        return pl.BlockSpec((rows, cb), lambda j: (0, jnp.clip(j - first, 0, count - 1)))

    col = pl.BlockSpec((rows, cb), lambda j: (0, j))
    wsp = pl.BlockSpec((SSM_CONV, cb), lambda j: (0, j))
    bsp = pl.BlockSpec((1, cb), lambda j: (0, j))
    return _call(
        body, name, [S((rows, D_XBC), BF16), S((SSM_CONV, D_XBC), F32), S((1, D_XBC), F32)], (D_XBC // cb,),
        [pl.BlockSpec((rows, cb), lambda j: (0, off + j)), part(0, nx), part(nx, nbc), part(nx + nbc, nbc), wsp, bsp],
        [col, wsp, bsp], ("arbitrary",), (zx, dxs, dbm, dcm, w, b), scratch=[pltpu.VMEM((rows, cb), F32)], comm=comm)


def _ffn_act_fwd(u, w, b, name, comm=None):
    rows = u.shape[0]
    cb = 256
    nb = D_FF // cb

    def body(g_ref, v_ref, wg_ref, wv_ref, bg_ref, bv_ref, o_ref):
        it = lax.broadcasted_iota(jnp.int32, (rows, 1), 0)
        g = _conv(g_ref[...], wg_ref, bg_ref, FFN_CONV, it)
        v = _conv(v_ref[...], wv_ref, bv_ref, FFN_CONV, it)
        o_ref[...] = (g * _sigmoid(g) * v).astype(o_ref.dtype)

    def sp(r, shift):
        return pl.BlockSpec((r, cb), lambda j: (0, shift + j))

    return _call(
        body, name, S((rows, D_FF), BF16), (nb,),
        [sp(rows, 0), sp(rows, nb), sp(FFN_CONV, 0), sp(FFN_CONV, nb), sp(1, 0), sp(1, nb)],
        sp(rows, 0), ("parallel",), (u, u, w, w, b, b), comm=comm)


def _ffn_act_bwd(u, dact, w, b, name, comm=None):
    rows = u.shape[0]
    cb = 256
    nb = D_FF // cb

    def body(g_ref, v_ref, d_ref, wg_ref, wv_ref, bg_ref, bv_ref, du_ref, dw_ref, db_ref):
        it = lax.broadcasted_iota(jnp.int32, (rows, 1), 0)
        xg, xv = _conv_taps(g_ref[...], FFN_CONV, it), _conv_taps(v_ref[...], FFN_CONV, it)
        g = _conv(None, wg_ref, bg_ref, FFN_CONV, it, xg)
        v = _conv(None, wv_ref, bv_ref, FFN_CONV, it, xv)
        sg = _sigmoid(g)
        d = d_ref[...]
        dgate = d * v * sg * (1.0 + g * (1.0 - sg))
        dval = d * g * sg
        du_ref[0] = _conv_bwd(xg, dgate, wg_ref, dw_ref.at[0], db_ref.at[0], FFN_CONV, it).astype(du_ref.dtype)
        du_ref[1] = _conv_bwd(xv, dval, wv_ref, dw_ref.at[1], db_ref.at[1], FFN_CONV, it).astype(du_ref.dtype)

    def sp(r, shift):
        return pl.BlockSpec((r, cb), lambda j: (0, shift + j))

    def both(r):
        return pl.BlockSpec((2, r, cb), lambda j: (0, 0, j))

    return _call(
        body, name, [S((2, rows, D_FF), BF16), S((2, FFN_CONV, D_FF), F32), S((2, 1, D_FF), F32)], (nb,),
        [sp(rows, 0), sp(rows, nb), sp(rows, 0), sp(FFN_CONV, 0), sp(FFN_CONV, nb), sp(1, 0), sp(1, nb)],
        [both(rows), both(FFN_CONV), both(1)], ("parallel",), (u, u, dact, w, w, b, b), comm=comm)


def _ssd_consts(dtp_ref, bias_ref, alog_ref, hg):
    lane = lax.broadcasted_iota(jnp.int32, (1, LANE), 1)
    pre = dtp_ref[...] + bias_ref[...]
    dt = _softplus(pre)
    a_row = jnp.where(lane < hg, -jnp.exp(alog_ref[...]), 0.0)
    ri = lax.broadcasted_iota(jnp.int32, (T, T), 0)
    ci = lax.broadcasted_iota(jnp.int32, (T, T), 1)
    cs = _dot_hi((ri >= ci).astype(F32), dt * a_row)
    return pre, dt, a_row, cs, ri, ci, lane


def _head_rows(src, hg):
    return jnp.concatenate([jnp.broadcast_to(src[k:k + 1, :], (HEAD_P, src.shape[1])) for k in range(hg)], axis=0)


def _ssd_fwd(xbc, zx, bias, alog, dsk, name, comm=None):
    rows = xbc.shape[0]
    nc = rows // T
    hg = SSM_HEADS // SSM_GROUPS
    gw = hg * HEAD_P
    xoff, boff, coff = 0, D_INNER // D_STATE, (D_INNER + D_BC) // D_STATE
    dtoff = (D_INNER + D_XBC) // LANE

    def body(x_ref, b_ref, c_ref, dtp_ref, bias_ref, alog_ref, dsk_ref, y_ref, hst_ref, hs):
        c = pl.program_id(1)

        @pl.when(c == 0)
        def _():
            hs[...] = jnp.zeros_like(hs)

        _, dt, _, cs, ri, ci, _ = _ssd_consts(dtp_ref, bias_ref, alog_ref, hg)
        cst, dtt = cs.T, dt.T
        xt = x_ref[...].T
        bb, cbf = b_ref[...].astype(BF16), c_ref[...].astype(BF16)
        gt = _dot(bb, cbf, _NT)
        causal_t = ci >= ri
        dskv = dsk_ref[...]
        hall = hs[...]
        hst_ref[0, 0] = hall
        cs8 = cst[0:8, :]
        cl8 = cs8[:, T - 1:T]
        xdt = xt * _head_rows(dtt, hg)
        yo = _head_rows(jnp.exp(cs8), hg) * _dot(hall.astype(BF16), cbf, _NT)
        st = _dot((xdt * _head_rows(jnp.exp(cl8 - cs8), hg)).astype(BF16), bb)
        hs[...] = _head_rows(jnp.exp(cl8), hg) * hall + st
        yds = []
        for k in range(hg):
            sl = slice(k * HEAD_P, (k + 1) * HEAD_P)
            lt = jnp.exp(jnp.where(causal_t, cst[k:k + 1, :] - cs[:, k:k + 1], NEG))
            yds.append(_dot(xdt[sl, :].astype(BF16), (gt * lt).astype(BF16)))
        dsk_r = jnp.concatenate([jnp.broadcast_to(dskv[:, k:k + 1], (HEAD_P, 1)) for k in range(hg)], axis=0)
        y_ref[...] = (jnp.concatenate(yds, axis=0) + yo + dsk_r * xt).T

    vec = pl.BlockSpec((1, LANE), lambda g, c: (0, g))
    return _call(
        body, name, [S((rows, D_INNER), F32), S((nc, SSM_GROUPS, gw, D_STATE), F32)], (SSM_GROUPS, nc),
        [pl.BlockSpec((T, gw), lambda g, c: (c, xoff + g)),
         pl.BlockSpec((T, D_STATE), lambda g, c: (c, boff + g)),
         pl.BlockSpec((T, D_STATE), lambda g, c: (c, coff + g)),
         pl.BlockSpec((T, LANE), lambda g, c: (c, dtoff + g)), vec, vec, vec],
        [pl.BlockSpec((T, gw), lambda g, c: (c, g)), pl.BlockSpec((1, 1, gw, D_STATE), lambda g, c: (c, g, 0, 0))],
        ("parallel", "arbitrary"), (xbc, xbc, xbc, zx, bias, alog, dsk),
        scratch=[pltpu.VMEM((gw, D_STATE), F32)], comm=comm)


def _ssd_bwd(xbc, zx, bias, alog, dsk, dy, hst, name, comm=None):
    rows = xbc.shape[0]
    nc = rows // T
    hg = SSM_HEADS // SSM_GROUPS
    gw = hg * HEAD_P
    boff, coff = D_INNER // D_STATE, (D_INNER + D_BC) // D_STATE
    dtoff = (D_INNER + D_XBC) // LANE

    def body(x_ref, b_ref, c_ref, dtp_ref, bias_ref, alog_ref, dsk_ref, dy_ref, hst_ref,
             dx_ref, db_ref, dc_ref, ddtp_ref, dalog_ref, ddsk_ref, dbias_ref, dhs):
        step = pl.program_id(1)

        @pl.when(step == 0)
        def _():
            dhs[...] = jnp.zeros_like(dhs)
            dalog_ref[...] = jnp.zeros_like(dalog_ref)
            ddsk_ref[...] = jnp.zeros_like(ddsk_ref)
            dbias_ref[...] = jnp.zeros_like(dbias_ref)

        pre, dt, a_row, cs, ri, ci, lane = _ssd_consts(dtp_ref, bias_ref, alog_ref, hg)
        cst, dtt = cs.T, dt.T
        xt, dyt = x_ref[...].T, dy_ref[...].T
        bb, cbf = b_ref[...].astype(BF16), c_ref[...].astype(BF16)
        gt = _dot(bb, cbf, _NT)
        causal_t = ci >= ri
        dskv = dsk_ref[...]
        hall, dhall = hst_ref[0, 0], dhs[...]
        head_row = lax.broadcasted_iota(jnp.int32, (T, 1), 0)
        last_l = lax.broadcasted_iota(jnp.int32, (1, T), 1) == T - 1
        cs8, dt8 = cst[0:8, :], dtt[0:8, :]
        cl8 = cs8[:, T - 1:T]
        e8, wdec8 = jnp.exp(cs8), jnp.exp(cl8 - cs8)
        w8 = wdec8 * dt8
        dt_r, e_r, w_r, ecl_r = _head_rows(dt8, hg), _head_rows(e8, hg), _head_rows(w8, hg), _head_rows(jnp.exp(cl8), hg)
        dsk_r = jnp.concatenate([jnp.broadcast_to(dskv[:, k:k + 1], (HEAD_P, 1)) for k in range(hg)], axis=0)
        hb, dhb = hall.astype(BF16), dhall.astype(BF16)
        xdt = xt * dt_r
        dye = (dyt * e_r).astype(BF16)
        rt = _dot(dhb, bb, _NT)
        yo = e_r * _dot(hb, cbf, _NT)
        dhs[...] = ecl_r * dhall + _dot(dye, cbf)
        dc_acc = _dot(dye, hb, _TN)
        db_acc = _dot((xt * w_r).astype(BF16), dhb, _TN)
        rtx, dyyo, hdh, dyx = rt * xt, dyt * yo, dhall * hall, dyt * xt
        dgt = jnp.zeros((T, T), F32)
        ddt_rows = jnp.zeros((T, T), F32)
        dcs_rows = jnp.zeros((T, T), F32)
        qrow_cols = jnp.zeros((T, LANE), F32)
        ddsk_acc = jnp.zeros((1, LANE), F32)
        dxdts = []
        for k in range(hg):
            sl = slice(k * HEAD_P, (k + 1) * HEAD_P)
            lt = jnp.exp(jnp.where(causal_t, cst[k:k + 1, :] - cs[:, k:k + 1], NEG))
            mpt = gt * lt
            dyb = dyt[sl, :].astype(BF16)
            dxdt = _dot(dyb, mpt.astype(BF16), _NT)
            dmt = _dot(xdt[sl, :].astype(BF16), dyb, _TN)
            dgt = dgt + dmt * lt
            q = dmt * mpt
            q_rows = jnp.sum(q, axis=1, keepdims=True)
            q_cols = jnp.sum(q, axis=0, keepdims=True)
            dxdts.append(dxdt)
            xz = jnp.sum(xt[sl, :] * dxdt, axis=0, keepdims=True)
            dw = jnp.sum(rtx[sl, :], axis=0, keepdims=True)
            wk, wdeck = w8[k:k + 1, :], wdec8[k:k + 1, :]
            dcl = jnp.exp(cl8[k:k + 1, :]) * jnp.sum(hdh[sl, :]) + jnp.sum(dw * wk)
            dcs_r = jnp.sum(dyyo[sl, :], axis=0, keepdims=True) + q_cols - dw * wk + jnp.where(last_l, dcl, 0.0)
            onehot = (lane == k).astype(F32)
            ddt_rows = ddt_rows + jnp.where(head_row == k, xz + dw * wdeck, 0.0)
            dcs_rows = dcs_rows + jnp.where(head_row == k, dcs_r, 0.0)
            qrow_cols = qrow_cols + q_rows * onehot
            ddsk_acc = ddsk_acc + jnp.sum(dyx[sl, :]) * onehot
        dx_ref[...] = (dt_r * jnp.concatenate(dxdts, axis=0) + dsk_r * dyt + rt * w_r).T
        dc_ref[...] = _dot(dgt.T.astype(BF16), bb) + dc_acc
        db_ref[...] = _dot(dgt.astype(BF16), cbf) + db_acc
        da = _dot_hi((ci >= ri).astype(F32), dcs_rows.T - qrow_cols)
        ddtp = (ddt_rows.T + da * a_row) * _sigmoid(pre)
        ddtp = jnp.where(lane < hg, ddtp, 0.0)
        ddtp_ref[...] = ddtp
        dbias_ref[...] += jnp.sum(ddtp, axis=0, keepdims=True)
        dalog_ref[...] += jnp.sum(da * dt, axis=0, keepdims=True) * a_row
        ddsk_ref[...] += ddsk_acc

    def rc(c):
        return nc - 1 - c

    vec = pl.BlockSpec((1, LANE), lambda g, c: (0, g))
    xsp = pl.BlockSpec((T, gw), lambda g, c: (rc(c), g))
    return _call(
        body, name,
        [S((rows, D_INNER), F32), S((rows, D_BC), F32), S((rows, D_BC), F32),
         S((rows, SSM_GROUPS * LANE), F32), S((1, SSM_GROUPS * LANE), F32),
         S((1, SSM_GROUPS * LANE), F32), S((1, SSM_GROUPS * LANE), F32)],
        (SSM_GROUPS, nc),
        [xsp,
         pl.BlockSpec((T, D_STATE), lambda g, c: (rc(c), boff + g)),
         pl.BlockSpec((T, D_STATE), lambda g, c: (rc(c), coff + g)),
         pl.BlockSpec((T, LANE), lambda g, c: (rc(c), dtoff + g)), vec, vec, vec,
         xsp, pl.BlockSpec((1, 1, gw, D_STATE), lambda g, c: (rc(c), g, 0, 0))],
        [xsp,
         pl.BlockSpec((T, D_STATE), lambda g, c: (rc(c), g)),
         pl.BlockSpec((T, D_STATE), lambda g, c: (rc(c), g)),
         pl.BlockSpec((T, LANE), lambda g, c: (rc(c), g)), vec, vec, vec],
        ("parallel", "arbitrary"), (xbc, xbc, xbc, zx, bias, alog, dsk, dy, hst),
        scratch=[pltpu.VMEM((gw, D_STATE), F32)], comm=comm)


def _attn_tiles(kv_ref, j):
    prev = jnp.maximum(j - 1, 0)
    meta = kv_ref[0:T, :]
    prv = kv_ref[pl.ds(pl.multiple_of(prev * T, T), T), :]
    cur = kv_ref[pl.ds(pl.multiple_of(j * T, T), T), :]
    return jnp.concatenate([meta, prv, cur], axis=0)


def _attn_mask(j):
    r = j * T + lax.broadcasted_iota(jnp.int32, (3 * T, T), 1)
    row = lax.broadcasted_iota(jnp.int32, (3 * T, T), 0)
    t0, t1 = row < T, row < 2 * T
    s = jnp.where(t0, row, (j - 2) * T + row)
    ok = (s <= r) & ((s < N_META) | (s > r - WINDOW))
    use = (t0 & (j >= 2) & (row < N_META)) | (jnp.logical_not(t0) & t1 & (j >= 1)) | jnp.logical_not(t1)
    return ok & use


def _attn_fwd(q, kv, sinks, name, comm=None):
    rows = q.shape[0]
    scale = 1.0 / math.sqrt(ATTN_DH)
    qpk = N_Q_HEADS // N_KV_HEADS

    def body(q_ref, kv_ref, s_ref, o_ref, lse_ref):
        j = pl.program_id(0)
        kv3 = _attn_tiles(kv_ref, j).astype(BF16)
        mask = _attn_mask(j)
        qv = (q_ref[...] * scale).astype(BF16)
        sk = s_ref[...]
        lses = []
        for kh in range(N_KV_HEADS):
            k3 = kv3[:, kh * ATTN_DH:(kh + 1) * ATTN_DH]
            v3 = kv3[:, D_KV + kh * ATTN_DH:D_KV + (kh + 1) * ATTN_DH]
            for g in range(qpk):
                h = kh * qpk + g
                sink = sk[:, h:h + 1]
                sc = jnp.where(mask, _dot(k3, qv[:, h * ATTN_DH:(h + 1) * ATTN_DH], _NT), NEG)
                m = jnp.maximum(jnp.max(sc, axis=0, keepdims=True), sink)
                p = jnp.exp(sc - m)
                den = jnp.sum(p, axis=0, keepdims=True) + jnp.exp(sink - m)
                p = p * (1.0 / den)
                lses.append(m + jnp.log(den))
                o_ref[:, h * ATTN_DH:(h + 1) * ATTN_DH] = _dot(p.astype(BF16), v3, _TN).astype(o_ref.dtype)
        lse_ref[...] = jnp.concatenate(lses, axis=0)

    return _call(
        body, name, [S((rows, D_MODEL), BF16), S((N_Q_HEADS, rows), F32)], (rows // T,),
        [pl.BlockSpec((T, D_MODEL), lambda j: (j, 0)), pl.BlockSpec((rows, 2 * D_KV), lambda j: (0, 0)),
         pl.BlockSpec((1, N_Q_HEADS), lambda j: (0, 0))],
        [pl.BlockSpec((T, D_MODEL), lambda j: (j, 0)), pl.BlockSpec((N_Q_HEADS, T), lambda j: (0, j))],
        ("parallel",), (q, kv, sinks), comm=comm)


def _attn_bwd(q, kv, sinks, do, lse, name, comm=None):
    rows = q.shape[0]
    scale = 1.0 / math.sqrt(ATTN_DH)
    qpk = N_Q_HEADS // N_KV_HEADS

    def body(q_ref, kv_ref, s_ref, do_ref, lse_ref, dq_ref, dkv_ref, ds_ref):
        j = pl.program_id(0)

        @pl.when(j == 0)
        def _():
            dkv_ref[...] = jnp.zeros_like(dkv_ref)
            ds_ref[...] = jnp.zeros_like(ds_ref)

        kv3 = _attn_tiles(kv_ref, j).astype(BF16)
        mask = _attn_mask(j)
        qv = (q_ref[...] * scale).astype(BF16)
        dov = do_ref[...].astype(BF16)
        sk = s_ref[...]
        lsev = lse_ref[...]
        lane = lax.broadcasted_iota(jnp.int32, (1, LANE), 1)
        ds_acc = jnp.zeros((1, LANE), F32)
        prev = jnp.maximum(j - 1, 0)
        dqts = []
        for kh in range(N_KV_HEADS):
            ksl = slice(kh * ATTN_DH, (kh + 1) * ATTN_DH)
            vsl = slice(D_KV + kh * ATTN_DH, D_KV + (kh + 1) * ATTN_DH)
            k3, v3 = kv3[:, ksl], kv3[:, vsl]
            k3t = k3.T
            dk3 = jnp.zeros((3 * T, ATTN_DH), F32)
            dv3 = jnp.zeros((3 * T, ATTN_DH), F32)
            for g in range(qpk):
                h = kh * qpk + g
                hs = slice(h * ATTN_DH, (h + 1) * ATTN_DH)
                qh, doh = qv[:, hs], dov[:, hs]
                lh = lsev[h:h + 1, :]
                p = jnp.exp(jnp.where(mask, _dot(k3, qh, _NT), NEG) - lh)
                ps = jnp.exp(sk[:, h:h + 1] - lh)
                dp = _dot(v3, doh, _NT)
                delta = jnp.sum(p * dp, axis=0, keepdims=True)
                dsc = (p * (dp - delta)).astype(BF16)
                dqts.append(_dot(k3t, dsc) * scale)
                dk3 = dk3 + _dot(dsc, qh)
                dv3 = dv3 + _dot(p.astype(BF16), doh)
                ds_acc = ds_acc - jnp.sum(ps * delta) * (lane == h).astype(F32)
            for t, start in enumerate((0, pl.multiple_of(prev * T, T), pl.multiple_of(j * T, T))):
                rsl = pl.ds(start, T)
                dkv_ref[rsl, ksl] += dk3[t * T:(t + 1) * T, :]
                dkv_ref[rsl, vsl] += dv3[t * T:(t + 1) * T, :]
        ds_ref[...] += ds_acc
        dq_ref[...] = jnp.concatenate(dqts, axis=0).T.astype(dq_ref.dtype)

    blk = pl.BlockSpec((T, D_MODEL), lambda j: (j, 0))
    full = pl.BlockSpec((rows, 2 * D_KV), lambda j: (0, 0))
    return _call(
        body, name, [S((rows, D_MODEL), BF16), S((rows, 2 * D_KV), F32), S((1, LANE), F32)], (rows // T,),
        [blk, full, pl.BlockSpec((1, N_Q_HEADS), lambda j: (0, 0)), blk, pl.BlockSpec((N_Q_HEADS, T), lambda j: (0, j))],
        [blk, full, pl.BlockSpec((1, LANE), lambda j: (0, 0))], ("arbitrary",), (q, kv, sinks, do, lse), comm=comm)


BLOCK_BYTES = 1 << 20


def _div_tile(rows, cols):
    cap = max(16, BLOCK_BYTES // (4 * cols))
    best = None
    for t in range(16, min(rows, cap) + 1, 16):
        if rows % t == 0:
            best = t
    return best if best is not None else rows


def _adamw(parts, w, m, v, name, comm=None):
    layers, rows, cols = w.shape
    n = parts[0].shape[0]
    tr = _div_tile(rows, cols)
    tc = _pick(cols, 256) if tr == rows and rows * cols * 4 > 2 * BLOCK_BYTES else cols
    c1 = 1.0 / (1.0 - B1 ** STEP)
    c2 = 1.0 / (1.0 - B2 ** STEP)

    def body(*refs):
        p_refs = refs[:layers]
        w_ref, m_ref, v_ref, g_ref, d_ref, nm_ref, nv_ref = refs[layers:]
        layer = pl.program_id(0)
        for l in range(layers):
            @pl.when(layer == l)
            def _(p_ref=p_refs[l]):
                g = p_ref[0].astype(F32)
                for i in range(1, n):
                    g = g + p_ref[i].astype(F32)
                nm = B1 * m_ref[...] + (1.0 - B1) * g
                nv = B2 * v_ref[...] + (1.0 - B2) * (g * g)
                g_ref[...] = g
                nm_ref[...] = nm
                nv_ref[...] = nv
                d_ref[...] = -LR * ((nm * c1) / (jnp.sqrt(nv * c2) + EPS) + WD * w_ref[...])

    def part_spec(l):
        return pl.BlockSpec((n, tr, tc), lambda k, i, j: (0, jnp.where(k == l, i, 0), jnp.where(k == l, j, 0)))

    row = pl.BlockSpec((None, tr, tc), lambda k, i, j: (k, i, j))
    return _call(body, name, [S((layers, rows, cols), F32)] * 4, (layers, rows // tr, cols // tc),
                 [part_spec(l) for l in range(layers)] + [row, row, row], [row] * 4,
                 ("parallel", "parallel", "parallel"), (*parts, w, m, v), comm=comm)


def _sum_parts(parts, name):
    n, rows, cols = parts[0].shape
    nb = len(parts)
    tr = _div_tile(rows, cols)

    def body(*refs):
        o_ref = refs[nb]
        blk = pl.program_id(0)
        for l in range(nb):
            @pl.when(blk == l)
            def _(p_ref=refs[l]):
                g = p_ref[0].astype(F32)
                for i in range(1, n):
                    g = g + p_ref[i].astype(F32)
                o_ref[...] = g

    def part_spec(l):
        return pl.BlockSpec((n, tr, cols), lambda k, i: (0, jnp.where(k == l, i, 0), 0))

    per = rows // tr
    return pl.pallas_call(body, name=name, out_shape=S((nb * rows, cols), F32), grid=(nb, per),
                          in_specs=[part_spec(l) for l in range(nb)],
                          out_specs=pl.BlockSpec((tr, cols), lambda k, i: (k * per + i, 0)),
                          compiler_params=_cp(("parallel", "parallel")))(*parts)


def _col_segments(ws, runs):
    segs = []
    for glo, mlo, n in runs:
        while n > 0:
            d, off = divmod(glo, ws)
            take = min(n, ws - off)
            segs.append((d, off, mlo, take))
            glo, mlo, n = glo + take, mlo + take, n - take
    return segs


def _assemble_cols(gs, width, segs, name):
    _, rows, ws = gs[0].shape
    nb = len(gs)
    rb = _div_tile(rows, width // 2)
    per = rows // rb

    def body(*refs):
        o_ref = refs[nb]
        piece = pl.program_id(0)
        for l in range(nb):
            @pl.when(piece == l)
            def _(g_ref=refs[l]):
                o_ref[...] = jnp.zeros_like(o_ref)
                for d, off, mlo, n in segs:
                    o_ref[:, mlo:mlo + n] = g_ref[d, :, off:off + n]

    def piece_spec(l):
        return pl.BlockSpec((N_DEV, rb, ws), lambda k, i: (0, jnp.where(k == l, i, 0), 0))

    return pl.pallas_call(
        body, name=name, out_shape=S((nb * rows, width), gs[0].dtype), grid=(nb, per),
        in_specs=[piece_spec(l) for l in range(nb)],
        out_specs=pl.BlockSpec((rb, width), lambda k, i: (k * per + i, 0)),
        compiler_params=_cp(("parallel", "parallel")))(*gs)


def _scatter_cols(dw, ws, segs, name):
    rows, width = dw.shape
    rb = _div_tile(rows, width)

    def body(w_ref, o_ref):
        for d, off, mlo, n in segs:
            o_ref[d, :, off:off + n] = w_ref[:, mlo:mlo + n].astype(o_ref.dtype)

    return pl.pallas_call(
        body, name=name, out_shape=S((N_DEV, rows, ws), BF16), grid=(rows // rb,),
        in_specs=[pl.BlockSpec((rb, width), lambda i: (i, 0))],
        out_specs=pl.BlockSpec((N_DEV, rb, ws), lambda i: (0, i, 0)), compiler_params=_cp(("parallel",)))(dw)


def _gather_comm(xs):
    n = len(xs)

    def setup(x_refs, out_refs, sems):
        send_sems, recv_sems, local_sems = sems
        mx, my, mc = lax.axis_index("x"), lax.axis_index("y"), lax.axis_index("c")
        me, sibling = (mx, my, mc), (mx, my, 1 - mc)
        chips = [(1 - mx, my), (mx, 1 - my), (1 - mx, 1 - my)]

        def blk(a, px, py, pc):
            return out_refs[a].at[4 * px + 2 * py + pc]

        def copy(a, k, block, to, src=None):
            return pltpu.make_async_remote_copy(
                src_ref=blk(a, *block) if src is None else src, dst_ref=blk(a, *block),
                send_sem=send_sems.at[a, k], recv_sem=recv_sems.at[a, k], device_id=to, device_id_type=_MESH)

        mine = [pltpu.make_async_copy(x_refs[a], blk(a, *me), local_sems.at[a]) for a in range(n)]
        own = []
        for a in range(n):
            own.append(copy(a, 0, me, sibling, src=x_refs[a]))
            own += [copy(a, 1 + i, me, (*chip, mc), src=x_refs[a]) for i, chip in enumerate(chips)]
        return me, sibling, chips, mc, copy, mine, own

    def first(x_refs, out_refs, sems):
        _, _, _, _, _, mine, own = setup(x_refs, out_refs, sems)
        for cp in mine + own:
            cp.start()

    def last(x_refs, out_refs, sems):
        me, sibling, chips, mc, copy, mine, own = setup(x_refs, out_refs, sems)
        passed = []
        for a in range(n):
            for i, chip in enumerate(chips):
                copy(a, 1 + i, (*chip, mc), me).wait_recv()
                passed.append(copy(a, 4 + i, (*chip, mc), sibling))
                passed[-1].start()
        for a in range(n):
            copy(a, 0, sibling, me).wait_recv()
            for i, chip in enumerate(chips):
                copy(a, 4 + i, (*chip, 1 - mc), me).wait_recv()
        for cp in own + passed:
            cp.wait_send()
        for cp in mine:
            cp.wait()

    return _Comm(list(xs), [S((N_DEV,) + x.shape, x.dtype) for x in xs],
                 [pltpu.SemaphoreType.DMA((n, 7)), pltpu.SemaphoreType.DMA((n, 7)), pltpu.SemaphoreType.DMA((n,))],
                 first, last)


def _swap_comm(gs):
    n = len(gs)

    def copies(g_refs, out_refs, sems):
        send_sems, recv_sems = sems
        mx, my, mc = lax.axis_index("x"), lax.axis_index("y"), lax.axis_index("c")
        return [pltpu.make_async_remote_copy(
            src_ref=g_refs[a].at[2 * k + 1 - mc], dst_ref=out_refs[a].at[k], send_sem=send_sems.at[a, k],
            recv_sem=recv_sems.at[a, k], device_id=(mx, my, 1 - mc), device_id_type=_MESH)
            for a in range(n) for k in range(4)]

    def first(g_refs, out_refs, sems):
        for cp in copies(g_refs, out_refs, sems):
            cp.start()

    def last(g_refs, out_refs, sems):
        for cp in copies(g_refs, out_refs, sems):
            cp.wait()

    return _Comm(list(gs), [S((4,) + g.shape[1:], g.dtype) for g in gs],
                 [pltpu.SemaphoreType.DMA((n, 4)), pltpu.SemaphoreType.DMA((n, 4))], first, last)


def _chips_comm(parts):
    n = len(parts)

    def copies(p_refs, out_refs, sems):
        send_sems, recv_sems, local_sems = sems
        mx, my, mc = lax.axis_index("x"), lax.axis_index("y"), lax.axis_index("c")
        mychip = 2 * mx + my
        chips = [(1 - mx, my), (mx, 1 - my), (1 - mx, 1 - my)]
        mine = [pltpu.make_async_copy(p_refs[a].at[mychip], out_refs[a].at[mychip], local_sems.at[a])
                for a in range(n)]
        return mine + [pltpu.make_async_remote_copy(
            src_ref=p_refs[a].at[2 * cx + cy], dst_ref=out_refs[a].at[mychip], send_sem=send_sems.at[a, i],
            recv_sem=recv_sems.at[a, i], device_id=(cx, cy, mc), device_id_type=_MESH)
            for a in range(n) for i, (cx, cy) in enumerate(chips)]

    def first(p_refs, out_refs, sems):
        for cp in copies(p_refs, out_refs, sems):
            cp.start()

    def last(p_refs, out_refs, sems):
        for cp in copies(p_refs, out_refs, sems):
            cp.wait()

    return _Comm(list(parts), [S(p.shape, p.dtype) for p in parts],
                 [pltpu.SemaphoreType.DMA((n, 3)), pltpu.SemaphoreType.DMA((n, 3)), pltpu.SemaphoreType.DMA((n,))],
                 first, last)


def _join_comms(comms):
    def split(refs, counts):
        out, p = [], 0
        for cnt in counts:
            out.append(refs[p:p + cnt])
            p += cnt
        return out

    ni = [len(c.ins) for c in comms]
    no = [len(c.out_shapes) for c in comms]
    ns = [len(c.scratch) for c in comms]

    def first(in_refs, out_refs, sems):
        for c, i, o, s in zip(comms, split(in_refs, ni), split(out_refs, no), split(sems, ns)):
            c.first(i, o, s)

    def last(in_refs, out_refs, sems):
        for c, i, o, s in zip(comms, split(in_refs, ni), split(out_refs, no), split(sems, ns)):
            c.last(i, o, s)

    return _Comm([x for c in comms for x in c.ins], [x for c in comms for x in c.out_shapes],
                 [x for c in comms for x in c.scratch], first, last)


def _add_pairs(mine, theirs, core, name):
    _, rows, cols = mine.shape
    tr = _div_tile(rows, cols)

    def body(core_ref, a_ref, b_ref, o_ref):
        o_ref[...] = (a_ref[...].astype(F32) + b_ref[...].astype(F32)).astype(o_ref.dtype)

    return pl.pallas_call(
        body, name=name, out_shape=S((4, rows, cols), BF16),
        grid_spec=pltpu.PrefetchScalarGridSpec(
            num_scalar_prefetch=1, grid=(4, rows // tr),
            in_specs=[pl.BlockSpec((None, tr, cols), lambda k, i, c: (2 * k + c[0], i, 0)),
                      pl.BlockSpec((None, tr, cols), lambda k, i, c: (k, i, 0))],
            out_specs=pl.BlockSpec((None, tr, cols), lambda k, i, c: (k, i, 0))),
        compiler_params=_cp(("parallel", "parallel")))(core, mine, theirs)


def _run_comm(comm, name):
    ci, co = len(comm.ins), len(comm.out_shapes)

    def body(*refs):
        comm.first(refs[:ci], refs[ci:ci + co], refs[ci + co:])
        comm.last(refs[:ci], refs[ci:ci + co], refs[ci + co:])

    return pl.pallas_call(body, name=name, out_shape=list(comm.out_shapes), in_specs=[_HBM] * ci,
                          out_specs=[_HBM] * co, scratch_shapes=list(comm.scratch))(*comm.ins)


def _flat_rows(n_elems, mult):
    rows = -(-n_elems // LANE)
    return -(-rows // mult) * mult


def _pack(arrs, lead, mult, dtype):
    lead_shape = arrs[0].shape[:lead]
    flat = jnp.concatenate([a.astype(dtype).reshape(lead_shape + (-1,)) for a in arrs], axis=-1)
    n = flat.shape[-1]
    rows = _flat_rows(n, mult)
    flat = jnp.pad(flat, [(0, 0)] * lead + [(0, rows * LANE - n)])
    return flat.reshape(lead_shape + (rows, LANE))


def _unpack(flat, lead, shapes):
    lead_shape = flat.shape[:lead]
    flat = flat.reshape(lead_shape + (-1,))
    out, off = [], 0
    for shp in shapes:
        n = math.prod(shp)
        out.append(flat[..., off:off + n].reshape(lead_shape + tuple(shp)))
        off += n
    return out


def _split8(full, ax, n):
    shp = full.shape
    return jnp.moveaxis(full.reshape(shp[:ax] + (N_DEV, n) + shp[ax + 1:]), ax, 0)


def _join8(g, ax):
    shp = g.shape[1:]
    return jnp.moveaxis(g, 0, ax).reshape(shp[:ax] + (N_DEV * shp[ax],) + shp[ax + 1:])


def _group_lanes(v, hg):
    v = v.reshape(SSM_GROUPS, hg)
    return jnp.pad(v, ((0, 0), (0, LANE - hg))).reshape(1, SSM_GROUPS * LANE)


def _ungroup_lanes(v, hg):
    return v.reshape(SSM_GROUPS, LANE)[:, :hg].reshape(1, SSM_GROUPS * hg)


def kernel(x, meta_tokens, a_norm_pre, a_w_in, a_conv_w, a_conv_b, a_dt_bias, a_a_log, a_d_skip, a_gate_norm, a_w_out, a_norm_post, kv_norm, w_kv, b_norm_pre, b_w_q, b_sinks, b_w_o, b_norm_post, f_norm_pre, f_w_up, f_conv_w, f_conv_b, f_w_down, f_norm_post, loss_target, m_meta_tokens, m_a_norm_pre, m_a_w_in, m_a_conv_w, m_a_conv_b, m_a_dt_bias, m_a_a_log, m_a_d_skip, m_a_gate_norm, m_a_w_out, m_a_norm_post, m_kv_norm, m_w_kv, m_b_norm_pre, m_b_w_q, m_b_sinks, m_b_w_o, m_b_norm_post, m_f_norm_pre, m_f_w_up, m_f_conv_w, m_f_conv_b, m_f_w_down, m_f_norm_post, v_meta_tokens, v_a_norm_pre, v_a_w_in, v_a_conv_w, v_a_conv_b, v_a_dt_bias, v_a_a_log, v_a_d_skip, v_a_gate_norm, v_a_w_out, v_a_norm_post, v_kv_norm, v_w_kv, v_b_norm_pre, v_b_w_q, v_b_sinks, v_b_w_o, v_b_norm_post, v_f_norm_pre, v_f_w_up, v_f_conv_w, v_f_conv_b, v_f_w_down, v_f_norm_post):
    args = locals()
    wts = {n: args[n] for n in WEIGHTS}
    mom = {n: args["m_" + n] for n in WEIGHTS}
    var = {n: args["v_" + n] for n in WEIGHTS}
    mx, my, mc = lax.axis_index("x"), lax.axis_index("y"), lax.axis_index("c")
    me = 4 * mx + 2 * my + mc
    rows = _seq_rows()
    hg = SSM_HEADS // SSM_GROUPS
    d = D_MODEL

    n_main = D_INNER + D_XBC
    ws_in, ws_up = a_w_in.shape[2], f_w_up.shape[2]
    segs_in = _col_segments(ws_in, [(0, 0, n_main)] + [(n_main + hg * g, n_main + LANE * g, hg)
                                                      for g in range(SSM_GROUPS)])
    segs_up = _col_segments(ws_up, [(0, 0, 2 * D_FF)])
    def gather_of(*ws):
        return _gather_comm([w.astype(BF16) for w in ws])

    small_full, = _run_comm(_gather_comm([_pack([wts[n] for n in SMALL], 0, 8, F32)]), "gather_small")
    full = {}
    for n, g in zip(SMALL, _unpack(small_full, 1, [wts[n].shape for n in SMALL])):
        full[n] = _join8(g, SHARD_AXIS[n])
    (h0, hn0), (g_in,) = _embed_norm(full["meta_tokens"], x[0], full["a_norm_pre"], rows, "embed_norm",
                                     comm=gather_of(a_w_in[0]))
    w_in_all = _assemble_cols([g_in], n_main + SSM_GROUPS * LANE, segs_in, "asm_w_in")
    w_up, w_down = [None, None], [None, None]
    bias_g = _group_lanes(wts["a_dt_bias"], hg)
    alog_g = _group_lanes(wts["a_a_log"], hg)
    dsk_g = _group_lanes(wts["a_d_skip"], hg)
    a_conv_w, a_conv_b = full["a_conv_w"][0], full["a_conv_b"]
    f_cw, f_cb = full["f_conv_w"], wts["f_conv_b"]
    fpre, fpost = wts["f_norm_pre"], wts["f_norm_post"]

    tgt = jnp.pad(loss_target[0], ((N_META, rows - N_META - SEQ), (0, 0)))

    zx, (g_out,) = _mm(hn0, w_in_all, "nn", F32, "mm_in", comm=gather_of(a_w_out[0]))
    w_out = g_out.reshape(D_INNER, d)
    xbc = _conv_silu_fwd(zx, a_conv_w, a_conv_b, "conv_a")
    (y_ssd, hst), (g_up0,) = _ssd_fwd(xbc, zx, bias_g, alog_g, dsk_g, "ssd_fwd", comm=gather_of(f_w_up[0]))
    w_up[0] = _assemble_cols([g_up0], 2 * D_FF, segs_up, "asm_w_up0")
    yn = _gatenorm_fwd(y_ssd, zx, full["a_gate_norm"], "gatenorm")
    mix_a, (g_o,) = _mm(yn, w_out, "nn", F32, "mm_out", comm=gather_of(b_w_o[0]))
    h1, (fn0,) = _resid_norm(h0, mix_a, full["a_norm_post"], [fpre[0:1]], "resid_a")

    half = d // 2
    u0, (g_dn0,) = _mm(fn0, w_up[0], "nn", F32, "mm_up0", comm=gather_of(f_w_down[0]))
    act0, (g_up1a,) = _ffn_act_fwd(u0, f_cw[0], f_cb[0:1], "ffn_act0", comm=gather_of(f_w_up[1, :half]))
    ffn0, (g_kv, g_q) = _mm(act0, g_dn0.reshape(D_FF, d), "nn", F32, "mm_down0", comm=gather_of(w_kv, b_w_q[0]))
    w_kvf, w_q, w_o = g_kv.reshape(d, 2 * D_KV), g_q.reshape(d, d), g_o.reshape(d, d)
    h2, (kvn, bn) = _resid_norm(h1, ffn0, fpost[0:1], [wts["kv_norm"].reshape(1, d), wts["b_norm_pre"]], "resid_f0")
    kv = _mm(kvn, w_kvf, "nn", F32, "mm_kv")
    q = _mm(bn, w_q, "nn", F32, "mm_q")
    (o, lse), (g_up1b,) = _attn_fwd(q, kv, wts["b_sinks"], "attn_fwd", comm=gather_of(f_w_up[1, half:]))
    w_up[1] = _assemble_cols([g_up1a, g_up1b], 2 * D_FF, segs_up, "asm_w_up1")
    mix_b = _mm(o, w_o, "nn", F32, "mm_o")
    h3, (fn1,) = _resid_norm(h2, mix_b, wts["b_norm_post"], [fpre[1:2]], "resid_b")
    u1, (g_dn1,) = _mm(fn1, w_up[1], "nn", F32, "mm_up1", comm=gather_of(f_w_down[1]))
    w_down = [g_dn0.reshape(D_FF, d), g_dn1.reshape(D_FF, d)]
    act1 = _ffn_act_fwd(u1, f_cw[1], f_cb[1:2], "ffn_act1")
    ffn1 = _mm(act1, w_down[1], "nn", F32, "mm_down1")
    dh4, loss_row, dffn1, dw_post1 = _final_loss(h3, ffn1, fpost[1:2], tgt, "loss")
    loss = lax.psum(loss_row[0, 0], ("x", "y", "c"))

    grads = {}

    core = mc.astype(jnp.int32).reshape(1)

    def carried(res, comm):
        return res if comm is not None else (res, None)

    def ffn_bwd(dh_out, dffn, h_in, fn, u, act, i, then, c_dact=None, c_dwdown=None, c_dwup=None):
        dact, got_a = carried(_mm(dffn, w_down[i], "nt", F32, f"mm_dact{i}", comm=c_dact), c_dact)
        dw_down, got_b = carried(_mm(act, dffn, "tn", BF16, f"mm_dwdown{i}", comm=c_dwdown), c_dwdown)
        dw_down = dw_down.reshape(N_DEV, -1, d)
        du, dwc, dbc = _ffn_act_bwd(u, dact, f_cw[i], f_cb[i:i + 1], f"ffn_act_bwd{i}")
        dfn, (s_dn,) = _mm(du, w_up[i], "nt", F32, f"mm_dfn{i}", comm=_swap_comm([dw_down]))
        sum_dn = _add_pairs(dw_down, s_dn, core, f"rs_add_dn{i}")
        dw_up, got_c = carried(_mm(fn, du, "tn", BF16, f"mm_dwup{i}", comm=c_dwup, shard_cols=ws_up), c_dwup)
        (dh_in, dw_pre, dbranch, dw_branch), (s_up,) = _norm_bwd(
            h_in, fpre[i:i + 1], dfn, dh_out, F32, f"nb_fpre{i}", comm=_swap_comm([dw_up]), then=then)
        sum_up = _add_pairs(dw_up, s_up, core, f"rs_add_up{i}")
        return dh_in, dbranch, dw_branch, dict(sum_down=sum_dn, cw=jnp.concatenate([dwc[0], dwc[1]], axis=1),
                                               cb=jnp.concatenate([dbc[0], dbc[1]], axis=1), sum_up=sum_up,
                                               pre=dw_pre), got_a, got_b, got_c

    dh3, dmix_b, grads["b_norm_post"], gf1, _, _, _ = ffn_bwd(dh4, dffn1, h3, fn1, u1, act1, 1,
                                                              (mix_b, wts["b_norm_post"]))
    do = _mm(dmix_b, w_o, "nt", F32, "mm_do")
    dw_o = _mm(o, dmix_b, "tn", BF16, "mm_dwo").reshape(N_DEV, -1, d)
    (dq, dkv, dsinks), (p_up1, s_o) = _attn_bwd(
        q, kv, wts["b_sinks"], do, lse, "attn_bwd",
        comm=_join_comms([_chips_comm([gf1["sum_up"]]), _swap_comm([dw_o])]))
    sum_o = _add_pairs(dw_o, s_o, core, "rs_add_o")
    grads["b_sinks"] = dsinks[:, :N_Q_HEADS]
    dbn = _mm(dq, w_q, "nt", F32, "mm_dbn")
    dw_q = _mm(bn, dq, "tn", BF16, "mm_dwq").reshape(N_DEV, -1, d)
    dkv16 = dkv.astype(BF16)
    dkvn = _mm(dkv16, w_kvf, "nt", F32, "mm_dkvn")
    dw_kv = _mm(kvn, dkv16, "tn", BF16, "mm_dwkv").reshape(N_DEV, -1, 2 * D_KV)
    (dh2, grads["b_norm_pre"]), (s_q, s_kv) = _norm_bwd(h2, wts["b_norm_pre"], dbn, dh3, F32, "nb_bpre",
                                                        comm=_swap_comm([dw_q, dw_kv]))
    sum_q, sum_kv = _add_pairs(dw_q, s_q, core, "rs_add_q"), _add_pairs(dw_kv, s_kv, core, "rs_add_kv")
    dh2, dw_kvn, dffn0, dw_post0 = _norm_bwd(h2, wts["kv_norm"].reshape(1, d), dkvn, dh2, F32, "nb_kv",
                                             then=(ffn0, fpost[0:1]))
    grads["kv_norm"] = dw_kvn.reshape(d)
    dh1, dmix_a, grads["a_norm_post"], gf0, (p_o,), (p_q, p_kv), (p_dn1,) = ffn_bwd(
        dh2, dffn0, h1, fn0, u0, act0, 0, (mix_a, full["a_norm_post"]), c_dact=_chips_comm([sum_o]),
        c_dwdown=_chips_comm([sum_q, sum_kv]), c_dwup=_chips_comm([gf1["sum_down"]]))
    grads["f_norm_post"] = jnp.concatenate([dw_post0, dw_post1], axis=0)
    grads["f_norm_pre"] = jnp.concatenate([gf0["pre"], gf1["pre"]], axis=0)
    grads["f_conv_w"] = jnp.stack([gf0["cw"], gf1["cw"]])
    grads["f_conv_b"] = jnp.concatenate([gf0["cb"], gf1["cb"]], axis=0)

    dyn = _mm(dmix_a, w_out, "nt", F32, "mm_dyn")
    dw_out = _mm(yn, dmix_a, "tn", BF16, "mm_dwout").reshape(N_DEV, -1, d)
    (dy_ssd, dz, grads["a_gate_norm"]), (s_out,) = _gatenorm_bwd(y_ssd, zx, full["a_gate_norm"], dyn, "gatenorm_bwd",
                                                                 comm=_swap_comm([dw_out]))
    sum_out = _add_pairs(dw_out, s_out, core, "rs_add_out")
    (dxs, dbm, dcm, ddtp, dalog, ddsk, dbias), (p_up0,) = _ssd_bwd(
        xbc, zx, bias_g, alog_g, dsk_g, dy_ssd, hst, "ssd_bwd", comm=_chips_comm([gf0["sum_up"]]))
    grads["a_a_log"] = _ungroup_lanes(dalog, hg)
    grads["a_d_skip"] = _ungroup_lanes(ddsk, hg)
    grads["a_dt_bias"] = _ungroup_lanes(dbias, hg)
    dpre, dcw, dcb = _conv_silu_bwd(zx, dxs, dbm, dcm, a_conv_w, a_conv_b, "conv_a_bwd")
    grads["a_conv_w"], grads["a_conv_b"] = dcw[None], dcb
    dzx = jnp.concatenate([dz, dpre, ddtp.astype(BF16)], axis=1)
    dw_in_all, (p_dn0,) = _mm(hn0, dzx, "tn", BF16, "mm_dwin", comm=_chips_comm([gf0["sum_down"]]))
    dw_in8 = _scatter_cols(dw_in_all, ws_in, segs_in, "scat_w_in")
    dhn0, (s_in, p_out) = _mm(dzx, w_in_all, "nt", F32, "mm_dhn0",
                              comm=_join_comms([_swap_comm([dw_in8]), _chips_comm([sum_out])]))
    sum_in = _add_pairs(dw_in8, s_in, core, "rs_add_in")
    half_in = sum_in.shape[1] // 2
    (dh0, grads["a_norm_pre"]), (p_in_a,) = _norm_bwd(h0, full["a_norm_pre"], dhn0, dh1, F32, "nb_apre",
                                                      comm=_chips_comm([sum_in[:, :half_in]]))
    grad_x = dh0[N_META:N_META + SEQ][None]
    grads["meta_tokens"] = dh0[:N_META]

    small_local = _pack([_split8(grads[n], SHARD_AXIS[n], wts[n].shape[SHARD_AXIS[n]]) for n in SMALL], 1, 8, F32)
    repl_local = _pack([grads[n] for n in REPL], 0, 8, F32)
    n_sr = small_local.shape[1]
    small_vec = jnp.concatenate([small_local.reshape(N_DEV * n_sr, LANE), repl_local], axis=0)
    tail = _join_comms([_chips_comm([sum_in[:, half_in:]]), _gather_comm([small_vec])])
    parts_big = dict(a_w_out=[p_out], w_kv=[p_kv], b_w_q=[p_q], b_w_o=[p_o], f_w_down=[p_dn0, p_dn1])

    def flat_f32(dct, names, mult):
        return _pack([dct[n] for n in names], 0, mult, F32)

    def adamw_big(n, comm=None):
        shp3 = (len(parts_big[n]),) + parts_big[n][0].shape[1:]
        res = _adamw(parts_big[n], *[dct[n].reshape(shp3) for dct in (wts, mom, var)], f"adamw_{n}", comm=comm)
        res, got = res if comm is not None else (res, None)
        big_out[n] = [r.reshape(wts[n].shape) for r in res]
        return got

    big_out = {}
    def swap_last(a):
        return jnp.swapaxes(a, -1, -2)

    g_up_t = swap_last(_sum_parts([p_up0, p_up1], "sum_w_up").reshape(f_w_up.shape))
    res, (p_in_b, small_all) = _adamw([g_up_t[0:1], g_up_t[1:2]], *[swap_last(dct["f_w_up"]) for dct in (wts, mom, var)],
                                      "adamw_f_w_up", comm=tail)
    big_out["f_w_up"] = [swap_last(r) for r in res]
    for n in BIG:
        if n not in ("f_w_up", "a_w_in"):
            adamw_big(n)
    g_in_t = swap_last(_sum_parts([p_in_a, p_in_b], "sum_w_in"))[None]
    res = _adamw([g_in_t], *[swap_last(dct["a_w_in"]) for dct in (wts, mom, var)], "adamw_a_w_in")
    big_out["a_w_in"] = [swap_last(r) for r in res]
    mine_small = lax.dynamic_slice_in_dim(small_all, me * n_sr, n_sr, axis=1)
    parts_small = jnp.concatenate([mine_small, small_all[:, N_DEV * n_sr:]], axis=1)
    sm_in = [jnp.concatenate([flat_f32(dct, SMALL, 8), flat_f32(dct, REPL, 8)], axis=0)[None] for dct in (wts, mom, var)]
    small_out = [r[0] for r in _adamw([parts_small], *sm_in, "adamw_small")]

    outs = []
    for kind in range(4):
        res = {n: big_out[n][kind] for n in BIG}
        for n, a in zip(SMALL, _unpack(small_out[kind][:n_sr], 0, [wts[n].shape for n in SMALL])):
            res[n] = a
        for n, a in zip(REPL, _unpack(small_out[kind][n_sr:], 0, [wts[n].shape for n in REPL])):
            res[n] = a
        outs.append(res)
    return (loss, grad_x, *[outs[0][n] for n in WEIGHTS], *[outs[1][n] for n in WEIGHTS],
            *[outs[2][n] for n in WEIGHTS], *[outs[3][n] for n in WEIGHTS])
```

```python
import functools
import math

import jax
import jax.numpy as jnp
from jax import lax
from jax.experimental import pallas as pl
from jax.experimental.pallas import tpu as pltpu

F32, BF16 = jnp.float32, jnp.bfloat16
S = jax.ShapeDtypeStruct

D_MODEL = 1024
SEQ = 2048
N_META = 16
D_INNER = 2048
HEAD_P = 64
SSM_HEADS = D_INNER // HEAD_P
SSM_GROUPS = 4
D_STATE = 128
SSM_CONV = 4
D_BC = SSM_GROUPS * D_STATE
D_XBC = D_INNER + 2 * D_BC
ATTN_DH = 64
N_Q_HEADS = D_MODEL // ATTN_DH
N_KV_HEADS = 4
D_KV = N_KV_HEADS * ATTN_DH
WINDOW = 128
D_FF = 2816
FFN_CONV = 3
RMS_EPS = 1e-6
NEG = -1e30
LR, B1, B2, EPS, WD, STEP = 0.001, 0.9, 0.999, 1e-08, 0.01, 10

N_DEV = 8
T = 128
LANE = 128
VMEM_LIMIT = 48 * 1024 * 1024

BIG = ("a_w_in", "a_w_out", "w_kv", "b_w_q", "b_w_o", "f_w_up", "f_w_down")
SMALL = ("meta_tokens", "a_norm_pre", "a_conv_w", "a_conv_b", "a_gate_norm", "a_norm_post", "f_conv_w")
REPL = ("a_dt_bias", "a_a_log", "a_d_skip", "kv_norm", "b_norm_pre", "b_sinks", "b_norm_post",
        "f_norm_pre", "f_conv_b", "f_norm_post")
SHARD_AXIS = dict(a_w_in=2, a_w_out=1, w_kv=0, b_w_q=1, b_w_o=1, f_w_up=2, f_w_down=1, meta_tokens=1,
                  a_norm_pre=1, a_conv_w=2, a_conv_b=1, a_gate_norm=1, a_norm_post=1, f_conv_w=2)
WEIGHTS = ("meta_tokens", "a_norm_pre", "a_w_in", "a_conv_w", "a_conv_b", "a_dt_bias", "a_a_log", "a_d_skip",
           "a_gate_norm", "a_w_out", "a_norm_post", "kv_norm", "w_kv", "b_norm_pre", "b_w_q", "b_sinks", "b_w_o",
           "b_norm_post", "f_norm_pre", "f_w_up", "f_conv_w", "f_conv_b", "f_w_down", "f_norm_post")


def _seq_rows():
    return -(-(N_META + SEQ) // T) * T


def _cp(sem=None):
    return pltpu.CompilerParams(dimension_semantics=sem, vmem_limit_bytes=VMEM_LIMIT)


def _pick(n, target):
    t = min(n, target)
    t -= t % LANE
    while n % t:
        t -= LANE
    return t


def _sigmoid(x):
    return 0.5 * jnp.tanh(0.5 * x) + 0.5


def _softplus(x):
    return jnp.maximum(x, 0.0) + jnp.log(1.0 + jnp.exp(-jnp.abs(x)))


_NN = (((1,), (0,)), ((), ()))
_NT = (((1,), (1,)), ((), ()))
_TN = (((0,), (0,)), ((), ()))


def _dot(a, b, dims=_NN):
    return lax.dot_general(a, b, dims, preferred_element_type=F32)


def _dot_hi(a, b):
    return lax.dot_general(a, b, _NN, precision=lax.Precision.HIGHEST, preferred_element_type=F32)


_HBM = pl.BlockSpec(memory_space=pltpu.HBM)
_MESH = pl.DeviceIdType.MESH


class _Comm:
    def __init__(self, ins, out_shapes, scratch, first, last):
        self.ins, self.out_shapes, self.scratch, self.first, self.last = ins, out_shapes, scratch, first, last


def _call(body, name, out_shape, grid, in_specs, out_specs, sem, args, scratch=(), comm=None):
    if comm is None:
        return pl.pallas_call(body, name=name, out_shape=out_shape, grid=grid, in_specs=in_specs, out_specs=out_specs,
                              scratch_shapes=list(scratch), compiler_params=_cp(sem))(*args)
    single = not isinstance(out_shape, (list, tuple))
    outs = [out_shape] if single else list(out_shape)
    ospecs = [out_specs] if single else list(out_specs)
    n_in, n_out, n_scr, ci, co = len(in_specs), len(outs), len(scratch), len(comm.ins), len(comm.out_shapes)

    def carrier(*refs):
        p = 0
        parts = []
        for cnt in (n_in, ci, n_out, co, n_scr, len(comm.scratch)):
            parts.append(refs[p:p + cnt])
            p += cnt
        ins, cins, outs_r, couts, scr, cscr = parts
        ids = [pl.program_id(i) for i in range(len(grid))]
        first, last = ids[0] == 0, ids[0] == grid[0] - 1
        for i in range(1, len(grid)):
            first, last = first & (ids[i] == 0), last & (ids[i] == grid[i] - 1)

        @pl.when(first)
        def _():
            comm.first(cins, couts, cscr)

        body(*ins, *outs_r, *scr)

        @pl.when(last)
        def _():
            comm.last(cins, couts, cscr)

    res = pl.pallas_call(
        carrier, name=name, out_shape=outs + list(comm.out_shapes), grid=grid,
        in_specs=list(in_specs) + [_HBM] * ci, out_specs=ospecs + [_HBM] * co,
        scratch_shapes=list(scratch) + list(comm.scratch),
        compiler_params=_cp(("arbitrary",) * len(grid)))(*args, *comm.ins)
    mine = res[0] if single else list(res[:n_out])
    return mine, list(res[n_out:])


def _mm(a, b, mode, out_dtype, name, comm=None, shard_cols=None):
    if mode == "tn":
        m, kk = a.shape
        planes, width = (b.shape[0], b.shape[2]) if b.ndim == 3 else (1, b.shape[1])
        n = planes * width
        tko, tn = _pick(kk, 512), _pick(width, 1536)
        per = width // tn
        b_spec = (pl.BlockSpec((None, m, tn), lambda i, j: (j // per, 0, j % per)) if b.ndim == 3
                  else pl.BlockSpec((m, tn), lambda i, j: (0, j)))
        if shard_cols is None:
            def body(a_ref, b_ref, o_ref):
                o_ref[...] = _dot(a_ref[...], b_ref[...], _TN).astype(o_ref.dtype)

            out_shape, out_spec = S((kk, n), out_dtype), pl.BlockSpec((tko, tn), lambda i, j: (i, j))
        else:
            shards = tn // shard_cols
            assert tn % shard_cols == 0

            def body(a_ref, b_ref, o_ref):
                res = _dot(a_ref[...], b_ref[...], _TN).astype(o_ref.dtype)
                for p in range(shards):
                    o_ref[p] = res[:, p * shard_cols:(p + 1) * shard_cols]

            out_shape = S((n // shard_cols, kk, shard_cols), out_dtype)
            out_spec = pl.BlockSpec((shards, tko, shard_cols), lambda i, j: (j, i, 0))
        return _call(
            body, name, out_shape, (kk // tko, n // tn), [pl.BlockSpec((m, tko), lambda i, j: (0, i)), b_spec],
            out_spec, ("parallel", "parallel"), (a, b), comm=comm)

    planes, width = (a.shape[0], a.shape[2]) if a.ndim == 3 else (1, a.shape[1])
    m, kk = a.shape[-2], planes * width
    n = b.shape[1] if mode == "nn" else b.shape[0]
    dims = _NN if mode == "nn" else _NT

    if kk > 2048:
        tm = m // 4
        assert m % 4 == 0 and tm % 16 == 0

        def body(a_ref, b_ref, o_ref):
            if a.ndim == 2:
                res = _dot(a_ref[...], b_ref[...], dims)
            else:
                res = None
                for p in range(planes):
                    bp = b_ref[p * width:(p + 1) * width, :] if mode == "nn" else b_ref[:, p * width:(p + 1) * width]
                    part = _dot(a_ref[p], bp, dims)
                    res = part if res is None else res + part
            o_ref[...] = res.astype(o_ref.dtype)

        a_spec = (pl.BlockSpec((planes, tm, width), lambda i: (0, i, 0)) if a.ndim == 3
                  else pl.BlockSpec((tm, kk), lambda i: (i, 0)))
        return _call(
            body, name, S((m, n), out_dtype), (m // tm,),
            [a_spec, pl.BlockSpec(b.shape, lambda i: (0, 0), pipeline_mode=pl.Buffered(1))],
            pl.BlockSpec((tm, n), lambda i: (i, 0)), ("parallel",), (a, b), comm=comm)

    tn = _pick(n, 512)

    def body(a_ref, b_ref, o_ref):
        o_ref[...] = _dot(a_ref[...], b_ref[...], dims).astype(o_ref.dtype)

    b_spec = (pl.BlockSpec((kk, tn), lambda j: (0, j)) if mode == "nn" else pl.BlockSpec((tn, kk), lambda j: (j, 0)))
    return _call(
        body, name, S((m, n), out_dtype), (n // tn,), [pl.BlockSpec((m, kk), lambda j: (0, 0)), b_spec],
        pl.BlockSpec((m, tn), lambda j: (0, j)), ("parallel",), (a, b), comm=comm)


def _rms(x, w):
    return x * lax.rsqrt(jnp.mean(x * x, axis=-1, keepdims=True) + RMS_EPS) * w


def _row_tile(rows):
    return rows // 8


def _embed_norm(meta, x, w, rows, name, comm=None):
    n_meta, d = meta.shape
    n_x = x.shape[0]
    last = rows // T - 1
    assert n_meta % 8 == 0 and n_meta < T and n_meta + n_x == last * T + n_meta and last * T >= n_x

    def body(m_ref, x_ref, w_ref, h_ref, hn_ref):
        i = pl.program_id(0)

        @pl.when(i == 0)
        def _():
            h_ref[0:n_meta, :] = m_ref[...]
            h_ref[n_meta:T, :] = x_ref[0:T - n_meta, :]

        @pl.when((i > 0) & (i < last))
        def _():
            h_ref[...] = x_ref[pl.ds(pl.multiple_of(i * T - n_meta, 8), T), :]

        @pl.when(i == last)
        def _():
            h_ref[0:n_meta, :] = x_ref[n_x - n_meta:n_x, :]
            h_ref[n_meta:T, :] = jnp.zeros((T - n_meta, d), F32)

        hn_ref[...] = _rms(h_ref[...], w_ref[...]).astype(hn_ref.dtype)

    row = pl.BlockSpec((T, d), lambda i: (i, 0))
    return _call(body, name, [S((rows, d), F32), S((rows, d), BF16)], (rows // T,),
                 [pl.BlockSpec((n_meta, d), lambda i: (0, 0)), pl.BlockSpec((n_x, d), lambda i: (0, 0)),
                  pl.BlockSpec((1, d), lambda i: (0, 0))], [row, row], ("parallel",), (meta, x, w), comm=comm)


def _resid_norm(h, br, w_post, next_ws, name):
    rows, d = h.shape
    tr = _row_tile(rows)
    has_br = br is not None
    nw = len(next_ws)

    def body(*refs):
        h_ref = refs[0]
        pos = 1
        x = h_ref[...]
        if has_br:
            x = x + _rms(refs[1][...], refs[2][...])
            pos = 3
        w_refs = refs[pos:pos + nw]
        outs = refs[pos + nw:]
        if has_br:
            outs[0][...] = x
            outs = outs[1:]
        for w_ref, o_ref in zip(w_refs, outs):
            o_ref[...] = _rms(x, w_ref[...]).astype(o_ref.dtype)

    row = pl.BlockSpec((tr, d), lambda i: (i, 0))
    vec = pl.BlockSpec((1, d), lambda i: (0, 0))
    ins = [h] + ([br, w_post] if has_br else []) + list(next_ws)
    in_specs = [row] + ([row, vec] if has_br else []) + [vec] * nw
    out_shape = ([S((rows, d), F32)] if has_br else []) + [S((rows, d), BF16)] * nw
    res = pl.pallas_call(body, name=name, out_shape=out_shape, grid=(rows // tr,), in_specs=in_specs,
                         out_specs=[row] * len(out_shape), compiler_params=_cp(("parallel",)))(*ins)
    if has_br:
        return res[0], list(res[1:])
    return h, list(res)


def _rms_bwd(xv, w, dyv):
    r = lax.rsqrt(jnp.mean(xv * xv, axis=-1, keepdims=True) + RMS_EPS)
    wdy = dyv * w
    dx = r * wdy - xv * (r * r * r) * jnp.mean(xv * wdy, axis=-1, keepdims=True)
    return dx, jnp.sum(dyv * xv * r, axis=0, keepdims=True)


def _norm_bwd(x, w, dy, add, out_dtype, name, comm=None, then=None):
    rows, d = x.shape
    tr = _row_tile(rows)
    has_add = add is not None
    n_in = 3 + has_add + (2 if then is not None else 0)

    def body(*refs):
        x_ref, w_ref, dy_ref = refs[:3]
        outs = refs[n_in:]
        dx, dw = _rms_bwd(x_ref[...], w_ref[...], dy_ref[...].astype(F32))
        if has_add:
            dx = dx + refs[3][...]
        outs[0][...] = dx.astype(outs[0].dtype)
        first = pl.program_id(0) == 0

        @pl.when(first)
        def _():
            outs[1][...] = jnp.zeros_like(outs[1])

        outs[1][...] += dw
        if then is not None:
            dx2, dw2 = _rms_bwd(refs[n_in - 2][...], refs[n_in - 1][...], dx)
            outs[2][...] = dx2.astype(outs[2].dtype)

            @pl.when(first)
            def _():
                outs[3][...] = jnp.zeros_like(outs[3])

            outs[3][...] += dw2

    row = pl.BlockSpec((tr, d), lambda i: (i, 0))
    vec = pl.BlockSpec((1, d), lambda i: (0, 0))
    ins = [x, w, dy] + ([add] if has_add else []) + (list(then) if then is not None else [])
    in_specs = [row, vec, row] + ([row] if has_add else []) + ([row, vec] if then is not None else [])
    out_shape = [S((rows, d), out_dtype), S((1, d), F32)] + ([S((rows, d), BF16), S((1, d), F32)] if then is not None else [])
    return _call(body, name, out_shape, (rows // tr,), in_specs, [row, vec] * (len(out_shape) // 2), ("arbitrary",),
                 ins, comm=comm)


def _final_loss(h, br, w_post, tgt, name):
    rows, d = h.shape
    tr = _row_tile(rows)

    def body(h_ref, br_ref, w_ref, t_ref, dh_ref, loss_ref, dbr_ref, dw_ref):
        i = pl.program_id(0)
        brv, wv = br_ref[...], w_ref[...]
        y = h_ref[...] + _rms(brv, wv)
        r = i * tr + lax.broadcasted_iota(jnp.int32, (tr, 1), 0)
        real = (r >= N_META) & (r < N_META + SEQ)
        diff = jnp.where(real, y - t_ref[...], 0.0)
        dh = diff * (1.0 / d)
        dh_ref[...] = dh
        dbr, dw = _rms_bwd(brv, wv, dh)
        dbr_ref[...] = dbr.astype(dbr_ref.dtype)

        @pl.when(i == 0)
        def _():
            loss_ref[...] = jnp.zeros_like(loss_ref)
            dw_ref[...] = jnp.zeros_like(dw_ref)

        loss_ref[...] += jnp.sum(diff * diff) * (0.5 / d)
        dw_ref[...] += dw

    row = pl.BlockSpec((tr, d), lambda i: (i, 0))
    vec = pl.BlockSpec((1, d), lambda i: (0, 0))
    return pl.pallas_call(body, name=name,
                          out_shape=[S((rows, d), F32), S((1, LANE), F32), S((rows, d), BF16), S((1, d), F32)],
                          grid=(rows // tr,), in_specs=[row, row, vec, row],
                          out_specs=[row, pl.BlockSpec((1, LANE), lambda i: (0, 0)), row, vec],
                          compiler_params=_cp(("arbitrary",)))(h, br, w_post, tgt)


def _gatenorm_fwd(y, zx, w, name, comm=None):
    rows, d = y.shape
    tr = _row_tile(rows)

    def body(y_ref, z_ref, w_ref, o_ref):
        z = z_ref[...]
        o_ref[...] = _rms(y_ref[...] * z * _sigmoid(z), w_ref[...]).astype(o_ref.dtype)

    row = pl.BlockSpec((tr, d), lambda i: (i, 0))
    return _call(body, name, S((rows, d), BF16), (rows // tr,), [row, row, pl.BlockSpec((1, d), lambda i: (0, 0))],
                 row, ("parallel",), (y, zx, w), comm=comm)


def _gatenorm_bwd(y, zx, w, dyn, name, comm=None):
    rows, d = y.shape
    tr = _row_tile(rows)

    def body(y_ref, z_ref, w_ref, dyn_ref, dy_ref, dz_ref, dw_ref):
        yv, z = y_ref[...], z_ref[...]
        sg = _sigmoid(z)
        sz = z * sg
        g = yv * sz
        r = lax.rsqrt(jnp.mean(g * g, axis=-1, keepdims=True) + RMS_EPS)
        dyn_v = dyn_ref[...]
        wdy = dyn_v * w_ref[...]
        dg = r * wdy - g * (r * r * r) * jnp.mean(g * wdy, axis=-1, keepdims=True)
        dy_ref[...] = dg * sz
        dz_ref[...] = (dg * yv * sg * (1.0 + z * (1.0 - sg))).astype(dz_ref.dtype)

        @pl.when(pl.program_id(0) == 0)
        def _():
            dw_ref[...] = jnp.zeros_like(dw_ref)

        dw_ref[...] += jnp.sum(dyn_v * g * r, axis=0, keepdims=True)

    row = pl.BlockSpec((tr, d), lambda i: (i, 0))
    vec = pl.BlockSpec((1, d), lambda i: (0, 0))
    return _call(body, name, [S((rows, d), F32), S((rows, d), BF16), S((1, d), F32)], (rows // tr,),
                 [row, row, vec, row], [row, row, vec], ("arbitrary",), (y, zx, w, dyn), comm=comm)


def _shift_down(x, s, rows_iota):
    if s == 0:
        return x
    return jnp.where(rows_iota >= s, pltpu.roll(x, s, 0), 0.0)


def _shift_up(x, s, rows_iota):
    if s == 0:
        return x
    rows = x.shape[0]
    return jnp.where(rows_iota < rows - s, pltpu.roll(x, rows - s, 0), 0.0)


def _r16(v):
    return v.astype(BF16).astype(F32)


def _conv_taps(x, taps, rows_iota):
    x = _r16(x)
    return [_shift_down(x, taps - 1 - k, rows_iota) for k in range(taps)]


def _conv(x, w_ref, b_ref, taps, rows_iota, shifted=None):
    shifted = _conv_taps(x, taps, rows_iota) if shifted is None else shifted
    acc = jnp.zeros_like(shifted[0])
    for k in range(taps):
        acc = acc + _r16(w_ref[k:k + 1, :]) * shifted[k]
    return acc + b_ref[...]


def _conv_bwd(shifted, du, w_ref, dw_ref, db_ref, taps, rows_iota):
    db_ref[...] = jnp.sum(du, axis=0, keepdims=True)
    du = _r16(du)
    dx = jnp.zeros_like(du)
    for k in range(taps):
        dx = dx + _r16(w_ref[k:k + 1, :]) * _shift_up(du, taps - 1 - k, rows_iota)
        dw_ref[k:k + 1, :] = jnp.sum(du * shifted[k], axis=0, keepdims=True)
    return dx


def _conv_silu_fwd(zx, w, b, name, comm=None):
    rows = zx.shape[0]
    cb = 512
    off = D_INNER // cb

    def body(x_ref, w_ref, b_ref, o_ref):
        it = lax.broadcasted_iota(jnp.int32, (rows, 1), 0)
        u = _conv(x_ref[...], w_ref, b_ref, SSM_CONV, it)
        o_ref[...] = u * _sigmoid(u)

    return _call(
        body, name, S((rows, D_XBC), F32), (D_XBC // cb,),
        [pl.BlockSpec((rows, cb), lambda j: (0, off + j)), pl.BlockSpec((SSM_CONV, cb), lambda j: (0, j)),
         pl.BlockSpec((1, cb), lambda j: (0, j))],
        pl.BlockSpec((rows, cb), lambda j: (0, j)), ("parallel",), (zx, w, b), comm=comm)


def _conv_silu_bwd(zx, dxs, dbm, dcm, w, b, name, comm=None):
    rows = zx.shape[0]
    cb = 256
    off = D_INNER // cb
    nx, nbc = D_INNER // cb, D_BC // cb

    def body(x_ref, dx_in, db_in, dc_in, w_ref, b_ref, dx_ref, dw_ref, db_ref, dbuf):
        j = pl.program_id(0)
        for cond, src in ((j < nx, dx_in), ((j >= nx) & (j < nx + nbc), db_in), (j >= nx + nbc, dc_in)):
            @pl.when(cond)
            def _(src=src):
                dbuf[...] = src[...]
        it = lax.broadcasted_iota(jnp.int32, (rows, 1), 0)
        xs = _conv_taps(x_ref[...], SSM_CONV, it)
        u = _conv(None, w_ref, b_ref, SSM_CONV, it, xs)
        sg = _sigmoid(u)
        du = dbuf[...] * sg * (1.0 + u * (1.0 - sg))
        dx_ref[...] = _conv_bwd(xs, du, w_ref, dw_ref, db_ref, SSM_CONV, it).astype(dx_ref.dtype)

    def part(first, count):
        return pl.BlockSpec((rows, cb), lambda j: (0, jnp.clip(j - first, 0, count - 1)))

    col = pl.BlockSpec((rows, cb), lambda j: (0, j))
    wsp = pl.BlockSpec((SSM_CONV, cb), lambda j: (0, j))
    bsp = pl.BlockSpec((1, cb), lambda j: (0, j))
    return _call(
        body, name, [S((rows, D_XBC), BF16), S((SSM_CONV, D_XBC), F32), S((1, D_XBC), F32)], (D_XBC // cb,),
        [pl.BlockSpec((rows, cb), lambda j: (0, off + j)), part(0, nx), part(nx, nbc), part(nx + nbc, nbc), wsp, bsp],
        [col, wsp, bsp], ("arbitrary",), (zx, dxs, dbm, dcm, w, b), scratch=[pltpu.VMEM((rows, cb), F32)], comm=comm)


def _ffn_act_fwd(u, w, b, name, comm=None):
    rows = u.shape[0]
    cb = 256
    nb = D_FF // cb

    def body(g_ref, v_ref, wg_ref, wv_ref, bg_ref, bv_ref, o_ref):
        it = lax.broadcasted_iota(jnp.int32, (rows, 1), 0)
        g = _conv(g_ref[...], wg_ref, bg_ref, FFN_CONV, it)
        v = _conv(v_ref[...], wv_ref, bv_ref, FFN_CONV, it)
        o_ref[...] = (g * _sigmoid(g) * v).astype(o_ref.dtype)

    def sp(r, shift):
        return pl.BlockSpec((r, cb), lambda j: (0, shift + j))

    return _call(
        body, name, S((rows, D_FF), BF16), (nb,),
        [sp(rows, 0), sp(rows, nb), sp(FFN_CONV, 0), sp(FFN_CONV, nb), sp(1, 0), sp(1, nb)],
        sp(rows, 0), ("parallel",), (u, u, w, w, b, b), comm=comm)


def _ffn_act_bwd(u, dact, w, b, name, comm=None):
    rows = u.shape[0]
    cb = 256
    nb = D_FF // cb

    def body(g_ref, v_ref, d_ref, wg_ref, wv_ref, bg_ref, bv_ref, du_ref, dw_ref, db_ref):
        it = lax.broadcasted_iota(jnp.int32, (rows, 1), 0)
        xg, xv = _conv_taps(g_ref[...], FFN_CONV, it), _conv_taps(v_ref[...], FFN_CONV, it)
        g = _conv(None, wg_ref, bg_ref, FFN_CONV, it, xg)
        v = _conv(None, wv_ref, bv_ref, FFN_CONV, it, xv)
        sg = _sigmoid(g)
        d = d_ref[...]
        dgate = d * v * sg * (1.0 + g * (1.0 - sg))
        dval = d * g * sg
        du_ref[0] = _conv_bwd(xg, dgate, wg_ref, dw_ref.at[0], db_ref.at[0], FFN_CONV, it).astype(du_ref.dtype)
        du_ref[1] = _conv_bwd(xv, dval, wv_ref, dw_ref.at[1], db_ref.at[1], FFN_CONV, it).astype(du_ref.dtype)

    def sp(r, shift):
        return pl.BlockSpec((r, cb), lambda j: (0, shift + j))

    def both(r):
        return pl.BlockSpec((2, r, cb), lambda j: (0, 0, j))

    return _call(
        body, name, [S((2, rows, D_FF), BF16), S((2, FFN_CONV, D_FF), F32), S((2, 1, D_FF), F32)], (nb,),
        [sp(rows, 0), sp(rows, nb), sp(rows, 0), sp(FFN_CONV, 0), sp(FFN_CONV, nb), sp(1, 0), sp(1, nb)],
        [both(rows), both(FFN_CONV), both(1)], ("parallel",), (u, u, dact, w, w, b, b), comm=comm)


def _ssd_consts(dtp_ref, bias_ref, alog_ref, hg):
    lane = lax.broadcasted_iota(jnp.int32, (1, LANE), 1)
    pre = dtp_ref[...] + bias_ref[...]
    dt = _softplus(pre)
    a_row = jnp.where(lane < hg, -jnp.exp(alog_ref[...]), 0.0)
    ri = lax.broadcasted_iota(jnp.int32, (T, T), 0)
    ci = lax.broadcasted_iota(jnp.int32, (T, T), 1)
    cs = _dot_hi((ri >= ci).astype(F32), dt * a_row)
    return pre, dt, a_row, cs, ri, ci, lane


def _head_rows(src, hg):
    return jnp.concatenate([jnp.broadcast_to(src[k:k + 1, :], (HEAD_P, src.shape[1])) for k in range(hg)], axis=0)


def _ssd_fwd(xbc, zx, bias, alog, dsk, name, comm=None):
    rows = xbc.shape[0]
    nc = rows // T
    hg = SSM_HEADS // SSM_GROUPS
    gw = hg * HEAD_P
    xoff, boff, coff = 0, D_INNER // D_STATE, (D_INNER + D_BC) // D_STATE
    dtoff = (D_INNER + D_XBC) // LANE

    def body(x_ref, b_ref, c_ref, dtp_ref, bias_ref, alog_ref, dsk_ref, y_ref, hst_ref, hs):
        c = pl.program_id(1)

        @pl.when(c == 0)
        def _():
            hs[...] = jnp.zeros_like(hs)

        _, dt, _, cs, ri, ci, _ = _ssd_consts(dtp_ref, bias_ref, alog_ref, hg)
        cst, dtt = cs.T, dt.T
        xt = x_ref[...].T
        bb, cbf = b_ref[...].astype(BF16), c_ref[...].astype(BF16)
        gt = _dot(bb, cbf, _NT)
        causal_t = ci >= ri
        dskv = dsk_ref[...]
        hall = hs[...]
        hst_ref[0, 0] = hall
        cs8 = cst[0:8, :]
        cl8 = cs8[:, T - 1:T]
        xdt = xt * _head_rows(dtt, hg)
        yo = _head_rows(jnp.exp(cs8), hg) * _dot(hall.astype(BF16), cbf, _NT)
        st = _dot((xdt * _head_rows(jnp.exp(cl8 - cs8), hg)).astype(BF16), bb)
        hs[...] = _head_rows(jnp.exp(cl8), hg) * hall + st
        yds = []
        for k in range(hg):
            sl = slice(k * HEAD_P, (k + 1) * HEAD_P)
            lt = jnp.exp(jnp.where(causal_t, cst[k:k + 1, :] - cs[:, k:k + 1], NEG))
            yds.append(_dot(xdt[sl, :].astype(BF16), (gt * lt).astype(BF16)))
        dsk_r = jnp.concatenate([jnp.broadcast_to(dskv[:, k:k + 1], (HEAD_P, 1)) for k in range(hg)], axis=0)
        y_ref[...] = (jnp.concatenate(yds, axis=0) + yo + dsk_r * xt).T

    vec = pl.BlockSpec((1, LANE), lambda g, c: (0, g))
    return _call(
        body, name, [S((rows, D_INNER), F32), S((nc, SSM_GROUPS, gw, D_STATE), F32)], (SSM_GROUPS, nc),
        [pl.BlockSpec((T, gw), lambda g, c: (c, xoff + g)),
         pl.BlockSpec((T, D_STATE), lambda g, c: (c, boff + g)),
         pl.BlockSpec((T, D_STATE), lambda g, c: (c, coff + g)),
         pl.BlockSpec((T, LANE), lambda g, c: (c, dtoff + g)), vec, vec, vec],
        [pl.BlockSpec((T, gw), lambda g, c: (c, g)), pl.BlockSpec((1, 1, gw, D_STATE), lambda g, c: (c, g, 0, 0))],
        ("parallel", "arbitrary"), (xbc, xbc, xbc, zx, bias, alog, dsk),
        scratch=[pltpu.VMEM((gw, D_STATE), F32)], comm=comm)


def _ssd_bwd(xbc, zx, bias, alog, dsk, dy, hst, name, comm=None):
    rows = xbc.shape[0]
    nc = rows // T
    hg = SSM_HEADS // SSM_GROUPS
    gw = hg * HEAD_P
    boff, coff = D_INNER // D_STATE, (D_INNER + D_BC) // D_STATE
    dtoff = (D_INNER + D_XBC) // LANE

    def body(x_ref, b_ref, c_ref, dtp_ref, bias_ref, alog_ref, dsk_ref, dy_ref, hst_ref,
             dx_ref, db_ref, dc_ref, ddtp_ref, dalog_ref, ddsk_ref, dbias_ref, dhs):
        step = pl.program_id(1)

        @pl.when(step == 0)
        def _():
            dhs[...] = jnp.zeros_like(dhs)
            dalog_ref[...] = jnp.zeros_like(dalog_ref)
            ddsk_ref[...] = jnp.zeros_like(ddsk_ref)
            dbias_ref[...] = jnp.zeros_like(dbias_ref)

        pre, dt, a_row, cs, ri, ci, lane = _ssd_consts(dtp_ref, bias_ref, alog_ref, hg)
        cst, dtt = cs.T, dt.T
        xt, dyt = x_ref[...].T, dy_ref[...].T
        bb, cbf = b_ref[...].astype(BF16), c_ref[...].astype(BF16)
        gt = _dot(bb, cbf, _NT)
        causal_t = ci >= ri
        dskv = dsk_ref[...]
        hall, dhall = hst_ref[0, 0], dhs[...]
        head_row = lax.broadcasted_iota(jnp.int32, (T, 1), 0)
        last_l = lax.broadcasted_iota(jnp.int32, (1, T), 1) == T - 1
        cs8, dt8 = cst[0:8, :], dtt[0:8, :]
        cl8 = cs8[:, T - 1:T]
        e8, wdec8 = jnp.exp(cs8), jnp.exp(cl8 - cs8)
        w8 = wdec8 * dt8
        dt_r, e_r, w_r, ecl_r = _head_rows(dt8, hg), _head_rows(e8, hg), _head_rows(w8, hg), _head_rows(jnp.exp(cl8), hg)
        dsk_r = jnp.concatenate([jnp.broadcast_to(dskv[:, k:k + 1], (HEAD_P, 1)) for k in range(hg)], axis=0)
        hb, dhb = hall.astype(BF16), dhall.astype(BF16)
        xdt = xt * dt_r
        dye = (dyt * e_r).astype(BF16)
        rt = _dot(dhb, bb, _NT)
        yo = e_r * _dot(hb, cbf, _NT)
        dhs[...] = ecl_r * dhall + _dot(dye, cbf)
        dc_acc = _dot(dye, hb, _TN)
        db_acc = _dot((xt * w_r).astype(BF16), dhb, _TN)
        rtx, dyyo, hdh, dyx = rt * xt, dyt * yo, dhall * hall, dyt * xt
        dgt = jnp.zeros((T, T), F32)
        ddt_rows = jnp.zeros((T, T), F32)
        dcs_rows = jnp.zeros((T, T), F32)
        qrow_cols = jnp.zeros((T, LANE), F32)
        ddsk_acc = jnp.zeros((1, LANE), F32)
        dxdts = []
        for k in range(hg):
            sl = slice(k * HEAD_P, (k + 1) * HEAD_P)
            lt = jnp.exp(jnp.where(causal_t, cst[k:k + 1, :] - cs[:, k:k + 1], NEG))
            mpt = gt * lt
            dyb = dyt[sl, :].astype(BF16)
            dxdt = _dot(dyb, mpt.astype(BF16), _NT)
            dmt = _dot(xdt[sl, :].astype(BF16), dyb, _TN)
            dgt = dgt + dmt * lt
            q = dmt * mpt
            q_rows = jnp.sum(q, axis=1, keepdims=True)
            q_cols = jnp.sum(q, axis=0, keepdims=True)
            dxdts.append(dxdt)
            xz = jnp.sum(xt[sl, :] * dxdt, axis=0, keepdims=True)
            dw = jnp.sum(rtx[sl, :], axis=0, keepdims=True)
            wk, wdeck = w8[k:k + 1, :], wdec8[k:k + 1, :]
            dcl = jnp.exp(cl8[k:k + 1, :]) * jnp.sum(hdh[sl, :]) + jnp.sum(dw * wk)
            dcs_r = jnp.sum(dyyo[sl, :], axis=0, keepdims=True) + q_cols - dw * wk + jnp.where(last_l, dcl, 0.0)
            onehot = (lane == k).astype(F32)
            ddt_rows = ddt_rows + jnp.where(head_row == k, xz + dw * wdeck, 0.0)
            dcs_rows = dcs_rows + jnp.where(head_row == k, dcs_r, 0.0)
            qrow_cols = qrow_cols + q_rows * onehot
            ddsk_acc = ddsk_acc + jnp.sum(dyx[sl, :]) * onehot
        dx_ref[...] = (dt_r * jnp.concatenate(dxdts, axis=0) + dsk_r * dyt + rt * w_r).T
        dc_ref[...] = _dot(dgt.T.astype(BF16), bb) + dc_acc
        db_ref[...] = _dot(dgt.astype(BF16), cbf) + db_acc
        da = _dot_hi((ci >= ri).astype(F32), dcs_rows.T - qrow_cols)
        ddtp = (ddt_rows.T + da * a_row) * _sigmoid(pre)
        ddtp = jnp.where(lane < hg, ddtp, 0.0)
        ddtp_ref[...] = ddtp
        dbias_ref[...] += jnp.sum(ddtp, axis=0, keepdims=True)
        dalog_ref[...] += jnp.sum(da * dt, axis=0, keepdims=True) * a_row
        ddsk_ref[...] += ddsk_acc

    def rc(c):
        return nc - 1 - c

    vec = pl.BlockSpec((1, LANE), lambda g, c: (0, g))
    xsp = pl.BlockSpec((T, gw), lambda g, c: (rc(c), g))
    return _call(
        body, name,
        [S((rows, D_INNER), F32), S((rows, D_BC), F32), S((rows, D_BC), F32),
         S((rows, SSM_GROUPS * LANE), F32), S((1, SSM_GROUPS * LANE), F32),
         S((1, SSM_GROUPS * LANE), F32), S((1, SSM_GROUPS * LANE), F32)],
        (SSM_GROUPS, nc),
        [xsp,
         pl.BlockSpec((T, D_STATE), lambda g, c: (rc(c), boff + g)),
         pl.BlockSpec((T, D_STATE), lambda g, c: (rc(c), coff + g)),
         pl.BlockSpec((T, LANE), lambda g, c: (rc(c), dtoff + g)), vec, vec, vec,
         xsp, pl.BlockSpec((1, 1, gw, D_STATE), lambda g, c: (rc(c), g, 0, 0))],
        [xsp,
         pl.BlockSpec((T, D_STATE), lambda g, c: (rc(c), g)),
         pl.BlockSpec((T, D_STATE), lambda g, c: (rc(c), g)),
         pl.BlockSpec((T, LANE), lambda g, c: (rc(c), g)), vec, vec, vec],
        ("parallel", "arbitrary"), (xbc, xbc, xbc, zx, bias, alog, dsk, dy, hst),
        scratch=[pltpu.VMEM((gw, D_STATE), F32)], comm=comm)


def _attn_tiles(kv_ref, j):
    prev = jnp.maximum(j - 1, 0)
    meta = kv_ref[0:T, :]
    prv = kv_ref[pl.ds(pl.multiple_of(prev * T, T), T), :]
    cur = kv_ref[pl.ds(pl.multiple_of(j * T, T), T), :]
    return jnp.concatenate([meta, prv, cur], axis=0)


def _attn_mask(j):
    r = j * T + lax.broadcasted_iota(jnp.int32, (3 * T, T), 1)
    row = lax.broadcasted_iota(jnp.int32, (3 * T, T), 0)
    t0, t1 = row < T, row < 2 * T
    s = jnp.where(t0, row, (j - 2) * T + row)
    ok = (s <= r) & ((s < N_META) | (s > r - WINDOW))
    use = (t0 & (j >= 2) & (row < N_META)) | (jnp.logical_not(t0) & t1 & (j >= 1)) | jnp.logical_not(t1)
    return ok & use


def _attn_fwd(q, kv, sinks, name, comm=None):
    rows = q.shape[0]
    scale = 1.0 / math.sqrt(ATTN_DH)
    qpk = N_Q_HEADS // N_KV_HEADS

    def body(q_ref, kv_ref, s_ref, o_ref, lse_ref):
        j = pl.program_id(0)
        kv3 = _attn_tiles(kv_ref, j).astype(BF16)
        mask = _attn_mask(j)
        qv = (q_ref[...] * scale).astype(BF16)
        sk = s_ref[...]
        lses = []
        for kh in range(N_KV_HEADS):
            k3 = kv3[:, kh * ATTN_DH:(kh + 1) * ATTN_DH]
            v3 = kv3[:, D_KV + kh * ATTN_DH:D_KV + (kh + 1) * ATTN_DH]
            for g in range(qpk):
                h = kh * qpk + g
                sink = sk[:, h:h + 1]
                sc = jnp.where(mask, _dot(k3, qv[:, h * ATTN_DH:(h + 1) * ATTN_DH], _NT), NEG)
                m = jnp.maximum(jnp.max(sc, axis=0, keepdims=True), sink)
                p = jnp.exp(sc - m)
                den = jnp.sum(p, axis=0, keepdims=True) + jnp.exp(sink - m)
                p = p * (1.0 / den)
                lses.append(m + jnp.log(den))
                o_ref[:, h * ATTN_DH:(h + 1) * ATTN_DH] = _dot(p.astype(BF16), v3, _TN).astype(o_ref.dtype)
        lse_ref[...] = jnp.concatenate(lses, axis=0)

    return _call(
        body, name, [S((rows, D_MODEL), BF16), S((N_Q_HEADS, rows), F32)], (rows // T,),
        [pl.BlockSpec((T, D_MODEL), lambda j: (j, 0)), pl.BlockSpec((rows, 2 * D_KV), lambda j: (0, 0)),
         pl.BlockSpec((1, N_Q_HEADS), lambda j: (0, 0))],
        [pl.BlockSpec((T, D_MODEL), lambda j: (j, 0)), pl.BlockSpec((N_Q_HEADS, T), lambda j: (0, j))],
        ("parallel",), (q, kv, sinks), comm=comm)


def _attn_bwd(q, kv, sinks, do, lse, name, comm=None):
    rows = q.shape[0]
    scale = 1.0 / math.sqrt(ATTN_DH)
    qpk = N_Q_HEADS // N_KV_HEADS

    def body(q_ref, kv_ref, s_ref, do_ref, lse_ref, dq_ref, dkv_ref, ds_ref):
        j = pl.program_id(0)

        @pl.when(j == 0)
        def _():
            dkv_ref[...] = jnp.zeros_like(dkv_ref)
            ds_ref[...] = jnp.zeros_like(ds_ref)

        kv3 = _attn_tiles(kv_ref, j).astype(BF16)
        mask = _attn_mask(j)
        qv = (q_ref[...] * scale).astype(BF16)
        dov = do_ref[...].astype(BF16)
        sk = s_ref[...]
        lsev = lse_ref[...]
        lane = lax.broadcasted_iota(jnp.int32, (1, LANE), 1)
        ds_acc = jnp.zeros((1, LANE), F32)
        prev = jnp.maximum(j - 1, 0)
        mask4 = jnp.concatenate([mask] * qpk, axis=1)
        dqts = []
        for kh in range(N_KV_HEADS):
            ksl = slice(kh * ATTN_DH, (kh + 1) * ATTN_DH)
            vsl = slice(D_KV + kh * ATTN_DH, D_KV + (kh + 1) * ATTN_DH)
            k3, v3 = kv3[:, ksl], kv3[:, vsl]
            heads = [kh * qpk + g for g in range(qpk)]
            q4 = jnp.concatenate([qv[:, h * ATTN_DH:(h + 1) * ATTN_DH] for h in heads], axis=0)
            do4 = jnp.concatenate([dov[:, h * ATTN_DH:(h + 1) * ATTN_DH] for h in heads], axis=0)
            lse4 = jnp.concatenate([lsev[h:h + 1, :] for h in heads], axis=1)
            sink4 = jnp.concatenate([jnp.broadcast_to(sk[:, h:h + 1], (1, T)) for h in heads], axis=1)
            p = jnp.exp(jnp.where(mask4, _dot(k3, q4, _NT), NEG) - lse4)
            ps = jnp.exp(sink4 - lse4)
            dp = _dot(v3, do4, _NT)
            delta = jnp.sum(p * dp, axis=0, keepdims=True)
            dsc = (p * (dp - delta)).astype(BF16)
            dq4 = _dot(k3.T, dsc) * scale
            dk3 = _dot(dsc, q4)
            dv3 = _dot(p.astype(BF16), do4)
            psd = ps * delta
            for g, h in enumerate(heads):
                dqts.append(dq4[:, g * T:(g + 1) * T])
                ds_acc = ds_acc - jnp.sum(psd[:, g * T:(g + 1) * T]) * (lane == h).astype(F32)
            for t, start in enumerate((0, pl.multiple_of(prev * T, T), pl.multiple_of(j * T, T))):
                rsl = pl.ds(start, T)
                dkv_ref[rsl, ksl] += dk3[t * T:(t + 1) * T, :]
                dkv_ref[rsl, vsl] += dv3[t * T:(t + 1) * T, :]
        ds_ref[...] += ds_acc
        dq_ref[...] = jnp.concatenate(dqts, axis=0).T.astype(dq_ref.dtype)

    blk = pl.BlockSpec((T, D_MODEL), lambda j: (j, 0))
    full = pl.BlockSpec((rows, 2 * D_KV), lambda j: (0, 0))
    return _call(
        body, name, [S((rows, D_MODEL), BF16), S((rows, 2 * D_KV), F32), S((1, LANE), F32)], (rows // T,),
        [blk, full, pl.BlockSpec((1, N_Q_HEADS), lambda j: (0, 0)), blk, pl.BlockSpec((N_Q_HEADS, T), lambda j: (0, j))],
        [blk, full, pl.BlockSpec((1, LANE), lambda j: (0, 0))], ("arbitrary",), (q, kv, sinks, do, lse), comm=comm)


BLOCK_BYTES = 1 << 20


def _div_tile(rows, cols):
    cap = max(16, BLOCK_BYTES // (4 * cols))
    best = None
    for t in range(16, min(rows, cap) + 1, 16):
        if rows % t == 0:
            best = t
    return best if best is not None else rows


def _adamw(parts, w, m, v, name, comm=None):
    layers, rows, cols = w.shape
    n = parts[0].shape[0]
    tr = _div_tile(rows, cols)
    tc = _pick(cols, 256) if tr == rows and rows * cols * 4 > 2 * BLOCK_BYTES else cols
    c1 = 1.0 / (1.0 - B1 ** STEP)
    c2 = 1.0 / (1.0 - B2 ** STEP)

    def body(*refs):
        p_refs = refs[:layers]
        w_ref, m_ref, v_ref, g_ref, d_ref, nm_ref, nv_ref = refs[layers:]
        layer = pl.program_id(0)
        for l in range(layers):
            @pl.when(layer == l)
            def _(p_ref=p_refs[l]):
                g = p_ref[0].astype(F32)
                for i in range(1, n):
                    g = g + p_ref[i].astype(F32)
                nm = B1 * m_ref[...] + (1.0 - B1) * g
                nv = B2 * v_ref[...] + (1.0 - B2) * (g * g)
                g_ref[...] = g
                nm_ref[...] = nm
                nv_ref[...] = nv
                d_ref[...] = -LR * ((nm * c1) / (jnp.sqrt(nv * c2) + EPS) + WD * w_ref[...])

    def part_spec(l):
        return pl.BlockSpec((n, tr, tc), lambda k, i, j: (0, jnp.where(k == l, i, 0), jnp.where(k == l, j, 0)))

    row = pl.BlockSpec((None, tr, tc), lambda k, i, j: (k, i, j))
    return _call(body, name, [S((layers, rows, cols), F32)] * 4, (layers, rows // tr, cols // tc),
                 [part_spec(l) for l in range(layers)] + [row, row, row], [row] * 4,
                 ("parallel", "parallel", "parallel"), (*parts, w, m, v), comm=comm)


def _sum_parts(parts, name):
    n, rows, cols = parts[0].shape
    nb = len(parts)
    tr = _div_tile(rows, cols)

    def body(*refs):
        o_ref = refs[nb]
        blk = pl.program_id(0)
        for l in range(nb):
            @pl.when(blk == l)
            def _(p_ref=refs[l]):
                g = p_ref[0].astype(F32)
                for i in range(1, n):
                    g = g + p_ref[i].astype(F32)
                o_ref[...] = g

    def part_spec(l):
        return pl.BlockSpec((n, tr, cols), lambda k, i: (0, jnp.where(k == l, i, 0), 0))

    per = rows // tr
    return pl.pallas_call(body, name=name, out_shape=S((nb * rows, cols), F32), grid=(nb, per),
                          in_specs=[part_spec(l) for l in range(nb)],
                          out_specs=pl.BlockSpec((tr, cols), lambda k, i: (k * per + i, 0)),
                          compiler_params=_cp(("parallel", "parallel")))(*parts)


def _col_segments(ws, runs):
    segs = []
    for glo, mlo, n in runs:
        while n > 0:
            d, off = divmod(glo, ws)
            take = min(n, ws - off)
            segs.append((d, off, mlo, take))
            glo, mlo, n = glo + take, mlo + take, n - take
    return segs


def _assemble_cols(gs, width, segs, name):
    _, rows, ws = gs[0].shape
    nb = len(gs)
    rb = _div_tile(rows, width // 2)
    per = rows // rb

    def body(*refs):
        o_ref = refs[nb]
        piece = pl.program_id(0)
        for l in range(nb):
            @pl.when(piece == l)
            def _(g_ref=refs[l]):
                o_ref[...] = jnp.zeros_like(o_ref)
                for d, off, mlo, n in segs:
                    o_ref[:, mlo:mlo + n] = g_ref[d, :, off:off + n]

    def piece_spec(l):
        return pl.BlockSpec((N_DEV, rb, ws), lambda k, i: (0, jnp.where(k == l, i, 0), 0))

    return pl.pallas_call(
        body, name=name, out_shape=S((nb * rows, width), gs[0].dtype), grid=(nb, per),
        in_specs=[piece_spec(l) for l in range(nb)],
        out_specs=pl.BlockSpec((rb, width), lambda k, i: (k * per + i, 0)),
        compiler_params=_cp(("parallel", "parallel")))(*gs)


def _scatter_cols(dw, ws, segs, name):
    rows, width = dw.shape
    rb = _div_tile(rows, width)

    def body(w_ref, o_ref):
        for d, off, mlo, n in segs:
            o_ref[d, :, off:off + n] = w_ref[:, mlo:mlo + n].astype(o_ref.dtype)

    return pl.pallas_call(
        body, name=name, out_shape=S((N_DEV, rows, ws), BF16), grid=(rows // rb,),
        in_specs=[pl.BlockSpec((rb, width), lambda i: (i, 0))],
        out_specs=pl.BlockSpec((N_DEV, rb, ws), lambda i: (0, i, 0)), compiler_params=_cp(("parallel",)))(dw)


def _gather_comm(xs):
    n = len(xs)

    def setup(x_refs, out_refs, sems):
        send_sems, recv_sems, local_sems = sems
        mx, my, mc = lax.axis_index("x"), lax.axis_index("y"), lax.axis_index("c")
        me, sibling = (mx, my, mc), (mx, my, 1 - mc)
        chips = [(1 - mx, my), (mx, 1 - my), (1 - mx, 1 - my)]

        def blk(a, px, py, pc):
            return out_refs[a].at[4 * px + 2 * py + pc]

        def copy(a, k, block, to, src=None):
            return pltpu.make_async_remote_copy(
                src_ref=blk(a, *block) if src is None else src, dst_ref=blk(a, *block),
                send_sem=send_sems.at[a, k], recv_sem=recv_sems.at[a, k], device_id=to, device_id_type=_MESH)

        mine = [pltpu.make_async_copy(x_refs[a], blk(a, *me), local_sems.at[a]) for a in range(n)]
        own = []
        for a in range(n):
            own.append(copy(a, 0, me, sibling, src=x_refs[a]))
            own += [copy(a, 1 + i, me, (*chip, mc), src=x_refs[a]) for i, chip in enumerate(chips)]
        return me, sibling, chips, mc, copy, mine, own

    def first(x_refs, out_refs, sems):
        _, _, _, _, _, mine, own = setup(x_refs, out_refs, sems)
        for cp in mine + own:
            cp.start()

    def last(x_refs, out_refs, sems):
        me, sibling, chips, mc, copy, mine, own = setup(x_refs, out_refs, sems)
        passed = []
        for a in range(n):
            for i, chip in enumerate(chips):
                copy(a, 1 + i, (*chip, mc), me).wait_recv()
                passed.append(copy(a, 4 + i, (*chip, mc), sibling))
                passed[-1].start()
        for a in range(n):
            copy(a, 0, sibling, me).wait_recv()
            for i, chip in enumerate(chips):
                copy(a, 4 + i, (*chip, 1 - mc), me).wait_recv()
        for cp in own + passed:
            cp.wait_send()
        for cp in mine:
            cp.wait()

    return _Comm(list(xs), [S((N_DEV,) + x.shape, x.dtype) for x in xs],
                 [pltpu.SemaphoreType.DMA((n, 7)), pltpu.SemaphoreType.DMA((n, 7)), pltpu.SemaphoreType.DMA((n,))],
                 first, last)


def _swap_comm(gs):
    n = len(gs)

    def copies(g_refs, out_refs, sems):
        send_sems, recv_sems = sems
        mx, my, mc = lax.axis_index("x"), lax.axis_index("y"), lax.axis_index("c")
        return [pltpu.make_async_remote_copy(
            src_ref=g_refs[a].at[2 * k + 1 - mc], dst_ref=out_refs[a].at[k], send_sem=send_sems.at[a, k],
            recv_sem=recv_sems.at[a, k], device_id=(mx, my, 1 - mc), device_id_type=_MESH)
            for a in range(n) for k in range(4)]

    def first(g_refs, out_refs, sems):
        for cp in copies(g_refs, out_refs, sems):
            cp.start()

    def last(g_refs, out_refs, sems):
        for cp in copies(g_refs, out_refs, sems):
            cp.wait()

    return _Comm(list(gs), [S((4,) + g.shape[1:], g.dtype) for g in gs],
                 [pltpu.SemaphoreType.DMA((n, 4)), pltpu.SemaphoreType.DMA((n, 4))], first, last)


def _chips_comm(parts):
    n = len(parts)

    def copies(p_refs, out_refs, sems):
        send_sems, recv_sems, local_sems = sems
        mx, my, mc = lax.axis_index("x"), lax.axis_index("y"), lax.axis_index("c")
        mychip = 2 * mx + my
        chips = [(1 - mx, my), (mx, 1 - my), (1 - mx, 1 - my)]
        mine = [pltpu.make_async_copy(p_refs[a].at[mychip], out_refs[a].at[mychip], local_sems.at[a])
                for a in range(n)]
        return mine + [pltpu.make_async_remote_copy(
            src_ref=p_refs[a].at[2 * cx + cy], dst_ref=out_refs[a].at[mychip], send_sem=send_sems.at[a, i],
            recv_sem=recv_sems.at[a, i], device_id=(cx, cy, mc), device_id_type=_MESH)
            for a in range(n) for i, (cx, cy) in enumerate(chips)]

    def first(p_refs, out_refs, sems):
        for cp in copies(p_refs, out_refs, sems):
            cp.start()

    def last(p_refs, out_refs, sems):
        for cp in copies(p_refs, out_refs, sems):
            cp.wait()

    return _Comm(list(parts), [S(p.shape, p.dtype) for p in parts],
                 [pltpu.SemaphoreType.DMA((n, 3)), pltpu.SemaphoreType.DMA((n, 3)), pltpu.SemaphoreType.DMA((n,))],
                 first, last)


def _join_comms(comms):
    def split(refs, counts):
        out, p = [], 0
        for cnt in counts:
            out.append(refs[p:p + cnt])
            p += cnt
        return out

    ni = [len(c.ins) for c in comms]
    no = [len(c.out_shapes) for c in comms]
    ns = [len(c.scratch) for c in comms]

    def first(in_refs, out_refs, sems):
        for c, i, o, s in zip(comms, split(in_refs, ni), split(out_refs, no), split(sems, ns)):
            c.first(i, o, s)

    def last(in_refs, out_refs, sems):
        for c, i, o, s in zip(comms, split(in_refs, ni), split(out_refs, no), split(sems, ns)):
            c.last(i, o, s)

    return _Comm([x for c in comms for x in c.ins], [x for c in comms for x in c.out_shapes],
                 [x for c in comms for x in c.scratch], first, last)


def _add_pairs(mine, theirs, core, name):
    _, rows, cols = mine.shape
    tr = _div_tile(rows, cols)

    def body(core_ref, a_ref, b_ref, o_ref):
        o_ref[...] = (a_ref[...].astype(F32) + b_ref[...].astype(F32)).astype(o_ref.dtype)

    return pl.pallas_call(
        body, name=name, out_shape=S((4, rows, cols), BF16),
        grid_spec=pltpu.PrefetchScalarGridSpec(
            num_scalar_prefetch=1, grid=(4, rows // tr),
            in_specs=[pl.BlockSpec((None, tr, cols), lambda k, i, c: (2 * k + c[0], i, 0)),
                      pl.BlockSpec((None, tr, cols), lambda k, i, c: (k, i, 0))],
            out_specs=pl.BlockSpec((None, tr, cols), lambda k, i, c: (k, i, 0))),
        compiler_params=_cp(("parallel", "parallel")))(core, mine, theirs)


def _run_comm(comm, name):
    ci, co = len(comm.ins), len(comm.out_shapes)

    def body(*refs):
        comm.first(refs[:ci], refs[ci:ci + co], refs[ci + co:])
        comm.last(refs[:ci], refs[ci:ci + co], refs[ci + co:])

    return pl.pallas_call(body, name=name, out_shape=list(comm.out_shapes), in_specs=[_HBM] * ci,
                          out_specs=[_HBM] * co, scratch_shapes=list(comm.scratch))(*comm.ins)


def _flat_rows(n_elems, mult):
    rows = -(-n_elems // LANE)
    return -(-rows // mult) * mult


def _pack(arrs, lead, mult, dtype):
    lead_shape = arrs[0].shape[:lead]
    flat = jnp.concatenate([a.astype(dtype).reshape(lead_shape + (-1,)) for a in arrs], axis=-1)
    n = flat.shape[-1]
    rows = _flat_rows(n, mult)
    flat = jnp.pad(flat, [(0, 0)] * lead + [(0, rows * LANE - n)])
    return flat.reshape(lead_shape + (rows, LANE))


def _unpack(flat, lead, shapes):
    lead_shape = flat.shape[:lead]
    flat = flat.reshape(lead_shape + (-1,))
    out, off = [], 0
    for shp in shapes:
        n = math.prod(shp)
        out.append(flat[..., off:off + n].reshape(lead_shape + tuple(shp)))
        off += n
    return out


def _split8(full, ax, n):
    shp = full.shape
    return jnp.moveaxis(full.reshape(shp[:ax] + (N_DEV, n) + shp[ax + 1:]), ax, 0)


def _join8(g, ax):
    shp = g.shape[1:]
    return jnp.moveaxis(g, 0, ax).reshape(shp[:ax] + (N_DEV * shp[ax],) + shp[ax + 1:])


def _group_lanes(v, hg):
    v = v.reshape(SSM_GROUPS, hg)
    return jnp.pad(v, ((0, 0), (0, LANE - hg))).reshape(1, SSM_GROUPS * LANE)


def _ungroup_lanes(v, hg):
    return v.reshape(SSM_GROUPS, LANE)[:, :hg].reshape(1, SSM_GROUPS * hg)


def kernel(x, meta_tokens, a_norm_pre, a_w_in, a_conv_w, a_conv_b, a_dt_bias, a_a_log, a_d_skip, a_gate_norm, a_w_out, a_norm_post, kv_norm, w_kv, b_norm_pre, b_w_q, b_sinks, b_w_o, b_norm_post, f_norm_pre, f_w_up, f_conv_w, f_conv_b, f_w_down, f_norm_post, loss_target, m_meta_tokens, m_a_norm_pre, m_a_w_in, m_a_conv_w, m_a_conv_b, m_a_dt_bias, m_a_a_log, m_a_d_skip, m_a_gate_norm, m_a_w_out, m_a_norm_post, m_kv_norm, m_w_kv, m_b_norm_pre, m_b_w_q, m_b_sinks, m_b_w_o, m_b_norm_post, m_f_norm_pre, m_f_w_up, m_f_conv_w, m_f_conv_b, m_f_w_down, m_f_norm_post, v_meta_tokens, v_a_norm_pre, v_a_w_in, v_a_conv_w, v_a_conv_b, v_a_dt_bias, v_a_a_log, v_a_d_skip, v_a_gate_norm, v_a_w_out, v_a_norm_post, v_kv_norm, v_w_kv, v_b_norm_pre, v_b_w_q, v_b_sinks, v_b_w_o, v_b_norm_post, v_f_norm_pre, v_f_w_up, v_f_conv_w, v_f_conv_b, v_f_w_down, v_f_norm_post):
    args = locals()
    wts = {n: args[n] for n in WEIGHTS}
    mom = {n: args["m_" + n] for n in WEIGHTS}
    var = {n: args["v_" + n] for n in WEIGHTS}
    mx, my, mc = lax.axis_index("x"), lax.axis_index("y"), lax.axis_index("c")
    me = 4 * mx + 2 * my + mc
    rows = _seq_rows()
    hg = SSM_HEADS // SSM_GROUPS
    d = D_MODEL

    n_main = D_INNER + D_XBC
    ws_in, ws_up = a_w_in.shape[2], f_w_up.shape[2]
    segs_in = _col_segments(ws_in, [(0, 0, n_main)] + [(n_main + hg * g, n_main + LANE * g, hg)
                                                      for g in range(SSM_GROUPS)])
    segs_up = _col_segments(ws_up, [(0, 0, 2 * D_FF)])
    def gather_of(*ws):
        return _gather_comm([w.astype(BF16) for w in ws])

    small_full, = _run_comm(_gather_comm([_pack([wts[n] for n in SMALL], 0, 8, F32)]), "gather_small")
    full = {}
    for n, g in zip(SMALL, _unpack(small_full, 1, [wts[n].shape for n in SMALL])):
        full[n] = _join8(g, SHARD_AXIS[n])
    (h0, hn0), (g_in,) = _embed_norm(full["meta_tokens"], x[0], full["a_norm_pre"], rows, "embed_norm",
                                     comm=gather_of(a_w_in[0]))
    w_in_all = _assemble_cols([g_in], n_main + SSM_GROUPS * LANE, segs_in, "asm_w_in")
    w_up, w_down = [None, None], [None, None]
    bias_g = _group_lanes(wts["a_dt_bias"], hg)
    alog_g = _group_lanes(wts["a_a_log"], hg)
    dsk_g = _group_lanes(wts["a_d_skip"], hg)
    a_conv_w, a_conv_b = full["a_conv_w"][0], full["a_conv_b"]
    f_cw, f_cb = full["f_conv_w"], wts["f_conv_b"]
    fpre, fpost = wts["f_norm_pre"], wts["f_norm_post"]

    tgt = jnp.pad(loss_target[0], ((N_META, rows - N_META - SEQ), (0, 0)))

    zx, (g_out,) = _mm(hn0, w_in_all, "nn", F32, "mm_in", comm=gather_of(a_w_out[0]))
    w_out = g_out.reshape(D_INNER, d)
    xbc = _conv_silu_fwd(zx, a_conv_w, a_conv_b, "conv_a")
    (y_ssd, hst), (g_up0,) = _ssd_fwd(xbc, zx, bias_g, alog_g, dsk_g, "ssd_fwd", comm=gather_of(f_w_up[0]))
    w_up[0] = _assemble_cols([g_up0], 2 * D_FF, segs_up, "asm_w_up0")
    yn = _gatenorm_fwd(y_ssd, zx, full["a_gate_norm"], "gatenorm")
    mix_a, (g_o,) = _mm(yn, w_out, "nn", F32, "mm_out", comm=gather_of(b_w_o[0]))
    h1, (fn0,) = _resid_norm(h0, mix_a, full["a_norm_post"], [fpre[0:1]], "resid_a")

    half = d // 2
    u0, (g_dn0,) = _mm(fn0, w_up[0], "nn", F32, "mm_up0", comm=gather_of(f_w_down[0]))
    act0, (g_up1a,) = _ffn_act_fwd(u0, f_cw[0], f_cb[0:1], "ffn_act0", comm=gather_of(f_w_up[1, :half]))
    ffn0, (g_kv, g_q) = _mm(act0, g_dn0.reshape(D_FF, d), "nn", F32, "mm_down0", comm=gather_of(w_kv, b_w_q[0]))
    w_kvf, w_q, w_o = g_kv.reshape(d, 2 * D_KV), g_q.reshape(d, d), g_o.reshape(d, d)
    h2, (kvn, bn) = _resid_norm(h1, ffn0, fpost[0:1], [wts["kv_norm"].reshape(1, d), wts["b_norm_pre"]], "resid_f0")
    kv = _mm(kvn, w_kvf, "nn", F32, "mm_kv")
    q = _mm(bn, w_q, "nn", F32, "mm_q")
    (o, lse), (g_up1b,) = _attn_fwd(q, kv, wts["b_sinks"], "attn_fwd", comm=gather_of(f_w_up[1, half:]))
    w_up[1] = _assemble_cols([g_up1a, g_up1b], 2 * D_FF, segs_up, "asm_w_up1")
    mix_b = _mm(o, w_o, "nn", F32, "mm_o")
    h3, (fn1,) = _resid_norm(h2, mix_b, wts["b_norm_post"], [fpre[1:2]], "resid_b")
    u1, (g_dn1,) = _mm(fn1, w_up[1], "nn", F32, "mm_up1", comm=gather_of(f_w_down[1]))
    w_down = [g_dn0.reshape(D_FF, d), g_dn1.reshape(D_FF, d)]
    act1 = _ffn_act_fwd(u1, f_cw[1], f_cb[1:2], "ffn_act1")
    ffn1 = _mm(act1, w_down[1], "nn", F32, "mm_down1")
    dh4, loss_row, dffn1, dw_post1 = _final_loss(h3, ffn1, fpost[1:2], tgt, "loss")
    loss = lax.psum(loss_row[0, 0], ("x", "y", "c"))

    grads = {}

    core = mc.astype(jnp.int32).reshape(1)

    def carried(res, comm):
        return res if comm is not None else (res, None)

    def ffn_bwd(dh_out, dffn, h_in, fn, u, act, i, then, c_dact=None, c_dwdown=None, c_dwup=None, c_dfn=None):
        dact, got_a = carried(_mm(dffn, w_down[i], "nt", F32, f"mm_dact{i}", comm=c_dact), c_dact)
        dw_down, got_b = carried(_mm(act, dffn, "tn", BF16, f"mm_dwdown{i}", comm=c_dwdown), c_dwdown)
        dw_down = dw_down.reshape(N_DEV, -1, d)
        du, dwc, dbc = _ffn_act_bwd(u, dact, f_cw[i], f_cb[i:i + 1], f"ffn_act_bwd{i}")
        dfn, (s_dn, *got_d) = _mm(du, w_up[i], "nt", F32, f"mm_dfn{i}", comm=_join_comms(
            [_swap_comm([dw_down])] + ([c_dfn] if c_dfn is not None else [])))
        sum_dn = _add_pairs(dw_down, s_dn, core, f"rs_add_dn{i}")
        dw_up, got_c = carried(_mm(fn, du, "tn", BF16, f"mm_dwup{i}", comm=c_dwup, shard_cols=ws_up), c_dwup)
        (dh_in, dw_pre, dbranch, dw_branch), (s_up,) = _norm_bwd(
            h_in, fpre[i:i + 1], dfn, dh_out, F32, f"nb_fpre{i}", comm=_swap_comm([dw_up]), then=then)
        sum_up = _add_pairs(dw_up, s_up, core, f"rs_add_up{i}")
        return dh_in, dbranch, dw_branch, dict(sum_down=sum_dn, cw=jnp.concatenate([dwc[0], dwc[1]], axis=1),
                                               cb=jnp.concatenate([dbc[0], dbc[1]], axis=1), sum_up=sum_up,
                                               pre=dw_pre), got_a, got_b, got_c, got_d

    dh3, dmix_b, grads["b_norm_post"], gf1, _, _, _, _ = ffn_bwd(dh4, dffn1, h3, fn1, u1, act1, 1,
                                                                 (mix_b, wts["b_norm_post"]))
    do = _mm(dmix_b, w_o, "nt", F32, "mm_do")
    dw_o = _mm(o, dmix_b, "tn", BF16, "mm_dwo").reshape(N_DEV, -1, d)
    half_up = gf1["sum_up"].shape[1] // 2
    (dq, dkv, dsinks), (p_up1a, s_o) = _attn_bwd(
        q, kv, wts["b_sinks"], do, lse, "attn_bwd",
        comm=_join_comms([_chips_comm([gf1["sum_up"][:, :half_up]]), _swap_comm([dw_o])]))
    sum_o = _add_pairs(dw_o, s_o, core, "rs_add_o")
    grads["b_sinks"] = dsinks[:, :N_Q_HEADS]
    dbn = _mm(dq, w_q, "nt", F32, "mm_dbn")
    dw_q = _mm(bn, dq, "tn", BF16, "mm_dwq").reshape(N_DEV, -1, d)
    dkv16 = dkv.astype(BF16)
    dkvn = _mm(dkv16, w_kvf, "nt", F32, "mm_dkvn")
    dw_kv = _mm(kvn, dkv16, "tn", BF16, "mm_dwkv").reshape(N_DEV, -1, 2 * D_KV)
    (dh2, grads["b_norm_pre"]), (s_q, s_kv) = _norm_bwd(h2, wts["b_norm_pre"], dbn, dh3, F32, "nb_bpre",
                                                        comm=_swap_comm([dw_q, dw_kv]))
    sum_q, sum_kv = _add_pairs(dw_q, s_q, core, "rs_add_q"), _add_pairs(dw_kv, s_kv, core, "rs_add_kv")
    dh2, dw_kvn, dffn0, dw_post0 = _norm_bwd(h2, wts["kv_norm"].reshape(1, d), dkvn, dh2, F32, "nb_kv",
                                             then=(ffn0, fpost[0:1]))
    grads["kv_norm"] = dw_kvn.reshape(d)
    dh1, dmix_a, grads["a_norm_post"], gf0, (p_o,), (p_q, p_kv), (p_dn1,), (p_up1b,) = ffn_bwd(
        dh2, dffn0, h1, fn0, u0, act0, 0, (mix_a, full["a_norm_post"]), c_dact=_chips_comm([sum_o]),
        c_dwdown=_chips_comm([sum_q, sum_kv]), c_dwup=_chips_comm([gf1["sum_down"]]),
        c_dfn=_chips_comm([gf1["sum_up"][:, half_up:]]))
    p_up1 = jnp.concatenate([p_up1a, p_up1b], axis=1)
    grads["f_norm_post"] = jnp.concatenate([dw_post0, dw_post1], axis=0)
    grads["f_norm_pre"] = jnp.concatenate([gf0["pre"], gf1["pre"]], axis=0)
    grads["f_conv_w"] = jnp.stack([gf0["cw"], gf1["cw"]])
    grads["f_conv_b"] = jnp.concatenate([gf0["cb"], gf1["cb"]], axis=0)

    dyn = _mm(dmix_a, w_out, "nt", F32, "mm_dyn")
    dw_out = _mm(yn, dmix_a, "tn", BF16, "mm_dwout").reshape(N_DEV, -1, d)
    (dy_ssd, dz, grads["a_gate_norm"]), (s_out,) = _gatenorm_bwd(y_ssd, zx, full["a_gate_norm"], dyn, "gatenorm_bwd",
                                                                 comm=_swap_comm([dw_out]))
    sum_out = _add_pairs(dw_out, s_out, core, "rs_add_out")
    (dxs, dbm, dcm, ddtp, dalog, ddsk, dbias), (p_up0,) = _ssd_bwd(
        xbc, zx, bias_g, alog_g, dsk_g, dy_ssd, hst, "ssd_bwd", comm=_chips_comm([gf0["sum_up"]]))
    grads["a_a_log"] = _ungroup_lanes(dalog, hg)
    grads["a_d_skip"] = _ungroup_lanes(ddsk, hg)
    grads["a_dt_bias"] = _ungroup_lanes(dbias, hg)
    dpre, dcw, dcb = _conv_silu_bwd(zx, dxs, dbm, dcm, a_conv_w, a_conv_b, "conv_a_bwd")
    grads["a_conv_w"], grads["a_conv_b"] = dcw[None], dcb
    dzx = jnp.concatenate([dz, dpre, ddtp.astype(BF16)], axis=1)
    dw_in_all, (p_dn0,) = _mm(hn0, dzx, "tn", BF16, "mm_dwin", comm=_chips_comm([gf0["sum_down"]]))
    dw_in8 = _scatter_cols(dw_in_all, ws_in, segs_in, "scat_w_in")
    dhn0, (s_in, p_out) = _mm(dzx, w_in_all, "nt", F32, "mm_dhn0",
                              comm=_join_comms([_swap_comm([dw_in8]), _chips_comm([sum_out])]))
    sum_in = _add_pairs(dw_in8, s_in, core, "rs_add_in")
    half_in = sum_in.shape[1] // 2
    (dh0, grads["a_norm_pre"]), (p_in_a,) = _norm_bwd(h0, full["a_norm_pre"], dhn0, dh1, F32, "nb_apre",
                                                      comm=_chips_comm([sum_in[:, :half_in]]))
    grad_x = dh0[N_META:N_META + SEQ][None]
    grads["meta_tokens"] = dh0[:N_META]

    small_local = _pack([_split8(grads[n], SHARD_AXIS[n], wts[n].shape[SHARD_AXIS[n]]) for n in SMALL], 1, 8, F32)
    repl_local = _pack([grads[n] for n in REPL], 0, 8, F32)
    n_sr = small_local.shape[1]
    small_vec = jnp.concatenate([small_local.reshape(N_DEV * n_sr, LANE), repl_local], axis=0)
    tail = _join_comms([_chips_comm([sum_in[:, half_in:]]), _gather_comm([small_vec])])
    parts_big = dict(a_w_out=[p_out], w_kv=[p_kv], b_w_q=[p_q], b_w_o=[p_o], f_w_down=[p_dn0, p_dn1])

    def flat_f32(dct, names, mult):
        return _pack([dct[n] for n in names], 0, mult, F32)

    def adamw_big(n, comm=None):
        shp3 = (len(parts_big[n]),) + parts_big[n][0].shape[1:]
        res = _adamw(parts_big[n], *[dct[n].reshape(shp3) for dct in (wts, mom, var)], f"adamw_{n}", comm=comm)
        res, got = res if comm is not None else (res, None)
        big_out[n] = [r.reshape(wts[n].shape) for r in res]
        return got

    big_out = {}
    def swap_last(a):
        return jnp.swapaxes(a, -1, -2)

    g_up_t = swap_last(_sum_parts([p_up0, p_up1], "sum_w_up").reshape(f_w_up.shape))
    res, (p_in_b, small_all) = _adamw([g_up_t[0:1], g_up_t[1:2]], *[swap_last(dct["f_w_up"]) for dct in (wts, mom, var)],
                                      "adamw_f_w_up", comm=tail)
    big_out["f_w_up"] = [swap_last(r) for r in res]
    for n in BIG:
        if n not in ("f_w_up", "a_w_in"):
            adamw_big(n)
    g_in_t = swap_last(_sum_parts([p_in_a, p_in_b], "sum_w_in"))[None]
    res = _adamw([g_in_t], *[swap_last(dct["a_w_in"]) for dct in (wts, mom, var)], "adamw_a_w_in")
    big_out["a_w_in"] = [swap_last(r) for r in res]
    mine_small = lax.dynamic_slice_in_dim(small_all, me * n_sr, n_sr, axis=1)
    parts_small = jnp.concatenate([mine_small, small_all[:, N_DEV * n_sr:]], axis=1)
    sm_in = [jnp.concatenate([flat_f32(dct, SMALL, 8), flat_f32(dct, REPL, 8)], axis=0)[None] for dct in (wts, mom, var)]
    small_out = [r[0] for r in _adamw([parts_small], *sm_in, "adamw_small")]

    outs = []
    for kind in range(4):
        res = {n: big_out[n][kind] for n in BIG}
        for n, a in zip(SMALL, _unpack(small_out[kind][:n_sr], 0, [wts[n].shape for n in SMALL])):
            res[n] = a
        for n, a in zip(REPL, _unpack(small_out[kind][n_sr:], 0, [wts[n].shape for n in REPL])):
            res[n] = a
        outs.append(res)
    return (loss, grad_x, *[outs[0][n] for n in WEIGHTS], *[outs[1][n] for n in WEIGHTS],
            *[outs[2][n] for n in WEIGHTS], *[outs[3][n] for n in WEIGHTS])
```

```python
import functools
import math

import jax
import jax.numpy as jnp
from jax import lax
from jax.experimental import pallas as pl
from jax.experimental.pallas import tpu as pltpu

F32, BF16 = jnp.float32, jnp.bfloat16
S = jax.ShapeDtypeStruct

D_MODEL = 1024
SEQ = 2048
N_META = 16
D_INNER = 2048
HEAD_P = 64
SSM_HEADS = D_INNER // HEAD_P
SSM_GROUPS = 4
D_STATE = 128
SSM_CONV = 4
D_BC = SSM_GROUPS * D_STATE
D_XBC = D_INNER + 2 * D_BC
ATTN_DH = 64
N_Q_HEADS = D_MODEL // ATTN_DH
N_KV_HEADS = 4
D_KV = N_KV_HEADS * ATTN_DH
WINDOW = 128
D_FF = 2816
FFN_CONV = 3
RMS_EPS = 1e-6
NEG = -1e30
LR, B1, B2, EPS, WD, STEP = 0.001, 0.9, 0.999, 1e-08, 0.01, 10

N_DEV = 8
T = 128
LANE = 128
VMEM_LIMIT = 48 * 1024 * 1024

BIG = ("a_w_in", "a_w_out", "w_kv", "b_w_q", "b_w_o", "f_w_up", "f_w_down")
SMALL = ("meta_tokens", "a_norm_pre", "a_conv_w", "a_conv_b", "a_gate_norm", "a_norm_post", "f_conv_w")
REPL = ("a_dt_bias", "a_a_log", "a_d_skip", "kv_norm", "b_norm_pre", "b_sinks", "b_norm_post",
        "f_norm_pre", "f_conv_b", "f_norm_post")
SHARD_AXIS = dict(a_w_in=2, a_w_out=1, w_kv=0, b_w_q=1, b_w_o=1, f_w_up=2, f_w_down=1, meta_tokens=1,
                  a_norm_pre=1, a_conv_w=2, a_conv_b=1, a_gate_norm=1, a_norm_post=1, f_conv_w=2)
WEIGHTS = ("meta_tokens", "a_norm_pre", "a_w_in", "a_conv_w", "a_conv_b", "a_dt_bias", "a_a_log", "a_d_skip",
           "a_gate_norm", "a_w_out", "a_norm_post", "kv_norm", "w_kv", "b_norm_pre", "b_w_q", "b_sinks", "b_w_o",
           "b_norm_post", "f_norm_pre", "f_w_up", "f_conv_w", "f_conv_b", "f_w_down", "f_norm_post")


def _seq_rows():
    return -(-(N_META + SEQ) // T) * T


def _cp(sem=None):
    return pltpu.CompilerParams(dimension_semantics=sem, vmem_limit_bytes=VMEM_LIMIT)


def _pick(n, target):
    t = min(n, target)
    t -= t % LANE
    while n % t:
        t -= LANE
    return t


def _sigmoid(x):
    return 0.5 * jnp.tanh(0.5 * x) + 0.5


def _softplus(x):
    return jnp.maximum(x, 0.0) + jnp.log(1.0 + jnp.exp(-jnp.abs(x)))


_NN = (((1,), (0,)), ((), ()))
_NT = (((1,), (1,)), ((), ()))
_TN = (((0,), (0,)), ((), ()))


def _dot(a, b, dims=_NN):
    return lax.dot_general(a, b, dims, preferred_element_type=F32)


def _dot_hi(a, b):
    return lax.dot_general(a, b, _NN, precision=lax.Precision.HIGHEST, preferred_element_type=F32)


_HBM = pl.BlockSpec(memory_space=pltpu.HBM)
_MESH = pl.DeviceIdType.MESH


class _Comm:
    def __init__(self, ins, out_shapes, scratch, first, last):
        self.ins, self.out_shapes, self.scratch, self.first, self.last = ins, out_shapes, scratch, first, last


def _call(body, name, out_shape, grid, in_specs, out_specs, sem, args, scratch=(), comm=None):
    if comm is None:
        return pl.pallas_call(body, name=name, out_shape=out_shape, grid=grid, in_specs=in_specs, out_specs=out_specs,
                              scratch_shapes=list(scratch), compiler_params=_cp(sem))(*args)
    single = not isinstance(out_shape, (list, tuple))
    outs = [out_shape] if single else list(out_shape)
    ospecs = [out_specs] if single else list(out_specs)
    n_in, n_out, n_scr, ci, co = len(in_specs), len(outs), len(scratch), len(comm.ins), len(comm.out_shapes)

    def carrier(*refs):
        p = 0
        parts = []
        for cnt in (n_in, ci, n_out, co, n_scr, len(comm.scratch)):
            parts.append(refs[p:p + cnt])
            p += cnt
        ins, cins, outs_r, couts, scr, cscr = parts
        ids = [pl.program_id(i) for i in range(len(grid))]
        first, last = ids[0] == 0, ids[0] == grid[0] - 1
        for i in range(1, len(grid)):
            first, last = first & (ids[i] == 0), last & (ids[i] == grid[i] - 1)

        @pl.when(first)
        def _():
            comm.first(cins, couts, cscr)

        body(*ins, *outs_r, *scr)

        @pl.when(last)
        def _():
            comm.last(cins, couts, cscr)

    res = pl.pallas_call(
        carrier, name=name, out_shape=outs + list(comm.out_shapes), grid=grid,
        in_specs=list(in_specs) + [_HBM] * ci, out_specs=ospecs + [_HBM] * co,
        scratch_shapes=list(scratch) + list(comm.scratch),
        compiler_params=_cp(("arbitrary",) * len(grid)))(*args, *comm.ins)
    mine = res[0] if single else list(res[:n_out])
    return mine, list(res[n_out:])


def _mm(a, b, mode, out_dtype, name, comm=None, shard_cols=None):
    if mode == "tn":
        m, kk = a.shape
        planes, width = (b.shape[0], b.shape[2]) if b.ndim == 3 else (1, b.shape[1])
        n = planes * width
        tko, tn = _pick(kk, 512), _pick(width, 1536)
        per = width // tn
        b_spec = (pl.BlockSpec((None, m, tn), lambda i, j: (j // per, 0, j % per)) if b.ndim == 3
                  else pl.BlockSpec((m, tn), lambda i, j: (0, j)))
        if shard_cols is None:
            def body(a_ref, b_ref, o_ref):
                o_ref[...] = _dot(a_ref[...], b_ref[...], _TN).astype(o_ref.dtype)

            out_shape, out_spec = S((kk, n), out_dtype), pl.BlockSpec((tko, tn), lambda i, j: (i, j))
        else:
            shards = tn // shard_cols
            assert tn % shard_cols == 0

            def body(a_ref, b_ref, o_ref):
                res = _dot(a_ref[...], b_ref[...], _TN).astype(o_ref.dtype)
                for p in range(shards):
                    o_ref[p] = res[:, p * shard_cols:(p + 1) * shard_cols]

            out_shape = S((n // shard_cols, kk, shard_cols), out_dtype)
            out_spec = pl.BlockSpec((shards, tko, shard_cols), lambda i, j: (j, i, 0))
        return _call(
            body, name, out_shape, (kk // tko, n // tn), [pl.BlockSpec((m, tko), lambda i, j: (0, i)), b_spec],
            out_spec, ("parallel", "parallel"), (a, b), comm=comm)

    planes, width = (a.shape[0], a.shape[2]) if a.ndim == 3 else (1, a.shape[1])
    m, kk = a.shape[-2], planes * width
    n = b.shape[1] if mode == "nn" else b.shape[0]
    dims = _NN if mode == "nn" else _NT

    if kk > 2048:
        tm = m // 4
        assert m % 4 == 0 and tm % 16 == 0

        def body(a_ref, b_ref, o_ref):
            if a.ndim == 2:
                res = _dot(a_ref[...], b_ref[...], dims)
            else:
                res = None
                for p in range(planes):
                    bp = b_ref[p * width:(p + 1) * width, :] if mode == "nn" else b_ref[:, p * width:(p + 1) * width]
                    part = _dot(a_ref[p], bp, dims)
                    res = part if res is None else res + part
            o_ref[...] = res.astype(o_ref.dtype)

        a_spec = (pl.BlockSpec((planes, tm, width), lambda i: (0, i, 0)) if a.ndim == 3
                  else pl.BlockSpec((tm, kk), lambda i: (i, 0)))
        return _call(
            body, name, S((m, n), out_dtype), (m // tm,),
            [a_spec, pl.BlockSpec(b.shape, lambda i: (0, 0), pipeline_mode=pl.Buffered(1))],
            pl.BlockSpec((tm, n), lambda i: (i, 0)), ("parallel",), (a, b), comm=comm)

    tn = _pick(n, 512)

    def body(a_ref, b_ref, o_ref):
        o_ref[...] = _dot(a_ref[...], b_ref[...], dims).astype(o_ref.dtype)

    b_spec = (pl.BlockSpec((kk, tn), lambda j: (0, j)) if mode == "nn" else pl.BlockSpec((tn, kk), lambda j: (j, 0)))
    return _call(
        body, name, S((m, n), out_dtype), (n // tn,), [pl.BlockSpec((m, kk), lambda j: (0, 0)), b_spec],
        pl.BlockSpec((m, tn), lambda j: (0, j)), ("parallel",), (a, b), comm=comm)


def _rms(x, w):
    return x * lax.rsqrt(jnp.mean(x * x, axis=-1, keepdims=True) + RMS_EPS) * w


def _row_tile(rows):
    return rows // 8


def _embed_norm(meta, x, w, rows, name, comm=None):
    n_meta, d = meta.shape
    n_x = x.shape[0]
    last = rows // T - 1
    assert n_meta % 8 == 0 and n_meta < T and n_meta + n_x == last * T + n_meta and last * T >= n_x

    def body(m_ref, x_ref, w_ref, h_ref, hn_ref):
        i = pl.program_id(0)

        @pl.when(i == 0)
        def _():
            h_ref[0:n_meta, :] = m_ref[...]
            h_ref[n_meta:T, :] = x_ref[0:T - n_meta, :]

        @pl.when((i > 0) & (i < last))
        def _():
            h_ref[...] = x_ref[pl.ds(pl.multiple_of(i * T - n_meta, 8), T), :]

        @pl.when(i == last)
        def _():
            h_ref[0:n_meta, :] = x_ref[n_x - n_meta:n_x, :]
            h_ref[n_meta:T, :] = jnp.zeros((T - n_meta, d), F32)

        hn_ref[...] = _rms(h_ref[...], w_ref[...]).astype(hn_ref.dtype)

    row = pl.BlockSpec((T, d), lambda i: (i, 0))
    return _call(body, name, [S((rows, d), F32), S((rows, d), BF16)], (rows // T,),
                 [pl.BlockSpec((n_meta, d), lambda i: (0, 0)), pl.BlockSpec((n_x, d), lambda i: (0, 0)),
                  pl.BlockSpec((1, d), lambda i: (0, 0))], [row, row], ("parallel",), (meta, x, w), comm=comm)


def _resid_norm(h, br, w_post, next_ws, name):
    rows, d = h.shape
    tr = _row_tile(rows)
    has_br = br is not None
    nw = len(next_ws)

    def body(*refs):
        h_ref = refs[0]
        pos = 1
        x = h_ref[...]
        if has_br:
            x = x + _rms(refs[1][...], refs[2][...])
            pos = 3
        w_refs = refs[pos:pos + nw]
        outs = refs[pos + nw:]
        if has_br:
            outs[0][...] = x
            outs = outs[1:]
        for w_ref, o_ref in zip(w_refs, outs):
            o_ref[...] = _rms(x, w_ref[...]).astype(o_ref.dtype)

    row = pl.BlockSpec((tr, d), lambda i: (i, 0))
    vec = pl.BlockSpec((1, d), lambda i: (0, 0))
    ins = [h] + ([br, w_post] if has_br else []) + list(next_ws)
    in_specs = [row] + ([row, vec] if has_br else []) + [vec] * nw
    out_shape = ([S((rows, d), F32)] if has_br else []) + [S((rows, d), BF16)] * nw
    res = pl.pallas_call(body, name=name, out_shape=out_shape, grid=(rows // tr,), in_specs=in_specs,
                         out_specs=[row] * len(out_shape), compiler_params=_cp(("parallel",)))(*ins)
    if has_br:
        return res[0], list(res[1:])
    return h, list(res)


def _rms_bwd(xv, w, dyv):
    r = lax.rsqrt(jnp.mean(xv * xv, axis=-1, keepdims=True) + RMS_EPS)
    wdy = dyv * w
    dx = r * wdy - xv * (r * r * r) * jnp.mean(xv * wdy, axis=-1, keepdims=True)
    return dx, jnp.sum(dyv * xv * r, axis=0, keepdims=True)


def _norm_bwd(x, w, dy, add, out_dtype, name, comm=None, then=None):
    rows, d = x.shape
    tr = _row_tile(rows)
    has_add = add is not None
    n_in = 3 + has_add + (2 if then is not None else 0)

    def body(*refs):
        x_ref, w_ref, dy_ref = refs[:3]
        outs = refs[n_in:]
        dx, dw = _rms_bwd(x_ref[...], w_ref[...], dy_ref[...].astype(F32))
        if has_add:
            dx = dx + refs[3][...]
        outs[0][...] = dx.astype(outs[0].dtype)
        first = pl.program_id(0) == 0

        @pl.when(first)
        def _():
            outs[1][...] = jnp.zeros_like(outs[1])

        outs[1][...] += dw
        if then is not None:
            dx2, dw2 = _rms_bwd(refs[n_in - 2][...], refs[n_in - 1][...], dx)
            outs[2][...] = dx2.astype(outs[2].dtype)

            @pl.when(first)
            def _():
                outs[3][...] = jnp.zeros_like(outs[3])

            outs[3][...] += dw2

    row = pl.BlockSpec((tr, d), lambda i: (i, 0))
    vec = pl.BlockSpec((1, d), lambda i: (0, 0))
    ins = [x, w, dy] + ([add] if has_add else []) + (list(then) if then is not None else [])
    in_specs = [row, vec, row] + ([row] if has_add else []) + ([row, vec] if then is not None else [])
    out_shape = [S((rows, d), out_dtype), S((1, d), F32)] + ([S((rows, d), BF16), S((1, d), F32)] if then is not None else [])
    return _call(body, name, out_shape, (rows // tr,), in_specs, [row, vec] * (len(out_shape) // 2), ("arbitrary",),
                 ins, comm=comm)


def _final_loss(h, br, w_post, tgt, name):
    rows, d = h.shape
    tr = _row_tile(rows)

    def body(h_ref, br_ref, w_ref, t_ref, dh_ref, loss_ref, dbr_ref, dw_ref):
        i = pl.program_id(0)
        brv, wv = br_ref[...], w_ref[...]
        y = h_ref[...] + _rms(brv, wv)
        r = i * tr + lax.broadcasted_iota(jnp.int32, (tr, 1), 0)
        real = (r >= N_META) & (r < N_META + SEQ)
        diff = jnp.where(real, y - t_ref[...], 0.0)
        dh = diff * (1.0 / d)
        dh_ref[...] = dh
        dbr, dw = _rms_bwd(brv, wv, dh)
        dbr_ref[...] = dbr.astype(dbr_ref.dtype)

        @pl.when(i == 0)
        def _():
            loss_ref[...] = jnp.zeros_like(loss_ref)
            dw_ref[...] = jnp.zeros_like(dw_ref)

        loss_ref[...] += jnp.sum(diff * diff) * (0.5 / d)
        dw_ref[...] += dw

    row = pl.BlockSpec((tr, d), lambda i: (i, 0))
    vec = pl.BlockSpec((1, d), lambda i: (0, 0))
    return pl.pallas_call(body, name=name,
                          out_shape=[S((rows, d), F32), S((1, LANE), F32), S((rows, d), BF16), S((1, d), F32)],
                          grid=(rows // tr,), in_specs=[row, row, vec, row],
                          out_specs=[row, pl.BlockSpec((1, LANE), lambda i: (0, 0)), row, vec],
                          compiler_params=_cp(("arbitrary",)))(h, br, w_post, tgt)


def _gatenorm_fwd(y, zx, w, name, comm=None):
    rows, d = y.shape
    tr = _row_tile(rows)

    def body(y_ref, z_ref, w_ref, o_ref):
        z = z_ref[...]
        o_ref[...] = _rms(y_ref[...] * z * _sigmoid(z), w_ref[...]).astype(o_ref.dtype)

    row = pl.BlockSpec((tr, d), lambda i: (i, 0))
    return _call(body, name, S((rows, d), BF16), (rows // tr,), [row, row, pl.BlockSpec((1, d), lambda i: (0, 0))],
                 row, ("parallel",), (y, zx, w), comm=comm)


def _gatenorm_bwd(y, zx, w, dyn, name, comm=None):
    rows, d = y.shape
    tr = _row_tile(rows)

    def body(y_ref, z_ref, w_ref, dyn_ref, dy_ref, dz_ref, dw_ref):
        yv, z = y_ref[...], z_ref[...]
        sg = _sigmoid(z)
        sz = z * sg
        g = yv * sz
        r = lax.rsqrt(jnp.mean(g * g, axis=-1, keepdims=True) + RMS_EPS)
        dyn_v = dyn_ref[...]
        wdy = dyn_v * w_ref[...]
        dg = r * wdy - g * (r * r * r) * jnp.mean(g * wdy, axis=-1, keepdims=True)
        dy_ref[...] = dg * sz
        dz_ref[...] = (dg * yv * sg * (1.0 + z * (1.0 - sg))).astype(dz_ref.dtype)

        @pl.when(pl.program_id(0) == 0)
        def _():
            dw_ref[...] = jnp.zeros_like(dw_ref)

        dw_ref[...] += jnp.sum(dyn_v * g * r, axis=0, keepdims=True)

    row = pl.BlockSpec((tr, d), lambda i: (i, 0))
    vec = pl.BlockSpec((1, d), lambda i: (0, 0))
    return _call(body, name, [S((rows, d), F32), S((rows, d), BF16), S((1, d), F32)], (rows // tr,),
                 [row, row, vec, row], [row, row, vec], ("arbitrary",), (y, zx, w, dyn), comm=comm)


def _shift_down(x, s, rows_iota):
    if s == 0:
        return x
    return jnp.where(rows_iota >= s, pltpu.roll(x, s, 0), 0.0)


def _shift_up(x, s, rows_iota):
    if s == 0:
        return x
    rows = x.shape[0]
    return jnp.where(rows_iota < rows - s, pltpu.roll(x, rows - s, 0), 0.0)


def _r16(v):
    return v.astype(BF16).astype(F32)


def _conv_taps(x, taps, rows_iota):
    x = _r16(x)
    return [_shift_down(x, taps - 1 - k, rows_iota) for k in range(taps)]


def _conv(x, w_ref, b_ref, taps, rows_iota, shifted=None):
    shifted = _conv_taps(x, taps, rows_iota) if shifted is None else shifted
    acc = jnp.zeros_like(shifted[0])
    for k in range(taps):
        acc = acc + _r16(w_ref[k:k + 1, :]) * shifted[k]
    return acc + b_ref[...]


def _conv_bwd(shifted, du, w_ref, dw_ref, db_ref, taps, rows_iota):
    db_ref[...] = jnp.sum(du, axis=0, keepdims=True)
    du = _r16(du)
    dx = jnp.zeros_like(du)
    for k in range(taps):
        dx = dx + _r16(w_ref[k:k + 1, :]) * _shift_up(du, taps - 1 - k, rows_iota)
        dw_ref[k:k + 1, :] = jnp.sum(du * shifted[k], axis=0, keepdims=True)
    return dx


def _conv_silu_fwd(zx, w, b, name, comm=None):
    rows = zx.shape[0]
    cb = 512
    off = D_INNER // cb

    def body(x_ref, w_ref, b_ref, o_ref):
        it = lax.broadcasted_iota(jnp.int32, (rows, 1), 0)
        u = _conv(x_ref[...], w_ref, b_ref, SSM_CONV, it)
        o_ref[...] = u * _sigmoid(u)

    return _call(
        body, name, S((rows, D_XBC), F32), (D_XBC // cb,),
        [pl.BlockSpec((rows, cb), lambda j: (0, off + j)), pl.BlockSpec((SSM_CONV, cb), lambda j: (0, j)),
         pl.BlockSpec((1, cb), lambda j: (0, j))],
        pl.BlockSpec((rows, cb), lambda j: (0, j)), ("parallel",), (zx, w, b), comm=comm)


def _conv_silu_bwd(zx, dxs, dbm, dcm, w, b, name, comm=None):
    rows = zx.shape[0]
    cb = 256
    off = D_INNER // cb
    nx, nbc = D_INNER // cb, D_BC // cb

    def body(x_ref, dx_in, db_in, dc_in, w_ref, b_ref, dx_ref, dw_ref, db_ref, dbuf):
        j = pl.program_id(0)
        for cond, src in ((j < nx, dx_in), ((j >= nx) & (j < nx + nbc), db_in), (j >= nx + nbc, dc_in)):
            @pl.when(cond)
            def _(src=src):
                dbuf[...] = src[...]
        it = lax.broadcasted_iota(jnp.int32, (rows, 1), 0)
        xs = _conv_taps(x_ref[...], SSM_CONV, it)
        u = _conv(None, w_ref, b_ref, SSM_CONV, it, xs)
        sg = _sigmoid(u)
        du = dbuf[...] * sg * (1.0 + u * (1.0 - sg))
        dx_ref[...] = _conv_bwd(xs, du, w_ref, dw_ref, db_ref, SSM_CONV, it).astype(dx_ref.dtype)

    def part(first, count):
        return pl.BlockSpec((rows, cb), lambda j: (0, jnp.clip(j - first, 0, count - 1)))

    col = pl.BlockSpec((rows, cb), lambda j: (0, j))
    wsp = pl.BlockSpec((SSM_CONV, cb), lambda j: (0, j))
    bsp = pl.BlockSpec((1, cb), lambda j: (0, j))
    return _call(
        body, name, [S((rows, D_XBC), BF16), S((SSM_CONV, D_XBC), F32), S((1, D_XBC), F32)], (D_XBC // cb,),
        [pl.BlockSpec((rows, cb), lambda j: (0, off + j)), part(0, nx), part(nx, nbc), part(nx + nbc, nbc), wsp, bsp],
        [col, wsp, bsp], ("arbitrary",), (zx, dxs, dbm, dcm, w, b), scratch=[pltpu.VMEM((rows, cb), F32)], comm=comm)


def _ffn_act_fwd(u, w, b, name, comm=None):
    rows = u.shape[0]
    cb = 256
    nb = D_FF // cb

    def body(g_ref, v_ref, wg_ref, wv_ref, bg_ref, bv_ref, o_ref, gv_ref):
        it = lax.broadcasted_iota(jnp.int32, (rows, 1), 0)
        g = _conv(g_ref[...], wg_ref, bg_ref, FFN_CONV, it)
        v = _conv(v_ref[...], wv_ref, bv_ref, FFN_CONV, it)
        gv_ref[0] = g
        gv_ref[1] = v
        o_ref[...] = (g * _sigmoid(g) * v).astype(o_ref.dtype)

    def sp(r, shift):
        return pl.BlockSpec((r, cb), lambda j: (0, shift + j))

    return _call(
        body, name, [S((rows, D_FF), BF16), S((2, rows, D_FF), F32)], (nb,),
        [sp(rows, 0), sp(rows, nb), sp(FFN_CONV, 0), sp(FFN_CONV, nb), sp(1, 0), sp(1, nb)],
        [sp(rows, 0), pl.BlockSpec((2, rows, cb), lambda j: (0, 0, j))], ("parallel",), (u, u, w, w, b, b), comm=comm)


def _ffn_act_bwd(u, gv, dact, w, name, comm=None):
    rows = u.shape[0]
    cb = 256
    nb = D_FF // cb

    def body(g_ref, v_ref, gv_ref, d_ref, wg_ref, wv_ref, du_ref, dw_ref, db_ref):
        it = lax.broadcasted_iota(jnp.int32, (rows, 1), 0)
        xg, xv = _conv_taps(g_ref[...], FFN_CONV, it), _conv_taps(v_ref[...], FFN_CONV, it)
        g, v = gv_ref[0], gv_ref[1]
        sg = _sigmoid(g)
        d = d_ref[...]
        dgate = d * v * sg * (1.0 + g * (1.0 - sg))
        dval = d * g * sg
        du_ref[0] = _conv_bwd(xg, dgate, wg_ref, dw_ref.at[0], db_ref.at[0], FFN_CONV, it).astype(du_ref.dtype)
        du_ref[1] = _conv_bwd(xv, dval, wv_ref, dw_ref.at[1], db_ref.at[1], FFN_CONV, it).astype(du_ref.dtype)

    def sp(r, shift):
        return pl.BlockSpec((r, cb), lambda j: (0, shift + j))

    def both(r):
        return pl.BlockSpec((2, r, cb), lambda j: (0, 0, j))

    return _call(
        body, name, [S((2, rows, D_FF), BF16), S((2, FFN_CONV, D_FF), F32), S((2, 1, D_FF), F32)], (nb,),
        [sp(rows, 0), sp(rows, nb), both(rows), sp(rows, 0), sp(FFN_CONV, 0), sp(FFN_CONV, nb)],
        [both(rows), both(FFN_CONV), both(1)], ("parallel",), (u, u, gv, dact, w, w), comm=comm)


def _ssd_consts(dtp_ref, bias_ref, alog_ref, hg):
    lane = lax.broadcasted_iota(jnp.int32, (1, LANE), 1)
    pre = dtp_ref[...] + bias_ref[...]
    dt = _softplus(pre)
    a_row = jnp.where(lane < hg, -jnp.exp(alog_ref[...]), 0.0)
    ri = lax.broadcasted_iota(jnp.int32, (T, T), 0)
    ci = lax.broadcasted_iota(jnp.int32, (T, T), 1)
    cs = _dot_hi((ri >= ci).astype(F32), dt * a_row)
    return pre, dt, a_row, cs, ri, ci, lane


def _head_rows(src, hg):
    return jnp.concatenate([jnp.broadcast_to(src[k:k + 1, :], (HEAD_P, src.shape[1])) for k in range(hg)], axis=0)


def _ssd_fwd(xbc, zx, bias, alog, dsk, name, comm=None):
    rows = xbc.shape[0]
    nc = rows // T
    hg = SSM_HEADS // SSM_GROUPS
    gw = hg * HEAD_P
    xoff, boff, coff = 0, D_INNER // D_STATE, (D_INNER + D_BC) // D_STATE
    dtoff = (D_INNER + D_XBC) // LANE

    def body(x_ref, b_ref, c_ref, dtp_ref, bias_ref, alog_ref, dsk_ref, y_ref, hst_ref, hs):
        c = pl.program_id(1)

        @pl.when(c == 0)
        def _():
            hs[...] = jnp.zeros_like(hs)

        _, dt, _, cs, ri, ci, _ = _ssd_consts(dtp_ref, bias_ref, alog_ref, hg)
        cst, dtt = cs.T, dt.T
        xt = x_ref[...].T
        bb, cbf = b_ref[...].astype(BF16), c_ref[...].astype(BF16)
        gt = _dot(bb, cbf, _NT)
        causal_t = ci >= ri
        dskv = dsk_ref[...]
        hall = hs[...]
        hst_ref[0, 0] = hall
        cs8 = cst[0:8, :]
        cl8 = cs8[:, T - 1:T]
        xdt = xt * _head_rows(dtt, hg)
        yo = _head_rows(jnp.exp(cs8), hg) * _dot(hall.astype(BF16), cbf, _NT)
        st = _dot((xdt * _head_rows(jnp.exp(cl8 - cs8), hg)).astype(BF16), bb)
        hs[...] = _head_rows(jnp.exp(cl8), hg) * hall + st
        yds = []
        for k in range(hg):
            sl = slice(k * HEAD_P, (k + 1) * HEAD_P)
            lt = jnp.exp(jnp.where(causal_t, cst[k:k + 1, :] - cs[:, k:k + 1], NEG))
            yds.append(_dot(xdt[sl, :].astype(BF16), (gt * lt).astype(BF16)))
        dsk_r = jnp.concatenate([jnp.broadcast_to(dskv[:, k:k + 1], (HEAD_P, 1)) for k in range(hg)], axis=0)
        y_ref[...] = (jnp.concatenate(yds, axis=0) + yo + dsk_r * xt).T

    vec = pl.BlockSpec((1, LANE), lambda g, c: (0, g))
    return _call(
        body, name, [S((rows, D_INNER), F32), S((nc, SSM_GROUPS, gw, D_STATE), F32)], (SSM_GROUPS, nc),
        [pl.BlockSpec((T, gw), lambda g, c: (c, xoff + g)),
         pl.BlockSpec((T, D_STATE), lambda g, c: (c, boff + g)),
         pl.BlockSpec((T, D_STATE), lambda g, c: (c, coff + g)),
         pl.BlockSpec((T, LANE), lambda g, c: (c, dtoff + g)), vec, vec, vec],
        [pl.BlockSpec((T, gw), lambda g, c: (c, g)), pl.BlockSpec((1, 1, gw, D_STATE), lambda g, c: (c, g, 0, 0))],
        ("parallel", "arbitrary"), (xbc, xbc, xbc, zx, bias, alog, dsk),
        scratch=[pltpu.VMEM((gw, D_STATE), F32)], comm=comm)


def _ssd_bwd(xbc, zx, bias, alog, dsk, dy, hst, name, comm=None):
    rows = xbc.shape[0]
    nc = rows // T
    hg = SSM_HEADS // SSM_GROUPS
    gw = hg * HEAD_P
    boff, coff = D_INNER // D_STATE, (D_INNER + D_BC) // D_STATE
    dtoff = (D_INNER + D_XBC) // LANE

    def body(x_ref, b_ref, c_ref, dtp_ref, bias_ref, alog_ref, dsk_ref, dy_ref, hst_ref,
             dx_ref, db_ref, dc_ref, ddtp_ref, dalog_ref, ddsk_ref, dbias_ref, dhs):
        step = pl.program_id(1)

        @pl.when(step == 0)
        def _():
            dhs[...] = jnp.zeros_like(dhs)
            dalog_ref[...] = jnp.zeros_like(dalog_ref)
            ddsk_ref[...] = jnp.zeros_like(ddsk_ref)
            dbias_ref[...] = jnp.zeros_like(dbias_ref)

        pre, dt, a_row, cs, ri, ci, lane = _ssd_consts(dtp_ref, bias_ref, alog_ref, hg)
        cst, dtt = cs.T, dt.T
        xt, dyt = x_ref[...].T, dy_ref[...].T
        bb, cbf = b_ref[...].astype(BF16), c_ref[...].astype(BF16)
        gt = _dot(bb, cbf, _NT)
        causal_t = ci >= ri
        dskv = dsk_ref[...]
        hall, dhall = hst_ref[0, 0], dhs[...]
        head_row = lax.broadcasted_iota(jnp.int32, (T, 1), 0)
        last_l = lax.broadcasted_iota(jnp.int32, (1, T), 1) == T - 1
        cs8, dt8 = cst[0:8, :], dtt[0:8, :]
        cl8 = cs8[:, T - 1:T]
        e8, wdec8 = jnp.exp(cs8), jnp.exp(cl8 - cs8)
        w8 = wdec8 * dt8
        dt_r, e_r, w_r, ecl_r = _head_rows(dt8, hg), _head_rows(e8, hg), _head_rows(w8, hg), _head_rows(jnp.exp(cl8), hg)
        dsk_r = jnp.concatenate([jnp.broadcast_to(dskv[:, k:k + 1], (HEAD_P, 1)) for k in range(hg)], axis=0)
        hb, dhb = hall.astype(BF16), dhall.astype(BF16)
        xdt = xt * dt_r
        dye = (dyt * e_r).astype(BF16)
        rt = _dot(dhb, bb, _NT)
        yo = e_r * _dot(hb, cbf, _NT)
        dhs[...] = ecl_r * dhall + _dot(dye, cbf)
        dc_acc = _dot(dye, hb, _TN)
        db_acc = _dot((xt * w_r).astype(BF16), dhb, _TN)
        rtx, dyyo, hdh, dyx = rt * xt, dyt * yo, dhall * hall, dyt * xt
        dgt = jnp.zeros((T, T), F32)
        ddt_rows = jnp.zeros((T, T), F32)
        dcs_rows = jnp.zeros((T, T), F32)
        qrow_cols = jnp.zeros((T, LANE), F32)
        ddsk_acc = jnp.zeros((1, LANE), F32)
        dxdts = []
        for k in range(hg):
            sl = slice(k * HEAD_P, (k + 1) * HEAD_P)
            lt = jnp.exp(jnp.where(causal_t, cst[k:k + 1, :] - cs[:, k:k + 1], NEG))
            mpt = gt * lt
            dyb = dyt[sl, :].astype(BF16)
            dxdt = _dot(dyb, mpt.astype(BF16), _NT)
            dmt = _dot(xdt[sl, :].astype(BF16), dyb, _TN)
            dgt = dgt + dmt * lt
            q = dmt * mpt
            q_rows = jnp.sum(q, axis=1, keepdims=True)
            q_cols = jnp.sum(q, axis=0, keepdims=True)
            dxdts.append(dxdt)
            xz = jnp.sum(xt[sl, :] * dxdt, axis=0, keepdims=True)
            dw = jnp.sum(rtx[sl, :], axis=0, keepdims=True)
            wk, wdeck = w8[k:k + 1, :], wdec8[k:k + 1, :]
            dcl = jnp.exp(cl8[k:k + 1, :]) * jnp.sum(hdh[sl, :]) + jnp.sum(dw * wk)
            dcs_r = jnp.sum(dyyo[sl, :], axis=0, keepdims=True) + q_cols - dw * wk + jnp.where(last_l, dcl, 0.0)
            onehot = (lane == k).astype(F32)
            ddt_rows = ddt_rows + jnp.where(head_row == k, xz + dw * wdeck, 0.0)
            dcs_rows = dcs_rows + jnp.where(head_row == k, dcs_r, 0.0)
            qrow_cols = qrow_cols + q_rows * onehot
            ddsk_acc = ddsk_acc + jnp.sum(dyx[sl, :]) * onehot
        dx_ref[...] = (dt_r * jnp.concatenate(dxdts, axis=0) + dsk_r * dyt + rt * w_r).T
        dc_ref[...] = _dot(dgt.T.astype(BF16), bb) + dc_acc
        db_ref[...] = _dot(dgt.astype(BF16), cbf) + db_acc
        da = _dot_hi((ci >= ri).astype(F32), dcs_rows.T - qrow_cols)
        ddtp = (ddt_rows.T + da * a_row) * _sigmoid(pre)
        ddtp = jnp.where(lane < hg, ddtp, 0.0)
        ddtp_ref[...] = ddtp
        dbias_ref[...] += jnp.sum(ddtp, axis=0, keepdims=True)
        dalog_ref[...] += jnp.sum(da * dt, axis=0, keepdims=True) * a_row
        ddsk_ref[...] += ddsk_acc

    def rc(c):
        return nc - 1 - c

    vec = pl.BlockSpec((1, LANE), lambda g, c: (0, g))
    xsp = pl.BlockSpec((T, gw), lambda g, c: (rc(c), g))
    return _call(
        body, name,
        [S((rows, D_INNER), F32), S((rows, D_BC), F32), S((rows, D_BC), F32),
         S((rows, SSM_GROUPS * LANE), F32), S((1, SSM_GROUPS * LANE), F32),
         S((1, SSM_GROUPS * LANE), F32), S((1, SSM_GROUPS * LANE), F32)],
        (SSM_GROUPS, nc),
        [xsp,
         pl.BlockSpec((T, D_STATE), lambda g, c: (rc(c), boff + g)),
         pl.BlockSpec((T, D_STATE), lambda g, c: (rc(c), coff + g)),
         pl.BlockSpec((T, LANE), lambda g, c: (rc(c), dtoff + g)), vec, vec, vec,
         xsp, pl.BlockSpec((1, 1, gw, D_STATE), lambda g, c: (rc(c), g, 0, 0))],
        [xsp,
         pl.BlockSpec((T, D_STATE), lambda g, c: (rc(c), g)),
         pl.BlockSpec((T, D_STATE), lambda g, c: (rc(c), g)),
         pl.BlockSpec((T, LANE), lambda g, c: (rc(c), g)), vec, vec, vec],
        ("parallel", "arbitrary"), (xbc, xbc, xbc, zx, bias, alog, dsk, dy, hst),
        scratch=[pltpu.VMEM((gw, D_STATE), F32)], comm=comm)


def _attn_tiles(kv_ref, j):
    prev = jnp.maximum(j - 1, 0)
    meta = kv_ref[0:T, :]
    prv = kv_ref[pl.ds(pl.multiple_of(prev * T, T), T), :]
    cur = kv_ref[pl.ds(pl.multiple_of(j * T, T), T), :]
    return jnp.concatenate([meta, prv, cur], axis=0)


def _attn_mask(j):
    r = j * T + lax.broadcasted_iota(jnp.int32, (3 * T, T), 1)
    row = lax.broadcasted_iota(jnp.int32, (3 * T, T), 0)
    t0, t1 = row < T, row < 2 * T
    s = jnp.where(t0, row, (j - 2) * T + row)
    ok = (s <= r) & ((s < N_META) | (s > r - WINDOW))
    use = (t0 & (j >= 2) & (row < N_META)) | (jnp.logical_not(t0) & t1 & (j >= 1)) | jnp.logical_not(t1)
    return ok & use


def _attn_fwd(q, kv, sinks, name, comm=None):
    rows = q.shape[0]
    scale = 1.0 / math.sqrt(ATTN_DH)
    qpk = N_Q_HEADS // N_KV_HEADS

    def body(q_ref, kv_ref, s_ref, o_ref, lse_ref):
        j = pl.program_id(0)
        kv3 = _attn_tiles(kv_ref, j).astype(BF16)
        mask = _attn_mask(j)
        qv = (q_ref[...] * scale).astype(BF16)
        sk = s_ref[...]
        lses = []
        for kh in range(N_KV_HEADS):
            k3 = kv3[:, kh * ATTN_DH:(kh + 1) * ATTN_DH]
            v3 = kv3[:, D_KV + kh * ATTN_DH:D_KV + (kh + 1) * ATTN_DH]
            for g in range(qpk):
                h = kh * qpk + g
                sink = sk[:, h:h + 1]
                sc = jnp.where(mask, _dot(k3, qv[:, h * ATTN_DH:(h + 1) * ATTN_DH], _NT), NEG)
                m = jnp.maximum(jnp.max(sc, axis=0, keepdims=True), sink)
                p = jnp.exp(sc - m)
                den = jnp.sum(p, axis=0, keepdims=True) + jnp.exp(sink - m)
                p = p * (1.0 / den)
                lses.append(m + jnp.log(den))
                o_ref[:, h * ATTN_DH:(h + 1) * ATTN_DH] = _dot(p.astype(BF16), v3, _TN).astype(o_ref.dtype)
        lse_ref[...] = jnp.concatenate(lses, axis=0)

    return _call(
        body, name, [S((rows, D_MODEL), BF16), S((N_Q_HEADS, rows), F32)], (rows // T,),
        [pl.BlockSpec((T, D_MODEL), lambda j: (j, 0)), pl.BlockSpec((rows, 2 * D_KV), lambda j: (0, 0)),
         pl.BlockSpec((1, N_Q_HEADS), lambda j: (0, 0))],
        [pl.BlockSpec((T, D_MODEL), lambda j: (j, 0)), pl.BlockSpec((N_Q_HEADS, T), lambda j: (0, j))],
        ("parallel",), (q, kv, sinks), comm=comm)


def _attn_bwd(q, kv, sinks, do, lse, name, comm=None):
    rows = q.shape[0]
    scale = 1.0 / math.sqrt(ATTN_DH)
    qpk = N_Q_HEADS // N_KV_HEADS

    def body(q_ref, kv_ref, s_ref, do_ref, lse_ref, dq_ref, dkv_ref, ds_ref):
        j = pl.program_id(0)

        @pl.when(j == 0)
        def _():
            dkv_ref[...] = jnp.zeros_like(dkv_ref)
            ds_ref[...] = jnp.zeros_like(ds_ref)

        kv3 = _attn_tiles(kv_ref, j).astype(BF16)
        mask = _attn_mask(j)
        qv = (q_ref[...] * scale).astype(BF16)
        dov = do_ref[...].astype(BF16)
        sk = s_ref[...]
        lsev = lse_ref[...]
        lane = lax.broadcasted_iota(jnp.int32, (1, LANE), 1)
        ds_acc = jnp.zeros((1, LANE), F32)
        prev = jnp.maximum(j - 1, 0)
        mask4 = jnp.concatenate([mask] * qpk, axis=1)
        dqts = []
        for kh in range(N_KV_HEADS):
            ksl = slice(kh * ATTN_DH, (kh + 1) * ATTN_DH)
            vsl = slice(D_KV + kh * ATTN_DH, D_KV + (kh + 1) * ATTN_DH)
            k3, v3 = kv3[:, ksl], kv3[:, vsl]
            heads = [kh * qpk + g for g in range(qpk)]
            q4 = jnp.concatenate([qv[:, h * ATTN_DH:(h + 1) * ATTN_DH] for h in heads], axis=0)
            do4 = jnp.concatenate([dov[:, h * ATTN_DH:(h + 1) * ATTN_DH] for h in heads], axis=0)
            lse4 = jnp.concatenate([lsev[h:h + 1, :] for h in heads], axis=1)
            sink4 = jnp.concatenate([jnp.broadcast_to(sk[:, h:h + 1], (1, T)) for h in heads], axis=1)
            p = jnp.exp(jnp.where(mask4, _dot(k3, q4, _NT), NEG) - lse4)
            ps = jnp.exp(sink4 - lse4)
            dp = _dot(v3, do4, _NT)
            delta = jnp.sum(p * dp, axis=0, keepdims=True)
            dsc = (p * (dp - delta)).astype(BF16)
            dq4 = _dot(k3.T, dsc) * scale
            dk3 = _dot(dsc, q4)
            dv3 = _dot(p.astype(BF16), do4)
            psd = ps * delta
            for g, h in enumerate(heads):
                dqts.append(dq4[:, g * T:(g + 1) * T])
                ds_acc = ds_acc - jnp.sum(psd[:, g * T:(g + 1) * T]) * (lane == h).astype(F32)
            for t, start in enumerate((0, pl.multiple_of(prev * T, T), pl.multiple_of(j * T, T))):
                rsl = pl.ds(start, T)
                dkv_ref[rsl, ksl] += dk3[t * T:(t + 1) * T, :]
                dkv_ref[rsl, vsl] += dv3[t * T:(t + 1) * T, :]
        ds_ref[...] += ds_acc
        dq_ref[...] = jnp.concatenate(dqts, axis=0).T.astype(dq_ref.dtype)

    blk = pl.BlockSpec((T, D_MODEL), lambda j: (j, 0))
    full = pl.BlockSpec((rows, 2 * D_KV), lambda j: (0, 0))
    return _call(
        body, name, [S((rows, D_MODEL), BF16), S((rows, 2 * D_KV), F32), S((1, LANE), F32)], (rows // T,),
        [blk, full, pl.BlockSpec((1, N_Q_HEADS), lambda j: (0, 0)), blk, pl.BlockSpec((N_Q_HEADS, T), lambda j: (0, j))],
        [blk, full, pl.BlockSpec((1, LANE), lambda j: (0, 0))], ("arbitrary",), (q, kv, sinks, do, lse), comm=comm)


BLOCK_BYTES = 1 << 20


def _div_tile(rows, cols):
    cap = max(16, BLOCK_BYTES // (4 * cols))
    best = None
    for t in range(16, min(rows, cap) + 1, 16):
        if rows % t == 0:
            best = t
    return best if best is not None else rows


def _adamw(parts, w, m, v, name, comm=None):
    layers, rows, cols = w.shape
    n = parts[0].shape[0]
    tr = _div_tile(rows, cols)
    tc = _pick(cols, 256) if tr == rows and rows * cols * 4 > 2 * BLOCK_BYTES else cols
    c1 = 1.0 / (1.0 - B1 ** STEP)
    c2 = 1.0 / (1.0 - B2 ** STEP)

    def body(*refs):
        p_refs = refs[:layers]
        w_ref, m_ref, v_ref, g_ref, d_ref, nm_ref, nv_ref = refs[layers:]
        layer = pl.program_id(0)
        for l in range(layers):
            @pl.when(layer == l)
            def _(p_ref=p_refs[l]):
                g = p_ref[0].astype(F32)
                for i in range(1, n):
                    g = g + p_ref[i].astype(F32)
                nm = B1 * m_ref[...] + (1.0 - B1) * g
                nv = B2 * v_ref[...] + (1.0 - B2) * (g * g)
                g_ref[...] = g
                nm_ref[...] = nm
                nv_ref[...] = nv
                d_ref[...] = -LR * ((nm * c1) / (jnp.sqrt(nv * c2) + EPS) + WD * w_ref[...])

    def part_spec(l):
        return pl.BlockSpec((n, tr, tc), lambda k, i, j: (0, jnp.where(k == l, i, 0), jnp.where(k == l, j, 0)))

    row = pl.BlockSpec((None, tr, tc), lambda k, i, j: (k, i, j))
    return _call(body, name, [S((layers, rows, cols), F32)] * 4, (layers, rows // tr, cols // tc),
                 [part_spec(l) for l in range(layers)] + [row, row, row], [row] * 4,
                 ("parallel", "parallel", "parallel"), (*parts, w, m, v), comm=comm)


def _sum_parts(parts, name):
    n, rows, cols = parts[0].shape
    nb = len(parts)
    tr = _div_tile(rows, cols)

    def body(*refs):
        o_ref = refs[nb]
        blk = pl.program_id(0)
        for l in range(nb):
            @pl.when(blk == l)
            def _(p_ref=refs[l]):
                g = p_ref[0].astype(F32)
                for i in range(1, n):
                    g = g + p_ref[i].astype(F32)
                o_ref[...] = g

    def part_spec(l):
        return pl.BlockSpec((n, tr, cols), lambda k, i: (0, jnp.where(k == l, i, 0), 0))

    per = rows // tr
    return pl.pallas_call(body, name=name, out_shape=S((nb * rows, cols), F32), grid=(nb, per),
                          in_specs=[part_spec(l) for l in range(nb)],
                          out_specs=pl.BlockSpec((tr, cols), lambda k, i: (k * per + i, 0)),
                          compiler_params=_cp(("parallel", "parallel")))(*parts)


def _col_segments(ws, runs):
    segs = []
    for glo, mlo, n in runs:
        while n > 0:
            d, off = divmod(glo, ws)
            take = min(n, ws - off)
            segs.append((d, off, mlo, take))
            glo, mlo, n = glo + take, mlo + take, n - take
    return segs


def _assemble_cols(gs, width, segs, name):
    _, rows, ws = gs[0].shape
    nb = len(gs)
    rb = _div_tile(rows, width // 2)
    per = rows // rb

    def body(*refs):
        o_ref = refs[nb]
        piece = pl.program_id(0)
        for l in range(nb):
            @pl.when(piece == l)
            def _(g_ref=refs[l]):
                o_ref[...] = jnp.zeros_like(o_ref)
                for d, off, mlo, n in segs:
                    o_ref[:, mlo:mlo + n] = g_ref[d, :, off:off + n]

    def piece_spec(l):
        return pl.BlockSpec((N_DEV, rb, ws), lambda k, i: (0, jnp.where(k == l, i, 0), 0))

    return pl.pallas_call(
        body, name=name, out_shape=S((nb * rows, width), gs[0].dtype), grid=(nb, per),
        in_specs=[piece_spec(l) for l in range(nb)],
        out_specs=pl.BlockSpec((rb, width), lambda k, i: (k * per + i, 0)),
        compiler_params=_cp(("parallel", "parallel")))(*gs)


def _scatter_cols(dw, ws, segs, name):
    rows, width = dw.shape
    rb = _div_tile(rows, width)

    def body(w_ref, o_ref):
        for d, off, mlo, n in segs:
            o_ref[d, :, off:off + n] = w_ref[:, mlo:mlo + n].astype(o_ref.dtype)

    return pl.pallas_call(
        body, name=name, out_shape=S((N_DEV, rows, ws), BF16), grid=(rows // rb,),
        in_specs=[pl.BlockSpec((rb, width), lambda i: (i, 0))],
        out_specs=pl.BlockSpec((N_DEV, rb, ws), lambda i: (0, i, 0)), compiler_params=_cp(("parallel",)))(dw)


def _gather_comm(xs):
    n = len(xs)

    def setup(x_refs, out_refs, sems):
        send_sems, recv_sems, local_sems = sems
        mx, my, mc = lax.axis_index("x"), lax.axis_index("y"), lax.axis_index("c")
        me, sibling = (mx, my, mc), (mx, my, 1 - mc)
        chips = [(1 - mx, my), (mx, 1 - my), (1 - mx, 1 - my)]

        def blk(a, px, py, pc):
            return out_refs[a].at[4 * px + 2 * py + pc]

        def copy(a, k, block, to, src=None):
            return pltpu.make_async_remote_copy(
                src_ref=blk(a, *block) if src is None else src, dst_ref=blk(a, *block),
                send_sem=send_sems.at[a, k], recv_sem=recv_sems.at[a, k], device_id=to, device_id_type=_MESH)

        mine = [pltpu.make_async_copy(x_refs[a], blk(a, *me), local_sems.at[a]) for a in range(n)]
        own = []
        for a in range(n):
            own.append(copy(a, 0, me, sibling, src=x_refs[a]))
            own += [copy(a, 1 + i, me, (*chip, mc), src=x_refs[a]) for i, chip in enumerate(chips)]
        return me, sibling, chips, mc, copy, mine, own

    def first(x_refs, out_refs, sems):
        _, _, _, _, _, mine, own = setup(x_refs, out_refs, sems)
        for cp in mine + own:
            cp.start()

    def last(x_refs, out_refs, sems):
        me, sibling, chips, mc, copy, mine, own = setup(x_refs, out_refs, sems)
        passed = []
        for a in range(n):
            for i, chip in enumerate(chips):
                copy(a, 1 + i, (*chip, mc), me).wait_recv()
                passed.append(copy(a, 4 + i, (*chip, mc), sibling))
                passed[-1].start()
        for a in range(n):
            copy(a, 0, sibling, me).wait_recv()
            for i, chip in enumerate(chips):
                copy(a, 4 + i, (*chip, 1 - mc), me).wait_recv()
        for cp in own + passed:
            cp.wait_send()
        for cp in mine:
            cp.wait()

    return _Comm(list(xs), [S((N_DEV,) + x.shape, x.dtype) for x in xs],
                 [pltpu.SemaphoreType.DMA((n, 7)), pltpu.SemaphoreType.DMA((n, 7)), pltpu.SemaphoreType.DMA((n,))],
                 first, last)


def _swap_comm(gs):
    n = len(gs)

    def copies(g_refs, out_refs, sems):
        send_sems, recv_sems = sems
        mx, my, mc = lax.axis_index("x"), lax.axis_index("y"), lax.axis_index("c")
        return [pltpu.make_async_remote_copy(
            src_ref=g_refs[a].at[2 * k + 1 - mc], dst_ref=out_refs[a].at[k], send_sem=send_sems.at[a, k],
            recv_sem=recv_sems.at[a, k], device_id=(mx, my, 1 - mc), device_id_type=_MESH)
            for a in range(n) for k in range(4)]

    def first(g_refs, out_refs, sems):
        for cp in copies(g_refs, out_refs, sems):
            cp.start()

    def last(g_refs, out_refs, sems):
        for cp in copies(g_refs, out_refs, sems):
            cp.wait()

    return _Comm(list(gs), [S((4,) + g.shape[1:], g.dtype) for g in gs],
                 [pltpu.SemaphoreType.DMA((n, 4)), pltpu.SemaphoreType.DMA((n, 4))], first, last)


def _chips_comm(parts):
    n = len(parts)

    def copies(p_refs, out_refs, sems):
        send_sems, recv_sems, local_sems = sems
        mx, my, mc = lax.axis_index("x"), lax.axis_index("y"), lax.axis_index("c")
        mychip = 2 * mx + my
        chips = [(1 - mx, my), (mx, 1 - my), (1 - mx, 1 - my)]
        mine = [pltpu.make_async_copy(p_refs[a].at[mychip], out_refs[a].at[mychip], local_sems.at[a])
                for a in range(n)]
        return mine + [pltpu.make_async_remote_copy(
            src_ref=p_refs[a].at[2 * cx + cy], dst_ref=out_refs[a].at[mychip], send_sem=send_sems.at[a, i],
            recv_sem=recv_sems.at[a, i], device_id=(cx, cy, mc), device_id_type=_MESH)
            for a in range(n) for i, (cx, cy) in enumerate(chips)]

    def first(p_refs, out_refs, sems):
        for cp in copies(p_refs, out_refs, sems):
            cp.start()

    def last(p_refs, out_refs, sems):
        for cp in copies(p_refs, out_refs, sems):
            cp.wait()

    return _Comm(list(parts), [S(p.shape, p.dtype) for p in parts],
                 [pltpu.SemaphoreType.DMA((n, 3)), pltpu.SemaphoreType.DMA((n, 3)), pltpu.SemaphoreType.DMA((n,))],
                 first, last)


def _join_comms(comms):
    def split(refs, counts):
        out, p = [], 0
        for cnt in counts:
            out.append(refs[p:p + cnt])
            p += cnt
        return out

    ni = [len(c.ins) for c in comms]
    no = [len(c.out_shapes) for c in comms]
    ns = [len(c.scratch) for c in comms]

    def first(in_refs, out_refs, sems):
        for c, i, o, s in zip(comms, split(in_refs, ni), split(out_refs, no), split(sems, ns)):
            c.first(i, o, s)

    def last(in_refs, out_refs, sems):
        for c, i, o, s in zip(comms, split(in_refs, ni), split(out_refs, no), split(sems, ns)):
            c.last(i, o, s)

    return _Comm([x for c in comms for x in c.ins], [x for c in comms for x in c.out_shapes],
                 [x for c in comms for x in c.scratch], first, last)


def _add_pairs(mine, theirs, core, name):
    _, rows, cols = mine.shape
    tr = _div_tile(rows, cols)

    def body(core_ref, a_ref, b_ref, o_ref):
        o_ref[...] = (a_ref[...].astype(F32) + b_ref[...].astype(F32)).astype(o_ref.dtype)

    return pl.pallas_call(
        body, name=name, out_shape=S((4, rows, cols), BF16),
        grid_spec=pltpu.PrefetchScalarGridSpec(
            num_scalar_prefetch=1, grid=(4, rows // tr),
            in_specs=[pl.BlockSpec((None, tr, cols), lambda k, i, c: (2 * k + c[0], i, 0)),
                      pl.BlockSpec((None, tr, cols), lambda k, i, c: (k, i, 0))],
            out_specs=pl.BlockSpec((None, tr, cols), lambda k, i, c: (k, i, 0))),
        compiler_params=_cp(("parallel", "parallel")))(core, mine, theirs)


def _run_comm(comm, name):
    ci, co = len(comm.ins), len(comm.out_shapes)

    def body(*refs):
        comm.first(refs[:ci], refs[ci:ci + co], refs[ci + co:])
        comm.last(refs[:ci], refs[ci:ci + co], refs[ci + co:])

    return pl.pallas_call(body, name=name, out_shape=list(comm.out_shapes), in_specs=[_HBM] * ci,
                          out_specs=[_HBM] * co, scratch_shapes=list(comm.scratch))(*comm.ins)


def _flat_rows(n_elems, mult):
    rows = -(-n_elems // LANE)
    return -(-rows // mult) * mult


def _pack(arrs, lead, mult, dtype):
    lead_shape = arrs[0].shape[:lead]
    flat = jnp.concatenate([a.astype(dtype).reshape(lead_shape + (-1,)) for a in arrs], axis=-1)
    n = flat.shape[-1]
    rows = _flat_rows(n, mult)
    flat = jnp.pad(flat, [(0, 0)] * lead + [(0, rows * LANE - n)])
    return flat.reshape(lead_shape + (rows, LANE))


def _unpack(flat, lead, shapes):
    lead_shape = flat.shape[:lead]
    flat = flat.reshape(lead_shape + (-1,))
    out, off = [], 0
    for shp in shapes:
        n = math.prod(shp)
        out.append(flat[..., off:off + n].reshape(lead_shape + tuple(shp)))
        off += n
    return out


def _split8(full, ax, n):
    shp = full.shape
    return jnp.moveaxis(full.reshape(shp[:ax] + (N_DEV, n) + shp[ax + 1:]), ax, 0)


def _join8(g, ax):
    shp = g.shape[1:]
    return jnp.moveaxis(g, 0, ax).reshape(shp[:ax] + (N_DEV * shp[ax],) + shp[ax + 1:])


def _group_lanes(v, hg):
    v = v.reshape(SSM_GROUPS, hg)
    return jnp.pad(v, ((0, 0), (0, LANE - hg))).reshape(1, SSM_GROUPS * LANE)


def _ungroup_lanes(v, hg):
    return v.reshape(SSM_GROUPS, LANE)[:, :hg].reshape(1, SSM_GROUPS * hg)


def kernel(x, meta_tokens, a_norm_pre, a_w_in, a_conv_w, a_conv_b, a_dt_bias, a_a_log, a_d_skip, a_gate_norm, a_w_out, a_norm_post, kv_norm, w_kv, b_norm_pre, b_w_q, b_sinks, b_w_o, b_norm_post, f_norm_pre, f_w_up, f_conv_w, f_conv_b, f_w_down, f_norm_post, loss_target, m_meta_tokens, m_a_norm_pre, m_a_w_in, m_a_conv_w, m_a_conv_b, m_a_dt_bias, m_a_a_log, m_a_d_skip, m_a_gate_norm, m_a_w_out, m_a_norm_post, m_kv_norm, m_w_kv, m_b_norm_pre, m_b_w_q, m_b_sinks, m_b_w_o, m_b_norm_post, m_f_norm_pre, m_f_w_up, m_f_conv_w, m_f_conv_b, m_f_w_down, m_f_norm_post, v_meta_tokens, v_a_norm_pre, v_a_w_in, v_a_conv_w, v_a_conv_b, v_a_dt_bias, v_a_a_log, v_a_d_skip, v_a_gate_norm, v_a_w_out, v_a_norm_post, v_kv_norm, v_w_kv, v_b_norm_pre, v_b_w_q, v_b_sinks, v_b_w_o, v_b_norm_post, v_f_norm_pre, v_f_w_up, v_f_conv_w, v_f_conv_b, v_f_w_down, v_f_norm_post):
    args = locals()
    wts = {n: args[n] for n in WEIGHTS}
    mom = {n: args["m_" + n] for n in WEIGHTS}
    var = {n: args["v_" + n] for n in WEIGHTS}
    mx, my, mc = lax.axis_index("x"), lax.axis_index("y"), lax.axis_index("c")
    me = 4 * mx + 2 * my + mc
    rows = _seq_rows()
    hg = SSM_HEADS // SSM_GROUPS
    d = D_MODEL

    n_main = D_INNER + D_XBC
    ws_in, ws_up = a_w_in.shape[2], f_w_up.shape[2]
    segs_in = _col_segments(ws_in, [(0, 0, n_main)] + [(n_main + hg * g, n_main + LANE * g, hg)
                                                      for g in range(SSM_GROUPS)])
    segs_up = _col_segments(ws_up, [(0, 0, 2 * D_FF)])
    def gather_of(*ws):
        return _gather_comm([w.astype(BF16) for w in ws])

    small_full, = _run_comm(_gather_comm([_pack([wts[n] for n in SMALL], 0, 8, F32)]), "gather_small")
    full = {}
    for n, g in zip(SMALL, _unpack(small_full, 1, [wts[n].shape for n in SMALL])):
        full[n] = _join8(g, SHARD_AXIS[n])
    (h0, hn0), (g_in,) = _embed_norm(full["meta_tokens"], x[0], full["a_norm_pre"], rows, "embed_norm",
                                     comm=gather_of(a_w_in[0]))
    w_in_all = _assemble_cols([g_in], n_main + SSM_GROUPS * LANE, segs_in, "asm_w_in")
    w_up, w_down = [None, None], [None, None]
    bias_g = _group_lanes(wts["a_dt_bias"], hg)
    alog_g = _group_lanes(wts["a_a_log"], hg)
    dsk_g = _group_lanes(wts["a_d_skip"], hg)
    a_conv_w, a_conv_b = full["a_conv_w"][0], full["a_conv_b"]
    f_cw, f_cb = full["f_conv_w"], wts["f_conv_b"]
    fpre, fpost = wts["f_norm_pre"], wts["f_norm_post"]

    tgt = jnp.pad(loss_target[0], ((N_META, rows - N_META - SEQ), (0, 0)))

    zx, (g_out,) = _mm(hn0, w_in_all, "nn", F32, "mm_in", comm=gather_of(a_w_out[0]))
    w_out = g_out.reshape(D_INNER, d)
    xbc = _conv_silu_fwd(zx, a_conv_w, a_conv_b, "conv_a")
    (y_ssd, hst), (g_up0,) = _ssd_fwd(xbc, zx, bias_g, alog_g, dsk_g, "ssd_fwd", comm=gather_of(f_w_up[0]))
    w_up[0] = _assemble_cols([g_up0], 2 * D_FF, segs_up, "asm_w_up0")
    yn = _gatenorm_fwd(y_ssd, zx, full["a_gate_norm"], "gatenorm")
    mix_a, (g_o,) = _mm(yn, w_out, "nn", F32, "mm_out", comm=gather_of(b_w_o[0]))
    h1, (fn0,) = _resid_norm(h0, mix_a, full["a_norm_post"], [fpre[0:1]], "resid_a")

    half = d // 2
    u0, (g_dn0,) = _mm(fn0, w_up[0], "nn", F32, "mm_up0", comm=gather_of(f_w_down[0]))
    (act0, gv0), (g_up1a,) = _ffn_act_fwd(u0, f_cw[0], f_cb[0:1], "ffn_act0", comm=gather_of(f_w_up[1, :half]))
    ffn0, (g_kv, g_q) = _mm(act0, g_dn0.reshape(D_FF, d), "nn", F32, "mm_down0", comm=gather_of(w_kv, b_w_q[0]))
    w_kvf, w_q, w_o = g_kv.reshape(d, 2 * D_KV), g_q.reshape(d, d), g_o.reshape(d, d)
    h2, (kvn, bn) = _resid_norm(h1, ffn0, fpost[0:1], [wts["kv_norm"].reshape(1, d), wts["b_norm_pre"]], "resid_f0")
    kv = _mm(kvn, w_kvf, "nn", F32, "mm_kv")
    q = _mm(bn, w_q, "nn", F32, "mm_q")
    (o, lse), (g_up1b,) = _attn_fwd(q, kv, wts["b_sinks"], "attn_fwd", comm=gather_of(f_w_up[1, half:]))
    w_up[1] = _assemble_cols([g_up1a, g_up1b], 2 * D_FF, segs_up, "asm_w_up1")
    mix_b = _mm(o, w_o, "nn", F32, "mm_o")
    h3, (fn1,) = _resid_norm(h2, mix_b, wts["b_norm_post"], [fpre[1:2]], "resid_b")
    u1, (g_dn1,) = _mm(fn1, w_up[1], "nn", F32, "mm_up1", comm=gather_of(f_w_down[1]))
    w_down = [g_dn0.reshape(D_FF, d), g_dn1.reshape(D_FF, d)]
    act1, gv1 = _ffn_act_fwd(u1, f_cw[1], f_cb[1:2], "ffn_act1")
    ffn1 = _mm(act1, w_down[1], "nn", F32, "mm_down1")
    dh4, loss_row, dffn1, dw_post1 = _final_loss(h3, ffn1, fpost[1:2], tgt, "loss")
    loss = lax.psum(loss_row[0, 0], ("x", "y", "c"))

    grads = {}

    core = mc.astype(jnp.int32).reshape(1)

    def carried(res, comm):
        return res if comm is not None else (res, None)

    def ffn_bwd(dh_out, dffn, h_in, fn, u, gv, act, i, then, c_dact=None, c_dwdown=None, c_dwup=None, c_dfn=None):
        dact, got_a = carried(_mm(dffn, w_down[i], "nt", F32, f"mm_dact{i}", comm=c_dact), c_dact)
        dw_down, got_b = carried(_mm(act, dffn, "tn", BF16, f"mm_dwdown{i}", comm=c_dwdown), c_dwdown)
        dw_down = dw_down.reshape(N_DEV, -1, d)
        du, dwc, dbc = _ffn_act_bwd(u, gv, dact, f_cw[i], f"ffn_act_bwd{i}")
        dfn, (s_dn, *got_d) = _mm(du, w_up[i], "nt", F32, f"mm_dfn{i}", comm=_join_comms(
            [_swap_comm([dw_down])] + ([c_dfn] if c_dfn is not None else [])))
        sum_dn = _add_pairs(dw_down, s_dn, core, f"rs_add_dn{i}")
        dw_up, got_c = carried(_mm(fn, du, "tn", BF16, f"mm_dwup{i}", comm=c_dwup, shard_cols=ws_up), c_dwup)
        (dh_in, dw_pre, dbranch, dw_branch), (s_up,) = _norm_bwd(
            h_in, fpre[i:i + 1], dfn, dh_out, F32, f"nb_fpre{i}", comm=_swap_comm([dw_up]), then=then)
        sum_up = _add_pairs(dw_up, s_up, core, f"rs_add_up{i}")
        return dh_in, dbranch, dw_branch, dict(sum_down=sum_dn, cw=jnp.concatenate([dwc[0], dwc[1]], axis=1),
                                               cb=jnp.concatenate([dbc[0], dbc[1]], axis=1), sum_up=sum_up,
                                               pre=dw_pre), got_a, got_b, got_c, got_d

    dh3, dmix_b, grads["b_norm_post"], gf1, _, _, _, _ = ffn_bwd(dh4, dffn1, h3, fn1, u1, gv1, act1, 1,
                                                                 (mix_b, wts["b_norm_post"]))
    do = _mm(dmix_b, w_o, "nt", F32, "mm_do")
    dw_o = _mm(o, dmix_b, "tn", BF16, "mm_dwo").reshape(N_DEV, -1, d)
    half_up = gf1["sum_up"].shape[1] // 2
    (dq, dkv, dsinks), (p_up1a, s_o) = _attn_bwd(
        q, kv, wts["b_sinks"], do, lse, "attn_bwd",
        comm=_join_comms([_chips_comm([gf1["sum_up"][:, :half_up]]), _swap_comm([dw_o])]))
    sum_o = _add_pairs(dw_o, s_o, core, "rs_add_o")
    grads["b_sinks"] = dsinks[:, :N_Q_HEADS]
    dbn = _mm(dq, w_q, "nt", F32, "mm_dbn")
    dw_q = _mm(bn, dq, "tn", BF16, "mm_dwq").reshape(N_DEV, -1, d)
    dkv16 = dkv.astype(BF16)
    dkvn = _mm(dkv16, w_kvf, "nt", F32, "mm_dkvn")
    dw_kv = _mm(kvn, dkv16, "tn", BF16, "mm_dwkv").reshape(N_DEV, -1, 2 * D_KV)
    (dh2, grads["b_norm_pre"]), (s_q, s_kv) = _norm_bwd(h2, wts["b_norm_pre"], dbn, dh3, F32, "nb_bpre",
                                                        comm=_swap_comm([dw_q, dw_kv]))
    sum_q, sum_kv = _add_pairs(dw_q, s_q, core, "rs_add_q"), _add_pairs(dw_kv, s_kv, core, "rs_add_kv")
    dh2, dw_kvn, dffn0, dw_post0 = _norm_bwd(h2, wts["kv_norm"].reshape(1, d), dkvn, dh2, F32, "nb_kv",
                                             then=(ffn0, fpost[0:1]))
    grads["kv_norm"] = dw_kvn.reshape(d)
    dh1, dmix_a, grads["a_norm_post"], gf0, (p_o,), (p_q, p_kv), (p_dn1,), (p_up1b,) = ffn_bwd(
        dh2, dffn0, h1, fn0, u0, gv0, act0, 0, (mix_a, full["a_norm_post"]), c_dact=_chips_comm([sum_o]),
        c_dwdown=_chips_comm([sum_q, sum_kv]), c_dwup=_chips_comm([gf1["sum_down"]]),
        c_dfn=_chips_comm([gf1["sum_up"][:, half_up:]]))
    p_up1 = jnp.concatenate([p_up1a, p_up1b], axis=1)
    grads["f_norm_post"] = jnp.concatenate([dw_post0, dw_post1], axis=0)
    grads["f_norm_pre"] = jnp.concatenate([gf0["pre"], gf1["pre"]], axis=0)
    grads["f_conv_w"] = jnp.stack([gf0["cw"], gf1["cw"]])
    grads["f_conv_b"] = jnp.concatenate([gf0["cb"], gf1["cb"]], axis=0)

    dyn = _mm(dmix_a, w_out, "nt", F32, "mm_dyn")
    dw_out = _mm(yn, dmix_a, "tn", BF16, "mm_dwout").reshape(N_DEV, -1, d)
    (dy_ssd, dz, grads["a_gate_norm"]), (s_out,) = _gatenorm_bwd(y_ssd, zx, full["a_gate_norm"], dyn, "gatenorm_bwd",
                                                                 comm=_swap_comm([dw_out]))
    sum_out = _add_pairs(dw_out, s_out, core, "rs_add_out")
    (dxs, dbm, dcm, ddtp, dalog, ddsk, dbias), (p_up0,) = _ssd_bwd(
        xbc, zx, bias_g, alog_g, dsk_g, dy_ssd, hst, "ssd_bwd", comm=_chips_comm([gf0["sum_up"]]))
    grads["a_a_log"] = _ungroup_lanes(dalog, hg)
    grads["a_d_skip"] = _ungroup_lanes(ddsk, hg)
    grads["a_dt_bias"] = _ungroup_lanes(dbias, hg)
    dpre, dcw, dcb = _conv_silu_bwd(zx, dxs, dbm, dcm, a_conv_w, a_conv_b, "conv_a_bwd")
    grads["a_conv_w"], grads["a_conv_b"] = dcw[None], dcb
    dzx = jnp.concatenate([dz, dpre, ddtp.astype(BF16)], axis=1)
    dw_in_all, (p_dn0,) = _mm(hn0, dzx, "tn", BF16, "mm_dwin", comm=_chips_comm([gf0["sum_down"]]))
    dw_in8 = _scatter_cols(dw_in_all, ws_in, segs_in, "scat_w_in")
    dhn0, (s_in, p_out) = _mm(dzx, w_in_all, "nt", F32, "mm_dhn0",
                              comm=_join_comms([_swap_comm([dw_in8]), _chips_comm([sum_out])]))
    sum_in = _add_pairs(dw_in8, s_in, core, "rs_add_in")
    half_in = sum_in.shape[1] // 2
    (dh0, grads["a_norm_pre"]), (p_in_a,) = _norm_bwd(h0, full["a_norm_pre"], dhn0, dh1, F32, "nb_apre",
                                                      comm=_chips_comm([sum_in[:, :half_in]]))
    grad_x = dh0[N_META:N_META + SEQ][None]
    grads["meta_tokens"] = dh0[:N_META]

    small_local = _pack([_split8(grads[n], SHARD_AXIS[n], wts[n].shape[SHARD_AXIS[n]]) for n in SMALL], 1, 8, F32)
    repl_local = _pack([grads[n] for n in REPL], 0, 8, F32)
    n_sr = small_local.shape[1]
    small_vec = jnp.concatenate([small_local.reshape(N_DEV * n_sr, LANE), repl_local], axis=0)
    tail = _join_comms([_chips_comm([sum_in[:, half_in:]]), _gather_comm([small_vec])])
    parts_big = dict(a_w_out=[p_out], w_kv=[p_kv], b_w_q=[p_q], b_w_o=[p_o], f_w_down=[p_dn0, p_dn1])

    def flat_f32(dct, names, mult):
        return _pack([dct[n] for n in names], 0, mult, F32)

    def adamw_big(n, comm=None):
        shp3 = (len(parts_big[n]),) + parts_big[n][0].shape[1:]
        res = _adamw(parts_big[n], *[dct[n].reshape(shp3) for dct in (wts, mom, var)], f"adamw_{n}", comm=comm)
        res, got = res if comm is not None else (res, None)
        big_out[n] = [r.reshape(wts[n].shape) for r in res]
        return got

    big_out = {}
    def swap_last(a):
        return jnp.swapaxes(a, -1, -2)

    g_up_t = swap_last(_sum_parts([p_up0, p_up1], "sum_w_up").reshape(f_w_up.shape))
    res, (p_in_b, small_all) = _adamw([g_up_t[0:1], g_up_t[1:2]], *[swap_last(dct["f_w_up"]) for dct in (wts, mom, var)],
                                      "adamw_f_w_up", comm=tail)
    big_out["f_w_up"] = [swap_last(r) for r in res]
    for n in BIG:
        if n not in ("f_w_up", "a_w_in"):
            adamw_big(n)
    g_in_t = swap_last(_sum_parts([p_in_a, p_in_b], "sum_w_in"))[None]
    res = _adamw([g_in_t], *[swap_last(dct["a_w_in"]) for dct in (wts, mom, var)], "adamw_a_w_in")
    big_out["a_w_in"] = [swap_last(r) for r in res]
    mine_small = lax.dynamic_slice_in_dim(small_all, me * n_sr, n_sr, axis=1)
    parts_small = jnp.concatenate([mine_small, small_all[:, N_DEV * n_sr:]], axis=1)
    sm_in = [jnp.concatenate([flat_f32(dct, SMALL, 8), flat_f32(dct, REPL, 8)], axis=0)[None] for dct in (wts, mom, var)]
    small_out = [r[0] for r in _adamw([parts_small], *sm_in, "adamw_small")]

    outs = []
    for kind in range(4):
        res = {n: big_out[n][kind] for n in BIG}
        for n, a in zip(SMALL, _unpack(small_out[kind][:n_sr], 0, [wts[n].shape for n in SMALL])):
            res[n] = a
        for n, a in zip(REPL, _unpack(small_out[kind][n_sr:], 0, [wts[n].shape for n in REPL])):
            res[n] = a
        outs.append(res)
    return (loss, grad_x, *[outs[0][n] for n in WEIGHTS], *[outs[1][n] for n in WEIGHTS],
            *[outs[2][n] for n in WEIGHTS], *[outs[3][n] for n in WEIGHTS])
```

```python
import functools
import math

import jax
import jax.numpy as jnp
from jax import lax
from jax.experimental import pallas as pl
from jax.experimental.pallas import tpu as pltpu

F32, BF16 = jnp.float32, jnp.bfloat16
S = jax.ShapeDtypeStruct

D_MODEL = 1024
SEQ = 2048
N_META = 16
D_INNER = 2048
HEAD_P = 64
SSM_HEADS = D_INNER // HEAD_P
SSM_GROUPS = 4
D_STATE = 128
SSM_CONV = 4
D_BC = SSM_GROUPS * D_STATE
D_XBC = D_INNER + 2 * D_BC
ATTN_DH = 64
N_Q_HEADS = D_MODEL // ATTN_DH
N_KV_HEADS = 4
D_KV = N_KV_HEADS * ATTN_DH
WINDOW = 128
D_FF = 2816
FFN_CONV = 3
RMS_EPS = 1e-6
NEG = -1e30
LR, B1, B2, EPS, WD, STEP = 0.001, 0.9, 0.999, 1e-08, 0.01, 10

N_DEV = 8
T = 128
LANE = 128
VMEM_LIMIT = 48 * 1024 * 1024

BIG = ("a_w_in", "a_w_out", "w_kv", "b_w_q", "b_w_o", "f_w_up", "f_w_down")
SMALL = ("meta_tokens", "a_norm_pre", "a_conv_w", "a_conv_b", "a_gate_norm", "a_norm_post", "f_conv_w")
REPL = ("a_dt_bias", "a_a_log", "a_d_skip", "kv_norm", "b_norm_pre", "b_sinks", "b_norm_post",
        "f_norm_pre", "f_conv_b", "f_norm_post")
SHARD_AXIS = dict(a_w_in=2, a_w_out=1, w_kv=0, b_w_q=1, b_w_o=1, f_w_up=2, f_w_down=1, meta_tokens=1,
                  a_norm_pre=1, a_conv_w=2, a_conv_b=1, a_gate_norm=1, a_norm_post=1, f_conv_w=2)
WEIGHTS = ("meta_tokens", "a_norm_pre", "a_w_in", "a_conv_w", "a_conv_b", "a_dt_bias", "a_a_log", "a_d_skip",
           "a_gate_norm", "a_w_out", "a_norm_post", "kv_norm", "w_kv", "b_norm_pre", "b_w_q", "b_sinks", "b_w_o",
           "b_norm_post", "f_norm_pre", "f_w_up", "f_conv_w", "f_conv_b", "f_w_down", "f_norm_post")


def _seq_rows():
    return -(-(N_META + SEQ) // T) * T


def _cp(sem=None):
    return pltpu.CompilerParams(dimension_semantics=sem, vmem_limit_bytes=VMEM_LIMIT)


def _pick(n, target):
    t = min(n, target)
    t -= t % LANE
    while n % t:
        t -= LANE
    return t


def _sigmoid(x):
    return 0.5 * jnp.tanh(0.5 * x) + 0.5


def _softplus(x):
    return jnp.maximum(x, 0.0) + jnp.log(1.0 + jnp.exp(-jnp.abs(x)))


_NN = (((1,), (0,)), ((), ()))
_NT = (((1,), (1,)), ((), ()))
_TN = (((0,), (0,)), ((), ()))


def _dot(a, b, dims=_NN):
    return lax.dot_general(a, b, dims, preferred_element_type=F32)


def _dot_hi(a, b):
    return lax.dot_general(a, b, _NN, precision=lax.Precision.HIGHEST, preferred_element_type=F32)


_HBM = pl.BlockSpec(memory_space=pltpu.HBM)
_MESH = pl.DeviceIdType.MESH


class _Comm:
    def __init__(self, ins, out_shapes, scratch, first, last):
        self.ins, self.out_shapes, self.scratch, self.first, self.last = ins, out_shapes, scratch, first, last


def _call(body, name, out_shape, grid, in_specs, out_specs, sem, args, scratch=(), comm=None):
    if comm is None:
        return pl.pallas_call(body, name=name, out_shape=out_shape, grid=grid, in_specs=in_specs, out_specs=out_specs,
                              scratch_shapes=list(scratch), compiler_params=_cp(sem))(*args)
    single = not isinstance(out_shape, (list, tuple))
    outs = [out_shape] if single else list(out_shape)
    ospecs = [out_specs] if single else list(out_specs)
    n_in, n_out, n_scr, ci, co = len(in_specs), len(outs), len(scratch), len(comm.ins), len(comm.out_shapes)

    def carrier(*refs):
        p = 0
        parts = []
        for cnt in (n_in, ci, n_out, co, n_scr, len(comm.scratch)):
            parts.append(refs[p:p + cnt])
            p += cnt
        ins, cins, outs_r, couts, scr, cscr = parts
        ids = [pl.program_id(i) for i in range(len(grid))]
        first, last = ids[0] == 0, ids[0] == grid[0] - 1
        for i in range(1, len(grid)):
            first, last = first & (ids[i] == 0), last & (ids[i] == grid[i] - 1)

        @pl.when(first)
        def _():
            comm.first(cins, couts, cscr)

        body(*ins, *outs_r, *scr)

        @pl.when(last)
        def _():
            comm.last(cins, couts, cscr)

    res = pl.pallas_call(
        carrier, name=name, out_shape=outs + list(comm.out_shapes), grid=grid,
        in_specs=list(in_specs) + [_HBM] * ci, out_specs=ospecs + [_HBM] * co,
        scratch_shapes=list(scratch) + list(comm.scratch),
        compiler_params=_cp(("arbitrary",) * len(grid)))(*args, *comm.ins)
    mine = res[0] if single else list(res[:n_out])
    return mine, list(res[n_out:])


def _mm(a, b, mode, out_dtype, name, comm=None, shard_cols=None):
    if mode == "tn":
        m, kk = a.shape
        planes, width = (b.shape[0], b.shape[2]) if b.ndim == 3 else (1, b.shape[1])
        n = planes * width
        tko, tn = _pick(kk, 512), _pick(width, 1536)
        per = width // tn
        b_spec = (pl.BlockSpec((None, m, tn), lambda i, j: (j // per, 0, j % per)) if b.ndim == 3
                  else pl.BlockSpec((m, tn), lambda i, j: (0, j)))
        if shard_cols is None:
            def body(a_ref, b_ref, o_ref):
                o_ref[...] = _dot(a_ref[...], b_ref[...], _TN).astype(o_ref.dtype)

            out_shape, out_spec = S((kk, n), out_dtype), pl.BlockSpec((tko, tn), lambda i, j: (i, j))
        else:
            shards = tn // shard_cols
            assert tn % shard_cols == 0

            def body(a_ref, b_ref, o_ref):
                res = _dot(a_ref[...], b_ref[...], _TN).astype(o_ref.dtype)
                for p in range(shards):
                    o_ref[p] = res[:, p * shard_cols:(p + 1) * shard_cols]

            out_shape = S((n // shard_cols, kk, shard_cols), out_dtype)
            out_spec = pl.BlockSpec((shards, tko, shard_cols), lambda i, j: (j, i, 0))
        return _call(
            body, name, out_shape, (kk // tko, n // tn), [pl.BlockSpec((m, tko), lambda i, j: (0, i)), b_spec],
            out_spec, ("parallel", "parallel"), (a, b), comm=comm)

    planes, width = (a.shape[0], a.shape[2]) if a.ndim == 3 else (1, a.shape[1])
    m, kk = a.shape[-2], planes * width
    n = b.shape[1] if mode == "nn" else b.shape[0]
    dims = _NN if mode == "nn" else _NT

    if kk > 2048:
        tm = m // 4
        assert m % 4 == 0 and tm % 16 == 0

        def body(a_ref, b_ref, o_ref):
            if a.ndim == 2:
                res = _dot(a_ref[...], b_ref[...], dims)
            else:
                res = None
                for p in range(planes):
                    bp = b_ref[p * width:(p + 1) * width, :] if mode == "nn" else b_ref[:, p * width:(p + 1) * width]
                    part = _dot(a_ref[p], bp, dims)
                    res = part if res is None else res + part
            o_ref[...] = res.astype(o_ref.dtype)

        a_spec = (pl.BlockSpec((planes, tm, width), lambda i: (0, i, 0)) if a.ndim == 3
                  else pl.BlockSpec((tm, kk), lambda i: (i, 0)))
        return _call(
            body, name, S((m, n), out_dtype), (m // tm,),
            [a_spec, pl.BlockSpec(b.shape, lambda i: (0, 0), pipeline_mode=pl.Buffered(1))],
            pl.BlockSpec((tm, n), lambda i: (i, 0)), ("parallel",), (a, b), comm=comm)

    tn = _pick(n, 512)

    def body(a_ref, b_ref, o_ref):
        o_ref[...] = _dot(a_ref[...], b_ref[...], dims).astype(o_ref.dtype)

    b_spec = (pl.BlockSpec((kk, tn), lambda j: (0, j)) if mode == "nn" else pl.BlockSpec((tn, kk), lambda j: (j, 0)))
    return _call(
        body, name, S((m, n), out_dtype), (n // tn,), [pl.BlockSpec((m, kk), lambda j: (0, 0)), b_spec],
        pl.BlockSpec((m, tn), lambda j: (0, j)), ("parallel",), (a, b), comm=comm)


def _rms(x, w):
    return x * lax.rsqrt(jnp.mean(x * x, axis=-1, keepdims=True) + RMS_EPS) * w


def _row_tile(rows):
    return rows // 8


def _embed_norm(meta, x, w, rows, name, comm=None):
    n_meta, d = meta.shape
    n_x = x.shape[0]
    last = rows // T - 1
    assert n_meta % 8 == 0 and n_meta < T and n_meta + n_x == last * T + n_meta and last * T >= n_x

    def body(m_ref, x_ref, w_ref, h_ref, hn_ref):
        i = pl.program_id(0)

        @pl.when(i == 0)
        def _():
            h_ref[0:n_meta, :] = m_ref[...]
            h_ref[n_meta:T, :] = x_ref[0:T - n_meta, :]

        @pl.when((i > 0) & (i < last))
        def _():
            h_ref[...] = x_ref[pl.ds(pl.multiple_of(i * T - n_meta, 8), T), :]

        @pl.when(i == last)
        def _():
            h_ref[0:n_meta, :] = x_ref[n_x - n_meta:n_x, :]
            h_ref[n_meta:T, :] = jnp.zeros((T - n_meta, d), F32)

        hn_ref[...] = _rms(h_ref[...], w_ref[...]).astype(hn_ref.dtype)

    row = pl.BlockSpec((T, d), lambda i: (i, 0))
    return _call(body, name, [S((rows, d), F32), S((rows, d), BF16)], (rows // T,),
                 [pl.BlockSpec((n_meta, d), lambda i: (0, 0)), pl.BlockSpec((n_x, d), lambda i: (0, 0)),
                  pl.BlockSpec((1, d), lambda i: (0, 0))], [row, row], ("parallel",), (meta, x, w), comm=comm)


def _resid_norm(h, br, w_post, next_ws, name):
    rows, d = h.shape
    tr = _row_tile(rows)
    has_br = br is not None
    nw = len(next_ws)

    def body(*refs):
        h_ref = refs[0]
        pos = 1
        x = h_ref[...]
        if has_br:
            x = x + _rms(refs[1][...], refs[2][...])
            pos = 3
        w_refs = refs[pos:pos + nw]
        outs = refs[pos + nw:]
        if has_br:
            outs[0][...] = x
            outs = outs[1:]
        for w_ref, o_ref in zip(w_refs, outs):
            o_ref[...] = _rms(x, w_ref[...]).astype(o_ref.dtype)

    row = pl.BlockSpec((tr, d), lambda i: (i, 0))
    vec = pl.BlockSpec((1, d), lambda i: (0, 0))
    ins = [h] + ([br, w_post] if has_br else []) + list(next_ws)
    in_specs = [row] + ([row, vec] if has_br else []) + [vec] * nw
    out_shape = ([S((rows, d), F32)] if has_br else []) + [S((rows, d), BF16)] * nw
    res = pl.pallas_call(body, name=name, out_shape=out_shape, grid=(rows // tr,), in_specs=in_specs,
                         out_specs=[row] * len(out_shape), compiler_params=_cp(("parallel",)))(*ins)
    if has_br:
        return res[0], list(res[1:])
    return h, list(res)


def _rms_bwd(xv, w, dyv):
    r = lax.rsqrt(jnp.mean(xv * xv, axis=-1, keepdims=True) + RMS_EPS)
    wdy = dyv * w
    dx = r * wdy - xv * (r * r * r) * jnp.mean(xv * wdy, axis=-1, keepdims=True)
    return dx, jnp.sum(dyv * xv * r, axis=0, keepdims=True)


def _norm_bwd(x, w, dy, add, out_dtype, name, comm=None, then=None, split_rows=None):
    rows, d = x.shape
    tr = _row_tile(rows)
    has_add = add is not None
    n_in = 3 + has_add + (2 if then is not None else 0)
    if split_rows is not None:
        last, tail = _real_rows(rows, tr, split_rows)

    def body(*refs):
        x_ref, w_ref, dy_ref = refs[:3]
        outs = refs[n_in:]
        dx, dw = _rms_bwd(x_ref[...], w_ref[...], dy_ref[...].astype(F32))
        if has_add:
            dx = dx + refs[3][...]
        if split_rows is None:
            outs[0][...] = dx.astype(outs[0].dtype)
        else:
            i = pl.program_id(0)
            gm_ref, gx_ref = outs[0], outs[1]
            outs = outs[1:]

            @pl.when(i == 0)
            def _():
                gm_ref[...] = dx[0:N_META, :]
                gx_ref[0:tr - N_META, :] = dx[N_META:tr, :]

            @pl.when((i > 0) & (i < last))
            def _():
                gx_ref[pl.ds(pl.multiple_of(i * tr - N_META, 8), tr), :] = dx

            @pl.when(i == last)
            def _():
                gx_ref[split_rows - tail:split_rows, :] = dx[0:tail, :]
        first = pl.program_id(0) == 0

        @pl.when(first)
        def _():
            outs[1][...] = jnp.zeros_like(outs[1])

        outs[1][...] += dw
        if then is not None:
            dx2, dw2 = _rms_bwd(refs[n_in - 2][...], refs[n_in - 1][...], dx)
            outs[2][...] = dx2.astype(outs[2].dtype)

            @pl.when(first)
            def _():
                outs[3][...] = jnp.zeros_like(outs[3])

            outs[3][...] += dw2

    row = pl.BlockSpec((tr, d), lambda i: (i, 0))
    vec = pl.BlockSpec((1, d), lambda i: (0, 0))
    ins = [x, w, dy] + ([add] if has_add else []) + (list(then) if then is not None else [])
    in_specs = [row, vec, row] + ([row] if has_add else []) + ([row, vec] if then is not None else [])
    out_shape = [S((rows, d), out_dtype), S((1, d), F32)] + ([S((rows, d), BF16), S((1, d), F32)] if then is not None else [])
    out_specs = [row, vec] * (len(out_shape) // 2)
    if split_rows is not None:
        out_shape = [S((N_META, d), F32), S((split_rows, d), F32)] + out_shape[1:]
        out_specs = [pl.BlockSpec((N_META, d), lambda i: (0, 0)), pl.BlockSpec((split_rows, d), lambda i: (0, 0))] + out_specs[1:]
    return _call(body, name, out_shape, (rows // tr,), in_specs, out_specs, ("arbitrary",), ins, comm=comm)


def _real_rows(rows, tr, n_x):
    last = (N_META + n_x - 1) // tr
    tail = N_META + n_x - last * tr
    assert last == rows // tr - 1 and N_META % 8 == 0 and tail % 8 == 0 and N_META < tr
    return last, tail


def _final_loss(h, br, w_post, target, name):
    rows, d = h.shape
    tr = _row_tile(rows)
    n_x = target.shape[0]
    last, tail = _real_rows(rows, tr, n_x)

    def body(h_ref, br_ref, w_ref, t_ref, dh_ref, loss_ref, dbr_ref, dw_ref, tbuf):
        i = pl.program_id(0)

        @pl.when(i == 0)
        def _():
            tbuf[0:N_META, :] = jnp.zeros((N_META, d), F32)
            tbuf[N_META:tr, :] = t_ref[0:tr - N_META, :]

        @pl.when((i > 0) & (i < last))
        def _():
            tbuf[...] = t_ref[pl.ds(pl.multiple_of(i * tr - N_META, 8), tr), :]

        @pl.when(i == last)
        def _():
            tbuf[0:tail, :] = t_ref[n_x - tail:n_x, :]
            if tail < tr:
                tbuf[tail:tr, :] = jnp.zeros((tr - tail, d), F32)

        brv, wv = br_ref[...], w_ref[...]
        y = h_ref[...] + _rms(brv, wv)
        r = i * tr + lax.broadcasted_iota(jnp.int32, (tr, 1), 0)
        real = (r >= N_META) & (r < N_META + SEQ)
        diff = jnp.where(real, y - tbuf[...], 0.0)
        dh = diff * (1.0 / d)
        dh_ref[...] = dh
        dbr, dw = _rms_bwd(brv, wv, dh)
        dbr_ref[...] = dbr.astype(dbr_ref.dtype)

        @pl.when(i == 0)
        def _():
            loss_ref[...] = jnp.zeros_like(loss_ref)
            dw_ref[...] = jnp.zeros_like(dw_ref)

        loss_ref[...] += jnp.sum(diff * diff) * (0.5 / d)
        dw_ref[...] += dw

    row = pl.BlockSpec((tr, d), lambda i: (i, 0))
    vec = pl.BlockSpec((1, d), lambda i: (0, 0))
    return pl.pallas_call(body, name=name,
                          out_shape=[S((rows, d), F32), S((1, LANE), F32), S((rows, d), BF16), S((1, d), F32)],
                          grid=(rows // tr,), in_specs=[row, row, vec, pl.BlockSpec((n_x, d), lambda i: (0, 0))],
                          out_specs=[row, pl.BlockSpec((1, LANE), lambda i: (0, 0)), row, vec],
                          scratch_shapes=[pltpu.VMEM((tr, d), F32)],
                          compiler_params=_cp(("arbitrary",)))(h, br, w_post, target)


def _gatenorm_fwd(y, zx, w, name, comm=None):
    rows, d = y.shape
    tr = _row_tile(rows)

    def body(y_ref, z_ref, w_ref, o_ref):
        z = z_ref[...]
        o_ref[...] = _rms(y_ref[...] * z * _sigmoid(z), w_ref[...]).astype(o_ref.dtype)

    row = pl.BlockSpec((tr, d), lambda i: (i, 0))
    return _call(body, name, S((rows, d), BF16), (rows // tr,), [row, row, pl.BlockSpec((1, d), lambda i: (0, 0))],
                 row, ("parallel",), (y, zx, w), comm=comm)


def _gatenorm_bwd(y, zx, w, dyn, name, comm=None):
    rows, d = y.shape
    tr = _row_tile(rows)

    def body(y_ref, z_ref, w_ref, dyn_ref, dy_ref, dz_ref, dw_ref):
        yv, z = y_ref[...], z_ref[...]
        sg = _sigmoid(z)
        sz = z * sg
        g = yv * sz
        r = lax.rsqrt(jnp.mean(g * g, axis=-1, keepdims=True) + RMS_EPS)
        dyn_v = dyn_ref[...]
        wdy = dyn_v * w_ref[...]
        dg = r * wdy - g * (r * r * r) * jnp.mean(g * wdy, axis=-1, keepdims=True)
        dy_ref[...] = dg * sz
        dz_ref[...] = (dg * yv * sg * (1.0 + z * (1.0 - sg))).astype(dz_ref.dtype)

        @pl.when(pl.program_id(0) == 0)
        def _():
            dw_ref[...] = jnp.zeros_like(dw_ref)

        dw_ref[...] += jnp.sum(dyn_v * g * r, axis=0, keepdims=True)

    row = pl.BlockSpec((tr, d), lambda i: (i, 0))
    vec = pl.BlockSpec((1, d), lambda i: (0, 0))
    return _call(body, name, [S((rows, d), F32), S((rows, d), BF16), S((1, d), F32)], (rows // tr,),
                 [row, row, vec, row], [row, row, vec], ("arbitrary",), (y, zx, w, dyn), comm=comm)


def _shift_down(x, s, rows_iota):
    if s == 0:
        return x
    return jnp.where(rows_iota >= s, pltpu.roll(x, s, 0), 0.0)


def _shift_up(x, s, rows_iota):
    if s == 0:
        return x
    rows = x.shape[0]
    return jnp.where(rows_iota < rows - s, pltpu.roll(x, rows - s, 0), 0.0)


def _r16(v):
    return v.astype(BF16).astype(F32)


def _conv_taps(x, taps, rows_iota):
    x = _r16(x)
    return [_shift_down(x, taps - 1 - k, rows_iota) for k in range(taps)]


def _conv(x, w_ref, b_ref, taps, rows_iota, shifted=None):
    shifted = _conv_taps(x, taps, rows_iota) if shifted is None else shifted
    acc = jnp.zeros_like(shifted[0])
    for k in range(taps):
        acc = acc + _r16(w_ref[k:k + 1, :]) * shifted[k]
    return acc + b_ref[...]


def _conv_bwd(shifted, du, w_ref, dw_ref, db_ref, taps, rows_iota):
    db_ref[...] = jnp.sum(du, axis=0, keepdims=True)
    du = _r16(du)
    dx = jnp.zeros_like(du)
    for k in range(taps):
        dx = dx + _r16(w_ref[k:k + 1, :]) * _shift_up(du, taps - 1 - k, rows_iota)
        dw_ref[k:k + 1, :] = jnp.sum(du * shifted[k], axis=0, keepdims=True)
    return dx


def _conv_silu_fwd(zx, w, b, name, comm=None):
    rows = zx.shape[0]
    cb = 512
    off = D_INNER // cb

    def body(x_ref, w_ref, b_ref, o_ref):
        it = lax.broadcasted_iota(jnp.int32, (rows, 1), 0)
        u = _conv(x_ref[...], w_ref, b_ref, SSM_CONV, it)
        o_ref[...] = u * _sigmoid(u)

    return _call(
        body, name, S((rows, D_XBC), F32), (D_XBC // cb,),
        [pl.BlockSpec((rows, cb), lambda j: (0, off + j)), pl.BlockSpec((SSM_CONV, cb), lambda j: (0, j)),
         pl.BlockSpec((1, cb), lambda j: (0, j))],
        pl.BlockSpec((rows, cb), lambda j: (0, j)), ("parallel",), (zx, w, b), comm=comm)


def _conv_silu_bwd(zx, dxs, dbm, dcm, w, b, name, comm=None):
    rows = zx.shape[0]
    cb = 256
    off = D_INNER // cb
    nx, nbc = D_INNER // cb, D_BC // cb

    def body(x_ref, dx_in, db_in, dc_in, w_ref, b_ref, dx_ref, dw_ref, db_ref, dbuf):
        j = pl.program_id(0)
        for cond, src in ((j < nx, dx_in), ((j >= nx) & (j < nx + nbc), db_in), (j >= nx + nbc, dc_in)):
            @pl.when(cond)
            def _(src=src):
                dbuf[...] = src[...]
        it = lax.broadcasted_iota(jnp.int32, (rows, 1), 0)
        xs = _conv_taps(x_ref[...], SSM_CONV, it)
        u = _conv(None, w_ref, b_ref, SSM_CONV, it, xs)
        sg = _sigmoid(u)
        du = dbuf[...] * sg * (1.0 + u * (1.0 - sg))
        dx_ref[...] = _conv_bwd(xs, du, w_ref, dw_ref, db_ref, SSM_CONV, it).astype(dx_ref.dtype)

    def part(first, count):
        return pl.BlockSpec((rows, cb), lambda j: (0, jnp.clip(j - first, 0, count - 1)))

    col = pl.BlockSpec((rows, cb), lambda j: (0, j))
    wsp = pl.BlockSpec((SSM_CONV, cb), lambda j: (0, j))
    bsp = pl.BlockSpec((1, cb), lambda j: (0, j))
    return _call(
        body, name, [S((rows, D_XBC), BF16), S((SSM_CONV, D_XBC), F32), S((1, D_XBC), F32)], (D_XBC // cb,),
        [pl.BlockSpec((rows, cb), lambda j: (0, off + j)), part(0, nx), part(nx, nbc), part(nx + nbc, nbc), wsp, bsp],
        [col, wsp, bsp], ("arbitrary",), (zx, dxs, dbm, dcm, w, b), scratch=[pltpu.VMEM((rows, cb), F32)], comm=comm)


def _ffn_act_fwd(u, w, b, name, comm=None):
    rows = u.shape[0]
    cb = 256
    nb = D_FF // cb

    def body(g_ref, v_ref, wg_ref, wv_ref, bg_ref, bv_ref, o_ref):
        it = lax.broadcasted_iota(jnp.int32, (rows, 1), 0)
        g = _conv(g_ref[...], wg_ref, bg_ref, FFN_CONV, it)
        v = _conv(v_ref[...], wv_ref, bv_ref, FFN_CONV, it)
        o_ref[...] = (g * _sigmoid(g) * v).astype(o_ref.dtype)

    def sp(r, shift):
        return pl.BlockSpec((r, cb), lambda j: (0, shift + j))

    return _call(
        body, name, S((rows, D_FF), BF16), (nb,),
        [sp(rows, 0), sp(rows, nb), sp(FFN_CONV, 0), sp(FFN_CONV, nb), sp(1, 0), sp(1, nb)],
        sp(rows, 0), ("parallel",), (u, u, w, w, b, b), comm=comm)


def _ffn_act_bwd(u, dact, w, b, name, comm=None):
    rows = u.shape[0]
    cb = 256
    nb = D_FF // cb

    def body(g_ref, v_ref, d_ref, wg_ref, wv_ref, bg_ref, bv_ref, du_ref, dw_ref, db_ref):
        it = lax.broadcasted_iota(jnp.int32, (rows, 1), 0)
        xg, xv = _conv_taps(g_ref[...], FFN_CONV, it), _conv_taps(v_ref[...], FFN_CONV, it)
        g = _conv(None, wg_ref, bg_ref, FFN_CONV, it, xg)
        v = _conv(None, wv_ref, bv_ref, FFN_CONV, it, xv)
        sg = _sigmoid(g)
        d = d_ref[...]
        dgate = d * v * sg * (1.0 + g * (1.0 - sg))
        dval = d * g * sg
        du_ref[0] = _conv_bwd(xg, dgate, wg_ref, dw_ref.at[0], db_ref.at[0], FFN_CONV, it).astype(du_ref.dtype)
        du_ref[1] = _conv_bwd(xv, dval, wv_ref, dw_ref.at[1], db_ref.at[1], FFN_CONV, it).astype(du_ref.dtype)

    def sp(r, shift):
        return pl.BlockSpec((r, cb), lambda j: (0, shift + j))

    def both(r):
        return pl.BlockSpec((2, r, cb), lambda j: (0, 0, j))

    return _call(
        body, name, [S((2, rows, D_FF), BF16), S((2, FFN_CONV, D_FF), F32), S((2, 1, D_FF), F32)], (nb,),
        [sp(rows, 0), sp(rows, nb), sp(rows, 0), sp(FFN_CONV, 0), sp(FFN_CONV, nb), sp(1, 0), sp(1, nb)],
        [both(rows), both(FFN_CONV), both(1)], ("parallel",), (u, u, dact, w, w, b, b), comm=comm)


def _ssd_consts(dtp_ref, bias_ref, alog_ref, hg):
    lane = lax.broadcasted_iota(jnp.int32, (1, LANE), 1)
    pre = dtp_ref[...] + bias_ref[...]
    dt = _softplus(pre)
    a_row = jnp.where(lane < hg, -jnp.exp(alog_ref[...]), 0.0)
    ri = lax.broadcasted_iota(jnp.int32, (T, T), 0)
    ci = lax.broadcasted_iota(jnp.int32, (T, T), 1)
    cs = _dot_hi((ri >= ci).astype(F32), dt * a_row)
    return pre, dt, a_row, cs, ri, ci, lane


def _head_rows(src, hg):
    return jnp.concatenate([jnp.broadcast_to(src[k:k + 1, :], (HEAD_P, src.shape[1])) for k in range(hg)], axis=0)


def _ssd_fwd(xbc, zx, bias, alog, dsk, name, comm=None):
    rows = xbc.shape[0]
    nc = rows // T
    hg = SSM_HEADS // SSM_GROUPS
    gw = hg * HEAD_P
    xoff, boff, coff = 0, D_INNER // D_STATE, (D_INNER + D_BC) // D_STATE
    dtoff = (D_INNER + D_XBC) // LANE

    def body(x_ref, b_ref, c_ref, dtp_ref, bias_ref, alog_ref, dsk_ref, y_ref, hst_ref, hs):
        c = pl.program_id(1)

        @pl.when(c == 0)
        def _():
            hs[...] = jnp.zeros_like(hs)

        _, dt, _, cs, ri, ci, _ = _ssd_consts(dtp_ref, bias_ref, alog_ref, hg)
        cst, dtt = cs.T, dt.T
        xt = x_ref[...].T
        bb, cbf = b_ref[...].astype(BF16), c_ref[...].astype(BF16)
        gt = _dot(bb, cbf, _NT)
        causal_t = ci >= ri
        dskv = dsk_ref[...]
        hall = hs[...]
        hst_ref[0, 0] = hall
        cs8 = cst[0:8, :]
        cl8 = cs8[:, T - 1:T]
        xdt = xt * _head_rows(dtt, hg)
        yo = _head_rows(jnp.exp(cs8), hg) * _dot(hall.astype(BF16), cbf, _NT)
        st = _dot((xdt * _head_rows(jnp.exp(cl8 - cs8), hg)).astype(BF16), bb)
        hs[...] = _head_rows(jnp.exp(cl8), hg) * hall + st
        yds = []
        for k in range(hg):
            sl = slice(k * HEAD_P, (k + 1) * HEAD_P)
            lt = jnp.exp(jnp.where(causal_t, cst[k:k + 1, :] - cs[:, k:k + 1], NEG))
            yds.append(_dot(xdt[sl, :].astype(BF16), (gt * lt).astype(BF16)))
        dsk_r = jnp.concatenate([jnp.broadcast_to(dskv[:, k:k + 1], (HEAD_P, 1)) for k in range(hg)], axis=0)
        y_ref[...] = (jnp.concatenate(yds, axis=0) + yo + dsk_r * xt).T

    vec = pl.BlockSpec((1, LANE), lambda g, c: (0, g))
    return _call(
        body, name, [S((rows, D_INNER), F32), S((nc, SSM_GROUPS, gw, D_STATE), F32)], (SSM_GROUPS, nc),
        [pl.BlockSpec((T, gw), lambda g, c: (c, xoff + g)),
         pl.BlockSpec((T, D_STATE), lambda g, c: (c, boff + g)),
         pl.BlockSpec((T, D_STATE), lambda g, c: (c, coff + g)),
         pl.BlockSpec((T, LANE), lambda g, c: (c, dtoff + g)), vec, vec, vec],
        [pl.BlockSpec((T, gw), lambda g, c: (c, g)), pl.BlockSpec((1, 1, gw, D_STATE), lambda g, c: (c, g, 0, 0))],
        ("parallel", "arbitrary"), (xbc, xbc, xbc, zx, bias, alog, dsk),
        scratch=[pltpu.VMEM((gw, D_STATE), F32)], comm=comm)


def _ssd_bwd(xbc, zx, bias, alog, dsk, dy, hst, name, comm=None):
    rows = xbc.shape[0]
    nc = rows // T
    hg = SSM_HEADS // SSM_GROUPS
    gw = hg * HEAD_P
    boff, coff = D_INNER // D_STATE, (D_INNER + D_BC) // D_STATE
    dtoff = (D_INNER + D_XBC) // LANE

    def body(x_ref, b_ref, c_ref, dtp_ref, bias_ref, alog_ref, dsk_ref, dy_ref, hst_ref,
             dx_ref, db_ref, dc_ref, ddtp_ref, dalog_ref, ddsk_ref, dbias_ref, dhs):
        step = pl.program_id(1)

        @pl.when(step == 0)
        def _():
            dhs[...] = jnp.zeros_like(dhs)
            dalog_ref[...] = jnp.zeros_like(dalog_ref)
            ddsk_ref[...] = jnp.zeros_like(ddsk_ref)
            dbias_ref[...] = jnp.zeros_like(dbias_ref)

        pre, dt, a_row, cs, ri, ci, lane = _ssd_consts(dtp_ref, bias_ref, alog_ref, hg)
        cst, dtt = cs.T, dt.T
        xt, dyt = x_ref[...].T, dy_ref[...].T
        bb, cbf = b_ref[...].astype(BF16), c_ref[...].astype(BF16)
        gt = _dot(bb, cbf, _NT)
        causal_t = ci >= ri
        dskv = dsk_ref[...]
        hall, dhall = hst_ref[0, 0], dhs[...]
        head_row = lax.broadcasted_iota(jnp.int32, (T, 1), 0)
        last_l = lax.broadcasted_iota(jnp.int32, (1, T), 1) == T - 1
        cs8, dt8 = cst[0:8, :], dtt[0:8, :]
        cl8 = cs8[:, T - 1:T]
        e8, wdec8 = jnp.exp(cs8), jnp.exp(cl8 - cs8)
        w8 = wdec8 * dt8
        dt_r, e_r, w_r, ecl_r = _head_rows(dt8, hg), _head_rows(e8, hg), _head_rows(w8, hg), _head_rows(jnp.exp(cl8), hg)
        dsk_r = jnp.concatenate([jnp.broadcast_to(dskv[:, k:k + 1], (HEAD_P, 1)) for k in range(hg)], axis=0)
        hb, dhb = hall.astype(BF16), dhall.astype(BF16)
        xdt = xt * dt_r
        dye = (dyt * e_r).astype(BF16)
        rt = _dot(dhb, bb, _NT)
        yo = e_r * _dot(hb, cbf, _NT)
        dhs[...] = ecl_r * dhall + _dot(dye, cbf)
        dc_acc = _dot(dye, hb, _TN)
        db_acc = _dot((xt * w_r).astype(BF16), dhb, _TN)
        rtx, dyyo, hdh, dyx = rt * xt, dyt * yo, dhall * hall, dyt * xt
        dgt = jnp.zeros((T, T), F32)
        ddt_rows = jnp.zeros((T, T), F32)
        dcs_rows = jnp.zeros((T, T), F32)
        qrow_cols = jnp.zeros((T, LANE), F32)
        ddsk_acc = jnp.zeros((1, LANE), F32)
        dxdts = []
        for k in range(hg):
            sl = slice(k * HEAD_P, (k + 1) * HEAD_P)
            lt = jnp.exp(jnp.where(causal_t, cst[k:k + 1, :] - cs[:, k:k + 1], NEG))
            mpt = gt * lt
            dyb = dyt[sl, :].astype(BF16)
            dxdt = _dot(dyb, mpt.astype(BF16), _NT)
            dmt = _dot(xdt[sl, :].astype(BF16), dyb, _TN)
            dgt = dgt + dmt * lt
            q = dmt * mpt
            q_rows = jnp.sum(q, axis=1, keepdims=True)
            q_cols = jnp.sum(q, axis=0, keepdims=True)
            dxdts.append(dxdt)
            xz = jnp.sum(xt[sl, :] * dxdt, axis=0, keepdims=True)
            dw = jnp.sum(rtx[sl, :], axis=0, keepdims=True)
            wk, wdeck = w8[k:k + 1, :], wdec8[k:k + 1, :]
            dcl = jnp.exp(cl8[k:k + 1, :]) * jnp.sum(hdh[sl, :]) + jnp.sum(dw * wk)
            dcs_r = jnp.sum(dyyo[sl, :], axis=0, keepdims=True) + q_cols - dw * wk + jnp.where(last_l, dcl, 0.0)
            onehot = (lane == k).astype(F32)
            ddt_rows = ddt_rows + jnp.where(head_row == k, xz + dw * wdeck, 0.0)
            dcs_rows = dcs_rows + jnp.where(head_row == k, dcs_r, 0.0)
            qrow_cols = qrow_cols + q_rows * onehot
            ddsk_acc = ddsk_acc + jnp.sum(dyx[sl, :]) * onehot
        dx_ref[...] = (dt_r * jnp.concatenate(dxdts, axis=0) + dsk_r * dyt + rt * w_r).T
        dc_ref[...] = _dot(dgt.T.astype(BF16), bb) + dc_acc
        db_ref[...] = _dot(dgt.astype(BF16), cbf) + db_acc
        da = _dot_hi((ci >= ri).astype(F32), dcs_rows.T - qrow_cols)
        ddtp = (ddt_rows.T + da * a_row) * _sigmoid(pre)
        ddtp = jnp.where(lane < hg, ddtp, 0.0)
        ddtp_ref[...] = ddtp
        dbias_ref[...] += jnp.sum(ddtp, axis=0, keepdims=True)
        dalog_ref[...] += jnp.sum(da * dt, axis=0, keepdims=True) * a_row
        ddsk_ref[...] += ddsk_acc

    def rc(c):
        return nc - 1 - c

    vec = pl.BlockSpec((1, LANE), lambda g, c: (0, g))
    xsp = pl.BlockSpec((T, gw), lambda g, c: (rc(c), g))
    return _call(
        body, name,
        [S((rows, D_INNER), F32), S((rows, D_BC), F32), S((rows, D_BC), F32),
         S((rows, SSM_GROUPS * LANE), F32), S((1, SSM_GROUPS * LANE), F32),
         S((1, SSM_GROUPS * LANE), F32), S((1, SSM_GROUPS * LANE), F32)],
        (SSM_GROUPS, nc),
        [xsp,
         pl.BlockSpec((T, D_STATE), lambda g, c: (rc(c), boff + g)),
         pl.BlockSpec((T, D_STATE), lambda g, c: (rc(c), coff + g)),
         pl.BlockSpec((T, LANE), lambda g, c: (rc(c), dtoff + g)), vec, vec, vec,
         xsp, pl.BlockSpec((1, 1, gw, D_STATE), lambda g, c: (rc(c), g, 0, 0))],
        [xsp,
         pl.BlockSpec((T, D_STATE), lambda g, c: (rc(c), g)),
         pl.BlockSpec((T, D_STATE), lambda g, c: (rc(c), g)),
         pl.BlockSpec((T, LANE), lambda g, c: (rc(c), g)), vec, vec, vec],
        ("parallel", "arbitrary"), (xbc, xbc, xbc, zx, bias, alog, dsk, dy, hst),
        scratch=[pltpu.VMEM((gw, D_STATE), F32)], comm=comm)


def _attn_tiles(kv_ref, j):
    prev = jnp.maximum(j - 1, 0)
    meta = kv_ref[0:T, :]
    prv = kv_ref[pl.ds(pl.multiple_of(prev * T, T), T), :]
    cur = kv_ref[pl.ds(pl.multiple_of(j * T, T), T), :]
    return jnp.concatenate([meta, prv, cur], axis=0)


def _attn_mask(j):
    r = j * T + lax.broadcasted_iota(jnp.int32, (3 * T, T), 1)
    row = lax.broadcasted_iota(jnp.int32, (3 * T, T), 0)
    t0, t1 = row < T, row < 2 * T
    s = jnp.where(t0, row, (j - 2) * T + row)
    ok = (s <= r) & ((s < N_META) | (s > r - WINDOW))
    use = (t0 & (j >= 2) & (row < N_META)) | (jnp.logical_not(t0) & t1 & (j >= 1)) | jnp.logical_not(t1)
    return ok & use


def _attn_fwd(q, kv, sinks, name, comm=None):
    rows = q.shape[0]
    scale = 1.0 / math.sqrt(ATTN_DH)
    qpk = N_Q_HEADS // N_KV_HEADS

    def body(q_ref, kv_ref, s_ref, o_ref, lse_ref):
        j = pl.program_id(0)
        kv3 = _attn_tiles(kv_ref, j).astype(BF16)
        mask = _attn_mask(j)
        qv = (q_ref[...] * scale).astype(BF16)
        sk = s_ref[...]
        lses = []
        for kh in range(N_KV_HEADS):
            k3 = kv3[:, kh * ATTN_DH:(kh + 1) * ATTN_DH]
            v3 = kv3[:, D_KV + kh * ATTN_DH:D_KV + (kh + 1) * ATTN_DH]
            for g in range(qpk):
                h = kh * qpk + g
                sink = sk[:, h:h + 1]
                sc = jnp.where(mask, _dot(k3, qv[:, h * ATTN_DH:(h + 1) * ATTN_DH], _NT), NEG)
                m = jnp.maximum(jnp.max(sc, axis=0, keepdims=True), sink)
                p = jnp.exp(sc - m)
                den = jnp.sum(p, axis=0, keepdims=True) + jnp.exp(sink - m)
                p = p * (1.0 / den)
                lses.append(m + jnp.log(den))
                o_ref[:, h * ATTN_DH:(h + 1) * ATTN_DH] = _dot(p.astype(BF16), v3, _TN).astype(o_ref.dtype)
        lse_ref[...] = jnp.concatenate(lses, axis=0)

    return _call(
        body, name, [S((rows, D_MODEL), BF16), S((N_Q_HEADS, rows), F32)], (rows // T,),
        [pl.BlockSpec((T, D_MODEL), lambda j: (j, 0)), pl.BlockSpec((rows, 2 * D_KV), lambda j: (0, 0)),
         pl.BlockSpec((1, N_Q_HEADS), lambda j: (0, 0))],
        [pl.BlockSpec((T, D_MODEL), lambda j: (j, 0)), pl.BlockSpec((N_Q_HEADS, T), lambda j: (0, j))],
        ("parallel",), (q, kv, sinks), comm=comm)


def _attn_bwd(q, kv, sinks, do, lse, name, comm=None):
    rows = q.shape[0]
    scale = 1.0 / math.sqrt(ATTN_DH)
    qpk = N_Q_HEADS // N_KV_HEADS

    def body(q_ref, kv_ref, s_ref, do_ref, lse_ref, dq_ref, dkv_ref, ds_ref):
        j = pl.program_id(0)

        @pl.when(j == 0)
        def _():
            dkv_ref[...] = jnp.zeros_like(dkv_ref)
            ds_ref[...] = jnp.zeros_like(ds_ref)

        kv3 = _attn_tiles(kv_ref, j).astype(BF16)
        mask = _attn_mask(j)
        qv = (q_ref[...] * scale).astype(BF16)
        dov = do_ref[...].astype(BF16)
        sk = s_ref[...]
        lsev = lse_ref[...]
        lane = lax.broadcasted_iota(jnp.int32, (1, LANE), 1)
        ds_acc = jnp.zeros((1, LANE), F32)
        prev = jnp.maximum(j - 1, 0)
        mask4 = jnp.concatenate([mask] * qpk, axis=1)
        dqts = []
        for kh in range(N_KV_HEADS):
            ksl = slice(kh * ATTN_DH, (kh + 1) * ATTN_DH)
            vsl = slice(D_KV + kh * ATTN_DH, D_KV + (kh + 1) * ATTN_DH)
            k3, v3 = kv3[:, ksl], kv3[:, vsl]
            heads = [kh * qpk + g for g in range(qpk)]
            q4 = jnp.concatenate([qv[:, h * ATTN_DH:(h + 1) * ATTN_DH] for h in heads], axis=0)
            do4 = jnp.concatenate([dov[:, h * ATTN_DH:(h + 1) * ATTN_DH] for h in heads], axis=0)
            lse4 = jnp.concatenate([lsev[h:h + 1, :] for h in heads], axis=1)
            sink4 = jnp.concatenate([jnp.broadcast_to(sk[:, h:h + 1], (1, T)) for h in heads], axis=1)
            p = jnp.exp(jnp.where(mask4, _dot(k3, q4, _NT), NEG) - lse4)
            ps = jnp.exp(sink4 - lse4)
            dp = _dot(v3, do4, _NT)
            delta = jnp.sum(p * dp, axis=0, keepdims=True)
            dsc = (p * (dp - delta)).astype(BF16)
            dq4 = _dot(k3.T, dsc) * scale
            dk3 = _dot(dsc, q4)
            dv3 = _dot(p.astype(BF16), do4)
            psd = ps * delta
            for g, h in enumerate(heads):
                dqts.append(dq4[:, g * T:(g + 1) * T])
                ds_acc = ds_acc - jnp.sum(psd[:, g * T:(g + 1) * T]) * (lane == h).astype(F32)
            for t, start in enumerate((0, pl.multiple_of(prev * T, T), pl.multiple_of(j * T, T))):
                rsl = pl.ds(start, T)
                dkv_ref[rsl, ksl] += dk3[t * T:(t + 1) * T, :]
                dkv_ref[rsl, vsl] += dv3[t * T:(t + 1) * T, :]
        ds_ref[...] += ds_acc
        dq_ref[...] = jnp.concatenate(dqts, axis=0).T.astype(dq_ref.dtype)

    blk = pl.BlockSpec((T, D_MODEL), lambda j: (j, 0))
    full = pl.BlockSpec((rows, 2 * D_KV), lambda j: (0, 0))
    return _call(
        body, name, [S((rows, D_MODEL), BF16), S((rows, 2 * D_KV), F32), S((1, LANE), F32)], (rows // T,),
        [blk, full, pl.BlockSpec((1, N_Q_HEADS), lambda j: (0, 0)), blk, pl.BlockSpec((N_Q_HEADS, T), lambda j: (0, j))],
        [blk, full, pl.BlockSpec((1, LANE), lambda j: (0, 0))], ("arbitrary",), (q, kv, sinks, do, lse), comm=comm)


BLOCK_BYTES = 1 << 20


def _div_tile(rows, cols):
    cap = max(16, BLOCK_BYTES // (4 * cols))
    best = None
    for t in range(16, min(rows, cap) + 1, 16):
        if rows % t == 0:
            best = t
    return best if best is not None else rows


def _adamw(parts, w, m, v, name, comm=None):
    layers, rows, cols = w.shape
    n = parts[0].shape[0]
    tr = _div_tile(rows, cols)
    tc = _pick(cols, 256) if tr == rows and rows * cols * 4 > 2 * BLOCK_BYTES else cols
    c1 = 1.0 / (1.0 - B1 ** STEP)
    c2 = 1.0 / (1.0 - B2 ** STEP)

    def body(*refs):
        p_refs = refs[:layers]
        w_ref, m_ref, v_ref, g_ref, d_ref, nm_ref, nv_ref = refs[layers:]
        layer = pl.program_id(0)
        for l in range(layers):
            @pl.when(layer == l)
            def _(p_ref=p_refs[l]):
                g = p_ref[0].astype(F32)
                for i in range(1, n):
                    g = g + p_ref[i].astype(F32)
                nm = B1 * m_ref[...] + (1.0 - B1) * g
                nv = B2 * v_ref[...] + (1.0 - B2) * (g * g)
                g_ref[...] = g
                nm_ref[...] = nm
                nv_ref[...] = nv
                d_ref[...] = -LR * ((nm * c1) / (jnp.sqrt(nv * c2) + EPS) + WD * w_ref[...])

    def part_spec(l):
        return pl.BlockSpec((n, tr, tc), lambda k, i, j: (0, jnp.where(k == l, i, 0), jnp.where(k == l, j, 0)))

    row = pl.BlockSpec((None, tr, tc), lambda k, i, j: (k, i, j))
    return _call(body, name, [S((layers, rows, cols), F32)] * 4, (layers, rows // tr, cols // tc),
                 [part_spec(l) for l in range(layers)] + [row, row, row], [row] * 4,
                 ("parallel", "parallel", "parallel"), (*parts, w, m, v), comm=comm)


def _sum_parts(parts, name):
    n, rows, cols = parts[0].shape
    nb = len(parts)
    tr = _div_tile(rows, cols)

    def body(*refs):
        o_ref = refs[nb]
        blk = pl.program_id(0)
        for l in range(nb):
            @pl.when(blk == l)
            def _(p_ref=refs[l]):
                g = p_ref[0].astype(F32)
                for i in range(1, n):
                    g = g + p_ref[i].astype(F32)
                o_ref[...] = g

    def part_spec(l):
        return pl.BlockSpec((n, tr, cols), lambda k, i: (0, jnp.where(k == l, i, 0), 0))

    per = rows // tr
    return pl.pallas_call(body, name=name, out_shape=S((nb * rows, cols), F32), grid=(nb, per),
                          in_specs=[part_spec(l) for l in range(nb)],
                          out_specs=pl.BlockSpec((tr, cols), lambda k, i: (k * per + i, 0)),
                          compiler_params=_cp(("parallel", "parallel")))(*parts)


def _col_segments(ws, runs):
    segs = []
    for glo, mlo, n in runs:
        while n > 0:
            d, off = divmod(glo, ws)
            take = min(n, ws - off)
            segs.append((d, off, mlo, take))
            glo, mlo, n = glo + take, mlo + take, n - take
    return segs


def _assemble_cols(gs, width, segs, name):
    _, rows, ws = gs[0].shape
    nb = len(gs)
    rb = _div_tile(rows, width // 2)
    per = rows // rb

    def body(*refs):
        o_ref = refs[nb]
        piece = pl.program_id(0)
        for l in range(nb):
            @pl.when(piece == l)
            def _(g_ref=refs[l]):
                o_ref[...] = jnp.zeros_like(o_ref)
                for d, off, mlo, n in segs:
                    o_ref[:, mlo:mlo + n] = g_ref[d, :, off:off + n]

    def piece_spec(l):
        return pl.BlockSpec((N_DEV, rb, ws), lambda k, i: (0, jnp.where(k == l, i, 0), 0))

    return pl.pallas_call(
        body, name=name, out_shape=S((nb * rows, width), gs[0].dtype), grid=(nb, per),
        in_specs=[piece_spec(l) for l in range(nb)],
        out_specs=pl.BlockSpec((rb, width), lambda k, i: (k * per + i, 0)),
        compiler_params=_cp(("parallel", "parallel")))(*gs)


def _scatter_cols(dw, ws, segs, name):
    rows, width = dw.shape
    rb = _div_tile(rows, width)

    def body(w_ref, o_ref):
        for d, off, mlo, n in segs:
            o_ref[d, :, off:off + n] = w_ref[:, mlo:mlo + n].astype(o_ref.dtype)

    return pl.pallas_call(
        body, name=name, out_shape=S((N_DEV, rows, ws), BF16), grid=(rows // rb,),
        in_specs=[pl.BlockSpec((rb, width), lambda i: (i, 0))],
        out_specs=pl.BlockSpec((N_DEV, rb, ws), lambda i: (0, i, 0)), compiler_params=_cp(("parallel",)))(dw)


def _gather_comm(xs):
    n = len(xs)

    def setup(x_refs, out_refs, sems):
        send_sems, recv_sems, local_sems = sems
        mx, my, mc = lax.axis_index("x"), lax.axis_index("y"), lax.axis_index("c")
        me, sibling = (mx, my, mc), (mx, my, 1 - mc)
        chips = [(1 - mx, my), (mx, 1 - my), (1 - mx, 1 - my)]

        def blk(a, px, py, pc):
            return out_refs[a].at[4 * px + 2 * py + pc]

        def copy(a, k, block, to, src=None):
            return pltpu.make_async_remote_copy(
                src_ref=blk(a, *block) if src is None else src, dst_ref=blk(a, *block),
                send_sem=send_sems.at[a, k], recv_sem=recv_sems.at[a, k], device_id=to, device_id_type=_MESH)

        mine = [pltpu.make_async_copy(x_refs[a], blk(a, *me), local_sems.at[a]) for a in range(n)]
        own = []
        for a in range(n):
            own.append(copy(a, 0, me, sibling, src=x_refs[a]))
            own += [copy(a, 1 + i, me, (*chip, mc), src=x_refs[a]) for i, chip in enumerate(chips)]
        return me, sibling, chips, mc, copy, mine, own

    def first(x_refs, out_refs, sems):
        _, _, _, _, _, mine, own = setup(x_refs, out_refs, sems)
        for cp in mine + own:
            cp.start()

    def last(x_refs, out_refs, sems):
        me, sibling, chips, mc, copy, mine, own = setup(x_refs, out_refs, sems)
        passed = []
        for a in range(n):
            for i, chip in enumerate(chips):
                copy(a, 1 + i, (*chip, mc), me).wait_recv()
                passed.append(copy(a, 4 + i, (*chip, mc), sibling))
                passed[-1].start()
        for a in range(n):
            copy(a, 0, sibling, me).wait_recv()
            for i, chip in enumerate(chips):
                copy(a, 4 + i, (*chip, 1 - mc), me).wait_recv()
        for cp in own + passed:
            cp.wait_send()
        for cp in mine:
            cp.wait()

    return _Comm(list(xs), [S((N_DEV,) + x.shape, x.dtype) for x in xs],
                 [pltpu.SemaphoreType.DMA((n, 7)), pltpu.SemaphoreType.DMA((n, 7)), pltpu.SemaphoreType.DMA((n,))],
                 first, last)


def _swap_comm(gs):
    n = len(gs)

    def copies(g_refs, out_refs, sems):
        send_sems, recv_sems = sems
        mx, my, mc = lax.axis_index("x"), lax.axis_index("y"), lax.axis_index("c")
        return [pltpu.make_async_remote_copy(
            src_ref=g_refs[a].at[2 * k + 1 - mc], dst_ref=out_refs[a].at[k], send_sem=send_sems.at[a, k],
            recv_sem=recv_sems.at[a, k], device_id=(mx, my, 1 - mc), device_id_type=_MESH)
            for a in range(n) for k in range(4)]

    def first(g_refs, out_refs, sems):
        for cp in copies(g_refs, out_refs, sems):
            cp.start()

    def last(g_refs, out_refs, sems):
        for cp in copies(g_refs, out_refs, sems):
            cp.wait()

    return _Comm(list(gs), [S((4,) + g.shape[1:], g.dtype) for g in gs],
                 [pltpu.SemaphoreType.DMA((n, 4)), pltpu.SemaphoreType.DMA((n, 4))], first, last)


def _chips_comm(parts):
    n = len(parts)

    def copies(p_refs, out_refs, sems):
        send_sems, recv_sems, local_sems = sems
        mx, my, mc = lax.axis_index("x"), lax.axis_index("y"), lax.axis_index("c")
        mychip = 2 * mx + my
        chips = [(1 - mx, my), (mx, 1 - my), (1 - mx, 1 - my)]
        mine = [pltpu.make_async_copy(p_refs[a].at[mychip], out_refs[a].at[mychip], local_sems.at[a])
                for a in range(n)]
        return mine + [pltpu.make_async_remote_copy(
            src_ref=p_refs[a].at[2 * cx + cy], dst_ref=out_refs[a].at[mychip], send_sem=send_sems.at[a, i],
            recv_sem=recv_sems.at[a, i], device_id=(cx, cy, mc), device_id_type=_MESH)
            for a in range(n) for i, (cx, cy) in enumerate(chips)]

    def first(p_refs, out_refs, sems):
        for cp in copies(p_refs, out_refs, sems):
            cp.start()

    def last(p_refs, out_refs, sems):
        for cp in copies(p_refs, out_refs, sems):
            cp.wait()

    return _Comm(list(parts), [S(p.shape, p.dtype) for p in parts],
                 [pltpu.SemaphoreType.DMA((n, 3)), pltpu.SemaphoreType.DMA((n, 3)), pltpu.SemaphoreType.DMA((n,))],
                 first, last)


def _join_comms(comms):
    def split(refs, counts):
        out, p = [], 0
        for cnt in counts:
            out.append(refs[p:p + cnt])
            p += cnt
        return out

    ni = [len(c.ins) for c in comms]
    no = [len(c.out_shapes) for c in comms]
    ns = [len(c.scratch) for c in comms]

    def first(in_refs, out_refs, sems):
        for c, i, o, s in zip(comms, split(in_refs, ni), split(out_refs, no), split(sems, ns)):
            c.first(i, o, s)

    def last(in_refs, out_refs, sems):
        for c, i, o, s in zip(comms, split(in_refs, ni), split(out_refs, no), split(sems, ns)):
            c.last(i, o, s)

    return _Comm([x for c in comms for x in c.ins], [x for c in comms for x in c.out_shapes],
                 [x for c in comms for x in c.scratch], first, last)


def _add_pairs(mine, theirs, core, name):
    _, rows, cols = mine.shape
    tr = _div_tile(rows, cols)

    def body(core_ref, a_ref, b_ref, o_ref):
        o_ref[...] = (a_ref[...].astype(F32) + b_ref[...].astype(F32)).astype(o_ref.dtype)

    return pl.pallas_call(
        body, name=name, out_shape=S((4, rows, cols), BF16),
        grid_spec=pltpu.PrefetchScalarGridSpec(
            num_scalar_prefetch=1, grid=(4, rows // tr),
            in_specs=[pl.BlockSpec((None, tr, cols), lambda k, i, c: (2 * k + c[0], i, 0)),
                      pl.BlockSpec((None, tr, cols), lambda k, i, c: (k, i, 0))],
            out_specs=pl.BlockSpec((None, tr, cols), lambda k, i, c: (k, i, 0))),
        compiler_params=_cp(("parallel", "parallel")))(core, mine, theirs)


def _run_comm(comm, name):
    ci, co = len(comm.ins), len(comm.out_shapes)

    def body(*refs):
        comm.first(refs[:ci], refs[ci:ci + co], refs[ci + co:])
        comm.last(refs[:ci], refs[ci:ci + co], refs[ci + co:])

    return pl.pallas_call(body, name=name, out_shape=list(comm.out_shapes), in_specs=[_HBM] * ci,
                          out_specs=[_HBM] * co, scratch_shapes=list(comm.scratch))(*comm.ins)


def _flat_rows(n_elems, mult):
    rows = -(-n_elems // LANE)
    return -(-rows // mult) * mult


def _pack(arrs, lead, mult, dtype):
    lead_shape = arrs[0].shape[:lead]
    flat = jnp.concatenate([a.astype(dtype).reshape(lead_shape + (-1,)) for a in arrs], axis=-1)
    n = flat.shape[-1]
    rows = _flat_rows(n, mult)
    flat = jnp.pad(flat, [(0, 0)] * lead + [(0, rows * LANE - n)])
    return flat.reshape(lead_shape + (rows, LANE))


def _unpack(flat, lead, shapes):
    lead_shape = flat.shape[:lead]
    flat = flat.reshape(lead_shape + (-1,))
    out, off = [], 0
    for shp in shapes:
        n = math.prod(shp)
        out.append(flat[..., off:off + n].reshape(lead_shape + tuple(shp)))
        off += n
    return out


def _split8(full, ax, n):
    shp = full.shape
    return jnp.moveaxis(full.reshape(shp[:ax] + (N_DEV, n) + shp[ax + 1:]), ax, 0)


def _join8(g, ax):
    shp = g.shape[1:]
    return jnp.moveaxis(g, 0, ax).reshape(shp[:ax] + (N_DEV * shp[ax],) + shp[ax + 1:])


def _group_lanes(v, hg):
    v = v.reshape(SSM_GROUPS, hg)
    return jnp.pad(v, ((0, 0), (0, LANE - hg))).reshape(1, SSM_GROUPS * LANE)


def _ungroup_lanes(v, hg):
    return v.reshape(SSM_GROUPS, LANE)[:, :hg].reshape(1, SSM_GROUPS * hg)


def kernel(x, meta_tokens, a_norm_pre, a_w_in, a_conv_w, a_conv_b, a_dt_bias, a_a_log, a_d_skip, a_gate_norm, a_w_out, a_norm_post, kv_norm, w_kv, b_norm_pre, b_w_q, b_sinks, b_w_o, b_norm_post, f_norm_pre, f_w_up, f_conv_w, f_conv_b, f_w_down, f_norm_post, loss_target, m_meta_tokens, m_a_norm_pre, m_a_w_in, m_a_conv_w, m_a_conv_b, m_a_dt_bias, m_a_a_log, m_a_d_skip, m_a_gate_norm, m_a_w_out, m_a_norm_post, m_kv_norm, m_w_kv, m_b_norm_pre, m_b_w_q, m_b_sinks, m_b_w_o, m_b_norm_post, m_f_norm_pre, m_f_w_up, m_f_conv_w, m_f_conv_b, m_f_w_down, m_f_norm_post, v_meta_tokens, v_a_norm_pre, v_a_w_in, v_a_conv_w, v_a_conv_b, v_a_dt_bias, v_a_a_log, v_a_d_skip, v_a_gate_norm, v_a_w_out, v_a_norm_post, v_kv_norm, v_w_kv, v_b_norm_pre, v_b_w_q, v_b_sinks, v_b_w_o, v_b_norm_post, v_f_norm_pre, v_f_w_up, v_f_conv_w, v_f_conv_b, v_f_w_down, v_f_norm_post):
    args = locals()
    wts = {n: args[n] for n in WEIGHTS}
    mom = {n: args["m_" + n] for n in WEIGHTS}
    var = {n: args["v_" + n] for n in WEIGHTS}
    mx, my, mc = lax.axis_index("x"), lax.axis_index("y"), lax.axis_index("c")
    me = 4 * mx + 2 * my + mc
    rows = _seq_rows()
    hg = SSM_HEADS // SSM_GROUPS
    d = D_MODEL

    n_main = D_INNER + D_XBC
    ws_in, ws_up = a_w_in.shape[2], f_w_up.shape[2]
    segs_in = _col_segments(ws_in, [(0, 0, n_main)] + [(n_main + hg * g, n_main + LANE * g, hg)
                                                      for g in range(SSM_GROUPS)])
    segs_up = _col_segments(ws_up, [(0, 0, 2 * D_FF)])
    def gather_of(*ws):
        return _gather_comm([w.astype(BF16) for w in ws])

    small_full, = _run_comm(_gather_comm([_pack([wts[n] for n in SMALL], 0, 8, F32)]), "gather_small")
    full = {}
    for n, g in zip(SMALL, _unpack(small_full, 1, [wts[n].shape for n in SMALL])):
        full[n] = _join8(g, SHARD_AXIS[n])
    (h0, hn0), (g_in,) = _embed_norm(full["meta_tokens"], x[0], full["a_norm_pre"], rows, "embed_norm",
                                     comm=gather_of(a_w_in[0]))
    w_in_all = _assemble_cols([g_in], n_main + SSM_GROUPS * LANE, segs_in, "asm_w_in")
    w_up, w_down = [None, None], [None, None]
    bias_g = _group_lanes(wts["a_dt_bias"], hg)
    alog_g = _group_lanes(wts["a_a_log"], hg)
    dsk_g = _group_lanes(wts["a_d_skip"], hg)
    a_conv_w, a_conv_b = full["a_conv_w"][0], full["a_conv_b"]
    f_cw, f_cb = full["f_conv_w"], wts["f_conv_b"]
    fpre, fpost = wts["f_norm_pre"], wts["f_norm_post"]


    zx, (g_out,) = _mm(hn0, w_in_all, "nn", F32, "mm_in", comm=gather_of(a_w_out[0]))
    w_out = g_out.reshape(D_INNER, d)
    xbc = _conv_silu_fwd(zx, a_conv_w, a_conv_b, "conv_a")
    (y_ssd, hst), (g_up0,) = _ssd_fwd(xbc, zx, bias_g, alog_g, dsk_g, "ssd_fwd", comm=gather_of(f_w_up[0]))
    w_up[0] = _assemble_cols([g_up0], 2 * D_FF, segs_up, "asm_w_up0")
    yn = _gatenorm_fwd(y_ssd, zx, full["a_gate_norm"], "gatenorm")
    mix_a, (g_o,) = _mm(yn, w_out, "nn", F32, "mm_out", comm=gather_of(b_w_o[0]))
    h1, (fn0,) = _resid_norm(h0, mix_a, full["a_norm_post"], [fpre[0:1]], "resid_a")

    half = d // 2
    u0, (g_dn0,) = _mm(fn0, w_up[0], "nn", F32, "mm_up0", comm=gather_of(f_w_down[0]))
    act0, (g_up1a,) = _ffn_act_fwd(u0, f_cw[0], f_cb[0:1], "ffn_act0", comm=gather_of(f_w_up[1, :half]))
    ffn0, (g_kv, g_q) = _mm(act0, g_dn0.reshape(D_FF, d), "nn", F32, "mm_down0", comm=gather_of(w_kv, b_w_q[0]))
    w_kvf, w_q, w_o = g_kv.reshape(d, 2 * D_KV), g_q.reshape(d, d), g_o.reshape(d, d)
    h2, (kvn, bn) = _resid_norm(h1, ffn0, fpost[0:1], [wts["kv_norm"].reshape(1, d), wts["b_norm_pre"]], "resid_f0")
    kv = _mm(kvn, w_kvf, "nn", F32, "mm_kv")
    q = _mm(bn, w_q, "nn", F32, "mm_q")
    (o, lse), (g_up1b,) = _attn_fwd(q, kv, wts["b_sinks"], "attn_fwd", comm=gather_of(f_w_up[1, half:]))
    w_up[1] = _assemble_cols([g_up1a, g_up1b], 2 * D_FF, segs_up, "asm_w_up1")
    mix_b = _mm(o, w_o, "nn", F32, "mm_o")
    h3, (fn1,) = _resid_norm(h2, mix_b, wts["b_norm_post"], [fpre[1:2]], "resid_b")
    u1, (g_dn1,) = _mm(fn1, w_up[1], "nn", F32, "mm_up1", comm=gather_of(f_w_down[1]))
    w_down = [g_dn0.reshape(D_FF, d), g_dn1.reshape(D_FF, d)]
    act1 = _ffn_act_fwd(u1, f_cw[1], f_cb[1:2], "ffn_act1")
    ffn1 = _mm(act1, w_down[1], "nn", F32, "mm_down1")
    dh4, loss_row, dffn1, dw_post1 = _final_loss(h3, ffn1, fpost[1:2], loss_target[0], "loss")
    loss = lax.psum(loss_row[0, 0], ("x", "y", "c"))

    grads = {}

    core = mc.astype(jnp.int32).reshape(1)

    def carried(res, comm):
        return res if comm is not None else (res, None)

    def ffn_bwd(dh_out, dffn, h_in, fn, u, act, i, then, c_dact=None, c_dwdown=None, c_dwup=None, c_dfn=None):
        dact, got_a = carried(_mm(dffn, w_down[i], "nt", F32, f"mm_dact{i}", comm=c_dact), c_dact)
        dw_down, got_b = carried(_mm(act, dffn, "tn", BF16, f"mm_dwdown{i}", comm=c_dwdown), c_dwdown)
        dw_down = dw_down.reshape(N_DEV, -1, d)
        du, dwc, dbc = _ffn_act_bwd(u, dact, f_cw[i], f_cb[i:i + 1], f"ffn_act_bwd{i}")
        dfn, (s_dn, *got_d) = _mm(du, w_up[i], "nt", F32, f"mm_dfn{i}", comm=_join_comms(
            [_swap_comm([dw_down])] + ([c_dfn] if c_dfn is not None else [])))
        sum_dn = _add_pairs(dw_down, s_dn, core, f"rs_add_dn{i}")
        dw_up, got_c = carried(_mm(fn, du, "tn", BF16, f"mm_dwup{i}", comm=c_dwup, shard_cols=ws_up), c_dwup)
        (dh_in, dw_pre, dbranch, dw_branch), (s_up,) = _norm_bwd(
            h_in, fpre[i:i + 1], dfn, dh_out, F32, f"nb_fpre{i}", comm=_swap_comm([dw_up]), then=then)
        sum_up = _add_pairs(dw_up, s_up, core, f"rs_add_up{i}")
        return dh_in, dbranch, dw_branch, dict(sum_down=sum_dn, cw=jnp.concatenate([dwc[0], dwc[1]], axis=1),
                                               cb=jnp.concatenate([dbc[0], dbc[1]], axis=1), sum_up=sum_up,
                                               pre=dw_pre), got_a, got_b, got_c, got_d

    dh3, dmix_b, grads["b_norm_post"], gf1, _, _, _, _ = ffn_bwd(dh4, dffn1, h3, fn1, u1, act1, 1,
                                                                 (mix_b, wts["b_norm_post"]))
    do = _mm(dmix_b, w_o, "nt", F32, "mm_do")
    dw_o = _mm(o, dmix_b, "tn", BF16, "mm_dwo").reshape(N_DEV, -1, d)
    half_up = gf1["sum_up"].shape[1] // 2
    (dq, dkv, dsinks), (p_up1a, s_o) = _attn_bwd(
        q, kv, wts["b_sinks"], do, lse, "attn_bwd",
        comm=_join_comms([_chips_comm([gf1["sum_up"][:, :half_up]]), _swap_comm([dw_o])]))
    sum_o = _add_pairs(dw_o, s_o, core, "rs_add_o")
    grads["b_sinks"] = dsinks[:, :N_Q_HEADS]
    dbn = _mm(dq, w_q, "nt", F32, "mm_dbn")
    dw_q = _mm(bn, dq, "tn", BF16, "mm_dwq").reshape(N_DEV, -1, d)
    dkv16 = dkv.astype(BF16)
    dkvn = _mm(dkv16, w_kvf, "nt", F32, "mm_dkvn")
    dw_kv = _mm(kvn, dkv16, "tn", BF16, "mm_dwkv").reshape(N_DEV, -1, 2 * D_KV)
    (dh2, grads["b_norm_pre"]), (s_q, s_kv) = _norm_bwd(h2, wts["b_norm_pre"], dbn, dh3, F32, "nb_bpre",
                                                        comm=_swap_comm([dw_q, dw_kv]))
    sum_q, sum_kv = _add_pairs(dw_q, s_q, core, "rs_add_q"), _add_pairs(dw_kv, s_kv, core, "rs_add_kv")
    dh2, dw_kvn, dffn0, dw_post0 = _norm_bwd(h2, wts["kv_norm"].reshape(1, d), dkvn, dh2, F32, "nb_kv",
                                             then=(ffn0, fpost[0:1]))
    grads["kv_norm"] = dw_kvn.reshape(d)
    dh1, dmix_a, grads["a_norm_post"], gf0, (p_o,), (p_q, p_kv), (p_dn1,), (p_up1b,) = ffn_bwd(
        dh2, dffn0, h1, fn0, u0, act0, 0, (mix_a, full["a_norm_post"]), c_dact=_chips_comm([sum_o]),
        c_dwdown=_chips_comm([sum_q, sum_kv]), c_dwup=_chips_comm([gf1["sum_down"]]),
        c_dfn=_chips_comm([gf1["sum_up"][:, half_up:]]))
    p_up1 = jnp.concatenate([p_up1a, p_up1b], axis=1)
    grads["f_norm_post"] = jnp.concatenate([dw_post0, dw_post1], axis=0)
    grads["f_norm_pre"] = jnp.concatenate([gf0["pre"], gf1["pre"]], axis=0)
    grads["f_conv_w"] = jnp.stack([gf0["cw"], gf1["cw"]])
    grads["f_conv_b"] = jnp.concatenate([gf0["cb"], gf1["cb"]], axis=0)

    dyn = _mm(dmix_a, w_out, "nt", F32, "mm_dyn")
    dw_out = _mm(yn, dmix_a, "tn", BF16, "mm_dwout").reshape(N_DEV, -1, d)
    (dy_ssd, dz, grads["a_gate_norm"]), (s_out,) = _gatenorm_bwd(y_ssd, zx, full["a_gate_norm"], dyn, "gatenorm_bwd",
                                                                 comm=_swap_comm([dw_out]))
    sum_out = _add_pairs(dw_out, s_out, core, "rs_add_out")
    (dxs, dbm, dcm, ddtp, dalog, ddsk, dbias), (p_up0,) = _ssd_bwd(
        xbc, zx, bias_g, alog_g, dsk_g, dy_ssd, hst, "ssd_bwd", comm=_chips_comm([gf0["sum_up"]]))
    grads["a_a_log"] = _ungroup_lanes(dalog, hg)
    grads["a_d_skip"] = _ungroup_lanes(ddsk, hg)
    grads["a_dt_bias"] = _ungroup_lanes(dbias, hg)
    dpre, dcw, dcb = _conv_silu_bwd(zx, dxs, dbm, dcm, a_conv_w, a_conv_b, "conv_a_bwd")
    grads["a_conv_w"], grads["a_conv_b"] = dcw[None], dcb
    dzx = jnp.concatenate([dz, dpre, ddtp.astype(BF16)], axis=1)
    dw_in_all, (p_dn0,) = _mm(hn0, dzx, "tn", BF16, "mm_dwin", comm=_chips_comm([gf0["sum_down"]]))
    dw_in8 = _scatter_cols(dw_in_all, ws_in, segs_in, "scat_w_in")
    dhn0, (s_in, p_out) = _mm(dzx, w_in_all, "nt", F32, "mm_dhn0",
                              comm=_join_comms([_swap_comm([dw_in8]), _chips_comm([sum_out])]))
    sum_in = _add_pairs(dw_in8, s_in, core, "rs_add_in")
    half_in = sum_in.shape[1] // 2
    (grads["meta_tokens"], g_x, grads["a_norm_pre"]), (p_in_a,) = _norm_bwd(
        h0, full["a_norm_pre"], dhn0, dh1, F32, "nb_apre", comm=_chips_comm([sum_in[:, :half_in]]), split_rows=SEQ)
    grad_x = g_x[None]

    small_local = _pack([_split8(grads[n], SHARD_AXIS[n], wts[n].shape[SHARD_AXIS[n]]) for n in SMALL], 1, 8, F32)
    repl_local = _pack([grads[n] for n in REPL], 0, 8, F32)
    n_sr = small_local.shape[1]
    small_vec = jnp.concatenate([small_local.reshape(N_DEV * n_sr, LANE), repl_local], axis=0)
    tail = _join_comms([_chips_comm([sum_in[:, half_in:]]), _gather_comm([small_vec])])
    parts_big = dict(a_w_out=[p_out], w_kv=[p_kv], b_w_q=[p_q], b_w_o=[p_o], f_w_down=[p_dn0, p_dn1])

    def flat_f32(dct, names, mult):
        return _pack([dct[n] for n in names], 0, mult, F32)

    def adamw_big(n, comm=None):
        shp3 = (len(parts_big[n]),) + parts_big[n][0].shape[1:]
        res = _adamw(parts_big[n], *[dct[n].reshape(shp3) for dct in (wts, mom, var)], f"adamw_{n}", comm=comm)
        res, got = res if comm is not None else (res, None)
        big_out[n] = [r.reshape(wts[n].shape) for r in res]
        return got

    big_out = {}
    def swap_last(a):
        return jnp.swapaxes(a, -1, -2)

    g_up_t = swap_last(_sum_parts([p_up0, p_up1], "sum_w_up").reshape(f_w_up.shape))
    res, (p_in_b, small_all) = _adamw([g_up_t[0:1], g_up_t[1:2]], *[swap_last(dct["f_w_up"]) for dct in (wts, mom, var)],
                                      "adamw_f_w_up", comm=tail)
    big_out["f_w_up"] = [swap_last(r) for r in res]
    for n in BIG:
        if n not in ("f_w_up", "a_w_in"):
            adamw_big(n)
    g_in_t = swap_last(_sum_parts([p_in_a, p_in_b], "sum_w_in"))[None]
    res = _adamw([g_in_t], *[swap_last(dct["a_w_in"]) for dct in (wts, mom, var)], "adamw_a_w_in")
    big_out["a_w_in"] = [swap_last(r) for r in res]
    mine_small = lax.dynamic_slice_in_dim(small_all, me * n_sr, n_sr, axis=1)
    parts_small = jnp.concatenate([mine_small, small_all[:, N_DEV * n_sr:]], axis=1)
    sm_in = [jnp.concatenate([flat_f32(dct, SMALL, 8), flat_f32(dct, REPL, 8)], axis=0)[None] for dct in (wts, mom, var)]
    small_out = [r[0] for r in _adamw([parts_small], *sm_in, "adamw_small")]

    outs = []
    for kind in range(4):
        res = {n: big_out[n][kind] for n in BIG}
        for n, a in zip(SMALL, _unpack(small_out[kind][:n_sr], 0, [wts[n].shape for n in SMALL])):
            res[n] = a
        for n, a in zip(REPL, _unpack(small_out[kind][n_sr:], 0, [wts[n].shape for n in REPL])):
            res[n] = a
        outs.append(res)
    return (loss, grad_x, *[outs[0][n] for n in WEIGHTS], *[outs[1][n] for n in WEIGHTS],
            *[outs[2][n] for n in WEIGHTS], *[outs[3][n] for n in WEIGHTS])
```

```python
import functools
import math

import jax
import jax.numpy as jnp
from jax import lax
from jax.experimental import pallas as pl
from jax.experimental.pallas import tpu as pltpu

F32, BF16 = jnp.float32, jnp.bfloat16
S = jax.ShapeDtypeStruct

D_MODEL = 1024
SEQ = 2048
N_META = 16
D_INNER = 2048
HEAD_P = 64
SSM_HEADS = D_INNER // HEAD_P
SSM_GROUPS = 4
D_STATE = 128
SSM_CONV = 4
D_BC = SSM_GROUPS * D_STATE
D_XBC = D_INNER + 2 * D_BC
ATTN_DH = 64
N_Q_HEADS = D_MODEL // ATTN_DH
N_KV_HEADS = 4
D_KV = N_KV_HEADS * ATTN_DH
WINDOW = 128
D_FF = 2816
FFN_CONV = 3
RMS_EPS = 1e-6
NEG = -1e30
LR, B1, B2, EPS, WD, STEP = 0.001, 0.9, 0.999, 1e-08, 0.01, 10

N_DEV = 8
T = 128
LANE = 128
VMEM_LIMIT = 48 * 1024 * 1024

BIG = ("a_w_in", "a_w_out", "w_kv", "b_w_q", "b_w_o", "f_w_up", "f_w_down")
SMALL = ("meta_tokens", "a_norm_pre", "a_conv_w", "a_conv_b", "a_gate_norm", "a_norm_post", "f_conv_w")
REPL = ("a_dt_bias", "a_a_log", "a_d_skip", "kv_norm", "b_norm_pre", "b_sinks", "b_norm_post",
        "f_norm_pre", "f_conv_b", "f_norm_post")
SHARD_AXIS = dict(a_w_in=2, a_w_out=1, w_kv=0, b_w_q=1, b_w_o=1, f_w_up=2, f_w_down=1, meta_tokens=1,
                  a_norm_pre=1, a_conv_w=2, a_conv_b=1, a_gate_norm=1, a_norm_post=1, f_conv_w=2)
WEIGHTS = ("meta_tokens", "a_norm_pre", "a_w_in", "a_conv_w", "a_conv_b", "a_dt_bias", "a_a_log", "a_d_skip",
           "a_gate_norm", "a_w_out", "a_norm_post", "kv_norm", "w_kv", "b_norm_pre", "b_w_q", "b_sinks", "b_w_o",
           "b_norm_post", "f_norm_pre", "f_w_up", "f_conv_w", "f_conv_b", "f_w_down", "f_norm_post")


def _seq_rows():
    return -(-(N_META + SEQ) // T) * T


def _cp(sem=None):
    return pltpu.CompilerParams(dimension_semantics=sem, vmem_limit_bytes=VMEM_LIMIT)


def _pick(n, target):
    t = min(n, target)
    t -= t % LANE
    while n % t:
        t -= LANE
    return t


def _sigmoid(x):
    return 0.5 * jnp.tanh(0.5 * x) + 0.5


def _softplus(x):
    return jnp.maximum(x, 0.0) + jnp.log(1.0 + jnp.exp(-jnp.abs(x)))


_NN = (((1,), (0,)), ((), ()))
_NT = (((1,), (1,)), ((), ()))
_TN = (((0,), (0,)), ((), ()))


def _dot(a, b, dims=_NN):
    return lax.dot_general(a, b, dims, preferred_element_type=F32)


def _dot_hi(a, b):
    return lax.dot_general(a, b, _NN, precision=lax.Precision.HIGHEST, preferred_element_type=F32)


_HBM = pl.BlockSpec(memory_space=pltpu.HBM)
_MESH = pl.DeviceIdType.MESH


class _Comm:
    def __init__(self, ins, out_shapes, scratch, first, last):
        self.ins, self.out_shapes, self.scratch, self.first, self.last = ins, out_shapes, scratch, first, last


_ANY = pl.BlockSpec(memory_space=pl.ANY)


def _call(body, name, out_shape, grid, in_specs, out_specs, sem, args, scratch=(), comm=None, aliases=None):
    aliases = aliases or {}
    if comm is None:
        return pl.pallas_call(body, name=name, out_shape=out_shape, grid=grid, in_specs=in_specs, out_specs=out_specs,
                              scratch_shapes=list(scratch), input_output_aliases=aliases,
                              compiler_params=_cp(sem))(*args)
    single = not isinstance(out_shape, (list, tuple))
    outs = [out_shape] if single else list(out_shape)
    ospecs = [out_specs] if single else list(out_specs)
    n_in, n_out, n_scr, ci, co = len(in_specs), len(outs), len(scratch), len(comm.ins), len(comm.out_shapes)

    def carrier(*refs):
        p = 0
        parts = []
        for cnt in (n_in, ci, n_out, co, n_scr, len(comm.scratch)):
            parts.append(refs[p:p + cnt])
            p += cnt
        ins, cins, outs_r, couts, scr, cscr = parts
        ids = [pl.program_id(i) for i in range(len(grid))]
        first, last = ids[0] == 0, ids[0] == grid[0] - 1
        for i in range(1, len(grid)):
            first, last = first & (ids[i] == 0), last & (ids[i] == grid[i] - 1)

        @pl.when(first)
        def _():
            comm.first(cins, couts, cscr)

        body(*ins, *outs_r, *scr)

        @pl.when(last)
        def _():
            comm.last(cins, couts, cscr)

    res = pl.pallas_call(
        carrier, name=name, out_shape=outs + list(comm.out_shapes), grid=grid,
        in_specs=list(in_specs) + [_HBM] * ci, out_specs=ospecs + [_HBM] * co,
        scratch_shapes=list(scratch) + list(comm.scratch), input_output_aliases=aliases,
        compiler_params=_cp(("arbitrary",) * len(grid)))(*args, *comm.ins)
    mine = res[0] if single else list(res[:n_out])
    return mine, list(res[n_out:])


def _mm(a, b, mode, out_dtype, name, comm=None, shard_cols=None):
    if mode == "tn":
        m, kk = a.shape
        planes, width = (b.shape[0], b.shape[2]) if b.ndim == 3 else (1, b.shape[1])
        n = planes * width
        tko, tn = _pick(kk, 512), _pick(width, 1536)
        per = width // tn
        b_spec = (pl.BlockSpec((None, m, tn), lambda i, j: (j // per, 0, j % per)) if b.ndim == 3
                  else pl.BlockSpec((m, tn), lambda i, j: (0, j)))
        if shard_cols is None:
            def body(a_ref, b_ref, o_ref):
                o_ref[...] = _dot(a_ref[...], b_ref[...], _TN).astype(o_ref.dtype)

            out_shape, out_spec = S((kk, n), out_dtype), pl.BlockSpec((tko, tn), lambda i, j: (i, j))
        else:
            shards = tn // shard_cols
            assert tn % shard_cols == 0

            def body(a_ref, b_ref, o_ref):
                res = _dot(a_ref[...], b_ref[...], _TN).astype(o_ref.dtype)
                for p in range(shards):
                    o_ref[p] = res[:, p * shard_cols:(p + 1) * shard_cols]

            out_shape = S((n // shard_cols, kk, shard_cols), out_dtype)
            out_spec = pl.BlockSpec((shards, tko, shard_cols), lambda i, j: (j, i, 0))
        return _call(
            body, name, out_shape, (kk // tko, n // tn), [pl.BlockSpec((m, tko), lambda i, j: (0, i)), b_spec],
            out_spec, ("parallel", "parallel"), (a, b), comm=comm)

    planes, width = (a.shape[0], a.shape[2]) if a.ndim == 3 else (1, a.shape[1])
    m, kk = a.shape[-2], planes * width
    n = b.shape[1] if mode == "nn" else b.shape[0]
    dims = _NN if mode == "nn" else _NT

    if kk > 2048:
        tm = m // 4
        assert m % 4 == 0 and tm % 16 == 0

        def body(a_ref, b_ref, o_ref):
            if a.ndim == 2:
                res = _dot(a_ref[...], b_ref[...], dims)
            else:
                res = None
                for p in range(planes):
                    bp = b_ref[p * width:(p + 1) * width, :] if mode == "nn" else b_ref[:, p * width:(p + 1) * width]
                    part = _dot(a_ref[p], bp, dims)
                    res = part if res is None else res + part
            o_ref[...] = res.astype(o_ref.dtype)

        a_spec = (pl.BlockSpec((planes, tm, width), lambda i: (0, i, 0)) if a.ndim == 3
                  else pl.BlockSpec((tm, kk), lambda i: (i, 0)))
        return _call(
            body, name, S((m, n), out_dtype), (m // tm,),
            [a_spec, pl.BlockSpec(b.shape, lambda i: (0, 0), pipeline_mode=pl.Buffered(1))],
            pl.BlockSpec((tm, n), lambda i: (i, 0)), ("parallel",), (a, b), comm=comm)

    tn = _pick(n, 512)

    def body(a_ref, b_ref, o_ref):
        o_ref[...] = _dot(a_ref[...], b_ref[...], dims).astype(o_ref.dtype)

    b_spec = (pl.BlockSpec((kk, tn), lambda j: (0, j)) if mode == "nn" else pl.BlockSpec((tn, kk), lambda j: (j, 0)))
    return _call(
        body, name, S((m, n), out_dtype), (n // tn,), [pl.BlockSpec((m, kk), lambda j: (0, 0)), b_spec],
        pl.BlockSpec((m, tn), lambda j: (0, j)), ("parallel",), (a, b), comm=comm)


def _rms(x, w):
    return x * lax.rsqrt(jnp.mean(x * x, axis=-1, keepdims=True) + RMS_EPS) * w


def _row_tile(rows):
    return rows // 8


def _embed_norm(meta, x, w, rows, name, comm=None):
    n_meta, d = meta.shape
    n_x = x.shape[0]
    last = rows // T - 1
    assert n_meta % 8 == 0 and n_meta < T and n_meta + n_x == last * T + n_meta and last * T >= n_x

    def body(m_ref, x_ref, w_ref, h_ref, hn_ref):
        i = pl.program_id(0)

        @pl.when(i == 0)
        def _():
            h_ref[0:n_meta, :] = m_ref[...]
            h_ref[n_meta:T, :] = x_ref[0:T - n_meta, :]

        @pl.when((i > 0) & (i < last))
        def _():
            h_ref[...] = x_ref[pl.ds(pl.multiple_of(i * T - n_meta, 8), T), :]

        @pl.when(i == last)
        def _():
            h_ref[0:n_meta, :] = x_ref[n_x - n_meta:n_x, :]
            h_ref[n_meta:T, :] = jnp.zeros((T - n_meta, d), F32)

        hn_ref[...] = _rms(h_ref[...], w_ref[...]).astype(hn_ref.dtype)

    row = pl.BlockSpec((T, d), lambda i: (i, 0))
    return _call(body, name, [S((rows, d), F32), S((rows, d), BF16)], (rows // T,),
                 [pl.BlockSpec((n_meta, d), lambda i: (0, 0)), pl.BlockSpec((n_x, d), lambda i: (0, 0)),
                  pl.BlockSpec((1, d), lambda i: (0, 0))], [row, row], ("parallel",), (meta, x, w), comm=comm)


def _resid_norm(h, br, w_post, next_ws, name):
    rows, d = h.shape
    tr = _row_tile(rows)
    has_br = br is not None
    nw = len(next_ws)

    def body(*refs):
        h_ref = refs[0]
        pos = 1
        x = h_ref[...]
        if has_br:
            x = x + _rms(refs[1][...], refs[2][...])
            pos = 3
        w_refs = refs[pos:pos + nw]
        outs = refs[pos + nw:]
        if has_br:
            outs[0][...] = x
            outs = outs[1:]
        for w_ref, o_ref in zip(w_refs, outs):
            o_ref[...] = _rms(x, w_ref[...]).astype(o_ref.dtype)

    row = pl.BlockSpec((tr, d), lambda i: (i, 0))
    vec = pl.BlockSpec((1, d), lambda i: (0, 0))
    ins = [h] + ([br, w_post] if has_br else []) + list(next_ws)
    in_specs = [row] + ([row, vec] if has_br else []) + [vec] * nw
    out_shape = ([S((rows, d), F32)] if has_br else []) + [S((rows, d), BF16)] * nw
    res = pl.pallas_call(body, name=name, out_shape=out_shape, grid=(rows // tr,), in_specs=in_specs,
                         out_specs=[row] * len(out_shape), compiler_params=_cp(("parallel",)))(*ins)
    if has_br:
        return res[0], list(res[1:])
    return h, list(res)


def _rms_bwd(xv, w, dyv):
    r = lax.rsqrt(jnp.mean(xv * xv, axis=-1, keepdims=True) + RMS_EPS)
    wdy = dyv * w
    dx = r * wdy - xv * (r * r * r) * jnp.mean(xv * wdy, axis=-1, keepdims=True)
    return dx, jnp.sum(dyv * xv * r, axis=0, keepdims=True)


def _norm_bwd(x, w, dy, add, out_dtype, name, comm=None, then=None, split_rows=None):
    rows, d = x.shape
    tr = _row_tile(rows)
    has_add = add is not None
    n_in = 3 + has_add + (2 if then is not None else 0)
    if split_rows is not None:
        last, tail = _real_rows(rows, tr, split_rows)

    def body(*refs):
        x_ref, w_ref, dy_ref = refs[:3]
        outs = refs[n_in:]
        dx, dw = _rms_bwd(x_ref[...], w_ref[...], dy_ref[...].astype(F32))
        if has_add:
            dx = dx + refs[3][...]
        if split_rows is None:
            outs[0][...] = dx.astype(outs[0].dtype)
        else:
            i = pl.program_id(0)
            gm_ref, gx_ref = outs[0], outs[1]
            outs = outs[1:]

            @pl.when(i == 0)
            def _():
                gm_ref[...] = dx[0:N_META, :]
                gx_ref[0:tr - N_META, :] = dx[N_META:tr, :]

            @pl.when((i > 0) & (i < last))
            def _():
                gx_ref[pl.ds(pl.multiple_of(i * tr - N_META, 8), tr), :] = dx

            @pl.when(i == last)
            def _():
                gx_ref[split_rows - tail:split_rows, :] = dx[0:tail, :]
        first = pl.program_id(0) == 0

        @pl.when(first)
        def _():
            outs[1][...] = jnp.zeros_like(outs[1])

        outs[1][...] += dw
        if then is not None:
            dx2, dw2 = _rms_bwd(refs[n_in - 2][...], refs[n_in - 1][...], dx)
            outs[2][...] = dx2.astype(outs[2].dtype)

            @pl.when(first)
            def _():
                outs[3][...] = jnp.zeros_like(outs[3])

            outs[3][...] += dw2

    row = pl.BlockSpec((tr, d), lambda i: (i, 0))
    vec = pl.BlockSpec((1, d), lambda i: (0, 0))
    ins = [x, w, dy] + ([add] if has_add else []) + (list(then) if then is not None else [])
    in_specs = [row, vec, row] + ([row] if has_add else []) + ([row, vec] if then is not None else [])
    out_shape = [S((rows, d), out_dtype), S((1, d), F32)] + ([S((rows, d), BF16), S((1, d), F32)] if then is not None else [])
    out_specs = [row, vec] * (len(out_shape) // 2)
    if split_rows is not None:
        out_shape = [S((N_META, d), F32), S((split_rows, d), F32)] + out_shape[1:]
        out_specs = [pl.BlockSpec((N_META, d), lambda i: (0, 0)), pl.BlockSpec((split_rows, d), lambda i: (0, 0))] + out_specs[1:]
    return _call(body, name, out_shape, (rows // tr,), in_specs, out_specs, ("arbitrary",), ins, comm=comm)


def _real_rows(rows, tr, n_x):
    last = (N_META + n_x - 1) // tr
    tail = N_META + n_x - last * tr
    assert last == rows // tr - 1 and N_META % 8 == 0 and tail % 8 == 0 and N_META < tr
    return last, tail


def _final_loss(h, br, w_post, target, name):
    rows, d = h.shape
    tr = _row_tile(rows)
    n_x = target.shape[0]
    last, tail = _real_rows(rows, tr, n_x)

    def body(h_ref, br_ref, w_ref, t_ref, dh_ref, loss_ref, dbr_ref, dw_ref, tbuf):
        i = pl.program_id(0)

        @pl.when(i == 0)
        def _():
            tbuf[0:N_META, :] = jnp.zeros((N_META, d), F32)
            tbuf[N_META:tr, :] = t_ref[0:tr - N_META, :]

        @pl.when((i > 0) & (i < last))
        def _():
            tbuf[...] = t_ref[pl.ds(pl.multiple_of(i * tr - N_META, 8), tr), :]

        @pl.when(i == last)
        def _():
            tbuf[0:tail, :] = t_ref[n_x - tail:n_x, :]
            if tail < tr:
                tbuf[tail:tr, :] = jnp.zeros((tr - tail, d), F32)

        brv, wv = br_ref[...], w_ref[...]
        y = h_ref[...] + _rms(brv, wv)
        r = i * tr + lax.broadcasted_iota(jnp.int32, (tr, 1), 0)
        real = (r >= N_META) & (r < N_META + SEQ)
        diff = jnp.where(real, y - tbuf[...], 0.0)
        dh = diff * (1.0 / d)
        dh_ref[...] = dh
        dbr, dw = _rms_bwd(brv, wv, dh)
        dbr_ref[...] = dbr.astype(dbr_ref.dtype)

        @pl.when(i == 0)
        def _():
            loss_ref[...] = jnp.zeros_like(loss_ref)
            dw_ref[...] = jnp.zeros_like(dw_ref)

        loss_ref[...] += jnp.sum(diff * diff) * (0.5 / d)
        dw_ref[...] += dw

    row = pl.BlockSpec((tr, d), lambda i: (i, 0))
    vec = pl.BlockSpec((1, d), lambda i: (0, 0))
    return pl.pallas_call(body, name=name,
                          out_shape=[S((rows, d), F32), S((1, LANE), F32), S((rows, d), BF16), S((1, d), F32)],
                          grid=(rows // tr,), in_specs=[row, row, vec, pl.BlockSpec((n_x, d), lambda i: (0, 0))],
                          out_specs=[row, pl.BlockSpec((1, LANE), lambda i: (0, 0)), row, vec],
                          scratch_shapes=[pltpu.VMEM((tr, d), F32)],
                          compiler_params=_cp(("arbitrary",)))(h, br, w_post, target)


def _gatenorm_fwd(y, zx, w, name, comm=None):
    rows, d = y.shape
    tr = _row_tile(rows)

    def body(y_ref, z_ref, w_ref, o_ref):
        z = z_ref[...]
        o_ref[...] = _rms(y_ref[...] * z * _sigmoid(z), w_ref[...]).astype(o_ref.dtype)

    row = pl.BlockSpec((tr, d), lambda i: (i, 0))
    return _call(body, name, S((rows, d), BF16), (rows // tr,), [row, row, pl.BlockSpec((1, d), lambda i: (0, 0))],
                 row, ("parallel",), (y, zx, w), comm=comm)


def _gatenorm_bwd(y, zx, w, dyn, name, comm=None):
    rows, d = y.shape
    tr = _row_tile(rows)

    def body(y_ref, z_ref, w_ref, dyn_ref, dy_ref, dz_ref, dw_ref):
        yv, z = y_ref[...], z_ref[...]
        sg = _sigmoid(z)
        sz = z * sg
        g = yv * sz
        r = lax.rsqrt(jnp.mean(g * g, axis=-1, keepdims=True) + RMS_EPS)
        dyn_v = dyn_ref[...]
        wdy = dyn_v * w_ref[...]
        dg = r * wdy - g * (r * r * r) * jnp.mean(g * wdy, axis=-1, keepdims=True)
        dy_ref[...] = dg * sz
        dz_ref[...] = (dg * yv * sg * (1.0 + z * (1.0 - sg))).astype(dz_ref.dtype)

        @pl.when(pl.program_id(0) == 0)
        def _():
            dw_ref[...] = jnp.zeros_like(dw_ref)

        dw_ref[...] += jnp.sum(dyn_v * g * r, axis=0, keepdims=True)

    row = pl.BlockSpec((tr, d), lambda i: (i, 0))
    vec = pl.BlockSpec((1, d), lambda i: (0, 0))
    return _call(body, name, [S((rows, d), F32), S((rows, zx.shape[1]), BF16), S((1, d), F32)], (rows // tr,),
                 [row, row, vec, row], [row, row, vec], ("arbitrary",), (y, zx, w, dyn), comm=comm)


def _shift_down(x, s, rows_iota):
    if s == 0:
        return x
    return jnp.where(rows_iota >= s, pltpu.roll(x, s, 0), 0.0)


def _shift_up(x, s, rows_iota):
    if s == 0:
        return x
    rows = x.shape[0]
    return jnp.where(rows_iota < rows - s, pltpu.roll(x, rows - s, 0), 0.0)


def _r16(v):
    return v.astype(BF16).astype(F32)


def _conv_taps(x, taps, rows_iota):
    x = _r16(x)
    return [_shift_down(x, taps - 1 - k, rows_iota) for k in range(taps)]


def _conv(x, w_ref, b_ref, taps, rows_iota, shifted=None):
    shifted = _conv_taps(x, taps, rows_iota) if shifted is None else shifted
    acc = jnp.zeros_like(shifted[0])
    for k in range(taps):
        acc = acc + _r16(w_ref[k:k + 1, :]) * shifted[k]
    return acc + b_ref[...]


def _conv_bwd(shifted, du, w_ref, dw_ref, db_ref, taps, rows_iota):
    db_ref[...] = jnp.sum(du, axis=0, keepdims=True)
    du = _r16(du)
    dx = jnp.zeros_like(du)
    for k in range(taps):
        dx = dx + _r16(w_ref[k:k + 1, :]) * _shift_up(du, taps - 1 - k, rows_iota)
        dw_ref[k:k + 1, :] = jnp.sum(du * shifted[k], axis=0, keepdims=True)
    return dx


def _conv_silu_fwd(zx, w, b, name, comm=None):
    rows = zx.shape[0]
    cb = 512
    off = D_INNER // cb

    def body(x_ref, w_ref, b_ref, o_ref):
        it = lax.broadcasted_iota(jnp.int32, (rows, 1), 0)
        u = _conv(x_ref[...], w_ref, b_ref, SSM_CONV, it)
        o_ref[...] = u * _sigmoid(u)

    return _call(
        body, name, S((rows, D_XBC), F32), (D_XBC // cb,),
        [pl.BlockSpec((rows, cb), lambda j: (0, off + j)), pl.BlockSpec((SSM_CONV, cb), lambda j: (0, j)),
         pl.BlockSpec((1, cb), lambda j: (0, j))],
        pl.BlockSpec((rows, cb), lambda j: (0, j)), ("parallel",), (zx, w, b), comm=comm)


def _conv_silu_bwd(zx, dxs, dbm, dcm, w, b, dzx, name, comm=None):
    rows = zx.shape[0]
    cb = 256
    off = D_INNER // cb
    nx, nbc = D_INNER // cb, D_BC // cb

    def body(x_ref, dx_in, db_in, dc_in, w_ref, b_ref, dzx_in, dx_ref, dw_ref, db_ref, dbuf):
        del dzx_in
        j = pl.program_id(0)
        for cond, src in ((j < nx, dx_in), ((j >= nx) & (j < nx + nbc), db_in), (j >= nx + nbc, dc_in)):
            @pl.when(cond)
            def _(src=src):
                dbuf[...] = src[...]
        it = lax.broadcasted_iota(jnp.int32, (rows, 1), 0)
        xs = _conv_taps(x_ref[...], SSM_CONV, it)
        u = _conv(None, w_ref, b_ref, SSM_CONV, it, xs)
        sg = _sigmoid(u)
        du = dbuf[...] * sg * (1.0 + u * (1.0 - sg))
        dx_ref[...] = _conv_bwd(xs, du, w_ref, dw_ref, db_ref, SSM_CONV, it).astype(dx_ref.dtype)

    def part(first, count):
        return pl.BlockSpec((rows, cb), lambda j: (0, jnp.clip(j - first, 0, count - 1)))

    col = pl.BlockSpec((rows, cb), lambda j: (0, j))
    wsp = pl.BlockSpec((SSM_CONV, cb), lambda j: (0, j))
    bsp = pl.BlockSpec((1, cb), lambda j: (0, j))
    xbc_cols = pl.BlockSpec((rows, cb), lambda j: (0, off + j))
    return _call(
        body, name, [S(dzx.shape, dzx.dtype), S((SSM_CONV, D_XBC), F32), S((1, D_XBC), F32)], (D_XBC // cb,),
        [xbc_cols, part(0, nx), part(nx, nbc), part(nx + nbc, nbc), wsp, bsp, _ANY],
        [xbc_cols, wsp, bsp], ("arbitrary",), (zx, dxs, dbm, dcm, w, b, dzx), scratch=[pltpu.VMEM((rows, cb), F32)],
        comm=comm, aliases={6: 0})


def _ffn_act_fwd(u, w, b, name, comm=None):
    rows = u.shape[0]
    cb = 256
    nb = D_FF // cb

    def body(g_ref, v_ref, wg_ref, wv_ref, bg_ref, bv_ref, o_ref):
        it = lax.broadcasted_iota(jnp.int32, (rows, 1), 0)
        g = _conv(g_ref[...], wg_ref, bg_ref, FFN_CONV, it)
        v = _conv(v_ref[...], wv_ref, bv_ref, FFN_CONV, it)
        o_ref[...] = (g * _sigmoid(g) * v).astype(o_ref.dtype)

    def sp(r, shift):
        return pl.BlockSpec((r, cb), lambda j: (0, shift + j))

    return _call(
        body, name, S((rows, D_FF), BF16), (nb,),
        [sp(rows, 0), sp(rows, nb), sp(FFN_CONV, 0), sp(FFN_CONV, nb), sp(1, 0), sp(1, nb)],
        sp(rows, 0), ("parallel",), (u, u, w, w, b, b), comm=comm)


def _ffn_act_bwd(u, dact, w, b, name, comm=None):
    rows = u.shape[0]
    cb = 256
    nb = D_FF // cb

    def body(g_ref, v_ref, d_ref, wg_ref, wv_ref, bg_ref, bv_ref, du_ref, dw_ref, db_ref):
        it = lax.broadcasted_iota(jnp.int32, (rows, 1), 0)
        xg, xv = _conv_taps(g_ref[...], FFN_CONV, it), _conv_taps(v_ref[...], FFN_CONV, it)
        g = _conv(None, wg_ref, bg_ref, FFN_CONV, it, xg)
        v = _conv(None, wv_ref, bv_ref, FFN_CONV, it, xv)
        sg = _sigmoid(g)
        d = d_ref[...]
        dgate = d * v * sg * (1.0 + g * (1.0 - sg))
        dval = d * g * sg
        du_ref[0] = _conv_bwd(xg, dgate, wg_ref, dw_ref.at[0], db_ref.at[0], FFN_CONV, it).astype(du_ref.dtype)
        du_ref[1] = _conv_bwd(xv, dval, wv_ref, dw_ref.at[1], db_ref.at[1], FFN_CONV, it).astype(du_ref.dtype)

    def sp(r, shift):
        return pl.BlockSpec((r, cb), lambda j: (0, shift + j))

    def both(r):
        return pl.BlockSpec((2, r, cb), lambda j: (0, 0, j))

    return _call(
        body, name, [S((2, rows, D_FF), BF16), S((2, FFN_CONV, D_FF), F32), S((2, 1, D_FF), F32)], (nb,),
        [sp(rows, 0), sp(rows, nb), sp(rows, 0), sp(FFN_CONV, 0), sp(FFN_CONV, nb), sp(1, 0), sp(1, nb)],
        [both(rows), both(FFN_CONV), both(1)], ("parallel",), (u, u, dact, w, w, b, b), comm=comm)


def _ssd_consts(dtp_ref, bias_ref, alog_ref, hg):
    lane = lax.broadcasted_iota(jnp.int32, (1, LANE), 1)
    pre = dtp_ref[...] + bias_ref[...]
    dt = _softplus(pre)
    a_row = jnp.where(lane < hg, -jnp.exp(alog_ref[...]), 0.0)
    ri = lax.broadcasted_iota(jnp.int32, (T, T), 0)
    ci = lax.broadcasted_iota(jnp.int32, (T, T), 1)
    cs = _dot_hi((ri >= ci).astype(F32), dt * a_row)
    return pre, dt, a_row, cs, ri, ci, lane


def _head_rows(src, hg):
    return jnp.concatenate([jnp.broadcast_to(src[k:k + 1, :], (HEAD_P, src.shape[1])) for k in range(hg)], axis=0)


def _ssd_fwd(xbc, zx, bias, alog, dsk, name, comm=None):
    rows = xbc.shape[0]
    nc = rows // T
    hg = SSM_HEADS // SSM_GROUPS
    gw = hg * HEAD_P
    xoff, boff, coff = 0, D_INNER // D_STATE, (D_INNER + D_BC) // D_STATE
    dtoff = (D_INNER + D_XBC) // LANE

    def body(x_ref, b_ref, c_ref, dtp_ref, bias_ref, alog_ref, dsk_ref, y_ref, hst_ref, hs):
        c = pl.program_id(1)

        @pl.when(c == 0)
        def _():
            hs[...] = jnp.zeros_like(hs)

        _, dt, _, cs, ri, ci, _ = _ssd_consts(dtp_ref, bias_ref, alog_ref, hg)
        cst, dtt = cs.T, dt.T
        xt = x_ref[...].T
        bb, cbf = b_ref[...].astype(BF16), c_ref[...].astype(BF16)
        gt = _dot(bb, cbf, _NT)
        causal_t = ci >= ri
        dskv = dsk_ref[...]
        hall = hs[...]
        hst_ref[0, 0] = hall
        cs8 = cst[0:8, :]
        cl8 = cs8[:, T - 1:T]
        xdt = xt * _head_rows(dtt, hg)
        yo = _head_rows(jnp.exp(cs8), hg) * _dot(hall.astype(BF16), cbf, _NT)
        st = _dot((xdt * _head_rows(jnp.exp(cl8 - cs8), hg)).astype(BF16), bb)
        hs[...] = _head_rows(jnp.exp(cl8), hg) * hall + st
        yds = []
        for k in range(hg):
            sl = slice(k * HEAD_P, (k + 1) * HEAD_P)
            lt = jnp.exp(jnp.where(causal_t, cst[k:k + 1, :] - cs[:, k:k + 1], NEG))
            yds.append(_dot(xdt[sl, :].astype(BF16), (gt * lt).astype(BF16)))
        dsk_r = jnp.concatenate([jnp.broadcast_to(dskv[:, k:k + 1], (HEAD_P, 1)) for k in range(hg)], axis=0)
        y_ref[...] = (jnp.concatenate(yds, axis=0) + yo + dsk_r * xt).T

    vec = pl.BlockSpec((1, LANE), lambda g, c: (0, g))
    return _call(
        body, name, [S((rows, D_INNER), F32), S((nc, SSM_GROUPS, gw, D_STATE), F32)], (SSM_GROUPS, nc),
        [pl.BlockSpec((T, gw), lambda g, c: (c, xoff + g)),
         pl.BlockSpec((T, D_STATE), lambda g, c: (c, boff + g)),
         pl.BlockSpec((T, D_STATE), lambda g, c: (c, coff + g)),
         pl.BlockSpec((T, LANE), lambda g, c: (c, dtoff + g)), vec, vec, vec],
        [pl.BlockSpec((T, gw), lambda g, c: (c, g)), pl.BlockSpec((1, 1, gw, D_STATE), lambda g, c: (c, g, 0, 0))],
        ("parallel", "arbitrary"), (xbc, xbc, xbc, zx, bias, alog, dsk),
        scratch=[pltpu.VMEM((gw, D_STATE), F32)], comm=comm)


def _ssd_bwd(xbc, zx, bias, alog, dsk, dy, hst, dzx, name, comm=None):
    rows = xbc.shape[0]
    nc = rows // T
    hg = SSM_HEADS // SSM_GROUPS
    gw = hg * HEAD_P
    boff, coff = D_INNER // D_STATE, (D_INNER + D_BC) // D_STATE
    dtoff = (D_INNER + D_XBC) // LANE

    def body(x_ref, b_ref, c_ref, dtp_ref, bias_ref, alog_ref, dsk_ref, dy_ref, hst_ref, dzx_in,
             dx_ref, db_ref, dc_ref, ddtp_ref, dalog_ref, ddsk_ref, dbias_ref, dhs):
        del dzx_in
        step = pl.program_id(1)

        @pl.when(step == 0)
        def _():
            dhs[...] = jnp.zeros_like(dhs)
            dalog_ref[...] = jnp.zeros_like(dalog_ref)
            ddsk_ref[...] = jnp.zeros_like(ddsk_ref)
            dbias_ref[...] = jnp.zeros_like(dbias_ref)

        pre, dt, a_row, cs, ri, ci, lane = _ssd_consts(dtp_ref, bias_ref, alog_ref, hg)
        cst, dtt = cs.T, dt.T
        xt, dyt = x_ref[...].T, dy_ref[...].T
        bb, cbf = b_ref[...].astype(BF16), c_ref[...].astype(BF16)
        gt = _dot(bb, cbf, _NT)
        causal_t = ci >= ri
        dskv = dsk_ref[...]
        hall, dhall = hst_ref[0, 0], dhs[...]
        head_row = lax.broadcasted_iota(jnp.int32, (T, 1), 0)
        last_l = lax.broadcasted_iota(jnp.int32, (1, T), 1) == T - 1
        cs8, dt8 = cst[0:8, :], dtt[0:8, :]
        cl8 = cs8[:, T - 1:T]
        e8, wdec8 = jnp.exp(cs8), jnp.exp(cl8 - cs8)
        w8 = wdec8 * dt8
        dt_r, e_r, w_r, ecl_r = _head_rows(dt8, hg), _head_rows(e8, hg), _head_rows(w8, hg), _head_rows(jnp.exp(cl8), hg)
        dsk_r = jnp.concatenate([jnp.broadcast_to(dskv[:, k:k + 1], (HEAD_P, 1)) for k in range(hg)], axis=0)
        hb, dhb = hall.astype(BF16), dhall.astype(BF16)
        xdt = xt * dt_r
        dye = (dyt * e_r).astype(BF16)
        rt = _dot(dhb, bb, _NT)
        yo = e_r * _dot(hb, cbf, _NT)
        dhs[...] = ecl_r * dhall + _dot(dye, cbf)
        dc_acc = _dot(dye, hb, _TN)
        db_acc = _dot((xt * w_r).astype(BF16), dhb, _TN)
        rtx, dyyo, hdh, dyx = rt * xt, dyt * yo, dhall * hall, dyt * xt
        dgt = jnp.zeros((T, T), F32)
        ddt_rows = jnp.zeros((T, T), F32)
        dcs_rows = jnp.zeros((T, T), F32)
        qrow_cols = jnp.zeros((T, LANE), F32)
        ddsk_acc = jnp.zeros((1, LANE), F32)
        dxdts = []
        for k in range(hg):
            sl = slice(k * HEAD_P, (k + 1) * HEAD_P)
            lt = jnp.exp(jnp.where(causal_t, cst[k:k + 1, :] - cs[:, k:k + 1], NEG))
            mpt = gt * lt
            dyb = dyt[sl, :].astype(BF16)
            dxdt = _dot(dyb, mpt.astype(BF16), _NT)
            dmt = _dot(xdt[sl, :].astype(BF16), dyb, _TN)
            dgt = dgt + dmt * lt
            q = dmt * mpt
            q_rows = jnp.sum(q, axis=1, keepdims=True)
            q_cols = jnp.sum(q, axis=0, keepdims=True)
            dxdts.append(dxdt)
            xz = jnp.sum(xt[sl, :] * dxdt, axis=0, keepdims=True)
            dw = jnp.sum(rtx[sl, :], axis=0, keepdims=True)
            wk, wdeck = w8[k:k + 1, :], wdec8[k:k + 1, :]
            dcl = jnp.exp(cl8[k:k + 1, :]) * jnp.sum(hdh[sl, :]) + jnp.sum(dw * wk)
            dcs_r = jnp.sum(dyyo[sl, :], axis=0, keepdims=True) + q_cols - dw * wk + jnp.where(last_l, dcl, 0.0)
            onehot = (lane == k).astype(F32)
            ddt_rows = ddt_rows + jnp.where(head_row == k, xz + dw * wdeck, 0.0)
            dcs_rows = dcs_rows + jnp.where(head_row == k, dcs_r, 0.0)
            qrow_cols = qrow_cols + q_rows * onehot
            ddsk_acc = ddsk_acc + jnp.sum(dyx[sl, :]) * onehot
        dx_ref[...] = (dt_r * jnp.concatenate(dxdts, axis=0) + dsk_r * dyt + rt * w_r).T
        dc_ref[...] = _dot(dgt.T.astype(BF16), bb) + dc_acc
        db_ref[...] = _dot(dgt.astype(BF16), cbf) + db_acc
        da = _dot_hi((ci >= ri).astype(F32), dcs_rows.T - qrow_cols)
        ddtp = (ddt_rows.T + da * a_row) * _sigmoid(pre)
        ddtp = jnp.where(lane < hg, ddtp, 0.0)
        ddtp_ref[...] = ddtp.astype(ddtp_ref.dtype)
        dbias_ref[...] += jnp.sum(ddtp, axis=0, keepdims=True)
        dalog_ref[...] += jnp.sum(da * dt, axis=0, keepdims=True) * a_row
        ddsk_ref[...] += ddsk_acc

    def rc(c):
        return nc - 1 - c

    vec = pl.BlockSpec((1, LANE), lambda g, c: (0, g))
    xsp = pl.BlockSpec((T, gw), lambda g, c: (rc(c), g))
    return _call(
        body, name,
        [S((rows, D_INNER), F32), S((rows, D_BC), F32), S((rows, D_BC), F32),
         S(dzx.shape, dzx.dtype), S((1, SSM_GROUPS * LANE), F32),
         S((1, SSM_GROUPS * LANE), F32), S((1, SSM_GROUPS * LANE), F32)],
        (SSM_GROUPS, nc),
        [xsp,
         pl.BlockSpec((T, D_STATE), lambda g, c: (rc(c), boff + g)),
         pl.BlockSpec((T, D_STATE), lambda g, c: (rc(c), coff + g)),
         pl.BlockSpec((T, LANE), lambda g, c: (rc(c), dtoff + g)), vec, vec, vec,
         xsp, pl.BlockSpec((1, 1, gw, D_STATE), lambda g, c: (rc(c), g, 0, 0)), _ANY],
        [xsp,
         pl.BlockSpec((T, D_STATE), lambda g, c: (rc(c), g)),
         pl.BlockSpec((T, D_STATE), lambda g, c: (rc(c), g)),
         pl.BlockSpec((T, LANE), lambda g, c: (rc(c), dtoff + g)), vec, vec, vec],
        ("parallel", "arbitrary"), (xbc, xbc, xbc, zx, bias, alog, dsk, dy, hst, dzx),
        scratch=[pltpu.VMEM((gw, D_STATE), F32)], comm=comm, aliases={9: 3})


def _attn_tiles(kv_ref, j):
    prev = jnp.maximum(j - 1, 0)
    meta = kv_ref[0:T, :]
    prv = kv_ref[pl.ds(pl.multiple_of(prev * T, T), T), :]
    cur = kv_ref[pl.ds(pl.multiple_of(j * T, T), T), :]
    return jnp.concatenate([meta, prv, cur], axis=0)


def _attn_mask(j):
    r = j * T + lax.broadcasted_iota(jnp.int32, (3 * T, T), 1)
    row = lax.broadcasted_iota(jnp.int32, (3 * T, T), 0)
    t0, t1 = row < T, row < 2 * T
    s = jnp.where(t0, row, (j - 2) * T + row)
    ok = (s <= r) & ((s < N_META) | (s > r - WINDOW))
    use = (t0 & (j >= 2) & (row < N_META)) | (jnp.logical_not(t0) & t1 & (j >= 1)) | jnp.logical_not(t1)
    return ok & use


def _attn_fwd(q, kv, sinks, name, comm=None):
    rows = q.shape[0]
    scale = 1.0 / math.sqrt(ATTN_DH)
    qpk = N_Q_HEADS // N_KV_HEADS

    def body(q_ref, kv_ref, s_ref, o_ref, lse_ref):
        j = pl.program_id(0)
        kv3 = _attn_tiles(kv_ref, j).astype(BF16)
        mask = _attn_mask(j)
        qv = (q_ref[...] * scale).astype(BF16)
        sk = s_ref[...]
        lses = []
        for kh in range(N_KV_HEADS):
            k3 = kv3[:, kh * ATTN_DH:(kh + 1) * ATTN_DH]
            v3 = kv3[:, D_KV + kh * ATTN_DH:D_KV + (kh + 1) * ATTN_DH]
            for g in range(qpk):
                h = kh * qpk + g
                sink = sk[:, h:h + 1]
                sc = jnp.where(mask, _dot(k3, qv[:, h * ATTN_DH:(h + 1) * ATTN_DH], _NT), NEG)
                m = jnp.maximum(jnp.max(sc, axis=0, keepdims=True), sink)
                p = jnp.exp(sc - m)
                den = jnp.sum(p, axis=0, keepdims=True) + jnp.exp(sink - m)
                p = p * (1.0 / den)
                lses.append(m + jnp.log(den))
                o_ref[:, h * ATTN_DH:(h + 1) * ATTN_DH] = _dot(p.astype(BF16), v3, _TN).astype(o_ref.dtype)
        lse_ref[...] = jnp.concatenate(lses, axis=0)

    return _call(
        body, name, [S((rows, D_MODEL), BF16), S((N_Q_HEADS, rows), F32)], (rows // T,),
        [pl.BlockSpec((T, D_MODEL), lambda j: (j, 0)), pl.BlockSpec((rows, 2 * D_KV), lambda j: (0, 0)),
         pl.BlockSpec((1, N_Q_HEADS), lambda j: (0, 0))],
        [pl.BlockSpec((T, D_MODEL), lambda j: (j, 0)), pl.BlockSpec((N_Q_HEADS, T), lambda j: (0, j))],
        ("parallel",), (q, kv, sinks), comm=comm)


def _attn_bwd(q, kv, sinks, do, lse, name, comm=None):
    rows = q.shape[0]
    scale = 1.0 / math.sqrt(ATTN_DH)
    qpk = N_Q_HEADS // N_KV_HEADS

    def body(q_ref, kv_ref, s_ref, do_ref, lse_ref, dq_ref, dkv_ref, ds_ref):
        j = pl.program_id(0)

        @pl.when(j == 0)
        def _():
            dkv_ref[...] = jnp.zeros_like(dkv_ref)
            ds_ref[...] = jnp.zeros_like(ds_ref)

        kv3 = _attn_tiles(kv_ref, j).astype(BF16)
        mask = _attn_mask(j)
        qv = (q_ref[...] * scale).astype(BF16)
        dov = do_ref[...].astype(BF16)
        sk = s_ref[...]
        lsev = lse_ref[...]
        lane = lax.broadcasted_iota(jnp.int32, (1, LANE), 1)
        ds_acc = jnp.zeros((1, LANE), F32)
        prev = jnp.maximum(j - 1, 0)
        mask4 = jnp.concatenate([mask] * qpk, axis=1)
        dqts = []
        for kh in range(N_KV_HEADS):
            ksl = slice(kh * ATTN_DH, (kh + 1) * ATTN_DH)
            vsl = slice(D_KV + kh * ATTN_DH, D_KV + (kh + 1) * ATTN_DH)
            k3, v3 = kv3[:, ksl], kv3[:, vsl]
            heads = [kh * qpk + g for g in range(qpk)]
            q4 = jnp.concatenate([qv[:, h * ATTN_DH:(h + 1) * ATTN_DH] for h in heads], axis=0)
            do4 = jnp.concatenate([dov[:, h * ATTN_DH:(h + 1) * ATTN_DH] for h in heads], axis=0)
            lse4 = jnp.concatenate([lsev[h:h + 1, :] for h in heads], axis=1)
            sink4 = jnp.concatenate([jnp.broadcast_to(sk[:, h:h + 1], (1, T)) for h in heads], axis=1)
            p = jnp.exp(jnp.where(mask4, _dot(k3, q4, _NT), NEG) - lse4)
            ps = jnp.exp(sink4 - lse4)
            dp = _dot(v3, do4, _NT)
            delta = jnp.sum(p * dp, axis=0, keepdims=True)
            dsc = (p * (dp - delta)).astype(BF16)
            dq4 = _dot(k3.T, dsc) * scale
            dk3 = _dot(dsc, q4)
            dv3 = _dot(p.astype(BF16), do4)
            psd = ps * delta
            for g, h in enumerate(heads):
                dqts.append(dq4[:, g * T:(g + 1) * T])
                ds_acc = ds_acc - jnp.sum(psd[:, g * T:(g + 1) * T]) * (lane == h).astype(F32)
            for t, start in enumerate((0, pl.multiple_of(prev * T, T), pl.multiple_of(j * T, T))):
                rsl = pl.ds(start, T)
                dkv_ref[rsl, ksl] += dk3[t * T:(t + 1) * T, :]
                dkv_ref[rsl, vsl] += dv3[t * T:(t + 1) * T, :]
        ds_ref[...] += ds_acc
        dq_ref[...] = jnp.concatenate(dqts, axis=0).T.astype(dq_ref.dtype)

    blk = pl.BlockSpec((T, D_MODEL), lambda j: (j, 0))
    full = pl.BlockSpec((rows, 2 * D_KV), lambda j: (0, 0))
    return _call(
        body, name, [S((rows, D_MODEL), BF16), S((rows, 2 * D_KV), F32), S((1, LANE), F32)], (rows // T,),
        [blk, full, pl.BlockSpec((1, N_Q_HEADS), lambda j: (0, 0)), blk, pl.BlockSpec((N_Q_HEADS, T), lambda j: (0, j))],
        [blk, full, pl.BlockSpec((1, LANE), lambda j: (0, 0))], ("arbitrary",), (q, kv, sinks, do, lse), comm=comm)


BLOCK_BYTES = 1 << 20


def _div_tile(rows, cols):
    cap = max(16, BLOCK_BYTES // (4 * cols))
    best = None
    for t in range(16, min(rows, cap) + 1, 16):
        if rows % t == 0:
            best = t
    return best if best is not None else rows


def _adamw(parts, w, m, v, name, comm=None):
    layers, rows, cols = w.shape
    n = parts[0].shape[0]
    tr = _div_tile(rows, cols)
    tc = _pick(cols, 256) if tr == rows and rows * cols * 4 > 2 * BLOCK_BYTES else cols
    c1 = 1.0 / (1.0 - B1 ** STEP)
    c2 = 1.0 / (1.0 - B2 ** STEP)

    def body(*refs):
        p_refs = refs[:layers]
        w_ref, m_ref, v_ref, g_ref, d_ref, nm_ref, nv_ref = refs[layers:]
        layer = pl.program_id(0)
        for l in range(layers):
            @pl.when(layer == l)
            def _(p_ref=p_refs[l]):
                g = p_ref[0].astype(F32)
                for i in range(1, n):
                    g = g + p_ref[i].astype(F32)
                nm = B1 * m_ref[...] + (1.0 - B1) * g
                nv = B2 * v_ref[...] + (1.0 - B2) * (g * g)
                g_ref[...] = g
                nm_ref[...] = nm
                nv_ref[...] = nv
                d_ref[...] = -LR * ((nm * c1) / (jnp.sqrt(nv * c2) + EPS) + WD * w_ref[...])

    def part_spec(l):
        return pl.BlockSpec((n, tr, tc), lambda k, i, j: (0, jnp.where(k == l, i, 0), jnp.where(k == l, j, 0)))

    row = pl.BlockSpec((None, tr, tc), lambda k, i, j: (k, i, j))
    return _call(body, name, [S((layers, rows, cols), F32)] * 4, (layers, rows // tr, cols // tc),
                 [part_spec(l) for l in range(layers)] + [row, row, row], [row] * 4,
                 ("parallel", "parallel", "parallel"), (*parts, w, m, v), comm=comm)


def _sum_parts(parts, name):
    n, rows, cols = parts[0].shape
    nb = len(parts)
    tr = _div_tile(rows, cols)

    def body(*refs):
        o_ref = refs[nb]
        blk = pl.program_id(0)
        for l in range(nb):
            @pl.when(blk == l)
            def _(p_ref=refs[l]):
                g = p_ref[0].astype(F32)
                for i in range(1, n):
                    g = g + p_ref[i].astype(F32)
                o_ref[...] = g

    def part_spec(l):
        return pl.BlockSpec((n, tr, cols), lambda k, i: (0, jnp.where(k == l, i, 0), 0))

    per = rows // tr
    return pl.pallas_call(body, name=name, out_shape=S((nb * rows, cols), F32), grid=(nb, per),
                          in_specs=[part_spec(l) for l in range(nb)],
                          out_specs=pl.BlockSpec((tr, cols), lambda k, i: (k * per + i, 0)),
                          compiler_params=_cp(("parallel", "parallel")))(*parts)


def _col_segments(ws, runs):
    segs = []
    for glo, mlo, n in runs:
        while n > 0:
            d, off = divmod(glo, ws)
            take = min(n, ws - off)
            segs.append((d, off, mlo, take))
            glo, mlo, n = glo + take, mlo + take, n - take
    return segs


def _assemble_cols(gs, width, segs, name):
    _, rows, ws = gs[0].shape
    nb = len(gs)
    rb = _div_tile(rows, width // 2)
    per = rows // rb

    def body(*refs):
        o_ref = refs[nb]
        piece = pl.program_id(0)
        for l in range(nb):
            @pl.when(piece == l)
            def _(g_ref=refs[l]):
                o_ref[...] = jnp.zeros_like(o_ref)
                for d, off, mlo, n in segs:
                    o_ref[:, mlo:mlo + n] = g_ref[d, :, off:off + n]

    def piece_spec(l):
        return pl.BlockSpec((N_DEV, rb, ws), lambda k, i: (0, jnp.where(k == l, i, 0), 0))

    return pl.pallas_call(
        body, name=name, out_shape=S((nb * rows, width), gs[0].dtype), grid=(nb, per),
        in_specs=[piece_spec(l) for l in range(nb)],
        out_specs=pl.BlockSpec((rb, width), lambda k, i: (k * per + i, 0)),
        compiler_params=_cp(("parallel", "parallel")))(*gs)


def _scatter_cols(dw, ws, segs, name):
    rows, width = dw.shape
    rb = _div_tile(rows, width)

    def body(w_ref, o_ref):
        for d, off, mlo, n in segs:
            o_ref[d, :, off:off + n] = w_ref[:, mlo:mlo + n].astype(o_ref.dtype)

    return pl.pallas_call(
        body, name=name, out_shape=S((N_DEV, rows, ws), BF16), grid=(rows // rb,),
        in_specs=[pl.BlockSpec((rb, width), lambda i: (i, 0))],
        out_specs=pl.BlockSpec((N_DEV, rb, ws), lambda i: (0, i, 0)), compiler_params=_cp(("parallel",)))(dw)


def _gather_comm(xs):
    n = len(xs)

    def setup(x_refs, out_refs, sems):
        send_sems, recv_sems, local_sems = sems
        mx, my, mc = lax.axis_index("x"), lax.axis_index("y"), lax.axis_index("c")
        me, sibling = (mx, my, mc), (mx, my, 1 - mc)
        chips = [(1 - mx, my), (mx, 1 - my), (1 - mx, 1 - my)]

        def blk(a, px, py, pc):
            return out_refs[a].at[4 * px + 2 * py + pc]

        def copy(a, k, block, to, src=None):
            return pltpu.make_async_remote_copy(
                src_ref=blk(a, *block) if src is None else src, dst_ref=blk(a, *block),
                send_sem=send_sems.at[a, k], recv_sem=recv_sems.at[a, k], device_id=to, device_id_type=_MESH)

        mine = [pltpu.make_async_copy(x_refs[a], blk(a, *me), local_sems.at[a]) for a in range(n)]
        own = []
        for a in range(n):
            own.append(copy(a, 0, me, sibling, src=x_refs[a]))
            own += [copy(a, 1 + i, me, (*chip, mc), src=x_refs[a]) for i, chip in enumerate(chips)]
        return me, sibling, chips, mc, copy, mine, own

    def first(x_refs, out_refs, sems):
        _, _, _, _, _, mine, own = setup(x_refs, out_refs, sems)
        for cp in mine + own:
            cp.start()

    def last(x_refs, out_refs, sems):
        me, sibling, chips, mc, copy, mine, own = setup(x_refs, out_refs, sems)
        passed = []
        for a in range(n):
            for i, chip in enumerate(chips):
                copy(a, 1 + i, (*chip, mc), me).wait_recv()
                passed.append(copy(a, 4 + i, (*chip, mc), sibling))
                passed[-1].start()
        for a in range(n):
            copy(a, 0, sibling, me).wait_recv()
            for i, chip in enumerate(chips):
                copy(a, 4 + i, (*chip, 1 - mc), me).wait_recv()
        for cp in own + passed:
            cp.wait_send()
        for cp in mine:
            cp.wait()

    return _Comm(list(xs), [S((N_DEV,) + x.shape, x.dtype) for x in xs],
                 [pltpu.SemaphoreType.DMA((n, 7)), pltpu.SemaphoreType.DMA((n, 7)), pltpu.SemaphoreType.DMA((n,))],
                 first, last)


def _swap_comm(gs):
    n = len(gs)

    def copies(g_refs, out_refs, sems):
        send_sems, recv_sems = sems
        mx, my, mc = lax.axis_index("x"), lax.axis_index("y"), lax.axis_index("c")
        return [pltpu.make_async_remote_copy(
            src_ref=g_refs[a].at[2 * k + 1 - mc], dst_ref=out_refs[a].at[k], send_sem=send_sems.at[a, k],
            recv_sem=recv_sems.at[a, k], device_id=(mx, my, 1 - mc), device_id_type=_MESH)
            for a in range(n) for k in range(4)]

    def first(g_refs, out_refs, sems):
        for cp in copies(g_refs, out_refs, sems):
            cp.start()

    def last(g_refs, out_refs, sems):
        for cp in copies(g_refs, out_refs, sems):
            cp.wait()

    return _Comm(list(gs), [S((4,) + g.shape[1:], g.dtype) for g in gs],
                 [pltpu.SemaphoreType.DMA((n, 4)), pltpu.SemaphoreType.DMA((n, 4))], first, last)


def _chips_comm(parts):
    n = len(parts)

    def copies(p_refs, out_refs, sems):
        send_sems, recv_sems, local_sems = sems
        mx, my, mc = lax.axis_index("x"), lax.axis_index("y"), lax.axis_index("c")
        mychip = 2 * mx + my
        chips = [(1 - mx, my), (mx, 1 - my), (1 - mx, 1 - my)]
        mine = [pltpu.make_async_copy(p_refs[a].at[mychip], out_refs[a].at[mychip], local_sems.at[a])
                for a in range(n)]
        return mine + [pltpu.make_async_remote_copy(
            src_ref=p_refs[a].at[2 * cx + cy], dst_ref=out_refs[a].at[mychip], send_sem=send_sems.at[a, i],
            recv_sem=recv_sems.at[a, i], device_id=(cx, cy, mc), device_id_type=_MESH)
            for a in range(n) for i, (cx, cy) in enumerate(chips)]

    def first(p_refs, out_refs, sems):
        for cp in copies(p_refs, out_refs, sems):
            cp.start()

    def last(p_refs, out_refs, sems):
        for cp in copies(p_refs, out_refs, sems):
            cp.wait()

    return _Comm(list(parts), [S(p.shape, p.dtype) for p in parts],
                 [pltpu.SemaphoreType.DMA((n, 3)), pltpu.SemaphoreType.DMA((n, 3)), pltpu.SemaphoreType.DMA((n,))],
                 first, last)


def _join_comms(comms):
    def split(refs, counts):
        out, p = [], 0
        for cnt in counts:
            out.append(refs[p:p + cnt])
            p += cnt
        return out

    ni = [len(c.ins) for c in comms]
    no = [len(c.out_shapes) for c in comms]
    ns = [len(c.scratch) for c in comms]

    def first(in_refs, out_refs, sems):
        for c, i, o, s in zip(comms, split(in_refs, ni), split(out_refs, no), split(sems, ns)):
            c.first(i, o, s)

    def last(in_refs, out_refs, sems):
        for c, i, o, s in zip(comms, split(in_refs, ni), split(out_refs, no), split(sems, ns)):
            c.last(i, o, s)

    return _Comm([x for c in comms for x in c.ins], [x for c in comms for x in c.out_shapes],
                 [x for c in comms for x in c.scratch], first, last)


def _add_pairs(mine, theirs, core, name):
    _, rows, cols = mine.shape
    tr = _div_tile(rows, cols)

    def body(core_ref, a_ref, b_ref, o_ref):
        o_ref[...] = (a_ref[...].astype(F32) + b_ref[...].astype(F32)).astype(o_ref.dtype)

    return pl.pallas_call(
        body, name=name, out_shape=S((4, rows, cols), BF16),
        grid_spec=pltpu.PrefetchScalarGridSpec(
            num_scalar_prefetch=1, grid=(4, rows // tr),
            in_specs=[pl.BlockSpec((None, tr, cols), lambda k, i, c: (2 * k + c[0], i, 0)),
                      pl.BlockSpec((None, tr, cols), lambda k, i, c: (k, i, 0))],
            out_specs=pl.BlockSpec((None, tr, cols), lambda k, i, c: (k, i, 0))),
        compiler_params=_cp(("parallel", "parallel")))(core, mine, theirs)


def _run_comm(comm, name):
    ci, co = len(comm.ins), len(comm.out_shapes)

    def body(*refs):
        comm.first(refs[:ci], refs[ci:ci + co], refs[ci + co:])
        comm.last(refs[:ci], refs[ci:ci + co], refs[ci + co:])

    return pl.pallas_call(body, name=name, out_shape=list(comm.out_shapes), in_specs=[_HBM] * ci,
                          out_specs=[_HBM] * co, scratch_shapes=list(comm.scratch))(*comm.ins)


def _flat_rows(n_elems, mult):
    rows = -(-n_elems // LANE)
    return -(-rows // mult) * mult


def _pack(arrs, lead, mult, dtype):
    lead_shape = arrs[0].shape[:lead]
    flat = jnp.concatenate([a.astype(dtype).reshape(lead_shape + (-1,)) for a in arrs], axis=-1)
    n = flat.shape[-1]
    rows = _flat_rows(n, mult)
    flat = jnp.pad(flat, [(0, 0)] * lead + [(0, rows * LANE - n)])
    return flat.reshape(lead_shape + (rows, LANE))


def _unpack(flat, lead, shapes):
    lead_shape = flat.shape[:lead]
    flat = flat.reshape(lead_shape + (-1,))
    out, off = [], 0
    for shp in shapes:
        n = math.prod(shp)
        out.append(flat[..., off:off + n].reshape(lead_shape + tuple(shp)))
        off += n
    return out


def _split8(full, ax, n):
    shp = full.shape
    return jnp.moveaxis(full.reshape(shp[:ax] + (N_DEV, n) + shp[ax + 1:]), ax, 0)


def _join8(g, ax):
    shp = g.shape[1:]
    return jnp.moveaxis(g, 0, ax).reshape(shp[:ax] + (N_DEV * shp[ax],) + shp[ax + 1:])


def _group_lanes(v, hg):
    v = v.reshape(SSM_GROUPS, hg)
    return jnp.pad(v, ((0, 0), (0, LANE - hg))).reshape(1, SSM_GROUPS * LANE)


def _ungroup_lanes(v, hg):
    return v.reshape(SSM_GROUPS, LANE)[:, :hg].reshape(1, SSM_GROUPS * hg)


def kernel(x, meta_tokens, a_norm_pre, a_w_in, a_conv_w, a_conv_b, a_dt_bias, a_a_log, a_d_skip, a_gate_norm, a_w_out, a_norm_post, kv_norm, w_kv, b_norm_pre, b_w_q, b_sinks, b_w_o, b_norm_post, f_norm_pre, f_w_up, f_conv_w, f_conv_b, f_w_down, f_norm_post, loss_target, m_meta_tokens, m_a_norm_pre, m_a_w_in, m_a_conv_w, m_a_conv_b, m_a_dt_bias, m_a_a_log, m_a_d_skip, m_a_gate_norm, m_a_w_out, m_a_norm_post, m_kv_norm, m_w_kv, m_b_norm_pre, m_b_w_q, m_b_sinks, m_b_w_o, m_b_norm_post, m_f_norm_pre, m_f_w_up, m_f_conv_w, m_f_conv_b, m_f_w_down, m_f_norm_post, v_meta_tokens, v_a_norm_pre, v_a_w_in, v_a_conv_w, v_a_conv_b, v_a_dt_bias, v_a_a_log, v_a_d_skip, v_a_gate_norm, v_a_w_out, v_a_norm_post, v_kv_norm, v_w_kv, v_b_norm_pre, v_b_w_q, v_b_sinks, v_b_w_o, v_b_norm_post, v_f_norm_pre, v_f_w_up, v_f_conv_w, v_f_conv_b, v_f_w_down, v_f_norm_post):
    args = locals()
    wts = {n: args[n] for n in WEIGHTS}
    mom = {n: args["m_" + n] for n in WEIGHTS}
    var = {n: args["v_" + n] for n in WEIGHTS}
    mx, my, mc = lax.axis_index("x"), lax.axis_index("y"), lax.axis_index("c")
    me = 4 * mx + 2 * my + mc
    rows = _seq_rows()
    hg = SSM_HEADS // SSM_GROUPS
    d = D_MODEL

    n_main = D_INNER + D_XBC
    ws_in, ws_up = a_w_in.shape[2], f_w_up.shape[2]
    segs_in = _col_segments(ws_in, [(0, 0, n_main)] + [(n_main + hg * g, n_main + LANE * g, hg)
                                                      for g in range(SSM_GROUPS)])
    segs_up = _col_segments(ws_up, [(0, 0, 2 * D_FF)])
    def gather_of(*ws):
        return _gather_comm([w.astype(BF16) for w in ws])

    small_full, = _run_comm(_gather_comm([_pack([wts[n] for n in SMALL], 0, 8, F32)]), "gather_small")
    full = {}
    for n, g in zip(SMALL, _unpack(small_full, 1, [wts[n].shape for n in SMALL])):
        full[n] = _join8(g, SHARD_AXIS[n])
    (h0, hn0), (g_in,) = _embed_norm(full["meta_tokens"], x[0], full["a_norm_pre"], rows, "embed_norm",
                                     comm=gather_of(a_w_in[0]))
    w_in_all = _assemble_cols([g_in], n_main + SSM_GROUPS * LANE, segs_in, "asm_w_in")
    w_up, w_down = [None, None], [None, None]
    bias_g = _group_lanes(wts["a_dt_bias"], hg)
    alog_g = _group_lanes(wts["a_a_log"], hg)
    dsk_g = _group_lanes(wts["a_d_skip"], hg)
    a_conv_w, a_conv_b = full["a_conv_w"][0], full["a_conv_b"]
    f_cw, f_cb = full["f_conv_w"], wts["f_conv_b"]
    fpre, fpost = wts["f_norm_pre"], wts["f_norm_post"]


    zx, (g_out,) = _mm(hn0, w_in_all, "nn", F32, "mm_in", comm=gather_of(a_w_out[0]))
    w_out = g_out.reshape(D_INNER, d)
    xbc = _conv_silu_fwd(zx, a_conv_w, a_conv_b, "conv_a")
    (y_ssd, hst), (g_up0,) = _ssd_fwd(xbc, zx, bias_g, alog_g, dsk_g, "ssd_fwd", comm=gather_of(f_w_up[0]))
    w_up[0] = _assemble_cols([g_up0], 2 * D_FF, segs_up, "asm_w_up0")
    yn = _gatenorm_fwd(y_ssd, zx, full["a_gate_norm"], "gatenorm")
    mix_a, (g_o,) = _mm(yn, w_out, "nn", F32, "mm_out", comm=gather_of(b_w_o[0]))
    h1, (fn0,) = _resid_norm(h0, mix_a, full["a_norm_post"], [fpre[0:1]], "resid_a")

    half = d // 2
    u0, (g_dn0,) = _mm(fn0, w_up[0], "nn", F32, "mm_up0", comm=gather_of(f_w_down[0]))
    act0, (g_up1a,) = _ffn_act_fwd(u0, f_cw[0], f_cb[0:1], "ffn_act0", comm=gather_of(f_w_up[1, :half]))
    ffn0, (g_kv, g_q) = _mm(act0, g_dn0.reshape(D_FF, d), "nn", F32, "mm_down0", comm=gather_of(w_kv, b_w_q[0]))
    w_kvf, w_q, w_o = g_kv.reshape(d, 2 * D_KV), g_q.reshape(d, d), g_o.reshape(d, d)
    h2, (kvn, bn) = _resid_norm(h1, ffn0, fpost[0:1], [wts["kv_norm"].reshape(1, d), wts["b_norm_pre"]], "resid_f0")
    kv = _mm(kvn, w_kvf, "nn", F32, "mm_kv")
    q = _mm(bn, w_q, "nn", F32, "mm_q")
    (o, lse), (g_up1b,) = _attn_fwd(q, kv, wts["b_sinks"], "attn_fwd", comm=gather_of(f_w_up[1, half:]))
    w_up[1] = _assemble_cols([g_up1a, g_up1b], 2 * D_FF, segs_up, "asm_w_up1")
    mix_b = _mm(o, w_o, "nn", F32, "mm_o")
    h3, (fn1,) = _resid_norm(h2, mix_b, wts["b_norm_post"], [fpre[1:2]], "resid_b")
    u1, (g_dn1,) = _mm(fn1, w_up[1], "nn", F32, "mm_up1", comm=gather_of(f_w_down[1]))
    w_down = [g_dn0.reshape(D_FF, d), g_dn1.reshape(D_FF, d)]
    act1 = _ffn_act_fwd(u1, f_cw[1], f_cb[1:2], "ffn_act1")
    ffn1 = _mm(act1, w_down[1], "nn", F32, "mm_down1")
    dh4, loss_row, dffn1, dw_post1 = _final_loss(h3, ffn1, fpost[1:2], loss_target[0], "loss")
    loss = lax.psum(loss_row[0, 0], ("x", "y", "c"))

    grads = {}

    core = mc.astype(jnp.int32).reshape(1)

    def carried(res, comm):
        return res if comm is not None else (res, None)

    def ffn_bwd(dh_out, dffn, h_in, fn, u, act, i, then, c_dact=None, c_dwdown=None, c_dwup=None, c_dfn=None):
        dact, got_a = carried(_mm(dffn, w_down[i], "nt", F32, f"mm_dact{i}", comm=c_dact), c_dact)
        dw_down, got_b = carried(_mm(act, dffn, "tn", BF16, f"mm_dwdown{i}", comm=c_dwdown), c_dwdown)
        dw_down = dw_down.reshape(N_DEV, -1, d)
        du, dwc, dbc = _ffn_act_bwd(u, dact, f_cw[i], f_cb[i:i + 1], f"ffn_act_bwd{i}")
        dfn, (s_dn, *got_d) = _mm(du, w_up[i], "nt", F32, f"mm_dfn{i}", comm=_join_comms(
            [_swap_comm([dw_down])] + ([c_dfn] if c_dfn is not None else [])))
        sum_dn = _add_pairs(dw_down, s_dn, core, f"rs_add_dn{i}")
        dw_up, got_c = carried(_mm(fn, du, "tn", BF16, f"mm_dwup{i}", comm=c_dwup, shard_cols=ws_up), c_dwup)
        (dh_in, dw_pre, dbranch, dw_branch), (s_up,) = _norm_bwd(
            h_in, fpre[i:i + 1], dfn, dh_out, F32, f"nb_fpre{i}", comm=_swap_comm([dw_up]), then=then)
        sum_up = _add_pairs(dw_up, s_up, core, f"rs_add_up{i}")
        return dh_in, dbranch, dw_branch, dict(sum_down=sum_dn, cw=jnp.concatenate([dwc[0], dwc[1]], axis=1),
                                               cb=jnp.concatenate([dbc[0], dbc[1]], axis=1), sum_up=sum_up,
                                               pre=dw_pre), got_a, got_b, got_c, got_d

    dh3, dmix_b, grads["b_norm_post"], gf1, _, _, _, _ = ffn_bwd(dh4, dffn1, h3, fn1, u1, act1, 1,
                                                                 (mix_b, wts["b_norm_post"]))
    do = _mm(dmix_b, w_o, "nt", F32, "mm_do")
    dw_o = _mm(o, dmix_b, "tn", BF16, "mm_dwo").reshape(N_DEV, -1, d)
    half_up = gf1["sum_up"].shape[1] // 2
    (dq, dkv, dsinks), (p_up1a, s_o) = _attn_bwd(
        q, kv, wts["b_sinks"], do, lse, "attn_bwd",
        comm=_join_comms([_chips_comm([gf1["sum_up"][:, :half_up]]), _swap_comm([dw_o])]))
    sum_o = _add_pairs(dw_o, s_o, core, "rs_add_o")
    grads["b_sinks"] = dsinks[:, :N_Q_HEADS]
    dbn = _mm(dq, w_q, "nt", F32, "mm_dbn")
    dw_q = _mm(bn, dq, "tn", BF16, "mm_dwq").reshape(N_DEV, -1, d)
    dkv16 = dkv.astype(BF16)
    dkvn = _mm(dkv16, w_kvf, "nt", F32, "mm_dkvn")
    dw_kv = _mm(kvn, dkv16, "tn", BF16, "mm_dwkv").reshape(N_DEV, -1, 2 * D_KV)
    (dh2, grads["b_norm_pre"]), (s_q, s_kv) = _norm_bwd(h2, wts["b_norm_pre"], dbn, dh3, F32, "nb_bpre",
                                                        comm=_swap_comm([dw_q, dw_kv]))
    sum_q, sum_kv = _add_pairs(dw_q, s_q, core, "rs_add_q"), _add_pairs(dw_kv, s_kv, core, "rs_add_kv")
    dh2, dw_kvn, dffn0, dw_post0 = _norm_bwd(h2, wts["kv_norm"].reshape(1, d), dkvn, dh2, F32, "nb_kv",
                                             then=(ffn0, fpost[0:1]))
    grads["kv_norm"] = dw_kvn.reshape(d)
    dh1, dmix_a, grads["a_norm_post"], gf0, (p_o,), (p_q, p_kv), (p_dn1,), (p_up1b,) = ffn_bwd(
        dh2, dffn0, h1, fn0, u0, act0, 0, (mix_a, full["a_norm_post"]), c_dact=_chips_comm([sum_o]),
        c_dwdown=_chips_comm([sum_q, sum_kv]), c_dwup=_chips_comm([gf1["sum_down"]]),
        c_dfn=_chips_comm([gf1["sum_up"][:, half_up:]]))
    p_up1 = jnp.concatenate([p_up1a, p_up1b], axis=1)
    grads["f_norm_post"] = jnp.concatenate([dw_post0, dw_post1], axis=0)
    grads["f_norm_pre"] = jnp.concatenate([gf0["pre"], gf1["pre"]], axis=0)
    grads["f_conv_w"] = jnp.stack([gf0["cw"], gf1["cw"]])
    grads["f_conv_b"] = jnp.concatenate([gf0["cb"], gf1["cb"]], axis=0)

    dyn = _mm(dmix_a, w_out, "nt", F32, "mm_dyn")
    dw_out = _mm(yn, dmix_a, "tn", BF16, "mm_dwout").reshape(N_DEV, -1, d)
    (dy_ssd, dzx, grads["a_gate_norm"]), (s_out,) = _gatenorm_bwd(y_ssd, zx, full["a_gate_norm"], dyn, "gatenorm_bwd",
                                                                  comm=_swap_comm([dw_out]))
    sum_out = _add_pairs(dw_out, s_out, core, "rs_add_out")
    (dxs, dbm, dcm, dzx, dalog, ddsk, dbias), (p_up0,) = _ssd_bwd(
        xbc, zx, bias_g, alog_g, dsk_g, dy_ssd, hst, dzx, "ssd_bwd", comm=_chips_comm([gf0["sum_up"]]))
    grads["a_a_log"] = _ungroup_lanes(dalog, hg)
    grads["a_d_skip"] = _ungroup_lanes(ddsk, hg)
    grads["a_dt_bias"] = _ungroup_lanes(dbias, hg)
    dzx, dcw, dcb = _conv_silu_bwd(zx, dxs, dbm, dcm, a_conv_w, a_conv_b, dzx, "conv_a_bwd")
    grads["a_conv_w"], grads["a_conv_b"] = dcw[None], dcb
    dw_in_all, (p_dn0,) = _mm(hn0, dzx, "tn", BF16, "mm_dwin", comm=_chips_comm([gf0["sum_down"]]))
    dw_in8 = _scatter_cols(dw_in_all, ws_in, segs_in, "scat_w_in")
    dhn0, (s_in, p_out) = _mm(dzx, w_in_all, "nt", F32, "mm_dhn0",
                              comm=_join_comms([_swap_comm([dw_in8]), _chips_comm([sum_out])]))
    sum_in = _add_pairs(dw_in8, s_in, core, "rs_add_in")
    half_in = sum_in.shape[1] // 2
    (grads["meta_tokens"], g_x, grads["a_norm_pre"]), (p_in_a,) = _norm_bwd(
        h0, full["a_norm_pre"], dhn0, dh1, F32, "nb_apre", comm=_chips_comm([sum_in[:, :half_in]]), split_rows=SEQ)
    grad_x = g_x[None]

    small_local = _pack([_split8(grads[n], SHARD_AXIS[n], wts[n].shape[SHARD_AXIS[n]]) for n in SMALL], 1, 8, F32)
    repl_local = _pack([grads[n] for n in REPL], 0, 8, F32)
    n_sr = small_local.shape[1]
    small_vec = jnp.concatenate([small_local.reshape(N_DEV * n_sr, LANE), repl_local], axis=0)
    tail = _join_comms([_chips_comm([sum_in[:, half_in:]]), _gather_comm([small_vec])])
    parts_big = dict(a_w_out=[p_out], w_kv=[p_kv], b_w_q=[p_q], b_w_o=[p_o], f_w_down=[p_dn0, p_dn1])

    def flat_f32(dct, names, mult):
        return _pack([dct[n] for n in names], 0, mult, F32)

    def adamw_big(n, comm=None):
        shp3 = (len(parts_big[n]),) + parts_big[n][0].shape[1:]
        res = _adamw(parts_big[n], *[dct[n].reshape(shp3) for dct in (wts, mom, var)], f"adamw_{n}", comm=comm)
        res, got = res if comm is not None else (res, None)
        big_out[n] = [r.reshape(wts[n].shape) for r in res]
        return got

    big_out = {}
    def swap_last(a):
        return jnp.swapaxes(a, -1, -2)

    g_up_t = swap_last(_sum_parts([p_up0, p_up1], "sum_w_up").reshape(f_w_up.shape))
    res, (p_in_b, small_all) = _adamw([g_up_t[0:1], g_up_t[1:2]], *[swap_last(dct["f_w_up"]) for dct in (wts, mom, var)],
                                      "adamw_f_w_up", comm=tail)
    big_out["f_w_up"] = [swap_last(r) for r in res]
    for n in BIG:
        if n not in ("f_w_up", "a_w_in"):
            adamw_big(n)
    g_in_t = swap_last(_sum_parts([p_in_a, p_in_b], "sum_w_in"))[None]
    res = _adamw([g_in_t], *[swap_last(dct["a_w_in"]) for dct in (wts, mom, var)], "adamw_a_w_in")
    big_out["a_w_in"] = [swap_last(r) for r in res]
    mine_small = lax.dynamic_slice_in_dim(small_all, me * n_sr, n_sr, axis=1)
    parts_small = jnp.concatenate([mine_small, small_all[:, N_DEV * n_sr:]], axis=1)
    sm_in = [jnp.concatenate([flat_f32(dct, SMALL, 8), flat_f32(dct, REPL, 8)], axis=0)[None] for dct in (wts, mom, var)]
    small_out = [r[0] for r in _adamw([parts_small], *sm_in, "adamw_small")]

    outs = []
    for kind in range(4):
        res = {n: big_out[n][kind] for n in BIG}
        for n, a in zip(SMALL, _unpack(small_out[kind][:n_sr], 0, [wts[n].shape for n in SMALL])):
            res[n] = a
        for n, a in zip(REPL, _unpack(small_out[kind][n_sr:], 0, [wts[n].shape for n in REPL])):
            res[n] = a
        outs.append(res)
    return (loss, grad_x, *[outs[0][n] for n in WEIGHTS], *[outs[1][n] for n in WEIGHTS],
            *[outs[2][n] for n in WEIGHTS], *[outs[3][n] for n in WEIGHTS])
```

```python
import functools
import math

import jax
import jax.numpy as jnp
from jax import lax
from jax.experimental import pallas as pl
from jax.experimental.pallas import tpu as pltpu

F32, BF16 = jnp.float32, jnp.bfloat16
S = jax.ShapeDtypeStruct

D_MODEL = 1024
SEQ = 2048
N_META = 16
D_INNER = 2048
HEAD_P = 64
SSM_HEADS = D_INNER // HEAD_P
SSM_GROUPS = 4
D_STATE = 128
SSM_CONV = 4
D_BC = SSM_GROUPS * D_STATE
D_XBC = D_INNER + 2 * D_BC
ATTN_DH = 64
N_Q_HEADS = D_MODEL // ATTN_DH
N_KV_HEADS = 4
D_KV = N_KV_HEADS * ATTN_DH
WINDOW = 128
D_FF = 2816
FFN_CONV = 3
RMS_EPS = 1e-6
NEG = -1e30
LR, B1, B2, EPS, WD, STEP = 0.001, 0.9, 0.999, 1e-08, 0.01, 10

N_DEV = 8
T = 128
LANE = 128
VMEM_LIMIT = 48 * 1024 * 1024

BIG = ("a_w_in", "a_w_out", "w_kv", "b_w_q", "b_w_o", "f_w_up", "f_w_down")
SMALL = ("meta_tokens", "a_norm_pre", "a_conv_w", "a_conv_b", "a_gate_norm", "a_norm_post", "f_conv_w")
REPL = ("a_dt_bias", "a_a_log", "a_d_skip", "kv_norm", "b_norm_pre", "b_sinks", "b_norm_post",
        "f_norm_pre", "f_conv_b", "f_norm_post")
SHARD_AXIS = dict(a_w_in=2, a_w_out=1, w_kv=0, b_w_q=1, b_w_o=1, f_w_up=2, f_w_down=1, meta_tokens=1,
                  a_norm_pre=1, a_conv_w=2, a_conv_b=1, a_gate_norm=1, a_norm_post=1, f_conv_w=2)
WEIGHTS = ("meta_tokens", "a_norm_pre", "a_w_in", "a_conv_w", "a_conv_b", "a_dt_bias", "a_a_log", "a_d_skip",
           "a_gate_norm", "a_w_out", "a_norm_post", "kv_norm", "w_kv", "b_norm_pre", "b_w_q", "b_sinks", "b_w_o",
           "b_norm_post", "f_norm_pre", "f_w_up", "f_conv_w", "f_conv_b", "f_w_down", "f_norm_post")


def _seq_rows():
    return -(-(N_META + SEQ) // T) * T


def _cp(sem=None):
    return pltpu.CompilerParams(dimension_semantics=sem, vmem_limit_bytes=VMEM_LIMIT)


def _pick(n, target):
    t = min(n, target)
    t -= t % LANE
    while n % t:
        t -= LANE
    return t


def _sigmoid(x):
    return 0.5 * jnp.tanh(0.5 * x) + 0.5


def _softplus(x):
    return jnp.maximum(x, 0.0) + jnp.log(1.0 + jnp.exp(-jnp.abs(x)))


_NN = (((1,), (0,)), ((), ()))
_NT = (((1,), (1,)), ((), ()))
_TN = (((0,), (0,)), ((), ()))


def _dot(a, b, dims=_NN):
    return lax.dot_general(a, b, dims, preferred_element_type=F32)


def _dot_hi(a, b):
    return lax.dot_general(a, b, _NN, precision=lax.Precision.HIGHEST, preferred_element_type=F32)


_HBM = pl.BlockSpec(memory_space=pltpu.HBM)
_MESH = pl.DeviceIdType.MESH


class _Comm:
    def __init__(self, ins, out_shapes, scratch, first, last):
        self.ins, self.out_shapes, self.scratch, self.first, self.last = ins, out_shapes, scratch, first, last


_ANY = pl.BlockSpec(memory_space=pl.ANY)


def _call(body, name, out_shape, grid, in_specs, out_specs, sem, args, scratch=(), comm=None, aliases=None):
    aliases = aliases or {}
    if comm is None:
        return pl.pallas_call(body, name=name, out_shape=out_shape, grid=grid, in_specs=in_specs, out_specs=out_specs,
                              scratch_shapes=list(scratch), input_output_aliases=aliases,
                              compiler_params=_cp(sem))(*args)
    single = not isinstance(out_shape, (list, tuple))
    outs = [out_shape] if single else list(out_shape)
    ospecs = [out_specs] if single else list(out_specs)
    n_in, n_out, n_scr, ci, co = len(in_specs), len(outs), len(scratch), len(comm.ins), len(comm.out_shapes)

    def carrier(*refs):
        p = 0
        parts = []
        for cnt in (n_in, ci, n_out, co, n_scr, len(comm.scratch)):
            parts.append(refs[p:p + cnt])
            p += cnt
        ins, cins, outs_r, couts, scr, cscr = parts
        ids = [pl.program_id(i) for i in range(len(grid))]
        first, last = ids[0] == 0, ids[0] == grid[0] - 1
        for i in range(1, len(grid)):
            first, last = first & (ids[i] == 0), last & (ids[i] == grid[i] - 1)

        @pl.when(first)
        def _():
            comm.first(cins, couts, cscr)

        body(*ins, *outs_r, *scr)

        @pl.when(last)
        def _():
            comm.last(cins, couts, cscr)

    res = pl.pallas_call(
        carrier, name=name, out_shape=outs + list(comm.out_shapes), grid=grid,
        in_specs=list(in_specs) + [_HBM] * ci, out_specs=ospecs + [_HBM] * co,
        scratch_shapes=list(scratch) + list(comm.scratch), input_output_aliases=aliases,
        compiler_params=_cp(("arbitrary",) * len(grid)))(*args, *comm.ins)
    mine = res[0] if single else list(res[:n_out])
    return mine, list(res[n_out:])


def _mm(a, b, mode, out_dtype, name, comm=None, shard_cols=None):
    if mode == "tn":
        m, kk = a.shape
        planes, width = (b.shape[0], b.shape[2]) if b.ndim == 3 else (1, b.shape[1])
        n = planes * width
        tko, tn = _pick(kk, 512), _pick(width, 1536)
        per = width // tn
        b_spec = (pl.BlockSpec((None, m, tn), lambda i, j: (j // per, 0, j % per)) if b.ndim == 3
                  else pl.BlockSpec((m, tn), lambda i, j: (0, j)))
        if shard_cols is None:
            def body(a_ref, b_ref, o_ref):
                o_ref[...] = _dot(a_ref[...], b_ref[...], _TN).astype(o_ref.dtype)

            out_shape, out_spec = S((kk, n), out_dtype), pl.BlockSpec((tko, tn), lambda i, j: (i, j))
        else:
            shards = tn // shard_cols
            assert tn % shard_cols == 0

            def body(a_ref, b_ref, o_ref):
                res = _dot(a_ref[...], b_ref[...], _TN).astype(o_ref.dtype)
                for p in range(shards):
                    o_ref[p] = res[:, p * shard_cols:(p + 1) * shard_cols]

            out_shape = S((n // shard_cols, kk, shard_cols), out_dtype)
            out_spec = pl.BlockSpec((shards, tko, shard_cols), lambda i, j: (j, i, 0))
        return _call(
            body, name, out_shape, (kk // tko, n // tn), [pl.BlockSpec((m, tko), lambda i, j: (0, i)), b_spec],
            out_spec, ("parallel", "parallel"), (a, b), comm=comm)

    planes, width = (a.shape[0], a.shape[2]) if a.ndim == 3 else (1, a.shape[1])
    m, kk = a.shape[-2], planes * width
    n = b.shape[1] if mode == "nn" else b.shape[0]
    dims = _NN if mode == "nn" else _NT

    if kk > 2048:
        tm = m // 4
        assert m % 4 == 0 and tm % 16 == 0

        def body(a_ref, b_ref, o_ref):
            if a.ndim == 2:
                res = _dot(a_ref[...], b_ref[...], dims)
            else:
                res = None
                for p in range(planes):
                    bp = b_ref[p * width:(p + 1) * width, :] if mode == "nn" else b_ref[:, p * width:(p + 1) * width]
                    part = _dot(a_ref[p], bp, dims)
                    res = part if res is None else res + part
            o_ref[...] = res.astype(o_ref.dtype)

        a_spec = (pl.BlockSpec((planes, tm, width), lambda i: (0, i, 0)) if a.ndim == 3
                  else pl.BlockSpec((tm, kk), lambda i: (i, 0)))
        return _call(
            body, name, S((m, n), out_dtype), (m // tm,),
            [a_spec, pl.BlockSpec(b.shape, lambda i: (0, 0), pipeline_mode=pl.Buffered(1))],
            pl.BlockSpec((tm, n), lambda i: (i, 0)), ("parallel",), (a, b), comm=comm)

    tn = _pick(n, 512)

    def body(a_ref, b_ref, o_ref):
        o_ref[...] = _dot(a_ref[...], b_ref[...], dims).astype(o_ref.dtype)

    b_spec = (pl.BlockSpec((kk, tn), lambda j: (0, j)) if mode == "nn" else pl.BlockSpec((tn, kk), lambda j: (j, 0)))
    return _call(
        body, name, S((m, n), out_dtype), (n // tn,), [pl.BlockSpec((m, kk), lambda j: (0, 0)), b_spec],
        pl.BlockSpec((m, tn), lambda j: (0, j)), ("parallel",), (a, b), comm=comm)


def _rms(x, w):
    return x * lax.rsqrt(jnp.mean(x * x, axis=-1, keepdims=True) + RMS_EPS) * w


def _row_tile(rows, d):
    return rows // 4 if d <= 1024 and (rows // 4) % 16 == 0 else rows // 8


def _embed_norm(meta, x, w, rows, name, comm=None):
    n_meta, d = meta.shape
    n_x = x.shape[0]
    last = rows // T - 1
    assert n_meta % 8 == 0 and n_meta < T and n_meta + n_x == last * T + n_meta and last * T >= n_x

    def body(m_ref, x_ref, w_ref, h_ref, hn_ref):
        i = pl.program_id(0)

        @pl.when(i == 0)
        def _():
            h_ref[0:n_meta, :] = m_ref[...]
            h_ref[n_meta:T, :] = x_ref[0:T - n_meta, :]

        @pl.when((i > 0) & (i < last))
        def _():
            h_ref[...] = x_ref[pl.ds(pl.multiple_of(i * T - n_meta, 8), T), :]

        @pl.when(i == last)
        def _():
            h_ref[0:n_meta, :] = x_ref[n_x - n_meta:n_x, :]
            h_ref[n_meta:T, :] = jnp.zeros((T - n_meta, d), F32)

        hn_ref[...] = _rms(h_ref[...], w_ref[...]).astype(hn_ref.dtype)

    row = pl.BlockSpec((T, d), lambda i: (i, 0))
    return _call(body, name, [S((rows, d), F32), S((rows, d), BF16)], (rows // T,),
                 [pl.BlockSpec((n_meta, d), lambda i: (0, 0)), pl.BlockSpec((n_x, d), lambda i: (0, 0)),
                  pl.BlockSpec((1, d), lambda i: (0, 0))], [row, row], ("parallel",), (meta, x, w), comm=comm)


def _resid_norm(h, br, w_post, next_ws, name):
    rows, d = h.shape
    tr = _row_tile(rows, d)
    has_br = br is not None
    nw = len(next_ws)

    def body(*refs):
        h_ref = refs[0]
        pos = 1
        x = h_ref[...]
        if has_br:
            x = x + _rms(refs[1][...], refs[2][...])
            pos = 3
        w_refs = refs[pos:pos + nw]
        outs = refs[pos + nw:]
        if has_br:
            outs[0][...] = x
            outs = outs[1:]
        for w_ref, o_ref in zip(w_refs, outs):
            o_ref[...] = _rms(x, w_ref[...]).astype(o_ref.dtype)

    row = pl.BlockSpec((tr, d), lambda i: (i, 0))
    vec = pl.BlockSpec((1, d), lambda i: (0, 0))
    ins = [h] + ([br, w_post] if has_br else []) + list(next_ws)
    in_specs = [row] + ([row, vec] if has_br else []) + [vec] * nw
    out_shape = ([S((rows, d), F32)] if has_br else []) + [S((rows, d), BF16)] * nw
    res = pl.pallas_call(body, name=name, out_shape=out_shape, grid=(rows // tr,), in_specs=in_specs,
                         out_specs=[row] * len(out_shape), compiler_params=_cp(("parallel",)))(*ins)
    if has_br:
        return res[0], list(res[1:])
    return h, list(res)


def _rms_bwd(xv, w, dyv):
    r = lax.rsqrt(jnp.mean(xv * xv, axis=-1, keepdims=True) + RMS_EPS)
    wdy = dyv * w
    dx = r * wdy - xv * (r * r * r) * jnp.mean(xv * wdy, axis=-1, keepdims=True)
    return dx, jnp.sum(dyv * xv * r, axis=0, keepdims=True)


def _norm_bwd(x, w, dy, add, out_dtype, name, comm=None, then=None, split_rows=None):
    rows, d = x.shape
    tr = _row_tile(rows, d)
    has_add = add is not None
    n_in = 3 + has_add + (2 if then is not None else 0)
    if split_rows is not None:
        last, tail = _real_rows(rows, tr, split_rows)

    def body(*refs):
        x_ref, w_ref, dy_ref = refs[:3]
        outs = refs[n_in:]
        dx, dw = _rms_bwd(x_ref[...], w_ref[...], dy_ref[...].astype(F32))
        if has_add:
            dx = dx + refs[3][...]
        if split_rows is None:
            outs[0][...] = dx.astype(outs[0].dtype)
        else:
            i = pl.program_id(0)
            gm_ref, gx_ref = outs[0], outs[1]
            outs = outs[1:]

            @pl.when(i == 0)
            def _():
                gm_ref[...] = dx[0:N_META, :]
                gx_ref[0:tr - N_META, :] = dx[N_META:tr, :]

            @pl.when((i > 0) & (i < last))
            def _():
                gx_ref[pl.ds(pl.multiple_of(i * tr - N_META, 8), tr), :] = dx

            @pl.when(i == last)
            def _():
                gx_ref[split_rows - tail:split_rows, :] = dx[0:tail, :]
        first = pl.program_id(0) == 0

        @pl.when(first)
        def _():
            outs[1][...] = jnp.zeros_like(outs[1])

        outs[1][...] += dw
        if then is not None:
            dx2, dw2 = _rms_bwd(refs[n_in - 2][...], refs[n_in - 1][...], dx)
            outs[2][...] = dx2.astype(outs[2].dtype)

            @pl.when(first)
            def _():
                outs[3][...] = jnp.zeros_like(outs[3])

            outs[3][...] += dw2

    row = pl.BlockSpec((tr, d), lambda i: (i, 0))
    vec = pl.BlockSpec((1, d), lambda i: (0, 0))
    ins = [x, w, dy] + ([add] if has_add else []) + (list(then) if then is not None else [])
    in_specs = [row, vec, row] + ([row] if has_add else []) + ([row, vec] if then is not None else [])
    out_shape = [S((rows, d), out_dtype), S((1, d), F32)] + ([S((rows, d), BF16), S((1, d), F32)] if then is not None else [])
    out_specs = [row, vec] * (len(out_shape) // 2)
    if split_rows is not None:
        out_shape = [S((N_META, d), F32), S((split_rows, d), F32)] + out_shape[1:]
        out_specs = [pl.BlockSpec((N_META, d), lambda i: (0, 0)), pl.BlockSpec((split_rows, d), lambda i: (0, 0))] + out_specs[1:]
    return _call(body, name, out_shape, (rows // tr,), in_specs, out_specs, ("arbitrary",), ins, comm=comm)


def _real_rows(rows, tr, n_x):
    last = (N_META + n_x - 1) // tr
    tail = N_META + n_x - last * tr
    assert last == rows // tr - 1 and N_META % 8 == 0 and tail % 8 == 0 and N_META < tr
    return last, tail


def _final_loss(h, br, w_post, target, name):
    rows, d = h.shape
    tr = _row_tile(rows, d)
    n_x = target.shape[0]
    last, tail = _real_rows(rows, tr, n_x)

    def body(h_ref, br_ref, w_ref, t_ref, dh_ref, loss_ref, dbr_ref, dw_ref, tbuf):
        i = pl.program_id(0)

        @pl.when(i == 0)
        def _():
            tbuf[0:N_META, :] = jnp.zeros((N_META, d), F32)
            tbuf[N_META:tr, :] = t_ref[0:tr - N_META, :]

        @pl.when((i > 0) & (i < last))
        def _():
            tbuf[...] = t_ref[pl.ds(pl.multiple_of(i * tr - N_META, 8), tr), :]

        @pl.when(i == last)
        def _():
            tbuf[0:tail, :] = t_ref[n_x - tail:n_x, :]
            if tail < tr:
                tbuf[tail:tr, :] = jnp.zeros((tr - tail, d), F32)

        brv, wv = br_ref[...], w_ref[...]
        y = h_ref[...] + _rms(brv, wv)
        r = i * tr + lax.broadcasted_iota(jnp.int32, (tr, 1), 0)
        real = (r >= N_META) & (r < N_META + SEQ)
        diff = jnp.where(real, y - tbuf[...], 0.0)
        dh = diff * (1.0 / d)
        dh_ref[...] = dh
        dbr, dw = _rms_bwd(brv, wv, dh)
        dbr_ref[...] = dbr.astype(dbr_ref.dtype)

        @pl.when(i == 0)
        def _():
            loss_ref[...] = jnp.zeros_like(loss_ref)
            dw_ref[...] = jnp.zeros_like(dw_ref)

        loss_ref[...] += jnp.sum(diff * diff) * (0.5 / d)
        dw_ref[...] += dw

    row = pl.BlockSpec((tr, d), lambda i: (i, 0))
    vec = pl.BlockSpec((1, d), lambda i: (0, 0))
    return pl.pallas_call(body, name=name,
                          out_shape=[S((rows, d), F32), S((1, LANE), F32), S((rows, d), BF16), S((1, d), F32)],
                          grid=(rows // tr,), in_specs=[row, row, vec, pl.BlockSpec((n_x, d), lambda i: (0, 0))],
                          out_specs=[row, pl.BlockSpec((1, LANE), lambda i: (0, 0)), row, vec],
                          scratch_shapes=[pltpu.VMEM((tr, d), F32)],
                          compiler_params=_cp(("arbitrary",)))(h, br, w_post, target)


def _gatenorm_fwd(y, zx, w, name, comm=None):
    rows, d = y.shape
    tr = _row_tile(rows, d)

    def body(y_ref, z_ref, w_ref, o_ref):
        z = z_ref[...]
        o_ref[...] = _rms(y_ref[...] * z * _sigmoid(z), w_ref[...]).astype(o_ref.dtype)

    row = pl.BlockSpec((tr, d), lambda i: (i, 0))
    return _call(body, name, S((rows, d), BF16), (rows // tr,), [row, row, pl.BlockSpec((1, d), lambda i: (0, 0))],
                 row, ("parallel",), (y, zx, w), comm=comm)


def _gatenorm_bwd(y, zx, w, dyn, name, comm=None):
    rows, d = y.shape
    tr = _row_tile(rows, d)

    def body(y_ref, z_ref, w_ref, dyn_ref, dy_ref, dz_ref, dw_ref):
        yv, z = y_ref[...], z_ref[...]
        sg = _sigmoid(z)
        sz = z * sg
        g = yv * sz
        r = lax.rsqrt(jnp.mean(g * g, axis=-1, keepdims=True) + RMS_EPS)
        dyn_v = dyn_ref[...]
        wdy = dyn_v * w_ref[...]
        dg = r * wdy - g * (r * r * r) * jnp.mean(g * wdy, axis=-1, keepdims=True)
        dy_ref[...] = dg * sz
        dz_ref[...] = (dg * yv * sg * (1.0 + z * (1.0 - sg))).astype(dz_ref.dtype)

        @pl.when(pl.program_id(0) == 0)
        def _():
            dw_ref[...] = jnp.zeros_like(dw_ref)

        dw_ref[...] += jnp.sum(dyn_v * g * r, axis=0, keepdims=True)

    row = pl.BlockSpec((tr, d), lambda i: (i, 0))
    vec = pl.BlockSpec((1, d), lambda i: (0, 0))
    return _call(body, name, [S((rows, d), F32), S((rows, zx.shape[1]), BF16), S((1, d), F32)], (rows // tr,),
                 [row, row, vec, row], [row, row, vec], ("arbitrary",), (y, zx, w, dyn), comm=comm)


def _shift_down(x, s, rows_iota):
    if s == 0:
        return x
    return jnp.where(rows_iota >= s, pltpu.roll(x, s, 0), 0.0)


def _shift_up(x, s, rows_iota):
    if s == 0:
        return x
    rows = x.shape[0]
    return jnp.where(rows_iota < rows - s, pltpu.roll(x, rows - s, 0), 0.0)


def _r16(v):
    return v.astype(BF16).astype(F32)


def _conv_taps(x, taps, rows_iota):
    x = _r16(x)
    return [_shift_down(x, taps - 1 - k, rows_iota) for k in range(taps)]


def _conv(x, w_ref, b_ref, taps, rows_iota, shifted=None):
    shifted = _conv_taps(x, taps, rows_iota) if shifted is None else shifted
    acc = jnp.zeros_like(shifted[0])
    for k in range(taps):
        acc = acc + _r16(w_ref[k:k + 1, :]) * shifted[k]
    return acc + b_ref[...]


def _conv_bwd(shifted, du, w_ref, dw_ref, db_ref, taps, rows_iota):
    db_ref[...] = jnp.sum(du, axis=0, keepdims=True)
    du = _r16(du)
    dx = jnp.zeros_like(du)
    for k in range(taps):
        dx = dx + _r16(w_ref[k:k + 1, :]) * _shift_up(du, taps - 1 - k, rows_iota)
        dw_ref[k:k + 1, :] = jnp.sum(du * shifted[k], axis=0, keepdims=True)
    return dx


def _conv_silu_fwd(zx, w, b, name, comm=None):
    rows = zx.shape[0]
    cb = 512
    off = D_INNER // cb

    def body(x_ref, w_ref, b_ref, o_ref):
        it = lax.broadcasted_iota(jnp.int32, (rows, 1), 0)
        u = _conv(x_ref[...], w_ref, b_ref, SSM_CONV, it)
        o_ref[...] = u * _sigmoid(u)

    return _call(
        body, name, S((rows, D_XBC), F32), (D_XBC // cb,),
        [pl.BlockSpec((rows, cb), lambda j: (0, off + j)), pl.BlockSpec((SSM_CONV, cb), lambda j: (0, j)),
         pl.BlockSpec((1, cb), lambda j: (0, j))],
        pl.BlockSpec((rows, cb), lambda j: (0, j)), ("parallel",), (zx, w, b), comm=comm)


def _conv_silu_bwd(zx, dxs, dbm, dcm, w, b, dzx, name, comm=None):
    rows = zx.shape[0]
    cb = 256
    off = D_INNER // cb
    nx, nbc = D_INNER // cb, D_BC // cb

    def body(x_ref, dx_in, db_in, dc_in, w_ref, b_ref, dzx_in, dx_ref, dw_ref, db_ref, dbuf):
        del dzx_in
        j = pl.program_id(0)
        for cond, src in ((j < nx, dx_in), ((j >= nx) & (j < nx + nbc), db_in), (j >= nx + nbc, dc_in)):
            @pl.when(cond)
            def _(src=src):
                dbuf[...] = src[...]
        it = lax.broadcasted_iota(jnp.int32, (rows, 1), 0)
        xs = _conv_taps(x_ref[...], SSM_CONV, it)
        u = _conv(None, w_ref, b_ref, SSM_CONV, it, xs)
        sg = _sigmoid(u)
        du = dbuf[...] * sg * (1.0 + u * (1.0 - sg))
        dx_ref[...] = _conv_bwd(xs, du, w_ref, dw_ref, db_ref, SSM_CONV, it).astype(dx_ref.dtype)

    def part(first, count):
        return pl.BlockSpec((rows, cb), lambda j: (0, jnp.clip(j - first, 0, count - 1)))

    col = pl.BlockSpec((rows, cb), lambda j: (0, j))
    wsp = pl.BlockSpec((SSM_CONV, cb), lambda j: (0, j))
    bsp = pl.BlockSpec((1, cb), lambda j: (0, j))
    xbc_cols = pl.BlockSpec((rows, cb), lambda j: (0, off + j))
    return _call(
        body, name, [S(dzx.shape, dzx.dtype), S((SSM_CONV, D_XBC), F32), S((1, D_XBC), F32)], (D_XBC // cb,),
        [xbc_cols, part(0, nx), part(nx, nbc), part(nx + nbc, nbc), wsp, bsp, _ANY],
        [xbc_cols, wsp, bsp], ("arbitrary",), (zx, dxs, dbm, dcm, w, b, dzx), scratch=[pltpu.VMEM((rows, cb), F32)],
        comm=comm, aliases={6: 0})


def _ffn_act_fwd(u, w, b, name, comm=None):
    rows = u.shape[0]
    cb = 256
    nb = D_FF // cb

    def body(g_ref, v_ref, wg_ref, wv_ref, bg_ref, bv_ref, o_ref):
        it = lax.broadcasted_iota(jnp.int32, (rows, 1), 0)
        g = _conv(g_ref[...], wg_ref, bg_ref, FFN_CONV, it)
        v = _conv(v_ref[...], wv_ref, bv_ref, FFN_CONV, it)
        o_ref[...] = (g * _sigmoid(g) * v).astype(o_ref.dtype)

    def sp(r, shift):
        return pl.BlockSpec((r, cb), lambda j: (0, shift + j))

    return _call(
        body, name, S((rows, D_FF), BF16), (nb,),
        [sp(rows, 0), sp(rows, nb), sp(FFN_CONV, 0), sp(FFN_CONV, nb), sp(1, 0), sp(1, nb)],
        sp(rows, 0), ("parallel",), (u, u, w, w, b, b), comm=comm)


def _ffn_act_bwd(u, dact, w, b, name, comm=None):
    rows = u.shape[0]
    cb = 256
    nb = D_FF // cb

    def body(g_ref, v_ref, d_ref, wg_ref, wv_ref, bg_ref, bv_ref, du_ref, dw_ref, db_ref):
        it = lax.broadcasted_iota(jnp.int32, (rows, 1), 0)
        xg, xv = _conv_taps(g_ref[...], FFN_CONV, it), _conv_taps(v_ref[...], FFN_CONV, it)
        g = _conv(None, wg_ref, bg_ref, FFN_CONV, it, xg)
        v = _conv(None, wv_ref, bv_ref, FFN_CONV, it, xv)
        sg = _sigmoid(g)
        d = d_ref[...]
        dgate = d * v * sg * (1.0 + g * (1.0 - sg))
        dval = d * g * sg
        du_ref[0] = _conv_bwd(xg, dgate, wg_ref, dw_ref.at[0], db_ref.at[0], FFN_CONV, it).astype(du_ref.dtype)
        du_ref[1] = _conv_bwd(xv, dval, wv_ref, dw_ref.at[1], db_ref.at[1], FFN_CONV, it).astype(du_ref.dtype)

    def sp(r, shift):
        return pl.BlockSpec((r, cb), lambda j: (0, shift + j))

    def both(r):
        return pl.BlockSpec((2, r, cb), lambda j: (0, 0, j))

    return _call(
        body, name, [S((2, rows, D_FF), BF16), S((2, FFN_CONV, D_FF), F32), S((2, 1, D_FF), F32)], (nb,),
        [sp(rows, 0), sp(rows, nb), sp(rows, 0), sp(FFN_CONV, 0), sp(FFN_CONV, nb), sp(1, 0), sp(1, nb)],
        [both(rows), both(FFN_CONV), both(1)], ("parallel",), (u, u, dact, w, w, b, b), comm=comm)


def _ssd_consts(dtp_ref, bias_ref, alog_ref, hg):
    lane = lax.broadcasted_iota(jnp.int32, (1, LANE), 1)
    pre = dtp_ref[...] + bias_ref[...]
    dt = _softplus(pre)
    a_row = jnp.where(lane < hg, -jnp.exp(alog_ref[...]), 0.0)
    ri = lax.broadcasted_iota(jnp.int32, (T, T), 0)
    ci = lax.broadcasted_iota(jnp.int32, (T, T), 1)
    cs = _dot_hi((ri >= ci).astype(F32), dt * a_row)
    return pre, dt, a_row, cs, ri, ci, lane


def _head_rows(src, hg):
    return jnp.concatenate([jnp.broadcast_to(src[k:k + 1, :], (HEAD_P, src.shape[1])) for k in range(hg)], axis=0)


def _ssd_fwd(xbc, zx, bias, alog, dsk, name, comm=None):
    rows = xbc.shape[0]
    nc = rows // T
    hg = SSM_HEADS // SSM_GROUPS
    gw = hg * HEAD_P
    xoff, boff, coff = 0, D_INNER // D_STATE, (D_INNER + D_BC) // D_STATE
    dtoff = (D_INNER + D_XBC) // LANE

    def body(x_ref, b_ref, c_ref, dtp_ref, bias_ref, alog_ref, dsk_ref, y_ref, hst_ref, hs):
        c = pl.program_id(1)

        @pl.when(c == 0)
        def _():
            hs[...] = jnp.zeros_like(hs)

        _, dt, _, cs, ri, ci, _ = _ssd_consts(dtp_ref, bias_ref, alog_ref, hg)
        cst, dtt = cs.T, dt.T
        xt = x_ref[...].T
        bb, cbf = b_ref[...].astype(BF16), c_ref[...].astype(BF16)
        gt = _dot(bb, cbf, _NT)
        causal_t = ci >= ri
        dskv = dsk_ref[...]
        hall = hs[...]
        hst_ref[0, 0] = hall
        cs8 = cst[0:8, :]
        cl8 = cs8[:, T - 1:T]
        xdt = xt * _head_rows(dtt, hg)
        yo = _head_rows(jnp.exp(cs8), hg) * _dot(hall.astype(BF16), cbf, _NT)
        st = _dot((xdt * _head_rows(jnp.exp(cl8 - cs8), hg)).astype(BF16), bb)
        hs[...] = _head_rows(jnp.exp(cl8), hg) * hall + st
        yds = []
        for k in range(hg):
            sl = slice(k * HEAD_P, (k + 1) * HEAD_P)
            lt = jnp.exp(jnp.where(causal_t, cst[k:k + 1, :] - cs[:, k:k + 1], NEG))
            yds.append(_dot(xdt[sl, :].astype(BF16), (gt * lt).astype(BF16)))
        dsk_r = jnp.concatenate([jnp.broadcast_to(dskv[:, k:k + 1], (HEAD_P, 1)) for k in range(hg)], axis=0)
        y_ref[...] = (jnp.concatenate(yds, axis=0) + yo + dsk_r * xt).T

    vec = pl.BlockSpec((1, LANE), lambda g, c: (0, g))
    return _call(
        body, name, [S((rows, D_INNER), F32), S((nc, SSM_GROUPS, gw, D_STATE), F32)], (SSM_GROUPS, nc),
        [pl.BlockSpec((T, gw), lambda g, c: (c, xoff + g)),
         pl.BlockSpec((T, D_STATE), lambda g, c: (c, boff + g)),
         pl.BlockSpec((T, D_STATE), lambda g, c: (c, coff + g)),
         pl.BlockSpec((T, LANE), lambda g, c: (c, dtoff + g)), vec, vec, vec],
        [pl.BlockSpec((T, gw), lambda g, c: (c, g)), pl.BlockSpec((1, 1, gw, D_STATE), lambda g, c: (c, g, 0, 0))],
        ("parallel", "arbitrary"), (xbc, xbc, xbc, zx, bias, alog, dsk),
        scratch=[pltpu.VMEM((gw, D_STATE), F32)], comm=comm)


def _ssd_bwd(xbc, zx, bias, alog, dsk, dy, hst, dzx, name, comm=None):
    rows = xbc.shape[0]
    nc = rows // T
    hg = SSM_HEADS // SSM_GROUPS
    gw = hg * HEAD_P
    boff, coff = D_INNER // D_STATE, (D_INNER + D_BC) // D_STATE
    dtoff = (D_INNER + D_XBC) // LANE

    def body(x_ref, b_ref, c_ref, dtp_ref, bias_ref, alog_ref, dsk_ref, dy_ref, hst_ref, dzx_in,
             dx_ref, db_ref, dc_ref, ddtp_ref, dalog_ref, ddsk_ref, dbias_ref, dhs):
        del dzx_in
        step = pl.program_id(1)

        @pl.when(step == 0)
        def _():
            dhs[...] = jnp.zeros_like(dhs)
            dalog_ref[...] = jnp.zeros_like(dalog_ref)
            ddsk_ref[...] = jnp.zeros_like(ddsk_ref)
            dbias_ref[...] = jnp.zeros_like(dbias_ref)

        pre, dt, a_row, cs, ri, ci, lane = _ssd_consts(dtp_ref, bias_ref, alog_ref, hg)
        cst, dtt = cs.T, dt.T
        xt, dyt = x_ref[...].T, dy_ref[...].T
        bb, cbf = b_ref[...].astype(BF16), c_ref[...].astype(BF16)
        gt = _dot(bb, cbf, _NT)
        causal_t = ci >= ri
        dskv = dsk_ref[...]
        hall, dhall = hst_ref[0, 0], dhs[...]
        head_row = lax.broadcasted_iota(jnp.int32, (T, 1), 0)
        last_l = lax.broadcasted_iota(jnp.int32, (1, T), 1) == T - 1
        cs8, dt8 = cst[0:8, :], dtt[0:8, :]
        cl8 = cs8[:, T - 1:T]
        e8, wdec8 = jnp.exp(cs8), jnp.exp(cl8 - cs8)
        w8 = wdec8 * dt8
        dt_r, e_r, w_r, ecl_r = _head_rows(dt8, hg), _head_rows(e8, hg), _head_rows(w8, hg), _head_rows(jnp.exp(cl8), hg)
        dsk_r = jnp.concatenate([jnp.broadcast_to(dskv[:, k:k + 1], (HEAD_P, 1)) for k in range(hg)], axis=0)
        hb, dhb = hall.astype(BF16), dhall.astype(BF16)
        xdt = xt * dt_r
        dye = (dyt * e_r).astype(BF16)
        rt = _dot(dhb, bb, _NT)
        yo = e_r * _dot(hb, cbf, _NT)
        dhs[...] = ecl_r * dhall + _dot(dye, cbf)
        dc_acc = _dot(dye, hb, _TN)
        db_acc = _dot((xt * w_r).astype(BF16), dhb, _TN)
        rtx, dyyo, hdh, dyx = rt * xt, dyt * yo, dhall * hall, dyt * xt
        dgt = jnp.zeros((T, T), F32)
        ddt_rows = jnp.zeros((T, T), F32)
        dcs_rows = jnp.zeros((T, T), F32)
        qrow_cols = jnp.zeros((T, LANE), F32)
        ddsk_acc = jnp.zeros((1, LANE), F32)
        dxdts = []
        for k in range(hg):
            sl = slice(k * HEAD_P, (k + 1) * HEAD_P)
            lt = jnp.exp(jnp.where(causal_t, cst[k:k + 1, :] - cs[:, k:k + 1], NEG))
            mpt = gt * lt
            dyb = dyt[sl, :].astype(BF16)
            dxdt = _dot(dyb, mpt.astype(BF16), _NT)
            dmt = _dot(xdt[sl, :].astype(BF16), dyb, _TN)
            dgt = dgt + dmt * lt
            q = dmt * mpt
            q_rows = jnp.sum(q, axis=1, keepdims=True)
            q_cols = jnp.sum(q, axis=0, keepdims=True)
            dxdts.append(dxdt)
            xz = jnp.sum(xt[sl, :] * dxdt, axis=0, keepdims=True)
            dw = jnp.sum(rtx[sl, :], axis=0, keepdims=True)
            wk, wdeck = w8[k:k + 1, :], wdec8[k:k + 1, :]
            dcl = jnp.exp(cl8[k:k + 1, :]) * jnp.sum(hdh[sl, :]) + jnp.sum(dw * wk)
            dcs_r = jnp.sum(dyyo[sl, :], axis=0, keepdims=True) + q_cols - dw * wk + jnp.where(last_l, dcl, 0.0)
            onehot = (lane == k).astype(F32)
            ddt_rows = ddt_rows + jnp.where(head_row == k, xz + dw * wdeck, 0.0)
            dcs_rows = dcs_rows + jnp.where(head_row == k, dcs_r, 0.0)
            qrow_cols = qrow_cols + q_rows * onehot
            ddsk_acc = ddsk_acc + jnp.sum(dyx[sl, :]) * onehot
        dx_ref[...] = (dt_r * jnp.concatenate(dxdts, axis=0) + dsk_r * dyt + rt * w_r).T
        dc_ref[...] = _dot(dgt.T.astype(BF16), bb) + dc_acc
        db_ref[...] = _dot(dgt.astype(BF16), cbf) + db_acc
        da = _dot_hi((ci >= ri).astype(F32), dcs_rows.T - qrow_cols)
        ddtp = (ddt_rows.T + da * a_row) * _sigmoid(pre)
        ddtp = jnp.where(lane < hg, ddtp, 0.0)
        ddtp_ref[...] = ddtp.astype(ddtp_ref.dtype)
        dbias_ref[...] += jnp.sum(ddtp, axis=0, keepdims=True)
        dalog_ref[...] += jnp.sum(da * dt, axis=0, keepdims=True) * a_row
        ddsk_ref[...] += ddsk_acc

    def rc(c):
        return nc - 1 - c

    vec = pl.BlockSpec((1, LANE), lambda g, c: (0, g))
    xsp = pl.BlockSpec((T, gw), lambda g, c: (rc(c), g))
    return _call(
        body, name,
        [S((rows, D_INNER), F32), S((rows, D_BC), F32), S((rows, D_BC), F32),
         S(dzx.shape, dzx.dtype), S((1, SSM_GROUPS * LANE), F32),
         S((1, SSM_GROUPS * LANE), F32), S((1, SSM_GROUPS * LANE), F32)],
        (SSM_GROUPS, nc),
        [xsp,
         pl.BlockSpec((T, D_STATE), lambda g, c: (rc(c), boff + g)),
         pl.BlockSpec((T, D_STATE), lambda g, c: (rc(c), coff + g)),
         pl.BlockSpec((T, LANE), lambda g, c: (rc(c), dtoff + g)), vec, vec, vec,
         xsp, pl.BlockSpec((1, 1, gw, D_STATE), lambda g, c: (rc(c), g, 0, 0)), _ANY],
        [xsp,
         pl.BlockSpec((T, D_STATE), lambda g, c: (rc(c), g)),
         pl.BlockSpec((T, D_STATE), lambda g, c: (rc(c), g)),
         pl.BlockSpec((T, LANE), lambda g, c: (rc(c), dtoff + g)), vec, vec, vec],
        ("parallel", "arbitrary"), (xbc, xbc, xbc, zx, bias, alog, dsk, dy, hst, dzx),
        scratch=[pltpu.VMEM((gw, D_STATE), F32)], comm=comm, aliases={9: 3})


def _attn_tiles(kv_ref, j):
    prev = jnp.maximum(j - 1, 0)
    meta = kv_ref[0:T, :]
    prv = kv_ref[pl.ds(pl.multiple_of(prev * T, T), T), :]
    cur = kv_ref[pl.ds(pl.multiple_of(j * T, T), T), :]
    return jnp.concatenate([meta, prv, cur], axis=0)


def _attn_mask(j):
    r = j * T + lax.broadcasted_iota(jnp.int32, (3 * T, T), 1)
    row = lax.broadcasted_iota(jnp.int32, (3 * T, T), 0)
    t0, t1 = row < T, row < 2 * T
    s = jnp.where(t0, row, (j - 2) * T + row)
    ok = (s <= r) & ((s < N_META) | (s > r - WINDOW))
    use = (t0 & (j >= 2) & (row < N_META)) | (jnp.logical_not(t0) & t1 & (j >= 1)) | jnp.logical_not(t1)
    return ok & use


def _attn_fwd(q, kv, sinks, name, comm=None):
    rows = q.shape[0]
    scale = 1.0 / math.sqrt(ATTN_DH)
    qpk = N_Q_HEADS // N_KV_HEADS

    def body(q_ref, kv_ref, s_ref, o_ref, lse_ref):
        j = pl.program_id(0)
        kv3 = _attn_tiles(kv_ref, j).astype(BF16)
        mask = _attn_mask(j)
        qv = (q_ref[...] * scale).astype(BF16)
        sk = s_ref[...]
        lses = []
        for kh in range(N_KV_HEADS):
            k3 = kv3[:, kh * ATTN_DH:(kh + 1) * ATTN_DH]
            v3 = kv3[:, D_KV + kh * ATTN_DH:D_KV + (kh + 1) * ATTN_DH]
            for g in range(qpk):
                h = kh * qpk + g
                sink = sk[:, h:h + 1]
                sc = jnp.where(mask, _dot(k3, qv[:, h * ATTN_DH:(h + 1) * ATTN_DH], _NT), NEG)
                m = jnp.maximum(jnp.max(sc, axis=0, keepdims=True), sink)
                p = jnp.exp(sc - m)
                den = jnp.sum(p, axis=0, keepdims=True) + jnp.exp(sink - m)
                p = p * (1.0 / den)
                lses.append(m + jnp.log(den))
                o_ref[:, h * ATTN_DH:(h + 1) * ATTN_DH] = _dot(p.astype(BF16), v3, _TN).astype(o_ref.dtype)
        lse_ref[...] = jnp.concatenate(lses, axis=0)

    return _call(
        body, name, [S((rows, D_MODEL), BF16), S((N_Q_HEADS, rows), F32)], (rows // T,),
        [pl.BlockSpec((T, D_MODEL), lambda j: (j, 0)), pl.BlockSpec((rows, 2 * D_KV), lambda j: (0, 0)),
         pl.BlockSpec((1, N_Q_HEADS), lambda j: (0, 0))],
        [pl.BlockSpec((T, D_MODEL), lambda j: (j, 0)), pl.BlockSpec((N_Q_HEADS, T), lambda j: (0, j))],
        ("parallel",), (q, kv, sinks), comm=comm)


def _attn_bwd(q, kv, sinks, do, lse, name, comm=None):
    rows = q.shape[0]
    scale = 1.0 / math.sqrt(ATTN_DH)
    qpk = N_Q_HEADS // N_KV_HEADS

    def body(q_ref, kv_ref, s_ref, do_ref, lse_ref, dq_ref, dkv_ref, ds_ref):
        j = pl.program_id(0)

        @pl.when(j == 0)
        def _():
            dkv_ref[...] = jnp.zeros_like(dkv_ref)
            ds_ref[...] = jnp.zeros_like(ds_ref)

        kv3 = _attn_tiles(kv_ref, j).astype(BF16)
        mask = _attn_mask(j)
        qv = (q_ref[...] * scale).astype(BF16)
        dov = do_ref[...].astype(BF16)
        sk = s_ref[...]
        lsev = lse_ref[...]
        lane = lax.broadcasted_iota(jnp.int32, (1, LANE), 1)
        ds_acc = jnp.zeros((1, LANE), F32)
        prev = jnp.maximum(j - 1, 0)
        mask4 = jnp.concatenate([mask] * qpk, axis=1)
        dqts = []
        for kh in range(N_KV_HEADS):
            ksl = slice(kh * ATTN_DH, (kh + 1) * ATTN_DH)
            vsl = slice(D_KV + kh * ATTN_DH, D_KV + (kh + 1) * ATTN_DH)
            k3, v3 = kv3[:, ksl], kv3[:, vsl]
            heads = [kh * qpk + g for g in range(qpk)]
            q4 = jnp.concatenate([qv[:, h * ATTN_DH:(h + 1) * ATTN_DH] for h in heads], axis=0)
            do4 = jnp.concatenate([dov[:, h * ATTN_DH:(h + 1) * ATTN_DH] for h in heads], axis=0)
            lse4 = jnp.concatenate([lsev[h:h + 1, :] for h in heads], axis=1)
            sink4 = jnp.concatenate([jnp.broadcast_to(sk[:, h:h + 1], (1, T)) for h in heads], axis=1)
            p = jnp.exp(jnp.where(mask4, _dot(k3, q4, _NT), NEG) - lse4)
            ps = jnp.exp(sink4 - lse4)
            dp = _dot(v3, do4, _NT)
            delta = jnp.sum(p * dp, axis=0, keepdims=True)
            dsc = (p * (dp - delta)).astype(BF16)
            dq4 = _dot(k3.T, dsc) * scale
            dk3 = _dot(dsc, q4)
            dv3 = _dot(p.astype(BF16), do4)
            psd = ps * delta
            for g, h in enumerate(heads):
                dqts.append(dq4[:, g * T:(g + 1) * T])
                ds_acc = ds_acc - jnp.sum(psd[:, g * T:(g + 1) * T]) * (lane == h).astype(F32)
            for t, start in enumerate((0, pl.multiple_of(prev * T, T), pl.multiple_of(j * T, T))):
                rsl = pl.ds(start, T)
                dkv_ref[rsl, ksl] += dk3[t * T:(t + 1) * T, :]
                dkv_ref[rsl, vsl] += dv3[t * T:(t + 1) * T, :]
        ds_ref[...] += ds_acc
        dq_ref[...] = jnp.concatenate(dqts, axis=0).T.astype(dq_ref.dtype)

    blk = pl.BlockSpec((T, D_MODEL), lambda j: (j, 0))
    full = pl.BlockSpec((rows, 2 * D_KV), lambda j: (0, 0))
    return _call(
        body, name, [S((rows, D_MODEL), BF16), S((rows, 2 * D_KV), F32), S((1, LANE), F32)], (rows // T,),
        [blk, full, pl.BlockSpec((1, N_Q_HEADS), lambda j: (0, 0)), blk, pl.BlockSpec((N_Q_HEADS, T), lambda j: (0, j))],
        [blk, full, pl.BlockSpec((1, LANE), lambda j: (0, 0))], ("arbitrary",), (q, kv, sinks, do, lse), comm=comm)


BLOCK_BYTES = 1 << 20


def _div_tile(rows, cols):
    cap = max(16, BLOCK_BYTES // (4 * cols))
    best = None
    for t in range(16, min(rows, cap) + 1, 16):
        if rows % t == 0:
            best = t
    return best if best is not None else rows


def _adamw(parts, w, m, v, name, comm=None):
    layers, rows, cols = w.shape
    n = parts[0].shape[0]
    tr = _div_tile(rows, cols)
    tc = _pick(cols, 256) if tr == rows and rows * cols * 4 > 2 * BLOCK_BYTES else cols
    c1 = 1.0 / (1.0 - B1 ** STEP)
    c2 = 1.0 / (1.0 - B2 ** STEP)

    def body(*refs):
        p_refs = refs[:layers]
        w_ref, m_ref, v_ref, g_ref, d_ref, nm_ref, nv_ref = refs[layers:]
        layer = pl.program_id(0)
        for l in range(layers):
            @pl.when(layer == l)
            def _(p_ref=p_refs[l]):
                g = p_ref[0].astype(F32)
                for i in range(1, n):
                    g = g + p_ref[i].astype(F32)
                nm = B1 * m_ref[...] + (1.0 - B1) * g
                nv = B2 * v_ref[...] + (1.0 - B2) * (g * g)
                g_ref[...] = g
                nm_ref[...] = nm
                nv_ref[...] = nv
                d_ref[...] = -LR * ((nm * c1) / (jnp.sqrt(nv * c2) + EPS) + WD * w_ref[...])

    def part_spec(l):
        return pl.BlockSpec((n, tr, tc), lambda k, i, j: (0, jnp.where(k == l, i, 0), jnp.where(k == l, j, 0)))

    row = pl.BlockSpec((None, tr, tc), lambda k, i, j: (k, i, j))
    return _call(body, name, [S((layers, rows, cols), F32)] * 4, (layers, rows // tr, cols // tc),
                 [part_spec(l) for l in range(layers)] + [row, row, row], [row] * 4,
                 ("parallel", "parallel", "parallel"), (*parts, w, m, v), comm=comm)


def _sum_parts(parts, name):
    n, rows, cols = parts[0].shape
    nb = len(parts)
    tr = _div_tile(rows, cols)

    def body(*refs):
        o_ref = refs[nb]
        blk = pl.program_id(0)
        for l in range(nb):
            @pl.when(blk == l)
            def _(p_ref=refs[l]):
                g = p_ref[0].astype(F32)
                for i in range(1, n):
                    g = g + p_ref[i].astype(F32)
                o_ref[...] = g

    def part_spec(l):
        return pl.BlockSpec((n, tr, cols), lambda k, i: (0, jnp.where(k == l, i, 0), 0))

    per = rows // tr
    return pl.pallas_call(body, name=name, out_shape=S((nb * rows, cols), F32), grid=(nb, per),
                          in_specs=[part_spec(l) for l in range(nb)],
                          out_specs=pl.BlockSpec((tr, cols), lambda k, i: (k * per + i, 0)),
                          compiler_params=_cp(("parallel", "parallel")))(*parts)


def _col_segments(ws, runs):
    segs = []
    for glo, mlo, n in runs:
        while n > 0:
            d, off = divmod(glo, ws)
            take = min(n, ws - off)
            segs.append((d, off, mlo, take))
            glo, mlo, n = glo + take, mlo + take, n - take
    return segs


def _assemble_cols(gs, width, segs, name):
    _, rows, ws = gs[0].shape
    nb = len(gs)
    rb = _div_tile(rows, width // 2)
    per = rows // rb

    def body(*refs):
        o_ref = refs[nb]
        piece = pl.program_id(0)
        for l in range(nb):
            @pl.when(piece == l)
            def _(g_ref=refs[l]):
                o_ref[...] = jnp.zeros_like(o_ref)
                for d, off, mlo, n in segs:
                    o_ref[:, mlo:mlo + n] = g_ref[d, :, off:off + n]

    def piece_spec(l):
        return pl.BlockSpec((N_DEV, rb, ws), lambda k, i: (0, jnp.where(k == l, i, 0), 0))

    return pl.pallas_call(
        body, name=name, out_shape=S((nb * rows, width), gs[0].dtype), grid=(nb, per),
        in_specs=[piece_spec(l) for l in range(nb)],
        out_specs=pl.BlockSpec((rb, width), lambda k, i: (k * per + i, 0)),
        compiler_params=_cp(("parallel", "parallel")))(*gs)


def _scatter_cols(dw, ws, segs, name):
    rows, width = dw.shape
    rb = _div_tile(rows, width)

    def body(w_ref, o_ref):
        for d, off, mlo, n in segs:
            o_ref[d, :, off:off + n] = w_ref[:, mlo:mlo + n].astype(o_ref.dtype)

    return pl.pallas_call(
        body, name=name, out_shape=S((N_DEV, rows, ws), BF16), grid=(rows // rb,),
        in_specs=[pl.BlockSpec((rb, width), lambda i: (i, 0))],
        out_specs=pl.BlockSpec((N_DEV, rb, ws), lambda i: (0, i, 0)), compiler_params=_cp(("parallel",)))(dw)


def _gather_comm(xs):
    n = len(xs)

    def setup(x_refs, out_refs, sems):
        send_sems, recv_sems, local_sems = sems
        mx, my, mc = lax.axis_index("x"), lax.axis_index("y"), lax.axis_index("c")
        me, sibling = (mx, my, mc), (mx, my, 1 - mc)
        chips = [(1 - mx, my), (mx, 1 - my), (1 - mx, 1 - my)]

        def blk(a, px, py, pc):
            return out_refs[a].at[4 * px + 2 * py + pc]

        def copy(a, k, block, to, src=None):
            return pltpu.make_async_remote_copy(
                src_ref=blk(a, *block) if src is None else src, dst_ref=blk(a, *block),
                send_sem=send_sems.at[a, k], recv_sem=recv_sems.at[a, k], device_id=to, device_id_type=_MESH)

        mine = [pltpu.make_async_copy(x_refs[a], blk(a, *me), local_sems.at[a]) for a in range(n)]
        own = []
        for a in range(n):
            own.append(copy(a, 0, me, sibling, src=x_refs[a]))
            own += [copy(a, 1 + i, me, (*chip, mc), src=x_refs[a]) for i, chip in enumerate(chips)]
        return me, sibling, chips, mc, copy, mine, own

    def first(x_refs, out_refs, sems):
        _, _, _, _, _, mine, own = setup(x_refs, out_refs, sems)
        for cp in mine + own:
            cp.start()

    def last(x_refs, out_refs, sems):
        me, sibling, chips, mc, copy, mine, own = setup(x_refs, out_refs, sems)
        passed = []
        for a in range(n):
            for i, chip in enumerate(chips):
                copy(a, 1 + i, (*chip, mc), me).wait_recv()
                passed.append(copy(a, 4 + i, (*chip, mc), sibling))
                passed[-1].start()
        for a in range(n):
            copy(a, 0, sibling, me).wait_recv()
            for i, chip in enumerate(chips):
                copy(a, 4 + i, (*chip, 1 - mc), me).wait_recv()
        for cp in own + passed:
            cp.wait_send()
        for cp in mine:
            cp.wait()

    return _Comm(list(xs), [S((N_DEV,) + x.shape, x.dtype) for x in xs],
                 [pltpu.SemaphoreType.DMA((n, 7)), pltpu.SemaphoreType.DMA((n, 7)), pltpu.SemaphoreType.DMA((n,))],
                 first, last)


def _swap_comm(gs):
    n = len(gs)

    def copies(g_refs, out_refs, sems):
        send_sems, recv_sems = sems
        mx, my, mc = lax.axis_index("x"), lax.axis_index("y"), lax.axis_index("c")
        return [pltpu.make_async_remote_copy(
            src_ref=g_refs[a].at[2 * k + 1 - mc], dst_ref=out_refs[a].at[k], send_sem=send_sems.at[a, k],
            recv_sem=recv_sems.at[a, k], device_id=(mx, my, 1 - mc), device_id_type=_MESH)
            for a in range(n) for k in range(4)]

    def first(g_refs, out_refs, sems):
        for cp in copies(g_refs, out_refs, sems):
            cp.start()

    def last(g_refs, out_refs, sems):
        for cp in copies(g_refs, out_refs, sems):
            cp.wait()

    return _Comm(list(gs), [S((4,) + g.shape[1:], g.dtype) for g in gs],
                 [pltpu.SemaphoreType.DMA((n, 4)), pltpu.SemaphoreType.DMA((n, 4))], first, last)


def _chips_comm(parts):
    n = len(parts)

    def copies(p_refs, out_refs, sems):
        send_sems, recv_sems, local_sems = sems
        mx, my, mc = lax.axis_index("x"), lax.axis_index("y"), lax.axis_index("c")
        mychip = 2 * mx + my
        chips = [(1 - mx, my), (mx, 1 - my), (1 - mx, 1 - my)]
        mine = [pltpu.make_async_copy(p_refs[a].at[mychip], out_refs[a].at[mychip], local_sems.at[a])
                for a in range(n)]
        return mine + [pltpu.make_async_remote_copy(
            src_ref=p_refs[a].at[2 * cx + cy], dst_ref=out_refs[a].at[mychip], send_sem=send_sems.at[a, i],
            recv_sem=recv_sems.at[a, i], device_id=(cx, cy, mc), device_id_type=_MESH)
            for a in range(n) for i, (cx, cy) in enumerate(chips)]

    def first(p_refs, out_refs, sems):
        for cp in copies(p_refs, out_refs, sems):
            cp.start()

    def last(p_refs, out_refs, sems):
        for cp in copies(p_refs, out_refs, sems):
            cp.wait()

    return _Comm(list(parts), [S(p.shape, p.dtype) for p in parts],
                 [pltpu.SemaphoreType.DMA((n, 3)), pltpu.SemaphoreType.DMA((n, 3)), pltpu.SemaphoreType.DMA((n,))],
                 first, last)


def _join_comms(comms):
    def split(refs, counts):
        out, p = [], 0
        for cnt in counts:
            out.append(refs[p:p + cnt])
            p += cnt
        return out

    ni = [len(c.ins) for c in comms]
    no = [len(c.out_shapes) for c in comms]
    ns = [len(c.scratch) for c in comms]

    def first(in_refs, out_refs, sems):
        for c, i, o, s in zip(comms, split(in_refs, ni), split(out_refs, no), split(sems, ns)):
            c.first(i, o, s)

    def last(in_refs, out_refs, sems):
        for c, i, o, s in zip(comms, split(in_refs, ni), split(out_refs, no), split(sems, ns)):
            c.last(i, o, s)

    return _Comm([x for c in comms for x in c.ins], [x for c in comms for x in c.out_shapes],
                 [x for c in comms for x in c.scratch], first, last)


def _add_pairs(mine, theirs, core, name):
    _, rows, cols = mine.shape
    tr = _div_tile(rows, cols)

    def body(core_ref, a_ref, b_ref, o_ref):
        o_ref[...] = (a_ref[...].astype(F32) + b_ref[...].astype(F32)).astype(o_ref.dtype)

    return pl.pallas_call(
        body, name=name, out_shape=S((4, rows, cols), BF16),
        grid_spec=pltpu.PrefetchScalarGridSpec(
            num_scalar_prefetch=1, grid=(4, rows // tr),
            in_specs=[pl.BlockSpec((None, tr, cols), lambda k, i, c: (2 * k + c[0], i, 0)),
                      pl.BlockSpec((None, tr, cols), lambda k, i, c: (k, i, 0))],
            out_specs=pl.BlockSpec((None, tr, cols), lambda k, i, c: (k, i, 0))),
        compiler_params=_cp(("parallel", "parallel")))(core, mine, theirs)


def _run_comm(comm, name):
    ci, co = len(comm.ins), len(comm.out_shapes)

    def body(*refs):
        comm.first(refs[:ci], refs[ci:ci + co], refs[ci + co:])
        comm.last(refs[:ci], refs[ci:ci + co], refs[ci + co:])

    return pl.pallas_call(body, name=name, out_shape=list(comm.out_shapes), in_specs=[_HBM] * ci,
                          out_specs=[_HBM] * co, scratch_shapes=list(comm.scratch))(*comm.ins)


def _flat_rows(n_elems, mult):
    rows = -(-n_elems // LANE)
    return -(-rows // mult) * mult


def _pack(arrs, lead, mult, dtype):
    lead_shape = arrs[0].shape[:lead]
    flat = jnp.concatenate([a.astype(dtype).reshape(lead_shape + (-1,)) for a in arrs], axis=-1)
    n = flat.shape[-1]
    rows = _flat_rows(n, mult)
    flat = jnp.pad(flat, [(0, 0)] * lead + [(0, rows * LANE - n)])
    return flat.reshape(lead_shape + (rows, LANE))


def _unpack(flat, lead, shapes):
    lead_shape = flat.shape[:lead]
    flat = flat.reshape(lead_shape + (-1,))
    out, off = [], 0
    for shp in shapes:
        n = math.prod(shp)
        out.append(flat[..., off:off + n].reshape(lead_shape + tuple(shp)))
        off += n
    return out


def _split8(full, ax, n):
    shp = full.shape
    return jnp.moveaxis(full.reshape(shp[:ax] + (N_DEV, n) + shp[ax + 1:]), ax, 0)


def _join8(g, ax):
    shp = g.shape[1:]
    return jnp.moveaxis(g, 0, ax).reshape(shp[:ax] + (N_DEV * shp[ax],) + shp[ax + 1:])


def _group_lanes(v, hg):
    v = v.reshape(SSM_GROUPS, hg)
    return jnp.pad(v, ((0, 0), (0, LANE - hg))).reshape(1, SSM_GROUPS * LANE)


def _ungroup_lanes(v, hg):
    return v.reshape(SSM_GROUPS, LANE)[:, :hg].reshape(1, SSM_GROUPS * hg)


def kernel(x, meta_tokens, a_norm_pre, a_w_in, a_conv_w, a_conv_b, a_dt_bias, a_a_log, a_d_skip, a_gate_norm, a_w_out, a_norm_post, kv_norm, w_kv, b_norm_pre, b_w_q, b_sinks, b_w_o, b_norm_post, f_norm_pre, f_w_up, f_conv_w, f_conv_b, f_w_down, f_norm_post, loss_target, m_meta_tokens, m_a_norm_pre, m_a_w_in, m_a_conv_w, m_a_conv_b, m_a_dt_bias, m_a_a_log, m_a_d_skip, m_a_gate_norm, m_a_w_out, m_a_norm_post, m_kv_norm, m_w_kv, m_b_norm_pre, m_b_w_q, m_b_sinks, m_b_w_o, m_b_norm_post, m_f_norm_pre, m_f_w_up, m_f_conv_w, m_f_conv_b, m_f_w_down, m_f_norm_post, v_meta_tokens, v_a_norm_pre, v_a_w_in, v_a_conv_w, v_a_conv_b, v_a_dt_bias, v_a_a_log, v_a_d_skip, v_a_gate_norm, v_a_w_out, v_a_norm_post, v_kv_norm, v_w_kv, v_b_norm_pre, v_b_w_q, v_b_sinks, v_b_w_o, v_b_norm_post, v_f_norm_pre, v_f_w_up, v_f_conv_w, v_f_conv_b, v_f_w_down, v_f_norm_post):
    args = locals()
    wts = {n: args[n] for n in WEIGHTS}
    mom = {n: args["m_" + n] for n in WEIGHTS}
    var = {n: args["v_" + n] for n in WEIGHTS}
    mx, my, mc = lax.axis_index("x"), lax.axis_index("y"), lax.axis_index("c")
    me = 4 * mx + 2 * my + mc
    rows = _seq_rows()
    hg = SSM_HEADS // SSM_GROUPS
    d = D_MODEL

    n_main = D_INNER + D_XBC
    ws_in, ws_up = a_w_in.shape[2], f_w_up.shape[2]
    segs_in = _col_segments(ws_in, [(0, 0, n_main)] + [(n_main + hg * g, n_main + LANE * g, hg)
                                                      for g in range(SSM_GROUPS)])
    segs_up = _col_segments(ws_up, [(0, 0, 2 * D_FF)])
    def gather_of(*ws):
        return _gather_comm([w.astype(BF16) for w in ws])

    small_full, = _run_comm(_gather_comm([_pack([wts[n] for n in SMALL], 0, 8, F32)]), "gather_small")
    full = {}
    for n, g in zip(SMALL, _unpack(small_full, 1, [wts[n].shape for n in SMALL])):
        full[n] = _join8(g, SHARD_AXIS[n])
    (h0, hn0), (g_in,) = _embed_norm(full["meta_tokens"], x[0], full["a_norm_pre"], rows, "embed_norm",
                                     comm=gather_of(a_w_in[0]))
    w_in_all = _assemble_cols([g_in], n_main + SSM_GROUPS * LANE, segs_in, "asm_w_in")
    w_up, w_down = [None, None], [None, None]
    bias_g = _group_lanes(wts["a_dt_bias"], hg)
    alog_g = _group_lanes(wts["a_a_log"], hg)
    dsk_g = _group_lanes(wts["a_d_skip"], hg)
    a_conv_w, a_conv_b = full["a_conv_w"][0], full["a_conv_b"]
    f_cw, f_cb = full["f_conv_w"], wts["f_conv_b"]
    fpre, fpost = wts["f_norm_pre"], wts["f_norm_post"]


    zx, (g_out,) = _mm(hn0, w_in_all, "nn", F32, "mm_in", comm=gather_of(a_w_out[0]))
    w_out = g_out.reshape(D_INNER, d)
    xbc = _conv_silu_fwd(zx, a_conv_w, a_conv_b, "conv_a")
    (y_ssd, hst), (g_up0,) = _ssd_fwd(xbc, zx, bias_g, alog_g, dsk_g, "ssd_fwd", comm=gather_of(f_w_up[0]))
    w_up[0] = _assemble_cols([g_up0], 2 * D_FF, segs_up, "asm_w_up0")
    yn = _gatenorm_fwd(y_ssd, zx, full["a_gate_norm"], "gatenorm")
    mix_a, (g_o,) = _mm(yn, w_out, "nn", F32, "mm_out", comm=gather_of(b_w_o[0]))
    h1, (fn0,) = _resid_norm(h0, mix_a, full["a_norm_post"], [fpre[0:1]], "resid_a")

    half = d // 2
    u0, (g_dn0,) = _mm(fn0, w_up[0], "nn", F32, "mm_up0", comm=gather_of(f_w_down[0]))
    act0, (g_up1a,) = _ffn_act_fwd(u0, f_cw[0], f_cb[0:1], "ffn_act0", comm=gather_of(f_w_up[1, :half]))
    ffn0, (g_kv, g_q) = _mm(act0, g_dn0.reshape(D_FF, d), "nn", F32, "mm_down0", comm=gather_of(w_kv, b_w_q[0]))
    w_kvf, w_q, w_o = g_kv.reshape(d, 2 * D_KV), g_q.reshape(d, d), g_o.reshape(d, d)
    h2, (kvn, bn) = _resid_norm(h1, ffn0, fpost[0:1], [wts["kv_norm"].reshape(1, d), wts["b_norm_pre"]], "resid_f0")
    kv = _mm(kvn, w_kvf, "nn", F32, "mm_kv")
    q = _mm(bn, w_q, "nn", F32, "mm_q")
    (o, lse), (g_up1b,) = _attn_fwd(q, kv, wts["b_sinks"], "attn_fwd", comm=gather_of(f_w_up[1, half:]))
    w_up[1] = _assemble_cols([g_up1a, g_up1b], 2 * D_FF, segs_up, "asm_w_up1")
    mix_b = _mm(o, w_o, "nn", F32, "mm_o")
    h3, (fn1,) = _resid_norm(h2, mix_b, wts["b_norm_post"], [fpre[1:2]], "resid_b")
    u1, (g_dn1,) = _mm(fn1, w_up[1], "nn", F32, "mm_up1", comm=gather_of(f_w_down[1]))
    w_down = [g_dn0.reshape(D_FF, d), g_dn1.reshape(D_FF, d)]
    act1 = _ffn_act_fwd(u1, f_cw[1], f_cb[1:2], "ffn_act1")
    ffn1 = _mm(act1, w_down[1], "nn", F32, "mm_down1")
    dh4, loss_row, dffn1, dw_post1 = _final_loss(h3, ffn1, fpost[1:2], loss_target[0], "loss")
    loss = lax.psum(loss_row[0, 0], ("x", "y", "c"))

    grads = {}

    core = mc.astype(jnp.int32).reshape(1)

    def carried(res, comm):
        return res if comm is not None else (res, None)

    def ffn_bwd(dh_out, dffn, h_in, fn, u, act, i, then, c_dact=None, c_dwdown=None, c_dwup=None, c_dfn=None):
        dact, got_a = carried(_mm(dffn, w_down[i], "nt", F32, f"mm_dact{i}", comm=c_dact), c_dact)
        dw_down, got_b = carried(_mm(act, dffn, "tn", BF16, f"mm_dwdown{i}", comm=c_dwdown), c_dwdown)
        dw_down = dw_down.reshape(N_DEV, -1, d)
        du, dwc, dbc = _ffn_act_bwd(u, dact, f_cw[i], f_cb[i:i + 1], f"ffn_act_bwd{i}")
        dfn, (s_dn, *got_d) = _mm(du, w_up[i], "nt", F32, f"mm_dfn{i}", comm=_join_comms(
            [_swap_comm([dw_down])] + ([c_dfn] if c_dfn is not None else [])))
        sum_dn = _add_pairs(dw_down, s_dn, core, f"rs_add_dn{i}")
        dw_up, got_c = carried(_mm(fn, du, "tn", BF16, f"mm_dwup{i}", comm=c_dwup, shard_cols=ws_up), c_dwup)
        (dh_in, dw_pre, dbranch, dw_branch), (s_up,) = _norm_bwd(
            h_in, fpre[i:i + 1], dfn, dh_out, F32, f"nb_fpre{i}", comm=_swap_comm([dw_up]), then=then)
        sum_up = _add_pairs(dw_up, s_up, core, f"rs_add_up{i}")
        return dh_in, dbranch, dw_branch, dict(sum_down=sum_dn, cw=jnp.concatenate([dwc[0], dwc[1]], axis=1),
                                               cb=jnp.concatenate([dbc[0], dbc[1]], axis=1), sum_up=sum_up,
                                               pre=dw_pre), got_a, got_b, got_c, got_d

    dh3, dmix_b, grads["b_norm_post"], gf1, _, _, _, _ = ffn_bwd(dh4, dffn1, h3, fn1, u1, act1, 1,
                                                                 (mix_b, wts["b_norm_post"]))
    do = _mm(dmix_b, w_o, "nt", F32, "mm_do")
    dw_o = _mm(o, dmix_b, "tn", BF16, "mm_dwo").reshape(N_DEV, -1, d)
    half_up = gf1["sum_up"].shape[1] // 2
    (dq, dkv, dsinks), (p_up1a, s_o) = _attn_bwd(
        q, kv, wts["b_sinks"], do, lse, "attn_bwd",
        comm=_join_comms([_chips_comm([gf1["sum_up"][:, :half_up]]), _swap_comm([dw_o])]))
    sum_o = _add_pairs(dw_o, s_o, core, "rs_add_o")
    grads["b_sinks"] = dsinks[:, :N_Q_HEADS]
    dbn = _mm(dq, w_q, "nt", F32, "mm_dbn")
    dw_q = _mm(bn, dq, "tn", BF16, "mm_dwq").reshape(N_DEV, -1, d)
    dkv16 = dkv.astype(BF16)
    dkvn = _mm(dkv16, w_kvf, "nt", F32, "mm_dkvn")
    dw_kv = _mm(kvn, dkv16, "tn", BF16, "mm_dwkv").reshape(N_DEV, -1, 2 * D_KV)
    (dh2, grads["b_norm_pre"]), (s_q, s_kv) = _norm_bwd(h2, wts["b_norm_pre"], dbn, dh3, F32, "nb_bpre",
                                                        comm=_swap_comm([dw_q, dw_kv]))
    sum_q, sum_kv = _add_pairs(dw_q, s_q, core, "rs_add_q"), _add_pairs(dw_kv, s_kv, core, "rs_add_kv")
    dh2, dw_kvn, dffn0, dw_post0 = _norm_bwd(h2, wts["kv_norm"].reshape(1, d), dkvn, dh2, F32, "nb_kv",
                                             then=(ffn0, fpost[0:1]))
    grads["kv_norm"] = dw_kvn.reshape(d)
    dh1, dmix_a, grads["a_norm_post"], gf0, (p_o,), (p_q, p_kv), (p_dn1,), (p_up1b,) = ffn_bwd(
        dh2, dffn0, h1, fn0, u0, act0, 0, (mix_a, full["a_norm_post"]), c_dact=_chips_comm([sum_o]),
        c_dwdown=_chips_comm([sum_q, sum_kv]), c_dwup=_chips_comm([gf1["sum_down"]]),
        c_dfn=_chips_comm([gf1["sum_up"][:, half_up:]]))
    p_up1 = jnp.concatenate([p_up1a, p_up1b], axis=1)
    grads["f_norm_post"] = jnp.concatenate([dw_post0, dw_post1], axis=0)
    grads["f_norm_pre"] = jnp.concatenate([gf0["pre"], gf1["pre"]], axis=0)
    grads["f_conv_w"] = jnp.stack([gf0["cw"], gf1["cw"]])
    grads["f_conv_b"] = jnp.concatenate([gf0["cb"], gf1["cb"]], axis=0)

    dyn = _mm(dmix_a, w_out, "nt", F32, "mm_dyn")
    dw_out = _mm(yn, dmix_a, "tn", BF16, "mm_dwout").reshape(N_DEV, -1, d)
    (dy_ssd, dzx, grads["a_gate_norm"]), (s_out,) = _gatenorm_bwd(y_ssd, zx, full["a_gate_norm"], dyn, "gatenorm_bwd",
                                                                  comm=_swap_comm([dw_out]))
    sum_out = _add_pairs(dw_out, s_out, core, "rs_add_out")
    (dxs, dbm, dcm, dzx, dalog, ddsk, dbias), (p_up0,) = _ssd_bwd(
        xbc, zx, bias_g, alog_g, dsk_g, dy_ssd, hst, dzx, "ssd_bwd", comm=_chips_comm([gf0["sum_up"]]))
    grads["a_a_log"] = _ungroup_lanes(dalog, hg)
    grads["a_d_skip"] = _ungroup_lanes(ddsk, hg)
    grads["a_dt_bias"] = _ungroup_lanes(dbias, hg)
    dzx, dcw, dcb = _conv_silu_bwd(zx, dxs, dbm, dcm, a_conv_w, a_conv_b, dzx, "conv_a_bwd")
    grads["a_conv_w"], grads["a_conv_b"] = dcw[None], dcb
    dw_in_all, (p_dn0,) = _mm(hn0, dzx, "tn", BF16, "mm_dwin", comm=_chips_comm([gf0["sum_down"]]))
    dw_in8 = _scatter_cols(dw_in_all, ws_in, segs_in, "scat_w_in")
    dhn0, (s_in, p_out) = _mm(dzx, w_in_all, "nt", F32, "mm_dhn0",
                              comm=_join_comms([_swap_comm([dw_in8]), _chips_comm([sum_out])]))
    sum_in = _add_pairs(dw_in8, s_in, core, "rs_add_in")
    half_in = sum_in.shape[1] // 2
    (grads["meta_tokens"], g_x, grads["a_norm_pre"]), (p_in_a,) = _norm_bwd(
        h0, full["a_norm_pre"], dhn0, dh1, F32, "nb_apre", comm=_chips_comm([sum_in[:, :half_in]]), split_rows=SEQ)
    grad_x = g_x[None]

    small_local = _pack([_split8(grads[n], SHARD_AXIS[n], wts[n].shape[SHARD_AXIS[n]]) for n in SMALL], 1, 8, F32)
    repl_local = _pack([grads[n] for n in REPL], 0, 8, F32)
    n_sr = small_local.shape[1]
    small_vec = jnp.concatenate([small_local.reshape(N_DEV * n_sr, LANE), repl_local], axis=0)
    tail = _join_comms([_chips_comm([sum_in[:, half_in:]]), _gather_comm([small_vec])])
    parts_big = dict(a_w_out=[p_out], w_kv=[p_kv], b_w_q=[p_q], b_w_o=[p_o], f_w_down=[p_dn0, p_dn1])

    def flat_f32(dct, names, mult):
        return _pack([dct[n] for n in names], 0, mult, F32)

    def adamw_big(n, comm=None):
        shp3 = (len(parts_big[n]),) + parts_big[n][0].shape[1:]
        res = _adamw(parts_big[n], *[dct[n].reshape(shp3) for dct in (wts, mom, var)], f"adamw_{n}", comm=comm)
        res, got = res if comm is not None else (res, None)
        big_out[n] = [r.reshape(wts[n].shape) for r in res]
        return got

    big_out = {}
    def swap_last(a):
        return jnp.swapaxes(a, -1, -2)

    g_up_t = swap_last(_sum_parts([p_up0, p_up1], "sum_w_up").reshape(f_w_up.shape))
    res, (p_in_b, small_all) = _adamw([g_up_t[0:1], g_up_t[1:2]], *[swap_last(dct["f_w_up"]) for dct in (wts, mom, var)],
                                      "adamw_f_w_up", comm=tail)
    big_out["f_w_up"] = [swap_last(r) for r in res]
    for n in BIG:
        if n not in ("f_w_up", "a_w_in"):
            adamw_big(n)
    g_in_t = swap_last(_sum_parts([p_in_a, p_in_b], "sum_w_in"))[None]
    res = _adamw([g_in_t], *[swap_last(dct["a_w_in"]) for dct in (wts, mom, var)], "adamw_a_w_in")
    big_out["a_w_in"] = [swap_last(r) for r in res]
    mine_small = lax.dynamic_slice_in_dim(small_all, me * n_sr, n_sr, axis=1)
    parts_small = jnp.concatenate([mine_small, small_all[:, N_DEV * n_sr:]], axis=1)
    sm_in = [jnp.concatenate([flat_f32(dct, SMALL, 8), flat_f32(dct, REPL, 8)], axis=0)[None] for dct in (wts, mom, var)]
    small_out = [r[0] for r in _adamw([parts_small], *sm_in, "adamw_small")]

    outs = []
    for kind in range(4):
        res = {n: big_out[n][kind] for n in BIG}
        for n, a in zip(SMALL, _unpack(small_out[kind][:n_sr], 0, [wts[n].shape for n in SMALL])):
            res[n] = a
        for n, a in zip(REPL, _unpack(small_out[kind][n_sr:], 0, [wts[n].shape for n in REPL])):
            res[n] = a
        outs.append(res)
    return (loss, grad_x, *[outs[0][n] for n in WEIGHTS], *[outs[1][n] for n in WEIGHTS],
            *[outs[2][n] for n in WEIGHTS], *[outs[3][n] for n in WEIGHTS])
```

```python
import functools
import math

import jax
import jax.numpy as jnp
from jax import lax
from jax.experimental import pallas as pl
from jax.experimental.pallas import tpu as pltpu

F32, BF16 = jnp.float32, jnp.bfloat16
S = jax.ShapeDtypeStruct

D_MODEL = 1024
SEQ = 2048
N_META = 16
D_INNER = 2048
HEAD_P = 64
SSM_HEADS = D_INNER // HEAD_P
SSM_GROUPS = 4
D_STATE = 128
SSM_CONV = 4
D_BC = SSM_GROUPS * D_STATE
D_XBC = D_INNER + 2 * D_BC
ATTN_DH = 64
N_Q_HEADS = D_MODEL // ATTN_DH
N_KV_HEADS = 4
D_KV = N_KV_HEADS * ATTN_DH
WINDOW = 128
D_FF = 2816
FFN_CONV = 3
RMS_EPS = 1e-6
NEG = -1e30
LR, B1, B2, EPS, WD, STEP = 0.001, 0.9, 0.999, 1e-08, 0.01, 10

N_DEV = 8
T = 128
LANE = 128
VMEM_LIMIT = 48 * 1024 * 1024

BIG = ("a_w_in", "a_w_out", "w_kv", "b_w_q", "b_w_o", "f_w_up", "f_w_down")
SMALL = ("meta_tokens", "a_norm_pre", "a_conv_w", "a_conv_b", "a_gate_norm", "a_norm_post", "f_conv_w")
REPL = ("a_dt_bias", "a_a_log", "a_d_skip", "kv_norm", "b_norm_pre", "b_sinks", "b_norm_post",
        "f_norm_pre", "f_conv_b", "f_norm_post")
SHARD_AXIS = dict(a_w_in=2, a_w_out=1, w_kv=0, b_w_q=1, b_w_o=1, f_w_up=2, f_w_down=1, meta_tokens=1,
                  a_norm_pre=1, a_conv_w=2, a_conv_b=1, a_gate_norm=1, a_norm_post=1, f_conv_w=2)
WEIGHTS = ("meta_tokens", "a_norm_pre", "a_w_in", "a_conv_w", "a_conv_b", "a_dt_bias", "a_a_log", "a_d_skip",
           "a_gate_norm", "a_w_out", "a_norm_post", "kv_norm", "w_kv", "b_norm_pre", "b_w_q", "b_sinks", "b_w_o",
           "b_norm_post", "f_norm_pre", "f_w_up", "f_conv_w", "f_conv_b", "f_w_down", "f_norm_post")


def _seq_rows():
    return -(-(N_META + SEQ) // T) * T


def _cp(sem=None):
    return pltpu.CompilerParams(dimension_semantics=sem, vmem_limit_bytes=VMEM_LIMIT)


def _pick(n, target):
    t = min(n, target)
    t -= t % LANE
    while n % t:
        t -= LANE
    return t


def _sigmoid(x):
    return 0.5 * jnp.tanh(0.5 * x) + 0.5


def _softplus(x):
    return jnp.maximum(x, 0.0) + jnp.log(1.0 + jnp.exp(-jnp.abs(x)))


_NN = (((1,), (0,)), ((), ()))
_NT = (((1,), (1,)), ((), ()))
_TN = (((0,), (0,)), ((), ()))


def _dot(a, b, dims=_NN):
    return lax.dot_general(a, b, dims, preferred_element_type=F32)


def _dot_hi(a, b):
    return lax.dot_general(a, b, _NN, precision=lax.Precision.HIGHEST, preferred_element_type=F32)


_HBM = pl.BlockSpec(memory_space=pltpu.HBM)
_MESH = pl.DeviceIdType.MESH


class _Comm:
    def __init__(self, ins, out_shapes, scratch, first, last):
        self.ins, self.out_shapes, self.scratch, self.first, self.last = ins, out_shapes, scratch, first, last


_ANY = pl.BlockSpec(memory_space=pl.ANY)


def _call(body, name, out_shape, grid, in_specs, out_specs, sem, args, scratch=(), comm=None, aliases=None):
    aliases = aliases or {}
    if comm is None:
        return pl.pallas_call(body, name=name, out_shape=out_shape, grid=grid, in_specs=in_specs, out_specs=out_specs,
                              scratch_shapes=list(scratch), input_output_aliases=aliases,
                              compiler_params=_cp(sem))(*args)
    single = not isinstance(out_shape, (list, tuple))
    outs = [out_shape] if single else list(out_shape)
    ospecs = [out_specs] if single else list(out_specs)
    n_in, n_out, n_scr, ci, co = len(in_specs), len(outs), len(scratch), len(comm.ins), len(comm.out_shapes)

    def carrier(*refs):
        p = 0
        parts = []
        for cnt in (n_in, ci, n_out, co, n_scr, len(comm.scratch)):
            parts.append(refs[p:p + cnt])
            p += cnt
        ins, cins, outs_r, couts, scr, cscr = parts
        ids = [pl.program_id(i) for i in range(len(grid))]
        first, last = ids[0] == 0, ids[0] == grid[0] - 1
        for i in range(1, len(grid)):
            first, last = first & (ids[i] == 0), last & (ids[i] == grid[i] - 1)

        @pl.when(first)
        def _():
            comm.first(cins, couts, cscr)

        body(*ins, *outs_r, *scr)

        @pl.when(last)
        def _():
            comm.last(cins, couts, cscr)

    res = pl.pallas_call(
        carrier, name=name, out_shape=outs + list(comm.out_shapes), grid=grid,
        in_specs=list(in_specs) + [_HBM] * ci, out_specs=ospecs + [_HBM] * co,
        scratch_shapes=list(scratch) + list(comm.scratch), input_output_aliases=aliases,
        compiler_params=_cp(("arbitrary",) * len(grid)))(*args, *comm.ins)
    mine = res[0] if single else list(res[:n_out])
    return mine, list(res[n_out:])


def _mm(a, b, mode, out_dtype, name, comm=None, shard_cols=None):
    if mode == "tn":
        m, kk = a.shape
        planes, width = (b.shape[0], b.shape[2]) if b.ndim == 3 else (1, b.shape[1])
        n = planes * width
        tko, tn = _pick(kk, 512), _pick(width, 1536)
        per = width // tn
        b_spec = (pl.BlockSpec((None, m, tn), lambda i, j: (j // per, 0, j % per)) if b.ndim == 3
                  else pl.BlockSpec((m, tn), lambda i, j: (0, j)))
        if shard_cols is None:
            def body(a_ref, b_ref, o_ref):
                o_ref[...] = _dot(a_ref[...], b_ref[...], _TN).astype(o_ref.dtype)

            out_shape, out_spec = S((kk, n), out_dtype), pl.BlockSpec((tko, tn), lambda i, j: (i, j))
        else:
            shards = tn // shard_cols
            assert tn % shard_cols == 0

            def body(a_ref, b_ref, o_ref):
                res = _dot(a_ref[...], b_ref[...], _TN).astype(o_ref.dtype)
                for p in range(shards):
                    o_ref[p] = res[:, p * shard_cols:(p + 1) * shard_cols]

            out_shape = S((n // shard_cols, kk, shard_cols), out_dtype)
            out_spec = pl.BlockSpec((shards, tko, shard_cols), lambda i, j: (j, i, 0))
        return _call(
            body, name, out_shape, (kk // tko, n // tn), [pl.BlockSpec((m, tko), lambda i, j: (0, i)), b_spec],
            out_spec, ("parallel", "parallel"), (a, b), comm=comm)

    planes, width = (a.shape[0], a.shape[2]) if a.ndim == 3 else (1, a.shape[1])
    m, kk = a.shape[-2], planes * width
    n = b.shape[1] if mode == "nn" else b.shape[0]
    dims = _NN if mode == "nn" else _NT

    if kk > 2048:
        tm = m // 4
        assert m % 4 == 0 and tm % 16 == 0

        def body(a_ref, b_ref, o_ref):
            if a.ndim == 2:
                res = _dot(a_ref[...], b_ref[...], dims)
            else:
                res = None
                for p in range(planes):
                    bp = b_ref[p * width:(p + 1) * width, :] if mode == "nn" else b_ref[:, p * width:(p + 1) * width]
                    part = _dot(a_ref[p], bp, dims)
                    res = part if res is None else res + part
            o_ref[...] = res.astype(o_ref.dtype)

        a_spec = (pl.BlockSpec((planes, tm, width), lambda i: (0, i, 0)) if a.ndim == 3
                  else pl.BlockSpec((tm, kk), lambda i: (i, 0)))
        return _call(
            body, name, S((m, n), out_dtype), (m // tm,),
            [a_spec, pl.BlockSpec(b.shape, lambda i: (0, 0), pipeline_mode=pl.Buffered(1))],
            pl.BlockSpec((tm, n), lambda i: (i, 0)), ("parallel",), (a, b), comm=comm)

    tn = _pick(n, 512)

    def body(a_ref, b_ref, o_ref):
        o_ref[...] = _dot(a_ref[...], b_ref[...], dims).astype(o_ref.dtype)

    b_spec = (pl.BlockSpec((kk, tn), lambda j: (0, j)) if mode == "nn" else pl.BlockSpec((tn, kk), lambda j: (j, 0)))
    return _call(
        body, name, S((m, n), out_dtype), (n // tn,), [pl.BlockSpec((m, kk), lambda j: (0, 0)), b_spec],
        pl.BlockSpec((m, tn), lambda j: (0, j)), ("parallel",), (a, b), comm=comm)


def _rms(x, w):
    return x * lax.rsqrt(jnp.mean(x * x, axis=-1, keepdims=True) + RMS_EPS) * w


def _row_tile(rows, d):
    return rows // 4 if d <= 1024 and (rows // 4) % 16 == 0 else rows // 8


def _embed_norm(meta, x, w, rows, name, comm=None):
    n_meta, d = meta.shape
    n_x = x.shape[0]
    last = rows // T - 1
    assert n_meta % 8 == 0 and n_meta < T and n_meta + n_x == last * T + n_meta and last * T >= n_x

    def body(m_ref, x_ref, w_ref, h_ref, hn_ref):
        i = pl.program_id(0)

        @pl.when(i == 0)
        def _():
            h_ref[0:n_meta, :] = m_ref[...]
            h_ref[n_meta:T, :] = x_ref[0:T - n_meta, :]

        @pl.when((i > 0) & (i < last))
        def _():
            h_ref[...] = x_ref[pl.ds(pl.multiple_of(i * T - n_meta, 8), T), :]

        @pl.when(i == last)
        def _():
            h_ref[0:n_meta, :] = x_ref[n_x - n_meta:n_x, :]
            h_ref[n_meta:T, :] = jnp.zeros((T - n_meta, d), F32)

        hn_ref[...] = _rms(h_ref[...], w_ref[...]).astype(hn_ref.dtype)

    row = pl.BlockSpec((T, d), lambda i: (i, 0))
    return _call(body, name, [S((rows, d), F32), S((rows, d), BF16)], (rows // T,),
                 [pl.BlockSpec((n_meta, d), lambda i: (0, 0)), pl.BlockSpec((n_x, d), lambda i: (0, 0)),
                  pl.BlockSpec((1, d), lambda i: (0, 0))], [row, row], ("parallel",), (meta, x, w), comm=comm)


def _resid_norm(h, br, w_post, next_ws, name):
    rows, d = h.shape
    tr = _row_tile(rows, d)
    has_br = br is not None
    nw = len(next_ws)

    def body(*refs):
        h_ref = refs[0]
        pos = 1
        x = h_ref[...]
        if has_br:
            x = x + _rms(refs[1][...], refs[2][...])
            pos = 3
        w_refs = refs[pos:pos + nw]
        outs = refs[pos + nw:]
        if has_br:
            outs[0][...] = x
            outs = outs[1:]
        for w_ref, o_ref in zip(w_refs, outs):
            o_ref[...] = _rms(x, w_ref[...]).astype(o_ref.dtype)

    row = pl.BlockSpec((tr, d), lambda i: (i, 0))
    vec = pl.BlockSpec((1, d), lambda i: (0, 0))
    ins = [h] + ([br, w_post] if has_br else []) + list(next_ws)
    in_specs = [row] + ([row, vec] if has_br else []) + [vec] * nw
    out_shape = ([S((rows, d), F32)] if has_br else []) + [S((rows, d), BF16)] * nw
    res = pl.pallas_call(body, name=name, out_shape=out_shape, grid=(rows // tr,), in_specs=in_specs,
                         out_specs=[row] * len(out_shape), compiler_params=_cp(("parallel",)))(*ins)
    if has_br:
        return res[0], list(res[1:])
    return h, list(res)


def _rms_bwd(xv, w, dyv):
    r = lax.rsqrt(jnp.mean(xv * xv, axis=-1, keepdims=True) + RMS_EPS)
    wdy = dyv * w
    dx = r * wdy - xv * (r * r * r) * jnp.mean(xv * wdy, axis=-1, keepdims=True)
    return dx, jnp.sum(dyv * xv * r, axis=0, keepdims=True)


def _norm_bwd(x, w, dy, add, out_dtype, name, comm=None, then=None, split_rows=None):
    rows, d = x.shape
    tr = _row_tile(rows, d)
    has_add = add is not None
    n_in = 3 + has_add + (2 if then is not None else 0)
    if split_rows is not None:
        last, tail = _real_rows(rows, tr, split_rows)

    def body(*refs):
        x_ref, w_ref, dy_ref = refs[:3]
        outs = refs[n_in:]
        dx, dw = _rms_bwd(x_ref[...], w_ref[...], dy_ref[...].astype(F32))
        if has_add:
            dx = dx + refs[3][...]
        if split_rows is None:
            outs[0][...] = dx.astype(outs[0].dtype)
        else:
            i = pl.program_id(0)
            gm_ref, gx_ref = outs[0], outs[1]
            outs = outs[1:]

            @pl.when(i == 0)
            def _():
                gm_ref[...] = dx[0:N_META, :]
                gx_ref[0:tr - N_META, :] = dx[N_META:tr, :]

            @pl.when((i > 0) & (i < last))
            def _():
                gx_ref[pl.ds(pl.multiple_of(i * tr - N_META, 8), tr), :] = dx

            @pl.when(i == last)
            def _():
                gx_ref[split_rows - tail:split_rows, :] = dx[0:tail, :]
        first = pl.program_id(0) == 0

        @pl.when(first)
        def _():
            outs[1][...] = jnp.zeros_like(outs[1])

        outs[1][...] += dw
        if then is not None:
            dx2, dw2 = _rms_bwd(refs[n_in - 2][...], refs[n_in - 1][...], dx)
            outs[2][...] = dx2.astype(outs[2].dtype)

            @pl.when(first)
            def _():
                outs[3][...] = jnp.zeros_like(outs[3])

            outs[3][...] += dw2

    row = pl.BlockSpec((tr, d), lambda i: (i, 0))
    vec = pl.BlockSpec((1, d), lambda i: (0, 0))
    ins = [x, w, dy] + ([add] if has_add else []) + (list(then) if then is not None else [])
    in_specs = [row, vec, row] + ([row] if has_add else []) + ([row, vec] if then is not None else [])
    out_shape = [S((rows, d), out_dtype), S((1, d), F32)] + ([S((rows, d), BF16), S((1, d), F32)] if then is not None else [])
    out_specs = [row, vec] * (len(out_shape) // 2)
    if split_rows is not None:
        out_shape = [S((N_META, d), F32), S((split_rows, d), F32)] + out_shape[1:]
        out_specs = [pl.BlockSpec((N_META, d), lambda i: (0, 0)), pl.BlockSpec((split_rows, d), lambda i: (0, 0))] + out_specs[1:]
    return _call(body, name, out_shape, (rows // tr,), in_specs, out_specs, ("arbitrary",), ins, comm=comm)


def _real_rows(rows, tr, n_x):
    last = (N_META + n_x - 1) // tr
    tail = N_META + n_x - last * tr
    assert last == rows // tr - 1 and N_META % 8 == 0 and tail % 8 == 0 and N_META < tr
    return last, tail


def _final_loss(h, br, w_post, target, name):
    rows, d = h.shape
    tr = _row_tile(rows, d)
    n_x = target.shape[0]
    last, tail = _real_rows(rows, tr, n_x)

    def body(h_ref, br_ref, w_ref, t_ref, dh_ref, loss_ref, dbr_ref, dw_ref, tbuf):
        i = pl.program_id(0)

        @pl.when(i == 0)
        def _():
            tbuf[0:N_META, :] = jnp.zeros((N_META, d), F32)
            tbuf[N_META:tr, :] = t_ref[0:tr - N_META, :]

        @pl.when((i > 0) & (i < last))
        def _():
            tbuf[...] = t_ref[pl.ds(pl.multiple_of(i * tr - N_META, 8), tr), :]

        @pl.when(i == last)
        def _():
            tbuf[0:tail, :] = t_ref[n_x - tail:n_x, :]
            if tail < tr:
                tbuf[tail:tr, :] = jnp.zeros((tr - tail, d), F32)

        brv, wv = br_ref[...], w_ref[...]
        y = h_ref[...] + _rms(brv, wv)
        r = i * tr + lax.broadcasted_iota(jnp.int32, (tr, 1), 0)
        real = (r >= N_META) & (r < N_META + SEQ)
        diff = jnp.where(real, y - tbuf[...], 0.0)
        dh = diff * (1.0 / d)
        dh_ref[...] = dh
        dbr, dw = _rms_bwd(brv, wv, dh)
        dbr_ref[...] = dbr.astype(dbr_ref.dtype)

        @pl.when(i == 0)
        def _():
            loss_ref[...] = jnp.zeros_like(loss_ref)
            dw_ref[...] = jnp.zeros_like(dw_ref)

        loss_ref[...] += jnp.sum(diff * diff) * (0.5 / d)
        dw_ref[...] += dw

    row = pl.BlockSpec((tr, d), lambda i: (i, 0))
    vec = pl.BlockSpec((1, d), lambda i: (0, 0))
    return pl.pallas_call(body, name=name,
                          out_shape=[S((rows, d), F32), S((1, LANE), F32), S((rows, d), BF16), S((1, d), F32)],
                          grid=(rows // tr,), in_specs=[row, row, vec, pl.BlockSpec((n_x, d), lambda i: (0, 0))],
                          out_specs=[row, pl.BlockSpec((1, LANE), lambda i: (0, 0)), row, vec],
                          scratch_shapes=[pltpu.VMEM((tr, d), F32)],
                          compiler_params=_cp(("arbitrary",)))(h, br, w_post, target)


def _gatenorm_fwd(y, zx, w, name, comm=None):
    rows, d = y.shape
    tr = _row_tile(rows, d)

    def body(y_ref, z_ref, w_ref, o_ref):
        z = z_ref[...]
        o_ref[...] = _rms(y_ref[...] * z * _sigmoid(z), w_ref[...]).astype(o_ref.dtype)

    row = pl.BlockSpec((tr, d), lambda i: (i, 0))
    return _call(body, name, S((rows, d), BF16), (rows // tr,), [row, row, pl.BlockSpec((1, d), lambda i: (0, 0))],
                 row, ("parallel",), (y, zx, w), comm=comm)


def _gatenorm_bwd(y, zx, w, dyn, name, comm=None):
    rows, d = y.shape
    tr = _row_tile(rows, d)

    def body(y_ref, z_ref, w_ref, dyn_ref, dy_ref, dz_ref, dw_ref):
        yv, z = y_ref[...], z_ref[...]
        sg = _sigmoid(z)
        sz = z * sg
        g = yv * sz
        r = lax.rsqrt(jnp.mean(g * g, axis=-1, keepdims=True) + RMS_EPS)
        dyn_v = dyn_ref[...]
        wdy = dyn_v * w_ref[...]
        dg = r * wdy - g * (r * r * r) * jnp.mean(g * wdy, axis=-1, keepdims=True)
        dy_ref[...] = dg * sz
        dz_ref[...] = (dg * yv * sg * (1.0 + z * (1.0 - sg))).astype(dz_ref.dtype)

        @pl.when(pl.program_id(0) == 0)
        def _():
            dw_ref[...] = jnp.zeros_like(dw_ref)

        dw_ref[...] += jnp.sum(dyn_v * g * r, axis=0, keepdims=True)

    row = pl.BlockSpec((tr, d), lambda i: (i, 0))
    vec = pl.BlockSpec((1, d), lambda i: (0, 0))
    return _call(body, name, [S((rows, d), F32), S((rows, zx.shape[1]), BF16), S((1, d), F32)], (rows // tr,),
                 [row, row, vec, row], [row, row, vec], ("arbitrary",), (y, zx, w, dyn), comm=comm)


def _shift_down(x, s, rows_iota):
    if s == 0:
        return x
    return jnp.where(rows_iota >= s, pltpu.roll(x, s, 0), 0.0)


def _shift_up(x, s, rows_iota):
    if s == 0:
        return x
    rows = x.shape[0]
    return jnp.where(rows_iota < rows - s, pltpu.roll(x, rows - s, 0), 0.0)


def _r16(v):
    return v.astype(BF16).astype(F32)


def _conv_taps(x, taps, rows_iota):
    x = _r16(x)
    return [_shift_down(x, taps - 1 - k, rows_iota) for k in range(taps)]


def _conv(x, w_ref, b_ref, taps, rows_iota, shifted=None):
    shifted = _conv_taps(x, taps, rows_iota) if shifted is None else shifted
    acc = jnp.zeros_like(shifted[0])
    for k in range(taps):
        acc = acc + _r16(w_ref[k:k + 1, :]) * shifted[k]
    return acc + b_ref[...]


def _conv_bwd(shifted, du, w_ref, dw_ref, db_ref, taps, rows_iota):
    db_ref[...] = jnp.sum(du, axis=0, keepdims=True)
    du = _r16(du)
    dx = jnp.zeros_like(du)
    for k in range(taps):
        dx = dx + _r16(w_ref[k:k + 1, :]) * _shift_up(du, taps - 1 - k, rows_iota)
        dw_ref[k:k + 1, :] = jnp.sum(du * shifted[k], axis=0, keepdims=True)
    return dx


def _conv_silu_fwd(zx, w, b, name, comm=None):
    rows = zx.shape[0]
    cb = 512
    off = D_INNER // cb

    def body(x_ref, w_ref, b_ref, o_ref):
        it = lax.broadcasted_iota(jnp.int32, (rows, 1), 0)
        u = _conv(x_ref[...], w_ref, b_ref, SSM_CONV, it)
        o_ref[...] = u * _sigmoid(u)

    return _call(
        body, name, S((rows, D_XBC), F32), (D_XBC // cb,),
        [pl.BlockSpec((rows, cb), lambda j: (0, off + j)), pl.BlockSpec((SSM_CONV, cb), lambda j: (0, j)),
         pl.BlockSpec((1, cb), lambda j: (0, j))],
        pl.BlockSpec((rows, cb), lambda j: (0, j)), ("parallel",), (zx, w, b), comm=comm)


def _conv_silu_bwd(zx, dxs, dbm, dcm, w, b, dzx, name, comm=None):
    rows = zx.shape[0]
    cb = 256
    off = D_INNER // cb
    nx, nbc = D_INNER // cb, D_BC // cb

    def body(x_ref, dx_in, db_in, dc_in, w_ref, b_ref, dzx_in, dx_ref, dw_ref, db_ref, dbuf):
        del dzx_in
        j = pl.program_id(0)
        for cond, src in ((j < nx, dx_in), ((j >= nx) & (j < nx + nbc), db_in), (j >= nx + nbc, dc_in)):
            @pl.when(cond)
            def _(src=src):
                dbuf[...] = src[...]
        it = lax.broadcasted_iota(jnp.int32, (rows, 1), 0)
        xs = _conv_taps(x_ref[...], SSM_CONV, it)
        u = _conv(None, w_ref, b_ref, SSM_CONV, it, xs)
        sg = _sigmoid(u)
        du = dbuf[...] * sg * (1.0 + u * (1.0 - sg))
        dx_ref[...] = _conv_bwd(xs, du, w_ref, dw_ref, db_ref, SSM_CONV, it).astype(dx_ref.dtype)

    def part(first, count):
        return pl.BlockSpec((rows, cb), lambda j: (0, jnp.clip(j - first, 0, count - 1)))

    col = pl.BlockSpec((rows, cb), lambda j: (0, j))
    wsp = pl.BlockSpec((SSM_CONV, cb), lambda j: (0, j))
    bsp = pl.BlockSpec((1, cb), lambda j: (0, j))
    xbc_cols = pl.BlockSpec((rows, cb), lambda j: (0, off + j))
    return _call(
        body, name, [S(dzx.shape, dzx.dtype), S((SSM_CONV, D_XBC), F32), S((1, D_XBC), F32)], (D_XBC // cb,),
        [xbc_cols, part(0, nx), part(nx, nbc), part(nx + nbc, nbc), wsp, bsp, _ANY],
        [xbc_cols, wsp, bsp], ("arbitrary",), (zx, dxs, dbm, dcm, w, b, dzx), scratch=[pltpu.VMEM((rows, cb), F32)],
        comm=comm, aliases={6: 0})


def _ffn_act_fwd(u, w, b, name, comm=None):
    rows = u.shape[0]
    cb = 256
    nb = D_FF // cb

    def body(g_ref, v_ref, wg_ref, wv_ref, bg_ref, bv_ref, o_ref):
        it = lax.broadcasted_iota(jnp.int32, (rows, 1), 0)
        g = _conv(g_ref[...], wg_ref, bg_ref, FFN_CONV, it)
        v = _conv(v_ref[...], wv_ref, bv_ref, FFN_CONV, it)
        o_ref[...] = (g * _sigmoid(g) * v).astype(o_ref.dtype)

    def sp(r, shift):
        return pl.BlockSpec((r, cb), lambda j: (0, shift + j))

    return _call(
        body, name, S((rows, D_FF), BF16), (nb,),
        [sp(rows, 0), sp(rows, nb), sp(FFN_CONV, 0), sp(FFN_CONV, nb), sp(1, 0), sp(1, nb)],
        sp(rows, 0), ("parallel",), (u, u, w, w, b, b), comm=comm)


def _ffn_act_bwd(u, dact, w, b, name, comm=None):
    rows = u.shape[0]
    cb = 256
    nb = D_FF // cb

    def body(g_ref, v_ref, d_ref, wg_ref, wv_ref, bg_ref, bv_ref, du_ref, dw_ref, db_ref):
        it = lax.broadcasted_iota(jnp.int32, (rows, 1), 0)
        xg, xv = _conv_taps(g_ref[...], FFN_CONV, it), _conv_taps(v_ref[...], FFN_CONV, it)
        g = _conv(None, wg_ref, bg_ref, FFN_CONV, it, xg)
        v = _conv(None, wv_ref, bv_ref, FFN_CONV, it, xv)
        sg = _sigmoid(g)
        d = d_ref[...]
        dgate = d * v * sg * (1.0 + g * (1.0 - sg))
        dval = d * g * sg
        du_ref[0] = _conv_bwd(xg, dgate, wg_ref, dw_ref.at[0], db_ref.at[0], FFN_CONV, it).astype(du_ref.dtype)
        du_ref[1] = _conv_bwd(xv, dval, wv_ref, dw_ref.at[1], db_ref.at[1], FFN_CONV, it).astype(du_ref.dtype)

    def sp(r, shift):
        return pl.BlockSpec((r, cb), lambda j: (0, shift + j))

    def both(r):
        return pl.BlockSpec((2, r, cb), lambda j: (0, 0, j))

    return _call(
        body, name, [S((2, rows, D_FF), BF16), S((2, FFN_CONV, D_FF), F32), S((2, 1, D_FF), F32)], (nb,),
        [sp(rows, 0), sp(rows, nb), sp(rows, 0), sp(FFN_CONV, 0), sp(FFN_CONV, nb), sp(1, 0), sp(1, nb)],
        [both(rows), both(FFN_CONV), both(1)], ("parallel",), (u, u, dact, w, w, b, b), comm=comm)


def _ssd_consts(dtp_ref, bias_ref, alog_ref, hg):
    lane = lax.broadcasted_iota(jnp.int32, (1, LANE), 1)
    pre = dtp_ref[...] + bias_ref[...]
    dt = _softplus(pre)
    a_row = jnp.where(lane < hg, -jnp.exp(alog_ref[...]), 0.0)
    ri = lax.broadcasted_iota(jnp.int32, (T, T), 0)
    ci = lax.broadcasted_iota(jnp.int32, (T, T), 1)
    cs = _dot_hi((ri >= ci).astype(F32), dt * a_row)
    return pre, dt, a_row, cs, ri, ci, lane


def _head_rows(src, hg):
    return jnp.concatenate([jnp.broadcast_to(src[k:k + 1, :], (HEAD_P, src.shape[1])) for k in range(hg)], axis=0)


def _ssd_fwd(xbc, zx, bias, alog, dsk, name, comm=None):
    rows = xbc.shape[0]
    nc = rows // T
    hg = SSM_HEADS // SSM_GROUPS
    gw = hg * HEAD_P
    xoff, boff, coff = 0, D_INNER // D_STATE, (D_INNER + D_BC) // D_STATE
    dtoff = (D_INNER + D_XBC) // LANE

    def body(x_ref, b_ref, c_ref, dtp_ref, bias_ref, alog_ref, dsk_ref, y_ref, hst_ref, hs):
        c = pl.program_id(1)

        @pl.when(c == 0)
        def _():
            hs[...] = jnp.zeros_like(hs)

        _, dt, _, cs, ri, ci, _ = _ssd_consts(dtp_ref, bias_ref, alog_ref, hg)
        cst, dtt = cs.T, dt.T
        xt = x_ref[...].T
        bb, cbf = b_ref[...].astype(BF16), c_ref[...].astype(BF16)
        gt = _dot(bb, cbf, _NT)
        causal_t = ci >= ri
        dskv = dsk_ref[...]
        hall = hs[...]
        hst_ref[0, 0] = hall
        cs8 = cst[0:8, :]
        cl8 = cs8[:, T - 1:T]
        xdt = xt * _head_rows(dtt, hg)
        yo = _head_rows(jnp.exp(cs8), hg) * _dot(hall.astype(BF16), cbf, _NT)
        st = _dot((xdt * _head_rows(jnp.exp(cl8 - cs8), hg)).astype(BF16), bb)
        hs[...] = _head_rows(jnp.exp(cl8), hg) * hall + st
        yds = []
        for k in range(hg):
            sl = slice(k * HEAD_P, (k + 1) * HEAD_P)
            lt = jnp.exp(jnp.where(causal_t, cst[k:k + 1, :] - cs[:, k:k + 1], NEG))
            yds.append(_dot(xdt[sl, :].astype(BF16), (gt * lt).astype(BF16)))
        dsk_r = jnp.concatenate([jnp.broadcast_to(dskv[:, k:k + 1], (HEAD_P, 1)) for k in range(hg)], axis=0)
        y_ref[...] = (jnp.concatenate(yds, axis=0) + yo + dsk_r * xt).T

    vec = pl.BlockSpec((1, LANE), lambda g, c: (0, g))
    return _call(
        body, name, [S((rows, D_INNER), F32), S((nc, SSM_GROUPS, gw, D_STATE), F32)], (SSM_GROUPS, nc),
        [pl.BlockSpec((T, gw), lambda g, c: (c, xoff + g)),
         pl.BlockSpec((T, D_STATE), lambda g, c: (c, boff + g)),
         pl.BlockSpec((T, D_STATE), lambda g, c: (c, coff + g)),
         pl.BlockSpec((T, LANE), lambda g, c: (c, dtoff + g)), vec, vec, vec],
        [pl.BlockSpec((T, gw), lambda g, c: (c, g)), pl.BlockSpec((1, 1, gw, D_STATE), lambda g, c: (c, g, 0, 0))],
        ("parallel", "arbitrary"), (xbc, xbc, xbc, zx, bias, alog, dsk),
        scratch=[pltpu.VMEM((gw, D_STATE), F32)], comm=comm)


def _ssd_bwd(xbc, zx, bias, alog, dsk, dy, hst, dzx, name, comm=None):
    rows = xbc.shape[0]
    nc = rows // T
    hg = SSM_HEADS // SSM_GROUPS
    gw = hg * HEAD_P
    boff, coff = D_INNER // D_STATE, (D_INNER + D_BC) // D_STATE
    dtoff = (D_INNER + D_XBC) // LANE

    def body(x_ref, b_ref, c_ref, dtp_ref, bias_ref, alog_ref, dsk_ref, dy_ref, hst_ref, dzx_in,
             dx_ref, db_ref, dc_ref, ddtp_ref, dalog_ref, ddsk_ref, dbias_ref, dhs):
        del dzx_in
        step = pl.program_id(1)

        @pl.when(step == 0)
        def _():
            dhs[...] = jnp.zeros_like(dhs)
            dalog_ref[...] = jnp.zeros_like(dalog_ref)
            ddsk_ref[...] = jnp.zeros_like(ddsk_ref)
            dbias_ref[...] = jnp.zeros_like(dbias_ref)

        pre, dt, a_row, cs, ri, ci, lane = _ssd_consts(dtp_ref, bias_ref, alog_ref, hg)
        cst, dtt = cs.T, dt.T
        xt, dyt = x_ref[...].T, dy_ref[...].T
        bb, cbf = b_ref[...].astype(BF16), c_ref[...].astype(BF16)
        gt = _dot(bb, cbf, _NT)
        causal_t = ci >= ri
        dskv = dsk_ref[...]
        hall, dhall = hst_ref[0, 0], dhs[...]
        head_row = lax.broadcasted_iota(jnp.int32, (T, 1), 0)
        last_l = lax.broadcasted_iota(jnp.int32, (1, T), 1) == T - 1
        cs8, dt8 = cst[0:8, :], dtt[0:8, :]
        cl8 = cs8[:, T - 1:T]
        e8, wdec8 = jnp.exp(cs8), jnp.exp(cl8 - cs8)
        w8 = wdec8 * dt8
        dt_r, e_r, w_r, ecl_r = _head_rows(dt8, hg), _head_rows(e8, hg), _head_rows(w8, hg), _head_rows(jnp.exp(cl8), hg)
        dsk_r = jnp.concatenate([jnp.broadcast_to(dskv[:, k:k + 1], (HEAD_P, 1)) for k in range(hg)], axis=0)
        hb, dhb = hall.astype(BF16), dhall.astype(BF16)
        xdt = xt * dt_r
        dye = (dyt * e_r).astype(BF16)
        rt = _dot(dhb, bb, _NT)
        yo = e_r * _dot(hb, cbf, _NT)
        dhs[...] = ecl_r * dhall + _dot(dye, cbf)
        dc_acc = _dot(dye, hb, _TN)
        db_acc = _dot((xt * w_r).astype(BF16), dhb, _TN)
        rtx, dyyo, hdh, dyx = rt * xt, dyt * yo, dhall * hall, dyt * xt
        dgt = jnp.zeros((T, T), F32)
        ddt_rows = jnp.zeros((T, T), F32)
        dcs_rows = jnp.zeros((T, T), F32)
        qrow_cols = jnp.zeros((T, LANE), F32)
        ddsk_acc = jnp.zeros((1, LANE), F32)
        dxdts = []
        for k in range(hg):
            sl = slice(k * HEAD_P, (k + 1) * HEAD_P)
            lt = jnp.exp(jnp.where(causal_t, cst[k:k + 1, :] - cs[:, k:k + 1], NEG))
            mpt = gt * lt
            dyb = dyt[sl, :].astype(BF16)
            dxdt = _dot(dyb, mpt.astype(BF16), _NT)
            dmt = _dot(xdt[sl, :].astype(BF16), dyb, _TN)
            dgt = dgt + dmt * lt
            q = dmt * mpt
            q_rows = jnp.sum(q, axis=1, keepdims=True)
            q_cols = jnp.sum(q, axis=0, keepdims=True)
            dxdts.append(dxdt)
            xz = jnp.sum(xt[sl, :] * dxdt, axis=0, keepdims=True)
            dw = jnp.sum(rtx[sl, :], axis=0, keepdims=True)
            wk, wdeck = w8[k:k + 1, :], wdec8[k:k + 1, :]
            dcl = jnp.exp(cl8[k:k + 1, :]) * jnp.sum(hdh[sl, :]) + jnp.sum(dw * wk)
            dcs_r = jnp.sum(dyyo[sl, :], axis=0, keepdims=True) + q_cols - dw * wk + jnp.where(last_l, dcl, 0.0)
            onehot = (lane == k).astype(F32)
            ddt_rows = ddt_rows + jnp.where(head_row == k, xz + dw * wdeck, 0.0)
            dcs_rows = dcs_rows + jnp.where(head_row == k, dcs_r, 0.0)
            qrow_cols = qrow_cols + q_rows * onehot
            ddsk_acc = ddsk_acc + jnp.sum(dyx[sl, :]) * onehot
        dx_ref[...] = (dt_r * jnp.concatenate(dxdts, axis=0) + dsk_r * dyt + rt * w_r).T
        dc_ref[...] = _dot(dgt.T.astype(BF16), bb) + dc_acc
        db_ref[...] = _dot(dgt.astype(BF16), cbf) + db_acc
        da = _dot_hi((ci >= ri).astype(F32), dcs_rows.T - qrow_cols)
        ddtp = (ddt_rows.T + da * a_row) * _sigmoid(pre)
        ddtp = jnp.where(lane < hg, ddtp, 0.0)
        ddtp_ref[...] = ddtp.astype(ddtp_ref.dtype)
        dbias_ref[...] += jnp.sum(ddtp, axis=0, keepdims=True)
        dalog_ref[...] += jnp.sum(da * dt, axis=0, keepdims=True) * a_row
        ddsk_ref[...] += ddsk_acc

    def rc(c):
        return nc - 1 - c

    vec = pl.BlockSpec((1, LANE), lambda g, c: (0, g))
    xsp = pl.BlockSpec((T, gw), lambda g, c: (rc(c), g))
    return _call(
        body, name,
        [S((rows, D_INNER), F32), S((rows, D_BC), F32), S((rows, D_BC), F32),
         S(dzx.shape, dzx.dtype), S((1, SSM_GROUPS * LANE), F32),
         S((1, SSM_GROUPS * LANE), F32), S((1, SSM_GROUPS * LANE), F32)],
        (SSM_GROUPS, nc),
        [xsp,
         pl.BlockSpec((T, D_STATE), lambda g, c: (rc(c), boff + g)),
         pl.BlockSpec((T, D_STATE), lambda g, c: (rc(c), coff + g)),
         pl.BlockSpec((T, LANE), lambda g, c: (rc(c), dtoff + g)), vec, vec, vec,
         xsp, pl.BlockSpec((1, 1, gw, D_STATE), lambda g, c: (rc(c), g, 0, 0)), _ANY],
        [xsp,
         pl.BlockSpec((T, D_STATE), lambda g, c: (rc(c), g)),
         pl.BlockSpec((T, D_STATE), lambda g, c: (rc(c), g)),
         pl.BlockSpec((T, LANE), lambda g, c: (rc(c), dtoff + g)), vec, vec, vec],
        ("parallel", "arbitrary"), (xbc, xbc, xbc, zx, bias, alog, dsk, dy, hst, dzx),
        scratch=[pltpu.VMEM((gw, D_STATE), F32)], comm=comm, aliases={9: 3})


def _attn_tiles(kv_ref, j):
    prev = jnp.maximum(j - 1, 0)
    meta = kv_ref[0:T, :]
    prv = kv_ref[pl.ds(pl.multiple_of(prev * T, T), T), :]
    cur = kv_ref[pl.ds(pl.multiple_of(j * T, T), T), :]
    return jnp.concatenate([meta, prv, cur], axis=0)


def _attn_mask(j):
    r = j * T + lax.broadcasted_iota(jnp.int32, (3 * T, T), 1)
    row = lax.broadcasted_iota(jnp.int32, (3 * T, T), 0)
    t0, t1 = row < T, row < 2 * T
    s = jnp.where(t0, row, (j - 2) * T + row)
    ok = (s <= r) & ((s < N_META) | (s > r - WINDOW))
    use = (t0 & (j >= 2) & (row < N_META)) | (jnp.logical_not(t0) & t1 & (j >= 1)) | jnp.logical_not(t1)
    return ok & use


def _attn_fwd(q, kv, sinks, name, comm=None):
    rows = q.shape[0]
    scale = 1.0 / math.sqrt(ATTN_DH)
    qpk = N_Q_HEADS // N_KV_HEADS

    def body(q_ref, kv_ref, s_ref, o_ref, lse_ref):
        j = pl.program_id(0)
        kv3 = _attn_tiles(kv_ref, j).astype(BF16)
        mask = _attn_mask(j)
        qv = (q_ref[...] * scale).astype(BF16)
        sk = s_ref[...]
        lses = []
        for kh in range(N_KV_HEADS):
            k3 = kv3[:, kh * ATTN_DH:(kh + 1) * ATTN_DH]
            v3 = kv3[:, D_KV + kh * ATTN_DH:D_KV + (kh + 1) * ATTN_DH]
            for g in range(qpk):
                h = kh * qpk + g
                sink = sk[:, h:h + 1]
                sc = jnp.where(mask, _dot(k3, qv[:, h * ATTN_DH:(h + 1) * ATTN_DH], _NT), NEG)
                m = jnp.maximum(jnp.max(sc, axis=0, keepdims=True), sink)
                p = jnp.exp(sc - m)
                den = jnp.sum(p, axis=0, keepdims=True) + jnp.exp(sink - m)
                p = p * (1.0 / den)
                lses.append(m + jnp.log(den))
                o_ref[:, h * ATTN_DH:(h + 1) * ATTN_DH] = _dot(p.astype(BF16), v3, _TN).astype(o_ref.dtype)
        lse_ref[...] = jnp.concatenate(lses, axis=0)

    return _call(
        body, name, [S((rows, D_MODEL), BF16), S((N_Q_HEADS, rows), F32)], (rows // T,),
        [pl.BlockSpec((T, D_MODEL), lambda j: (j, 0)), pl.BlockSpec((rows, 2 * D_KV), lambda j: (0, 0)),
         pl.BlockSpec((1, N_Q_HEADS), lambda j: (0, 0))],
        [pl.BlockSpec((T, D_MODEL), lambda j: (j, 0)), pl.BlockSpec((N_Q_HEADS, T), lambda j: (0, j))],
        ("parallel",), (q, kv, sinks), comm=comm)


def _attn_bwd(q, kv, sinks, do, lse, name, comm=None):
    rows = q.shape[0]
    scale = 1.0 / math.sqrt(ATTN_DH)
    qpk = N_Q_HEADS // N_KV_HEADS

    def body(q_ref, kv_ref, s_ref, do_ref, lse_ref, dq_ref, dkv_ref, ds_ref):
        j = pl.program_id(0)

        @pl.when(j == 0)
        def _():
            dkv_ref[...] = jnp.zeros_like(dkv_ref)
            ds_ref[...] = jnp.zeros_like(ds_ref)

        kv3 = _attn_tiles(kv_ref, j).astype(BF16)
        mask = _attn_mask(j)
        qv = (q_ref[...] * scale).astype(BF16)
        dov = do_ref[...].astype(BF16)
        sk = s_ref[...]
        lsev = lse_ref[...]
        lane = lax.broadcasted_iota(jnp.int32, (1, LANE), 1)
        ds_acc = jnp.zeros((1, LANE), F32)
        prev = jnp.maximum(j - 1, 0)
        mask4 = jnp.concatenate([mask] * qpk, axis=1)
        dqts = []
        for kh in range(N_KV_HEADS):
            ksl = slice(kh * ATTN_DH, (kh + 1) * ATTN_DH)
            vsl = slice(D_KV + kh * ATTN_DH, D_KV + (kh + 1) * ATTN_DH)
            k3, v3 = kv3[:, ksl], kv3[:, vsl]
            heads = [kh * qpk + g for g in range(qpk)]
            q4 = jnp.concatenate([qv[:, h * ATTN_DH:(h + 1) * ATTN_DH] for h in heads], axis=0)
            do4 = jnp.concatenate([dov[:, h * ATTN_DH:(h + 1) * ATTN_DH] for h in heads], axis=0)
            lse4 = jnp.concatenate([lsev[h:h + 1, :] for h in heads], axis=1)
            sink4 = jnp.concatenate([jnp.broadcast_to(sk[:, h:h + 1], (1, T)) for h in heads], axis=1)
            p = jnp.exp(jnp.where(mask4, _dot(k3, q4, _NT), NEG) - lse4)
            ps = jnp.exp(sink4 - lse4)
            dp = _dot(v3, do4, _NT)
            delta = jnp.sum(p * dp, axis=0, keepdims=True)
            dsc = (p * (dp - delta)).astype(BF16)
            dq4 = _dot(k3.T, dsc) * scale
            dk3 = _dot(dsc, q4)
            dv3 = _dot(p.astype(BF16), do4)
            psd = ps * delta
            for g, h in enumerate(heads):
                dqts.append(dq4[:, g * T:(g + 1) * T])
                ds_acc = ds_acc - jnp.sum(psd[:, g * T:(g + 1) * T]) * (lane == h).astype(F32)
            for t, start in enumerate((0, pl.multiple_of(prev * T, T), pl.multiple_of(j * T, T))):
                rsl = pl.ds(start, T)
                dkv_ref[rsl, ksl] += dk3[t * T:(t + 1) * T, :]
                dkv_ref[rsl, vsl] += dv3[t * T:(t + 1) * T, :]
        ds_ref[...] += ds_acc
        dq_ref[...] = jnp.concatenate(dqts, axis=0).T.astype(dq_ref.dtype)

    blk = pl.BlockSpec((T, D_MODEL), lambda j: (j, 0))
    full = pl.BlockSpec((rows, 2 * D_KV), lambda j: (0, 0))
    return _call(
        body, name, [S((rows, D_MODEL), BF16), S((rows, 2 * D_KV), F32), S((1, LANE), F32)], (rows // T,),
        [blk, full, pl.BlockSpec((1, N_Q_HEADS), lambda j: (0, 0)), blk, pl.BlockSpec((N_Q_HEADS, T), lambda j: (0, j))],
        [blk, full, pl.BlockSpec((1, LANE), lambda j: (0, 0))], ("arbitrary",), (q, kv, sinks, do, lse), comm=comm)


BLOCK_BYTES = 1 << 20


def _div_tile(rows, cols):
    cap = max(16, BLOCK_BYTES // (4 * cols))
    best = None
    for t in range(16, min(rows, cap) + 1, 16):
        if rows % t == 0:
            best = t
    return best if best is not None else rows


def _adamw(parts, w, m, v, name, comm=None):
    layers, rows, cols = w.shape
    n = parts[0].shape[0]
    tr = _div_tile(rows, cols)
    tc = _pick(cols, 256) if tr == rows and rows * cols * 4 > 2 * BLOCK_BYTES else cols
    c1 = 1.0 / (1.0 - B1 ** STEP)
    c2 = 1.0 / (1.0 - B2 ** STEP)

    def body(*refs):
        p_refs = refs[:layers]
        w_ref, m_ref, v_ref, g_ref, d_ref, nm_ref, nv_ref = refs[layers:]
        layer = pl.program_id(0)
        for l in range(layers):
            @pl.when(layer == l)
            def _(p_ref=p_refs[l]):
                g = p_ref[0].astype(F32)
                for i in range(1, n):
                    g = g + p_ref[i].astype(F32)
                nm = B1 * m_ref[...] + (1.0 - B1) * g
                nv = B2 * v_ref[...] + (1.0 - B2) * (g * g)
                g_ref[...] = g
                nm_ref[...] = nm
                nv_ref[...] = nv
                d_ref[...] = -LR * ((nm * c1) / (jnp.sqrt(nv * c2) + EPS) + WD * w_ref[...])

    def part_spec(l):
        return pl.BlockSpec((n, tr, tc), lambda k, i, j: (0, jnp.where(k == l, i, 0), jnp.where(k == l, j, 0)))

    row = pl.BlockSpec((None, tr, tc), lambda k, i, j: (k, i, j))
    return _call(body, name, [S((layers, rows, cols), F32)] * 4, (layers, rows // tr, cols // tc),
                 [part_spec(l) for l in range(layers)] + [row, row, row], [row] * 4,
                 ("parallel", "parallel", "parallel"), (*parts, w, m, v), comm=comm)


def _sum_parts(parts, name):
    n, rows, cols = parts[0].shape
    nb = len(parts)
    tr = _div_tile(rows, cols)

    def body(*refs):
        o_ref = refs[nb]
        blk = pl.program_id(0)
        for l in range(nb):
            @pl.when(blk == l)
            def _(p_ref=refs[l]):
                g = p_ref[0].astype(F32)
                for i in range(1, n):
                    g = g + p_ref[i].astype(F32)
                o_ref[...] = g

    def part_spec(l):
        return pl.BlockSpec((n, tr, cols), lambda k, i: (0, jnp.where(k == l, i, 0), 0))

    per = rows // tr
    return pl.pallas_call(body, name=name, out_shape=S((nb * rows, cols), F32), grid=(nb, per),
                          in_specs=[part_spec(l) for l in range(nb)],
                          out_specs=pl.BlockSpec((tr, cols), lambda k, i: (k * per + i, 0)),
                          compiler_params=_cp(("parallel", "parallel")))(*parts)


def _col_segments(ws, runs):
    segs = []
    for glo, mlo, n in runs:
        while n > 0:
            d, off = divmod(glo, ws)
            take = min(n, ws - off)
            segs.append((d, off, mlo, take))
            glo, mlo, n = glo + take, mlo + take, n - take
    return segs


def _assemble_cols(gs, width, segs, name):
    _, rows, ws = gs[0].shape
    nb = len(gs)
    rb = _div_tile(rows, width // 2)
    per = rows // rb

    def body(*refs):
        o_ref = refs[nb]
        piece = pl.program_id(0)
        for l in range(nb):
            @pl.when(piece == l)
            def _(g_ref=refs[l]):
                o_ref[...] = jnp.zeros_like(o_ref)
                for d, off, mlo, n in segs:
                    o_ref[:, mlo:mlo + n] = g_ref[d, :, off:off + n]

    def piece_spec(l):
        return pl.BlockSpec((N_DEV, rb, ws), lambda k, i: (0, jnp.where(k == l, i, 0), 0))

    return pl.pallas_call(
        body, name=name, out_shape=S((nb * rows, width), gs[0].dtype), grid=(nb, per),
        in_specs=[piece_spec(l) for l in range(nb)],
        out_specs=pl.BlockSpec((rb, width), lambda k, i: (k * per + i, 0)),
        compiler_params=_cp(("parallel", "parallel")))(*gs)


def _scatter_cols(dw, ws, segs, name):
    rows, width = dw.shape
    rb = _div_tile(rows, width)

    def body(w_ref, o_ref):
        for d, off, mlo, n in segs:
            o_ref[d, :, off:off + n] = w_ref[:, mlo:mlo + n].astype(o_ref.dtype)

    return pl.pallas_call(
        body, name=name, out_shape=S((N_DEV, rows, ws), BF16), grid=(rows // rb,),
        in_specs=[pl.BlockSpec((rb, width), lambda i: (i, 0))],
        out_specs=pl.BlockSpec((N_DEV, rb, ws), lambda i: (0, i, 0)), compiler_params=_cp(("parallel",)))(dw)


def _gather_comm(xs):
    n = len(xs)

    def setup(x_refs, out_refs, sems):
        send_sems, recv_sems, local_sems = sems
        mx, my, mc = lax.axis_index("x"), lax.axis_index("y"), lax.axis_index("c")
        me, sibling = (mx, my, mc), (mx, my, 1 - mc)
        chips = [(1 - mx, my), (mx, 1 - my), (1 - mx, 1 - my)]

        def blk(a, px, py, pc):
            return out_refs[a].at[4 * px + 2 * py + pc]

        def copy(a, k, block, to, src=None):
            return pltpu.make_async_remote_copy(
                src_ref=blk(a, *block) if src is None else src, dst_ref=blk(a, *block),
                send_sem=send_sems.at[a, k], recv_sem=recv_sems.at[a, k], device_id=to, device_id_type=_MESH)

        mine = [pltpu.make_async_copy(x_refs[a], blk(a, *me), local_sems.at[a]) for a in range(n)]
        own = []
        for a in range(n):
            own.append(copy(a, 0, me, sibling, src=x_refs[a]))
            own += [copy(a, 1 + i, me, (*chip, mc), src=x_refs[a]) for i, chip in enumerate(chips)]
        return me, sibling, chips, mc, copy, mine, own

    def first(x_refs, out_refs, sems):
        _, _, _, _, _, mine, own = setup(x_refs, out_refs, sems)
        for cp in mine + own:
            cp.start()

    def last(x_refs, out_refs, sems):
        me, sibling, chips, mc, copy, mine, own = setup(x_refs, out_refs, sems)
        passed = []
        for a in range(n):
            for i, chip in enumerate(chips):
                copy(a, 1 + i, (*chip, mc), me).wait_recv()
                passed.append(copy(a, 4 + i, (*chip, mc), sibling))
                passed[-1].start()
        for a in range(n):
            copy(a, 0, sibling, me).wait_recv()
            for i, chip in enumerate(chips):
                copy(a, 4 + i, (*chip, 1 - mc), me).wait_recv()
        for cp in own + passed:
            cp.wait_send()
        for cp in mine:
            cp.wait()

    return _Comm(list(xs), [S((N_DEV,) + x.shape, x.dtype) for x in xs],
                 [pltpu.SemaphoreType.DMA((n, 7)), pltpu.SemaphoreType.DMA((n, 7)), pltpu.SemaphoreType.DMA((n,))],
                 first, last)


def _swap_comm(gs):
    n = len(gs)

    def copies(g_refs, out_refs, sems):
        send_sems, recv_sems = sems
        mx, my, mc = lax.axis_index("x"), lax.axis_index("y"), lax.axis_index("c")
        return [pltpu.make_async_remote_copy(
            src_ref=g_refs[a].at[2 * k + 1 - mc], dst_ref=out_refs[a].at[k], send_sem=send_sems.at[a, k],
            recv_sem=recv_sems.at[a, k], device_id=(mx, my, 1 - mc), device_id_type=_MESH)
            for a in range(n) for k in range(4)]

    def first(g_refs, out_refs, sems):
        for cp in copies(g_refs, out_refs, sems):
            cp.start()

    def last(g_refs, out_refs, sems):
        for cp in copies(g_refs, out_refs, sems):
            cp.wait()

    return _Comm(list(gs), [S((4,) + g.shape[1:], g.dtype) for g in gs],
                 [pltpu.SemaphoreType.DMA((n, 4)), pltpu.SemaphoreType.DMA((n, 4))], first, last)


def _chips_comm(parts):
    n = len(parts)

    def copies(p_refs, out_refs, sems):
        send_sems, recv_sems, local_sems = sems
        mx, my, mc = lax.axis_index("x"), lax.axis_index("y"), lax.axis_index("c")
        mychip = 2 * mx + my
        chips = [(1 - mx, my), (mx, 1 - my), (1 - mx, 1 - my)]
        mine = [pltpu.make_async_copy(p_refs[a].at[mychip], out_refs[a].at[mychip], local_sems.at[a])
                for a in range(n)]
        return mine + [pltpu.make_async_remote_copy(
            src_ref=p_refs[a].at[2 * cx + cy], dst_ref=out_refs[a].at[mychip], send_sem=send_sems.at[a, i],
            recv_sem=recv_sems.at[a, i], device_id=(cx, cy, mc), device_id_type=_MESH)
            for a in range(n) for i, (cx, cy) in enumerate(chips)]

    def first(p_refs, out_refs, sems):
        for cp in copies(p_refs, out_refs, sems):
            cp.start()

    def last(p_refs, out_refs, sems):
        for cp in copies(p_refs, out_refs, sems):
            cp.wait()

    return _Comm(list(parts), [S(p.shape, p.dtype) for p in parts],
                 [pltpu.SemaphoreType.DMA((n, 3)), pltpu.SemaphoreType.DMA((n, 3)), pltpu.SemaphoreType.DMA((n,))],
                 first, last)


def _join_comms(comms):
    def split(refs, counts):
        out, p = [], 0
        for cnt in counts:
            out.append(refs[p:p + cnt])
            p += cnt
        return out

    ni = [len(c.ins) for c in comms]
    no = [len(c.out_shapes) for c in comms]
    ns = [len(c.scratch) for c in comms]

    def first(in_refs, out_refs, sems):
        for c, i, o, s in zip(comms, split(in_refs, ni), split(out_refs, no), split(sems, ns)):
            c.first(i, o, s)

    def last(in_refs, out_refs, sems):
        for c, i, o, s in zip(comms, split(in_refs, ni), split(out_refs, no), split(sems, ns)):
            c.last(i, o, s)

    return _Comm([x for c in comms for x in c.ins], [x for c in comms for x in c.out_shapes],
                 [x for c in comms for x in c.scratch], first, last)


def _add_pairs(mine, theirs, core, name):
    _, rows, cols = mine.shape
    tr = _div_tile(rows, cols // 2)

    def body(core_ref, a_ref, b_ref, o_ref):
        o_ref[...] = (a_ref[...].astype(F32) + b_ref[...].astype(F32)).astype(o_ref.dtype)

    return pl.pallas_call(
        body, name=name, out_shape=S((4, rows, cols), BF16),
        grid_spec=pltpu.PrefetchScalarGridSpec(
            num_scalar_prefetch=1, grid=(4, rows // tr),
            in_specs=[pl.BlockSpec((None, tr, cols), lambda k, i, c: (2 * k + c[0], i, 0)),
                      pl.BlockSpec((None, tr, cols), lambda k, i, c: (k, i, 0))],
            out_specs=pl.BlockSpec((None, tr, cols), lambda k, i, c: (k, i, 0))),
        compiler_params=_cp(("parallel", "parallel")))(core, mine, theirs)


def _run_comm(comm, name):
    ci, co = len(comm.ins), len(comm.out_shapes)

    def body(*refs):
        comm.first(refs[:ci], refs[ci:ci + co], refs[ci + co:])
        comm.last(refs[:ci], refs[ci:ci + co], refs[ci + co:])

    return pl.pallas_call(body, name=name, out_shape=list(comm.out_shapes), in_specs=[_HBM] * ci,
                          out_specs=[_HBM] * co, scratch_shapes=list(comm.scratch))(*comm.ins)


def _flat_rows(n_elems, mult):
    rows = -(-n_elems // LANE)
    return -(-rows // mult) * mult


def _pack(arrs, lead, mult, dtype):
    lead_shape = arrs[0].shape[:lead]
    flat = jnp.concatenate([a.astype(dtype).reshape(lead_shape + (-1,)) for a in arrs], axis=-1)
    n = flat.shape[-1]
    rows = _flat_rows(n, mult)
    flat = jnp.pad(flat, [(0, 0)] * lead + [(0, rows * LANE - n)])
    return flat.reshape(lead_shape + (rows, LANE))


def _unpack(flat, lead, shapes):
    lead_shape = flat.shape[:lead]
    flat = flat.reshape(lead_shape + (-1,))
    out, off = [], 0
    for shp in shapes:
        n = math.prod(shp)
        out.append(flat[..., off:off + n].reshape(lead_shape + tuple(shp)))
        off += n
    return out


def _split8(full, ax, n):
    shp = full.shape
    return jnp.moveaxis(full.reshape(shp[:ax] + (N_DEV, n) + shp[ax + 1:]), ax, 0)


def _join8(g, ax):
    shp = g.shape[1:]
    return jnp.moveaxis(g, 0, ax).reshape(shp[:ax] + (N_DEV * shp[ax],) + shp[ax + 1:])


def _group_lanes(v, hg):
    v = v.reshape(SSM_GROUPS, hg)
    return jnp.pad(v, ((0, 0), (0, LANE - hg))).reshape(1, SSM_GROUPS * LANE)


def _ungroup_lanes(v, hg):
    return v.reshape(SSM_GROUPS, LANE)[:, :hg].reshape(1, SSM_GROUPS * hg)


def kernel(x, meta_tokens, a_norm_pre, a_w_in, a_conv_w, a_conv_b, a_dt_bias, a_a_log, a_d_skip, a_gate_norm, a_w_out, a_norm_post, kv_norm, w_kv, b_norm_pre, b_w_q, b_sinks, b_w_o, b_norm_post, f_norm_pre, f_w_up, f_conv_w, f_conv_b, f_w_down, f_norm_post, loss_target, m_meta_tokens, m_a_norm_pre, m_a_w_in, m_a_conv_w, m_a_conv_b, m_a_dt_bias, m_a_a_log, m_a_d_skip, m_a_gate_norm, m_a_w_out, m_a_norm_post, m_kv_norm, m_w_kv, m_b_norm_pre, m_b_w_q, m_b_sinks, m_b_w_o, m_b_norm_post, m_f_norm_pre, m_f_w_up, m_f_conv_w, m_f_conv_b, m_f_w_down, m_f_norm_post, v_meta_tokens, v_a_norm_pre, v_a_w_in, v_a_conv_w, v_a_conv_b, v_a_dt_bias, v_a_a_log, v_a_d_skip, v_a_gate_norm, v_a_w_out, v_a_norm_post, v_kv_norm, v_w_kv, v_b_norm_pre, v_b_w_q, v_b_sinks, v_b_w_o, v_b_norm_post, v_f_norm_pre, v_f_w_up, v_f_conv_w, v_f_conv_b, v_f_w_down, v_f_norm_post):
    args = locals()
    wts = {n: args[n] for n in WEIGHTS}
    mom = {n: args["m_" + n] for n in WEIGHTS}
    var = {n: args["v_" + n] for n in WEIGHTS}
    mx, my, mc = lax.axis_index("x"), lax.axis_index("y"), lax.axis_index("c")
    me = 4 * mx + 2 * my + mc
    rows = _seq_rows()
    hg = SSM_HEADS // SSM_GROUPS
    d = D_MODEL

    n_main = D_INNER + D_XBC
    ws_in, ws_up = a_w_in.shape[2], f_w_up.shape[2]
    segs_in = _col_segments(ws_in, [(0, 0, n_main)] + [(n_main + hg * g, n_main + LANE * g, hg)
                                                      for g in range(SSM_GROUPS)])
    segs_up = _col_segments(ws_up, [(0, 0, 2 * D_FF)])
    def gather_of(*ws):
        return _gather_comm([w.astype(BF16) for w in ws])

    small_full, = _run_comm(_gather_comm([_pack([wts[n] for n in SMALL], 0, 8, F32)]), "gather_small")
    full = {}
    for n, g in zip(SMALL, _unpack(small_full, 1, [wts[n].shape for n in SMALL])):
        full[n] = _join8(g, SHARD_AXIS[n])
    (h0, hn0), (g_in,) = _embed_norm(full["meta_tokens"], x[0], full["a_norm_pre"], rows, "embed_norm",
                                     comm=gather_of(a_w_in[0]))
    w_in_all = _assemble_cols([g_in], n_main + SSM_GROUPS * LANE, segs_in, "asm_w_in")
    w_up, w_down = [None, None], [None, None]
    bias_g = _group_lanes(wts["a_dt_bias"], hg)
    alog_g = _group_lanes(wts["a_a_log"], hg)
    dsk_g = _group_lanes(wts["a_d_skip"], hg)
    a_conv_w, a_conv_b = full["a_conv_w"][0], full["a_conv_b"]
    f_cw, f_cb = full["f_conv_w"], wts["f_conv_b"]
    fpre, fpost = wts["f_norm_pre"], wts["f_norm_post"]


    zx, (g_out,) = _mm(hn0, w_in_all, "nn", F32, "mm_in", comm=gather_of(a_w_out[0]))
    w_out = g_out.reshape(D_INNER, d)
    xbc = _conv_silu_fwd(zx, a_conv_w, a_conv_b, "conv_a")
    (y_ssd, hst), (g_up0,) = _ssd_fwd(xbc, zx, bias_g, alog_g, dsk_g, "ssd_fwd", comm=gather_of(f_w_up[0]))
    w_up[0] = _assemble_cols([g_up0], 2 * D_FF, segs_up, "asm_w_up0")
    yn = _gatenorm_fwd(y_ssd, zx, full["a_gate_norm"], "gatenorm")
    mix_a, (g_o,) = _mm(yn, w_out, "nn", F32, "mm_out", comm=gather_of(b_w_o[0]))
    h1, (fn0,) = _resid_norm(h0, mix_a, full["a_norm_post"], [fpre[0:1]], "resid_a")

    half = d // 2
    u0, (g_dn0,) = _mm(fn0, w_up[0], "nn", F32, "mm_up0", comm=gather_of(f_w_down[0]))
    act0, (g_up1a,) = _ffn_act_fwd(u0, f_cw[0], f_cb[0:1], "ffn_act0", comm=gather_of(f_w_up[1, :half]))
    ffn0, (g_kv, g_q) = _mm(act0, g_dn0.reshape(D_FF, d), "nn", F32, "mm_down0", comm=gather_of(w_kv, b_w_q[0]))
    w_kvf, w_q, w_o = g_kv.reshape(d, 2 * D_KV), g_q.reshape(d, d), g_o.reshape(d, d)
    h2, (kvn, bn) = _resid_norm(h1, ffn0, fpost[0:1], [wts["kv_norm"].reshape(1, d), wts["b_norm_pre"]], "resid_f0")
    kv = _mm(kvn, w_kvf, "nn", F32, "mm_kv")
    q = _mm(bn, w_q, "nn", F32, "mm_q")
    (o, lse), (g_up1b,) = _attn_fwd(q, kv, wts["b_sinks"], "attn_fwd", comm=gather_of(f_w_up[1, half:]))
    w_up[1] = _assemble_cols([g_up1a, g_up1b], 2 * D_FF, segs_up, "asm_w_up1")
    mix_b = _mm(o, w_o, "nn", F32, "mm_o")
    h3, (fn1,) = _resid_norm(h2, mix_b, wts["b_norm_post"], [fpre[1:2]], "resid_b")
    u1, (g_dn1,) = _mm(fn1, w_up[1], "nn", F32, "mm_up1", comm=gather_of(f_w_down[1]))
    w_down = [g_dn0.reshape(D_FF, d), g_dn1.reshape(D_FF, d)]
    act1 = _ffn_act_fwd(u1, f_cw[1], f_cb[1:2], "ffn_act1")
    ffn1 = _mm(act1, w_down[1], "nn", F32, "mm_down1")
    dh4, loss_row, dffn1, dw_post1 = _final_loss(h3, ffn1, fpost[1:2], loss_target[0], "loss")
    loss = lax.psum(loss_row[0, 0], ("x", "y", "c"))

    grads = {}

    core = mc.astype(jnp.int32).reshape(1)

    def carried(res, comm):
        return res if comm is not None else (res, None)

    def ffn_bwd(dh_out, dffn, h_in, fn, u, act, i, then, c_dact=None, c_dwdown=None, c_dwup=None, c_dfn=None):
        dact, got_a = carried(_mm(dffn, w_down[i], "nt", F32, f"mm_dact{i}", comm=c_dact), c_dact)
        dw_down, got_b = carried(_mm(act, dffn, "tn", BF16, f"mm_dwdown{i}", comm=c_dwdown), c_dwdown)
        dw_down = dw_down.reshape(N_DEV, -1, d)
        du, dwc, dbc = _ffn_act_bwd(u, dact, f_cw[i], f_cb[i:i + 1], f"ffn_act_bwd{i}")
        dfn, (s_dn, *got_d) = _mm(du, w_up[i], "nt", F32, f"mm_dfn{i}", comm=_join_comms(
            [_swap_comm([dw_down])] + ([c_dfn] if c_dfn is not None else [])))
        sum_dn = _add_pairs(dw_down, s_dn, core, f"rs_add_dn{i}")
        dw_up, got_c = carried(_mm(fn, du, "tn", BF16, f"mm_dwup{i}", comm=c_dwup, shard_cols=ws_up), c_dwup)
        (dh_in, dw_pre, dbranch, dw_branch), (s_up,) = _norm_bwd(
            h_in, fpre[i:i + 1], dfn, dh_out, F32, f"nb_fpre{i}", comm=_swap_comm([dw_up]), then=then)
        sum_up = _add_pairs(dw_up, s_up, core, f"rs_add_up{i}")
        return dh_in, dbranch, dw_branch, dict(sum_down=sum_dn, cw=jnp.concatenate([dwc[0], dwc[1]], axis=1),
                                               cb=jnp.concatenate([dbc[0], dbc[1]], axis=1), sum_up=sum_up,
                                               pre=dw_pre), got_a, got_b, got_c, got_d

    dh3, dmix_b, grads["b_norm_post"], gf1, _, _, _, _ = ffn_bwd(dh4, dffn1, h3, fn1, u1, act1, 1,
                                                                 (mix_b, wts["b_norm_post"]))
    do = _mm(dmix_b, w_o, "nt", F32, "mm_do")
    dw_o = _mm(o, dmix_b, "tn", BF16, "mm_dwo").reshape(N_DEV, -1, d)
    half_up = gf1["sum_up"].shape[1] // 2
    (dq, dkv, dsinks), (p_up1a, s_o) = _attn_bwd(
        q, kv, wts["b_sinks"], do, lse, "attn_bwd",
        comm=_join_comms([_chips_comm([gf1["sum_up"][:, :half_up]]), _swap_comm([dw_o])]))
    sum_o = _add_pairs(dw_o, s_o, core, "rs_add_o")
    grads["b_sinks"] = dsinks[:, :N_Q_HEADS]
    dbn = _mm(dq, w_q, "nt", F32, "mm_dbn")
    dw_q = _mm(bn, dq, "tn", BF16, "mm_dwq").reshape(N_DEV, -1, d)
    dkv16 = dkv.astype(BF16)
    dkvn = _mm(dkv16, w_kvf, "nt", F32, "mm_dkvn")
    dw_kv = _mm(kvn, dkv16, "tn", BF16, "mm_dwkv").reshape(N_DEV, -1, 2 * D_KV)
    (dh2, grads["b_norm_pre"]), (s_q, s_kv) = _norm_bwd(h2, wts["b_norm_pre"], dbn, dh3, F32, "nb_bpre",
                                                        comm=_swap_comm([dw_q, dw_kv]))
    sum_q, sum_kv = _add_pairs(dw_q, s_q, core, "rs_add_q"), _add_pairs(dw_kv, s_kv, core, "rs_add_kv")
    dh2, dw_kvn, dffn0, dw_post0 = _norm_bwd(h2, wts["kv_norm"].reshape(1, d), dkvn, dh2, F32, "nb_kv",
                                             then=(ffn0, fpost[0:1]))
    grads["kv_norm"] = dw_kvn.reshape(d)
    dh1, dmix_a, grads["a_norm_post"], gf0, (p_o,), (p_q, p_kv), (p_dn1,), (p_up1b,) = ffn_bwd(
        dh2, dffn0, h1, fn0, u0, act0, 0, (mix_a, full["a_norm_post"]), c_dact=_chips_comm([sum_o]),
        c_dwdown=_chips_comm([sum_q, sum_kv]), c_dwup=_chips_comm([gf1["sum_down"]]),
        c_dfn=_chips_comm([gf1["sum_up"][:, half_up:]]))
    p_up1 = jnp.concatenate([p_up1a, p_up1b], axis=1)
    grads["f_norm_post"] = jnp.concatenate([dw_post0, dw_post1], axis=0)
    grads["f_norm_pre"] = jnp.concatenate([gf0["pre"], gf1["pre"]], axis=0)
    grads["f_conv_w"] = jnp.stack([gf0["cw"], gf1["cw"]])
    grads["f_conv_b"] = jnp.concatenate([gf0["cb"], gf1["cb"]], axis=0)

    dyn = _mm(dmix_a, w_out, "nt", F32, "mm_dyn")
    dw_out = _mm(yn, dmix_a, "tn", BF16, "mm_dwout").reshape(N_DEV, -1, d)
    (dy_ssd, dzx, grads["a_gate_norm"]), (s_out,) = _gatenorm_bwd(y_ssd, zx, full["a_gate_norm"], dyn, "gatenorm_bwd",
                                                                  comm=_swap_comm([dw_out]))
    sum_out = _add_pairs(dw_out, s_out, core, "rs_add_out")
    (dxs, dbm, dcm, dzx, dalog, ddsk, dbias), (p_up0,) = _ssd_bwd(
        xbc, zx, bias_g, alog_g, dsk_g, dy_ssd, hst, dzx, "ssd_bwd", comm=_chips_comm([gf0["sum_up"]]))
    grads["a_a_log"] = _ungroup_lanes(dalog, hg)
    grads["a_d_skip"] = _ungroup_lanes(ddsk, hg)
    grads["a_dt_bias"] = _ungroup_lanes(dbias, hg)
    dzx, dcw, dcb = _conv_silu_bwd(zx, dxs, dbm, dcm, a_conv_w, a_conv_b, dzx, "conv_a_bwd")
    grads["a_conv_w"], grads["a_conv_b"] = dcw[None], dcb
    dw_in_all, (p_dn0,) = _mm(hn0, dzx, "tn", BF16, "mm_dwin", comm=_chips_comm([gf0["sum_down"]]))
    dw_in8 = _scatter_cols(dw_in_all, ws_in, segs_in, "scat_w_in")
    dhn0, (s_in, p_out) = _mm(dzx, w_in_all, "nt", F32, "mm_dhn0",
                              comm=_join_comms([_swap_comm([dw_in8]), _chips_comm([sum_out])]))
    sum_in = _add_pairs(dw_in8, s_in, core, "rs_add_in")
    half_in = sum_in.shape[1] // 2
    (grads["meta_tokens"], g_x, grads["a_norm_pre"]), (p_in_a,) = _norm_bwd(
        h0, full["a_norm_pre"], dhn0, dh1, F32, "nb_apre", comm=_chips_comm([sum_in[:, :half_in]]), split_rows=SEQ)
    grad_x = g_x[None]

    small_local = _pack([_split8(grads[n], SHARD_AXIS[n], wts[n].shape[SHARD_AXIS[n]]) for n in SMALL], 1, 8, F32)
    repl_local = _pack([grads[n] for n in REPL], 0, 8, F32)
    n_sr = small_local.shape[1]
    small_vec = jnp.concatenate([small_local.reshape(N_DEV * n_sr, LANE), repl_local], axis=0)
    tail = _join_comms([_chips_comm([sum_in[:, half_in:]]), _gather_comm([small_vec])])
    parts_big = dict(a_w_out=[p_out], w_kv=[p_kv], b_w_q=[p_q], b_w_o=[p_o], f_w_down=[p_dn0, p_dn1])

    def flat_f32(dct, names, mult):
        return _pack([dct[n] for n in names], 0, mult, F32)

    def adamw_big(n, comm=None):
        shp3 = (len(parts_big[n]),) + parts_big[n][0].shape[1:]
        res = _adamw(parts_big[n], *[dct[n].reshape(shp3) for dct in (wts, mom, var)], f"adamw_{n}", comm=comm)
        res, got = res if comm is not None else (res, None)
        big_out[n] = [r.reshape(wts[n].shape) for r in res]
        return got

    big_out = {}
    def swap_last(a):
        return jnp.swapaxes(a, -1, -2)

    g_up_t = swap_last(_sum_parts([p_up0, p_up1], "sum_w_up").reshape(f_w_up.shape))
    res, (p_in_b, small_all) = _adamw([g_up_t[0:1], g_up_t[1:2]], *[swap_last(dct["f_w_up"]) for dct in (wts, mom, var)],
                                      "adamw_f_w_up", comm=tail)
    big_out["f_w_up"] = [swap_last(r) for r in res]
    for n in BIG:
        if n not in ("f_w_up", "a_w_in"):
            adamw_big(n)
    g_in_t = swap_last(_sum_parts([p_in_a, p_in_b], "sum_w_in"))[None]
    res = _adamw([g_in_t], *[swap_last(dct["a_w_in"]) for dct in (wts, mom, var)], "adamw_a_w_in")
    big_out["a_w_in"] = [swap_last(r) for r in res]
    mine_small = lax.dynamic_slice_in_dim(small_all, me * n_sr, n_sr, axis=1)
    parts_small = jnp.concatenate([mine_small, small_all[:, N_DEV * n_sr:]], axis=1)
    sm_in = [jnp.concatenate([flat_f32(dct, SMALL, 8), flat_f32(dct, REPL, 8)], axis=0)[None] for dct in (wts, mom, var)]
    small_out = [r[0] for r in _adamw([parts_small], *sm_in, "adamw_small")]

    outs = []
    for kind in range(4):
        res = {n: big_out[n][kind] for n in BIG}
        for n, a in zip(SMALL, _unpack(small_out[kind][:n_sr], 0, [wts[n].shape for n in SMALL])):
            res[n] = a
        for n, a in zip(REPL, _unpack(small_out[kind][n_sr:], 0, [wts[n].shape for n in REPL])):
            res[n] = a
        outs.append(res)
    return (loss, grad_x, *[outs[0][n] for n in WEIGHTS], *[outs[1][n] for n in WEIGHTS],
            *[outs[2][n] for n in WEIGHTS], *[outs[3][n] for n in WEIGHTS])
```

```python
import functools
import math

import jax
import jax.numpy as jnp
from jax import lax
from jax.experimental import pallas as pl
from jax.experimental.pallas import tpu as pltpu

F32, BF16 = jnp.float32, jnp.bfloat16
S = jax.ShapeDtypeStruct

D_MODEL = 1024
SEQ = 2048
N_META = 16
D_INNER = 2048
HEAD_P = 64
SSM_HEADS = D_INNER // HEAD_P
SSM_GROUPS = 4
D_STATE = 128
SSM_CONV = 4
D_BC = SSM_GROUPS * D_STATE
D_XBC = D_INNER + 2 * D_BC
ATTN_DH = 64
N_Q_HEADS = D_MODEL // ATTN_DH
N_KV_HEADS = 4
D_KV = N_KV_HEADS * ATTN_DH
WINDOW = 128
D_FF = 2816
FFN_CONV = 3
RMS_EPS = 1e-6
NEG = -1e30
LR, B1, B2, EPS, WD, STEP = 0.001, 0.9, 0.999, 1e-08, 0.01, 10

N_DEV = 8
T = 128
LANE = 128
VMEM_LIMIT = 48 * 1024 * 1024

BIG = ("a_w_in", "a_w_out", "w_kv", "b_w_q", "b_w_o", "f_w_up", "f_w_down")
SMALL = ("meta_tokens", "a_norm_pre", "a_conv_w", "a_conv_b", "a_gate_norm", "a_norm_post", "f_conv_w")
REPL = ("a_dt_bias", "a_a_log", "a_d_skip", "kv_norm", "b_norm_pre", "b_sinks", "b_norm_post",
        "f_norm_pre", "f_conv_b", "f_norm_post")
SHARD_AXIS = dict(a_w_in=2, a_w_out=1, w_kv=0, b_w_q=1, b_w_o=1, f_w_up=2, f_w_down=1, meta_tokens=1,
                  a_norm_pre=1, a_conv_w=2, a_conv_b=1, a_gate_norm=1, a_norm_post=1, f_conv_w=2)
WEIGHTS = ("meta_tokens", "a_norm_pre", "a_w_in", "a_conv_w", "a_conv_b", "a_dt_bias", "a_a_log", "a_d_skip",
           "a_gate_norm", "a_w_out", "a_norm_post", "kv_norm", "w_kv", "b_norm_pre", "b_w_q", "b_sinks", "b_w_o",
           "b_norm_post", "f_norm_pre", "f_w_up", "f_conv_w", "f_conv_b", "f_w_down", "f_norm_post")


def _seq_rows():
    return -(-(N_META + SEQ) // T) * T


def _cp(sem=None):
    return pltpu.CompilerParams(dimension_semantics=sem, vmem_limit_bytes=VMEM_LIMIT)


def _pick(n, target):
    t = min(n, target)
    t -= t % LANE
    while n % t:
        t -= LANE
    return t


def _sigmoid(x):
    return 0.5 * jnp.tanh(0.5 * x) + 0.5


def _softplus(x):
    return jnp.maximum(x, 0.0) + jnp.log(1.0 + jnp.exp(-jnp.abs(x)))


_NN = (((1,), (0,)), ((), ()))
_NT = (((1,), (1,)), ((), ()))
_TN = (((0,), (0,)), ((), ()))


def _dot(a, b, dims=_NN):
    return lax.dot_general(a, b, dims, preferred_element_type=F32)


def _dot_hi(a, b):
    return lax.dot_general(a, b, _NN, precision=lax.Precision.HIGHEST, preferred_element_type=F32)


_HBM = pl.BlockSpec(memory_space=pltpu.HBM)
_MESH = pl.DeviceIdType.MESH


class _Comm:
    def __init__(self, ins, out_shapes, scratch, first, last):
        self.ins, self.out_shapes, self.scratch, self.first, self.last = ins, out_shapes, scratch, first, last


_ANY = pl.BlockSpec(memory_space=pl.ANY)


def _call(body, name, out_shape, grid, in_specs, out_specs, sem, args, scratch=(), comm=None, aliases=None):
    aliases = aliases or {}
    if comm is None:
        return pl.pallas_call(body, name=name, out_shape=out_shape, grid=grid, in_specs=in_specs, out_specs=out_specs,
                              scratch_shapes=list(scratch), input_output_aliases=aliases,
                              compiler_params=_cp(sem))(*args)
    single = not isinstance(out_shape, (list, tuple))
    outs = [out_shape] if single else list(out_shape)
    ospecs = [out_specs] if single else list(out_specs)
    n_in, n_out, n_scr, ci, co = len(in_specs), len(outs), len(scratch), len(comm.ins), len(comm.out_shapes)

    def carrier(*refs):
        p = 0
        parts = []
        for cnt in (n_in, ci, n_out, co, n_scr, len(comm.scratch)):
            parts.append(refs[p:p + cnt])
            p += cnt
        ins, cins, outs_r, couts, scr, cscr = parts
        ids = [pl.program_id(i) for i in range(len(grid))]
        first, last = ids[0] == 0, ids[0] == grid[0] - 1
        for i in range(1, len(grid)):
            first, last = first & (ids[i] == 0), last & (ids[i] == grid[i] - 1)

        @pl.when(first)
        def _():
            comm.first(cins, couts, cscr)

        body(*ins, *outs_r, *scr)

        @pl.when(last)
        def _():
            comm.last(cins, couts, cscr)

    res = pl.pallas_call(
        carrier, name=name, out_shape=outs + list(comm.out_shapes), grid=grid,
        in_specs=list(in_specs) + [_HBM] * ci, out_specs=ospecs + [_HBM] * co,
        scratch_shapes=list(scratch) + list(comm.scratch), input_output_aliases=aliases,
        compiler_params=_cp(("arbitrary",) * len(grid)))(*args, *comm.ins)
    mine = res[0] if single else list(res[:n_out])
    return mine, list(res[n_out:])


def _mm(a, b, mode, out_dtype, name, comm=None, shard_cols=None):
    if mode == "tn":
        m, kk = a.shape
        planes, width = (b.shape[0], b.shape[2]) if b.ndim == 3 else (1, b.shape[1])
        n = planes * width
        tko, tn = _pick(kk, 512), _pick(width, 1536)
        per = width // tn
        b_spec = (pl.BlockSpec((None, m, tn), lambda i, j: (j // per, 0, j % per)) if b.ndim == 3
                  else pl.BlockSpec((m, tn), lambda i, j: (0, j)))
        if shard_cols is None:
            def body(a_ref, b_ref, o_ref):
                o_ref[...] = _dot(a_ref[...], b_ref[...], _TN).astype(o_ref.dtype)

            out_shape, out_spec = S((kk, n), out_dtype), pl.BlockSpec((tko, tn), lambda i, j: (i, j))
        else:
            shards = tn // shard_cols
            assert tn % shard_cols == 0

            def body(a_ref, b_ref, o_ref):
                res = _dot(a_ref[...], b_ref[...], _TN).astype(o_ref.dtype)
                for p in range(shards):
                    o_ref[p] = res[:, p * shard_cols:(p + 1) * shard_cols]

            out_shape = S((n // shard_cols, kk, shard_cols), out_dtype)
            out_spec = pl.BlockSpec((shards, tko, shard_cols), lambda i, j: (j, i, 0))
        return _call(
            body, name, out_shape, (kk // tko, n // tn), [pl.BlockSpec((m, tko), lambda i, j: (0, i)), b_spec],
            out_spec, ("parallel", "parallel"), (a, b), comm=comm)

    planes, width = (a.shape[0], a.shape[2]) if a.ndim == 3 else (1, a.shape[1])
    m, kk = a.shape[-2], planes * width
    n = b.shape[1] if mode == "nn" else b.shape[0]
    dims = _NN if mode == "nn" else _NT

    if kk > 2048:
        tm = m // 4
        assert m % 4 == 0 and tm % 16 == 0

        def body(a_ref, b_ref, o_ref):
            if a.ndim == 2:
                res = _dot(a_ref[...], b_ref[...], dims)
            else:
                res = None
                for p in range(planes):
                    bp = b_ref[p * width:(p + 1) * width, :] if mode == "nn" else b_ref[:, p * width:(p + 1) * width]
                    part = _dot(a_ref[p], bp, dims)
                    res = part if res is None else res + part
            o_ref[...] = res.astype(o_ref.dtype)

        a_spec = (pl.BlockSpec((planes, tm, width), lambda i: (0, i, 0)) if a.ndim == 3
                  else pl.BlockSpec((tm, kk), lambda i: (i, 0)))
        return _call(
            body, name, S((m, n), out_dtype), (m // tm,),
            [a_spec, pl.BlockSpec(b.shape, lambda i: (0, 0), pipeline_mode=pl.Buffered(1))],
            pl.BlockSpec((tm, n), lambda i: (i, 0)), ("parallel",), (a, b), comm=comm)

    tn = _pick(n, 512)

    def body(a_ref, b_ref, o_ref):
        o_ref[...] = _dot(a_ref[...], b_ref[...], dims).astype(o_ref.dtype)

    b_spec = (pl.BlockSpec((kk, tn), lambda j: (0, j)) if mode == "nn" else pl.BlockSpec((tn, kk), lambda j: (j, 0)))
    return _call(
        body, name, S((m, n), out_dtype), (n // tn,), [pl.BlockSpec((m, kk), lambda j: (0, 0)), b_spec],
        pl.BlockSpec((m, tn), lambda j: (0, j)), ("parallel",), (a, b), comm=comm)


def _rms(x, w):
    return x * lax.rsqrt(jnp.mean(x * x, axis=-1, keepdims=True) + RMS_EPS) * w


def _row_tile(rows, d):
    return rows // 4 if d <= 1024 and (rows // 4) % 16 == 0 else rows // 8


def _embed_norm(meta, x, w, rows, name, comm=None):
    n_meta, d = meta.shape
    n_x = x.shape[0]
    last = rows // T - 1
    assert n_meta % 8 == 0 and n_meta < T and n_meta + n_x == last * T + n_meta and last * T >= n_x

    def body(m_ref, x_ref, w_ref, h_ref, hn_ref):
        i = pl.program_id(0)

        @pl.when(i == 0)
        def _():
            h_ref[0:n_meta, :] = m_ref[...]
            h_ref[n_meta:T, :] = x_ref[0:T - n_meta, :]

        @pl.when((i > 0) & (i < last))
        def _():
            h_ref[...] = x_ref[pl.ds(pl.multiple_of(i * T - n_meta, 8), T), :]

        @pl.when(i == last)
        def _():
            h_ref[0:n_meta, :] = x_ref[n_x - n_meta:n_x, :]
            h_ref[n_meta:T, :] = jnp.zeros((T - n_meta, d), F32)

        hn_ref[...] = _rms(h_ref[...], w_ref[...]).astype(hn_ref.dtype)

    row = pl.BlockSpec((T, d), lambda i: (i, 0))
    return _call(body, name, [S((rows, d), F32), S((rows, d), BF16)], (rows // T,),
                 [pl.BlockSpec((n_meta, d), lambda i: (0, 0)), pl.BlockSpec((n_x, d), lambda i: (0, 0)),
                  pl.BlockSpec((1, d), lambda i: (0, 0))], [row, row], ("parallel",), (meta, x, w), comm=comm)


def _resid_norm(h, br, w_post, next_ws, name):
    rows, d = h.shape
    tr = _row_tile(rows, d)
    has_br = br is not None
    nw = len(next_ws)

    def body(*refs):
        h_ref = refs[0]
        pos = 1
        x = h_ref[...]
        if has_br:
            x = x + _rms(refs[1][...], refs[2][...])
            pos = 3
        w_refs = refs[pos:pos + nw]
        outs = refs[pos + nw:]
        if has_br:
            outs[0][...] = x
            outs = outs[1:]
        for w_ref, o_ref in zip(w_refs, outs):
            o_ref[...] = _rms(x, w_ref[...]).astype(o_ref.dtype)

    row = pl.BlockSpec((tr, d), lambda i: (i, 0))
    vec = pl.BlockSpec((1, d), lambda i: (0, 0))
    ins = [h] + ([br, w_post] if has_br else []) + list(next_ws)
    in_specs = [row] + ([row, vec] if has_br else []) + [vec] * nw
    out_shape = ([S((rows, d), F32)] if has_br else []) + [S((rows, d), BF16)] * nw
    res = pl.pallas_call(body, name=name, out_shape=out_shape, grid=(rows // tr,), in_specs=in_specs,
                         out_specs=[row] * len(out_shape), compiler_params=_cp(("parallel",)))(*ins)
    if has_br:
        return res[0], list(res[1:])
    return h, list(res)


def _rms_bwd(xv, w, dyv):
    r = lax.rsqrt(jnp.mean(xv * xv, axis=-1, keepdims=True) + RMS_EPS)
    wdy = dyv * w
    dx = r * wdy - xv * (r * r * r) * jnp.mean(xv * wdy, axis=-1, keepdims=True)
    return dx, jnp.sum(dyv * xv * r, axis=0, keepdims=True)


def _norm_bwd(x, w, dy, add, out_dtype, name, comm=None, then=None, split_rows=None):
    rows, d = x.shape
    tr = _row_tile(rows, d)
    has_add = add is not None
    n_in = 3 + has_add + (2 if then is not None else 0)
    if split_rows is not None:
        last, tail = _real_rows(rows, tr, split_rows)

    def body(*refs):
        x_ref, w_ref, dy_ref = refs[:3]
        outs = refs[n_in:]
        dx, dw = _rms_bwd(x_ref[...], w_ref[...], dy_ref[...].astype(F32))
        if has_add:
            dx = dx + refs[3][...]
        if split_rows is None:
            outs[0][...] = dx.astype(outs[0].dtype)
        else:
            i = pl.program_id(0)
            gm_ref, gx_ref = outs[0], outs[1]
            outs = outs[1:]

            @pl.when(i == 0)
            def _():
                gm_ref[...] = dx[0:N_META, :]
                gx_ref[0:tr - N_META, :] = dx[N_META:tr, :]

            @pl.when((i > 0) & (i < last))
            def _():
                gx_ref[pl.ds(pl.multiple_of(i * tr - N_META, 8), tr), :] = dx

            @pl.when(i == last)
            def _():
                gx_ref[split_rows - tail:split_rows, :] = dx[0:tail, :]
        first = pl.program_id(0) == 0

        @pl.when(first)
        def _():
            outs[1][...] = jnp.zeros_like(outs[1])

        outs[1][...] += dw
        if then is not None:
            dx2, dw2 = _rms_bwd(refs[n_in - 2][...], refs[n_in - 1][...], dx)
            outs[2][...] = dx2.astype(outs[2].dtype)

            @pl.when(first)
            def _():
                outs[3][...] = jnp.zeros_like(outs[3])

            outs[3][...] += dw2

    row = pl.BlockSpec((tr, d), lambda i: (i, 0))
    vec = pl.BlockSpec((1, d), lambda i: (0, 0))
    ins = [x, w, dy] + ([add] if has_add else []) + (list(then) if then is not None else [])
    in_specs = [row, vec, row] + ([row] if has_add else []) + ([row, vec] if then is not None else [])
    out_shape = [S((rows, d), out_dtype), S((1, d), F32)] + ([S((rows, d), BF16), S((1, d), F32)] if then is not None else [])
    out_specs = [row, vec] * (len(out_shape) // 2)
    if split_rows is not None:
        out_shape = [S((N_META, d), F32), S((split_rows, d), F32)] + out_shape[1:]
        out_specs = [pl.BlockSpec((N_META, d), lambda i: (0, 0)), pl.BlockSpec((split_rows, d), lambda i: (0, 0))] + out_specs[1:]
    return _call(body, name, out_shape, (rows // tr,), in_specs, out_specs, ("arbitrary",), ins, comm=comm)


def _real_rows(rows, tr, n_x):
    last = (N_META + n_x - 1) // tr
    tail = N_META + n_x - last * tr
    assert last == rows // tr - 1 and N_META % 8 == 0 and tail % 8 == 0 and N_META < tr
    return last, tail


def _final_loss(h, br, w_post, target, name):
    rows, d = h.shape
    tr = _row_tile(rows, d)
    n_x = target.shape[0]
    last, tail = _real_rows(rows, tr, n_x)

    def body(h_ref, br_ref, w_ref, t_ref, dh_ref, loss_ref, dbr_ref, dw_ref, tbuf):
        i = pl.program_id(0)

        @pl.when(i == 0)
        def _():
            tbuf[0:N_META, :] = jnp.zeros((N_META, d), F32)
            tbuf[N_META:tr, :] = t_ref[0:tr - N_META, :]

        @pl.when((i > 0) & (i < last))
        def _():
            tbuf[...] = t_ref[pl.ds(pl.multiple_of(i * tr - N_META, 8), tr), :]

        @pl.when(i == last)
        def _():
            tbuf[0:tail, :] = t_ref[n_x - tail:n_x, :]
            if tail < tr:
                tbuf[tail:tr, :] = jnp.zeros((tr - tail, d), F32)

        brv, wv = br_ref[...], w_ref[...]
        y = h_ref[...] + _rms(brv, wv)
        r = i * tr + lax.broadcasted_iota(jnp.int32, (tr, 1), 0)
        real = (r >= N_META) & (r < N_META + SEQ)
        diff = jnp.where(real, y - tbuf[...], 0.0)
        dh = diff * (1.0 / d)
        dh_ref[...] = dh
        dbr, dw = _rms_bwd(brv, wv, dh)
        dbr_ref[...] = dbr.astype(dbr_ref.dtype)

        @pl.when(i == 0)
        def _():
            loss_ref[...] = jnp.zeros_like(loss_ref)
            dw_ref[...] = jnp.zeros_like(dw_ref)

        loss_ref[...] += jnp.sum(diff * diff) * (0.5 / d)
        dw_ref[...] += dw

    row = pl.BlockSpec((tr, d), lambda i: (i, 0))
    vec = pl.BlockSpec((1, d), lambda i: (0, 0))
    return pl.pallas_call(body, name=name,
                          out_shape=[S((rows, d), F32), S((1, LANE), F32), S((rows, d), BF16), S((1, d), F32)],
                          grid=(rows // tr,), in_specs=[row, row, vec, pl.BlockSpec((n_x, d), lambda i: (0, 0))],
                          out_specs=[row, pl.BlockSpec((1, LANE), lambda i: (0, 0)), row, vec],
                          scratch_shapes=[pltpu.VMEM((tr, d), F32)],
                          compiler_params=_cp(("arbitrary",)))(h, br, w_post, target)


def _gatenorm_fwd(y, zx, w, name, comm=None):
    rows, d = y.shape
    tr = _row_tile(rows, d)

    def body(y_ref, z_ref, w_ref, o_ref):
        z = z_ref[...]
        o_ref[...] = _rms(y_ref[...] * z * _sigmoid(z), w_ref[...]).astype(o_ref.dtype)

    row = pl.BlockSpec((tr, d), lambda i: (i, 0))
    return _call(body, name, S((rows, d), BF16), (rows // tr,), [row, row, pl.BlockSpec((1, d), lambda i: (0, 0))],
                 row, ("parallel",), (y, zx, w), comm=comm)


def _gatenorm_bwd(y, zx, w, dyn, name, comm=None):
    rows, d = y.shape
    tr = _row_tile(rows, d)

    def body(y_ref, z_ref, w_ref, dyn_ref, dy_ref, dz_ref, dw_ref):
        yv, z = y_ref[...], z_ref[...]
        sg = _sigmoid(z)
        sz = z * sg
        g = yv * sz
        r = lax.rsqrt(jnp.mean(g * g, axis=-1, keepdims=True) + RMS_EPS)
        dyn_v = dyn_ref[...]
        wdy = dyn_v * w_ref[...]
        dg = r * wdy - g * (r * r * r) * jnp.mean(g * wdy, axis=-1, keepdims=True)
        dy_ref[...] = dg * sz
        dz_ref[...] = (dg * yv * sg * (1.0 + z * (1.0 - sg))).astype(dz_ref.dtype)

        @pl.when(pl.program_id(0) == 0)
        def _():
            dw_ref[...] = jnp.zeros_like(dw_ref)

        dw_ref[...] += jnp.sum(dyn_v * g * r, axis=0, keepdims=True)

    row = pl.BlockSpec((tr, d), lambda i: (i, 0))
    vec = pl.BlockSpec((1, d), lambda i: (0, 0))
    return _call(body, name, [S((rows, d), F32), S((rows, zx.shape[1]), BF16), S((1, d), F32)], (rows // tr,),
                 [row, row, vec, row], [row, row, vec], ("arbitrary",), (y, zx, w, dyn), comm=comm)


def _shift_down(x, s, rows_iota):
    if s == 0:
        return x
    return jnp.where(rows_iota >= s, pltpu.roll(x, s, 0), 0.0)


def _shift_up(x, s, rows_iota):
    if s == 0:
        return x
    rows = x.shape[0]
    return jnp.where(rows_iota < rows - s, pltpu.roll(x, rows - s, 0), 0.0)


def _r16(v):
    return v.astype(BF16).astype(F32)


def _conv_taps(x, taps, rows_iota):
    x = _r16(x)
    return [_shift_down(x, taps - 1 - k, rows_iota) for k in range(taps)]


def _conv(x, w_ref, b_ref, taps, rows_iota, shifted=None):
    shifted = _conv_taps(x, taps, rows_iota) if shifted is None else shifted
    acc = jnp.zeros_like(shifted[0])
    for k in range(taps):
        acc = acc + _r16(w_ref[k:k + 1, :]) * shifted[k]
    return acc + b_ref[...]


def _conv_bwd(shifted, du, w_ref, dw_ref, db_ref, taps, rows_iota):
    db_ref[...] = jnp.sum(du, axis=0, keepdims=True)
    du = _r16(du)
    dx = jnp.zeros_like(du)
    for k in range(taps):
        dx = dx + _r16(w_ref[k:k + 1, :]) * _shift_up(du, taps - 1 - k, rows_iota)
        dw_ref[k:k + 1, :] = jnp.sum(du * shifted[k], axis=0, keepdims=True)
    return dx


def _conv_silu_fwd(zx, w, b, name, comm=None):
    rows = zx.shape[0]
    cb = 512
    off = D_INNER // cb

    def body(x_ref, w_ref, b_ref, o_ref):
        it = lax.broadcasted_iota(jnp.int32, (rows, 1), 0)
        u = _conv(x_ref[...], w_ref, b_ref, SSM_CONV, it)
        o_ref[...] = u * _sigmoid(u)

    return _call(
        body, name, S((rows, D_XBC), F32), (D_XBC // cb,),
        [pl.BlockSpec((rows, cb), lambda j: (0, off + j)), pl.BlockSpec((SSM_CONV, cb), lambda j: (0, j)),
         pl.BlockSpec((1, cb), lambda j: (0, j))],
        pl.BlockSpec((rows, cb), lambda j: (0, j)), ("parallel",), (zx, w, b), comm=comm)


def _conv_silu_bwd(zx, dxs, dbm, dcm, w, b, dzx, name, comm=None):
    rows = zx.shape[0]
    cb = 256
    off = D_INNER // cb
    nx, nbc = D_INNER // cb, D_BC // cb

    def body(x_ref, dx_in, db_in, dc_in, w_ref, b_ref, dzx_in, dx_ref, dw_ref, db_ref, dbuf):
        del dzx_in
        j = pl.program_id(0)
        for cond, src in ((j < nx, dx_in), ((j >= nx) & (j < nx + nbc), db_in), (j >= nx + nbc, dc_in)):
            @pl.when(cond)
            def _(src=src):
                dbuf[...] = src[...]
        it = lax.broadcasted_iota(jnp.int32, (rows, 1), 0)
        xs = _conv_taps(x_ref[...], SSM_CONV, it)
        u = _conv(None, w_ref, b_ref, SSM_CONV, it, xs)
        sg = _sigmoid(u)
        du = dbuf[...] * sg * (1.0 + u * (1.0 - sg))
        dx_ref[...] = _conv_bwd(xs, du, w_ref, dw_ref, db_ref, SSM_CONV, it).astype(dx_ref.dtype)

    def part(first, count):
        return pl.BlockSpec((rows, cb), lambda j: (0, jnp.clip(j - first, 0, count - 1)))

    col = pl.BlockSpec((rows, cb), lambda j: (0, j))
    wsp = pl.BlockSpec((SSM_CONV, cb), lambda j: (0, j))
    bsp = pl.BlockSpec((1, cb), lambda j: (0, j))
    xbc_cols = pl.BlockSpec((rows, cb), lambda j: (0, off + j))
    return _call(
        body, name, [S(dzx.shape, dzx.dtype), S((SSM_CONV, D_XBC), F32), S((1, D_XBC), F32)], (D_XBC // cb,),
        [xbc_cols, part(0, nx), part(nx, nbc), part(nx + nbc, nbc), wsp, bsp, _ANY],
        [xbc_cols, wsp, bsp], ("arbitrary",), (zx, dxs, dbm, dcm, w, b, dzx), scratch=[pltpu.VMEM((rows, cb), F32)],
        comm=comm, aliases={6: 0})


def _ffn_act_fwd(u, w, b, name, comm=None):
    rows = u.shape[0]
    cb = 256
    nb = D_FF // cb

    def body(g_ref, v_ref, wg_ref, wv_ref, bg_ref, bv_ref, o_ref):
        it = lax.broadcasted_iota(jnp.int32, (rows, 1), 0)
        g = _conv(g_ref[...], wg_ref, bg_ref, FFN_CONV, it)
        v = _conv(v_ref[...], wv_ref, bv_ref, FFN_CONV, it)
        o_ref[...] = (g * _sigmoid(g) * v).astype(o_ref.dtype)

    def sp(r, shift):
        return pl.BlockSpec((r, cb), lambda j: (0, shift + j))

    return _call(
        body, name, S((rows, D_FF), BF16), (nb,),
        [sp(rows, 0), sp(rows, nb), sp(FFN_CONV, 0), sp(FFN_CONV, nb), sp(1, 0), sp(1, nb)],
        sp(rows, 0), ("parallel",), (u, u, w, w, b, b), comm=comm)


def _ffn_act_bwd(u, dact, w, b, name, comm=None):
    rows = u.shape[0]
    cb = 256
    nb = D_FF // cb

    def body(g_ref, v_ref, d_ref, wg_ref, wv_ref, bg_ref, bv_ref, du_ref, dw_ref, db_ref):
        it = lax.broadcasted_iota(jnp.int32, (rows, 1), 0)
        xg, xv = _conv_taps(g_ref[...], FFN_CONV, it), _conv_taps(v_ref[...], FFN_CONV, it)
        g = _conv(None, wg_ref, bg_ref, FFN_CONV, it, xg)
        v = _conv(None, wv_ref, bv_ref, FFN_CONV, it, xv)
        sg = _sigmoid(g)
        d = d_ref[...]
        dgate = d * v * sg * (1.0 + g * (1.0 - sg))
        dval = d * g * sg
        du_ref[0] = _conv_bwd(xg, dgate, wg_ref, dw_ref.at[0], db_ref.at[0], FFN_CONV, it).astype(du_ref.dtype)
        du_ref[1] = _conv_bwd(xv, dval, wv_ref, dw_ref.at[1], db_ref.at[1], FFN_CONV, it).astype(du_ref.dtype)

    def sp(r, shift):
        return pl.BlockSpec((r, cb), lambda j: (0, shift + j))

    def both(r):
        return pl.BlockSpec((2, r, cb), lambda j: (0, 0, j))

    return _call(
        body, name, [S((2, rows, D_FF), BF16), S((2, FFN_CONV, D_FF), F32), S((2, 1, D_FF), F32)], (nb,),
        [sp(rows, 0), sp(rows, nb), sp(rows, 0), sp(FFN_CONV, 0), sp(FFN_CONV, nb), sp(1, 0), sp(1, nb)],
        [both(rows), both(FFN_CONV), both(1)], ("parallel",), (u, u, dact, w, w, b, b), comm=comm)


def _ssd_consts(dtp_ref, bias_ref, alog_ref, hg):
    lane = lax.broadcasted_iota(jnp.int32, (1, LANE), 1)
    pre = dtp_ref[...] + bias_ref[...]
    dt = _softplus(pre)
    a_row = jnp.where(lane < hg, -jnp.exp(alog_ref[...]), 0.0)
    ri = lax.broadcasted_iota(jnp.int32, (T, T), 0)
    ci = lax.broadcasted_iota(jnp.int32, (T, T), 1)
    cs = _dot_hi((ri >= ci).astype(F32), dt * a_row)
    return pre, dt, a_row, cs, ri, ci, lane


def _head_rows(src, hg):
    return jnp.concatenate([jnp.broadcast_to(src[k:k + 1, :], (HEAD_P, src.shape[1])) for k in range(hg)], axis=0)


def _ssd_fwd(xbc, zx, bias, alog, dsk, name, comm=None):
    rows = xbc.shape[0]
    nc = rows // T
    hg = SSM_HEADS // SSM_GROUPS
    gw = hg * HEAD_P
    xoff, boff, coff = 0, D_INNER // D_STATE, (D_INNER + D_BC) // D_STATE
    dtoff = (D_INNER + D_XBC) // LANE

    def body(x_ref, b_ref, c_ref, dtp_ref, bias_ref, alog_ref, dsk_ref, y_ref, hst_ref, hs):
        c = pl.program_id(1)

        @pl.when(c == 0)
        def _():
            hs[...] = jnp.zeros_like(hs)

        _, dt, _, cs, ri, ci, _ = _ssd_consts(dtp_ref, bias_ref, alog_ref, hg)
        cst, dtt = cs.T, dt.T
        xt = x_ref[...].T
        bb, cbf = b_ref[...].astype(BF16), c_ref[...].astype(BF16)
        gt = _dot(bb, cbf, _NT)
        causal_t = ci >= ri
        dskv = dsk_ref[...]
        hall = hs[...]
        hst_ref[0, 0] = hall
        cs8 = cst[0:8, :]
        cl8 = cs8[:, T - 1:T]
        xdt = xt * _head_rows(dtt, hg)
        yo = _head_rows(jnp.exp(cs8), hg) * _dot(hall.astype(BF16), cbf, _NT)
        st = _dot((xdt * _head_rows(jnp.exp(cl8 - cs8), hg)).astype(BF16), bb)
        hs[...] = _head_rows(jnp.exp(cl8), hg) * hall + st
        yds = []
        for k in range(hg):
            sl = slice(k * HEAD_P, (k + 1) * HEAD_P)
            lt = jnp.exp(jnp.where(causal_t, cst[k:k + 1, :] - cs[:, k:k + 1], NEG))
            yds.append(_dot(xdt[sl, :].astype(BF16), (gt * lt).astype(BF16)))
        dsk_r = jnp.concatenate([jnp.broadcast_to(dskv[:, k:k + 1], (HEAD_P, 1)) for k in range(hg)], axis=0)
        y_ref[...] = (jnp.concatenate(yds, axis=0) + yo + dsk_r * xt).T

    vec = pl.BlockSpec((1, LANE), lambda g, c: (0, g))
    return _call(
        body, name, [S((rows, D_INNER), F32), S((nc, SSM_GROUPS, gw, D_STATE), F32)], (SSM_GROUPS, nc),
        [pl.BlockSpec((T, gw), lambda g, c: (c, xoff + g)),
         pl.BlockSpec((T, D_STATE), lambda g, c: (c, boff + g)),
         pl.BlockSpec((T, D_STATE), lambda g, c: (c, coff + g)),
         pl.BlockSpec((T, LANE), lambda g, c: (c, dtoff + g)), vec, vec, vec],
        [pl.BlockSpec((T, gw), lambda g, c: (c, g)), pl.BlockSpec((1, 1, gw, D_STATE), lambda g, c: (c, g, 0, 0))],
        ("parallel", "arbitrary"), (xbc, xbc, xbc, zx, bias, alog, dsk),
        scratch=[pltpu.VMEM((gw, D_STATE), F32)], comm=comm)


def _ssd_bwd(xbc, zx, bias, alog, dsk, dy, hst, dzx, name, comm=None):
    rows = xbc.shape[0]
    nc = rows // T
    hg = SSM_HEADS // SSM_GROUPS
    gw = hg * HEAD_P
    boff, coff = D_INNER // D_STATE, (D_INNER + D_BC) // D_STATE
    dtoff = (D_INNER + D_XBC) // LANE

    def body(x_ref, b_ref, c_ref, dtp_ref, bias_ref, alog_ref, dsk_ref, dy_ref, hst_ref, dzx_in,
             dx_ref, db_ref, dc_ref, ddtp_ref, dalog_ref, ddsk_ref, dbias_ref, dhs):
        del dzx_in
        step = pl.program_id(1)

        @pl.when(step == 0)
        def _():
            dhs[...] = jnp.zeros_like(dhs)
            dalog_ref[...] = jnp.zeros_like(dalog_ref)
            ddsk_ref[...] = jnp.zeros_like(ddsk_ref)
            dbias_ref[...] = jnp.zeros_like(dbias_ref)

        pre, dt, a_row, cs, ri, ci, lane = _ssd_consts(dtp_ref, bias_ref, alog_ref, hg)
        cst, dtt = cs.T, dt.T
        xt, dyt = x_ref[...].T, dy_ref[...].T
        bb, cbf = b_ref[...].astype(BF16), c_ref[...].astype(BF16)
        gt = _dot(bb, cbf, _NT)
        causal_t = ci >= ri
        dskv = dsk_ref[...]
        hall, dhall = hst_ref[0, 0], dhs[...]
        head_row = lax.broadcasted_iota(jnp.int32, (T, 1), 0)
        last_l = lax.broadcasted_iota(jnp.int32, (1, T), 1) == T - 1
        cs8, dt8 = cst[0:8, :], dtt[0:8, :]
        cl8 = cs8[:, T - 1:T]
        e8, wdec8 = jnp.exp(cs8), jnp.exp(cl8 - cs8)
        w8 = wdec8 * dt8
        dt_r, e_r, w_r, ecl_r = _head_rows(dt8, hg), _head_rows(e8, hg), _head_rows(w8, hg), _head_rows(jnp.exp(cl8), hg)
        dsk_r = jnp.concatenate([jnp.broadcast_to(dskv[:, k:k + 1], (HEAD_P, 1)) for k in range(hg)], axis=0)
        hb, dhb = hall.astype(BF16), dhall.astype(BF16)
        xdt = xt * dt_r
        dye = (dyt * e_r).astype(BF16)
        rt = _dot(dhb, bb, _NT)
        yo = e_r * _dot(hb, cbf, _NT)
        dhs[...] = ecl_r * dhall + _dot(dye, cbf)
        dc_acc = _dot(dye, hb, _TN)
        db_acc = _dot((xt * w_r).astype(BF16), dhb, _TN)
        rtx, dyyo, hdh, dyx = rt * xt, dyt * yo, dhall * hall, dyt * xt
        dgt = jnp.zeros((T, T), F32)
        ddt_rows = jnp.zeros((T, T), F32)
        dcs_rows = jnp.zeros((T, T), F32)
        qrow_cols = jnp.zeros((T, LANE), F32)
        ddsk_acc = jnp.zeros((1, LANE), F32)
        dxdts = []
        for k in range(hg):
            sl = slice(k * HEAD_P, (k + 1) * HEAD_P)
            lt = jnp.exp(jnp.where(causal_t, cst[k:k + 1, :] - cs[:, k:k + 1], NEG))
            mpt = gt * lt
            dyb = dyt[sl, :].astype(BF16)
            dxdt = _dot(dyb, mpt.astype(BF16), _NT)
            dmt = _dot(xdt[sl, :].astype(BF16), dyb, _TN)
            dgt = dgt + dmt * lt
            q = dmt * mpt
            q_rows = jnp.sum(q, axis=1, keepdims=True)
            q_cols = jnp.sum(q, axis=0, keepdims=True)
            dxdts.append(dxdt)
            xz = jnp.sum(xt[sl, :] * dxdt, axis=0, keepdims=True)
            dw = jnp.sum(rtx[sl, :], axis=0, keepdims=True)
            wk, wdeck = w8[k:k + 1, :], wdec8[k:k + 1, :]
            dcl = jnp.exp(cl8[k:k + 1, :]) * jnp.sum(hdh[sl, :]) + jnp.sum(dw * wk)
            dcs_r = jnp.sum(dyyo[sl, :], axis=0, keepdims=True) + q_cols - dw * wk + jnp.where(last_l, dcl, 0.0)
            onehot = (lane == k).astype(F32)
            ddt_rows = ddt_rows + jnp.where(head_row == k, xz + dw * wdeck, 0.0)
            dcs_rows = dcs_rows + jnp.where(head_row == k, dcs_r, 0.0)
            qrow_cols = qrow_cols + q_rows * onehot
            ddsk_acc = ddsk_acc + jnp.sum(dyx[sl, :]) * onehot
        dx_ref[...] = (dt_r * jnp.concatenate(dxdts, axis=0) + dsk_r * dyt + rt * w_r).T
        dc_ref[...] = _dot(dgt.T.astype(BF16), bb) + dc_acc
        db_ref[...] = _dot(dgt.astype(BF16), cbf) + db_acc
        da = _dot_hi((ci >= ri).astype(F32), dcs_rows.T - qrow_cols)
        ddtp = (ddt_rows.T + da * a_row) * _sigmoid(pre)
        ddtp = jnp.where(lane < hg, ddtp, 0.0)
        ddtp_ref[...] = ddtp.astype(ddtp_ref.dtype)
        dbias_ref[...] += jnp.sum(ddtp, axis=0, keepdims=True)
        dalog_ref[...] += jnp.sum(da * dt, axis=0, keepdims=True) * a_row
        ddsk_ref[...] += ddsk_acc

    def rc(c):
        return nc - 1 - c

    vec = pl.BlockSpec((1, LANE), lambda g, c: (0, g))
    xsp = pl.BlockSpec((T, gw), lambda g, c: (rc(c), g))
    return _call(
        body, name,
        [S((rows, D_INNER), F32), S((rows, D_BC), F32), S((rows, D_BC), F32),
         S(dzx.shape, dzx.dtype), S((1, SSM_GROUPS * LANE), F32),
         S((1, SSM_GROUPS * LANE), F32), S((1, SSM_GROUPS * LANE), F32)],
        (SSM_GROUPS, nc),
        [xsp,
         pl.BlockSpec((T, D_STATE), lambda g, c: (rc(c), boff + g)),
         pl.BlockSpec((T, D_STATE), lambda g, c: (rc(c), coff + g)),
         pl.BlockSpec((T, LANE), lambda g, c: (rc(c), dtoff + g)), vec, vec, vec,
         xsp, pl.BlockSpec((1, 1, gw, D_STATE), lambda g, c: (rc(c), g, 0, 0)), _ANY],
        [xsp,
         pl.BlockSpec((T, D_STATE), lambda g, c: (rc(c), g)),
         pl.BlockSpec((T, D_STATE), lambda g, c: (rc(c), g)),
         pl.BlockSpec((T, LANE), lambda g, c: (rc(c), dtoff + g)), vec, vec, vec],
        ("parallel", "arbitrary"), (xbc, xbc, xbc, zx, bias, alog, dsk, dy, hst, dzx),
        scratch=[pltpu.VMEM((gw, D_STATE), F32)], comm=comm, aliases={9: 3})


def _attn_tiles(kv_ref, j):
    prev = jnp.maximum(j - 1, 0)
    meta = kv_ref[0:T, :]
    prv = kv_ref[pl.ds(pl.multiple_of(prev * T, T), T), :]
    cur = kv_ref[pl.ds(pl.multiple_of(j * T, T), T), :]
    return jnp.concatenate([meta, prv, cur], axis=0)


def _attn_mask(j):
    r = j * T + lax.broadcasted_iota(jnp.int32, (3 * T, T), 1)
    row = lax.broadcasted_iota(jnp.int32, (3 * T, T), 0)
    t0, t1 = row < T, row < 2 * T
    s = jnp.where(t0, row, (j - 2) * T + row)
    ok = (s <= r) & ((s < N_META) | (s > r - WINDOW))
    use = (t0 & (j >= 2) & (row < N_META)) | (jnp.logical_not(t0) & t1 & (j >= 1)) | jnp.logical_not(t1)
    return ok & use


def _attn_fwd(q, kv, sinks, name, comm=None):
    rows = q.shape[0]
    scale = 1.0 / math.sqrt(ATTN_DH)
    qpk = N_Q_HEADS // N_KV_HEADS

    def body(q_ref, kv_ref, s_ref, o_ref, lse_ref):
        j = pl.program_id(0)
        kv3 = _attn_tiles(kv_ref, j).astype(BF16)
        mask = _attn_mask(j)
        qv = (q_ref[...] * scale).astype(BF16)
        sk = s_ref[...]
        lses = []
        for kh in range(N_KV_HEADS):
            k3 = kv3[:, kh * ATTN_DH:(kh + 1) * ATTN_DH]
            v3 = kv3[:, D_KV + kh * ATTN_DH:D_KV + (kh + 1) * ATTN_DH]
            for g in range(qpk):
                h = kh * qpk + g
                sink = sk[:, h:h + 1]
                sc = jnp.where(mask, _dot(k3, qv[:, h * ATTN_DH:(h + 1) * ATTN_DH], _NT), NEG)
                m = jnp.maximum(jnp.max(sc, axis=0, keepdims=True), sink)
                p = jnp.exp(sc - m)
                den = jnp.sum(p, axis=0, keepdims=True) + jnp.exp(sink - m)
                p = p * (1.0 / den)
                lses.append(m + jnp.log(den))
                o_ref[:, h * ATTN_DH:(h + 1) * ATTN_DH] = _dot(p.astype(BF16), v3, _TN).astype(o_ref.dtype)
        lse_ref[...] = jnp.concatenate(lses, axis=0)

    return _call(
        body, name, [S((rows, D_MODEL), BF16), S((N_Q_HEADS, rows), F32)], (rows // T,),
        [pl.BlockSpec((T, D_MODEL), lambda j: (j, 0)), pl.BlockSpec((rows, 2 * D_KV), lambda j: (0, 0)),
         pl.BlockSpec((1, N_Q_HEADS), lambda j: (0, 0))],
        [pl.BlockSpec((T, D_MODEL), lambda j: (j, 0)), pl.BlockSpec((N_Q_HEADS, T), lambda j: (0, j))],
        ("parallel",), (q, kv, sinks), comm=comm)


def _attn_bwd(q, kv, sinks, do, lse, name, comm=None):
    rows = q.shape[0]
    scale = 1.0 / math.sqrt(ATTN_DH)
    qpk = N_Q_HEADS // N_KV_HEADS

    def body(q_ref, kv_ref, s_ref, do_ref, lse_ref, dq_ref, dkv_ref, ds_ref):
        j = pl.program_id(0)

        @pl.when(j == 0)
        def _():
            dkv_ref[...] = jnp.zeros_like(dkv_ref)
            ds_ref[...] = jnp.zeros_like(ds_ref)

        kv3 = _attn_tiles(kv_ref, j).astype(BF16)
        mask = _attn_mask(j)
        qv = (q_ref[...] * scale).astype(BF16)
        dov = do_ref[...].astype(BF16)
        sk = s_ref[...]
        lsev = lse_ref[...]
        lane = lax.broadcasted_iota(jnp.int32, (1, LANE), 1)
        ds_acc = jnp.zeros((1, LANE), F32)
        prev = jnp.maximum(j - 1, 0)
        mask4 = jnp.concatenate([mask] * qpk, axis=1)
        dqts = []
        for kh in range(N_KV_HEADS):
            ksl = slice(kh * ATTN_DH, (kh + 1) * ATTN_DH)
            vsl = slice(D_KV + kh * ATTN_DH, D_KV + (kh + 1) * ATTN_DH)
            k3, v3 = kv3[:, ksl], kv3[:, vsl]
            heads = [kh * qpk + g for g in range(qpk)]
            q4 = jnp.concatenate([qv[:, h * ATTN_DH:(h + 1) * ATTN_DH] for h in heads], axis=0)
            do4 = jnp.concatenate([dov[:, h * ATTN_DH:(h + 1) * ATTN_DH] for h in heads], axis=0)
            lse4 = jnp.concatenate([lsev[h:h + 1, :] for h in heads], axis=1)
            sink4 = jnp.concatenate([jnp.broadcast_to(sk[:, h:h + 1], (1, T)) for h in heads], axis=1)
            p = jnp.exp(jnp.where(mask4, _dot(k3, q4, _NT), NEG) - lse4)
            ps = jnp.exp(sink4 - lse4)
            dp = _dot(v3, do4, _NT)
            delta = jnp.sum(p * dp, axis=0, keepdims=True)
            dsc = (p * (dp - delta)).astype(BF16)
            dq4 = _dot(k3.T, dsc) * scale
            dk3 = _dot(dsc, q4)
            dv3 = _dot(p.astype(BF16), do4)
            psd = ps * delta
            for g, h in enumerate(heads):
                dqts.append(dq4[:, g * T:(g + 1) * T])
                ds_acc = ds_acc - jnp.sum(psd[:, g * T:(g + 1) * T]) * (lane == h).astype(F32)
            for t, start in enumerate((0, pl.multiple_of(prev * T, T), pl.multiple_of(j * T, T))):
                rsl = pl.ds(start, T)
                dkv_ref[rsl, ksl] += dk3[t * T:(t + 1) * T, :]
                dkv_ref[rsl, vsl] += dv3[t * T:(t + 1) * T, :]
        ds_ref[...] += ds_acc
        dq_ref[...] = jnp.concatenate(dqts, axis=0).T.astype(dq_ref.dtype)

    blk = pl.BlockSpec((T, D_MODEL), lambda j: (j, 0))
    full = pl.BlockSpec((rows, 2 * D_KV), lambda j: (0, 0))
    return _call(
        body, name, [S((rows, D_MODEL), BF16), S((rows, 2 * D_KV), F32), S((1, LANE), F32)], (rows // T,),
        [blk, full, pl.BlockSpec((1, N_Q_HEADS), lambda j: (0, 0)), blk, pl.BlockSpec((N_Q_HEADS, T), lambda j: (0, j))],
        [blk, full, pl.BlockSpec((1, LANE), lambda j: (0, 0))], ("arbitrary",), (q, kv, sinks, do, lse), comm=comm)


BLOCK_BYTES = 1 << 20


def _div_tile(rows, cols, block_bytes=BLOCK_BYTES):
    cap = max(16, block_bytes // (4 * cols))
    best = None
    for t in range(16, min(rows, cap) + 1, 16):
        if rows % t == 0:
            best = t
    return best if best is not None else rows


def _adamw(parts, w, m, v, name, comm=None):
    layers, rows, cols = w.shape
    n = parts[0].shape[0]
    tr = _div_tile(rows, cols)
    tc = _pick(cols, 256) if tr == rows and rows * cols * 4 > 2 * BLOCK_BYTES else cols
    c1 = 1.0 / (1.0 - B1 ** STEP)
    c2 = 1.0 / (1.0 - B2 ** STEP)

    def body(*refs):
        p_refs = refs[:layers]
        w_ref, m_ref, v_ref, g_ref, d_ref, nm_ref, nv_ref = refs[layers:]
        layer = pl.program_id(0)
        for l in range(layers):
            @pl.when(layer == l)
            def _(p_ref=p_refs[l]):
                g = p_ref[0].astype(F32)
                for i in range(1, n):
                    g = g + p_ref[i].astype(F32)
                nm = B1 * m_ref[...] + (1.0 - B1) * g
                nv = B2 * v_ref[...] + (1.0 - B2) * (g * g)
                g_ref[...] = g
                nm_ref[...] = nm
                nv_ref[...] = nv
                d_ref[...] = -LR * ((nm * c1) / (jnp.sqrt(nv * c2) + EPS) + WD * w_ref[...])

    def part_spec(l):
        return pl.BlockSpec((n, tr, tc), lambda k, i, j: (0, jnp.where(k == l, i, 0), jnp.where(k == l, j, 0)))

    row = pl.BlockSpec((None, tr, tc), lambda k, i, j: (k, i, j))
    return _call(body, name, [S((layers, rows, cols), F32)] * 4, (layers, rows // tr, cols // tc),
                 [part_spec(l) for l in range(layers)] + [row, row, row], [row] * 4,
                 ("parallel", "parallel", "parallel"), (*parts, w, m, v), comm=comm)


def _sum_parts(parts, name):
    n, rows, cols = parts[0].shape
    nb = len(parts)
    tr = _div_tile(rows, cols)

    def body(*refs):
        o_ref = refs[nb]
        blk = pl.program_id(0)
        for l in range(nb):
            @pl.when(blk == l)
            def _(p_ref=refs[l]):
                g = p_ref[0].astype(F32)
                for i in range(1, n):
                    g = g + p_ref[i].astype(F32)
                o_ref[...] = g

    def part_spec(l):
        return pl.BlockSpec((n, tr, cols), lambda k, i: (0, jnp.where(k == l, i, 0), 0))

    per = rows // tr
    return pl.pallas_call(body, name=name, out_shape=S((nb * rows, cols), F32), grid=(nb, per),
                          in_specs=[part_spec(l) for l in range(nb)],
                          out_specs=pl.BlockSpec((tr, cols), lambda k, i: (k * per + i, 0)),
                          compiler_params=_cp(("parallel", "parallel")))(*parts)


def _col_segments(ws, runs):
    segs = []
    for glo, mlo, n in runs:
        while n > 0:
            d, off = divmod(glo, ws)
            take = min(n, ws - off)
            segs.append((d, off, mlo, take))
            glo, mlo, n = glo + take, mlo + take, n - take
    return segs


def _assemble_cols(gs, width, segs, name):
    _, rows, ws = gs[0].shape
    nb = len(gs)
    rb = _div_tile(rows, width // 2, 2 * BLOCK_BYTES)
    per = rows // rb

    def body(*refs):
        o_ref = refs[nb]
        piece = pl.program_id(0)
        for l in range(nb):
            @pl.when(piece == l)
            def _(g_ref=refs[l]):
                o_ref[...] = jnp.zeros_like(o_ref)
                for d, off, mlo, n in segs:
                    o_ref[:, mlo:mlo + n] = g_ref[d, :, off:off + n]

    def piece_spec(l):
        return pl.BlockSpec((N_DEV, rb, ws), lambda k, i: (0, jnp.where(k == l, i, 0), 0))

    return pl.pallas_call(
        body, name=name, out_shape=S((nb * rows, width), gs[0].dtype), grid=(nb, per),
        in_specs=[piece_spec(l) for l in range(nb)],
        out_specs=pl.BlockSpec((rb, width), lambda k, i: (k * per + i, 0)),
        compiler_params=_cp(("parallel", "parallel")))(*gs)


def _scatter_cols(dw, ws, segs, name):
    rows, width = dw.shape
    rb = _div_tile(rows, width, 2 * BLOCK_BYTES)

    def body(w_ref, o_ref):
        for d, off, mlo, n in segs:
            o_ref[d, :, off:off + n] = w_ref[:, mlo:mlo + n].astype(o_ref.dtype)

    return pl.pallas_call(
        body, name=name, out_shape=S((N_DEV, rows, ws), BF16), grid=(rows // rb,),
        in_specs=[pl.BlockSpec((rb, width), lambda i: (i, 0))],
        out_specs=pl.BlockSpec((N_DEV, rb, ws), lambda i: (0, i, 0)), compiler_params=_cp(("parallel",)))(dw)


def _gather_comm(xs):
    n = len(xs)

    def setup(x_refs, out_refs, sems):
        send_sems, recv_sems, local_sems = sems
        mx, my, mc = lax.axis_index("x"), lax.axis_index("y"), lax.axis_index("c")
        me, sibling = (mx, my, mc), (mx, my, 1 - mc)
        chips = [(1 - mx, my), (mx, 1 - my), (1 - mx, 1 - my)]

        def blk(a, px, py, pc):
            return out_refs[a].at[4 * px + 2 * py + pc]

        def copy(a, k, block, to, src=None):
            return pltpu.make_async_remote_copy(
                src_ref=blk(a, *block) if src is None else src, dst_ref=blk(a, *block),
                send_sem=send_sems.at[a, k], recv_sem=recv_sems.at[a, k], device_id=to, device_id_type=_MESH)

        mine = [pltpu.make_async_copy(x_refs[a], blk(a, *me), local_sems.at[a]) for a in range(n)]
        own = []
        for a in range(n):
            own.append(copy(a, 0, me, sibling, src=x_refs[a]))
            own += [copy(a, 1 + i, me, (*chip, mc), src=x_refs[a]) for i, chip in enumerate(chips)]
        return me, sibling, chips, mc, copy, mine, own

    def first(x_refs, out_refs, sems):
        _, _, _, _, _, mine, own = setup(x_refs, out_refs, sems)
        for cp in mine + own:
            cp.start()

    def last(x_refs, out_refs, sems):
        me, sibling, chips, mc, copy, mine, own = setup(x_refs, out_refs, sems)
        passed = []
        for a in range(n):
            for i, chip in enumerate(chips):
                copy(a, 1 + i, (*chip, mc), me).wait_recv()
                passed.append(copy(a, 4 + i, (*chip, mc), sibling))
                passed[-1].start()
        for a in range(n):
            copy(a, 0, sibling, me).wait_recv()
            for i, chip in enumerate(chips):
                copy(a, 4 + i, (*chip, 1 - mc), me).wait_recv()
        for cp in own + passed:
            cp.wait_send()
        for cp in mine:
            cp.wait()

    return _Comm(list(xs), [S((N_DEV,) + x.shape, x.dtype) for x in xs],
                 [pltpu.SemaphoreType.DMA((n, 7)), pltpu.SemaphoreType.DMA((n, 7)), pltpu.SemaphoreType.DMA((n,))],
                 first, last)


def _swap_comm(gs):
    n = len(gs)

    def copies(g_refs, out_refs, sems):
        send_sems, recv_sems = sems
        mx, my, mc = lax.axis_index("x"), lax.axis_index("y"), lax.axis_index("c")
        return [pltpu.make_async_remote_copy(
            src_ref=g_refs[a].at[2 * k + 1 - mc], dst_ref=out_refs[a].at[k], send_sem=send_sems.at[a, k],
            recv_sem=recv_sems.at[a, k], device_id=(mx, my, 1 - mc), device_id_type=_MESH)
            for a in range(n) for k in range(4)]

    def first(g_refs, out_refs, sems):
        for cp in copies(g_refs, out_refs, sems):
            cp.start()

    def last(g_refs, out_refs, sems):
        for cp in copies(g_refs, out_refs, sems):
            cp.wait()

    return _Comm(list(gs), [S((4,) + g.shape[1:], g.dtype) for g in gs],
                 [pltpu.SemaphoreType.DMA((n, 4)), pltpu.SemaphoreType.DMA((n, 4))], first, last)


def _chips_comm(parts):
    n = len(parts)

    def copies(p_refs, out_refs, sems):
        send_sems, recv_sems, local_sems = sems
        mx, my, mc = lax.axis_index("x"), lax.axis_index("y"), lax.axis_index("c")
        mychip = 2 * mx + my
        chips = [(1 - mx, my), (mx, 1 - my), (1 - mx, 1 - my)]
        mine = [pltpu.make_async_copy(p_refs[a].at[mychip], out_refs[a].at[mychip], local_sems.at[a])
                for a in range(n)]
        return mine + [pltpu.make_async_remote_copy(
            src_ref=p_refs[a].at[2 * cx + cy], dst_ref=out_refs[a].at[mychip], send_sem=send_sems.at[a, i],
            recv_sem=recv_sems.at[a, i], device_id=(cx, cy, mc), device_id_type=_MESH)
            for a in range(n) for i, (cx, cy) in enumerate(chips)]

    def first(p_refs, out_refs, sems):
        for cp in copies(p_refs, out_refs, sems):
            cp.start()

    def last(p_refs, out_refs, sems):
        for cp in copies(p_refs, out_refs, sems):
            cp.wait()

    return _Comm(list(parts), [S(p.shape, p.dtype) for p in parts],
                 [pltpu.SemaphoreType.DMA((n, 3)), pltpu.SemaphoreType.DMA((n, 3)), pltpu.SemaphoreType.DMA((n,))],
                 first, last)


def _join_comms(comms):
    def split(refs, counts):
        out, p = [], 0
        for cnt in counts:
            out.append(refs[p:p + cnt])
            p += cnt
        return out

    ni = [len(c.ins) for c in comms]
    no = [len(c.out_shapes) for c in comms]
    ns = [len(c.scratch) for c in comms]

    def first(in_refs, out_refs, sems):
        for c, i, o, s in zip(comms, split(in_refs, ni), split(out_refs, no), split(sems, ns)):
            c.first(i, o, s)

    def last(in_refs, out_refs, sems):
        for c, i, o, s in zip(comms, split(in_refs, ni), split(out_refs, no), split(sems, ns)):
            c.last(i, o, s)

    return _Comm([x for c in comms for x in c.ins], [x for c in comms for x in c.out_shapes],
                 [x for c in comms for x in c.scratch], first, last)


def _add_pairs(mine, theirs, core, name):
    _, rows, cols = mine.shape
    tr = _div_tile(rows, cols // 2)

    def body(core_ref, a_ref, b_ref, o_ref):
        o_ref[...] = (a_ref[...].astype(F32) + b_ref[...].astype(F32)).astype(o_ref.dtype)

    return pl.pallas_call(
        body, name=name, out_shape=S((4, rows, cols), BF16),
        grid_spec=pltpu.PrefetchScalarGridSpec(
            num_scalar_prefetch=1, grid=(4, rows // tr),
            in_specs=[pl.BlockSpec((None, tr, cols), lambda k, i, c: (2 * k + c[0], i, 0)),
                      pl.BlockSpec((None, tr, cols), lambda k, i, c: (k, i, 0))],
            out_specs=pl.BlockSpec((None, tr, cols), lambda k, i, c: (k, i, 0))),
        compiler_params=_cp(("parallel", "parallel")))(core, mine, theirs)


def _run_comm(comm, name):
    ci, co = len(comm.ins), len(comm.out_shapes)

    def body(*refs):
        comm.first(refs[:ci], refs[ci:ci + co], refs[ci + co:])
        comm.last(refs[:ci], refs[ci:ci + co], refs[ci + co:])

    return pl.pallas_call(body, name=name, out_shape=list(comm.out_shapes), in_specs=[_HBM] * ci,
                          out_specs=[_HBM] * co, scratch_shapes=list(comm.scratch))(*comm.ins)


def _flat_rows(n_elems, mult):
    rows = -(-n_elems // LANE)
    return -(-rows // mult) * mult


def _pack(arrs, lead, mult, dtype):
    lead_shape = arrs[0].shape[:lead]
    flat = jnp.concatenate([a.astype(dtype).reshape(lead_shape + (-1,)) for a in arrs], axis=-1)
    n = flat.shape[-1]
    rows = _flat_rows(n, mult)
    flat = jnp.pad(flat, [(0, 0)] * lead + [(0, rows * LANE - n)])
    return flat.reshape(lead_shape + (rows, LANE))


def _unpack(flat, lead, shapes):
    lead_shape = flat.shape[:lead]
    flat = flat.reshape(lead_shape + (-1,))
    out, off = [], 0
    for shp in shapes:
        n = math.prod(shp)
        out.append(flat[..., off:off + n].reshape(lead_shape + tuple(shp)))
        off += n
    return out


def _split8(full, ax, n):
    shp = full.shape
    return jnp.moveaxis(full.reshape(shp[:ax] + (N_DEV, n) + shp[ax + 1:]), ax, 0)


def _join8(g, ax):
    shp = g.shape[1:]
    return jnp.moveaxis(g, 0, ax).reshape(shp[:ax] + (N_DEV * shp[ax],) + shp[ax + 1:])


def _group_lanes(v, hg):
    v = v.reshape(SSM_GROUPS, hg)
    return jnp.pad(v, ((0, 0), (0, LANE - hg))).reshape(1, SSM_GROUPS * LANE)


def _ungroup_lanes(v, hg):
    return v.reshape(SSM_GROUPS, LANE)[:, :hg].reshape(1, SSM_GROUPS * hg)


def kernel(x, meta_tokens, a_norm_pre, a_w_in, a_conv_w, a_conv_b, a_dt_bias, a_a_log, a_d_skip, a_gate_norm, a_w_out, a_norm_post, kv_norm, w_kv, b_norm_pre, b_w_q, b_sinks, b_w_o, b_norm_post, f_norm_pre, f_w_up, f_conv_w, f_conv_b, f_w_down, f_norm_post, loss_target, m_meta_tokens, m_a_norm_pre, m_a_w_in, m_a_conv_w, m_a_conv_b, m_a_dt_bias, m_a_a_log, m_a_d_skip, m_a_gate_norm, m_a_w_out, m_a_norm_post, m_kv_norm, m_w_kv, m_b_norm_pre, m_b_w_q, m_b_sinks, m_b_w_o, m_b_norm_post, m_f_norm_pre, m_f_w_up, m_f_conv_w, m_f_conv_b, m_f_w_down, m_f_norm_post, v_meta_tokens, v_a_norm_pre, v_a_w_in, v_a_conv_w, v_a_conv_b, v_a_dt_bias, v_a_a_log, v_a_d_skip, v_a_gate_norm, v_a_w_out, v_a_norm_post, v_kv_norm, v_w_kv, v_b_norm_pre, v_b_w_q, v_b_sinks, v_b_w_o, v_b_norm_post, v_f_norm_pre, v_f_w_up, v_f_conv_w, v_f_conv_b, v_f_w_down, v_f_norm_post):
    args = locals()
    wts = {n: args[n] for n in WEIGHTS}
    mom = {n: args["m_" + n] for n in WEIGHTS}
    var = {n: args["v_" + n] for n in WEIGHTS}
    mx, my, mc = lax.axis_index("x"), lax.axis_index("y"), lax.axis_index("c")
    me = 4 * mx + 2 * my + mc
    rows = _seq_rows()
    hg = SSM_HEADS // SSM_GROUPS
    d = D_MODEL

    n_main = D_INNER + D_XBC
    ws_in, ws_up = a_w_in.shape[2], f_w_up.shape[2]
    segs_in = _col_segments(ws_in, [(0, 0, n_main)] + [(n_main + hg * g, n_main + LANE * g, hg)
                                                      for g in range(SSM_GROUPS)])
    segs_up = _col_segments(ws_up, [(0, 0, 2 * D_FF)])
    def gather_of(*ws):
        return _gather_comm([w.astype(BF16) for w in ws])

    small_full, = _run_comm(_gather_comm([_pack([wts[n] for n in SMALL], 0, 8, F32)]), "gather_small")
    full = {}
    for n, g in zip(SMALL, _unpack(small_full, 1, [wts[n].shape for n in SMALL])):
        full[n] = _join8(g, SHARD_AXIS[n])
    (h0, hn0), (g_in,) = _embed_norm(full["meta_tokens"], x[0], full["a_norm_pre"], rows, "embed_norm",
                                     comm=gather_of(a_w_in[0]))
    w_in_all = _assemble_cols([g_in], n_main + SSM_GROUPS * LANE, segs_in, "asm_w_in")
    w_up, w_down = [None, None], [None, None]
    bias_g = _group_lanes(wts["a_dt_bias"], hg)
    alog_g = _group_lanes(wts["a_a_log"], hg)
    dsk_g = _group_lanes(wts["a_d_skip"], hg)
    a_conv_w, a_conv_b = full["a_conv_w"][0], full["a_conv_b"]
    f_cw, f_cb = full["f_conv_w"], wts["f_conv_b"]
    fpre, fpost = wts["f_norm_pre"], wts["f_norm_post"]


    zx, (g_out,) = _mm(hn0, w_in_all, "nn", F32, "mm_in", comm=gather_of(a_w_out[0]))
    w_out = g_out.reshape(D_INNER, d)
    xbc = _conv_silu_fwd(zx, a_conv_w, a_conv_b, "conv_a")
    (y_ssd, hst), (g_up0,) = _ssd_fwd(xbc, zx, bias_g, alog_g, dsk_g, "ssd_fwd", comm=gather_of(f_w_up[0]))
    w_up[0] = _assemble_cols([g_up0], 2 * D_FF, segs_up, "asm_w_up0")
    yn = _gatenorm_fwd(y_ssd, zx, full["a_gate_norm"], "gatenorm")
    mix_a, (g_o,) = _mm(yn, w_out, "nn", F32, "mm_out", comm=gather_of(b_w_o[0]))
    h1, (fn0,) = _resid_norm(h0, mix_a, full["a_norm_post"], [fpre[0:1]], "resid_a")

    half = d // 2
    u0, (g_dn0,) = _mm(fn0, w_up[0], "nn", F32, "mm_up0", comm=gather_of(f_w_down[0]))
    act0, (g_up1a,) = _ffn_act_fwd(u0, f_cw[0], f_cb[0:1], "ffn_act0", comm=gather_of(f_w_up[1, :half]))
    ffn0, (g_kv, g_q) = _mm(act0, g_dn0.reshape(D_FF, d), "nn", F32, "mm_down0", comm=gather_of(w_kv, b_w_q[0]))
    w_kvf, w_q, w_o = g_kv.reshape(d, 2 * D_KV), g_q.reshape(d, d), g_o.reshape(d, d)
    h2, (kvn, bn) = _resid_norm(h1, ffn0, fpost[0:1], [wts["kv_norm"].reshape(1, d), wts["b_norm_pre"]], "resid_f0")
    kv = _mm(kvn, w_kvf, "nn", F32, "mm_kv")
    q = _mm(bn, w_q, "nn", F32, "mm_q")
    (o, lse), (g_up1b,) = _attn_fwd(q, kv, wts["b_sinks"], "attn_fwd", comm=gather_of(f_w_up[1, half:]))
    w_up[1] = _assemble_cols([g_up1a, g_up1b], 2 * D_FF, segs_up, "asm_w_up1")
    mix_b = _mm(o, w_o, "nn", F32, "mm_o")
    h3, (fn1,) = _resid_norm(h2, mix_b, wts["b_norm_post"], [fpre[1:2]], "resid_b")
    u1, (g_dn1,) = _mm(fn1, w_up[1], "nn", F32, "mm_up1", comm=gather_of(f_w_down[1]))
    w_down = [g_dn0.reshape(D_FF, d), g_dn1.reshape(D_FF, d)]
    act1 = _ffn_act_fwd(u1, f_cw[1], f_cb[1:2], "ffn_act1")
    ffn1 = _mm(act1, w_down[1], "nn", F32, "mm_down1")
    dh4, loss_row, dffn1, dw_post1 = _final_loss(h3, ffn1, fpost[1:2], loss_target[0], "loss")
    loss = lax.psum(loss_row[0, 0], ("x", "y", "c"))

    grads = {}

    core = mc.astype(jnp.int32).reshape(1)

    def carried(res, comm):
        return res if comm is not None else (res, None)

    def ffn_bwd(dh_out, dffn, h_in, fn, u, act, i, then, c_dact=None, c_dwdown=None, c_dwup=None, c_dfn=None):
        dact, got_a = carried(_mm(dffn, w_down[i], "nt", F32, f"mm_dact{i}", comm=c_dact), c_dact)
        dw_down, got_b = carried(_mm(act, dffn, "tn", BF16, f"mm_dwdown{i}", comm=c_dwdown), c_dwdown)
        dw_down = dw_down.reshape(N_DEV, -1, d)
        du, dwc, dbc = _ffn_act_bwd(u, dact, f_cw[i], f_cb[i:i + 1], f"ffn_act_bwd{i}")
        dfn, (s_dn, *got_d) = _mm(du, w_up[i], "nt", F32, f"mm_dfn{i}", comm=_join_comms(
            [_swap_comm([dw_down])] + ([c_dfn] if c_dfn is not None else [])))
        sum_dn = _add_pairs(dw_down, s_dn, core, f"rs_add_dn{i}")
        dw_up, got_c = carried(_mm(fn, du, "tn", BF16, f"mm_dwup{i}", comm=c_dwup, shard_cols=ws_up), c_dwup)
        (dh_in, dw_pre, dbranch, dw_branch), (s_up,) = _norm_bwd(
            h_in, fpre[i:i + 1], dfn, dh_out, F32, f"nb_fpre{i}", comm=_swap_comm([dw_up]), then=then)
        sum_up = _add_pairs(dw_up, s_up, core, f"rs_add_up{i}")
        return dh_in, dbranch, dw_branch, dict(sum_down=sum_dn, cw=jnp.concatenate([dwc[0], dwc[1]], axis=1),
                                               cb=jnp.concatenate([dbc[0], dbc[1]], axis=1), sum_up=sum_up,
                                               pre=dw_pre), got_a, got_b, got_c, got_d

    dh3, dmix_b, grads["b_norm_post"], gf1, _, _, _, _ = ffn_bwd(dh4, dffn1, h3, fn1, u1, act1, 1,
                                                                 (mix_b, wts["b_norm_post"]))
    do = _mm(dmix_b, w_o, "nt", F32, "mm_do")
    dw_o = _mm(o, dmix_b, "tn", BF16, "mm_dwo").reshape(N_DEV, -1, d)
    half_up = gf1["sum_up"].shape[1] // 2
    (dq, dkv, dsinks), (p_up1a, s_o) = _attn_bwd(
        q, kv, wts["b_sinks"], do, lse, "attn_bwd",
        comm=_join_comms([_chips_comm([gf1["sum_up"][:, :half_up]]), _swap_comm([dw_o])]))
    sum_o = _add_pairs(dw_o, s_o, core, "rs_add_o")
    grads["b_sinks"] = dsinks[:, :N_Q_HEADS]
    dbn = _mm(dq, w_q, "nt", F32, "mm_dbn")
    dw_q = _mm(bn, dq, "tn", BF16, "mm_dwq").reshape(N_DEV, -1, d)
    dkv16 = dkv.astype(BF16)
    dkvn = _mm(dkv16, w_kvf, "nt", F32, "mm_dkvn")
    dw_kv = _mm(kvn, dkv16, "tn", BF16, "mm_dwkv").reshape(N_DEV, -1, 2 * D_KV)
    (dh2, grads["b_norm_pre"]), (s_q, s_kv) = _norm_bwd(h2, wts["b_norm_pre"], dbn, dh3, F32, "nb_bpre",
                                                        comm=_swap_comm([dw_q, dw_kv]))
    sum_q, sum_kv = _add_pairs(dw_q, s_q, core, "rs_add_q"), _add_pairs(dw_kv, s_kv, core, "rs_add_kv")
    dh2, dw_kvn, dffn0, dw_post0 = _norm_bwd(h2, wts["kv_norm"].reshape(1, d), dkvn, dh2, F32, "nb_kv",
                                             then=(ffn0, fpost[0:1]))
    grads["kv_norm"] = dw_kvn.reshape(d)
    dh1, dmix_a, grads["a_norm_post"], gf0, (p_o,), (p_q, p_kv), (p_dn1,), (p_up1b,) = ffn_bwd(
        dh2, dffn0, h1, fn0, u0, act0, 0, (mix_a, full["a_norm_post"]), c_dact=_chips_comm([sum_o]),
        c_dwdown=_chips_comm([sum_q, sum_kv]), c_dwup=_chips_comm([gf1["sum_down"]]),
        c_dfn=_chips_comm([gf1["sum_up"][:, half_up:]]))
    p_up1 = jnp.concatenate([p_up1a, p_up1b], axis=1)
    grads["f_norm_post"] = jnp.concatenate([dw_post0, dw_post1], axis=0)
    grads["f_norm_pre"] = jnp.concatenate([gf0["pre"], gf1["pre"]], axis=0)
    grads["f_conv_w"] = jnp.stack([gf0["cw"], gf1["cw"]])
    grads["f_conv_b"] = jnp.concatenate([gf0["cb"], gf1["cb"]], axis=0)

    dyn = _mm(dmix_a, w_out, "nt", F32, "mm_dyn")
    dw_out = _mm(yn, dmix_a, "tn", BF16, "mm_dwout").reshape(N_DEV, -1, d)
    (dy_ssd, dzx, grads["a_gate_norm"]), (s_out,) = _gatenorm_bwd(y_ssd, zx, full["a_gate_norm"], dyn, "gatenorm_bwd",
                                                                  comm=_swap_comm([dw_out]))
    sum_out = _add_pairs(dw_out, s_out, core, "rs_add_out")
    (dxs, dbm, dcm, dzx, dalog, ddsk, dbias), (p_up0,) = _ssd_bwd(
        xbc, zx, bias_g, alog_g, dsk_g, dy_ssd, hst, dzx, "ssd_bwd", comm=_chips_comm([gf0["sum_up"]]))
    grads["a_a_log"] = _ungroup_lanes(dalog, hg)
    grads["a_d_skip"] = _ungroup_lanes(ddsk, hg)
    grads["a_dt_bias"] = _ungroup_lanes(dbias, hg)
    dzx, dcw, dcb = _conv_silu_bwd(zx, dxs, dbm, dcm, a_conv_w, a_conv_b, dzx, "conv_a_bwd")
    grads["a_conv_w"], grads["a_conv_b"] = dcw[None], dcb
    dw_in_all, (p_dn0,) = _mm(hn0, dzx, "tn", BF16, "mm_dwin", comm=_chips_comm([gf0["sum_down"]]))
    dw_in8 = _scatter_cols(dw_in_all, ws_in, segs_in, "scat_w_in")
    dhn0, (s_in, p_out) = _mm(dzx, w_in_all, "nt", F32, "mm_dhn0",
                              comm=_join_comms([_swap_comm([dw_in8]), _chips_comm([sum_out])]))
    sum_in = _add_pairs(dw_in8, s_in, core, "rs_add_in")
    half_in = sum_in.shape[1] // 2
    (grads["meta_tokens"], g_x, grads["a_norm_pre"]), (p_in_a,) = _norm_bwd(
        h0, full["a_norm_pre"], dhn0, dh1, F32, "nb_apre", comm=_chips_comm([sum_in[:, :half_in]]), split_rows=SEQ)
    grad_x = g_x[None]

    small_local = _pack([_split8(grads[n], SHARD_AXIS[n], wts[n].shape[SHARD_AXIS[n]]) for n in SMALL], 1, 8, F32)
    repl_local = _pack([grads[n] for n in REPL], 0, 8, F32)
    n_sr = small_local.shape[1]
    small_vec = jnp.concatenate([small_local.reshape(N_DEV * n_sr, LANE), repl_local], axis=0)
    tail = _join_comms([_chips_comm([sum_in[:, half_in:]]), _gather_comm([small_vec])])
    parts_big = dict(a_w_out=[p_out], w_kv=[p_kv], b_w_q=[p_q], b_w_o=[p_o], f_w_down=[p_dn0, p_dn1])

    def flat_f32(dct, names, mult):
        return _pack([dct[n] for n in names], 0, mult, F32)

    def adamw_big(n, comm=None):
        shp3 = (len(parts_big[n]),) + parts_big[n][0].shape[1:]
        res = _adamw(parts_big[n], *[dct[n].reshape(shp3) for dct in (wts, mom, var)], f"adamw_{n}", comm=comm)
        res, got = res if comm is not None else (res, None)
        big_out[n] = [r.reshape(wts[n].shape) for r in res]
        return got

    big_out = {}
    def swap_last(a):
        return jnp.swapaxes(a, -1, -2)

    g_up_t = swap_last(_sum_parts([p_up0, p_up1], "sum_w_up").reshape(f_w_up.shape))
    res, (p_in_b, small_all) = _adamw([g_up_t[0:1], g_up_t[1:2]], *[swap_last(dct["f_w_up"]) for dct in (wts, mom, var)],
                                      "adamw_f_w_up", comm=tail)
    big_out["f_w_up"] = [swap_last(r) for r in res]
    for n in BIG:
        if n not in ("f_w_up", "a_w_in"):
            adamw_big(n)
    g_in_t = swap_last(_sum_parts([p_in_a, p_in_b], "sum_w_in"))[None]
    res = _adamw([g_in_t], *[swap_last(dct["a_w_in"]) for dct in (wts, mom, var)], "adamw_a_w_in")
    big_out["a_w_in"] = [swap_last(r) for r in res]
    mine_small = lax.dynamic_slice_in_dim(small_all, me * n_sr, n_sr, axis=1)
    parts_small = jnp.concatenate([mine_small, small_all[:, N_DEV * n_sr:]], axis=1)
    sm_in = [jnp.concatenate([flat_f32(dct, SMALL, 8), flat_f32(dct, REPL, 8)], axis=0)[None] for dct in (wts, mom, var)]
    small_out = [r[0] for r in _adamw([parts_small], *sm_in, "adamw_small")]

    outs = []
    for kind in range(4):
        res = {n: big_out[n][kind] for n in BIG}
        for n, a in zip(SMALL, _unpack(small_out[kind][:n_sr], 0, [wts[n].shape for n in SMALL])):
            res[n] = a
        for n, a in zip(REPL, _unpack(small_out[kind][n_sr:], 0, [wts[n].shape for n in REPL])):
            res[n] = a
        outs.append(res)
    return (loss, grad_x, *[outs[0][n] for n in WEIGHTS], *[outs[1][n] for n in WEIGHTS],
            *[outs[2][n] for n in WEIGHTS], *[outs[3][n] for n in WEIGHTS])
```

```python
import functools
import math

import jax
import jax.numpy as jnp
from jax import lax
from jax.experimental import pallas as pl
from jax.experimental.pallas import tpu as pltpu

F32, BF16 = jnp.float32, jnp.bfloat16
S = jax.ShapeDtypeStruct

D_MODEL = 1024
SEQ = 2048
N_META = 16
D_INNER = 2048
HEAD_P = 64
SSM_HEADS = D_INNER // HEAD_P
SSM_GROUPS = 4
D_STATE = 128
SSM_CONV = 4
D_BC = SSM_GROUPS * D_STATE
D_XBC = D_INNER + 2 * D_BC
ATTN_DH = 64
N_Q_HEADS = D_MODEL // ATTN_DH
N_KV_HEADS = 4
D_KV = N_KV_HEADS * ATTN_DH
WINDOW = 128
D_FF = 2816
FFN_CONV = 3
RMS_EPS = 1e-6
NEG = -1e30
LR, B1, B2, EPS, WD, STEP = 0.001, 0.9, 0.999, 1e-08, 0.01, 10

N_DEV = 8
T = 128
LANE = 128
VMEM_LIMIT = 48 * 1024 * 1024

BIG = ("a_w_in", "a_w_out", "w_kv", "b_w_q", "b_w_o", "f_w_up", "f_w_down")
SMALL = ("meta_tokens", "a_norm_pre", "a_conv_w", "a_conv_b", "a_gate_norm", "a_norm_post", "f_conv_w")
REPL = ("a_dt_bias", "a_a_log", "a_d_skip", "kv_norm", "b_norm_pre", "b_sinks", "b_norm_post",
        "f_norm_pre", "f_conv_b", "f_norm_post")
SHARD_AXIS = dict(a_w_in=2, a_w_out=1, w_kv=0, b_w_q=1, b_w_o=1, f_w_up=2, f_w_down=1, meta_tokens=1,
                  a_norm_pre=1, a_conv_w=2, a_conv_b=1, a_gate_norm=1, a_norm_post=1, f_conv_w=2)
WEIGHTS = ("meta_tokens", "a_norm_pre", "a_w_in", "a_conv_w", "a_conv_b", "a_dt_bias", "a_a_log", "a_d_skip",
           "a_gate_norm", "a_w_out", "a_norm_post", "kv_norm", "w_kv", "b_norm_pre", "b_w_q", "b_sinks", "b_w_o",
           "b_norm_post", "f_norm_pre", "f_w_up", "f_conv_w", "f_conv_b", "f_w_down", "f_norm_post")


def _seq_rows():
    return -(-(N_META + SEQ) // T) * T


def _cp(sem=None):
    return pltpu.CompilerParams(dimension_semantics=sem, vmem_limit_bytes=VMEM_LIMIT)


def _pick(n, target):
    t = min(n, target)
    t -= t % LANE
    while n % t:
        t -= LANE
    return t


def _sigmoid(x):
    return 0.5 * jnp.tanh(0.5 * x) + 0.5


def _softplus(x):
    return jnp.maximum(x, 0.0) + jnp.log(1.0 + jnp.exp(-jnp.abs(x)))


_NN = (((1,), (0,)), ((), ()))
_NT = (((1,), (1,)), ((), ()))
_TN = (((0,), (0,)), ((), ()))


def _dot(a, b, dims=_NN):
    return lax.dot_general(a, b, dims, preferred_element_type=F32)


def _dot_hi(a, b):
    return lax.dot_general(a, b, _NN, precision=lax.Precision.HIGHEST, preferred_element_type=F32)


_HBM = pl.BlockSpec(memory_space=pltpu.HBM)
_MESH = pl.DeviceIdType.MESH


class _Comm:
    def __init__(self, ins, out_shapes, scratch, first, last):
        self.ins, self.out_shapes, self.scratch, self.first, self.last = ins, out_shapes, scratch, first, last


_ANY = pl.BlockSpec(memory_space=pl.ANY)


def _call(body, name, out_shape, grid, in_specs, out_specs, sem, args, scratch=(), comm=None, aliases=None):
    aliases = aliases or {}
    if comm is None:
        return pl.pallas_call(body, name=name, out_shape=out_shape, grid=grid, in_specs=in_specs, out_specs=out_specs,
                              scratch_shapes=list(scratch), input_output_aliases=aliases,
                              compiler_params=_cp(sem))(*args)
    single = not isinstance(out_shape, (list, tuple))
    outs = [out_shape] if single else list(out_shape)
    ospecs = [out_specs] if single else list(out_specs)
    n_in, n_out, n_scr, ci, co = len(in_specs), len(outs), len(scratch), len(comm.ins), len(comm.out_shapes)

    def carrier(*refs):
        p = 0
        parts = []
        for cnt in (n_in, ci, n_out, co, n_scr, len(comm.scratch)):
            parts.append(refs[p:p + cnt])
            p += cnt
        ins, cins, outs_r, couts, scr, cscr = parts
        ids = [pl.program_id(i) for i in range(len(grid))]
        first, last = ids[0] == 0, ids[0] == grid[0] - 1
        for i in range(1, len(grid)):
            first, last = first & (ids[i] == 0), last & (ids[i] == grid[i] - 1)

        @pl.when(first)
        def _():
            comm.first(cins, couts, cscr)

        body(*ins, *outs_r, *scr)

        @pl.when(last)
        def _():
            comm.last(cins, couts, cscr)

    res = pl.pallas_call(
        carrier, name=name, out_shape=outs + list(comm.out_shapes), grid=grid,
        in_specs=list(in_specs) + [_HBM] * ci, out_specs=ospecs + [_HBM] * co,
        scratch_shapes=list(scratch) + list(comm.scratch), input_output_aliases=aliases,
        compiler_params=_cp(("arbitrary",) * len(grid)))(*args, *comm.ins)
    mine = res[0] if single else list(res[:n_out])
    return mine, list(res[n_out:])


def _mm(a, b, mode, out_dtype, name, comm=None, shard_cols=None):
    if mode == "tn":
        m, kk = a.shape
        planes, width = (b.shape[0], b.shape[2]) if b.ndim == 3 else (1, b.shape[1])
        n = planes * width
        tko, tn = _pick(kk, 512), _pick(width, 1536)
        per = width // tn
        b_spec = (pl.BlockSpec((None, m, tn), lambda i, j: (j // per, 0, j % per)) if b.ndim == 3
                  else pl.BlockSpec((m, tn), lambda i, j: (0, j)))
        if shard_cols is None:
            def body(a_ref, b_ref, o_ref):
                o_ref[...] = _dot(a_ref[...], b_ref[...], _TN).astype(o_ref.dtype)

            out_shape, out_spec = S((kk, n), out_dtype), pl.BlockSpec((tko, tn), lambda i, j: (i, j))
        else:
            shards = tn // shard_cols
            assert tn % shard_cols == 0

            def body(a_ref, b_ref, o_ref):
                res = _dot(a_ref[...], b_ref[...], _TN).astype(o_ref.dtype)
                for p in range(shards):
                    o_ref[p] = res[:, p * shard_cols:(p + 1) * shard_cols]

            out_shape = S((n // shard_cols, kk, shard_cols), out_dtype)
            out_spec = pl.BlockSpec((shards, tko, shard_cols), lambda i, j: (j, i, 0))
        return _call(
            body, name, out_shape, (kk // tko, n // tn), [pl.BlockSpec((m, tko), lambda i, j: (0, i)), b_spec],
            out_spec, ("parallel", "parallel"), (a, b), comm=comm)

    planes, width = (a.shape[0], a.shape[2]) if a.ndim == 3 else (1, a.shape[1])
    m, kk = a.shape[-2], planes * width
    n = b.shape[1] if mode == "nn" else b.shape[0]
    dims = _NN if mode == "nn" else _NT

    if kk > 2048:
        tm = m // 4
        assert m % 4 == 0 and tm % 16 == 0

        def body(a_ref, b_ref, o_ref):
            if a.ndim == 2:
                res = _dot(a_ref[...], b_ref[...], dims)
            else:
                res = None
                for p in range(planes):
                    bp = b_ref[p * width:(p + 1) * width, :] if mode == "nn" else b_ref[:, p * width:(p + 1) * width]
                    part = _dot(a_ref[p], bp, dims)
                    res = part if res is None else res + part
            o_ref[...] = res.astype(o_ref.dtype)

        a_spec = (pl.BlockSpec((planes, tm, width), lambda i: (0, i, 0)) if a.ndim == 3
                  else pl.BlockSpec((tm, kk), lambda i: (i, 0)))
        return _call(
            body, name, S((m, n), out_dtype), (m // tm,),
            [a_spec, pl.BlockSpec(b.shape, lambda i: (0, 0), pipeline_mode=pl.Buffered(1))],
            pl.BlockSpec((tm, n), lambda i: (i, 0)), ("parallel",), (a, b), comm=comm)

    tn = _pick(n, 512)

    def body(a_ref, b_ref, o_ref):
        o_ref[...] = _dot(a_ref[...], b_ref[...], dims).astype(o_ref.dtype)

    b_spec = (pl.BlockSpec((kk, tn), lambda j: (0, j)) if mode == "nn" else pl.BlockSpec((tn, kk), lambda j: (j, 0)))
    return _call(
        body, name, S((m, n), out_dtype), (n // tn,), [pl.BlockSpec((m, kk), lambda j: (0, 0)), b_spec],
        pl.BlockSpec((m, tn), lambda j: (0, j)), ("parallel",), (a, b), comm=comm)


def _rms(x, w):
    return x * lax.rsqrt(jnp.mean(x * x, axis=-1, keepdims=True) + RMS_EPS) * w


def _row_tile(rows, d):
    return rows // 4 if d <= 1024 and (rows // 4) % 16 == 0 else rows // 8


def _embed_norm(meta, x, w, rows, name, comm=None):
    n_meta, d = meta.shape
    n_x = x.shape[0]
    last = rows // T - 1
    assert n_meta % 8 == 0 and n_meta < T and n_meta + n_x == last * T + n_meta and last * T >= n_x

    def body(m_ref, x_ref, w_ref, h_ref, hn_ref):
        i = pl.program_id(0)

        @pl.when(i == 0)
        def _():
            h_ref[0:n_meta, :] = m_ref[...]
            h_ref[n_meta:T, :] = x_ref[0:T - n_meta, :]

        @pl.when((i > 0) & (i < last))
        def _():
            h_ref[...] = x_ref[pl.ds(pl.multiple_of(i * T - n_meta, 8), T), :]

        @pl.when(i == last)
        def _():
            h_ref[0:n_meta, :] = x_ref[n_x - n_meta:n_x, :]
            h_ref[n_meta:T, :] = jnp.zeros((T - n_meta, d), F32)

        hn_ref[...] = _rms(h_ref[...], w_ref[...]).astype(hn_ref.dtype)

    row = pl.BlockSpec((T, d), lambda i: (i, 0))
    return _call(body, name, [S((rows, d), F32), S((rows, d), BF16)], (rows // T,),
                 [pl.BlockSpec((n_meta, d), lambda i: (0, 0)), pl.BlockSpec((n_x, d), lambda i: (0, 0)),
                  pl.BlockSpec((1, d), lambda i: (0, 0))], [row, row], ("parallel",), (meta, x, w), comm=comm)


def _resid_norm(h, br, w_post, next_ws, name):
    rows, d = h.shape
    tr = _row_tile(rows, d)
    has_br = br is not None
    nw = len(next_ws)

    def body(*refs):
        h_ref = refs[0]
        pos = 1
        x = h_ref[...]
        if has_br:
            x = x + _rms(refs[1][...], refs[2][...])
            pos = 3
        w_refs = refs[pos:pos + nw]
        outs = refs[pos + nw:]
        if has_br:
            outs[0][...] = x
            outs = outs[1:]
        for w_ref, o_ref in zip(w_refs, outs):
            o_ref[...] = _rms(x, w_ref[...]).astype(o_ref.dtype)

    row = pl.BlockSpec((tr, d), lambda i: (i, 0))
    vec = pl.BlockSpec((1, d), lambda i: (0, 0))
    ins = [h] + ([br, w_post] if has_br else []) + list(next_ws)
    in_specs = [row] + ([row, vec] if has_br else []) + [vec] * nw
    out_shape = ([S((rows, d), F32)] if has_br else []) + [S((rows, d), BF16)] * nw
    res = pl.pallas_call(body, name=name, out_shape=out_shape, grid=(rows // tr,), in_specs=in_specs,
                         out_specs=[row] * len(out_shape), compiler_params=_cp(("parallel",)))(*ins)
    if has_br:
        return res[0], list(res[1:])
    return h, list(res)


def _rms_bwd(xv, w, dyv):
    r = lax.rsqrt(jnp.mean(xv * xv, axis=-1, keepdims=True) + RMS_EPS)
    wdy = dyv * w
    dx = r * wdy - xv * (r * r * r) * jnp.mean(xv * wdy, axis=-1, keepdims=True)
    return dx, jnp.sum(dyv * xv * r, axis=0, keepdims=True)


def _norm_bwd(x, w, dy, add, out_dtype, name, comm=None, then=None, split_rows=None):
    rows, d = x.shape
    tr = _row_tile(rows, d)
    has_add = add is not None
    n_in = 3 + has_add + (2 if then is not None else 0)
    if split_rows is not None:
        last, tail = _real_rows(rows, tr, split_rows)

    def body(*refs):
        x_ref, w_ref, dy_ref = refs[:3]
        outs = refs[n_in:]
        dx, dw = _rms_bwd(x_ref[...], w_ref[...], dy_ref[...].astype(F32))
        if has_add:
            dx = dx + refs[3][...]
        if split_rows is None:
            outs[0][...] = dx.astype(outs[0].dtype)
        else:
            i = pl.program_id(0)
            gm_ref, gx_ref = outs[0], outs[1]
            outs = outs[1:]

            @pl.when(i == 0)
            def _():
                gm_ref[...] = dx[0:N_META, :]
                gx_ref[0:tr - N_META, :] = dx[N_META:tr, :]

            @pl.when((i > 0) & (i < last))
            def _():
                gx_ref[pl.ds(pl.multiple_of(i * tr - N_META, 8), tr), :] = dx

            @pl.when(i == last)
            def _():
                gx_ref[split_rows - tail:split_rows, :] = dx[0:tail, :]
        first = pl.program_id(0) == 0

        @pl.when(first)
        def _():
            outs[1][...] = jnp.zeros_like(outs[1])

        outs[1][...] += dw
        if then is not None:
            dx2, dw2 = _rms_bwd(refs[n_in - 2][...], refs[n_in - 1][...], dx)
            outs[2][...] = dx2.astype(outs[2].dtype)

            @pl.when(first)
            def _():
                outs[3][...] = jnp.zeros_like(outs[3])

            outs[3][...] += dw2

    row = pl.BlockSpec((tr, d), lambda i: (i, 0))
    vec = pl.BlockSpec((1, d), lambda i: (0, 0))
    ins = [x, w, dy] + ([add] if has_add else []) + (list(then) if then is not None else [])
    in_specs = [row, vec, row] + ([row] if has_add else []) + ([row, vec] if then is not None else [])
    out_shape = [S((rows, d), out_dtype), S((1, d), F32)] + ([S((rows, d), BF16), S((1, d), F32)] if then is not None else [])
    out_specs = [row, vec] * (len(out_shape) // 2)
    if split_rows is not None:
        out_shape = [S((N_META, d), F32), S((split_rows, d), F32)] + out_shape[1:]
        out_specs = [pl.BlockSpec((N_META, d), lambda i: (0, 0)), pl.BlockSpec((split_rows, d), lambda i: (0, 0))] + out_specs[1:]
    return _call(body, name, out_shape, (rows // tr,), in_specs, out_specs, ("arbitrary",), ins, comm=comm)


def _real_rows(rows, tr, n_x):
    last = (N_META + n_x - 1) // tr
    tail = N_META + n_x - last * tr
    assert last == rows // tr - 1 and N_META % 8 == 0 and tail % 8 == 0 and N_META < tr
    return last, tail


def _final_loss(h, br, w_post, target, name):
    rows, d = h.shape
    tr = _row_tile(rows, d)
    n_x = target.shape[0]
    last, tail = _real_rows(rows, tr, n_x)

    def body(h_ref, br_ref, w_ref, t_ref, dh_ref, loss_ref, dbr_ref, dw_ref, tbuf):
        i = pl.program_id(0)

        @pl.when(i == 0)
        def _():
            tbuf[0:N_META, :] = jnp.zeros((N_META, d), F32)
            tbuf[N_META:tr, :] = t_ref[0:tr - N_META, :]

        @pl.when((i > 0) & (i < last))
        def _():
            tbuf[...] = t_ref[pl.ds(pl.multiple_of(i * tr - N_META, 8), tr), :]

        @pl.when(i == last)
        def _():
            tbuf[0:tail, :] = t_ref[n_x - tail:n_x, :]
            if tail < tr:
                tbuf[tail:tr, :] = jnp.zeros((tr - tail, d), F32)

        brv, wv = br_ref[...], w_ref[...]
        y = h_ref[...] + _rms(brv, wv)
        r = i * tr + lax.broadcasted_iota(jnp.int32, (tr, 1), 0)
        real = (r >= N_META) & (r < N_META + SEQ)
        diff = jnp.where(real, y - tbuf[...], 0.0)
        dh = diff * (1.0 / d)
        dh_ref[...] = dh
        dbr, dw = _rms_bwd(brv, wv, dh)
        dbr_ref[...] = dbr.astype(dbr_ref.dtype)

        @pl.when(i == 0)
        def _():
            loss_ref[...] = jnp.zeros_like(loss_ref)
            dw_ref[...] = jnp.zeros_like(dw_ref)

        loss_ref[...] += jnp.sum(diff * diff) * (0.5 / d)
        dw_ref[...] += dw

    row = pl.BlockSpec((tr, d), lambda i: (i, 0))
    vec = pl.BlockSpec((1, d), lambda i: (0, 0))
    return pl.pallas_call(body, name=name,
                          out_shape=[S((rows, d), F32), S((1, LANE), F32), S((rows, d), BF16), S((1, d), F32)],
                          grid=(rows // tr,), in_specs=[row, row, vec, pl.BlockSpec((n_x, d), lambda i: (0, 0))],
                          out_specs=[row, pl.BlockSpec((1, LANE), lambda i: (0, 0)), row, vec],
                          scratch_shapes=[pltpu.VMEM((tr, d), F32)],
                          compiler_params=_cp(("arbitrary",)))(h, br, w_post, target)


def _gatenorm_fwd(y, zx, w, name, comm=None):
    rows, d = y.shape
    tr = _row_tile(rows, d)

    def body(y_ref, z_ref, w_ref, o_ref):
        z = z_ref[...]
        o_ref[...] = _rms(y_ref[...] * z * _sigmoid(z), w_ref[...]).astype(o_ref.dtype)

    row = pl.BlockSpec((tr, d), lambda i: (i, 0))
    return _call(body, name, S((rows, d), BF16), (rows // tr,), [row, row, pl.BlockSpec((1, d), lambda i: (0, 0))],
                 row, ("parallel",), (y, zx, w), comm=comm)


def _gatenorm_bwd(y, zx, w, dyn, name, comm=None):
    rows, d = y.shape
    tr = _row_tile(rows, d)

    def body(y_ref, z_ref, w_ref, dyn_ref, dy_ref, dz_ref, dw_ref):
        yv, z = y_ref[...], z_ref[...]
        sg = _sigmoid(z)
        sz = z * sg
        g = yv * sz
        r = lax.rsqrt(jnp.mean(g * g, axis=-1, keepdims=True) + RMS_EPS)
        dyn_v = dyn_ref[...]
        wdy = dyn_v * w_ref[...]
        dg = r * wdy - g * (r * r * r) * jnp.mean(g * wdy, axis=-1, keepdims=True)
        dy_ref[...] = dg * sz
        dz_ref[...] = (dg * yv * sg * (1.0 + z * (1.0 - sg))).astype(dz_ref.dtype)

        @pl.when(pl.program_id(0) == 0)
        def _():
            dw_ref[...] = jnp.zeros_like(dw_ref)

        dw_ref[...] += jnp.sum(dyn_v * g * r, axis=0, keepdims=True)

    row = pl.BlockSpec((tr, d), lambda i: (i, 0))
    vec = pl.BlockSpec((1, d), lambda i: (0, 0))
    return _call(body, name, [S((rows, d), F32), S((rows, zx.shape[1]), BF16), S((1, d), F32)], (rows // tr,),
                 [row, row, vec, row], [row, row, vec], ("arbitrary",), (y, zx, w, dyn), comm=comm)


def _shift_down(x, s, rows_iota):
    if s == 0:
        return x
    return jnp.where(rows_iota >= s, pltpu.roll(x, s, 0), 0.0)


def _shift_up(x, s, rows_iota):
    if s == 0:
        return x
    rows = x.shape[0]
    return jnp.where(rows_iota < rows - s, pltpu.roll(x, rows - s, 0), 0.0)


def _r16(v):
    return v.astype(BF16).astype(F32)


def _conv_taps(x, taps, rows_iota):
    x = _r16(x)
    return [_shift_down(x, taps - 1 - k, rows_iota) for k in range(taps)]


def _conv(x, w_ref, b_ref, taps, rows_iota, shifted=None):
    shifted = _conv_taps(x, taps, rows_iota) if shifted is None else shifted
    acc = jnp.zeros_like(shifted[0])
    for k in range(taps):
        acc = acc + _r16(w_ref[k:k + 1, :]) * shifted[k]
    return acc + b_ref[...]


def _conv_bwd(shifted, du, w_ref, dw_ref, db_ref, taps, rows_iota):
    db_ref[...] = jnp.sum(du, axis=0, keepdims=True)
    du = _r16(du)
    dx = jnp.zeros_like(du)
    for k in range(taps):
        dx = dx + _r16(w_ref[k:k + 1, :]) * _shift_up(du, taps - 1 - k, rows_iota)
        dw_ref[k:k + 1, :] = jnp.sum(du * shifted[k], axis=0, keepdims=True)
    return dx


def _conv_silu_fwd(zx, w, b, name, comm=None):
    rows = zx.shape[0]
    cb = 512
    off = D_INNER // cb

    def body(x_ref, w_ref, b_ref, o_ref):
        it = lax.broadcasted_iota(jnp.int32, (rows, 1), 0)
        u = _conv(x_ref[...], w_ref, b_ref, SSM_CONV, it)
        o_ref[...] = u * _sigmoid(u)

    return _call(
        body, name, S((rows, D_XBC), F32), (D_XBC // cb,),
        [pl.BlockSpec((rows, cb), lambda j: (0, off + j)), pl.BlockSpec((SSM_CONV, cb), lambda j: (0, j)),
         pl.BlockSpec((1, cb), lambda j: (0, j))],
        pl.BlockSpec((rows, cb), lambda j: (0, j)), ("parallel",), (zx, w, b), comm=comm)


def _conv_silu_bwd(zx, dxs, dbm, dcm, w, b, dzx, name, comm=None):
    rows = zx.shape[0]
    cb = 256
    off = D_INNER // cb
    nx, nbc = D_INNER // cb, D_BC // cb

    def body(x_ref, dx_in, db_in, dc_in, w_ref, b_ref, dzx_in, dx_ref, dw_ref, db_ref, dbuf):
        del dzx_in
        j = pl.program_id(0)
        for cond, src in ((j < nx, dx_in), ((j >= nx) & (j < nx + nbc), db_in), (j >= nx + nbc, dc_in)):
            @pl.when(cond)
            def _(src=src):
                dbuf[...] = src[...]
        it = lax.broadcasted_iota(jnp.int32, (rows, 1), 0)
        xs = _conv_taps(x_ref[...], SSM_CONV, it)
        u = _conv(None, w_ref, b_ref, SSM_CONV, it, xs)
        sg = _sigmoid(u)
        du = dbuf[...] * sg * (1.0 + u * (1.0 - sg))
        dx_ref[...] = _conv_bwd(xs, du, w_ref, dw_ref, db_ref, SSM_CONV, it).astype(dx_ref.dtype)

    def part(first, count):
        return pl.BlockSpec((rows, cb), lambda j: (0, jnp.clip(j - first, 0, count - 1)))

    col = pl.BlockSpec((rows, cb), lambda j: (0, j))
    wsp = pl.BlockSpec((SSM_CONV, cb), lambda j: (0, j))
    bsp = pl.BlockSpec((1, cb), lambda j: (0, j))
    xbc_cols = pl.BlockSpec((rows, cb), lambda j: (0, off + j))
    return _call(
        body, name, [S(dzx.shape, dzx.dtype), S((SSM_CONV, D_XBC), F32), S((1, D_XBC), F32)], (D_XBC // cb,),
        [xbc_cols, part(0, nx), part(nx, nbc), part(nx + nbc, nbc), wsp, bsp, _ANY],
        [xbc_cols, wsp, bsp], ("arbitrary",), (zx, dxs, dbm, dcm, w, b, dzx), scratch=[pltpu.VMEM((rows, cb), F32)],
        comm=comm, aliases={6: 0})


def _ffn_act_fwd(u, w, b, name, comm=None):
    rows = u.shape[0]
    cb = 256
    nb = D_FF // cb

    def body(g_ref, v_ref, wg_ref, wv_ref, bg_ref, bv_ref, o_ref):
        it = lax.broadcasted_iota(jnp.int32, (rows, 1), 0)
        g = _conv(g_ref[...], wg_ref, bg_ref, FFN_CONV, it)
        v = _conv(v_ref[...], wv_ref, bv_ref, FFN_CONV, it)
        o_ref[...] = (g * _sigmoid(g) * v).astype(o_ref.dtype)

    def sp(r, shift):
        return pl.BlockSpec((r, cb), lambda j: (0, shift + j))

    return _call(
        body, name, S((rows, D_FF), BF16), (nb,),
        [sp(rows, 0), sp(rows, nb), sp(FFN_CONV, 0), sp(FFN_CONV, nb), sp(1, 0), sp(1, nb)],
        sp(rows, 0), ("parallel",), (u, u, w, w, b, b), comm=comm)


def _ffn_act_bwd(u, dact, w, b, name, comm=None):
    rows = u.shape[0]
    cb = 256
    nb = D_FF // cb

    def body(g_ref, v_ref, d_ref, wg_ref, wv_ref, bg_ref, bv_ref, du_ref, dw_ref, db_ref):
        it = lax.broadcasted_iota(jnp.int32, (rows, 1), 0)
        xg, xv = _conv_taps(g_ref[...], FFN_CONV, it), _conv_taps(v_ref[...], FFN_CONV, it)
        g = _conv(None, wg_ref, bg_ref, FFN_CONV, it, xg)
        v = _conv(None, wv_ref, bv_ref, FFN_CONV, it, xv)
        sg = _sigmoid(g)
        d = d_ref[...]
        dgate = d * v * sg * (1.0 + g * (1.0 - sg))
        dval = d * g * sg
        du_ref[0] = _conv_bwd(xg, dgate, wg_ref, dw_ref.at[0], db_ref.at[0], FFN_CONV, it).astype(du_ref.dtype)
        du_ref[1] = _conv_bwd(xv, dval, wv_ref, dw_ref.at[1], db_ref.at[1], FFN_CONV, it).astype(du_ref.dtype)

    def sp(r, shift):
        return pl.BlockSpec((r, cb), lambda j: (0, shift + j))

    def both(r):
        return pl.BlockSpec((2, r, cb), lambda j: (0, 0, j))

    return _call(
        body, name, [S((2, rows, D_FF), BF16), S((2, FFN_CONV, D_FF), F32), S((2, 1, D_FF), F32)], (nb,),
        [sp(rows, 0), sp(rows, nb), sp(rows, 0), sp(FFN_CONV, 0), sp(FFN_CONV, nb), sp(1, 0), sp(1, nb)],
        [both(rows), both(FFN_CONV), both(1)], ("parallel",), (u, u, dact, w, w, b, b), comm=comm)


def _ssd_consts(dtp_ref, bias_ref, alog_ref, hg):
    lane = lax.broadcasted_iota(jnp.int32, (1, LANE), 1)
    pre = dtp_ref[...] + bias_ref[...]
    dt = _softplus(pre)
    a_row = jnp.where(lane < hg, -jnp.exp(alog_ref[...]), 0.0)
    ri = lax.broadcasted_iota(jnp.int32, (T, T), 0)
    ci = lax.broadcasted_iota(jnp.int32, (T, T), 1)
    cs = _dot_hi((ri >= ci).astype(F32), dt * a_row)
    return pre, dt, a_row, cs, ri, ci, lane


def _head_rows(src, hg):
    return jnp.concatenate([jnp.broadcast_to(src[k:k + 1, :], (HEAD_P, src.shape[1])) for k in range(hg)], axis=0)


def _ssd_fwd(xbc, zx, bias, alog, dsk, name, comm=None):
    rows = xbc.shape[0]
    nc = rows // T
    hg = SSM_HEADS // SSM_GROUPS
    gw = hg * HEAD_P
    xoff, boff, coff = 0, D_INNER // D_STATE, (D_INNER + D_BC) // D_STATE
    dtoff = (D_INNER + D_XBC) // LANE

    def body(x_ref, b_ref, c_ref, dtp_ref, bias_ref, alog_ref, dsk_ref, y_ref, hst_ref, hs):
        c = pl.program_id(1)

        @pl.when(c == 0)
        def _():
            hs[...] = jnp.zeros_like(hs)

        _, dt, _, cs, ri, ci, _ = _ssd_consts(dtp_ref, bias_ref, alog_ref, hg)
        cst, dtt = cs.T, dt.T
        xt = x_ref[...].T
        bb, cbf = b_ref[...].astype(BF16), c_ref[...].astype(BF16)
        gt = _dot(bb, cbf, _NT)
        causal_t = ci >= ri
        dskv = dsk_ref[...]
        hall = hs[...]
        hst_ref[0, 0] = hall
        cs8 = cst[0:8, :]
        cl8 = cs8[:, T - 1:T]
        xdt = xt * _head_rows(dtt, hg)
        yo = _head_rows(jnp.exp(cs8), hg) * _dot(hall.astype(BF16), cbf, _NT)
        st = _dot((xdt * _head_rows(jnp.exp(cl8 - cs8), hg)).astype(BF16), bb)
        hs[...] = _head_rows(jnp.exp(cl8), hg) * hall + st
        yds = []
        for k in range(hg):
            sl = slice(k * HEAD_P, (k + 1) * HEAD_P)
            lt = jnp.exp(jnp.where(causal_t, cst[k:k + 1, :] - cs[:, k:k + 1], NEG))
            yds.append(_dot(xdt[sl, :].astype(BF16), (gt * lt).astype(BF16)))
        dsk_r = jnp.concatenate([jnp.broadcast_to(dskv[:, k:k + 1], (HEAD_P, 1)) for k in range(hg)], axis=0)
        y_ref[...] = (jnp.concatenate(yds, axis=0) + yo + dsk_r * xt).T

    vec = pl.BlockSpec((1, LANE), lambda g, c: (0, g))
    return _call(
        body, name, [S((rows, D_INNER), F32), S((nc, SSM_GROUPS, gw, D_STATE), F32)], (SSM_GROUPS, nc),
        [pl.BlockSpec((T, gw), lambda g, c: (c, xoff + g)),
         pl.BlockSpec((T, D_STATE), lambda g, c: (c, boff + g)),
         pl.BlockSpec((T, D_STATE), lambda g, c: (c, coff + g)),
         pl.BlockSpec((T, LANE), lambda g, c: (c, dtoff + g)), vec, vec, vec],
        [pl.BlockSpec((T, gw), lambda g, c: (c, g)), pl.BlockSpec((1, 1, gw, D_STATE), lambda g, c: (c, g, 0, 0))],
        ("parallel", "arbitrary"), (xbc, xbc, xbc, zx, bias, alog, dsk),
        scratch=[pltpu.VMEM((gw, D_STATE), F32)], comm=comm)


def _ssd_bwd(xbc, zx, bias, alog, dsk, dy, hst, dzx, name, comm=None):
    rows = xbc.shape[0]
    nc = rows // T
    hg = SSM_HEADS // SSM_GROUPS
    gw = hg * HEAD_P
    boff, coff = D_INNER // D_STATE, (D_INNER + D_BC) // D_STATE
    dtoff = (D_INNER + D_XBC) // LANE

    def body(x_ref, b_ref, c_ref, dtp_ref, bias_ref, alog_ref, dsk_ref, dy_ref, hst_ref, dzx_in,
             dx_ref, db_ref, dc_ref, ddtp_ref, dalog_ref, ddsk_ref, dbias_ref, dhs):
        del dzx_in
        step = pl.program_id(1)

        @pl.when(step == 0)
        def _():
            dhs[...] = jnp.zeros_like(dhs)
            dalog_ref[...] = jnp.zeros_like(dalog_ref)
            ddsk_ref[...] = jnp.zeros_like(ddsk_ref)
            dbias_ref[...] = jnp.zeros_like(dbias_ref)

        pre, dt, a_row, cs, ri, ci, lane = _ssd_consts(dtp_ref, bias_ref, alog_ref, hg)
        cst, dtt = cs.T, dt.T
        xt, dyt = x_ref[...].T, dy_ref[...].T
        bb, cbf = b_ref[...].astype(BF16), c_ref[...].astype(BF16)
        gt = _dot(bb, cbf, _NT)
        causal_t = ci >= ri
        dskv = dsk_ref[...]
        hall, dhall = hst_ref[0, 0], dhs[...]
        head_row = lax.broadcasted_iota(jnp.int32, (T, 1), 0)
        last_l = lax.broadcasted_iota(jnp.int32, (1, T), 1) == T - 1
        cs8, dt8 = cst[0:8, :], dtt[0:8, :]
        cl8 = cs8[:, T - 1:T]
        e8, wdec8 = jnp.exp(cs8), jnp.exp(cl8 - cs8)
        w8 = wdec8 * dt8
        dt_r, e_r, w_r, ecl_r = _head_rows(dt8, hg), _head_rows(e8, hg), _head_rows(w8, hg), _head_rows(jnp.exp(cl8), hg)
        dsk_r = jnp.concatenate([jnp.broadcast_to(dskv[:, k:k + 1], (HEAD_P, 1)) for k in range(hg)], axis=0)
        hb, dhb = hall.astype(BF16), dhall.astype(BF16)
        xdt = xt * dt_r
        dye = (dyt * e_r).astype(BF16)
        rt = _dot(dhb, bb, _NT)
        yo = e_r * _dot(hb, cbf, _NT)
        dhs[...] = ecl_r * dhall + _dot(dye, cbf)
        dc_acc = _dot(dye, hb, _TN)
        db_acc = _dot((xt * w_r).astype(BF16), dhb, _TN)
        rtx, dyyo, hdh, dyx = rt * xt, dyt * yo, dhall * hall, dyt * xt
        dgt = jnp.zeros((T, T), F32)
        ddt_rows = jnp.zeros((T, T), F32)
        dcs_rows = jnp.zeros((T, T), F32)
        qrow_cols = jnp.zeros((T, LANE), F32)
        ddsk_acc = jnp.zeros((1, LANE), F32)
        dxdts = []
        for k in range(hg):
            sl = slice(k * HEAD_P, (k + 1) * HEAD_P)
            lt = jnp.exp(jnp.where(causal_t, cst[k:k + 1, :] - cs[:, k:k + 1], NEG))
            mpt = gt * lt
            dyb = dyt[sl, :].astype(BF16)
            dxdt = _dot(dyb, mpt.astype(BF16), _NT)
            dmt = _dot(xdt[sl, :].astype(BF16), dyb, _TN)
            dgt = dgt + dmt * lt
            q = dmt * mpt
            q_rows = jnp.sum(q, axis=1, keepdims=True)
            q_cols = jnp.sum(q, axis=0, keepdims=True)
            dxdts.append(dxdt)
            xz = jnp.sum(xt[sl, :] * dxdt, axis=0, keepdims=True)
            dw = jnp.sum(rtx[sl, :], axis=0, keepdims=True)
            wk, wdeck = w8[k:k + 1, :], wdec8[k:k + 1, :]
            dcl = jnp.exp(cl8[k:k + 1, :]) * jnp.sum(hdh[sl, :]) + jnp.sum(dw * wk)
            dcs_r = jnp.sum(dyyo[sl, :], axis=0, keepdims=True) + q_cols - dw * wk + jnp.where(last_l, dcl, 0.0)
            onehot = (lane == k).astype(F32)
            ddt_rows = ddt_rows + jnp.where(head_row == k, xz + dw * wdeck, 0.0)
            dcs_rows = dcs_rows + jnp.where(head_row == k, dcs_r, 0.0)
            qrow_cols = qrow_cols + q_rows * onehot
            ddsk_acc = ddsk_acc + jnp.sum(dyx[sl, :]) * onehot
        dx_ref[...] = (dt_r * jnp.concatenate(dxdts, axis=0) + dsk_r * dyt + rt * w_r).T
        dc_ref[...] = _dot(dgt.T.astype(BF16), bb) + dc_acc
        db_ref[...] = _dot(dgt.astype(BF16), cbf) + db_acc
        da = _dot_hi((ci >= ri).astype(F32), dcs_rows.T - qrow_cols)
        ddtp = (ddt_rows.T + da * a_row) * _sigmoid(pre)
        ddtp = jnp.where(lane < hg, ddtp, 0.0)
        ddtp_ref[...] = ddtp.astype(ddtp_ref.dtype)
        dbias_ref[...] += jnp.sum(ddtp, axis=0, keepdims=True)
        dalog_ref[...] += jnp.sum(da * dt, axis=0, keepdims=True) * a_row
        ddsk_ref[...] += ddsk_acc

    def rc(c):
        return nc - 1 - c

    vec = pl.BlockSpec((1, LANE), lambda g, c: (0, g))
    xsp = pl.BlockSpec((T, gw), lambda g, c: (rc(c), g))
    return _call(
        body, name,
        [S((rows, D_INNER), F32), S((rows, D_BC), F32), S((rows, D_BC), F32),
         S(dzx.shape, dzx.dtype), S((1, SSM_GROUPS * LANE), F32),
         S((1, SSM_GROUPS * LANE), F32), S((1, SSM_GROUPS * LANE), F32)],
        (SSM_GROUPS, nc),
        [xsp,
         pl.BlockSpec((T, D_STATE), lambda g, c: (rc(c), boff + g)),
         pl.BlockSpec((T, D_STATE), lambda g, c: (rc(c), coff + g)),
         pl.BlockSpec((T, LANE), lambda g, c: (rc(c), dtoff + g)), vec, vec, vec,
         xsp, pl.BlockSpec((1, 1, gw, D_STATE), lambda g, c: (rc(c), g, 0, 0)), _ANY],
        [xsp,
         pl.BlockSpec((T, D_STATE), lambda g, c: (rc(c), g)),
         pl.BlockSpec((T, D_STATE), lambda g, c: (rc(c), g)),
         pl.BlockSpec((T, LANE), lambda g, c: (rc(c), dtoff + g)), vec, vec, vec],
        ("parallel", "arbitrary"), (xbc, xbc, xbc, zx, bias, alog, dsk, dy, hst, dzx),
        scratch=[pltpu.VMEM((gw, D_STATE), F32)], comm=comm, aliases={9: 3})


def _attn_tiles(kv_ref, j):
    prev = jnp.maximum(j - 1, 0)
    meta = kv_ref[0:T, :]
    prv = kv_ref[pl.ds(pl.multiple_of(prev * T, T), T), :]
    cur = kv_ref[pl.ds(pl.multiple_of(j * T, T), T), :]
    return jnp.concatenate([meta, prv, cur], axis=0)


def _attn_mask(j):
    r = j * T + lax.broadcasted_iota(jnp.int32, (3 * T, T), 1)
    row = lax.broadcasted_iota(jnp.int32, (3 * T, T), 0)
    t0, t1 = row < T, row < 2 * T
    s = jnp.where(t0, row, (j - 2) * T + row)
    ok = (s <= r) & ((s < N_META) | (s > r - WINDOW))
    use = (t0 & (j >= 2) & (row < N_META)) | (jnp.logical_not(t0) & t1 & (j >= 1)) | jnp.logical_not(t1)
    return ok & use


def _attn_fwd(q, kv, sinks, name, comm=None):
    rows = q.shape[0]
    scale = 1.0 / math.sqrt(ATTN_DH)
    qpk = N_Q_HEADS // N_KV_HEADS

    def body(q_ref, kv_ref, s_ref, o_ref, lse_ref):
        j = pl.program_id(0)
        kv3 = _attn_tiles(kv_ref, j).astype(BF16)
        mask = _attn_mask(j)
        qv = (q_ref[...] * scale).astype(BF16)
        sk = s_ref[...]
        lses = []
        for kh in range(N_KV_HEADS):
            k3 = kv3[:, kh * ATTN_DH:(kh + 1) * ATTN_DH]
            v3 = kv3[:, D_KV + kh * ATTN_DH:D_KV + (kh + 1) * ATTN_DH]
            for g in range(qpk):
                h = kh * qpk + g
                sink = sk[:, h:h + 1]
                sc = jnp.where(mask, _dot(k3, qv[:, h * ATTN_DH:(h + 1) * ATTN_DH], _NT), NEG)
                m = jnp.maximum(jnp.max(sc, axis=0, keepdims=True), sink)
                p = jnp.exp(sc - m)
                den = jnp.sum(p, axis=0, keepdims=True) + jnp.exp(sink - m)
                p = p * (1.0 / den)
                lses.append(m + jnp.log(den))
                o_ref[:, h * ATTN_DH:(h + 1) * ATTN_DH] = _dot(p.astype(BF16), v3, _TN).astype(o_ref.dtype)
        lse_ref[...] = jnp.concatenate(lses, axis=0)

    return _call(
        body, name, [S((rows, D_MODEL), BF16), S((N_Q_HEADS, rows), F32)], (rows // T,),
        [pl.BlockSpec((T, D_MODEL), lambda j: (j, 0)), pl.BlockSpec((rows, 2 * D_KV), lambda j: (0, 0)),
         pl.BlockSpec((1, N_Q_HEADS), lambda j: (0, 0))],
        [pl.BlockSpec((T, D_MODEL), lambda j: (j, 0)), pl.BlockSpec((N_Q_HEADS, T), lambda j: (0, j))],
        ("parallel",), (q, kv, sinks), comm=comm)


def _attn_bwd(q, kv, sinks, do, lse, name, comm=None):
    rows = q.shape[0]
    scale = 1.0 / math.sqrt(ATTN_DH)
    qpk = N_Q_HEADS // N_KV_HEADS

    def body(q_ref, kv_ref, s_ref, do_ref, lse_ref, dq_ref, dkv_ref, ds_ref):
        j = pl.program_id(0)

        @pl.when(j == 0)
        def _():
            dkv_ref[...] = jnp.zeros_like(dkv_ref)
            ds_ref[...] = jnp.zeros_like(ds_ref)

        kv3 = _attn_tiles(kv_ref, j).astype(BF16)
        mask = _attn_mask(j)
        qv = (q_ref[...] * scale).astype(BF16)
        dov = do_ref[...].astype(BF16)
        sk = s_ref[...]
        lsev = lse_ref[...]
        lane = lax.broadcasted_iota(jnp.int32, (1, LANE), 1)
        ds_acc = jnp.zeros((1, LANE), F32)
        prev = jnp.maximum(j - 1, 0)
        mask4 = jnp.concatenate([mask] * qpk, axis=1)
        dqts = []
        for kh in range(N_KV_HEADS):
            ksl = slice(kh * ATTN_DH, (kh + 1) * ATTN_DH)
            vsl = slice(D_KV + kh * ATTN_DH, D_KV + (kh + 1) * ATTN_DH)
            k3, v3 = kv3[:, ksl], kv3[:, vsl]
            heads = [kh * qpk + g for g in range(qpk)]
            q4 = jnp.concatenate([qv[:, h * ATTN_DH:(h + 1) * ATTN_DH] for h in heads], axis=0)
            do4 = jnp.concatenate([dov[:, h * ATTN_DH:(h + 1) * ATTN_DH] for h in heads], axis=0)
            lse4 = jnp.concatenate([lsev[h:h + 1, :] for h in heads], axis=1)
            sink4 = jnp.concatenate([jnp.broadcast_to(sk[:, h:h + 1], (1, T)) for h in heads], axis=1)
            p = jnp.exp(jnp.where(mask4, _dot(k3, q4, _NT), NEG) - lse4)
            ps = jnp.exp(sink4 - lse4)
            dp = _dot(v3, do4, _NT)
            delta = jnp.sum(p * dp, axis=0, keepdims=True)
            dsc = (p * (dp - delta)).astype(BF16)
            dq4 = _dot(k3.T, dsc) * scale
            dk3 = _dot(dsc, q4)
            dv3 = _dot(p.astype(BF16), do4)
            psd = ps * delta
            for g, h in enumerate(heads):
                dqts.append(dq4[:, g * T:(g + 1) * T])
                ds_acc = ds_acc - jnp.sum(psd[:, g * T:(g + 1) * T]) * (lane == h).astype(F32)
            for t, start in enumerate((0, pl.multiple_of(prev * T, T), pl.multiple_of(j * T, T))):
                rsl = pl.ds(start, T)
                dkv_ref[rsl, ksl] += dk3[t * T:(t + 1) * T, :]
                dkv_ref[rsl, vsl] += dv3[t * T:(t + 1) * T, :]
        ds_ref[...] += ds_acc
        dq_ref[...] = jnp.concatenate(dqts, axis=0).T.astype(dq_ref.dtype)

    blk = pl.BlockSpec((T, D_MODEL), lambda j: (j, 0))
    full = pl.BlockSpec((rows, 2 * D_KV), lambda j: (0, 0))
    return _call(
        body, name, [S((rows, D_MODEL), BF16), S((rows, 2 * D_KV), F32), S((1, LANE), F32)], (rows // T,),
        [blk, full, pl.BlockSpec((1, N_Q_HEADS), lambda j: (0, 0)), blk, pl.BlockSpec((N_Q_HEADS, T), lambda j: (0, j))],
        [blk, full, pl.BlockSpec((1, LANE), lambda j: (0, 0))], ("arbitrary",), (q, kv, sinks, do, lse), comm=comm)


BLOCK_BYTES = 1 << 20


def _div_tile(rows, cols, block_bytes=BLOCK_BYTES):
    cap = max(16, block_bytes // (4 * cols))
    best = None
    for t in range(16, min(rows, cap) + 1, 16):
        if rows % t == 0:
            best = t
    return best if best is not None else rows


def _adamw(parts, w, m, v, name, comm=None):
    layers, rows, cols = w.shape
    n = parts[0].shape[0]
    tr = _div_tile(rows, cols)
    tc = _pick(cols, 256) if tr == rows and rows * cols * 4 > 2 * BLOCK_BYTES else cols
    c1 = 1.0 / (1.0 - B1 ** STEP)
    c2 = 1.0 / (1.0 - B2 ** STEP)

    def body(*refs):
        p_refs = refs[:layers]
        w_ref, m_ref, v_ref, g_ref, d_ref, nm_ref, nv_ref = refs[layers:]
        layer = pl.program_id(0)
        for l in range(layers):
            @pl.when(layer == l)
            def _(p_ref=p_refs[l]):
                g = p_ref[0].astype(F32)
                for i in range(1, n):
                    g = g + p_ref[i].astype(F32)
                nm = B1 * m_ref[...] + (1.0 - B1) * g
                nv = B2 * v_ref[...] + (1.0 - B2) * (g * g)
                g_ref[...] = g
                nm_ref[...] = nm
                nv_ref[...] = nv
                d_ref[...] = -LR * ((nm * c1) / (jnp.sqrt(nv * c2) + EPS) + WD * w_ref[...])

    def part_spec(l):
        return pl.BlockSpec((n, tr, tc), lambda k, i, j: (0, jnp.where(k == l, i, 0), jnp.where(k == l, j, 0)))

    row = pl.BlockSpec((None, tr, tc), lambda k, i, j: (k, i, j))
    return _call(body, name, [S((layers, rows, cols), F32)] * 4, (layers, rows // tr, cols // tc),
                 [part_spec(l) for l in range(layers)] + [row, row, row], [row] * 4,
                 ("parallel", "parallel", "parallel"), (*parts, w, m, v), comm=comm)


def _sum_parts(parts, name):
    n, rows, cols = parts[0].shape
    nb = len(parts)
    tr = _div_tile(rows, cols, 2 * BLOCK_BYTES)

    def body(*refs):
        o_ref = refs[nb]
        blk = pl.program_id(0)
        for l in range(nb):
            @pl.when(blk == l)
            def _(p_ref=refs[l]):
                g = p_ref[0].astype(F32)
                for i in range(1, n):
                    g = g + p_ref[i].astype(F32)
                o_ref[...] = g

    def part_spec(l):
        return pl.BlockSpec((n, tr, cols), lambda k, i: (0, jnp.where(k == l, i, 0), 0))

    per = rows // tr
    return pl.pallas_call(body, name=name, out_shape=S((nb * rows, cols), F32), grid=(nb, per),
                          in_specs=[part_spec(l) for l in range(nb)],
                          out_specs=pl.BlockSpec((tr, cols), lambda k, i: (k * per + i, 0)),
                          compiler_params=_cp(("parallel", "parallel")))(*parts)


def _col_segments(ws, runs):
    segs = []
    for glo, mlo, n in runs:
        while n > 0:
            d, off = divmod(glo, ws)
            take = min(n, ws - off)
            segs.append((d, off, mlo, take))
            glo, mlo, n = glo + take, mlo + take, n - take
    return segs


def _assemble_cols(gs, width, segs, name):
    _, rows, ws = gs[0].shape
    nb = len(gs)
    rb = _div_tile(rows, width // 2, 4 * BLOCK_BYTES)
    per = rows // rb

    def body(*refs):
        o_ref = refs[nb]
        piece = pl.program_id(0)
        for l in range(nb):
            @pl.when(piece == l)
            def _(g_ref=refs[l]):
                o_ref[...] = jnp.zeros_like(o_ref)
                for d, off, mlo, n in segs:
                    o_ref[:, mlo:mlo + n] = g_ref[d, :, off:off + n]

    def piece_spec(l):
        return pl.BlockSpec((N_DEV, rb, ws), lambda k, i: (0, jnp.where(k == l, i, 0), 0))

    return pl.pallas_call(
        body, name=name, out_shape=S((nb * rows, width), gs[0].dtype), grid=(nb, per),
        in_specs=[piece_spec(l) for l in range(nb)],
        out_specs=pl.BlockSpec((rb, width), lambda k, i: (k * per + i, 0)),
        compiler_params=_cp(("parallel", "parallel")))(*gs)


def _scatter_cols(dw, ws, segs, name):
    rows, width = dw.shape
    rb = _div_tile(rows, width, 4 * BLOCK_BYTES)

    def body(w_ref, o_ref):
        for d, off, mlo, n in segs:
            o_ref[d, :, off:off + n] = w_ref[:, mlo:mlo + n].astype(o_ref.dtype)

    return pl.pallas_call(
        body, name=name, out_shape=S((N_DEV, rows, ws), BF16), grid=(rows // rb,),
        in_specs=[pl.BlockSpec((rb, width), lambda i: (i, 0))],
        out_specs=pl.BlockSpec((N_DEV, rb, ws), lambda i: (0, i, 0)), compiler_params=_cp(("parallel",)))(dw)


def _gather_comm(xs):
    n = len(xs)

    def setup(x_refs, out_refs, sems):
        send_sems, recv_sems, local_sems = sems
        mx, my, mc = lax.axis_index("x"), lax.axis_index("y"), lax.axis_index("c")
        me, sibling = (mx, my, mc), (mx, my, 1 - mc)
        chips = [(1 - mx, my), (mx, 1 - my), (1 - mx, 1 - my)]

        def blk(a, px, py, pc):
            return out_refs[a].at[4 * px + 2 * py + pc]

        def copy(a, k, block, to, src=None):
            return pltpu.make_async_remote_copy(
                src_ref=blk(a, *block) if src is None else src, dst_ref=blk(a, *block),
                send_sem=send_sems.at[a, k], recv_sem=recv_sems.at[a, k], device_id=to, device_id_type=_MESH)

        mine = [pltpu.make_async_copy(x_refs[a], blk(a, *me), local_sems.at[a]) for a in range(n)]
        own = []
        for a in range(n):
            own.append(copy(a, 0, me, sibling, src=x_refs[a]))
            own += [copy(a, 1 + i, me, (*chip, mc), src=x_refs[a]) for i, chip in enumerate(chips)]
        return me, sibling, chips, mc, copy, mine, own

    def first(x_refs, out_refs, sems):
        _, _, _, _, _, mine, own = setup(x_refs, out_refs, sems)
        for cp in mine + own:
            cp.start()

    def last(x_refs, out_refs, sems):
        me, sibling, chips, mc, copy, mine, own = setup(x_refs, out_refs, sems)
        passed = []
        for a in range(n):
            for i, chip in enumerate(chips):
                copy(a, 1 + i, (*chip, mc), me).wait_recv()
                passed.append(copy(a, 4 + i, (*chip, mc), sibling))
                passed[-1].start()
        for a in range(n):
            copy(a, 0, sibling, me).wait_recv()
            for i, chip in enumerate(chips):
                copy(a, 4 + i, (*chip, 1 - mc), me).wait_recv()
        for cp in own + passed:
            cp.wait_send()
        for cp in mine:
            cp.wait()

    return _Comm(list(xs), [S((N_DEV,) + x.shape, x.dtype) for x in xs],
                 [pltpu.SemaphoreType.DMA((n, 7)), pltpu.SemaphoreType.DMA((n, 7)), pltpu.SemaphoreType.DMA((n,))],
                 first, last)


def _swap_comm(gs):
    n = len(gs)

    def copies(g_refs, out_refs, sems):
        send_sems, recv_sems = sems
        mx, my, mc = lax.axis_index("x"), lax.axis_index("y"), lax.axis_index("c")
        return [pltpu.make_async_remote_copy(
            src_ref=g_refs[a].at[2 * k + 1 - mc], dst_ref=out_refs[a].at[k], send_sem=send_sems.at[a, k],
            recv_sem=recv_sems.at[a, k], device_id=(mx, my, 1 - mc), device_id_type=_MESH)
            for a in range(n) for k in range(4)]

    def first(g_refs, out_refs, sems):
        for cp in copies(g_refs, out_refs, sems):
            cp.start()

    def last(g_refs, out_refs, sems):
        for cp in copies(g_refs, out_refs, sems):
            cp.wait()

    return _Comm(list(gs), [S((4,) + g.shape[1:], g.dtype) for g in gs],
                 [pltpu.SemaphoreType.DMA((n, 4)), pltpu.SemaphoreType.DMA((n, 4))], first, last)


def _chips_comm(parts):
    n = len(parts)

    def copies(p_refs, out_refs, sems):
        send_sems, recv_sems, local_sems = sems
        mx, my, mc = lax.axis_index("x"), lax.axis_index("y"), lax.axis_index("c")
        mychip = 2 * mx + my
        chips = [(1 - mx, my), (mx, 1 - my), (1 - mx, 1 - my)]
        mine = [pltpu.make_async_copy(p_refs[a].at[mychip], out_refs[a].at[mychip], local_sems.at[a])
                for a in range(n)]
        return mine + [pltpu.make_async_remote_copy(
            src_ref=p_refs[a].at[2 * cx + cy], dst_ref=out_refs[a].at[mychip], send_sem=send_sems.at[a, i],
            recv_sem=recv_sems.at[a, i], device_id=(cx, cy, mc), device_id_type=_MESH)
            for a in range(n) for i, (cx, cy) in enumerate(chips)]

    def first(p_refs, out_refs, sems):
        for cp in copies(p_refs, out_refs, sems):
            cp.start()

    def last(p_refs, out_refs, sems):
        for cp in copies(p_refs, out_refs, sems):
            cp.wait()

    return _Comm(list(parts), [S(p.shape, p.dtype) for p in parts],
                 [pltpu.SemaphoreType.DMA((n, 3)), pltpu.SemaphoreType.DMA((n, 3)), pltpu.SemaphoreType.DMA((n,))],
                 first, last)


def _join_comms(comms):
    def split(refs, counts):
        out, p = [], 0
        for cnt in counts:
            out.append(refs[p:p + cnt])
            p += cnt
        return out

    ni = [len(c.ins) for c in comms]
    no = [len(c.out_shapes) for c in comms]
    ns = [len(c.scratch) for c in comms]

    def first(in_refs, out_refs, sems):
        for c, i, o, s in zip(comms, split(in_refs, ni), split(out_refs, no), split(sems, ns)):
            c.first(i, o, s)

    def last(in_refs, out_refs, sems):
        for c, i, o, s in zip(comms, split(in_refs, ni), split(out_refs, no), split(sems, ns)):
            c.last(i, o, s)

    return _Comm([x for c in comms for x in c.ins], [x for c in comms for x in c.out_shapes],
                 [x for c in comms for x in c.scratch], first, last)


def _add_pairs(mine, theirs, core, name):
    _, rows, cols = mine.shape
    tr = _div_tile(rows, cols // 2)

    def body(core_ref, a_ref, b_ref, o_ref):
        o_ref[...] = (a_ref[...].astype(F32) + b_ref[...].astype(F32)).astype(o_ref.dtype)

    return pl.pallas_call(
        body, name=name, out_shape=S((4, rows, cols), BF16),
        grid_spec=pltpu.PrefetchScalarGridSpec(
            num_scalar_prefetch=1, grid=(4, rows // tr),
            in_specs=[pl.BlockSpec((None, tr, cols), lambda k, i, c: (2 * k + c[0], i, 0)),
                      pl.BlockSpec((None, tr, cols), lambda k, i, c: (k, i, 0))],
            out_specs=pl.BlockSpec((None, tr, cols), lambda k, i, c: (k, i, 0))),
        compiler_params=_cp(("parallel", "parallel")))(core, mine, theirs)


def _run_comm(comm, name):
    ci, co = len(comm.ins), len(comm.out_shapes)

    def body(*refs):
        comm.first(refs[:ci], refs[ci:ci + co], refs[ci + co:])
        comm.last(refs[:ci], refs[ci:ci + co], refs[ci + co:])

    return pl.pallas_call(body, name=name, out_shape=list(comm.out_shapes), in_specs=[_HBM] * ci,
                          out_specs=[_HBM] * co, scratch_shapes=list(comm.scratch))(*comm.ins)


def _flat_rows(n_elems, mult):
    rows = -(-n_elems // LANE)
    return -(-rows // mult) * mult


def _pack(arrs, lead, mult, dtype):
    lead_shape = arrs[0].shape[:lead]
    flat = jnp.concatenate([a.astype(dtype).reshape(lead_shape + (-1,)) for a in arrs], axis=-1)
    n = flat.shape[-1]
    rows = _flat_rows(n, mult)
    flat = jnp.pad(flat, [(0, 0)] * lead + [(0, rows * LANE - n)])
    return flat.reshape(lead_shape + (rows, LANE))


def _unpack(flat, lead, shapes):
    lead_shape = flat.shape[:lead]
    flat = flat.reshape(lead_shape + (-1,))
    out, off = [], 0
    for shp in shapes:
        n = math.prod(shp)
        out.append(flat[..., off:off + n].reshape(lead_shape + tuple(shp)))
        off += n
    return out


def _split8(full, ax, n):
    shp = full.shape
    return jnp.moveaxis(full.reshape(shp[:ax] + (N_DEV, n) + shp[ax + 1:]), ax, 0)


def _join8(g, ax):
    shp = g.shape[1:]
    return jnp.moveaxis(g, 0, ax).reshape(shp[:ax] + (N_DEV * shp[ax],) + shp[ax + 1:])


def _group_lanes(v, hg):
    v = v.reshape(SSM_GROUPS, hg)
    return jnp.pad(v, ((0, 0), (0, LANE - hg))).reshape(1, SSM_GROUPS * LANE)


def _ungroup_lanes(v, hg):
    return v.reshape(SSM_GROUPS, LANE)[:, :hg].reshape(1, SSM_GROUPS * hg)


def kernel(x, meta_tokens, a_norm_pre, a_w_in, a_conv_w, a_conv_b, a_dt_bias, a_a_log, a_d_skip, a_gate_norm, a_w_out, a_norm_post, kv_norm, w_kv, b_norm_pre, b_w_q, b_sinks, b_w_o, b_norm_post, f_norm_pre, f_w_up, f_conv_w, f_conv_b, f_w_down, f_norm_post, loss_target, m_meta_tokens, m_a_norm_pre, m_a_w_in, m_a_conv_w, m_a_conv_b, m_a_dt_bias, m_a_a_log, m_a_d_skip, m_a_gate_norm, m_a_w_out, m_a_norm_post, m_kv_norm, m_w_kv, m_b_norm_pre, m_b_w_q, m_b_sinks, m_b_w_o, m_b_norm_post, m_f_norm_pre, m_f_w_up, m_f_conv_w, m_f_conv_b, m_f_w_down, m_f_norm_post, v_meta_tokens, v_a_norm_pre, v_a_w_in, v_a_conv_w, v_a_conv_b, v_a_dt_bias, v_a_a_log, v_a_d_skip, v_a_gate_norm, v_a_w_out, v_a_norm_post, v_kv_norm, v_w_kv, v_b_norm_pre, v_b_w_q, v_b_sinks, v_b_w_o, v_b_norm_post, v_f_norm_pre, v_f_w_up, v_f_conv_w, v_f_conv_b, v_f_w_down, v_f_norm_post):
    args = locals()
    wts = {n: args[n] for n in WEIGHTS}
    mom = {n: args["m_" + n] for n in WEIGHTS}
    var = {n: args["v_" + n] for n in WEIGHTS}
    mx, my, mc = lax.axis_index("x"), lax.axis_index("y"), lax.axis_index("c")
    me = 4 * mx + 2 * my + mc
    rows = _seq_rows()
    hg = SSM_HEADS // SSM_GROUPS
    d = D_MODEL

    n_main = D_INNER + D_XBC
    ws_in, ws_up = a_w_in.shape[2], f_w_up.shape[2]
    segs_in = _col_segments(ws_in, [(0, 0, n_main)] + [(n_main + hg * g, n_main + LANE * g, hg)
                                                      for g in range(SSM_GROUPS)])
    segs_up = _col_segments(ws_up, [(0, 0, 2 * D_FF)])
    def gather_of(*ws):
        return _gather_comm([w.astype(BF16) for w in ws])

    small_full, = _run_comm(_gather_comm([_pack([wts[n] for n in SMALL], 0, 8, F32)]), "gather_small")
    full = {}
    for n, g in zip(SMALL, _unpack(small_full, 1, [wts[n].shape for n in SMALL])):
        full[n] = _join8(g, SHARD_AXIS[n])
    (h0, hn0), (g_in,) = _embed_norm(full["meta_tokens"], x[0], full["a_norm_pre"], rows, "embed_norm",
                                     comm=gather_of(a_w_in[0]))
    w_in_all = _assemble_cols([g_in], n_main + SSM_GROUPS * LANE, segs_in, "asm_w_in")
    w_up, w_down = [None, None], [None, None]
    bias_g = _group_lanes(wts["a_dt_bias"], hg)
    alog_g = _group_lanes(wts["a_a_log"], hg)
    dsk_g = _group_lanes(wts["a_d_skip"], hg)
    a_conv_w, a_conv_b = full["a_conv_w"][0], full["a_conv_b"]
    f_cw, f_cb = full["f_conv_w"], wts["f_conv_b"]
    fpre, fpost = wts["f_norm_pre"], wts["f_norm_post"]


    zx, (g_out,) = _mm(hn0, w_in_all, "nn", F32, "mm_in", comm=gather_of(a_w_out[0]))
    w_out = g_out.reshape(D_INNER, d)
    xbc = _conv_silu_fwd(zx, a_conv_w, a_conv_b, "conv_a")
    (y_ssd, hst), (g_up0,) = _ssd_fwd(xbc, zx, bias_g, alog_g, dsk_g, "ssd_fwd", comm=gather_of(f_w_up[0]))
    w_up[0] = _assemble_cols([g_up0], 2 * D_FF, segs_up, "asm_w_up0")
    yn = _gatenorm_fwd(y_ssd, zx, full["a_gate_norm"], "gatenorm")
    mix_a, (g_o,) = _mm(yn, w_out, "nn", F32, "mm_out", comm=gather_of(b_w_o[0]))
    h1, (fn0,) = _resid_norm(h0, mix_a, full["a_norm_post"], [fpre[0:1]], "resid_a")

    half = d // 2
    u0, (g_dn0,) = _mm(fn0, w_up[0], "nn", F32, "mm_up0", comm=gather_of(f_w_down[0]))
    act0, (g_up1a,) = _ffn_act_fwd(u0, f_cw[0], f_cb[0:1], "ffn_act0", comm=gather_of(f_w_up[1, :half]))
    ffn0, (g_kv, g_q) = _mm(act0, g_dn0.reshape(D_FF, d), "nn", F32, "mm_down0", comm=gather_of(w_kv, b_w_q[0]))
    w_kvf, w_q, w_o = g_kv.reshape(d, 2 * D_KV), g_q.reshape(d, d), g_o.reshape(d, d)
    h2, (kvn, bn) = _resid_norm(h1, ffn0, fpost[0:1], [wts["kv_norm"].reshape(1, d), wts["b_norm_pre"]], "resid_f0")
    kv = _mm(kvn, w_kvf, "nn", F32, "mm_kv")
    q = _mm(bn, w_q, "nn", F32, "mm_q")
    (o, lse), (g_up1b,) = _attn_fwd(q, kv, wts["b_sinks"], "attn_fwd", comm=gather_of(f_w_up[1, half:]))
    w_up[1] = _assemble_cols([g_up1a, g_up1b], 2 * D_FF, segs_up, "asm_w_up1")
    mix_b = _mm(o, w_o, "nn", F32, "mm_o")
    h3, (fn1,) = _resid_norm(h2, mix_b, wts["b_norm_post"], [fpre[1:2]], "resid_b")
    u1, (g_dn1,) = _mm(fn1, w_up[1], "nn", F32, "mm_up1", comm=gather_of(f_w_down[1]))
    w_down = [g_dn0.reshape(D_FF, d), g_dn1.reshape(D_FF, d)]
    act1 = _ffn_act_fwd(u1, f_cw[1], f_cb[1:2], "ffn_act1")
    ffn1 = _mm(act1, w_down[1], "nn", F32, "mm_down1")
    dh4, loss_row, dffn1, dw_post1 = _final_loss(h3, ffn1, fpost[1:2], loss_target[0], "loss")
    loss = lax.psum(loss_row[0, 0], ("x", "y", "c"))

    grads = {}

    core = mc.astype(jnp.int32).reshape(1)

    def carried(res, comm):
        return res if comm is not None else (res, None)

    def ffn_bwd(dh_out, dffn, h_in, fn, u, act, i, then, c_dact=None, c_dwdown=None, c_dwup=None, c_dfn=None):
        dact, got_a = carried(_mm(dffn, w_down[i], "nt", F32, f"mm_dact{i}", comm=c_dact), c_dact)
        dw_down, got_b = carried(_mm(act, dffn, "tn", BF16, f"mm_dwdown{i}", comm=c_dwdown), c_dwdown)
        dw_down = dw_down.reshape(N_DEV, -1, d)
        du, dwc, dbc = _ffn_act_bwd(u, dact, f_cw[i], f_cb[i:i + 1], f"ffn_act_bwd{i}")
        dfn, (s_dn, *got_d) = _mm(du, w_up[i], "nt", F32, f"mm_dfn{i}", comm=_join_comms(
            [_swap_comm([dw_down])] + ([c_dfn] if c_dfn is not None else [])))
        sum_dn = _add_pairs(dw_down, s_dn, core, f"rs_add_dn{i}")
        dw_up, got_c = carried(_mm(fn, du, "tn", BF16, f"mm_dwup{i}", comm=c_dwup, shard_cols=ws_up), c_dwup)
        (dh_in, dw_pre, dbranch, dw_branch), (s_up,) = _norm_bwd(
            h_in, fpre[i:i + 1], dfn, dh_out, F32, f"nb_fpre{i}", comm=_swap_comm([dw_up]), then=then)
        sum_up = _add_pairs(dw_up, s_up, core, f"rs_add_up{i}")
        return dh_in, dbranch, dw_branch, dict(sum_down=sum_dn, cw=jnp.concatenate([dwc[0], dwc[1]], axis=1),
                                               cb=jnp.concatenate([dbc[0], dbc[1]], axis=1), sum_up=sum_up,
                                               pre=dw_pre), got_a, got_b, got_c, got_d

    dh3, dmix_b, grads["b_norm_post"], gf1, _, _, _, _ = ffn_bwd(dh4, dffn1, h3, fn1, u1, act1, 1,
                                                                 (mix_b, wts["b_norm_post"]))
    do = _mm(dmix_b, w_o, "nt", F32, "mm_do")
    dw_o = _mm(o, dmix_b, "tn", BF16, "mm_dwo").reshape(N_DEV, -1, d)
    half_up = gf1["sum_up"].shape[1] // 2
    (dq, dkv, dsinks), (p_up1a, s_o) = _attn_bwd(
        q, kv, wts["b_sinks"], do, lse, "attn_bwd",
        comm=_join_comms([_chips_comm([gf1["sum_up"][:, :half_up]]), _swap_comm([dw_o])]))
    sum_o = _add_pairs(dw_o, s_o, core, "rs_add_o")
    grads["b_sinks"] = dsinks[:, :N_Q_HEADS]
    dbn = _mm(dq, w_q, "nt", F32, "mm_dbn")
    dw_q = _mm(bn, dq, "tn", BF16, "mm_dwq").reshape(N_DEV, -1, d)
    dkv16 = dkv.astype(BF16)
    dkvn = _mm(dkv16, w_kvf, "nt", F32, "mm_dkvn")
    dw_kv = _mm(kvn, dkv16, "tn", BF16, "mm_dwkv").reshape(N_DEV, -1, 2 * D_KV)
    (dh2, grads["b_norm_pre"]), (s_q, s_kv) = _norm_bwd(h2, wts["b_norm_pre"], dbn, dh3, F32, "nb_bpre",
                                                        comm=_swap_comm([dw_q, dw_kv]))
    sum_q, sum_kv = _add_pairs(dw_q, s_q, core, "rs_add_q"), _add_pairs(dw_kv, s_kv, core, "rs_add_kv")
    dh2, dw_kvn, dffn0, dw_post0 = _norm_bwd(h2, wts["kv_norm"].reshape(1, d), dkvn, dh2, F32, "nb_kv",
                                             then=(ffn0, fpost[0:1]))
    grads["kv_norm"] = dw_kvn.reshape(d)
    dh1, dmix_a, grads["a_norm_post"], gf0, (p_o,), (p_q, p_kv), (p_dn1,), (p_up1b,) = ffn_bwd(
        dh2, dffn0, h1, fn0, u0, act0, 0, (mix_a, full["a_norm_post"]), c_dact=_chips_comm([sum_o]),
        c_dwdown=_chips_comm([sum_q, sum_kv]), c_dwup=_chips_comm([gf1["sum_down"]]),
        c_dfn=_chips_comm([gf1["sum_up"][:, half_up:]]))
    p_up1 = jnp.concatenate([p_up1a, p_up1b], axis=1)
    grads["f_norm_post"] = jnp.concatenate([dw_post0, dw_post1], axis=0)
    grads["f_norm_pre"] = jnp.concatenate([gf0["pre"], gf1["pre"]], axis=0)
    grads["f_conv_w"] = jnp.stack([gf0["cw"], gf1["cw"]])
    grads["f_conv_b"] = jnp.concatenate([gf0["cb"], gf1["cb"]], axis=0)

    dyn = _mm(dmix_a, w_out, "nt", F32, "mm_dyn")
    dw_out = _mm(yn, dmix_a, "tn", BF16, "mm_dwout").reshape(N_DEV, -1, d)
    (dy_ssd, dzx, grads["a_gate_norm"]), (s_out,) = _gatenorm_bwd(y_ssd, zx, full["a_gate_norm"], dyn, "gatenorm_bwd",
                                                                  comm=_swap_comm([dw_out]))
    sum_out = _add_pairs(dw_out, s_out, core, "rs_add_out")
    (dxs, dbm, dcm, dzx, dalog, ddsk, dbias), (p_up0,) = _ssd_bwd(
        xbc, zx, bias_g, alog_g, dsk_g, dy_ssd, hst, dzx, "ssd_bwd", comm=_chips_comm([gf0["sum_up"]]))
    grads["a_a_log"] = _ungroup_lanes(dalog, hg)
    grads["a_d_skip"] = _ungroup_lanes(ddsk, hg)
    grads["a_dt_bias"] = _ungroup_lanes(dbias, hg)
    dzx, dcw, dcb = _conv_silu_bwd(zx, dxs, dbm, dcm, a_conv_w, a_conv_b, dzx, "conv_a_bwd")
    grads["a_conv_w"], grads["a_conv_b"] = dcw[None], dcb
    dw_in_all, (p_dn0,) = _mm(hn0, dzx, "tn", BF16, "mm_dwin", comm=_chips_comm([gf0["sum_down"]]))
    dw_in8 = _scatter_cols(dw_in_all, ws_in, segs_in, "scat_w_in")
    dhn0, (s_in, p_out) = _mm(dzx, w_in_all, "nt", F32, "mm_dhn0",
                              comm=_join_comms([_swap_comm([dw_in8]), _chips_comm([sum_out])]))
    sum_in = _add_pairs(dw_in8, s_in, core, "rs_add_in")
    half_in = sum_in.shape[1] // 2
    (grads["meta_tokens"], g_x, grads["a_norm_pre"]), (p_in_a,) = _norm_bwd(
        h0, full["a_norm_pre"], dhn0, dh1, F32, "nb_apre", comm=_chips_comm([sum_in[:, :half_in]]), split_rows=SEQ)
    grad_x = g_x[None]

    small_local = _pack([_split8(grads[n], SHARD_AXIS[n], wts[n].shape[SHARD_AXIS[n]]) for n in SMALL], 1, 8, F32)
    repl_local = _pack([grads[n] for n in REPL], 0, 8, F32)
    n_sr = small_local.shape[1]
    small_vec = jnp.concatenate([small_local.reshape(N_DEV * n_sr, LANE), repl_local], axis=0)
    tail = _join_comms([_chips_comm([sum_in[:, half_in:]]), _gather_comm([small_vec])])
    parts_big = dict(a_w_out=[p_out], w_kv=[p_kv], b_w_q=[p_q], b_w_o=[p_o], f_w_down=[p_dn0, p_dn1])

    def flat_f32(dct, names, mult):
        return _pack([dct[n] for n in names], 0, mult, F32)

    def adamw_big(n, comm=None):
        shp3 = (len(parts_big[n]),) + parts_big[n][0].shape[1:]
        res = _adamw(parts_big[n], *[dct[n].reshape(shp3) for dct in (wts, mom, var)], f"adamw_{n}", comm=comm)
        res, got = res if comm is not None else (res, None)
        big_out[n] = [r.reshape(wts[n].shape) for r in res]
        return got

    big_out = {}
    def swap_last(a):
        return jnp.swapaxes(a, -1, -2)

    g_up_t = swap_last(_sum_parts([p_up0, p_up1], "sum_w_up").reshape(f_w_up.shape))
    res, (p_in_b, small_all) = _adamw([g_up_t[0:1], g_up_t[1:2]], *[swap_last(dct["f_w_up"]) for dct in (wts, mom, var)],
                                      "adamw_f_w_up", comm=tail)
    big_out["f_w_up"] = [swap_last(r) for r in res]
    for n in BIG:
        if n not in ("f_w_up", "a_w_in"):
            adamw_big(n)
    g_in_t = swap_last(_sum_parts([p_in_a, p_in_b], "sum_w_in"))[None]
    res = _adamw([g_in_t], *[swap_last(dct["a_w_in"]) for dct in (wts, mom, var)], "adamw_a_w_in")
    big_out["a_w_in"] = [swap_last(r) for r in res]
    mine_small = lax.dynamic_slice_in_dim(small_all, me * n_sr, n_sr, axis=1)
    parts_small = jnp.concatenate([mine_small, small_all[:, N_DEV * n_sr:]], axis=1)
    sm_in = [jnp.concatenate([flat_f32(dct, SMALL, 8), flat_f32(dct, REPL, 8)], axis=0)[None] for dct in (wts, mom, var)]
    small_out = [r[0] for r in _adamw([parts_small], *sm_in, "adamw_small")]

    outs = []
    for kind in range(4):
        res = {n: big_out[n][kind] for n in BIG}
        for n, a in zip(SMALL, _unpack(small_out[kind][:n_sr], 0, [wts[n].shape for n in SMALL])):
            res[n] = a
        for n, a in zip(REPL, _unpack(small_out[kind][n_sr:], 0, [wts[n].shape for n in REPL])):
            res[n] = a
        outs.append(res)
    return (loss, grad_x, *[outs[0][n] for n in WEIGHTS], *[outs[1][n] for n in WEIGHTS],
            *[outs[2][n] for n in WEIGHTS], *[outs[3][n] for n in WEIGHTS])
```

```python
import functools
import math

import jax
import jax.numpy as jnp
from jax import lax
from jax.experimental import pallas as pl
from jax.experimental.pallas import tpu as pltpu

F32, BF16 = jnp.float32, jnp.bfloat16
S = jax.ShapeDtypeStruct

D_MODEL = 1024
SEQ = 2048
N_META = 16
D_INNER = 2048
HEAD_P = 64
SSM_HEADS = D_INNER // HEAD_P
SSM_GROUPS = 4
D_STATE = 128
SSM_CONV = 4
D_BC = SSM_GROUPS * D_STATE
D_XBC = D_INNER + 2 * D_BC
ATTN_DH = 64
N_Q_HEADS = D_MODEL // ATTN_DH
N_KV_HEADS = 4
D_KV = N_KV_HEADS * ATTN_DH
WINDOW = 128
D_FF = 2816
FFN_CONV = 3
RMS_EPS = 1e-6
NEG = -1e30
LR, B1, B2, EPS, WD, STEP = 0.001, 0.9, 0.999, 1e-08, 0.01, 10

N_DEV = 8
T = 128
LANE = 128
VMEM_LIMIT = 48 * 1024 * 1024

BIG = ("a_w_in", "a_w_out", "w_kv", "b_w_q", "b_w_o", "f_w_up", "f_w_down")
SMALL = ("meta_tokens", "a_norm_pre", "a_conv_w", "a_conv_b", "a_gate_norm", "a_norm_post", "f_conv_w")
REPL = ("a_dt_bias", "a_a_log", "a_d_skip", "kv_norm", "b_norm_pre", "b_sinks", "b_norm_post",
        "f_norm_pre", "f_conv_b", "f_norm_post")
SHARD_AXIS = dict(a_w_in=2, a_w_out=1, w_kv=0, b_w_q=1, b_w_o=1, f_w_up=2, f_w_down=1, meta_tokens=1,
                  a_norm_pre=1, a_conv_w=2, a_conv_b=1, a_gate_norm=1, a_norm_post=1, f_conv_w=2)
WEIGHTS = ("meta_tokens", "a_norm_pre", "a_w_in", "a_conv_w", "a_conv_b", "a_dt_bias", "a_a_log", "a_d_skip",
           "a_gate_norm", "a_w_out", "a_norm_post", "kv_norm", "w_kv", "b_norm_pre", "b_w_q", "b_sinks", "b_w_o",
           "b_norm_post", "f_norm_pre", "f_w_up", "f_conv_w", "f_conv_b", "f_w_down", "f_norm_post")


def _seq_rows():
    return -(-(N_META + SEQ) // T) * T


def _cp(sem=None):
    return pltpu.CompilerParams(dimension_semantics=sem, vmem_limit_bytes=VMEM_LIMIT)


def _pick(n, target):
    t = min(n, target)
    t -= t % LANE
    while n % t:
        t -= LANE
    return t


def _sigmoid(x):
    return 0.5 * jnp.tanh(0.5 * x) + 0.5


def _softplus(x):
    return jnp.maximum(x, 0.0) + jnp.log(1.0 + jnp.exp(-jnp.abs(x)))


_NN = (((1,), (0,)), ((), ()))
_NT = (((1,), (1,)), ((), ()))
_TN = (((0,), (0,)), ((), ()))


def _dot(a, b, dims=_NN):
    return lax.dot_general(a, b, dims, preferred_element_type=F32)


def _dot_hi(a, b):
    return lax.dot_general(a, b, _NN, precision=lax.Precision.HIGHEST, preferred_element_type=F32)


_HBM = pl.BlockSpec(memory_space=pltpu.HBM)
_MESH = pl.DeviceIdType.MESH


class _Comm:
    def __init__(self, ins, out_shapes, scratch, first, last):
        self.ins, self.out_shapes, self.scratch, self.first, self.last = ins, out_shapes, scratch, first, last


_ANY = pl.BlockSpec(memory_space=pl.ANY)


def _call(body, name, out_shape, grid, in_specs, out_specs, sem, args, scratch=(), comm=None, aliases=None):
    aliases = aliases or {}
    if comm is None:
        return pl.pallas_call(body, name=name, out_shape=out_shape, grid=grid, in_specs=in_specs, out_specs=out_specs,
                              scratch_shapes=list(scratch), input_output_aliases=aliases,
                              compiler_params=_cp(sem))(*args)
    single = not isinstance(out_shape, (list, tuple))
    outs = [out_shape] if single else list(out_shape)
    ospecs = [out_specs] if single else list(out_specs)
    n_in, n_out, n_scr, ci, co = len(in_specs), len(outs), len(scratch), len(comm.ins), len(comm.out_shapes)

    def carrier(*refs):
        p = 0
        parts = []
        for cnt in (n_in, ci, n_out, co, n_scr, len(comm.scratch)):
            parts.append(refs[p:p + cnt])
            p += cnt
        ins, cins, outs_r, couts, scr, cscr = parts
        ids = [pl.program_id(i) for i in range(len(grid))]
        first, last = ids[0] == 0, ids[0] == grid[0] - 1
        for i in range(1, len(grid)):
            first, last = first & (ids[i] == 0), last & (ids[i] == grid[i] - 1)

        @pl.when(first)
        def _():
            comm.first(cins, couts, cscr)

        body(*ins, *outs_r, *scr)

        @pl.when(last)
        def _():
            comm.last(cins, couts, cscr)

    res = pl.pallas_call(
        carrier, name=name, out_shape=outs + list(comm.out_shapes), grid=grid,
        in_specs=list(in_specs) + [_HBM] * ci, out_specs=ospecs + [_HBM] * co,
        scratch_shapes=list(scratch) + list(comm.scratch), input_output_aliases=aliases,
        compiler_params=_cp(("arbitrary",) * len(grid)))(*args, *comm.ins)
    mine = res[0] if single else list(res[:n_out])
    return mine, list(res[n_out:])


def _mm(a, b, mode, out_dtype, name, comm=None, shard_cols=None):
    if mode == "tn":
        m, kk = a.shape
        planes, width = (b.shape[0], b.shape[2]) if b.ndim == 3 else (1, b.shape[1])
        n = planes * width
        tko, tn = _pick(kk, 512), _pick(width, 1536)
        per = width // tn
        b_spec = (pl.BlockSpec((None, m, tn), lambda i, j: (j // per, 0, j % per)) if b.ndim == 3
                  else pl.BlockSpec((m, tn), lambda i, j: (0, j)))
        if shard_cols is None:
            def body(a_ref, b_ref, o_ref):
                o_ref[...] = _dot(a_ref[...], b_ref[...], _TN).astype(o_ref.dtype)

            out_shape, out_spec = S((kk, n), out_dtype), pl.BlockSpec((tko, tn), lambda i, j: (i, j))
        else:
            shards = tn // shard_cols
            assert tn % shard_cols == 0

            def body(a_ref, b_ref, o_ref):
                res = _dot(a_ref[...], b_ref[...], _TN).astype(o_ref.dtype)
                for p in range(shards):
                    o_ref[p] = res[:, p * shard_cols:(p + 1) * shard_cols]

            out_shape = S((n // shard_cols, kk, shard_cols), out_dtype)
            out_spec = pl.BlockSpec((shards, tko, shard_cols), lambda i, j: (j, i, 0))
        return _call(
            body, name, out_shape, (kk // tko, n // tn), [pl.BlockSpec((m, tko), lambda i, j: (0, i)), b_spec],
            out_spec, ("parallel", "parallel"), (a, b), comm=comm)

    planes, width = (a.shape[0], a.shape[2]) if a.ndim == 3 else (1, a.shape[1])
    m, kk = a.shape[-2], planes * width
    n = b.shape[1] if mode == "nn" else b.shape[0]
    dims = _NN if mode == "nn" else _NT

    if kk > 2048:
        tm = m // 4
        assert m % 4 == 0 and tm % 16 == 0

        def body(a_ref, b_ref, o_ref):
            if a.ndim == 2:
                res = _dot(a_ref[...], b_ref[...], dims)
            else:
                res = None
                for p in range(planes):
                    bp = b_ref[p * width:(p + 1) * width, :] if mode == "nn" else b_ref[:, p * width:(p + 1) * width]
                    part = _dot(a_ref[p], bp, dims)
                    res = part if res is None else res + part
            o_ref[...] = res.astype(o_ref.dtype)

        a_spec = (pl.BlockSpec((planes, tm, width), lambda i: (0, i, 0)) if a.ndim == 3
                  else pl.BlockSpec((tm, kk), lambda i: (i, 0)))
        return _call(
            body, name, S((m, n), out_dtype), (m // tm,),
            [a_spec, pl.BlockSpec(b.shape, lambda i: (0, 0), pipeline_mode=pl.Buffered(1))],
            pl.BlockSpec((tm, n), lambda i: (i, 0)), ("parallel",), (a, b), comm=comm)

    tn = _pick(n, 512)

    def body(a_ref, b_ref, o_ref):
        o_ref[...] = _dot(a_ref[...], b_ref[...], dims).astype(o_ref.dtype)

    b_spec = (pl.BlockSpec((kk, tn), lambda j: (0, j)) if mode == "nn" else pl.BlockSpec((tn, kk), lambda j: (j, 0)))
    return _call(
        body, name, S((m, n), out_dtype), (n // tn,), [pl.BlockSpec((m, kk), lambda j: (0, 0)), b_spec],
        pl.BlockSpec((m, tn), lambda j: (0, j)), ("parallel",), (a, b), comm=comm)


def _rms(x, w):
    return x * lax.rsqrt(jnp.mean(x * x, axis=-1, keepdims=True) + RMS_EPS) * w


def _row_tile(rows, d):
    return rows // 4 if d <= 1024 and (rows // 4) % 16 == 0 else rows // 8


def _embed_norm(meta, x, w, rows, name, comm=None):
    n_meta, d = meta.shape
    n_x = x.shape[0]
    last = rows // T - 1
    assert n_meta % 8 == 0 and n_meta < T and n_meta + n_x == last * T + n_meta and last * T >= n_x

    def body(m_ref, x_ref, w_ref, h_ref, hn_ref):
        i = pl.program_id(0)

        @pl.when(i == 0)
        def _():
            h_ref[0:n_meta, :] = m_ref[...]
            h_ref[n_meta:T, :] = x_ref[0:T - n_meta, :]

        @pl.when((i > 0) & (i < last))
        def _():
            h_ref[...] = x_ref[pl.ds(pl.multiple_of(i * T - n_meta, 8), T), :]

        @pl.when(i == last)
        def _():
            h_ref[0:n_meta, :] = x_ref[n_x - n_meta:n_x, :]
            h_ref[n_meta:T, :] = jnp.zeros((T - n_meta, d), F32)

        hn_ref[...] = _rms(h_ref[...], w_ref[...]).astype(hn_ref.dtype)

    row = pl.BlockSpec((T, d), lambda i: (i, 0))
    return _call(body, name, [S((rows, d), F32), S((rows, d), BF16)], (rows // T,),
                 [pl.BlockSpec((n_meta, d), lambda i: (0, 0)), pl.BlockSpec((n_x, d), lambda i: (0, 0)),
                  pl.BlockSpec((1, d), lambda i: (0, 0))], [row, row], ("parallel",), (meta, x, w), comm=comm)


def _resid_norm(h, br, w_post, next_ws, name):
    rows, d = h.shape
    tr = _row_tile(rows, d)
    has_br = br is not None
    nw = len(next_ws)

    def body(*refs):
        h_ref = refs[0]
        pos = 1
        x = h_ref[...]
        if has_br:
            x = x + _rms(refs[1][...], refs[2][...])
            pos = 3
        w_refs = refs[pos:pos + nw]
        outs = refs[pos + nw:]
        if has_br:
            outs[0][...] = x
            outs = outs[1:]
        for w_ref, o_ref in zip(w_refs, outs):
            o_ref[...] = _rms(x, w_ref[...]).astype(o_ref.dtype)

    row = pl.BlockSpec((tr, d), lambda i: (i, 0))
    vec = pl.BlockSpec((1, d), lambda i: (0, 0))
    ins = [h] + ([br, w_post] if has_br else []) + list(next_ws)
    in_specs = [row] + ([row, vec] if has_br else []) + [vec] * nw
    out_shape = ([S((rows, d), F32)] if has_br else []) + [S((rows, d), BF16)] * nw
    res = pl.pallas_call(body, name=name, out_shape=out_shape, grid=(rows // tr,), in_specs=in_specs,
                         out_specs=[row] * len(out_shape), compiler_params=_cp(("parallel",)))(*ins)
    if has_br:
        return res[0], list(res[1:])
    return h, list(res)


def _rms_bwd(xv, w, dyv):
    r = lax.rsqrt(jnp.mean(xv * xv, axis=-1, keepdims=True) + RMS_EPS)
    wdy = dyv * w
    dx = r * wdy - xv * (r * r * r) * jnp.mean(xv * wdy, axis=-1, keepdims=True)
    return dx, jnp.sum(dyv * xv * r, axis=0, keepdims=True)


def _norm_bwd(x, w, dy, add, out_dtype, name, comm=None, then=None, split_rows=None):
    rows, d = x.shape
    tr = _row_tile(rows, d)
    has_add = add is not None
    n_in = 3 + has_add + (2 if then is not None else 0)
    if split_rows is not None:
        last, tail = _real_rows(rows, tr, split_rows)

    def body(*refs):
        x_ref, w_ref, dy_ref = refs[:3]
        outs = refs[n_in:]
        dx, dw = _rms_bwd(x_ref[...], w_ref[...], dy_ref[...].astype(F32))
        if has_add:
            dx = dx + refs[3][...]
        if split_rows is None:
            outs[0][...] = dx.astype(outs[0].dtype)
        else:
            i = pl.program_id(0)
            gm_ref, gx_ref = outs[0], outs[1]
            outs = outs[1:]

            @pl.when(i == 0)
            def _():
                gm_ref[...] = dx[0:N_META, :]
                gx_ref[0:tr - N_META, :] = dx[N_META:tr, :]

            @pl.when((i > 0) & (i < last))
            def _():
                gx_ref[pl.ds(pl.multiple_of(i * tr - N_META, 8), tr), :] = dx

            @pl.when(i == last)
            def _():
                gx_ref[split_rows - tail:split_rows, :] = dx[0:tail, :]
        first = pl.program_id(0) == 0

        @pl.when(first)
        def _():
            outs[1][...] = jnp.zeros_like(outs[1])

        outs[1][...] += dw
        if then is not None:
            dx2, dw2 = _rms_bwd(refs[n_in - 2][...], refs[n_in - 1][...], dx)
            outs[2][...] = dx2.astype(outs[2].dtype)

            @pl.when(first)
            def _():
                outs[3][...] = jnp.zeros_like(outs[3])

            outs[3][...] += dw2

    row = pl.BlockSpec((tr, d), lambda i: (i, 0))
    vec = pl.BlockSpec((1, d), lambda i: (0, 0))
    ins = [x, w, dy] + ([add] if has_add else []) + (list(then) if then is not None else [])
    in_specs = [row, vec, row] + ([row] if has_add else []) + ([row, vec] if then is not None else [])
    out_shape = [S((rows, d), out_dtype), S((1, d), F32)] + ([S((rows, d), BF16), S((1, d), F32)] if then is not None else [])
    out_specs = [row, vec] * (len(out_shape) // 2)
    if split_rows is not None:
        out_shape = [S((N_META, d), F32), S((split_rows, d), F32)] + out_shape[1:]
        out_specs = [pl.BlockSpec((N_META, d), lambda i: (0, 0)), pl.BlockSpec((split_rows, d), lambda i: (0, 0))] + out_specs[1:]
    return _call(body, name, out_shape, (rows // tr,), in_specs, out_specs, ("arbitrary",), ins, comm=comm)


def _real_rows(rows, tr, n_x):
    last = (N_META + n_x - 1) // tr
    tail = N_META + n_x - last * tr
    assert last == rows // tr - 1 and N_META % 8 == 0 and tail % 8 == 0 and N_META < tr
    return last, tail


def _final_loss(h, br, w_post, target, name):
    rows, d = h.shape
    tr = _row_tile(rows, d)
    n_x = target.shape[0]
    last, tail = _real_rows(rows, tr, n_x)

    def body(h_ref, br_ref, w_ref, t_ref, dh_ref, loss_ref, dbr_ref, dw_ref, tbuf):
        i = pl.program_id(0)

        @pl.when(i == 0)
        def _():
            tbuf[0:N_META, :] = jnp.zeros((N_META, d), F32)
            tbuf[N_META:tr, :] = t_ref[0:tr - N_META, :]

        @pl.when((i > 0) & (i < last))
        def _():
            tbuf[...] = t_ref[pl.ds(pl.multiple_of(i * tr - N_META, 8), tr), :]

        @pl.when(i == last)
        def _():
            tbuf[0:tail, :] = t_ref[n_x - tail:n_x, :]
            if tail < tr:
                tbuf[tail:tr, :] = jnp.zeros((tr - tail, d), F32)

        brv, wv = br_ref[...], w_ref[...]
        y = h_ref[...] + _rms(brv, wv)
        r = i * tr + lax.broadcasted_iota(jnp.int32, (tr, 1), 0)
        real = (r >= N_META) & (r < N_META + SEQ)
        diff = jnp.where(real, y - tbuf[...], 0.0)
        dh = diff * (1.0 / d)
        dh_ref[...] = dh
        dbr, dw = _rms_bwd(brv, wv, dh)
        dbr_ref[...] = dbr.astype(dbr_ref.dtype)

        @pl.when(i == 0)
        def _():
            loss_ref[...] = jnp.zeros_like(loss_ref)
            dw_ref[...] = jnp.zeros_like(dw_ref)

        loss_ref[...] += jnp.sum(diff * diff) * (0.5 / d)
        dw_ref[...] += dw

    row = pl.BlockSpec((tr, d), lambda i: (i, 0))
    vec = pl.BlockSpec((1, d), lambda i: (0, 0))
    return pl.pallas_call(body, name=name,
                          out_shape=[S((rows, d), F32), S((1, LANE), F32), S((rows, d), BF16), S((1, d), F32)],
                          grid=(rows // tr,), in_specs=[row, row, vec, pl.BlockSpec((n_x, d), lambda i: (0, 0))],
                          out_specs=[row, pl.BlockSpec((1, LANE), lambda i: (0, 0)), row, vec],
                          scratch_shapes=[pltpu.VMEM((tr, d), F32)],
                          compiler_params=_cp(("arbitrary",)))(h, br, w_post, target)


def _gatenorm_fwd(y, zx, w, name, comm=None):
    rows, d = y.shape
    tr = _row_tile(rows, d)

    def body(y_ref, z_ref, w_ref, o_ref):
        z = z_ref[...]
        o_ref[...] = _rms(y_ref[...] * z * _sigmoid(z), w_ref[...]).astype(o_ref.dtype)

    row = pl.BlockSpec((tr, d), lambda i: (i, 0))
    return _call(body, name, S((rows, d), BF16), (rows // tr,), [row, row, pl.BlockSpec((1, d), lambda i: (0, 0))],
                 row, ("parallel",), (y, zx, w), comm=comm)


def _gatenorm_bwd(y, zx, w, dyn, name, comm=None):
    rows, d = y.shape
    tr = _row_tile(rows, d)

    def body(y_ref, z_ref, w_ref, dyn_ref, dy_ref, dz_ref, dw_ref):
        yv, z = y_ref[...], z_ref[...]
        sg = _sigmoid(z)
        sz = z * sg
        g = yv * sz
        r = lax.rsqrt(jnp.mean(g * g, axis=-1, keepdims=True) + RMS_EPS)
        dyn_v = dyn_ref[...]
        wdy = dyn_v * w_ref[...]
        dg = r * wdy - g * (r * r * r) * jnp.mean(g * wdy, axis=-1, keepdims=True)
        dy_ref[...] = dg * sz
        dz_ref[...] = (dg * yv * sg * (1.0 + z * (1.0 - sg))).astype(dz_ref.dtype)

        @pl.when(pl.program_id(0) == 0)
        def _():
            dw_ref[...] = jnp.zeros_like(dw_ref)

        dw_ref[...] += jnp.sum(dyn_v * g * r, axis=0, keepdims=True)

    row = pl.BlockSpec((tr, d), lambda i: (i, 0))
    vec = pl.BlockSpec((1, d), lambda i: (0, 0))
    return _call(body, name, [S((rows, d), F32), S((rows, zx.shape[1]), BF16), S((1, d), F32)], (rows // tr,),
                 [row, row, vec, row], [row, row, vec], ("arbitrary",), (y, zx, w, dyn), comm=comm)


def _shift_down(x, s, rows_iota):
    if s == 0:
        return x
    return jnp.where(rows_iota >= s, pltpu.roll(x, s, 0), 0.0)


def _shift_up(x, s, rows_iota):
    if s == 0:
        return x
    rows = x.shape[0]
    return jnp.where(rows_iota < rows - s, pltpu.roll(x, rows - s, 0), 0.0)


def _r16(v):
    return v.astype(BF16).astype(F32)


def _conv_taps(x, taps, rows_iota):
    x = _r16(x)
    return [_shift_down(x, taps - 1 - k, rows_iota) for k in range(taps)]


def _conv(x, w_ref, b_ref, taps, rows_iota, shifted=None):
    shifted = _conv_taps(x, taps, rows_iota) if shifted is None else shifted
    acc = jnp.zeros_like(shifted[0])
    for k in range(taps):
        acc = acc + _r16(w_ref[k:k + 1, :]) * shifted[k]
    return acc + b_ref[...]


def _conv_bwd(shifted, du, w_ref, dw_ref, db_ref, taps, rows_iota):
    db_ref[...] = jnp.sum(du, axis=0, keepdims=True)
    du = _r16(du)
    dx = jnp.zeros_like(du)
    for k in range(taps):
        dx = dx + _r16(w_ref[k:k + 1, :]) * _shift_up(du, taps - 1 - k, rows_iota)
        dw_ref[k:k + 1, :] = jnp.sum(du * shifted[k], axis=0, keepdims=True)
    return dx


def _conv_silu_fwd(zx, w, b, name, comm=None):
    rows = zx.shape[0]
    cb = 512
    off = D_INNER // cb

    def body(x_ref, w_ref, b_ref, o_ref):
        it = lax.broadcasted_iota(jnp.int32, (rows, 1), 0)
        u = _conv(x_ref[...], w_ref, b_ref, SSM_CONV, it)
        o_ref[...] = u * _sigmoid(u)

    return _call(
        body, name, S((rows, D_XBC), F32), (D_XBC // cb,),
        [pl.BlockSpec((rows, cb), lambda j: (0, off + j)), pl.BlockSpec((SSM_CONV, cb), lambda j: (0, j)),
         pl.BlockSpec((1, cb), lambda j: (0, j))],
        pl.BlockSpec((rows, cb), lambda j: (0, j)), ("parallel",), (zx, w, b), comm=comm)


def _conv_silu_bwd(zx, dxs, dbm, dcm, w, b, dzx, name, comm=None):
    rows = zx.shape[0]
    cb = 256
    off = D_INNER // cb
    nx, nbc = D_INNER // cb, D_BC // cb

    def body(x_ref, dx_in, db_in, dc_in, w_ref, b_ref, dzx_in, dx_ref, dw_ref, db_ref, dbuf):
        del dzx_in
        j = pl.program_id(0)
        for cond, src in ((j < nx, dx_in), ((j >= nx) & (j < nx + nbc), db_in), (j >= nx + nbc, dc_in)):
            @pl.when(cond)
            def _(src=src):
                dbuf[...] = src[...]
        it = lax.broadcasted_iota(jnp.int32, (rows, 1), 0)
        xs = _conv_taps(x_ref[...], SSM_CONV, it)
        u = _conv(None, w_ref, b_ref, SSM_CONV, it, xs)
        sg = _sigmoid(u)
        du = dbuf[...] * sg * (1.0 + u * (1.0 - sg))
        dx_ref[...] = _conv_bwd(xs, du, w_ref, dw_ref, db_ref, SSM_CONV, it).astype(dx_ref.dtype)

    def part(first, count):
        return pl.BlockSpec((rows, cb), lambda j: (0, jnp.clip(j - first, 0, count - 1)))

    col = pl.BlockSpec((rows, cb), lambda j: (0, j))
    wsp = pl.BlockSpec((SSM_CONV, cb), lambda j: (0, j))
    bsp = pl.BlockSpec((1, cb), lambda j: (0, j))
    xbc_cols = pl.BlockSpec((rows, cb), lambda j: (0, off + j))
    return _call(
        body, name, [S(dzx.shape, dzx.dtype), S((SSM_CONV, D_XBC), F32), S((1, D_XBC), F32)], (D_XBC // cb,),
        [xbc_cols, part(0, nx), part(nx, nbc), part(nx + nbc, nbc), wsp, bsp, _ANY],
        [xbc_cols, wsp, bsp], ("arbitrary",), (zx, dxs, dbm, dcm, w, b, dzx), scratch=[pltpu.VMEM((rows, cb), F32)],
        comm=comm, aliases={6: 0})


def _ffn_act_fwd(u, w, b, name, comm=None):
    rows = u.shape[0]
    cb = 256
    nb = D_FF // cb

    def body(g_ref, v_ref, wg_ref, wv_ref, bg_ref, bv_ref, o_ref):
        it = lax.broadcasted_iota(jnp.int32, (rows, 1), 0)
        g = _conv(g_ref[...], wg_ref, bg_ref, FFN_CONV, it)
        v = _conv(v_ref[...], wv_ref, bv_ref, FFN_CONV, it)
        o_ref[...] = (g * _sigmoid(g) * v).astype(o_ref.dtype)

    def sp(r, shift):
        return pl.BlockSpec((r, cb), lambda j: (0, shift + j))

    return _call(
        body, name, S((rows, D_FF), BF16), (nb,),
        [sp(rows, 0), sp(rows, nb), sp(FFN_CONV, 0), sp(FFN_CONV, nb), sp(1, 0), sp(1, nb)],
        sp(rows, 0), ("parallel",), (u, u, w, w, b, b), comm=comm)


def _ffn_act_bwd(u, dact, w, b, name, comm=None):
    rows = u.shape[0]
    cb = 256
    nb = D_FF // cb

    def body(g_ref, v_ref, d_ref, wg_ref, wv_ref, bg_ref, bv_ref, du_ref, dw_ref, db_ref):
        it = lax.broadcasted_iota(jnp.int32, (rows, 1), 0)
        xg, xv = _conv_taps(g_ref[...], FFN_CONV, it), _conv_taps(v_ref[...], FFN_CONV, it)
        g = _conv(None, wg_ref, bg_ref, FFN_CONV, it, xg)
        v = _conv(None, wv_ref, bv_ref, FFN_CONV, it, xv)
        sg = _sigmoid(g)
        d = d_ref[...]
        dgate = d * v * sg * (1.0 + g * (1.0 - sg))
        dval = d * g * sg
        du_ref[0] = _conv_bwd(xg, dgate, wg_ref, dw_ref.at[0], db_ref.at[0], FFN_CONV, it).astype(du_ref.dtype)
        du_ref[1] = _conv_bwd(xv, dval, wv_ref, dw_ref.at[1], db_ref.at[1], FFN_CONV, it).astype(du_ref.dtype)

    def sp(r, shift):
        return pl.BlockSpec((r, cb), lambda j: (0, shift + j))

    def both(r):
        return pl.BlockSpec((2, r, cb), lambda j: (0, 0, j))

    return _call(
        body, name, [S((2, rows, D_FF), BF16), S((2, FFN_CONV, D_FF), F32), S((2, 1, D_FF), F32)], (nb,),
        [sp(rows, 0), sp(rows, nb), sp(rows, 0), sp(FFN_CONV, 0), sp(FFN_CONV, nb), sp(1, 0), sp(1, nb)],
        [both(rows), both(FFN_CONV), both(1)], ("parallel",), (u, u, dact, w, w, b, b), comm=comm)


def _ssd_consts(dtp_ref, bias_ref, alog_ref, hg):
    lane = lax.broadcasted_iota(jnp.int32, (1, LANE), 1)
    pre = dtp_ref[...] + bias_ref[...]
    dt = _softplus(pre)
    a_row = jnp.where(lane < hg, -jnp.exp(alog_ref[...]), 0.0)
    ri = lax.broadcasted_iota(jnp.int32, (T, T), 0)
    ci = lax.broadcasted_iota(jnp.int32, (T, T), 1)
    cs = _dot_hi((ri >= ci).astype(F32), dt * a_row)
    return pre, dt, a_row, cs, ri, ci, lane


def _head_rows(src, hg):
    return jnp.concatenate([jnp.broadcast_to(src[k:k + 1, :], (HEAD_P, src.shape[1])) for k in range(hg)], axis=0)


def _ssd_fwd(xbc, zx, bias, alog, dsk, name, comm=None):
    rows = xbc.shape[0]
    nc = rows // T
    hg = SSM_HEADS // SSM_GROUPS
    gw = hg * HEAD_P
    xoff, boff, coff = 0, D_INNER // D_STATE, (D_INNER + D_BC) // D_STATE
    dtoff = (D_INNER + D_XBC) // LANE

    def body(x_ref, b_ref, c_ref, dtp_ref, bias_ref, alog_ref, dsk_ref, y_ref, hst_ref, hs):
        c = pl.program_id(1)

        @pl.when(c == 0)
        def _():
            hs[...] = jnp.zeros_like(hs)

        _, dt, _, cs, ri, ci, _ = _ssd_consts(dtp_ref, bias_ref, alog_ref, hg)
        cst, dtt = cs.T, dt.T
        xt = x_ref[...].T
        bb, cbf = b_ref[...].astype(BF16), c_ref[...].astype(BF16)
        gt = _dot(bb, cbf, _NT)
        causal_t = ci >= ri
        dskv = dsk_ref[...]
        hall = hs[...]
        hst_ref[0, 0] = hall
        cs8 = cst[0:8, :]
        cl8 = cs8[:, T - 1:T]
        xdt = xt * _head_rows(dtt, hg)
        yo = _head_rows(jnp.exp(cs8), hg) * _dot(hall.astype(BF16), cbf, _NT)
        st = _dot((xdt * _head_rows(jnp.exp(cl8 - cs8), hg)).astype(BF16), bb)
        hs[...] = _head_rows(jnp.exp(cl8), hg) * hall + st
        yds = []
        for k in range(hg):
            sl = slice(k * HEAD_P, (k + 1) * HEAD_P)
            lt = jnp.exp(jnp.where(causal_t, cst[k:k + 1, :] - cs[:, k:k + 1], NEG))
            yds.append(_dot(xdt[sl, :].astype(BF16), (gt * lt).astype(BF16)))
        dsk_r = jnp.concatenate([jnp.broadcast_to(dskv[:, k:k + 1], (HEAD_P, 1)) for k in range(hg)], axis=0)
        y_ref[...] = (jnp.concatenate(yds, axis=0) + yo + dsk_r * xt).T

    vec = pl.BlockSpec((1, LANE), lambda g, c: (0, g))
    return _call(
        body, name, [S((rows, D_INNER), F32), S((nc, SSM_GROUPS, gw, D_STATE), F32)], (SSM_GROUPS, nc),
        [pl.BlockSpec((T, gw), lambda g, c: (c, xoff + g)),
         pl.BlockSpec((T, D_STATE), lambda g, c: (c, boff + g)),
         pl.BlockSpec((T, D_STATE), lambda g, c: (c, coff + g)),
         pl.BlockSpec((T, LANE), lambda g, c: (c, dtoff + g)), vec, vec, vec],
        [pl.BlockSpec((T, gw), lambda g, c: (c, g)), pl.BlockSpec((1, 1, gw, D_STATE), lambda g, c: (c, g, 0, 0))],
        ("parallel", "arbitrary"), (xbc, xbc, xbc, zx, bias, alog, dsk),
        scratch=[pltpu.VMEM((gw, D_STATE), F32)], comm=comm)


def _ssd_bwd(xbc, zx, bias, alog, dsk, dy, hst, dzx, name, comm=None):
    rows = xbc.shape[0]
    nc = rows // T
    hg = SSM_HEADS // SSM_GROUPS
    gw = hg * HEAD_P
    boff, coff = D_INNER // D_STATE, (D_INNER + D_BC) // D_STATE
    dtoff = (D_INNER + D_XBC) // LANE

    def body(x_ref, b_ref, c_ref, dtp_ref, bias_ref, alog_ref, dsk_ref, dy_ref, hst_ref, dzx_in,
             dx_ref, db_ref, dc_ref, ddtp_ref, dalog_ref, ddsk_ref, dbias_ref, dhs):
        del dzx_in
        step = pl.program_id(1)

        @pl.when(step == 0)
        def _():
            dhs[...] = jnp.zeros_like(dhs)
            dalog_ref[...] = jnp.zeros_like(dalog_ref)
            ddsk_ref[...] = jnp.zeros_like(ddsk_ref)
            dbias_ref[...] = jnp.zeros_like(dbias_ref)

        pre, dt, a_row, cs, ri, ci, lane = _ssd_consts(dtp_ref, bias_ref, alog_ref, hg)
        cst, dtt = cs.T, dt.T
        xt, dyt = x_ref[...].T, dy_ref[...].T
        bb, cbf = b_ref[...].astype(BF16), c_ref[...].astype(BF16)
        gt = _dot(bb, cbf, _NT)
        causal_t = ci >= ri
        dskv = dsk_ref[...]
        hall, dhall = hst_ref[0, 0], dhs[...]
        head_row = lax.broadcasted_iota(jnp.int32, (T, 1), 0)
        last_l = lax.broadcasted_iota(jnp.int32, (1, T), 1) == T - 1
        cs8, dt8 = cst[0:8, :], dtt[0:8, :]
        cl8 = cs8[:, T - 1:T]
        e8, wdec8 = jnp.exp(cs8), jnp.exp(cl8 - cs8)
        w8 = wdec8 * dt8
        dt_r, e_r, w_r, ecl_r = _head_rows(dt8, hg), _head_rows(e8, hg), _head_rows(w8, hg), _head_rows(jnp.exp(cl8), hg)
        dsk_r = jnp.concatenate([jnp.broadcast_to(dskv[:, k:k + 1], (HEAD_P, 1)) for k in range(hg)], axis=0)
        hb, dhb = hall.astype(BF16), dhall.astype(BF16)
        xdt = xt * dt_r
        dye = (dyt * e_r).astype(BF16)
        rt = _dot(dhb, bb, _NT)
        yo = e_r * _dot(hb, cbf, _NT)
        dhs[...] = ecl_r * dhall + _dot(dye, cbf)
        dc_acc = _dot(dye, hb, _TN)
        db_acc = _dot((xt * w_r).astype(BF16), dhb, _TN)
        rtx, dyyo, hdh, dyx = rt * xt, dyt * yo, dhall * hall, dyt * xt
        dgt = jnp.zeros((T, T), F32)
        ddt_rows = jnp.zeros((T, T), F32)
        dcs_rows = jnp.zeros((T, T), F32)
        qrow_cols = jnp.zeros((T, LANE), F32)
        ddsk_acc = jnp.zeros((1, LANE), F32)
        dxdts = []
        for k in range(hg):
            sl = slice(k * HEAD_P, (k + 1) * HEAD_P)
            lt = jnp.exp(jnp.where(causal_t, cst[k:k + 1, :] - cs[:, k:k + 1], NEG))
            mpt = gt * lt
            dyb = dyt[sl, :].astype(BF16)
            dxdt = _dot(dyb, mpt.astype(BF16), _NT)
            dmt = _dot(xdt[sl, :].astype(BF16), dyb, _TN)
            dgt = dgt + dmt * lt
            q = dmt * mpt
            q_rows = jnp.sum(q, axis=1, keepdims=True)
            q_cols = jnp.sum(q, axis=0, keepdims=True)
            dxdts.append(dxdt)
            xz = jnp.sum(xt[sl, :] * dxdt, axis=0, keepdims=True)
            dw = jnp.sum(rtx[sl, :], axis=0, keepdims=True)
            wk, wdeck = w8[k:k + 1, :], wdec8[k:k + 1, :]
            dcl = jnp.exp(cl8[k:k + 1, :]) * jnp.sum(hdh[sl, :]) + jnp.sum(dw * wk)
            dcs_r = jnp.sum(dyyo[sl, :], axis=0, keepdims=True) + q_cols - dw * wk + jnp.where(last_l, dcl, 0.0)
            onehot = (lane == k).astype(F32)
            ddt_rows = ddt_rows + jnp.where(head_row == k, xz + dw * wdeck, 0.0)
            dcs_rows = dcs_rows + jnp.where(head_row == k, dcs_r, 0.0)
            qrow_cols = qrow_cols + q_rows * onehot
            ddsk_acc = ddsk_acc + jnp.sum(dyx[sl, :]) * onehot
        dx_ref[...] = (dt_r * jnp.concatenate(dxdts, axis=0) + dsk_r * dyt + rt * w_r).T
        dc_ref[...] = _dot(dgt.T.astype(BF16), bb) + dc_acc
        db_ref[...] = _dot(dgt.astype(BF16), cbf) + db_acc
        da = _dot_hi((ci >= ri).astype(F32), dcs_rows.T - qrow_cols)
        ddtp = (ddt_rows.T + da * a_row) * _sigmoid(pre)
        ddtp = jnp.where(lane < hg, ddtp, 0.0)
        ddtp_ref[...] = ddtp.astype(ddtp_ref.dtype)
        dbias_ref[...] += jnp.sum(ddtp, axis=0, keepdims=True)
        dalog_ref[...] += jnp.sum(da * dt, axis=0, keepdims=True) * a_row
        ddsk_ref[...] += ddsk_acc

    def rc(c):
        return nc - 1 - c

    vec = pl.BlockSpec((1, LANE), lambda g, c: (0, g))
    xsp = pl.BlockSpec((T, gw), lambda g, c: (rc(c), g))
    return _call(
        body, name,
        [S((rows, D_INNER), F32), S((rows, D_BC), F32), S((rows, D_BC), F32),
         S(dzx.shape, dzx.dtype), S((1, SSM_GROUPS * LANE), F32),
         S((1, SSM_GROUPS * LANE), F32), S((1, SSM_GROUPS * LANE), F32)],
        (SSM_GROUPS, nc),
        [xsp,
         pl.BlockSpec((T, D_STATE), lambda g, c: (rc(c), boff + g)),
         pl.BlockSpec((T, D_STATE), lambda g, c: (rc(c), coff + g)),
         pl.BlockSpec((T, LANE), lambda g, c: (rc(c), dtoff + g)), vec, vec, vec,
         xsp, pl.BlockSpec((1, 1, gw, D_STATE), lambda g, c: (rc(c), g, 0, 0)), _ANY],
        [xsp,
         pl.BlockSpec((T, D_STATE), lambda g, c: (rc(c), g)),
         pl.BlockSpec((T, D_STATE), lambda g, c: (rc(c), g)),
         pl.BlockSpec((T, LANE), lambda g, c: (rc(c), dtoff + g)), vec, vec, vec],
        ("parallel", "arbitrary"), (xbc, xbc, xbc, zx, bias, alog, dsk, dy, hst, dzx),
        scratch=[pltpu.VMEM((gw, D_STATE), F32)], comm=comm, aliases={9: 3})


def _attn_tiles(kv_ref, j):
    prev = jnp.maximum(j - 1, 0)
    meta = kv_ref[0:T, :]
    prv = kv_ref[pl.ds(pl.multiple_of(prev * T, T), T), :]
    cur = kv_ref[pl.ds(pl.multiple_of(j * T, T), T), :]
    return jnp.concatenate([meta, prv, cur], axis=0)


def _attn_mask(j):
    r = j * T + lax.broadcasted_iota(jnp.int32, (3 * T, T), 1)
    row = lax.broadcasted_iota(jnp.int32, (3 * T, T), 0)
    t0, t1 = row < T, row < 2 * T
    s = jnp.where(t0, row, (j - 2) * T + row)
    ok = (s <= r) & ((s < N_META) | (s > r - WINDOW))
    use = (t0 & (j >= 2) & (row < N_META)) | (jnp.logical_not(t0) & t1 & (j >= 1)) | jnp.logical_not(t1)
    return ok & use


def _attn_fwd(q, kv, sinks, name, comm=None):
    rows = q.shape[0]
    scale = 1.0 / math.sqrt(ATTN_DH)
    qpk = N_Q_HEADS // N_KV_HEADS

    def body(q_ref, kv_ref, s_ref, o_ref, lse_ref):
        j = pl.program_id(0)
        kv3 = _attn_tiles(kv_ref, j).astype(BF16)
        mask = _attn_mask(j)
        qv = (q_ref[...] * scale).astype(BF16)
        sk = s_ref[...]
        lses = []
        for kh in range(N_KV_HEADS):
            k3 = kv3[:, kh * ATTN_DH:(kh + 1) * ATTN_DH]
            v3 = kv3[:, D_KV + kh * ATTN_DH:D_KV + (kh + 1) * ATTN_DH]
            for g in range(qpk):
                h = kh * qpk + g
                sink = sk[:, h:h + 1]
                sc = jnp.where(mask, _dot(k3, qv[:, h * ATTN_DH:(h + 1) * ATTN_DH], _NT), NEG)
                m = jnp.maximum(jnp.max(sc, axis=0, keepdims=True), sink)
                p = jnp.exp(sc - m)
                den = jnp.sum(p, axis=0, keepdims=True) + jnp.exp(sink - m)
                p = p * (1.0 / den)
                lses.append(m + jnp.log(den))
                o_ref[:, h * ATTN_DH:(h + 1) * ATTN_DH] = _dot(p.astype(BF16), v3, _TN).astype(o_ref.dtype)
        lse_ref[...] = jnp.concatenate(lses, axis=0)

    return _call(
        body, name, [S((rows, D_MODEL), BF16), S((N_Q_HEADS, rows), F32)], (rows // T,),
        [pl.BlockSpec((T, D_MODEL), lambda j: (j, 0)), pl.BlockSpec((rows, 2 * D_KV), lambda j: (0, 0)),
         pl.BlockSpec((1, N_Q_HEADS), lambda j: (0, 0))],
        [pl.BlockSpec((T, D_MODEL), lambda j: (j, 0)), pl.BlockSpec((N_Q_HEADS, T), lambda j: (0, j))],
        ("parallel",), (q, kv, sinks), comm=comm)


def _attn_bwd(q, kv, sinks, do, lse, name, comm=None):
    rows = q.shape[0]
    scale = 1.0 / math.sqrt(ATTN_DH)
    qpk = N_Q_HEADS // N_KV_HEADS

    def body(q_ref, kv_ref, s_ref, do_ref, lse_ref, dq_ref, dkv_ref, ds_ref):
        j = pl.program_id(0)

        @pl.when(j == 0)
        def _():
            dkv_ref[...] = jnp.zeros_like(dkv_ref)
            ds_ref[...] = jnp.zeros_like(ds_ref)

        kv3 = _attn_tiles(kv_ref, j).astype(BF16)
        mask = _attn_mask(j)
        qv = (q_ref[...] * scale).astype(BF16)
        dov = do_ref[...].astype(BF16)
        sk = s_ref[...]
        lsev = lse_ref[...]
        lane = lax.broadcasted_iota(jnp.int32, (1, LANE), 1)
        ds_acc = jnp.zeros((1, LANE), F32)
        prev = jnp.maximum(j - 1, 0)
        mask4 = jnp.concatenate([mask] * qpk, axis=1)
        dqts = []
        for kh in range(N_KV_HEADS):
            ksl = slice(kh * ATTN_DH, (kh + 1) * ATTN_DH)
            vsl = slice(D_KV + kh * ATTN_DH, D_KV + (kh + 1) * ATTN_DH)
            k3, v3 = kv3[:, ksl], kv3[:, vsl]
            heads = [kh * qpk + g for g in range(qpk)]
            q4 = jnp.concatenate([qv[:, h * ATTN_DH:(h + 1) * ATTN_DH] for h in heads], axis=0)
            do4 = jnp.concatenate([dov[:, h * ATTN_DH:(h + 1) * ATTN_DH] for h in heads], axis=0)
            lse4 = jnp.concatenate([lsev[h:h + 1, :] for h in heads], axis=1)
            sink4 = jnp.concatenate([jnp.broadcast_to(sk[:, h:h + 1], (1, T)) for h in heads], axis=1)
            p = jnp.exp(jnp.where(mask4, _dot(k3, q4, _NT), NEG) - lse4)
            ps = jnp.exp(sink4 - lse4)
            dp = _dot(v3, do4, _NT)
            delta = jnp.sum(p * dp, axis=0, keepdims=True)
            dsc = (p * (dp - delta)).astype(BF16)
            dq4 = _dot(k3.T, dsc) * scale
            dk3 = _dot(dsc, q4)
            dv3 = _dot(p.astype(BF16), do4)
            psd = ps * delta
            for g, h in enumerate(heads):
                dqts.append(dq4[:, g * T:(g + 1) * T])
                ds_acc = ds_acc - jnp.sum(psd[:, g * T:(g + 1) * T]) * (lane == h).astype(F32)
            for t, start in enumerate((0, pl.multiple_of(prev * T, T), pl.multiple_of(j * T, T))):
                rsl = pl.ds(start, T)
                dkv_ref[rsl, ksl] += dk3[t * T:(t + 1) * T, :]
                dkv_ref[rsl, vsl] += dv3[t * T:(t + 1) * T, :]
        ds_ref[...] += ds_acc
        dq_ref[...] = jnp.concatenate(dqts, axis=0).T.astype(dq_ref.dtype)

    blk = pl.BlockSpec((T, D_MODEL), lambda j: (j, 0))
    full = pl.BlockSpec((rows, 2 * D_KV), lambda j: (0, 0))
    return _call(
        body, name, [S((rows, D_MODEL), BF16), S((rows, 2 * D_KV), F32), S((1, LANE), F32)], (rows // T,),
        [blk, full, pl.BlockSpec((1, N_Q_HEADS), lambda j: (0, 0)), blk, pl.BlockSpec((N_Q_HEADS, T), lambda j: (0, j))],
        [blk, full, pl.BlockSpec((1, LANE), lambda j: (0, 0))], ("arbitrary",), (q, kv, sinks, do, lse), comm=comm)


BLOCK_BYTES = 1 << 20


def _div_tile(rows, cols, block_bytes=BLOCK_BYTES):
    cap = max(16, block_bytes // (4 * cols))
    best = None
    for t in range(16, min(rows, cap) + 1, 16):
        if rows % t == 0:
            best = t
    return best if best is not None else rows


def _adamw(parts, w, m, v, name, comm=None):
    layers, rows, cols = w.shape
    n = parts[0].shape[0]
    tr = _div_tile(rows, cols)
    tc = _pick(cols, 256) if tr == rows and rows * cols * 4 > 2 * BLOCK_BYTES else cols
    c1 = 1.0 / (1.0 - B1 ** STEP)
    c2 = 1.0 / (1.0 - B2 ** STEP)

    def body(*refs):
        p_refs = refs[:layers]
        w_ref, m_ref, v_ref, g_ref, d_ref, nm_ref, nv_ref = refs[layers:]
        layer = pl.program_id(0)
        for l in range(layers):
            @pl.when(layer == l)
            def _(p_ref=p_refs[l]):
                g = p_ref[0].astype(F32)
                for i in range(1, n):
                    g = g + p_ref[i].astype(F32)
                nm = B1 * m_ref[...] + (1.0 - B1) * g
                nv = B2 * v_ref[...] + (1.0 - B2) * (g * g)
                g_ref[...] = g
                nm_ref[...] = nm
                nv_ref[...] = nv
                d_ref[...] = -LR * ((nm * c1) / (jnp.sqrt(nv * c2) + EPS) + WD * w_ref[...])

    def part_spec(l):
        return pl.BlockSpec((n, tr, tc), lambda k, i, j: (0, jnp.where(k == l, i, 0), jnp.where(k == l, j, 0)))

    row = pl.BlockSpec((None, tr, tc), lambda k, i, j: (k, i, j))
    return _call(body, name, [S((layers, rows, cols), F32)] * 4, (layers, rows // tr, cols // tc),
                 [part_spec(l) for l in range(layers)] + [row, row, row], [row] * 4,
                 ("parallel", "parallel", "parallel"), (*parts, w, m, v), comm=comm)


def _sum_parts(parts, name):
    n, rows, cols = parts[0].shape
    nb = len(parts)
    tr = _div_tile(rows, cols, 2 * BLOCK_BYTES)

    def body(*refs):
        o_ref = refs[nb]
        blk = pl.program_id(0)
        for l in range(nb):
            @pl.when(blk == l)
            def _(p_ref=refs[l]):
                g = p_ref[0].astype(F32)
                for i in range(1, n):
                    g = g + p_ref[i].astype(F32)
                o_ref[...] = g

    def part_spec(l):
        return pl.BlockSpec((n, tr, cols), lambda k, i: (0, jnp.where(k == l, i, 0), 0))

    per = rows // tr
    return pl.pallas_call(body, name=name, out_shape=S((nb * rows, cols), F32), grid=(nb, per),
                          in_specs=[part_spec(l) for l in range(nb)],
                          out_specs=pl.BlockSpec((tr, cols), lambda k, i: (k * per + i, 0)),
                          compiler_params=_cp(("parallel", "parallel")))(*parts)


def _col_segments(ws, runs):
    segs = []
    for glo, mlo, n in runs:
        while n > 0:
            d, off = divmod(glo, ws)
            take = min(n, ws - off)
            segs.append((d, off, mlo, take))
            glo, mlo, n = glo + take, mlo + take, n - take
    return segs


def _assemble_cols(gs, width, segs, name):
    _, rows, ws = gs[0].shape
    nb = len(gs)
    rb = _div_tile(rows, width // 2, 4 * BLOCK_BYTES)
    per = rows // rb

    def body(*refs):
        o_ref = refs[nb]
        piece = pl.program_id(0)
        for l in range(nb):
            @pl.when(piece == l)
            def _(g_ref=refs[l]):
                o_ref[...] = jnp.zeros_like(o_ref)
                for d, off, mlo, n in segs:
                    o_ref[:, mlo:mlo + n] = g_ref[d, :, off:off + n]

    def piece_spec(l):
        return pl.BlockSpec((N_DEV, rb, ws), lambda k, i: (0, jnp.where(k == l, i, 0), 0))

    return pl.pallas_call(
        body, name=name, out_shape=S((nb * rows, width), gs[0].dtype), grid=(nb, per),
        in_specs=[piece_spec(l) for l in range(nb)],
        out_specs=pl.BlockSpec((rb, width), lambda k, i: (k * per + i, 0)),
        compiler_params=_cp(("parallel", "parallel")))(*gs)


def _scatter_cols(dw, ws, segs, name):
    rows, width = dw.shape
    rb = _div_tile(rows, width, 4 * BLOCK_BYTES)

    def body(w_ref, o_ref):
        for d, off, mlo, n in segs:
            o_ref[d, :, off:off + n] = w_ref[:, mlo:mlo + n].astype(o_ref.dtype)

    return pl.pallas_call(
        body, name=name, out_shape=S((N_DEV, rows, ws), BF16), grid=(rows // rb,),
        in_specs=[pl.BlockSpec((rb, width), lambda i: (i, 0))],
        out_specs=pl.BlockSpec((N_DEV, rb, ws), lambda i: (0, i, 0)), compiler_params=_cp(("parallel",)))(dw)


def _gather_comm(xs):
    n = len(xs)

    def setup(x_refs, out_refs, sems):
        send_sems, recv_sems, local_sems = sems
        mx, my, mc = lax.axis_index("x"), lax.axis_index("y"), lax.axis_index("c")
        me, sibling = (mx, my, mc), (mx, my, 1 - mc)
        chips = [(1 - mx, my), (mx, 1 - my), (1 - mx, 1 - my)]

        def blk(a, px, py, pc):
            return out_refs[a].at[4 * px + 2 * py + pc]

        def copy(a, k, block, to, src=None):
            return pltpu.make_async_remote_copy(
                src_ref=blk(a, *block) if src is None else src, dst_ref=blk(a, *block),
                send_sem=send_sems.at[a, k], recv_sem=recv_sems.at[a, k], device_id=to, device_id_type=_MESH)

        mine = [pltpu.make_async_copy(x_refs[a], blk(a, *me), local_sems.at[a]) for a in range(n)]
        own = []
        for a in range(n):
            own.append(copy(a, 0, me, sibling, src=x_refs[a]))
            own += [copy(a, 1 + i, me, (*chip, mc), src=x_refs[a]) for i, chip in enumerate(chips)]
        return me, sibling, chips, mc, copy, mine, own

    def first(x_refs, out_refs, sems):
        _, _, _, _, _, mine, own = setup(x_refs, out_refs, sems)
        for cp in mine + own:
            cp.start()

    def last(x_refs, out_refs, sems):
        me, sibling, chips, mc, copy, mine, own = setup(x_refs, out_refs, sems)
        passed = []
        for a in range(n):
            for i, chip in enumerate(chips):
                copy(a, 1 + i, (*chip, mc), me).wait_recv()
                passed.append(copy(a, 4 + i, (*chip, mc), sibling))
                passed[-1].start()
        for a in range(n):
            copy(a, 0, sibling, me).wait_recv()
            for i, chip in enumerate(chips):
                copy(a, 4 + i, (*chip, 1 - mc), me).wait_recv()
        for cp in own + passed:
            cp.wait_send()
        for cp in mine:
            cp.wait()

    return _Comm(list(xs), [S((N_DEV,) + x.shape, x.dtype) for x in xs],
                 [pltpu.SemaphoreType.DMA((n, 7)), pltpu.SemaphoreType.DMA((n, 7)), pltpu.SemaphoreType.DMA((n,))],
                 first, last)


def _swap_comm(gs):
    n = len(gs)

    def copies(g_refs, out_refs, sems):
        send_sems, recv_sems = sems
        mx, my, mc = lax.axis_index("x"), lax.axis_index("y"), lax.axis_index("c")
        return [pltpu.make_async_remote_copy(
            src_ref=g_refs[a].at[2 * k + 1 - mc], dst_ref=out_refs[a].at[k], send_sem=send_sems.at[a, k],
            recv_sem=recv_sems.at[a, k], device_id=(mx, my, 1 - mc), device_id_type=_MESH)
            for a in range(n) for k in range(4)]

    def first(g_refs, out_refs, sems):
        for cp in copies(g_refs, out_refs, sems):
            cp.start()

    def last(g_refs, out_refs, sems):
        for cp in copies(g_refs, out_refs, sems):
            cp.wait()

    return _Comm(list(gs), [S((4,) + g.shape[1:], g.dtype) for g in gs],
                 [pltpu.SemaphoreType.DMA((n, 4)), pltpu.SemaphoreType.DMA((n, 4))], first, last)


def _chips_comm(parts):
    n = len(parts)

    def copies(p_refs, out_refs, sems):
        send_sems, recv_sems, local_sems = sems
        mx, my, mc = lax.axis_index("x"), lax.axis_index("y"), lax.axis_index("c")
        mychip = 2 * mx + my
        chips = [(1 - mx, my), (mx, 1 - my), (1 - mx, 1 - my)]
        mine = [pltpu.make_async_copy(p_refs[a].at[mychip], out_refs[a].at[mychip], local_sems.at[a])
                for a in range(n)]
        return mine + [pltpu.make_async_remote_copy(
            src_ref=p_refs[a].at[2 * cx + cy], dst_ref=out_refs[a].at[mychip], send_sem=send_sems.at[a, i],
            recv_sem=recv_sems.at[a, i], device_id=(cx, cy, mc), device_id_type=_MESH)
            for a in range(n) for i, (cx, cy) in enumerate(chips)]

    def first(p_refs, out_refs, sems):
        for cp in copies(p_refs, out_refs, sems):
            cp.start()

    def last(p_refs, out_refs, sems):
        for cp in copies(p_refs, out_refs, sems):
            cp.wait()

    return _Comm(list(parts), [S(p.shape, p.dtype) for p in parts],
                 [pltpu.SemaphoreType.DMA((n, 3)), pltpu.SemaphoreType.DMA((n, 3)), pltpu.SemaphoreType.DMA((n,))],
                 first, last)


def _join_comms(comms):
    def split(refs, counts):
        out, p = [], 0
        for cnt in counts:
            out.append(refs[p:p + cnt])
            p += cnt
        return out

    ni = [len(c.ins) for c in comms]
    no = [len(c.out_shapes) for c in comms]
    ns = [len(c.scratch) for c in comms]

    def first(in_refs, out_refs, sems):
        for c, i, o, s in zip(comms, split(in_refs, ni), split(out_refs, no), split(sems, ns)):
            c.first(i, o, s)

    def last(in_refs, out_refs, sems):
        for c, i, o, s in zip(comms, split(in_refs, ni), split(out_refs, no), split(sems, ns)):
            c.last(i, o, s)

    return _Comm([x for c in comms for x in c.ins], [x for c in comms for x in c.out_shapes],
                 [x for c in comms for x in c.scratch], first, last)


def _add_pairs(mine, theirs, core, name):
    _, rows, cols = mine.shape
    tr = _div_tile(rows, cols // 2)

    def body(core_ref, a_ref, b_ref, o_ref):
        o_ref[...] = (a_ref[...].astype(F32) + b_ref[...].astype(F32)).astype(o_ref.dtype)

    return pl.pallas_call(
        body, name=name, out_shape=S((4, rows, cols), BF16),
        grid_spec=pltpu.PrefetchScalarGridSpec(
            num_scalar_prefetch=1, grid=(4, rows // tr),
            in_specs=[pl.BlockSpec((None, tr, cols), lambda k, i, c: (2 * k + c[0], i, 0)),
                      pl.BlockSpec((None, tr, cols), lambda k, i, c: (k, i, 0))],
            out_specs=pl.BlockSpec((None, tr, cols), lambda k, i, c: (k, i, 0))),
        compiler_params=_cp(("parallel", "parallel")))(core, mine, theirs)


def _run_comm(comm, name):
    ci, co = len(comm.ins), len(comm.out_shapes)

    def body(*refs):
        comm.first(refs[:ci], refs[ci:ci + co], refs[ci + co:])
        comm.last(refs[:ci], refs[ci:ci + co], refs[ci + co:])

    return pl.pallas_call(body, name=name, out_shape=list(comm.out_shapes), in_specs=[_HBM] * ci,
                          out_specs=[_HBM] * co, scratch_shapes=list(comm.scratch))(*comm.ins)


def _flat_rows(n_elems, mult):
    rows = -(-n_elems // LANE)
    return -(-rows // mult) * mult


def _pack(arrs, lead, mult, dtype):
    lead_shape = arrs[0].shape[:lead]
    flat = jnp.concatenate([a.astype(dtype).reshape(lead_shape + (-1,)) for a in arrs], axis=-1)
    n = flat.shape[-1]
    rows = _flat_rows(n, mult)
    flat = jnp.pad(flat, [(0, 0)] * lead + [(0, rows * LANE - n)])
    return flat.reshape(lead_shape + (rows, LANE))


def _unpack(flat, lead, shapes):
    lead_shape = flat.shape[:lead]
    flat = flat.reshape(lead_shape + (-1,))
    out, off = [], 0
    for shp in shapes:
        n = math.prod(shp)
        out.append(flat[..., off:off + n].reshape(lead_shape + tuple(shp)))
        off += n
    return out


def _split8(full, ax, n):
    shp = full.shape
    return jnp.moveaxis(full.reshape(shp[:ax] + (N_DEV, n) + shp[ax + 1:]), ax, 0)


def _join8(g, ax):
    shp = g.shape[1:]
    return jnp.moveaxis(g, 0, ax).reshape(shp[:ax] + (N_DEV * shp[ax],) + shp[ax + 1:])


def _group_lanes(v, hg):
    v = v.reshape(SSM_GROUPS, hg)
    return jnp.pad(v, ((0, 0), (0, LANE - hg))).reshape(1, SSM_GROUPS * LANE)


def _ungroup_lanes(v, hg):
    return v.reshape(SSM_GROUPS, LANE)[:, :hg].reshape(1, SSM_GROUPS * hg)


def kernel(x, meta_tokens, a_norm_pre, a_w_in, a_conv_w, a_conv_b, a_dt_bias, a_a_log, a_d_skip, a_gate_norm, a_w_out, a_norm_post, kv_norm, w_kv, b_norm_pre, b_w_q, b_sinks, b_w_o, b_norm_post, f_norm_pre, f_w_up, f_conv_w, f_conv_b, f_w_down, f_norm_post, loss_target, m_meta_tokens, m_a_norm_pre, m_a_w_in, m_a_conv_w, m_a_conv_b, m_a_dt_bias, m_a_a_log, m_a_d_skip, m_a_gate_norm, m_a_w_out, m_a_norm_post, m_kv_norm, m_w_kv, m_b_norm_pre, m_b_w_q, m_b_sinks, m_b_w_o, m_b_norm_post, m_f_norm_pre, m_f_w_up, m_f_conv_w, m_f_conv_b, m_f_w_down, m_f_norm_post, v_meta_tokens, v_a_norm_pre, v_a_w_in, v_a_conv_w, v_a_conv_b, v_a_dt_bias, v_a_a_log, v_a_d_skip, v_a_gate_norm, v_a_w_out, v_a_norm_post, v_kv_norm, v_w_kv, v_b_norm_pre, v_b_w_q, v_b_sinks, v_b_w_o, v_b_norm_post, v_f_norm_pre, v_f_w_up, v_f_conv_w, v_f_conv_b, v_f_w_down, v_f_norm_post):
    args = locals()
    wts = {n: args[n] for n in WEIGHTS}
    mom = {n: args["m_" + n] for n in WEIGHTS}
    var = {n: args["v_" + n] for n in WEIGHTS}
    mx, my, mc = lax.axis_index("x"), lax.axis_index("y"), lax.axis_index("c")
    me = 4 * mx + 2 * my + mc
    rows = _seq_rows()
    hg = SSM_HEADS // SSM_GROUPS
    d = D_MODEL

    n_main = D_INNER + D_XBC
    ws_in, ws_up = a_w_in.shape[2], f_w_up.shape[2]
    segs_in = _col_segments(ws_in, [(0, 0, n_main)] + [(n_main + hg * g, n_main + LANE * g, hg)
                                                      for g in range(SSM_GROUPS)])
    segs_up = _col_segments(ws_up, [(0, 0, 2 * D_FF)])
    def gather_of(*ws):
        return _gather_comm([w.astype(BF16) for w in ws])

    small_full, = _run_comm(_gather_comm([_pack([wts[n] for n in SMALL], 0, 8, F32)]), "gather_small")
    full = {}
    for n, g in zip(SMALL, _unpack(small_full, 1, [wts[n].shape for n in SMALL])):
        full[n] = _join8(g, SHARD_AXIS[n])
    (h0, hn0), (g_in,) = _embed_norm(full["meta_tokens"], x[0], full["a_norm_pre"], rows, "embed_norm",
                                     comm=gather_of(a_w_in[0]))
    w_in_all = _assemble_cols([g_in], n_main + SSM_GROUPS * LANE, segs_in, "asm_w_in")
    w_up, w_down = [None, None], [None, None]
    bias_g = _group_lanes(wts["a_dt_bias"], hg)
    alog_g = _group_lanes(wts["a_a_log"], hg)
    dsk_g = _group_lanes(wts["a_d_skip"], hg)
    a_conv_w, a_conv_b = full["a_conv_w"][0], full["a_conv_b"]
    f_cw, f_cb = full["f_conv_w"], wts["f_conv_b"]
    fpre, fpost = wts["f_norm_pre"], wts["f_norm_post"]


    zx, (g_out,) = _mm(hn0, w_in_all, "nn", F32, "mm_in", comm=gather_of(a_w_out[0]))
    w_out = g_out.reshape(D_INNER, d)
    xbc = _conv_silu_fwd(zx, a_conv_w, a_conv_b, "conv_a")
    (y_ssd, hst), (g_up0, g_kv, g_q) = _ssd_fwd(xbc, zx, bias_g, alog_g, dsk_g, "ssd_fwd",
                                                comm=gather_of(f_w_up[0], w_kv, b_w_q[0]))
    w_up[0] = _assemble_cols([g_up0], 2 * D_FF, segs_up, "asm_w_up0")
    yn = _gatenorm_fwd(y_ssd, zx, full["a_gate_norm"], "gatenorm")
    mix_a, (g_o,) = _mm(yn, w_out, "nn", F32, "mm_out", comm=gather_of(b_w_o[0]))
    h1, (fn0,) = _resid_norm(h0, mix_a, full["a_norm_post"], [fpre[0:1]], "resid_a")

    half = d // 2
    u0, (g_dn0,) = _mm(fn0, w_up[0], "nn", F32, "mm_up0", comm=gather_of(f_w_down[0]))
    act0, (g_up1a,) = _ffn_act_fwd(u0, f_cw[0], f_cb[0:1], "ffn_act0", comm=gather_of(f_w_up[1, :half]))
    ffn0 = _mm(act0, g_dn0.reshape(D_FF, d), "nn", F32, "mm_down0")
    w_kvf, w_q, w_o = g_kv.reshape(d, 2 * D_KV), g_q.reshape(d, d), g_o.reshape(d, d)
    h2, (kvn, bn) = _resid_norm(h1, ffn0, fpost[0:1], [wts["kv_norm"].reshape(1, d), wts["b_norm_pre"]], "resid_f0")
    kv = _mm(kvn, w_kvf, "nn", F32, "mm_kv")
    q = _mm(bn, w_q, "nn", F32, "mm_q")
    (o, lse), (g_up1b,) = _attn_fwd(q, kv, wts["b_sinks"], "attn_fwd", comm=gather_of(f_w_up[1, half:]))
    w_up[1] = _assemble_cols([g_up1a, g_up1b], 2 * D_FF, segs_up, "asm_w_up1")
    mix_b = _mm(o, w_o, "nn", F32, "mm_o")
    h3, (fn1,) = _resid_norm(h2, mix_b, wts["b_norm_post"], [fpre[1:2]], "resid_b")
    u1, (g_dn1,) = _mm(fn1, w_up[1], "nn", F32, "mm_up1", comm=gather_of(f_w_down[1]))
    w_down = [g_dn0.reshape(D_FF, d), g_dn1.reshape(D_FF, d)]
    act1 = _ffn_act_fwd(u1, f_cw[1], f_cb[1:2], "ffn_act1")
    ffn1 = _mm(act1, w_down[1], "nn", F32, "mm_down1")
    dh4, loss_row, dffn1, dw_post1 = _final_loss(h3, ffn1, fpost[1:2], loss_target[0], "loss")
    loss = lax.psum(loss_row[0, 0], ("x", "y", "c"))

    grads = {}

    core = mc.astype(jnp.int32).reshape(1)

    def carried(res, comm):
        return res if comm is not None else (res, None)

    def ffn_bwd(dh_out, dffn, h_in, fn, u, act, i, then, c_dact=None, c_dwdown=None, c_dwup=None, c_dfn=None):
        dact, got_a = carried(_mm(dffn, w_down[i], "nt", F32, f"mm_dact{i}", comm=c_dact), c_dact)
        dw_down, got_b = carried(_mm(act, dffn, "tn", BF16, f"mm_dwdown{i}", comm=c_dwdown), c_dwdown)
        dw_down = dw_down.reshape(N_DEV, -1, d)
        du, dwc, dbc = _ffn_act_bwd(u, dact, f_cw[i], f_cb[i:i + 1], f"ffn_act_bwd{i}")
        dfn, (s_dn, *got_d) = _mm(du, w_up[i], "nt", F32, f"mm_dfn{i}", comm=_join_comms(
            [_swap_comm([dw_down])] + ([c_dfn] if c_dfn is not None else [])))
        sum_dn = _add_pairs(dw_down, s_dn, core, f"rs_add_dn{i}")
        dw_up, got_c = carried(_mm(fn, du, "tn", BF16, f"mm_dwup{i}", comm=c_dwup, shard_cols=ws_up), c_dwup)
        (dh_in, dw_pre, dbranch, dw_branch), (s_up,) = _norm_bwd(
            h_in, fpre[i:i + 1], dfn, dh_out, F32, f"nb_fpre{i}", comm=_swap_comm([dw_up]), then=then)
        sum_up = _add_pairs(dw_up, s_up, core, f"rs_add_up{i}")
        return dh_in, dbranch, dw_branch, dict(sum_down=sum_dn, cw=jnp.concatenate([dwc[0], dwc[1]], axis=1),
                                               cb=jnp.concatenate([dbc[0], dbc[1]], axis=1), sum_up=sum_up,
                                               pre=dw_pre), got_a, got_b, got_c, got_d

    dh3, dmix_b, grads["b_norm_post"], gf1, _, _, _, _ = ffn_bwd(dh4, dffn1, h3, fn1, u1, act1, 1,
                                                                 (mix_b, wts["b_norm_post"]))
    do = _mm(dmix_b, w_o, "nt", F32, "mm_do")
    dw_o = _mm(o, dmix_b, "tn", BF16, "mm_dwo").reshape(N_DEV, -1, d)
    half_up = gf1["sum_up"].shape[1] // 2
    (dq, dkv, dsinks), (p_up1a, s_o) = _attn_bwd(
        q, kv, wts["b_sinks"], do, lse, "attn_bwd",
        comm=_join_comms([_chips_comm([gf1["sum_up"][:, :half_up]]), _swap_comm([dw_o])]))
    sum_o = _add_pairs(dw_o, s_o, core, "rs_add_o")
    grads["b_sinks"] = dsinks[:, :N_Q_HEADS]
    dbn = _mm(dq, w_q, "nt", F32, "mm_dbn")
    dw_q = _mm(bn, dq, "tn", BF16, "mm_dwq").reshape(N_DEV, -1, d)
    dkv16 = dkv.astype(BF16)
    dkvn = _mm(dkv16, w_kvf, "nt", F32, "mm_dkvn")
    dw_kv = _mm(kvn, dkv16, "tn", BF16, "mm_dwkv").reshape(N_DEV, -1, 2 * D_KV)
    (dh2, grads["b_norm_pre"]), (s_q, s_kv) = _norm_bwd(h2, wts["b_norm_pre"], dbn, dh3, F32, "nb_bpre",
                                                        comm=_swap_comm([dw_q, dw_kv]))
    sum_q, sum_kv = _add_pairs(dw_q, s_q, core, "rs_add_q"), _add_pairs(dw_kv, s_kv, core, "rs_add_kv")
    dh2, dw_kvn, dffn0, dw_post0 = _norm_bwd(h2, wts["kv_norm"].reshape(1, d), dkvn, dh2, F32, "nb_kv",
                                             then=(ffn0, fpost[0:1]))
    grads["kv_norm"] = dw_kvn.reshape(d)
    dh1, dmix_a, grads["a_norm_post"], gf0, (p_o,), (p_q, p_kv), (p_dn1,), (p_up1b,) = ffn_bwd(
        dh2, dffn0, h1, fn0, u0, act0, 0, (mix_a, full["a_norm_post"]), c_dact=_chips_comm([sum_o]),
        c_dwdown=_chips_comm([sum_q, sum_kv]), c_dwup=_chips_comm([gf1["sum_down"]]),
        c_dfn=_chips_comm([gf1["sum_up"][:, half_up:]]))
    p_up1 = jnp.concatenate([p_up1a, p_up1b], axis=1)
    grads["f_norm_post"] = jnp.concatenate([dw_post0, dw_post1], axis=0)
    grads["f_norm_pre"] = jnp.concatenate([gf0["pre"], gf1["pre"]], axis=0)
    grads["f_conv_w"] = jnp.stack([gf0["cw"], gf1["cw"]])
    grads["f_conv_b"] = jnp.concatenate([gf0["cb"], gf1["cb"]], axis=0)

    dyn = _mm(dmix_a, w_out, "nt", F32, "mm_dyn")
    dw_out = _mm(yn, dmix_a, "tn", BF16, "mm_dwout").reshape(N_DEV, -1, d)
    (dy_ssd, dzx, grads["a_gate_norm"]), (s_out,) = _gatenorm_bwd(y_ssd, zx, full["a_gate_norm"], dyn, "gatenorm_bwd",
                                                                  comm=_swap_comm([dw_out]))
    sum_out = _add_pairs(dw_out, s_out, core, "rs_add_out")
    (dxs, dbm, dcm, dzx, dalog, ddsk, dbias), (p_up0,) = _ssd_bwd(
        xbc, zx, bias_g, alog_g, dsk_g, dy_ssd, hst, dzx, "ssd_bwd", comm=_chips_comm([gf0["sum_up"]]))
    grads["a_a_log"] = _ungroup_lanes(dalog, hg)
    grads["a_d_skip"] = _ungroup_lanes(ddsk, hg)
    grads["a_dt_bias"] = _ungroup_lanes(dbias, hg)
    dzx, dcw, dcb = _conv_silu_bwd(zx, dxs, dbm, dcm, a_conv_w, a_conv_b, dzx, "conv_a_bwd")
    grads["a_conv_w"], grads["a_conv_b"] = dcw[None], dcb
    dw_in_all, (p_dn0,) = _mm(hn0, dzx, "tn", BF16, "mm_dwin", comm=_chips_comm([gf0["sum_down"]]))
    dw_in8 = _scatter_cols(dw_in_all, ws_in, segs_in, "scat_w_in")
    dhn0, (s_in, p_out) = _mm(dzx, w_in_all, "nt", F32, "mm_dhn0",
                              comm=_join_comms([_swap_comm([dw_in8]), _chips_comm([sum_out])]))
    sum_in = _add_pairs(dw_in8, s_in, core, "rs_add_in")
    half_in = sum_in.shape[1] // 2
    (grads["meta_tokens"], g_x, grads["a_norm_pre"]), (p_in_a,) = _norm_bwd(
        h0, full["a_norm_pre"], dhn0, dh1, F32, "nb_apre", comm=_chips_comm([sum_in[:, :half_in]]), split_rows=SEQ)
    grad_x = g_x[None]

    small_local = _pack([_split8(grads[n], SHARD_AXIS[n], wts[n].shape[SHARD_AXIS[n]]) for n in SMALL], 1, 8, F32)
    repl_local = _pack([grads[n] for n in REPL], 0, 8, F32)
    n_sr = small_local.shape[1]
    small_vec = jnp.concatenate([small_local.reshape(N_DEV * n_sr, LANE), repl_local], axis=0)
    tail = _join_comms([_chips_comm([sum_in[:, half_in:]]), _gather_comm([small_vec])])
    parts_big = dict(a_w_out=[p_out], w_kv=[p_kv], b_w_q=[p_q], b_w_o=[p_o], f_w_down=[p_dn0, p_dn1])

    def flat_f32(dct, names, mult):
        return _pack([dct[n] for n in names], 0, mult, F32)

    def adamw_big(n, comm=None):
        shp3 = (len(parts_big[n]),) + parts_big[n][0].shape[1:]
        res = _adamw(parts_big[n], *[dct[n].reshape(shp3) for dct in (wts, mom, var)], f"adamw_{n}", comm=comm)
        res, got = res if comm is not None else (res, None)
        big_out[n] = [r.reshape(wts[n].shape) for r in res]
        return got

    big_out = {}
    def swap_last(a):
        return jnp.swapaxes(a, -1, -2)

    g_up_t = swap_last(_sum_parts([p_up0, p_up1], "sum_w_up").reshape(f_w_up.shape))
    res, (p_in_b, small_all) = _adamw([g_up_t[0:1], g_up_t[1:2]], *[swap_last(dct["f_w_up"]) for dct in (wts, mom, var)],
                                      "adamw_f_w_up", comm=tail)
    big_out["f_w_up"] = [swap_last(r) for r in res]
    for n in BIG:
        if n not in ("f_w_up", "a_w_in"):
            adamw_big(n)
    g_in_t = swap_last(_sum_parts([p_in_a, p_in_b], "sum_w_in"))[None]
    res = _adamw([g_in_t], *[swap_last(dct["a_w_in"]) for dct in (wts, mom, var)], "adamw_a_w_in")
    big_out["a_w_in"] = [swap_last(r) for r in res]
    mine_small = lax.dynamic_slice_in_dim(small_all, me * n_sr, n_sr, axis=1)
    parts_small = jnp.concatenate([mine_small, small_all[:, N_DEV * n_sr:]], axis=1)
    sm_in = [jnp.concatenate([flat_f32(dct, SMALL, 8), flat_f32(dct, REPL, 8)], axis=0)[None] for dct in (wts, mom, var)]
    small_out = [r[0] for r in _adamw([parts_small], *sm_in, "adamw_small")]

    outs = []
    for kind in range(4):
        res = {n: big_out[n][kind] for n in BIG}
        for n, a in zip(SMALL, _unpack(small_out[kind][:n_sr], 0, [wts[n].shape for n in SMALL])):
            res[n] = a
        for n, a in zip(REPL, _unpack(small_out[kind][n_sr:], 0, [wts[n].shape for n in REPL])):
            res[n] = a
        outs.append(res)
    return (loss, grad_x, *[outs[0][n] for n in WEIGHTS], *[outs[1][n] for n in WEIGHTS],
            *[outs[2][n] for n in WEIGHTS], *[outs[3][n] for n in WEIGHTS])
```

```python
import functools
import math

import jax
import jax.numpy as jnp
from jax import lax
from jax.experimental import pallas as pl
from jax.experimental.pallas import tpu as pltpu

F32, BF16 = jnp.float32, jnp.bfloat16
S = jax.ShapeDtypeStruct

D_MODEL = 1024
SEQ = 2048
N_META = 16
D_INNER = 2048
HEAD_P = 64
SSM_HEADS = D_INNER // HEAD_P
SSM_GROUPS = 4
D_STATE = 128
SSM_CONV = 4
D_BC = SSM_GROUPS * D_STATE
D_XBC = D_INNER + 2 * D_BC
ATTN_DH = 64
N_Q_HEADS = D_MODEL // ATTN_DH
N_KV_HEADS = 4
D_KV = N_KV_HEADS * ATTN_DH
WINDOW = 128
D_FF = 2816
FFN_CONV = 3
RMS_EPS = 1e-6
NEG = -1e30
LR, B1, B2, EPS, WD, STEP = 0.001, 0.9, 0.999, 1e-08, 0.01, 10

N_DEV = 8
T = 128
LANE = 128
VMEM_LIMIT = 48 * 1024 * 1024

BIG = ("a_w_in", "a_w_out", "w_kv", "b_w_q", "b_w_o", "f_w_up", "f_w_down")
SMALL = ("meta_tokens", "a_norm_pre", "a_conv_w", "a_conv_b", "a_gate_norm", "a_norm_post", "f_conv_w")
REPL = ("a_dt_bias", "a_a_log", "a_d_skip", "kv_norm", "b_norm_pre", "b_sinks", "b_norm_post",
        "f_norm_pre", "f_conv_b", "f_norm_post")
SHARD_AXIS = dict(a_w_in=2, a_w_out=1, w_kv=0, b_w_q=1, b_w_o=1, f_w_up=2, f_w_down=1, meta_tokens=1,
                  a_norm_pre=1, a_conv_w=2, a_conv_b=1, a_gate_norm=1, a_norm_post=1, f_conv_w=2)
WEIGHTS = ("meta_tokens", "a_norm_pre", "a_w_in", "a_conv_w", "a_conv_b", "a_dt_bias", "a_a_log", "a_d_skip",
           "a_gate_norm", "a_w_out", "a_norm_post", "kv_norm", "w_kv", "b_norm_pre", "b_w_q", "b_sinks", "b_w_o",
           "b_norm_post", "f_norm_pre", "f_w_up", "f_conv_w", "f_conv_b", "f_w_down", "f_norm_post")


def _seq_rows():
    return -(-(N_META + SEQ) // T) * T


def _cp(sem=None):
    return pltpu.CompilerParams(dimension_semantics=sem, vmem_limit_bytes=VMEM_LIMIT)


def _pick(n, target):
    t = min(n, target)
    t -= t % LANE
    while n % t:
        t -= LANE
    return t


def _sigmoid(x):
    return 0.5 * jnp.tanh(0.5 * x) + 0.5


def _softplus(x):
    return jnp.maximum(x, 0.0) + jnp.log(1.0 + jnp.exp(-jnp.abs(x)))


_NN = (((1,), (0,)), ((), ()))
_NT = (((1,), (1,)), ((), ()))
_TN = (((0,), (0,)), ((), ()))


def _dot(a, b, dims=_NN):
    return lax.dot_general(a, b, dims, preferred_element_type=F32)


def _dot_hi(a, b):
    return lax.dot_general(a, b, _NN, precision=lax.Precision.HIGHEST, preferred_element_type=F32)


_HBM = pl.BlockSpec(memory_space=pltpu.HBM)
_MESH = pl.DeviceIdType.MESH


class _Comm:
    def __init__(self, ins, out_shapes, scratch, first, last):
        self.ins, self.out_shapes, self.scratch, self.first, self.last = ins, out_shapes, scratch, first, last


_ANY = pl.BlockSpec(memory_space=pl.ANY)


def _call(body, name, out_shape, grid, in_specs, out_specs, sem, args, scratch=(), comm=None, aliases=None):
    aliases = aliases or {}
    if comm is None:
        return pl.pallas_call(body, name=name, out_shape=out_shape, grid=grid, in_specs=in_specs, out_specs=out_specs,
                              scratch_shapes=list(scratch), input_output_aliases=aliases,
                              compiler_params=_cp(sem))(*args)
    single = not isinstance(out_shape, (list, tuple))
    outs = [out_shape] if single else list(out_shape)
    ospecs = [out_specs] if single else list(out_specs)
    n_in, n_out, n_scr, ci, co = len(in_specs), len(outs), len(scratch), len(comm.ins), len(comm.out_shapes)

    def carrier(*refs):
        p = 0
        parts = []
        for cnt in (n_in, ci, n_out, co, n_scr, len(comm.scratch)):
            parts.append(refs[p:p + cnt])
            p += cnt
        ins, cins, outs_r, couts, scr, cscr = parts
        ids = [pl.program_id(i) for i in range(len(grid))]
        first, last = ids[0] == 0, ids[0] == grid[0] - 1
        for i in range(1, len(grid)):
            first, last = first & (ids[i] == 0), last & (ids[i] == grid[i] - 1)

        @pl.when(first)
        def _():
            comm.first(cins, couts, cscr)

        body(*ins, *outs_r, *scr)

        @pl.when(last)
        def _():
            comm.last(cins, couts, cscr)

    res = pl.pallas_call(
        carrier, name=name, out_shape=outs + list(comm.out_shapes), grid=grid,
        in_specs=list(in_specs) + [_HBM] * ci, out_specs=ospecs + [_HBM] * co,
        scratch_shapes=list(scratch) + list(comm.scratch), input_output_aliases=aliases,
        compiler_params=_cp(("arbitrary",) * len(grid)))(*args, *comm.ins)
    mine = res[0] if single else list(res[:n_out])
    return mine, list(res[n_out:])


def _mm(a, b, mode, out_dtype, name, comm=None, shard_cols=None):
    if mode == "tn":
        m, kk = a.shape
        planes, width = (b.shape[0], b.shape[2]) if b.ndim == 3 else (1, b.shape[1])
        n = planes * width
        tko, tn = _pick(kk, 512), _pick(width, 1536)
        per = width // tn
        b_spec = (pl.BlockSpec((None, m, tn), lambda i, j: (j // per, 0, j % per)) if b.ndim == 3
                  else pl.BlockSpec((m, tn), lambda i, j: (0, j)))
        if shard_cols is None:
            def body(a_ref, b_ref, o_ref):
                o_ref[...] = _dot(a_ref[...], b_ref[...], _TN).astype(o_ref.dtype)

            out_shape, out_spec = S((kk, n), out_dtype), pl.BlockSpec((tko, tn), lambda i, j: (i, j))
        else:
            shards = tn // shard_cols
            assert tn % shard_cols == 0

            def body(a_ref, b_ref, o_ref):
                res = _dot(a_ref[...], b_ref[...], _TN).astype(o_ref.dtype)
                for p in range(shards):
                    o_ref[p] = res[:, p * shard_cols:(p + 1) * shard_cols]

            out_shape = S((n // shard_cols, kk, shard_cols), out_dtype)
            out_spec = pl.BlockSpec((shards, tko, shard_cols), lambda i, j: (j, i, 0))
        return _call(
            body, name, out_shape, (kk // tko, n // tn), [pl.BlockSpec((m, tko), lambda i, j: (0, i)), b_spec],
            out_spec, ("parallel", "parallel"), (a, b), comm=comm)

    planes, width = (a.shape[0], a.shape[2]) if a.ndim == 3 else (1, a.shape[1])
    m, kk = a.shape[-2], planes * width
    n = b.shape[1] if mode == "nn" else b.shape[0]
    dims = _NN if mode == "nn" else _NT

    if kk > 2048:
        tm = m // 4
        assert m % 4 == 0 and tm % 16 == 0

        def body(a_ref, b_ref, o_ref):
            if a.ndim == 2:
                res = _dot(a_ref[...], b_ref[...], dims)
            else:
                res = None
                for p in range(planes):
                    bp = b_ref[p * width:(p + 1) * width, :] if mode == "nn" else b_ref[:, p * width:(p + 1) * width]
                    part = _dot(a_ref[p], bp, dims)
                    res = part if res is None else res + part
            o_ref[...] = res.astype(o_ref.dtype)

        a_spec = (pl.BlockSpec((planes, tm, width), lambda i: (0, i, 0)) if a.ndim == 3
                  else pl.BlockSpec((tm, kk), lambda i: (i, 0)))
        return _call(
            body, name, S((m, n), out_dtype), (m // tm,),
            [a_spec, pl.BlockSpec(b.shape, lambda i: (0, 0), pipeline_mode=pl.Buffered(1))],
            pl.BlockSpec((tm, n), lambda i: (i, 0)), ("parallel",), (a, b), comm=comm)

    tn = _pick(n, 512)

    def body(a_ref, b_ref, o_ref):
        o_ref[...] = _dot(a_ref[...], b_ref[...], dims).astype(o_ref.dtype)

    b_spec = (pl.BlockSpec((kk, tn), lambda j: (0, j)) if mode == "nn" else pl.BlockSpec((tn, kk), lambda j: (j, 0)))
    return _call(
        body, name, S((m, n), out_dtype), (n // tn,), [pl.BlockSpec((m, kk), lambda j: (0, 0)), b_spec],
        pl.BlockSpec((m, tn), lambda j: (0, j)), ("parallel",), (a, b), comm=comm)


def _rms(x, w):
    return x * lax.rsqrt(jnp.mean(x * x, axis=-1, keepdims=True) + RMS_EPS) * w


def _row_tile(rows, d):
    return rows // 4 if d <= 1024 and (rows // 4) % 16 == 0 else rows // 8


def _embed_norm(meta, x, w, rows, name, comm=None):
    n_meta, d = meta.shape
    n_x = x.shape[0]
    last = rows // T - 1
    assert n_meta % 8 == 0 and n_meta < T and n_meta + n_x == last * T + n_meta and last * T >= n_x

    def body(m_ref, x_ref, w_ref, h_ref, hn_ref):
        i = pl.program_id(0)

        @pl.when(i == 0)
        def _():
            h_ref[0:n_meta, :] = m_ref[...]
            h_ref[n_meta:T, :] = x_ref[0:T - n_meta, :]

        @pl.when((i > 0) & (i < last))
        def _():
            h_ref[...] = x_ref[pl.ds(pl.multiple_of(i * T - n_meta, 8), T), :]

        @pl.when(i == last)
        def _():
            h_ref[0:n_meta, :] = x_ref[n_x - n_meta:n_x, :]
            h_ref[n_meta:T, :] = jnp.zeros((T - n_meta, d), F32)

        hn_ref[...] = _rms(h_ref[...], w_ref[...]).astype(hn_ref.dtype)

    row = pl.BlockSpec((T, d), lambda i: (i, 0))
    return _call(body, name, [S((rows, d), F32), S((rows, d), BF16)], (rows // T,),
                 [pl.BlockSpec((n_meta, d), lambda i: (0, 0)), pl.BlockSpec((n_x, d), lambda i: (0, 0)),
                  pl.BlockSpec((1, d), lambda i: (0, 0))], [row, row], ("parallel",), (meta, x, w), comm=comm)


def _resid_norm(h, br, w_post, next_ws, name):
    rows, d = h.shape
    tr = _row_tile(rows, d)
    has_br = br is not None
    nw = len(next_ws)

    def body(*refs):
        h_ref = refs[0]
        pos = 1
        x = h_ref[...]
        if has_br:
            x = x + _rms(refs[1][...], refs[2][...])
            pos = 3
        w_refs = refs[pos:pos + nw]
        outs = refs[pos + nw:]
        if has_br:
            outs[0][...] = x
            outs = outs[1:]
        for w_ref, o_ref in zip(w_refs, outs):
            o_ref[...] = _rms(x, w_ref[...]).astype(o_ref.dtype)

    row = pl.BlockSpec((tr, d), lambda i: (i, 0))
    vec = pl.BlockSpec((1, d), lambda i: (0, 0))
    ins = [h] + ([br, w_post] if has_br else []) + list(next_ws)
    in_specs = [row] + ([row, vec] if has_br else []) + [vec] * nw
    out_shape = ([S((rows, d), F32)] if has_br else []) + [S((rows, d), BF16)] * nw
    res = pl.pallas_call(body, name=name, out_shape=out_shape, grid=(rows // tr,), in_specs=in_specs,
                         out_specs=[row] * len(out_shape), compiler_params=_cp(("parallel",)))(*ins)
    if has_br:
        return res[0], list(res[1:])
    return h, list(res)


def _rms_bwd(xv, w, dyv):
    r = lax.rsqrt(jnp.mean(xv * xv, axis=-1, keepdims=True) + RMS_EPS)
    wdy = dyv * w
    dx = r * wdy - xv * (r * r * r) * jnp.mean(xv * wdy, axis=-1, keepdims=True)
    return dx, jnp.sum(dyv * xv * r, axis=0, keepdims=True)


def _norm_bwd(x, w, dy, add, out_dtype, name, comm=None, then=None, split_rows=None):
    rows, d = x.shape
    tr = _row_tile(rows, d)
    has_add = add is not None
    n_in = 3 + has_add + (2 if then is not None else 0)
    if split_rows is not None:
        last, tail = _real_rows(rows, tr, split_rows)

    def body(*refs):
        x_ref, w_ref, dy_ref = refs[:3]
        outs = refs[n_in:]
        dx, dw = _rms_bwd(x_ref[...], w_ref[...], dy_ref[...].astype(F32))
        if has_add:
            dx = dx + refs[3][...]
        if split_rows is None:
            outs[0][...] = dx.astype(outs[0].dtype)
        else:
            i = pl.program_id(0)
            gm_ref, gx_ref = outs[0], outs[1]
            outs = outs[1:]

            @pl.when(i == 0)
            def _():
                gm_ref[...] = dx[0:N_META, :]
                gx_ref[0:tr - N_META, :] = dx[N_META:tr, :]

            @pl.when((i > 0) & (i < last))
            def _():
                gx_ref[pl.ds(pl.multiple_of(i * tr - N_META, 8), tr), :] = dx

            @pl.when(i == last)
            def _():
                gx_ref[split_rows - tail:split_rows, :] = dx[0:tail, :]
        first = pl.program_id(0) == 0

        @pl.when(first)
        def _():
            outs[1][...] = jnp.zeros_like(outs[1])

        outs[1][...] += dw
        if then is not None:
            dx2, dw2 = _rms_bwd(refs[n_in - 2][...], refs[n_in - 1][...], dx)
            outs[2][...] = dx2.astype(outs[2].dtype)

            @pl.when(first)
            def _():
                outs[3][...] = jnp.zeros_like(outs[3])

            outs[3][...] += dw2

    row = pl.BlockSpec((tr, d), lambda i: (i, 0))
    vec = pl.BlockSpec((1, d), lambda i: (0, 0))
    ins = [x, w, dy] + ([add] if has_add else []) + (list(then) if then is not None else [])
    in_specs = [row, vec, row] + ([row] if has_add else []) + ([row, vec] if then is not None else [])
    out_shape = [S((rows, d), out_dtype), S((1, d), F32)] + ([S((rows, d), BF16), S((1, d), F32)] if then is not None else [])
    out_specs = [row, vec] * (len(out_shape) // 2)
    if split_rows is not None:
        out_shape = [S((N_META, d), F32), S((split_rows, d), F32)] + out_shape[1:]
        out_specs = [pl.BlockSpec((N_META, d), lambda i: (0, 0)), pl.BlockSpec((split_rows, d), lambda i: (0, 0))] + out_specs[1:]
    return _call(body, name, out_shape, (rows // tr,), in_specs, out_specs, ("arbitrary",), ins, comm=comm)


def _real_rows(rows, tr, n_x):
    last = (N_META + n_x - 1) // tr
    tail = N_META + n_x - last * tr
    assert last == rows // tr - 1 and N_META % 8 == 0 and tail % 8 == 0 and N_META < tr
    return last, tail


def _final_loss(h, br, w_post, target, name):
    rows, d = h.shape
    tr = _row_tile(rows, d)
    n_x = target.shape[0]
    last, tail = _real_rows(rows, tr, n_x)

    def body(h_ref, br_ref, w_ref, t_ref, dh_ref, loss_ref, dbr_ref, dw_ref, tbuf):
        i = pl.program_id(0)

        @pl.when(i == 0)
        def _():
            tbuf[0:N_META, :] = jnp.zeros((N_META, d), F32)
            tbuf[N_META:tr, :] = t_ref[0:tr - N_META, :]

        @pl.when((i > 0) & (i < last))
        def _():
            tbuf[...] = t_ref[pl.ds(pl.multiple_of(i * tr - N_META, 8), tr), :]

        @pl.when(i == last)
        def _():
            tbuf[0:tail, :] = t_ref[n_x - tail:n_x, :]
            if tail < tr:
                tbuf[tail:tr, :] = jnp.zeros((tr - tail, d), F32)

        brv, wv = br_ref[...], w_ref[...]
        y = h_ref[...] + _rms(brv, wv)
        r = i * tr + lax.broadcasted_iota(jnp.int32, (tr, 1), 0)
        real = (r >= N_META) & (r < N_META + SEQ)
        diff = jnp.where(real, y - tbuf[...], 0.0)
        dh = diff * (1.0 / d)
        dh_ref[...] = dh
        dbr, dw = _rms_bwd(brv, wv, dh)
        dbr_ref[...] = dbr.astype(dbr_ref.dtype)

        @pl.when(i == 0)
        def _():
            loss_ref[...] = jnp.zeros_like(loss_ref)
            dw_ref[...] = jnp.zeros_like(dw_ref)

        loss_ref[...] += jnp.sum(diff * diff) * (0.5 / d)
        dw_ref[...] += dw

    row = pl.BlockSpec((tr, d), lambda i: (i, 0))
    vec = pl.BlockSpec((1, d), lambda i: (0, 0))
    return pl.pallas_call(body, name=name,
                          out_shape=[S((rows, d), F32), S((1, LANE), F32), S((rows, d), BF16), S((1, d), F32)],
                          grid=(rows // tr,), in_specs=[row, row, vec, pl.BlockSpec((n_x, d), lambda i: (0, 0))],
                          out_specs=[row, pl.BlockSpec((1, LANE), lambda i: (0, 0)), row, vec],
                          scratch_shapes=[pltpu.VMEM((tr, d), F32)],
                          compiler_params=_cp(("arbitrary",)))(h, br, w_post, target)


def _gatenorm_fwd(y, zx, w, name, comm=None):
    rows, d = y.shape
    tr = _row_tile(rows, d)

    def body(y_ref, z_ref, w_ref, o_ref):
        z = z_ref[...]
        o_ref[...] = _rms(y_ref[...] * z * _sigmoid(z), w_ref[...]).astype(o_ref.dtype)

    row = pl.BlockSpec((tr, d), lambda i: (i, 0))
    return _call(body, name, S((rows, d), BF16), (rows // tr,), [row, row, pl.BlockSpec((1, d), lambda i: (0, 0))],
                 row, ("parallel",), (y, zx, w), comm=comm)


def _gatenorm_bwd(y, zx, w, dyn, name, comm=None):
    rows, d = y.shape
    tr = _row_tile(rows, d)

    def body(y_ref, z_ref, w_ref, dyn_ref, dy_ref, dz_ref, dw_ref):
        yv, z = y_ref[...], z_ref[...]
        sg = _sigmoid(z)
        sz = z * sg
        g = yv * sz
        r = lax.rsqrt(jnp.mean(g * g, axis=-1, keepdims=True) + RMS_EPS)
        dyn_v = dyn_ref[...]
        wdy = dyn_v * w_ref[...]
        dg = r * wdy - g * (r * r * r) * jnp.mean(g * wdy, axis=-1, keepdims=True)
        dy_ref[...] = dg * sz
        dz_ref[...] = (dg * yv * sg * (1.0 + z * (1.0 - sg))).astype(dz_ref.dtype)

        @pl.when(pl.program_id(0) == 0)
        def _():
            dw_ref[...] = jnp.zeros_like(dw_ref)

        dw_ref[...] += jnp.sum(dyn_v * g * r, axis=0, keepdims=True)

    row = pl.BlockSpec((tr, d), lambda i: (i, 0))
    vec = pl.BlockSpec((1, d), lambda i: (0, 0))
    return _call(body, name, [S((rows, d), F32), S((rows, zx.shape[1]), BF16), S((1, d), F32)], (rows // tr,),
                 [row, row, vec, row], [row, row, vec], ("arbitrary",), (y, zx, w, dyn), comm=comm)


def _shift_down(x, s, rows_iota):
    if s == 0:
        return x
    return jnp.where(rows_iota >= s, pltpu.roll(x, s, 0), 0.0)


def _shift_up(x, s, rows_iota):
    if s == 0:
        return x
    rows = x.shape[0]
    return jnp.where(rows_iota < rows - s, pltpu.roll(x, rows - s, 0), 0.0)


def _r16(v):
    return v.astype(BF16).astype(F32)


def _conv_taps(x, taps, rows_iota):
    x = _r16(x)
    return [_shift_down(x, taps - 1 - k, rows_iota) for k in range(taps)]


def _conv(x, w_ref, b_ref, taps, rows_iota, shifted=None):
    shifted = _conv_taps(x, taps, rows_iota) if shifted is None else shifted
    acc = jnp.zeros_like(shifted[0])
    for k in range(taps):
        acc = acc + _r16(w_ref[k:k + 1, :]) * shifted[k]
    return acc + b_ref[...]


def _conv_bwd(shifted, du, w_ref, dw_ref, db_ref, taps, rows_iota):
    db_ref[...] = jnp.sum(du, axis=0, keepdims=True)
    du = _r16(du)
    dx = jnp.zeros_like(du)
    for k in range(taps):
        dx = dx + _r16(w_ref[k:k + 1, :]) * _shift_up(du, taps - 1 - k, rows_iota)
        dw_ref[k:k + 1, :] = jnp.sum(du * shifted[k], axis=0, keepdims=True)
    return dx


def _conv_silu_fwd(zx, w, b, name, comm=None):
    rows = zx.shape[0]
    cb = 512
    off = D_INNER // cb

    def body(x_ref, w_ref, b_ref, o_ref):
        it = lax.broadcasted_iota(jnp.int32, (rows, 1), 0)
        u = _conv(x_ref[...], w_ref, b_ref, SSM_CONV, it)
        o_ref[...] = u * _sigmoid(u)

    return _call(
        body, name, S((rows, D_XBC), F32), (D_XBC // cb,),
        [pl.BlockSpec((rows, cb), lambda j: (0, off + j)), pl.BlockSpec((SSM_CONV, cb), lambda j: (0, j)),
         pl.BlockSpec((1, cb), lambda j: (0, j))],
        pl.BlockSpec((rows, cb), lambda j: (0, j)), ("parallel",), (zx, w, b), comm=comm)


def _conv_silu_bwd(zx, dxs, dbm, dcm, w, b, dzx, name, comm=None):
    rows = zx.shape[0]
    cb = 256
    off = D_INNER // cb
    nx, nbc = D_INNER // cb, D_BC // cb

    def body(x_ref, dx_in, db_in, dc_in, w_ref, b_ref, dzx_in, dx_ref, dw_ref, db_ref, dbuf):
        del dzx_in
        j = pl.program_id(0)
        for cond, src in ((j < nx, dx_in), ((j >= nx) & (j < nx + nbc), db_in), (j >= nx + nbc, dc_in)):
            @pl.when(cond)
            def _(src=src):
                dbuf[...] = src[...]
        it = lax.broadcasted_iota(jnp.int32, (rows, 1), 0)
        xs = _conv_taps(x_ref[...], SSM_CONV, it)
        u = _conv(None, w_ref, b_ref, SSM_CONV, it, xs)
        sg = _sigmoid(u)
        du = dbuf[...] * sg * (1.0 + u * (1.0 - sg))
        dx_ref[...] = _conv_bwd(xs, du, w_ref, dw_ref, db_ref, SSM_CONV, it).astype(dx_ref.dtype)

    def part(first, count):
        return pl.BlockSpec((rows, cb), lambda j: (0, jnp.clip(j - first, 0, count - 1)))

    col = pl.BlockSpec((rows, cb), lambda j: (0, j))
    wsp = pl.BlockSpec((SSM_CONV, cb), lambda j: (0, j))
    bsp = pl.BlockSpec((1, cb), lambda j: (0, j))
    xbc_cols = pl.BlockSpec((rows, cb), lambda j: (0, off + j))
    return _call(
        body, name, [S(dzx.shape, dzx.dtype), S((SSM_CONV, D_XBC), F32), S((1, D_XBC), F32)], (D_XBC // cb,),
        [xbc_cols, part(0, nx), part(nx, nbc), part(nx + nbc, nbc), wsp, bsp, _ANY],
        [xbc_cols, wsp, bsp], ("arbitrary",), (zx, dxs, dbm, dcm, w, b, dzx), scratch=[pltpu.VMEM((rows, cb), F32)],
        comm=comm, aliases={6: 0})


def _ffn_act_fwd(u, w, b, name, comm=None):
    rows = u.shape[0]
    cb = 256
    nb = D_FF // cb

    def body(g_ref, v_ref, wg_ref, wv_ref, bg_ref, bv_ref, o_ref):
        it = lax.broadcasted_iota(jnp.int32, (rows, 1), 0)
        g = _conv(g_ref[...], wg_ref, bg_ref, FFN_CONV, it)
        v = _conv(v_ref[...], wv_ref, bv_ref, FFN_CONV, it)
        o_ref[...] = (g * _sigmoid(g) * v).astype(o_ref.dtype)

    def sp(r, shift):
        return pl.BlockSpec((r, cb), lambda j: (0, shift + j))

    return _call(
        body, name, S((rows, D_FF), BF16), (nb,),
        [sp(rows, 0), sp(rows, nb), sp(FFN_CONV, 0), sp(FFN_CONV, nb), sp(1, 0), sp(1, nb)],
        sp(rows, 0), ("parallel",), (u, u, w, w, b, b), comm=comm)


def _ffn_act_bwd(u, dact, w, b, name, comm=None):
    rows = u.shape[0]
    cb = 256
    nb = D_FF // cb

    def body(g_ref, v_ref, d_ref, wg_ref, wv_ref, bg_ref, bv_ref, du_ref, dw_ref, db_ref):
        it = lax.broadcasted_iota(jnp.int32, (rows, 1), 0)
        xg, xv = _conv_taps(g_ref[...], FFN_CONV, it), _conv_taps(v_ref[...], FFN_CONV, it)
        g = _conv(None, wg_ref, bg_ref, FFN_CONV, it, xg)
        v = _conv(None, wv_ref, bv_ref, FFN_CONV, it, xv)
        sg = _sigmoid(g)
        d = d_ref[...]
        dgate = d * v * sg * (1.0 + g * (1.0 - sg))
        dval = d * g * sg
        du_ref[0] = _conv_bwd(xg, dgate, wg_ref, dw_ref.at[0], db_ref.at[0], FFN_CONV, it).astype(du_ref.dtype)
        du_ref[1] = _conv_bwd(xv, dval, wv_ref, dw_ref.at[1], db_ref.at[1], FFN_CONV, it).astype(du_ref.dtype)

    def sp(r, shift):
        return pl.BlockSpec((r, cb), lambda j: (0, shift + j))

    def both(r):
        return pl.BlockSpec((2, r, cb), lambda j: (0, 0, j))

    return _call(
        body, name, [S((2, rows, D_FF), BF16), S((2, FFN_CONV, D_FF), F32), S((2, 1, D_FF), F32)], (nb,),
        [sp(rows, 0), sp(rows, nb), sp(rows, 0), sp(FFN_CONV, 0), sp(FFN_CONV, nb), sp(1, 0), sp(1, nb)],
        [both(rows), both(FFN_CONV), both(1)], ("parallel",), (u, u, dact, w, w, b, b), comm=comm)


def _ssd_consts(dtp_ref, bias_ref, alog_ref, hg):
    lane = lax.broadcasted_iota(jnp.int32, (1, LANE), 1)
    pre = dtp_ref[...] + bias_ref[...]
    dt = _softplus(pre)
    a_row = jnp.where(lane < hg, -jnp.exp(alog_ref[...]), 0.0)
    ri = lax.broadcasted_iota(jnp.int32, (T, T), 0)
    ci = lax.broadcasted_iota(jnp.int32, (T, T), 1)
    cs = _dot_hi((ri >= ci).astype(F32), dt * a_row)
    return pre, dt, a_row, cs, ri, ci, lane


def _head_rows(src, hg):
    return jnp.concatenate([jnp.broadcast_to(src[k:k + 1, :], (HEAD_P, src.shape[1])) for k in range(hg)], axis=0)


def _ssd_fwd(xbc, zx, bias, alog, dsk, name, comm=None):
    rows = xbc.shape[0]
    nc = rows // T
    hg = SSM_HEADS // SSM_GROUPS
    gw = hg * HEAD_P
    xoff, boff, coff = 0, D_INNER // D_STATE, (D_INNER + D_BC) // D_STATE
    dtoff = (D_INNER + D_XBC) // LANE

    def body(x_ref, b_ref, c_ref, dtp_ref, bias_ref, alog_ref, dsk_ref, y_ref, hst_ref, hs):
        c = pl.program_id(1)

        @pl.when(c == 0)
        def _():
            hs[...] = jnp.zeros_like(hs)

        _, dt, _, cs, ri, ci, _ = _ssd_consts(dtp_ref, bias_ref, alog_ref, hg)
        cst, dtt = cs.T, dt.T
        xt = x_ref[...].T
        bb, cbf = b_ref[...].astype(BF16), c_ref[...].astype(BF16)
        gt = _dot(bb, cbf, _NT)
        causal_t = ci >= ri
        dskv = dsk_ref[...]
        hall = hs[...]
        hst_ref[0, 0] = hall
        cs8 = cst[0:8, :]
        cl8 = cs8[:, T - 1:T]
        xdt = xt * _head_rows(dtt, hg)
        yo = _head_rows(jnp.exp(cs8), hg) * _dot(hall.astype(BF16), cbf, _NT)
        st = _dot((xdt * _head_rows(jnp.exp(cl8 - cs8), hg)).astype(BF16), bb)
        hs[...] = _head_rows(jnp.exp(cl8), hg) * hall + st
        yds = []
        for k in range(hg):
            sl = slice(k * HEAD_P, (k + 1) * HEAD_P)
            lt = jnp.exp(jnp.where(causal_t, cst[k:k + 1, :] - cs[:, k:k + 1], NEG))
            yds.append(_dot(xdt[sl, :].astype(BF16), (gt * lt).astype(BF16)))
        dsk_r = jnp.concatenate([jnp.broadcast_to(dskv[:, k:k + 1], (HEAD_P, 1)) for k in range(hg)], axis=0)
        y_ref[...] = (jnp.concatenate(yds, axis=0) + yo + dsk_r * xt).T

    vec = pl.BlockSpec((1, LANE), lambda g, c: (0, g))
    return _call(
        body, name, [S((rows, D_INNER), F32), S((nc, SSM_GROUPS, gw, D_STATE), F32)], (SSM_GROUPS, nc),
        [pl.BlockSpec((T, gw), lambda g, c: (c, xoff + g)),
         pl.BlockSpec((T, D_STATE), lambda g, c: (c, boff + g)),
         pl.BlockSpec((T, D_STATE), lambda g, c: (c, coff + g)),
         pl.BlockSpec((T, LANE), lambda g, c: (c, dtoff + g)), vec, vec, vec],
        [pl.BlockSpec((T, gw), lambda g, c: (c, g)), pl.BlockSpec((1, 1, gw, D_STATE), lambda g, c: (c, g, 0, 0))],
        ("parallel", "arbitrary"), (xbc, xbc, xbc, zx, bias, alog, dsk),
        scratch=[pltpu.VMEM((gw, D_STATE), F32)], comm=comm)


def _ssd_bwd(xbc, zx, bias, alog, dsk, dy, hst, dzx, name, comm=None):
    rows = xbc.shape[0]
    nc = rows // T
    hg = SSM_HEADS // SSM_GROUPS
    gw = hg * HEAD_P
    boff, coff = D_INNER // D_STATE, (D_INNER + D_BC) // D_STATE
    dtoff = (D_INNER + D_XBC) // LANE

    def body(x_ref, b_ref, c_ref, dtp_ref, bias_ref, alog_ref, dsk_ref, dy_ref, hst_ref, dzx_in,
             dx_ref, db_ref, dc_ref, ddtp_ref, dalog_ref, ddsk_ref, dbias_ref, dhs):
        del dzx_in
        step = pl.program_id(1)

        @pl.when(step == 0)
        def _():
            dhs[...] = jnp.zeros_like(dhs)
            dalog_ref[...] = jnp.zeros_like(dalog_ref)
            ddsk_ref[...] = jnp.zeros_like(ddsk_ref)
            dbias_ref[...] = jnp.zeros_like(dbias_ref)

        pre, dt, a_row, cs, ri, ci, lane = _ssd_consts(dtp_ref, bias_ref, alog_ref, hg)
        cst, dtt = cs.T, dt.T
        xt, dyt = x_ref[...].T, dy_ref[...].T
        bb, cbf = b_ref[...].astype(BF16), c_ref[...].astype(BF16)
        gt = _dot(bb, cbf, _NT)
        causal_t = ci >= ri
        dskv = dsk_ref[...]
        hall, dhall = hst_ref[0, 0], dhs[...]
        head_row = lax.broadcasted_iota(jnp.int32, (T, 1), 0)
        last_l = lax.broadcasted_iota(jnp.int32, (1, T), 1) == T - 1
        cs8, dt8 = cst[0:8, :], dtt[0:8, :]
        cl8 = cs8[:, T - 1:T]
        e8, wdec8 = jnp.exp(cs8), jnp.exp(cl8 - cs8)
        w8 = wdec8 * dt8
        dt_r, e_r, w_r, ecl_r = _head_rows(dt8, hg), _head_rows(e8, hg), _head_rows(w8, hg), _head_rows(jnp.exp(cl8), hg)
        dsk_r = jnp.concatenate([jnp.broadcast_to(dskv[:, k:k + 1], (HEAD_P, 1)) for k in range(hg)], axis=0)
        hb, dhb = hall.astype(BF16), dhall.astype(BF16)
        xdt = xt * dt_r
        dye = (dyt * e_r).astype(BF16)
        rt = _dot(dhb, bb, _NT)
        yo = e_r * _dot(hb, cbf, _NT)
        dhs[...] = ecl_r * dhall + _dot(dye, cbf)
        dc_acc = _dot(dye, hb, _TN)
        db_acc = _dot((xt * w_r).astype(BF16), dhb, _TN)
        rtx, dyyo, hdh, dyx = rt * xt, dyt * yo, dhall * hall, dyt * xt
        dgt = jnp.zeros((T, T), F32)
        ddt_rows = jnp.zeros((T, T), F32)
        dcs_rows = jnp.zeros((T, T), F32)
        qrow_cols = jnp.zeros((T, LANE), F32)
        ddsk_acc = jnp.zeros((1, LANE), F32)
        dxdts = []
        for k in range(hg):
            sl = slice(k * HEAD_P, (k + 1) * HEAD_P)
            lt = jnp.exp(jnp.where(causal_t, cst[k:k + 1, :] - cs[:, k:k + 1], NEG))
            mpt = gt * lt
            dyb = dyt[sl, :].astype(BF16)
            dxdt = _dot(dyb, mpt.astype(BF16), _NT)
            dmt = _dot(xdt[sl, :].astype(BF16), dyb, _TN)
            dgt = dgt + dmt * lt
            q = dmt * mpt
            q_rows = jnp.sum(q, axis=1, keepdims=True)
            q_cols = jnp.sum(q, axis=0, keepdims=True)
            dxdts.append(dxdt)
            xz = jnp.sum(xt[sl, :] * dxdt, axis=0, keepdims=True)
            dw = jnp.sum(rtx[sl, :], axis=0, keepdims=True)
            wk, wdeck = w8[k:k + 1, :], wdec8[k:k + 1, :]
            dcl = jnp.exp(cl8[k:k + 1, :]) * jnp.sum(hdh[sl, :]) + jnp.sum(dw * wk)
            dcs_r = jnp.sum(dyyo[sl, :], axis=0, keepdims=True) + q_cols - dw * wk + jnp.where(last_l, dcl, 0.0)
            onehot = (lane == k).astype(F32)
            ddt_rows = ddt_rows + jnp.where(head_row == k, xz + dw * wdeck, 0.0)
            dcs_rows = dcs_rows + jnp.where(head_row == k, dcs_r, 0.0)
            qrow_cols = qrow_cols + q_rows * onehot
            ddsk_acc = ddsk_acc + jnp.sum(dyx[sl, :]) * onehot
        dx_ref[...] = (dt_r * jnp.concatenate(dxdts, axis=0) + dsk_r * dyt + rt * w_r).T
        dc_ref[...] = _dot(dgt.T.astype(BF16), bb) + dc_acc
        db_ref[...] = _dot(dgt.astype(BF16), cbf) + db_acc
        da = _dot_hi((ci >= ri).astype(F32), dcs_rows.T - qrow_cols)
        ddtp = (ddt_rows.T + da * a_row) * _sigmoid(pre)
        ddtp = jnp.where(lane < hg, ddtp, 0.0)
        ddtp_ref[...] = ddtp.astype(ddtp_ref.dtype)
        dbias_ref[...] += jnp.sum(ddtp, axis=0, keepdims=True)
        dalog_ref[...] += jnp.sum(da * dt, axis=0, keepdims=True) * a_row
        ddsk_ref[...] += ddsk_acc

    def rc(c):
        return nc - 1 - c

    vec = pl.BlockSpec((1, LANE), lambda g, c: (0, g))
    xsp = pl.BlockSpec((T, gw), lambda g, c: (rc(c), g))
    return _call(
        body, name,
        [S((rows, D_INNER), F32), S((rows, D_BC), F32), S((rows, D_BC), F32),
         S(dzx.shape, dzx.dtype), S((1, SSM_GROUPS * LANE), F32),
         S((1, SSM_GROUPS * LANE), F32), S((1, SSM_GROUPS * LANE), F32)],
        (SSM_GROUPS, nc),
        [xsp,
         pl.BlockSpec((T, D_STATE), lambda g, c: (rc(c), boff + g)),
         pl.BlockSpec((T, D_STATE), lambda g, c: (rc(c), coff + g)),
         pl.BlockSpec((T, LANE), lambda g, c: (rc(c), dtoff + g)), vec, vec, vec,
         xsp, pl.BlockSpec((1, 1, gw, D_STATE), lambda g, c: (rc(c), g, 0, 0)), _ANY],
        [xsp,
         pl.BlockSpec((T, D_STATE), lambda g, c: (rc(c), g)),
         pl.BlockSpec((T, D_STATE), lambda g, c: (rc(c), g)),
         pl.BlockSpec((T, LANE), lambda g, c: (rc(c), dtoff + g)), vec, vec, vec],
        ("parallel", "arbitrary"), (xbc, xbc, xbc, zx, bias, alog, dsk, dy, hst, dzx),
        scratch=[pltpu.VMEM((gw, D_STATE), F32)], comm=comm, aliases={9: 3})


def _attn_tiles(kv_ref, j):
    prev = jnp.maximum(j - 1, 0)
    meta = kv_ref[0:T, :]
    prv = kv_ref[pl.ds(pl.multiple_of(prev * T, T), T), :]
    cur = kv_ref[pl.ds(pl.multiple_of(j * T, T), T), :]
    return jnp.concatenate([meta, prv, cur], axis=0)


def _attn_mask(j):
    r = j * T + lax.broadcasted_iota(jnp.int32, (3 * T, T), 1)
    row = lax.broadcasted_iota(jnp.int32, (3 * T, T), 0)
    t0, t1 = row < T, row < 2 * T
    s = jnp.where(t0, row, (j - 2) * T + row)
    ok = (s <= r) & ((s < N_META) | (s > r - WINDOW))
    use = (t0 & (j >= 2) & (row < N_META)) | (jnp.logical_not(t0) & t1 & (j >= 1)) | jnp.logical_not(t1)
    return ok & use


def _attn_fwd(q, kv, sinks, name, comm=None):
    rows = q.shape[0]
    scale = 1.0 / math.sqrt(ATTN_DH)
    qpk = N_Q_HEADS // N_KV_HEADS

    def body(q_ref, kv_ref, s_ref, o_ref, lse_ref):
        j = pl.program_id(0)
        kv3 = _attn_tiles(kv_ref, j).astype(BF16)
        mask = _attn_mask(j)
        qv = (q_ref[...] * scale).astype(BF16)
        sk = s_ref[...]
        lses = []
        for kh in range(N_KV_HEADS):
            k3 = kv3[:, kh * ATTN_DH:(kh + 1) * ATTN_DH]
            v3 = kv3[:, D_KV + kh * ATTN_DH:D_KV + (kh + 1) * ATTN_DH]
            for g in range(qpk):
                h = kh * qpk + g
                sink = sk[:, h:h + 1]
                sc = jnp.where(mask, _dot(k3, qv[:, h * ATTN_DH:(h + 1) * ATTN_DH], _NT), NEG)
                m = jnp.maximum(jnp.max(sc, axis=0, keepdims=True), sink)
                p = jnp.exp(sc - m)
                den = jnp.sum(p, axis=0, keepdims=True) + jnp.exp(sink - m)
                p = p * (1.0 / den)
                lses.append(m + jnp.log(den))
                o_ref[:, h * ATTN_DH:(h + 1) * ATTN_DH] = _dot(p.astype(BF16), v3, _TN).astype(o_ref.dtype)
        lse_ref[...] = jnp.concatenate(lses, axis=0)

    return _call(
        body, name, [S((rows, D_MODEL), BF16), S((N_Q_HEADS, rows), F32)], (rows // T,),
        [pl.BlockSpec((T, D_MODEL), lambda j: (j, 0)), pl.BlockSpec((rows, 2 * D_KV), lambda j: (0, 0)),
         pl.BlockSpec((1, N_Q_HEADS), lambda j: (0, 0))],
        [pl.BlockSpec((T, D_MODEL), lambda j: (j, 0)), pl.BlockSpec((N_Q_HEADS, T), lambda j: (0, j))],
        ("parallel",), (q, kv, sinks), comm=comm)


def _attn_bwd(q, kv, sinks, do, lse, name, comm=None):
    rows = q.shape[0]
    scale = 1.0 / math.sqrt(ATTN_DH)
    qpk = N_Q_HEADS // N_KV_HEADS

    def body(q_ref, kv_ref, s_ref, do_ref, lse_ref, dq_ref, dkv_ref, ds_ref):
        j = pl.program_id(0)

        @pl.when(j == 0)
        def _():
            dkv_ref[...] = jnp.zeros_like(dkv_ref)
            ds_ref[...] = jnp.zeros_like(ds_ref)

        kv3 = _attn_tiles(kv_ref, j).astype(BF16)
        mask = _attn_mask(j)
        qv = (q_ref[...] * scale).astype(BF16)
        dov = do_ref[...].astype(BF16)
        sk = s_ref[...]
        lsev = lse_ref[...]
        lane = lax.broadcasted_iota(jnp.int32, (1, LANE), 1)
        ds_acc = jnp.zeros((1, LANE), F32)
        prev = jnp.maximum(j - 1, 0)
        mask4 = jnp.concatenate([mask] * qpk, axis=1)
        dqts = []
        for kh in range(N_KV_HEADS):
            ksl = slice(kh * ATTN_DH, (kh + 1) * ATTN_DH)
            vsl = slice(D_KV + kh * ATTN_DH, D_KV + (kh + 1) * ATTN_DH)
            k3, v3 = kv3[:, ksl], kv3[:, vsl]
            heads = [kh * qpk + g for g in range(qpk)]
            q4 = jnp.concatenate([qv[:, h * ATTN_DH:(h + 1) * ATTN_DH] for h in heads], axis=0)
            do4 = jnp.concatenate([dov[:, h * ATTN_DH:(h + 1) * ATTN_DH] for h in heads], axis=0)
            lse4 = jnp.concatenate([lsev[h:h + 1, :] for h in heads], axis=1)
            sink4 = jnp.concatenate([jnp.broadcast_to(sk[:, h:h + 1], (1, T)) for h in heads], axis=1)
            p = jnp.exp(jnp.where(mask4, _dot(k3, q4, _NT), NEG) - lse4)
            ps = jnp.exp(sink4 - lse4)
            dp = _dot(v3, do4, _NT)
            delta = jnp.sum(p * dp, axis=0, keepdims=True)
            dsc = (p * (dp - delta)).astype(BF16)
            dq4 = _dot(k3.T, dsc) * scale
            dk3 = _dot(dsc, q4)
            dv3 = _dot(p.astype(BF16), do4)
            psd = ps * delta
            for g, h in enumerate(heads):
                dqts.append(dq4[:, g * T:(g + 1) * T])
                ds_acc = ds_acc - jnp.sum(psd[:, g * T:(g + 1) * T]) * (lane == h).astype(F32)
            for t, start in enumerate((0, pl.multiple_of(prev * T, T), pl.multiple_of(j * T, T))):
                rsl = pl.ds(start, T)
                dkv_ref[rsl, ksl] += dk3[t * T:(t + 1) * T, :]
                dkv_ref[rsl, vsl] += dv3[t * T:(t + 1) * T, :]
        ds_ref[...] += ds_acc
        dq_ref[...] = jnp.concatenate(dqts, axis=0).T.astype(dq_ref.dtype)

    blk = pl.BlockSpec((T, D_MODEL), lambda j: (j, 0))
    full = pl.BlockSpec((rows, 2 * D_KV), lambda j: (0, 0))
    return _call(
        body, name, [S((rows, D_MODEL), BF16), S((rows, 2 * D_KV), F32), S((1, LANE), F32)], (rows // T,),
        [blk, full, pl.BlockSpec((1, N_Q_HEADS), lambda j: (0, 0)), blk, pl.BlockSpec((N_Q_HEADS, T), lambda j: (0, j))],
        [blk, full, pl.BlockSpec((1, LANE), lambda j: (0, 0))], ("arbitrary",), (q, kv, sinks, do, lse), comm=comm)


BLOCK_BYTES = 1 << 20


def _div_tile(rows, cols, block_bytes=BLOCK_BYTES):
    cap = max(16, block_bytes // (4 * cols))
    best = None
    for t in range(16, min(rows, cap) + 1, 16):
        if rows % t == 0:
            best = t
    return best if best is not None else rows


def _adamw(parts, w, m, v, name, comm=None):
    layers, rows, cols = w.shape
    n = parts[0].shape[0]
    tr = _div_tile(rows, cols)
    tc = _pick(cols, 256) if tr == rows and rows * cols * 4 > 2 * BLOCK_BYTES else cols
    c1 = 1.0 / (1.0 - B1 ** STEP)
    c2 = 1.0 / (1.0 - B2 ** STEP)

    def body(*refs):
        p_refs = refs[:layers]
        w_ref, m_ref, v_ref, g_ref, d_ref, nm_ref, nv_ref = refs[layers:]
        layer = pl.program_id(0)
        for l in range(layers):
            @pl.when(layer == l)
            def _(p_ref=p_refs[l]):
                g = p_ref[0].astype(F32)
                for i in range(1, n):
                    g = g + p_ref[i].astype(F32)
                nm = B1 * m_ref[...] + (1.0 - B1) * g
                nv = B2 * v_ref[...] + (1.0 - B2) * (g * g)
                g_ref[...] = g
                nm_ref[...] = nm
                nv_ref[...] = nv
                d_ref[...] = -LR * ((nm * c1) / (jnp.sqrt(nv * c2) + EPS) + WD * w_ref[...])

    def part_spec(l):
        return pl.BlockSpec((n, tr, tc), lambda k, i, j: (0, jnp.where(k == l, i, 0), jnp.where(k == l, j, 0)))

    row = pl.BlockSpec((None, tr, tc), lambda k, i, j: (k, i, j))
    return _call(body, name, [S((layers, rows, cols), F32)] * 4, (layers, rows // tr, cols // tc),
                 [part_spec(l) for l in range(layers)] + [row, row, row], [row] * 4,
                 ("parallel", "parallel", "parallel"), (*parts, w, m, v), comm=comm)


def _sum_parts(parts, name):
    n, rows, cols = parts[0].shape
    nb = len(parts)
    tr = _div_tile(rows, cols, 2 * BLOCK_BYTES)

    def body(*refs):
        o_ref = refs[nb]
        blk = pl.program_id(0)
        for l in range(nb):
            @pl.when(blk == l)
            def _(p_ref=refs[l]):
                g = p_ref[0].astype(F32)
                for i in range(1, n):
                    g = g + p_ref[i].astype(F32)
                o_ref[...] = g

    def part_spec(l):
        return pl.BlockSpec((n, tr, cols), lambda k, i: (0, jnp.where(k == l, i, 0), 0))

    per = rows // tr
    return pl.pallas_call(body, name=name, out_shape=S((nb * rows, cols), F32), grid=(nb, per),
                          in_specs=[part_spec(l) for l in range(nb)],
                          out_specs=pl.BlockSpec((tr, cols), lambda k, i: (k * per + i, 0)),
                          compiler_params=_cp(("parallel", "parallel")))(*parts)


def _col_segments(ws, runs):
    segs = []
    for glo, mlo, n in runs:
        while n > 0:
            d, off = divmod(glo, ws)
            take = min(n, ws - off)
            segs.append((d, off, mlo, take))
            glo, mlo, n = glo + take, mlo + take, n - take
    return segs


def _assemble_cols(gs, width, segs, name):
    _, rows, ws = gs[0].shape
    nb = len(gs)
    rb = _div_tile(rows, width // 2, 4 * BLOCK_BYTES)
    per = rows // rb

    def body(*refs):
        o_ref = refs[nb]
        piece = pl.program_id(0)
        for l in range(nb):
            @pl.when(piece == l)
            def _(g_ref=refs[l]):
                o_ref[...] = jnp.zeros_like(o_ref)
                for d, off, mlo, n in segs:
                    o_ref[:, mlo:mlo + n] = g_ref[d, :, off:off + n]

    def piece_spec(l):
        return pl.BlockSpec((N_DEV, rb, ws), lambda k, i: (0, jnp.where(k == l, i, 0), 0))

    return pl.pallas_call(
        body, name=name, out_shape=S((nb * rows, width), gs[0].dtype), grid=(nb, per),
        in_specs=[piece_spec(l) for l in range(nb)],
        out_specs=pl.BlockSpec((rb, width), lambda k, i: (k * per + i, 0)),
        compiler_params=_cp(("parallel", "parallel")))(*gs)


def _scatter_cols(dw, ws, segs, name):
    rows, width = dw.shape
    rb = _div_tile(rows, width, 4 * BLOCK_BYTES)

    def body(w_ref, o_ref):
        for d, off, mlo, n in segs:
            o_ref[d, :, off:off + n] = w_ref[:, mlo:mlo + n].astype(o_ref.dtype)

    return pl.pallas_call(
        body, name=name, out_shape=S((N_DEV, rows, ws), BF16), grid=(rows // rb,),
        in_specs=[pl.BlockSpec((rb, width), lambda i: (i, 0))],
        out_specs=pl.BlockSpec((N_DEV, rb, ws), lambda i: (0, i, 0)), compiler_params=_cp(("parallel",)))(dw)


def _gather_comm(xs):
    n = len(xs)

    def setup(x_refs, out_refs, sems):
        send_sems, recv_sems, local_sems = sems
        mx, my, mc = lax.axis_index("x"), lax.axis_index("y"), lax.axis_index("c")
        me, sibling = (mx, my, mc), (mx, my, 1 - mc)
        chips = [(1 - mx, my), (mx, 1 - my), (1 - mx, 1 - my)]

        def blk(a, px, py, pc):
            return out_refs[a].at[4 * px + 2 * py + pc]

        def copy(a, k, block, to, src=None):
            return pltpu.make_async_remote_copy(
                src_ref=blk(a, *block) if src is None else src, dst_ref=blk(a, *block),
                send_sem=send_sems.at[a, k], recv_sem=recv_sems.at[a, k], device_id=to, device_id_type=_MESH)

        mine = [pltpu.make_async_copy(x_refs[a], blk(a, *me), local_sems.at[a]) for a in range(n)]
        own = []
        for a in range(n):
            own.append(copy(a, 0, me, sibling, src=x_refs[a]))
            own += [copy(a, 1 + i, me, (*chip, mc), src=x_refs[a]) for i, chip in enumerate(chips)]
        return me, sibling, chips, mc, copy, mine, own

    def first(x_refs, out_refs, sems):
        _, _, _, _, _, mine, own = setup(x_refs, out_refs, sems)
        for cp in mine + own:
            cp.start()

    def last(x_refs, out_refs, sems):
        me, sibling, chips, mc, copy, mine, own = setup(x_refs, out_refs, sems)
        passed = []
        for a in range(n):
            for i, chip in enumerate(chips):
                copy(a, 1 + i, (*chip, mc), me).wait_recv()
                passed.append(copy(a, 4 + i, (*chip, mc), sibling))
                passed[-1].start()
        for a in range(n):
            copy(a, 0, sibling, me).wait_recv()
            for i, chip in enumerate(chips):
                copy(a, 4 + i, (*chip, 1 - mc), me).wait_recv()
        for cp in own + passed:
            cp.wait_send()
        for cp in mine:
            cp.wait()

    return _Comm(list(xs), [S((N_DEV,) + x.shape, x.dtype) for x in xs],
                 [pltpu.SemaphoreType.DMA((n, 7)), pltpu.SemaphoreType.DMA((n, 7)), pltpu.SemaphoreType.DMA((n,))],
                 first, last)


def _swap_comm(gs):
    n = len(gs)

    def copies(g_refs, out_refs, sems):
        send_sems, recv_sems = sems
        mx, my, mc = lax.axis_index("x"), lax.axis_index("y"), lax.axis_index("c")
        return [pltpu.make_async_remote_copy(
            src_ref=g_refs[a].at[2 * k + 1 - mc], dst_ref=out_refs[a].at[k], send_sem=send_sems.at[a, k],
            recv_sem=recv_sems.at[a, k], device_id=(mx, my, 1 - mc), device_id_type=_MESH)
            for a in range(n) for k in range(4)]

    def first(g_refs, out_refs, sems):
        for cp in copies(g_refs, out_refs, sems):
            cp.start()

    def last(g_refs, out_refs, sems):
        for cp in copies(g_refs, out_refs, sems):
            cp.wait()

    return _Comm(list(gs), [S((4,) + g.shape[1:], g.dtype) for g in gs],
                 [pltpu.SemaphoreType.DMA((n, 4)), pltpu.SemaphoreType.DMA((n, 4))], first, last)


def _chips_comm(parts):
    n = len(parts)

    def copies(p_refs, out_refs, sems):
        send_sems, recv_sems, local_sems = sems
        mx, my, mc = lax.axis_index("x"), lax.axis_index("y"), lax.axis_index("c")
        mychip = 2 * mx + my
        chips = [(1 - mx, my), (mx, 1 - my), (1 - mx, 1 - my)]
        mine = [pltpu.make_async_copy(p_refs[a].at[mychip], out_refs[a].at[mychip], local_sems.at[a])
                for a in range(n)]
        return mine + [pltpu.make_async_remote_copy(
            src_ref=p_refs[a].at[2 * cx + cy], dst_ref=out_refs[a].at[mychip], send_sem=send_sems.at[a, i],
            recv_sem=recv_sems.at[a, i], device_id=(cx, cy, mc), device_id_type=_MESH)
            for a in range(n) for i, (cx, cy) in enumerate(chips)]

    def first(p_refs, out_refs, sems):
        for cp in copies(p_refs, out_refs, sems):
            cp.start()

    def last(p_refs, out_refs, sems):
        for cp in copies(p_refs, out_refs, sems):
            cp.wait()

    return _Comm(list(parts), [S(p.shape, p.dtype) for p in parts],
                 [pltpu.SemaphoreType.DMA((n, 3)), pltpu.SemaphoreType.DMA((n, 3)), pltpu.SemaphoreType.DMA((n,))],
                 first, last)


def _join_comms(comms):
    def split(refs, counts):
        out, p = [], 0
        for cnt in counts:
            out.append(refs[p:p + cnt])
            p += cnt
        return out

    ni = [len(c.ins) for c in comms]
    no = [len(c.out_shapes) for c in comms]
    ns = [len(c.scratch) for c in comms]

    def first(in_refs, out_refs, sems):
        for c, i, o, s in zip(comms, split(in_refs, ni), split(out_refs, no), split(sems, ns)):
            c.first(i, o, s)

    def last(in_refs, out_refs, sems):
        for c, i, o, s in zip(comms, split(in_refs, ni), split(out_refs, no), split(sems, ns)):
            c.last(i, o, s)

    return _Comm([x for c in comms for x in c.ins], [x for c in comms for x in c.out_shapes],
                 [x for c in comms for x in c.scratch], first, last)


def _add_pairs(mine, theirs, core, name):
    _, rows, cols = mine.shape
    tr = _div_tile(rows, cols // 2)

    def body(core_ref, a_ref, b_ref, o_ref):
        o_ref[...] = (a_ref[...].astype(F32) + b_ref[...].astype(F32)).astype(o_ref.dtype)

    return pl.pallas_call(
        body, name=name, out_shape=S((4, rows, cols), BF16),
        grid_spec=pltpu.PrefetchScalarGridSpec(
            num_scalar_prefetch=1, grid=(4, rows // tr),
            in_specs=[pl.BlockSpec((None, tr, cols), lambda k, i, c: (2 * k + c[0], i, 0)),
                      pl.BlockSpec((None, tr, cols), lambda k, i, c: (k, i, 0))],
            out_specs=pl.BlockSpec((None, tr, cols), lambda k, i, c: (k, i, 0))),
        compiler_params=_cp(("parallel", "parallel")))(core, mine, theirs)


def _run_comm(comm, name):
    ci, co = len(comm.ins), len(comm.out_shapes)

    def body(*refs):
        comm.first(refs[:ci], refs[ci:ci + co], refs[ci + co:])
        comm.last(refs[:ci], refs[ci:ci + co], refs[ci + co:])

    return pl.pallas_call(body, name=name, out_shape=list(comm.out_shapes), in_specs=[_HBM] * ci,
                          out_specs=[_HBM] * co, scratch_shapes=list(comm.scratch))(*comm.ins)


def _flat_rows(n_elems, mult):
    rows = -(-n_elems // LANE)
    return -(-rows // mult) * mult


def _pack(arrs, lead, mult, dtype):
    lead_shape = arrs[0].shape[:lead]
    flat = jnp.concatenate([a.astype(dtype).reshape(lead_shape + (-1,)) for a in arrs], axis=-1)
    n = flat.shape[-1]
    rows = _flat_rows(n, mult)
    flat = jnp.pad(flat, [(0, 0)] * lead + [(0, rows * LANE - n)])
    return flat.reshape(lead_shape + (rows, LANE))


def _unpack(flat, lead, shapes):
    lead_shape = flat.shape[:lead]
    flat = flat.reshape(lead_shape + (-1,))
    out, off = [], 0
    for shp in shapes:
        n = math.prod(shp)
        out.append(flat[..., off:off + n].reshape(lead_shape + tuple(shp)))
        off += n
    return out


def _split8(full, ax, n):
    shp = full.shape
    return jnp.moveaxis(full.reshape(shp[:ax] + (N_DEV, n) + shp[ax + 1:]), ax, 0)


def _join8(g, ax):
    shp = g.shape[1:]
    return jnp.moveaxis(g, 0, ax).reshape(shp[:ax] + (N_DEV * shp[ax],) + shp[ax + 1:])


def _group_lanes(v, hg):
    v = v.reshape(SSM_GROUPS, hg)
    return jnp.pad(v, ((0, 0), (0, LANE - hg))).reshape(1, SSM_GROUPS * LANE)


def _ungroup_lanes(v, hg):
    return v.reshape(SSM_GROUPS, LANE)[:, :hg].reshape(1, SSM_GROUPS * hg)


def kernel(x, meta_tokens, a_norm_pre, a_w_in, a_conv_w, a_conv_b, a_dt_bias, a_a_log, a_d_skip, a_gate_norm, a_w_out, a_norm_post, kv_norm, w_kv, b_norm_pre, b_w_q, b_sinks, b_w_o, b_norm_post, f_norm_pre, f_w_up, f_conv_w, f_conv_b, f_w_down, f_norm_post, loss_target, m_meta_tokens, m_a_norm_pre, m_a_w_in, m_a_conv_w, m_a_conv_b, m_a_dt_bias, m_a_a_log, m_a_d_skip, m_a_gate_norm, m_a_w_out, m_a_norm_post, m_kv_norm, m_w_kv, m_b_norm_pre, m_b_w_q, m_b_sinks, m_b_w_o, m_b_norm_post, m_f_norm_pre, m_f_w_up, m_f_conv_w, m_f_conv_b, m_f_w_down, m_f_norm_post, v_meta_tokens, v_a_norm_pre, v_a_w_in, v_a_conv_w, v_a_conv_b, v_a_dt_bias, v_a_a_log, v_a_d_skip, v_a_gate_norm, v_a_w_out, v_a_norm_post, v_kv_norm, v_w_kv, v_b_norm_pre, v_b_w_q, v_b_sinks, v_b_w_o, v_b_norm_post, v_f_norm_pre, v_f_w_up, v_f_conv_w, v_f_conv_b, v_f_w_down, v_f_norm_post):
    args = locals()
    wts = {n: args[n] for n in WEIGHTS}
    mom = {n: args["m_" + n] for n in WEIGHTS}
    var = {n: args["v_" + n] for n in WEIGHTS}
    mx, my, mc = lax.axis_index("x"), lax.axis_index("y"), lax.axis_index("c")
    me = 4 * mx + 2 * my + mc
    rows = _seq_rows()
    hg = SSM_HEADS // SSM_GROUPS
    d = D_MODEL

    n_main = D_INNER + D_XBC
    ws_in, ws_up = a_w_in.shape[2], f_w_up.shape[2]
    segs_in = _col_segments(ws_in, [(0, 0, n_main)] + [(n_main + hg * g, n_main + LANE * g, hg)
                                                      for g in range(SSM_GROUPS)])
    segs_up = _col_segments(ws_up, [(0, 0, 2 * D_FF)])
    def gather_of(*ws):
        return _gather_comm([w.astype(BF16) for w in ws])

    small_full, = _run_comm(_gather_comm([_pack([wts[n] for n in SMALL], 0, 8, F32)]), "gather_small")
    full = {}
    for n, g in zip(SMALL, _unpack(small_full, 1, [wts[n].shape for n in SMALL])):
        full[n] = _join8(g, SHARD_AXIS[n])
    (h0, hn0), (g_in,) = _embed_norm(full["meta_tokens"], x[0], full["a_norm_pre"], rows, "embed_norm",
                                     comm=gather_of(a_w_in[0]))
    w_in_all = _assemble_cols([g_in], n_main + SSM_GROUPS * LANE, segs_in, "asm_w_in")
    w_up, w_down = [None, None], [None, None]
    bias_g = _group_lanes(wts["a_dt_bias"], hg)
    alog_g = _group_lanes(wts["a_a_log"], hg)
    dsk_g = _group_lanes(wts["a_d_skip"], hg)
    a_conv_w, a_conv_b = full["a_conv_w"][0], full["a_conv_b"]
    f_cw, f_cb = full["f_conv_w"], wts["f_conv_b"]
    fpre, fpost = wts["f_norm_pre"], wts["f_norm_post"]


    zx, (g_out, g_o) = _mm(hn0, w_in_all, "nn", F32, "mm_in", comm=gather_of(a_w_out[0], b_w_o[0]))
    w_out = g_out.reshape(D_INNER, d)
    xbc = _conv_silu_fwd(zx, a_conv_w, a_conv_b, "conv_a")
    (y_ssd, hst), (g_up0, g_kv, g_q) = _ssd_fwd(xbc, zx, bias_g, alog_g, dsk_g, "ssd_fwd",
                                                comm=gather_of(f_w_up[0], w_kv, b_w_q[0]))
    w_up[0] = _assemble_cols([g_up0], 2 * D_FF, segs_up, "asm_w_up0")
    yn = _gatenorm_fwd(y_ssd, zx, full["a_gate_norm"], "gatenorm")
    mix_a = _mm(yn, w_out, "nn", F32, "mm_out")
    h1, (fn0,) = _resid_norm(h0, mix_a, full["a_norm_post"], [fpre[0:1]], "resid_a")

    half = d // 2
    u0, (g_dn0,) = _mm(fn0, w_up[0], "nn", F32, "mm_up0", comm=gather_of(f_w_down[0]))
    act0, (g_up1a,) = _ffn_act_fwd(u0, f_cw[0], f_cb[0:1], "ffn_act0", comm=gather_of(f_w_up[1, :half]))
    ffn0 = _mm(act0, g_dn0.reshape(D_FF, d), "nn", F32, "mm_down0")
    w_kvf, w_q, w_o = g_kv.reshape(d, 2 * D_KV), g_q.reshape(d, d), g_o.reshape(d, d)
    h2, (kvn, bn) = _resid_norm(h1, ffn0, fpost[0:1], [wts["kv_norm"].reshape(1, d), wts["b_norm_pre"]], "resid_f0")
    kv = _mm(kvn, w_kvf, "nn", F32, "mm_kv")
    q = _mm(bn, w_q, "nn", F32, "mm_q")
    (o, lse), (g_up1b,) = _attn_fwd(q, kv, wts["b_sinks"], "attn_fwd", comm=gather_of(f_w_up[1, half:]))
    w_up[1] = _assemble_cols([g_up1a, g_up1b], 2 * D_FF, segs_up, "asm_w_up1")
    mix_b = _mm(o, w_o, "nn", F32, "mm_o")
    h3, (fn1,) = _resid_norm(h2, mix_b, wts["b_norm_post"], [fpre[1:2]], "resid_b")
    u1, (g_dn1,) = _mm(fn1, w_up[1], "nn", F32, "mm_up1", comm=gather_of(f_w_down[1]))
    w_down = [g_dn0.reshape(D_FF, d), g_dn1.reshape(D_FF, d)]
    act1 = _ffn_act_fwd(u1, f_cw[1], f_cb[1:2], "ffn_act1")
    ffn1 = _mm(act1, w_down[1], "nn", F32, "mm_down1")
    dh4, loss_row, dffn1, dw_post1 = _final_loss(h3, ffn1, fpost[1:2], loss_target[0], "loss")
    loss = lax.psum(loss_row[0, 0], ("x", "y", "c"))

    grads = {}

    core = mc.astype(jnp.int32).reshape(1)

    def carried(res, comm):
        return res if comm is not None else (res, None)

    def ffn_bwd(dh_out, dffn, h_in, fn, u, act, i, then, c_dact=None, c_dwdown=None, c_dwup=None, c_dfn=None):
        dact, got_a = carried(_mm(dffn, w_down[i], "nt", F32, f"mm_dact{i}", comm=c_dact), c_dact)
        dw_down, got_b = carried(_mm(act, dffn, "tn", BF16, f"mm_dwdown{i}", comm=c_dwdown), c_dwdown)
        dw_down = dw_down.reshape(N_DEV, -1, d)
        du, dwc, dbc = _ffn_act_bwd(u, dact, f_cw[i], f_cb[i:i + 1], f"ffn_act_bwd{i}")
        dfn, (s_dn, *got_d) = _mm(du, w_up[i], "nt", F32, f"mm_dfn{i}", comm=_join_comms(
            [_swap_comm([dw_down])] + ([c_dfn] if c_dfn is not None else [])))
        sum_dn = _add_pairs(dw_down, s_dn, core, f"rs_add_dn{i}")
        dw_up, got_c = carried(_mm(fn, du, "tn", BF16, f"mm_dwup{i}", comm=c_dwup, shard_cols=ws_up), c_dwup)
        (dh_in, dw_pre, dbranch, dw_branch), (s_up,) = _norm_bwd(
            h_in, fpre[i:i + 1], dfn, dh_out, F32, f"nb_fpre{i}", comm=_swap_comm([dw_up]), then=then)
        sum_up = _add_pairs(dw_up, s_up, core, f"rs_add_up{i}")
        return dh_in, dbranch, dw_branch, dict(sum_down=sum_dn, cw=jnp.concatenate([dwc[0], dwc[1]], axis=1),
                                               cb=jnp.concatenate([dbc[0], dbc[1]], axis=1), sum_up=sum_up,
                                               pre=dw_pre), got_a, got_b, got_c, got_d

    dh3, dmix_b, grads["b_norm_post"], gf1, _, _, _, _ = ffn_bwd(dh4, dffn1, h3, fn1, u1, act1, 1,
                                                                 (mix_b, wts["b_norm_post"]))
    do = _mm(dmix_b, w_o, "nt", F32, "mm_do")
    dw_o = _mm(o, dmix_b, "tn", BF16, "mm_dwo").reshape(N_DEV, -1, d)
    half_up = gf1["sum_up"].shape[1] // 2
    (dq, dkv, dsinks), (p_up1a, s_o) = _attn_bwd(
        q, kv, wts["b_sinks"], do, lse, "attn_bwd",
        comm=_join_comms([_chips_comm([gf1["sum_up"][:, :half_up]]), _swap_comm([dw_o])]))
    sum_o = _add_pairs(dw_o, s_o, core, "rs_add_o")
    grads["b_sinks"] = dsinks[:, :N_Q_HEADS]
    dbn = _mm(dq, w_q, "nt", F32, "mm_dbn")
    dw_q = _mm(bn, dq, "tn", BF16, "mm_dwq").reshape(N_DEV, -1, d)
    dkv16 = dkv.astype(BF16)
    dkvn = _mm(dkv16, w_kvf, "nt", F32, "mm_dkvn")
    dw_kv = _mm(kvn, dkv16, "tn", BF16, "mm_dwkv").reshape(N_DEV, -1, 2 * D_KV)
    (dh2, grads["b_norm_pre"]), (s_q, s_kv) = _norm_bwd(h2, wts["b_norm_pre"], dbn, dh3, F32, "nb_bpre",
                                                        comm=_swap_comm([dw_q, dw_kv]))
    sum_q, sum_kv = _add_pairs(dw_q, s_q, core, "rs_add_q"), _add_pairs(dw_kv, s_kv, core, "rs_add_kv")
    dh2, dw_kvn, dffn0, dw_post0 = _norm_bwd(h2, wts["kv_norm"].reshape(1, d), dkvn, dh2, F32, "nb_kv",
                                             then=(ffn0, fpost[0:1]))
    grads["kv_norm"] = dw_kvn.reshape(d)
    dh1, dmix_a, grads["a_norm_post"], gf0, (p_o,), (p_q, p_kv), (p_dn1,), (p_up1b,) = ffn_bwd(
        dh2, dffn0, h1, fn0, u0, act0, 0, (mix_a, full["a_norm_post"]), c_dact=_chips_comm([sum_o]),
        c_dwdown=_chips_comm([sum_q, sum_kv]), c_dwup=_chips_comm([gf1["sum_down"]]),
        c_dfn=_chips_comm([gf1["sum_up"][:, half_up:]]))
    p_up1 = jnp.concatenate([p_up1a, p_up1b], axis=1)
    grads["f_norm_post"] = jnp.concatenate([dw_post0, dw_post1], axis=0)
    grads["f_norm_pre"] = jnp.concatenate([gf0["pre"], gf1["pre"]], axis=0)
    grads["f_conv_w"] = jnp.stack([gf0["cw"], gf1["cw"]])
    grads["f_conv_b"] = jnp.concatenate([gf0["cb"], gf1["cb"]], axis=0)

    dyn = _mm(dmix_a, w_out, "nt", F32, "mm_dyn")
    dw_out = _mm(yn, dmix_a, "tn", BF16, "mm_dwout").reshape(N_DEV, -1, d)
    (dy_ssd, dzx, grads["a_gate_norm"]), (s_out,) = _gatenorm_bwd(y_ssd, zx, full["a_gate_norm"], dyn, "gatenorm_bwd",
                                                                  comm=_swap_comm([dw_out]))
    sum_out = _add_pairs(dw_out, s_out, core, "rs_add_out")
    (dxs, dbm, dcm, dzx, dalog, ddsk, dbias), (p_up0,) = _ssd_bwd(
        xbc, zx, bias_g, alog_g, dsk_g, dy_ssd, hst, dzx, "ssd_bwd", comm=_chips_comm([gf0["sum_up"]]))
    grads["a_a_log"] = _ungroup_lanes(dalog, hg)
    grads["a_d_skip"] = _ungroup_lanes(ddsk, hg)
    grads["a_dt_bias"] = _ungroup_lanes(dbias, hg)
    dzx, dcw, dcb = _conv_silu_bwd(zx, dxs, dbm, dcm, a_conv_w, a_conv_b, dzx, "conv_a_bwd")
    grads["a_conv_w"], grads["a_conv_b"] = dcw[None], dcb
    dw_in_all, (p_dn0,) = _mm(hn0, dzx, "tn", BF16, "mm_dwin", comm=_chips_comm([gf0["sum_down"]]))
    dw_in8 = _scatter_cols(dw_in_all, ws_in, segs_in, "scat_w_in")
    dhn0, (s_in, p_out) = _mm(dzx, w_in_all, "nt", F32, "mm_dhn0",
                              comm=_join_comms([_swap_comm([dw_in8]), _chips_comm([sum_out])]))
    sum_in = _add_pairs(dw_in8, s_in, core, "rs_add_in")
    half_in = sum_in.shape[1] // 2
    (grads["meta_tokens"], g_x, grads["a_norm_pre"]), (p_in_a,) = _norm_bwd(
        h0, full["a_norm_pre"], dhn0, dh1, F32, "nb_apre", comm=_chips_comm([sum_in[:, :half_in]]), split_rows=SEQ)
    grad_x = g_x[None]

    small_local = _pack([_split8(grads[n], SHARD_AXIS[n], wts[n].shape[SHARD_AXIS[n]]) for n in SMALL], 1, 8, F32)
    repl_local = _pack([grads[n] for n in REPL], 0, 8, F32)
    n_sr = small_local.shape[1]
    small_vec = jnp.concatenate([small_local.reshape(N_DEV * n_sr, LANE), repl_local], axis=0)
    tail = _join_comms([_chips_comm([sum_in[:, half_in:]]), _gather_comm([small_vec])])
    parts_big = dict(a_w_out=[p_out], w_kv=[p_kv], b_w_q=[p_q], b_w_o=[p_o], f_w_down=[p_dn0, p_dn1])

    def flat_f32(dct, names, mult):
        return _pack([dct[n] for n in names], 0, mult, F32)

    def adamw_big(n, comm=None):
        shp3 = (len(parts_big[n]),) + parts_big[n][0].shape[1:]
        res = _adamw(parts_big[n], *[dct[n].reshape(shp3) for dct in (wts, mom, var)], f"adamw_{n}", comm=comm)
        res, got = res if comm is not None else (res, None)
        big_out[n] = [r.reshape(wts[n].shape) for r in res]
        return got

    big_out = {}
    def swap_last(a):
        return jnp.swapaxes(a, -1, -2)

    g_up_t = swap_last(_sum_parts([p_up0, p_up1], "sum_w_up").reshape(f_w_up.shape))
    res, (p_in_b, small_all) = _adamw([g_up_t[0:1], g_up_t[1:2]], *[swap_last(dct["f_w_up"]) for dct in (wts, mom, var)],
                                      "adamw_f_w_up", comm=tail)
    big_out["f_w_up"] = [swap_last(r) for r in res]
    for n in BIG:
        if n not in ("f_w_up", "a_w_in"):
            adamw_big(n)
    g_in_t = swap_last(_sum_parts([p_in_a, p_in_b], "sum_w_in"))[None]
    res = _adamw([g_in_t], *[swap_last(dct["a_w_in"]) for dct in (wts, mom, var)], "adamw_a_w_in")
    big_out["a_w_in"] = [swap_last(r) for r in res]
    mine_small = lax.dynamic_slice_in_dim(small_all, me * n_sr, n_sr, axis=1)
    parts_small = jnp.concatenate([mine_small, small_all[:, N_DEV * n_sr:]], axis=1)
    sm_in = [jnp.concatenate([flat_f32(dct, SMALL, 8), flat_f32(dct, REPL, 8)], axis=0)[None] for dct in (wts, mom, var)]
    small_out = [r[0] for r in _adamw([parts_small], *sm_in, "adamw_small")]

    outs = []
    for kind in range(4):
        res = {n: big_out[n][kind] for n in BIG}
        for n, a in zip(SMALL, _unpack(small_out[kind][:n_sr], 0, [wts[n].shape for n in SMALL])):
            res[n] = a
        for n, a in zip(REPL, _unpack(small_out[kind][n_sr:], 0, [wts[n].shape for n in REPL])):
            res[n] = a
        outs.append(res)
    return (loss, grad_x, *[outs[0][n] for n in WEIGHTS], *[outs[1][n] for n in WEIGHTS],
            *[outs[2][n] for n in WEIGHTS], *[outs[3][n] for n in WEIGHTS])
```

```python
import functools
import math

import jax
import jax.numpy as jnp
from jax import lax
from jax.experimental import pallas as pl
from jax.experimental.pallas import tpu as pltpu

F32, BF16 = jnp.float32, jnp.bfloat16
S = jax.ShapeDtypeStruct

D_MODEL = 1024
SEQ = 2048
N_META = 16
D_INNER = 2048
HEAD_P = 64
SSM_HEADS = D_INNER // HEAD_P
SSM_GROUPS = 4
D_STATE = 128
SSM_CONV = 4
D_BC = SSM_GROUPS * D_STATE
D_XBC = D_INNER + 2 * D_BC
ATTN_DH = 64
N_Q_HEADS = D_MODEL // ATTN_DH
N_KV_HEADS = 4
D_KV = N_KV_HEADS * ATTN_DH
WINDOW = 128
D_FF = 2816
FFN_CONV = 3
RMS_EPS = 1e-6
NEG = -1e30
LR, B1, B2, EPS, WD, STEP = 0.001, 0.9, 0.999, 1e-08, 0.01, 10

N_DEV = 8
T = 128
LANE = 128
VMEM_LIMIT = 48 * 1024 * 1024

BIG = ("a_w_in", "a_w_out", "w_kv", "b_w_q", "b_w_o", "f_w_up", "f_w_down")
SMALL = ("meta_tokens", "a_norm_pre", "a_conv_w", "a_conv_b", "a_gate_norm", "a_norm_post", "f_conv_w")
REPL = ("a_dt_bias", "a_a_log", "a_d_skip", "kv_norm", "b_norm_pre", "b_sinks", "b_norm_post",
        "f_norm_pre", "f_conv_b", "f_norm_post")
SHARD_AXIS = dict(a_w_in=2, a_w_out=1, w_kv=0, b_w_q=1, b_w_o=1, f_w_up=2, f_w_down=1, meta_tokens=1,
                  a_norm_pre=1, a_conv_w=2, a_conv_b=1, a_gate_norm=1, a_norm_post=1, f_conv_w=2)
WEIGHTS = ("meta_tokens", "a_norm_pre", "a_w_in", "a_conv_w", "a_conv_b", "a_dt_bias", "a_a_log", "a_d_skip",
           "a_gate_norm", "a_w_out", "a_norm_post", "kv_norm", "w_kv", "b_norm_pre", "b_w_q", "b_sinks", "b_w_o",
           "b_norm_post", "f_norm_pre", "f_w_up", "f_conv_w", "f_conv_b", "f_w_down", "f_norm_post")


def _seq_rows():
    return -(-(N_META + SEQ) // T) * T


def _cp(sem=None):
    return pltpu.CompilerParams(dimension_semantics=sem, vmem_limit_bytes=VMEM_LIMIT)


def _pick(n, target):
    t = min(n, target)
    t -= t % LANE
    while n % t:
        t -= LANE
    return t


def _sigmoid(x):
    return 0.5 * jnp.tanh(0.5 * x) + 0.5


def _softplus(x):
    return jnp.maximum(x, 0.0) + jnp.log(1.0 + jnp.exp(-jnp.abs(x)))


_NN = (((1,), (0,)), ((), ()))
_NT = (((1,), (1,)), ((), ()))
_TN = (((0,), (0,)), ((), ()))


def _dot(a, b, dims=_NN):
    return lax.dot_general(a, b, dims, preferred_element_type=F32)


def _dot_hi(a, b):
    return lax.dot_general(a, b, _NN, precision=lax.Precision.HIGHEST, preferred_element_type=F32)


_HBM = pl.BlockSpec(memory_space=pltpu.HBM)
_MESH = pl.DeviceIdType.MESH


class _Comm:
    def __init__(self, ins, out_shapes, scratch, first, last):
        self.ins, self.out_shapes, self.scratch, self.first, self.last = ins, out_shapes, scratch, first, last


_ANY = pl.BlockSpec(memory_space=pl.ANY)


def _call(body, name, out_shape, grid, in_specs, out_specs, sem, args, scratch=(), comm=None, aliases=None):
    aliases = aliases or {}
    if comm is None:
        return pl.pallas_call(body, name=name, out_shape=out_shape, grid=grid, in_specs=in_specs, out_specs=out_specs,
                              scratch_shapes=list(scratch), input_output_aliases=aliases,
                              compiler_params=_cp(sem))(*args)
    single = not isinstance(out_shape, (list, tuple))
    outs = [out_shape] if single else list(out_shape)
    ospecs = [out_specs] if single else list(out_specs)
    n_in, n_out, n_scr, ci, co = len(in_specs), len(outs), len(scratch), len(comm.ins), len(comm.out_shapes)

    def carrier(*refs):
        p = 0
        parts = []
        for cnt in (n_in, ci, n_out, co, n_scr, len(comm.scratch)):
            parts.append(refs[p:p + cnt])
            p += cnt
        ins, cins, outs_r, couts, scr, cscr = parts
        ids = [pl.program_id(i) for i in range(len(grid))]
        first, last = ids[0] == 0, ids[0] == grid[0] - 1
        for i in range(1, len(grid)):
            first, last = first & (ids[i] == 0), last & (ids[i] == grid[i] - 1)

        @pl.when(first)
        def _():
            comm.first(cins, couts, cscr)

        body(*ins, *outs_r, *scr)

        @pl.when(last)
        def _():
            comm.last(cins, couts, cscr)

    res = pl.pallas_call(
        carrier, name=name, out_shape=outs + list(comm.out_shapes), grid=grid,
        in_specs=list(in_specs) + [_HBM] * ci, out_specs=ospecs + [_HBM] * co,
        scratch_shapes=list(scratch) + list(comm.scratch), input_output_aliases=aliases,
        compiler_params=_cp(("arbitrary",) * len(grid)))(*args, *comm.ins)
    mine = res[0] if single else list(res[:n_out])
    return mine, list(res[n_out:])


def _mm(a, b, mode, out_dtype, name, comm=None, shard_cols=None):
    if mode == "tn":
        m, kk = a.shape
        planes, width = (b.shape[0], b.shape[2]) if b.ndim == 3 else (1, b.shape[1])
        n = planes * width
        tko, tn = _pick(kk, 512), _pick(width, 1536)
        per = width // tn
        b_spec = (pl.BlockSpec((None, m, tn), lambda i, j: (j // per, 0, j % per)) if b.ndim == 3
                  else pl.BlockSpec((m, tn), lambda i, j: (0, j)))
        if shard_cols is None:
            def body(a_ref, b_ref, o_ref):
                o_ref[...] = _dot(a_ref[...], b_ref[...], _TN).astype(o_ref.dtype)

            out_shape, out_spec = S((kk, n), out_dtype), pl.BlockSpec((tko, tn), lambda i, j: (i, j))
        else:
            shards = tn // shard_cols
            assert tn % shard_cols == 0

            def body(a_ref, b_ref, o_ref):
                res = _dot(a_ref[...], b_ref[...], _TN).astype(o_ref.dtype)
                for p in range(shards):
                    o_ref[p] = res[:, p * shard_cols:(p + 1) * shard_cols]

            out_shape = S((n // shard_cols, kk, shard_cols), out_dtype)
            out_spec = pl.BlockSpec((shards, tko, shard_cols), lambda i, j: (j, i, 0))
        return _call(
            body, name, out_shape, (kk // tko, n // tn), [pl.BlockSpec((m, tko), lambda i, j: (0, i)), b_spec],
            out_spec, ("parallel", "parallel"), (a, b), comm=comm)

    planes, width = (a.shape[0], a.shape[2]) if a.ndim == 3 else (1, a.shape[1])
    m, kk = a.shape[-2], planes * width
    n = b.shape[1] if mode == "nn" else b.shape[0]
    dims = _NN if mode == "nn" else _NT

    if kk > 2048:
        tm = m // 4
        assert m % 4 == 0 and tm % 16 == 0

        def body(a_ref, b_ref, o_ref):
            if a.ndim == 2:
                res = _dot(a_ref[...], b_ref[...], dims)
            else:
                res = None
                for p in range(planes):
                    bp = b_ref[p * width:(p + 1) * width, :] if mode == "nn" else b_ref[:, p * width:(p + 1) * width]
                    part = _dot(a_ref[p], bp, dims)
                    res = part if res is None else res + part
            o_ref[...] = res.astype(o_ref.dtype)

        a_spec = (pl.BlockSpec((planes, tm, width), lambda i: (0, i, 0)) if a.ndim == 3
                  else pl.BlockSpec((tm, kk), lambda i: (i, 0)))
        return _call(
            body, name, S((m, n), out_dtype), (m // tm,),
            [a_spec, pl.BlockSpec(b.shape, lambda i: (0, 0), pipeline_mode=pl.Buffered(1))],
            pl.BlockSpec((tm, n), lambda i: (i, 0)), ("parallel",), (a, b), comm=comm)

    tn = _pick(n, 512)

    def body(a_ref, b_ref, o_ref):
        o_ref[...] = _dot(a_ref[...], b_ref[...], dims).astype(o_ref.dtype)

    b_spec = (pl.BlockSpec((kk, tn), lambda j: (0, j)) if mode == "nn" else pl.BlockSpec((tn, kk), lambda j: (j, 0)))
    return _call(
        body, name, S((m, n), out_dtype), (n // tn,), [pl.BlockSpec((m, kk), lambda j: (0, 0)), b_spec],
        pl.BlockSpec((m, tn), lambda j: (0, j)), ("parallel",), (a, b), comm=comm)


def _rms(x, w):
    return x * lax.rsqrt(jnp.mean(x * x, axis=-1, keepdims=True) + RMS_EPS) * w


def _row_tile(rows, d):
    return rows // 4 if d <= 1024 and (rows // 4) % 16 == 0 else rows // 8


def _embed_norm(meta, x, w, rows, name, comm=None):
    n_meta, d = meta.shape
    n_x = x.shape[0]
    last = rows // T - 1
    assert n_meta % 8 == 0 and n_meta < T and n_meta + n_x == last * T + n_meta and last * T >= n_x

    def body(m_ref, x_ref, w_ref, h_ref, hn_ref):
        i = pl.program_id(0)

        @pl.when(i == 0)
        def _():
            h_ref[0:n_meta, :] = m_ref[...]
            h_ref[n_meta:T, :] = x_ref[0:T - n_meta, :]

        @pl.when((i > 0) & (i < last))
        def _():
            h_ref[...] = x_ref[pl.ds(pl.multiple_of(i * T - n_meta, 8), T), :]

        @pl.when(i == last)
        def _():
            h_ref[0:n_meta, :] = x_ref[n_x - n_meta:n_x, :]
            h_ref[n_meta:T, :] = jnp.zeros((T - n_meta, d), F32)

        hn_ref[...] = _rms(h_ref[...], w_ref[...]).astype(hn_ref.dtype)

    row = pl.BlockSpec((T, d), lambda i: (i, 0))
    return _call(body, name, [S((rows, d), F32), S((rows, d), BF16)], (rows // T,),
                 [pl.BlockSpec((n_meta, d), lambda i: (0, 0)), pl.BlockSpec((n_x, d), lambda i: (0, 0)),
                  pl.BlockSpec((1, d), lambda i: (0, 0))], [row, row], ("parallel",), (meta, x, w), comm=comm)


def _resid_norm(h, br, w_post, next_ws, name):
    rows, d = h.shape
    tr = _row_tile(rows, d)
    has_br = br is not None
    nw = len(next_ws)

    def body(*refs):
        h_ref = refs[0]
        pos = 1
        x = h_ref[...]
        if has_br:
            x = x + _rms(refs[1][...], refs[2][...])
            pos = 3
        w_refs = refs[pos:pos + nw]
        outs = refs[pos + nw:]
        if has_br:
            outs[0][...] = x
            outs = outs[1:]
        for w_ref, o_ref in zip(w_refs, outs):
            o_ref[...] = _rms(x, w_ref[...]).astype(o_ref.dtype)

    row = pl.BlockSpec((tr, d), lambda i: (i, 0))
    vec = pl.BlockSpec((1, d), lambda i: (0, 0))
    ins = [h] + ([br, w_post] if has_br else []) + list(next_ws)
    in_specs = [row] + ([row, vec] if has_br else []) + [vec] * nw
    out_shape = ([S((rows, d), F32)] if has_br else []) + [S((rows, d), BF16)] * nw
    res = pl.pallas_call(body, name=name, out_shape=out_shape, grid=(rows // tr,), in_specs=in_specs,
                         out_specs=[row] * len(out_shape), compiler_params=_cp(("parallel",)))(*ins)
    if has_br:
        return res[0], list(res[1:])
    return h, list(res)


def _rms_bwd(xv, w, dyv):
    r = lax.rsqrt(jnp.mean(xv * xv, axis=-1, keepdims=True) + RMS_EPS)
    wdy = dyv * w
    dx = r * wdy - xv * (r * r * r) * jnp.mean(xv * wdy, axis=-1, keepdims=True)
    return dx, jnp.sum(dyv * xv * r, axis=0, keepdims=True)


def _norm_bwd(x, w, dy, add, out_dtype, name, comm=None, then=None, split_rows=None):
    rows, d = x.shape
    tr = _row_tile(rows, d)
    has_add = add is not None
    n_in = 3 + has_add + (2 if then is not None else 0)
    if split_rows is not None:
        last, tail = _real_rows(rows, tr, split_rows)

    def body(*refs):
        x_ref, w_ref, dy_ref = refs[:3]
        outs = refs[n_in:]
        dx, dw = _rms_bwd(x_ref[...], w_ref[...], dy_ref[...].astype(F32))
        if has_add:
            dx = dx + refs[3][...]
        if split_rows is None:
            outs[0][...] = dx.astype(outs[0].dtype)
        else:
            i = pl.program_id(0)
            gm_ref, gx_ref = outs[0], outs[1]
            outs = outs[1:]

            @pl.when(i == 0)
            def _():
                gm_ref[...] = dx[0:N_META, :]
                gx_ref[0:tr - N_META, :] = dx[N_META:tr, :]

            @pl.when((i > 0) & (i < last))
            def _():
                gx_ref[pl.ds(pl.multiple_of(i * tr - N_META, 8), tr), :] = dx

            @pl.when(i == last)
            def _():
                gx_ref[split_rows - tail:split_rows, :] = dx[0:tail, :]
        first = pl.program_id(0) == 0

        @pl.when(first)
        def _():
            outs[1][...] = jnp.zeros_like(outs[1])

        outs[1][...] += dw
        if then is not None:
            dx2, dw2 = _rms_bwd(refs[n_in - 2][...], refs[n_in - 1][...], dx)
            outs[2][...] = dx2.astype(outs[2].dtype)

            @pl.when(first)
            def _():
                outs[3][...] = jnp.zeros_like(outs[3])

            outs[3][...] += dw2

    row = pl.BlockSpec((tr, d), lambda i: (i, 0))
    vec = pl.BlockSpec((1, d), lambda i: (0, 0))
    ins = [x, w, dy] + ([add] if has_add else []) + (list(then) if then is not None else [])
    in_specs = [row, vec, row] + ([row] if has_add else []) + ([row, vec] if then is not None else [])
    out_shape = [S((rows, d), out_dtype), S((1, d), F32)] + ([S((rows, d), BF16), S((1, d), F32)] if then is not None else [])
    out_specs = [row, vec] * (len(out_shape) // 2)
    if split_rows is not None:
        out_shape = [S((N_META, d), F32), S((split_rows, d), F32)] + out_shape[1:]
        out_specs = [pl.BlockSpec((N_META, d), lambda i: (0, 0)), pl.BlockSpec((split_rows, d), lambda i: (0, 0))] + out_specs[1:]
    return _call(body, name, out_shape, (rows // tr,), in_specs, out_specs, ("arbitrary",), ins, comm=comm)


def _real_rows(rows, tr, n_x):
    last = (N_META + n_x - 1) // tr
    tail = N_META + n_x - last * tr
    assert last == rows // tr - 1 and N_META % 8 == 0 and tail % 8 == 0 and N_META < tr
    return last, tail


def _final_loss(h, br, w_post, target, name):
    rows, d = h.shape
    tr = _row_tile(rows, d)
    n_x = target.shape[0]
    last, tail = _real_rows(rows, tr, n_x)

    def body(h_ref, br_ref, w_ref, t_ref, dh_ref, loss_ref, dbr_ref, dw_ref, tbuf):
        i = pl.program_id(0)

        @pl.when(i == 0)
        def _():
            tbuf[0:N_META, :] = jnp.zeros((N_META, d), F32)
            tbuf[N_META:tr, :] = t_ref[0:tr - N_META, :]

        @pl.when((i > 0) & (i < last))
        def _():
            tbuf[...] = t_ref[pl.ds(pl.multiple_of(i * tr - N_META, 8), tr), :]

        @pl.when(i == last)
        def _():
            tbuf[0:tail, :] = t_ref[n_x - tail:n_x, :]
            if tail < tr:
                tbuf[tail:tr, :] = jnp.zeros((tr - tail, d), F32)

        brv, wv = br_ref[...], w_ref[...]
        y = h_ref[...] + _rms(brv, wv)
        r = i * tr + lax.broadcasted_iota(jnp.int32, (tr, 1), 0)
        real = (r >= N_META) & (r < N_META + SEQ)
        diff = jnp.where(real, y - tbuf[...], 0.0)
        dh = diff * (1.0 / d)
        dh_ref[...] = dh
        dbr, dw = _rms_bwd(brv, wv, dh)
        dbr_ref[...] = dbr.astype(dbr_ref.dtype)

        @pl.when(i == 0)
        def _():
            loss_ref[...] = jnp.zeros_like(loss_ref)
            dw_ref[...] = jnp.zeros_like(dw_ref)

        loss_ref[...] += jnp.sum(diff * diff) * (0.5 / d)
        dw_ref[...] += dw

    row = pl.BlockSpec((tr, d), lambda i: (i, 0))
    vec = pl.BlockSpec((1, d), lambda i: (0, 0))
    return pl.pallas_call(body, name=name,
                          out_shape=[S((rows, d), F32), S((1, LANE), F32), S((rows, d), BF16), S((1, d), F32)],
                          grid=(rows // tr,), in_specs=[row, row, vec, pl.BlockSpec((n_x, d), lambda i: (0, 0))],
                          out_specs=[row, pl.BlockSpec((1, LANE), lambda i: (0, 0)), row, vec],
                          scratch_shapes=[pltpu.VMEM((tr, d), F32)],
                          compiler_params=_cp(("arbitrary",)))(h, br, w_post, target)


def _gatenorm_fwd(y, zx, w, name, comm=None):
    rows, d = y.shape
    tr = _row_tile(rows, d)

    def body(y_ref, z_ref, w_ref, o_ref):
        z = z_ref[...]
        o_ref[...] = _rms(y_ref[...] * z * _sigmoid(z), w_ref[...]).astype(o_ref.dtype)

    row = pl.BlockSpec((tr, d), lambda i: (i, 0))
    return _call(body, name, S((rows, d), BF16), (rows // tr,), [row, row, pl.BlockSpec((1, d), lambda i: (0, 0))],
                 row, ("parallel",), (y, zx, w), comm=comm)


def _gatenorm_bwd(y, zx, w, dyn, name, comm=None):
    rows, d = y.shape
    tr = _row_tile(rows, d)

    def body(y_ref, z_ref, w_ref, dyn_ref, dy_ref, dz_ref, dw_ref):
        yv, z = y_ref[...], z_ref[...]
        sg = _sigmoid(z)
        sz = z * sg
        g = yv * sz
        r = lax.rsqrt(jnp.mean(g * g, axis=-1, keepdims=True) + RMS_EPS)
        dyn_v = dyn_ref[...]
        wdy = dyn_v * w_ref[...]
        dg = r * wdy - g * (r * r * r) * jnp.mean(g * wdy, axis=-1, keepdims=True)
        dy_ref[...] = dg * sz
        dz_ref[...] = (dg * yv * sg * (1.0 + z * (1.0 - sg))).astype(dz_ref.dtype)

        @pl.when(pl.program_id(0) == 0)
        def _():
            dw_ref[...] = jnp.zeros_like(dw_ref)

        dw_ref[...] += jnp.sum(dyn_v * g * r, axis=0, keepdims=True)

    row = pl.BlockSpec((tr, d), lambda i: (i, 0))
    vec = pl.BlockSpec((1, d), lambda i: (0, 0))
    return _call(body, name, [S((rows, d), F32), S((rows, zx.shape[1]), BF16), S((1, d), F32)], (rows // tr,),
                 [row, row, vec, row], [row, row, vec], ("arbitrary",), (y, zx, w, dyn), comm=comm)


def _shift_down(x, s, rows_iota):
    if s == 0:
        return x
    return jnp.where(rows_iota >= s, pltpu.roll(x, s, 0), 0.0)


def _shift_up(x, s, rows_iota):
    if s == 0:
        return x
    rows = x.shape[0]
    return jnp.where(rows_iota < rows - s, pltpu.roll(x, rows - s, 0), 0.0)


def _r16(v):
    return v.astype(BF16).astype(F32)


def _conv_taps(x, taps, rows_iota):
    x = _r16(x)
    return [_shift_down(x, taps - 1 - k, rows_iota) for k in range(taps)]


def _conv(x, w_ref, b_ref, taps, rows_iota, shifted=None):
    shifted = _conv_taps(x, taps, rows_iota) if shifted is None else shifted
    acc = jnp.zeros_like(shifted[0])
    for k in range(taps):
        acc = acc + _r16(w_ref[k:k + 1, :]) * shifted[k]
    return acc + b_ref[...]


def _conv_bwd(shifted, du, w_ref, dw_ref, db_ref, taps, rows_iota):
    db_ref[...] = jnp.sum(du, axis=0, keepdims=True)
    du = _r16(du)
    dx = jnp.zeros_like(du)
    for k in range(taps):
        dx = dx + _r16(w_ref[k:k + 1, :]) * _shift_up(du, taps - 1 - k, rows_iota)
        dw_ref[k:k + 1, :] = jnp.sum(du * shifted[k], axis=0, keepdims=True)
    return dx


def _conv_silu_fwd(zx, w, b, name, comm=None):
    rows = zx.shape[0]
    cb = 512
    off = D_INNER // cb

    def body(x_ref, w_ref, b_ref, o_ref):
        it = lax.broadcasted_iota(jnp.int32, (rows, 1), 0)
        u = _conv(x_ref[...], w_ref, b_ref, SSM_CONV, it)
        o_ref[...] = u * _sigmoid(u)

    return _call(
        body, name, S((rows, D_XBC), F32), (D_XBC // cb,),
        [pl.BlockSpec((rows, cb), lambda j: (0, off + j)), pl.BlockSpec((SSM_CONV, cb), lambda j: (0, j)),
         pl.BlockSpec((1, cb), lambda j: (0, j))],
        pl.BlockSpec((rows, cb), lambda j: (0, j)), ("parallel",), (zx, w, b), comm=comm)


def _conv_silu_bwd(zx, dxs, dbm, dcm, w, b, dzx, name, comm=None):
    rows = zx.shape[0]
    cb = 256
    off = D_INNER // cb
    nx, nbc = D_INNER // cb, D_BC // cb

    def body(x_ref, dx_in, db_in, dc_in, w_ref, b_ref, dzx_in, dx_ref, dw_ref, db_ref, dbuf):
        del dzx_in
        j = pl.program_id(0)
        for cond, src in ((j < nx, dx_in), ((j >= nx) & (j < nx + nbc), db_in), (j >= nx + nbc, dc_in)):
            @pl.when(cond)
            def _(src=src):
                dbuf[...] = src[...]
        it = lax.broadcasted_iota(jnp.int32, (rows, 1), 0)
        xs = _conv_taps(x_ref[...], SSM_CONV, it)
        u = _conv(None, w_ref, b_ref, SSM_CONV, it, xs)
        sg = _sigmoid(u)
        du = dbuf[...] * sg * (1.0 + u * (1.0 - sg))
        dx_ref[...] = _conv_bwd(xs, du, w_ref, dw_ref, db_ref, SSM_CONV, it).astype(dx_ref.dtype)

    def part(first, count):
        return pl.BlockSpec((rows, cb), lambda j: (0, jnp.clip(j - first, 0, count - 1)))

    col = pl.BlockSpec((rows, cb), lambda j: (0, j))
    wsp = pl.BlockSpec((SSM_CONV, cb), lambda j: (0, j))
    bsp = pl.BlockSpec((1, cb), lambda j: (0, j))
    xbc_cols = pl.BlockSpec((rows, cb), lambda j: (0, off + j))
    return _call(
        body, name, [S(dzx.shape, dzx.dtype), S((SSM_CONV, D_XBC), F32), S((1, D_XBC), F32)], (D_XBC // cb,),
        [xbc_cols, part(0, nx), part(nx, nbc), part(nx + nbc, nbc), wsp, bsp, _ANY],
        [xbc_cols, wsp, bsp], ("arbitrary",), (zx, dxs, dbm, dcm, w, b, dzx), scratch=[pltpu.VMEM((rows, cb), F32)],
        comm=comm, aliases={6: 0})


def _ffn_act_fwd(u, w, b, name, comm=None):
    rows = u.shape[0]
    cb = 256
    nb = D_FF // cb

    def body(g_ref, v_ref, wg_ref, wv_ref, bg_ref, bv_ref, o_ref):
        it = lax.broadcasted_iota(jnp.int32, (rows, 1), 0)
        g = _conv(g_ref[...], wg_ref, bg_ref, FFN_CONV, it)
        v = _conv(v_ref[...], wv_ref, bv_ref, FFN_CONV, it)
        o_ref[...] = (g * _sigmoid(g) * v).astype(o_ref.dtype)

    def sp(r, shift):
        return pl.BlockSpec((r, cb), lambda j: (0, shift + j))

    return _call(
        body, name, S((rows, D_FF), BF16), (nb,),
        [sp(rows, 0), sp(rows, nb), sp(FFN_CONV, 0), sp(FFN_CONV, nb), sp(1, 0), sp(1, nb)],
        sp(rows, 0), ("parallel",), (u, u, w, w, b, b), comm=comm)


def _ffn_act_bwd(u, dact, w, b, name, comm=None):
    rows = u.shape[0]
    cb = 256
    nb = D_FF // cb

    def body(g_ref, v_ref, d_ref, wg_ref, wv_ref, bg_ref, bv_ref, du_ref, dw_ref, db_ref):
        it = lax.broadcasted_iota(jnp.int32, (rows, 1), 0)
        xg, xv = _conv_taps(g_ref[...], FFN_CONV, it), _conv_taps(v_ref[...], FFN_CONV, it)
        g = _conv(None, wg_ref, bg_ref, FFN_CONV, it, xg)
        v = _conv(None, wv_ref, bv_ref, FFN_CONV, it, xv)
        sg = _sigmoid(g)
        d = d_ref[...]
        dgate = d * v * sg * (1.0 + g * (1.0 - sg))
        dval = d * g * sg
        du_ref[0] = _conv_bwd(xg, dgate, wg_ref, dw_ref.at[0], db_ref.at[0], FFN_CONV, it).astype(du_ref.dtype)
        du_ref[1] = _conv_bwd(xv, dval, wv_ref, dw_ref.at[1], db_ref.at[1], FFN_CONV, it).astype(du_ref.dtype)

    def sp(r, shift):
        return pl.BlockSpec((r, cb), lambda j: (0, shift + j))

    def both(r):
        return pl.BlockSpec((2, r, cb), lambda j: (0, 0, j))

    return _call(
        body, name, [S((2, rows, D_FF), BF16), S((2, FFN_CONV, D_FF), F32), S((2, 1, D_FF), F32)], (nb,),
        [sp(rows, 0), sp(rows, nb), sp(rows, 0), sp(FFN_CONV, 0), sp(FFN_CONV, nb), sp(1, 0), sp(1, nb)],
        [both(rows), both(FFN_CONV), both(1)], ("parallel",), (u, u, dact, w, w, b, b), comm=comm)


def _ssd_consts(dtp_ref, bias_ref, alog_ref, hg):
    lane = lax.broadcasted_iota(jnp.int32, (1, LANE), 1)
    pre = dtp_ref[...] + bias_ref[...]
    dt = _softplus(pre)
    a_row = jnp.where(lane < hg, -jnp.exp(alog_ref[...]), 0.0)
    ri = lax.broadcasted_iota(jnp.int32, (T, T), 0)
    ci = lax.broadcasted_iota(jnp.int32, (T, T), 1)
    cs = _dot_hi((ri >= ci).astype(F32), dt * a_row)
    return pre, dt, a_row, cs, ri, ci, lane


def _head_rows(src, hg):
    return jnp.concatenate([jnp.broadcast_to(src[k:k + 1, :], (HEAD_P, src.shape[1])) for k in range(hg)], axis=0)


def _ssd_fwd(xbc, zx, bias, alog, dsk, name, comm=None):
    rows = xbc.shape[0]
    nc = rows // T
    hg = SSM_HEADS // SSM_GROUPS
    gw = hg * HEAD_P
    xoff, boff, coff = 0, D_INNER // D_STATE, (D_INNER + D_BC) // D_STATE
    dtoff = (D_INNER + D_XBC) // LANE

    def body(x_ref, b_ref, c_ref, dtp_ref, bias_ref, alog_ref, dsk_ref, y_ref, hst_ref, hs):
        c = pl.program_id(1)

        @pl.when(c == 0)
        def _():
            hs[...] = jnp.zeros_like(hs)

        _, dt, _, cs, ri, ci, _ = _ssd_consts(dtp_ref, bias_ref, alog_ref, hg)
        cst, dtt = cs.T, dt.T
        xt = x_ref[...].T
        bb, cbf = b_ref[...].astype(BF16), c_ref[...].astype(BF16)
        gt = _dot(bb, cbf, _NT)
        causal_t = ci >= ri
        dskv = dsk_ref[...]
        hall = hs[...]
        hst_ref[0, 0] = hall
        cs8 = cst[0:8, :]
        cl8 = cs8[:, T - 1:T]
        xdt = xt * _head_rows(dtt, hg)
        yo = _head_rows(jnp.exp(cs8), hg) * _dot(hall.astype(BF16), cbf, _NT)
        st = _dot((xdt * _head_rows(jnp.exp(cl8 - cs8), hg)).astype(BF16), bb)
        hs[...] = _head_rows(jnp.exp(cl8), hg) * hall + st
        yds = []
        for k in range(hg):
            sl = slice(k * HEAD_P, (k + 1) * HEAD_P)
            lt = jnp.exp(jnp.where(causal_t, cst[k:k + 1, :] - cs[:, k:k + 1], NEG))
            yds.append(_dot(xdt[sl, :].astype(BF16), (gt * lt).astype(BF16)))
        dsk_r = jnp.concatenate([jnp.broadcast_to(dskv[:, k:k + 1], (HEAD_P, 1)) for k in range(hg)], axis=0)
        y_ref[...] = (jnp.concatenate(yds, axis=0) + yo + dsk_r * xt).T

    vec = pl.BlockSpec((1, LANE), lambda g, c: (0, g))
    return _call(
        body, name, [S((rows, D_INNER), F32), S((nc, SSM_GROUPS, gw, D_STATE), F32)], (SSM_GROUPS, nc),
        [pl.BlockSpec((T, gw), lambda g, c: (c, xoff + g)),
         pl.BlockSpec((T, D_STATE), lambda g, c: (c, boff + g)),
         pl.BlockSpec((T, D_STATE), lambda g, c: (c, coff + g)),
         pl.BlockSpec((T, LANE), lambda g, c: (c, dtoff + g)), vec, vec, vec],
        [pl.BlockSpec((T, gw), lambda g, c: (c, g)), pl.BlockSpec((1, 1, gw, D_STATE), lambda g, c: (c, g, 0, 0))],
        ("parallel", "arbitrary"), (xbc, xbc, xbc, zx, bias, alog, dsk),
        scratch=[pltpu.VMEM((gw, D_STATE), F32)], comm=comm)


def _ssd_bwd(xbc, zx, bias, alog, dsk, dy, hst, dzx, name, comm=None):
    rows = xbc.shape[0]
    nc = rows // T
    hg = SSM_HEADS // SSM_GROUPS
    gw = hg * HEAD_P
    boff, coff = D_INNER // D_STATE, (D_INNER + D_BC) // D_STATE
    dtoff = (D_INNER + D_XBC) // LANE

    def body(x_ref, b_ref, c_ref, dtp_ref, bias_ref, alog_ref, dsk_ref, dy_ref, hst_ref, dzx_in,
             dx_ref, db_ref, dc_ref, ddtp_ref, dalog_ref, ddsk_ref, dbias_ref, dhs):
        del dzx_in
        step = pl.program_id(1)

        @pl.when(step == 0)
        def _():
            dhs[...] = jnp.zeros_like(dhs)
            dalog_ref[...] = jnp.zeros_like(dalog_ref)
            ddsk_ref[...] = jnp.zeros_like(ddsk_ref)
            dbias_ref[...] = jnp.zeros_like(dbias_ref)

        pre, dt, a_row, cs, ri, ci, lane = _ssd_consts(dtp_ref, bias_ref, alog_ref, hg)
        cst, dtt = cs.T, dt.T
        xt, dyt = x_ref[...].T, dy_ref[...].T
        bb, cbf = b_ref[...].astype(BF16), c_ref[...].astype(BF16)
        gt = _dot(bb, cbf, _NT)
        causal_t = ci >= ri
        dskv = dsk_ref[...]
        hall, dhall = hst_ref[0, 0], dhs[...]
        head_row = lax.broadcasted_iota(jnp.int32, (T, 1), 0)
        last_l = lax.broadcasted_iota(jnp.int32, (1, T), 1) == T - 1
        cs8, dt8 = cst[0:8, :], dtt[0:8, :]
        cl8 = cs8[:, T - 1:T]
        e8, wdec8 = jnp.exp(cs8), jnp.exp(cl8 - cs8)
        w8 = wdec8 * dt8
        dt_r, e_r, w_r, ecl_r = _head_rows(dt8, hg), _head_rows(e8, hg), _head_rows(w8, hg), _head_rows(jnp.exp(cl8), hg)
        dsk_r = jnp.concatenate([jnp.broadcast_to(dskv[:, k:k + 1], (HEAD_P, 1)) for k in range(hg)], axis=0)
        hb, dhb = hall.astype(BF16), dhall.astype(BF16)
        xdt = xt * dt_r
        dye = (dyt * e_r).astype(BF16)
        rt = _dot(dhb, bb, _NT)
        yo = e_r * _dot(hb, cbf, _NT)
        dhs[...] = ecl_r * dhall + _dot(dye, cbf)
        dc_acc = _dot(dye, hb, _TN)
        db_acc = _dot((xt * w_r).astype(BF16), dhb, _TN)
        rtx, dyyo, hdh, dyx = rt * xt, dyt * yo, dhall * hall, dyt * xt
        dgt = jnp.zeros((T, T), F32)
        ddt_rows = jnp.zeros((T, T), F32)
        dcs_rows = jnp.zeros((T, T), F32)
        qrow_cols = jnp.zeros((T, LANE), F32)
        ddsk_acc = jnp.zeros((1, LANE), F32)
        dxdts = []
        for k in range(hg):
            sl = slice(k * HEAD_P, (k + 1) * HEAD_P)
            lt = jnp.exp(jnp.where(causal_t, cst[k:k + 1, :] - cs[:, k:k + 1], NEG))
            mpt = gt * lt
            dyb = dyt[sl, :].astype(BF16)
            dxdt = _dot(dyb, mpt.astype(BF16), _NT)
            dmt = _dot(xdt[sl, :].astype(BF16), dyb, _TN)
            dgt = dgt + dmt * lt
            q = dmt * mpt
            q_rows = jnp.sum(q, axis=1, keepdims=True)
            q_cols = jnp.sum(q, axis=0, keepdims=True)
            dxdts.append(dxdt)
            xz = jnp.sum(xt[sl, :] * dxdt, axis=0, keepdims=True)
            dw = jnp.sum(rtx[sl, :], axis=0, keepdims=True)
            wk, wdeck = w8[k:k + 1, :], wdec8[k:k + 1, :]
            dcl = jnp.exp(cl8[k:k + 1, :]) * jnp.sum(hdh[sl, :]) + jnp.sum(dw * wk)
            dcs_r = jnp.sum(dyyo[sl, :], axis=0, keepdims=True) + q_cols - dw * wk + jnp.where(last_l, dcl, 0.0)
            onehot = (lane == k).astype(F32)
            ddt_rows = ddt_rows + jnp.where(head_row == k, xz + dw * wdeck, 0.0)
            dcs_rows = dcs_rows + jnp.where(head_row == k, dcs_r, 0.0)
            qrow_cols = qrow_cols + q_rows * onehot
            ddsk_acc = ddsk_acc + jnp.sum(dyx[sl, :]) * onehot
        dx_ref[...] = (dt_r * jnp.concatenate(dxdts, axis=0) + dsk_r * dyt + rt * w_r).T
        dc_ref[...] = _dot(dgt.T.astype(BF16), bb) + dc_acc
        db_ref[...] = _dot(dgt.astype(BF16), cbf) + db_acc
        da = _dot_hi((ci >= ri).astype(F32), dcs_rows.T - qrow_cols)
        ddtp = (ddt_rows.T + da * a_row) * _sigmoid(pre)
        ddtp = jnp.where(lane < hg, ddtp, 0.0)
        ddtp_ref[...] = ddtp.astype(ddtp_ref.dtype)
        dbias_ref[...] += jnp.sum(ddtp, axis=0, keepdims=True)
        dalog_ref[...] += jnp.sum(da * dt, axis=0, keepdims=True) * a_row
        ddsk_ref[...] += ddsk_acc

    def rc(c):
        return nc - 1 - c

    vec = pl.BlockSpec((1, LANE), lambda g, c: (0, g))
    xsp = pl.BlockSpec((T, gw), lambda g, c: (rc(c), g))
    return _call(
        body, name,
        [S((rows, D_INNER), F32), S((rows, D_BC), F32), S((rows, D_BC), F32),
         S(dzx.shape, dzx.dtype), S((1, SSM_GROUPS * LANE), F32),
         S((1, SSM_GROUPS * LANE), F32), S((1, SSM_GROUPS * LANE), F32)],
        (SSM_GROUPS, nc),
        [xsp,
         pl.BlockSpec((T, D_STATE), lambda g, c: (rc(c), boff + g)),
         pl.BlockSpec((T, D_STATE), lambda g, c: (rc(c), coff + g)),
         pl.BlockSpec((T, LANE), lambda g, c: (rc(c), dtoff + g)), vec, vec, vec,
         xsp, pl.BlockSpec((1, 1, gw, D_STATE), lambda g, c: (rc(c), g, 0, 0)), _ANY],
        [xsp,
         pl.BlockSpec((T, D_STATE), lambda g, c: (rc(c), g)),
         pl.BlockSpec((T, D_STATE), lambda g, c: (rc(c), g)),
         pl.BlockSpec((T, LANE), lambda g, c: (rc(c), dtoff + g)), vec, vec, vec],
        ("parallel", "arbitrary"), (xbc, xbc, xbc, zx, bias, alog, dsk, dy, hst, dzx),
        scratch=[pltpu.VMEM((gw, D_STATE), F32)], comm=comm, aliases={9: 3})


def _attn_tiles(kv_ref, j):
    prev = jnp.maximum(j - 1, 0)
    meta = kv_ref[0:T, :]
    prv = kv_ref[pl.ds(pl.multiple_of(prev * T, T), T), :]
    cur = kv_ref[pl.ds(pl.multiple_of(j * T, T), T), :]
    return jnp.concatenate([meta, prv, cur], axis=0)


def _attn_mask(j):
    r = j * T + lax.broadcasted_iota(jnp.int32, (3 * T, T), 1)
    row = lax.broadcasted_iota(jnp.int32, (3 * T, T), 0)
    t0, t1 = row < T, row < 2 * T
    s = jnp.where(t0, row, (j - 2) * T + row)
    ok = (s <= r) & ((s < N_META) | (s > r - WINDOW))
    use = (t0 & (j >= 2) & (row < N_META)) | (jnp.logical_not(t0) & t1 & (j >= 1)) | jnp.logical_not(t1)
    return ok & use


def _attn_fwd(q, kv, sinks, name, comm=None):
    rows = q.shape[0]
    scale = 1.0 / math.sqrt(ATTN_DH)
    qpk = N_Q_HEADS // N_KV_HEADS

    def body(q_ref, kv_ref, s_ref, o_ref, lse_ref):
        j = pl.program_id(0)
        kv3 = _attn_tiles(kv_ref, j).astype(BF16)
        mask = _attn_mask(j)
        qv = (q_ref[...] * scale).astype(BF16)
        sk = s_ref[...]
        lses = []
        for kh in range(N_KV_HEADS):
            k3 = kv3[:, kh * ATTN_DH:(kh + 1) * ATTN_DH]
            v3 = kv3[:, D_KV + kh * ATTN_DH:D_KV + (kh + 1) * ATTN_DH]
            for g in range(qpk):
                h = kh * qpk + g
                sink = sk[:, h:h + 1]
                sc = jnp.where(mask, _dot(k3, qv[:, h * ATTN_DH:(h + 1) * ATTN_DH], _NT), NEG)
                m = jnp.maximum(jnp.max(sc, axis=0, keepdims=True), sink)
                p = jnp.exp(sc - m)
                den = jnp.sum(p, axis=0, keepdims=True) + jnp.exp(sink - m)
                p = p * (1.0 / den)
                lses.append(m + jnp.log(den))
                o_ref[:, h * ATTN_DH:(h + 1) * ATTN_DH] = _dot(p.astype(BF16), v3, _TN).astype(o_ref.dtype)
        lse_ref[...] = jnp.concatenate(lses, axis=0)

    return _call(
        body, name, [S((rows, D_MODEL), BF16), S((N_Q_HEADS, rows), F32)], (rows // T,),
        [pl.BlockSpec((T, D_MODEL), lambda j: (j, 0)), pl.BlockSpec((rows, 2 * D_KV), lambda j: (0, 0)),
         pl.BlockSpec((1, N_Q_HEADS), lambda j: (0, 0))],
        [pl.BlockSpec((T, D_MODEL), lambda j: (j, 0)), pl.BlockSpec((N_Q_HEADS, T), lambda j: (0, j))],
        ("parallel",), (q, kv, sinks), comm=comm)


def _attn_bwd(q, kv, sinks, do, lse, name, comm=None):
    rows = q.shape[0]
    scale = 1.0 / math.sqrt(ATTN_DH)
    qpk = N_Q_HEADS // N_KV_HEADS

    def body(q_ref, kv_ref, s_ref, do_ref, lse_ref, dq_ref, dkv_ref, ds_ref):
        j = pl.program_id(0)

        @pl.when(j == 0)
        def _():
            dkv_ref[...] = jnp.zeros_like(dkv_ref)
            ds_ref[...] = jnp.zeros_like(ds_ref)

        kv3 = _attn_tiles(kv_ref, j).astype(BF16)
        mask = _attn_mask(j)
        qv = (q_ref[...] * scale).astype(BF16)
        dov = do_ref[...].astype(BF16)
        sk = s_ref[...]
        lsev = lse_ref[...]
        lane = lax.broadcasted_iota(jnp.int32, (1, LANE), 1)
        ds_acc = jnp.zeros((1, LANE), F32)
        prev = jnp.maximum(j - 1, 0)
        mask4 = jnp.concatenate([mask] * qpk, axis=1)
        dqts = []
        for kh in range(N_KV_HEADS):
            ksl = slice(kh * ATTN_DH, (kh + 1) * ATTN_DH)
            vsl = slice(D_KV + kh * ATTN_DH, D_KV + (kh + 1) * ATTN_DH)
            k3, v3 = kv3[:, ksl], kv3[:, vsl]
            heads = [kh * qpk + g for g in range(qpk)]
            q4 = jnp.concatenate([qv[:, h * ATTN_DH:(h + 1) * ATTN_DH] for h in heads], axis=0)
            do4 = jnp.concatenate([dov[:, h * ATTN_DH:(h + 1) * ATTN_DH] for h in heads], axis=0)
            lse4 = jnp.concatenate([lsev[h:h + 1, :] for h in heads], axis=1)
            sink4 = jnp.concatenate([jnp.broadcast_to(sk[:, h:h + 1], (1, T)) for h in heads], axis=1)
            p = jnp.exp(jnp.where(mask4, _dot(k3, q4, _NT), NEG) - lse4)
            ps = jnp.exp(sink4 - lse4)
            dp = _dot(v3, do4, _NT)
            delta = jnp.sum(p * dp, axis=0, keepdims=True)
            dsc = (p * (dp - delta)).astype(BF16)
            dq4 = _dot(k3.T, dsc) * scale
            dk3 = _dot(dsc, q4)
            dv3 = _dot(p.astype(BF16), do4)
            psd = ps * delta
            for g, h in enumerate(heads):
                dqts.append(dq4[:, g * T:(g + 1) * T])
                ds_acc = ds_acc - jnp.sum(psd[:, g * T:(g + 1) * T]) * (lane == h).astype(F32)
            for t, start in enumerate((0, pl.multiple_of(prev * T, T), pl.multiple_of(j * T, T))):
                rsl = pl.ds(start, T)
                dkv_ref[rsl, ksl] += dk3[t * T:(t + 1) * T, :]
                dkv_ref[rsl, vsl] += dv3[t * T:(t + 1) * T, :]
        ds_ref[...] += ds_acc
        dq_ref[...] = jnp.concatenate(dqts, axis=0).T.astype(dq_ref.dtype)

    blk = pl.BlockSpec((T, D_MODEL), lambda j: (j, 0))
    full = pl.BlockSpec((rows, 2 * D_KV), lambda j: (0, 0))
    return _call(
        body, name, [S((rows, D_MODEL), BF16), S((rows, 2 * D_KV), F32), S((1, LANE), F32)], (rows // T,),
        [blk, full, pl.BlockSpec((1, N_Q_HEADS), lambda j: (0, 0)), blk, pl.BlockSpec((N_Q_HEADS, T), lambda j: (0, j))],
        [blk, full, pl.BlockSpec((1, LANE), lambda j: (0, 0))], ("arbitrary",), (q, kv, sinks, do, lse), comm=comm)


BLOCK_BYTES = 1 << 20


def _div_tile(rows, cols, block_bytes=BLOCK_BYTES):
    cap = max(16, block_bytes // (4 * cols))
    best = None
    for t in range(16, min(rows, cap) + 1, 16):
        if rows % t == 0:
            best = t
    return best if best is not None else rows


def _adamw(parts, w, m, v, name, comm=None):
    layers, rows, cols = w.shape
    n = parts[0].shape[0]
    tr = _div_tile(rows, cols)
    tc = _pick(cols, 256) if tr == rows and rows * cols * 4 > 2 * BLOCK_BYTES else cols
    c1 = 1.0 / (1.0 - B1 ** STEP)
    c2 = 1.0 / (1.0 - B2 ** STEP)

    def body(*refs):
        p_refs = refs[:layers]
        w_ref, m_ref, v_ref, g_ref, d_ref, nm_ref, nv_ref = refs[layers:]
        layer = pl.program_id(0)
        for l in range(layers):
            @pl.when(layer == l)
            def _(p_ref=p_refs[l]):
                g = p_ref[0].astype(F32)
                for i in range(1, n):
                    g = g + p_ref[i].astype(F32)
                nm = B1 * m_ref[...] + (1.0 - B1) * g
                nv = B2 * v_ref[...] + (1.0 - B2) * (g * g)
                g_ref[...] = g
                nm_ref[...] = nm
                nv_ref[...] = nv
                d_ref[...] = -LR * ((nm * c1) / (jnp.sqrt(nv * c2) + EPS) + WD * w_ref[...])

    def part_spec(l):
        return pl.BlockSpec((n, tr, tc), lambda k, i, j: (0, jnp.where(k == l, i, 0), jnp.where(k == l, j, 0)))

    row = pl.BlockSpec((None, tr, tc), lambda k, i, j: (k, i, j))
    return _call(body, name, [S((layers, rows, cols), F32)] * 4, (layers, rows // tr, cols // tc),
                 [part_spec(l) for l in range(layers)] + [row, row, row], [row] * 4,
                 ("parallel", "parallel", "parallel"), (*parts, w, m, v), comm=comm)


def _sum_parts(parts, name):
    n, rows, cols = parts[0].shape
    nb = len(parts)
    tr = _div_tile(rows, cols, 2 * BLOCK_BYTES)

    def body(*refs):
        o_ref = refs[nb]
        blk = pl.program_id(0)
        for l in range(nb):
            @pl.when(blk == l)
            def _(p_ref=refs[l]):
                g = p_ref[0].astype(F32)
                for i in range(1, n):
                    g = g + p_ref[i].astype(F32)
                o_ref[...] = g

    def part_spec(l):
        return pl.BlockSpec((n, tr, cols), lambda k, i: (0, jnp.where(k == l, i, 0), 0))

    per = rows // tr
    return pl.pallas_call(body, name=name, out_shape=S((nb * rows, cols), F32), grid=(nb, per),
                          in_specs=[part_spec(l) for l in range(nb)],
                          out_specs=pl.BlockSpec((tr, cols), lambda k, i: (k * per + i, 0)),
                          compiler_params=_cp(("parallel", "parallel")))(*parts)


def _col_segments(ws, runs):
    segs = []
    for glo, mlo, n in runs:
        while n > 0:
            d, off = divmod(glo, ws)
            take = min(n, ws - off)
            segs.append((d, off, mlo, take))
            glo, mlo, n = glo + take, mlo + take, n - take
    return segs


def _assemble_cols(gs, width, segs, name):
    _, rows, ws = gs[0].shape
    nb = len(gs)
    rb = _div_tile(rows, width // 2, 4 * BLOCK_BYTES)
    per = rows // rb

    def body(*refs):
        o_ref = refs[nb]
        piece = pl.program_id(0)
        for l in range(nb):
            @pl.when(piece == l)
            def _(g_ref=refs[l]):
                o_ref[...] = jnp.zeros_like(o_ref)
                for d, off, mlo, n in segs:
                    o_ref[:, mlo:mlo + n] = g_ref[d, :, off:off + n]

    def piece_spec(l):
        return pl.BlockSpec((N_DEV, rb, ws), lambda k, i: (0, jnp.where(k == l, i, 0), 0))

    return pl.pallas_call(
        body, name=name, out_shape=S((nb * rows, width), gs[0].dtype), grid=(nb, per),
        in_specs=[piece_spec(l) for l in range(nb)],
        out_specs=pl.BlockSpec((rb, width), lambda k, i: (k * per + i, 0)),
        compiler_params=_cp(("parallel", "parallel")))(*gs)


def _scatter_cols(dw, ws, segs, name):
    rows, width = dw.shape
    rb = _div_tile(rows, width, 4 * BLOCK_BYTES)

    def body(w_ref, o_ref):
        for d, off, mlo, n in segs:
            o_ref[d, :, off:off + n] = w_ref[:, mlo:mlo + n].astype(o_ref.dtype)

    return pl.pallas_call(
        body, name=name, out_shape=S((N_DEV, rows, ws), BF16), grid=(rows // rb,),
        in_specs=[pl.BlockSpec((rb, width), lambda i: (i, 0))],
        out_specs=pl.BlockSpec((N_DEV, rb, ws), lambda i: (0, i, 0)), compiler_params=_cp(("parallel",)))(dw)


def _gather_comm(xs):
    n = len(xs)

    def setup(x_refs, out_refs, sems):
        send_sems, recv_sems, local_sems = sems
        mx, my, mc = lax.axis_index("x"), lax.axis_index("y"), lax.axis_index("c")
        me, sibling = (mx, my, mc), (mx, my, 1 - mc)
        chips = [(1 - mx, my), (mx, 1 - my), (1 - mx, 1 - my)]

        def blk(a, px, py, pc):
            return out_refs[a].at[4 * px + 2 * py + pc]

        def copy(a, k, block, to, src=None):
            return pltpu.make_async_remote_copy(
                src_ref=blk(a, *block) if src is None else src, dst_ref=blk(a, *block),
                send_sem=send_sems.at[a, k], recv_sem=recv_sems.at[a, k], device_id=to, device_id_type=_MESH)

        mine = [pltpu.make_async_copy(x_refs[a], blk(a, *me), local_sems.at[a]) for a in range(n)]
        own = []
        for a in range(n):
            own.append(copy(a, 0, me, sibling, src=x_refs[a]))
            own += [copy(a, 1 + i, me, (*chip, mc), src=x_refs[a]) for i, chip in enumerate(chips)]
        return me, sibling, chips, mc, copy, mine, own

    def first(x_refs, out_refs, sems):
        _, _, _, _, _, mine, own = setup(x_refs, out_refs, sems)
        for cp in mine + own:
            cp.start()

    def last(x_refs, out_refs, sems):
        me, sibling, chips, mc, copy, mine, own = setup(x_refs, out_refs, sems)
        passed = []
        for a in range(n):
            for i, chip in enumerate(chips):
                copy(a, 1 + i, (*chip, mc), me).wait_recv()
                passed.append(copy(a, 4 + i, (*chip, mc), sibling))
                passed[-1].start()
        for a in range(n):
            copy(a, 0, sibling, me).wait_recv()
            for i, chip in enumerate(chips):
                copy(a, 4 + i, (*chip, 1 - mc), me).wait_recv()
        for cp in own + passed:
            cp.wait_send()
        for cp in mine:
            cp.wait()

    return _Comm(list(xs), [S((N_DEV,) + x.shape, x.dtype) for x in xs],
                 [pltpu.SemaphoreType.DMA((n, 7)), pltpu.SemaphoreType.DMA((n, 7)), pltpu.SemaphoreType.DMA((n,))],
                 first, last)


def _swap_comm(gs):
    n = len(gs)

    def copies(g_refs, out_refs, sems):
        send_sems, recv_sems = sems
        mx, my, mc = lax.axis_index("x"), lax.axis_index("y"), lax.axis_index("c")
        return [pltpu.make_async_remote_copy(
            src_ref=g_refs[a].at[2 * k + 1 - mc], dst_ref=out_refs[a].at[k], send_sem=send_sems.at[a, k],
            recv_sem=recv_sems.at[a, k], device_id=(mx, my, 1 - mc), device_id_type=_MESH)
            for a in range(n) for k in range(4)]

    def first(g_refs, out_refs, sems):
        for cp in copies(g_refs, out_refs, sems):
            cp.start()

    def last(g_refs, out_refs, sems):
        for cp in copies(g_refs, out_refs, sems):
            cp.wait()

    return _Comm(list(gs), [S((4,) + g.shape[1:], g.dtype) for g in gs],
                 [pltpu.SemaphoreType.DMA((n, 4)), pltpu.SemaphoreType.DMA((n, 4))], first, last)


def _chips_comm(parts):
    n = len(parts)

    def copies(p_refs, out_refs, sems):
        send_sems, recv_sems, local_sems = sems
        mx, my, mc = lax.axis_index("x"), lax.axis_index("y"), lax.axis_index("c")
        mychip = 2 * mx + my
        chips = [(1 - mx, my), (mx, 1 - my), (1 - mx, 1 - my)]
        mine = [pltpu.make_async_copy(p_refs[a].at[mychip], out_refs[a].at[mychip], local_sems.at[a])
                for a in range(n)]
        return mine + [pltpu.make_async_remote_copy(
            src_ref=p_refs[a].at[2 * cx + cy], dst_ref=out_refs[a].at[mychip], send_sem=send_sems.at[a, i],
            recv_sem=recv_sems.at[a, i], device_id=(cx, cy, mc), device_id_type=_MESH)
            for a in range(n) for i, (cx, cy) in enumerate(chips)]

    def first(p_refs, out_refs, sems):
        for cp in copies(p_refs, out_refs, sems):
            cp.start()

    def last(p_refs, out_refs, sems):
        for cp in copies(p_refs, out_refs, sems):
            cp.wait()

    return _Comm(list(parts), [S(p.shape, p.dtype) for p in parts],
                 [pltpu.SemaphoreType.DMA((n, 3)), pltpu.SemaphoreType.DMA((n, 3)), pltpu.SemaphoreType.DMA((n,))],
                 first, last)


def _join_comms(comms):
    def split(refs, counts):
        out, p = [], 0
        for cnt in counts:
            out.append(refs[p:p + cnt])
            p += cnt
        return out

    ni = [len(c.ins) for c in comms]
    no = [len(c.out_shapes) for c in comms]
    ns = [len(c.scratch) for c in comms]

    def first(in_refs, out_refs, sems):
        for c, i, o, s in zip(comms, split(in_refs, ni), split(out_refs, no), split(sems, ns)):
            c.first(i, o, s)

    def last(in_refs, out_refs, sems):
        for c, i, o, s in zip(comms, split(in_refs, ni), split(out_refs, no), split(sems, ns)):
            c.last(i, o, s)

    return _Comm([x for c in comms for x in c.ins], [x for c in comms for x in c.out_shapes],
                 [x for c in comms for x in c.scratch], first, last)


def _add_pairs(mine, theirs, core, name):
    _, rows, cols = mine.shape
    tr = _div_tile(rows, cols // 2)

    def body(core_ref, a_ref, b_ref, o_ref):
        o_ref[...] = (a_ref[...].astype(F32) + b_ref[...].astype(F32)).astype(o_ref.dtype)

    return pl.pallas_call(
        body, name=name, out_shape=S((4, rows, cols), BF16),
        grid_spec=pltpu.PrefetchScalarGridSpec(
            num_scalar_prefetch=1, grid=(4, rows // tr),
            in_specs=[pl.BlockSpec((None, tr, cols), lambda k, i, c: (2 * k + c[0], i, 0)),
                      pl.BlockSpec((None, tr, cols), lambda k, i, c: (k, i, 0))],
            out_specs=pl.BlockSpec((None, tr, cols), lambda k, i, c: (k, i, 0))),
        compiler_params=_cp(("parallel", "parallel")))(core, mine, theirs)


def _run_comm(comm, name):
    ci, co = len(comm.ins), len(comm.out_shapes)

    def body(*refs):
        comm.first(refs[:ci], refs[ci:ci + co], refs[ci + co:])
        comm.last(refs[:ci], refs[ci:ci + co], refs[ci + co:])

    return pl.pallas_call(body, name=name, out_shape=list(comm.out_shapes), in_specs=[_HBM] * ci,
                          out_specs=[_HBM] * co, scratch_shapes=list(comm.scratch))(*comm.ins)


def _flat_rows(n_elems, mult):
    rows = -(-n_elems // LANE)
    return -(-rows // mult) * mult


def _pack(arrs, lead, mult, dtype):
    lead_shape = arrs[0].shape[:lead]
    flat = jnp.concatenate([a.astype(dtype).reshape(lead_shape + (-1,)) for a in arrs], axis=-1)
    n = flat.shape[-1]
    rows = _flat_rows(n, mult)
    flat = jnp.pad(flat, [(0, 0)] * lead + [(0, rows * LANE - n)])
    return flat.reshape(lead_shape + (rows, LANE))


def _unpack(flat, lead, shapes):
    lead_shape = flat.shape[:lead]
    flat = flat.reshape(lead_shape + (-1,))
    out, off = [], 0
    for shp in shapes:
        n = math.prod(shp)
        out.append(flat[..., off:off + n].reshape(lead_shape + tuple(shp)))
        off += n
    return out


def _split8(full, ax, n):
    shp = full.shape
    return jnp.moveaxis(full.reshape(shp[:ax] + (N_DEV, n) + shp[ax + 1:]), ax, 0)


def _join8(g, ax):
    shp = g.shape[1:]
    return jnp.moveaxis(g, 0, ax).reshape(shp[:ax] + (N_DEV * shp[ax],) + shp[ax + 1:])


def _group_lanes(v, hg):
    v = v.reshape(SSM_GROUPS, hg)
    return jnp.pad(v, ((0, 0), (0, LANE - hg))).reshape(1, SSM_GROUPS * LANE)


def _ungroup_lanes(v, hg):
    return v.reshape(SSM_GROUPS, LANE)[:, :hg].reshape(1, SSM_GROUPS * hg)


def kernel(x, meta_tokens, a_norm_pre, a_w_in, a_conv_w, a_conv_b, a_dt_bias, a_a_log, a_d_skip, a_gate_norm, a_w_out, a_norm_post, kv_norm, w_kv, b_norm_pre, b_w_q, b_sinks, b_w_o, b_norm_post, f_norm_pre, f_w_up, f_conv_w, f_conv_b, f_w_down, f_norm_post, loss_target, m_meta_tokens, m_a_norm_pre, m_a_w_in, m_a_conv_w, m_a_conv_b, m_a_dt_bias, m_a_a_log, m_a_d_skip, m_a_gate_norm, m_a_w_out, m_a_norm_post, m_kv_norm, m_w_kv, m_b_norm_pre, m_b_w_q, m_b_sinks, m_b_w_o, m_b_norm_post, m_f_norm_pre, m_f_w_up, m_f_conv_w, m_f_conv_b, m_f_w_down, m_f_norm_post, v_meta_tokens, v_a_norm_pre, v_a_w_in, v_a_conv_w, v_a_conv_b, v_a_dt_bias, v_a_a_log, v_a_d_skip, v_a_gate_norm, v_a_w_out, v_a_norm_post, v_kv_norm, v_w_kv, v_b_norm_pre, v_b_w_q, v_b_sinks, v_b_w_o, v_b_norm_post, v_f_norm_pre, v_f_w_up, v_f_conv_w, v_f_conv_b, v_f_w_down, v_f_norm_post):
    args = locals()
    wts = {n: args[n] for n in WEIGHTS}
    mom = {n: args["m_" + n] for n in WEIGHTS}
    var = {n: args["v_" + n] for n in WEIGHTS}
    mx, my, mc = lax.axis_index("x"), lax.axis_index("y"), lax.axis_index("c")
    me = 4 * mx + 2 * my + mc
    rows = _seq_rows()
    hg = SSM_HEADS // SSM_GROUPS
    d = D_MODEL

    n_main = D_INNER + D_XBC
    ws_in, ws_up = a_w_in.shape[2], f_w_up.shape[2]
    segs_in = _col_segments(ws_in, [(0, 0, n_main)] + [(n_main + hg * g, n_main + LANE * g, hg)
                                                      for g in range(SSM_GROUPS)])
    segs_up = _col_segments(ws_up, [(0, 0, 2 * D_FF)])
    def gather_of(*ws):
        return _gather_comm([w.astype(BF16) for w in ws])

    small_full, = _run_comm(_gather_comm([_pack([wts[n] for n in SMALL], 0, 8, F32)]), "gather_small")
    full = {}
    for n, g in zip(SMALL, _unpack(small_full, 1, [wts[n].shape for n in SMALL])):
        full[n] = _join8(g, SHARD_AXIS[n])
    (h0, hn0), (g_in,) = _embed_norm(full["meta_tokens"], x[0], full["a_norm_pre"], rows, "embed_norm",
                                     comm=gather_of(a_w_in[0]))
    w_in_all = _assemble_cols([g_in], n_main + SSM_GROUPS * LANE, segs_in, "asm_w_in")
    w_up, w_down = [None, None], [None, None]
    bias_g = _group_lanes(wts["a_dt_bias"], hg)
    alog_g = _group_lanes(wts["a_a_log"], hg)
    dsk_g = _group_lanes(wts["a_d_skip"], hg)
    a_conv_w, a_conv_b = full["a_conv_w"][0], full["a_conv_b"]
    f_cw, f_cb = full["f_conv_w"], wts["f_conv_b"]
    fpre, fpost = wts["f_norm_pre"], wts["f_norm_post"]


    zx, (g_out, g_o) = _mm(hn0, w_in_all, "nn", F32, "mm_in", comm=gather_of(a_w_out[0], b_w_o[0]))
    w_out = g_out.reshape(D_INNER, d)
    xbc = _conv_silu_fwd(zx, a_conv_w, a_conv_b, "conv_a")
    (y_ssd, hst), (g_up0, g_kv, g_q) = _ssd_fwd(xbc, zx, bias_g, alog_g, dsk_g, "ssd_fwd",
                                                comm=gather_of(f_w_up[0], w_kv, b_w_q[0]))
    w_up[0] = _assemble_cols([g_up0], 2 * D_FF, segs_up, "asm_w_up0")
    yn = _gatenorm_fwd(y_ssd, zx, full["a_gate_norm"], "gatenorm")
    mix_a = _mm(yn, w_out, "nn", F32, "mm_out")
    h1, (fn0,) = _resid_norm(h0, mix_a, full["a_norm_post"], [fpre[0:1]], "resid_a")

    half, quarter = d // 2, d // 4
    u0, (g_dn0,) = _mm(fn0, w_up[0], "nn", F32, "mm_up0", comm=gather_of(f_w_down[0]))
    act0, (g_up1a,) = _ffn_act_fwd(u0, f_cw[0], f_cb[0:1], "ffn_act0", comm=gather_of(f_w_up[1, :quarter]))
    ffn0, (g_up1b,) = _mm(act0, g_dn0.reshape(D_FF, d), "nn", F32, "mm_down0",
                          comm=gather_of(f_w_up[1, quarter:half]))
    w_kvf, w_q, w_o = g_kv.reshape(d, 2 * D_KV), g_q.reshape(d, d), g_o.reshape(d, d)
    h2, (kvn, bn) = _resid_norm(h1, ffn0, fpost[0:1], [wts["kv_norm"].reshape(1, d), wts["b_norm_pre"]], "resid_f0")
    kv = _mm(kvn, w_kvf, "nn", F32, "mm_kv")
    q = _mm(bn, w_q, "nn", F32, "mm_q")
    (o, lse), (g_up1c, g_up1d) = _attn_fwd(q, kv, wts["b_sinks"], "attn_fwd",
                                           comm=gather_of(f_w_up[1, half:half + quarter], f_w_up[1, half + quarter:]))
    w_up[1] = _assemble_cols([g_up1a, g_up1b, g_up1c, g_up1d], 2 * D_FF, segs_up, "asm_w_up1")
    mix_b = _mm(o, w_o, "nn", F32, "mm_o")
    h3, (fn1,) = _resid_norm(h2, mix_b, wts["b_norm_post"], [fpre[1:2]], "resid_b")
    u1, (g_dn1,) = _mm(fn1, w_up[1], "nn", F32, "mm_up1", comm=gather_of(f_w_down[1]))
    w_down = [g_dn0.reshape(D_FF, d), g_dn1.reshape(D_FF, d)]
    act1 = _ffn_act_fwd(u1, f_cw[1], f_cb[1:2], "ffn_act1")
    ffn1 = _mm(act1, w_down[1], "nn", F32, "mm_down1")
    dh4, loss_row, dffn1, dw_post1 = _final_loss(h3, ffn1, fpost[1:2], loss_target[0], "loss")
    loss = lax.psum(loss_row[0, 0], ("x", "y", "c"))

    grads = {}

    core = mc.astype(jnp.int32).reshape(1)

    def carried(res, comm):
        return res if comm is not None else (res, None)

    def ffn_bwd(dh_out, dffn, h_in, fn, u, act, i, then, c_dact=None, c_dwdown=None, c_dwup=None, c_dfn=None):
        dact, got_a = carried(_mm(dffn, w_down[i], "nt", F32, f"mm_dact{i}", comm=c_dact), c_dact)
        dw_down, got_b = carried(_mm(act, dffn, "tn", BF16, f"mm_dwdown{i}", comm=c_dwdown), c_dwdown)
        dw_down = dw_down.reshape(N_DEV, -1, d)
        du, dwc, dbc = _ffn_act_bwd(u, dact, f_cw[i], f_cb[i:i + 1], f"ffn_act_bwd{i}")
        dfn, (s_dn, *got_d) = _mm(du, w_up[i], "nt", F32, f"mm_dfn{i}", comm=_join_comms(
            [_swap_comm([dw_down])] + ([c_dfn] if c_dfn is not None else [])))
        sum_dn = _add_pairs(dw_down, s_dn, core, f"rs_add_dn{i}")
        dw_up, got_c = carried(_mm(fn, du, "tn", BF16, f"mm_dwup{i}", comm=c_dwup, shard_cols=ws_up), c_dwup)
        (dh_in, dw_pre, dbranch, dw_branch), (s_up,) = _norm_bwd(
            h_in, fpre[i:i + 1], dfn, dh_out, F32, f"nb_fpre{i}", comm=_swap_comm([dw_up]), then=then)
        sum_up = _add_pairs(dw_up, s_up, core, f"rs_add_up{i}")
        return dh_in, dbranch, dw_branch, dict(sum_down=sum_dn, cw=jnp.concatenate([dwc[0], dwc[1]], axis=1),
                                               cb=jnp.concatenate([dbc[0], dbc[1]], axis=1), sum_up=sum_up,
                                               pre=dw_pre), got_a, got_b, got_c, got_d

    dh3, dmix_b, grads["b_norm_post"], gf1, _, _, _, _ = ffn_bwd(dh4, dffn1, h3, fn1, u1, act1, 1,
                                                                 (mix_b, wts["b_norm_post"]))
    do = _mm(dmix_b, w_o, "nt", F32, "mm_do")
    dw_o = _mm(o, dmix_b, "tn", BF16, "mm_dwo").reshape(N_DEV, -1, d)
    half_up = gf1["sum_up"].shape[1] // 2
    (dq, dkv, dsinks), (p_up1a, s_o) = _attn_bwd(
        q, kv, wts["b_sinks"], do, lse, "attn_bwd",
        comm=_join_comms([_chips_comm([gf1["sum_up"][:, :half_up]]), _swap_comm([dw_o])]))
    sum_o = _add_pairs(dw_o, s_o, core, "rs_add_o")
    grads["b_sinks"] = dsinks[:, :N_Q_HEADS]
    dbn = _mm(dq, w_q, "nt", F32, "mm_dbn")
    dw_q = _mm(bn, dq, "tn", BF16, "mm_dwq").reshape(N_DEV, -1, d)
    dkv16 = dkv.astype(BF16)
    dkvn = _mm(dkv16, w_kvf, "nt", F32, "mm_dkvn")
    dw_kv = _mm(kvn, dkv16, "tn", BF16, "mm_dwkv").reshape(N_DEV, -1, 2 * D_KV)
    (dh2, grads["b_norm_pre"]), (s_q, s_kv) = _norm_bwd(h2, wts["b_norm_pre"], dbn, dh3, F32, "nb_bpre",
                                                        comm=_swap_comm([dw_q, dw_kv]))
    sum_q, sum_kv = _add_pairs(dw_q, s_q, core, "rs_add_q"), _add_pairs(dw_kv, s_kv, core, "rs_add_kv")
    dh2, dw_kvn, dffn0, dw_post0 = _norm_bwd(h2, wts["kv_norm"].reshape(1, d), dkvn, dh2, F32, "nb_kv",
                                             then=(ffn0, fpost[0:1]))
    grads["kv_norm"] = dw_kvn.reshape(d)
    dh1, dmix_a, grads["a_norm_post"], gf0, (p_o,), (p_q, p_kv), (p_dn1,), (p_up1b,) = ffn_bwd(
        dh2, dffn0, h1, fn0, u0, act0, 0, (mix_a, full["a_norm_post"]), c_dact=_chips_comm([sum_o]),
        c_dwdown=_chips_comm([sum_q, sum_kv]), c_dwup=_chips_comm([gf1["sum_down"]]),
        c_dfn=_chips_comm([gf1["sum_up"][:, half_up:]]))
    p_up1 = jnp.concatenate([p_up1a, p_up1b], axis=1)
    grads["f_norm_post"] = jnp.concatenate([dw_post0, dw_post1], axis=0)
    grads["f_norm_pre"] = jnp.concatenate([gf0["pre"], gf1["pre"]], axis=0)
    grads["f_conv_w"] = jnp.stack([gf0["cw"], gf1["cw"]])
    grads["f_conv_b"] = jnp.concatenate([gf0["cb"], gf1["cb"]], axis=0)

    dyn = _mm(dmix_a, w_out, "nt", F32, "mm_dyn")
    dw_out = _mm(yn, dmix_a, "tn", BF16, "mm_dwout").reshape(N_DEV, -1, d)
    (dy_ssd, dzx, grads["a_gate_norm"]), (s_out,) = _gatenorm_bwd(y_ssd, zx, full["a_gate_norm"], dyn, "gatenorm_bwd",
                                                                  comm=_swap_comm([dw_out]))
    sum_out = _add_pairs(dw_out, s_out, core, "rs_add_out")
    (dxs, dbm, dcm, dzx, dalog, ddsk, dbias), (p_up0,) = _ssd_bwd(
        xbc, zx, bias_g, alog_g, dsk_g, dy_ssd, hst, dzx, "ssd_bwd", comm=_chips_comm([gf0["sum_up"]]))
    grads["a_a_log"] = _ungroup_lanes(dalog, hg)
    grads["a_d_skip"] = _ungroup_lanes(ddsk, hg)
    grads["a_dt_bias"] = _ungroup_lanes(dbias, hg)
    dzx, dcw, dcb = _conv_silu_bwd(zx, dxs, dbm, dcm, a_conv_w, a_conv_b, dzx, "conv_a_bwd")
    grads["a_conv_w"], grads["a_conv_b"] = dcw[None], dcb
    dw_in_all, (p_dn0,) = _mm(hn0, dzx, "tn", BF16, "mm_dwin", comm=_chips_comm([gf0["sum_down"]]))
    dw_in8 = _scatter_cols(dw_in_all, ws_in, segs_in, "scat_w_in")
    dhn0, (s_in, p_out) = _mm(dzx, w_in_all, "nt", F32, "mm_dhn0",
                              comm=_join_comms([_swap_comm([dw_in8]), _chips_comm([sum_out])]))
    sum_in = _add_pairs(dw_in8, s_in, core, "rs_add_in")
    half_in = sum_in.shape[1] // 2
    (grads["meta_tokens"], g_x, grads["a_norm_pre"]), (p_in_a,) = _norm_bwd(
        h0, full["a_norm_pre"], dhn0, dh1, F32, "nb_apre", comm=_chips_comm([sum_in[:, :half_in]]), split_rows=SEQ)
    grad_x = g_x[None]

    small_local = _pack([_split8(grads[n], SHARD_AXIS[n], wts[n].shape[SHARD_AXIS[n]]) for n in SMALL], 1, 8, F32)
    repl_local = _pack([grads[n] for n in REPL], 0, 8, F32)
    n_sr = small_local.shape[1]
    small_vec = jnp.concatenate([small_local.reshape(N_DEV * n_sr, LANE), repl_local], axis=0)
    tail = _join_comms([_chips_comm([sum_in[:, half_in:]]), _gather_comm([small_vec])])
    parts_big = dict(a_w_out=[p_out], w_kv=[p_kv], b_w_q=[p_q], b_w_o=[p_o], f_w_down=[p_dn0, p_dn1])

    def flat_f32(dct, names, mult):
        return _pack([dct[n] for n in names], 0, mult, F32)

    def adamw_big(n, comm=None):
        shp3 = (len(parts_big[n]),) + parts_big[n][0].shape[1:]
        res = _adamw(parts_big[n], *[dct[n].reshape(shp3) for dct in (wts, mom, var)], f"adamw_{n}", comm=comm)
        res, got = res if comm is not None else (res, None)
        big_out[n] = [r.reshape(wts[n].shape) for r in res]
        return got

    big_out = {}
    def swap_last(a):
        return jnp.swapaxes(a, -1, -2)

    g_up_t = swap_last(_sum_parts([p_up0, p_up1], "sum_w_up").reshape(f_w_up.shape))
    res, (p_in_b, small_all) = _adamw([g_up_t[0:1], g_up_t[1:2]], *[swap_last(dct["f_w_up"]) for dct in (wts, mom, var)],
                                      "adamw_f_w_up", comm=tail)
    big_out["f_w_up"] = [swap_last(r) for r in res]
    for n in BIG:
        if n not in ("f_w_up", "a_w_in"):
            adamw_big(n)
    g_in_t = swap_last(_sum_parts([p_in_a, p_in_b], "sum_w_in"))[None]
    res = _adamw([g_in_t], *[swap_last(dct["a_w_in"]) for dct in (wts, mom, var)], "adamw_a_w_in")
    big_out["a_w_in"] = [swap_last(r) for r in res]
    mine_small = lax.dynamic_slice_in_dim(small_all, me * n_sr, n_sr, axis=1)
    parts_small = jnp.concatenate([mine_small, small_all[:, N_DEV * n_sr:]], axis=1)
    sm_in = [jnp.concatenate([flat_f32(dct, SMALL, 8), flat_f32(dct, REPL, 8)], axis=0)[None] for dct in (wts, mom, var)]
    small_out = [r[0] for r in _adamw([parts_small], *sm_in, "adamw_small")]

    outs = []
    for kind in range(4):
        res = {n: big_out[n][kind] for n in BIG}
        for n, a in zip(SMALL, _unpack(small_out[kind][:n_sr], 0, [wts[n].shape for n in SMALL])):
            res[n] = a
        for n, a in zip(REPL, _unpack(small_out[kind][n_sr:], 0, [wts[n].shape for n in REPL])):
            res[n] = a
        outs.append(res)
    return (loss, grad_x, *[outs[0][n] for n in WEIGHTS], *[outs[1][n] for n in WEIGHTS],
            *[outs[2][n] for n in WEIGHTS], *[outs[3][n] for n in WEIGHTS])
```
